```python
import jax, jax.numpy as jnp
from jax import lax
import numpy as np

D_MODEL = 1024
BATCH = 8
SEQ = 4096
DEPTH = 1

CHUNK = 64
HEAD_DIM = 64
ATTN_WIDTH = D_MODEL // 2
CONV_WIDTH = D_MODEL - ATTN_WIDTH
N_ATTN_HEADS = ATTN_WIDTH // HEAD_DIM
N_CONV_GROUPS = CONV_WIDTH // HEAD_DIM
CONV_KERNEL = 3
Q_BLOCK = 128
D_FF = -(-8 * D_MODEL // (3 * 256)) * 256
IN_WIDTH = 3 * ATTN_WIDTH + N_ATTN_HEADS + 3 * CONV_WIDTH
EPS = 1e-6
FORGET_BIAS_INIT = 3.0

kernel_name = "hymba_fox_shortconv_sandwich_block"


def rms_norm(x, g):
    xf = x.astype(jnp.float32)
    y = xf * lax.rsqrt(jnp.mean(xf * xf, axis=-1, keepdims=True) + EPS)
    return (y * g.astype(jnp.float32)).astype(x.dtype)


def group_rms_norm(y, g, n_groups):
    b, s, w = y.shape
    yf = y.astype(jnp.float32).reshape(b, s, n_groups, w // n_groups)
    yf = yf * lax.rsqrt(jnp.mean(yf * yf, axis=-1, keepdims=True) + EPS)
    return (yf.reshape(b, s, w) * g.astype(jnp.float32)).astype(y.dtype)


def forgetting_attention(q, k, v, log_f):
    b, s, h, dh = q.shape
    nb = s // Q_BLOCK
    c = jnp.transpose(jnp.cumsum(log_f.astype(jnp.float32), axis=1), (0, 2, 1))
    qf = jnp.transpose(q, (0, 2, 1, 3)).astype(jnp.float32) * (dh ** -0.5)
    kf = jnp.transpose(k, (0, 2, 1, 3)).astype(jnp.float32)
    vf = jnp.transpose(v, (0, 2, 1, 3)).astype(jnp.float32)
    qb = qf.reshape(b, h, nb, Q_BLOCK, dh).transpose(2, 0, 1, 3, 4)
    cqb = c.reshape(b, h, nb, Q_BLOCK).transpose(2, 0, 1, 3)
    key_pos = jnp.arange(s)

    def one_block(args):
        i, q_blk, cq_blk = args
        q_pos = i * Q_BLOCK + jnp.arange(Q_BLOCK)
        sc = jnp.einsum('bhqd,bhkd->bhqk', q_blk, kf) + cq_blk[..., :, None] - c[:, :, None, :]
        sc = jnp.where(key_pos[None, :] <= q_pos[:, None], sc, -jnp.inf)
        p = jax.nn.softmax(sc, axis=-1)
        return jnp.einsum('bhqk,bhkd->bhqd', p, vf)

    out = lax.map(one_block, (jnp.arange(nb), qb, cqb))
    out = out.transpose(1, 0, 3, 2, 4).reshape(b, s, h * dh)
    return out.astype(q.dtype)


def causal_depthwise_conv(u, w):
    kw = w.shape[0]
    s = u.shape[1]
    up = jnp.pad(u, ((0, 0), (kw - 1, 0), (0, 0)))
    out = up[:, 0:s, :] * w[0]
    for j in range(1, kw):
        out = out + up[:, j:j + s, :] * w[j]
    return out


def _fwd_setup_inputs(seed: int = 0) -> dict:
    key = jax.random.key(seed)
    ks = jax.random.split(key, 14)
    f32 = jnp.float32

    def gain(k):
        return 1.0 + 0.05 * jax.random.normal(k, (DEPTH, D_MODEL), f32)

    return {
        "x": jax.random.normal(ks[0], (BATCH, SEQ, D_MODEL), f32),
        "w_in": jax.random.normal(ks[1], (DEPTH, D_MODEL, IN_WIDTH), f32) * D_MODEL ** -0.5,
        "b_forget": FORGET_BIAS_INIT + 0.5 * jax.random.normal(ks[2], (DEPTH, N_ATTN_HEADS), f32),
        "conv_w": jax.random.normal(ks[3], (DEPTH, CONV_KERNEL, CONV_WIDTH), f32) * CONV_KERNEL ** -0.5,
        "g_attn_out": 1.0 + 0.05 * jax.random.normal(ks[4], (DEPTH, ATTN_WIDTH), f32),
        "g_conv_out": 1.0 + 0.05 * jax.random.normal(ks[5], (DEPTH, CONV_WIDTH), f32),
        "w_out": jax.random.normal(ks[6], (DEPTH, D_MODEL, D_MODEL), f32) * D_MODEL ** -0.5,
        "g_mix_pre": gain(ks[7]),
        "g_mix_post": gain(ks[8]),
        "w_gate_up": jax.random.normal(ks[9], (DEPTH, D_MODEL, 2 * D_FF), f32) * D_MODEL ** -0.5,
        "w_down": jax.random.normal(ks[10], (DEPTH, D_FF, D_MODEL), f32) * D_FF ** -0.5,
        "g_ffn_pre": gain(ks[11]),
        "g_ffn_post": gain(ks[12]),
    }


def _fwd_reference(x, w_in, b_forget, conv_w, g_attn_out, g_conv_out, w_out,
              g_mix_pre, g_mix_post, w_gate_up, w_down, g_ffn_pre, g_ffn_post):
    b, s, _ = x.shape
    splits = [ATTN_WIDTH, 2 * ATTN_WIDTH, 3 * ATTN_WIDTH,
              3 * ATTN_WIDTH + N_ATTN_HEADS,
              3 * ATTN_WIDTH + N_ATTN_HEADS + CONV_WIDTH,
              3 * ATTN_WIDTH + N_ATTN_HEADS + 2 * CONV_WIDTH]
    for l in range(DEPTH):
        h = rms_norm(x, g_mix_pre[l])
        proj = jnp.einsum('bsd,de->bse', h, w_in[l])
        q, k, v, f_logit, gate_b, gate_c, u = jnp.split(proj, splits, axis=-1)
        log_f = jax.nn.log_sigmoid(f_logit.astype(jnp.float32) + b_forget[l].astype(jnp.float32))
        attn = forgetting_attention(q.reshape(b, s, N_ATTN_HEADS, HEAD_DIM),
                                    k.reshape(b, s, N_ATTN_HEADS, HEAD_DIM),
                                    v.reshape(b, s, N_ATTN_HEADS, HEAD_DIM), log_f)
        conv = gate_b * causal_depthwise_conv(gate_c * u, conv_w[l])
        merged = jnp.concatenate([group_rms_norm(attn, g_attn_out[l], N_ATTN_HEADS),
                                  group_rms_norm(conv, g_conv_out[l], N_CONV_GROUPS)], axis=-1)
        y = jnp.einsum('bse,ed->bsd', merged, w_out[l])
        x = x + rms_norm(y, g_mix_post[l])
        h = rms_norm(x, g_ffn_pre[l])
        gu = jnp.einsum('bsd,df->bsf', h, w_gate_up[l])
        g, up = jnp.split(gu, [D_FF], axis=-1)
        ff = jnp.einsum('bsf,fd->bsd', jax.nn.silu(g) * up, w_down[l])
        x = x + rms_norm(ff, g_ffn_post[l])
    return x


import jax as _jax
import jax.numpy as _jnp

TWIN_FORMAT = 'train_step'
FWD_PARAMS = ['x', 'w_in', 'b_forget', 'conv_w', 'g_attn_out', 'g_conv_out', 'w_out', 'g_mix_pre', 'g_mix_post', 'w_gate_up', 'w_down', 'g_ffn_pre', 'g_ffn_post']
TWIN_WEIGHTS = ['w_in', 'b_forget', 'conv_w', 'g_attn_out', 'g_conv_out', 'w_out', 'g_mix_pre', 'g_mix_post', 'w_gate_up', 'w_down', 'g_ffn_pre', 'g_ffn_post']
TWIN_DIFF_INPUT = 'x'
TWIN_INPUTS = ['x', 'w_in', 'b_forget', 'conv_w', 'g_attn_out', 'g_conv_out', 'w_out', 'g_mix_pre', 'g_mix_post', 'w_gate_up', 'w_down', 'g_ffn_pre', 'g_ffn_post', 'loss_target', 'm_w_in', 'm_b_forget', 'm_conv_w', 'm_g_attn_out', 'm_g_conv_out', 'm_w_out', 'm_g_mix_pre', 'm_g_mix_post', 'm_w_gate_up', 'm_w_down', 'm_g_ffn_pre', 'm_g_ffn_post', 'v_w_in', 'v_b_forget', 'v_conv_w', 'v_g_attn_out', 'v_g_conv_out', 'v_w_out', 'v_g_mix_pre', 'v_g_mix_post', 'v_w_gate_up', 'v_w_down', 'v_g_ffn_pre', 'v_g_ffn_post']
TWIN_OUTPUTS = ['loss', 'grad_x', 'grad_w_in', 'grad_b_forget', 'grad_conv_w', 'grad_g_attn_out', 'grad_g_conv_out', 'grad_w_out', 'grad_g_mix_pre', 'grad_g_mix_post', 'grad_w_gate_up', 'grad_w_down', 'grad_g_ffn_pre', 'grad_g_ffn_post', 'delta_w_in', 'delta_b_forget', 'delta_conv_w', 'delta_g_attn_out', 'delta_g_conv_out', 'delta_w_out', 'delta_g_mix_pre', 'delta_g_mix_post', 'delta_w_gate_up', 'delta_w_down', 'delta_g_ffn_pre', 'delta_g_ffn_post', 'new_m_w_in', 'new_m_b_forget', 'new_m_conv_w', 'new_m_g_attn_out', 'new_m_g_conv_out', 'new_m_w_out', 'new_m_g_mix_pre', 'new_m_g_mix_post', 'new_m_w_gate_up', 'new_m_w_down', 'new_m_g_ffn_pre', 'new_m_g_ffn_post', 'new_v_w_in', 'new_v_b_forget', 'new_v_conv_w', 'new_v_g_attn_out', 'new_v_g_conv_out', 'new_v_w_out', 'new_v_g_mix_pre', 'new_v_g_mix_post', 'new_v_w_gate_up', 'new_v_w_down', 'new_v_g_ffn_pre', 'new_v_g_ffn_post']
TWIN_LEAF_KINDS = {'loss': 'loss', 'grad_x': 'grad_x', 'grad_w_in': 'grad_w', 'grad_b_forget': 'grad_w', 'grad_conv_w': 'grad_w', 'grad_g_attn_out': 'grad_w', 'grad_g_conv_out': 'grad_w', 'grad_w_out': 'grad_w', 'grad_g_mix_pre': 'grad_w', 'grad_g_mix_post': 'grad_w', 'grad_w_gate_up': 'grad_w', 'grad_w_down': 'grad_w', 'grad_g_ffn_pre': 'grad_w', 'grad_g_ffn_post': 'grad_w', 'delta_w_in': 'delta_w', 'delta_b_forget': 'delta_w', 'delta_conv_w': 'delta_w', 'delta_g_attn_out': 'delta_w', 'delta_g_conv_out': 'delta_w', 'delta_w_out': 'delta_w', 'delta_g_mix_pre': 'delta_w', 'delta_g_mix_post': 'delta_w', 'delta_w_gate_up': 'delta_w', 'delta_w_down': 'delta_w', 'delta_g_ffn_pre': 'delta_w', 'delta_g_ffn_post': 'delta_w', 'new_m_w_in': 'new_m', 'new_m_b_forget': 'new_m', 'new_m_conv_w': 'new_m', 'new_m_g_attn_out': 'new_m', 'new_m_g_conv_out': 'new_m', 'new_m_w_out': 'new_m', 'new_m_g_mix_pre': 'new_m', 'new_m_g_mix_post': 'new_m', 'new_m_w_gate_up': 'new_m', 'new_m_w_down': 'new_m', 'new_m_g_ffn_pre': 'new_m', 'new_m_g_ffn_post': 'new_m', 'new_v_w_in': 'new_v', 'new_v_b_forget': 'new_v', 'new_v_conv_w': 'new_v', 'new_v_g_attn_out': 'new_v', 'new_v_g_conv_out': 'new_v', 'new_v_w_out': 'new_v', 'new_v_g_mix_pre': 'new_v', 'new_v_g_mix_post': 'new_v', 'new_v_w_gate_up': 'new_v', 'new_v_w_down': 'new_v', 'new_v_g_ffn_pre': 'new_v', 'new_v_g_ffn_post': 'new_v'}


def _forward(args):
    return _fwd_reference(*[args[k] for k in FWD_PARAMS])


def _output_shape():
    out = _jax.eval_shape(lambda: _forward(_fwd_setup_inputs(0)))
    return out.shape, out.dtype

N_MICROBATCH = 1
ADAM_LR = 0.001
ADAM_B1 = 0.9
ADAM_B2 = 0.999
ADAM_EPS = 1e-08
ADAM_WD = 0.01
ADAM_STEP = 10
PER_EXAMPLE_BATCH_AXIS = {'x': 0, 'loss_target': 0}
SHARED_INPUTS = []
_WEIGHT_DTYPES = {'w_in': _jnp.float32, 'b_forget': _jnp.float32, 'conv_w': _jnp.float32, 'g_attn_out': _jnp.float32, 'g_conv_out': _jnp.float32, 'w_out': _jnp.float32, 'g_mix_pre': _jnp.float32, 'g_mix_post': _jnp.float32, 'w_gate_up': _jnp.float32, 'w_down': _jnp.float32, 'g_ffn_pre': _jnp.float32, 'g_ffn_post': _jnp.float32}
MOMENT_SCALE = {'w_in': 4.441086e-01, 'b_forget': 2.198117e+00, 'conv_w': 5.044575e-01, 'g_attn_out': 4.991807e-01, 'g_conv_out': 5.710366e-01, 'w_out': 5.113420e-01, 'g_mix_pre': 8.151519e-01, 'g_mix_post': 3.190748e+01, 'w_gate_up': 2.331189e-01, 'w_down': 4.516025e-01, 'g_ffn_pre': 5.299588e-01, 'g_ffn_post': 3.200535e+01}


def _to_microbatches(a, axis):
    t = _jnp.moveaxis(a, axis, 0)
    t = t.reshape((N_MICROBATCH, t.shape[0] // N_MICROBATCH) + t.shape[1:])
    return _jnp.moveaxis(t, 1, axis + 1)


def setup_inputs(seed: int = 0) -> dict:
    inp = _fwd_setup_inputs(seed)
    key = _jax.random.fold_in(_jax.random.key(seed), 7919)
    shape, _ = _output_shape()
    out = dict(inp)
    out["loss_target"] = _jax.random.normal(_jax.random.fold_in(key, 0), shape, _jnp.float32)
    for i, name in enumerate(TWIN_WEIGHTS):
        w = inp[name].astype(_jnp.float32)
        if MOMENT_SCALE is None:
            s = _jnp.sqrt(_jnp.mean(_jnp.square(w)) + 1e-30)
        else:
            s = MOMENT_SCALE[name]
        km, kv = _jax.random.split(_jax.random.fold_in(key, i + 1))
        out[name] = w
        out["m_" + name] = s * _jax.random.normal(km, w.shape, _jnp.float32)
        out["v_" + name] = (s * s) * _jax.random.uniform(kv, w.shape, _jnp.float32, 0.5, 1.5)
    if N_MICROBATCH > 1:
        for name, axis in PER_EXAMPLE_BATCH_AXIS.items():
            out[name] = _to_microbatches(out[name], axis)
    return {'x': out['x'], 'w_in': out['w_in'], 'b_forget': out['b_forget'], 'conv_w': out['conv_w'], 'g_attn_out': out['g_attn_out'], 'g_conv_out': out['g_conv_out'], 'w_out': out['w_out'], 'g_mix_pre': out['g_mix_pre'], 'g_mix_post': out['g_mix_post'], 'w_gate_up': out['w_gate_up'], 'w_down': out['w_down'], 'g_ffn_pre': out['g_ffn_pre'], 'g_ffn_post': out['g_ffn_post'], 'loss_target': out['loss_target'], 'm_w_in': out['m_w_in'], 'm_b_forget': out['m_b_forget'], 'm_conv_w': out['m_conv_w'], 'm_g_attn_out': out['m_g_attn_out'], 'm_g_conv_out': out['m_g_conv_out'], 'm_w_out': out['m_w_out'], 'm_g_mix_pre': out['m_g_mix_pre'], 'm_g_mix_post': out['m_g_mix_post'], 'm_w_gate_up': out['m_w_gate_up'], 'm_w_down': out['m_w_down'], 'm_g_ffn_pre': out['m_g_ffn_pre'], 'm_g_ffn_post': out['m_g_ffn_post'], 'v_w_in': out['v_w_in'], 'v_b_forget': out['v_b_forget'], 'v_conv_w': out['v_conv_w'], 'v_g_attn_out': out['v_g_attn_out'], 'v_g_conv_out': out['v_g_conv_out'], 'v_w_out': out['v_w_out'], 'v_g_mix_pre': out['v_g_mix_pre'], 'v_g_mix_post': out['v_g_mix_post'], 'v_w_gate_up': out['v_w_gate_up'], 'v_w_down': out['v_w_down'], 'v_g_ffn_pre': out['v_g_ffn_pre'], 'v_g_ffn_post': out['v_g_ffn_post']}


def _loss(weights, diff, rest, loss_target):
    with _jax.named_scope("forward"):
        args = {**rest, TWIN_DIFF_INPUT: diff, **{k: w.astype(_WEIGHT_DTYPES[k]) for k, w in weights.items()}}
        y = _forward(args)
    with _jax.named_scope("loss_head"):
        err = _jnp.square(y.astype(_jnp.float32) - loss_target)
        return 0.5 * _jnp.sum(_jnp.mean(err, axis=-1)) if err.ndim else 0.5 * err


def _adamw(w, g, m, v):
    m = ADAM_B1 * m + (1.0 - ADAM_B1) * g
    v = ADAM_B2 * v + (1.0 - ADAM_B2) * _jnp.square(g)
    m_hat = m / (1.0 - ADAM_B1 ** ADAM_STEP)
    v_hat = v / (1.0 - ADAM_B2 ** ADAM_STEP)
    delta = -ADAM_LR * (m_hat / (_jnp.sqrt(v_hat) + ADAM_EPS) + ADAM_WD * w)
    return delta, m, v


def reference(x, w_in, b_forget, conv_w, g_attn_out, g_conv_out, w_out, g_mix_pre, g_mix_post, w_gate_up, w_down, g_ffn_pre, g_ffn_post, loss_target, m_w_in, m_b_forget, m_conv_w, m_g_attn_out, m_g_conv_out, m_w_out, m_g_mix_pre, m_g_mix_post, m_w_gate_up, m_w_down, m_g_ffn_pre, m_g_ffn_post, v_w_in, v_b_forget, v_conv_w, v_g_attn_out, v_g_conv_out, v_w_out, v_g_mix_pre, v_g_mix_post, v_w_gate_up, v_w_down, v_g_ffn_pre, v_g_ffn_post):
    given = dict(x=x, w_in=w_in, b_forget=b_forget, conv_w=conv_w, g_attn_out=g_attn_out, g_conv_out=g_conv_out, w_out=w_out, g_mix_pre=g_mix_pre, g_mix_post=g_mix_post, w_gate_up=w_gate_up, w_down=w_down, g_ffn_pre=g_ffn_pre, g_ffn_post=g_ffn_post, loss_target=loss_target, m_w_in=m_w_in, m_b_forget=m_b_forget, m_conv_w=m_conv_w, m_g_attn_out=m_g_attn_out, m_g_conv_out=m_g_conv_out, m_w_out=m_w_out, m_g_mix_pre=m_g_mix_pre, m_g_mix_post=m_g_mix_post, m_w_gate_up=m_w_gate_up, m_w_down=m_w_down, m_g_ffn_pre=m_g_ffn_pre, m_g_ffn_post=m_g_ffn_post, v_w_in=v_w_in, v_b_forget=v_b_forget, v_conv_w=v_conv_w, v_g_attn_out=v_g_attn_out, v_g_conv_out=v_g_conv_out, v_w_out=v_w_out, v_g_mix_pre=v_g_mix_pre, v_g_mix_post=v_g_mix_post, v_w_gate_up=v_w_gate_up, v_w_down=v_w_down, v_g_ffn_pre=v_g_ffn_pre, v_g_ffn_post=v_g_ffn_post)
    weights = {n: given[n] for n in TWIN_WEIGHTS}
    shared = {n: given[n] for n in SHARED_INPUTS}
    per_example = {n: given[n] for n in ['x']}
    grad_fn = _jax.value_and_grad(_loss, argnums=(0, 1))

    def one_microbatch(ex, loss_target):
        ex = dict(ex)
        diff = ex.pop(TWIN_DIFF_INPUT)
        return grad_fn(weights, diff, {**shared, **ex}, loss_target)

    if N_MICROBATCH == 1:
        loss, (grad_w, grad_x) = one_microbatch(per_example, given["loss_target"])
    else:
        def body(carry, xs):
            loss_sum, grad_sum = carry
            l_k, (gw_k, gx_k) = one_microbatch(xs[0], xs[1])
            with _jax.named_scope("update"):
                return (loss_sum + l_k, _jax.tree.map(_jnp.add, grad_sum, gw_k)), gx_k

        init = (_jnp.zeros((), _jnp.float32), _jax.tree.map(_jnp.zeros_like, weights))
        (loss, grad_w), grad_x = _jax.lax.scan(body, init, (per_example, given["loss_target"]))
    with _jax.named_scope("update"):
        delta_w, new_m, new_v = {}, {}, {}
        for n in TWIN_WEIGHTS:
            delta_w[n], new_m[n], new_v[n] = _adamw(weights[n], grad_w[n], given["m_" + n], given["v_" + n])
    return (loss, grad_x, *[grad_w[n] for n in TWIN_WEIGHTS], *[delta_w[n] for n in TWIN_WEIGHTS],
            *[new_m[n] for n in TWIN_WEIGHTS], *[new_v[n] for n in TWIN_WEIGHTS])
```

```python
import functools

import jax
import jax.numpy as jnp
from jax import lax
from jax.experimental import pallas as pl
from jax.experimental.pallas import tpu as pltpu

F32 = jnp.float32
BF16 = jnp.bfloat16
HIGHEST = lax.Precision.HIGHEST
MESH_ID = pl.DeviceIdType.MESH

D = 1024
H = 8
DH = 64
AW = 512
CW = 512
DFF = 2816
HP = 128
WP = 2 * H * HP + AW + 3 * CW + 128
OFF_Q, OFF_K, OFF_V, OFF_BCU, OFF_F = 0, 1024, 2048, 2560, 4096
EPS = 1e-6
NDEV = 8
LANES = 128
SUBLANES = 8

ROWS_IN, ROWS_OUT, ROWS_GU, ROWS_DOWN = 3080, 1024, 5632, 2816
ROWS = 12560

ADAM_LR, ADAM_B1, ADAM_B2, ADAM_EPS, ADAM_WD, ADAM_STEP = 0.001, 0.9, 0.999, 1e-08, 0.01, 10

NT = (((1,), (1,)), ((), ()))
TN = (((0,), (0,)), ((), ()))


def _cparams(vmem_mb=None, sem=None):
    kw = {}
    if vmem_mb is not None:
        kw["vmem_limit_bytes"] = vmem_mb << 20
    if sem is not None:
        kw["dimension_semantics"] = sem
    return pltpu.CompilerParams(**kw)


def _full(shape):
    return pl.BlockSpec(shape, lambda *_: (0,) * len(shape))


def _resident(shape):
    return pl.BlockSpec(shape, lambda *_: (0,) * len(shape), pipeline_mode=pl.Buffered(1))


def _rows(tm, width):
    return pl.BlockSpec((tm, width), lambda i: (i, 0))


def _fold8(v):
    r, w = v.shape
    return jnp.sum(v.reshape(r // SUBLANES, SUBLANES, w), axis=0)


def _split_dot(v, m01):
    hi = v.astype(BF16)
    lo = (v - hi.astype(F32)).astype(BF16)
    return (jnp.dot(hi, m01, preferred_element_type=F32)
            + jnp.dot(lo, m01, preferred_element_type=F32))


def _rms_fwd(v, g):
    r = lax.rsqrt(jnp.mean(v * v, axis=-1, keepdims=True) + EPS)
    n = v * r
    return n * g, n, r


def _rms_bwd(do, n, r, g):
    dn = do * g
    return r * (dn - n * jnp.mean(dn * n, axis=-1, keepdims=True)), do * n


def _in_proj(x, g1, wp, bfp, pq, pk, oq, ok, *, tm):
    s = x.shape[0]

    def body(x_ref, g_ref, w_ref, bf_ref, pq_ref, pk_ref, oq_ref, ok_ref,
             h_ref, qp_ref, kp_ref, v_ref, bcu_ref, z_ref, carry):
        @pl.when(pl.program_id(0) == 0)
        def _():
            carry[...] = jnp.zeros_like(carry)

        h = _rms_fwd(x_ref[...], g_ref[...])[0].astype(BF16)
        h_ref[...] = h
        z = jnp.dot(h, w_ref[:, OFF_F:OFF_F + 128], preferred_element_type=F32) + bf_ref[...]
        z_ref[...] = z
        lane = lax.broadcasted_iota(jnp.int32, (tm, 128), 1)
        logf = jnp.where(lane < H, jnp.minimum(z, 0.0) - jnp.log(1.0 + jnp.exp(-jnp.abs(z))), 0.0)
        row = lax.broadcasted_iota(jnp.int32, (tm, tm), 0)
        col = lax.broadcasted_iota(jnp.int32, (tm, tm), 1)
        tri = (col <= row).astype(F32)
        c = jnp.dot(tri, logf, precision=HIGHEST, preferred_element_type=F32) + carry[0:1, :]
        carry[...] = jnp.broadcast_to(c[tm - 1:tm, :], carry.shape)
        c1 = c.astype(BF16).astype(F32)
        r1 = c - c1
        c2 = r1.astype(BF16).astype(F32)
        c3 = (r1 - c2).astype(BF16).astype(F32)
        zc = (c1 + pltpu.roll(c2, 8, axis=1) + pltpu.roll(c3, 16, axis=1)).astype(BF16)
        q = jnp.dot(h, w_ref[:, OFF_Q:OFF_Q + 1024], preferred_element_type=F32)
        qp_ref[...] = (q + jnp.dot(zc, pq_ref[...], preferred_element_type=F32) + oq_ref[...]).astype(BF16)
        k = jnp.dot(h, w_ref[:, OFF_K:OFF_K + 1024], preferred_element_type=F32)
        kp_ref[...] = (k + jnp.dot(zc, pk_ref[...], preferred_element_type=F32) + ok_ref[...]).astype(BF16)
        v_ref[...] = jnp.dot(h, w_ref[:, OFF_V:OFF_V + AW], preferred_element_type=F32).astype(BF16)
        bcu_ref[...] = jnp.dot(h, w_ref[:, OFF_BCU:OFF_BCU + 3 * CW], preferred_element_type=F32)

    return pl.pallas_call(
        body, name="in_proj", grid=(s // tm,),
        in_specs=[_rows(tm, D), _full((1, D)), _resident((D, WP)), _full((1, 128)),
                  _full((128, 1024)), _full((128, 1024)), _full((1, 1024)), _full((1, 1024))],
        out_specs=[_rows(tm, D), _rows(tm, 1024), _rows(tm, 1024), _rows(tm, AW), _rows(tm, 3 * CW), _rows(tm, 128)],
        out_shape=[jax.ShapeDtypeStruct((s, D), BF16), jax.ShapeDtypeStruct((s, 1024), BF16),
                   jax.ShapeDtypeStruct((s, 1024), BF16), jax.ShapeDtypeStruct((s, AW), BF16),
                   jax.ShapeDtypeStruct((s, 3 * CW), F32), jax.ShapeDtypeStruct((s, 128), F32)],
        scratch_shapes=[pltpu.VMEM((SUBLANES, 128), F32)],
        compiler_params=_cparams(56, ("arbitrary",)),
    )(x, g1, wp, bfp, pq, pk, oq, ok)


def _attn_fwd(qp, kp, v, *, t):
    s = qp.shape[0]
    nq = s // t

    def body(q_ref, k_ref, v_ref, o_ref, lse_ref):
        qi = pl.program_id(1)
        row = lax.broadcasted_iota(jnp.int32, (t, t), 0)
        col = lax.broadcasted_iota(jnp.int32, (t, t), 1)
        lane = lax.broadcasted_iota(jnp.int32, (t, 128), 1)
        outs, lses = [], []
        for hh in range(2):
            q = q_ref[:, HP * hh:HP * (hh + 1)]

            def step(ki, carry, masked, hh=hh, q=q):
                m, l, acc = carry
                off = pl.multiple_of(ki * t, t)
                k = k_ref[pl.ds(off, t), HP * hh:HP * (hh + 1)]
                sc = lax.dot_general(q, k, NT, preferred_element_type=F32)
                if masked:
                    sc = jnp.where(col <= row, sc, -1e30)
                mn = jnp.maximum(m, jnp.max(sc, axis=-1, keepdims=True))
                p = jnp.exp(sc - mn)
                a = jnp.exp(m - mn)
                l = a * l + jnp.sum(p, axis=-1, keepdims=True)
                p_hi = p.astype(BF16)
                p_lo = (p - p_hi.astype(F32)).astype(BF16)
                vv = v_ref[pl.ds(off, t), :]
                acc = a * acc + (jnp.dot(p_hi, vv, preferred_element_type=F32)
                                 + jnp.dot(p_lo, vv, preferred_element_type=F32))
                return mn, l, acc

            init = (jnp.full((t, 1), -1e30, F32), jnp.zeros((t, 1), F32), jnp.zeros((t, 128), F32))
            carry = lax.fori_loop(0, qi, functools.partial(step, masked=False), init)
            m, l, acc = step(qi, carry, True)
            outs.append(acc / l)
            lses.append(jnp.broadcast_to(m + jnp.log(l), (t, 128)))
        o_ref[...] = jnp.where(lane < DH, outs[0], outs[1])
        lse_ref[...] = jnp.where(lane < DH, lses[0], lses[1])

    return pl.pallas_call(
        body, name="attn_fwd", grid=(H // 2, nq),
        in_specs=[pl.BlockSpec((t, 2 * HP), lambda p, i: (i, p)),
                  pl.BlockSpec((s, 2 * HP), lambda p, i: (0, p)),
                  pl.BlockSpec((s, 128), lambda p, i: (0, p))],
        out_specs=[pl.BlockSpec((t, 128), lambda p, i: (i, p)), pl.BlockSpec((t, 128), lambda p, i: (i, p))],
        out_shape=[jax.ShapeDtypeStruct((s, AW), F32), jax.ShapeDtypeStruct((s, AW), F32)],
        compiler_params=_cparams(48, ("arbitrary", "arbitrary")),
    )(qp, kp, v)


def _conv_taps(bcu_ref, halo_ref, first, tm):
    z = bcu_ref[:, CW:2 * CW] * bcu_ref[:, 2 * CW:3 * CW]
    zh = jnp.where(first, 0.0, halo_ref[:, CW:2 * CW] * halo_ref[:, 2 * CW:3 * CW])
    row = lax.broadcasted_iota(jnp.int32, (tm, CW), 0)
    z1 = jnp.where(row == 0, zh[7:8, :], pltpu.roll(z, 1, axis=0))
    z2 = jnp.where(row == 0, zh[6:7, :], jnp.where(row == 1, zh[7:8, :], pltpu.roll(z, 2, axis=0)))
    return z, z1, z2


def _halo_before(tm, width):
    return pl.BlockSpec((SUBLANES, width), lambda i: (jnp.maximum(i * (tm // SUBLANES) - 1, 0), 0))


def _mix_out(o, bcu, cw8, ga, gc, gsum, w_out, x, g_post, *, tm):
    s = x.shape[0]

    def body(o_ref, bcu_ref, halo_ref, cw_ref, ga_ref, gc_ref, gs_ref, w_ref, x_ref, g_ref,
             merged_ref, y_ref, x2_ref, cv_ref):
        z, z1, z2 = _conv_taps(bcu_ref, halo_ref, pl.program_id(0) == 0, tm)
        cv = cw_ref[0:1, :] * z2 + cw_ref[1:2, :] * z1 + cw_ref[2:3, :] * z
        cv_ref[...] = cv
        conv = bcu_ref[:, 0:CW] * cv
        ov = o_ref[...]
        ra = lax.rsqrt(_split_dot(ov * ov, gs_ref[...]) * (1.0 / DH) + EPS)
        rc = lax.rsqrt(_split_dot(conv * conv, gs_ref[...]) * (1.0 / DH) + EPS)
        merged = jnp.concatenate([ov * ra * ga_ref[...], conv * rc * gc_ref[...]], axis=1).astype(BF16)
        merged_ref[...] = merged
        y = jnp.dot(merged, w_ref[...], preferred_element_type=F32)
        y_ref[...] = y
        x2_ref[...] = x_ref[...] + _rms_fwd(y, g_ref[...])[0]

    return pl.pallas_call(
        body, name="mix_out", grid=(s // tm,),
        in_specs=[_rows(tm, AW), _rows(tm, 3 * CW), _halo_before(tm, 3 * CW), _full((SUBLANES, CW)),
                  _full((1, AW)), _full((1, CW)), _full((CW, CW)), _resident((D, D)), _rows(tm, D), _full((1, D))],
        out_specs=[_rows(tm, D), _rows(tm, D), _rows(tm, D), _rows(tm, CW)],
        out_shape=[jax.ShapeDtypeStruct((s, D), BF16), jax.ShapeDtypeStruct((s, D), F32),
                   jax.ShapeDtypeStruct((s, D), F32), jax.ShapeDtypeStruct((s, CW), F32)],
        compiler_params=_cparams(48, ("arbitrary",)),
    )(o, bcu, bcu, cw8, ga, gc, gsum, w_out, x, g_post)


def _ffn_up(x2, g_pre, w_g, w_u, *, tm):
    s = x2.shape[0]

    def body(x_ref, g_ref, wg_ref, wu_ref, h_ref, gate_ref, up_ref, a_ref):
        h = _rms_fwd(x_ref[...], g_ref[...])[0].astype(BF16)
        h_ref[...] = h
        gate = jnp.dot(h, wg_ref[...], preferred_element_type=F32)
        up = jnp.dot(h, wu_ref[...], preferred_element_type=F32)
        gate_ref[...] = gate.astype(BF16)
        up_ref[...] = up.astype(BF16)
        a_ref[...] = (gate * jax.nn.sigmoid(gate) * up).astype(BF16)

    return pl.pallas_call(
        body, name="ffn_up", grid=(s // tm,),
        in_specs=[_rows(tm, D), _full((1, D)), _resident((D, DFF)), _resident((D, DFF))],
        out_specs=[_rows(tm, D), _rows(tm, DFF), _rows(tm, DFF), _rows(tm, DFF)],
        out_shape=[jax.ShapeDtypeStruct((s, D), BF16)] + [jax.ShapeDtypeStruct((s, DFF), BF16)] * 3,
        compiler_params=_cparams(56, ("arbitrary",)),
    )(x2, g_pre, w_g, w_u)


def _ffn_down_loss(a, w_down, x2, target, g_post, *, tm):
    s = x2.shape[0]

    def body(a_ref, w_ref, x2_ref, t_ref, g_ref, dx3_ref, dff_ref, loss_ref, dg_ref):
        @pl.when(pl.program_id(0) == 0)
        def _():
            loss_ref[...] = jnp.zeros_like(loss_ref)
            dg_ref[...] = jnp.zeros_like(dg_ref)

        ff = jnp.dot(a_ref[...], w_ref[...], preferred_element_type=F32)
        out, n, r = _rms_fwd(ff, g_ref[...])
        e = x2_ref[...] + out - t_ref[...]
        loss_ref[...] += _fold8(e * e)
        dx3 = e * (1.0 / D)
        dx3_ref[...] = dx3
        dff, dg = _rms_bwd(dx3, n, r, g_ref[...])
        dff_ref[...] = dff.astype(BF16)
        dg_ref[...] += _fold8(dg)

    return pl.pallas_call(
        body, name="ffn_down_loss", grid=(s // tm,),
        in_specs=[_rows(tm, DFF), _resident((DFF, D)), _rows(tm, D), _rows(tm, D), _full((1, D))],
        out_specs=[_rows(tm, D), _rows(tm, D), _full((SUBLANES, D)), _full((SUBLANES, D))],
        out_shape=[jax.ShapeDtypeStruct((s, D), F32), jax.ShapeDtypeStruct((s, D), BF16),
                   jax.ShapeDtypeStruct((SUBLANES, D), F32), jax.ShapeDtypeStruct((SUBLANES, D), F32)],
        compiler_params=_cparams(48, ("arbitrary",)),
    )(a, w_down, x2, target, g_post)


def _ffn_bwd_act(dff, w_down, gate, up, *, tm):
    s = dff.shape[0]

    def body(dff_ref, w_ref, gate_ref, up_ref, dgate_ref, dup_ref):
        da = lax.dot_general(dff_ref[...], w_ref[...], NT, preferred_element_type=F32)
        g = gate_ref[...].astype(F32)
        sg = jax.nn.sigmoid(g)
        dgate_ref[...] = (da * up_ref[...].astype(F32) * (sg * (1.0 + g * (1.0 - sg)))).astype(BF16)
        dup_ref[...] = (da * (g * sg)).astype(BF16)

    return pl.pallas_call(
        body, name="ffn_bwd_act", grid=(s // tm,),
        in_specs=[_rows(tm, D), _resident((DFF, D)), _rows(tm, DFF), _rows(tm, DFF)],
        out_specs=[_rows(tm, DFF), _rows(tm, DFF)],
        out_shape=[jax.ShapeDtypeStruct((s, DFF), BF16)] * 2,
        compiler_params=_cparams(48, ("arbitrary",)),
    )(dff, w_down, gate, up)


def _grad_matmul(a, b, *, ta, tb, ts, name):
    s, ka = a.shape
    nb = b.shape[1]
    nk = s // ts

    def body(a_ref, b_ref, o_ref):
        @pl.when(pl.program_id(2) == 0)
        def _():
            o_ref[...] = jnp.zeros_like(o_ref)

        o_ref[...] += lax.dot_general(a_ref[...], b_ref[...], TN, preferred_element_type=F32)

    return pl.pallas_call(
        body, name=name, grid=(ka // ta, nb // tb, nk),
        in_specs=[pl.BlockSpec((ts, ta), lambda i, j, k: (k, i)), pl.BlockSpec((ts, tb), lambda i, j, k: (k, j))],
        out_specs=pl.BlockSpec((ta, tb), lambda i, j, k: (i, j)),
        out_shape=jax.ShapeDtypeStruct((ka, nb), F32),
        compiler_params=_cparams(48, ("arbitrary", "arbitrary", "arbitrary")),
    )(a, b)


def _ffn_bwd_in(dgate, dup, w_g, w_u, x2, g_pre, dx3, y, g_post, *, tm):
    s = x2.shape[0]

    def body(dgate_ref, dup_ref, wg_ref, wu_ref, x2_ref, gpre_ref, dx3_ref, y_ref, gpost_ref,
             dx2_ref, dy_ref, dgpre_ref, dgpost_ref):
        @pl.when(pl.program_id(0) == 0)
        def _():
            dgpre_ref[...] = jnp.zeros_like(dgpre_ref)
            dgpost_ref[...] = jnp.zeros_like(dgpost_ref)

        dh2 = (lax.dot_general(dgate_ref[...], wg_ref[...], NT, preferred_element_type=F32)
               + lax.dot_general(dup_ref[...], wu_ref[...], NT, preferred_element_type=F32))
        _, n2, r2 = _rms_fwd(x2_ref[...], gpre_ref[...])
        dxn, dg = _rms_bwd(dh2, n2, r2, gpre_ref[...])
        dgpre_ref[...] += _fold8(dg)
        dx2 = dx3_ref[...] + dxn
        dx2_ref[...] = dx2
        _, ny, ry = _rms_fwd(y_ref[...], gpost_ref[...])
        dy, dg2 = _rms_bwd(dx2, ny, ry, gpost_ref[...])
        dy_ref[...] = dy.astype(BF16)
        dgpost_ref[...] += _fold8(dg2)

    return pl.pallas_call(
        body, name="ffn_bwd_in", grid=(s // tm,),
        in_specs=[_rows(tm, DFF), _rows(tm, DFF), _resident((D, DFF)), _resident((D, DFF)), _rows(tm, D), _full((1, D)),
                  _rows(tm, D), _rows(tm, D), _full((1, D))],
        out_specs=[_rows(tm, D), _rows(tm, D), _full((SUBLANES, D)), _full((SUBLANES, D))],
        out_shape=[jax.ShapeDtypeStruct((s, D), F32), jax.ShapeDtypeStruct((s, D), BF16),
                   jax.ShapeDtypeStruct((SUBLANES, D), F32), jax.ShapeDtypeStruct((SUBLANES, D), F32)],
        compiler_params=_cparams(56, ("arbitrary",)),
    )(dgate, dup, w_g, w_u, x2, g_pre, dx3, y, g_post)


def _mix_bwd(dy, w_out, o, cv, bcu, ga, gc, gsum, *, tm):
    s = dy.shape[0]

    def group_norm_bwd(dn_out, v, g, gs):
        r = lax.rsqrt(_split_dot(v * v, gs) * (1.0 / DH) + EPS)
        n = v * r
        dn = dn_out * g
        return r * (dn - n * (_split_dot(dn * n, gs) * (1.0 / DH))), dn_out * n

    def body(dy_ref, w_ref, o_ref, cv_ref, bcu_ref, ga_ref, gc_ref, gs_ref,
             do_ref, dl_ref, dcv_ref, db_ref, dga_ref, dgc_ref):
        @pl.when(pl.program_id(0) == 0)
        def _():
            dga_ref[...] = jnp.zeros_like(dga_ref)
            dgc_ref[...] = jnp.zeros_like(dgc_ref)

        dm = lax.dot_general(dy_ref[...], w_ref[...], NT, preferred_element_type=F32)
        ov = o_ref[...]
        do, dga = group_norm_bwd(dm[:, 0:AW], ov, ga_ref[...], gs_ref[...])
        dob = do.astype(BF16)
        do_ref[...] = dob
        dl_ref[...] = _split_dot(dob.astype(F32) * ov, gs_ref[...])
        dga_ref[...] += _fold8(dga)
        gate_b = bcu_ref[:, 0:CW]
        cv = cv_ref[...]
        dconv, dgc = group_norm_bwd(dm[:, AW:D], gate_b * cv, gc_ref[...], gs_ref[...])
        dgc_ref[...] += _fold8(dgc)
        dcv_ref[...] = dconv * gate_b
        db_ref[...] = (dconv * cv).astype(BF16)

    return pl.pallas_call(
        body, name="mix_bwd", grid=(s // tm,),
        in_specs=[_rows(tm, D), _resident((D, D)), _rows(tm, AW), _rows(tm, CW), _rows(tm, 3 * CW),
                  _full((1, AW)), _full((1, CW)), _full((CW, CW))],
        out_specs=[_rows(tm, AW), _rows(tm, AW), _rows(tm, CW), _rows(tm, CW),
                   _full((SUBLANES, AW)), _full((SUBLANES, CW))],
        out_shape=[jax.ShapeDtypeStruct((s, AW), BF16), jax.ShapeDtypeStruct((s, AW), F32),
                   jax.ShapeDtypeStruct((s, CW), F32), jax.ShapeDtypeStruct((s, CW), BF16),
                   jax.ShapeDtypeStruct((SUBLANES, AW), F32), jax.ShapeDtypeStruct((SUBLANES, CW), F32)],
        compiler_params=_cparams(48, ("arbitrary",)),
    )(dy, w_out, o, cv, bcu, ga, gc, gsum)


def _conv_bwd(dcv, bcu, cw8, *, tm):
    s = dcv.shape[0]
    nt = s // tm

    def body(dcv_ref, nxt_ref, bcu_ref, halo_ref, cw_ref, dc_ref, du_ref, dw_ref):
        i = pl.program_id(0)

        @pl.when(i == 0)
        def _():
            dw_ref[...] = jnp.zeros_like(dw_ref)

        z, z1, z2 = _conv_taps(bcu_ref, halo_ref, i == 0, tm)
        d = dcv_ref[...]
        dw_ref[0] += _fold8(d * z2)
        dw_ref[1] += _fold8(d * z1)
        dw_ref[2] += _fold8(d * z)
        nx = jnp.where(i == nt - 1, 0.0, nxt_ref[...])
        row = lax.broadcasted_iota(jnp.int32, (tm, CW), 0)
        d1 = jnp.where(row == tm - 1, nx[0:1, :], pltpu.roll(d, tm - 1, axis=0))
        d2 = jnp.where(row == tm - 2, nx[0:1, :], jnp.where(row == tm - 1, nx[1:2, :], pltpu.roll(d, tm - 2, axis=0)))
        dz = cw_ref[2:3, :] * d + cw_ref[1:2, :] * d1 + cw_ref[0:1, :] * d2
        dc_ref[...] = (dz * bcu_ref[:, 2 * CW:3 * CW]).astype(BF16)
        du_ref[...] = (dz * bcu_ref[:, CW:2 * CW]).astype(BF16)

    return pl.pallas_call(
        body, name="conv_bwd", grid=(nt,),
        in_specs=[_rows(tm, CW),
                  pl.BlockSpec((SUBLANES, CW), lambda i: (jnp.minimum((i + 1) * (tm // SUBLANES), s // SUBLANES - 1), 0)),
                  _rows(tm, 3 * CW), _halo_before(tm, 3 * CW), _full((SUBLANES, CW))],
        out_specs=[_rows(tm, CW), _rows(tm, CW), _full((3, SUBLANES, CW))],
        out_shape=[jax.ShapeDtypeStruct((s, CW), BF16), jax.ShapeDtypeStruct((s, CW), BF16),
                   jax.ShapeDtypeStruct((3, SUBLANES, CW), F32)],
        compiler_params=_cparams(48, ("arbitrary",)),
    )(dcv, dcv, bcu, bcu, cw8)


def _attn_bwd(qp, kp, v, do, lse, dl, *, t):
    s = qp.shape[0]
    nq = s // t

    def body(q_ref, k_ref, v_ref, do_ref, lse_ref, dl_ref, dq_ref, dk_ref, dv_ref):
        ki = pl.program_id(1)

        @pl.when(ki == 0)
        def _():
            dq_ref[...] = jnp.zeros_like(dq_ref)

        row = lax.broadcasted_iota(jnp.int32, (t, t), 0)
        col = lax.broadcasted_iota(jnp.int32, (t, t), 1)
        lane = lax.broadcasted_iota(jnp.int32, (t, 128), 1)
        v2 = v_ref[...]
        dv = jnp.zeros((t, 128), F32)
        for hh in range(2):
            kh = k_ref[:, HP * hh:HP * (hh + 1)]
            in_head = (lane >= DH * hh) & (lane < DH * (hh + 1))

            def step(qi, carry, masked, hh=hh, kh=kh, in_head=in_head):
                dk, dv = carry
                off = pl.multiple_of(qi * t, t)
                q = q_ref[pl.ds(off, t), HP * hh:HP * (hh + 1)]
                dom = jnp.where(in_head, do_ref[pl.ds(off, t), :], jnp.zeros((), BF16))
                sc = lax.dot_general(q, kh, NT, preferred_element_type=F32)
                sc = sc - lse_ref[pl.ds(off, t), DH * hh:DH * hh + 1]
                if masked:
                    sc = jnp.where(col <= row, sc, -1e30)
                p = jnp.exp(sc)
                dp = lax.dot_general(dom, v2, NT, preferred_element_type=F32)
                ds32 = p * (dp - dl_ref[pl.ds(off, t), DH * hh:DH * hh + 1])
                ds = ds32.astype(BF16)
                ds_lo = (ds32 - ds.astype(F32)).astype(BF16)
                dv = dv + lax.dot_general(p.astype(BF16), dom, TN, preferred_element_type=F32)
                dk = dk + (lax.dot_general(ds, q, TN, preferred_element_type=F32)
                           + lax.dot_general(ds_lo, q, TN, preferred_element_type=F32))
                dq_ref[pl.ds(off, t), HP * hh:HP * (hh + 1)] += jnp.dot(ds, kh, preferred_element_type=F32)
                return dk, dv

            carry = step(ki, (jnp.zeros((t, HP), F32), dv), True)
            dk, dv = lax.fori_loop(ki + 1, nq, functools.partial(step, masked=False), carry)
            dk_ref[:, HP * hh:HP * (hh + 1)] = dk
        dv_ref[...] = dv.astype(BF16)

    return pl.pallas_call(
        body, name="attn_bwd", grid=(H // 2, nq),
        in_specs=[pl.BlockSpec((s, 2 * HP), lambda p, i: (0, p)),
                  pl.BlockSpec((t, 2 * HP), lambda p, i: (i, p)),
                  pl.BlockSpec((t, 128), lambda p, i: (i, p)),
                  pl.BlockSpec((s, 128), lambda p, i: (0, p)),
                  pl.BlockSpec((s, 128), lambda p, i: (0, p)),
                  pl.BlockSpec((s, 128), lambda p, i: (0, p))],
        out_specs=[pl.BlockSpec((s, 2 * HP), lambda p, i: (0, p)),
                   pl.BlockSpec((t, 2 * HP), lambda p, i: (i, p)),
                   pl.BlockSpec((t, 128), lambda p, i: (i, p))],
        out_shape=[jax.ShapeDtypeStruct((s, 1024), F32), jax.ShapeDtypeStruct((s, 1024), F32),
                   jax.ShapeDtypeStruct((s, AW), BF16)],
        compiler_params=_cparams(56, ("arbitrary", "arbitrary")),
    )(qp, kp, v, do, lse, dl)


def _forget_bwd(dkp, z, sel, *, tm):
    s = dkp.shape[0]
    nt = s // tm

    def body(dk_ref, z_ref, sel_ref, dfl_ref, dbf_ref, carry):
        @pl.when(pl.program_id(0) == 0)
        def _():
            carry[...] = jnp.zeros_like(carry)
            dbf_ref[...] = jnp.zeros_like(dbf_ref)

        dc = _split_dot(dk_ref[...], sel_ref[...])
        row = lax.broadcasted_iota(jnp.int32, (tm, tm), 0)
        col = lax.broadcasted_iota(jnp.int32, (tm, tm), 1)
        tri = (col >= row).astype(F32)
        dlogf = jnp.dot(tri, dc, precision=HIGHEST, preferred_element_type=F32) + carry[0:1, :]
        carry[...] = jnp.broadcast_to(dlogf[0:1, :], carry.shape)
        dz = dlogf * (1.0 - jax.nn.sigmoid(z_ref[...]))
        dfl_ref[...] = dz.astype(BF16)
        dbf_ref[...] += _fold8(dz)

    rev = lambda i: (nt - 1 - i, 0)
    return pl.pallas_call(
        body, name="forget_bwd", grid=(nt,),
        in_specs=[pl.BlockSpec((tm, 1024), rev), pl.BlockSpec((tm, 128), rev), _full((1024, 128))],
        out_specs=[pl.BlockSpec((tm, 128), rev), _full((SUBLANES, 128))],
        out_shape=[jax.ShapeDtypeStruct((s, 128), BF16), jax.ShapeDtypeStruct((SUBLANES, 128), F32)],
        scratch_shapes=[pltpu.VMEM((SUBLANES, 128), F32)],
        compiler_params=_cparams(48, ("arbitrary",)),
    )(dkp, z, sel)


def _in_proj_bwd(dproj, wp, x, g1, dx2, *, tm):
    s = x.shape[0]

    def body(dp_ref, w_ref, x_ref, g_ref, dx2_ref, dx_ref, dg_ref):
        @pl.when(pl.program_id(0) == 0)
        def _():
            dg_ref[...] = jnp.zeros_like(dg_ref)

        dh = lax.dot_general(dp_ref[...], w_ref[...], NT, preferred_element_type=F32)
        _, n, r = _rms_fwd(x_ref[...], g_ref[...])
        dxn, dg = _rms_bwd(dh, n, r, g_ref[...])
        dx_ref[...] = dx2_ref[...] + dxn
        dg_ref[...] += _fold8(dg)

    return pl.pallas_call(
        body, name="in_proj_bwd", grid=(s // tm,),
        in_specs=[_rows(tm, WP), _resident((D, WP)), _rows(tm, D), _full((1, D)), _rows(tm, D)],
        out_specs=[_rows(tm, D), _full((SUBLANES, D))],
        out_shape=[jax.ShapeDtypeStruct((s, D), F32), jax.ShapeDtypeStruct((SUBLANES, D), F32)],
        compiler_params=_cparams(56, ("arbitrary",)),
    )(dproj, wp, x, g1, dx2)


def _position():
    return lax.axis_index("x"), lax.axis_index("y"), lax.axis_index("c")


ANY = pl.BlockSpec(memory_space=pl.ANY)


def _all_gather(shard):
    def body(x_ref, out_ref, send_sems, recv_sems, local_sem):
        x, y, c = _position()
        me, sibling = (x, y, c), (x, y, 1 - c)
        chips = [(1 - x, y), (x, 1 - y), (1 - x, 1 - y)]

        def slot(px, py, pc):
            return out_ref.at[4 * px + 2 * py + pc]

        def copy(k, block, to, src=None):
            return pltpu.make_async_remote_copy(
                src_ref=slot(*block) if src is None else src, dst_ref=slot(*block),
                send_sem=send_sems.at[k], recv_sem=recv_sems.at[k], device_id=to, device_id_type=MESH_ID)

        mine = pltpu.make_async_copy(x_ref, slot(*me), local_sem)
        mine.start()
        first = [copy(0, me, sibling, src=x_ref)]
        first += [copy(1 + j, me, (*chip, c), src=x_ref) for j, chip in enumerate(chips)]
        for cp in first:
            cp.start()
        passed = [copy(4 + j, (*chip, c), sibling) for j, chip in enumerate(chips)]
        for j, chip in enumerate(chips):
            copy(1 + j, (*chip, c), me).wait_recv()
            passed[j].start()
        copy(0, sibling, me).wait_recv()
        for j, chip in enumerate(chips):
            copy(4 + j, (*chip, 1 - c), me).wait_recv()
        for cp in first + passed:
            cp.wait_send()
        mine.wait()

    return pl.pallas_call(
        body, name="all_gather_weights",
        out_shape=jax.ShapeDtypeStruct((NDEV,) + shard.shape, shard.dtype),
        in_specs=[ANY], out_specs=ANY,
        scratch_shapes=[pltpu.SemaphoreType.DMA((7,)), pltpu.SemaphoreType.DMA((7,)), pltpu.SemaphoreType.DMA],
    )(shard)


def _pair_exchange(gc_b):
    def body(g_ref, out_ref, send_sem, recv_sem):
        x, y, c = _position()
        cp = pltpu.make_async_remote_copy(
            src_ref=g_ref.at[1 - c], dst_ref=out_ref, send_sem=send_sem, recv_sem=recv_sem,
            device_id=(x, y, 1 - c), device_id_type=MESH_ID)
        cp.start()
        cp.wait()

    return pl.pallas_call(
        body, name="grad_pair_exchange",
        out_shape=jax.ShapeDtypeStruct(gc_b.shape[1:], gc_b.dtype),
        in_specs=[ANY], out_specs=ANY,
        scratch_shapes=[pltpu.SemaphoreType.DMA, pltpu.SemaphoreType.DMA],
    )(gc_b)


def _pair_sum(gc, got, idx, *, tr):
    rows = gc.shape[2]

    def body(idx_ref, g_ref, got_ref, pb_ref, own_ref):
        p = g_ref[0, 0] + got_ref[0].astype(F32)
        pb_ref[0] = p.astype(BF16)

        @pl.when(pl.program_id(1) == idx_ref[1])
        def _():
            own_ref[...] = p

    return pl.pallas_call(
        body, name="grad_pair_sum",
        grid_spec=pltpu.PrefetchScalarGridSpec(
            num_scalar_prefetch=1, grid=(rows // tr, 4),
            in_specs=[pl.BlockSpec((1, 1, tr, LANES), lambda i, j, idx: (idx[0], j, i, 0)),
                      pl.BlockSpec((1, tr, LANES), lambda i, j, idx: (j, i, 0))],
            out_specs=[pl.BlockSpec((1, tr, LANES), lambda i, j, idx: (j, i, 0)),
                       pl.BlockSpec((tr, LANES), lambda i, j, idx: (i, 0))]),
        out_shape=[jax.ShapeDtypeStruct((4, rows, LANES), BF16), jax.ShapeDtypeStruct((rows, LANES), F32)],
        compiler_params=_cparams(32, ("arbitrary", "arbitrary")),
    )(idx, gc, got)


def _chip_exchange(pb):
    def body(p_ref, out_ref, send_sems, recv_sems, local_sem):
        x, y, c = _position()
        my_chip = 2 * x + y
        mine = pltpu.make_async_copy(p_ref.at[my_chip], out_ref.at[my_chip], local_sem)
        mine.start()
        chips = [(1 - x, y), (x, 1 - y), (1 - x, 1 - y)]
        copies = [
            pltpu.make_async_remote_copy(
                src_ref=p_ref.at[2 * px + py], dst_ref=out_ref.at[my_chip],
                send_sem=send_sems.at[j], recv_sem=recv_sems.at[j], device_id=(px, py, c), device_id_type=MESH_ID)
            for j, (px, py) in enumerate(chips)]
        for cp in copies:
            cp.start()
        for j, (px, py) in enumerate(chips):
            pltpu.make_async_remote_copy(
                src_ref=p_ref.at[my_chip], dst_ref=out_ref.at[2 * px + py],
                send_sem=send_sems.at[j], recv_sem=recv_sems.at[j], device_id=(px, py, c),
                device_id_type=MESH_ID).wait_recv()
        for cp in copies:
            cp.wait_send()
        mine.wait()

    return pl.pallas_call(
        body, name="grad_chip_exchange",
        out_shape=jax.ShapeDtypeStruct(pb.shape, pb.dtype),
        in_specs=[ANY], out_specs=ANY,
        scratch_shapes=[pltpu.SemaphoreType.DMA((3,)), pltpu.SemaphoreType.DMA((3,)), pltpu.SemaphoreType.DMA],
    )(pb)


def _chip_sum(got, own, idx, *, tr):
    rows = own.shape[0]

    def body(idx_ref, got_ref, own_ref, o_ref):
        acc = jnp.zeros((tr, LANES), F32)
        for j in range(4):
            acc = acc + jnp.where(idx_ref[1] == j, own_ref[...], got_ref[j].astype(F32))
        o_ref[...] = acc

    return pl.pallas_call(
        body, name="grad_chip_sum",
        grid_spec=pltpu.PrefetchScalarGridSpec(
            num_scalar_prefetch=1, grid=(rows // tr,),
            in_specs=[pl.BlockSpec((4, tr, LANES), lambda i, idx: (0, i, 0)),
                      pl.BlockSpec((tr, LANES), lambda i, idx: (i, 0))],
            out_specs=pl.BlockSpec((tr, LANES), lambda i, idx: (i, 0))),
        out_shape=jax.ShapeDtypeStruct((rows, LANES), F32),
        compiler_params=_cparams(32, ("arbitrary",)),
    )(idx, got, own)


def _small_all_reduce(parts):
    def body(gmp_ref, gmo_ref, gfp_ref, gfo_ref, ga_ref, gc_ref, dw_ref, bf_ref, loss_ref,
             out_ref, buf, send_sems, recv_sems):
        x, y, c = _position()
        me = 4 * x + 2 * y + c

        def colsum(v):
            return jnp.sum(v, axis=0, keepdims=True)

        loss = jnp.sum(colsum(loss_ref[...]), axis=1, keepdims=True) * (0.5 / D)
        rows = [colsum(gmp_ref[...]), colsum(gmo_ref[...]), colsum(gfp_ref[...]), colsum(gfo_ref[...]),
                jnp.concatenate([colsum(ga_ref[...]), colsum(gc_ref[...])], axis=1),
                jnp.concatenate([colsum(dw_ref[0]), colsum(dw_ref[1])], axis=1),
                jnp.concatenate([colsum(dw_ref[2]), colsum(bf_ref[...]), jnp.broadcast_to(loss, (1, 128)),
                                 jnp.zeros((1, 256), F32)], axis=1),
                jnp.zeros((1, D), F32)]
        buf[me] = jnp.concatenate(rows, axis=0)
        copies = []
        for mm in range(1, NDEV):
            peer = (x ^ (mm >> 2), y ^ ((mm >> 1) & 1), c ^ (mm & 1))
            copies.append(pltpu.make_async_remote_copy(
                src_ref=buf.at[me], dst_ref=buf.at[me], send_sem=send_sems.at[mm - 1], recv_sem=recv_sems.at[mm - 1],
                device_id=peer, device_id_type=MESH_ID))
        for cp in copies:
            cp.start()
        for cp in copies:
            cp.wait_recv()
        for cp in copies:
            cp.wait_send()
        acc = buf[0]
        for d in range(1, NDEV):
            acc = acc + buf[d]
        out_ref[...] = acc

    vm = pl.BlockSpec(memory_space=pltpu.VMEM)
    return pl.pallas_call(
        body, name="small_all_reduce",
        out_shape=jax.ShapeDtypeStruct((SUBLANES, D), F32),
        in_specs=[vm] * len(parts), out_specs=vm,
        scratch_shapes=[pltpu.VMEM((NDEV, SUBLANES, D), F32), pltpu.SemaphoreType.DMA((7,)), pltpu.SemaphoreType.DMA((7,))],
    )(*parts)


def _adamw(w, g, m, v, *, tr, name):
    rows, cols = w.shape

    def body(w_ref, g_ref, m_ref, v_ref, d_ref, nm_ref, nv_ref):
        gv = g_ref[...]
        nm = ADAM_B1 * m_ref[...] + (1.0 - ADAM_B1) * gv
        nv = ADAM_B2 * v_ref[...] + (1.0 - ADAM_B2) * (gv * gv)
        m_hat = nm / (1.0 - ADAM_B1 ** ADAM_STEP)
        v_hat = nv / (1.0 - ADAM_B2 ** ADAM_STEP)
        d_ref[...] = -ADAM_LR * (m_hat / (jnp.sqrt(v_hat) + ADAM_EPS) + ADAM_WD * w_ref[...])
        nm_ref[...] = nm
        nv_ref[...] = nv

    spec = pl.BlockSpec((tr, cols), lambda i: (i, 0))
    return pl.pallas_call(
        body, name=name, grid=(rows // tr,),
        in_specs=[spec] * 4, out_specs=[spec] * 3,
        out_shape=[jax.ShapeDtypeStruct((rows, cols), F32)] * 3,
        compiler_params=_cparams(32, ("arbitrary",)),
    )(w, g, m, v)


def _placement_constants():
    j = jnp.arange(128)[:, None]
    lane = jnp.arange(1024)[None, :]
    head, sub = lane // HP, lane % HP
    piece, jh = j // H, j % H
    valid = (j < 3 * H) & (jh == head)
    pq = jnp.where(valid & (sub == DH + piece), 1.0, 0.0).astype(BF16)
    pk = jnp.where(valid & (sub == DH + 3 + piece), -1.0, 0.0).astype(BF16)
    sub1 = sub[0:1]
    oq = jnp.where((sub1 >= DH + 3) & (sub1 < DH + 6), 1.0, 0.0).astype(F32)
    ok = jnp.where((sub1 >= DH) & (sub1 < DH + 3), 1.0, 0.0).astype(F32)
    r = jnp.arange(1024)[:, None]
    cc = jnp.arange(128)[None, :]
    sel = jnp.where((r % HP == DH + 3) & (r // HP == cc), -1.0, 0.0).astype(BF16)
    gi = jnp.arange(CW)
    gsum = (gi[:, None] // DH == gi[None, :] // DH).astype(BF16)
    return pq, pk, oq, ok, sel, gsum


def _pad_heads(w):
    return jnp.pad(w.reshape(D, H, DH), ((0, 0), (0, 0), (0, HP - DH))).reshape(D, H * HP)


def _unpad_heads(w):
    return w.reshape(D, H, HP)[:, :, :DH].reshape(D, H * DH)


def _pack_shard(w_in, w_out, w_gate_up, w_down):
    parts = [w_in.reshape(ROWS_IN, LANES), w_out.reshape(ROWS_OUT, LANES),
             w_gate_up.reshape(ROWS_GU, LANES), w_down.reshape(ROWS_DOWN, LANES),
             jnp.zeros((ROWS - ROWS_IN - ROWS_OUT - ROWS_GU - ROWS_DOWN, LANES), w_in.dtype)]
    return jnp.concatenate(parts, axis=0)


def _unpack(buf):
    o1, o2, o3 = ROWS_IN, ROWS_IN + ROWS_OUT, ROWS_IN + ROWS_OUT + ROWS_GU
    return buf[..., :o1, :], buf[..., o1:o2, :], buf[..., o2:o3, :], buf[..., o3:o3 + ROWS_DOWN, :]


def _local_step(xs, tgt, wp, w_out_f, w_g, w_u, w_down_f, cw8, bfp, g_attn_out, g_conv_out,
                g_mix_pre, g_mix_post, g_ffn_pre, g_ffn_post):
    pq, pk, oq, ok, sel, gsum = _placement_constants()
    h1, qp, kp, vv, bcu, zf = _in_proj(xs, g_mix_pre, wp, bfp, pq, pk, oq, ok, tm=512)
    o, lse = _attn_fwd(qp, kp, vv, t=512)
    merged, y, x2, cv = _mix_out(o, bcu, cw8, g_attn_out, g_conv_out, gsum, w_out_f, xs, g_mix_post, tm=512)
    h2, gate, up, act = _ffn_up(x2, g_ffn_pre, w_g, w_u, tm=256)
    dx3, dff, loss_p, dg_ffn_post = _ffn_down_loss(act, w_down_f, x2, tgt, g_ffn_post, tm=512)

    dgate, dup = _ffn_bwd_act(dff, w_down_f, gate, up, tm=256)
    dw_down = _grad_matmul(act, dff, ta=1408, tb=1024, ts=512, name="grad_w_down")
    dw_g = _grad_matmul(h2, dgate, ta=1024, tb=1408, ts=512, name="grad_w_gate")
    dw_u = _grad_matmul(h2, dup, ta=1024, tb=1408, ts=512, name="grad_w_up")
    dx2, dy, dg_ffn_pre, dg_mix_post = _ffn_bwd_in(dgate, dup, w_g, w_u, x2, g_ffn_pre, dx3, y, g_mix_post, tm=256)
    dw_out = _grad_matmul(merged, dy, ta=1024, tb=1024, ts=512, name="grad_w_out")
    do, dl, dcv, db, dg_attn, dg_conv = _mix_bwd(dy, w_out_f, o, cv, bcu, g_attn_out, g_conv_out, gsum, tm=512)
    dc, du, dtaps = _conv_bwd(dcv, bcu, cw8, tm=512)
    dqp, dkp, dv = _attn_bwd(qp, kp, vv, do, lse, dl, t=512)
    dfl, dbf = _forget_bwd(dkp, zf, sel, tm=512)
    dproj = jnp.concatenate([dqp.astype(BF16), dkp.astype(BF16), dv, db, dc, du, dfl], axis=1)
    dwp = _grad_matmul(h1, dproj, ta=1024, tb=1408, ts=512, name="grad_w_in")
    grad_x, dg_mix_pre = _in_proj_bwd(dproj, wp, xs, g_mix_pre, dx2, tm=512)

    return (grad_x, dwp, dw_out, dw_g, dw_u, dw_down, dg_mix_pre, dg_mix_post, dg_ffn_pre, dg_ffn_post, dg_attn, dg_conv,
            dtaps, dbf, loss_p)


def kernel(x, w_in, b_forget, conv_w, g_attn_out, g_conv_out, w_out, g_mix_pre, g_mix_post, w_gate_up, w_down, g_ffn_pre, g_ffn_post, loss_target, m_w_in, m_b_forget, m_conv_w, m_g_attn_out, m_g_conv_out, m_w_out, m_g_mix_pre, m_g_mix_post, m_w_gate_up, m_w_down, m_g_ffn_pre, m_g_ffn_post, v_w_in, v_b_forget, v_conv_w, v_g_attn_out, v_g_conv_out, v_w_out, v_g_mix_pre, v_g_mix_post, v_w_gate_up, v_w_down, v_g_ffn_pre, v_g_ffn_post):
    xc, yc, cc = _position()
    my_chip = 2 * xc + yc
    me = 2 * my_chip + cc
    idx = jnp.stack([cc, my_chip]).astype(jnp.int32)
    shard = _pack_shard(w_in[0], w_out[0], w_gate_up[0], w_down[0]).astype(BF16)
    gathered = _all_gather(shard)
    b_in, b_out, b_gu, b_down = _unpack(gathered)
    w_in_f = b_in.reshape(NDEV, D, 385).transpose(1, 0, 2).reshape(D, 3080)
    f_cols = jnp.pad(w_in_f[:, 3 * AW:3 * AW + H], ((0, 0), (0, 128 - H)))
    wp = jnp.concatenate([_pad_heads(w_in_f[:, 0:AW] * 0.125), _pad_heads(w_in_f[:, AW:2 * AW]),
                          w_in_f[:, 2 * AW:3 * AW], w_in_f[:, 3 * AW + H:], f_cols], axis=1)
    w_out_f = b_out.reshape(D, D)
    w_gu_b = b_gu.reshape(NDEV, D, 704)
    w_g = w_gu_b[:4].transpose(1, 0, 2).reshape(D, DFF)
    w_u = w_gu_b[4:].transpose(1, 0, 2).reshape(D, DFF)
    w_down_f = b_down.reshape(DFF, D)

    xs, tgt = x[0], loss_target[0]
    bfp = jnp.pad(b_forget, ((0, 0), (0, 128 - H)))

    zero8 = jnp.zeros((SUBLANES, D), F32)
    taps = jnp.zeros((3, SUBLANES, CW), F32)
    taps = lax.dynamic_update_slice(taps, conv_w[0][:, None, :], (0, 0, me * 64))
    gathered_small = _small_all_reduce([zero8, zero8, zero8, zero8, zero8[:, :AW], zero8[:, :CW], taps,
                                        zero8[:, :128], zero8])
    cw8 = jnp.concatenate([gathered_small[5:6, 0:CW], gathered_small[5:6, CW:], gathered_small[6:7, 0:CW],
                           jnp.zeros((5, CW), F32)], axis=0)

    (grad_x, dwp, dw_out, dw_g, dw_u, dw_down, dg_mix_pre, dg_mix_post, dg_ffn_pre, dg_ffn_post, dg_attn, dg_conv,
     dtaps, dbf, loss_p) = _local_step(xs, tgt, wp, w_out_f, w_g, w_u, w_down_f, cw8, bfp, g_attn_out, g_conv_out,
                                        g_mix_pre, g_mix_post, g_ffn_pre, g_ffn_post)

    dw_in = jnp.concatenate([_unpad_heads(dwp[:, OFF_Q:OFF_Q + 1024]) * 0.125, _unpad_heads(dwp[:, OFF_K:OFF_K + 1024]),
                             dwp[:, OFF_V:OFF_V + AW], dwp[:, OFF_F:OFF_F + H], dwp[:, OFF_BCU:OFF_BCU + 3 * CW]], axis=1)
    blk_in = dw_in.reshape(D, NDEV, 385).transpose(1, 0, 2).reshape(NDEV, ROWS_IN, LANES)
    blk_out = dw_out.reshape(NDEV, ROWS_OUT, LANES)
    blk_gu = jnp.concatenate([dw_g, dw_u], axis=1).reshape(D, NDEV, 704).transpose(1, 0, 2).reshape(NDEV, ROWS_GU, LANES)
    blk_down = dw_down.reshape(NDEV, ROWS_DOWN, LANES)
    pad = jnp.zeros((NDEV, ROWS - ROWS_IN - ROWS_OUT - ROWS_GU - ROWS_DOWN, LANES), F32)
    g_all = jnp.concatenate([blk_in, blk_out, blk_gu, blk_down, pad], axis=1)
    gc = g_all.reshape(4, 2, ROWS, LANES).transpose(1, 0, 2, 3)
    from_sibling = _pair_exchange(gc.astype(BF16))
    pair_b, pair_own = _pair_sum(gc, from_sibling, idx, tr=2512)
    from_chips = _chip_exchange(pair_b)
    g_shard = _chip_sum(from_chips, pair_own, idx, tr=2512)
    s_in, s_out, s_gu, s_down = _unpack(g_shard)
    grad_w_in = s_in.reshape(D, 385)
    grad_w_out = s_out.reshape(128, D)
    grad_w_gu = s_gu.reshape(D, 704)
    grad_w_down = s_down.reshape(352, D)

    small = _small_all_reduce([dg_mix_pre, dg_mix_post, dg_ffn_pre, dg_ffn_post, dg_attn, dg_conv, dtaps, dbf, loss_p])
    grad_g_mix_pre, grad_g_mix_post = small[0:1], small[1:2]
    grad_g_ffn_pre, grad_g_ffn_post = small[2:3], small[3:4]
    grad_g_attn, grad_g_conv = small[4:5, :AW], small[4:5, AW:]
    taps_full = jnp.concatenate([small[5:6, :CW], small[5:6, CW:], small[6:7, :CW]], axis=0)
    grad_conv_w = lax.dynamic_slice(taps_full, (0, me * 64), (3, 64))
    grad_b_forget = small[6:7, CW:CW + H]
    loss = small[6, CW + 128]

    def upd(w, g, m, v, tr, name):
        d, nm, nv = _adamw(w[0], g, m[0], v[0], tr=tr, name=name)
        return g[None], d[None], nm[None], nv[None]

    res = {
        "w_in": upd(w_in, grad_w_in, m_w_in, v_w_in, 256, "adamw_w_in"),
        "b_forget": upd(b_forget[None], grad_b_forget, m_b_forget[None], v_b_forget[None], 1, "adamw_b_forget"),
        "conv_w": upd(conv_w, grad_conv_w, m_conv_w, v_conv_w, 3, "adamw_conv_w"),
        "g_attn_out": upd(g_attn_out[None], grad_g_attn, m_g_attn_out[None], v_g_attn_out[None], 1, "adamw_g_attn_out"),
        "g_conv_out": upd(g_conv_out[None], grad_g_conv, m_g_conv_out[None], v_g_conv_out[None], 1, "adamw_g_conv_out"),
        "w_out": upd(w_out, grad_w_out, m_w_out, v_w_out, 128, "adamw_w_out"),
        "g_mix_pre": upd(g_mix_pre[None], grad_g_mix_pre, m_g_mix_pre[None], v_g_mix_pre[None], 1, "adamw_g_mix_pre"),
        "g_mix_post": upd(g_mix_post[None], grad_g_mix_post, m_g_mix_post[None], v_g_mix_post[None], 1, "adamw_g_mix_post"),
        "w_gate_up": upd(w_gate_up, grad_w_gu, m_w_gate_up, v_w_gate_up, 256, "adamw_w_gate_up"),
        "w_down": upd(w_down, grad_w_down, m_w_down, v_w_down, 176, "adamw_w_down"),
        "g_ffn_pre": upd(g_ffn_pre[None], grad_g_ffn_pre, m_g_ffn_pre[None], v_g_ffn_pre[None], 1, "adamw_g_ffn_pre"),
        "g_ffn_post": upd(g_ffn_post[None], grad_g_ffn_post, m_g_ffn_post[None], v_g_ffn_post[None], 1, "adamw_g_ffn_post"),
    }
    order = ["w_in", "b_forget", "conv_w", "g_attn_out", "g_conv_out", "w_out", "g_mix_pre", "g_mix_post",
             "w_gate_up", "w_down", "g_ffn_pre", "g_ffn_post"]
    small_names = {"b_forget", "g_attn_out", "g_conv_out", "g_mix_pre", "g_mix_post", "g_ffn_pre", "g_ffn_post"}

    def leaf(name, k):
        a = res[name][k]
        return a[0] if name in small_names else a

    outs = [loss, grad_x[None]]
    for k in range(4):
        outs += [leaf(n, k) for n in order]
    return tuple(outs)
```

```python
import functools

import numpy as np

import jax
import jax.numpy as jnp
from jax import lax
from jax.experimental import pallas as pl
from jax.experimental.pallas import tpu as pltpu

F32 = jnp.float32
BF16 = jnp.bfloat16
HIGHEST = lax.Precision.HIGHEST
MESH_ID = pl.DeviceIdType.MESH

D = 1024
H = 8
DH = 64
AW = 512
CW = 512
DFF = 2816
FB = DFF // 4
HP = 128
OFF_Q, OFF_K, OFF_V, OFF_BCU, OFF_F = 0, 1024, 2048, 2560, 4096
WP = OFF_F + 128
PIECES = ((OFF_Q, OFF_K), (OFF_K, OFF_V), (OFF_V, OFF_BCU), (OFF_BCU, OFF_F), (OFF_F, WP))
EPS = 1e-6
NDEV = 8
LANES = 128
SUBLANES = 8
IN_COLS = 385
IN_PAD = 512
WIN = 896

ADAM_LR, ADAM_B1, ADAM_B2, ADAM_EPS, ADAM_WD, ADAM_STEP = 0.001, 0.9, 0.999, 1e-08, 0.01, 10

NT = (((1,), (1,)), ((), ()))
TN = (((0,), (0,)), ((), ()))


def _cparams(vmem_mb=None, sem=None):
    kw = {}
    if vmem_mb is not None:
        kw["vmem_limit_bytes"] = vmem_mb << 20
    if sem is not None:
        kw["dimension_semantics"] = sem
    return pltpu.CompilerParams(**kw)


def _full(shape):
    return pl.BlockSpec(shape, lambda *_: (0,) * len(shape))


def _resident(shape):
    return pl.BlockSpec(shape, lambda *_: (0,) * len(shape), pipeline_mode=pl.Buffered(1))


def _rows(tm, width):
    return pl.BlockSpec((tm, width), lambda i: (i, 0))


def _fold8(v):
    r, w = v.shape
    return jnp.sum(v.reshape(r // SUBLANES, SUBLANES, w), axis=0)


def _split_dot(v, m01):
    hi = v.astype(BF16)
    lo = (v - hi.astype(F32)).astype(BF16)
    return (jnp.dot(hi, m01, preferred_element_type=F32)
            + jnp.dot(lo, m01, preferred_element_type=F32))


def _rms_fwd(v, g):
    r = lax.rsqrt(jnp.mean(v * v, axis=-1, keepdims=True) + EPS)
    n = v * r
    return n * g, n, r


def _rms_bwd(do, n, r, g):
    dn = do * g
    return r * (dn - n * jnp.mean(dn * n, axis=-1, keepdims=True)), do * n


def _padded_column(n):
    if n < AW:
        return OFF_Q + HP * (n // DH) + n % DH, 0.125
    if n < 2 * AW:
        m = n - AW
        return OFF_K + HP * (m // DH) + m % DH, 1.0
    if n < 3 * AW:
        return OFF_V + n - 2 * AW, 1.0
    if n < 3 * AW + H:
        return OFF_F + n - 3 * AW, 1.0
    return OFF_BCU + n - 3 * AW - H, 1.0


def _in_layout_tables():
    dest = -np.ones((IN_PAD, LANES), np.int32)
    dest_f = -np.ones((IN_PAD, LANES), np.int32)
    scale = np.zeros((IN_PAD, LANES), np.float32)
    starts = []
    for k in range(NDEV):
        cols = [_padded_column(IN_COLS * k + j) for j in range(IN_COLS)]
        main = [c for c, _ in cols if c < OFF_F]
        ws = min((min(main) // LANES) * LANES, OFF_F - WIN)
        assert ws <= min(main) and max(main) < ws + WIN
        starts.append(ws)
        for j, (c, sc) in enumerate(cols):
            scale[j, k] = sc
            if c < OFF_F:
                dest[j, k] = c - ws
            else:
                dest_f[j, k] = c - OFF_F
    f_shards = tuple(k for k in range(NDEV) if (dest_f[:, k] >= 0).any())
    return tuple(starts), f_shards, jnp.asarray(dest), jnp.asarray(dest_f), jnp.asarray(scale)


def _perm(dest_ref, scale_ref, k, width):
    lane = lax.broadcasted_iota(jnp.int32, (IN_PAD, width), 1)
    return jnp.where(dest_ref[:, k:k + 1] == lane, scale_ref[:, k:k + 1], 0.0).astype(BF16)


def _assemble_w_in(blocks, tables, *, tr):
    starts, f_shards, dest, dest_f, scale = tables

    def body(b_ref, dest_ref, destf_ref, scale_ref, o_ref):
        o_ref[...] = jnp.zeros_like(o_ref)
        for k in range(NDEV):
            b = b_ref[k]
            ws = starts[k]
            part = jnp.dot(b, _perm(dest_ref, scale_ref, k, WIN), preferred_element_type=F32)
            o_ref[:, ws:ws + WIN] += part.astype(BF16)
            if k in f_shards:
                part = jnp.dot(b, _perm(destf_ref, scale_ref, k, 128), preferred_element_type=F32)
                o_ref[:, OFF_F:WP] += part.astype(BF16)

    tab = _full((IN_PAD, LANES))
    return pl.pallas_call(
        body, name="assemble_w_in", grid=(D // tr,),
        in_specs=[pl.BlockSpec((NDEV, tr, IN_PAD), lambda i: (0, i, 0)), tab, tab, tab],
        out_specs=_rows(tr, WP),
        out_shape=jax.ShapeDtypeStruct((D, WP), BF16),
        compiler_params=_cparams(48, ("arbitrary",)),
    )(blocks, dest, dest_f, scale)


def _disassemble_w_in(pieces, tables, *, tr):
    starts, f_shards, dest, dest_f, scale = tables

    def body(q_ref, k_ref, v_ref, bcu_ref, f_ref, dest_ref, destf_ref, scale_ref, o_ref):
        refs = (q_ref, k_ref, v_ref, bcu_ref, f_ref)

        def window(ws):
            parts = []
            for ref, (lo, hi) in zip(refs, PIECES):
                a, b = max(ws, lo), min(ws + WIN, hi)
                if a < b:
                    parts.append(ref[:, a - lo:b - lo])
            return parts[0] if len(parts) == 1 else jnp.concatenate(parts, axis=1)

        for k in range(NDEV):
            acc = lax.dot_general(window(starts[k]), _perm(dest_ref, scale_ref, k, WIN), NT, preferred_element_type=F32)
            if k in f_shards:
                acc = acc + lax.dot_general(f_ref[...], _perm(destf_ref, scale_ref, k, 128), NT, preferred_element_type=F32)
            o_ref[k] = acc.astype(BF16)

    tab = _full((IN_PAD, LANES))
    return pl.pallas_call(
        body, name="disassemble_w_in", grid=(D // tr,),
        in_specs=[_rows(tr, hi - lo) for lo, hi in PIECES] + [tab, tab, tab],
        out_specs=pl.BlockSpec((NDEV, tr, IN_PAD), lambda i: (0, i, 0)),
        out_shape=jax.ShapeDtypeStruct((NDEV, D, IN_PAD), BF16),
        compiler_params=_cparams(48, ("arbitrary",)),
    )(*pieces, dest, dest_f, scale)


def _in_proj(x, g1, wp, bfp, pq, pk, oq, ok, *, tm):
    s = x.shape[0]

    def body(x_ref, g_ref, w_ref, bf_ref, pq_ref, pk_ref, oq_ref, ok_ref,
             h_ref, qp_ref, kp_ref, v_ref, bcu_ref, z_ref, carry):
        @pl.when(pl.program_id(0) == 0)
        def _():
            carry[...] = jnp.zeros_like(carry)

        h = _rms_fwd(x_ref[...], g_ref[...])[0].astype(BF16)
        h_ref[...] = h
        z = jnp.dot(h, w_ref[:, OFF_F:WP], preferred_element_type=F32) + bf_ref[...]
        z_ref[...] = z
        lane = lax.broadcasted_iota(jnp.int32, (tm, 128), 1)
        logf = jnp.where(lane < H, jnp.minimum(z, 0.0) - jnp.log(1.0 + jnp.exp(-jnp.abs(z))), 0.0)
        row = lax.broadcasted_iota(jnp.int32, (tm, tm), 0)
        col = lax.broadcasted_iota(jnp.int32, (tm, tm), 1)
        tri = (col <= row).astype(F32)
        c = jnp.dot(tri, logf, precision=HIGHEST, preferred_element_type=F32) + carry[0:1, :]
        carry[...] = jnp.broadcast_to(c[tm - 1:tm, :], carry.shape)
        c1 = c.astype(BF16).astype(F32)
        r1 = c - c1
        c2 = r1.astype(BF16).astype(F32)
        c3 = (r1 - c2).astype(BF16).astype(F32)
        zc = (c1 + pltpu.roll(c2, 8, axis=1) + pltpu.roll(c3, 16, axis=1)).astype(BF16)
        q = jnp.dot(h, w_ref[:, OFF_Q:OFF_K], preferred_element_type=F32)
        qp_ref[...] = (q + jnp.dot(zc, pq_ref[...], preferred_element_type=F32) + oq_ref[...]).astype(BF16)
        k = jnp.dot(h, w_ref[:, OFF_K:OFF_V], preferred_element_type=F32)
        kp_ref[...] = (k + jnp.dot(zc, pk_ref[...], preferred_element_type=F32) + ok_ref[...]).astype(BF16)
        v_ref[...] = jnp.dot(h, w_ref[:, OFF_V:OFF_BCU], preferred_element_type=F32).astype(BF16)
        bcu_ref[...] = jnp.dot(h, w_ref[:, OFF_BCU:OFF_F], preferred_element_type=F32)

    return pl.pallas_call(
        body, name="in_proj", grid=(s // tm,),
        in_specs=[_rows(tm, D), _full((1, D)), _resident((D, WP)), _full((1, 128)),
                  _full((128, 1024)), _full((128, 1024)), _full((1, 1024)), _full((1, 1024))],
        out_specs=[_rows(tm, D), _rows(tm, 1024), _rows(tm, 1024), _rows(tm, AW), _rows(tm, 3 * CW), _rows(tm, 128)],
        out_shape=[jax.ShapeDtypeStruct((s, D), BF16), jax.ShapeDtypeStruct((s, 1024), BF16),
                   jax.ShapeDtypeStruct((s, 1024), BF16), jax.ShapeDtypeStruct((s, AW), BF16),
                   jax.ShapeDtypeStruct((s, 3 * CW), F32), jax.ShapeDtypeStruct((s, 128), F32)],
        scratch_shapes=[pltpu.VMEM((SUBLANES, 128), F32)],
        compiler_params=_cparams(56, ("arbitrary",)),
    )(x, g1, wp, bfp, pq, pk, oq, ok)


def _attn_fwd(qp, kp, v, *, t):
    s = qp.shape[0]
    nq = s // t

    def body(q_ref, k_ref, v_ref, o_ref, lse_ref):
        qi = pl.program_id(1)
        row = lax.broadcasted_iota(jnp.int32, (t, t), 0)
        col = lax.broadcasted_iota(jnp.int32, (t, t), 1)
        lane = lax.broadcasted_iota(jnp.int32, (t, 128), 1)
        outs, lses = [], []
        for hh in range(2):
            q = q_ref[:, HP * hh:HP * (hh + 1)]

            def step(ki, carry, masked, hh=hh, q=q):
                m, l, acc = carry
                off = pl.multiple_of(ki * t, t)
                k = k_ref[pl.ds(off, t), HP * hh:HP * (hh + 1)]
                sc = lax.dot_general(q, k, NT, preferred_element_type=F32)
                if masked:
                    sc = jnp.where(col <= row, sc, -1e30)
                mn = jnp.maximum(m, jnp.max(sc, axis=-1, keepdims=True))
                p = jnp.exp(sc - mn)
                a = jnp.exp(m - mn)
                l = a * l + jnp.sum(p, axis=-1, keepdims=True)
                p_hi = p.astype(BF16)
                p_lo = (p - p_hi.astype(F32)).astype(BF16)
                vv = v_ref[pl.ds(off, t), :]
                acc = a * acc + (jnp.dot(p_hi, vv, preferred_element_type=F32)
                                 + jnp.dot(p_lo, vv, preferred_element_type=F32))
                return mn, l, acc

            init = (jnp.full((t, 1), -1e30, F32), jnp.zeros((t, 1), F32), jnp.zeros((t, 128), F32))
            carry = lax.fori_loop(0, qi, functools.partial(step, masked=False), init)
            m, l, acc = step(qi, carry, True)
            outs.append(acc / l)
            lses.append(jnp.broadcast_to(m + jnp.log(l), (t, 128)))
        o_ref[...] = jnp.where(lane < DH, outs[0], outs[1])
        lse_ref[...] = jnp.where(lane < DH, lses[0], lses[1])

    return pl.pallas_call(
        body, name="attn_fwd", grid=(H // 2, nq),
        in_specs=[pl.BlockSpec((t, 2 * HP), lambda p, i: (i, p)),
                  pl.BlockSpec((s, 2 * HP), lambda p, i: (0, p)),
                  pl.BlockSpec((s, 128), lambda p, i: (0, p))],
        out_specs=[pl.BlockSpec((t, 128), lambda p, i: (i, p)), pl.BlockSpec((t, 128), lambda p, i: (i, p))],
        out_shape=[jax.ShapeDtypeStruct((s, AW), F32), jax.ShapeDtypeStruct((s, AW), F32)],
        compiler_params=_cparams(48, ("arbitrary", "arbitrary")),
    )(qp, kp, v)


def _conv_taps(bcu_ref, halo_ref, first, tm):
    z = bcu_ref[:, CW:2 * CW] * bcu_ref[:, 2 * CW:3 * CW]
    zh = jnp.where(first, 0.0, halo_ref[:, CW:2 * CW] * halo_ref[:, 2 * CW:3 * CW])
    row = lax.broadcasted_iota(jnp.int32, (tm, CW), 0)
    z1 = jnp.where(row == 0, zh[7:8, :], pltpu.roll(z, 1, axis=0))
    z2 = jnp.where(row == 0, zh[6:7, :], jnp.where(row == 1, zh[7:8, :], pltpu.roll(z, 2, axis=0)))
    return z, z1, z2


def _halo_before(tm, width):
    return pl.BlockSpec((SUBLANES, width), lambda i: (jnp.maximum(i * (tm // SUBLANES) - 1, 0), 0))


def _mix_out(o, bcu, cw8, ga, gc, gsum, w_out, x, g_post, g_ffn_pre, *, tm):
    s = x.shape[0]

    def body(o_ref, bcu_ref, halo_ref, cw_ref, ga_ref, gc_ref, gs_ref, w_ref, x_ref, g_ref, gf_ref,
             merged_ref, y_ref, x2_ref, cv_ref, h2_ref):
        z, z1, z2 = _conv_taps(bcu_ref, halo_ref, pl.program_id(0) == 0, tm)
        cv = cw_ref[0:1, :] * z2 + cw_ref[1:2, :] * z1 + cw_ref[2:3, :] * z
        cv_ref[...] = cv
        conv = bcu_ref[:, 0:CW] * cv
        ov = o_ref[...]
        ra = lax.rsqrt(_split_dot(ov * ov, gs_ref[...]) * (1.0 / DH) + EPS)
        rc = lax.rsqrt(_split_dot(conv * conv, gs_ref[...]) * (1.0 / DH) + EPS)
        merged = jnp.concatenate([ov * ra * ga_ref[...], conv * rc * gc_ref[...]], axis=1).astype(BF16)
        merged_ref[...] = merged
        y = jnp.dot(merged, w_ref[...], preferred_element_type=F32)
        y_ref[...] = y
        x2 = x_ref[...] + _rms_fwd(y, g_ref[...])[0]
        x2_ref[...] = x2
        h2_ref[...] = _rms_fwd(x2, gf_ref[...])[0].astype(BF16)

    return pl.pallas_call(
        body, name="mix_out", grid=(s // tm,),
        in_specs=[_rows(tm, AW), _rows(tm, 3 * CW), _halo_before(tm, 3 * CW), _full((SUBLANES, CW)),
                  _full((1, AW)), _full((1, CW)), _full((CW, CW)), _resident((D, D)), _rows(tm, D), _full((1, D)),
                  _full((1, D))],
        out_specs=[_rows(tm, D), _rows(tm, D), _rows(tm, D), _rows(tm, CW), _rows(tm, D)],
        out_shape=[jax.ShapeDtypeStruct((s, D), BF16), jax.ShapeDtypeStruct((s, D), F32),
                   jax.ShapeDtypeStruct((s, D), F32), jax.ShapeDtypeStruct((s, CW), F32),
                   jax.ShapeDtypeStruct((s, D), BF16)],
        compiler_params=_cparams(48, ("arbitrary",)),
    )(o, bcu, bcu, cw8, ga, gc, gsum, w_out, x, g_post, g_ffn_pre)


def _ffn_up(h2, wgu, *, tm):
    s = h2.shape[0]

    def body(h_ref, w_ref, gate_ref, up_ref, a_ref):
        h = h_ref[...]
        gate = jnp.dot(h, w_ref[0, 0], preferred_element_type=F32)
        up = jnp.dot(h, w_ref[1, 0], preferred_element_type=F32)
        gate_ref[0] = gate.astype(BF16)
        up_ref[0] = up.astype(BF16)
        a_ref[0] = (gate * jax.nn.sigmoid(gate) * up).astype(BF16)

    blk = pl.BlockSpec((1, tm, FB), lambda j, i: (j, i, 0))
    return pl.pallas_call(
        body, name="ffn_up", grid=(4, s // tm),
        in_specs=[pl.BlockSpec((tm, D), lambda j, i: (i, 0)),
                  pl.BlockSpec((2, 1, D, FB), lambda j, i: (0, j, 0, 0))],
        out_specs=[blk, blk, blk],
        out_shape=[jax.ShapeDtypeStruct((4, s, FB), BF16)] * 3,
        compiler_params=_cparams(48, ("arbitrary", "arbitrary")),
    )(h2, wgu)


def _ffn_down_loss(a, wd, x2, target, g_post, *, tm):
    s = x2.shape[0]

    def body(a_ref, w_ref, x2_ref, t_ref, g_ref, dx3_ref, dff_ref, loss_ref, dg_ref):
        @pl.when(pl.program_id(0) == 0)
        def _():
            loss_ref[...] = jnp.zeros_like(loss_ref)
            dg_ref[...] = jnp.zeros_like(dg_ref)

        ff = jnp.dot(a_ref[0], w_ref[0], preferred_element_type=F32)
        for j in range(1, 4):
            ff = ff + jnp.dot(a_ref[j], w_ref[j], preferred_element_type=F32)
        out, n, r = _rms_fwd(ff, g_ref[...])
        e = x2_ref[...] + out - t_ref[...]
        loss_ref[...] += _fold8(e * e)
        dx3 = e * (1.0 / D)
        dx3_ref[...] = dx3
        dff, dg = _rms_bwd(dx3, n, r, g_ref[...])
        dff_ref[...] = dff.astype(BF16)
        dg_ref[...] += _fold8(dg)

    return pl.pallas_call(
        body, name="ffn_down_loss", grid=(s // tm,),
        in_specs=[pl.BlockSpec((4, tm, FB), lambda i: (0, i, 0)), _resident((4, FB, D)), _rows(tm, D), _rows(tm, D),
                  _full((1, D))],
        out_specs=[_rows(tm, D), _rows(tm, D), _full((SUBLANES, D)), _full((SUBLANES, D))],
        out_shape=[jax.ShapeDtypeStruct((s, D), F32), jax.ShapeDtypeStruct((s, D), BF16),
                   jax.ShapeDtypeStruct((SUBLANES, D), F32), jax.ShapeDtypeStruct((SUBLANES, D), F32)],
        compiler_params=_cparams(48, ("arbitrary",)),
    )(a, wd, x2, target, g_post)


def _ffn_bwd_act(dff, wd, gate, up, *, tm):
    s = dff.shape[0]

    def body(dff_ref, w_ref, gate_ref, up_ref, dgu_ref):
        da = lax.dot_general(dff_ref[...], w_ref[0], NT, preferred_element_type=F32)
        g = gate_ref[0].astype(F32)
        sg = jax.nn.sigmoid(g)
        dgu_ref[0, 0] = (da * up_ref[0].astype(F32) * (sg * (1.0 + g * (1.0 - sg)))).astype(BF16)
        dgu_ref[1, 0] = (da * (g * sg)).astype(BF16)

    blk = pl.BlockSpec((1, tm, FB), lambda j, i: (j, i, 0))
    return pl.pallas_call(
        body, name="ffn_bwd_act", grid=(4, s // tm),
        in_specs=[pl.BlockSpec((tm, D), lambda j, i: (i, 0)), pl.BlockSpec((1, FB, D), lambda j, i: (j, 0, 0)), blk, blk],
        out_specs=pl.BlockSpec((2, 1, tm, FB), lambda j, i: (0, j, i, 0)),
        out_shape=jax.ShapeDtypeStruct((2, 4, s, FB), BF16),
        compiler_params=_cparams(48, ("arbitrary", "arbitrary")),
    )(dff, wd, gate, up)


def _grad_matmul(a, b, *, ta, tb, ts, name):
    s, ka = a.shape
    nb = b.shape[1]
    nk = s // ts

    def body(a_ref, b_ref, o_ref, acc):
        k = pl.program_id(2)

        @pl.when(k == 0)
        def _():
            acc[...] = jnp.zeros_like(acc)

        acc[...] += lax.dot_general(a_ref[...], b_ref[...], TN, preferred_element_type=F32)

        @pl.when(k == nk - 1)
        def _():
            o_ref[...] = acc[...].astype(BF16)

    return pl.pallas_call(
        body, name=name, grid=(ka // ta, nb // tb, nk),
        in_specs=[pl.BlockSpec((ts, ta), lambda i, j, k: (k, i)), pl.BlockSpec((ts, tb), lambda i, j, k: (k, j))],
        out_specs=pl.BlockSpec((ta, tb), lambda i, j, k: (i, j)),
        out_shape=jax.ShapeDtypeStruct((ka, nb), BF16),
        scratch_shapes=[pltpu.VMEM((ta, tb), F32)],
        compiler_params=_cparams(48, ("arbitrary", "arbitrary", "arbitrary")),
    )(a, b)


def _grad_matmul_blocks(a, b, *, ts, name):
    nblk = a.shape[0] if a.ndim == 3 else b.shape[0]
    s = a.shape[-2]
    ka, nb = a.shape[-1], b.shape[-1]
    nk = s // ts

    def body(a_ref, b_ref, o_ref, acc):
        k = pl.program_id(1)

        @pl.when(k == 0)
        def _():
            acc[...] = jnp.zeros_like(acc)

        av = a_ref[0] if a.ndim == 3 else a_ref[...]
        bv = b_ref[0] if b.ndim == 3 else b_ref[...]
        acc[...] += lax.dot_general(av, bv, TN, preferred_element_type=F32)

        @pl.when(k == nk - 1)
        def _():
            o_ref[0] = acc[...].astype(BF16)

    def spec(arr, width):
        if arr.ndim == 3:
            return pl.BlockSpec((1, ts, width), lambda j, k: (j, k, 0))
        return pl.BlockSpec((ts, width), lambda j, k: (k, 0))

    return pl.pallas_call(
        body, name=name, grid=(nblk, nk),
        in_specs=[spec(a, ka), spec(b, nb)],
        out_specs=pl.BlockSpec((1, ka, nb), lambda j, k: (j, 0, 0)),
        out_shape=jax.ShapeDtypeStruct((nblk, ka, nb), BF16),
        scratch_shapes=[pltpu.VMEM((ka, nb), F32)],
        compiler_params=_cparams(48, ("arbitrary", "arbitrary")),
    )(a, b)


def _ffn_bwd_in(dgu, wgu, x2, g_pre, dx3, y, g_post, *, tm):
    s = x2.shape[0]

    def body(dgu_ref, w_ref, x2_ref, gpre_ref, dx3_ref, y_ref, gpost_ref,
             dx2_ref, dy_ref, dgpre_ref, dgpost_ref):
        @pl.when(pl.program_id(0) == 0)
        def _():
            dgpre_ref[...] = jnp.zeros_like(dgpre_ref)
            dgpost_ref[...] = jnp.zeros_like(dgpost_ref)

        dh2 = None
        for a in range(2):
            for j in range(4):
                part = lax.dot_general(dgu_ref[a, j], w_ref[a, j], NT, preferred_element_type=F32)
                dh2 = part if dh2 is None else dh2 + part
        _, n2, r2 = _rms_fwd(x2_ref[...], gpre_ref[...])
        dxn, dg = _rms_bwd(dh2, n2, r2, gpre_ref[...])
        dgpre_ref[...] += _fold8(dg)
        dx2 = dx3_ref[...] + dxn
        dx2_ref[...] = dx2
        _, ny, ry = _rms_fwd(y_ref[...], gpost_ref[...])
        dy, dg2 = _rms_bwd(dx2, ny, ry, gpost_ref[...])
        dy_ref[...] = dy.astype(BF16)
        dgpost_ref[...] += _fold8(dg2)

    return pl.pallas_call(
        body, name="ffn_bwd_in", grid=(s // tm,),
        in_specs=[pl.BlockSpec((2, 4, tm, FB), lambda i: (0, 0, i, 0)), _resident((2, 4, D, FB)), _rows(tm, D),
                  _full((1, D)), _rows(tm, D), _rows(tm, D), _full((1, D))],
        out_specs=[_rows(tm, D), _rows(tm, D), _full((SUBLANES, D)), _full((SUBLANES, D))],
        out_shape=[jax.ShapeDtypeStruct((s, D), F32), jax.ShapeDtypeStruct((s, D), BF16),
                   jax.ShapeDtypeStruct((SUBLANES, D), F32), jax.ShapeDtypeStruct((SUBLANES, D), F32)],
        compiler_params=_cparams(56, ("arbitrary",)),
    )(dgu, wgu, x2, g_pre, dx3, y, g_post)


def _mix_bwd(dy, w_out, o, cv, bcu, ga, gc, gsum, *, tm):
    s = dy.shape[0]

    def group_norm_bwd(dn_out, v, g, gs):
        r = lax.rsqrt(_split_dot(v * v, gs) * (1.0 / DH) + EPS)
        n = v * r
        dn = dn_out * g
        return r * (dn - n * (_split_dot(dn * n, gs) * (1.0 / DH))), dn_out * n

    def body(dy_ref, w_ref, o_ref, cv_ref, bcu_ref, ga_ref, gc_ref, gs_ref,
             do_ref, dl_ref, dcv_ref, db_ref, dga_ref, dgc_ref):
        @pl.when(pl.program_id(0) == 0)
        def _():
            dga_ref[...] = jnp.zeros_like(dga_ref)
            dgc_ref[...] = jnp.zeros_like(dgc_ref)

        dm = lax.dot_general(dy_ref[...], w_ref[...], NT, preferred_element_type=F32)
        ov = o_ref[...]
        do, dga = group_norm_bwd(dm[:, 0:AW], ov, ga_ref[...], gs_ref[...])
        dob = do.astype(BF16)
        do_ref[...] = dob
        dl_ref[...] = _split_dot(dob.astype(F32) * ov, gs_ref[...])
        dga_ref[...] += _fold8(dga)
        gate_b = bcu_ref[:, 0:CW]
        cv = cv_ref[...]
        dconv, dgc = group_norm_bwd(dm[:, AW:D], gate_b * cv, gc_ref[...], gs_ref[...])
        dgc_ref[...] += _fold8(dgc)
        dcv_ref[...] = dconv * gate_b
        db_ref[...] = (dconv * cv).astype(BF16)

    return pl.pallas_call(
        body, name="mix_bwd", grid=(s // tm,),
        in_specs=[_rows(tm, D), _resident((D, D)), _rows(tm, AW), _rows(tm, CW), _rows(tm, 3 * CW),
                  _full((1, AW)), _full((1, CW)), _full((CW, CW))],
        out_specs=[_rows(tm, AW), _rows(tm, AW), _rows(tm, CW), _rows(tm, CW),
                   _full((SUBLANES, AW)), _full((SUBLANES, CW))],
        out_shape=[jax.ShapeDtypeStruct((s, AW), BF16), jax.ShapeDtypeStruct((s, AW), F32),
                   jax.ShapeDtypeStruct((s, CW), F32), jax.ShapeDtypeStruct((s, CW), BF16),
                   jax.ShapeDtypeStruct((SUBLANES, AW), F32), jax.ShapeDtypeStruct((SUBLANES, CW), F32)],
        compiler_params=_cparams(48, ("arbitrary",)),
    )(dy, w_out, o, cv, bcu, ga, gc, gsum)


def _conv_bwd(dcv, db, bcu, cw8, *, tm):
    s = dcv.shape[0]
    nt = s // tm

    def body(dcv_ref, nxt_ref, db_ref, bcu_ref, halo_ref, cw_ref, dbcu_ref, dw_ref):
        i = pl.program_id(0)

        @pl.when(i == 0)
        def _():
            dw_ref[...] = jnp.zeros_like(dw_ref)

        z, z1, z2 = _conv_taps(bcu_ref, halo_ref, i == 0, tm)
        d = dcv_ref[...]
        dw_ref[0] += _fold8(d * z2)
        dw_ref[1] += _fold8(d * z1)
        dw_ref[2] += _fold8(d * z)
        nx = jnp.where(i == nt - 1, 0.0, nxt_ref[...])
        row = lax.broadcasted_iota(jnp.int32, (tm, CW), 0)
        d1 = jnp.where(row == tm - 1, nx[0:1, :], pltpu.roll(d, tm - 1, axis=0))
        d2 = jnp.where(row == tm - 2, nx[0:1, :], jnp.where(row == tm - 1, nx[1:2, :], pltpu.roll(d, tm - 2, axis=0)))
        dz = cw_ref[2:3, :] * d + cw_ref[1:2, :] * d1 + cw_ref[0:1, :] * d2
        dbcu_ref[:, 0:CW] = db_ref[...]
        dbcu_ref[:, CW:2 * CW] = (dz * bcu_ref[:, 2 * CW:3 * CW]).astype(BF16)
        dbcu_ref[:, 2 * CW:3 * CW] = (dz * bcu_ref[:, CW:2 * CW]).astype(BF16)

    return pl.pallas_call(
        body, name="conv_bwd", grid=(nt,),
        in_specs=[_rows(tm, CW),
                  pl.BlockSpec((SUBLANES, CW), lambda i: (jnp.minimum((i + 1) * (tm // SUBLANES), s // SUBLANES - 1), 0)),
                  _rows(tm, CW), _rows(tm, 3 * CW), _halo_before(tm, 3 * CW), _full((SUBLANES, CW))],
        out_specs=[_rows(tm, 3 * CW), _full((3, SUBLANES, CW))],
        out_shape=[jax.ShapeDtypeStruct((s, 3 * CW), BF16), jax.ShapeDtypeStruct((3, SUBLANES, CW), F32)],
        compiler_params=_cparams(48, ("arbitrary",)),
    )(dcv, dcv, db, bcu, bcu, cw8)


def _attn_bwd(qp, kp, v, do, lse, dl, *, t):
    s = qp.shape[0]
    nq = s // t

    def body(q_ref, k_ref, v_ref, do_ref, lse_ref, dl_ref, dq_ref, dk_ref, dv_ref, dkx_ref, dq_acc):
        ki = pl.program_id(1)

        @pl.when(ki == 0)
        def _():
            dq_acc[...] = jnp.zeros_like(dq_acc)

        row = lax.broadcasted_iota(jnp.int32, (t, t), 0)
        col = lax.broadcasted_iota(jnp.int32, (t, t), 1)
        lane = lax.broadcasted_iota(jnp.int32, (t, 128), 1)
        v2 = v_ref[...]
        dv = jnp.zeros((t, 128), F32)
        dks = []
        for hh in range(2):
            kh = k_ref[:, HP * hh:HP * (hh + 1)]
            in_head = (lane >= DH * hh) & (lane < DH * (hh + 1))

            def step(qi, carry, masked, hh=hh, kh=kh, in_head=in_head):
                dk, dv = carry
                off = pl.multiple_of(qi * t, t)
                q = q_ref[pl.ds(off, t), HP * hh:HP * (hh + 1)]
                dom = jnp.where(in_head, do_ref[pl.ds(off, t), :], jnp.zeros((), BF16))
                sc = lax.dot_general(q, kh, NT, preferred_element_type=F32)
                sc = sc - lse_ref[pl.ds(off, t), DH * hh:DH * hh + 1]
                if masked:
                    sc = jnp.where(col <= row, sc, -1e30)
                p = jnp.exp(sc)
                dp = lax.dot_general(dom, v2, NT, preferred_element_type=F32)
                ds32 = p * (dp - dl_ref[pl.ds(off, t), DH * hh:DH * hh + 1])
                ds = ds32.astype(BF16)
                ds_lo = (ds32 - ds.astype(F32)).astype(BF16)
                dv = dv + lax.dot_general(p.astype(BF16), dom, TN, preferred_element_type=F32)
                dk = dk + (lax.dot_general(ds, q, TN, preferred_element_type=F32)
                           + lax.dot_general(ds_lo, q, TN, preferred_element_type=F32))
                dq_acc[pl.ds(off, t), HP * hh:HP * (hh + 1)] += jnp.dot(ds, kh, preferred_element_type=F32)
                return dk, dv

            carry = step(ki, (jnp.zeros((t, HP), F32), dv), True)
            dk, dv = lax.fori_loop(ki + 1, nq, functools.partial(step, masked=False), carry)
            dk_ref[:, HP * hh:HP * (hh + 1)] = dk.astype(BF16)
            dks.append(dk)
        dv_ref[...] = dv.astype(BF16)
        dkx_ref[...] = jnp.where(lane < DH, pltpu.roll(dks[0], DH, axis=1), dks[1])

        @pl.when(ki == nq - 1)
        def _():
            dq_ref[...] = dq_acc[...].astype(BF16)

    return pl.pallas_call(
        body, name="attn_bwd", grid=(H // 2, nq),
        in_specs=[pl.BlockSpec((s, 2 * HP), lambda p, i: (0, p)),
                  pl.BlockSpec((t, 2 * HP), lambda p, i: (i, p)),
                  pl.BlockSpec((t, 128), lambda p, i: (i, p)),
                  pl.BlockSpec((s, 128), lambda p, i: (0, p)),
                  pl.BlockSpec((s, 128), lambda p, i: (0, p)),
                  pl.BlockSpec((s, 128), lambda p, i: (0, p))],
        out_specs=[pl.BlockSpec((s, 2 * HP), lambda p, i: (0, p)),
                   pl.BlockSpec((t, 2 * HP), lambda p, i: (i, p)),
                   pl.BlockSpec((t, 128), lambda p, i: (i, p)),
                   pl.BlockSpec((t, 128), lambda p, i: (i, p))],
        out_shape=[jax.ShapeDtypeStruct((s, 1024), BF16), jax.ShapeDtypeStruct((s, 1024), BF16),
                   jax.ShapeDtypeStruct((s, AW), BF16), jax.ShapeDtypeStruct((s, AW), F32)],
        scratch_shapes=[pltpu.VMEM((s, 2 * HP), F32)],
        compiler_params=_cparams(56, ("arbitrary", "arbitrary")),
    )(qp, kp, v, do, lse, dl)


def _forget_bwd(dkx, z, sel, *, tm):
    s = dkx.shape[0]
    nt = s // tm

    def body(dk_ref, z_ref, sel_ref, dfl_ref, dbf_ref, carry):
        @pl.when(pl.program_id(0) == 0)
        def _():
            carry[...] = jnp.zeros_like(carry)
            dbf_ref[...] = jnp.zeros_like(dbf_ref)

        dc = _split_dot(dk_ref[...], sel_ref[...])
        row = lax.broadcasted_iota(jnp.int32, (tm, tm), 0)
        col = lax.broadcasted_iota(jnp.int32, (tm, tm), 1)
        tri = (col >= row).astype(F32)
        dlogf = jnp.dot(tri, dc, precision=HIGHEST, preferred_element_type=F32) + carry[0:1, :]
        carry[...] = jnp.broadcast_to(dlogf[0:1, :], carry.shape)
        dz = dlogf * (1.0 - jax.nn.sigmoid(z_ref[...]))
        dfl_ref[...] = dz.astype(BF16)
        dbf_ref[...] += _fold8(dz)

    rev = lambda i: (nt - 1 - i, 0)
    return pl.pallas_call(
        body, name="forget_bwd", grid=(nt,),
        in_specs=[pl.BlockSpec((tm, AW), rev), pl.BlockSpec((tm, 128), rev), _full((AW, 128))],
        out_specs=[pl.BlockSpec((tm, 128), rev), _full((SUBLANES, 128))],
        out_shape=[jax.ShapeDtypeStruct((s, 128), BF16), jax.ShapeDtypeStruct((SUBLANES, 128), F32)],
        scratch_shapes=[pltpu.VMEM((SUBLANES, 128), F32)],
        compiler_params=_cparams(48, ("arbitrary",)),
    )(dkx, z, sel)


def _in_proj_bwd(pieces, wp, x, g1, dx2, *, tm):
    s = x.shape[0]

    def body(q_ref, k_ref, v_ref, bcu_ref, f_ref, w_ref, x_ref, g_ref, dx2_ref, dx_ref, dg_ref):
        @pl.when(pl.program_id(0) == 0)
        def _():
            dg_ref[...] = jnp.zeros_like(dg_ref)

        dh = None
        for ref, (lo, hi) in zip((q_ref, k_ref, v_ref, bcu_ref, f_ref), PIECES):
            part = lax.dot_general(ref[...], w_ref[:, lo:hi], NT, preferred_element_type=F32)
            dh = part if dh is None else dh + part
        _, n, r = _rms_fwd(x_ref[...], g_ref[...])
        dxn, dg = _rms_bwd(dh, n, r, g_ref[...])
        dx_ref[...] = dx2_ref[...] + dxn
        dg_ref[...] += _fold8(dg)

    return pl.pallas_call(
        body, name="in_proj_bwd", grid=(s // tm,),
        in_specs=[_rows(tm, hi - lo) for lo, hi in PIECES] + [_resident((D, WP)), _rows(tm, D), _full((1, D)), _rows(tm, D)],
        out_specs=[_rows(tm, D), _full((SUBLANES, D))],
        out_shape=[jax.ShapeDtypeStruct((s, D), F32), jax.ShapeDtypeStruct((SUBLANES, D), F32)],
        compiler_params=_cparams(56, ("arbitrary",)),
    )(*pieces, wp, x, g1, dx2)


def _position():
    return lax.axis_index("x"), lax.axis_index("y"), lax.axis_index("c")


ANY = pl.BlockSpec(memory_space=pl.ANY)


def _all_gather(shards):
    n = len(shards)

    def body(*refs):
        x_refs, out_refs = refs[:n], refs[n:2 * n]
        send_sems, recv_sems, local_sems = refs[2 * n:]
        x, y, c = _position()
        me, sibling = (x, y, c), (x, y, 1 - c)
        chips = [(1 - x, y), (x, 1 - y), (1 - x, 1 - y)]

        def copy(a, k, block, to, own=False):
            slot = out_refs[a].at[4 * block[0] + 2 * block[1] + block[2]]
            return pltpu.make_async_remote_copy(
                src_ref=x_refs[a] if own else slot, dst_ref=slot,
                send_sem=send_sems.at[7 * a + k], recv_sem=recv_sems.at[7 * a + k], device_id=to, device_id_type=MESH_ID)

        mine = [pltpu.make_async_copy(x_refs[a], out_refs[a].at[4 * x + 2 * y + c], local_sems.at[a]) for a in range(n)]
        for cp in mine:
            cp.start()
        first = []
        for a in range(n):
            first.append(copy(a, 0, me, sibling, own=True))
            first += [copy(a, 1 + j, me, (*chip, c), own=True) for j, chip in enumerate(chips)]
        for cp in first:
            cp.start()
        passed = []
        for j, chip in enumerate(chips):
            for a in range(n):
                copy(a, 1 + j, (*chip, c), me).wait_recv()
                fwd = copy(a, 4 + j, (*chip, c), sibling)
                fwd.start()
                passed.append(fwd)
        for a in range(n):
            copy(a, 0, sibling, me).wait_recv()
            for j, chip in enumerate(chips):
                copy(a, 4 + j, (*chip, 1 - c), me).wait_recv()
        for cp in first + passed:
            cp.wait_send()
        for cp in mine:
            cp.wait()

    return pl.pallas_call(
        body, name="all_gather_weights",
        out_shape=[jax.ShapeDtypeStruct((NDEV,) + sh.shape, sh.dtype) for sh in shards],
        in_specs=[ANY] * n, out_specs=[ANY] * n,
        scratch_shapes=[pltpu.SemaphoreType.DMA((7 * n,)), pltpu.SemaphoreType.DMA((7 * n,)), pltpu.SemaphoreType.DMA((n,))],
    )(*shards)


def _pair_exchange(grads):
    n = len(grads)

    def body(*refs):
        g_refs, out_refs = refs[:n], refs[n:2 * n]
        send_sems, recv_sems = refs[2 * n:]
        x, y, c = _position()
        copies = [pltpu.make_async_remote_copy(
            src_ref=g_refs[a].at[:, pl.ds(1 - c, 1)], dst_ref=out_refs[a], send_sem=send_sems.at[a],
            recv_sem=recv_sems.at[a], device_id=(x, y, 1 - c), device_id_type=MESH_ID) for a in range(n)]
        for cp in copies:
            cp.start()
        for cp in copies:
            cp.wait()

    return pl.pallas_call(
        body, name="grad_pair_exchange",
        out_shape=[jax.ShapeDtypeStruct((4, 1) + g.shape[2:], g.dtype) for g in grads],
        in_specs=[ANY] * n, out_specs=[ANY] * n,
        scratch_shapes=[pltpu.SemaphoreType.DMA((n,)), pltpu.SemaphoreType.DMA((n,))],
    )(*grads)


def _pair_sum(g, got, idx, *, tr, name):
    r, c = g.shape[2:]

    def body(idx_ref, g_ref, got_ref, pb_ref, own_ref):
        p = g_ref[0, 0].astype(F32) + got_ref[0, 0].astype(F32)
        pb_ref[0] = p.astype(BF16)

        @pl.when(pl.program_id(1) == idx_ref[1])
        def _():
            own_ref[...] = p

    return pl.pallas_call(
        body, name=name,
        grid_spec=pltpu.PrefetchScalarGridSpec(
            num_scalar_prefetch=1, grid=(r // tr, 4),
            in_specs=[pl.BlockSpec((1, 1, tr, c), lambda i, j, idx: (j, idx[0], i, 0)),
                      pl.BlockSpec((1, 1, tr, c), lambda i, j, idx: (j, 0, i, 0))],
            out_specs=[pl.BlockSpec((1, tr, c), lambda i, j, idx: (j, i, 0)),
                       pl.BlockSpec((tr, c), lambda i, j, idx: (i, 0))]),
        out_shape=[jax.ShapeDtypeStruct((4, r, c), BF16), jax.ShapeDtypeStruct((r, c), F32)],
        compiler_params=_cparams(32, ("arbitrary", "arbitrary")),
    )(idx, g, got)


def _chip_exchange(sums):
    n = len(sums)

    def body(*refs):
        p_refs, out_refs = refs[:n], refs[n:2 * n]
        send_sems, recv_sems, local_sems = refs[2 * n:]
        x, y, c = _position()
        my_chip = 2 * x + y
        mine = [pltpu.make_async_copy(p_refs[a].at[my_chip], out_refs[a].at[my_chip], local_sems.at[a]) for a in range(n)]
        for cp in mine:
            cp.start()
        chips = [(1 - x, y), (x, 1 - y), (1 - x, 1 - y)]

        def copy(a, j):
            px, py = chips[j]
            return pltpu.make_async_remote_copy(
                src_ref=p_refs[a].at[2 * px + py], dst_ref=out_refs[a].at[my_chip],
                send_sem=send_sems.at[3 * a + j], recv_sem=recv_sems.at[3 * a + j], device_id=(px, py, c),
                device_id_type=MESH_ID)

        def arrival(a, j):
            px, py = chips[j]
            return pltpu.make_async_remote_copy(
                src_ref=p_refs[a].at[my_chip], dst_ref=out_refs[a].at[2 * px + py],
                send_sem=send_sems.at[3 * a + j], recv_sem=recv_sems.at[3 * a + j], device_id=(px, py, c),
                device_id_type=MESH_ID)

        copies = [copy(a, j) for a in range(n) for j in range(3)]
        for cp in copies:
            cp.start()
        for a in range(n):
            for j in range(3):
                arrival(a, j).wait_recv()
        for cp in copies:
            cp.wait_send()
        for cp in mine:
            cp.wait()

    return pl.pallas_call(
        body, name="grad_chip_exchange",
        out_shape=[jax.ShapeDtypeStruct(p.shape, p.dtype) for p in sums],
        in_specs=[ANY] * n, out_specs=[ANY] * n,
        scratch_shapes=[pltpu.SemaphoreType.DMA((3 * n,)), pltpu.SemaphoreType.DMA((3 * n,)), pltpu.SemaphoreType.DMA((n,))],
    )(*sums)


def _small_all_reduce(parts):
    def body(gmp_ref, gmo_ref, gfp_ref, gfo_ref, ga_ref, gc_ref, dw_ref, bf_ref, loss_ref,
             out_ref, buf, send_sems, recv_sems):
        x, y, c = _position()
        me = 4 * x + 2 * y + c

        def colsum(v):
            return jnp.sum(v, axis=0, keepdims=True)

        loss = jnp.sum(colsum(loss_ref[...]), axis=1, keepdims=True) * (0.5 / D)
        rows = [colsum(gmp_ref[...]), colsum(gmo_ref[...]), colsum(gfp_ref[...]), colsum(gfo_ref[...]),
                jnp.concatenate([colsum(ga_ref[...]), colsum(gc_ref[...])], axis=1),
                jnp.concatenate([colsum(dw_ref[0]), colsum(dw_ref[1])], axis=1),
                jnp.concatenate([colsum(dw_ref[2]), colsum(bf_ref[...]), jnp.broadcast_to(loss, (1, 128)),
                                 jnp.zeros((1, 256), F32)], axis=1),
                jnp.zeros((1, D), F32)]
        buf[me] = jnp.concatenate(rows, axis=0)
        copies = []
        for mm in range(1, NDEV):
            peer = (x ^ (mm >> 2), y ^ ((mm >> 1) & 1), c ^ (mm & 1))
            copies.append(pltpu.make_async_remote_copy(
                src_ref=buf.at[me], dst_ref=buf.at[me], send_sem=send_sems.at[mm - 1], recv_sem=recv_sems.at[mm - 1],
                device_id=peer, device_id_type=MESH_ID))
        for cp in copies:
            cp.start()
        for cp in copies:
            cp.wait_recv()
        for cp in copies:
            cp.wait_send()
        acc = buf[0]
        for d in range(1, NDEV):
            acc = acc + buf[d]
        out_ref[...] = acc

    vm = pl.BlockSpec(memory_space=pltpu.VMEM)
    return pl.pallas_call(
        body, name="small_all_reduce",
        out_shape=jax.ShapeDtypeStruct((SUBLANES, D), F32),
        in_specs=[vm] * len(parts), out_specs=vm,
        scratch_shapes=[pltpu.VMEM((NDEV, SUBLANES, D), F32), pltpu.SemaphoreType.DMA((7,)), pltpu.SemaphoreType.DMA((7,))],
    )(*parts)


def _adam_update(w, g, m, v):
    nm = ADAM_B1 * m + (1.0 - ADAM_B1) * g
    nv = ADAM_B2 * v + (1.0 - ADAM_B2) * (g * g)
    m_hat = nm / (1.0 - ADAM_B1 ** ADAM_STEP)
    v_hat = nv / (1.0 - ADAM_B2 ** ADAM_STEP)
    return -ADAM_LR * (m_hat / (jnp.sqrt(v_hat) + ADAM_EPS) + ADAM_WD * w), nm, nv


def _adamw(w, g, m, v, *, tr, name):
    rows, cols = w.shape

    def body(w_ref, g_ref, m_ref, v_ref, d_ref, nm_ref, nv_ref):
        d_ref[...], nm_ref[...], nv_ref[...] = _adam_update(w_ref[...], g_ref[...], m_ref[...], v_ref[...])

    spec = pl.BlockSpec((tr, cols), lambda i: (i, 0))
    return pl.pallas_call(
        body, name=name, grid=(rows // tr,),
        in_specs=[spec] * 4, out_specs=[spec] * 3,
        out_shape=[jax.ShapeDtypeStruct((rows, cols), F32)] * 3,
        compiler_params=_cparams(32, ("arbitrary",)),
    )(w, g, m, v)


def _chip_sum_adamw(got, own, idx, w, m, v, *, tr, name):
    rows, cols = w.shape

    def body(idx_ref, got_ref, own_ref, w_ref, m_ref, v_ref, g_ref, d_ref, nm_ref, nv_ref):
        g = jnp.zeros((tr, cols), F32)
        for j in range(4):
            g = g + jnp.where(idx_ref[1] == j, own_ref[...], got_ref[j].astype(F32))
        g_ref[...] = g
        d_ref[...], nm_ref[...], nv_ref[...] = _adam_update(w_ref[...], g, m_ref[...], v_ref[...])

    spec = pl.BlockSpec((tr, cols), lambda i, idx: (i, 0))
    return pl.pallas_call(
        body, name=name,
        grid_spec=pltpu.PrefetchScalarGridSpec(
            num_scalar_prefetch=1, grid=(rows // tr,),
            in_specs=[pl.BlockSpec((4, tr, cols), lambda i, idx: (0, i, 0)), spec, spec, spec, spec],
            out_specs=[spec] * 4),
        out_shape=[jax.ShapeDtypeStruct((rows, cols), F32)] * 4,
        compiler_params=_cparams(32, ("arbitrary",)),
    )(idx, got, own, w, m, v)


def _placement_constants():
    j = jnp.arange(128)[:, None]
    lane = jnp.arange(1024)[None, :]
    head, sub = lane // HP, lane % HP
    piece, jh = j // H, j % H
    valid = (j < 3 * H) & (jh == head)
    pq = jnp.where(valid & (sub == DH + piece), 1.0, 0.0).astype(BF16)
    pk = jnp.where(valid & (sub == DH + 3 + piece), -1.0, 0.0).astype(BF16)
    oq = jnp.where((sub >= DH + 3) & (sub < DH + 6), 1.0, 0.0).astype(F32)
    ok = jnp.where((sub >= DH) & (sub < DH + 3), 1.0, 0.0).astype(F32)
    r = jnp.arange(AW)[:, None]
    cc = jnp.arange(128)[None, :]
    sel = jnp.where((r % DH == 3) & (r // DH == cc), -1.0, 0.0).astype(BF16)
    gi = jnp.arange(CW)
    gsum = (gi[:, None] // DH == gi[None, :] // DH).astype(BF16)
    return pq, pk, oq, ok, sel, gsum


def _local_step(xs, tgt, wp, w_out_f, wgu, wd, cw8, bfp, g_attn_out, g_conv_out,
                g_mix_pre, g_mix_post, g_ffn_pre, g_ffn_post):
    pq, pk, oq, ok, sel, gsum = _placement_constants()
    h1, qp, kp, vv, bcu, zf = _in_proj(xs, g_mix_pre, wp, bfp, pq, pk, oq, ok, tm=512)
    o, lse = _attn_fwd(qp, kp, vv, t=512)
    merged, y, x2, cv, h2 = _mix_out(o, bcu, cw8, g_attn_out, g_conv_out, gsum, w_out_f, xs, g_mix_post, g_ffn_pre, tm=512)
    gate, up, act = _ffn_up(h2, wgu, tm=512)
    dx3, dff, loss_p, dg_ffn_post = _ffn_down_loss(act, wd, x2, tgt, g_ffn_post, tm=512)

    dgu = _ffn_bwd_act(dff, wd, gate, up, tm=512)
    dw_down = _grad_matmul_blocks(act, dff, ts=512, name="grad_w_down")
    dw_gu = _grad_matmul_blocks(h2, dgu.reshape(NDEV, -1, FB), ts=512, name="grad_w_gate_up")
    dx2, dy, dg_ffn_pre, dg_mix_post = _ffn_bwd_in(dgu, wgu, x2, g_ffn_pre, dx3, y, g_mix_post, tm=256)
    dw_out = _grad_matmul(merged, dy, ta=1024, tb=1024, ts=512, name="grad_w_out")
    do, dl, dcv, db, dg_attn, dg_conv = _mix_bwd(dy, w_out_f, o, cv, bcu, g_attn_out, g_conv_out, gsum, tm=512)
    dbcu, dtaps = _conv_bwd(dcv, db, bcu, cw8, tm=512)
    dqp, dkp, dv, dkx = _attn_bwd(qp, kp, vv, do, lse, dl, t=512)
    dfl, dbf = _forget_bwd(dkx, zf, sel, tm=512)
    pieces = (dqp, dkp, dv, dbcu, dfl)
    names = ("grad_w_in_q", "grad_w_in_k", "grad_w_in_v", "grad_w_in_bcu", "grad_w_in_f")
    tbs = (1024, 1024, 512, 768, 128)
    dwp = tuple(_grad_matmul(h1, p, ta=1024, tb=tb, ts=512, name=nm) for p, nm, tb in zip(pieces, names, tbs))
    grad_x, dg_mix_pre = _in_proj_bwd(pieces, wp, xs, g_mix_pre, dx2, tm=512)
    return (grad_x, dwp, dw_out, dw_gu, dw_down, dg_mix_pre, dg_mix_post, dg_ffn_pre, dg_ffn_post, dg_attn, dg_conv,
            dtaps, dbf, loss_p)


def kernel(x, w_in, b_forget, conv_w, g_attn_out, g_conv_out, w_out, g_mix_pre, g_mix_post, w_gate_up, w_down, g_ffn_pre, g_ffn_post, loss_target, m_w_in, m_b_forget, m_conv_w, m_g_attn_out, m_g_conv_out, m_w_out, m_g_mix_pre, m_g_mix_post, m_w_gate_up, m_w_down, m_g_ffn_pre, m_g_ffn_post, v_w_in, v_b_forget, v_conv_w, v_g_attn_out, v_g_conv_out, v_w_out, v_g_mix_pre, v_g_mix_post, v_w_gate_up, v_w_down, v_g_ffn_pre, v_g_ffn_post):
    xc, yc, cc = _position()
    my_chip = 2 * xc + yc
    me = 2 * my_chip + cc
    idx = jnp.stack([cc, my_chip]).astype(jnp.int32)
    tables = _in_layout_tables()
    pad_in = lambda a: jnp.pad(a, ((0, 0), (0, IN_PAD - IN_COLS)))

    g_in, g_out, g_gu, g_down, g_taps = _all_gather([
        pad_in(w_in[0]).astype(BF16), w_out[0].astype(BF16), w_gate_up[0].astype(BF16), w_down[0].astype(BF16), conv_w[0]])
    wp = _assemble_w_in(g_in, tables, tr=256)
    w_out_f = g_out.reshape(D, D)
    wgu = g_gu.reshape(2, 4, D, FB)
    wd = g_down.reshape(4, FB, D)
    cw8 = jnp.pad(g_taps.transpose(1, 0, 2).reshape(3, CW), ((0, SUBLANES - 3), (0, 0)))
    bfp = jnp.pad(b_forget, ((0, 0), (0, 128 - H)))

    (grad_x, dwp, dw_out, dw_gu, dw_down, dg_mix_pre, dg_mix_post, dg_ffn_pre, dg_ffn_post, dg_attn, dg_conv,
     dtaps, dbf, loss_p) = _local_step(x[0], loss_target[0], wp, w_out_f, wgu, wd, cw8, bfp, g_attn_out, g_conv_out,
                                        g_mix_pre, g_mix_post, g_ffn_pre, g_ffn_post)

    grads = [_disassemble_w_in(dwp, tables, tr=256).reshape(4, 2, D, IN_PAD),
             dw_out.reshape(4, 2, D // NDEV, D),
             dw_gu.reshape(4, 2, D, FB),
             dw_down.reshape(4, 2, DFF // NDEV, D)]
    from_sibling = _pair_exchange(grads)
    tiles = (256, 128, 256, 176)
    pair = [_pair_sum(g, got, idx, tr=tr, name=f"grad_pair_sum_{i}")
            for i, (g, got, tr) in enumerate(zip(grads, from_sibling, tiles))]
    from_chips = _chip_exchange([p[0] for p in pair])

    small = _small_all_reduce([dg_mix_pre, dg_mix_post, dg_ffn_pre, dg_ffn_post, dg_attn, dg_conv, dtaps, dbf, loss_p])
    taps_full = jnp.concatenate([small[5:6, :CW], small[5:6, CW:], small[6:7, :CW]], axis=0)
    small_grads = {
        "b_forget": small[6:7, CW:CW + H], "conv_w": lax.dynamic_slice(taps_full, (0, me * 64), (3, 64)),
        "g_attn_out": small[4:5, :AW], "g_conv_out": small[4:5, AW:], "g_mix_pre": small[0:1], "g_mix_post": small[1:2],
        "g_ffn_pre": small[2:3], "g_ffn_post": small[3:4]}
    loss = small[6, CW + 128]

    big = {"w_in": (pad_in(w_in[0]), pad_in(m_w_in[0]), pad_in(v_w_in[0])), "w_out": (w_out[0], m_w_out[0], v_w_out[0]),
           "w_gate_up": (w_gate_up[0], m_w_gate_up[0], v_w_gate_up[0]), "w_down": (w_down[0], m_w_down[0], v_w_down[0])}
    res = {}
    for (name, (w, m, v)), got, p, tr in zip(big.items(), from_chips, pair, tiles):
        outs = _chip_sum_adamw(got, p[1], idx, w, m, v, tr=tr, name="adamw_" + name)
        if name == "w_in":
            outs = [o[:, :IN_COLS] for o in outs]
        res[name] = [o[None] for o in outs]
    smalls = {"b_forget": (b_forget, m_b_forget, v_b_forget), "conv_w": (conv_w[0], m_conv_w[0], v_conv_w[0]),
              "g_attn_out": (g_attn_out, m_g_attn_out, v_g_attn_out), "g_conv_out": (g_conv_out, m_g_conv_out, v_g_conv_out),
              "g_mix_pre": (g_mix_pre, m_g_mix_pre, v_g_mix_pre), "g_mix_post": (g_mix_post, m_g_mix_post, v_g_mix_post),
              "g_ffn_pre": (g_ffn_pre, m_g_ffn_pre, v_g_ffn_pre), "g_ffn_post": (g_ffn_post, m_g_ffn_post, v_g_ffn_post)}
    for name, (w, m, v) in smalls.items():
        g = small_grads[name]
        outs = [g] + list(_adamw(w, g, m, v, tr=w.shape[0], name="adamw_" + name))
        res[name] = [o[None] for o in outs] if name == "conv_w" else outs

    order = ["w_in", "b_forget", "conv_w", "g_attn_out", "g_conv_out", "w_out", "g_mix_pre", "g_mix_post",
             "w_gate_up", "w_down", "g_ffn_pre", "g_ffn_post"]
    outs = [loss, grad_x[None]]
    for k in range(4):
        outs += [res[n][k] for n in order]
    return tuple(outs)
```

```python
import functools

import numpy as np

import jax
import jax.numpy as jnp
from jax import lax
from jax.experimental import pallas as pl
from jax.experimental.pallas import tpu as pltpu

F32 = jnp.float32
BF16 = jnp.bfloat16
HIGHEST = lax.Precision.HIGHEST
MESH_ID = pl.DeviceIdType.MESH

D = 1024
H = 8
DH = 64
AW = 512
CW = 512
DFF = 2816
FB = DFF // 4
HP = 128
OFF_Q, OFF_K, OFF_V, OFF_BCU, OFF_F = 0, 1024, 2048, 2560, 4096
WP = OFF_F + 128
PIECES = ((OFF_Q, OFF_K), (OFF_K, OFF_V), (OFF_V, OFF_BCU), (OFF_BCU, OFF_F), (OFF_F, WP))
EPS = 1e-6
NDEV = 8
LANES = 128
SUBLANES = 8
IN_COLS = 385
IN_PAD = 512
WIN = 896

ADAM_LR, ADAM_B1, ADAM_B2, ADAM_EPS, ADAM_WD, ADAM_STEP = 0.001, 0.9, 0.999, 1e-08, 0.01, 10

NT = (((1,), (1,)), ((), ()))
TN = (((0,), (0,)), ((), ()))


def _cparams(vmem_mb=None, sem=None):
    kw = {}
    if vmem_mb is not None:
        kw["vmem_limit_bytes"] = vmem_mb << 20
    if sem is not None:
        kw["dimension_semantics"] = sem
    return pltpu.CompilerParams(**kw)


def _full(shape):
    return pl.BlockSpec(shape, lambda *_: (0,) * len(shape))


def _resident(shape):
    return pl.BlockSpec(shape, lambda *_: (0,) * len(shape), pipeline_mode=pl.Buffered(1))


def _rows(tm, width):
    return pl.BlockSpec((tm, width), lambda i: (i, 0))


def _fold8(v):
    r, w = v.shape
    return jnp.sum(v.reshape(r // SUBLANES, SUBLANES, w), axis=0)


def _split_dot(v, m01):
    hi = v.astype(BF16)
    lo = (v - hi.astype(F32)).astype(BF16)
    return (jnp.dot(hi, m01, preferred_element_type=F32)
            + jnp.dot(lo, m01, preferred_element_type=F32))


def _rms_fwd(v, g):
    r = lax.rsqrt(jnp.mean(v * v, axis=-1, keepdims=True) + EPS)
    n = v * r
    return n * g, n, r


def _rms_bwd(do, n, r, g):
    dn = do * g
    return r * (dn - n * jnp.mean(dn * n, axis=-1, keepdims=True)), do * n


def _padded_column(n):
    if n < AW:
        return OFF_Q + HP * (n // DH) + n % DH, 0.125
    if n < 2 * AW:
        m = n - AW
        return OFF_K + HP * (m // DH) + m % DH, 1.0
    if n < 3 * AW:
        return OFF_V + n - 2 * AW, 1.0
    if n < 3 * AW + H:
        return OFF_F + n - 3 * AW, 1.0
    return OFF_BCU + n - 3 * AW - H, 1.0


def _in_layout_tables():
    dest = -np.ones((IN_PAD, LANES), np.int32)
    dest_f = -np.ones((IN_PAD, LANES), np.int32)
    scale = np.zeros((IN_PAD, LANES), np.float32)
    starts = []
    for k in range(NDEV):
        cols = [_padded_column(IN_COLS * k + j) for j in range(IN_COLS)]
        main = [c for c, _ in cols if c < OFF_F]
        ws = min((min(main) // LANES) * LANES, OFF_F - WIN)
        assert ws <= min(main) and max(main) < ws + WIN
        starts.append(ws)
        for j, (c, sc) in enumerate(cols):
            scale[j, k] = sc
            if c < OFF_F:
                dest[j, k] = c - ws
            else:
                dest_f[j, k] = c - OFF_F
    f_shards = tuple(k for k in range(NDEV) if (dest_f[:, k] >= 0).any())
    return tuple(starts), f_shards, jnp.asarray(dest), jnp.asarray(dest_f), jnp.asarray(scale)


def _perm(dest_ref, scale_ref, k, width):
    lane = lax.broadcasted_iota(jnp.int32, (IN_PAD, width), 1)
    return jnp.where(dest_ref[:, k:k + 1] == lane, scale_ref[:, k:k + 1], 0.0).astype(BF16)


def _assemble_w_in(blocks, tables, *, tr):
    starts, f_shards, dest, dest_f, scale = tables

    def body(b_ref, dest_ref, destf_ref, scale_ref, o_ref):
        o_ref[...] = jnp.zeros_like(o_ref)
        for k in range(NDEV):
            b = b_ref[k]
            ws = starts[k]
            part = jnp.dot(b, _perm(dest_ref, scale_ref, k, WIN), preferred_element_type=F32)
            o_ref[:, ws:ws + WIN] += part.astype(BF16)
            if k in f_shards:
                part = jnp.dot(b, _perm(destf_ref, scale_ref, k, 128), preferred_element_type=F32)
                o_ref[:, OFF_F:WP] += part.astype(BF16)

    tab = _full((IN_PAD, LANES))
    return pl.pallas_call(
        body, name="assemble_w_in", grid=(D // tr,),
        in_specs=[pl.BlockSpec((NDEV, tr, IN_PAD), lambda i: (0, i, 0)), tab, tab, tab],
        out_specs=_rows(tr, WP),
        out_shape=jax.ShapeDtypeStruct((D, WP), BF16),
        compiler_params=_cparams(48, ("arbitrary",)),
    )(blocks, dest, dest_f, scale)


def _disassemble_w_in(pieces, tables, *, tr):
    starts, f_shards, dest, dest_f, scale = tables

    def body(q_ref, k_ref, v_ref, bcu_ref, f_ref, dest_ref, destf_ref, scale_ref, o_ref):
        refs = (q_ref, k_ref, v_ref, bcu_ref, f_ref)

        def window(ws):
            parts = []
            for ref, (lo, hi) in zip(refs, PIECES):
                a, b = max(ws, lo), min(ws + WIN, hi)
                if a < b:
                    parts.append(ref[:, a - lo:b - lo])
            return parts[0] if len(parts) == 1 else jnp.concatenate(parts, axis=1)

        for k in range(NDEV):
            acc = lax.dot_general(window(starts[k]), _perm(dest_ref, scale_ref, k, WIN), NT, preferred_element_type=F32)
            if k in f_shards:
                acc = acc + lax.dot_general(f_ref[...], _perm(destf_ref, scale_ref, k, 128), NT, preferred_element_type=F32)
            o_ref[k] = acc.astype(BF16)

    tab = _full((IN_PAD, LANES))
    return pl.pallas_call(
        body, name="disassemble_w_in", grid=(D // tr,),
        in_specs=[_rows(tr, hi - lo) for lo, hi in PIECES] + [tab, tab, tab],
        out_specs=pl.BlockSpec((NDEV, tr, IN_PAD), lambda i: (0, i, 0)),
        out_shape=jax.ShapeDtypeStruct((NDEV, D, IN_PAD), BF16),
        compiler_params=_cparams(48, ("arbitrary",)),
    )(*pieces, dest, dest_f, scale)


def _in_proj(x, g1, wp, bfp, pq, pk, oq, ok, *, tm):
    s = x.shape[0]

    def body(x_ref, g_ref, w_ref, bf_ref, pq_ref, pk_ref, oq_ref, ok_ref,
             h_ref, qp_ref, kp_ref, v_ref, bcu_ref, z_ref, carry):
        @pl.when(pl.program_id(0) == 0)
        def _():
            carry[...] = jnp.zeros_like(carry)

        h = _rms_fwd(x_ref[...], g_ref[...])[0].astype(BF16)
        h_ref[...] = h
        z = jnp.dot(h, w_ref[:, OFF_F:WP], preferred_element_type=F32) + bf_ref[...]
        z_ref[...] = z
        lane = lax.broadcasted_iota(jnp.int32, (tm, 128), 1)
        logf = jnp.where(lane < H, jnp.minimum(z, 0.0) - jnp.log(1.0 + jnp.exp(-jnp.abs(z))), 0.0)
        row = lax.broadcasted_iota(jnp.int32, (tm, tm), 0)
        col = lax.broadcasted_iota(jnp.int32, (tm, tm), 1)
        tri = (col <= row).astype(F32)
        c = jnp.dot(tri, logf, precision=HIGHEST, preferred_element_type=F32) + carry[0:1, :]
        carry[...] = jnp.broadcast_to(c[tm - 1:tm, :], carry.shape)
        c1 = c.astype(BF16).astype(F32)
        r1 = c - c1
        c2 = r1.astype(BF16).astype(F32)
        c3 = (r1 - c2).astype(BF16).astype(F32)
        zc = (c1 + pltpu.roll(c2, 8, axis=1) + pltpu.roll(c3, 16, axis=1)).astype(BF16)
        q = jnp.dot(h, w_ref[:, OFF_Q:OFF_K], preferred_element_type=F32)
        qp_ref[...] = (q + jnp.dot(zc, pq_ref[...], preferred_element_type=F32) + oq_ref[...]).astype(BF16)
        k = jnp.dot(h, w_ref[:, OFF_K:OFF_V], preferred_element_type=F32)
        kp_ref[...] = (k + jnp.dot(zc, pk_ref[...], preferred_element_type=F32) + ok_ref[...]).astype(BF16)
        v_ref[...] = jnp.dot(h, w_ref[:, OFF_V:OFF_BCU], preferred_element_type=F32).astype(BF16)
        bcu_ref[...] = jnp.dot(h, w_ref[:, OFF_BCU:OFF_F], preferred_element_type=F32)

    return pl.pallas_call(
        body, name="in_proj", grid=(s // tm,),
        in_specs=[_rows(tm, D), _full((1, D)), _resident((D, WP)), _full((1, 128)),
                  _full((128, 1024)), _full((128, 1024)), _full((1, 1024)), _full((1, 1024))],
        out_specs=[_rows(tm, D), _rows(tm, 1024), _rows(tm, 1024), _rows(tm, AW), _rows(tm, 3 * CW), _rows(tm, 128)],
        out_shape=[jax.ShapeDtypeStruct((s, D), BF16), jax.ShapeDtypeStruct((s, 1024), BF16),
                   jax.ShapeDtypeStruct((s, 1024), BF16), jax.ShapeDtypeStruct((s, AW), BF16),
                   jax.ShapeDtypeStruct((s, 3 * CW), F32), jax.ShapeDtypeStruct((s, 128), F32)],
        scratch_shapes=[pltpu.VMEM((SUBLANES, 128), F32)],
        compiler_params=_cparams(56, ("arbitrary",)),
    )(x, g1, wp, bfp, pq, pk, oq, ok)


def _attn_fwd(qp, kp, v, *, t):
    s = qp.shape[0]
    nq = s // t

    def body(q_ref, k_ref, v_ref, o_ref, lse_ref):
        qi = pl.program_id(1)
        row = lax.broadcasted_iota(jnp.int32, (t, t), 0)
        col = lax.broadcasted_iota(jnp.int32, (t, t), 1)
        lane = lax.broadcasted_iota(jnp.int32, (t, 128), 1)
        outs, lses = [], []
        for hh in range(2):
            q = q_ref[:, HP * hh:HP * (hh + 1)]

            def step(ki, carry, masked, hh=hh, q=q):
                m, l, acc = carry
                off = pl.multiple_of(ki * t, t)
                k = k_ref[pl.ds(off, t), HP * hh:HP * (hh + 1)]
                sc = lax.dot_general(q, k, NT, preferred_element_type=F32)
                if masked:
                    sc = jnp.where(col <= row, sc, -1e30)
                mn = jnp.maximum(m, jnp.max(sc, axis=-1, keepdims=True))
                p = jnp.exp(sc - mn)
                a = jnp.exp(m - mn)
                l = a * l + jnp.sum(p, axis=-1, keepdims=True)
                p_hi = p.astype(BF16)
                p_lo = (p - p_hi.astype(F32)).astype(BF16)
                vv = v_ref[pl.ds(off, t), :]
                acc = a * acc + (jnp.dot(p_hi, vv, preferred_element_type=F32)
                                 + jnp.dot(p_lo, vv, preferred_element_type=F32))
                return mn, l, acc

            init = (jnp.full((t, 1), -1e30, F32), jnp.zeros((t, 1), F32), jnp.zeros((t, 128), F32))
            carry = lax.fori_loop(0, qi, functools.partial(step, masked=False), init)
            m, l, acc = step(qi, carry, True)
            outs.append(acc / l)
            lses.append(jnp.broadcast_to(m + jnp.log(l), (t, 128)))
        o_ref[...] = jnp.where(lane < DH, outs[0], outs[1])
        lse_ref[...] = jnp.where(lane < DH, lses[0], lses[1])

    return pl.pallas_call(
        body, name="attn_fwd", grid=(H // 2, nq),
        in_specs=[pl.BlockSpec((t, 2 * HP), lambda p, i: (i, p)),
                  pl.BlockSpec((s, 2 * HP), lambda p, i: (0, p)),
                  pl.BlockSpec((s, 128), lambda p, i: (0, p))],
        out_specs=[pl.BlockSpec((t, 128), lambda p, i: (i, p)), pl.BlockSpec((t, 128), lambda p, i: (i, p))],
        out_shape=[jax.ShapeDtypeStruct((s, AW), F32), jax.ShapeDtypeStruct((s, AW), F32)],
        compiler_params=_cparams(48, ("arbitrary", "arbitrary")),
    )(qp, kp, v)


def _conv_taps(bcu_ref, halo_ref, first, tm):
    z = bcu_ref[:, CW:2 * CW] * bcu_ref[:, 2 * CW:3 * CW]
    zh = jnp.where(first, 0.0, halo_ref[:, CW:2 * CW] * halo_ref[:, 2 * CW:3 * CW])
    row = lax.broadcasted_iota(jnp.int32, (tm, CW), 0)
    z1 = jnp.where(row == 0, zh[7:8, :], pltpu.roll(z, 1, axis=0))
    z2 = jnp.where(row == 0, zh[6:7, :], jnp.where(row == 1, zh[7:8, :], pltpu.roll(z, 2, axis=0)))
    return z, z1, z2


def _halo_before(tm, width):
    return pl.BlockSpec((SUBLANES, width), lambda i: (jnp.maximum(i * (tm // SUBLANES) - 1, 0), 0))


def _mix_out(o, bcu, cw8, ga, gc, gsum, w_out, x, g_post, g_ffn_pre, *, tm):
    s = x.shape[0]

    def body(o_ref, bcu_ref, halo_ref, cw_ref, ga_ref, gc_ref, gs_ref, w_ref, x_ref, g_ref, gf_ref,
             merged_ref, y_ref, x2_ref, cv_ref, h2_ref):
        z, z1, z2 = _conv_taps(bcu_ref, halo_ref, pl.program_id(0) == 0, tm)
        cv = cw_ref[0:1, :] * z2 + cw_ref[1:2, :] * z1 + cw_ref[2:3, :] * z
        cv_ref[...] = cv
        conv = bcu_ref[:, 0:CW] * cv
        ov = o_ref[...]
        ra = lax.rsqrt(_split_dot(ov * ov, gs_ref[...]) * (1.0 / DH) + EPS)
        rc = lax.rsqrt(_split_dot(conv * conv, gs_ref[...]) * (1.0 / DH) + EPS)
        merged = jnp.concatenate([ov * ra * ga_ref[...], conv * rc * gc_ref[...]], axis=1).astype(BF16)
        merged_ref[...] = merged
        y = jnp.dot(merged, w_ref[...], preferred_element_type=F32)
        y_ref[...] = y
        x2 = x_ref[...] + _rms_fwd(y, g_ref[...])[0]
        x2_ref[...] = x2
        h2_ref[...] = _rms_fwd(x2, gf_ref[...])[0].astype(BF16)

    return pl.pallas_call(
        body, name="mix_out", grid=(s // tm,),
        in_specs=[_rows(tm, AW), _rows(tm, 3 * CW), _halo_before(tm, 3 * CW), _full((SUBLANES, CW)),
                  _full((1, AW)), _full((1, CW)), _full((CW, CW)), _resident((D, D)), _rows(tm, D), _full((1, D)),
                  _full((1, D))],
        out_specs=[_rows(tm, D), _rows(tm, D), _rows(tm, D), _rows(tm, CW), _rows(tm, D)],
        out_shape=[jax.ShapeDtypeStruct((s, D), BF16), jax.ShapeDtypeStruct((s, D), F32),
                   jax.ShapeDtypeStruct((s, D), F32), jax.ShapeDtypeStruct((s, CW), F32),
                   jax.ShapeDtypeStruct((s, D), BF16)],
        compiler_params=_cparams(48, ("arbitrary",)),
    )(o, bcu, bcu, cw8, ga, gc, gsum, w_out, x, g_post, g_ffn_pre)


def _ffn_up(h2, wgu, *, tm):
    s = h2.shape[0]

    def body(h_ref, w_ref, gate_ref, up_ref, a_ref):
        h = h_ref[...]
        gate = jnp.dot(h, w_ref[0, 0], preferred_element_type=F32)
        up = jnp.dot(h, w_ref[1, 0], preferred_element_type=F32)
        gate_ref[0] = gate.astype(BF16)
        up_ref[0] = up.astype(BF16)
        a_ref[0] = (gate * jax.nn.sigmoid(gate) * up).astype(BF16)

    blk = pl.BlockSpec((1, tm, FB), lambda j, i: (j, i, 0))
    return pl.pallas_call(
        body, name="ffn_up", grid=(4, s // tm),
        in_specs=[pl.BlockSpec((tm, D), lambda j, i: (i, 0)),
                  pl.BlockSpec((2, 1, D, FB), lambda j, i: (0, j, 0, 0))],
        out_specs=[blk, blk, blk],
        out_shape=[jax.ShapeDtypeStruct((4, s, FB), BF16)] * 3,
        compiler_params=_cparams(48, ("arbitrary", "arbitrary")),
    )(h2, wgu)


def _ffn_down_loss(a, wd, x2, target, g_post, *, tm):
    s = x2.shape[0]

    def body(a_ref, w_ref, x2_ref, t_ref, g_ref, dx3_ref, dff_ref, loss_ref, dg_ref):
        @pl.when(pl.program_id(0) == 0)
        def _():
            loss_ref[...] = jnp.zeros_like(loss_ref)
            dg_ref[...] = jnp.zeros_like(dg_ref)

        ff = jnp.dot(a_ref[0], w_ref[0], preferred_element_type=F32)
        for j in range(1, 4):
            ff = ff + jnp.dot(a_ref[j], w_ref[j], preferred_element_type=F32)
        out, n, r = _rms_fwd(ff, g_ref[...])
        e = x2_ref[...] + out - t_ref[...]
        loss_ref[...] += _fold8(e * e)
        dx3 = e * (1.0 / D)
        dx3_ref[...] = dx3
        dff, dg = _rms_bwd(dx3, n, r, g_ref[...])
        dff_ref[...] = dff.astype(BF16)
        dg_ref[...] += _fold8(dg)

    return pl.pallas_call(
        body, name="ffn_down_loss", grid=(s // tm,),
        in_specs=[pl.BlockSpec((4, tm, FB), lambda i: (0, i, 0)), _resident((4, FB, D)), _rows(tm, D), _rows(tm, D),
                  _full((1, D))],
        out_specs=[_rows(tm, D), _rows(tm, D), _full((SUBLANES, D)), _full((SUBLANES, D))],
        out_shape=[jax.ShapeDtypeStruct((s, D), F32), jax.ShapeDtypeStruct((s, D), BF16),
                   jax.ShapeDtypeStruct((SUBLANES, D), F32), jax.ShapeDtypeStruct((SUBLANES, D), F32)],
        compiler_params=_cparams(48, ("arbitrary",)),
    )(a, wd, x2, target, g_post)


def _ffn_bwd_act(dff, wd, gate, up, *, tm):
    s = dff.shape[0]

    def body(dff_ref, w_ref, gate_ref, up_ref, dgu_ref):
        da = lax.dot_general(dff_ref[...], w_ref[0], NT, preferred_element_type=F32)
        g = gate_ref[0].astype(F32)
        sg = jax.nn.sigmoid(g)
        dgu_ref[0, 0] = (da * up_ref[0].astype(F32) * (sg * (1.0 + g * (1.0 - sg)))).astype(BF16)
        dgu_ref[1, 0] = (da * (g * sg)).astype(BF16)

    blk = pl.BlockSpec((1, tm, FB), lambda j, i: (j, i, 0))
    return pl.pallas_call(
        body, name="ffn_bwd_act", grid=(4, s // tm),
        in_specs=[pl.BlockSpec((tm, D), lambda j, i: (i, 0)), pl.BlockSpec((1, FB, D), lambda j, i: (j, 0, 0)), blk, blk],
        out_specs=pl.BlockSpec((2, 1, tm, FB), lambda j, i: (0, j, i, 0)),
        out_shape=jax.ShapeDtypeStruct((2, 4, s, FB), BF16),
        compiler_params=_cparams(48, ("arbitrary", "arbitrary")),
    )(dff, wd, gate, up)


def _grad_matmul(a, b, *, ta, tb, ts, name):
    s, ka = a.shape
    nb = b.shape[1]
    nk = s // ts

    def body(a_ref, b_ref, o_ref, acc):
        k = pl.program_id(2)

        @pl.when(k == 0)
        def _():
            acc[...] = jnp.zeros_like(acc)

        acc[...] += lax.dot_general(a_ref[...], b_ref[...], TN, preferred_element_type=F32)

        @pl.when(k == nk - 1)
        def _():
            o_ref[...] = acc[...].astype(BF16)

    return pl.pallas_call(
        body, name=name, grid=(ka // ta, nb // tb, nk),
        in_specs=[pl.BlockSpec((ts, ta), lambda i, j, k: (k, i)), pl.BlockSpec((ts, tb), lambda i, j, k: (k, j))],
        out_specs=pl.BlockSpec((ta, tb), lambda i, j, k: (i, j)),
        out_shape=jax.ShapeDtypeStruct((ka, nb), BF16),
        scratch_shapes=[pltpu.VMEM((ta, tb), F32)],
        compiler_params=_cparams(48, ("arbitrary", "arbitrary", "arbitrary")),
    )(a, b)


def _grad_matmul_blocks(a, b, *, ts, name):
    nblk = a.shape[0] if a.ndim == 3 else b.shape[0]
    s = a.shape[-2]
    ka, nb = a.shape[-1], b.shape[-1]
    nk = s // ts

    def body(a_ref, b_ref, o_ref, acc):
        k = pl.program_id(1)

        @pl.when(k == 0)
        def _():
            acc[...] = jnp.zeros_like(acc)

        av = a_ref[0] if a.ndim == 3 else a_ref[...]
        bv = b_ref[0] if b.ndim == 3 else b_ref[...]
        acc[...] += lax.dot_general(av, bv, TN, preferred_element_type=F32)

        @pl.when(k == nk - 1)
        def _():
            o_ref[0] = acc[...].astype(BF16)

    def spec(arr, width):
        if arr.ndim == 3:
            return pl.BlockSpec((1, ts, width), lambda j, k: (j, k, 0))
        return pl.BlockSpec((ts, width), lambda j, k: (k, 0))

    return pl.pallas_call(
        body, name=name, grid=(nblk, nk),
        in_specs=[spec(a, ka), spec(b, nb)],
        out_specs=pl.BlockSpec((1, ka, nb), lambda j, k: (j, 0, 0)),
        out_shape=jax.ShapeDtypeStruct((nblk, ka, nb), BF16),
        scratch_shapes=[pltpu.VMEM((ka, nb), F32)],
        compiler_params=_cparams(48, ("arbitrary", "arbitrary")),
    )(a, b)


def _ffn_bwd_in(dgu, wgu, x2, g_pre, dx3, y, g_post, *, tm):
    s = x2.shape[0]

    def body(dgu_ref, w_ref, x2_ref, gpre_ref, dx3_ref, y_ref, gpost_ref,
             dx2_ref, dy_ref, dgpre_ref, dgpost_ref):
        @pl.when(pl.program_id(0) == 0)
        def _():
            dgpre_ref[...] = jnp.zeros_like(dgpre_ref)
            dgpost_ref[...] = jnp.zeros_like(dgpost_ref)

        dh2 = None
        for a in range(2):
            for j in range(4):
                part = lax.dot_general(dgu_ref[a, j], w_ref[a, j], NT, preferred_element_type=F32)
                dh2 = part if dh2 is None else dh2 + part
        _, n2, r2 = _rms_fwd(x2_ref[...], gpre_ref[...])
        dxn, dg = _rms_bwd(dh2, n2, r2, gpre_ref[...])
        dgpre_ref[...] += _fold8(dg)
        dx2 = dx3_ref[...] + dxn
        dx2_ref[...] = dx2
        _, ny, ry = _rms_fwd(y_ref[...], gpost_ref[...])
        dy, dg2 = _rms_bwd(dx2, ny, ry, gpost_ref[...])
        dy_ref[...] = dy.astype(BF16)
        dgpost_ref[...] += _fold8(dg2)

    return pl.pallas_call(
        body, name="ffn_bwd_in", grid=(s // tm,),
        in_specs=[pl.BlockSpec((2, 4, tm, FB), lambda i: (0, 0, i, 0)), _resident((2, 4, D, FB)), _rows(tm, D),
                  _full((1, D)), _rows(tm, D), _rows(tm, D), _full((1, D))],
        out_specs=[_rows(tm, D), _rows(tm, D), _full((SUBLANES, D)), _full((SUBLANES, D))],
        out_shape=[jax.ShapeDtypeStruct((s, D), F32), jax.ShapeDtypeStruct((s, D), BF16),
                   jax.ShapeDtypeStruct((SUBLANES, D), F32), jax.ShapeDtypeStruct((SUBLANES, D), F32)],
        compiler_params=_cparams(56, ("arbitrary",)),
    )(dgu, wgu, x2, g_pre, dx3, y, g_post)


def _mix_bwd(dy, w_out, o, cv, bcu, ga, gc, gsum, *, tm):
    s = dy.shape[0]

    def group_norm_bwd(dn_out, v, g, gs):
        r = lax.rsqrt(_split_dot(v * v, gs) * (1.0 / DH) + EPS)
        n = v * r
        dn = dn_out * g
        return r * (dn - n * (_split_dot(dn * n, gs) * (1.0 / DH))), dn_out * n

    def body(dy_ref, w_ref, o_ref, cv_ref, bcu_ref, ga_ref, gc_ref, gs_ref,
             do_ref, dl_ref, dcv_ref, db_ref, dga_ref, dgc_ref):
        @pl.when(pl.program_id(0) == 0)
        def _():
            dga_ref[...] = jnp.zeros_like(dga_ref)
            dgc_ref[...] = jnp.zeros_like(dgc_ref)

        dm = lax.dot_general(dy_ref[...], w_ref[...], NT, preferred_element_type=F32)
        ov = o_ref[...]
        do, dga = group_norm_bwd(dm[:, 0:AW], ov, ga_ref[...], gs_ref[...])
        dob = do.astype(BF16)
        do_ref[...] = dob
        dl_ref[...] = _split_dot(dob.astype(F32) * ov, gs_ref[...])
        dga_ref[...] += _fold8(dga)
        gate_b = bcu_ref[:, 0:CW]
        cv = cv_ref[...]
        dconv, dgc = group_norm_bwd(dm[:, AW:D], gate_b * cv, gc_ref[...], gs_ref[...])
        dgc_ref[...] += _fold8(dgc)
        dcv_ref[...] = dconv * gate_b
        db_ref[...] = (dconv * cv).astype(BF16)

    return pl.pallas_call(
        body, name="mix_bwd", grid=(s // tm,),
        in_specs=[_rows(tm, D), _resident((D, D)), _rows(tm, AW), _rows(tm, CW), _rows(tm, 3 * CW),
                  _full((1, AW)), _full((1, CW)), _full((CW, CW))],
        out_specs=[_rows(tm, AW), _rows(tm, AW), _rows(tm, CW), _rows(tm, CW),
                   _full((SUBLANES, AW)), _full((SUBLANES, CW))],
        out_shape=[jax.ShapeDtypeStruct((s, AW), BF16), jax.ShapeDtypeStruct((s, AW), F32),
                   jax.ShapeDtypeStruct((s, CW), F32), jax.ShapeDtypeStruct((s, CW), BF16),
                   jax.ShapeDtypeStruct((SUBLANES, AW), F32), jax.ShapeDtypeStruct((SUBLANES, CW), F32)],
        compiler_params=_cparams(48, ("arbitrary",)),
    )(dy, w_out, o, cv, bcu, ga, gc, gsum)


def _conv_bwd(dcv, db, bcu, cw8, *, tm):
    s = dcv.shape[0]
    nt = s // tm

    def body(dcv_ref, nxt_ref, db_ref, bcu_ref, halo_ref, cw_ref, dbcu_ref, dw_ref):
        i = pl.program_id(0)

        @pl.when(i == 0)
        def _():
            dw_ref[...] = jnp.zeros_like(dw_ref)

        z, z1, z2 = _conv_taps(bcu_ref, halo_ref, i == 0, tm)
        d = dcv_ref[...]
        dw_ref[0] += _fold8(d * z2)
        dw_ref[1] += _fold8(d * z1)
        dw_ref[2] += _fold8(d * z)
        nx = jnp.where(i == nt - 1, 0.0, nxt_ref[...])
        row = lax.broadcasted_iota(jnp.int32, (tm, CW), 0)
        d1 = jnp.where(row == tm - 1, nx[0:1, :], pltpu.roll(d, tm - 1, axis=0))
        d2 = jnp.where(row == tm - 2, nx[0:1, :], jnp.where(row == tm - 1, nx[1:2, :], pltpu.roll(d, tm - 2, axis=0)))
        dz = cw_ref[2:3, :] * d + cw_ref[1:2, :] * d1 + cw_ref[0:1, :] * d2
        dbcu_ref[:, 0:CW] = db_ref[...]
        dbcu_ref[:, CW:2 * CW] = (dz * bcu_ref[:, 2 * CW:3 * CW]).astype(BF16)
        dbcu_ref[:, 2 * CW:3 * CW] = (dz * bcu_ref[:, CW:2 * CW]).astype(BF16)

    return pl.pallas_call(
        body, name="conv_bwd", grid=(nt,),
        in_specs=[_rows(tm, CW),
                  pl.BlockSpec((SUBLANES, CW), lambda i: (jnp.minimum((i + 1) * (tm // SUBLANES), s // SUBLANES - 1), 0)),
                  _rows(tm, CW), _rows(tm, 3 * CW), _halo_before(tm, 3 * CW), _full((SUBLANES, CW))],
        out_specs=[_rows(tm, 3 * CW), _full((3, SUBLANES, CW))],
        out_shape=[jax.ShapeDtypeStruct((s, 3 * CW), BF16), jax.ShapeDtypeStruct((3, SUBLANES, CW), F32)],
        compiler_params=_cparams(48, ("arbitrary",)),
    )(dcv, dcv, db, bcu, bcu, cw8)


def _attn_bwd(qp, kp, v, do, lse, dl, *, t):
    s = qp.shape[0]
    nq = s // t

    def body(q_ref, k_ref, v_ref, do_ref, lse_ref, dl_ref, dq_ref, dk_ref, dv_ref, dkx_ref, dq_acc):
        ki = pl.program_id(1)

        @pl.when(ki == 0)
        def _():
            dq_acc[...] = jnp.zeros_like(dq_acc)

        row = lax.broadcasted_iota(jnp.int32, (t, t), 0)
        col = lax.broadcasted_iota(jnp.int32, (t, t), 1)
        lane = lax.broadcasted_iota(jnp.int32, (t, 128), 1)
        v2 = v_ref[...]
        dv = jnp.zeros((t, 128), F32)
        dks = []
        for hh in range(2):
            kh = k_ref[:, HP * hh:HP * (hh + 1)]
            in_head = (lane >= DH * hh) & (lane < DH * (hh + 1))

            def step(qi, carry, masked, hh=hh, kh=kh, in_head=in_head):
                dk, dv = carry
                off = pl.multiple_of(qi * t, t)
                q = q_ref[pl.ds(off, t), HP * hh:HP * (hh + 1)]
                dom = jnp.where(in_head, do_ref[pl.ds(off, t), :], jnp.zeros((), BF16))
                sc = lax.dot_general(q, kh, NT, preferred_element_type=F32)
                sc = sc - lse_ref[pl.ds(off, t), DH * hh:DH * hh + 1]
                if masked:
                    sc = jnp.where(col <= row, sc, -1e30)
                p = jnp.exp(sc)
                dp = lax.dot_general(dom, v2, NT, preferred_element_type=F32)
                ds32 = p * (dp - dl_ref[pl.ds(off, t), DH * hh:DH * hh + 1])
                ds = ds32.astype(BF16)
                ds_lo = (ds32 - ds.astype(F32)).astype(BF16)
                dv = dv + lax.dot_general(p.astype(BF16), dom, TN, preferred_element_type=F32)
                dk = dk + (lax.dot_general(ds, q, TN, preferred_element_type=F32)
                           + lax.dot_general(ds_lo, q, TN, preferred_element_type=F32))
                dq_acc[pl.ds(off, t), HP * hh:HP * (hh + 1)] += jnp.dot(ds, kh, preferred_element_type=F32)
                return dk, dv

            carry = step(ki, (jnp.zeros((t, HP), F32), dv), True)
            dk, dv = lax.fori_loop(ki + 1, nq, functools.partial(step, masked=False), carry)
            dk_ref[:, HP * hh:HP * (hh + 1)] = dk.astype(BF16)
            dks.append(dk)
        dv_ref[...] = dv.astype(BF16)
        dkx_ref[...] = jnp.where(lane < DH, pltpu.roll(dks[0], DH, axis=1), dks[1])

        @pl.when(ki == nq - 1)
        def _():
            dq_ref[...] = dq_acc[...].astype(BF16)

    return pl.pallas_call(
        body, name="attn_bwd", grid=(H // 2, nq),
        in_specs=[pl.BlockSpec((s, 2 * HP), lambda p, i: (0, p)),
                  pl.BlockSpec((t, 2 * HP), lambda p, i: (i, p)),
                  pl.BlockSpec((t, 128), lambda p, i: (i, p)),
                  pl.BlockSpec((s, 128), lambda p, i: (0, p)),
                  pl.BlockSpec((s, 128), lambda p, i: (0, p)),
                  pl.BlockSpec((s, 128), lambda p, i: (0, p))],
        out_specs=[pl.BlockSpec((s, 2 * HP), lambda p, i: (0, p)),
                   pl.BlockSpec((t, 2 * HP), lambda p, i: (i, p)),
                   pl.BlockSpec((t, 128), lambda p, i: (i, p)),
                   pl.BlockSpec((t, 128), lambda p, i: (i, p))],
        out_shape=[jax.ShapeDtypeStruct((s, 1024), BF16), jax.ShapeDtypeStruct((s, 1024), BF16),
                   jax.ShapeDtypeStruct((s, AW), BF16), jax.ShapeDtypeStruct((s, AW), F32)],
        scratch_shapes=[pltpu.VMEM((s, 2 * HP), F32)],
        compiler_params=_cparams(56, ("arbitrary", "arbitrary")),
    )(qp, kp, v, do, lse, dl)


def _forget_bwd(dkx, z, sel, *, tm):
    s = dkx.shape[0]
    nt = s // tm

    def body(dk_ref, z_ref, sel_ref, dfl_ref, dbf_ref, carry):
        @pl.when(pl.program_id(0) == 0)
        def _():
            carry[...] = jnp.zeros_like(carry)
            dbf_ref[...] = jnp.zeros_like(dbf_ref)

        dc = _split_dot(dk_ref[...], sel_ref[...])
        row = lax.broadcasted_iota(jnp.int32, (tm, tm), 0)
        col = lax.broadcasted_iota(jnp.int32, (tm, tm), 1)
        tri = (col >= row).astype(F32)
        dlogf = jnp.dot(tri, dc, precision=HIGHEST, preferred_element_type=F32) + carry[0:1, :]
        carry[...] = jnp.broadcast_to(dlogf[0:1, :], carry.shape)
        dz = dlogf * (1.0 - jax.nn.sigmoid(z_ref[...]))
        dfl_ref[...] = dz.astype(BF16)
        dbf_ref[...] += _fold8(dz)

    rev = lambda i: (nt - 1 - i, 0)
    return pl.pallas_call(
        body, name="forget_bwd", grid=(nt,),
        in_specs=[pl.BlockSpec((tm, AW), rev), pl.BlockSpec((tm, 128), rev), _full((AW, 128))],
        out_specs=[pl.BlockSpec((tm, 128), rev), _full((SUBLANES, 128))],
        out_shape=[jax.ShapeDtypeStruct((s, 128), BF16), jax.ShapeDtypeStruct((SUBLANES, 128), F32)],
        scratch_shapes=[pltpu.VMEM((SUBLANES, 128), F32)],
        compiler_params=_cparams(48, ("arbitrary",)),
    )(dkx, z, sel)


def _in_proj_bwd(pieces, wp, x, g1, dx2, *, tm):
    s = x.shape[0]

    def body(q_ref, k_ref, v_ref, bcu_ref, f_ref, w_ref, x_ref, g_ref, dx2_ref, dx_ref, dg_ref):
        @pl.when(pl.program_id(0) == 0)
        def _():
            dg_ref[...] = jnp.zeros_like(dg_ref)

        dh = None
        for ref, (lo, hi) in zip((q_ref, k_ref, v_ref, bcu_ref, f_ref), PIECES):
            part = lax.dot_general(ref[...], w_ref[:, lo:hi], NT, preferred_element_type=F32)
            dh = part if dh is None else dh + part
        _, n, r = _rms_fwd(x_ref[...], g_ref[...])
        dxn, dg = _rms_bwd(dh, n, r, g_ref[...])
        dx_ref[...] = dx2_ref[...] + dxn
        dg_ref[...] += _fold8(dg)

    return pl.pallas_call(
        body, name="in_proj_bwd", grid=(s // tm,),
        in_specs=[_rows(tm, hi - lo) for lo, hi in PIECES] + [_resident((D, WP)), _rows(tm, D), _full((1, D)), _rows(tm, D)],
        out_specs=[_rows(tm, D), _full((SUBLANES, D))],
        out_shape=[jax.ShapeDtypeStruct((s, D), F32), jax.ShapeDtypeStruct((SUBLANES, D), F32)],
        compiler_params=_cparams(56, ("arbitrary",)),
    )(*pieces, wp, x, g1, dx2)


def _position():
    return lax.axis_index("x"), lax.axis_index("y"), lax.axis_index("c")


ANY = pl.BlockSpec(memory_space=pl.ANY)


def _all_gather(shards):
    n = len(shards)

    def body(*refs):
        x_refs, out_refs = refs[:n], refs[n:2 * n]
        send_sems, recv_sems, local_sems = refs[2 * n:]
        x, y, c = _position()
        me, sibling = (x, y, c), (x, y, 1 - c)
        chips = [(1 - x, y), (x, 1 - y), (1 - x, 1 - y)]

        def copy(a, k, block, to, own=False):
            slot = out_refs[a].at[4 * block[0] + 2 * block[1] + block[2]]
            return pltpu.make_async_remote_copy(
                src_ref=x_refs[a] if own else slot, dst_ref=slot,
                send_sem=send_sems.at[7 * a + k], recv_sem=recv_sems.at[7 * a + k], device_id=to, device_id_type=MESH_ID)

        mine = [pltpu.make_async_copy(x_refs[a], out_refs[a].at[4 * x + 2 * y + c], local_sems.at[a]) for a in range(n)]
        for cp in mine:
            cp.start()
        first = []
        for a in range(n):
            first.append(copy(a, 0, me, sibling, own=True))
            first += [copy(a, 1 + j, me, (*chip, c), own=True) for j, chip in enumerate(chips)]
        for cp in first:
            cp.start()
        passed = []
        for j, chip in enumerate(chips):
            for a in range(n):
                copy(a, 1 + j, (*chip, c), me).wait_recv()
                fwd = copy(a, 4 + j, (*chip, c), sibling)
                fwd.start()
                passed.append(fwd)
        for a in range(n):
            copy(a, 0, sibling, me).wait_recv()
            for j, chip in enumerate(chips):
                copy(a, 4 + j, (*chip, 1 - c), me).wait_recv()
        for cp in first + passed:
            cp.wait_send()
        for cp in mine:
            cp.wait()

    return pl.pallas_call(
        body, name="all_gather_weights",
        out_shape=[jax.ShapeDtypeStruct((NDEV,) + sh.shape, sh.dtype) for sh in shards],
        in_specs=[ANY] * n, out_specs=[ANY] * n,
        scratch_shapes=[pltpu.SemaphoreType.DMA((7 * n,)), pltpu.SemaphoreType.DMA((7 * n,)), pltpu.SemaphoreType.DMA((n,))],
    )(*shards)


def _pair_exchange(grads):
    n = len(grads)

    def body(*refs):
        g_refs, out_refs = refs[:n], refs[n:2 * n]
        send_sems, recv_sems = refs[2 * n:]
        x, y, c = _position()
        copies = [pltpu.make_async_remote_copy(
            src_ref=g_refs[a].at[:, pl.ds(1 - c, 1)], dst_ref=out_refs[a], send_sem=send_sems.at[a],
            recv_sem=recv_sems.at[a], device_id=(x, y, 1 - c), device_id_type=MESH_ID) for a in range(n)]
        for cp in copies:
            cp.start()
        for cp in copies:
            cp.wait()

    return pl.pallas_call(
        body, name="grad_pair_exchange",
        out_shape=[jax.ShapeDtypeStruct((4, 1) + g.shape[2:], g.dtype) for g in grads],
        in_specs=[ANY] * n, out_specs=[ANY] * n,
        scratch_shapes=[pltpu.SemaphoreType.DMA((n,)), pltpu.SemaphoreType.DMA((n,))],
    )(*grads)


def _pair_sum(g, got, idx, *, tr, name):
    r, c = g.shape[2:]

    def body(idx_ref, g_ref, got_ref, pb_ref, own_ref):
        p = g_ref[0, 0].astype(F32) + got_ref[0, 0].astype(F32)
        pb_ref[0] = p.astype(BF16)

        @pl.when(pl.program_id(1) == idx_ref[1])
        def _():
            own_ref[...] = p

    return pl.pallas_call(
        body, name=name,
        grid_spec=pltpu.PrefetchScalarGridSpec(
            num_scalar_prefetch=1, grid=(r // tr, 4),
            in_specs=[pl.BlockSpec((1, 1, tr, c), lambda i, j, idx: (j, idx[0], i, 0)),
                      pl.BlockSpec((1, 1, tr, c), lambda i, j, idx: (j, 0, i, 0))],
            out_specs=[pl.BlockSpec((1, tr, c), lambda i, j, idx: (j, i, 0)),
                       pl.BlockSpec((tr, c), lambda i, j, idx: (i, 0))]),
        out_shape=[jax.ShapeDtypeStruct((4, r, c), BF16), jax.ShapeDtypeStruct((r, c), F32)],
        compiler_params=_cparams(32, ("arbitrary", "arbitrary")),
    )(idx, g, got)


def _chip_exchange(sums):
    n = len(sums)

    def body(*refs):
        p_refs, out_refs = refs[:n], refs[n:2 * n]
        send_sems, recv_sems, local_sems = refs[2 * n:]
        x, y, c = _position()
        my_chip = 2 * x + y
        mine = [pltpu.make_async_copy(p_refs[a].at[my_chip], out_refs[a].at[my_chip], local_sems.at[a]) for a in range(n)]
        for cp in mine:
            cp.start()
        chips = [(1 - x, y), (x, 1 - y), (1 - x, 1 - y)]

        def copy(a, j):
            px, py = chips[j]
            return pltpu.make_async_remote_copy(
                src_ref=p_refs[a].at[2 * px + py], dst_ref=out_refs[a].at[my_chip],
                send_sem=send_sems.at[3 * a + j], recv_sem=recv_sems.at[3 * a + j], device_id=(px, py, c),
                device_id_type=MESH_ID)

        def arrival(a, j):
            px, py = chips[j]
            return pltpu.make_async_remote_copy(
                src_ref=p_refs[a].at[my_chip], dst_ref=out_refs[a].at[2 * px + py],
                send_sem=send_sems.at[3 * a + j], recv_sem=recv_sems.at[3 * a + j], device_id=(px, py, c),
                device_id_type=MESH_ID)

        copies = [copy(a, j) for a in range(n) for j in range(3)]
        for cp in copies:
            cp.start()
        for a in range(n):
            for j in range(3):
                arrival(a, j).wait_recv()
        for cp in copies:
            cp.wait_send()
        for cp in mine:
            cp.wait()

    return pl.pallas_call(
        body, name="grad_chip_exchange",
        out_shape=[jax.ShapeDtypeStruct(p.shape, p.dtype) for p in sums],
        in_specs=[ANY] * n, out_specs=[ANY] * n,
        scratch_shapes=[pltpu.SemaphoreType.DMA((3 * n,)), pltpu.SemaphoreType.DMA((3 * n,)), pltpu.SemaphoreType.DMA((n,))],
    )(*sums)


HBM = pl.BlockSpec(memory_space=pltpu.HBM)
SEM = pl.BlockSpec(memory_space=pltpu.SEMAPHORE)
DATAFLOW = pltpu.SideEffectType.DATAFLOW_SIDE_EFFECTING


def _exchange_copies(src_refs, land_refs, send_sems, recv_sems, scatter):
    x, y, c = _position()
    me = 4 * x + 2 * y + c
    copies = []
    for a, (s_ref, l_ref) in enumerate(zip(src_refs, land_refs)):
        for k in range(NDEV - 1):
            px, py, pc = x ^ ((k + 1) >> 2), y ^ (((k + 1) >> 1) & 1), c ^ ((k + 1) & 1)
            copies.append(pltpu.make_async_remote_copy(
                src_ref=s_ref.at[4 * px + 2 * py + pc] if scatter else s_ref, dst_ref=l_ref.at[me],
                send_sem=send_sems.at[7 * a + k], recv_sem=recv_sems.at[7 * a + k],
                device_id=(px, py, pc), device_id_type=MESH_ID))
    return copies


def _exchange_start(srcs, lands, *, scatter, name):
    n = len(srcs)

    def body(*refs):
        token = refs[-1]
        for cp in _exchange_copies(refs[:n], refs[n:2 * n], refs[2 * n], refs[2 * n + 1], scatter):
            cp.start()
        token[...] = jnp.zeros_like(token)

    arrays = list(srcs) + list(lands)
    outs = pl.pallas_call(
        body, name=name,
        out_shape=(pltpu.SemaphoreType.DMA((7 * n,)), pltpu.SemaphoreType.DMA((7 * n,)),
                   *[pltpu.HBM(a.shape, a.dtype) for a in arrays], jax.ShapeDtypeStruct((SUBLANES, LANES), F32)),
        in_specs=[HBM] * (2 * n),
        out_specs=(SEM, SEM, *[HBM] * (2 * n), pl.BlockSpec(memory_space=pltpu.VMEM)),
        input_output_aliases={i: 2 + i for i in range(2 * n)},
        compiler_params=pltpu.CompilerParams(has_side_effects=DATAFLOW),
    )(*[pltpu.with_memory_space_constraint(a, pltpu.HBM) for a in arrays])
    return outs[0], outs[1], outs[2:2 + n], outs[2 + n:2 + 2 * n], outs[-1]


def _exchange_wait(send_sems, recv_sems, srcs, lands, after, *, scatter, name):
    n = len(srcs)

    def body(*refs):
        for cp in _exchange_copies(refs[:n], refs[n:2 * n], refs[2 * n], refs[2 * n + 1], scatter):
            cp.wait_send()
            cp.wait_recv()

    arrays = list(srcs) + list(lands)
    outs = pl.pallas_call(
        body, name=name,
        out_shape=tuple(pltpu.HBM(a.shape, a.dtype) for a in arrays),
        in_specs=[HBM] * (2 * n) + [SEM, SEM, ANY],
        out_specs=tuple([HBM] * (2 * n)),
        input_output_aliases={i: i for i in range(2 * n)},
        compiler_params=pltpu.CompilerParams(has_side_effects=DATAFLOW),
    )(*arrays, send_sems, recv_sems, after)
    return outs[n:]


def _own_slot(value, me):
    return lax.dynamic_update_index_in_dim(lax.empty((NDEV,) + value.shape, value.dtype), value, me, 0)


def _small_all_reduce(parts):
    def body(gmp_ref, gmo_ref, gfp_ref, gfo_ref, ga_ref, gc_ref, dw_ref, bf_ref, loss_ref,
             out_ref, buf, send_sems, recv_sems):
        x, y, c = _position()
        me = 4 * x + 2 * y + c

        def colsum(v):
            return jnp.sum(v, axis=0, keepdims=True)

        loss = jnp.sum(colsum(loss_ref[...]), axis=1, keepdims=True) * (0.5 / D)
        rows = [colsum(gmp_ref[...]), colsum(gmo_ref[...]), colsum(gfp_ref[...]), colsum(gfo_ref[...]),
                jnp.concatenate([colsum(ga_ref[...]), colsum(gc_ref[...])], axis=1),
                jnp.concatenate([colsum(dw_ref[0]), colsum(dw_ref[1])], axis=1),
                jnp.concatenate([colsum(dw_ref[2]), colsum(bf_ref[...]), jnp.broadcast_to(loss, (1, 128)),
                                 jnp.zeros((1, 256), F32)], axis=1),
                jnp.zeros((1, D), F32)]
        buf[me] = jnp.concatenate(rows, axis=0)
        copies = []
        for mm in range(1, NDEV):
            peer = (x ^ (mm >> 2), y ^ ((mm >> 1) & 1), c ^ (mm & 1))
            copies.append(pltpu.make_async_remote_copy(
                src_ref=buf.at[me], dst_ref=buf.at[me], send_sem=send_sems.at[mm - 1], recv_sem=recv_sems.at[mm - 1],
                device_id=peer, device_id_type=MESH_ID))
        for cp in copies:
            cp.start()
        for cp in copies:
            cp.wait_recv()
        for cp in copies:
            cp.wait_send()
        acc = buf[0]
        for d in range(1, NDEV):
            acc = acc + buf[d]
        out_ref[...] = acc

    vm = pl.BlockSpec(memory_space=pltpu.VMEM)
    return pl.pallas_call(
        body, name="small_all_reduce",
        out_shape=jax.ShapeDtypeStruct((SUBLANES, D), F32),
        in_specs=[vm] * len(parts), out_specs=vm,
        scratch_shapes=[pltpu.VMEM((NDEV, SUBLANES, D), F32), pltpu.SemaphoreType.DMA((7,)), pltpu.SemaphoreType.DMA((7,))],
    )(*parts)


def _adam_update(w, g, m, v):
    nm = ADAM_B1 * m + (1.0 - ADAM_B1) * g
    nv = ADAM_B2 * v + (1.0 - ADAM_B2) * (g * g)
    m_hat = nm / (1.0 - ADAM_B1 ** ADAM_STEP)
    v_hat = nv / (1.0 - ADAM_B2 ** ADAM_STEP)
    return -ADAM_LR * (m_hat / (jnp.sqrt(v_hat) + ADAM_EPS) + ADAM_WD * w), nm, nv


def _adamw(w, g, m, v, *, tr, name):
    rows, cols = w.shape

    def body(w_ref, g_ref, m_ref, v_ref, d_ref, nm_ref, nv_ref):
        d_ref[...], nm_ref[...], nv_ref[...] = _adam_update(w_ref[...], g_ref[...], m_ref[...], v_ref[...])

    spec = pl.BlockSpec((tr, cols), lambda i: (i, 0))
    return pl.pallas_call(
        body, name=name, grid=(rows // tr,),
        in_specs=[spec] * 4, out_specs=[spec] * 3,
        out_shape=[jax.ShapeDtypeStruct((rows, cols), F32)] * 3,
        compiler_params=_cparams(32, ("arbitrary",)),
    )(w, g, m, v)


def _chip_sum_adamw(got, own, idx, w, m, v, *, tr, name):
    rows, cols = w.shape

    def body(idx_ref, got_ref, own_ref, w_ref, m_ref, v_ref, g_ref, d_ref, nm_ref, nv_ref):
        g = jnp.zeros((tr, cols), F32)
        for j in range(4):
            g = g + jnp.where(idx_ref[1] == j, own_ref[...], got_ref[j].astype(F32))
        g_ref[...] = g
        d_ref[...], nm_ref[...], nv_ref[...] = _adam_update(w_ref[...], g, m_ref[...], v_ref[...])

    spec = pl.BlockSpec((tr, cols), lambda i, idx: (i, 0))
    return pl.pallas_call(
        body, name=name,
        grid_spec=pltpu.PrefetchScalarGridSpec(
            num_scalar_prefetch=1, grid=(rows // tr,),
            in_specs=[pl.BlockSpec((4, tr, cols), lambda i, idx: (0, i, 0)), spec, spec, spec, spec],
            out_specs=[spec] * 4),
        out_shape=[jax.ShapeDtypeStruct((rows, cols), F32)] * 4,
        compiler_params=_cparams(32, ("arbitrary",)),
    )(idx, got, own, w, m, v)


def _device_sum_adamw(land, w, m, v, *, tr, name):
    rows, cols = w.shape

    def body(land_ref, w_ref, m_ref, v_ref, g_ref, d_ref, nm_ref, nv_ref):
        g = land_ref[0].astype(F32)
        for dev in range(1, NDEV):
            g = g + land_ref[dev].astype(F32)
        g_ref[...] = g
        d_ref[...], nm_ref[...], nv_ref[...] = _adam_update(w_ref[...], g, m_ref[...], v_ref[...])

    spec = pl.BlockSpec((tr, cols), lambda i: (i, 0))
    return pl.pallas_call(
        body, name=name, grid=(rows // tr,),
        in_specs=[pl.BlockSpec((NDEV, tr, cols), lambda i: (0, i, 0)), spec, spec, spec],
        out_specs=[spec] * 4,
        out_shape=[jax.ShapeDtypeStruct((rows, cols), F32)] * 4,
        compiler_params=_cparams(32, ("arbitrary",)),
    )(land, w, m, v)


def _placement_constants():
    j = jnp.arange(128)[:, None]
    lane = jnp.arange(1024)[None, :]
    head, sub = lane // HP, lane % HP
    piece, jh = j // H, j % H
    valid = (j < 3 * H) & (jh == head)
    pq = jnp.where(valid & (sub == DH + piece), 1.0, 0.0).astype(BF16)
    pk = jnp.where(valid & (sub == DH + 3 + piece), -1.0, 0.0).astype(BF16)
    oq = jnp.where((sub >= DH + 3) & (sub < DH + 6), 1.0, 0.0).astype(F32)
    ok = jnp.where((sub >= DH) & (sub < DH + 3), 1.0, 0.0).astype(F32)
    r = jnp.arange(AW)[:, None]
    cc = jnp.arange(128)[None, :]
    sel = jnp.where((r % DH == 3) & (r // DH == cc), -1.0, 0.0).astype(BF16)
    gi = jnp.arange(CW)
    gsum = (gi[:, None] // DH == gi[None, :] // DH).astype(BF16)
    return pq, pk, oq, ok, sel, gsum


def _local_step(xs, tgt, wp, late_weights, cw8, bfp, g_attn_out, g_conv_out,
                g_mix_pre, g_mix_post, g_ffn_pre, g_ffn_post, early_grads=None):
    pq, pk, oq, ok, sel, gsum = _placement_constants()
    h1, qp, kp, vv, bcu, zf = _in_proj(xs, g_mix_pre, wp, bfp, pq, pk, oq, ok, tm=512)
    o, lse = _attn_fwd(qp, kp, vv, t=512)
    w_out_f, wgu, wd = late_weights(lse)
    merged, y, x2, cv, h2 = _mix_out(o, bcu, cw8, g_attn_out, g_conv_out, gsum, w_out_f, xs, g_mix_post, g_ffn_pre, tm=512)
    gate, up, act = _ffn_up(h2, wgu, tm=512)
    dx3, dff, loss_p, dg_ffn_post = _ffn_down_loss(act, wd, x2, tgt, g_ffn_post, tm=512)

    dgu = _ffn_bwd_act(dff, wd, gate, up, tm=512)
    dw_down = _grad_matmul_blocks(act, dff, ts=512, name="grad_w_down")
    dw_gu = _grad_matmul_blocks(h2, dgu.reshape(NDEV, -1, FB), ts=512, name="grad_w_gate_up")
    dx2, dy, dg_ffn_pre, dg_mix_post = _ffn_bwd_in(dgu, wgu, x2, g_ffn_pre, dx3, y, g_mix_post, tm=256)
    dw_out = _grad_matmul(merged, dy, ta=1024, tb=1024, ts=512, name="grad_w_out")
    token = early_grads(dw_out, dw_gu, dw_down) if early_grads is not None else None
    ga = g_attn_out if token is None else g_attn_out + token[0:1, 0:1]
    do, dl, dcv, db, dg_attn, dg_conv = _mix_bwd(dy, w_out_f, o, cv, bcu, ga, g_conv_out, gsum, tm=512)
    dbcu, dtaps = _conv_bwd(dcv, db, bcu, cw8, tm=512)
    dqp, dkp, dv, dkx = _attn_bwd(qp, kp, vv, do, lse, dl, t=512)
    dfl, dbf = _forget_bwd(dkx, zf, sel, tm=512)
    pieces = (dqp, dkp, dv, dbcu, dfl)
    names = ("grad_w_in_q", "grad_w_in_k", "grad_w_in_v", "grad_w_in_bcu", "grad_w_in_f")
    tbs = (1024, 1024, 512, 768, 128)
    dwp = tuple(_grad_matmul(h1, p, ta=1024, tb=tb, ts=512, name=nm) for p, nm, tb in zip(pieces, names, tbs))
    grad_x, dg_mix_pre = _in_proj_bwd(pieces, wp, xs, g_mix_pre, dx2, tm=512)
    return (grad_x, dwp, dw_out, dw_gu, dw_down, dg_mix_pre, dg_mix_post, dg_ffn_pre, dg_ffn_post, dg_attn, dg_conv,
            dtaps, dbf, loss_p)


BIG_TILES = {"w_in": 256, "w_out": 128, "w_gate_up": 256, "w_down": 176}


def kernel(x, w_in, b_forget, conv_w, g_attn_out, g_conv_out, w_out, g_mix_pre, g_mix_post, w_gate_up, w_down, g_ffn_pre, g_ffn_post, loss_target, m_w_in, m_b_forget, m_conv_w, m_g_attn_out, m_g_conv_out, m_w_out, m_g_mix_pre, m_g_mix_post, m_w_gate_up, m_w_down, m_g_ffn_pre, m_g_ffn_post, v_w_in, v_b_forget, v_conv_w, v_g_attn_out, v_g_conv_out, v_w_out, v_g_mix_pre, v_g_mix_post, v_w_gate_up, v_w_down, v_g_ffn_pre, v_g_ffn_post):
    xc, yc, cc = _position()
    my_chip = 2 * xc + yc
    me = 2 * my_chip + cc
    idx = jnp.stack([cc, my_chip]).astype(jnp.int32)
    tables = _in_layout_tables()
    pad_in = lambda a: jnp.pad(a, ((0, 0), (0, IN_PAD - IN_COLS)))

    g_in, g_taps = _all_gather([pad_in(w_in[0]).astype(BF16), conv_w[0]])
    wp = _assemble_w_in(g_in, tables, tr=256)
    cw8 = jnp.pad(g_taps.transpose(1, 0, 2).reshape(3, CW), ((0, SUBLANES - 3), (0, 0)))

    late = [w_out[0].astype(BF16), w_gate_up[0].astype(BF16), w_down[0].astype(BF16)]
    ssem, rsem, late_thru, land_thru, token = _exchange_start(
        late, [_own_slot(s, me) for s in late], scatter=False, name="gather_late_start")
    bfp = jnp.pad(b_forget, ((0, 0), (0, 128 - H))) + token[0:1, :]

    def late_weights(after):
        l_out, l_gu, l_down = _exchange_wait(ssem, rsem, late_thru, land_thru, after, scatter=False, name="gather_late_wait")
        return l_out.reshape(D, D), l_gu.reshape(2, 4, D, FB), l_down.reshape(4, FB, D)

    early = {}

    def early_grads(dw_out, dw_gu, dw_down):
        srcs = [dw_out.reshape(NDEV, D // NDEV, D), dw_gu, dw_down.reshape(NDEV, DFF // NDEV, D)]
        lands = [_own_slot(lax.dynamic_index_in_dim(s, me, 0, keepdims=False), me) for s in srcs]
        early["handles"] = _exchange_start(srcs, lands, scatter=True, name="scatter_early_start")
        return early["handles"][4]

    (grad_x, dwp, dw_out, dw_gu, dw_down, dg_mix_pre, dg_mix_post, dg_ffn_pre, dg_ffn_post, dg_attn, dg_conv,
     dtaps, dbf, loss_p) = _local_step(x[0], loss_target[0], wp, late_weights, cw8, bfp, g_attn_out, g_conv_out,
                                        g_mix_pre, g_mix_post, g_ffn_pre, g_ffn_post, early_grads)
    e_ssem, e_rsem, e_srcs, e_lands, _ = early["handles"]
    land_out, land_gu, land_down = _exchange_wait(e_ssem, e_rsem, e_srcs, e_lands, dg_mix_pre, scatter=True,
                                                  name="scatter_early_wait")

    g_w_in = _disassemble_w_in(dwp, tables, tr=256).reshape(4, 2, D, IN_PAD)
    (from_sibling,) = _pair_exchange([g_w_in])
    pair_b, pair_own = _pair_sum(g_w_in, from_sibling, idx, tr=BIG_TILES["w_in"], name="grad_pair_sum_w_in")
    (from_chips,) = _chip_exchange([pair_b])

    small = _small_all_reduce([dg_mix_pre, dg_mix_post, dg_ffn_pre, dg_ffn_post, dg_attn, dg_conv, dtaps, dbf, loss_p])
    taps_full = jnp.concatenate([small[5:6, :CW], small[5:6, CW:], small[6:7, :CW]], axis=0)
    small_grads = {
        "b_forget": small[6:7, CW:CW + H], "conv_w": lax.dynamic_slice(taps_full, (0, me * 64), (3, 64)),
        "g_attn_out": small[4:5, :AW], "g_conv_out": small[4:5, AW:], "g_mix_pre": small[0:1], "g_mix_post": small[1:2],
        "g_ffn_pre": small[2:3], "g_ffn_post": small[3:4]}
    loss = small[6, CW + 128]

    res = {}
    outs = _chip_sum_adamw(from_chips, pair_own, idx, pad_in(w_in[0]), pad_in(m_w_in[0]), pad_in(v_w_in[0]),
                           tr=BIG_TILES["w_in"], name="adamw_w_in")
    res["w_in"] = [o[:, :IN_COLS][None] for o in outs]
    big = {"w_out": (land_out, w_out[0], m_w_out[0], v_w_out[0]),
           "w_gate_up": (land_gu, w_gate_up[0], m_w_gate_up[0], v_w_gate_up[0]),
           "w_down": (land_down, w_down[0], m_w_down[0], v_w_down[0])}
    for name, (land, w, m, v) in big.items():
        res[name] = [o[None] for o in _device_sum_adamw(land, w, m, v, tr=BIG_TILES[name], name="adamw_" + name)]
    smalls = {"b_forget": (b_forget, m_b_forget, v_b_forget), "conv_w": (conv_w[0], m_conv_w[0], v_conv_w[0]),
              "g_attn_out": (g_attn_out, m_g_attn_out, v_g_attn_out), "g_conv_out": (g_conv_out, m_g_conv_out, v_g_conv_out),
              "g_mix_pre": (g_mix_pre, m_g_mix_pre, v_g_mix_pre), "g_mix_post": (g_mix_post, m_g_mix_post, v_g_mix_post),
              "g_ffn_pre": (g_ffn_pre, m_g_ffn_pre, v_g_ffn_pre), "g_ffn_post": (g_ffn_post, m_g_ffn_post, v_g_ffn_post)}
    for name, (w, m, v) in smalls.items():
        g = small_grads[name]
        outs = [g] + list(_adamw(w, g, m, v, tr=w.shape[0], name="adamw_" + name))
        res[name] = [o[None] for o in outs] if name == "conv_w" else outs

    order = ["w_in", "b_forget", "conv_w", "g_attn_out", "g_conv_out", "w_out", "g_mix_pre", "g_mix_post",
             "w_gate_up", "w_down", "g_ffn_pre", "g_ffn_post"]
    outs = [loss, grad_x[None]]
    for k in range(4):
        outs += [res[n][k] for n in order]
    return tuple(outs)
```

```python
import functools

import numpy as np

import jax
import jax.numpy as jnp
from jax import lax
from jax.experimental import pallas as pl
from jax.experimental.pallas import tpu as pltpu

F32 = jnp.float32
BF16 = jnp.bfloat16
HIGHEST = lax.Precision.HIGHEST
MESH_ID = pl.DeviceIdType.MESH

D = 1024
H = 8
DH = 64
AW = 512
CW = 512
DFF = 2816
FB = DFF // 4
HP = 128
OFF_Q, OFF_K, OFF_V, OFF_BCU, OFF_F = 0, 1024, 2048, 2560, 4096
WP = OFF_F + 128
PIECES = ((OFF_Q, OFF_K), (OFF_K, OFF_V), (OFF_V, OFF_BCU), (OFF_BCU, OFF_F), (OFF_F, WP))
EPS = 1e-6
NDEV = 8
LANES = 128
SUBLANES = 8
IN_COLS = 385
IN_PAD = 512
WIN = 896

ADAM_LR, ADAM_B1, ADAM_B2, ADAM_EPS, ADAM_WD, ADAM_STEP = 0.001, 0.9, 0.999, 1e-08, 0.01, 10

NT = (((1,), (1,)), ((), ()))
TN = (((0,), (0,)), ((), ()))


def _cparams(vmem_mb=None, sem=None):
    kw = {}
    if vmem_mb is not None:
        kw["vmem_limit_bytes"] = vmem_mb << 20
    if sem is not None:
        kw["dimension_semantics"] = sem
    return pltpu.CompilerParams(**kw)


def _full(shape):
    return pl.BlockSpec(shape, lambda *_: (0,) * len(shape))


def _resident(shape):
    return pl.BlockSpec(shape, lambda *_: (0,) * len(shape), pipeline_mode=pl.Buffered(1))


def _rows(tm, width):
    return pl.BlockSpec((tm, width), lambda i: (i, 0))


def _fold8(v):
    r, w = v.shape
    return jnp.sum(v.reshape(r // SUBLANES, SUBLANES, w), axis=0)


def _split_dot(v, m01):
    hi = v.astype(BF16)
    lo = (v - hi.astype(F32)).astype(BF16)
    return (jnp.dot(hi, m01, preferred_element_type=F32)
            + jnp.dot(lo, m01, preferred_element_type=F32))


def _rms_fwd(v, g):
    r = lax.rsqrt(jnp.mean(v * v, axis=-1, keepdims=True) + EPS)
    n = v * r
    return n * g, n, r


def _rms_bwd(do, n, r, g):
    dn = do * g
    return r * (dn - n * jnp.mean(dn * n, axis=-1, keepdims=True)), do * n


def _padded_column(n):
    if n < AW:
        return OFF_Q + HP * (n // DH) + n % DH, 0.125
    if n < 2 * AW:
        m = n - AW
        return OFF_K + HP * (m // DH) + m % DH, 1.0
    if n < 3 * AW:
        return OFF_V + n - 2 * AW, 1.0
    if n < 3 * AW + H:
        return OFF_F + n - 3 * AW, 1.0
    return OFF_BCU + n - 3 * AW - H, 1.0


def _in_layout_tables():
    dest = -np.ones((IN_PAD, LANES), np.int32)
    dest_f = -np.ones((IN_PAD, LANES), np.int32)
    scale = np.zeros((IN_PAD, LANES), np.float32)
    starts = []
    for k in range(NDEV):
        cols = [_padded_column(IN_COLS * k + j) for j in range(IN_COLS)]
        main = [c for c, _ in cols if c < OFF_F]
        ws = min((min(main) // LANES) * LANES, OFF_F - WIN)
        assert ws <= min(main) and max(main) < ws + WIN
        starts.append(ws)
        for j, (c, sc) in enumerate(cols):
            scale[j, k] = sc
            if c < OFF_F:
                dest[j, k] = c - ws
            else:
                dest_f[j, k] = c - OFF_F
    f_shards = tuple(k for k in range(NDEV) if (dest_f[:, k] >= 0).any())
    return tuple(starts), f_shards, jnp.asarray(dest), jnp.asarray(dest_f), jnp.asarray(scale)


def _perm(dest_ref, scale_ref, k, width):
    lane = lax.broadcasted_iota(jnp.int32, (IN_PAD, width), 1)
    return jnp.where(dest_ref[:, k:k + 1] == lane, scale_ref[:, k:k + 1], 0.0).astype(BF16)


def _assemble_w_in(blocks, tables, *, tr):
    starts, f_shards, dest, dest_f, scale = tables

    def body(b_ref, dest_ref, destf_ref, scale_ref, o_ref):
        o_ref[...] = jnp.zeros_like(o_ref)
        for k in range(NDEV):
            b = b_ref[k]
            ws = starts[k]
            part = jnp.dot(b, _perm(dest_ref, scale_ref, k, WIN), preferred_element_type=F32)
            o_ref[:, ws:ws + WIN] += part.astype(BF16)
            if k in f_shards:
                part = jnp.dot(b, _perm(destf_ref, scale_ref, k, 128), preferred_element_type=F32)
                o_ref[:, OFF_F:WP] += part.astype(BF16)

    tab = _full((IN_PAD, LANES))
    return pl.pallas_call(
        body, name="assemble_w_in", grid=(D // tr,),
        in_specs=[pl.BlockSpec((NDEV, tr, IN_PAD), lambda i: (0, i, 0)), tab, tab, tab],
        out_specs=_rows(tr, WP),
        out_shape=jax.ShapeDtypeStruct((D, WP), BF16),
        compiler_params=_cparams(48, ("arbitrary",)),
    )(blocks, dest, dest_f, scale)


def _disassemble_w_in(pieces, tables, *, tr):
    starts, f_shards, dest, dest_f, scale = tables

    def body(q_ref, k_ref, v_ref, bcu_ref, f_ref, dest_ref, destf_ref, scale_ref, o_ref):
        refs = (q_ref, k_ref, v_ref, bcu_ref, f_ref)

        def window(ws):
            parts = []
            for ref, (lo, hi) in zip(refs, PIECES):
                a, b = max(ws, lo), min(ws + WIN, hi)
                if a < b:
                    parts.append(ref[:, a - lo:b - lo])
            return parts[0] if len(parts) == 1 else jnp.concatenate(parts, axis=1)

        for k in range(NDEV):
            acc = lax.dot_general(window(starts[k]), _perm(dest_ref, scale_ref, k, WIN), NT, preferred_element_type=F32)
            if k in f_shards:
                acc = acc + lax.dot_general(f_ref[...], _perm(destf_ref, scale_ref, k, 128), NT, preferred_element_type=F32)
            o_ref[k] = acc.astype(BF16)

    tab = _full((IN_PAD, LANES))
    return pl.pallas_call(
        body, name="disassemble_w_in", grid=(D // tr,),
        in_specs=[_rows(tr, hi - lo) for lo, hi in PIECES] + [tab, tab, tab],
        out_specs=pl.BlockSpec((NDEV, tr, IN_PAD), lambda i: (0, i, 0)),
        out_shape=jax.ShapeDtypeStruct((NDEV, D, IN_PAD), BF16),
        compiler_params=_cparams(48, ("arbitrary",)),
    )(*pieces, dest, dest_f, scale)


def _in_proj(x, g1, wp, bfp, pq, pk, oq, ok, *, tm):
    s = x.shape[0]

    def body(x_ref, g_ref, w_ref, bf_ref, pq_ref, pk_ref, oq_ref, ok_ref,
             h_ref, qp_ref, kp_ref, v_ref, bcu_ref, z_ref, carry):
        @pl.when(pl.program_id(0) == 0)
        def _():
            carry[...] = jnp.zeros_like(carry)

        h = _rms_fwd(x_ref[...], g_ref[...])[0].astype(BF16)
        h_ref[...] = h
        z = jnp.dot(h, w_ref[:, OFF_F:WP], preferred_element_type=F32) + bf_ref[...]
        z_ref[...] = z
        lane = lax.broadcasted_iota(jnp.int32, (tm, 128), 1)
        logf = jnp.where(lane < H, jnp.minimum(z, 0.0) - jnp.log(1.0 + jnp.exp(-jnp.abs(z))), 0.0)
        row = lax.broadcasted_iota(jnp.int32, (tm, tm), 0)
        col = lax.broadcasted_iota(jnp.int32, (tm, tm), 1)
        tri = (col <= row).astype(F32)
        c = jnp.dot(tri, logf, precision=HIGHEST, preferred_element_type=F32) + carry[0:1, :]
        carry[...] = jnp.broadcast_to(c[tm - 1:tm, :], carry.shape)
        c1 = c.astype(BF16).astype(F32)
        r1 = c - c1
        c2 = r1.astype(BF16).astype(F32)
        c3 = (r1 - c2).astype(BF16).astype(F32)
        zc = (c1 + pltpu.roll(c2, 8, axis=1) + pltpu.roll(c3, 16, axis=1)).astype(BF16)
        q = jnp.dot(h, w_ref[:, OFF_Q:OFF_K], preferred_element_type=F32)
        qp_ref[...] = (q + jnp.dot(zc, pq_ref[...], preferred_element_type=F32) + oq_ref[...]).astype(BF16)
        k = jnp.dot(h, w_ref[:, OFF_K:OFF_V], preferred_element_type=F32)
        kp_ref[...] = (k + jnp.dot(zc, pk_ref[...], preferred_element_type=F32) + ok_ref[...]).astype(BF16)
        v_ref[...] = jnp.dot(h, w_ref[:, OFF_V:OFF_BCU], preferred_element_type=F32).astype(BF16)
        bcu_ref[...] = jnp.dot(h, w_ref[:, OFF_BCU:OFF_F], preferred_element_type=F32)

    return pl.pallas_call(
        body, name="in_proj", grid=(s // tm,),
        in_specs=[_rows(tm, D), _full((1, D)), _resident((D, WP)), _full((1, 128)),
                  _full((128, 1024)), _full((128, 1024)), _full((1, 1024)), _full((1, 1024))],
        out_specs=[_rows(tm, D), _rows(tm, 1024), _rows(tm, 1024), _rows(tm, AW), _rows(tm, 3 * CW), _rows(tm, 128)],
        out_shape=[jax.ShapeDtypeStruct((s, D), BF16), jax.ShapeDtypeStruct((s, 1024), BF16),
                   jax.ShapeDtypeStruct((s, 1024), BF16), jax.ShapeDtypeStruct((s, AW), BF16),
                   jax.ShapeDtypeStruct((s, 3 * CW), F32), jax.ShapeDtypeStruct((s, 128), F32)],
        scratch_shapes=[pltpu.VMEM((SUBLANES, 128), F32)],
        compiler_params=_cparams(56, ("arbitrary",)),
    )(x, g1, wp, bfp, pq, pk, oq, ok)


def _attn_fwd(qp, kp, v, *, t):
    s = qp.shape[0]
    nq = s // t

    def body(q_ref, k_ref, v_ref, o_ref, lse_ref, mk_ref):
        qi = pl.program_id(1)
        row = lax.broadcasted_iota(jnp.int32, (t, t), 0)
        col = lax.broadcasted_iota(jnp.int32, (t, t), 1)
        lane = lax.broadcasted_iota(jnp.int32, (t, 128), 1)

        def head_step(hh, ki, carry, masked):
            m, l, acc = carry
            off = pl.multiple_of(ki * t, t)
            q = q_ref[:, HP * hh:HP * (hh + 1)]
            k = k_ref[pl.ds(off, t), HP * hh:HP * (hh + 1)]
            sc = lax.dot_general(q, k, NT, preferred_element_type=F32)
            if masked:
                sc = jnp.where(col <= row, sc, -1e30)
            mn = jnp.maximum(m, jnp.max(sc, axis=-1, keepdims=True))
            p = jnp.exp(sc - mn)
            a = jnp.exp(m - mn)
            l = a * l + jnp.sum(p, axis=-1, keepdims=True)
            acc = a * acc + jnp.dot(p.astype(BF16), v_ref[pl.ds(off, t), :], preferred_element_type=F32)
            return mn, l, acc

        def step(ki, carry, masked):
            new = tuple(head_step(hh, ki, carry[hh], masked) for hh in range(2))
            mk_ref[ki] = jnp.where(lane < DH, jnp.broadcast_to(new[0][0], (t, 128)), jnp.broadcast_to(new[1][0], (t, 128)))
            return new

        init = (jnp.full((t, 1), -1e30, F32), jnp.zeros((t, 1), F32), jnp.zeros((t, 128), F32))
        carry = lax.fori_loop(0, qi, functools.partial(step, masked=False), (init, init))
        (m0, l0, acc0), (m1, l1, acc1) = step(qi, carry, True)
        o_ref[...] = jnp.where(lane < DH, acc0 / l0, acc1 / l1)
        lse_ref[...] = jnp.where(lane < DH, jnp.broadcast_to(m0 + jnp.log(l0), (t, 128)),
                                 jnp.broadcast_to(m1 + jnp.log(l1), (t, 128)))

    return pl.pallas_call(
        body, name="attn_fwd", grid=(H // 2, nq),
        in_specs=[pl.BlockSpec((t, 2 * HP), lambda p, i: (i, p)),
                  pl.BlockSpec((s, 2 * HP), lambda p, i: (0, p)),
                  pl.BlockSpec((s, 128), lambda p, i: (0, p))],
        out_specs=[pl.BlockSpec((t, 128), lambda p, i: (i, p)), pl.BlockSpec((t, 128), lambda p, i: (i, p)),
                   pl.BlockSpec((nq, t, 128), lambda p, i: (0, i, p))],
        out_shape=[jax.ShapeDtypeStruct((s, AW), F32), jax.ShapeDtypeStruct((s, AW), F32),
                   jax.ShapeDtypeStruct((nq, s, AW), F32)],
        compiler_params=_cparams(48, ("arbitrary", "arbitrary")),
    )(qp, kp, v)


def _conv_taps(bcu_ref, halo_ref, first, tm):
    z = bcu_ref[:, CW:2 * CW] * bcu_ref[:, 2 * CW:3 * CW]
    zh = jnp.where(first, 0.0, halo_ref[:, CW:2 * CW] * halo_ref[:, 2 * CW:3 * CW])
    row = lax.broadcasted_iota(jnp.int32, (tm, CW), 0)
    z1 = jnp.where(row == 0, zh[7:8, :], pltpu.roll(z, 1, axis=0))
    z2 = jnp.where(row == 0, zh[6:7, :], jnp.where(row == 1, zh[7:8, :], pltpu.roll(z, 2, axis=0)))
    return z, z1, z2


def _halo_before(tm, width):
    return pl.BlockSpec((SUBLANES, width), lambda i: (jnp.maximum(i * (tm // SUBLANES) - 1, 0), 0))


def _mix_out(o, bcu, cw8, ga, gc, gsum, w_out, x, g_post, g_ffn_pre, *, tm):
    s = x.shape[0]

    def body(o_ref, bcu_ref, halo_ref, cw_ref, ga_ref, gc_ref, gs_ref, w_ref, x_ref, g_ref, gf_ref,
             merged_ref, y_ref, x2_ref, cv_ref, h2_ref):
        z, z1, z2 = _conv_taps(bcu_ref, halo_ref, pl.program_id(0) == 0, tm)
        cv = cw_ref[0:1, :] * z2 + cw_ref[1:2, :] * z1 + cw_ref[2:3, :] * z
        cv_ref[...] = cv
        conv = bcu_ref[:, 0:CW] * cv
        ov = o_ref[...]
        ra = lax.rsqrt(_split_dot(ov * ov, gs_ref[...]) * (1.0 / DH) + EPS)
        rc = lax.rsqrt(_split_dot(conv * conv, gs_ref[...]) * (1.0 / DH) + EPS)
        merged = jnp.concatenate([ov * ra * ga_ref[...], conv * rc * gc_ref[...]], axis=1).astype(BF16)
        merged_ref[...] = merged
        y = jnp.dot(merged, w_ref[...], preferred_element_type=F32)
        y_ref[...] = y
        x2 = x_ref[...] + _rms_fwd(y, g_ref[...])[0]
        x2_ref[...] = x2
        h2_ref[...] = _rms_fwd(x2, gf_ref[...])[0].astype(BF16)

    return pl.pallas_call(
        body, name="mix_out", grid=(s // tm,),
        in_specs=[_rows(tm, AW), _rows(tm, 3 * CW), _halo_before(tm, 3 * CW), _full((SUBLANES, CW)),
                  _full((1, AW)), _full((1, CW)), _full((CW, CW)), _resident((D, D)), _rows(tm, D), _full((1, D)),
                  _full((1, D))],
        out_specs=[_rows(tm, D), _rows(tm, D), _rows(tm, D), _rows(tm, CW), _rows(tm, D)],
        out_shape=[jax.ShapeDtypeStruct((s, D), BF16), jax.ShapeDtypeStruct((s, D), F32),
                   jax.ShapeDtypeStruct((s, D), F32), jax.ShapeDtypeStruct((s, CW), F32),
                   jax.ShapeDtypeStruct((s, D), BF16)],
        compiler_params=_cparams(48, ("arbitrary",)),
    )(o, bcu, bcu, cw8, ga, gc, gsum, w_out, x, g_post, g_ffn_pre)


def _ffn_up(h2, wgu, *, tm):
    s = h2.shape[0]

    def body(h_ref, w_ref, gate_ref, up_ref, a_ref):
        h = h_ref[...]
        gate = jnp.dot(h, w_ref[0, 0], preferred_element_type=F32)
        up = jnp.dot(h, w_ref[1, 0], preferred_element_type=F32)
        gate_ref[0] = gate.astype(BF16)
        up_ref[0] = up.astype(BF16)
        a_ref[0] = (gate * jax.nn.sigmoid(gate) * up).astype(BF16)

    blk = pl.BlockSpec((1, tm, FB), lambda j, i: (j, i, 0))
    return pl.pallas_call(
        body, name="ffn_up", grid=(4, s // tm),
        in_specs=[pl.BlockSpec((tm, D), lambda j, i: (i, 0)),
                  pl.BlockSpec((2, 1, D, FB), lambda j, i: (0, j, 0, 0))],
        out_specs=[blk, blk, blk],
        out_shape=[jax.ShapeDtypeStruct((4, s, FB), BF16)] * 3,
        compiler_params=_cparams(48, ("arbitrary", "arbitrary")),
    )(h2, wgu)


def _ffn_down_loss(a, wd, x2, target, g_post, *, tm):
    s = x2.shape[0]

    def body(a_ref, w_ref, x2_ref, t_ref, g_ref, dx3_ref, dff_ref, loss_ref, dg_ref):
        @pl.when(pl.program_id(0) == 0)
        def _():
            loss_ref[...] = jnp.zeros_like(loss_ref)
            dg_ref[...] = jnp.zeros_like(dg_ref)

        ff = jnp.dot(a_ref[0], w_ref[0], preferred_element_type=F32)
        for j in range(1, 4):
            ff = ff + jnp.dot(a_ref[j], w_ref[j], preferred_element_type=F32)
        out, n, r = _rms_fwd(ff, g_ref[...])
        e = x2_ref[...] + out - t_ref[...]
        loss_ref[...] += _fold8(e * e)
        dx3 = e * (1.0 / D)
        dx3_ref[...] = dx3
        dff, dg = _rms_bwd(dx3, n, r, g_ref[...])
        dff_ref[...] = dff.astype(BF16)
        dg_ref[...] += _fold8(dg)

    return pl.pallas_call(
        body, name="ffn_down_loss", grid=(s // tm,),
        in_specs=[pl.BlockSpec((4, tm, FB), lambda i: (0, i, 0)), _resident((4, FB, D)), _rows(tm, D), _rows(tm, D),
                  _full((1, D))],
        out_specs=[_rows(tm, D), _rows(tm, D), _full((SUBLANES, D)), _full((SUBLANES, D))],
        out_shape=[jax.ShapeDtypeStruct((s, D), F32), jax.ShapeDtypeStruct((s, D), BF16),
                   jax.ShapeDtypeStruct((SUBLANES, D), F32), jax.ShapeDtypeStruct((SUBLANES, D), F32)],
        compiler_params=_cparams(48, ("arbitrary",)),
    )(a, wd, x2, target, g_post)


def _ffn_bwd_act(dff, wd, gate, up, *, tm):
    s = dff.shape[0]

    def body(dff_ref, w_ref, gate_ref, up_ref, dgu_ref):
        da = lax.dot_general(dff_ref[...], w_ref[0], NT, preferred_element_type=F32)
        g = gate_ref[0].astype(F32)
        sg = jax.nn.sigmoid(g)
        dgu_ref[0, 0] = (da * up_ref[0].astype(F32) * (sg * (1.0 + g * (1.0 - sg)))).astype(BF16)
        dgu_ref[1, 0] = (da * (g * sg)).astype(BF16)

    blk = pl.BlockSpec((1, tm, FB), lambda j, i: (j, i, 0))
    return pl.pallas_call(
        body, name="ffn_bwd_act", grid=(4, s // tm),
        in_specs=[pl.BlockSpec((tm, D), lambda j, i: (i, 0)), pl.BlockSpec((1, FB, D), lambda j, i: (j, 0, 0)), blk, blk],
        out_specs=pl.BlockSpec((2, 1, tm, FB), lambda j, i: (0, j, i, 0)),
        out_shape=jax.ShapeDtypeStruct((2, 4, s, FB), BF16),
        compiler_params=_cparams(48, ("arbitrary", "arbitrary")),
    )(dff, wd, gate, up)


def _grad_matmul(a, b, *, ta, tb, ts, name):
    s, ka = a.shape
    nb = b.shape[1]
    nk = s // ts

    def body(a_ref, b_ref, o_ref, acc):
        k = pl.program_id(2)

        @pl.when(k == 0)
        def _():
            acc[...] = jnp.zeros_like(acc)

        acc[...] += lax.dot_general(a_ref[...], b_ref[...], TN, preferred_element_type=F32)

        @pl.when(k == nk - 1)
        def _():
            o_ref[...] = acc[...].astype(BF16)

    return pl.pallas_call(
        body, name=name, grid=(ka // ta, nb // tb, nk),
        in_specs=[pl.BlockSpec((ts, ta), lambda i, j, k: (k, i)), pl.BlockSpec((ts, tb), lambda i, j, k: (k, j))],
        out_specs=pl.BlockSpec((ta, tb), lambda i, j, k: (i, j)),
        out_shape=jax.ShapeDtypeStruct((ka, nb), BF16),
        scratch_shapes=[pltpu.VMEM((ta, tb), F32)],
        compiler_params=_cparams(48, ("arbitrary", "arbitrary", "arbitrary")),
    )(a, b)


def _grad_matmul_blocks(a, b, *, ts, name):
    nblk = a.shape[0] if a.ndim == 3 else b.shape[0]
    s = a.shape[-2]
    ka, nb = a.shape[-1], b.shape[-1]
    nk = s // ts

    def body(a_ref, b_ref, o_ref, acc):
        k = pl.program_id(1)

        @pl.when(k == 0)
        def _():
            acc[...] = jnp.zeros_like(acc)

        av = a_ref[0] if a.ndim == 3 else a_ref[...]
        bv = b_ref[0] if b.ndim == 3 else b_ref[...]
        acc[...] += lax.dot_general(av, bv, TN, preferred_element_type=F32)

        @pl.when(k == nk - 1)
        def _():
            o_ref[0] = acc[...].astype(BF16)

    def spec(arr, width):
        if arr.ndim == 3:
            return pl.BlockSpec((1, ts, width), lambda j, k: (j, k, 0))
        return pl.BlockSpec((ts, width), lambda j, k: (k, 0))

    return pl.pallas_call(
        body, name=name, grid=(nblk, nk),
        in_specs=[spec(a, ka), spec(b, nb)],
        out_specs=pl.BlockSpec((1, ka, nb), lambda j, k: (j, 0, 0)),
        out_shape=jax.ShapeDtypeStruct((nblk, ka, nb), BF16),
        scratch_shapes=[pltpu.VMEM((ka, nb), F32)],
        compiler_params=_cparams(48, ("arbitrary", "arbitrary")),
    )(a, b)


def _ffn_bwd_in(dgu, wgu, x2, g_pre, dx3, y, g_post, *, tm):
    s = x2.shape[0]

    def body(dgu_ref, w_ref, x2_ref, gpre_ref, dx3_ref, y_ref, gpost_ref,
             dx2_ref, dy_ref, dgpre_ref, dgpost_ref):
        @pl.when(pl.program_id(0) == 0)
        def _():
            dgpre_ref[...] = jnp.zeros_like(dgpre_ref)
            dgpost_ref[...] = jnp.zeros_like(dgpost_ref)

        dh2 = None
        for a in range(2):
            for j in range(4):
                part = lax.dot_general(dgu_ref[a, j], w_ref[a, j], NT, preferred_element_type=F32)
                dh2 = part if dh2 is None else dh2 + part
        _, n2, r2 = _rms_fwd(x2_ref[...], gpre_ref[...])
        dxn, dg = _rms_bwd(dh2, n2, r2, gpre_ref[...])
        dgpre_ref[...] += _fold8(dg)
        dx2 = dx3_ref[...] + dxn
        dx2_ref[...] = dx2
        _, ny, ry = _rms_fwd(y_ref[...], gpost_ref[...])
        dy, dg2 = _rms_bwd(dx2, ny, ry, gpost_ref[...])
        dy_ref[...] = dy.astype(BF16)
        dgpost_ref[...] += _fold8(dg2)

    return pl.pallas_call(
        body, name="ffn_bwd_in", grid=(s // tm,),
        in_specs=[pl.BlockSpec((2, 4, tm, FB), lambda i: (0, 0, i, 0)), _resident((2, 4, D, FB)), _rows(tm, D),
                  _full((1, D)), _rows(tm, D), _rows(tm, D), _full((1, D))],
        out_specs=[_rows(tm, D), _rows(tm, D), _full((SUBLANES, D)), _full((SUBLANES, D))],
        out_shape=[jax.ShapeDtypeStruct((s, D), F32), jax.ShapeDtypeStruct((s, D), BF16),
                   jax.ShapeDtypeStruct((SUBLANES, D), F32), jax.ShapeDtypeStruct((SUBLANES, D), F32)],
        compiler_params=_cparams(56, ("arbitrary",)),
    )(dgu, wgu, x2, g_pre, dx3, y, g_post)


def _mix_bwd(dy, w_out, o, cv, bcu, ga, gc, gsum, *, tm):
    s = dy.shape[0]

    def group_norm_bwd(dn_out, v, g, gs):
        r = lax.rsqrt(_split_dot(v * v, gs) * (1.0 / DH) + EPS)
        n = v * r
        dn = dn_out * g
        return r * (dn - n * (_split_dot(dn * n, gs) * (1.0 / DH))), dn_out * n

    def body(dy_ref, w_ref, o_ref, cv_ref, bcu_ref, ga_ref, gc_ref, gs_ref,
             do_ref, dl_ref, dcv_ref, db_ref, dga_ref, dgc_ref):
        @pl.when(pl.program_id(0) == 0)
        def _():
            dga_ref[...] = jnp.zeros_like(dga_ref)
            dgc_ref[...] = jnp.zeros_like(dgc_ref)

        dm = lax.dot_general(dy_ref[...], w_ref[...], NT, preferred_element_type=F32)
        ov = o_ref[...]
        do, dga = group_norm_bwd(dm[:, 0:AW], ov, ga_ref[...], gs_ref[...])
        dob = do.astype(BF16)
        do_ref[...] = dob
        dl_ref[...] = _split_dot(dob.astype(F32) * ov, gs_ref[...])
        dga_ref[...] += _fold8(dga)
        gate_b = bcu_ref[:, 0:CW]
        cv = cv_ref[...]
        dconv, dgc = group_norm_bwd(dm[:, AW:D], gate_b * cv, gc_ref[...], gs_ref[...])
        dgc_ref[...] += _fold8(dgc)
        dcv_ref[...] = dconv * gate_b
        db_ref[...] = (dconv * cv).astype(BF16)

    return pl.pallas_call(
        body, name="mix_bwd", grid=(s // tm,),
        in_specs=[_rows(tm, D), _resident((D, D)), _rows(tm, AW), _rows(tm, CW), _rows(tm, 3 * CW),
                  _full((1, AW)), _full((1, CW)), _full((CW, CW))],
        out_specs=[_rows(tm, AW), _rows(tm, AW), _rows(tm, CW), _rows(tm, CW),
                   _full((SUBLANES, AW)), _full((SUBLANES, CW))],
        out_shape=[jax.ShapeDtypeStruct((s, AW), BF16), jax.ShapeDtypeStruct((s, AW), F32),
                   jax.ShapeDtypeStruct((s, CW), F32), jax.ShapeDtypeStruct((s, CW), BF16),
                   jax.ShapeDtypeStruct((SUBLANES, AW), F32), jax.ShapeDtypeStruct((SUBLANES, CW), F32)],
        compiler_params=_cparams(48, ("arbitrary",)),
    )(dy, w_out, o, cv, bcu, ga, gc, gsum)


def _conv_bwd(dcv, db, bcu, cw8, *, tm):
    s = dcv.shape[0]
    nt = s // tm

    def body(dcv_ref, nxt_ref, db_ref, bcu_ref, halo_ref, cw_ref, dbcu_ref, dw_ref):
        i = pl.program_id(0)

        @pl.when(i == 0)
        def _():
            dw_ref[...] = jnp.zeros_like(dw_ref)

        z, z1, z2 = _conv_taps(bcu_ref, halo_ref, i == 0, tm)
        d = dcv_ref[...]
        dw_ref[0] += _fold8(d * z2)
        dw_ref[1] += _fold8(d * z1)
        dw_ref[2] += _fold8(d * z)
        nx = jnp.where(i == nt - 1, 0.0, nxt_ref[...])
        row = lax.broadcasted_iota(jnp.int32, (tm, CW), 0)
        d1 = jnp.where(row == tm - 1, nx[0:1, :], pltpu.roll(d, tm - 1, axis=0))
        d2 = jnp.where(row == tm - 2, nx[0:1, :], jnp.where(row == tm - 1, nx[1:2, :], pltpu.roll(d, tm - 2, axis=0)))
        dz = cw_ref[2:3, :] * d + cw_ref[1:2, :] * d1 + cw_ref[0:1, :] * d2
        dbcu_ref[:, 0:CW] = db_ref[...]
        dbcu_ref[:, CW:2 * CW] = (dz * bcu_ref[:, 2 * CW:3 * CW]).astype(BF16)
        dbcu_ref[:, 2 * CW:3 * CW] = (dz * bcu_ref[:, CW:2 * CW]).astype(BF16)

    return pl.pallas_call(
        body, name="conv_bwd", grid=(nt,),
        in_specs=[_rows(tm, CW),
                  pl.BlockSpec((SUBLANES, CW), lambda i: (jnp.minimum((i + 1) * (tm // SUBLANES), s // SUBLANES - 1), 0)),
                  _rows(tm, CW), _rows(tm, 3 * CW), _halo_before(tm, 3 * CW), _full((SUBLANES, CW))],
        out_specs=[_rows(tm, 3 * CW), _full((3, SUBLANES, CW))],
        out_shape=[jax.ShapeDtypeStruct((s, 3 * CW), BF16), jax.ShapeDtypeStruct((3, SUBLANES, CW), F32)],
        compiler_params=_cparams(48, ("arbitrary",)),
    )(dcv, dcv, db, bcu, bcu, cw8)


def _attn_bwd(qp, kp, v, do, lse, dl, mk, *, t):
    s = qp.shape[0]
    nq = s // t

    def body(q_ref, k_ref, v_ref, do_ref, lse_ref, dl_ref, mk_ref, dq_ref, dk_ref, dv_ref, dkx_ref, dq_acc):
        ki = pl.program_id(1)

        @pl.when(ki == 0)
        def _():
            dq_acc[...] = jnp.zeros_like(dq_acc)

        row = lax.broadcasted_iota(jnp.int32, (t, t), 0)
        col = lax.broadcasted_iota(jnp.int32, (t, t), 1)
        lane = lax.broadcasted_iota(jnp.int32, (t, 128), 1)

        def head_step(hh, qi, carry, masked):
            dk, dv, cs = carry
            off = pl.multiple_of(qi * t, t)
            rows = pl.ds(off, t)
            kh = k_ref[:, HP * hh:HP * (hh + 1)]
            q = q_ref[rows, HP * hh:HP * (hh + 1)]
            in_head = (lane >= DH * hh) & (lane < DH * (hh + 1))
            m_col = mk_ref[0, rows, DH * hh:DH * hh + 1]
            scale = jnp.exp(m_col - lse_ref[rows, DH * hh:DH * hh + 1])
            dom = jnp.where(in_head, do_ref[rows, :], jnp.zeros((), BF16))
            sc = lax.dot_general(q, kh, NT, preferred_element_type=F32) - m_col
            if masked:
                sc = jnp.where(col <= row, sc, -1e30)
            pt = jnp.exp(sc).astype(BF16)
            dp = lax.dot_general(dom, v_ref[...], NT, preferred_element_type=F32)
            ds32 = (pt.astype(F32) * scale) * (dp - dl_ref[rows, DH * hh:DH * hh + 1])
            ds = ds32.astype(BF16)
            cs = cs + _fold8(ds32)
            dv = dv + lax.dot_general(pt, (dom.astype(F32) * scale).astype(BF16), TN, preferred_element_type=F32)
            dk = dk + lax.dot_general(ds, q, TN, preferred_element_type=F32)
            dq_acc[rows, HP * hh:HP * (hh + 1)] += jnp.dot(ds, kh, preferred_element_type=F32)
            return dk, dv, cs

        def step(qi, carry, masked):
            return tuple(head_step(hh, qi, carry[hh], masked) for hh in range(2))

        zero = (jnp.zeros((t, HP), F32), jnp.zeros((t, 128), F32), jnp.zeros((SUBLANES, t), F32))
        carry = step(ki, (zero, zero), True)
        (dk0, dv0, cs0), (dk1, dv1, cs1) = lax.fori_loop(ki + 1, nq, functools.partial(step, masked=False), carry)
        dk_ref[:, 0:HP] = dk0.astype(BF16)
        dk_ref[:, HP:2 * HP] = dk1.astype(BF16)
        dv_ref[...] = (dv0 + dv1).astype(BF16)

        def as_column(cs):
            return lax.dot_general(cs, jnp.ones((SUBLANES, 128), F32), TN, precision=HIGHEST, preferred_element_type=F32)

        dkx_ref[...] = jnp.where(lane < DH, as_column(cs0), as_column(cs1))

        @pl.when(ki == nq - 1)
        def _():
            dq_ref[...] = dq_acc[...].astype(BF16)

    return pl.pallas_call(
        body, name="attn_bwd", grid=(H // 2, nq),
        in_specs=[pl.BlockSpec((s, 2 * HP), lambda p, i: (0, p)),
                  pl.BlockSpec((t, 2 * HP), lambda p, i: (i, p)),
                  pl.BlockSpec((t, 128), lambda p, i: (i, p)),
                  pl.BlockSpec((s, 128), lambda p, i: (0, p)),
                  pl.BlockSpec((s, 128), lambda p, i: (0, p)),
                  pl.BlockSpec((s, 128), lambda p, i: (0, p)),
                  pl.BlockSpec((1, s, 128), lambda p, i: (i, 0, p))],
        out_specs=[pl.BlockSpec((s, 2 * HP), lambda p, i: (0, p)),
                   pl.BlockSpec((t, 2 * HP), lambda p, i: (i, p)),
                   pl.BlockSpec((t, 128), lambda p, i: (i, p)),
                   pl.BlockSpec((t, 128), lambda p, i: (i, p))],
        out_shape=[jax.ShapeDtypeStruct((s, 1024), BF16), jax.ShapeDtypeStruct((s, 1024), BF16),
                   jax.ShapeDtypeStruct((s, AW), BF16), jax.ShapeDtypeStruct((s, AW), F32)],
        scratch_shapes=[pltpu.VMEM((s, 2 * HP), F32)],
        compiler_params=_cparams(56, ("arbitrary", "arbitrary")),
    )(qp, kp, v, do, lse, dl, mk)


def _forget_bwd(dkx, z, sel, *, tm):
    s = dkx.shape[0]
    nt = s // tm

    def body(dk_ref, z_ref, sel_ref, dfl_ref, dbf_ref, carry):
        @pl.when(pl.program_id(0) == 0)
        def _():
            carry[...] = jnp.zeros_like(carry)
            dbf_ref[...] = jnp.zeros_like(dbf_ref)

        dc = _split_dot(dk_ref[...], sel_ref[...])
        row = lax.broadcasted_iota(jnp.int32, (tm, tm), 0)
        col = lax.broadcasted_iota(jnp.int32, (tm, tm), 1)
        tri = (col >= row).astype(F32)
        dlogf = jnp.dot(tri, dc, precision=HIGHEST, preferred_element_type=F32) + carry[0:1, :]
        carry[...] = jnp.broadcast_to(dlogf[0:1, :], carry.shape)
        dz = dlogf * (1.0 - jax.nn.sigmoid(z_ref[...]))
        dfl_ref[...] = dz.astype(BF16)
        dbf_ref[...] += _fold8(dz)

    rev = lambda i: (nt - 1 - i, 0)
    return pl.pallas_call(
        body, name="forget_bwd", grid=(nt,),
        in_specs=[pl.BlockSpec((tm, AW), rev), pl.BlockSpec((tm, 128), rev), _full((AW, 128))],
        out_specs=[pl.BlockSpec((tm, 128), rev), _full((SUBLANES, 128))],
        out_shape=[jax.ShapeDtypeStruct((s, 128), BF16), jax.ShapeDtypeStruct((SUBLANES, 128), F32)],
        scratch_shapes=[pltpu.VMEM((SUBLANES, 128), F32)],
        compiler_params=_cparams(48, ("arbitrary",)),
    )(dkx, z, sel)


def _in_proj_bwd(pieces, wp, x, g1, dx2, *, tm):
    s = x.shape[0]

    def body(q_ref, k_ref, v_ref, bcu_ref, f_ref, w_ref, x_ref, g_ref, dx2_ref, dx_ref, dg_ref):
        @pl.when(pl.program_id(0) == 0)
        def _():
            dg_ref[...] = jnp.zeros_like(dg_ref)

        dh = None
        for ref, (lo, hi) in zip((q_ref, k_ref, v_ref, bcu_ref, f_ref), PIECES):
            part = lax.dot_general(ref[...], w_ref[:, lo:hi], NT, preferred_element_type=F32)
            dh = part if dh is None else dh + part
        _, n, r = _rms_fwd(x_ref[...], g_ref[...])
        dxn, dg = _rms_bwd(dh, n, r, g_ref[...])
        dx_ref[...] = dx2_ref[...] + dxn
        dg_ref[...] += _fold8(dg)

    return pl.pallas_call(
        body, name="in_proj_bwd", grid=(s // tm,),
        in_specs=[_rows(tm, hi - lo) for lo, hi in PIECES] + [_resident((D, WP)), _rows(tm, D), _full((1, D)), _rows(tm, D)],
        out_specs=[_rows(tm, D), _full((SUBLANES, D))],
        out_shape=[jax.ShapeDtypeStruct((s, D), F32), jax.ShapeDtypeStruct((SUBLANES, D), F32)],
        compiler_params=_cparams(56, ("arbitrary",)),
    )(*pieces, wp, x, g1, dx2)


def _position():
    return lax.axis_index("x"), lax.axis_index("y"), lax.axis_index("c")


ANY = pl.BlockSpec(memory_space=pl.ANY)


def _all_gather(shards):
    n = len(shards)

    def body(*refs):
        x_refs, out_refs = refs[:n], refs[n:2 * n]
        send_sems, recv_sems, local_sems = refs[2 * n:]
        x, y, c = _position()
        me, sibling = (x, y, c), (x, y, 1 - c)
        chips = [(1 - x, y), (x, 1 - y), (1 - x, 1 - y)]

        def copy(a, k, block, to, own=False):
            slot = out_refs[a].at[4 * block[0] + 2 * block[1] + block[2]]
            return pltpu.make_async_remote_copy(
                src_ref=x_refs[a] if own else slot, dst_ref=slot,
                send_sem=send_sems.at[7 * a + k], recv_sem=recv_sems.at[7 * a + k], device_id=to, device_id_type=MESH_ID)

        mine = [pltpu.make_async_copy(x_refs[a], out_refs[a].at[4 * x + 2 * y + c], local_sems.at[a]) for a in range(n)]
        for cp in mine:
            cp.start()
        first = []
        for a in range(n):
            first.append(copy(a, 0, me, sibling, own=True))
            first += [copy(a, 1 + j, me, (*chip, c), own=True) for j, chip in enumerate(chips)]
        for cp in first:
            cp.start()
        passed = []
        for j, chip in enumerate(chips):
            for a in range(n):
                copy(a, 1 + j, (*chip, c), me).wait_recv()
                fwd = copy(a, 4 + j, (*chip, c), sibling)
                fwd.start()
                passed.append(fwd)
        for a in range(n):
            copy(a, 0, sibling, me).wait_recv()
            for j, chip in enumerate(chips):
                copy(a, 4 + j, (*chip, 1 - c), me).wait_recv()
        for cp in first + passed:
            cp.wait_send()
        for cp in mine:
            cp.wait()

    return pl.pallas_call(
        body, name="all_gather_weights",
        out_shape=[jax.ShapeDtypeStruct((NDEV,) + sh.shape, sh.dtype) for sh in shards],
        in_specs=[ANY] * n, out_specs=[ANY] * n,
        scratch_shapes=[pltpu.SemaphoreType.DMA((7 * n,)), pltpu.SemaphoreType.DMA((7 * n,)), pltpu.SemaphoreType.DMA((n,))],
    )(*shards)


def _pair_exchange(grads):
    n = len(grads)

    def body(*refs):
        g_refs, out_refs = refs[:n], refs[n:2 * n]
        send_sems, recv_sems = refs[2 * n:]
        x, y, c = _position()
        copies = [pltpu.make_async_remote_copy(
            src_ref=g_refs[a].at[:, pl.ds(1 - c, 1)], dst_ref=out_refs[a], send_sem=send_sems.at[a],
            recv_sem=recv_sems.at[a], device_id=(x, y, 1 - c), device_id_type=MESH_ID) for a in range(n)]
        for cp in copies:
            cp.start()
        for cp in copies:
            cp.wait()

    return pl.pallas_call(
        body, name="grad_pair_exchange",
        out_shape=[jax.ShapeDtypeStruct((4, 1) + g.shape[2:], g.dtype) for g in grads],
        in_specs=[ANY] * n, out_specs=[ANY] * n,
        scratch_shapes=[pltpu.SemaphoreType.DMA((n,)), pltpu.SemaphoreType.DMA((n,))],
    )(*grads)


def _pair_sum(g, got, idx, *, tr, name):
    r, c = g.shape[2:]

    def body(idx_ref, g_ref, got_ref, pb_ref, own_ref):
        p = g_ref[0, 0].astype(F32) + got_ref[0, 0].astype(F32)
        pb_ref[0] = p.astype(BF16)

        @pl.when(pl.program_id(1) == idx_ref[1])
        def _():
            own_ref[...] = p

    return pl.pallas_call(
        body, name=name,
        grid_spec=pltpu.PrefetchScalarGridSpec(
            num_scalar_prefetch=1, grid=(r // tr, 4),
            in_specs=[pl.BlockSpec((1, 1, tr, c), lambda i, j, idx: (j, idx[0], i, 0)),
                      pl.BlockSpec((1, 1, tr, c), lambda i, j, idx: (j, 0, i, 0))],
            out_specs=[pl.BlockSpec((1, tr, c), lambda i, j, idx: (j, i, 0)),
                       pl.BlockSpec((tr, c), lambda i, j, idx: (i, 0))]),
        out_shape=[jax.ShapeDtypeStruct((4, r, c), BF16), jax.ShapeDtypeStruct((r, c), F32)],
        compiler_params=_cparams(32, ("arbitrary", "arbitrary")),
    )(idx, g, got)


def _chip_exchange(sums):
    n = len(sums)

    def body(*refs):
        p_refs, out_refs = refs[:n], refs[n:2 * n]
        send_sems, recv_sems, local_sems = refs[2 * n:]
        x, y, c = _position()
        my_chip = 2 * x + y
        mine = [pltpu.make_async_copy(p_refs[a].at[my_chip], out_refs[a].at[my_chip], local_sems.at[a]) for a in range(n)]
        for cp in mine:
            cp.start()
        chips = [(1 - x, y), (x, 1 - y), (1 - x, 1 - y)]

        def copy(a, j):
            px, py = chips[j]
            return pltpu.make_async_remote_copy(
                src_ref=p_refs[a].at[2 * px + py], dst_ref=out_refs[a].at[my_chip],
                send_sem=send_sems.at[3 * a + j], recv_sem=recv_sems.at[3 * a + j], device_id=(px, py, c),
                device_id_type=MESH_ID)

        def arrival(a, j):
            px, py = chips[j]
            return pltpu.make_async_remote_copy(
                src_ref=p_refs[a].at[my_chip], dst_ref=out_refs[a].at[2 * px + py],
                send_sem=send_sems.at[3 * a + j], recv_sem=recv_sems.at[3 * a + j], device_id=(px, py, c),
                device_id_type=MESH_ID)

        copies = [copy(a, j) for a in range(n) for j in range(3)]
        for cp in copies:
            cp.start()
        for a in range(n):
            for j in range(3):
                arrival(a, j).wait_recv()
        for cp in copies:
            cp.wait_send()
        for cp in mine:
            cp.wait()

    return pl.pallas_call(
        body, name="grad_chip_exchange",
        out_shape=[jax.ShapeDtypeStruct(p.shape, p.dtype) for p in sums],
        in_specs=[ANY] * n, out_specs=[ANY] * n,
        scratch_shapes=[pltpu.SemaphoreType.DMA((3 * n,)), pltpu.SemaphoreType.DMA((3 * n,)), pltpu.SemaphoreType.DMA((n,))],
    )(*sums)


HBM = pl.BlockSpec(memory_space=pltpu.HBM)
SEM = pl.BlockSpec(memory_space=pltpu.SEMAPHORE)
DATAFLOW = pltpu.SideEffectType.DATAFLOW_SIDE_EFFECTING


def _exchange_copies(src_refs, land_refs, send_sems, recv_sems, scatter):
    x, y, c = _position()
    me = 4 * x + 2 * y + c
    copies = []
    for a, (s_ref, l_ref) in enumerate(zip(src_refs, land_refs)):
        for k in range(NDEV - 1):
            px, py, pc = x ^ ((k + 1) >> 2), y ^ (((k + 1) >> 1) & 1), c ^ ((k + 1) & 1)
            copies.append(pltpu.make_async_remote_copy(
                src_ref=s_ref.at[4 * px + 2 * py + pc] if scatter else s_ref, dst_ref=l_ref.at[me],
                send_sem=send_sems.at[7 * a + k], recv_sem=recv_sems.at[7 * a + k],
                device_id=(px, py, pc), device_id_type=MESH_ID))
    return copies


def _exchange_start(srcs, lands, *, scatter, name):
    n = len(srcs)

    def body(*refs):
        token = refs[-1]
        for cp in _exchange_copies(refs[:n], refs[n:2 * n], refs[2 * n], refs[2 * n + 1], scatter):
            cp.start()
        token[...] = jnp.zeros_like(token)

    arrays = list(srcs) + list(lands)
    outs = pl.pallas_call(
        body, name=name,
        out_shape=(pltpu.SemaphoreType.DMA((7 * n,)), pltpu.SemaphoreType.DMA((7 * n,)),
                   *[pltpu.HBM(a.shape, a.dtype) for a in arrays], jax.ShapeDtypeStruct((SUBLANES, LANES), F32)),
        in_specs=[HBM] * (2 * n),
        out_specs=(SEM, SEM, *[HBM] * (2 * n), pl.BlockSpec(memory_space=pltpu.VMEM)),
        input_output_aliases={i: 2 + i for i in range(2 * n)},
        compiler_params=pltpu.CompilerParams(has_side_effects=DATAFLOW),
    )(*[pltpu.with_memory_space_constraint(a, pltpu.HBM) for a in arrays])
    return outs[0], outs[1], outs[2:2 + n], outs[2 + n:2 + 2 * n], outs[-1]


def _exchange_wait(send_sems, recv_sems, srcs, lands, after, *, scatter, name):
    n = len(srcs)

    def body(*refs):
        for cp in _exchange_copies(refs[:n], refs[n:2 * n], refs[2 * n], refs[2 * n + 1], scatter):
            cp.wait_send()
            cp.wait_recv()

    arrays = list(srcs) + list(lands)
    outs = pl.pallas_call(
        body, name=name,
        out_shape=tuple(pltpu.HBM(a.shape, a.dtype) for a in arrays),
        in_specs=[HBM] * (2 * n) + [SEM, SEM, ANY],
        out_specs=tuple([HBM] * (2 * n)),
        input_output_aliases={i: i for i in range(2 * n)},
        compiler_params=pltpu.CompilerParams(has_side_effects=DATAFLOW),
    )(*arrays, send_sems, recv_sems, after)
    return outs[n:]


def _own_slot(value, me):
    return lax.dynamic_update_index_in_dim(lax.empty((NDEV,) + value.shape, value.dtype), value, me, 0)


def _small_all_reduce(parts):
    def body(gmp_ref, gmo_ref, gfp_ref, gfo_ref, ga_ref, gc_ref, dw_ref, bf_ref, loss_ref,
             out_ref, buf, send_sems, recv_sems):
        x, y, c = _position()
        me = 4 * x + 2 * y + c

        def colsum(v):
            return jnp.sum(v, axis=0, keepdims=True)

        loss = jnp.sum(colsum(loss_ref[...]), axis=1, keepdims=True) * (0.5 / D)
        rows = [colsum(gmp_ref[...]), colsum(gmo_ref[...]), colsum(gfp_ref[...]), colsum(gfo_ref[...]),
                jnp.concatenate([colsum(ga_ref[...]), colsum(gc_ref[...])], axis=1),
                jnp.concatenate([colsum(dw_ref[0]), colsum(dw_ref[1])], axis=1),
                jnp.concatenate([colsum(dw_ref[2]), colsum(bf_ref[...]), jnp.broadcast_to(loss, (1, 128)),
                                 jnp.zeros((1, 256), F32)], axis=1),
                jnp.zeros((1, D), F32)]
        buf[me] = jnp.concatenate(rows, axis=0)
        copies = []
        for mm in range(1, NDEV):
            peer = (x ^ (mm >> 2), y ^ ((mm >> 1) & 1), c ^ (mm & 1))
            copies.append(pltpu.make_async_remote_copy(
                src_ref=buf.at[me], dst_ref=buf.at[me], send_sem=send_sems.at[mm - 1], recv_sem=recv_sems.at[mm - 1],
                device_id=peer, device_id_type=MESH_ID))
        for cp in copies:
            cp.start()
        for cp in copies:
            cp.wait_recv()
        for cp in copies:
            cp.wait_send()
        acc = buf[0]
        for d in range(1, NDEV):
            acc = acc + buf[d]
        out_ref[...] = acc

    vm = pl.BlockSpec(memory_space=pltpu.VMEM)
    return pl.pallas_call(
        body, name="small_all_reduce",
        out_shape=jax.ShapeDtypeStruct((SUBLANES, D), F32),
        in_specs=[vm] * len(parts), out_specs=vm,
        scratch_shapes=[pltpu.VMEM((NDEV, SUBLANES, D), F32), pltpu.SemaphoreType.DMA((7,)), pltpu.SemaphoreType.DMA((7,))],
    )(*parts)


def _adam_update(w, g, m, v):
    nm = ADAM_B1 * m + (1.0 - ADAM_B1) * g
    nv = ADAM_B2 * v + (1.0 - ADAM_B2) * (g * g)
    m_hat = nm / (1.0 - ADAM_B1 ** ADAM_STEP)
    v_hat = nv / (1.0 - ADAM_B2 ** ADAM_STEP)
    return -ADAM_LR * (m_hat / (jnp.sqrt(v_hat) + ADAM_EPS) + ADAM_WD * w), nm, nv


def _adamw(w, g, m, v, *, tr, name):
    rows, cols = w.shape

    def body(w_ref, g_ref, m_ref, v_ref, d_ref, nm_ref, nv_ref):
        d_ref[...], nm_ref[...], nv_ref[...] = _adam_update(w_ref[...], g_ref[...], m_ref[...], v_ref[...])

    spec = pl.BlockSpec((tr, cols), lambda i: (i, 0))
    return pl.pallas_call(
        body, name=name, grid=(rows // tr,),
        in_specs=[spec] * 4, out_specs=[spec] * 3,
        out_shape=[jax.ShapeDtypeStruct((rows, cols), F32)] * 3,
        compiler_params=_cparams(32, ("arbitrary",)),
    )(w, g, m, v)


def _chip_sum_adamw(got, own, idx, w, m, v, *, tr, name):
    rows, cols = w.shape

    def body(idx_ref, got_ref, own_ref, w_ref, m_ref, v_ref, g_ref, d_ref, nm_ref, nv_ref):
        g = jnp.zeros((tr, cols), F32)
        for j in range(4):
            g = g + jnp.where(idx_ref[1] == j, own_ref[...], got_ref[j].astype(F32))
        g_ref[...] = g
        d_ref[...], nm_ref[...], nv_ref[...] = _adam_update(w_ref[...], g, m_ref[...], v_ref[...])

    spec = pl.BlockSpec((tr, cols), lambda i, idx: (i, 0))
    return pl.pallas_call(
        body, name=name,
        grid_spec=pltpu.PrefetchScalarGridSpec(
            num_scalar_prefetch=1, grid=(rows // tr,),
            in_specs=[pl.BlockSpec((4, tr, cols), lambda i, idx: (0, i, 0)), spec, spec, spec, spec],
            out_specs=[spec] * 4),
        out_shape=[jax.ShapeDtypeStruct((rows, cols), F32)] * 4,
        compiler_params=_cparams(32, ("arbitrary",)),
    )(idx, got, own, w, m, v)


def _device_sum_adamw(land, w, m, v, *, tr, name):
    rows, cols = w.shape

    def body(land_ref, w_ref, m_ref, v_ref, g_ref, d_ref, nm_ref, nv_ref):
        g = land_ref[0].astype(F32)
        for dev in range(1, NDEV):
            g = g + land_ref[dev].astype(F32)
        g_ref[...] = g
        d_ref[...], nm_ref[...], nv_ref[...] = _adam_update(w_ref[...], g, m_ref[...], v_ref[...])

    spec = pl.BlockSpec((tr, cols), lambda i: (i, 0))
    return pl.pallas_call(
        body, name=name, grid=(rows // tr,),
        in_specs=[pl.BlockSpec((NDEV, tr, cols), lambda i: (0, i, 0)), spec, spec, spec],
        out_specs=[spec] * 4,
        out_shape=[jax.ShapeDtypeStruct((rows, cols), F32)] * 4,
        compiler_params=_cparams(32, ("arbitrary",)),
    )(land, w, m, v)


def _placement_constants():
    j = jnp.arange(128)[:, None]
    lane = jnp.arange(1024)[None, :]
    head, sub = lane // HP, lane % HP
    piece, jh = j // H, j % H
    valid = (j < 3 * H) & (jh == head)
    pq = jnp.where(valid & (sub == DH + piece), 1.0, 0.0).astype(BF16)
    pk = jnp.where(valid & (sub == DH + 3 + piece), -1.0, 0.0).astype(BF16)
    oq = jnp.where((sub >= DH + 3) & (sub < DH + 6), 1.0, 0.0).astype(F32)
    ok = jnp.where((sub >= DH) & (sub < DH + 3), 1.0, 0.0).astype(F32)
    r = jnp.arange(AW)[:, None]
    cc = jnp.arange(128)[None, :]
    sel = jnp.where((r % DH == 3) & (r // DH == cc), -1.0, 0.0).astype(BF16)
    gi = jnp.arange(CW)
    gsum = (gi[:, None] // DH == gi[None, :] // DH).astype(BF16)
    return pq, pk, oq, ok, sel, gsum


def _local_step(xs, tgt, wp, late_weights, cw8, bfp, g_attn_out, g_conv_out,
                g_mix_pre, g_mix_post, g_ffn_pre, g_ffn_post, early_grads=None):
    pq, pk, oq, ok, sel, gsum = _placement_constants()
    h1, qp, kp, vv, bcu, zf = _in_proj(xs, g_mix_pre, wp, bfp, pq, pk, oq, ok, tm=512)
    o, lse, mk = _attn_fwd(qp, kp, vv, t=512)
    w_out_f, wgu, wd = late_weights(lse)
    merged, y, x2, cv, h2 = _mix_out(o, bcu, cw8, g_attn_out, g_conv_out, gsum, w_out_f, xs, g_mix_post, g_ffn_pre, tm=512)
    gate, up, act = _ffn_up(h2, wgu, tm=512)
    dx3, dff, loss_p, dg_ffn_post = _ffn_down_loss(act, wd, x2, tgt, g_ffn_post, tm=512)

    dgu = _ffn_bwd_act(dff, wd, gate, up, tm=512)
    dw_down = _grad_matmul_blocks(act, dff, ts=512, name="grad_w_down")
    dw_gu = _grad_matmul_blocks(h2, dgu.reshape(NDEV, -1, FB), ts=512, name="grad_w_gate_up")
    dx2, dy, dg_ffn_pre, dg_mix_post = _ffn_bwd_in(dgu, wgu, x2, g_ffn_pre, dx3, y, g_mix_post, tm=256)
    dw_out = _grad_matmul(merged, dy, ta=1024, tb=1024, ts=512, name="grad_w_out")
    token = early_grads(dw_out, dw_gu, dw_down) if early_grads is not None else None
    ga = g_attn_out if token is None else g_attn_out + token[0:1, 0:1]
    do, dl, dcv, db, dg_attn, dg_conv = _mix_bwd(dy, w_out_f, o, cv, bcu, ga, g_conv_out, gsum, tm=512)
    dbcu, dtaps = _conv_bwd(dcv, db, bcu, cw8, tm=512)
    dqp, dkp, dv, dkx = _attn_bwd(qp, kp, vv, do, lse, dl, mk, t=512)
    dfl, dbf = _forget_bwd(dkx, zf, sel, tm=512)
    pieces = (dqp, dkp, dv, dbcu, dfl)
    names = ("grad_w_in_q", "grad_w_in_k", "grad_w_in_v", "grad_w_in_bcu", "grad_w_in_f")
    tbs = (1024, 1024, 512, 768, 128)
    dwp = tuple(_grad_matmul(h1, p, ta=1024, tb=tb, ts=512, name=nm) for p, nm, tb in zip(pieces, names, tbs))
    grad_x, dg_mix_pre = _in_proj_bwd(pieces, wp, xs, g_mix_pre, dx2, tm=512)
    return (grad_x, dwp, dw_out, dw_gu, dw_down, dg_mix_pre, dg_mix_post, dg_ffn_pre, dg_ffn_post, dg_attn, dg_conv,
            dtaps, dbf, loss_p)


BIG_TILES = {"w_in": 256, "w_out": 128, "w_gate_up": 256, "w_down": 176}


def kernel(x, w_in, b_forget, conv_w, g_attn_out, g_conv_out, w_out, g_mix_pre, g_mix_post, w_gate_up, w_down, g_ffn_pre, g_ffn_post, loss_target, m_w_in, m_b_forget, m_conv_w, m_g_attn_out, m_g_conv_out, m_w_out, m_g_mix_pre, m_g_mix_post, m_w_gate_up, m_w_down, m_g_ffn_pre, m_g_ffn_post, v_w_in, v_b_forget, v_conv_w, v_g_attn_out, v_g_conv_out, v_w_out, v_g_mix_pre, v_g_mix_post, v_w_gate_up, v_w_down, v_g_ffn_pre, v_g_ffn_post):
    xc, yc, cc = _position()
    my_chip = 2 * xc + yc
    me = 2 * my_chip + cc
    idx = jnp.stack([cc, my_chip]).astype(jnp.int32)
    tables = _in_layout_tables()
    pad_in = lambda a: jnp.pad(a, ((0, 0), (0, IN_PAD - IN_COLS)))

    g_in, g_taps = _all_gather([pad_in(w_in[0]).astype(BF16), conv_w[0]])
    wp = _assemble_w_in(g_in, tables, tr=256)
    cw8 = jnp.pad(g_taps.transpose(1, 0, 2).reshape(3, CW), ((0, SUBLANES - 3), (0, 0)))

    late = [w_out[0].astype(BF16), w_gate_up[0].astype(BF16), w_down[0].astype(BF16)]
    ssem, rsem, late_thru, land_thru, token = _exchange_start(
        late, [_own_slot(s, me) for s in late], scatter=False, name="gather_late_start")
    bfp = jnp.pad(b_forget, ((0, 0), (0, 128 - H))) + token[0:1, :]

    def late_weights(after):
        l_out, l_gu, l_down = _exchange_wait(ssem, rsem, late_thru, land_thru, after, scatter=False, name="gather_late_wait")
        return l_out.reshape(D, D), l_gu.reshape(2, 4, D, FB), l_down.reshape(4, FB, D)

    early = {}

    def early_grads(dw_out, dw_gu, dw_down):
        srcs = [dw_out.reshape(NDEV, D // NDEV, D), dw_gu, dw_down.reshape(NDEV, DFF // NDEV, D)]
        lands = [_own_slot(lax.dynamic_index_in_dim(s, me, 0, keepdims=False), me) for s in srcs]
        early["handles"] = _exchange_start(srcs, lands, scatter=True, name="scatter_early_start")
        return early["handles"][4]

    (grad_x, dwp, dw_out, dw_gu, dw_down, dg_mix_pre, dg_mix_post, dg_ffn_pre, dg_ffn_post, dg_attn, dg_conv,
     dtaps, dbf, loss_p) = _local_step(x[0], loss_target[0], wp, late_weights, cw8, bfp, g_attn_out, g_conv_out,
                                        g_mix_pre, g_mix_post, g_ffn_pre, g_ffn_post, early_grads)
    e_ssem, e_rsem, e_srcs, e_lands, _ = early["handles"]
    land_out, land_gu, land_down = _exchange_wait(e_ssem, e_rsem, e_srcs, e_lands, dg_mix_pre, scatter=True,
                                                  name="scatter_early_wait")

    g_w_in = _disassemble_w_in(dwp, tables, tr=256).reshape(4, 2, D, IN_PAD)
    (from_sibling,) = _pair_exchange([g_w_in])
    pair_b, pair_own = _pair_sum(g_w_in, from_sibling, idx, tr=BIG_TILES["w_in"], name="grad_pair_sum_w_in")
    (from_chips,) = _chip_exchange([pair_b])

    small = _small_all_reduce([dg_mix_pre, dg_mix_post, dg_ffn_pre, dg_ffn_post, dg_attn, dg_conv, dtaps, dbf, loss_p])
    taps_full = jnp.concatenate([small[5:6, :CW], small[5:6, CW:], small[6:7, :CW]], axis=0)
    small_grads = {
        "b_forget": small[6:7, CW:CW + H], "conv_w": lax.dynamic_slice(taps_full, (0, me * 64), (3, 64)),
        "g_attn_out": small[4:5, :AW], "g_conv_out": small[4:5, AW:], "g_mix_pre": small[0:1], "g_mix_post": small[1:2],
        "g_ffn_pre": small[2:3], "g_ffn_post": small[3:4]}
    loss = small[6, CW + 128]

    res = {}
    outs = _chip_sum_adamw(from_chips, pair_own, idx, pad_in(w_in[0]), pad_in(m_w_in[0]), pad_in(v_w_in[0]),
                           tr=BIG_TILES["w_in"], name="adamw_w_in")
    res["w_in"] = [o[:, :IN_COLS][None] for o in outs]
    big = {"w_out": (land_out, w_out[0], m_w_out[0], v_w_out[0]),
           "w_gate_up": (land_gu, w_gate_up[0], m_w_gate_up[0], v_w_gate_up[0]),
           "w_down": (land_down, w_down[0], m_w_down[0], v_w_down[0])}
    for name, (land, w, m, v) in big.items():
        res[name] = [o[None] for o in _device_sum_adamw(land, w, m, v, tr=BIG_TILES[name], name="adamw_" + name)]
    smalls = {"b_forget": (b_forget, m_b_forget, v_b_forget), "conv_w": (conv_w[0], m_conv_w[0], v_conv_w[0]),
              "g_attn_out": (g_attn_out, m_g_attn_out, v_g_attn_out), "g_conv_out": (g_conv_out, m_g_conv_out, v_g_conv_out),
              "g_mix_pre": (g_mix_pre, m_g_mix_pre, v_g_mix_pre), "g_mix_post": (g_mix_post, m_g_mix_post, v_g_mix_post),
              "g_ffn_pre": (g_ffn_pre, m_g_ffn_pre, v_g_ffn_pre), "g_ffn_post": (g_ffn_post, m_g_ffn_post, v_g_ffn_post)}
    for name, (w, m, v) in smalls.items():
        g = small_grads[name]
        outs = [g] + list(_adamw(w, g, m, v, tr=w.shape[0], name="adamw_" + name))
        res[name] = [o[None] for o in outs] if name == "conv_w" else outs

    order = ["w_in", "b_forget", "conv_w", "g_attn_out", "g_conv_out", "w_out", "g_mix_pre", "g_mix_post",
             "w_gate_up", "w_down", "g_ffn_pre", "g_ffn_post"]
    outs = [loss, grad_x[None]]
    for k in range(4):
        outs += [res[n][k] for n in order]
    return tuple(outs)
```

```python
import functools

import numpy as np

import jax
import jax.numpy as jnp
from jax import lax
from jax.experimental import pallas as pl
from jax.experimental.pallas import tpu as pltpu

F32 = jnp.float32
BF16 = jnp.bfloat16
HIGHEST = lax.Precision.HIGHEST
MESH_ID = pl.DeviceIdType.MESH

D = 1024
H = 8
DH = 64
AW = 512
CW = 512
DFF = 2816
FB = DFF // 4
HP = 128
OFF_Q, OFF_K, OFF_V, OFF_BCU, OFF_F = 0, 1024, 2048, 2560, 4096
WP = OFF_F + 128
PIECES = ((OFF_Q, OFF_K), (OFF_K, OFF_V), (OFF_V, OFF_BCU), (OFF_BCU, OFF_F), (OFF_F, WP))
EPS = 1e-6
NDEV = 8
LANES = 128
SUBLANES = 8
IN_COLS = 385
IN_PAD = 512
WIN = 896
ADAM_LR, ADAM_B1, ADAM_B2, ADAM_EPS, ADAM_WD, ADAM_STEP = 0.001, 0.9, 0.999, 1e-08, 0.01, 10

NT = (((1,), (1,)), ((), ()))
TN = (((0,), (0,)), ((), ()))


def _cparams(vmem_mb=None, sem=None):
    kw = {}
    if vmem_mb is not None:
        kw["vmem_limit_bytes"] = vmem_mb << 20
    if sem is not None:
        kw["dimension_semantics"] = sem
    return pltpu.CompilerParams(**kw)


def _full(shape):
    return pl.BlockSpec(shape, lambda *_: (0,) * len(shape))


def _resident(shape):
    return pl.BlockSpec(shape, lambda *_: (0,) * len(shape), pipeline_mode=pl.Buffered(1))


def _rows(tm, width):
    return pl.BlockSpec((tm, width), lambda i: (i, 0))


def _fold8(v):
    r, w = v.shape
    return jnp.sum(v.reshape(r // SUBLANES, SUBLANES, w), axis=0)


def _split_dot(v, m01):
    hi = v.astype(BF16)
    lo = (v - hi.astype(F32)).astype(BF16)
    return (jnp.dot(hi, m01, preferred_element_type=F32)
            + jnp.dot(lo, m01, preferred_element_type=F32))


def _rms_fwd(v, g):
    r = lax.rsqrt(jnp.mean(v * v, axis=-1, keepdims=True) + EPS)
    n = v * r
    return n * g, n, r


def _rms_bwd(do, n, r, g):
    dn = do * g
    return r * (dn - n * jnp.mean(dn * n, axis=-1, keepdims=True)), do * n


def _padded_column(n):
    if n < AW:
        return OFF_Q + HP * (n // DH) + n % DH, 0.125
    if n < 2 * AW:
        m = n - AW
        return OFF_K + HP * (m // DH) + m % DH, 1.0
    if n < 3 * AW:
        return OFF_V + n - 2 * AW, 1.0
    if n < 3 * AW + H:
        return OFF_F + n - 3 * AW, 1.0
    return OFF_BCU + n - 3 * AW - H, 1.0


def _in_layout_tables():
    dest = -np.ones((IN_PAD, LANES), np.int32)
    dest_f = -np.ones((IN_PAD, LANES), np.int32)
    scale = np.zeros((IN_PAD, LANES), np.float32)
    starts = []
    for k in range(NDEV):
        cols = [_padded_column(IN_COLS * k + j) for j in range(IN_COLS)]
        main = [c for c, _ in cols if c < OFF_F]
        ws = min((min(main) // LANES) * LANES, OFF_F - WIN)
        assert ws <= min(main) and max(main) < ws + WIN
        starts.append(ws)
        for j, (c, sc) in enumerate(cols):
            scale[j, k] = sc
            if c < OFF_F:
                dest[j, k] = c - ws
            else:
                dest_f[j, k] = c - OFF_F
    f_shards = tuple(k for k in range(NDEV) if (dest_f[:, k] >= 0).any())
    return tuple(starts), f_shards, jnp.asarray(dest), jnp.asarray(dest_f), jnp.asarray(scale)


def _perm(dest_ref, scale_ref, k, width):
    lane = lax.broadcasted_iota(jnp.int32, (IN_PAD, width), 1)
    return jnp.where(dest_ref[:, k:k + 1] == lane, scale_ref[:, k:k + 1], 0.0).astype(BF16)


def _assemble_w_in(blocks, tables, *, tr):
    starts, f_shards, dest, dest_f, scale = tables

    def body(b_ref, dest_ref, destf_ref, scale_ref, o_ref):
        o_ref[...] = jnp.zeros_like(o_ref)
        for k in range(NDEV):
            b = b_ref[k]
            ws = starts[k]
            part = jnp.dot(b, _perm(dest_ref, scale_ref, k, WIN), preferred_element_type=F32)
            o_ref[:, ws:ws + WIN] += part.astype(BF16)
            if k in f_shards:
                part = jnp.dot(b, _perm(destf_ref, scale_ref, k, 128), preferred_element_type=F32)
                o_ref[:, OFF_F:WP] += part.astype(BF16)

    tab = _full((IN_PAD, LANES))
    return pl.pallas_call(
        body, name="assemble_w_in", grid=(D // tr,),
        in_specs=[pl.BlockSpec((NDEV, tr, IN_PAD), lambda i: (0, i, 0)), tab, tab, tab],
        out_specs=_rows(tr, WP),
        out_shape=jax.ShapeDtypeStruct((D, WP), BF16),
        compiler_params=_cparams(48, ("arbitrary",)),
    )(blocks, dest, dest_f, scale)


def _disassemble_w_in(pieces, tables, *, tr):
    starts, f_shards, dest, dest_f, scale = tables

    def body(q_ref, k_ref, v_ref, bcu_ref, f_ref, dest_ref, destf_ref, scale_ref, o_ref):
        refs = (q_ref, k_ref, v_ref, bcu_ref, f_ref)

        def window(ws):
            parts = []
            for ref, (lo, hi) in zip(refs, PIECES):
                a, b = max(ws, lo), min(ws + WIN, hi)
                if a < b:
                    parts.append(ref[:, a - lo:b - lo])
            return parts[0] if len(parts) == 1 else jnp.concatenate(parts, axis=1)

        for k in range(NDEV):
            acc = lax.dot_general(window(starts[k]), _perm(dest_ref, scale_ref, k, WIN), NT, preferred_element_type=F32)
            if k in f_shards:
                acc = acc + lax.dot_general(f_ref[...], _perm(destf_ref, scale_ref, k, 128), NT, preferred_element_type=F32)
            o_ref[k] = acc.astype(BF16)

    tab = _full((IN_PAD, LANES))
    return pl.pallas_call(
        body, name="disassemble_w_in", grid=(D // tr,),
        in_specs=[_rows(tr, hi - lo) for lo, hi in PIECES] + [tab, tab, tab],
        out_specs=pl.BlockSpec((NDEV, tr, IN_PAD), lambda i: (0, i, 0)),
        out_shape=jax.ShapeDtypeStruct((NDEV, D, IN_PAD), BF16),
        compiler_params=_cparams(48, ("arbitrary",)),
    )(*pieces, dest, dest_f, scale)


def _in_proj(x, g1, wp, bfp, pq, pk, oq, ok, *, tm):
    s = x.shape[0]

    def body(x_ref, g_ref, w_ref, bf_ref, pq_ref, pk_ref, oq_ref, ok_ref,
             h_ref, qp_ref, kp_ref, v_ref, bcu_ref, z_ref, carry):
        @pl.when(pl.program_id(0) == 0)
        def _():
            carry[...] = jnp.zeros_like(carry)

        h = _rms_fwd(x_ref[...], g_ref[...])[0].astype(BF16)
        h_ref[...] = h
        z = jnp.dot(h, w_ref[:, OFF_F:WP], preferred_element_type=F32) + bf_ref[...]
        z_ref[...] = z
        lane = lax.broadcasted_iota(jnp.int32, (tm, 128), 1)
        logf = jnp.where(lane < H, jnp.minimum(z, 0.0) - jnp.log(1.0 + jnp.exp(-jnp.abs(z))), 0.0)
        row = lax.broadcasted_iota(jnp.int32, (tm, tm), 0)
        col = lax.broadcasted_iota(jnp.int32, (tm, tm), 1)
        tri = (col <= row).astype(F32)
        c = jnp.dot(tri, logf, precision=HIGHEST, preferred_element_type=F32) + carry[0:1, :]
        carry[...] = jnp.broadcast_to(c[tm - 1:tm, :], carry.shape)
        c1 = c.astype(BF16).astype(F32)
        r1 = c - c1
        c2 = r1.astype(BF16).astype(F32)
        c3 = (r1 - c2).astype(BF16).astype(F32)
        zc = (c1 + pltpu.roll(c2, 8, axis=1) + pltpu.roll(c3, 16, axis=1)).astype(BF16)
        q = jnp.dot(h, w_ref[:, OFF_Q:OFF_K], preferred_element_type=F32)
        qp_ref[...] = (q + jnp.dot(zc, pq_ref[...], preferred_element_type=F32) + oq_ref[...]).astype(BF16)
        k = jnp.dot(h, w_ref[:, OFF_K:OFF_V], preferred_element_type=F32)
        kp_ref[...] = (k + jnp.dot(zc, pk_ref[...], preferred_element_type=F32) + ok_ref[...]).astype(BF16)
        v_ref[...] = jnp.dot(h, w_ref[:, OFF_V:OFF_BCU], preferred_element_type=F32).astype(BF16)
        bcu_ref[...] = jnp.dot(h, w_ref[:, OFF_BCU:OFF_F], preferred_element_type=F32)

    return pl.pallas_call(
        body, name="in_proj", grid=(s // tm,),
        in_specs=[_rows(tm, D), _full((1, D)), _resident((D, WP)), _full((1, 128)),
                  _full((128, 1024)), _full((128, 1024)), _full((1, 1024)), _full((1, 1024))],
        out_specs=[_rows(tm, D), _rows(tm, 1024), _rows(tm, 1024), _rows(tm, AW), _rows(tm, 3 * CW), _rows(tm, 128)],
        out_shape=[jax.ShapeDtypeStruct((s, D), BF16), jax.ShapeDtypeStruct((s, 1024), BF16),
                   jax.ShapeDtypeStruct((s, 1024), BF16), jax.ShapeDtypeStruct((s, AW), BF16),
                   jax.ShapeDtypeStruct((s, 3 * CW), F32), jax.ShapeDtypeStruct((s, 128), F32)],
        scratch_shapes=[pltpu.VMEM((SUBLANES, 128), F32)],
        compiler_params=_cparams(56, ("arbitrary",)),
    )(x, g1, wp, bfp, pq, pk, oq, ok)


def _attn_fwd(qp, kp, v, *, t):
    s = qp.shape[0]
    nq = s // t

    def body(q_ref, k_ref, v_ref, o_ref, lse_ref, mk_ref):
        qi = pl.program_id(1)
        row = lax.broadcasted_iota(jnp.int32, (t, t), 0)
        col = lax.broadcasted_iota(jnp.int32, (t, t), 1)
        lane = lax.broadcasted_iota(jnp.int32, (t, 128), 1)

        def head_step(hh, ki, carry, masked):
            m, l, acc = carry
            off = pl.multiple_of(ki * t, t)
            q = q_ref[:, HP * hh:HP * (hh + 1)]
            k = k_ref[pl.ds(off, t), HP * hh:HP * (hh + 1)]
            sc = lax.dot_general(q, k, NT, preferred_element_type=F32)
            if masked:
                sc = jnp.where(col <= row, sc, -1e30)
            mn = jnp.maximum(m, jnp.max(sc, axis=-1, keepdims=True))
            p = jnp.exp(sc - mn)
            a = jnp.exp(m - mn)
            l = a * l + jnp.sum(p, axis=-1, keepdims=True)
            acc = a * acc + jnp.dot(p.astype(BF16), v_ref[pl.ds(off, t), :], preferred_element_type=F32)
            return mn, l, acc

        def step(ki, carry, masked):
            new = tuple(head_step(hh, ki, carry[hh], masked) for hh in range(2))
            mk_ref[ki] = jnp.where(lane < DH, jnp.broadcast_to(new[0][0], (t, 128)), jnp.broadcast_to(new[1][0], (t, 128)))
            return new

        init = (jnp.full((t, 1), -1e30, F32), jnp.zeros((t, 1), F32), jnp.zeros((t, 128), F32))
        carry = lax.fori_loop(0, qi, functools.partial(step, masked=False), (init, init))
        (m0, l0, acc0), (m1, l1, acc1) = step(qi, carry, True)
        o_ref[...] = jnp.where(lane < DH, acc0 / l0, acc1 / l1)
        lse_ref[...] = jnp.where(lane < DH, jnp.broadcast_to(m0 + jnp.log(l0), (t, 128)),
                                 jnp.broadcast_to(m1 + jnp.log(l1), (t, 128)))

    return pl.pallas_call(
        body, name="attn_fwd", grid=(H // 2, nq),
        in_specs=[pl.BlockSpec((t, 2 * HP), lambda p, i: (i, p)),
                  pl.BlockSpec((s, 2 * HP), lambda p, i: (0, p)),
                  pl.BlockSpec((s, 128), lambda p, i: (0, p))],
        out_specs=[pl.BlockSpec((t, 128), lambda p, i: (i, p)), pl.BlockSpec((t, 128), lambda p, i: (i, p)),
                   pl.BlockSpec((nq, t, 128), lambda p, i: (0, i, p))],
        out_shape=[jax.ShapeDtypeStruct((s, AW), F32), jax.ShapeDtypeStruct((s, AW), F32),
                   jax.ShapeDtypeStruct((nq, s, AW), F32)],
        compiler_params=_cparams(48, ("arbitrary", "arbitrary")),
    )(qp, kp, v)


def _conv_taps(bcu_ref, halo_ref, first, tm):
    z = bcu_ref[:, CW:2 * CW] * bcu_ref[:, 2 * CW:3 * CW]
    zh = jnp.where(first, 0.0, halo_ref[:, CW:2 * CW] * halo_ref[:, 2 * CW:3 * CW])
    row = lax.broadcasted_iota(jnp.int32, (tm, CW), 0)
    z1 = jnp.where(row == 0, zh[7:8, :], pltpu.roll(z, 1, axis=0))
    z2 = jnp.where(row == 0, zh[6:7, :], jnp.where(row == 1, zh[7:8, :], pltpu.roll(z, 2, axis=0)))
    return z, z1, z2


def _halo_before(tm, width):
    return pl.BlockSpec((SUBLANES, width), lambda i: (jnp.maximum(i * (tm // SUBLANES) - 1, 0), 0))


def _mix_out(o, bcu, cw8, ga, gc, gsum, w_out, x, g_post, g_ffn_pre, *, tm):
    s = x.shape[0]

    def body(o_ref, bcu_ref, halo_ref, cw_ref, ga_ref, gc_ref, gs_ref, w_ref, x_ref, g_ref, gf_ref,
             merged_ref, y_ref, x2_ref, cv_ref, h2_ref):
        z, z1, z2 = _conv_taps(bcu_ref, halo_ref, pl.program_id(0) == 0, tm)
        cv = cw_ref[0:1, :] * z2 + cw_ref[1:2, :] * z1 + cw_ref[2:3, :] * z
        cv_ref[...] = cv
        conv = bcu_ref[:, 0:CW] * cv
        ov = o_ref[...]
        ra = lax.rsqrt(_split_dot(ov * ov, gs_ref[...]) * (1.0 / DH) + EPS)
        rc = lax.rsqrt(_split_dot(conv * conv, gs_ref[...]) * (1.0 / DH) + EPS)
        merged = jnp.concatenate([ov * ra * ga_ref[...], conv * rc * gc_ref[...]], axis=1).astype(BF16)
        merged_ref[...] = merged
        y = jnp.dot(merged, w_ref[...], preferred_element_type=F32)
        y_ref[...] = y
        x2 = x_ref[...] + _rms_fwd(y, g_ref[...])[0]
        x2_ref[...] = x2
        h2_ref[...] = _rms_fwd(x2, gf_ref[...])[0].astype(BF16)

    return pl.pallas_call(
        body, name="mix_out", grid=(s // tm,),
        in_specs=[_rows(tm, AW), _rows(tm, 3 * CW), _halo_before(tm, 3 * CW), _full((SUBLANES, CW)),
                  _full((1, AW)), _full((1, CW)), _full((CW, CW)), _resident((D, D)), _rows(tm, D), _full((1, D)),
                  _full((1, D))],
        out_specs=[_rows(tm, D), _rows(tm, D), _rows(tm, D), _rows(tm, CW), _rows(tm, D)],
        out_shape=[jax.ShapeDtypeStruct((s, D), BF16), jax.ShapeDtypeStruct((s, D), F32),
                   jax.ShapeDtypeStruct((s, D), F32), jax.ShapeDtypeStruct((s, CW), F32),
                   jax.ShapeDtypeStruct((s, D), BF16)],
        compiler_params=_cparams(48, ("arbitrary",)),
    )(o, bcu, bcu, cw8, ga, gc, gsum, w_out, x, g_post, g_ffn_pre)


def _ffn_up(h2, wgu, *, tm):
    s = h2.shape[0]

    def body(h_ref, w_ref, gate_ref, up_ref, a_ref):
        h = h_ref[...]
        gate = jnp.dot(h, w_ref[0, 0], preferred_element_type=F32)
        up = jnp.dot(h, w_ref[1, 0], preferred_element_type=F32)
        gate_ref[0] = gate.astype(BF16)
        up_ref[0] = up.astype(BF16)
        a_ref[0] = (gate * jax.nn.sigmoid(gate) * up).astype(BF16)

    blk = pl.BlockSpec((1, tm, FB), lambda j, i: (j, i, 0))
    return pl.pallas_call(
        body, name="ffn_up", grid=(4, s // tm),
        in_specs=[pl.BlockSpec((tm, D), lambda j, i: (i, 0)),
                  pl.BlockSpec((2, 1, D, FB), lambda j, i: (0, j, 0, 0))],
        out_specs=[blk, blk, blk],
        out_shape=[jax.ShapeDtypeStruct((4, s, FB), BF16)] * 3,
        compiler_params=_cparams(48, ("arbitrary", "arbitrary")),
    )(h2, wgu)


def _ffn_down_loss(a, wd, x2, target, g_post, *, tm):
    s = x2.shape[0]

    def body(a_ref, w_ref, x2_ref, t_ref, g_ref, dx3_ref, dff_ref, loss_ref, dg_ref):
        @pl.when(pl.program_id(0) == 0)
        def _():
            loss_ref[...] = jnp.zeros_like(loss_ref)
            dg_ref[...] = jnp.zeros_like(dg_ref)

        ff = jnp.dot(a_ref[0], w_ref[0], preferred_element_type=F32)
        for j in range(1, 4):
            ff = ff + jnp.dot(a_ref[j], w_ref[j], preferred_element_type=F32)
        out, n, r = _rms_fwd(ff, g_ref[...])
        e = x2_ref[...] + out - t_ref[...]
        loss_ref[...] += _fold8(e * e)
        dx3 = e * (1.0 / D)
        dx3_ref[...] = dx3
        dff, dg = _rms_bwd(dx3, n, r, g_ref[...])
        dff_ref[...] = dff.astype(BF16)
        dg_ref[...] += _fold8(dg)

    return pl.pallas_call(
        body, name="ffn_down_loss", grid=(s // tm,),
        in_specs=[pl.BlockSpec((4, tm, FB), lambda i: (0, i, 0)), _resident((4, FB, D)), _rows(tm, D), _rows(tm, D),
                  _full((1, D))],
        out_specs=[_rows(tm, D), _rows(tm, D), _full((SUBLANES, D)), _full((SUBLANES, D))],
        out_shape=[jax.ShapeDtypeStruct((s, D), F32), jax.ShapeDtypeStruct((s, D), BF16),
                   jax.ShapeDtypeStruct((SUBLANES, D), F32), jax.ShapeDtypeStruct((SUBLANES, D), F32)],
        compiler_params=_cparams(48, ("arbitrary",)),
    )(a, wd, x2, target, g_post)


def _ffn_bwd_act(dff, wd, gate, up, *, tm):
    s = dff.shape[0]

    def body(dff_ref, w_ref, gate_ref, up_ref, dgu_ref):
        da = lax.dot_general(dff_ref[...], w_ref[0], NT, preferred_element_type=F32)
        g = gate_ref[0].astype(F32)
        sg = jax.nn.sigmoid(g)
        dgu_ref[0, 0] = (da * up_ref[0].astype(F32) * (sg * (1.0 + g * (1.0 - sg)))).astype(BF16)
        dgu_ref[1, 0] = (da * (g * sg)).astype(BF16)

    blk = pl.BlockSpec((1, tm, FB), lambda j, i: (j, i, 0))
    return pl.pallas_call(
        body, name="ffn_bwd_act", grid=(4, s // tm),
        in_specs=[pl.BlockSpec((tm, D), lambda j, i: (i, 0)), pl.BlockSpec((1, FB, D), lambda j, i: (j, 0, 0)), blk, blk],
        out_specs=pl.BlockSpec((2, 1, tm, FB), lambda j, i: (0, j, i, 0)),
        out_shape=jax.ShapeDtypeStruct((2, 4, s, FB), BF16),
        compiler_params=_cparams(48, ("arbitrary", "arbitrary")),
    )(dff, wd, gate, up)


def _grad_matmul(a, b, *, ta, tb, ts, name):
    s, ka = a.shape
    nb = b.shape[1]
    nk = s // ts

    def body(a_ref, b_ref, o_ref, acc):
        k = pl.program_id(2)

        @pl.when(k == 0)
        def _():
            acc[...] = jnp.zeros_like(acc)

        acc[...] += lax.dot_general(a_ref[...], b_ref[...], TN, preferred_element_type=F32)

        @pl.when(k == nk - 1)
        def _():
            o_ref[...] = acc[...].astype(BF16)

    return pl.pallas_call(
        body, name=name, grid=(ka // ta, nb // tb, nk),
        in_specs=[pl.BlockSpec((ts, ta), lambda i, j, k: (k, i)), pl.BlockSpec((ts, tb), lambda i, j, k: (k, j))],
        out_specs=pl.BlockSpec((ta, tb), lambda i, j, k: (i, j)),
        out_shape=jax.ShapeDtypeStruct((ka, nb), BF16),
        scratch_shapes=[pltpu.VMEM((ta, tb), F32)],
        compiler_params=_cparams(48, ("arbitrary", "arbitrary", "arbitrary")),
    )(a, b)


def _grad_matmul_t(at, b, *, tb, name):
    ka, s = at.shape
    blocked = b.ndim == 3
    nb = b.shape[-1]
    steps = b.shape[0] if blocked else nb // tb
    width = nb if blocked else tb

    def body(a_ref, b_ref, o_ref):
        bv = b_ref[0] if blocked else b_ref[...]
        res = jnp.dot(a_ref[...], bv, preferred_element_type=F32).astype(BF16)
        if blocked:
            o_ref[0] = res
        else:
            o_ref[...] = res

    if blocked:
        b_spec = pl.BlockSpec((1, s, nb), lambda j: (j, 0, 0))
        o_spec = pl.BlockSpec((1, ka, nb), lambda j: (j, 0, 0))
        o_shape = jax.ShapeDtypeStruct((steps, ka, nb), BF16)
    else:
        b_spec = pl.BlockSpec((s, width), lambda j: (0, j))
        o_spec = pl.BlockSpec((ka, width), lambda j: (0, j))
        o_shape = jax.ShapeDtypeStruct((ka, nb), BF16)
    return pl.pallas_call(
        body, name=name, grid=(steps,),
        in_specs=[_resident((ka, s)), b_spec], out_specs=o_spec, out_shape=o_shape,
        compiler_params=_cparams(56, ("arbitrary",)),
    )(at, b)


def _grad_matmul_blocks(a, b, *, ts, name):
    nblk = a.shape[0] if a.ndim == 3 else b.shape[0]
    s = a.shape[-2]
    ka, nb = a.shape[-1], b.shape[-1]
    nk = s // ts

    def body(a_ref, b_ref, o_ref, acc):
        k = pl.program_id(1)

        @pl.when(k == 0)
        def _():
            acc[...] = jnp.zeros_like(acc)

        av = a_ref[0] if a.ndim == 3 else a_ref[...]
        bv = b_ref[0] if b.ndim == 3 else b_ref[...]
        acc[...] += lax.dot_general(av, bv, TN, preferred_element_type=F32)

        @pl.when(k == nk - 1)
        def _():
            o_ref[0] = acc[...].astype(BF16)

    def spec(arr, width):
        if arr.ndim == 3:
            return pl.BlockSpec((1, ts, width), lambda j, k: (j, k, 0))
        return pl.BlockSpec((ts, width), lambda j, k: (k, 0))

    return pl.pallas_call(
        body, name=name, grid=(nblk, nk),
        in_specs=[spec(a, ka), spec(b, nb)],
        out_specs=pl.BlockSpec((1, ka, nb), lambda j, k: (j, 0, 0)),
        out_shape=jax.ShapeDtypeStruct((nblk, ka, nb), BF16),
        scratch_shapes=[pltpu.VMEM((ka, nb), F32)],
        compiler_params=_cparams(48, ("arbitrary", "arbitrary")),
    )(a, b)


def _ffn_bwd_in(dgu, wgu, x2, g_pre, dx3, y, g_post, *, tm):
    s = x2.shape[0]

    def body(dgu_ref, w_ref, x2_ref, gpre_ref, dx3_ref, y_ref, gpost_ref,
             dx2_ref, dy_ref, dgpre_ref, dgpost_ref):
        @pl.when(pl.program_id(0) == 0)
        def _():
            dgpre_ref[...] = jnp.zeros_like(dgpre_ref)
            dgpost_ref[...] = jnp.zeros_like(dgpost_ref)

        dh2 = None
        for a in range(2):
            for j in range(4):
                part = lax.dot_general(dgu_ref[a, j], w_ref[a, j], NT, preferred_element_type=F32)
                dh2 = part if dh2 is None else dh2 + part
        _, n2, r2 = _rms_fwd(x2_ref[...], gpre_ref[...])
        dxn, dg = _rms_bwd(dh2, n2, r2, gpre_ref[...])
        dgpre_ref[...] += _fold8(dg)
        dx2 = dx3_ref[...] + dxn
        dx2_ref[...] = dx2
        _, ny, ry = _rms_fwd(y_ref[...], gpost_ref[...])
        dy, dg2 = _rms_bwd(dx2, ny, ry, gpost_ref[...])
        dy_ref[...] = dy.astype(BF16)
        dgpost_ref[...] += _fold8(dg2)

    return pl.pallas_call(
        body, name="ffn_bwd_in", grid=(s // tm,),
        in_specs=[pl.BlockSpec((2, 4, tm, FB), lambda i: (0, 0, i, 0)), _resident((2, 4, D, FB)), _rows(tm, D),
                  _full((1, D)), _rows(tm, D), _rows(tm, D), _full((1, D))],
        out_specs=[_rows(tm, D), _rows(tm, D), _full((SUBLANES, D)), _full((SUBLANES, D))],
        out_shape=[jax.ShapeDtypeStruct((s, D), F32), jax.ShapeDtypeStruct((s, D), BF16),
                   jax.ShapeDtypeStruct((SUBLANES, D), F32), jax.ShapeDtypeStruct((SUBLANES, D), F32)],
        compiler_params=_cparams(56, ("arbitrary",)),
    )(dgu, wgu, x2, g_pre, dx3, y, g_post)


def _mix_bwd(dy, w_out, o, cv, bcu, ga, gc, gsum, *, tm):
    s = dy.shape[0]

    def group_norm_bwd(dn_out, v, g, gs):
        r = lax.rsqrt(_split_dot(v * v, gs) * (1.0 / DH) + EPS)
        n = v * r
        dn = dn_out * g
        return r * (dn - n * (_split_dot(dn * n, gs) * (1.0 / DH))), dn_out * n

    def body(dy_ref, w_ref, o_ref, cv_ref, bcu_ref, ga_ref, gc_ref, gs_ref,
             do_ref, dl_ref, dcv_ref, db_ref, dga_ref, dgc_ref):
        @pl.when(pl.program_id(0) == 0)
        def _():
            dga_ref[...] = jnp.zeros_like(dga_ref)
            dgc_ref[...] = jnp.zeros_like(dgc_ref)

        dm = lax.dot_general(dy_ref[...], w_ref[...], NT, preferred_element_type=F32)
        ov = o_ref[...]
        do, dga = group_norm_bwd(dm[:, 0:AW], ov, ga_ref[...], gs_ref[...])
        dob = do.astype(BF16)
        do_ref[...] = dob
        dl_ref[...] = _split_dot(dob.astype(F32) * ov, gs_ref[...])
        dga_ref[...] += _fold8(dga)
        gate_b = bcu_ref[:, 0:CW]
        cv = cv_ref[...]
        dconv, dgc = group_norm_bwd(dm[:, AW:D], gate_b * cv, gc_ref[...], gs_ref[...])
        dgc_ref[...] += _fold8(dgc)
        dcv_ref[...] = dconv * gate_b
        db_ref[...] = (dconv * cv).astype(BF16)

    return pl.pallas_call(
        body, name="mix_bwd", grid=(s // tm,),
        in_specs=[_rows(tm, D), _resident((D, D)), _rows(tm, AW), _rows(tm, CW), _rows(tm, 3 * CW),
                  _full((1, AW)), _full((1, CW)), _full((CW, CW))],
        out_specs=[_rows(tm, AW), _rows(tm, AW), _rows(tm, CW), _rows(tm, CW),
                   _full((SUBLANES, AW)), _full((SUBLANES, CW))],
        out_shape=[jax.ShapeDtypeStruct((s, AW), BF16), jax.ShapeDtypeStruct((s, AW), F32),
                   jax.ShapeDtypeStruct((s, CW), F32), jax.ShapeDtypeStruct((s, CW), BF16),
                   jax.ShapeDtypeStruct((SUBLANES, AW), F32), jax.ShapeDtypeStruct((SUBLANES, CW), F32)],
        compiler_params=_cparams(48, ("arbitrary",)),
    )(dy, w_out, o, cv, bcu, ga, gc, gsum)


def _conv_bwd(dcv, db, bcu, cw8, *, tm):
    s = dcv.shape[0]
    nt = s // tm

    def body(dcv_ref, nxt_ref, db_ref, bcu_ref, halo_ref, cw_ref, dbcu_ref, dw_ref):
        i = pl.program_id(0)

        @pl.when(i == 0)
        def _():
            dw_ref[...] = jnp.zeros_like(dw_ref)

        z, z1, z2 = _conv_taps(bcu_ref, halo_ref, i == 0, tm)
        d = dcv_ref[...]
        dw_ref[0] += _fold8(d * z2)
        dw_ref[1] += _fold8(d * z1)
        dw_ref[2] += _fold8(d * z)
        nx = jnp.where(i == nt - 1, 0.0, nxt_ref[...])
        row = lax.broadcasted_iota(jnp.int32, (tm, CW), 0)
        d1 = jnp.where(row == tm - 1, nx[0:1, :], pltpu.roll(d, tm - 1, axis=0))
        d2 = jnp.where(row == tm - 2, nx[0:1, :], jnp.where(row == tm - 1, nx[1:2, :], pltpu.roll(d, tm - 2, axis=0)))
        dz = cw_ref[2:3, :] * d + cw_ref[1:2, :] * d1 + cw_ref[0:1, :] * d2
        dbcu_ref[:, 0:CW] = db_ref[...]
        dbcu_ref[:, CW:2 * CW] = (dz * bcu_ref[:, 2 * CW:3 * CW]).astype(BF16)
        dbcu_ref[:, 2 * CW:3 * CW] = (dz * bcu_ref[:, CW:2 * CW]).astype(BF16)

    return pl.pallas_call(
        body, name="conv_bwd", grid=(nt,),
        in_specs=[_rows(tm, CW),
                  pl.BlockSpec((SUBLANES, CW), lambda i: (jnp.minimum((i + 1) * (tm // SUBLANES), s // SUBLANES - 1), 0)),
                  _rows(tm, CW), _rows(tm, 3 * CW), _halo_before(tm, 3 * CW), _full((SUBLANES, CW))],
        out_specs=[_rows(tm, 3 * CW), _full((3, SUBLANES, CW))],
        out_shape=[jax.ShapeDtypeStruct((s, 3 * CW), BF16), jax.ShapeDtypeStruct((3, SUBLANES, CW), F32)],
        compiler_params=_cparams(48, ("arbitrary",)),
    )(dcv, dcv, db, bcu, bcu, cw8)


def _attn_bwd(qp, kp, v, do, lse, dl, mk, *, t):
    s = qp.shape[0]
    nq = s // t

    def body(q_ref, k_ref, v_ref, do_ref, lse_ref, dl_ref, mk_ref, dq_ref, dk_ref, dv_ref, dkx_ref, dq_acc):
        ki = pl.program_id(1)

        @pl.when(ki == 0)
        def _():
            dq_acc[...] = jnp.zeros_like(dq_acc)

        row = lax.broadcasted_iota(jnp.int32, (t, t), 0)
        col = lax.broadcasted_iota(jnp.int32, (t, t), 1)
        lane = lax.broadcasted_iota(jnp.int32, (t, 128), 1)

        def head_step(hh, qi, carry, masked):
            dk, dv, cs = carry
            off = pl.multiple_of(qi * t, t)
            rows = pl.ds(off, t)
            kh = k_ref[:, HP * hh:HP * (hh + 1)]
            q = q_ref[rows, HP * hh:HP * (hh + 1)]
            in_head = (lane >= DH * hh) & (lane < DH * (hh + 1))
            m_col = mk_ref[0, rows, DH * hh:DH * hh + 1]
            scale = jnp.exp(m_col - lse_ref[rows, DH * hh:DH * hh + 1])
            dom = jnp.where(in_head, do_ref[rows, :], jnp.zeros((), BF16))
            sc = lax.dot_general(q, kh, NT, preferred_element_type=F32) - m_col
            if masked:
                sc = jnp.where(col <= row, sc, -1e30)
            pt = jnp.exp(sc).astype(BF16)
            dp = lax.dot_general(dom, v_ref[...], NT, preferred_element_type=F32)
            ds32 = (pt.astype(F32) * scale) * (dp - dl_ref[rows, DH * hh:DH * hh + 1])
            ds = ds32.astype(BF16)
            cs = cs + _fold8(ds32)
            dv = dv + lax.dot_general(pt, (dom.astype(F32) * scale).astype(BF16), TN, preferred_element_type=F32)
            dk = dk + lax.dot_general(ds, q, TN, preferred_element_type=F32)
            dq_acc[rows, HP * hh:HP * (hh + 1)] += jnp.dot(ds, kh, preferred_element_type=F32)
            return dk, dv, cs

        def step(qi, carry, masked):
            return tuple(head_step(hh, qi, carry[hh], masked) for hh in range(2))

        zero = (jnp.zeros((t, HP), F32), jnp.zeros((t, 128), F32), jnp.zeros((SUBLANES, t), F32))
        carry = step(ki, (zero, zero), True)
        (dk0, dv0, cs0), (dk1, dv1, cs1) = lax.fori_loop(ki + 1, nq, functools.partial(step, masked=False), carry)
        dk_ref[:, 0:HP] = dk0.astype(BF16)
        dk_ref[:, HP:2 * HP] = dk1.astype(BF16)
        dv_ref[...] = (dv0 + dv1).astype(BF16)

        def as_column(cs):
            return lax.dot_general(cs, jnp.ones((SUBLANES, 128), F32), TN, precision=HIGHEST, preferred_element_type=F32)

        dkx_ref[...] = jnp.where(lane < DH, as_column(cs0), as_column(cs1))

        @pl.when(ki == nq - 1)
        def _():
            dq_ref[...] = dq_acc[...].astype(BF16)

    return pl.pallas_call(
        body, name="attn_bwd", grid=(H // 2, nq),
        in_specs=[pl.BlockSpec((s, 2 * HP), lambda p, i: (0, p)),
                  pl.BlockSpec((t, 2 * HP), lambda p, i: (i, p)),
                  pl.BlockSpec((t, 128), lambda p, i: (i, p)),
                  pl.BlockSpec((s, 128), lambda p, i: (0, p)),
                  pl.BlockSpec((s, 128), lambda p, i: (0, p)),
                  pl.BlockSpec((s, 128), lambda p, i: (0, p)),
                  pl.BlockSpec((1, s, 128), lambda p, i: (i, 0, p))],
        out_specs=[pl.BlockSpec((s, 2 * HP), lambda p, i: (0, p)),
                   pl.BlockSpec((t, 2 * HP), lambda p, i: (i, p)),
                   pl.BlockSpec((t, 128), lambda p, i: (i, p)),
                   pl.BlockSpec((t, 128), lambda p, i: (i, p))],
        out_shape=[jax.ShapeDtypeStruct((s, 1024), BF16), jax.ShapeDtypeStruct((s, 1024), BF16),
                   jax.ShapeDtypeStruct((s, AW), BF16), jax.ShapeDtypeStruct((s, AW), F32)],
        scratch_shapes=[pltpu.VMEM((s, 2 * HP), F32)],
        compiler_params=_cparams(56, ("arbitrary", "arbitrary")),
    )(qp, kp, v, do, lse, dl, mk)


def _forget_bwd(dkx, z, sel, *, tm):
    s = dkx.shape[0]
    nt = s // tm

    def body(dk_ref, z_ref, sel_ref, dfl_ref, dbf_ref, carry):
        @pl.when(pl.program_id(0) == 0)
        def _():
            carry[...] = jnp.zeros_like(carry)
            dbf_ref[...] = jnp.zeros_like(dbf_ref)

        dc = _split_dot(dk_ref[...], sel_ref[...])
        row = lax.broadcasted_iota(jnp.int32, (tm, tm), 0)
        col = lax.broadcasted_iota(jnp.int32, (tm, tm), 1)
        tri = (col >= row).astype(F32)
        dlogf = jnp.dot(tri, dc, precision=HIGHEST, preferred_element_type=F32) + carry[0:1, :]
        carry[...] = jnp.broadcast_to(dlogf[0:1, :], carry.shape)
        dz = dlogf * (1.0 - jax.nn.sigmoid(z_ref[...]))
        dfl_ref[...] = dz.astype(BF16)
        dbf_ref[...] += _fold8(dz)

    rev = lambda i: (nt - 1 - i, 0)
    return pl.pallas_call(
        body, name="forget_bwd", grid=(nt,),
        in_specs=[pl.BlockSpec((tm, AW), rev), pl.BlockSpec((tm, 128), rev), _full((AW, 128))],
        out_specs=[pl.BlockSpec((tm, 128), rev), _full((SUBLANES, 128))],
        out_shape=[jax.ShapeDtypeStruct((s, 128), BF16), jax.ShapeDtypeStruct((SUBLANES, 128), F32)],
        scratch_shapes=[pltpu.VMEM((SUBLANES, 128), F32)],
        compiler_params=_cparams(48, ("arbitrary",)),
    )(dkx, z, sel)


def _in_proj_bwd(pieces, wp, x, g1, dx2, *, tm):
    s = x.shape[0]

    def body(q_ref, k_ref, v_ref, bcu_ref, f_ref, w_ref, x_ref, g_ref, dx2_ref, dx_ref, dg_ref):
        @pl.when(pl.program_id(0) == 0)
        def _():
            dg_ref[...] = jnp.zeros_like(dg_ref)

        dh = None
        for ref, (lo, hi) in zip((q_ref, k_ref, v_ref, bcu_ref, f_ref), PIECES):
            part = lax.dot_general(ref[...], w_ref[:, lo:hi], NT, preferred_element_type=F32)
            dh = part if dh is None else dh + part
        _, n, r = _rms_fwd(x_ref[...], g_ref[...])
        dxn, dg = _rms_bwd(dh, n, r, g_ref[...])
        dx_ref[...] = dx2_ref[...] + dxn
        dg_ref[...] += _fold8(dg)

    return pl.pallas_call(
        body, name="in_proj_bwd", grid=(s // tm,),
        in_specs=[_rows(tm, hi - lo) for lo, hi in PIECES] + [_resident((D, WP)), _rows(tm, D), _full((1, D)), _rows(tm, D)],
        out_specs=[_rows(tm, D), _full((SUBLANES, D))],
        out_shape=[jax.ShapeDtypeStruct((s, D), F32), jax.ShapeDtypeStruct((SUBLANES, D), F32)],
        compiler_params=_cparams(56, ("arbitrary",)),
    )(*pieces, wp, x, g1, dx2)


def _position():
    return lax.axis_index("x"), lax.axis_index("y"), lax.axis_index("c")


ANY = pl.BlockSpec(memory_space=pl.ANY)


def _all_gather(shards):
    n = len(shards)

    def body(*refs):
        x_refs, out_refs = refs[:n], refs[n:2 * n]
        send_sems, recv_sems, local_sems = refs[2 * n:]
        x, y, c = _position()
        me, sibling = (x, y, c), (x, y, 1 - c)
        chips = [(1 - x, y), (x, 1 - y), (1 - x, 1 - y)]

        def copy(a, k, block, to, own=False):
            slot = out_refs[a].at[4 * block[0] + 2 * block[1] + block[2]]
            return pltpu.make_async_remote_copy(
                src_ref=x_refs[a] if own else slot, dst_ref=slot,
                send_sem=send_sems.at[7 * a + k], recv_sem=recv_sems.at[7 * a + k], device_id=to, device_id_type=MESH_ID)

        mine = [pltpu.make_async_copy(x_refs[a], out_refs[a].at[4 * x + 2 * y + c], local_sems.at[a]) for a in range(n)]
        for cp in mine:
            cp.start()
        first = []
        for a in range(n):
            first.append(copy(a, 0, me, sibling, own=True))
            first += [copy(a, 1 + j, me, (*chip, c), own=True) for j, chip in enumerate(chips)]
        for cp in first:
            cp.start()
        passed = []
        for j, chip in enumerate(chips):
            for a in range(n):
                copy(a, 1 + j, (*chip, c), me).wait_recv()
                fwd = copy(a, 4 + j, (*chip, c), sibling)
                fwd.start()
                passed.append(fwd)
        for a in range(n):
            copy(a, 0, sibling, me).wait_recv()
            for j, chip in enumerate(chips):
                copy(a, 4 + j, (*chip, 1 - c), me).wait_recv()
        for cp in first + passed:
            cp.wait_send()
        for cp in mine:
            cp.wait()

    return pl.pallas_call(
        body, name="all_gather_weights",
        out_shape=[jax.ShapeDtypeStruct((NDEV,) + sh.shape, sh.dtype) for sh in shards],
        in_specs=[ANY] * n, out_specs=[ANY] * n,
        scratch_shapes=[pltpu.SemaphoreType.DMA((7 * n,)), pltpu.SemaphoreType.DMA((7 * n,)), pltpu.SemaphoreType.DMA((n,))],
    )(*shards)


def _pair_exchange(grads):
    n = len(grads)

    def body(*refs):
        g_refs, out_refs = refs[:n], refs[n:2 * n]
        send_sems, recv_sems = refs[2 * n:]
        x, y, c = _position()
        copies = [pltpu.make_async_remote_copy(
            src_ref=g_refs[a].at[:, pl.ds(1 - c, 1)], dst_ref=out_refs[a], send_sem=send_sems.at[a],
            recv_sem=recv_sems.at[a], device_id=(x, y, 1 - c), device_id_type=MESH_ID) for a in range(n)]
        for cp in copies:
            cp.start()
        for cp in copies:
            cp.wait()

    return pl.pallas_call(
        body, name="grad_pair_exchange",
        out_shape=[jax.ShapeDtypeStruct((4, 1) + g.shape[2:], g.dtype) for g in grads],
        in_specs=[ANY] * n, out_specs=[ANY] * n,
        scratch_shapes=[pltpu.SemaphoreType.DMA((n,)), pltpu.SemaphoreType.DMA((n,))],
    )(*grads)


def _pair_sum(g, got, idx, *, tr, name):
    r, c = g.shape[2:]

    def body(idx_ref, g_ref, got_ref, pb_ref, own_ref):
        p = g_ref[0, 0].astype(F32) + got_ref[0, 0].astype(F32)
        pb_ref[0] = p.astype(BF16)

        @pl.when(pl.program_id(1) == idx_ref[1])
        def _():
            own_ref[...] = p

    return pl.pallas_call(
        body, name=name,
        grid_spec=pltpu.PrefetchScalarGridSpec(
            num_scalar_prefetch=1, grid=(r // tr, 4),
            in_specs=[pl.BlockSpec((1, 1, tr, c), lambda i, j, idx: (j, idx[0], i, 0)),
                      pl.BlockSpec((1, 1, tr, c), lambda i, j, idx: (j, 0, i, 0))],
            out_specs=[pl.BlockSpec((1, tr, c), lambda i, j, idx: (j, i, 0)),
                       pl.BlockSpec((tr, c), lambda i, j, idx: (i, 0))]),
        out_shape=[jax.ShapeDtypeStruct((4, r, c), BF16), jax.ShapeDtypeStruct((r, c), F32)],
        compiler_params=_cparams(32, ("arbitrary", "arbitrary")),
    )(idx, g, got)


HBM = pl.BlockSpec(memory_space=pltpu.HBM)
SEM = pl.BlockSpec(memory_space=pltpu.SEMAPHORE)
DATAFLOW = pltpu.SideEffectType.DATAFLOW_SIDE_EFFECTING


PEERS = {"gather": NDEV - 1, "scatter": NDEV - 1, "chips": 3}


def _exchange_copies(src_refs, land_refs, send_sems, recv_sems, mode):
    x, y, c = _position()
    me, my_chip = 4 * x + 2 * y + c, 2 * x + y
    npeers = PEERS[mode]
    copies = []
    for a, (s_ref, l_ref) in enumerate(zip(src_refs, land_refs)):
        for k in range(npeers):
            if mode == "chips":
                px, py, pc = x ^ ((k + 1) >> 1), y ^ ((k + 1) & 1), c
                src, dst = s_ref.at[2 * px + py], l_ref.at[my_chip]
            else:
                px, py, pc = x ^ ((k + 1) >> 2), y ^ (((k + 1) >> 1) & 1), c ^ ((k + 1) & 1)
                src, dst = (s_ref.at[4 * px + 2 * py + pc] if mode == "scatter" else s_ref), l_ref.at[me]
            copies.append(pltpu.make_async_remote_copy(
                src_ref=src, dst_ref=dst, send_sem=send_sems.at[npeers * a + k], recv_sem=recv_sems.at[npeers * a + k],
                device_id=(px, py, pc), device_id_type=MESH_ID))
    return copies


def _exchange_start(srcs, lands, *, mode, name):
    n = len(srcs)
    nsem = PEERS[mode] * n

    def body(*refs):
        token = refs[-1]
        for cp in _exchange_copies(refs[:n], refs[n:2 * n], refs[2 * n], refs[2 * n + 1], mode):
            cp.start()
        token[...] = jnp.zeros_like(token)

    arrays = list(srcs) + list(lands)
    outs = pl.pallas_call(
        body, name=name,
        out_shape=(pltpu.SemaphoreType.DMA((nsem,)), pltpu.SemaphoreType.DMA((nsem,)),
                   *[pltpu.HBM(a.shape, a.dtype) for a in arrays], jax.ShapeDtypeStruct((SUBLANES, LANES), F32)),
        in_specs=[HBM] * (2 * n),
        out_specs=(SEM, SEM, *[HBM] * (2 * n), pl.BlockSpec(memory_space=pltpu.VMEM)),
        input_output_aliases={i: 2 + i for i in range(2 * n)},
        compiler_params=pltpu.CompilerParams(has_side_effects=DATAFLOW),
    )(*[pltpu.with_memory_space_constraint(a, pltpu.HBM) for a in arrays])
    return outs[0], outs[1], outs[2:2 + n], outs[2 + n:2 + 2 * n], outs[-1]


def _exchange_wait(send_sems, recv_sems, srcs, lands, after, *, mode, name):
    n = len(srcs)

    def body(*refs):
        for cp in _exchange_copies(refs[:n], refs[n:2 * n], refs[2 * n], refs[2 * n + 1], mode):
            cp.wait_send()
            cp.wait_recv()

    arrays = list(srcs) + list(lands)
    outs = pl.pallas_call(
        body, name=name,
        out_shape=tuple(pltpu.HBM(a.shape, a.dtype) for a in arrays),
        in_specs=[HBM] * (2 * n) + [SEM, SEM, ANY],
        out_specs=tuple([HBM] * (2 * n)),
        input_output_aliases={i: i for i in range(2 * n)},
        compiler_params=pltpu.CompilerParams(has_side_effects=DATAFLOW),
    )(*arrays, send_sems, recv_sems, after)
    return outs[n:]


def _own_slot(value, me):
    return lax.dynamic_update_index_in_dim(lax.empty((NDEV,) + value.shape, value.dtype), value, me, 0)


def _small_all_reduce(parts):
    def body(gmp_ref, gmo_ref, gfp_ref, gfo_ref, ga_ref, gc_ref, dw_ref, bf_ref, loss_ref,
             out_ref, buf, send_sems, recv_sems):
        x, y, c = _position()
        me = 4 * x + 2 * y + c

        def colsum(v):
            return jnp.sum(v, axis=0, keepdims=True)

        loss = jnp.sum(colsum(loss_ref[...]), axis=1, keepdims=True) * (0.5 / D)
        rows = [colsum(gmp_ref[...]), colsum(gmo_ref[...]), colsum(gfp_ref[...]), colsum(gfo_ref[...]),
                jnp.concatenate([colsum(ga_ref[...]), colsum(gc_ref[...])], axis=1),
                jnp.concatenate([colsum(dw_ref[0]), colsum(dw_ref[1])], axis=1),
                jnp.concatenate([colsum(dw_ref[2]), colsum(bf_ref[...]), jnp.broadcast_to(loss, (1, 128)),
                                 jnp.zeros((1, 256), F32)], axis=1),
                jnp.zeros((1, D), F32)]
        buf[me] = jnp.concatenate(rows, axis=0)
        copies = []
        for mm in range(1, NDEV):
            peer = (x ^ (mm >> 2), y ^ ((mm >> 1) & 1), c ^ (mm & 1))
            copies.append(pltpu.make_async_remote_copy(
                src_ref=buf.at[me], dst_ref=buf.at[me], send_sem=send_sems.at[mm - 1], recv_sem=recv_sems.at[mm - 1],
                device_id=peer, device_id_type=MESH_ID))
        for cp in copies:
            cp.start()
        for cp in copies:
            cp.wait_recv()
        for cp in copies:
            cp.wait_send()
        acc = buf[0]
        for d in range(1, NDEV):
            acc = acc + buf[d]
        out_ref[...] = acc

    vm = pl.BlockSpec(memory_space=pltpu.VMEM)
    return pl.pallas_call(
        body, name="small_all_reduce",
        out_shape=jax.ShapeDtypeStruct((SUBLANES, D), F32),
        in_specs=[vm] * len(parts), out_specs=vm,
        scratch_shapes=[pltpu.VMEM((NDEV, SUBLANES, D), F32), pltpu.SemaphoreType.DMA((7,)), pltpu.SemaphoreType.DMA((7,))],
    )(*parts)


def _adam_update(w, g, m, v):
    nm = ADAM_B1 * m + (1.0 - ADAM_B1) * g
    nv = ADAM_B2 * v + (1.0 - ADAM_B2) * (g * g)
    m_hat = nm / (1.0 - ADAM_B1 ** ADAM_STEP)
    v_hat = nv / (1.0 - ADAM_B2 ** ADAM_STEP)
    return -ADAM_LR * (m_hat / (jnp.sqrt(v_hat) + ADAM_EPS) + ADAM_WD * w), nm, nv


def _adamw(w, g, m, v, *, tr, name):
    rows, cols = w.shape

    def body(w_ref, g_ref, m_ref, v_ref, d_ref, nm_ref, nv_ref):
        d_ref[...], nm_ref[...], nv_ref[...] = _adam_update(w_ref[...], g_ref[...], m_ref[...], v_ref[...])

    spec = pl.BlockSpec((tr, cols), lambda i: (i, 0))
    return pl.pallas_call(
        body, name=name, grid=(rows // tr,),
        in_specs=[spec] * 4, out_specs=[spec] * 3,
        out_shape=[jax.ShapeDtypeStruct((rows, cols), F32)] * 3,
        compiler_params=_cparams(32, ("arbitrary",)),
    )(w, g, m, v)


def _chip_sum_adamw(got, own, idx, w, m, v, *, tr, name):
    rows, cols = w.shape

    def body(idx_ref, got_ref, own_ref, w_ref, m_ref, v_ref, g_ref, d_ref, nm_ref, nv_ref):
        g = jnp.zeros((tr, cols), F32)
        for j in range(4):
            g = g + jnp.where(idx_ref[1] == j, own_ref[...], got_ref[j].astype(F32))
        g_ref[...] = g
        d_ref[...], nm_ref[...], nv_ref[...] = _adam_update(w_ref[...], g, m_ref[...], v_ref[...])

    spec = pl.BlockSpec((tr, cols), lambda i, idx: (i, 0))
    return pl.pallas_call(
        body, name=name,
        grid_spec=pltpu.PrefetchScalarGridSpec(
            num_scalar_prefetch=1, grid=(rows // tr,),
            in_specs=[pl.BlockSpec((4, tr, cols), lambda i, idx: (0, i, 0)), spec, spec, spec, spec],
            out_specs=[spec] * 4),
        out_shape=[jax.ShapeDtypeStruct((rows, cols), F32)] * 4,
        compiler_params=_cparams(32, ("arbitrary",)),
    )(idx, got, own, w, m, v)


def _device_sum_adamw(land, w, m, v, *, tr, name):
    rows, cols = w.shape

    def body(land_ref, w_ref, m_ref, v_ref, g_ref, d_ref, nm_ref, nv_ref):
        g = land_ref[0].astype(F32)
        for dev in range(1, NDEV):
            g = g + land_ref[dev].astype(F32)
        g_ref[...] = g
        d_ref[...], nm_ref[...], nv_ref[...] = _adam_update(w_ref[...], g, m_ref[...], v_ref[...])

    spec = pl.BlockSpec((tr, cols), lambda i: (i, 0))
    return pl.pallas_call(
        body, name=name, grid=(rows // tr,),
        in_specs=[pl.BlockSpec((NDEV, tr, cols), lambda i: (0, i, 0)), spec, spec, spec],
        out_specs=[spec] * 4,
        out_shape=[jax.ShapeDtypeStruct((rows, cols), F32)] * 4,
        compiler_params=_cparams(32, ("arbitrary",)),
    )(land, w, m, v)


def _placement_constants():
    j = jnp.arange(128)[:, None]
    lane = jnp.arange(1024)[None, :]
    head, sub = lane // HP, lane % HP
    piece, jh = j // H, j % H
    valid = (j < 3 * H) & (jh == head)
    pq = jnp.where(valid & (sub == DH + piece), 1.0, 0.0).astype(BF16)
    pk = jnp.where(valid & (sub == DH + 3 + piece), -1.0, 0.0).astype(BF16)
    oq = jnp.where((sub >= DH + 3) & (sub < DH + 6), 1.0, 0.0).astype(F32)
    ok = jnp.where((sub >= DH) & (sub < DH + 3), 1.0, 0.0).astype(F32)
    r = jnp.arange(AW)[:, None]
    cc = jnp.arange(128)[None, :]
    sel = jnp.where((r % DH == 3) & (r // DH == cc), -1.0, 0.0).astype(BF16)
    gi = jnp.arange(CW)
    gsum = (gi[:, None] // DH == gi[None, :] // DH).astype(BF16)
    return pq, pk, oq, ok, sel, gsum


def _local_step(xs, tgt, wp, late_weights, cw8, bfp, g_attn_out, g_conv_out,
                g_mix_pre, g_mix_post, g_ffn_pre, g_ffn_post, early_grads=None):
    pq, pk, oq, ok, sel, gsum = _placement_constants()
    h1, qp, kp, vv, bcu, zf = _in_proj(xs, g_mix_pre, wp, bfp, pq, pk, oq, ok, tm=512)
    o, lse, mk = _attn_fwd(qp, kp, vv, t=512)
    w_out_f, wgu, wd = late_weights(lse)
    merged, y, x2, cv, h2 = _mix_out(o, bcu, cw8, g_attn_out, g_conv_out, gsum, w_out_f, xs, g_mix_post, g_ffn_pre, tm=512)
    gate, up, act = _ffn_up(h2, wgu, tm=512)
    dx3, dff, loss_p, dg_ffn_post = _ffn_down_loss(act, wd, x2, tgt, g_ffn_post, tm=512)

    dgu = _ffn_bwd_act(dff, wd, gate, up, tm=512)
    dw_down = _grad_matmul_blocks(act, dff, ts=512, name="grad_w_down")
    dw_gu = _grad_matmul_t(h2.T, dgu.reshape(NDEV, -1, FB), tb=FB, name="grad_w_gate_up")
    dx2, dy, dg_ffn_pre, dg_mix_post = _ffn_bwd_in(dgu, wgu, x2, g_ffn_pre, dx3, y, g_mix_post, tm=256)
    dw_out = _grad_matmul(merged, dy, ta=1024, tb=1024, ts=512, name="grad_w_out")
    token = early_grads(dw_out, dw_gu, dw_down) if early_grads is not None else None
    ga = g_attn_out if token is None else g_attn_out + token[0:1, 0:1]
    do, dl, dcv, db, dg_attn, dg_conv = _mix_bwd(dy, w_out_f, o, cv, bcu, ga, g_conv_out, gsum, tm=512)
    dbcu, dtaps = _conv_bwd(dcv, db, bcu, cw8, tm=512)
    dqp, dkp, dv, dkx = _attn_bwd(qp, kp, vv, do, lse, dl, mk, t=512)
    dfl, dbf = _forget_bwd(dkx, zf, sel, tm=512)
    pieces = (dqp, dkp, dv, dbcu, dfl)
    names = ("grad_w_in_q", "grad_w_in_k", "grad_w_in_v", "grad_w_in_bcu", "grad_w_in_f")
    h1t = h1.T
    dwp = tuple(_grad_matmul_t(h1t, p, tb=min(p.shape[1], 512), name=nm) for p, nm in zip(pieces, names))
    grad_x, dg_mix_pre = _in_proj_bwd(pieces, wp, xs, g_mix_pre, dx2, tm=512)
    return (grad_x, dwp, dw_out, dw_gu, dw_down, dg_mix_pre, dg_mix_post, dg_ffn_pre, dg_ffn_post, dg_attn, dg_conv,
            dtaps, dbf, loss_p)


BIG_TILES = {"w_in": 256, "w_out": 128, "w_gate_up": 256, "w_down": 176}


def kernel(x, w_in, b_forget, conv_w, g_attn_out, g_conv_out, w_out, g_mix_pre, g_mix_post, w_gate_up, w_down, g_ffn_pre, g_ffn_post, loss_target, m_w_in, m_b_forget, m_conv_w, m_g_attn_out, m_g_conv_out, m_w_out, m_g_mix_pre, m_g_mix_post, m_w_gate_up, m_w_down, m_g_ffn_pre, m_g_ffn_post, v_w_in, v_b_forget, v_conv_w, v_g_attn_out, v_g_conv_out, v_w_out, v_g_mix_pre, v_g_mix_post, v_w_gate_up, v_w_down, v_g_ffn_pre, v_g_ffn_post):
    xc, yc, cc = _position()
    my_chip = 2 * xc + yc
    me = 2 * my_chip + cc
    idx = jnp.stack([cc, my_chip]).astype(jnp.int32)
    tables = _in_layout_tables()
    pad_in = lambda a: jnp.pad(a, ((0, 0), (0, IN_PAD - IN_COLS)))

    g_in, g_taps = _all_gather([pad_in(w_in[0]).astype(BF16), conv_w[0]])
    wp = _assemble_w_in(g_in, tables, tr=256)
    cw8 = jnp.pad(g_taps.transpose(1, 0, 2).reshape(3, CW), ((0, SUBLANES - 3), (0, 0)))

    late = [w_out[0].astype(BF16), w_gate_up[0].astype(BF16), w_down[0].astype(BF16)]
    ssem, rsem, late_thru, land_thru, token = _exchange_start(
        late, [_own_slot(s, me) for s in late], mode="gather", name="gather_late_start")
    bfp = jnp.pad(b_forget, ((0, 0), (0, 128 - H))) + token[0:1, :]

    def late_weights(after):
        l_out, l_gu, l_down = _exchange_wait(ssem, rsem, late_thru, land_thru, after, mode="gather", name="gather_late_wait")
        return l_out.reshape(D, D), l_gu.reshape(2, 4, D, FB), l_down.reshape(4, FB, D)

    early = {}

    def early_grads(dw_out, dw_gu, dw_down):
        srcs = [dw_out.reshape(NDEV, D // NDEV, D), dw_gu, dw_down.reshape(NDEV, DFF // NDEV, D)]
        lands = [_own_slot(lax.dynamic_index_in_dim(s, me, 0, keepdims=False), me) for s in srcs]
        early["handles"] = _exchange_start(srcs, lands, mode="scatter", name="scatter_early_start")
        return early["handles"][4]

    (grad_x, dwp, dw_out, dw_gu, dw_down, dg_mix_pre, dg_mix_post, dg_ffn_pre, dg_ffn_post, dg_attn, dg_conv,
     dtaps, dbf, loss_p) = _local_step(x[0], loss_target[0], wp, late_weights, cw8, bfp, g_attn_out, g_conv_out,
                                        g_mix_pre, g_mix_post, g_ffn_pre, g_ffn_post, early_grads)

    g_w_in = _disassemble_w_in(dwp, tables, tr=256).reshape(4, 2, D, IN_PAD)
    (from_sibling,) = _pair_exchange([g_w_in])
    pair_b, pair_own = _pair_sum(g_w_in, from_sibling, idx, tr=BIG_TILES["w_in"], name="grad_pair_sum_w_in")
    chip_land = lax.dynamic_update_index_in_dim(lax.empty(pair_b.shape, pair_b.dtype),
                                                lax.dynamic_index_in_dim(pair_b, my_chip, 0, keepdims=False), my_chip, 0)
    c_ssem, c_rsem, c_srcs, c_lands, c_token = _exchange_start([pair_b], [chip_land], mode="chips", name="chips_w_in_start")

    e_ssem, e_rsem, e_srcs, e_lands, _ = early["handles"]
    land_out, land_gu, land_down = _exchange_wait(e_ssem, e_rsem, e_srcs, e_lands, c_token, mode="scatter",
                                                  name="scatter_early_wait")

    small = _small_all_reduce([dg_mix_pre, dg_mix_post, dg_ffn_pre, dg_ffn_post, dg_attn, dg_conv, dtaps, dbf + c_token,
                               loss_p])
    taps_full = jnp.concatenate([small[5:6, :CW], small[5:6, CW:], small[6:7, :CW]], axis=0)
    small_grads = {
        "b_forget": small[6:7, CW:CW + H], "conv_w": lax.dynamic_slice(taps_full, (0, me * 64), (3, 64)),
        "g_attn_out": small[4:5, :AW], "g_conv_out": small[4:5, AW:], "g_mix_pre": small[0:1], "g_mix_post": small[1:2],
        "g_ffn_pre": small[2:3], "g_ffn_post": small[3:4]}
    loss = small[6, CW + 128]

    res = {}
    big = {"w_out": (land_out, w_out[0], m_w_out[0], v_w_out[0]),
           "w_gate_up": (land_gu, w_gate_up[0], m_w_gate_up[0], v_w_gate_up[0]),
           "w_down": (land_down, w_down[0], m_w_down[0], v_w_down[0])}
    for name, (land, w, m, v) in big.items():
        res[name] = [o[None] for o in _device_sum_adamw(land, w, m, v, tr=BIG_TILES[name], name="adamw_" + name)]
    smalls = {"b_forget": (b_forget, m_b_forget, v_b_forget), "conv_w": (conv_w[0], m_conv_w[0], v_conv_w[0]),
              "g_attn_out": (g_attn_out, m_g_attn_out, v_g_attn_out), "g_conv_out": (g_conv_out, m_g_conv_out, v_g_conv_out),
              "g_mix_pre": (g_mix_pre, m_g_mix_pre, v_g_mix_pre), "g_mix_post": (g_mix_post, m_g_mix_post, v_g_mix_post),
              "g_ffn_pre": (g_ffn_pre, m_g_ffn_pre, v_g_ffn_pre), "g_ffn_post": (g_ffn_post, m_g_ffn_post, v_g_ffn_post)}
    for name, (w, m, v) in smalls.items():
        g = small_grads[name]
        outs = [g] + list(_adamw(w, g, m, v, tr=w.shape[0], name="adamw_" + name))
        res[name] = [o[None] for o in outs] if name == "conv_w" else outs

    after = small[:, :LANES] + sum(res[n][1][0, :SUBLANES, :LANES] for n in big)
    (from_chips,) = _exchange_wait(c_ssem, c_rsem, c_srcs, c_lands, after, mode="chips", name="chips_w_in_wait")
    outs = _chip_sum_adamw(from_chips, pair_own, idx, pad_in(w_in[0]), pad_in(m_w_in[0]), pad_in(v_w_in[0]),
                           tr=BIG_TILES["w_in"], name="adamw_w_in")
    res["w_in"] = [o[:, :IN_COLS][None] for o in outs]

    order = ["w_in", "b_forget", "conv_w", "g_attn_out", "g_conv_out", "w_out", "g_mix_pre", "g_mix_post",
             "w_gate_up", "w_down", "g_ffn_pre", "g_ffn_post"]
    outs = [loss, grad_x[None]]
    for k in range(4):
        outs += [res[n][k] for n in order]
    return tuple(outs)
```

```python
import functools

import numpy as np

import jax
import jax.numpy as jnp
from jax import lax
from jax.experimental import pallas as pl
from jax.experimental.pallas import tpu as pltpu

F32 = jnp.float32
BF16 = jnp.bfloat16
HIGHEST = lax.Precision.HIGHEST
MESH_ID = pl.DeviceIdType.MESH

D = 1024
H = 8
DH = 64
AW = 512
CW = 512
DFF = 2816
FB = DFF // 4
HP = 128
OFF_Q, OFF_K, OFF_V, OFF_BCU, OFF_F = 0, 1024, 2048, 2560, 4096
WP = OFF_F + 128
PIECES = ((OFF_Q, OFF_K), (OFF_K, OFF_V), (OFF_V, OFF_BCU), (OFF_BCU, OFF_F), (OFF_F, WP))
EPS = 1e-6
NDEV = 8
LANES = 128
SUBLANES = 8
IN_COLS = 385
IN_PAD = 512
WIN = 896
ADAM_LR, ADAM_B1, ADAM_B2, ADAM_EPS, ADAM_WD, ADAM_STEP = 0.001, 0.9, 0.999, 1e-08, 0.01, 10

NT = (((1,), (1,)), ((), ()))
TN = (((0,), (0,)), ((), ()))


def _cparams(vmem_mb=None, sem=None):
    kw = {}
    if vmem_mb is not None:
        kw["vmem_limit_bytes"] = vmem_mb << 20
    if sem is not None:
        kw["dimension_semantics"] = sem
    return pltpu.CompilerParams(**kw)


def _full(shape):
    return pl.BlockSpec(shape, lambda *_: (0,) * len(shape))


def _resident(shape):
    return pl.BlockSpec(shape, lambda *_: (0,) * len(shape), pipeline_mode=pl.Buffered(1))


def _rows(tm, width):
    return pl.BlockSpec((tm, width), lambda i: (i, 0))


def _fold8(v):
    r, w = v.shape
    return jnp.sum(v.reshape(r // SUBLANES, SUBLANES, w), axis=0)


def _split_dot(v, m01):
    hi = v.astype(BF16)
    lo = (v - hi.astype(F32)).astype(BF16)
    return (jnp.dot(hi, m01, preferred_element_type=F32)
            + jnp.dot(lo, m01, preferred_element_type=F32))


def _rms_fwd(v, g):
    r = lax.rsqrt(jnp.mean(v * v, axis=-1, keepdims=True) + EPS)
    n = v * r
    return n * g, n, r


def _rms_bwd(do, n, r, g):
    dn = do * g
    return r * (dn - n * jnp.mean(dn * n, axis=-1, keepdims=True)), do * n


def _padded_column(n):
    if n < AW:
        return OFF_Q + HP * (n // DH) + n % DH, 0.125
    if n < 2 * AW:
        m = n - AW
        return OFF_K + HP * (m // DH) + m % DH, 1.0
    if n < 3 * AW:
        return OFF_V + n - 2 * AW, 1.0
    if n < 3 * AW + H:
        return OFF_F + n - 3 * AW, 1.0
    return OFF_BCU + n - 3 * AW - H, 1.0


def _in_layout_tables():
    dest = -np.ones((IN_PAD, LANES), np.int32)
    dest_f = -np.ones((IN_PAD, LANES), np.int32)
    scale = np.zeros((IN_PAD, LANES), np.float32)
    starts = []
    for k in range(NDEV):
        cols = [_padded_column(IN_COLS * k + j) for j in range(IN_COLS)]
        main = [c for c, _ in cols if c < OFF_F]
        ws = min((min(main) // LANES) * LANES, OFF_F - WIN)
        assert ws <= min(main) and max(main) < ws + WIN
        starts.append(ws)
        for j, (c, sc) in enumerate(cols):
            scale[j, k] = sc
            if c < OFF_F:
                dest[j, k] = c - ws
            else:
                dest_f[j, k] = c - OFF_F
    f_shards = tuple(k for k in range(NDEV) if (dest_f[:, k] >= 0).any())
    return tuple(starts), f_shards, jnp.asarray(dest), jnp.asarray(dest_f), jnp.asarray(scale)


def _perm(dest_ref, scale_ref, k, width):
    lane = lax.broadcasted_iota(jnp.int32, (IN_PAD, width), 1)
    return jnp.where(dest_ref[:, k:k + 1] == lane, scale_ref[:, k:k + 1], 0.0).astype(BF16)


def _assemble_w_in(blocks, tables, *, tr):
    starts, f_shards, dest, dest_f, scale = tables

    def body(b_ref, dest_ref, destf_ref, scale_ref, o_ref):
        o_ref[...] = jnp.zeros_like(o_ref)
        for k in range(NDEV):
            b = b_ref[k]
            ws = starts[k]
            part = jnp.dot(b, _perm(dest_ref, scale_ref, k, WIN), preferred_element_type=F32)
            o_ref[:, ws:ws + WIN] += part.astype(BF16)
            if k in f_shards:
                part = jnp.dot(b, _perm(destf_ref, scale_ref, k, 128), preferred_element_type=F32)
                o_ref[:, OFF_F:WP] += part.astype(BF16)

    tab = _full((IN_PAD, LANES))
    return pl.pallas_call(
        body, name="assemble_w_in", grid=(D // tr,),
        in_specs=[pl.BlockSpec((NDEV, tr, IN_PAD), lambda i: (0, i, 0)), tab, tab, tab],
        out_specs=_rows(tr, WP),
        out_shape=jax.ShapeDtypeStruct((D, WP), BF16),
        compiler_params=_cparams(48, ("arbitrary",)),
    )(blocks, dest, dest_f, scale)


def _disassemble_w_in(dwp, tables, *, tr):
    starts, f_shards, dest, dest_f, scale = tables
    width = dwp.shape[1]

    def body(g_ref, dest_ref, destf_ref, scale_ref, o_ref):
        for k in range(NDEV):
            ws = starts[k]
            acc = lax.dot_general(g_ref[:, ws:ws + WIN], _perm(dest_ref, scale_ref, k, WIN), NT, preferred_element_type=F32)
            if k in f_shards:
                acc = acc + lax.dot_general(g_ref[:, OFF_F:WP], _perm(destf_ref, scale_ref, k, 128), NT,
                                            preferred_element_type=F32)
            o_ref[k] = acc.astype(BF16)

    tab = _full((IN_PAD, LANES))
    return pl.pallas_call(
        body, name="disassemble_w_in", grid=(D // tr,),
        in_specs=[_rows(tr, width), tab, tab, tab],
        out_specs=pl.BlockSpec((NDEV, tr, IN_PAD), lambda i: (0, i, 0)),
        out_shape=jax.ShapeDtypeStruct((NDEV, D, IN_PAD), BF16),
        compiler_params=_cparams(48, ("arbitrary",)),
    )(dwp, dest, dest_f, scale)


def _in_proj(x, g1, wp, bfp, pq, pk, oq, ok, *, tm):
    s = x.shape[0]

    def body(x_ref, g_ref, w_ref, bf_ref, pq_ref, pk_ref, oq_ref, ok_ref,
             ht_ref, qp_ref, kp_ref, v_ref, bcu_ref, z_ref, carry):
        @pl.when(pl.program_id(0) == 0)
        def _():
            carry[...] = jnp.zeros_like(carry)

        h = _rms_fwd(x_ref[...], g_ref[...])[0].astype(BF16)
        ht_ref[...] = h.T
        z = jnp.dot(h, w_ref[:, OFF_F:WP], preferred_element_type=F32) + bf_ref[...]
        z_ref[...] = z
        lane = lax.broadcasted_iota(jnp.int32, (tm, 128), 1)
        logf = jnp.where(lane < H, jnp.minimum(z, 0.0) - jnp.log(1.0 + jnp.exp(-jnp.abs(z))), 0.0)
        row = lax.broadcasted_iota(jnp.int32, (tm, tm), 0)
        col = lax.broadcasted_iota(jnp.int32, (tm, tm), 1)
        tri = (col <= row).astype(F32)
        c = jnp.dot(tri, logf, precision=HIGHEST, preferred_element_type=F32) + carry[0:1, :]
        carry[...] = jnp.broadcast_to(c[tm - 1:tm, :], carry.shape)
        c1 = c.astype(BF16).astype(F32)
        r1 = c - c1
        c2 = r1.astype(BF16).astype(F32)
        c3 = (r1 - c2).astype(BF16).astype(F32)
        zc = (c1 + pltpu.roll(c2, 8, axis=1) + pltpu.roll(c3, 16, axis=1)).astype(BF16)
        q = jnp.dot(h, w_ref[:, OFF_Q:OFF_K], preferred_element_type=F32)
        qp_ref[...] = (q + jnp.dot(zc, pq_ref[...], preferred_element_type=F32) + oq_ref[...]).astype(BF16)
        k = jnp.dot(h, w_ref[:, OFF_K:OFF_V], preferred_element_type=F32)
        kp_ref[...] = (k + jnp.dot(zc, pk_ref[...], preferred_element_type=F32) + ok_ref[...]).astype(BF16)
        v_ref[...] = jnp.dot(h, w_ref[:, OFF_V:OFF_BCU], preferred_element_type=F32).astype(BF16)
        bcu_ref[...] = jnp.dot(h, w_ref[:, OFF_BCU:OFF_F], preferred_element_type=F32)

    return pl.pallas_call(
        body, name="in_proj", grid=(s // tm,),
        in_specs=[_rows(tm, D), _full((1, D)), _resident((D, WP)), _full((1, 128)),
                  _full((128, 1024)), _full((128, 1024)), _full((1, 1024)), _full((1, 1024))],
        out_specs=[pl.BlockSpec((D, tm), lambda i: (0, i)), _rows(tm, 1024), _rows(tm, 1024), _rows(tm, AW),
                   _rows(tm, 3 * CW), _rows(tm, 128)],
        out_shape=[jax.ShapeDtypeStruct((D, s), BF16), jax.ShapeDtypeStruct((s, 1024), BF16),
                   jax.ShapeDtypeStruct((s, 1024), BF16), jax.ShapeDtypeStruct((s, AW), BF16),
                   jax.ShapeDtypeStruct((s, 3 * CW), F32), jax.ShapeDtypeStruct((s, 128), F32)],
        scratch_shapes=[pltpu.VMEM((SUBLANES, 128), F32)],
        compiler_params=_cparams(56, ("arbitrary",)),
    )(x, g1, wp, bfp, pq, pk, oq, ok)


def _attn_fwd(qp, kp, v, *, t):
    s = qp.shape[0]
    nq = s // t

    def body(q_ref, k_ref, v_ref, o_ref, lse_ref, mk_ref):
        qi = pl.program_id(1)
        row = lax.broadcasted_iota(jnp.int32, (t, t), 0)
        col = lax.broadcasted_iota(jnp.int32, (t, t), 1)
        lane = lax.broadcasted_iota(jnp.int32, (t, 128), 1)

        def head_step(hh, ki, carry, masked):
            m, l, acc = carry
            off = pl.multiple_of(ki * t, t)
            q = q_ref[:, HP * hh:HP * (hh + 1)]
            k = k_ref[pl.ds(off, t), HP * hh:HP * (hh + 1)]
            sc = lax.dot_general(q, k, NT, preferred_element_type=F32)
            if masked:
                sc = jnp.where(col <= row, sc, -1e30)
            mn = jnp.maximum(m, jnp.max(sc, axis=-1, keepdims=True))
            p = jnp.exp(sc - mn)
            a = jnp.exp(m - mn)
            l = a * l + jnp.sum(p, axis=-1, keepdims=True)
            acc = a * acc + jnp.dot(p.astype(BF16), v_ref[pl.ds(off, t), :], preferred_element_type=F32)
            return mn, l, acc

        def step(ki, carry, masked):
            new = tuple(head_step(hh, ki, carry[hh], masked) for hh in range(2))
            mk_ref[ki] = jnp.where(lane < DH, jnp.broadcast_to(new[0][0], (t, 128)), jnp.broadcast_to(new[1][0], (t, 128)))
            return new

        init = (jnp.full((t, 1), -1e30, F32), jnp.zeros((t, 1), F32), jnp.zeros((t, 128), F32))
        carry = lax.fori_loop(0, qi, functools.partial(step, masked=False), (init, init))
        (m0, l0, acc0), (m1, l1, acc1) = step(qi, carry, True)
        o_ref[...] = jnp.where(lane < DH, acc0 / l0, acc1 / l1)
        lse_ref[...] = jnp.where(lane < DH, jnp.broadcast_to(m0 + jnp.log(l0), (t, 128)),
                                 jnp.broadcast_to(m1 + jnp.log(l1), (t, 128)))

    return pl.pallas_call(
        body, name="attn_fwd", grid=(H // 2, nq),
        in_specs=[pl.BlockSpec((t, 2 * HP), lambda p, i: (i, p)),
                  pl.BlockSpec((s, 2 * HP), lambda p, i: (0, p)),
                  pl.BlockSpec((s, 128), lambda p, i: (0, p))],
        out_specs=[pl.BlockSpec((t, 128), lambda p, i: (i, p)), pl.BlockSpec((t, 128), lambda p, i: (i, p)),
                   pl.BlockSpec((nq, t, 128), lambda p, i: (0, i, p))],
        out_shape=[jax.ShapeDtypeStruct((s, AW), F32), jax.ShapeDtypeStruct((s, AW), F32),
                   jax.ShapeDtypeStruct((nq, s, AW), F32)],
        compiler_params=_cparams(48, ("arbitrary", "arbitrary")),
    )(qp, kp, v)


def _conv_taps(bcu_ref, halo_ref, first, tm):
    z = bcu_ref[:, CW:2 * CW] * bcu_ref[:, 2 * CW:3 * CW]
    zh = jnp.where(first, 0.0, halo_ref[:, CW:2 * CW] * halo_ref[:, 2 * CW:3 * CW])
    row = lax.broadcasted_iota(jnp.int32, (tm, CW), 0)
    z1 = jnp.where(row == 0, zh[7:8, :], pltpu.roll(z, 1, axis=0))
    z2 = jnp.where(row == 0, zh[6:7, :], jnp.where(row == 1, zh[7:8, :], pltpu.roll(z, 2, axis=0)))
    return z, z1, z2


def _halo_before(tm, width):
    return pl.BlockSpec((SUBLANES, width), lambda i: (jnp.maximum(i * (tm // SUBLANES) - 1, 0), 0))


def _mix_out(o, bcu, cw8, ga, gc, gsum, w_out, x, g_post, g_ffn_pre, *, tm):
    s = x.shape[0]

    def body(o_ref, bcu_ref, halo_ref, cw_ref, ga_ref, gc_ref, gs_ref, w_ref, x_ref, g_ref, gf_ref,
             merged_ref, y_ref, x2_ref, cv_ref, h2_ref, h2t_ref):
        z, z1, z2 = _conv_taps(bcu_ref, halo_ref, pl.program_id(0) == 0, tm)
        cv = cw_ref[0:1, :] * z2 + cw_ref[1:2, :] * z1 + cw_ref[2:3, :] * z
        cv_ref[...] = cv
        conv = bcu_ref[:, 0:CW] * cv
        ov = o_ref[...]
        ra = lax.rsqrt(_split_dot(ov * ov, gs_ref[...]) * (1.0 / DH) + EPS)
        rc = lax.rsqrt(_split_dot(conv * conv, gs_ref[...]) * (1.0 / DH) + EPS)
        merged = jnp.concatenate([ov * ra * ga_ref[...], conv * rc * gc_ref[...]], axis=1).astype(BF16)
        merged_ref[...] = merged
        y = jnp.dot(merged, w_ref[...], preferred_element_type=F32)
        y_ref[...] = y
        x2 = x_ref[...] + _rms_fwd(y, g_ref[...])[0]
        x2_ref[...] = x2
        h2 = _rms_fwd(x2, gf_ref[...])[0].astype(BF16)
        h2_ref[...] = h2
        h2t_ref[...] = h2.T

    return pl.pallas_call(
        body, name="mix_out", grid=(s // tm,),
        in_specs=[_rows(tm, AW), _rows(tm, 3 * CW), _halo_before(tm, 3 * CW), _full((SUBLANES, CW)),
                  _full((1, AW)), _full((1, CW)), _full((CW, CW)), _resident((D, D)), _rows(tm, D), _full((1, D)),
                  _full((1, D))],
        out_specs=[_rows(tm, D), _rows(tm, D), _rows(tm, D), _rows(tm, CW), _rows(tm, D),
                   pl.BlockSpec((D, tm), lambda i: (0, i))],
        out_shape=[jax.ShapeDtypeStruct((s, D), BF16), jax.ShapeDtypeStruct((s, D), F32),
                   jax.ShapeDtypeStruct((s, D), F32), jax.ShapeDtypeStruct((s, CW), F32),
                   jax.ShapeDtypeStruct((s, D), BF16), jax.ShapeDtypeStruct((D, s), BF16)],
        compiler_params=_cparams(48, ("arbitrary",)),
    )(o, bcu, bcu, cw8, ga, gc, gsum, w_out, x, g_post, g_ffn_pre)


def _ffn_up(h2, wgu, *, tm):
    s = h2.shape[0]

    def body(h_ref, w_ref, gate_ref, up_ref, a_ref):
        h = h_ref[...]
        gate = jnp.dot(h, w_ref[0, 0], preferred_element_type=F32)
        up = jnp.dot(h, w_ref[1, 0], preferred_element_type=F32)
        gate_ref[0] = gate.astype(BF16)
        up_ref[0] = up.astype(BF16)
        a_ref[0] = (gate * jax.nn.sigmoid(gate) * up).astype(BF16)

    blk = pl.BlockSpec((1, tm, FB), lambda j, i: (j, i, 0))
    return pl.pallas_call(
        body, name="ffn_up", grid=(4, s // tm),
        in_specs=[pl.BlockSpec((tm, D), lambda j, i: (i, 0)),
                  pl.BlockSpec((2, 1, D, FB), lambda j, i: (0, j, 0, 0))],
        out_specs=[blk, blk, blk],
        out_shape=[jax.ShapeDtypeStruct((4, s, FB), BF16)] * 3,
        compiler_params=_cparams(48, ("arbitrary", "arbitrary")),
    )(h2, wgu)


def _ffn_down_loss(a, wd, x2, target, g_post, *, tm):
    s = x2.shape[0]

    def body(a_ref, w_ref, x2_ref, t_ref, g_ref, dx3_ref, dff_ref, loss_ref, dg_ref):
        @pl.when(pl.program_id(0) == 0)
        def _():
            loss_ref[...] = jnp.zeros_like(loss_ref)
            dg_ref[...] = jnp.zeros_like(dg_ref)

        ff = jnp.dot(a_ref[0], w_ref[0], preferred_element_type=F32)
        for j in range(1, 4):
            ff = ff + jnp.dot(a_ref[j], w_ref[j], preferred_element_type=F32)
        out, n, r = _rms_fwd(ff, g_ref[...])
        e = x2_ref[...] + out - t_ref[...]
        loss_ref[...] += _fold8(e * e)
        dx3 = e * (1.0 / D)
        dx3_ref[...] = dx3
        dff, dg = _rms_bwd(dx3, n, r, g_ref[...])
        dff_ref[...] = dff.astype(BF16)
        dg_ref[...] += _fold8(dg)

    return pl.pallas_call(
        body, name="ffn_down_loss", grid=(s // tm,),
        in_specs=[pl.BlockSpec((4, tm, FB), lambda i: (0, i, 0)), _resident((4, FB, D)), _rows(tm, D), _rows(tm, D),
                  _full((1, D))],
        out_specs=[_rows(tm, D), _rows(tm, D), _full((SUBLANES, D)), _full((SUBLANES, D))],
        out_shape=[jax.ShapeDtypeStruct((s, D), F32), jax.ShapeDtypeStruct((s, D), BF16),
                   jax.ShapeDtypeStruct((SUBLANES, D), F32), jax.ShapeDtypeStruct((SUBLANES, D), F32)],
        compiler_params=_cparams(48, ("arbitrary",)),
    )(a, wd, x2, target, g_post)


def _ffn_bwd_act(dff, wd, gate, up, *, tm):
    s = dff.shape[0]

    def body(dff_ref, w_ref, gate_ref, up_ref, dgu_ref):
        da = lax.dot_general(dff_ref[...], w_ref[0], NT, preferred_element_type=F32)
        g = gate_ref[0].astype(F32)
        sg = jax.nn.sigmoid(g)
        dgu_ref[0, 0] = (da * up_ref[0].astype(F32) * (sg * (1.0 + g * (1.0 - sg)))).astype(BF16)
        dgu_ref[1, 0] = (da * (g * sg)).astype(BF16)

    blk = pl.BlockSpec((1, tm, FB), lambda j, i: (j, i, 0))
    return pl.pallas_call(
        body, name="ffn_bwd_act", grid=(4, s // tm),
        in_specs=[pl.BlockSpec((tm, D), lambda j, i: (i, 0)), pl.BlockSpec((1, FB, D), lambda j, i: (j, 0, 0)), blk, blk],
        out_specs=pl.BlockSpec((2, 1, tm, FB), lambda j, i: (0, j, i, 0)),
        out_shape=jax.ShapeDtypeStruct((2, 4, s, FB), BF16),
        compiler_params=_cparams(48, ("arbitrary", "arbitrary")),
    )(dff, wd, gate, up)


def _grad_matmul(a, b, *, ta, tb, ts, name):
    s, ka = a.shape
    nb = b.shape[1]
    nk = s // ts

    def body(a_ref, b_ref, o_ref, acc):
        k = pl.program_id(2)

        @pl.when(k == 0)
        def _():
            acc[...] = jnp.zeros_like(acc)

        acc[...] += lax.dot_general(a_ref[...], b_ref[...], TN, preferred_element_type=F32)

        @pl.when(k == nk - 1)
        def _():
            o_ref[...] = acc[...].astype(BF16)

    return pl.pallas_call(
        body, name=name, grid=(ka // ta, nb // tb, nk),
        in_specs=[pl.BlockSpec((ts, ta), lambda i, j, k: (k, i)), pl.BlockSpec((ts, tb), lambda i, j, k: (k, j))],
        out_specs=pl.BlockSpec((ta, tb), lambda i, j, k: (i, j)),
        out_shape=jax.ShapeDtypeStruct((ka, nb), BF16),
        scratch_shapes=[pltpu.VMEM((ta, tb), F32)],
        compiler_params=_cparams(48, ("arbitrary", "arbitrary", "arbitrary")),
    )(a, b)


def _grad_matmul_t(at, b, *, tb, name):
    ka, s = at.shape
    blocked = b.ndim == 3
    nb = b.shape[-1]
    steps = b.shape[0] if blocked else nb // tb
    width = nb if blocked else tb

    def body(a_ref, b_ref, o_ref):
        bv = b_ref[0] if blocked else b_ref[...]
        res = jnp.dot(a_ref[...], bv, preferred_element_type=F32).astype(BF16)
        if blocked:
            o_ref[0] = res
        else:
            o_ref[...] = res

    if blocked:
        b_spec = pl.BlockSpec((1, s, nb), lambda j: (j, 0, 0))
        o_spec = pl.BlockSpec((1, ka, nb), lambda j: (j, 0, 0))
        o_shape = jax.ShapeDtypeStruct((steps, ka, nb), BF16)
    else:
        b_spec = pl.BlockSpec((s, width), lambda j: (0, j))
        o_spec = pl.BlockSpec((ka, width), lambda j: (0, j))
        o_shape = jax.ShapeDtypeStruct((ka, nb), BF16)
    return pl.pallas_call(
        body, name=name, grid=(steps,),
        in_specs=[_resident((ka, s)), b_spec], out_specs=o_spec, out_shape=o_shape,
        compiler_params=_cparams(56, ("arbitrary",)),
    )(at, b)


GW_TILE = 256


def _grad_w_in(h1t, pieces):
    ka, s = h1t.shape
    widths = [p.shape[1] for p in pieces]
    assert all(w % GW_TILE == 0 for w in widths)
    first = [sum(widths[:i]) // GW_TILE for i in range(len(pieces))]
    count = [w // GW_TILE for w in widths]

    def body(a_ref, *refs):
        o_ref = refs[-1]
        j = pl.program_id(0)
        for ref, f0, n in zip(refs[:-1], first, count):
            @pl.when((j >= f0) & (j < f0 + n))
            def _(ref=ref):
                o_ref[...] = jnp.dot(a_ref[...], ref[...], preferred_element_type=F32).astype(BF16)

    def spec(f0, n):
        return pl.BlockSpec((s, GW_TILE), lambda j: (0, jnp.clip(j - f0, 0, n - 1)))

    return pl.pallas_call(
        body, name="grad_w_in", grid=(sum(count),),
        in_specs=[_resident((ka, s))] + [spec(f0, n) for f0, n in zip(first, count)],
        out_specs=pl.BlockSpec((ka, GW_TILE), lambda j: (0, j)),
        out_shape=jax.ShapeDtypeStruct((ka, sum(widths)), BF16),
        compiler_params=_cparams(56, ("arbitrary",)),
    )(h1t, *pieces)


def _grad_matmul_blocks(a, b, *, ts, name):
    nblk = a.shape[0] if a.ndim == 3 else b.shape[0]
    s = a.shape[-2]
    ka, nb = a.shape[-1], b.shape[-1]
    nk = s // ts

    def body(a_ref, b_ref, o_ref, acc):
        k = pl.program_id(1)

        @pl.when(k == 0)
        def _():
            acc[...] = jnp.zeros_like(acc)

        av = a_ref[0] if a.ndim == 3 else a_ref[...]
        bv = b_ref[0] if b.ndim == 3 else b_ref[...]
        acc[...] += lax.dot_general(av, bv, TN, preferred_element_type=F32)

        @pl.when(k == nk - 1)
        def _():
            o_ref[0] = acc[...].astype(BF16)

    def spec(arr, width):
        if arr.ndim == 3:
            return pl.BlockSpec((1, ts, width), lambda j, k: (j, k, 0))
        return pl.BlockSpec((ts, width), lambda j, k: (k, 0))

    return pl.pallas_call(
        body, name=name, grid=(nblk, nk),
        in_specs=[spec(a, ka), spec(b, nb)],
        out_specs=pl.BlockSpec((1, ka, nb), lambda j, k: (j, 0, 0)),
        out_shape=jax.ShapeDtypeStruct((nblk, ka, nb), BF16),
        scratch_shapes=[pltpu.VMEM((ka, nb), F32)],
        compiler_params=_cparams(48, ("arbitrary", "arbitrary")),
    )(a, b)


def _ffn_bwd_in(dgu, wgu, x2, g_pre, dx3, y, g_post, *, tm):
    s = x2.shape[0]

    def body(dgu_ref, w_ref, x2_ref, gpre_ref, dx3_ref, y_ref, gpost_ref,
             dx2_ref, dy_ref, dgpre_ref, dgpost_ref):
        @pl.when(pl.program_id(0) == 0)
        def _():
            dgpre_ref[...] = jnp.zeros_like(dgpre_ref)
            dgpost_ref[...] = jnp.zeros_like(dgpost_ref)

        dh2 = None
        for a in range(2):
            for j in range(4):
                part = lax.dot_general(dgu_ref[a, j], w_ref[a, j], NT, preferred_element_type=F32)
                dh2 = part if dh2 is None else dh2 + part
        _, n2, r2 = _rms_fwd(x2_ref[...], gpre_ref[...])
        dxn, dg = _rms_bwd(dh2, n2, r2, gpre_ref[...])
        dgpre_ref[...] += _fold8(dg)
        dx2 = dx3_ref[...] + dxn
        dx2_ref[...] = dx2
        _, ny, ry = _rms_fwd(y_ref[...], gpost_ref[...])
        dy, dg2 = _rms_bwd(dx2, ny, ry, gpost_ref[...])
        dy_ref[...] = dy.astype(BF16)
        dgpost_ref[...] += _fold8(dg2)

    return pl.pallas_call(
        body, name="ffn_bwd_in", grid=(s // tm,),
        in_specs=[pl.BlockSpec((2, 4, tm, FB), lambda i: (0, 0, i, 0)), _resident((2, 4, D, FB)), _rows(tm, D),
                  _full((1, D)), _rows(tm, D), _rows(tm, D), _full((1, D))],
        out_specs=[_rows(tm, D), _rows(tm, D), _full((SUBLANES, D)), _full((SUBLANES, D))],
        out_shape=[jax.ShapeDtypeStruct((s, D), F32), jax.ShapeDtypeStruct((s, D), BF16),
                   jax.ShapeDtypeStruct((SUBLANES, D), F32), jax.ShapeDtypeStruct((SUBLANES, D), F32)],
        compiler_params=_cparams(56, ("arbitrary",)),
    )(dgu, wgu, x2, g_pre, dx3, y, g_post)


def _mix_bwd(dy, w_out, o, cv, bcu, ga, gc, gsum, *, tm):
    s = dy.shape[0]

    def group_norm_bwd(dn_out, v, g, gs):
        r = lax.rsqrt(_split_dot(v * v, gs) * (1.0 / DH) + EPS)
        n = v * r
        dn = dn_out * g
        return r * (dn - n * (_split_dot(dn * n, gs) * (1.0 / DH))), dn_out * n

    def body(dy_ref, w_ref, o_ref, cv_ref, bcu_ref, ga_ref, gc_ref, gs_ref,
             do_ref, dl_ref, dcv_ref, db_ref, dga_ref, dgc_ref):
        @pl.when(pl.program_id(0) == 0)
        def _():
            dga_ref[...] = jnp.zeros_like(dga_ref)
            dgc_ref[...] = jnp.zeros_like(dgc_ref)

        dm = lax.dot_general(dy_ref[...], w_ref[...], NT, preferred_element_type=F32)
        ov = o_ref[...]
        do, dga = group_norm_bwd(dm[:, 0:AW], ov, ga_ref[...], gs_ref[...])
        dob = do.astype(BF16)
        do_ref[...] = dob
        dl_ref[...] = _split_dot(dob.astype(F32) * ov, gs_ref[...])
        dga_ref[...] += _fold8(dga)
        gate_b = bcu_ref[:, 0:CW]
        cv = cv_ref[...]
        dconv, dgc = group_norm_bwd(dm[:, AW:D], gate_b * cv, gc_ref[...], gs_ref[...])
        dgc_ref[...] += _fold8(dgc)
        dcv_ref[...] = dconv * gate_b
        db_ref[...] = (dconv * cv).astype(BF16)

    return pl.pallas_call(
        body, name="mix_bwd", grid=(s // tm,),
        in_specs=[_rows(tm, D), _resident((D, D)), _rows(tm, AW), _rows(tm, CW), _rows(tm, 3 * CW),
                  _full((1, AW)), _full((1, CW)), _full((CW, CW))],
        out_specs=[_rows(tm, AW), _rows(tm, AW), _rows(tm, CW), _rows(tm, CW),
                   _full((SUBLANES, AW)), _full((SUBLANES, CW))],
        out_shape=[jax.ShapeDtypeStruct((s, AW), BF16), jax.ShapeDtypeStruct((s, AW), F32),
                   jax.ShapeDtypeStruct((s, CW), F32), jax.ShapeDtypeStruct((s, CW), BF16),
                   jax.ShapeDtypeStruct((SUBLANES, AW), F32), jax.ShapeDtypeStruct((SUBLANES, CW), F32)],
        compiler_params=_cparams(48, ("arbitrary",)),
    )(dy, w_out, o, cv, bcu, ga, gc, gsum)


def _conv_bwd(dcv, db, bcu, cw8, *, tm):
    s = dcv.shape[0]
    nt = s // tm

    def body(dcv_ref, nxt_ref, db_ref, bcu_ref, halo_ref, cw_ref, dbcu_ref, dw_ref):
        i = pl.program_id(0)

        @pl.when(i == 0)
        def _():
            dw_ref[...] = jnp.zeros_like(dw_ref)

        z, z1, z2 = _conv_taps(bcu_ref, halo_ref, i == 0, tm)
        d = dcv_ref[...]
        dw_ref[0] += _fold8(d * z2)
        dw_ref[1] += _fold8(d * z1)
        dw_ref[2] += _fold8(d * z)
        nx = jnp.where(i == nt - 1, 0.0, nxt_ref[...])
        row = lax.broadcasted_iota(jnp.int32, (tm, CW), 0)
        d1 = jnp.where(row == tm - 1, nx[0:1, :], pltpu.roll(d, tm - 1, axis=0))
        d2 = jnp.where(row == tm - 2, nx[0:1, :], jnp.where(row == tm - 1, nx[1:2, :], pltpu.roll(d, tm - 2, axis=0)))
        dz = cw_ref[2:3, :] * d + cw_ref[1:2, :] * d1 + cw_ref[0:1, :] * d2
        dbcu_ref[:, 0:CW] = db_ref[...]
        dbcu_ref[:, CW:2 * CW] = (dz * bcu_ref[:, 2 * CW:3 * CW]).astype(BF16)
        dbcu_ref[:, 2 * CW:3 * CW] = (dz * bcu_ref[:, CW:2 * CW]).astype(BF16)

    return pl.pallas_call(
        body, name="conv_bwd", grid=(nt,),
        in_specs=[_rows(tm, CW),
                  pl.BlockSpec((SUBLANES, CW), lambda i: (jnp.minimum((i + 1) * (tm // SUBLANES), s // SUBLANES - 1), 0)),
                  _rows(tm, CW), _rows(tm, 3 * CW), _halo_before(tm, 3 * CW), _full((SUBLANES, CW))],
        out_specs=[_rows(tm, 3 * CW), _full((3, SUBLANES, CW))],
        out_shape=[jax.ShapeDtypeStruct((s, 3 * CW), BF16), jax.ShapeDtypeStruct((3, SUBLANES, CW), F32)],
        compiler_params=_cparams(48, ("arbitrary",)),
    )(dcv, dcv, db, bcu, bcu, cw8)


def _attn_bwd(qp, kp, v, do, lse, dl, mk, *, t):
    s = qp.shape[0]
    nq = s // t

    def body(q_ref, k_ref, v_ref, do_ref, lse_ref, dl_ref, mk_ref, dq_ref, dk_ref, dv_ref, dkx_ref, dq_acc):
        ki = pl.program_id(1)

        @pl.when(ki == 0)
        def _():
            dq_acc[...] = jnp.zeros_like(dq_acc)

        row = lax.broadcasted_iota(jnp.int32, (t, t), 0)
        col = lax.broadcasted_iota(jnp.int32, (t, t), 1)
        lane = lax.broadcasted_iota(jnp.int32, (t, 128), 1)

        def head_step(hh, qi, carry, masked):
            dk, dv, cs = carry
            off = pl.multiple_of(qi * t, t)
            rows = pl.ds(off, t)
            kh = k_ref[:, HP * hh:HP * (hh + 1)]
            q = q_ref[rows, HP * hh:HP * (hh + 1)]
            in_head = (lane >= DH * hh) & (lane < DH * (hh + 1))
            m_col = mk_ref[0, rows, DH * hh:DH * hh + 1]
            scale = jnp.exp(m_col - lse_ref[rows, DH * hh:DH * hh + 1])
            dom = jnp.where(in_head, do_ref[rows, :], jnp.zeros((), BF16))
            sc = lax.dot_general(q, kh, NT, preferred_element_type=F32) - m_col
            if masked:
                sc = jnp.where(col <= row, sc, -1e30)
            pt = jnp.exp(sc).astype(BF16)
            dp = lax.dot_general(dom, v_ref[...], NT, preferred_element_type=F32)
            ds32 = (pt.astype(F32) * scale) * (dp - dl_ref[rows, DH * hh:DH * hh + 1])
            ds = ds32.astype(BF16)
            cs = cs + _fold8(ds32)
            dv = dv + lax.dot_general(pt, (dom.astype(F32) * scale).astype(BF16), TN, preferred_element_type=F32)
            dk = dk + lax.dot_general(ds, q, TN, preferred_element_type=F32)
            dq_acc[rows, HP * hh:HP * (hh + 1)] += jnp.dot(ds, kh, preferred_element_type=F32)
            return dk, dv, cs

        def step(qi, carry, masked):
            return tuple(head_step(hh, qi, carry[hh], masked) for hh in range(2))

        zero = (jnp.zeros((t, HP), F32), jnp.zeros((t, 128), F32), jnp.zeros((SUBLANES, t), F32))
        carry = step(ki, (zero, zero), True)
        (dk0, dv0, cs0), (dk1, dv1, cs1) = lax.fori_loop(ki + 1, nq, functools.partial(step, masked=False), carry)
        dk_ref[:, 0:HP] = dk0.astype(BF16)
        dk_ref[:, HP:2 * HP] = dk1.astype(BF16)
        dv_ref[...] = (dv0 + dv1).astype(BF16)

        def as_column(cs):
            return lax.dot_general(cs, jnp.ones((SUBLANES, 128), F32), TN, precision=HIGHEST, preferred_element_type=F32)

        dkx_ref[...] = jnp.where(lane < DH, as_column(cs0), as_column(cs1))

        @pl.when(ki == nq - 1)
        def _():
            dq_ref[...] = dq_acc[...].astype(BF16)

    return pl.pallas_call(
        body, name="attn_bwd", grid=(H // 2, nq),
        in_specs=[pl.BlockSpec((s, 2 * HP), lambda p, i: (0, p)),
                  pl.BlockSpec((t, 2 * HP), lambda p, i: (i, p)),
                  pl.BlockSpec((t, 128), lambda p, i: (i, p)),
                  pl.BlockSpec((s, 128), lambda p, i: (0, p)),
                  pl.BlockSpec((s, 128), lambda p, i: (0, p)),
                  pl.BlockSpec((s, 128), lambda p, i: (0, p)),
                  pl.BlockSpec((1, s, 128), lambda p, i: (i, 0, p))],
        out_specs=[pl.BlockSpec((s, 2 * HP), lambda p, i: (0, p)),
                   pl.BlockSpec((t, 2 * HP), lambda p, i: (i, p)),
                   pl.BlockSpec((t, 128), lambda p, i: (i, p)),
                   pl.BlockSpec((t, 128), lambda p, i: (i, p))],
        out_shape=[jax.ShapeDtypeStruct((s, 1024), BF16), jax.ShapeDtypeStruct((s, 1024), BF16),
                   jax.ShapeDtypeStruct((s, AW), BF16), jax.ShapeDtypeStruct((s, AW), F32)],
        scratch_shapes=[pltpu.VMEM((s, 2 * HP), F32)],
        compiler_params=_cparams(56, ("arbitrary", "arbitrary")),
    )(qp, kp, v, do, lse, dl, mk)


def _forget_bwd(dkx, z, sel, *, tm):
    s = dkx.shape[0]
    nt = s // tm

    def body(dk_ref, z_ref, sel_ref, dfl_ref, dbf_ref, carry):
        @pl.when(pl.program_id(0) == 0)
        def _():
            carry[...] = jnp.zeros_like(carry)
            dbf_ref[...] = jnp.zeros_like(dbf_ref)

        dc = _split_dot(dk_ref[...], sel_ref[...])
        row = lax.broadcasted_iota(jnp.int32, (tm, tm), 0)
        col = lax.broadcasted_iota(jnp.int32, (tm, tm), 1)
        tri = (col >= row).astype(F32)
        dlogf = jnp.dot(tri, dc, precision=HIGHEST, preferred_element_type=F32) + carry[0:1, :]
        carry[...] = jnp.broadcast_to(dlogf[0:1, :], carry.shape)
        dz = dlogf * (1.0 - jax.nn.sigmoid(z_ref[...]))
        dfl_ref[:, 0:128] = dz.astype(BF16)
        dfl_ref[:, 128:GW_TILE] = jnp.zeros((tm, GW_TILE - 128), BF16)
        dbf_ref[...] += _fold8(dz)

    rev = lambda i: (nt - 1 - i, 0)
    return pl.pallas_call(
        body, name="forget_bwd", grid=(nt,),
        in_specs=[pl.BlockSpec((tm, AW), rev), pl.BlockSpec((tm, 128), rev), _full((AW, 128))],
        out_specs=[pl.BlockSpec((tm, GW_TILE), rev), _full((SUBLANES, 128))],
        out_shape=[jax.ShapeDtypeStruct((s, GW_TILE), BF16), jax.ShapeDtypeStruct((SUBLANES, 128), F32)],
        scratch_shapes=[pltpu.VMEM((SUBLANES, 128), F32)],
        compiler_params=_cparams(48, ("arbitrary",)),
    )(dkx, z, sel)


def _in_proj_bwd(pieces, wp, x, g1, dx2, *, tm):
    s = x.shape[0]

    def body(q_ref, k_ref, v_ref, bcu_ref, f_ref, w_ref, x_ref, g_ref, dx2_ref, dx_ref, dg_ref):
        @pl.when(pl.program_id(0) == 0)
        def _():
            dg_ref[...] = jnp.zeros_like(dg_ref)

        dh = None
        for ref, (lo, hi) in zip((q_ref, k_ref, v_ref, bcu_ref, f_ref), PIECES):
            part = lax.dot_general(ref[...], w_ref[:, lo:hi], NT, preferred_element_type=F32)
            dh = part if dh is None else dh + part
        _, n, r = _rms_fwd(x_ref[...], g_ref[...])
        dxn, dg = _rms_bwd(dh, n, r, g_ref[...])
        dx_ref[...] = dx2_ref[...] + dxn
        dg_ref[...] += _fold8(dg)

    return pl.pallas_call(
        body, name="in_proj_bwd", grid=(s // tm,),
        in_specs=[_rows(tm, hi - lo) for lo, hi in PIECES] + [_resident((D, WP)), _rows(tm, D), _full((1, D)), _rows(tm, D)],
        out_specs=[_rows(tm, D), _full((SUBLANES, D))],
        out_shape=[jax.ShapeDtypeStruct((s, D), F32), jax.ShapeDtypeStruct((SUBLANES, D), F32)],
        compiler_params=_cparams(56, ("arbitrary",)),
    )(*pieces, wp, x, g1, dx2)


def _position():
    return lax.axis_index("x"), lax.axis_index("y"), lax.axis_index("c")


ANY = pl.BlockSpec(memory_space=pl.ANY)


def _all_gather(shards):
    n = len(shards)

    def body(*refs):
        x_refs, out_refs = refs[:n], refs[n:2 * n]
        send_sems, recv_sems, local_sems = refs[2 * n:]
        x, y, c = _position()
        me, sibling = (x, y, c), (x, y, 1 - c)
        chips = [(1 - x, y), (x, 1 - y), (1 - x, 1 - y)]

        def copy(a, k, block, to, own=False):
            slot = out_refs[a].at[4 * block[0] + 2 * block[1] + block[2]]
            return pltpu.make_async_remote_copy(
                src_ref=x_refs[a] if own else slot, dst_ref=slot,
                send_sem=send_sems.at[7 * a + k], recv_sem=recv_sems.at[7 * a + k], device_id=to, device_id_type=MESH_ID)

        mine = [pltpu.make_async_copy(x_refs[a], out_refs[a].at[4 * x + 2 * y + c], local_sems.at[a]) for a in range(n)]
        for cp in mine:
            cp.start()
        first = []
        for a in range(n):
            first.append(copy(a, 0, me, sibling, own=True))
            first += [copy(a, 1 + j, me, (*chip, c), own=True) for j, chip in enumerate(chips)]
        for cp in first:
            cp.start()
        passed = []
        for j, chip in enumerate(chips):
            for a in range(n):
                copy(a, 1 + j, (*chip, c), me).wait_recv()
                fwd = copy(a, 4 + j, (*chip, c), sibling)
                fwd.start()
                passed.append(fwd)
        for a in range(n):
            copy(a, 0, sibling, me).wait_recv()
            for j, chip in enumerate(chips):
                copy(a, 4 + j, (*chip, 1 - c), me).wait_recv()
        for cp in first + passed:
            cp.wait_send()
        for cp in mine:
            cp.wait()

    return pl.pallas_call(
        body, name="all_gather_weights",
        out_shape=[jax.ShapeDtypeStruct((NDEV,) + sh.shape, sh.dtype) for sh in shards],
        in_specs=[ANY] * n, out_specs=[ANY] * n,
        scratch_shapes=[pltpu.SemaphoreType.DMA((7 * n,)), pltpu.SemaphoreType.DMA((7 * n,)), pltpu.SemaphoreType.DMA((n,))],
    )(*shards)


def _pair_exchange(grads):
    n = len(grads)

    def body(*refs):
        g_refs, out_refs = refs[:n], refs[n:2 * n]
        send_sems, recv_sems = refs[2 * n:]
        x, y, c = _position()
        copies = [pltpu.make_async_remote_copy(
            src_ref=g_refs[a].at[:, pl.ds(1 - c, 1)], dst_ref=out_refs[a], send_sem=send_sems.at[a],
            recv_sem=recv_sems.at[a], device_id=(x, y, 1 - c), device_id_type=MESH_ID) for a in range(n)]
        for cp in copies:
            cp.start()
        for cp in copies:
            cp.wait()

    return pl.pallas_call(
        body, name="grad_pair_exchange",
        out_shape=[jax.ShapeDtypeStruct((4, 1) + g.shape[2:], g.dtype) for g in grads],
        in_specs=[ANY] * n, out_specs=[ANY] * n,
        scratch_shapes=[pltpu.SemaphoreType.DMA((n,)), pltpu.SemaphoreType.DMA((n,))],
    )(*grads)


def _pair_sum(g, got, idx, *, tr, name):
    r, c = g.shape[2:]

    def body(idx_ref, g_ref, got_ref, pb_ref, own_ref):
        p = g_ref[0, 0].astype(F32) + got_ref[0, 0].astype(F32)
        pb_ref[0] = p.astype(BF16)

        @pl.when(pl.program_id(1) == idx_ref[1])
        def _():
            own_ref[...] = p

    return pl.pallas_call(
        body, name=name,
        grid_spec=pltpu.PrefetchScalarGridSpec(
            num_scalar_prefetch=1, grid=(r // tr, 4),
            in_specs=[pl.BlockSpec((1, 1, tr, c), lambda i, j, idx: (j, idx[0], i, 0)),
                      pl.BlockSpec((1, 1, tr, c), lambda i, j, idx: (j, 0, i, 0))],
            out_specs=[pl.BlockSpec((1, tr, c), lambda i, j, idx: (j, i, 0)),
                       pl.BlockSpec((tr, c), lambda i, j, idx: (i, 0))]),
        out_shape=[jax.ShapeDtypeStruct((4, r, c), BF16), jax.ShapeDtypeStruct((r, c), F32)],
        compiler_params=_cparams(32, ("arbitrary", "arbitrary")),
    )(idx, g, got)


HBM = pl.BlockSpec(memory_space=pltpu.HBM)
SEM = pl.BlockSpec(memory_space=pltpu.SEMAPHORE)
DATAFLOW = pltpu.SideEffectType.DATAFLOW_SIDE_EFFECTING


PEERS = {"gather": NDEV - 1, "scatter": NDEV - 1, "chips": 3}


def _exchange_copies(src_refs, land_refs, send_sems, recv_sems, mode):
    x, y, c = _position()
    me, my_chip = 4 * x + 2 * y + c, 2 * x + y
    npeers = PEERS[mode]
    copies = []
    for a, (s_ref, l_ref) in enumerate(zip(src_refs, land_refs)):
        for k in range(npeers):
            if mode == "chips":
                px, py, pc = x ^ ((k + 1) >> 1), y ^ ((k + 1) & 1), c
                src, dst = s_ref.at[2 * px + py], l_ref.at[my_chip]
            else:
                px, py, pc = x ^ ((k + 1) >> 2), y ^ (((k + 1) >> 1) & 1), c ^ ((k + 1) & 1)
                src, dst = (s_ref.at[4 * px + 2 * py + pc] if mode == "scatter" else s_ref), l_ref.at[me]
            copies.append(pltpu.make_async_remote_copy(
                src_ref=src, dst_ref=dst, send_sem=send_sems.at[npeers * a + k], recv_sem=recv_sems.at[npeers * a + k],
                device_id=(px, py, pc), device_id_type=MESH_ID))
    return copies


def _exchange_start(srcs, lands, *, mode, name):
    n = len(srcs)
    nsem = PEERS[mode] * n

    def body(*refs):
        token = refs[-1]
        for cp in _exchange_copies(refs[:n], refs[n:2 * n], refs[2 * n], refs[2 * n + 1], mode):
            cp.start()
        token[...] = jnp.zeros_like(token)

    arrays = list(srcs) + list(lands)
    outs = pl.pallas_call(
        body, name=name,
        out_shape=(pltpu.SemaphoreType.DMA((nsem,)), pltpu.SemaphoreType.DMA((nsem,)),
                   *[pltpu.HBM(a.shape, a.dtype) for a in arrays], jax.ShapeDtypeStruct((SUBLANES, LANES), F32)),
        in_specs=[HBM] * (2 * n),
        out_specs=(SEM, SEM, *[HBM] * (2 * n), pl.BlockSpec(memory_space=pltpu.VMEM)),
        input_output_aliases={i: 2 + i for i in range(2 * n)},
        compiler_params=pltpu.CompilerParams(has_side_effects=DATAFLOW),
    )(*[pltpu.with_memory_space_constraint(a, pltpu.HBM) for a in arrays])
    return outs[0], outs[1], outs[2:2 + n], outs[2 + n:2 + 2 * n], outs[-1]


def _exchange_wait(send_sems, recv_sems, srcs, lands, after, *, mode, name):
    n = len(srcs)

    def body(*refs):
        for cp in _exchange_copies(refs[:n], refs[n:2 * n], refs[2 * n], refs[2 * n + 1], mode):
            cp.wait_send()
            cp.wait_recv()

    arrays = list(srcs) + list(lands)
    outs = pl.pallas_call(
        body, name=name,
        out_shape=tuple(pltpu.HBM(a.shape, a.dtype) for a in arrays),
        in_specs=[HBM] * (2 * n) + [SEM, SEM, ANY],
        out_specs=tuple([HBM] * (2 * n)),
        input_output_aliases={i: i for i in range(2 * n)},
        compiler_params=pltpu.CompilerParams(has_side_effects=DATAFLOW),
    )(*arrays, send_sems, recv_sems, after)
    return outs[n:]


def _own_slot(value, me):
    return lax.dynamic_update_index_in_dim(lax.empty((NDEV,) + value.shape, value.dtype), value, me, 0)


def _small_all_reduce(parts):
    def body(gmp_ref, gmo_ref, gfp_ref, gfo_ref, ga_ref, gc_ref, dw_ref, bf_ref, loss_ref,
             out_ref, buf, send_sems, recv_sems):
        x, y, c = _position()
        me = 4 * x + 2 * y + c

        def colsum(v):
            return jnp.sum(v, axis=0, keepdims=True)

        loss = jnp.sum(colsum(loss_ref[...]), axis=1, keepdims=True) * (0.5 / D)
        rows = [colsum(gmp_ref[...]), colsum(gmo_ref[...]), colsum(gfp_ref[...]), colsum(gfo_ref[...]),
                jnp.concatenate([colsum(ga_ref[...]), colsum(gc_ref[...])], axis=1),
                jnp.concatenate([colsum(dw_ref[0]), colsum(dw_ref[1])], axis=1),
                jnp.concatenate([colsum(dw_ref[2]), colsum(bf_ref[...]), jnp.broadcast_to(loss, (1, 128)),
                                 jnp.zeros((1, 256), F32)], axis=1),
                jnp.zeros((1, D), F32)]
        buf[me] = jnp.concatenate(rows, axis=0)
        copies = []
        for mm in range(1, NDEV):
            peer = (x ^ (mm >> 2), y ^ ((mm >> 1) & 1), c ^ (mm & 1))
            copies.append(pltpu.make_async_remote_copy(
                src_ref=buf.at[me], dst_ref=buf.at[me], send_sem=send_sems.at[mm - 1], recv_sem=recv_sems.at[mm - 1],
                device_id=peer, device_id_type=MESH_ID))
        for cp in copies:
            cp.start()
        for cp in copies:
            cp.wait_recv()
        for cp in copies:
            cp.wait_send()
        acc = buf[0]
        for d in range(1, NDEV):
            acc = acc + buf[d]
        out_ref[...] = acc

    vm = pl.BlockSpec(memory_space=pltpu.VMEM)
    return pl.pallas_call(
        body, name="small_all_reduce",
        out_shape=jax.ShapeDtypeStruct((SUBLANES, D), F32),
        in_specs=[vm] * len(parts), out_specs=vm,
        scratch_shapes=[pltpu.VMEM((NDEV, SUBLANES, D), F32), pltpu.SemaphoreType.DMA((7,)), pltpu.SemaphoreType.DMA((7,))],
    )(*parts)


def _adam_update(w, g, m, v):
    nm = ADAM_B1 * m + (1.0 - ADAM_B1) * g
    nv = ADAM_B2 * v + (1.0 - ADAM_B2) * (g * g)
    m_hat = nm / (1.0 - ADAM_B1 ** ADAM_STEP)
    v_hat = nv / (1.0 - ADAM_B2 ** ADAM_STEP)
    return -ADAM_LR * (m_hat / (jnp.sqrt(v_hat) + ADAM_EPS) + ADAM_WD * w), nm, nv


def _adamw(w, g, m, v, *, tr, name):
    rows, cols = w.shape

    def body(w_ref, g_ref, m_ref, v_ref, d_ref, nm_ref, nv_ref):
        d_ref[...], nm_ref[...], nv_ref[...] = _adam_update(w_ref[...], g_ref[...], m_ref[...], v_ref[...])

    spec = pl.BlockSpec((tr, cols), lambda i: (i, 0))
    return pl.pallas_call(
        body, name=name, grid=(rows // tr,),
        in_specs=[spec] * 4, out_specs=[spec] * 3,
        out_shape=[jax.ShapeDtypeStruct((rows, cols), F32)] * 3,
        compiler_params=_cparams(32, ("arbitrary",)),
    )(w, g, m, v)


def _chip_sum_adamw(got, own, idx, w, m, v, *, tr, name):
    rows, cols = w.shape

    def body(idx_ref, got_ref, own_ref, w_ref, m_ref, v_ref, g_ref, d_ref, nm_ref, nv_ref):
        g = jnp.zeros((tr, cols), F32)
        for j in range(4):
            g = g + jnp.where(idx_ref[1] == j, own_ref[...], got_ref[j].astype(F32))
        g_ref[...] = g
        d_ref[...], nm_ref[...], nv_ref[...] = _adam_update(w_ref[...], g, m_ref[...], v_ref[...])

    spec = pl.BlockSpec((tr, cols), lambda i, idx: (i, 0))
    return pl.pallas_call(
        body, name=name,
        grid_spec=pltpu.PrefetchScalarGridSpec(
            num_scalar_prefetch=1, grid=(rows // tr,),
            in_specs=[pl.BlockSpec((4, tr, cols), lambda i, idx: (0, i, 0)), spec, spec, spec, spec],
            out_specs=[spec] * 4),
        out_shape=[jax.ShapeDtypeStruct((rows, cols), F32)] * 4,
        compiler_params=_cparams(32, ("arbitrary",)),
    )(idx, got, own, w, m, v)


def _device_sum_adamw(land, w, m, v, *, tr, name):
    rows, cols = w.shape

    def body(land_ref, w_ref, m_ref, v_ref, g_ref, d_ref, nm_ref, nv_ref):
        g = land_ref[0].astype(F32)
        for dev in range(1, NDEV):
            g = g + land_ref[dev].astype(F32)
        g_ref[...] = g
        d_ref[...], nm_ref[...], nv_ref[...] = _adam_update(w_ref[...], g, m_ref[...], v_ref[...])

    spec = pl.BlockSpec((tr, cols), lambda i: (i, 0))
    return pl.pallas_call(
        body, name=name, grid=(rows // tr,),
        in_specs=[pl.BlockSpec((NDEV, tr, cols), lambda i: (0, i, 0)), spec, spec, spec],
        out_specs=[spec] * 4,
        out_shape=[jax.ShapeDtypeStruct((rows, cols), F32)] * 4,
        compiler_params=_cparams(32, ("arbitrary",)),
    )(land, w, m, v)


def _placement_constants():
    j = jnp.arange(128)[:, None]
    lane = jnp.arange(1024)[None, :]
    head, sub = lane // HP, lane % HP
    piece, jh = j // H, j % H
    valid = (j < 3 * H) & (jh == head)
    pq = jnp.where(valid & (sub == DH + piece), 1.0, 0.0).astype(BF16)
    pk = jnp.where(valid & (sub == DH + 3 + piece), -1.0, 0.0).astype(BF16)
    oq = jnp.where((sub >= DH + 3) & (sub < DH + 6), 1.0, 0.0).astype(F32)
    ok = jnp.where((sub >= DH) & (sub < DH + 3), 1.0, 0.0).astype(F32)
    r = jnp.arange(AW)[:, None]
    cc = jnp.arange(128)[None, :]
    sel = jnp.where((r % DH == 3) & (r // DH == cc), -1.0, 0.0).astype(BF16)
    gi = jnp.arange(CW)
    gsum = (gi[:, None] // DH == gi[None, :] // DH).astype(BF16)
    return pq, pk, oq, ok, sel, gsum


def _local_step(xs, tgt, wp, late_weights, cw8, bfp, g_attn_out, g_conv_out,
                g_mix_pre, g_mix_post, g_ffn_pre, g_ffn_post, early_grads=None, last_grad=None):
    pq, pk, oq, ok, sel, gsum = _placement_constants()
    h1t, qp, kp, vv, bcu, zf = _in_proj(xs, g_mix_pre, wp, bfp, pq, pk, oq, ok, tm=512)
    o, lse, mk = _attn_fwd(qp, kp, vv, t=512)
    w_out_f, wgu, wd = late_weights(lse)
    merged, y, x2, cv, h2, h2t = _mix_out(o, bcu, cw8, g_attn_out, g_conv_out, gsum, w_out_f, xs, g_mix_post, g_ffn_pre,
                                          tm=512)
    gate, up, act = _ffn_up(h2, wgu, tm=512)
    dx3, dff, loss_p, dg_ffn_post = _ffn_down_loss(act, wd, x2, tgt, g_ffn_post, tm=512)

    dgu = _ffn_bwd_act(dff, wd, gate, up, tm=512)
    dw_down = _grad_matmul_blocks(act, dff, ts=512, name="grad_w_down")
    dw_gu = _grad_matmul_t(h2t, dgu.reshape(NDEV, -1, FB), tb=FB, name="grad_w_gate_up")
    dx2, dy, dg_ffn_pre, dg_mix_post = _ffn_bwd_in(dgu, wgu, x2, g_ffn_pre, dx3, y, g_mix_post, tm=256)
    dw_out = _grad_matmul(merged, dy, ta=1024, tb=1024, ts=512, name="grad_w_out")
    token = early_grads(dw_out, dw_gu, dw_down) if early_grads is not None else None
    ga = g_attn_out if token is None else g_attn_out + token[0:1, 0:1]
    do, dl, dcv, db, dg_attn, dg_conv = _mix_bwd(dy, w_out_f, o, cv, bcu, ga, g_conv_out, gsum, tm=512)
    dbcu, dtaps = _conv_bwd(dcv, db, bcu, cw8, tm=512)
    dqp, dkp, dv, dkx = _attn_bwd(qp, kp, vv, do, lse, dl, mk, t=512)
    dfl, dbf = _forget_bwd(dkx, zf, sel, tm=512)
    pieces = (dqp, dkp, dv, dbcu, dfl)
    dwp = _grad_w_in(h1t, pieces)
    token = last_grad(dwp) if last_grad is not None else None
    g1 = g_mix_pre if token is None else g_mix_pre + token[0:1, 0:1]
    grad_x, dg_mix_pre = _in_proj_bwd(pieces, wp, xs, g1, dx2, tm=512)
    return (grad_x, dwp, dw_out, dw_gu, dw_down, dg_mix_pre, dg_mix_post, dg_ffn_pre, dg_ffn_post, dg_attn, dg_conv,
            dtaps, dbf, loss_p)


BIG_TILES = {"w_in": 256, "w_out": 128, "w_gate_up": 256, "w_down": 176}


def kernel(x, w_in, b_forget, conv_w, g_attn_out, g_conv_out, w_out, g_mix_pre, g_mix_post, w_gate_up, w_down, g_ffn_pre, g_ffn_post, loss_target, m_w_in, m_b_forget, m_conv_w, m_g_attn_out, m_g_conv_out, m_w_out, m_g_mix_pre, m_g_mix_post, m_w_gate_up, m_w_down, m_g_ffn_pre, m_g_ffn_post, v_w_in, v_b_forget, v_conv_w, v_g_attn_out, v_g_conv_out, v_w_out, v_g_mix_pre, v_g_mix_post, v_w_gate_up, v_w_down, v_g_ffn_pre, v_g_ffn_post):
    xc, yc, cc = _position()
    my_chip = 2 * xc + yc
    me = 2 * my_chip + cc
    idx = jnp.stack([cc, my_chip]).astype(jnp.int32)
    tables = _in_layout_tables()
    pad_in = lambda a: jnp.pad(a, ((0, 0), (0, IN_PAD - IN_COLS)))

    g_in, g_taps = _all_gather([pad_in(w_in[0]).astype(BF16), conv_w[0]])
    wp = _assemble_w_in(g_in, tables, tr=256)
    cw8 = jnp.pad(g_taps.transpose(1, 0, 2).reshape(3, CW), ((0, SUBLANES - 3), (0, 0)))

    late = [w_out[0].astype(BF16), w_gate_up[0].astype(BF16), w_down[0].astype(BF16)]
    ssem, rsem, late_thru, land_thru, token = _exchange_start(
        late, [_own_slot(s, me) for s in late], mode="gather", name="gather_late_start")
    bfp = jnp.pad(b_forget, ((0, 0), (0, 128 - H))) + token[0:1, :]

    def late_weights(after):
        l_out, l_gu, l_down = _exchange_wait(ssem, rsem, late_thru, land_thru, after, mode="gather", name="gather_late_wait")
        return l_out.reshape(D, D), l_gu.reshape(2, 4, D, FB), l_down.reshape(4, FB, D)

    early = {}

    def early_grads(dw_out, dw_gu, dw_down):
        srcs = [dw_out.reshape(NDEV, D // NDEV, D), dw_gu, dw_down.reshape(NDEV, DFF // NDEV, D)]
        lands = [_own_slot(lax.dynamic_index_in_dim(s, me, 0, keepdims=False), me) for s in srcs]
        early["handles"] = _exchange_start(srcs, lands, mode="scatter", name="scatter_early_start")
        return early["handles"][4]

    last = {}

    def last_grad(dwp):
        g_w_in = _disassemble_w_in(dwp, tables, tr=256).reshape(4, 2, D, IN_PAD)
        (from_sibling,) = _pair_exchange([g_w_in])
        pair_b, last["own"] = _pair_sum(g_w_in, from_sibling, idx, tr=BIG_TILES["w_in"], name="grad_pair_sum_w_in")
        land = lax.dynamic_update_index_in_dim(lax.empty(pair_b.shape, pair_b.dtype),
                                               lax.dynamic_index_in_dim(pair_b, my_chip, 0, keepdims=False), my_chip, 0)
        last["handles"] = _exchange_start([pair_b], [land], mode="chips", name="chips_w_in_start")
        return last["handles"][4]

    (grad_x, dwp, dw_out, dw_gu, dw_down, dg_mix_pre, dg_mix_post, dg_ffn_pre, dg_ffn_post, dg_attn, dg_conv,
     dtaps, dbf, loss_p) = _local_step(x[0], loss_target[0], wp, late_weights, cw8, bfp, g_attn_out, g_conv_out,
                                        g_mix_pre, g_mix_post, g_ffn_pre, g_ffn_post, early_grads, last_grad)

    e_ssem, e_rsem, e_srcs, e_lands, _ = early["handles"]
    land_out, land_gu, land_down = _exchange_wait(e_ssem, e_rsem, e_srcs, e_lands, dg_mix_pre, mode="scatter",
                                                  name="scatter_early_wait")
    res = {}
    big = {"w_out": (land_out, w_out[0], m_w_out[0], v_w_out[0]),
           "w_gate_up": (land_gu, w_gate_up[0], m_w_gate_up[0], v_w_gate_up[0]),
           "w_down": (land_down, w_down[0], m_w_down[0], v_w_down[0])}
    for name, (land, w, m, v) in big.items():
        res[name] = [o[None] for o in _device_sum_adamw(land, w, m, v, tr=BIG_TILES[name], name="adamw_" + name)]
    c_ssem, c_rsem, c_srcs, c_lands, _ = last["handles"]
    after = sum(res[n][1][0, :SUBLANES, :LANES] for n in big)
    (from_chips,) = _exchange_wait(c_ssem, c_rsem, c_srcs, c_lands, after, mode="chips", name="chips_w_in_wait")
    outs = _chip_sum_adamw(from_chips, last["own"], idx, pad_in(w_in[0]), pad_in(m_w_in[0]), pad_in(v_w_in[0]),
                           tr=BIG_TILES["w_in"], name="adamw_w_in")
    res["w_in"] = [o[:, :IN_COLS][None] for o in outs]

    small = _small_all_reduce([dg_mix_pre, dg_mix_post, dg_ffn_pre, dg_ffn_post, dg_attn, dg_conv, dtaps, dbf, loss_p])
    taps_full = jnp.concatenate([small[5:6, :CW], small[5:6, CW:], small[6:7, :CW]], axis=0)
    small_grads = {
        "b_forget": small[6:7, CW:CW + H], "conv_w": lax.dynamic_slice(taps_full, (0, me * 64), (3, 64)),
        "g_attn_out": small[4:5, :AW], "g_conv_out": small[4:5, AW:], "g_mix_pre": small[0:1], "g_mix_post": small[1:2],
        "g_ffn_pre": small[2:3], "g_ffn_post": small[3:4]}
    loss = small[6, CW + 128]
    smalls = {"b_forget": (b_forget, m_b_forget, v_b_forget), "conv_w": (conv_w[0], m_conv_w[0], v_conv_w[0]),
              "g_attn_out": (g_attn_out, m_g_attn_out, v_g_attn_out), "g_conv_out": (g_conv_out, m_g_conv_out, v_g_conv_out),
              "g_mix_pre": (g_mix_pre, m_g_mix_pre, v_g_mix_pre), "g_mix_post": (g_mix_post, m_g_mix_post, v_g_mix_post),
              "g_ffn_pre": (g_ffn_pre, m_g_ffn_pre, v_g_ffn_pre), "g_ffn_post": (g_ffn_post, m_g_ffn_post, v_g_ffn_post)}
    for name, (w, m, v) in smalls.items():
        g = small_grads[name]
        outs = [g] + list(_adamw(w, g, m, v, tr=w.shape[0], name="adamw_" + name))
        res[name] = [o[None] for o in outs] if name == "conv_w" else outs

    order = ["w_in", "b_forget", "conv_w", "g_attn_out", "g_conv_out", "w_out", "g_mix_pre", "g_mix_post",
             "w_gate_up", "w_down", "g_ffn_pre", "g_ffn_post"]
    outs = [loss, grad_x[None]]
    for k in range(4):
        outs += [res[n][k] for n in order]
    return tuple(outs)
```

```python
import functools

import numpy as np

import jax
import jax.numpy as jnp
from jax import lax
from jax.experimental import pallas as pl
from jax.experimental.pallas import tpu as pltpu

F32 = jnp.float32
BF16 = jnp.bfloat16
HIGHEST = lax.Precision.HIGHEST
MESH_ID = pl.DeviceIdType.MESH

D = 1024
H = 8
DH = 64
AW = 512
CW = 512
DFF = 2816
FB = DFF // 4
HP = 128
OFF_Q, OFF_K, OFF_V, OFF_BCU, OFF_F = 0, 1024, 2048, 2560, 4096
WP = OFF_F + 128
PIECES = ((OFF_Q, OFF_K), (OFF_K, OFF_V), (OFF_V, OFF_BCU), (OFF_BCU, OFF_F), (OFF_F, WP))
EPS = 1e-6
NDEV = 8
LANES = 128
SUBLANES = 8
IN_COLS = 385
IN_PAD = 512
WIN = 896
ADAM_LR, ADAM_B1, ADAM_B2, ADAM_EPS, ADAM_WD, ADAM_STEP = 0.001, 0.9, 0.999, 1e-08, 0.01, 10

NT = (((1,), (1,)), ((), ()))
TN = (((0,), (0,)), ((), ()))


def _cparams(vmem_mb=None, sem=None):
    kw = {}
    if vmem_mb is not None:
        kw["vmem_limit_bytes"] = vmem_mb << 20
    if sem is not None:
        kw["dimension_semantics"] = sem
    return pltpu.CompilerParams(**kw)


def _full(shape):
    return pl.BlockSpec(shape, lambda *_: (0,) * len(shape))


def _resident(shape):
    return pl.BlockSpec(shape, lambda *_: (0,) * len(shape), pipeline_mode=pl.Buffered(1))


def _rows(tm, width):
    return pl.BlockSpec((tm, width), lambda i: (i, 0))


def _fold8(v):
    r, w = v.shape
    return jnp.sum(v.reshape(r // SUBLANES, SUBLANES, w), axis=0)


def _split_dot(v, m01):
    hi = v.astype(BF16)
    lo = (v - hi.astype(F32)).astype(BF16)
    return (jnp.dot(hi, m01, preferred_element_type=F32)
            + jnp.dot(lo, m01, preferred_element_type=F32))


def _rms_fwd(v, g):
    r = lax.rsqrt(jnp.mean(v * v, axis=-1, keepdims=True) + EPS)
    n = v * r
    return n * g, n, r


def _rms_bwd(do, n, r, g):
    dn = do * g
    return r * (dn - n * jnp.mean(dn * n, axis=-1, keepdims=True)), do * n


def _padded_column(n):
    if n < AW:
        return OFF_Q + HP * (n // DH) + n % DH, 0.125
    if n < 2 * AW:
        m = n - AW
        return OFF_K + HP * (m // DH) + m % DH, 1.0
    if n < 3 * AW:
        return OFF_V + n - 2 * AW, 1.0
    if n < 3 * AW + H:
        return OFF_F + n - 3 * AW, 1.0
    return OFF_BCU + n - 3 * AW - H, 1.0


def _in_layout_tables():
    dest = -np.ones((IN_PAD, LANES), np.int32)
    dest_f = -np.ones((IN_PAD, LANES), np.int32)
    scale = np.zeros((IN_PAD, LANES), np.float32)
    starts = []
    for k in range(NDEV):
        cols = [_padded_column(IN_COLS * k + j) for j in range(IN_COLS)]
        main = [c for c, _ in cols if c < OFF_F]
        ws = min((min(main) // LANES) * LANES, OFF_F - WIN)
        assert ws <= min(main) and max(main) < ws + WIN
        starts.append(ws)
        for j, (c, sc) in enumerate(cols):
            scale[j, k] = sc
            if c < OFF_F:
                dest[j, k] = c - ws
            else:
                dest_f[j, k] = c - OFF_F
    f_shards = tuple(k for k in range(NDEV) if (dest_f[:, k] >= 0).any())
    return tuple(starts), f_shards, jnp.asarray(dest), jnp.asarray(dest_f), jnp.asarray(scale)


def _perm(dest_ref, scale_ref, k, width):
    lane = lax.broadcasted_iota(jnp.int32, (IN_PAD, width), 1)
    return jnp.where(dest_ref[:, k:k + 1] == lane, scale_ref[:, k:k + 1], 0.0).astype(BF16)


def _assemble_w_in(blocks, tables, *, tr):
    starts, f_shards, dest, dest_f, scale = tables

    def body(b_ref, dest_ref, destf_ref, scale_ref, o_ref):
        o_ref[...] = jnp.zeros_like(o_ref)
        for k in range(NDEV):
            b = b_ref[k]
            ws = starts[k]
            part = jnp.dot(b, _perm(dest_ref, scale_ref, k, WIN), preferred_element_type=F32)
            o_ref[:, ws:ws + WIN] += part.astype(BF16)
            if k in f_shards:
                part = jnp.dot(b, _perm(destf_ref, scale_ref, k, 128), preferred_element_type=F32)
                o_ref[:, OFF_F:WP] += part.astype(BF16)

    tab = _full((IN_PAD, LANES))
    return pl.pallas_call(
        body, name="assemble_w_in", grid=(D // tr,),
        in_specs=[pl.BlockSpec((NDEV, tr, IN_PAD), lambda i: (0, i, 0)), tab, tab, tab],
        out_specs=_rows(tr, WP),
        out_shape=jax.ShapeDtypeStruct((D, WP), BF16),
        compiler_params=_cparams(48, ("arbitrary",)),
    )(blocks, dest, dest_f, scale)


def _disassemble_w_in(dwp, tables, *, tr):
    starts, f_shards, dest, dest_f, scale = tables
    width = dwp.shape[1]

    def body(g_ref, dest_ref, destf_ref, scale_ref, o_ref):
        for k in range(NDEV):
            ws = starts[k]
            acc = lax.dot_general(g_ref[:, ws:ws + WIN], _perm(dest_ref, scale_ref, k, WIN), NT, preferred_element_type=F32)
            if k in f_shards:
                acc = acc + lax.dot_general(g_ref[:, OFF_F:WP], _perm(destf_ref, scale_ref, k, 128), NT,
                                            preferred_element_type=F32)
            o_ref[k] = acc.astype(BF16)

    tab = _full((IN_PAD, LANES))
    return pl.pallas_call(
        body, name="disassemble_w_in", grid=(D // tr,),
        in_specs=[_rows(tr, width), tab, tab, tab],
        out_specs=pl.BlockSpec((NDEV, tr, IN_PAD), lambda i: (0, i, 0)),
        out_shape=jax.ShapeDtypeStruct((NDEV, D, IN_PAD), BF16),
        compiler_params=_cparams(48, ("arbitrary",)),
    )(dwp, dest, dest_f, scale)


def _in_proj(x, g1, wp, bfp, pq, pk, oq, ok, *, tm):
    s = x.shape[0]

    def body(x_ref, g_ref, w_ref, bf_ref, pq_ref, pk_ref, oq_ref, ok_ref,
             ht_ref, qp_ref, kp_ref, v_ref, bcu_ref, z_ref, carry):
        @pl.when(pl.program_id(0) == 0)
        def _():
            carry[...] = jnp.zeros_like(carry)

        h = _rms_fwd(x_ref[...], g_ref[...])[0].astype(BF16)
        ht_ref[...] = h.T
        z = jnp.dot(h, w_ref[:, OFF_F:WP], preferred_element_type=F32) + bf_ref[...]
        z_ref[...] = z
        lane = lax.broadcasted_iota(jnp.int32, (tm, 128), 1)
        logf = jnp.where(lane < H, jnp.minimum(z, 0.0) - jnp.log(1.0 + jnp.exp(-jnp.abs(z))), 0.0)
        row = lax.broadcasted_iota(jnp.int32, (tm, tm), 0)
        col = lax.broadcasted_iota(jnp.int32, (tm, tm), 1)
        tri = (col <= row).astype(F32)
        c = jnp.dot(tri, logf, precision=HIGHEST, preferred_element_type=F32) + carry[0:1, :]
        carry[...] = jnp.broadcast_to(c[tm - 1:tm, :], carry.shape)
        c1 = c.astype(BF16).astype(F32)
        r1 = c - c1
        c2 = r1.astype(BF16).astype(F32)
        c3 = (r1 - c2).astype(BF16).astype(F32)
        zc = (c1 + pltpu.roll(c2, 8, axis=1) + pltpu.roll(c3, 16, axis=1)).astype(BF16)
        q = jnp.dot(h, w_ref[:, OFF_Q:OFF_K], preferred_element_type=F32)
        qp_ref[...] = (q + jnp.dot(zc, pq_ref[...], preferred_element_type=F32) + oq_ref[...]).astype(BF16)
        k = jnp.dot(h, w_ref[:, OFF_K:OFF_V], preferred_element_type=F32)
        kp_ref[...] = (k + jnp.dot(zc, pk_ref[...], preferred_element_type=F32) + ok_ref[...]).astype(BF16)
        v_ref[...] = jnp.dot(h, w_ref[:, OFF_V:OFF_BCU], preferred_element_type=F32).astype(BF16)
        bcu_ref[...] = jnp.dot(h, w_ref[:, OFF_BCU:OFF_F], preferred_element_type=F32)

    return pl.pallas_call(
        body, name="in_proj", grid=(s // tm,),
        in_specs=[_rows(tm, D), _full((1, D)), _resident((D, WP)), _full((1, 128)),
                  _full((128, 1024)), _full((128, 1024)), _full((1, 1024)), _full((1, 1024))],
        out_specs=[pl.BlockSpec((D, tm), lambda i: (0, i)), _rows(tm, 1024), _rows(tm, 1024), _rows(tm, AW),
                   _rows(tm, 3 * CW), _rows(tm, 128)],
        out_shape=[jax.ShapeDtypeStruct((D, s), BF16), jax.ShapeDtypeStruct((s, 1024), BF16),
                   jax.ShapeDtypeStruct((s, 1024), BF16), jax.ShapeDtypeStruct((s, AW), BF16),
                   jax.ShapeDtypeStruct((s, 3 * CW), F32), jax.ShapeDtypeStruct((s, 128), F32)],
        scratch_shapes=[pltpu.VMEM((SUBLANES, 128), F32)],
        compiler_params=_cparams(56, ("arbitrary",)),
    )(x, g1, wp, bfp, pq, pk, oq, ok)


def _attn_fwd(qp, kp, v, *, t):
    s = qp.shape[0]
    nq = s // t

    def body(q_ref, k_ref, v_ref, o_ref, lse_ref, mk_ref):
        qi = pl.program_id(1)
        row = lax.broadcasted_iota(jnp.int32, (t, t), 0)
        col = lax.broadcasted_iota(jnp.int32, (t, t), 1)
        lane = lax.broadcasted_iota(jnp.int32, (t, 128), 1)

        def head_step(hh, ki, carry, masked):
            m, l, acc = carry
            off = pl.multiple_of(ki * t, t)
            q = q_ref[:, HP * hh:HP * (hh + 1)]
            k = k_ref[pl.ds(off, t), HP * hh:HP * (hh + 1)]
            sc = lax.dot_general(q, k, NT, preferred_element_type=F32)
            if masked:
                sc = jnp.where(col <= row, sc, -1e30)
            mn = jnp.maximum(m, jnp.max(sc, axis=-1, keepdims=True))
            p = jnp.exp(sc - mn)
            a = jnp.exp(m - mn)
            l = a * l + jnp.sum(p, axis=-1, keepdims=True)
            acc = a * acc + jnp.dot(p.astype(BF16), v_ref[pl.ds(off, t), :], preferred_element_type=F32)
            return mn, l, acc

        def step(ki, carry, masked):
            new = tuple(head_step(hh, ki, carry[hh], masked) for hh in range(2))
            mk_ref[ki] = jnp.where(lane < DH, jnp.broadcast_to(new[0][0], (t, 128)), jnp.broadcast_to(new[1][0], (t, 128)))
            return new

        init = (jnp.full((t, 1), -1e30, F32), jnp.zeros((t, 1), F32), jnp.zeros((t, 128), F32))
        carry = lax.fori_loop(0, qi, functools.partial(step, masked=False), (init, init))
        (m0, l0, acc0), (m1, l1, acc1) = step(qi, carry, True)
        o_ref[...] = jnp.where(lane < DH, acc0 / l0, acc1 / l1)
        lse_ref[...] = jnp.where(lane < DH, jnp.broadcast_to(m0 + jnp.log(l0), (t, 128)),
                                 jnp.broadcast_to(m1 + jnp.log(l1), (t, 128)))

    return pl.pallas_call(
        body, name="attn_fwd", grid=(H // 2, nq),
        in_specs=[pl.BlockSpec((t, 2 * HP), lambda p, i: (i, p)),
                  pl.BlockSpec((s, 2 * HP), lambda p, i: (0, p)),
                  pl.BlockSpec((s, 128), lambda p, i: (0, p))],
        out_specs=[pl.BlockSpec((t, 128), lambda p, i: (i, p)), pl.BlockSpec((t, 128), lambda p, i: (i, p)),
                   pl.BlockSpec((nq, t, 128), lambda p, i: (0, i, p))],
        out_shape=[jax.ShapeDtypeStruct((s, AW), F32), jax.ShapeDtypeStruct((s, AW), F32),
                   jax.ShapeDtypeStruct((nq, s, AW), F32)],
        compiler_params=_cparams(48, ("arbitrary", "arbitrary")),
    )(qp, kp, v)


def _conv_taps(bcu_ref, halo_ref, first, tm):
    z = bcu_ref[:, CW:2 * CW] * bcu_ref[:, 2 * CW:3 * CW]
    zh = jnp.where(first, 0.0, halo_ref[:, CW:2 * CW] * halo_ref[:, 2 * CW:3 * CW])
    row = lax.broadcasted_iota(jnp.int32, (tm, CW), 0)
    z1 = jnp.where(row == 0, zh[7:8, :], pltpu.roll(z, 1, axis=0))
    z2 = jnp.where(row == 0, zh[6:7, :], jnp.where(row == 1, zh[7:8, :], pltpu.roll(z, 2, axis=0)))
    return z, z1, z2


def _halo_before(tm, width):
    return pl.BlockSpec((SUBLANES, width), lambda i: (jnp.maximum(i * (tm // SUBLANES) - 1, 0), 0))


def _mix_out(o, bcu, cw8, ga, gc, gsum, w_out, x, g_post, g_ffn_pre, *, tm):
    s = x.shape[0]

    def body(o_ref, bcu_ref, halo_ref, cw_ref, ga_ref, gc_ref, gs_ref, w_ref, x_ref, g_ref, gf_ref,
             merged_ref, y_ref, x2_ref, cv_ref, h2_ref):
        z, z1, z2 = _conv_taps(bcu_ref, halo_ref, pl.program_id(0) == 0, tm)
        cv = cw_ref[0:1, :] * z2 + cw_ref[1:2, :] * z1 + cw_ref[2:3, :] * z
        cv_ref[...] = cv
        conv = bcu_ref[:, 0:CW] * cv
        ov = o_ref[...]
        ra = lax.rsqrt(_split_dot(ov * ov, gs_ref[...]) * (1.0 / DH) + EPS)
        rc = lax.rsqrt(_split_dot(conv * conv, gs_ref[...]) * (1.0 / DH) + EPS)
        merged = jnp.concatenate([ov * ra * ga_ref[...], conv * rc * gc_ref[...]], axis=1).astype(BF16)
        merged_ref[...] = merged
        y = jnp.dot(merged, w_ref[...], preferred_element_type=F32)
        y_ref[...] = y
        x2 = x_ref[...] + _rms_fwd(y, g_ref[...])[0]
        x2_ref[...] = x2
        h2_ref[...] = _rms_fwd(x2, gf_ref[...])[0].astype(BF16)

    return pl.pallas_call(
        body, name="mix_out", grid=(s // tm,),
        in_specs=[_rows(tm, AW), _rows(tm, 3 * CW), _halo_before(tm, 3 * CW), _full((SUBLANES, CW)),
                  _full((1, AW)), _full((1, CW)), _full((CW, CW)), _resident((D, D)), _rows(tm, D), _full((1, D)),
                  _full((1, D))],
        out_specs=[_rows(tm, D), _rows(tm, D), _rows(tm, D), _rows(tm, CW), _rows(tm, D)],
        out_shape=[jax.ShapeDtypeStruct((s, D), BF16), jax.ShapeDtypeStruct((s, D), F32),
                   jax.ShapeDtypeStruct((s, D), F32), jax.ShapeDtypeStruct((s, CW), F32),
                   jax.ShapeDtypeStruct((s, D), BF16)],
        compiler_params=_cparams(48, ("arbitrary",)),
    )(o, bcu, bcu, cw8, ga, gc, gsum, w_out, x, g_post, g_ffn_pre)


def _ffn_up(h2, wgu, *, tm):
    s = h2.shape[0]

    def body(h_ref, w_ref, gate_ref, up_ref, a_ref):
        h = h_ref[...]
        gate = jnp.dot(h, w_ref[0, 0], preferred_element_type=F32)
        up = jnp.dot(h, w_ref[1, 0], preferred_element_type=F32)
        gate_ref[0] = gate.astype(BF16)
        up_ref[0] = up.astype(BF16)
        a_ref[0] = (gate * jax.nn.sigmoid(gate) * up).astype(BF16)

    blk = pl.BlockSpec((1, tm, FB), lambda j, i: (j, i, 0))
    return pl.pallas_call(
        body, name="ffn_up", grid=(4, s // tm),
        in_specs=[pl.BlockSpec((tm, D), lambda j, i: (i, 0)),
                  pl.BlockSpec((2, 1, D, FB), lambda j, i: (0, j, 0, 0))],
        out_specs=[blk, blk, blk],
        out_shape=[jax.ShapeDtypeStruct((4, s, FB), BF16)] * 3,
        compiler_params=_cparams(48, ("arbitrary", "arbitrary")),
    )(h2, wgu)


def _ffn_down_loss(a, wd, x2, target, g_post, *, tm):
    s = x2.shape[0]

    def body(a_ref, w_ref, x2_ref, t_ref, g_ref, dx3_ref, dff_ref, loss_ref, dg_ref):
        @pl.when(pl.program_id(0) == 0)
        def _():
            loss_ref[...] = jnp.zeros_like(loss_ref)
            dg_ref[...] = jnp.zeros_like(dg_ref)

        ff = jnp.dot(a_ref[0], w_ref[0], preferred_element_type=F32)
        for j in range(1, 4):
            ff = ff + jnp.dot(a_ref[j], w_ref[j], preferred_element_type=F32)
        out, n, r = _rms_fwd(ff, g_ref[...])
        e = x2_ref[...] + out - t_ref[...]
        loss_ref[...] += _fold8(e * e)
        dx3 = e * (1.0 / D)
        dx3_ref[...] = dx3
        dff, dg = _rms_bwd(dx3, n, r, g_ref[...])
        dff_ref[...] = dff.astype(BF16)
        dg_ref[...] += _fold8(dg)

    return pl.pallas_call(
        body, name="ffn_down_loss", grid=(s // tm,),
        in_specs=[pl.BlockSpec((4, tm, FB), lambda i: (0, i, 0)), _resident((4, FB, D)), _rows(tm, D), _rows(tm, D),
                  _full((1, D))],
        out_specs=[_rows(tm, D), _rows(tm, D), _full((SUBLANES, D)), _full((SUBLANES, D))],
        out_shape=[jax.ShapeDtypeStruct((s, D), F32), jax.ShapeDtypeStruct((s, D), BF16),
                   jax.ShapeDtypeStruct((SUBLANES, D), F32), jax.ShapeDtypeStruct((SUBLANES, D), F32)],
        compiler_params=_cparams(48, ("arbitrary",)),
    )(a, wd, x2, target, g_post)


def _ffn_bwd_act(dff, wd, gate, up, *, tm):
    s = dff.shape[0]

    def body(dff_ref, w_ref, gate_ref, up_ref, dgu_ref):
        da = lax.dot_general(dff_ref[...], w_ref[0], NT, preferred_element_type=F32)
        g = gate_ref[0].astype(F32)
        sg = jax.nn.sigmoid(g)
        dgu_ref[0, 0] = (da * up_ref[0].astype(F32) * (sg * (1.0 + g * (1.0 - sg)))).astype(BF16)
        dgu_ref[1, 0] = (da * (g * sg)).astype(BF16)

    blk = pl.BlockSpec((1, tm, FB), lambda j, i: (j, i, 0))
    return pl.pallas_call(
        body, name="ffn_bwd_act", grid=(4, s // tm),
        in_specs=[pl.BlockSpec((tm, D), lambda j, i: (i, 0)), pl.BlockSpec((1, FB, D), lambda j, i: (j, 0, 0)), blk, blk],
        out_specs=pl.BlockSpec((2, 1, tm, FB), lambda j, i: (0, j, i, 0)),
        out_shape=jax.ShapeDtypeStruct((2, 4, s, FB), BF16),
        compiler_params=_cparams(48, ("arbitrary", "arbitrary")),
    )(dff, wd, gate, up)


def _grad_matmul(a, b, *, ta, tb, ts, name):
    s, ka = a.shape
    nb = b.shape[1]
    ts = min(ts, s)
    nk = s // ts

    def body(a_ref, b_ref, o_ref, acc):
        k = pl.program_id(2)

        @pl.when(k == 0)
        def _():
            acc[...] = jnp.zeros_like(acc)

        acc[...] += lax.dot_general(a_ref[...], b_ref[...], TN, preferred_element_type=F32)

        @pl.when(k == nk - 1)
        def _():
            o_ref[...] = acc[...].astype(BF16)

    return pl.pallas_call(
        body, name=name, grid=(ka // ta, nb // tb, nk),
        in_specs=[pl.BlockSpec((ts, ta), lambda i, j, k: (k, i)), pl.BlockSpec((ts, tb), lambda i, j, k: (k, j))],
        out_specs=pl.BlockSpec((ta, tb), lambda i, j, k: (i, j)),
        out_shape=jax.ShapeDtypeStruct((ka, nb), BF16),
        scratch_shapes=[pltpu.VMEM((ta, tb), F32)],
        compiler_params=_cparams(48, ("arbitrary", "arbitrary", "arbitrary")),
    )(a, b)


def _grad_matmul_t(at, b, *, tb, name):
    ka, s = at.shape
    blocked = b.ndim == 3
    nb = b.shape[-1]
    steps = b.shape[0] if blocked else nb // tb
    width = nb if blocked else tb

    def body(a_ref, b_ref, o_ref):
        bv = b_ref[0] if blocked else b_ref[...]
        res = jnp.dot(a_ref[...], bv, preferred_element_type=F32).astype(BF16)
        if blocked:
            o_ref[0] = res
        else:
            o_ref[...] = res

    if blocked:
        b_spec = pl.BlockSpec((1, s, nb), lambda j: (j, 0, 0))
        o_spec = pl.BlockSpec((1, ka, nb), lambda j: (j, 0, 0))
        o_shape = jax.ShapeDtypeStruct((steps, ka, nb), BF16)
    else:
        b_spec = pl.BlockSpec((s, width), lambda j: (0, j))
        o_spec = pl.BlockSpec((ka, width), lambda j: (0, j))
        o_shape = jax.ShapeDtypeStruct((ka, nb), BF16)
    return pl.pallas_call(
        body, name=name, grid=(steps,),
        in_specs=[_resident((ka, s)), b_spec], out_specs=o_spec, out_shape=o_shape,
        compiler_params=_cparams(56, ("arbitrary",)),
    )(at, b)


GW_TILE = 256


def _grad_w_in(h1t, pieces):
    ka, s = h1t.shape
    widths = [p.shape[1] for p in pieces]
    assert all(w % GW_TILE == 0 for w in widths)
    first = [sum(widths[:i]) // GW_TILE for i in range(len(pieces))]
    count = [w // GW_TILE for w in widths]

    def body(a_ref, *refs):
        o_ref = refs[-1]
        j = pl.program_id(0)
        for ref, f0, n in zip(refs[:-1], first, count):
            @pl.when((j >= f0) & (j < f0 + n))
            def _(ref=ref):
                o_ref[...] = jnp.dot(a_ref[...], ref[...], preferred_element_type=F32).astype(BF16)

    def spec(f0, n):
        return pl.BlockSpec((s, GW_TILE), lambda j: (0, jnp.clip(j - f0, 0, n - 1)))

    return pl.pallas_call(
        body, name="grad_w_in", grid=(sum(count),),
        in_specs=[_resident((ka, s))] + [spec(f0, n) for f0, n in zip(first, count)],
        out_specs=pl.BlockSpec((ka, GW_TILE), lambda j: (0, j)),
        out_shape=jax.ShapeDtypeStruct((ka, sum(widths)), BF16),
        compiler_params=_cparams(56, ("arbitrary",)),
    )(h1t, *pieces)


def _grad_matmul_blocks(a, b, *, ts, name):
    nblk = a.shape[0] if a.ndim == 3 else b.shape[0]
    s = a.shape[-2]
    ka, nb = a.shape[-1], b.shape[-1]
    ts = min(ts, s)
    nk = s // ts

    def body(a_ref, b_ref, o_ref, acc):
        k = pl.program_id(1)

        @pl.when(k == 0)
        def _():
            acc[...] = jnp.zeros_like(acc)

        av = a_ref[0] if a.ndim == 3 else a_ref[...]
        bv = b_ref[0] if b.ndim == 3 else b_ref[...]
        acc[...] += lax.dot_general(av, bv, TN, preferred_element_type=F32)

        @pl.when(k == nk - 1)
        def _():
            o_ref[0] = acc[...].astype(BF16)

    def spec(arr, width):
        if arr.ndim == 3:
            return pl.BlockSpec((1, ts, width), lambda j, k: (j, k, 0))
        return pl.BlockSpec((ts, width), lambda j, k: (k, 0))

    return pl.pallas_call(
        body, name=name, grid=(nblk, nk),
        in_specs=[spec(a, ka), spec(b, nb)],
        out_specs=pl.BlockSpec((1, ka, nb), lambda j, k: (j, 0, 0)),
        out_shape=jax.ShapeDtypeStruct((nblk, ka, nb), BF16),
        scratch_shapes=[pltpu.VMEM((ka, nb), F32)],
        compiler_params=_cparams(48, ("arbitrary", "arbitrary")),
    )(a, b)


def _ffn_bwd_in(dgu, wgu, x2, g_pre, dx3, y, g_post, *, tm):
    s = x2.shape[0]

    def body(dgu_ref, w_ref, x2_ref, gpre_ref, dx3_ref, y_ref, gpost_ref,
             dx2_ref, dy_ref, dgpre_ref, dgpost_ref):
        @pl.when(pl.program_id(0) == 0)
        def _():
            dgpre_ref[...] = jnp.zeros_like(dgpre_ref)
            dgpost_ref[...] = jnp.zeros_like(dgpost_ref)

        dh2 = None
        for a in range(2):
            for j in range(4):
                part = lax.dot_general(dgu_ref[a, j], w_ref[a, j], NT, preferred_element_type=F32)
                dh2 = part if dh2 is None else dh2 + part
        _, n2, r2 = _rms_fwd(x2_ref[...], gpre_ref[...])
        dxn, dg = _rms_bwd(dh2, n2, r2, gpre_ref[...])
        dgpre_ref[...] += _fold8(dg)
        dx2 = dx3_ref[...] + dxn
        dx2_ref[...] = dx2
        _, ny, ry = _rms_fwd(y_ref[...], gpost_ref[...])
        dy, dg2 = _rms_bwd(dx2, ny, ry, gpost_ref[...])
        dy_ref[...] = dy.astype(BF16)
        dgpost_ref[...] += _fold8(dg2)

    return pl.pallas_call(
        body, name="ffn_bwd_in", grid=(s // tm,),
        in_specs=[pl.BlockSpec((2, 4, tm, FB), lambda i: (0, 0, i, 0)), _resident((2, 4, D, FB)), _rows(tm, D),
                  _full((1, D)), _rows(tm, D), _rows(tm, D), _full((1, D))],
        out_specs=[_rows(tm, D), _rows(tm, D), _full((SUBLANES, D)), _full((SUBLANES, D))],
        out_shape=[jax.ShapeDtypeStruct((s, D), F32), jax.ShapeDtypeStruct((s, D), BF16),
                   jax.ShapeDtypeStruct((SUBLANES, D), F32), jax.ShapeDtypeStruct((SUBLANES, D), F32)],
        compiler_params=_cparams(56, ("arbitrary",)),
    )(dgu, wgu, x2, g_pre, dx3, y, g_post)


def _mix_bwd(dy, w_out, o, cv, bcu, ga, gc, gsum, *, tm):
    s = dy.shape[0]

    def group_norm_bwd(dn_out, v, g, gs):
        r = lax.rsqrt(_split_dot(v * v, gs) * (1.0 / DH) + EPS)
        n = v * r
        dn = dn_out * g
        return r * (dn - n * (_split_dot(dn * n, gs) * (1.0 / DH))), dn_out * n

    def body(dy_ref, w_ref, o_ref, cv_ref, bcu_ref, ga_ref, gc_ref, gs_ref,
             do_ref, dl_ref, dcv_ref, db_ref, dga_ref, dgc_ref):
        @pl.when(pl.program_id(0) == 0)
        def _():
            dga_ref[...] = jnp.zeros_like(dga_ref)
            dgc_ref[...] = jnp.zeros_like(dgc_ref)

        dm = lax.dot_general(dy_ref[...], w_ref[...], NT, preferred_element_type=F32)
        ov = o_ref[...]
        do, dga = group_norm_bwd(dm[:, 0:AW], ov, ga_ref[...], gs_ref[...])
        dob = do.astype(BF16)
        do_ref[...] = dob
        dl_ref[...] = _split_dot(dob.astype(F32) * ov, gs_ref[...])
        dga_ref[...] += _fold8(dga)
        gate_b = bcu_ref[:, 0:CW]
        cv = cv_ref[...]
        dconv, dgc = group_norm_bwd(dm[:, AW:D], gate_b * cv, gc_ref[...], gs_ref[...])
        dgc_ref[...] += _fold8(dgc)
        dcv_ref[...] = dconv * gate_b
        db_ref[...] = (dconv * cv).astype(BF16)

    return pl.pallas_call(
        body, name="mix_bwd", grid=(s // tm,),
        in_specs=[_rows(tm, D), _resident((D, D)), _rows(tm, AW), _rows(tm, CW), _rows(tm, 3 * CW),
                  _full((1, AW)), _full((1, CW)), _full((CW, CW))],
        out_specs=[_rows(tm, AW), _rows(tm, AW), _rows(tm, CW), _rows(tm, CW),
                   _full((SUBLANES, AW)), _full((SUBLANES, CW))],
        out_shape=[jax.ShapeDtypeStruct((s, AW), BF16), jax.ShapeDtypeStruct((s, AW), F32),
                   jax.ShapeDtypeStruct((s, CW), F32), jax.ShapeDtypeStruct((s, CW), BF16),
                   jax.ShapeDtypeStruct((SUBLANES, AW), F32), jax.ShapeDtypeStruct((SUBLANES, CW), F32)],
        compiler_params=_cparams(48, ("arbitrary",)),
    )(dy, w_out, o, cv, bcu, ga, gc, gsum)


def _conv_bwd(dcv, db, bcu, cw8, *, tm):
    s = dcv.shape[0]
    nt = s // tm

    def body(dcv_ref, nxt_ref, db_ref, bcu_ref, halo_ref, cw_ref, dbcu_ref, dw_ref):
        i = pl.program_id(0)

        @pl.when(i == 0)
        def _():
            dw_ref[...] = jnp.zeros_like(dw_ref)

        z, z1, z2 = _conv_taps(bcu_ref, halo_ref, i == 0, tm)
        d = dcv_ref[...]
        dw_ref[0] += _fold8(d * z2)
        dw_ref[1] += _fold8(d * z1)
        dw_ref[2] += _fold8(d * z)
        nx = jnp.where(i == nt - 1, 0.0, nxt_ref[...])
        row = lax.broadcasted_iota(jnp.int32, (tm, CW), 0)
        d1 = jnp.where(row == tm - 1, nx[0:1, :], pltpu.roll(d, tm - 1, axis=0))
        d2 = jnp.where(row == tm - 2, nx[0:1, :], jnp.where(row == tm - 1, nx[1:2, :], pltpu.roll(d, tm - 2, axis=0)))
        dz = cw_ref[2:3, :] * d + cw_ref[1:2, :] * d1 + cw_ref[0:1, :] * d2
        dbcu_ref[:, 0:CW] = db_ref[...]
        dbcu_ref[:, CW:2 * CW] = (dz * bcu_ref[:, 2 * CW:3 * CW]).astype(BF16)
        dbcu_ref[:, 2 * CW:3 * CW] = (dz * bcu_ref[:, CW:2 * CW]).astype(BF16)

    return pl.pallas_call(
        body, name="conv_bwd", grid=(nt,),
        in_specs=[_rows(tm, CW),
                  pl.BlockSpec((SUBLANES, CW), lambda i: (jnp.minimum((i + 1) * (tm // SUBLANES), s // SUBLANES - 1), 0)),
                  _rows(tm, CW), _rows(tm, 3 * CW), _halo_before(tm, 3 * CW), _full((SUBLANES, CW))],
        out_specs=[_rows(tm, 3 * CW), _full((3, SUBLANES, CW))],
        out_shape=[jax.ShapeDtypeStruct((s, 3 * CW), BF16), jax.ShapeDtypeStruct((3, SUBLANES, CW), F32)],
        compiler_params=_cparams(48, ("arbitrary",)),
    )(dcv, dcv, db, bcu, bcu, cw8)


def _attn_bwd(qp, kp, v, do, lse, dl, mk, *, t):
    s = qp.shape[0]
    nq = s // t

    def body(q_ref, k_ref, v_ref, do_ref, lse_ref, dl_ref, mk_ref, dq_ref, dk_ref, dv_ref, dkx_ref, dq_acc):
        ki = pl.program_id(1)

        @pl.when(ki == 0)
        def _():
            dq_acc[...] = jnp.zeros_like(dq_acc)

        row = lax.broadcasted_iota(jnp.int32, (t, t), 0)
        col = lax.broadcasted_iota(jnp.int32, (t, t), 1)
        lane = lax.broadcasted_iota(jnp.int32, (t, 128), 1)

        def head_step(hh, qi, carry, masked):
            dk, dv, cs = carry
            off = pl.multiple_of(qi * t, t)
            rows = pl.ds(off, t)
            kh = k_ref[:, HP * hh:HP * (hh + 1)]
            q = q_ref[rows, HP * hh:HP * (hh + 1)]
            in_head = (lane >= DH * hh) & (lane < DH * (hh + 1))
            m_col = mk_ref[0, rows, DH * hh:DH * hh + 1]
            scale = jnp.exp(m_col - lse_ref[rows, DH * hh:DH * hh + 1])
            dom = jnp.where(in_head, do_ref[rows, :], jnp.zeros((), BF16))
            sc = lax.dot_general(q, kh, NT, preferred_element_type=F32) - m_col
            if masked:
                sc = jnp.where(col <= row, sc, -1e30)
            pt = jnp.exp(sc).astype(BF16)
            dp = lax.dot_general(dom, v_ref[...], NT, preferred_element_type=F32)
            ds32 = (pt.astype(F32) * scale) * (dp - dl_ref[rows, DH * hh:DH * hh + 1])
            ds = ds32.astype(BF16)
            cs = cs + _fold8(ds32)
            dv = dv + lax.dot_general(pt, (dom.astype(F32) * scale).astype(BF16), TN, preferred_element_type=F32)
            dk = dk + lax.dot_general(ds, q, TN, preferred_element_type=F32)
            dq_acc[rows, HP * hh:HP * (hh + 1)] += jnp.dot(ds, kh, preferred_element_type=F32)
            return dk, dv, cs

        def step(qi, carry, masked):
            return tuple(head_step(hh, qi, carry[hh], masked) for hh in range(2))

        zero = (jnp.zeros((t, HP), F32), jnp.zeros((t, 128), F32), jnp.zeros((SUBLANES, t), F32))
        carry = step(ki, (zero, zero), True)
        (dk0, dv0, cs0), (dk1, dv1, cs1) = lax.fori_loop(ki + 1, nq, functools.partial(step, masked=False), carry)
        dk_ref[:, 0:HP] = dk0.astype(BF16)
        dk_ref[:, HP:2 * HP] = dk1.astype(BF16)
        dv_ref[...] = (dv0 + dv1).astype(BF16)

        def as_column(cs):
            return lax.dot_general(cs, jnp.ones((SUBLANES, 128), F32), TN, precision=HIGHEST, preferred_element_type=F32)

        dkx_ref[...] = jnp.where(lane < DH, as_column(cs0), as_column(cs1))

        @pl.when(ki == nq - 1)
        def _():
            dq_ref[...] = dq_acc[...].astype(BF16)

    return pl.pallas_call(
        body, name="attn_bwd", grid=(H // 2, nq),
        in_specs=[pl.BlockSpec((s, 2 * HP), lambda p, i: (0, p)),
                  pl.BlockSpec((t, 2 * HP), lambda p, i: (i, p)),
                  pl.BlockSpec((t, 128), lambda p, i: (i, p)),
                  pl.BlockSpec((s, 128), lambda p, i: (0, p)),
                  pl.BlockSpec((s, 128), lambda p, i: (0, p)),
                  pl.BlockSpec((s, 128), lambda p, i: (0, p)),
                  pl.BlockSpec((1, s, 128), lambda p, i: (i, 0, p))],
        out_specs=[pl.BlockSpec((s, 2 * HP), lambda p, i: (0, p)),
                   pl.BlockSpec((t, 2 * HP), lambda p, i: (i, p)),
                   pl.BlockSpec((t, 128), lambda p, i: (i, p)),
                   pl.BlockSpec((t, 128), lambda p, i: (i, p))],
        out_shape=[jax.ShapeDtypeStruct((s, 1024), BF16), jax.ShapeDtypeStruct((s, 1024), BF16),
                   jax.ShapeDtypeStruct((s, AW), BF16), jax.ShapeDtypeStruct((s, AW), F32)],
        scratch_shapes=[pltpu.VMEM((s, 2 * HP), F32)],
        compiler_params=_cparams(56, ("arbitrary", "arbitrary")),
    )(qp, kp, v, do, lse, dl, mk)


def _forget_bwd(dkx, z, sel, *, tm):
    s = dkx.shape[0]
    nt = s // tm

    def body(dk_ref, z_ref, sel_ref, dfl_ref, dbf_ref, carry):
        @pl.when(pl.program_id(0) == 0)
        def _():
            carry[...] = jnp.zeros_like(carry)
            dbf_ref[...] = jnp.zeros_like(dbf_ref)

        dc = _split_dot(dk_ref[...], sel_ref[...])
        row = lax.broadcasted_iota(jnp.int32, (tm, tm), 0)
        col = lax.broadcasted_iota(jnp.int32, (tm, tm), 1)
        tri = (col >= row).astype(F32)
        dlogf = jnp.dot(tri, dc, precision=HIGHEST, preferred_element_type=F32) + carry[0:1, :]
        carry[...] = jnp.broadcast_to(dlogf[0:1, :], carry.shape)
        dz = dlogf * (1.0 - jax.nn.sigmoid(z_ref[...]))
        dfl_ref[:, 0:128] = dz.astype(BF16)
        dfl_ref[:, 128:GW_TILE] = jnp.zeros((tm, GW_TILE - 128), BF16)
        dbf_ref[...] += _fold8(dz)

    rev = lambda i: (nt - 1 - i, 0)
    return pl.pallas_call(
        body, name="forget_bwd", grid=(nt,),
        in_specs=[pl.BlockSpec((tm, AW), rev), pl.BlockSpec((tm, 128), rev), _full((AW, 128))],
        out_specs=[pl.BlockSpec((tm, GW_TILE), rev), _full((SUBLANES, 128))],
        out_shape=[jax.ShapeDtypeStruct((s, GW_TILE), BF16), jax.ShapeDtypeStruct((SUBLANES, 128), F32)],
        scratch_shapes=[pltpu.VMEM((SUBLANES, 128), F32)],
        compiler_params=_cparams(48, ("arbitrary",)),
    )(dkx, z, sel)


def _in_proj_bwd(pieces, wp, x, g1, dx2, *, tm):
    s = x.shape[0]

    def body(q_ref, k_ref, v_ref, bcu_ref, f_ref, w_ref, x_ref, g_ref, dx2_ref, dx_ref, dg_ref):
        @pl.when(pl.program_id(0) == 0)
        def _():
            dg_ref[...] = jnp.zeros_like(dg_ref)

        dh = None
        for ref, (lo, hi) in zip((q_ref, k_ref, v_ref, bcu_ref, f_ref), PIECES):
            part = lax.dot_general(ref[...], w_ref[:, lo:hi], NT, preferred_element_type=F32)
            dh = part if dh is None else dh + part
        _, n, r = _rms_fwd(x_ref[...], g_ref[...])
        dxn, dg = _rms_bwd(dh, n, r, g_ref[...])
        dx_ref[...] = dx2_ref[...] + dxn
        dg_ref[...] += _fold8(dg)

    return pl.pallas_call(
        body, name="in_proj_bwd", grid=(s // tm,),
        in_specs=[_rows(tm, hi - lo) for lo, hi in PIECES] + [_resident((D, WP)), _rows(tm, D), _full((1, D)), _rows(tm, D)],
        out_specs=[_rows(tm, D), _full((SUBLANES, D))],
        out_shape=[jax.ShapeDtypeStruct((s, D), F32), jax.ShapeDtypeStruct((SUBLANES, D), F32)],
        compiler_params=_cparams(56, ("arbitrary",)),
    )(*pieces, wp, x, g1, dx2)


def _position():
    return lax.axis_index("x"), lax.axis_index("y"), lax.axis_index("c")


ANY = pl.BlockSpec(memory_space=pl.ANY)


def _all_gather(shards):
    n = len(shards)

    def body(*refs):
        x_refs, out_refs = refs[:n], refs[n:2 * n]
        send_sems, recv_sems, local_sems = refs[2 * n:]
        x, y, c = _position()
        me, sibling = (x, y, c), (x, y, 1 - c)
        chips = [(1 - x, y), (x, 1 - y), (1 - x, 1 - y)]

        def copy(a, k, block, to, own=False):
            slot = out_refs[a].at[4 * block[0] + 2 * block[1] + block[2]]
            return pltpu.make_async_remote_copy(
                src_ref=x_refs[a] if own else slot, dst_ref=slot,
                send_sem=send_sems.at[7 * a + k], recv_sem=recv_sems.at[7 * a + k], device_id=to, device_id_type=MESH_ID)

        mine = [pltpu.make_async_copy(x_refs[a], out_refs[a].at[4 * x + 2 * y + c], local_sems.at[a]) for a in range(n)]
        for cp in mine:
            cp.start()
        first = []
        for a in range(n):
            first.append(copy(a, 0, me, sibling, own=True))
            first += [copy(a, 1 + j, me, (*chip, c), own=True) for j, chip in enumerate(chips)]
        for cp in first:
            cp.start()
        passed = []
        for j, chip in enumerate(chips):
            for a in range(n):
                copy(a, 1 + j, (*chip, c), me).wait_recv()
                fwd = copy(a, 4 + j, (*chip, c), sibling)
                fwd.start()
                passed.append(fwd)
        for a in range(n):
            copy(a, 0, sibling, me).wait_recv()
            for j, chip in enumerate(chips):
                copy(a, 4 + j, (*chip, 1 - c), me).wait_recv()
        for cp in first + passed:
            cp.wait_send()
        for cp in mine:
            cp.wait()

    return pl.pallas_call(
        body, name="all_gather_weights",
        out_shape=[jax.ShapeDtypeStruct((NDEV,) + sh.shape, sh.dtype) for sh in shards],
        in_specs=[ANY] * n, out_specs=[ANY] * n,
        scratch_shapes=[pltpu.SemaphoreType.DMA((7 * n,)), pltpu.SemaphoreType.DMA((7 * n,)), pltpu.SemaphoreType.DMA((n,))],
    )(*shards)


def _pair_exchange(grads):
    n = len(grads)

    def body(*refs):
        g_refs, out_refs = refs[:n], refs[n:2 * n]
        send_sems, recv_sems = refs[2 * n:]
        x, y, c = _position()
        copies = [pltpu.make_async_remote_copy(
            src_ref=g_refs[a].at[:, pl.ds(1 - c, 1)], dst_ref=out_refs[a], send_sem=send_sems.at[a],
            recv_sem=recv_sems.at[a], device_id=(x, y, 1 - c), device_id_type=MESH_ID) for a in range(n)]
        for cp in copies:
            cp.start()
        for cp in copies:
            cp.wait()

    return pl.pallas_call(
        body, name="grad_pair_exchange",
        out_shape=[jax.ShapeDtypeStruct((4, 1) + g.shape[2:], g.dtype) for g in grads],
        in_specs=[ANY] * n, out_specs=[ANY] * n,
        scratch_shapes=[pltpu.SemaphoreType.DMA((n,)), pltpu.SemaphoreType.DMA((n,))],
    )(*grads)


def _pair_sum(g, got, idx, *, tr, name):
    r, c = g.shape[2:]

    def body(idx_ref, g_ref, got_ref, pb_ref, own_ref):
        p = g_ref[0, 0].astype(F32) + got_ref[0, 0].astype(F32)
        pb_ref[0] = p.astype(BF16)

        @pl.when(pl.program_id(1) == idx_ref[1])
        def _():
            own_ref[...] = p

    return pl.pallas_call(
        body, name=name,
        grid_spec=pltpu.PrefetchScalarGridSpec(
            num_scalar_prefetch=1, grid=(r // tr, 4),
            in_specs=[pl.BlockSpec((1, 1, tr, c), lambda i, j, idx: (j, idx[0], i, 0)),
                      pl.BlockSpec((1, 1, tr, c), lambda i, j, idx: (j, 0, i, 0))],
            out_specs=[pl.BlockSpec((1, tr, c), lambda i, j, idx: (j, i, 0)),
                       pl.BlockSpec((tr, c), lambda i, j, idx: (i, 0))]),
        out_shape=[jax.ShapeDtypeStruct((4, r, c), BF16), jax.ShapeDtypeStruct((r, c), F32)],
        compiler_params=_cparams(32, ("arbitrary", "arbitrary")),
    )(idx, g, got)


HBM = pl.BlockSpec(memory_space=pltpu.HBM)
SEM = pl.BlockSpec(memory_space=pltpu.SEMAPHORE)
DATAFLOW = pltpu.SideEffectType.DATAFLOW_SIDE_EFFECTING


PEERS = {"gather": NDEV - 1, "scatter": NDEV - 1, "chips": 3}


def _exchange_copies(src_refs, land_refs, send_sems, recv_sems, mode):
    x, y, c = _position()
    me, my_chip = 4 * x + 2 * y + c, 2 * x + y
    npeers = PEERS[mode]
    copies = []
    for a, (s_ref, l_ref) in enumerate(zip(src_refs, land_refs)):
        for k in range(npeers):
            if mode == "chips":
                px, py, pc = x ^ ((k + 1) >> 1), y ^ ((k + 1) & 1), c
                src, dst = s_ref.at[2 * px + py], l_ref.at[my_chip]
            else:
                px, py, pc = x ^ ((k + 1) >> 2), y ^ (((k + 1) >> 1) & 1), c ^ ((k + 1) & 1)
                src, dst = (s_ref.at[4 * px + 2 * py + pc] if mode == "scatter" else s_ref), l_ref.at[me]
            copies.append(pltpu.make_async_remote_copy(
                src_ref=src, dst_ref=dst, send_sem=send_sems.at[npeers * a + k], recv_sem=recv_sems.at[npeers * a + k],
                device_id=(px, py, pc), device_id_type=MESH_ID))
    return copies


def _exchange_start(srcs, lands, *, mode, name):
    n = len(srcs)
    nsem = PEERS[mode] * n

    def body(*refs):
        token = refs[-1]
        for cp in _exchange_copies(refs[:n], refs[n:2 * n], refs[2 * n], refs[2 * n + 1], mode):
            cp.start()
        token[...] = jnp.zeros_like(token)

    arrays = list(srcs) + list(lands)
    outs = pl.pallas_call(
        body, name=name,
        out_shape=(pltpu.SemaphoreType.DMA((nsem,)), pltpu.SemaphoreType.DMA((nsem,)),
                   *[pltpu.HBM(a.shape, a.dtype) for a in arrays], jax.ShapeDtypeStruct((SUBLANES, LANES), F32)),
        in_specs=[HBM] * (2 * n),
        out_specs=(SEM, SEM, *[HBM] * (2 * n), pl.BlockSpec(memory_space=pltpu.VMEM)),
        input_output_aliases={i: 2 + i for i in range(2 * n)},
        compiler_params=pltpu.CompilerParams(has_side_effects=DATAFLOW),
    )(*[pltpu.with_memory_space_constraint(a, pltpu.HBM) for a in arrays])
    return outs[0], outs[1], outs[2:2 + n], outs[2 + n:2 + 2 * n], outs[-1]


def _exchange_wait(send_sems, recv_sems, srcs, lands, after, *, mode, name):
    n = len(srcs)

    def body(*refs):
        for cp in _exchange_copies(refs[:n], refs[n:2 * n], refs[2 * n], refs[2 * n + 1], mode):
            cp.wait_send()
            cp.wait_recv()

    arrays = list(srcs) + list(lands)
    outs = pl.pallas_call(
        body, name=name,
        out_shape=tuple(pltpu.HBM(a.shape, a.dtype) for a in arrays),
        in_specs=[HBM] * (2 * n) + [SEM, SEM, ANY],
        out_specs=tuple([HBM] * (2 * n)),
        input_output_aliases={i: i for i in range(2 * n)},
        compiler_params=pltpu.CompilerParams(has_side_effects=DATAFLOW),
    )(*arrays, send_sems, recv_sems, after)
    return outs[n:]


def _own_slot(value, me):
    return lax.dynamic_update_index_in_dim(lax.empty((NDEV,) + value.shape, value.dtype), value, me, 0)


def _small_all_reduce(parts):
    def body(gmp_ref, gmo_ref, gfp_ref, gfo_ref, ga_ref, gc_ref, dw_ref, bf_ref, loss_ref,
             out_ref, buf, send_sems, recv_sems):
        x, y, c = _position()
        me = 4 * x + 2 * y + c

        def colsum(v):
            return jnp.sum(v, axis=0, keepdims=True)

        loss = jnp.sum(colsum(loss_ref[...]), axis=1, keepdims=True) * (0.5 / D)
        rows = [colsum(gmp_ref[...]), colsum(gmo_ref[...]), colsum(gfp_ref[...]), colsum(gfo_ref[...]),
                jnp.concatenate([colsum(ga_ref[...]), colsum(gc_ref[...])], axis=1),
                jnp.concatenate([colsum(dw_ref[0]), colsum(dw_ref[1])], axis=1),
                jnp.concatenate([colsum(dw_ref[2]), colsum(bf_ref[...]), jnp.broadcast_to(loss, (1, 128)),
                                 jnp.zeros((1, 256), F32)], axis=1),
                jnp.zeros((1, D), F32)]
        buf[me] = jnp.concatenate(rows, axis=0)
        copies = []
        for mm in range(1, NDEV):
            peer = (x ^ (mm >> 2), y ^ ((mm >> 1) & 1), c ^ (mm & 1))
            copies.append(pltpu.make_async_remote_copy(
                src_ref=buf.at[me], dst_ref=buf.at[me], send_sem=send_sems.at[mm - 1], recv_sem=recv_sems.at[mm - 1],
                device_id=peer, device_id_type=MESH_ID))
        for cp in copies:
            cp.start()
        for cp in copies:
            cp.wait_recv()
        for cp in copies:
            cp.wait_send()
        acc = buf[0]
        for d in range(1, NDEV):
            acc = acc + buf[d]
        out_ref[...] = acc

    vm = pl.BlockSpec(memory_space=pltpu.VMEM)
    return pl.pallas_call(
        body, name="small_all_reduce",
        out_shape=jax.ShapeDtypeStruct((SUBLANES, D), F32),
        in_specs=[vm] * len(parts), out_specs=vm,
        scratch_shapes=[pltpu.VMEM((NDEV, SUBLANES, D), F32), pltpu.SemaphoreType.DMA((7,)), pltpu.SemaphoreType.DMA((7,))],
    )(*parts)


def _adam_update(w, g, m, v):
    nm = ADAM_B1 * m + (1.0 - ADAM_B1) * g
    nv = ADAM_B2 * v + (1.0 - ADAM_B2) * (g * g)
    m_hat = nm / (1.0 - ADAM_B1 ** ADAM_STEP)
    v_hat = nv / (1.0 - ADAM_B2 ** ADAM_STEP)
    return -ADAM_LR * (m_hat / (jnp.sqrt(v_hat) + ADAM_EPS) + ADAM_WD * w), nm, nv


def _adamw(w, g, m, v, *, tr, name):
    rows, cols = w.shape

    def body(w_ref, g_ref, m_ref, v_ref, d_ref, nm_ref, nv_ref):
        d_ref[...], nm_ref[...], nv_ref[...] = _adam_update(w_ref[...], g_ref[...], m_ref[...], v_ref[...])

    spec = pl.BlockSpec((tr, cols), lambda i: (i, 0))
    return pl.pallas_call(
        body, name=name, grid=(rows // tr,),
        in_specs=[spec] * 4, out_specs=[spec] * 3,
        out_shape=[jax.ShapeDtypeStruct((rows, cols), F32)] * 3,
        compiler_params=_cparams(32, ("arbitrary",)),
    )(w, g, m, v)


def _chip_sum_adamw(got, own, idx, w, m, v, *, tr, name):
    rows, cols = w.shape

    def body(idx_ref, got_ref, own_ref, w_ref, m_ref, v_ref, g_ref, d_ref, nm_ref, nv_ref):
        g = jnp.zeros((tr, cols), F32)
        for j in range(4):
            g = g + jnp.where(idx_ref[1] == j, own_ref[...], got_ref[j].astype(F32))
        g_ref[...] = g
        d_ref[...], nm_ref[...], nv_ref[...] = _adam_update(w_ref[...], g, m_ref[...], v_ref[...])

    spec = pl.BlockSpec((tr, cols), lambda i, idx: (i, 0))
    return pl.pallas_call(
        body, name=name,
        grid_spec=pltpu.PrefetchScalarGridSpec(
            num_scalar_prefetch=1, grid=(rows // tr,),
            in_specs=[pl.BlockSpec((4, tr, cols), lambda i, idx: (0, i, 0)), spec, spec, spec, spec],
            out_specs=[spec] * 4),
        out_shape=[jax.ShapeDtypeStruct((rows, cols), F32)] * 4,
        compiler_params=_cparams(32, ("arbitrary",)),
    )(idx, got, own, w, m, v)


def _device_sum_adamw(land, w, m, v, *, tr, name):
    rows, cols = w.shape

    def body(land_ref, w_ref, m_ref, v_ref, g_ref, d_ref, nm_ref, nv_ref):
        g = land_ref[0].astype(F32)
        for dev in range(1, NDEV):
            g = g + land_ref[dev].astype(F32)
        g_ref[...] = g
        d_ref[...], nm_ref[...], nv_ref[...] = _adam_update(w_ref[...], g, m_ref[...], v_ref[...])

    spec = pl.BlockSpec((tr, cols), lambda i: (i, 0))
    return pl.pallas_call(
        body, name=name, grid=(rows // tr,),
        in_specs=[pl.BlockSpec((NDEV, tr, cols), lambda i: (0, i, 0)), spec, spec, spec],
        out_specs=[spec] * 4,
        out_shape=[jax.ShapeDtypeStruct((rows, cols), F32)] * 4,
        compiler_params=_cparams(32, ("arbitrary",)),
    )(land, w, m, v)


def _placement_constants():
    j = jnp.arange(128)[:, None]
    lane = jnp.arange(1024)[None, :]
    head, sub = lane // HP, lane % HP
    piece, jh = j // H, j % H
    valid = (j < 3 * H) & (jh == head)
    pq = jnp.where(valid & (sub == DH + piece), 1.0, 0.0).astype(BF16)
    pk = jnp.where(valid & (sub == DH + 3 + piece), -1.0, 0.0).astype(BF16)
    oq = jnp.where((sub >= DH + 3) & (sub < DH + 6), 1.0, 0.0).astype(F32)
    ok = jnp.where((sub >= DH) & (sub < DH + 3), 1.0, 0.0).astype(F32)
    r = jnp.arange(AW)[:, None]
    cc = jnp.arange(128)[None, :]
    sel = jnp.where((r % DH == 3) & (r // DH == cc), -1.0, 0.0).astype(BF16)
    gi = jnp.arange(CW)
    gsum = (gi[:, None] // DH == gi[None, :] // DH).astype(BF16)
    return pq, pk, oq, ok, sel, gsum


def _local_step(xs, tgt, wp, late_weights, cw8, bfp, g_attn_out, g_conv_out,
                g_mix_pre, g_mix_post, g_ffn_pre, g_ffn_post, early_grads=None, last_grad=None):
    pq, pk, oq, ok, sel, gsum = _placement_constants()
    h1t, qp, kp, vv, bcu, zf = _in_proj(xs, g_mix_pre, wp, bfp, pq, pk, oq, ok, tm=512)
    o, lse, mk = _attn_fwd(qp, kp, vv, t=512)
    w_out_f, wgu, wd = late_weights(lse)
    merged, y, x2, cv, h2 = _mix_out(o, bcu, cw8, g_attn_out, g_conv_out, gsum, w_out_f, xs, g_mix_post, g_ffn_pre, tm=512)
    gate, up, act = _ffn_up(h2, wgu, tm=1024)
    dx3, dff, loss_p, dg_ffn_post = _ffn_down_loss(act, wd, x2, tgt, g_ffn_post, tm=512)

    dgu = _ffn_bwd_act(dff, wd, gate, up, tm=1024)
    dw_down = _grad_matmul_blocks(act, dff, ts=4096, name="grad_w_down")
    dw_gu = _grad_matmul_blocks(dgu.reshape(NDEV, -1, FB), h2, ts=4096, name="grad_w_gate_up")
    dx2, dy, dg_ffn_pre, dg_mix_post = _ffn_bwd_in(dgu, wgu, x2, g_ffn_pre, dx3, y, g_mix_post, tm=512)
    dw_out = _grad_matmul(merged, dy, ta=1024, tb=1024, ts=2048, name="grad_w_out")
    token = early_grads(dw_out, dw_gu, dw_down) if early_grads is not None else None
    ga = g_attn_out if token is None else g_attn_out + token[0:1, 0:1]
    do, dl, dcv, db, dg_attn, dg_conv = _mix_bwd(dy, w_out_f, o, cv, bcu, ga, g_conv_out, gsum, tm=512)
    dbcu, dtaps = _conv_bwd(dcv, db, bcu, cw8, tm=512)
    dqp, dkp, dv, dkx = _attn_bwd(qp, kp, vv, do, lse, dl, mk, t=512)
    dfl, dbf = _forget_bwd(dkx, zf, sel, tm=512)
    pieces = (dqp, dkp, dv, dbcu, dfl)
    dwp = _grad_w_in(h1t, pieces)
    token = last_grad(dwp) if last_grad is not None else None
    g1 = g_mix_pre if token is None else g_mix_pre + token[0:1, 0:1]
    grad_x, dg_mix_pre = _in_proj_bwd(pieces, wp, xs, g1, dx2, tm=512)
    return (grad_x, dwp, dw_out, dw_gu, dw_down, dg_mix_pre, dg_mix_post, dg_ffn_pre, dg_ffn_post, dg_attn, dg_conv,
            dtaps, dbf, loss_p)


BIG_TILES = {"w_in": 256, "w_out": 128, "w_gate_up": 176, "w_down": 176}


def kernel(x, w_in, b_forget, conv_w, g_attn_out, g_conv_out, w_out, g_mix_pre, g_mix_post, w_gate_up, w_down, g_ffn_pre, g_ffn_post, loss_target, m_w_in, m_b_forget, m_conv_w, m_g_attn_out, m_g_conv_out, m_w_out, m_g_mix_pre, m_g_mix_post, m_w_gate_up, m_w_down, m_g_ffn_pre, m_g_ffn_post, v_w_in, v_b_forget, v_conv_w, v_g_attn_out, v_g_conv_out, v_w_out, v_g_mix_pre, v_g_mix_post, v_w_gate_up, v_w_down, v_g_ffn_pre, v_g_ffn_post):
    xc, yc, cc = _position()
    my_chip = 2 * xc + yc
    me = 2 * my_chip + cc
    idx = jnp.stack([cc, my_chip]).astype(jnp.int32)
    tables = _in_layout_tables()
    pad_in = lambda a: jnp.pad(a, ((0, 0), (0, IN_PAD - IN_COLS)))

    g_in, g_taps = _all_gather([pad_in(w_in[0]).astype(BF16), conv_w[0]])
    wp = _assemble_w_in(g_in, tables, tr=256)
    cw8 = jnp.pad(g_taps.transpose(1, 0, 2).reshape(3, CW), ((0, SUBLANES - 3), (0, 0)))

    late = [w_out[0].astype(BF16), w_gate_up[0].astype(BF16), w_down[0].astype(BF16)]
    ssem, rsem, late_thru, land_thru, token = _exchange_start(
        late, [_own_slot(s, me) for s in late], mode="gather", name="gather_late_start")
    bfp = jnp.pad(b_forget, ((0, 0), (0, 128 - H))) + token[0:1, :]

    def late_weights(after):
        l_out, l_gu, l_down = _exchange_wait(ssem, rsem, late_thru, land_thru, after, mode="gather", name="gather_late_wait")
        return l_out.reshape(D, D), l_gu.reshape(2, 4, D, FB), l_down.reshape(4, FB, D)

    early = {}

    def early_grads(dw_out, dw_gu, dw_down):
        srcs = [dw_out.reshape(NDEV, D // NDEV, D), dw_gu, dw_down.reshape(NDEV, DFF // NDEV, D)]
        lands = [_own_slot(lax.dynamic_index_in_dim(s, me, 0, keepdims=False), me) for s in srcs]
        early["handles"] = _exchange_start(srcs, lands, mode="scatter", name="scatter_early_start")
        return early["handles"][4]

    last = {}

    def last_grad(dwp):
        g_w_in = _disassemble_w_in(dwp, tables, tr=256).reshape(4, 2, D, IN_PAD)
        (from_sibling,) = _pair_exchange([g_w_in])
        pair_b, last["own"] = _pair_sum(g_w_in, from_sibling, idx, tr=BIG_TILES["w_in"], name="grad_pair_sum_w_in")
        land = lax.dynamic_update_index_in_dim(lax.empty(pair_b.shape, pair_b.dtype),
                                               lax.dynamic_index_in_dim(pair_b, my_chip, 0, keepdims=False), my_chip, 0)
        last["handles"] = _exchange_start([pair_b], [land], mode="chips", name="chips_w_in_start")
        return last["handles"][4]

    (grad_x, dwp, dw_out, dw_gu, dw_down, dg_mix_pre, dg_mix_post, dg_ffn_pre, dg_ffn_post, dg_attn, dg_conv,
     dtaps, dbf, loss_p) = _local_step(x[0], loss_target[0], wp, late_weights, cw8, bfp, g_attn_out, g_conv_out,
                                        g_mix_pre, g_mix_post, g_ffn_pre, g_ffn_post, early_grads, last_grad)

    e_ssem, e_rsem, e_srcs, e_lands, _ = early["handles"]
    land_out, land_gu, land_down = _exchange_wait(e_ssem, e_rsem, e_srcs, e_lands, dg_mix_pre, mode="scatter",
                                                  name="scatter_early_wait")
    res = {}
    big = {"w_out": (land_out, w_out[0], m_w_out[0], v_w_out[0]),
           "w_gate_up": (land_gu, w_gate_up[0].T, m_w_gate_up[0].T, v_w_gate_up[0].T),
           "w_down": (land_down, w_down[0], m_w_down[0], v_w_down[0])}
    for name, (land, w, m, v) in big.items():
        outs = _device_sum_adamw(land, w, m, v, tr=BIG_TILES[name], name="adamw_" + name)
        res[name] = [(o.T if name == "w_gate_up" else o)[None] for o in outs]
    c_ssem, c_rsem, c_srcs, c_lands, _ = last["handles"]
    after = sum(res[n][1][0, :SUBLANES, :LANES] for n in big)
    (from_chips,) = _exchange_wait(c_ssem, c_rsem, c_srcs, c_lands, after, mode="chips", name="chips_w_in_wait")
    outs = _chip_sum_adamw(from_chips, last["own"], idx, pad_in(w_in[0]), pad_in(m_w_in[0]), pad_in(v_w_in[0]),
                           tr=BIG_TILES["w_in"], name="adamw_w_in")
    res["w_in"] = [o[:, :IN_COLS][None] for o in outs]

    small = _small_all_reduce([dg_mix_pre, dg_mix_post, dg_ffn_pre, dg_ffn_post, dg_attn, dg_conv, dtaps, dbf, loss_p])
    taps_full = jnp.concatenate([small[5:6, :CW], small[5:6, CW:], small[6:7, :CW]], axis=0)
    small_grads = {
        "b_forget": small[6:7, CW:CW + H], "conv_w": lax.dynamic_slice(taps_full, (0, me * 64), (3, 64)),
        "g_attn_out": small[4:5, :AW], "g_conv_out": small[4:5, AW:], "g_mix_pre": small[0:1], "g_mix_post": small[1:2],
        "g_ffn_pre": small[2:3], "g_ffn_post": small[3:4]}
    loss = small[6, CW + 128]
    smalls = {"b_forget": (b_forget, m_b_forget, v_b_forget), "conv_w": (conv_w[0], m_conv_w[0], v_conv_w[0]),
              "g_attn_out": (g_attn_out, m_g_attn_out, v_g_attn_out), "g_conv_out": (g_conv_out, m_g_conv_out, v_g_conv_out),
              "g_mix_pre": (g_mix_pre, m_g_mix_pre, v_g_mix_pre), "g_mix_post": (g_mix_post, m_g_mix_post, v_g_mix_post),
              "g_ffn_pre": (g_ffn_pre, m_g_ffn_pre, v_g_ffn_pre), "g_ffn_post": (g_ffn_post, m_g_ffn_post, v_g_ffn_post)}
    for name, (w, m, v) in smalls.items():
        g = small_grads[name]
        outs = [g] + list(_adamw(w, g, m, v, tr=w.shape[0], name="adamw_" + name))
        res[name] = [o[None] for o in outs] if name == "conv_w" else outs

    order = ["w_in", "b_forget", "conv_w", "g_attn_out", "g_conv_out", "w_out", "g_mix_pre", "g_mix_post",
             "w_gate_up", "w_down", "g_ffn_pre", "g_ffn_post"]
    outs = [loss, grad_x[None]]
    for k in range(4):
        outs += [res[n][k] for n in order]
    return tuple(outs)
```

```python
import functools

import numpy as np

import jax
import jax.numpy as jnp
from jax import lax
from jax.experimental import pallas as pl
from jax.experimental.pallas import tpu as pltpu

F32 = jnp.float32
BF16 = jnp.bfloat16
HIGHEST = lax.Precision.HIGHEST
MESH_ID = pl.DeviceIdType.MESH

D = 1024
H = 8
DH = 64
AW = 512
CW = 512
DFF = 2816
FB = DFF // 4
HP = 128
OFF_Q, OFF_K, OFF_V, OFF_BCU, OFF_F = 0, 1024, 2048, 2560, 4096
WP = OFF_F + 128
PIECES = ((OFF_Q, OFF_K), (OFF_K, OFF_V), (OFF_V, OFF_BCU), (OFF_BCU, OFF_F), (OFF_F, WP))
EPS = 1e-6
NDEV = 8
LANES = 128
SUBLANES = 8
IN_COLS = 385
IN_PAD = 512
WIN = 896
ADAM_LR, ADAM_B1, ADAM_B2, ADAM_EPS, ADAM_WD, ADAM_STEP = 0.001, 0.9, 0.999, 1e-08, 0.01, 10

NT = (((1,), (1,)), ((), ()))
TN = (((0,), (0,)), ((), ()))


def _cparams(vmem_mb=None, sem=None):
    kw = {}
    if vmem_mb is not None:
        kw["vmem_limit_bytes"] = vmem_mb << 20
    if sem is not None:
        kw["dimension_semantics"] = sem
    return pltpu.CompilerParams(**kw)


def _full(shape):
    return pl.BlockSpec(shape, lambda *_: (0,) * len(shape))


def _resident(shape):
    return pl.BlockSpec(shape, lambda *_: (0,) * len(shape), pipeline_mode=pl.Buffered(1))


def _rows(tm, width):
    return pl.BlockSpec((tm, width), lambda i: (i, 0))


def _fold8(v):
    r, w = v.shape
    return jnp.sum(v.reshape(r // SUBLANES, SUBLANES, w), axis=0)


def _split_dot(v, m01):
    hi = v.astype(BF16)
    lo = (v - hi.astype(F32)).astype(BF16)
    return (jnp.dot(hi, m01, preferred_element_type=F32)
            + jnp.dot(lo, m01, preferred_element_type=F32))


def _rms_fwd(v, g):
    r = lax.rsqrt(jnp.mean(v * v, axis=-1, keepdims=True) + EPS)
    n = v * r
    return n * g, n, r


def _rms_bwd(do, n, r, g):
    dn = do * g
    return r * (dn - n * jnp.mean(dn * n, axis=-1, keepdims=True)), do * n


def _padded_column(n):
    if n < AW:
        return OFF_Q + HP * (n // DH) + n % DH, 0.125
    if n < 2 * AW:
        m = n - AW
        return OFF_K + HP * (m // DH) + m % DH, 1.0
    if n < 3 * AW:
        return OFF_V + n - 2 * AW, 1.0
    if n < 3 * AW + H:
        return OFF_F + n - 3 * AW, 1.0
    return OFF_BCU + n - 3 * AW - H, 1.0


def _in_layout_tables():
    dest = -np.ones((IN_PAD, LANES), np.int32)
    dest_f = -np.ones((IN_PAD, LANES), np.int32)
    scale = np.zeros((IN_PAD, LANES), np.float32)
    starts = []
    for k in range(NDEV):
        cols = [_padded_column(IN_COLS * k + j) for j in range(IN_COLS)]
        main = [c for c, _ in cols if c < OFF_F]
        ws = min((min(main) // LANES) * LANES, OFF_F - WIN)
        assert ws <= min(main) and max(main) < ws + WIN
        starts.append(ws)
        for j, (c, sc) in enumerate(cols):
            scale[j, k] = sc
            if c < OFF_F:
                dest[j, k] = c - ws
            else:
                dest_f[j, k] = c - OFF_F
    f_shards = tuple(k for k in range(NDEV) if (dest_f[:, k] >= 0).any())
    return tuple(starts), f_shards, jnp.asarray(dest), jnp.asarray(dest_f), jnp.asarray(scale)


def _perm(dest_ref, scale_ref, k, width):
    lane = lax.broadcasted_iota(jnp.int32, (IN_PAD, width), 1)
    return jnp.where(dest_ref[:, k:k + 1] == lane, scale_ref[:, k:k + 1], 0.0).astype(BF16)


def _assemble_w_in(blocks, tables, *, tr):
    starts, f_shards, dest, dest_f, scale = tables

    def body(b_ref, dest_ref, destf_ref, scale_ref, o_ref):
        o_ref[...] = jnp.zeros_like(o_ref)
        for k in range(NDEV):
            b = b_ref[k]
            ws = starts[k]
            part = jnp.dot(b, _perm(dest_ref, scale_ref, k, WIN), preferred_element_type=F32)
            o_ref[:, ws:ws + WIN] += part.astype(BF16)
            if k in f_shards:
                part = jnp.dot(b, _perm(destf_ref, scale_ref, k, 128), preferred_element_type=F32)
                o_ref[:, OFF_F:WP] += part.astype(BF16)

    tab = _full((IN_PAD, LANES))
    return pl.pallas_call(
        body, name="assemble_w_in", grid=(D // tr,),
        in_specs=[pl.BlockSpec((NDEV, tr, IN_PAD), lambda i: (0, i, 0)), tab, tab, tab],
        out_specs=_rows(tr, WP),
        out_shape=jax.ShapeDtypeStruct((D, WP), BF16),
        compiler_params=_cparams(48, ("arbitrary",)),
    )(blocks, dest, dest_f, scale)


def _disassemble_w_in(dwp, tables, *, tr):
    starts, f_shards, dest, dest_f, scale = tables
    width = dwp.shape[1]

    def body(g_ref, dest_ref, destf_ref, scale_ref, o_ref):
        for k in range(NDEV):
            ws = starts[k]
            acc = lax.dot_general(g_ref[:, ws:ws + WIN], _perm(dest_ref, scale_ref, k, WIN), NT, preferred_element_type=F32)
            if k in f_shards:
                acc = acc + lax.dot_general(g_ref[:, OFF_F:WP], _perm(destf_ref, scale_ref, k, 128), NT,
                                            preferred_element_type=F32)
            o_ref[k] = acc.astype(BF16)

    tab = _full((IN_PAD, LANES))
    return pl.pallas_call(
        body, name="disassemble_w_in", grid=(D // tr,),
        in_specs=[_rows(tr, width), tab, tab, tab],
        out_specs=pl.BlockSpec((NDEV, tr, IN_PAD), lambda i: (0, i, 0)),
        out_shape=jax.ShapeDtypeStruct((NDEV, D, IN_PAD), BF16),
        compiler_params=_cparams(48, ("arbitrary",)),
    )(dwp, dest, dest_f, scale)


def _in_proj(x, g1, wp, bfp, pq, pk, oq, ok, *, tm):
    s = x.shape[0]

    def body(x_ref, g_ref, w_ref, bf_ref, pq_ref, pk_ref, oq_ref, ok_ref,
             ht_ref, qp_ref, kp_ref, v_ref, bcu_ref, z_ref, carry):
        @pl.when(pl.program_id(0) == 0)
        def _():
            carry[...] = jnp.zeros_like(carry)

        h = _rms_fwd(x_ref[...], g_ref[...])[0].astype(BF16)
        ht_ref[...] = h.T
        z = jnp.dot(h, w_ref[:, OFF_F:WP], preferred_element_type=F32) + bf_ref[...]
        z_ref[...] = z
        lane = lax.broadcasted_iota(jnp.int32, (tm, 128), 1)
        logf = jnp.where(lane < H, jnp.minimum(z, 0.0) - jnp.log(1.0 + jnp.exp(-jnp.abs(z))), 0.0)
        row = lax.broadcasted_iota(jnp.int32, (tm, tm), 0)
        col = lax.broadcasted_iota(jnp.int32, (tm, tm), 1)
        tri = (col <= row).astype(F32)
        c = jnp.dot(tri, logf, precision=HIGHEST, preferred_element_type=F32) + carry[0:1, :]
        carry[...] = jnp.broadcast_to(c[tm - 1:tm, :], carry.shape)
        c1 = c.astype(BF16).astype(F32)
        r1 = c - c1
        c2 = r1.astype(BF16).astype(F32)
        c3 = (r1 - c2).astype(BF16).astype(F32)
        zc = (c1 + pltpu.roll(c2, 8, axis=1) + pltpu.roll(c3, 16, axis=1)).astype(BF16)
        q = jnp.dot(h, w_ref[:, OFF_Q:OFF_K], preferred_element_type=F32)
        qp_ref[...] = (q + jnp.dot(zc, pq_ref[...], preferred_element_type=F32) + oq_ref[...]).astype(BF16)
        k = jnp.dot(h, w_ref[:, OFF_K:OFF_V], preferred_element_type=F32)
        kp_ref[...] = (k + jnp.dot(zc, pk_ref[...], preferred_element_type=F32) + ok_ref[...]).astype(BF16)
        v_ref[...] = jnp.dot(h, w_ref[:, OFF_V:OFF_BCU], preferred_element_type=F32).astype(BF16)
        bcu_ref[...] = jnp.dot(h, w_ref[:, OFF_BCU:OFF_F], preferred_element_type=F32)

    return pl.pallas_call(
        body, name="in_proj", grid=(s // tm,),
        in_specs=[_rows(tm, D), _full((1, D)), _resident((D, WP)), _full((1, 128)),
                  _full((128, 1024)), _full((128, 1024)), _full((1, 1024)), _full((1, 1024))],
        out_specs=[pl.BlockSpec((D, tm), lambda i: (0, i)), _rows(tm, 1024), _rows(tm, 1024), _rows(tm, AW),
                   _rows(tm, 3 * CW), _rows(tm, 128)],
        out_shape=[jax.ShapeDtypeStruct((D, s), BF16), jax.ShapeDtypeStruct((s, 1024), BF16),
                   jax.ShapeDtypeStruct((s, 1024), BF16), jax.ShapeDtypeStruct((s, AW), BF16),
                   jax.ShapeDtypeStruct((s, 3 * CW), F32), jax.ShapeDtypeStruct((s, 128), F32)],
        scratch_shapes=[pltpu.VMEM((SUBLANES, 128), F32)],
        compiler_params=_cparams(56, ("arbitrary",)),
    )(x, g1, wp, bfp, pq, pk, oq, ok)


def _attn_fwd(qp, kp, v, *, t):
    s = qp.shape[0]
    nq = s // t

    def body(q_ref, k_ref, v_ref, o_ref, lse_ref, mk_ref):
        qi = pl.program_id(1)
        row = lax.broadcasted_iota(jnp.int32, (t, t), 0)
        col = lax.broadcasted_iota(jnp.int32, (t, t), 1)
        lane = lax.broadcasted_iota(jnp.int32, (t, 128), 1)

        def head_step(hh, ki, carry, masked):
            m, l, acc = carry
            off = pl.multiple_of(ki * t, t)
            q = q_ref[:, HP * hh:HP * (hh + 1)]
            k = k_ref[pl.ds(off, t), HP * hh:HP * (hh + 1)]
            sc = lax.dot_general(q, k, NT, preferred_element_type=F32)
            if masked:
                sc = jnp.where(col <= row, sc, -1e30)
            mn = jnp.maximum(m, jnp.max(sc, axis=-1, keepdims=True))
            p = jnp.exp(sc - mn)
            a = jnp.exp(m - mn)
            l = a * l + jnp.sum(p, axis=-1, keepdims=True)
            acc = a * acc + jnp.dot(p.astype(BF16), v_ref[pl.ds(off, t), :], preferred_element_type=F32)
            return mn, l, acc

        def step(ki, carry, masked):
            new = tuple(head_step(hh, ki, carry[hh], masked) for hh in range(2))
            mk_ref[ki] = jnp.where(lane < DH, jnp.broadcast_to(new[0][0], (t, 128)), jnp.broadcast_to(new[1][0], (t, 128)))
            return new

        init = (jnp.full((t, 1), -1e30, F32), jnp.zeros((t, 1), F32), jnp.zeros((t, 128), F32))
        carry = lax.fori_loop(0, qi, functools.partial(step, masked=False), (init, init))
        (m0, l0, acc0), (m1, l1, acc1) = step(qi, carry, True)
        o_ref[...] = jnp.where(lane < DH, acc0 / l0, acc1 / l1)
        lse_ref[...] = jnp.where(lane < DH, jnp.broadcast_to(m0 + jnp.log(l0), (t, 128)),
                                 jnp.broadcast_to(m1 + jnp.log(l1), (t, 128)))

    return pl.pallas_call(
        body, name="attn_fwd", grid=(H // 2, nq),
        in_specs=[pl.BlockSpec((t, 2 * HP), lambda p, i: (i, p)),
                  pl.BlockSpec((s, 2 * HP), lambda p, i: (0, p)),
                  pl.BlockSpec((s, 128), lambda p, i: (0, p))],
        out_specs=[pl.BlockSpec((t, 128), lambda p, i: (i, p)), pl.BlockSpec((t, 128), lambda p, i: (i, p)),
                   pl.BlockSpec((nq, t, 128), lambda p, i: (0, i, p))],
        out_shape=[jax.ShapeDtypeStruct((s, AW), F32), jax.ShapeDtypeStruct((s, AW), F32),
                   jax.ShapeDtypeStruct((nq, s, AW), F32)],
        compiler_params=_cparams(48, ("arbitrary", "arbitrary")),
    )(qp, kp, v)


def _conv_taps(bcu_ref, halo_ref, first, tm):
    z = bcu_ref[:, CW:2 * CW] * bcu_ref[:, 2 * CW:3 * CW]
    zh = jnp.where(first, 0.0, halo_ref[:, CW:2 * CW] * halo_ref[:, 2 * CW:3 * CW])
    row = lax.broadcasted_iota(jnp.int32, (tm, CW), 0)
    z1 = jnp.where(row == 0, zh[7:8, :], pltpu.roll(z, 1, axis=0))
    z2 = jnp.where(row == 0, zh[6:7, :], jnp.where(row == 1, zh[7:8, :], pltpu.roll(z, 2, axis=0)))
    return z, z1, z2


def _halo_before(tm, width):
    return pl.BlockSpec((SUBLANES, width), lambda i: (jnp.maximum(i * (tm // SUBLANES) - 1, 0), 0))


def _mix_out(o, bcu, cw8, ga, gc, gsum, w_out, x, g_post, g_ffn_pre, *, tm):
    s = x.shape[0]

    def body(o_ref, bcu_ref, halo_ref, cw_ref, ga_ref, gc_ref, gs_ref, w_ref, x_ref, g_ref, gf_ref,
             merged_ref, y_ref, x2_ref, cv_ref, h2_ref):
        z, z1, z2 = _conv_taps(bcu_ref, halo_ref, pl.program_id(0) == 0, tm)
        cv = cw_ref[0:1, :] * z2 + cw_ref[1:2, :] * z1 + cw_ref[2:3, :] * z
        cv_ref[...] = cv
        conv = bcu_ref[:, 0:CW] * cv
        ov = o_ref[...]
        ra = lax.rsqrt(_split_dot(ov * ov, gs_ref[...]) * (1.0 / DH) + EPS)
        rc = lax.rsqrt(_split_dot(conv * conv, gs_ref[...]) * (1.0 / DH) + EPS)
        merged = jnp.concatenate([ov * ra * ga_ref[...], conv * rc * gc_ref[...]], axis=1).astype(BF16)
        merged_ref[...] = merged
        y = jnp.dot(merged, w_ref[...], preferred_element_type=F32)
        y_ref[...] = y
        x2 = x_ref[...] + _rms_fwd(y, g_ref[...])[0]
        x2_ref[...] = x2
        h2_ref[...] = _rms_fwd(x2, gf_ref[...])[0].astype(BF16)

    return pl.pallas_call(
        body, name="mix_out", grid=(s // tm,),
        in_specs=[_rows(tm, AW), _rows(tm, 3 * CW), _halo_before(tm, 3 * CW), _full((SUBLANES, CW)),
                  _full((1, AW)), _full((1, CW)), _full((CW, CW)), _resident((D, D)), _rows(tm, D), _full((1, D)),
                  _full((1, D))],
        out_specs=[_rows(tm, D), _rows(tm, D), _rows(tm, D), _rows(tm, CW), _rows(tm, D)],
        out_shape=[jax.ShapeDtypeStruct((s, D), BF16), jax.ShapeDtypeStruct((s, D), F32),
                   jax.ShapeDtypeStruct((s, D), F32), jax.ShapeDtypeStruct((s, CW), F32),
                   jax.ShapeDtypeStruct((s, D), BF16)],
        compiler_params=_cparams(48, ("arbitrary",)),
    )(o, bcu, bcu, cw8, ga, gc, gsum, w_out, x, g_post, g_ffn_pre)


def _ffn_up(h2, wgu, *, tm):
    s = h2.shape[0]

    def body(h_ref, w_ref, gate_ref, up_ref, a_ref):
        h = h_ref[...]
        gate = jnp.dot(h, w_ref[0, 0], preferred_element_type=F32)
        up = jnp.dot(h, w_ref[1, 0], preferred_element_type=F32)
        gate_ref[0] = gate.astype(BF16)
        up_ref[0] = up.astype(BF16)
        a_ref[0] = (gate * jax.nn.sigmoid(gate) * up).astype(BF16)

    blk = pl.BlockSpec((1, tm, FB), lambda j, i: (j, i, 0))
    return pl.pallas_call(
        body, name="ffn_up", grid=(4, s // tm),
        in_specs=[pl.BlockSpec((tm, D), lambda j, i: (i, 0)),
                  pl.BlockSpec((2, 1, D, FB), lambda j, i: (0, j, 0, 0))],
        out_specs=[blk, blk, blk],
        out_shape=[jax.ShapeDtypeStruct((4, s, FB), BF16)] * 3,
        compiler_params=_cparams(48, ("arbitrary", "arbitrary")),
    )(h2, wgu)


def _ffn_down_loss(a, wd, x2, target, g_post, *, tm):
    s = x2.shape[0]

    def body(a_ref, w_ref, x2_ref, t_ref, g_ref, dx3_ref, dff_ref, loss_ref, dg_ref):
        @pl.when(pl.program_id(0) == 0)
        def _():
            loss_ref[...] = jnp.zeros_like(loss_ref)
            dg_ref[...] = jnp.zeros_like(dg_ref)

        ff = jnp.dot(a_ref[0], w_ref[0], preferred_element_type=F32)
        for j in range(1, 4):
            ff = ff + jnp.dot(a_ref[j], w_ref[j], preferred_element_type=F32)
        out, n, r = _rms_fwd(ff, g_ref[...])
        e = x2_ref[...] + out - t_ref[...]
        loss_ref[...] += _fold8(e * e)
        dx3 = e * (1.0 / D)
        dx3_ref[...] = dx3
        dff, dg = _rms_bwd(dx3, n, r, g_ref[...])
        dff_ref[...] = dff.astype(BF16)
        dg_ref[...] += _fold8(dg)

    return pl.pallas_call(
        body, name="ffn_down_loss", grid=(s // tm,),
        in_specs=[pl.BlockSpec((4, tm, FB), lambda i: (0, i, 0)), _resident((4, FB, D)), _rows(tm, D), _rows(tm, D),
                  _full((1, D))],
        out_specs=[_rows(tm, D), _rows(tm, D), _full((SUBLANES, D)), _full((SUBLANES, D))],
        out_shape=[jax.ShapeDtypeStruct((s, D), F32), jax.ShapeDtypeStruct((s, D), BF16),
                   jax.ShapeDtypeStruct((SUBLANES, D), F32), jax.ShapeDtypeStruct((SUBLANES, D), F32)],
        compiler_params=_cparams(48, ("arbitrary",)),
    )(a, wd, x2, target, g_post)


def _ffn_bwd_act(dff, wd, gate, up, *, tm):
    s = dff.shape[0]

    def body(dff_ref, w_ref, gate_ref, up_ref, dgu_ref):
        da = lax.dot_general(dff_ref[...], w_ref[0], NT, preferred_element_type=F32)
        g = gate_ref[0].astype(F32)
        sg = jax.nn.sigmoid(g)
        dgu_ref[0, 0] = (da * up_ref[0].astype(F32) * (sg * (1.0 + g * (1.0 - sg)))).astype(BF16)
        dgu_ref[1, 0] = (da * (g * sg)).astype(BF16)

    blk = pl.BlockSpec((1, tm, FB), lambda j, i: (j, i, 0))
    return pl.pallas_call(
        body, name="ffn_bwd_act", grid=(4, s // tm),
        in_specs=[pl.BlockSpec((tm, D), lambda j, i: (i, 0)), pl.BlockSpec((1, FB, D), lambda j, i: (j, 0, 0)), blk, blk],
        out_specs=pl.BlockSpec((2, 1, tm, FB), lambda j, i: (0, j, i, 0)),
        out_shape=jax.ShapeDtypeStruct((2, 4, s, FB), BF16),
        compiler_params=_cparams(48, ("arbitrary", "arbitrary")),
    )(dff, wd, gate, up)


def _grad_matmul(a, b, *, ta, tb, ts, name):
    s, ka = a.shape
    nb = b.shape[1]
    ts = min(ts, s)
    nk = s // ts

    def body(a_ref, b_ref, o_ref, acc):
        k = pl.program_id(2)

        @pl.when(k == 0)
        def _():
            acc[...] = jnp.zeros_like(acc)

        acc[...] += lax.dot_general(a_ref[...], b_ref[...], TN, preferred_element_type=F32)

        @pl.when(k == nk - 1)
        def _():
            o_ref[...] = acc[...].astype(BF16)

    return pl.pallas_call(
        body, name=name, grid=(ka // ta, nb // tb, nk),
        in_specs=[pl.BlockSpec((ts, ta), lambda i, j, k: (k, i)), pl.BlockSpec((ts, tb), lambda i, j, k: (k, j))],
        out_specs=pl.BlockSpec((ta, tb), lambda i, j, k: (i, j)),
        out_shape=jax.ShapeDtypeStruct((ka, nb), BF16),
        scratch_shapes=[pltpu.VMEM((ta, tb), F32)],
        compiler_params=_cparams(48, ("arbitrary", "arbitrary", "arbitrary")),
    )(a, b)


def _grad_matmul_t(at, b, *, tb, name):
    ka, s = at.shape
    blocked = b.ndim == 3
    nb = b.shape[-1]
    steps = b.shape[0] if blocked else nb // tb
    width = nb if blocked else tb

    def body(a_ref, b_ref, o_ref):
        bv = b_ref[0] if blocked else b_ref[...]
        res = jnp.dot(a_ref[...], bv, preferred_element_type=F32).astype(BF16)
        if blocked:
            o_ref[0] = res
        else:
            o_ref[...] = res

    if blocked:
        b_spec = pl.BlockSpec((1, s, nb), lambda j: (j, 0, 0))
        o_spec = pl.BlockSpec((1, ka, nb), lambda j: (j, 0, 0))
        o_shape = jax.ShapeDtypeStruct((steps, ka, nb), BF16)
    else:
        b_spec = pl.BlockSpec((s, width), lambda j: (0, j))
        o_spec = pl.BlockSpec((ka, width), lambda j: (0, j))
        o_shape = jax.ShapeDtypeStruct((ka, nb), BF16)
    return pl.pallas_call(
        body, name=name, grid=(steps,),
        in_specs=[_resident((ka, s)), b_spec], out_specs=o_spec, out_shape=o_shape,
        compiler_params=_cparams(56, ("arbitrary",)),
    )(at, b)


GW_TILE = 256


def _grad_w_in(h1t, pieces):
    ka, s = h1t.shape
    widths = [p.shape[1] for p in pieces]
    assert all(w % GW_TILE == 0 for w in widths)
    first = [sum(widths[:i]) // GW_TILE for i in range(len(pieces))]
    count = [w // GW_TILE for w in widths]

    def body(a_ref, *refs):
        o_ref = refs[-1]
        j = pl.program_id(0)
        for ref, f0, n in zip(refs[:-1], first, count):
            @pl.when((j >= f0) & (j < f0 + n))
            def _(ref=ref):
                o_ref[...] = jnp.dot(a_ref[...], ref[...], preferred_element_type=F32).astype(BF16)

    def spec(f0, n):
        return pl.BlockSpec((s, GW_TILE), lambda j: (0, jnp.clip(j - f0, 0, n - 1)))

    return pl.pallas_call(
        body, name="grad_w_in", grid=(sum(count),),
        in_specs=[_resident((ka, s))] + [spec(f0, n) for f0, n in zip(first, count)],
        out_specs=pl.BlockSpec((ka, GW_TILE), lambda j: (0, j)),
        out_shape=jax.ShapeDtypeStruct((ka, sum(widths)), BF16),
        compiler_params=_cparams(56, ("arbitrary",)),
    )(h1t, *pieces)


def _grad_matmul_blocks(a, b, *, ts, name):
    nblk = a.shape[0] if a.ndim == 3 else b.shape[0]
    s = a.shape[-2]
    ka, nb = a.shape[-1], b.shape[-1]
    ts = min(ts, s)
    nk = s // ts

    def body(a_ref, b_ref, o_ref, acc):
        k = pl.program_id(1)

        @pl.when(k == 0)
        def _():
            acc[...] = jnp.zeros_like(acc)

        av = a_ref[0] if a.ndim == 3 else a_ref[...]
        bv = b_ref[0] if b.ndim == 3 else b_ref[...]
        acc[...] += lax.dot_general(av, bv, TN, preferred_element_type=F32)

        @pl.when(k == nk - 1)
        def _():
            o_ref[0] = acc[...].astype(BF16)

    def spec(arr, width):
        if arr.ndim == 3:
            return pl.BlockSpec((1, ts, width), lambda j, k: (j, k, 0))
        return pl.BlockSpec((ts, width), lambda j, k: (k, 0))

    return pl.pallas_call(
        body, name=name, grid=(nblk, nk),
        in_specs=[spec(a, ka), spec(b, nb)],
        out_specs=pl.BlockSpec((1, ka, nb), lambda j, k: (j, 0, 0)),
        out_shape=jax.ShapeDtypeStruct((nblk, ka, nb), BF16),
        scratch_shapes=[pltpu.VMEM((ka, nb), F32)],
        compiler_params=_cparams(48, ("arbitrary", "arbitrary")),
    )(a, b)


def _ffn_bwd_in(dgu, wgu, x2, g_pre, dx3, y, g_post, *, tm):
    s = x2.shape[0]

    def body(dgu_ref, w_ref, x2_ref, gpre_ref, dx3_ref, y_ref, gpost_ref,
             dx2_ref, dy_ref, dgpre_ref, dgpost_ref):
        @pl.when(pl.program_id(0) == 0)
        def _():
            dgpre_ref[...] = jnp.zeros_like(dgpre_ref)
            dgpost_ref[...] = jnp.zeros_like(dgpost_ref)

        dh2 = None
        for a in range(2):
            for j in range(4):
                part = lax.dot_general(dgu_ref[a, j], w_ref[a, j], NT, preferred_element_type=F32)
                dh2 = part if dh2 is None else dh2 + part
        _, n2, r2 = _rms_fwd(x2_ref[...], gpre_ref[...])
        dxn, dg = _rms_bwd(dh2, n2, r2, gpre_ref[...])
        dgpre_ref[...] += _fold8(dg)
        dx2 = dx3_ref[...] + dxn
        dx2_ref[...] = dx2
        _, ny, ry = _rms_fwd(y_ref[...], gpost_ref[...])
        dy, dg2 = _rms_bwd(dx2, ny, ry, gpost_ref[...])
        dy_ref[...] = dy.astype(BF16)
        dgpost_ref[...] += _fold8(dg2)

    return pl.pallas_call(
        body, name="ffn_bwd_in", grid=(s // tm,),
        in_specs=[pl.BlockSpec((2, 4, tm, FB), lambda i: (0, 0, i, 0)), _resident((2, 4, D, FB)), _rows(tm, D),
                  _full((1, D)), _rows(tm, D), _rows(tm, D), _full((1, D))],
        out_specs=[_rows(tm, D), _rows(tm, D), _full((SUBLANES, D)), _full((SUBLANES, D))],
        out_shape=[jax.ShapeDtypeStruct((s, D), F32), jax.ShapeDtypeStruct((s, D), BF16),
                   jax.ShapeDtypeStruct((SUBLANES, D), F32), jax.ShapeDtypeStruct((SUBLANES, D), F32)],
        compiler_params=_cparams(56, ("arbitrary",)),
    )(dgu, wgu, x2, g_pre, dx3, y, g_post)


def _mix_bwd(dy, w_out, o, cv, bcu, ga, gc, gsum, *, tm):
    s = dy.shape[0]

    def group_norm_bwd(dn_out, v, g, gs):
        r = lax.rsqrt(_split_dot(v * v, gs) * (1.0 / DH) + EPS)
        n = v * r
        dn = dn_out * g
        return r * (dn - n * (_split_dot(dn * n, gs) * (1.0 / DH))), dn_out * n

    def body(dy_ref, w_ref, o_ref, cv_ref, bcu_ref, ga_ref, gc_ref, gs_ref,
             do_ref, dl_ref, dcv_ref, db_ref, dga_ref, dgc_ref):
        @pl.when(pl.program_id(0) == 0)
        def _():
            dga_ref[...] = jnp.zeros_like(dga_ref)
            dgc_ref[...] = jnp.zeros_like(dgc_ref)

        dm = lax.dot_general(dy_ref[...], w_ref[...], NT, preferred_element_type=F32)
        ov = o_ref[...]
        do, dga = group_norm_bwd(dm[:, 0:AW], ov, ga_ref[...], gs_ref[...])
        dob = do.astype(BF16)
        do_ref[...] = dob
        dl_ref[...] = _split_dot(dob.astype(F32) * ov, gs_ref[...])
        dga_ref[...] += _fold8(dga)
        gate_b = bcu_ref[:, 0:CW]
        cv = cv_ref[...]
        dconv, dgc = group_norm_bwd(dm[:, AW:D], gate_b * cv, gc_ref[...], gs_ref[...])
        dgc_ref[...] += _fold8(dgc)
        dcv_ref[...] = dconv * gate_b
        db_ref[...] = (dconv * cv).astype(BF16)

    return pl.pallas_call(
        body, name="mix_bwd", grid=(s // tm,),
        in_specs=[_rows(tm, D), _resident((D, D)), _rows(tm, AW), _rows(tm, CW), _rows(tm, 3 * CW),
                  _full((1, AW)), _full((1, CW)), _full((CW, CW))],
        out_specs=[_rows(tm, AW), _rows(tm, AW), _rows(tm, CW), _rows(tm, CW),
                   _full((SUBLANES, AW)), _full((SUBLANES, CW))],
        out_shape=[jax.ShapeDtypeStruct((s, AW), BF16), jax.ShapeDtypeStruct((s, AW), F32),
                   jax.ShapeDtypeStruct((s, CW), F32), jax.ShapeDtypeStruct((s, CW), BF16),
                   jax.ShapeDtypeStruct((SUBLANES, AW), F32), jax.ShapeDtypeStruct((SUBLANES, CW), F32)],
        compiler_params=_cparams(48, ("arbitrary",)),
    )(dy, w_out, o, cv, bcu, ga, gc, gsum)


def _conv_bwd(dcv, db, bcu, cw8, *, tm):
    s = dcv.shape[0]
    nt = s // tm

    def body(dcv_ref, nxt_ref, db_ref, bcu_ref, halo_ref, cw_ref, dbcu_ref, dw_ref):
        i = pl.program_id(0)

        @pl.when(i == 0)
        def _():
            dw_ref[...] = jnp.zeros_like(dw_ref)

        z, z1, z2 = _conv_taps(bcu_ref, halo_ref, i == 0, tm)
        d = dcv_ref[...]
        dw_ref[0] += _fold8(d * z2)
        dw_ref[1] += _fold8(d * z1)
        dw_ref[2] += _fold8(d * z)
        nx = jnp.where(i == nt - 1, 0.0, nxt_ref[...])
        row = lax.broadcasted_iota(jnp.int32, (tm, CW), 0)
        d1 = jnp.where(row == tm - 1, nx[0:1, :], pltpu.roll(d, tm - 1, axis=0))
        d2 = jnp.where(row == tm - 2, nx[0:1, :], jnp.where(row == tm - 1, nx[1:2, :], pltpu.roll(d, tm - 2, axis=0)))
        dz = cw_ref[2:3, :] * d + cw_ref[1:2, :] * d1 + cw_ref[0:1, :] * d2
        dbcu_ref[:, 0:CW] = db_ref[...]
        dbcu_ref[:, CW:2 * CW] = (dz * bcu_ref[:, 2 * CW:3 * CW]).astype(BF16)
        dbcu_ref[:, 2 * CW:3 * CW] = (dz * bcu_ref[:, CW:2 * CW]).astype(BF16)

    return pl.pallas_call(
        body, name="conv_bwd", grid=(nt,),
        in_specs=[_rows(tm, CW),
                  pl.BlockSpec((SUBLANES, CW), lambda i: (jnp.minimum((i + 1) * (tm // SUBLANES), s // SUBLANES - 1), 0)),
                  _rows(tm, CW), _rows(tm, 3 * CW), _halo_before(tm, 3 * CW), _full((SUBLANES, CW))],
        out_specs=[_rows(tm, 3 * CW), _full((3, SUBLANES, CW))],
        out_shape=[jax.ShapeDtypeStruct((s, 3 * CW), BF16), jax.ShapeDtypeStruct((3, SUBLANES, CW), F32)],
        compiler_params=_cparams(48, ("arbitrary",)),
    )(dcv, dcv, db, bcu, bcu, cw8)


def _attn_bwd(qp, kp, v, do, lse, dl, mk, *, t):
    s = qp.shape[0]
    nq = s // t

    def body(q_ref, k_ref, v_ref, do_ref, lse_ref, dl_ref, mk_ref, dq_ref, dk_ref, dv_ref, dkx_ref, dq_acc):
        ki = pl.program_id(1)

        @pl.when(ki == 0)
        def _():
            dq_acc[...] = jnp.zeros_like(dq_acc)

        row = lax.broadcasted_iota(jnp.int32, (t, t), 0)
        col = lax.broadcasted_iota(jnp.int32, (t, t), 1)
        lane = lax.broadcasted_iota(jnp.int32, (t, 128), 1)

        def head_step(hh, qi, carry, masked):
            dk, dv, cs = carry
            off = pl.multiple_of(qi * t, t)
            rows = pl.ds(off, t)
            kh = k_ref[:, HP * hh:HP * (hh + 1)]
            q = q_ref[rows, HP * hh:HP * (hh + 1)]
            in_head = (lane >= DH * hh) & (lane < DH * (hh + 1))
            m_col = mk_ref[0, rows, DH * hh:DH * hh + 1]
            scale = jnp.exp(m_col - lse_ref[rows, DH * hh:DH * hh + 1])
            dom = jnp.where(in_head, do_ref[rows, :], jnp.zeros((), BF16))
            sc = lax.dot_general(q, kh, NT, preferred_element_type=F32) - m_col
            if masked:
                sc = jnp.where(col <= row, sc, -1e30)
            pt = jnp.exp(sc).astype(BF16)
            dp = lax.dot_general(dom, v_ref[...], NT, preferred_element_type=F32)
            ds32 = (pt.astype(F32) * scale) * (dp - dl_ref[rows, DH * hh:DH * hh + 1])
            ds = ds32.astype(BF16)
            cs = cs + _fold8(ds32)
            dv = dv + jnp.dot((dom.astype(F32) * scale).astype(BF16).T, pt, preferred_element_type=F32)
            dk = dk + jnp.dot(q.T, ds, preferred_element_type=F32)
            dq_acc[rows, HP * hh:HP * (hh + 1)] += jnp.dot(ds, kh, preferred_element_type=F32)
            return dk, dv, cs

        def step(qi, carry, masked):
            return tuple(head_step(hh, qi, carry[hh], masked) for hh in range(2))

        zero = (jnp.zeros((HP, t), F32), jnp.zeros((128, t), F32), jnp.zeros((SUBLANES, t), F32))
        carry = step(ki, (zero, zero), True)
        (dk0, dv0, cs0), (dk1, dv1, cs1) = lax.fori_loop(ki + 1, nq, functools.partial(step, masked=False), carry)
        dk_ref[:, 0:HP] = dk0.T.astype(BF16)
        dk_ref[:, HP:2 * HP] = dk1.T.astype(BF16)
        dv_ref[...] = (dv0 + dv1).T.astype(BF16)

        def as_column(cs):
            return lax.dot_general(cs, jnp.ones((SUBLANES, 128), F32), TN, precision=HIGHEST, preferred_element_type=F32)

        dkx_ref[...] = jnp.where(lane < DH, as_column(cs0), as_column(cs1))

        @pl.when(ki == nq - 1)
        def _():
            dq_ref[...] = dq_acc[...].astype(BF16)

    return pl.pallas_call(
        body, name="attn_bwd", grid=(H // 2, nq),
        in_specs=[pl.BlockSpec((s, 2 * HP), lambda p, i: (0, p)),
                  pl.BlockSpec((t, 2 * HP), lambda p, i: (i, p)),
                  pl.BlockSpec((t, 128), lambda p, i: (i, p)),
                  pl.BlockSpec((s, 128), lambda p, i: (0, p)),
                  pl.BlockSpec((s, 128), lambda p, i: (0, p)),
                  pl.BlockSpec((s, 128), lambda p, i: (0, p)),
                  pl.BlockSpec((1, s, 128), lambda p, i: (i, 0, p))],
        out_specs=[pl.BlockSpec((s, 2 * HP), lambda p, i: (0, p)),
                   pl.BlockSpec((t, 2 * HP), lambda p, i: (i, p)),
                   pl.BlockSpec((t, 128), lambda p, i: (i, p)),
                   pl.BlockSpec((t, 128), lambda p, i: (i, p))],
        out_shape=[jax.ShapeDtypeStruct((s, 1024), BF16), jax.ShapeDtypeStruct((s, 1024), BF16),
                   jax.ShapeDtypeStruct((s, AW), BF16), jax.ShapeDtypeStruct((s, AW), F32)],
        scratch_shapes=[pltpu.VMEM((s, 2 * HP), F32)],
        compiler_params=_cparams(56, ("arbitrary", "arbitrary")),
    )(qp, kp, v, do, lse, dl, mk)


def _forget_bwd(dkx, z, sel, *, tm):
    s = dkx.shape[0]
    nt = s // tm

    def body(dk_ref, z_ref, sel_ref, dfl_ref, dbf_ref, carry):
        @pl.when(pl.program_id(0) == 0)
        def _():
            carry[...] = jnp.zeros_like(carry)
            dbf_ref[...] = jnp.zeros_like(dbf_ref)

        dc = _split_dot(dk_ref[...], sel_ref[...])
        row = lax.broadcasted_iota(jnp.int32, (tm, tm), 0)
        col = lax.broadcasted_iota(jnp.int32, (tm, tm), 1)
        tri = (col >= row).astype(F32)
        dlogf = jnp.dot(tri, dc, precision=HIGHEST, preferred_element_type=F32) + carry[0:1, :]
        carry[...] = jnp.broadcast_to(dlogf[0:1, :], carry.shape)
        dz = dlogf * (1.0 - jax.nn.sigmoid(z_ref[...]))
        dfl_ref[:, 0:128] = dz.astype(BF16)
        dfl_ref[:, 128:GW_TILE] = jnp.zeros((tm, GW_TILE - 128), BF16)
        dbf_ref[...] += _fold8(dz)

    rev = lambda i: (nt - 1 - i, 0)
    return pl.pallas_call(
        body, name="forget_bwd", grid=(nt,),
        in_specs=[pl.BlockSpec((tm, AW), rev), pl.BlockSpec((tm, 128), rev), _full((AW, 128))],
        out_specs=[pl.BlockSpec((tm, GW_TILE), rev), _full((SUBLANES, 128))],
        out_shape=[jax.ShapeDtypeStruct((s, GW_TILE), BF16), jax.ShapeDtypeStruct((SUBLANES, 128), F32)],
        scratch_shapes=[pltpu.VMEM((SUBLANES, 128), F32)],
        compiler_params=_cparams(48, ("arbitrary",)),
    )(dkx, z, sel)


def _in_proj_bwd(pieces, wp, x, g1, dx2, *, tm):
    s = x.shape[0]

    def body(q_ref, k_ref, v_ref, bcu_ref, f_ref, w_ref, x_ref, g_ref, dx2_ref, dx_ref, dg_ref):
        @pl.when(pl.program_id(0) == 0)
        def _():
            dg_ref[...] = jnp.zeros_like(dg_ref)

        dh = None
        for ref, (lo, hi) in zip((q_ref, k_ref, v_ref, bcu_ref, f_ref), PIECES):
            part = lax.dot_general(ref[...], w_ref[:, lo:hi], NT, preferred_element_type=F32)
            dh = part if dh is None else dh + part
        _, n, r = _rms_fwd(x_ref[...], g_ref[...])
        dxn, dg = _rms_bwd(dh, n, r, g_ref[...])
        dx_ref[...] = dx2_ref[...] + dxn
        dg_ref[...] += _fold8(dg)

    return pl.pallas_call(
        body, name="in_proj_bwd", grid=(s // tm,),
        in_specs=[_rows(tm, hi - lo) for lo, hi in PIECES] + [_resident((D, WP)), _rows(tm, D), _full((1, D)), _rows(tm, D)],
        out_specs=[_rows(tm, D), _full((SUBLANES, D))],
        out_shape=[jax.ShapeDtypeStruct((s, D), F32), jax.ShapeDtypeStruct((SUBLANES, D), F32)],
        compiler_params=_cparams(56, ("arbitrary",)),
    )(*pieces, wp, x, g1, dx2)


def _position():
    return lax.axis_index("x"), lax.axis_index("y"), lax.axis_index("c")


ANY = pl.BlockSpec(memory_space=pl.ANY)


def _all_gather(shards):
    n = len(shards)

    def body(*refs):
        x_refs, out_refs = refs[:n], refs[n:2 * n]
        send_sems, recv_sems, local_sems = refs[2 * n:]
        x, y, c = _position()
        me, sibling = (x, y, c), (x, y, 1 - c)
        chips = [(1 - x, y), (x, 1 - y), (1 - x, 1 - y)]

        def copy(a, k, block, to, own=False):
            slot = out_refs[a].at[4 * block[0] + 2 * block[1] + block[2]]
            return pltpu.make_async_remote_copy(
                src_ref=x_refs[a] if own else slot, dst_ref=slot,
                send_sem=send_sems.at[7 * a + k], recv_sem=recv_sems.at[7 * a + k], device_id=to, device_id_type=MESH_ID)

        mine = [pltpu.make_async_copy(x_refs[a], out_refs[a].at[4 * x + 2 * y + c], local_sems.at[a]) for a in range(n)]
        for cp in mine:
            cp.start()
        first = []
        for a in range(n):
            first.append(copy(a, 0, me, sibling, own=True))
            first += [copy(a, 1 + j, me, (*chip, c), own=True) for j, chip in enumerate(chips)]
        for cp in first:
            cp.start()
        passed = []
        for j, chip in enumerate(chips):
            for a in range(n):
                copy(a, 1 + j, (*chip, c), me).wait_recv()
                fwd = copy(a, 4 + j, (*chip, c), sibling)
                fwd.start()
                passed.append(fwd)
        for a in range(n):
            copy(a, 0, sibling, me).wait_recv()
            for j, chip in enumerate(chips):
                copy(a, 4 + j, (*chip, 1 - c), me).wait_recv()
        for cp in first + passed:
            cp.wait_send()
        for cp in mine:
            cp.wait()

    return pl.pallas_call(
        body, name="all_gather_weights",
        out_shape=[jax.ShapeDtypeStruct((NDEV,) + sh.shape, sh.dtype) for sh in shards],
        in_specs=[ANY] * n, out_specs=[ANY] * n,
        scratch_shapes=[pltpu.SemaphoreType.DMA((7 * n,)), pltpu.SemaphoreType.DMA((7 * n,)), pltpu.SemaphoreType.DMA((n,))],
    )(*shards)


def _pair_exchange(grads):
    n = len(grads)

    def body(*refs):
        g_refs, out_refs = refs[:n], refs[n:2 * n]
        send_sems, recv_sems = refs[2 * n:]
        x, y, c = _position()
        copies = [pltpu.make_async_remote_copy(
            src_ref=g_refs[a].at[:, pl.ds(1 - c, 1)], dst_ref=out_refs[a], send_sem=send_sems.at[a],
            recv_sem=recv_sems.at[a], device_id=(x, y, 1 - c), device_id_type=MESH_ID) for a in range(n)]
        for cp in copies:
            cp.start()
        for cp in copies:
            cp.wait()

    return pl.pallas_call(
        body, name="grad_pair_exchange",
        out_shape=[jax.ShapeDtypeStruct((4, 1) + g.shape[2:], g.dtype) for g in grads],
        in_specs=[ANY] * n, out_specs=[ANY] * n,
        scratch_shapes=[pltpu.SemaphoreType.DMA((n,)), pltpu.SemaphoreType.DMA((n,))],
    )(*grads)


def _pair_sum(g, got, idx, *, tr, name):
    r, c = g.shape[2:]

    def body(idx_ref, g_ref, got_ref, pb_ref, own_ref):
        p = g_ref[0, 0].astype(F32) + got_ref[0, 0].astype(F32)
        pb_ref[0] = p.astype(BF16)

        @pl.when(pl.program_id(1) == idx_ref[1])
        def _():
            own_ref[...] = p

    return pl.pallas_call(
        body, name=name,
        grid_spec=pltpu.PrefetchScalarGridSpec(
            num_scalar_prefetch=1, grid=(r // tr, 4),
            in_specs=[pl.BlockSpec((1, 1, tr, c), lambda i, j, idx: (j, idx[0], i, 0)),
                      pl.BlockSpec((1, 1, tr, c), lambda i, j, idx: (j, 0, i, 0))],
            out_specs=[pl.BlockSpec((1, tr, c), lambda i, j, idx: (j, i, 0)),
                       pl.BlockSpec((tr, c), lambda i, j, idx: (i, 0))]),
        out_shape=[jax.ShapeDtypeStruct((4, r, c), BF16), jax.ShapeDtypeStruct((r, c), F32)],
        compiler_params=_cparams(32, ("arbitrary", "arbitrary")),
    )(idx, g, got)


HBM = pl.BlockSpec(memory_space=pltpu.HBM)
SEM = pl.BlockSpec(memory_space=pltpu.SEMAPHORE)
DATAFLOW = pltpu.SideEffectType.DATAFLOW_SIDE_EFFECTING


PEERS = {"gather": NDEV - 1, "scatter": NDEV - 1, "chips": 3}


def _exchange_copies(src_refs, land_refs, send_sems, recv_sems, mode):
    x, y, c = _position()
    me, my_chip = 4 * x + 2 * y + c, 2 * x + y
    npeers = PEERS[mode]
    copies = []
    for a, (s_ref, l_ref) in enumerate(zip(src_refs, land_refs)):
        for k in range(npeers):
            if mode == "chips":
                px, py, pc = x ^ ((k + 1) >> 1), y ^ ((k + 1) & 1), c
                src, dst = s_ref.at[2 * px + py], l_ref.at[my_chip]
            else:
                px, py, pc = x ^ ((k + 1) >> 2), y ^ (((k + 1) >> 1) & 1), c ^ ((k + 1) & 1)
                src, dst = (s_ref.at[4 * px + 2 * py + pc] if mode == "scatter" else s_ref), l_ref.at[me]
            copies.append(pltpu.make_async_remote_copy(
                src_ref=src, dst_ref=dst, send_sem=send_sems.at[npeers * a + k], recv_sem=recv_sems.at[npeers * a + k],
                device_id=(px, py, pc), device_id_type=MESH_ID))
    return copies


def _exchange_start(srcs, lands, *, mode, name):
    n = len(srcs)
    nsem = PEERS[mode] * n

    def body(*refs):
        token = refs[-1]
        for cp in _exchange_copies(refs[:n], refs[n:2 * n], refs[2 * n], refs[2 * n + 1], mode):
            cp.start()
        token[...] = jnp.zeros_like(token)

    arrays = list(srcs) + list(lands)
    outs = pl.pallas_call(
        body, name=name,
        out_shape=(pltpu.SemaphoreType.DMA((nsem,)), pltpu.SemaphoreType.DMA((nsem,)),
                   *[pltpu.HBM(a.shape, a.dtype) for a in arrays], jax.ShapeDtypeStruct((SUBLANES, LANES), F32)),
        in_specs=[HBM] * (2 * n),
        out_specs=(SEM, SEM, *[HBM] * (2 * n), pl.BlockSpec(memory_space=pltpu.VMEM)),
        input_output_aliases={i: 2 + i for i in range(2 * n)},
        compiler_params=pltpu.CompilerParams(has_side_effects=DATAFLOW),
    )(*[pltpu.with_memory_space_constraint(a, pltpu.HBM) for a in arrays])
    return outs[0], outs[1], outs[2:2 + n], outs[2 + n:2 + 2 * n], outs[-1]


def _exchange_wait(send_sems, recv_sems, srcs, lands, after, *, mode, name):
    n = len(srcs)

    def body(*refs):
        for cp in _exchange_copies(refs[:n], refs[n:2 * n], refs[2 * n], refs[2 * n + 1], mode):
            cp.wait_send()
            cp.wait_recv()

    arrays = list(srcs) + list(lands)
    outs = pl.pallas_call(
        body, name=name,
        out_shape=tuple(pltpu.HBM(a.shape, a.dtype) for a in arrays),
        in_specs=[HBM] * (2 * n) + [SEM, SEM, ANY],
        out_specs=tuple([HBM] * (2 * n)),
        input_output_aliases={i: i for i in range(2 * n)},
        compiler_params=pltpu.CompilerParams(has_side_effects=DATAFLOW),
    )(*arrays, send_sems, recv_sems, after)
    return outs[n:]


def _own_slot(value, me):
    return lax.dynamic_update_index_in_dim(lax.empty((NDEV,) + value.shape, value.dtype), value, me, 0)


def _small_all_reduce(parts):
    def body(gmp_ref, gmo_ref, gfp_ref, gfo_ref, ga_ref, gc_ref, dw_ref, bf_ref, loss_ref,
             out_ref, buf, send_sems, recv_sems):
        x, y, c = _position()
        me = 4 * x + 2 * y + c

        def colsum(v):
            return jnp.sum(v, axis=0, keepdims=True)

        loss = jnp.sum(colsum(loss_ref[...]), axis=1, keepdims=True) * (0.5 / D)
        rows = [colsum(gmp_ref[...]), colsum(gmo_ref[...]), colsum(gfp_ref[...]), colsum(gfo_ref[...]),
                jnp.concatenate([colsum(ga_ref[...]), colsum(gc_ref[...])], axis=1),
                jnp.concatenate([colsum(dw_ref[0]), colsum(dw_ref[1])], axis=1),
                jnp.concatenate([colsum(dw_ref[2]), colsum(bf_ref[...]), jnp.broadcast_to(loss, (1, 128)),
                                 jnp.zeros((1, 256), F32)], axis=1),
                jnp.zeros((1, D), F32)]
        buf[me] = jnp.concatenate(rows, axis=0)
        copies = []
        for mm in range(1, NDEV):
            peer = (x ^ (mm >> 2), y ^ ((mm >> 1) & 1), c ^ (mm & 1))
            copies.append(pltpu.make_async_remote_copy(
                src_ref=buf.at[me], dst_ref=buf.at[me], send_sem=send_sems.at[mm - 1], recv_sem=recv_sems.at[mm - 1],
                device_id=peer, device_id_type=MESH_ID))
        for cp in copies:
            cp.start()
        for cp in copies:
            cp.wait_recv()
        for cp in copies:
            cp.wait_send()
        acc = buf[0]
        for d in range(1, NDEV):
            acc = acc + buf[d]
        out_ref[...] = acc

    vm = pl.BlockSpec(memory_space=pltpu.VMEM)
    return pl.pallas_call(
        body, name="small_all_reduce",
        out_shape=jax.ShapeDtypeStruct((SUBLANES, D), F32),
        in_specs=[vm] * len(parts), out_specs=vm,
        scratch_shapes=[pltpu.VMEM((NDEV, SUBLANES, D), F32), pltpu.SemaphoreType.DMA((7,)), pltpu.SemaphoreType.DMA((7,))],
    )(*parts)


def _adam_update(w, g, m, v):
    nm = ADAM_B1 * m + (1.0 - ADAM_B1) * g
    nv = ADAM_B2 * v + (1.0 - ADAM_B2) * (g * g)
    m_hat = nm / (1.0 - ADAM_B1 ** ADAM_STEP)
    v_hat = nv / (1.0 - ADAM_B2 ** ADAM_STEP)
    return -ADAM_LR * (m_hat / (jnp.sqrt(v_hat) + ADAM_EPS) + ADAM_WD * w), nm, nv


def _adamw(w, g, m, v, *, tr, name):
    rows, cols = w.shape

    def body(w_ref, g_ref, m_ref, v_ref, d_ref, nm_ref, nv_ref):
        d_ref[...], nm_ref[...], nv_ref[...] = _adam_update(w_ref[...], g_ref[...], m_ref[...], v_ref[...])

    spec = pl.BlockSpec((tr, cols), lambda i: (i, 0))
    return pl.pallas_call(
        body, name=name, grid=(rows // tr,),
        in_specs=[spec] * 4, out_specs=[spec] * 3,
        out_shape=[jax.ShapeDtypeStruct((rows, cols), F32)] * 3,
        compiler_params=_cparams(32, ("arbitrary",)),
    )(w, g, m, v)


def _chip_sum_adamw(got, own, idx, w, m, v, *, tr, name):
    rows, cols = w.shape

    def body(idx_ref, got_ref, own_ref, w_ref, m_ref, v_ref, g_ref, d_ref, nm_ref, nv_ref):
        g = jnp.zeros((tr, cols), F32)
        for j in range(4):
            g = g + jnp.where(idx_ref[1] == j, own_ref[...], got_ref[j].astype(F32))
        g_ref[...] = g
        d_ref[...], nm_ref[...], nv_ref[...] = _adam_update(w_ref[...], g, m_ref[...], v_ref[...])

    spec = pl.BlockSpec((tr, cols), lambda i, idx: (i, 0))
    return pl.pallas_call(
        body, name=name,
        grid_spec=pltpu.PrefetchScalarGridSpec(
            num_scalar_prefetch=1, grid=(rows // tr,),
            in_specs=[pl.BlockSpec((4, tr, cols), lambda i, idx: (0, i, 0)), spec, spec, spec, spec],
            out_specs=[spec] * 4),
        out_shape=[jax.ShapeDtypeStruct((rows, cols), F32)] * 4,
        compiler_params=_cparams(32, ("arbitrary",)),
    )(idx, got, own, w, m, v)


def _device_sum_adamw(land, w, m, v, *, tr, name):
    rows, cols = w.shape

    def body(land_ref, w_ref, m_ref, v_ref, g_ref, d_ref, nm_ref, nv_ref):
        g = land_ref[0].astype(F32)
        for dev in range(1, NDEV):
            g = g + land_ref[dev].astype(F32)
        g_ref[...] = g
        d_ref[...], nm_ref[...], nv_ref[...] = _adam_update(w_ref[...], g, m_ref[...], v_ref[...])

    spec = pl.BlockSpec((tr, cols), lambda i: (i, 0))
    return pl.pallas_call(
        body, name=name, grid=(rows // tr,),
        in_specs=[pl.BlockSpec((NDEV, tr, cols), lambda i: (0, i, 0)), spec, spec, spec],
        out_specs=[spec] * 4,
        out_shape=[jax.ShapeDtypeStruct((rows, cols), F32)] * 4,
        compiler_params=_cparams(32, ("arbitrary",)),
    )(land, w, m, v)


def _placement_constants():
    j = jnp.arange(128)[:, None]
    lane = jnp.arange(1024)[None, :]
    head, sub = lane // HP, lane % HP
    piece, jh = j // H, j % H
    valid = (j < 3 * H) & (jh == head)
    pq = jnp.where(valid & (sub == DH + piece), 1.0, 0.0).astype(BF16)
    pk = jnp.where(valid & (sub == DH + 3 + piece), -1.0, 0.0).astype(BF16)
    oq = jnp.where((sub >= DH + 3) & (sub < DH + 6), 1.0, 0.0).astype(F32)
    ok = jnp.where((sub >= DH) & (sub < DH + 3), 1.0, 0.0).astype(F32)
    r = jnp.arange(AW)[:, None]
    cc = jnp.arange(128)[None, :]
    sel = jnp.where((r % DH == 3) & (r // DH == cc), -1.0, 0.0).astype(BF16)
    gi = jnp.arange(CW)
    gsum = (gi[:, None] // DH == gi[None, :] // DH).astype(BF16)
    return pq, pk, oq, ok, sel, gsum


def _local_step(xs, tgt, wp, late_weights, cw8, bfp, g_attn_out, g_conv_out,
                g_mix_pre, g_mix_post, g_ffn_pre, g_ffn_post, early_grads=None, last_grad=None):
    pq, pk, oq, ok, sel, gsum = _placement_constants()
    h1t, qp, kp, vv, bcu, zf = _in_proj(xs, g_mix_pre, wp, bfp, pq, pk, oq, ok, tm=512)
    o, lse, mk = _attn_fwd(qp, kp, vv, t=512)
    w_out_f, wgu, wd = late_weights(lse)
    merged, y, x2, cv, h2 = _mix_out(o, bcu, cw8, g_attn_out, g_conv_out, gsum, w_out_f, xs, g_mix_post, g_ffn_pre, tm=512)
    gate, up, act = _ffn_up(h2, wgu, tm=1024)
    dx3, dff, loss_p, dg_ffn_post = _ffn_down_loss(act, wd, x2, tgt, g_ffn_post, tm=512)

    dgu = _ffn_bwd_act(dff, wd, gate, up, tm=1024)
    dw_down = _grad_matmul_blocks(act, dff, ts=4096, name="grad_w_down")
    dw_gu = _grad_matmul_blocks(dgu.reshape(NDEV, -1, FB), h2, ts=4096, name="grad_w_gate_up")
    dx2, dy, dg_ffn_pre, dg_mix_post = _ffn_bwd_in(dgu, wgu, x2, g_ffn_pre, dx3, y, g_mix_post, tm=512)
    dw_out = _grad_matmul(merged, dy, ta=1024, tb=1024, ts=2048, name="grad_w_out")
    token = early_grads(dw_out, dw_gu, dw_down) if early_grads is not None else None
    ga = g_attn_out if token is None else g_attn_out + token[0:1, 0:1]
    do, dl, dcv, db, dg_attn, dg_conv = _mix_bwd(dy, w_out_f, o, cv, bcu, ga, g_conv_out, gsum, tm=512)
    dbcu, dtaps = _conv_bwd(dcv, db, bcu, cw8, tm=512)
    dqp, dkp, dv, dkx = _attn_bwd(qp, kp, vv, do, lse, dl, mk, t=512)
    dfl, dbf = _forget_bwd(dkx, zf, sel, tm=512)
    pieces = (dqp, dkp, dv, dbcu, dfl)
    dwp = _grad_w_in(h1t, pieces)
    token = last_grad(dwp) if last_grad is not None else None
    g1 = g_mix_pre if token is None else g_mix_pre + token[0:1, 0:1]
    grad_x, dg_mix_pre = _in_proj_bwd(pieces, wp, xs, g1, dx2, tm=512)
    return (grad_x, dwp, dw_out, dw_gu, dw_down, dg_mix_pre, dg_mix_post, dg_ffn_pre, dg_ffn_post, dg_attn, dg_conv,
            dtaps, dbf, loss_p)


BIG_TILES = {"w_in": 256, "w_out": 128, "w_gate_up": 176, "w_down": 176}


def kernel(x, w_in, b_forget, conv_w, g_attn_out, g_conv_out, w_out, g_mix_pre, g_mix_post, w_gate_up, w_down, g_ffn_pre, g_ffn_post, loss_target, m_w_in, m_b_forget, m_conv_w, m_g_attn_out, m_g_conv_out, m_w_out, m_g_mix_pre, m_g_mix_post, m_w_gate_up, m_w_down, m_g_ffn_pre, m_g_ffn_post, v_w_in, v_b_forget, v_conv_w, v_g_attn_out, v_g_conv_out, v_w_out, v_g_mix_pre, v_g_mix_post, v_w_gate_up, v_w_down, v_g_ffn_pre, v_g_ffn_post):
    xc, yc, cc = _position()
    my_chip = 2 * xc + yc
    me = 2 * my_chip + cc
    idx = jnp.stack([cc, my_chip]).astype(jnp.int32)
    tables = _in_layout_tables()
    pad_in = lambda a: jnp.pad(a, ((0, 0), (0, IN_PAD - IN_COLS)))

    g_in, g_taps = _all_gather([pad_in(w_in[0]).astype(BF16), conv_w[0]])
    wp = _assemble_w_in(g_in, tables, tr=256)
    cw8 = jnp.pad(g_taps.transpose(1, 0, 2).reshape(3, CW), ((0, SUBLANES - 3), (0, 0)))

    late = [w_out[0].astype(BF16), w_gate_up[0].astype(BF16), w_down[0].astype(BF16)]
    ssem, rsem, late_thru, land_thru, token = _exchange_start(
        late, [_own_slot(s, me) for s in late], mode="gather", name="gather_late_start")
    bfp = jnp.pad(b_forget, ((0, 0), (0, 128 - H))) + token[0:1, :]

    def late_weights(after):
        l_out, l_gu, l_down = _exchange_wait(ssem, rsem, late_thru, land_thru, after, mode="gather", name="gather_late_wait")
        return l_out.reshape(D, D), l_gu.reshape(2, 4, D, FB), l_down.reshape(4, FB, D)

    early = {}

    def early_grads(dw_out, dw_gu, dw_down):
        srcs = [dw_out.reshape(NDEV, D // NDEV, D), dw_gu, dw_down.reshape(NDEV, DFF // NDEV, D)]
        lands = [_own_slot(lax.dynamic_index_in_dim(s, me, 0, keepdims=False), me) for s in srcs]
        early["handles"] = _exchange_start(srcs, lands, mode="scatter", name="scatter_early_start")
        return early["handles"][4]

    last = {}

    def last_grad(dwp):
        g_w_in = _disassemble_w_in(dwp, tables, tr=256).reshape(4, 2, D, IN_PAD)
        (from_sibling,) = _pair_exchange([g_w_in])
        pair_b, last["own"] = _pair_sum(g_w_in, from_sibling, idx, tr=BIG_TILES["w_in"], name="grad_pair_sum_w_in")
        land = lax.dynamic_update_index_in_dim(lax.empty(pair_b.shape, pair_b.dtype),
                                               lax.dynamic_index_in_dim(pair_b, my_chip, 0, keepdims=False), my_chip, 0)
        last["handles"] = _exchange_start([pair_b], [land], mode="chips", name="chips_w_in_start")
        return last["handles"][4]

    (grad_x, dwp, dw_out, dw_gu, dw_down, dg_mix_pre, dg_mix_post, dg_ffn_pre, dg_ffn_post, dg_attn, dg_conv,
     dtaps, dbf, loss_p) = _local_step(x[0], loss_target[0], wp, late_weights, cw8, bfp, g_attn_out, g_conv_out,
                                        g_mix_pre, g_mix_post, g_ffn_pre, g_ffn_post, early_grads, last_grad)

    e_ssem, e_rsem, e_srcs, e_lands, _ = early["handles"]
    land_out, land_gu, land_down = _exchange_wait(e_ssem, e_rsem, e_srcs, e_lands, dg_mix_pre, mode="scatter",
                                                  name="scatter_early_wait")
    res = {}
    big = {"w_out": (land_out, w_out[0], m_w_out[0], v_w_out[0]),
           "w_gate_up": (land_gu, w_gate_up[0].T, m_w_gate_up[0].T, v_w_gate_up[0].T),
           "w_down": (land_down, w_down[0], m_w_down[0], v_w_down[0])}
    for name, (land, w, m, v) in big.items():
        outs = _device_sum_adamw(land, w, m, v, tr=BIG_TILES[name], name="adamw_" + name)
        res[name] = [(o.T if name == "w_gate_up" else o)[None] for o in outs]
    c_ssem, c_rsem, c_srcs, c_lands, _ = last["handles"]
    after = sum(res[n][1][0, :SUBLANES, :LANES] for n in big)
    (from_chips,) = _exchange_wait(c_ssem, c_rsem, c_srcs, c_lands, after, mode="chips", name="chips_w_in_wait")
    outs = _chip_sum_adamw(from_chips, last["own"], idx, pad_in(w_in[0]), pad_in(m_w_in[0]), pad_in(v_w_in[0]),
                           tr=BIG_TILES["w_in"], name="adamw_w_in")
    res["w_in"] = [o[:, :IN_COLS][None] for o in outs]

    small = _small_all_reduce([dg_mix_pre, dg_mix_post, dg_ffn_pre, dg_ffn_post, dg_attn, dg_conv, dtaps, dbf, loss_p])
    taps_full = jnp.concatenate([small[5:6, :CW], small[5:6, CW:], small[6:7, :CW]], axis=0)
    small_grads = {
        "b_forget": small[6:7, CW:CW + H], "conv_w": lax.dynamic_slice(taps_full, (0, me * 64), (3, 64)),
        "g_attn_out": small[4:5, :AW], "g_conv_out": small[4:5, AW:], "g_mix_pre": small[0:1], "g_mix_post": small[1:2],
        "g_ffn_pre": small[2:3], "g_ffn_post": small[3:4]}
    loss = small[6, CW + 128]
    smalls = {"b_forget": (b_forget, m_b_forget, v_b_forget), "conv_w": (conv_w[0], m_conv_w[0], v_conv_w[0]),
              "g_attn_out": (g_attn_out, m_g_attn_out, v_g_attn_out), "g_conv_out": (g_conv_out, m_g_conv_out, v_g_conv_out),
              "g_mix_pre": (g_mix_pre, m_g_mix_pre, v_g_mix_pre), "g_mix_post": (g_mix_post, m_g_mix_post, v_g_mix_post),
              "g_ffn_pre": (g_ffn_pre, m_g_ffn_pre, v_g_ffn_pre), "g_ffn_post": (g_ffn_post, m_g_ffn_post, v_g_ffn_post)}
    for name, (w, m, v) in smalls.items():
        g = small_grads[name]
        outs = [g] + list(_adamw(w, g, m, v, tr=w.shape[0], name="adamw_" + name))
        res[name] = [o[None] for o in outs] if name == "conv_w" else outs

    order = ["w_in", "b_forget", "conv_w", "g_attn_out", "g_conv_out", "w_out", "g_mix_pre", "g_mix_post",
             "w_gate_up", "w_down", "g_ffn_pre", "g_ffn_post"]
    outs = [loss, grad_x[None]]
    for k in range(4):
        outs += [res[n][k] for n in order]
    return tuple(outs)
```

```python
import functools

import numpy as np

import jax
import jax.numpy as jnp
from jax import lax
from jax.experimental import pallas as pl
from jax.experimental.pallas import tpu as pltpu

F32 = jnp.float32
BF16 = jnp.bfloat16
HIGHEST = lax.Precision.HIGHEST
MESH_ID = pl.DeviceIdType.MESH

D = 1024
H = 8
DH = 64
AW = 512
CW = 512
DFF = 2816
FB = DFF // 4
HP = 128
OFF_Q, OFF_K, OFF_V, OFF_BCU, OFF_F = 0, 1024, 2048, 2560, 4096
WP = OFF_F + 128
PIECES = ((OFF_Q, OFF_K), (OFF_K, OFF_V), (OFF_V, OFF_BCU), (OFF_BCU, OFF_F), (OFF_F, WP))
EPS = 1e-6
NDEV = 8
LANES = 128
SUBLANES = 8
IN_COLS = 385
IN_PAD = 512
WIN = 896
ADAM_LR, ADAM_B1, ADAM_B2, ADAM_EPS, ADAM_WD, ADAM_STEP = 0.001, 0.9, 0.999, 1e-08, 0.01, 10

NT = (((1,), (1,)), ((), ()))
TN = (((0,), (0,)), ((), ()))


def _cparams(vmem_mb=None, sem=None):
    kw = {}
    if vmem_mb is not None:
        kw["vmem_limit_bytes"] = vmem_mb << 20
    if sem is not None:
        kw["dimension_semantics"] = sem
    return pltpu.CompilerParams(**kw)


def _full(shape):
    return pl.BlockSpec(shape, lambda *_: (0,) * len(shape))


def _resident(shape):
    return pl.BlockSpec(shape, lambda *_: (0,) * len(shape), pipeline_mode=pl.Buffered(1))


def _rows(tm, width):
    return pl.BlockSpec((tm, width), lambda i: (i, 0))


def _fold8(v):
    r, w = v.shape
    return jnp.sum(v.reshape(r // SUBLANES, SUBLANES, w), axis=0)


def _split_dot(v, m01):
    hi = v.astype(BF16)
    lo = (v - hi.astype(F32)).astype(BF16)
    return (jnp.dot(hi, m01, preferred_element_type=F32)
            + jnp.dot(lo, m01, preferred_element_type=F32))


def _exact_dot01(m01, v):
    p1 = v.astype(BF16)
    r1 = v - p1.astype(F32)
    p2 = r1.astype(BF16)
    p3 = (r1 - p2.astype(F32)).astype(BF16)
    return (jnp.dot(m01, p1, preferred_element_type=F32) + jnp.dot(m01, p2, preferred_element_type=F32)
            + jnp.dot(m01, p3, preferred_element_type=F32))


def _rms_fwd(v, g):
    r = lax.rsqrt(jnp.mean(v * v, axis=-1, keepdims=True) + EPS)
    n = v * r
    return n * g, n, r


def _rms_bwd(do, n, r, g):
    dn = do * g
    return r * (dn - n * jnp.mean(dn * n, axis=-1, keepdims=True)), do * n


def _padded_column(n):
    if n < AW:
        return OFF_Q + HP * (n // DH) + n % DH, 0.125
    if n < 2 * AW:
        m = n - AW
        return OFF_K + HP * (m // DH) + m % DH, 1.0
    if n < 3 * AW:
        return OFF_V + n - 2 * AW, 1.0
    if n < 3 * AW + H:
        return OFF_F + n - 3 * AW, 1.0
    return OFF_BCU + n - 3 * AW - H, 1.0


def _in_layout_tables():
    dest = -np.ones((IN_PAD, LANES), np.int32)
    dest_f = -np.ones((IN_PAD, LANES), np.int32)
    scale = np.zeros((IN_PAD, LANES), np.float32)
    starts = []
    for k in range(NDEV):
        cols = [_padded_column(IN_COLS * k + j) for j in range(IN_COLS)]
        main = [c for c, _ in cols if c < OFF_F]
        ws = min((min(main) // LANES) * LANES, OFF_F - WIN)
        assert ws <= min(main) and max(main) < ws + WIN
        starts.append(ws)
        for j, (c, sc) in enumerate(cols):
            scale[j, k] = sc
            if c < OFF_F:
                dest[j, k] = c - ws
            else:
                dest_f[j, k] = c - OFF_F
    f_shards = tuple(k for k in range(NDEV) if (dest_f[:, k] >= 0).any())
    return tuple(starts), f_shards, jnp.asarray(dest), jnp.asarray(dest_f), jnp.asarray(scale)


def _perm(dest_ref, scale_ref, k, width):
    lane = lax.broadcasted_iota(jnp.int32, (IN_PAD, width), 1)
    return jnp.where(dest_ref[:, k:k + 1] == lane, scale_ref[:, k:k + 1], 0.0).astype(BF16)


def _assemble_w_in(blocks, tables, *, tr):
    starts, f_shards, dest, dest_f, scale = tables

    def body(b_ref, dest_ref, destf_ref, scale_ref, o_ref):
        o_ref[...] = jnp.zeros_like(o_ref)
        for k in range(NDEV):
            b = b_ref[k]
            ws = starts[k]
            part = jnp.dot(b, _perm(dest_ref, scale_ref, k, WIN), preferred_element_type=F32)
            o_ref[:, ws:ws + WIN] += part.astype(BF16)
            if k in f_shards:
                part = jnp.dot(b, _perm(destf_ref, scale_ref, k, 128), preferred_element_type=F32)
                o_ref[:, OFF_F:WP] += part.astype(BF16)

    tab = _full((IN_PAD, LANES))
    return pl.pallas_call(
        body, name="assemble_w_in", grid=(D // tr,),
        in_specs=[pl.BlockSpec((NDEV, tr, IN_PAD), lambda i: (0, i, 0)), tab, tab, tab],
        out_specs=_rows(tr, WP),
        out_shape=jax.ShapeDtypeStruct((D, WP), BF16),
        compiler_params=_cparams(48, ("arbitrary",)),
    )(blocks, dest, dest_f, scale)


def _disassemble_w_in(dwp, tables, *, tr):
    starts, f_shards, dest, dest_f, scale = tables
    width = dwp.shape[1]

    def body(g_ref, dest_ref, destf_ref, scale_ref, o_ref):
        for k in range(NDEV):
            ws = starts[k]
            acc = lax.dot_general(g_ref[:, ws:ws + WIN], _perm(dest_ref, scale_ref, k, WIN), NT, preferred_element_type=F32)
            if k in f_shards:
                acc = acc + lax.dot_general(g_ref[:, OFF_F:WP], _perm(destf_ref, scale_ref, k, 128), NT,
                                            preferred_element_type=F32)
            o_ref[k] = acc.astype(BF16)

    tab = _full((IN_PAD, LANES))
    return pl.pallas_call(
        body, name="disassemble_w_in", grid=(D // tr,),
        in_specs=[_rows(tr, width), tab, tab, tab],
        out_specs=pl.BlockSpec((NDEV, tr, IN_PAD), lambda i: (0, i, 0)),
        out_shape=jax.ShapeDtypeStruct((NDEV, D, IN_PAD), BF16),
        compiler_params=_cparams(48, ("arbitrary",)),
    )(dwp, dest, dest_f, scale)


def _in_proj(x, g1, wp, bfp, pq, pk, oq, ok, *, tm):
    s = x.shape[0]

    def body(x_ref, g_ref, w_ref, bf_ref, pq_ref, pk_ref, oq_ref, ok_ref,
             ht_ref, qp_ref, kp_ref, v_ref, bcu_ref, z_ref, carry):
        @pl.when(pl.program_id(0) == 0)
        def _():
            carry[...] = jnp.zeros_like(carry)

        h = _rms_fwd(x_ref[...], g_ref[...])[0].astype(BF16)
        ht_ref[...] = h.T
        z = jnp.dot(h, w_ref[:, OFF_F:WP], preferred_element_type=F32) + bf_ref[...]
        z_ref[...] = z
        lane = lax.broadcasted_iota(jnp.int32, (tm, 128), 1)
        logf = jnp.where(lane < H, jnp.minimum(z, 0.0) - jnp.log(1.0 + jnp.exp(-jnp.abs(z))), 0.0)
        row = lax.broadcasted_iota(jnp.int32, (tm, tm), 0)
        col = lax.broadcasted_iota(jnp.int32, (tm, tm), 1)
        tri = (col <= row).astype(BF16)
        c = _exact_dot01(tri, logf) + carry[0:1, :]
        carry[...] = jnp.broadcast_to(c[tm - 1:tm, :], carry.shape)
        c1 = c.astype(BF16).astype(F32)
        r1 = c - c1
        c2 = r1.astype(BF16).astype(F32)
        c3 = (r1 - c2).astype(BF16).astype(F32)
        zc = (c1 + pltpu.roll(c2, 8, axis=1) + pltpu.roll(c3, 16, axis=1)).astype(BF16)
        q = jnp.dot(h, w_ref[:, OFF_Q:OFF_K], preferred_element_type=F32)
        qp_ref[...] = (q + jnp.dot(zc, pq_ref[...], preferred_element_type=F32) + oq_ref[...]).astype(BF16)
        k = jnp.dot(h, w_ref[:, OFF_K:OFF_V], preferred_element_type=F32)
        kp_ref[...] = (k + jnp.dot(zc, pk_ref[...], preferred_element_type=F32) + ok_ref[...]).astype(BF16)
        v_ref[...] = jnp.dot(h, w_ref[:, OFF_V:OFF_BCU], preferred_element_type=F32).astype(BF16)
        bcu_ref[...] = jnp.dot(h, w_ref[:, OFF_BCU:OFF_F], preferred_element_type=F32)

    return pl.pallas_call(
        body, name="in_proj", grid=(s // tm,),
        in_specs=[_rows(tm, D), _full((1, D)), _resident((D, WP)), _full((1, 128)),
                  _full((128, 1024)), _full((128, 1024)), _full((1, 1024)), _full((1, 1024))],
        out_specs=[pl.BlockSpec((D, tm), lambda i: (0, i)), _rows(tm, 1024), _rows(tm, 1024), _rows(tm, AW),
                   _rows(tm, 3 * CW), _rows(tm, 128)],
        out_shape=[jax.ShapeDtypeStruct((D, s), BF16), jax.ShapeDtypeStruct((s, 1024), BF16),
                   jax.ShapeDtypeStruct((s, 1024), BF16), jax.ShapeDtypeStruct((s, AW), BF16),
                   jax.ShapeDtypeStruct((s, 3 * CW), F32), jax.ShapeDtypeStruct((s, 128), F32)],
        scratch_shapes=[pltpu.VMEM((SUBLANES, 128), F32)],
        compiler_params=_cparams(56, ("arbitrary",)),
    )(x, g1, wp, bfp, pq, pk, oq, ok)


def _attn_fwd(qp, kp, v, *, t):
    s = qp.shape[0]
    nq = s // t

    def body(q_ref, k_ref, v_ref, o_ref, lse_ref, mk_ref):
        qi = pl.program_id(1)
        row = lax.broadcasted_iota(jnp.int32, (t, t), 0)
        col = lax.broadcasted_iota(jnp.int32, (t, t), 1)
        lane = lax.broadcasted_iota(jnp.int32, (t, 128), 1)

        def head_step(hh, ki, carry, masked):
            m, l, acc = carry
            off = pl.multiple_of(ki * t, t)
            q = q_ref[:, HP * hh:HP * (hh + 1)]
            k = k_ref[pl.ds(off, t), HP * hh:HP * (hh + 1)]
            sc = lax.dot_general(q, k, NT, preferred_element_type=F32)
            if masked:
                sc = jnp.where(col <= row, sc, -1e30)
            mn = jnp.maximum(m, jnp.max(sc, axis=-1, keepdims=True))
            p = jnp.exp(sc - mn)
            a = jnp.exp(m - mn)
            l = a * l + jnp.sum(p, axis=-1, keepdims=True)
            acc = a * acc + jnp.dot(p.astype(BF16), v_ref[pl.ds(off, t), :], preferred_element_type=F32)
            return mn, l, acc

        def step(ki, carry, masked):
            new = tuple(head_step(hh, ki, carry[hh], masked) for hh in range(2))
            mk_ref[ki] = jnp.where(lane < DH, jnp.broadcast_to(new[0][0], (t, 128)), jnp.broadcast_to(new[1][0], (t, 128)))
            return new

        init = (jnp.full((t, 1), -1e30, F32), jnp.zeros((t, 1), F32), jnp.zeros((t, 128), F32))
        carry = lax.fori_loop(0, qi, functools.partial(step, masked=False), (init, init))
        (m0, l0, acc0), (m1, l1, acc1) = step(qi, carry, True)
        o_ref[...] = jnp.where(lane < DH, acc0 / l0, acc1 / l1)
        lse_ref[...] = jnp.where(lane < DH, jnp.broadcast_to(m0 + jnp.log(l0), (t, 128)),
                                 jnp.broadcast_to(m1 + jnp.log(l1), (t, 128)))

    return pl.pallas_call(
        body, name="attn_fwd", grid=(H // 2, nq),
        in_specs=[pl.BlockSpec((t, 2 * HP), lambda p, i: (i, p)),
                  pl.BlockSpec((s, 2 * HP), lambda p, i: (0, p)),
                  pl.BlockSpec((s, 128), lambda p, i: (0, p))],
        out_specs=[pl.BlockSpec((t, 128), lambda p, i: (i, p)), pl.BlockSpec((t, 128), lambda p, i: (i, p)),
                   pl.BlockSpec((nq, t, 128), lambda p, i: (0, i, p))],
        out_shape=[jax.ShapeDtypeStruct((s, AW), F32), jax.ShapeDtypeStruct((s, AW), F32),
                   jax.ShapeDtypeStruct((nq, s, AW), F32)],
        compiler_params=_cparams(48, ("arbitrary", "arbitrary")),
    )(qp, kp, v)


def _conv_taps(bcu_ref, halo_ref, first, tm):
    z = bcu_ref[:, CW:2 * CW] * bcu_ref[:, 2 * CW:3 * CW]
    zh = jnp.where(first, 0.0, halo_ref[:, CW:2 * CW] * halo_ref[:, 2 * CW:3 * CW])
    row = lax.broadcasted_iota(jnp.int32, (tm, CW), 0)
    z1 = jnp.where(row == 0, zh[7:8, :], pltpu.roll(z, 1, axis=0))
    z2 = jnp.where(row == 0, zh[6:7, :], jnp.where(row == 1, zh[7:8, :], pltpu.roll(z, 2, axis=0)))
    return z, z1, z2


def _halo_before(tm, width):
    return pl.BlockSpec((SUBLANES, width), lambda i: (jnp.maximum(i * (tm // SUBLANES) - 1, 0), 0))


def _mix_out(o, bcu, cw8, ga, gc, gsum, w_out, x, g_post, g_ffn_pre, *, tm):
    s = x.shape[0]

    def body(o_ref, bcu_ref, halo_ref, cw_ref, ga_ref, gc_ref, gs_ref, w_ref, x_ref, g_ref, gf_ref,
             merged_ref, y_ref, x2_ref, cv_ref, h2_ref):
        z, z1, z2 = _conv_taps(bcu_ref, halo_ref, pl.program_id(0) == 0, tm)
        cv = cw_ref[0:1, :] * z2 + cw_ref[1:2, :] * z1 + cw_ref[2:3, :] * z
        cv_ref[...] = cv
        conv = bcu_ref[:, 0:CW] * cv
        ov = o_ref[...]
        ra = lax.rsqrt(_split_dot(ov * ov, gs_ref[...]) * (1.0 / DH) + EPS)
        rc = lax.rsqrt(_split_dot(conv * conv, gs_ref[...]) * (1.0 / DH) + EPS)
        merged = jnp.concatenate([ov * ra * ga_ref[...], conv * rc * gc_ref[...]], axis=1).astype(BF16)
        merged_ref[...] = merged
        y = jnp.dot(merged, w_ref[...], preferred_element_type=F32)
        y_ref[...] = y
        x2 = x_ref[...] + _rms_fwd(y, g_ref[...])[0]
        x2_ref[...] = x2
        h2_ref[...] = _rms_fwd(x2, gf_ref[...])[0].astype(BF16)

    return pl.pallas_call(
        body, name="mix_out", grid=(s // tm,),
        in_specs=[_rows(tm, AW), _rows(tm, 3 * CW), _halo_before(tm, 3 * CW), _full((SUBLANES, CW)),
                  _full((1, AW)), _full((1, CW)), _full((CW, CW)), _resident((D, D)), _rows(tm, D), _full((1, D)),
                  _full((1, D))],
        out_specs=[_rows(tm, D), _rows(tm, D), _rows(tm, D), _rows(tm, CW), _rows(tm, D)],
        out_shape=[jax.ShapeDtypeStruct((s, D), BF16), jax.ShapeDtypeStruct((s, D), F32),
                   jax.ShapeDtypeStruct((s, D), F32), jax.ShapeDtypeStruct((s, CW), F32),
                   jax.ShapeDtypeStruct((s, D), BF16)],
        compiler_params=_cparams(48, ("arbitrary",)),
    )(o, bcu, bcu, cw8, ga, gc, gsum, w_out, x, g_post, g_ffn_pre)


def _ffn_up(h2, wgu, *, tm):
    s = h2.shape[0]

    def body(h_ref, w_ref, gate_ref, up_ref, a_ref):
        h = h_ref[...]
        gate = jnp.dot(h, w_ref[0, 0], preferred_element_type=F32)
        up = jnp.dot(h, w_ref[1, 0], preferred_element_type=F32)
        gate_ref[0] = gate.astype(BF16)
        up_ref[0] = up.astype(BF16)
        a_ref[0] = (gate * jax.nn.sigmoid(gate) * up).astype(BF16)

    blk = pl.BlockSpec((1, tm, FB), lambda j, i: (j, i, 0))
    return pl.pallas_call(
        body, name="ffn_up", grid=(4, s // tm),
        in_specs=[pl.BlockSpec((tm, D), lambda j, i: (i, 0)),
                  pl.BlockSpec((2, 1, D, FB), lambda j, i: (0, j, 0, 0))],
        out_specs=[blk, blk, blk],
        out_shape=[jax.ShapeDtypeStruct((4, s, FB), BF16)] * 3,
        compiler_params=_cparams(48, ("arbitrary", "arbitrary")),
    )(h2, wgu)


def _ffn_down_loss(a, wd, x2, target, g_post, *, tm):
    s = x2.shape[0]

    def body(a_ref, w_ref, x2_ref, t_ref, g_ref, dx3_ref, dff_ref, loss_ref, dg_ref):
        @pl.when(pl.program_id(0) == 0)
        def _():
            loss_ref[...] = jnp.zeros_like(loss_ref)
            dg_ref[...] = jnp.zeros_like(dg_ref)

        ff = jnp.dot(a_ref[0], w_ref[0], preferred_element_type=F32)
        for j in range(1, 4):
            ff = ff + jnp.dot(a_ref[j], w_ref[j], preferred_element_type=F32)
        out, n, r = _rms_fwd(ff, g_ref[...])
        e = x2_ref[...] + out - t_ref[...]
        loss_ref[...] += _fold8(e * e)
        dx3 = e * (1.0 / D)
        dx3_ref[...] = dx3
        dff, dg = _rms_bwd(dx3, n, r, g_ref[...])
        dff_ref[...] = dff.astype(BF16)
        dg_ref[...] += _fold8(dg)

    return pl.pallas_call(
        body, name="ffn_down_loss", grid=(s // tm,),
        in_specs=[pl.BlockSpec((4, tm, FB), lambda i: (0, i, 0)), _resident((4, FB, D)), _rows(tm, D), _rows(tm, D),
                  _full((1, D))],
        out_specs=[_rows(tm, D), _rows(tm, D), _full((SUBLANES, D)), _full((SUBLANES, D))],
        out_shape=[jax.ShapeDtypeStruct((s, D), F32), jax.ShapeDtypeStruct((s, D), BF16),
                   jax.ShapeDtypeStruct((SUBLANES, D), F32), jax.ShapeDtypeStruct((SUBLANES, D), F32)],
        compiler_params=_cparams(48, ("arbitrary",)),
    )(a, wd, x2, target, g_post)


def _ffn_bwd(dff, wd, gate, up, wgu, x2, g_pre, dx3, y, g_post, *, tm):
    s = x2.shape[0]

    def body(dff_ref, wd_ref, gate_ref, up_ref, w_ref, x2_ref, gpre_ref, dx3_ref, y_ref, gpost_ref,
             dgu_ref, dx2_ref, dy_ref, dgpre_ref, dgpost_ref):
        @pl.when(pl.program_id(0) == 0)
        def _():
            dgpre_ref[...] = jnp.zeros_like(dgpre_ref)
            dgpost_ref[...] = jnp.zeros_like(dgpost_ref)

        dff = dff_ref[...]
        dh2 = None
        for j in range(4):
            da = lax.dot_general(dff, wd_ref[j], NT, preferred_element_type=F32)
            g = gate_ref[j].astype(F32)
            sg = jax.nn.sigmoid(g)
            dgate = (da * up_ref[j].astype(F32) * (sg * (1.0 + g * (1.0 - sg)))).astype(BF16)
            dup = (da * (g * sg)).astype(BF16)
            dgu_ref[0, j] = dgate
            dgu_ref[1, j] = dup
            part = (lax.dot_general(dgate, w_ref[0, j], NT, preferred_element_type=F32)
                    + lax.dot_general(dup, w_ref[1, j], NT, preferred_element_type=F32))
            dh2 = part if dh2 is None else dh2 + part
        _, n2, r2 = _rms_fwd(x2_ref[...], gpre_ref[...])
        dxn, dg = _rms_bwd(dh2, n2, r2, gpre_ref[...])
        dgpre_ref[...] += _fold8(dg)
        dx2 = dx3_ref[...] + dxn
        dx2_ref[...] = dx2
        _, ny, ry = _rms_fwd(y_ref[...], gpost_ref[...])
        dy, dg2 = _rms_bwd(dx2, ny, ry, gpost_ref[...])
        dy_ref[...] = dy.astype(BF16)
        dgpost_ref[...] += _fold8(dg2)

    blk4 = pl.BlockSpec((4, tm, FB), lambda i: (0, i, 0))
    return pl.pallas_call(
        body, name="ffn_bwd", grid=(s // tm,),
        in_specs=[_rows(tm, D), _resident((4, FB, D)), blk4, blk4, _resident((2, 4, D, FB)), _rows(tm, D), _full((1, D)),
                  _rows(tm, D), _rows(tm, D), _full((1, D))],
        out_specs=[pl.BlockSpec((2, 4, tm, FB), lambda i: (0, 0, i, 0)), _rows(tm, D), _rows(tm, D),
                   _full((SUBLANES, D)), _full((SUBLANES, D))],
        out_shape=[jax.ShapeDtypeStruct((2, 4, s, FB), BF16), jax.ShapeDtypeStruct((s, D), F32),
                   jax.ShapeDtypeStruct((s, D), BF16), jax.ShapeDtypeStruct((SUBLANES, D), F32),
                   jax.ShapeDtypeStruct((SUBLANES, D), F32)],
        compiler_params=_cparams(56, ("arbitrary",)),
    )(dff, wd, gate, up, wgu, x2, g_pre, dx3, y, g_post)


def _ffn_bwd_act(dff, wd, gate, up, *, tm):
    s = dff.shape[0]

    def body(dff_ref, w_ref, gate_ref, up_ref, dgu_ref):
        da = lax.dot_general(dff_ref[...], w_ref[0], NT, preferred_element_type=F32)
        g = gate_ref[0].astype(F32)
        sg = jax.nn.sigmoid(g)
        dgu_ref[0, 0] = (da * up_ref[0].astype(F32) * (sg * (1.0 + g * (1.0 - sg)))).astype(BF16)
        dgu_ref[1, 0] = (da * (g * sg)).astype(BF16)

    blk = pl.BlockSpec((1, tm, FB), lambda j, i: (j, i, 0))
    return pl.pallas_call(
        body, name="ffn_bwd_act", grid=(4, s // tm),
        in_specs=[pl.BlockSpec((tm, D), lambda j, i: (i, 0)), pl.BlockSpec((1, FB, D), lambda j, i: (j, 0, 0)), blk, blk],
        out_specs=pl.BlockSpec((2, 1, tm, FB), lambda j, i: (0, j, i, 0)),
        out_shape=jax.ShapeDtypeStruct((2, 4, s, FB), BF16),
        compiler_params=_cparams(48, ("arbitrary", "arbitrary")),
    )(dff, wd, gate, up)


def _grad_matmul(a, b, *, ta, tb, ts, name):
    s, ka = a.shape
    nb = b.shape[1]
    ts = min(ts, s)
    nk = s // ts

    def body(a_ref, b_ref, o_ref, acc):
        k = pl.program_id(2)

        @pl.when(k == 0)
        def _():
            acc[...] = jnp.zeros_like(acc)

        acc[...] += lax.dot_general(a_ref[...], b_ref[...], TN, preferred_element_type=F32)

        @pl.when(k == nk - 1)
        def _():
            o_ref[...] = acc[...].astype(BF16)

    return pl.pallas_call(
        body, name=name, grid=(ka // ta, nb // tb, nk),
        in_specs=[pl.BlockSpec((ts, ta), lambda i, j, k: (k, i)), pl.BlockSpec((ts, tb), lambda i, j, k: (k, j))],
        out_specs=pl.BlockSpec((ta, tb), lambda i, j, k: (i, j)),
        out_shape=jax.ShapeDtypeStruct((ka, nb), BF16),
        scratch_shapes=[pltpu.VMEM((ta, tb), F32)],
        compiler_params=_cparams(48, ("arbitrary", "arbitrary", "arbitrary")),
    )(a, b)


def _grad_matmul_t(at, b, *, tb, name):
    ka, s = at.shape
    blocked = b.ndim == 3
    nb = b.shape[-1]
    steps = b.shape[0] if blocked else nb // tb
    width = nb if blocked else tb

    def body(a_ref, b_ref, o_ref):
        bv = b_ref[0] if blocked else b_ref[...]
        res = jnp.dot(a_ref[...], bv, preferred_element_type=F32).astype(BF16)
        if blocked:
            o_ref[0] = res
        else:
            o_ref[...] = res

    if blocked:
        b_spec = pl.BlockSpec((1, s, nb), lambda j: (j, 0, 0))
        o_spec = pl.BlockSpec((1, ka, nb), lambda j: (j, 0, 0))
        o_shape = jax.ShapeDtypeStruct((steps, ka, nb), BF16)
    else:
        b_spec = pl.BlockSpec((s, width), lambda j: (0, j))
        o_spec = pl.BlockSpec((ka, width), lambda j: (0, j))
        o_shape = jax.ShapeDtypeStruct((ka, nb), BF16)
    return pl.pallas_call(
        body, name=name, grid=(steps,),
        in_specs=[_resident((ka, s)), b_spec], out_specs=o_spec, out_shape=o_shape,
        compiler_params=_cparams(56, ("arbitrary",)),
    )(at, b)


GW_TILE = 256


def _grad_w_in(h1t, pieces):
    ka, s = h1t.shape
    widths = [p.shape[1] for p in pieces]
    assert all(w % GW_TILE == 0 for w in widths)
    first = [sum(widths[:i]) // GW_TILE for i in range(len(pieces))]
    count = [w // GW_TILE for w in widths]

    def body(a_ref, *refs):
        o_ref = refs[-1]
        j = pl.program_id(0)
        for ref, f0, n in zip(refs[:-1], first, count):
            @pl.when((j >= f0) & (j < f0 + n))
            def _(ref=ref):
                o_ref[...] = jnp.dot(a_ref[...], ref[...], preferred_element_type=F32).astype(BF16)

    def spec(f0, n):
        return pl.BlockSpec((s, GW_TILE), lambda j: (0, jnp.clip(j - f0, 0, n - 1)))

    return pl.pallas_call(
        body, name="grad_w_in", grid=(sum(count),),
        in_specs=[_resident((ka, s))] + [spec(f0, n) for f0, n in zip(first, count)],
        out_specs=pl.BlockSpec((ka, GW_TILE), lambda j: (0, j)),
        out_shape=jax.ShapeDtypeStruct((ka, sum(widths)), BF16),
        compiler_params=_cparams(56, ("arbitrary",)),
    )(h1t, *pieces)


def _grad_matmul_blocks(a, b, *, ts, name):
    nblk = a.shape[0] if a.ndim == 3 else b.shape[0]
    s = a.shape[-2]
    ka, nb = a.shape[-1], b.shape[-1]
    ts = min(ts, s)
    nk = s // ts

    def body(a_ref, b_ref, o_ref, acc):
        k = pl.program_id(1)

        @pl.when(k == 0)
        def _():
            acc[...] = jnp.zeros_like(acc)

        av = a_ref[0] if a.ndim == 3 else a_ref[...]
        bv = b_ref[0] if b.ndim == 3 else b_ref[...]
        acc[...] += lax.dot_general(av, bv, TN, preferred_element_type=F32)

        @pl.when(k == nk - 1)
        def _():
            o_ref[0] = acc[...].astype(BF16)

    def spec(arr, width):
        if arr.ndim == 3:
            return pl.BlockSpec((1, ts, width), lambda j, k: (j, k, 0))
        return pl.BlockSpec((ts, width), lambda j, k: (k, 0))

    return pl.pallas_call(
        body, name=name, grid=(nblk, nk),
        in_specs=[spec(a, ka), spec(b, nb)],
        out_specs=pl.BlockSpec((1, ka, nb), lambda j, k: (j, 0, 0)),
        out_shape=jax.ShapeDtypeStruct((nblk, ka, nb), BF16),
        scratch_shapes=[pltpu.VMEM((ka, nb), F32)],
        compiler_params=_cparams(48, ("arbitrary", "arbitrary")),
    )(a, b)


def _ffn_bwd_in(dgu, wgu, x2, g_pre, dx3, y, g_post, *, tm):
    s = x2.shape[0]

    def body(dgu_ref, w_ref, x2_ref, gpre_ref, dx3_ref, y_ref, gpost_ref,
             dx2_ref, dy_ref, dgpre_ref, dgpost_ref):
        @pl.when(pl.program_id(0) == 0)
        def _():
            dgpre_ref[...] = jnp.zeros_like(dgpre_ref)
            dgpost_ref[...] = jnp.zeros_like(dgpost_ref)

        dh2 = None
        for a in range(2):
            for j in range(4):
                part = lax.dot_general(dgu_ref[a, j], w_ref[a, j], NT, preferred_element_type=F32)
                dh2 = part if dh2 is None else dh2 + part
        _, n2, r2 = _rms_fwd(x2_ref[...], gpre_ref[...])
        dxn, dg = _rms_bwd(dh2, n2, r2, gpre_ref[...])
        dgpre_ref[...] += _fold8(dg)
        dx2 = dx3_ref[...] + dxn
        dx2_ref[...] = dx2
        _, ny, ry = _rms_fwd(y_ref[...], gpost_ref[...])
        dy, dg2 = _rms_bwd(dx2, ny, ry, gpost_ref[...])
        dy_ref[...] = dy.astype(BF16)
        dgpost_ref[...] += _fold8(dg2)

    return pl.pallas_call(
        body, name="ffn_bwd_in", grid=(s // tm,),
        in_specs=[pl.BlockSpec((2, 4, tm, FB), lambda i: (0, 0, i, 0)), _resident((2, 4, D, FB)), _rows(tm, D),
                  _full((1, D)), _rows(tm, D), _rows(tm, D), _full((1, D))],
        out_specs=[_rows(tm, D), _rows(tm, D), _full((SUBLANES, D)), _full((SUBLANES, D))],
        out_shape=[jax.ShapeDtypeStruct((s, D), F32), jax.ShapeDtypeStruct((s, D), BF16),
                   jax.ShapeDtypeStruct((SUBLANES, D), F32), jax.ShapeDtypeStruct((SUBLANES, D), F32)],
        compiler_params=_cparams(56, ("arbitrary",)),
    )(dgu, wgu, x2, g_pre, dx3, y, g_post)


def _mix_bwd(dy, w_out, o, cv, bcu, ga, gc, gsum, *, tm):
    s = dy.shape[0]

    def group_norm_bwd(dn_out, v, g, gs):
        r = lax.rsqrt(_split_dot(v * v, gs) * (1.0 / DH) + EPS)
        n = v * r
        dn = dn_out * g
        return r * (dn - n * (_split_dot(dn * n, gs) * (1.0 / DH))), dn_out * n

    def body(dy_ref, w_ref, o_ref, cv_ref, bcu_ref, ga_ref, gc_ref, gs_ref,
             do_ref, dl_ref, dcv_ref, db_ref, dga_ref, dgc_ref):
        @pl.when(pl.program_id(0) == 0)
        def _():
            dga_ref[...] = jnp.zeros_like(dga_ref)
            dgc_ref[...] = jnp.zeros_like(dgc_ref)

        dm = lax.dot_general(dy_ref[...], w_ref[...], NT, preferred_element_type=F32)
        ov = o_ref[...]
        do, dga = group_norm_bwd(dm[:, 0:AW], ov, ga_ref[...], gs_ref[...])
        dob = do.astype(BF16)
        do_ref[...] = dob
        dl_ref[...] = _split_dot(dob.astype(F32) * ov, gs_ref[...])
        dga_ref[...] += _fold8(dga)
        gate_b = bcu_ref[:, 0:CW]
        cv = cv_ref[...]
        dconv, dgc = group_norm_bwd(dm[:, AW:D], gate_b * cv, gc_ref[...], gs_ref[...])
        dgc_ref[...] += _fold8(dgc)
        dcv_ref[...] = dconv * gate_b
        db_ref[...] = (dconv * cv).astype(BF16)

    return pl.pallas_call(
        body, name="mix_bwd", grid=(s // tm,),
        in_specs=[_rows(tm, D), _resident((D, D)), _rows(tm, AW), _rows(tm, CW), _rows(tm, 3 * CW),
                  _full((1, AW)), _full((1, CW)), _full((CW, CW))],
        out_specs=[_rows(tm, AW), _rows(tm, AW), _rows(tm, CW), _rows(tm, CW),
                   _full((SUBLANES, AW)), _full((SUBLANES, CW))],
        out_shape=[jax.ShapeDtypeStruct((s, AW), BF16), jax.ShapeDtypeStruct((s, AW), F32),
                   jax.ShapeDtypeStruct((s, CW), F32), jax.ShapeDtypeStruct((s, CW), BF16),
                   jax.ShapeDtypeStruct((SUBLANES, AW), F32), jax.ShapeDtypeStruct((SUBLANES, CW), F32)],
        compiler_params=_cparams(48, ("arbitrary",)),
    )(dy, w_out, o, cv, bcu, ga, gc, gsum)


def _conv_bwd(dcv, db, bcu, cw8, *, tm):
    s = dcv.shape[0]
    nt = s // tm

    def body(dcv_ref, nxt_ref, db_ref, bcu_ref, halo_ref, cw_ref, dbcu_ref, dw_ref):
        i = pl.program_id(0)

        @pl.when(i == 0)
        def _():
            dw_ref[...] = jnp.zeros_like(dw_ref)

        z, z1, z2 = _conv_taps(bcu_ref, halo_ref, i == 0, tm)
        d = dcv_ref[...]
        dw_ref[0] += _fold8(d * z2)
        dw_ref[1] += _fold8(d * z1)
        dw_ref[2] += _fold8(d * z)
        nx = jnp.where(i == nt - 1, 0.0, nxt_ref[...])
        row = lax.broadcasted_iota(jnp.int32, (tm, CW), 0)
        d1 = jnp.where(row == tm - 1, nx[0:1, :], pltpu.roll(d, tm - 1, axis=0))
        d2 = jnp.where(row == tm - 2, nx[0:1, :], jnp.where(row == tm - 1, nx[1:2, :], pltpu.roll(d, tm - 2, axis=0)))
        dz = cw_ref[2:3, :] * d + cw_ref[1:2, :] * d1 + cw_ref[0:1, :] * d2
        dbcu_ref[:, 0:CW] = db_ref[...]
        dbcu_ref[:, CW:2 * CW] = (dz * bcu_ref[:, 2 * CW:3 * CW]).astype(BF16)
        dbcu_ref[:, 2 * CW:3 * CW] = (dz * bcu_ref[:, CW:2 * CW]).astype(BF16)

    return pl.pallas_call(
        body, name="conv_bwd", grid=(nt,),
        in_specs=[_rows(tm, CW),
                  pl.BlockSpec((SUBLANES, CW), lambda i: (jnp.minimum((i + 1) * (tm // SUBLANES), s // SUBLANES - 1), 0)),
                  _rows(tm, CW), _rows(tm, 3 * CW), _halo_before(tm, 3 * CW), _full((SUBLANES, CW))],
        out_specs=[_rows(tm, 3 * CW), _full((3, SUBLANES, CW))],
        out_shape=[jax.ShapeDtypeStruct((s, 3 * CW), BF16), jax.ShapeDtypeStruct((3, SUBLANES, CW), F32)],
        compiler_params=_cparams(48, ("arbitrary",)),
    )(dcv, dcv, db, bcu, bcu, cw8)


def _attn_bwd(qp, kp, v, do, lse, dl, mk, *, t):
    s = qp.shape[0]
    nq = s // t

    def body(q_ref, k_ref, v_ref, do_ref, lse_ref, dl_ref, mk_ref, dq_ref, dk_ref, dv_ref, dkx_ref, dq_acc):
        ki = pl.program_id(1)

        @pl.when(ki == 0)
        def _():
            dq_acc[...] = jnp.zeros_like(dq_acc)

        row = lax.broadcasted_iota(jnp.int32, (t, t), 0)
        col = lax.broadcasted_iota(jnp.int32, (t, t), 1)
        lane = lax.broadcasted_iota(jnp.int32, (t, 128), 1)

        def head_step(hh, qi, carry, masked):
            dk, dv, cs = carry
            off = pl.multiple_of(qi * t, t)
            rows = pl.ds(off, t)
            kh = k_ref[:, HP * hh:HP * (hh + 1)]
            q = q_ref[rows, HP * hh:HP * (hh + 1)]
            in_head = (lane >= DH * hh) & (lane < DH * (hh + 1))
            m_col = mk_ref[0, rows, DH * hh:DH * hh + 1]
            scale = jnp.exp(m_col - lse_ref[rows, DH * hh:DH * hh + 1])
            dom = jnp.where(in_head, do_ref[rows, :], jnp.zeros((), BF16))
            sc = lax.dot_general(q, kh, NT, preferred_element_type=F32) - m_col
            if masked:
                sc = jnp.where(col <= row, sc, -1e30)
            pt = jnp.exp(sc).astype(BF16)
            dp = lax.dot_general(dom, v_ref[...], NT, preferred_element_type=F32)
            ds32 = (pt.astype(F32) * scale) * (dp - dl_ref[rows, DH * hh:DH * hh + 1])
            ds = ds32.astype(BF16)
            cs = cs + _fold8(ds32)
            dv = dv + jnp.dot((dom.astype(F32) * scale).astype(BF16).T, pt, preferred_element_type=F32)
            dk = dk + jnp.dot(q.T, ds, preferred_element_type=F32)
            dq_acc[rows, HP * hh:HP * (hh + 1)] += jnp.dot(ds, kh, preferred_element_type=F32)
            return dk, dv, cs

        def step(qi, carry, masked):
            return tuple(head_step(hh, qi, carry[hh], masked) for hh in range(2))

        zero = (jnp.zeros((HP, t), F32), jnp.zeros((128, t), F32), jnp.zeros((SUBLANES, t), F32))
        carry = step(ki, (zero, zero), True)
        (dk0, dv0, cs0), (dk1, dv1, cs1) = lax.fori_loop(ki + 1, nq, functools.partial(step, masked=False), carry)
        dk_ref[:, 0:HP] = dk0.T.astype(BF16)
        dk_ref[:, HP:2 * HP] = dk1.T.astype(BF16)
        dv_ref[...] = (dv0 + dv1).T.astype(BF16)

        def as_column(cs):
            return lax.dot_general(cs, jnp.ones((SUBLANES, 128), F32), TN, precision=HIGHEST, preferred_element_type=F32)

        dkx_ref[...] = jnp.where(lane < DH, as_column(cs0), as_column(cs1))

        @pl.when(ki == nq - 1)
        def _():
            dq_ref[...] = dq_acc[...].astype(BF16)

    return pl.pallas_call(
        body, name="attn_bwd", grid=(H // 2, nq),
        in_specs=[pl.BlockSpec((s, 2 * HP), lambda p, i: (0, p)),
                  pl.BlockSpec((t, 2 * HP), lambda p, i: (i, p)),
                  pl.BlockSpec((t, 128), lambda p, i: (i, p)),
                  pl.BlockSpec((s, 128), lambda p, i: (0, p)),
                  pl.BlockSpec((s, 128), lambda p, i: (0, p)),
                  pl.BlockSpec((s, 128), lambda p, i: (0, p)),
                  pl.BlockSpec((1, s, 128), lambda p, i: (i, 0, p))],
        out_specs=[pl.BlockSpec((s, 2 * HP), lambda p, i: (0, p)),
                   pl.BlockSpec((t, 2 * HP), lambda p, i: (i, p)),
                   pl.BlockSpec((t, 128), lambda p, i: (i, p)),
                   pl.BlockSpec((t, 128), lambda p, i: (i, p))],
        out_shape=[jax.ShapeDtypeStruct((s, 1024), BF16), jax.ShapeDtypeStruct((s, 1024), BF16),
                   jax.ShapeDtypeStruct((s, AW), BF16), jax.ShapeDtypeStruct((s, AW), F32)],
        scratch_shapes=[pltpu.VMEM((s, 2 * HP), F32)],
        compiler_params=_cparams(56, ("arbitrary", "arbitrary")),
    )(qp, kp, v, do, lse, dl, mk)


def _forget_bwd(dkx, z, sel, *, tm):
    s = dkx.shape[0]
    nt = s // tm

    def body(dk_ref, z_ref, sel_ref, dfl_ref, dbf_ref, carry):
        @pl.when(pl.program_id(0) == 0)
        def _():
            carry[...] = jnp.zeros_like(carry)
            dbf_ref[...] = jnp.zeros_like(dbf_ref)

        dc = _split_dot(dk_ref[...], sel_ref[...])
        row = lax.broadcasted_iota(jnp.int32, (tm, tm), 0)
        col = lax.broadcasted_iota(jnp.int32, (tm, tm), 1)
        tri = (col >= row).astype(BF16)
        dlogf = _exact_dot01(tri, dc) + carry[0:1, :]
        carry[...] = jnp.broadcast_to(dlogf[0:1, :], carry.shape)
        dz = dlogf * (1.0 - jax.nn.sigmoid(z_ref[...]))
        dfl_ref[:, 0:128] = dz.astype(BF16)
        dfl_ref[:, 128:GW_TILE] = jnp.zeros((tm, GW_TILE - 128), BF16)
        dbf_ref[...] += _fold8(dz)

    rev = lambda i: (nt - 1 - i, 0)
    return pl.pallas_call(
        body, name="forget_bwd", grid=(nt,),
        in_specs=[pl.BlockSpec((tm, AW), rev), pl.BlockSpec((tm, 128), rev), _full((AW, 128))],
        out_specs=[pl.BlockSpec((tm, GW_TILE), rev), _full((SUBLANES, 128))],
        out_shape=[jax.ShapeDtypeStruct((s, GW_TILE), BF16), jax.ShapeDtypeStruct((SUBLANES, 128), F32)],
        scratch_shapes=[pltpu.VMEM((SUBLANES, 128), F32)],
        compiler_params=_cparams(48, ("arbitrary",)),
    )(dkx, z, sel)


def _in_proj_bwd(pieces, wp, x, g1, dx2, *, tm):
    s = x.shape[0]

    def body(q_ref, k_ref, v_ref, bcu_ref, f_ref, w_ref, x_ref, g_ref, dx2_ref, dx_ref, dg_ref):
        @pl.when(pl.program_id(0) == 0)
        def _():
            dg_ref[...] = jnp.zeros_like(dg_ref)

        dh = None
        for ref, (lo, hi) in zip((q_ref, k_ref, v_ref, bcu_ref, f_ref), PIECES):
            part = lax.dot_general(ref[...], w_ref[:, lo:hi], NT, preferred_element_type=F32)
            dh = part if dh is None else dh + part
        _, n, r = _rms_fwd(x_ref[...], g_ref[...])
        dxn, dg = _rms_bwd(dh, n, r, g_ref[...])
        dx_ref[...] = dx2_ref[...] + dxn
        dg_ref[...] += _fold8(dg)

    return pl.pallas_call(
        body, name="in_proj_bwd", grid=(s // tm,),
        in_specs=[_rows(tm, hi - lo) for lo, hi in PIECES] + [_resident((D, WP)), _rows(tm, D), _full((1, D)), _rows(tm, D)],
        out_specs=[_rows(tm, D), _full((SUBLANES, D))],
        out_shape=[jax.ShapeDtypeStruct((s, D), F32), jax.ShapeDtypeStruct((SUBLANES, D), F32)],
        compiler_params=_cparams(56, ("arbitrary",)),
    )(*pieces, wp, x, g1, dx2)


def _position():
    return lax.axis_index("x"), lax.axis_index("y"), lax.axis_index("c")


ANY = pl.BlockSpec(memory_space=pl.ANY)


def _all_gather(shards):
    n = len(shards)

    def body(*refs):
        x_refs, out_refs = refs[:n], refs[n:2 * n]
        send_sems, recv_sems, local_sems = refs[2 * n:]
        x, y, c = _position()
        me, sibling = (x, y, c), (x, y, 1 - c)
        chips = [(1 - x, y), (x, 1 - y), (1 - x, 1 - y)]

        def copy(a, k, block, to, own=False):
            slot = out_refs[a].at[4 * block[0] + 2 * block[1] + block[2]]
            return pltpu.make_async_remote_copy(
                src_ref=x_refs[a] if own else slot, dst_ref=slot,
                send_sem=send_sems.at[7 * a + k], recv_sem=recv_sems.at[7 * a + k], device_id=to, device_id_type=MESH_ID)

        mine = [pltpu.make_async_copy(x_refs[a], out_refs[a].at[4 * x + 2 * y + c], local_sems.at[a]) for a in range(n)]
        for cp in mine:
            cp.start()
        first = []
        for a in range(n):
            first.append(copy(a, 0, me, sibling, own=True))
            first += [copy(a, 1 + j, me, (*chip, c), own=True) for j, chip in enumerate(chips)]
        for cp in first:
            cp.start()
        passed = []
        for j, chip in enumerate(chips):
            for a in range(n):
                copy(a, 1 + j, (*chip, c), me).wait_recv()
                fwd = copy(a, 4 + j, (*chip, c), sibling)
                fwd.start()
                passed.append(fwd)
        for a in range(n):
            copy(a, 0, sibling, me).wait_recv()
            for j, chip in enumerate(chips):
                copy(a, 4 + j, (*chip, 1 - c), me).wait_recv()
        for cp in first + passed:
            cp.wait_send()
        for cp in mine:
            cp.wait()

    return pl.pallas_call(
        body, name="all_gather_weights",
        out_shape=[jax.ShapeDtypeStruct((NDEV,) + sh.shape, sh.dtype) for sh in shards],
        in_specs=[ANY] * n, out_specs=[ANY] * n,
        scratch_shapes=[pltpu.SemaphoreType.DMA((7 * n,)), pltpu.SemaphoreType.DMA((7 * n,)), pltpu.SemaphoreType.DMA((n,))],
    )(*shards)


def _pair_exchange(grads):
    n = len(grads)

    def body(*refs):
        g_refs, out_refs = refs[:n], refs[n:2 * n]
        send_sems, recv_sems = refs[2 * n:]
        x, y, c = _position()
        copies = [pltpu.make_async_remote_copy(
            src_ref=g_refs[a].at[:, pl.ds(1 - c, 1)], dst_ref=out_refs[a], send_sem=send_sems.at[a],
            recv_sem=recv_sems.at[a], device_id=(x, y, 1 - c), device_id_type=MESH_ID) for a in range(n)]
        for cp in copies:
            cp.start()
        for cp in copies:
            cp.wait()

    return pl.pallas_call(
        body, name="grad_pair_exchange",
        out_shape=[jax.ShapeDtypeStruct((4, 1) + g.shape[2:], g.dtype) for g in grads],
        in_specs=[ANY] * n, out_specs=[ANY] * n,
        scratch_shapes=[pltpu.SemaphoreType.DMA((n,)), pltpu.SemaphoreType.DMA((n,))],
    )(*grads)


def _pair_sum(g, got, idx, *, tr, name):
    r, c = g.shape[2:]

    def body(idx_ref, g_ref, got_ref, pb_ref, own_ref):
        p = g_ref[0, 0].astype(F32) + got_ref[0, 0].astype(F32)
        pb_ref[0] = p.astype(BF16)

        @pl.when(pl.program_id(1) == idx_ref[1])
        def _():
            own_ref[...] = p

    return pl.pallas_call(
        body, name=name,
        grid_spec=pltpu.PrefetchScalarGridSpec(
            num_scalar_prefetch=1, grid=(r // tr, 4),
            in_specs=[pl.BlockSpec((1, 1, tr, c), lambda i, j, idx: (j, idx[0], i, 0)),
                      pl.BlockSpec((1, 1, tr, c), lambda i, j, idx: (j, 0, i, 0))],
            out_specs=[pl.BlockSpec((1, tr, c), lambda i, j, idx: (j, i, 0)),
                       pl.BlockSpec((tr, c), lambda i, j, idx: (i, 0))]),
        out_shape=[jax.ShapeDtypeStruct((4, r, c), BF16), jax.ShapeDtypeStruct((r, c), F32)],
        compiler_params=_cparams(32, ("arbitrary", "arbitrary")),
    )(idx, g, got)


HBM = pl.BlockSpec(memory_space=pltpu.HBM)
SEM = pl.BlockSpec(memory_space=pltpu.SEMAPHORE)
DATAFLOW = pltpu.SideEffectType.DATAFLOW_SIDE_EFFECTING


PEERS = {"gather": NDEV - 1, "scatter": NDEV - 1, "chips": 3}


def _exchange_copies(src_refs, land_refs, send_sems, recv_sems, mode):
    x, y, c = _position()
    me, my_chip = 4 * x + 2 * y + c, 2 * x + y
    npeers = PEERS[mode]
    copies = []
    for a, (s_ref, l_ref) in enumerate(zip(src_refs, land_refs)):
        for k in range(npeers):
            if mode == "chips":
                px, py, pc = x ^ ((k + 1) >> 1), y ^ ((k + 1) & 1), c
                src, dst = s_ref.at[2 * px + py], l_ref.at[my_chip]
            else:
                px, py, pc = x ^ ((k + 1) >> 2), y ^ (((k + 1) >> 1) & 1), c ^ ((k + 1) & 1)
                src, dst = (s_ref.at[4 * px + 2 * py + pc] if mode == "scatter" else s_ref), l_ref.at[me]
            copies.append(pltpu.make_async_remote_copy(
                src_ref=src, dst_ref=dst, send_sem=send_sems.at[npeers * a + k], recv_sem=recv_sems.at[npeers * a + k],
                device_id=(px, py, pc), device_id_type=MESH_ID))
    return copies


def _exchange_start(srcs, lands, *, mode, name):
    n = len(srcs)
    nsem = PEERS[mode] * n

    def body(*refs):
        token = refs[-1]
        for cp in _exchange_copies(refs[:n], refs[n:2 * n], refs[2 * n], refs[2 * n + 1], mode):
            cp.start()
        token[...] = jnp.zeros_like(token)

    arrays = list(srcs) + list(lands)
    outs = pl.pallas_call(
        body, name=name,
        out_shape=(pltpu.SemaphoreType.DMA((nsem,)), pltpu.SemaphoreType.DMA((nsem,)),
                   *[pltpu.HBM(a.shape, a.dtype) for a in arrays], jax.ShapeDtypeStruct((SUBLANES, LANES), F32)),
        in_specs=[HBM] * (2 * n),
        out_specs=(SEM, SEM, *[HBM] * (2 * n), pl.BlockSpec(memory_space=pltpu.VMEM)),
        input_output_aliases={i: 2 + i for i in range(2 * n)},
        compiler_params=pltpu.CompilerParams(has_side_effects=DATAFLOW),
    )(*[pltpu.with_memory_space_constraint(a, pltpu.HBM) for a in arrays])
    return outs[0], outs[1], outs[2:2 + n], outs[2 + n:2 + 2 * n], outs[-1]


def _exchange_wait(send_sems, recv_sems, srcs, lands, after, *, mode, name):
    n = len(srcs)

    def body(*refs):
        for cp in _exchange_copies(refs[:n], refs[n:2 * n], refs[2 * n], refs[2 * n + 1], mode):
            cp.wait_send()
            cp.wait_recv()

    arrays = list(srcs) + list(lands)
    outs = pl.pallas_call(
        body, name=name,
        out_shape=tuple(pltpu.HBM(a.shape, a.dtype) for a in arrays),
        in_specs=[HBM] * (2 * n) + [SEM, SEM, ANY],
        out_specs=tuple([HBM] * (2 * n)),
        input_output_aliases={i: i for i in range(2 * n)},
        compiler_params=pltpu.CompilerParams(has_side_effects=DATAFLOW),
    )(*arrays, send_sems, recv_sems, after)
    return outs[n:]


def _own_slot(value, me):
    return lax.dynamic_update_index_in_dim(lax.empty((NDEV,) + value.shape, value.dtype), value, me, 0)


def _small_all_reduce(parts):
    def body(gmp_ref, gmo_ref, gfp_ref, gfo_ref, ga_ref, gc_ref, dw_ref, bf_ref, loss_ref,
             out_ref, buf, send_sems, recv_sems):
        x, y, c = _position()
        me = 4 * x + 2 * y + c

        def colsum(v):
            return jnp.sum(v, axis=0, keepdims=True)

        loss = jnp.sum(colsum(loss_ref[...]), axis=1, keepdims=True) * (0.5 / D)
        rows = [colsum(gmp_ref[...]), colsum(gmo_ref[...]), colsum(gfp_ref[...]), colsum(gfo_ref[...]),
                jnp.concatenate([colsum(ga_ref[...]), colsum(gc_ref[...])], axis=1),
                jnp.concatenate([colsum(dw_ref[0]), colsum(dw_ref[1])], axis=1),
                jnp.concatenate([colsum(dw_ref[2]), colsum(bf_ref[...]), jnp.broadcast_to(loss, (1, 128)),
                                 jnp.zeros((1, 256), F32)], axis=1),
                jnp.zeros((1, D), F32)]
        buf[me] = jnp.concatenate(rows, axis=0)
        copies = []
        for mm in range(1, NDEV):
            peer = (x ^ (mm >> 2), y ^ ((mm >> 1) & 1), c ^ (mm & 1))
            copies.append(pltpu.make_async_remote_copy(
                src_ref=buf.at[me], dst_ref=buf.at[me], send_sem=send_sems.at[mm - 1], recv_sem=recv_sems.at[mm - 1],
                device_id=peer, device_id_type=MESH_ID))
        for cp in copies:
            cp.start()
        for cp in copies:
            cp.wait_recv()
        for cp in copies:
            cp.wait_send()
        acc = buf[0]
        for d in range(1, NDEV):
            acc = acc + buf[d]
        out_ref[...] = acc

    vm = pl.BlockSpec(memory_space=pltpu.VMEM)
    return pl.pallas_call(
        body, name="small_all_reduce",
        out_shape=jax.ShapeDtypeStruct((SUBLANES, D), F32),
        in_specs=[vm] * len(parts), out_specs=vm,
        scratch_shapes=[pltpu.VMEM((NDEV, SUBLANES, D), F32), pltpu.SemaphoreType.DMA((7,)), pltpu.SemaphoreType.DMA((7,))],
    )(*parts)


def _adam_update(w, g, m, v):
    nm = ADAM_B1 * m + (1.0 - ADAM_B1) * g
    nv = ADAM_B2 * v + (1.0 - ADAM_B2) * (g * g)
    m_hat = nm / (1.0 - ADAM_B1 ** ADAM_STEP)
    v_hat = nv / (1.0 - ADAM_B2 ** ADAM_STEP)
    return -ADAM_LR * (m_hat / (jnp.sqrt(v_hat) + ADAM_EPS) + ADAM_WD * w), nm, nv


def _adamw(w, g, m, v, *, tr, name):
    rows, cols = w.shape

    def body(w_ref, g_ref, m_ref, v_ref, d_ref, nm_ref, nv_ref):
        d_ref[...], nm_ref[...], nv_ref[...] = _adam_update(w_ref[...], g_ref[...], m_ref[...], v_ref[...])

    spec = pl.BlockSpec((tr, cols), lambda i: (i, 0))
    return pl.pallas_call(
        body, name=name, grid=(rows // tr,),
        in_specs=[spec] * 4, out_specs=[spec] * 3,
        out_shape=[jax.ShapeDtypeStruct((rows, cols), F32)] * 3,
        compiler_params=_cparams(32, ("arbitrary",)),
    )(w, g, m, v)


def _chip_sum_adamw(got, own, idx, w, m, v, *, tr, name):
    rows, cols = w.shape

    def body(idx_ref, got_ref, own_ref, w_ref, m_ref, v_ref, g_ref, d_ref, nm_ref, nv_ref):
        g = jnp.zeros((tr, cols), F32)
        for j in range(4):
            g = g + jnp.where(idx_ref[1] == j, own_ref[...], got_ref[j].astype(F32))
        g_ref[...] = g
        d_ref[...], nm_ref[...], nv_ref[...] = _adam_update(w_ref[...], g, m_ref[...], v_ref[...])

    spec = pl.BlockSpec((tr, cols), lambda i, idx: (i, 0))
    return pl.pallas_call(
        body, name=name,
        grid_spec=pltpu.PrefetchScalarGridSpec(
            num_scalar_prefetch=1, grid=(rows // tr,),
            in_specs=[pl.BlockSpec((4, tr, cols), lambda i, idx: (0, i, 0)), spec, spec, spec, spec],
            out_specs=[spec] * 4),
        out_shape=[jax.ShapeDtypeStruct((rows, cols), F32)] * 4,
        compiler_params=_cparams(32, ("arbitrary",)),
    )(idx, got, own, w, m, v)


def _device_sum_adamw(land, w, m, v, *, tr, name):
    rows, cols = w.shape

    def body(land_ref, w_ref, m_ref, v_ref, g_ref, d_ref, nm_ref, nv_ref):
        g = land_ref[0].astype(F32)
        for dev in range(1, NDEV):
            g = g + land_ref[dev].astype(F32)
        g_ref[...] = g
        d_ref[...], nm_ref[...], nv_ref[...] = _adam_update(w_ref[...], g, m_ref[...], v_ref[...])

    spec = pl.BlockSpec((tr, cols), lambda i: (i, 0))
    return pl.pallas_call(
        body, name=name, grid=(rows // tr,),
        in_specs=[pl.BlockSpec((NDEV, tr, cols), lambda i: (0, i, 0)), spec, spec, spec],
        out_specs=[spec] * 4,
        out_shape=[jax.ShapeDtypeStruct((rows, cols), F32)] * 4,
        compiler_params=_cparams(32, ("arbitrary",)),
    )(land, w, m, v)


def _placement_constants():
    j = jnp.arange(128)[:, None]
    lane = jnp.arange(1024)[None, :]
    head, sub = lane // HP, lane % HP
    piece, jh = j // H, j % H
    valid = (j < 3 * H) & (jh == head)
    pq = jnp.where(valid & (sub == DH + piece), 1.0, 0.0).astype(BF16)
    pk = jnp.where(valid & (sub == DH + 3 + piece), -1.0, 0.0).astype(BF16)
    oq = jnp.where((sub >= DH + 3) & (sub < DH + 6), 1.0, 0.0).astype(F32)
    ok = jnp.where((sub >= DH) & (sub < DH + 3), 1.0, 0.0).astype(F32)
    r = jnp.arange(AW)[:, None]
    cc = jnp.arange(128)[None, :]
    sel = jnp.where((r % DH == 3) & (r // DH == cc), -1.0, 0.0).astype(BF16)
    gi = jnp.arange(CW)
    gsum = (gi[:, None] // DH == gi[None, :] // DH).astype(BF16)
    return pq, pk, oq, ok, sel, gsum


def _local_step(xs, tgt, wp, late_weights, cw8, bfp, g_attn_out, g_conv_out,
                g_mix_pre, g_mix_post, g_ffn_pre, g_ffn_post, early_grads=None, last_grad=None):
    pq, pk, oq, ok, sel, gsum = _placement_constants()
    h1t, qp, kp, vv, bcu, zf = _in_proj(xs, g_mix_pre, wp, bfp, pq, pk, oq, ok, tm=512)
    o, lse, mk = _attn_fwd(qp, kp, vv, t=512)
    w_out_f, wgu, wd = late_weights(lse)
    merged, y, x2, cv, h2 = _mix_out(o, bcu, cw8, g_attn_out, g_conv_out, gsum, w_out_f, xs, g_mix_post, g_ffn_pre, tm=512)
    gate, up, act = _ffn_up(h2, wgu, tm=1024)
    dx3, dff, loss_p, dg_ffn_post = _ffn_down_loss(act, wd, x2, tgt, g_ffn_post, tm=512)

    dgu, dx2, dy, dg_ffn_pre, dg_mix_post = _ffn_bwd(dff, wd, gate, up, wgu, x2, g_ffn_pre, dx3, y, g_mix_post, tm=256)
    dw_down = _grad_matmul_blocks(act, dff, ts=4096, name="grad_w_down")
    dw_gu = _grad_matmul_blocks(dgu.reshape(NDEV, -1, FB), h2, ts=4096, name="grad_w_gate_up")
    dw_out = _grad_matmul(merged, dy, ta=1024, tb=1024, ts=2048, name="grad_w_out")
    token = early_grads(dw_out, dw_gu, dw_down) if early_grads is not None else None
    ga = g_attn_out if token is None else g_attn_out + token[0:1, 0:1]
    do, dl, dcv, db, dg_attn, dg_conv = _mix_bwd(dy, w_out_f, o, cv, bcu, ga, g_conv_out, gsum, tm=512)
    dbcu, dtaps = _conv_bwd(dcv, db, bcu, cw8, tm=512)
    dqp, dkp, dv, dkx = _attn_bwd(qp, kp, vv, do, lse, dl, mk, t=512)
    dfl, dbf = _forget_bwd(dkx, zf, sel, tm=512)
    pieces = (dqp, dkp, dv, dbcu, dfl)
    dwp = _grad_w_in(h1t, pieces)
    token = last_grad(dwp) if last_grad is not None else None
    g1 = g_mix_pre if token is None else g_mix_pre + token[0:1, 0:1]
    grad_x, dg_mix_pre = _in_proj_bwd(pieces, wp, xs, g1, dx2, tm=512)
    return (grad_x, dwp, dw_out, dw_gu, dw_down, dg_mix_pre, dg_mix_post, dg_ffn_pre, dg_ffn_post, dg_attn, dg_conv,
            dtaps, dbf, loss_p)


BIG_TILES = {"w_in": 256, "w_out": 128, "w_gate_up": 176, "w_down": 176}


def kernel(x, w_in, b_forget, conv_w, g_attn_out, g_conv_out, w_out, g_mix_pre, g_mix_post, w_gate_up, w_down, g_ffn_pre, g_ffn_post, loss_target, m_w_in, m_b_forget, m_conv_w, m_g_attn_out, m_g_conv_out, m_w_out, m_g_mix_pre, m_g_mix_post, m_w_gate_up, m_w_down, m_g_ffn_pre, m_g_ffn_post, v_w_in, v_b_forget, v_conv_w, v_g_attn_out, v_g_conv_out, v_w_out, v_g_mix_pre, v_g_mix_post, v_w_gate_up, v_w_down, v_g_ffn_pre, v_g_ffn_post):
    xc, yc, cc = _position()
    my_chip = 2 * xc + yc
    me = 2 * my_chip + cc
    idx = jnp.stack([cc, my_chip]).astype(jnp.int32)
    tables = _in_layout_tables()
    pad_in = lambda a: jnp.pad(a, ((0, 0), (0, IN_PAD - IN_COLS)))

    g_in, g_taps = _all_gather([pad_in(w_in[0]).astype(BF16), conv_w[0]])
    wp = _assemble_w_in(g_in, tables, tr=256)
    cw8 = jnp.pad(g_taps.transpose(1, 0, 2).reshape(3, CW), ((0, SUBLANES - 3), (0, 0)))

    late = [w_out[0].astype(BF16), w_gate_up[0].astype(BF16), w_down[0].astype(BF16)]
    ssem, rsem, late_thru, land_thru, token = _exchange_start(
        late, [_own_slot(s, me) for s in late], mode="gather", name="gather_late_start")
    bfp = jnp.pad(b_forget, ((0, 0), (0, 128 - H))) + token[0:1, :]

    def late_weights(after):
        l_out, l_gu, l_down = _exchange_wait(ssem, rsem, late_thru, land_thru, after, mode="gather", name="gather_late_wait")
        return l_out.reshape(D, D), l_gu.reshape(2, 4, D, FB), l_down.reshape(4, FB, D)

    early = {}

    def early_grads(dw_out, dw_gu, dw_down):
        srcs = [dw_out.reshape(NDEV, D // NDEV, D), dw_gu, dw_down.reshape(NDEV, DFF // NDEV, D)]
        lands = [_own_slot(lax.dynamic_index_in_dim(s, me, 0, keepdims=False), me) for s in srcs]
        early["handles"] = _exchange_start(srcs, lands, mode="scatter", name="scatter_early_start")
        return early["handles"][4]

    last = {}

    def last_grad(dwp):
        g_w_in = _disassemble_w_in(dwp, tables, tr=256).reshape(4, 2, D, IN_PAD)
        (from_sibling,) = _pair_exchange([g_w_in])
        pair_b, last["own"] = _pair_sum(g_w_in, from_sibling, idx, tr=BIG_TILES["w_in"], name="grad_pair_sum_w_in")
        land = lax.dynamic_update_index_in_dim(lax.empty(pair_b.shape, pair_b.dtype),
                                               lax.dynamic_index_in_dim(pair_b, my_chip, 0, keepdims=False), my_chip, 0)
        last["handles"] = _exchange_start([pair_b], [land], mode="chips", name="chips_w_in_start")
        return last["handles"][4]

    (grad_x, dwp, dw_out, dw_gu, dw_down, dg_mix_pre, dg_mix_post, dg_ffn_pre, dg_ffn_post, dg_attn, dg_conv,
     dtaps, dbf, loss_p) = _local_step(x[0], loss_target[0], wp, late_weights, cw8, bfp, g_attn_out, g_conv_out,
                                        g_mix_pre, g_mix_post, g_ffn_pre, g_ffn_post, early_grads, last_grad)

    e_ssem, e_rsem, e_srcs, e_lands, _ = early["handles"]
    land_out, land_gu, land_down = _exchange_wait(e_ssem, e_rsem, e_srcs, e_lands, dg_mix_pre, mode="scatter",
                                                  name="scatter_early_wait")
    res = {}
    big = {"w_out": (land_out, w_out[0], m_w_out[0], v_w_out[0]),
           "w_gate_up": (land_gu, w_gate_up[0].T, m_w_gate_up[0].T, v_w_gate_up[0].T),
           "w_down": (land_down, w_down[0], m_w_down[0], v_w_down[0])}
    for name, (land, w, m, v) in big.items():
        outs = _device_sum_adamw(land, w, m, v, tr=BIG_TILES[name], name="adamw_" + name)
        res[name] = [(o.T if name == "w_gate_up" else o)[None] for o in outs]
    c_ssem, c_rsem, c_srcs, c_lands, _ = last["handles"]
    after = sum(res[n][1][0, :SUBLANES, :LANES] for n in big)
    (from_chips,) = _exchange_wait(c_ssem, c_rsem, c_srcs, c_lands, after, mode="chips", name="chips_w_in_wait")
    outs = _chip_sum_adamw(from_chips, last["own"], idx, pad_in(w_in[0]), pad_in(m_w_in[0]), pad_in(v_w_in[0]),
                           tr=BIG_TILES["w_in"], name="adamw_w_in")
    res["w_in"] = [o[:, :IN_COLS][None] for o in outs]

    small = _small_all_reduce([dg_mix_pre, dg_mix_post, dg_ffn_pre, dg_ffn_post, dg_attn, dg_conv, dtaps, dbf, loss_p])
    taps_full = jnp.concatenate([small[5:6, :CW], small[5:6, CW:], small[6:7, :CW]], axis=0)
    small_grads = {
        "b_forget": small[6:7, CW:CW + H], "conv_w": lax.dynamic_slice(taps_full, (0, me * 64), (3, 64)),
        "g_attn_out": small[4:5, :AW], "g_conv_out": small[4:5, AW:], "g_mix_pre": small[0:1], "g_mix_post": small[1:2],
        "g_ffn_pre": small[2:3], "g_ffn_post": small[3:4]}
    loss = small[6, CW + 128]
    smalls = {"b_forget": (b_forget, m_b_forget, v_b_forget), "conv_w": (conv_w[0], m_conv_w[0], v_conv_w[0]),
              "g_attn_out": (g_attn_out, m_g_attn_out, v_g_attn_out), "g_conv_out": (g_conv_out, m_g_conv_out, v_g_conv_out),
              "g_mix_pre": (g_mix_pre, m_g_mix_pre, v_g_mix_pre), "g_mix_post": (g_mix_post, m_g_mix_post, v_g_mix_post),
              "g_ffn_pre": (g_ffn_pre, m_g_ffn_pre, v_g_ffn_pre), "g_ffn_post": (g_ffn_post, m_g_ffn_post, v_g_ffn_post)}
    for name, (w, m, v) in smalls.items():
        g = small_grads[name]
        outs = [g] + list(_adamw(w, g, m, v, tr=w.shape[0], name="adamw_" + name))
        res[name] = [o[None] for o in outs] if name == "conv_w" else outs

    order = ["w_in", "b_forget", "conv_w", "g_attn_out", "g_conv_out", "w_out", "g_mix_pre", "g_mix_post",
             "w_gate_up", "w_down", "g_ffn_pre", "g_ffn_post"]
    outs = [loss, grad_x[None]]
    for k in range(4):
        outs += [res[n][k] for n in order]
    return tuple(outs)
```

```python
import functools

import numpy as np

import jax
import jax.numpy as jnp
from jax import lax
from jax.experimental import pallas as pl
from jax.experimental.pallas import tpu as pltpu

F32 = jnp.float32
BF16 = jnp.bfloat16
HIGHEST = lax.Precision.HIGHEST
MESH_ID = pl.DeviceIdType.MESH

D = 1024
H = 8
DH = 64
AW = 512
CW = 512
DFF = 2816
FB = DFF // 4
HP = 128
OFF_Q, OFF_K, OFF_V, OFF_BCU, OFF_F = 0, 1024, 2048, 2560, 4096
WP = OFF_F + 128
PIECES = ((OFF_Q, OFF_K), (OFF_K, OFF_V), (OFF_V, OFF_BCU), (OFF_BCU, OFF_F), (OFF_F, WP))
EPS = 1e-6
NDEV = 8
LANES = 128
SUBLANES = 8
IN_COLS = 385
IN_PAD = 512
WIN = 896
ADAM_LR, ADAM_B1, ADAM_B2, ADAM_EPS, ADAM_WD, ADAM_STEP = 0.001, 0.9, 0.999, 1e-08, 0.01, 10

NT = (((1,), (1,)), ((), ()))
TN = (((0,), (0,)), ((), ()))


def _cparams(vmem_mb=None, sem=None):
    kw = {}
    if vmem_mb is not None:
        kw["vmem_limit_bytes"] = vmem_mb << 20
    if sem is not None:
        kw["dimension_semantics"] = sem
    return pltpu.CompilerParams(**kw)


def _full(shape):
    return pl.BlockSpec(shape, lambda *_: (0,) * len(shape))


def _resident(shape):
    return pl.BlockSpec(shape, lambda *_: (0,) * len(shape), pipeline_mode=pl.Buffered(1))


def _rows(tm, width):
    return pl.BlockSpec((tm, width), lambda i: (i, 0))


def _fold8(v):
    r, w = v.shape
    return jnp.sum(v.reshape(r // SUBLANES, SUBLANES, w), axis=0)


def _split_dot(v, m01):
    hi = v.astype(BF16)
    lo = (v - hi.astype(F32)).astype(BF16)
    return (jnp.dot(hi, m01, preferred_element_type=F32)
            + jnp.dot(lo, m01, preferred_element_type=F32))


def _exact_dot01(m01, v):
    p1 = v.astype(BF16)
    r1 = v - p1.astype(F32)
    p2 = r1.astype(BF16)
    p3 = (r1 - p2.astype(F32)).astype(BF16)
    return (jnp.dot(m01, p1, preferred_element_type=F32) + jnp.dot(m01, p2, preferred_element_type=F32)
            + jnp.dot(m01, p3, preferred_element_type=F32))


def _rms_fwd(v, g):
    r = lax.rsqrt(jnp.mean(v * v, axis=-1, keepdims=True) + EPS)
    n = v * r
    return n * g, n, r


def _rms_bwd(do, n, r, g):
    dn = do * g
    return r * (dn - n * jnp.mean(dn * n, axis=-1, keepdims=True)), do * n


def _padded_column(n):
    if n < AW:
        return OFF_Q + HP * (n // DH) + n % DH, 0.125
    if n < 2 * AW:
        m = n - AW
        return OFF_K + HP * (m // DH) + m % DH, 1.0
    if n < 3 * AW:
        return OFF_V + n - 2 * AW, 1.0
    if n < 3 * AW + H:
        return OFF_F + n - 3 * AW, 1.0
    return OFF_BCU + n - 3 * AW - H, 1.0


def _in_layout_tables():
    dest = -np.ones((IN_PAD, LANES), np.int32)
    dest_f = -np.ones((IN_PAD, LANES), np.int32)
    scale = np.zeros((IN_PAD, LANES), np.float32)
    starts = []
    for k in range(NDEV):
        cols = [_padded_column(IN_COLS * k + j) for j in range(IN_COLS)]
        main = [c for c, _ in cols if c < OFF_F]
        ws = min((min(main) // LANES) * LANES, OFF_F - WIN)
        assert ws <= min(main) and max(main) < ws + WIN
        starts.append(ws)
        for j, (c, sc) in enumerate(cols):
            scale[j, k] = sc
            if c < OFF_F:
                dest[j, k] = c - ws
            else:
                dest_f[j, k] = c - OFF_F
    f_shards = tuple(k for k in range(NDEV) if (dest_f[:, k] >= 0).any())
    return tuple(starts), f_shards, jnp.asarray(dest), jnp.asarray(dest_f), jnp.asarray(scale)


def _perm(dest_ref, scale_ref, k, width):
    lane = lax.broadcasted_iota(jnp.int32, (IN_PAD, width), 1)
    return jnp.where(dest_ref[:, k:k + 1] == lane, scale_ref[:, k:k + 1], 0.0).astype(BF16)


def _assemble_w_in(blocks, tables, *, tr):
    starts, f_shards, dest, dest_f, scale = tables

    def body(b_ref, dest_ref, destf_ref, scale_ref, o_ref):
        o_ref[...] = jnp.zeros_like(o_ref)
        for k in range(NDEV):
            b = b_ref[k]
            ws = starts[k]
            part = jnp.dot(b, _perm(dest_ref, scale_ref, k, WIN), preferred_element_type=F32)
            o_ref[:, ws:ws + WIN] += part.astype(BF16)
            if k in f_shards:
                part = jnp.dot(b, _perm(destf_ref, scale_ref, k, 128), preferred_element_type=F32)
                o_ref[:, OFF_F:WP] += part.astype(BF16)

    tab = _full((IN_PAD, LANES))
    return pl.pallas_call(
        body, name="assemble_w_in", grid=(D // tr,),
        in_specs=[pl.BlockSpec((NDEV, tr, IN_PAD), lambda i: (0, i, 0)), tab, tab, tab],
        out_specs=_rows(tr, WP),
        out_shape=jax.ShapeDtypeStruct((D, WP), BF16),
        compiler_params=_cparams(48, ("arbitrary",)),
    )(blocks, dest, dest_f, scale)


def _disassemble_w_in(dwp, tables, *, tr):
    starts, f_shards, dest, dest_f, scale = tables
    width = dwp.shape[1]

    def body(g_ref, dest_ref, destf_ref, scale_ref, o_ref):
        for k in range(NDEV):
            ws = starts[k]
            acc = lax.dot_general(g_ref[:, ws:ws + WIN], _perm(dest_ref, scale_ref, k, WIN), NT, preferred_element_type=F32)
            if k in f_shards:
                acc = acc + lax.dot_general(g_ref[:, OFF_F:WP], _perm(destf_ref, scale_ref, k, 128), NT,
                                            preferred_element_type=F32)
            o_ref[k] = acc.astype(BF16)

    tab = _full((IN_PAD, LANES))
    return pl.pallas_call(
        body, name="disassemble_w_in", grid=(D // tr,),
        in_specs=[_rows(tr, width), tab, tab, tab],
        out_specs=pl.BlockSpec((NDEV, tr, IN_PAD), lambda i: (0, i, 0)),
        out_shape=jax.ShapeDtypeStruct((NDEV, D, IN_PAD), BF16),
        compiler_params=_cparams(48, ("arbitrary",)),
    )(dwp, dest, dest_f, scale)


def _in_proj(x, g1, wp, bfp, pq, pk, oq, ok, *, tm):
    s = x.shape[0]

    def body(x_ref, g_ref, w_ref, bf_ref, pq_ref, pk_ref, oq_ref, ok_ref,
             ht_ref, qp_ref, kp_ref, v_ref, bcu_ref, z_ref, carry):
        @pl.when(pl.program_id(0) == 0)
        def _():
            carry[...] = jnp.zeros_like(carry)

        h = _rms_fwd(x_ref[...], g_ref[...])[0].astype(BF16)
        ht_ref[...] = h.T
        z = jnp.dot(h, w_ref[:, OFF_F:WP], preferred_element_type=F32) + bf_ref[...]
        z_ref[...] = z
        lane = lax.broadcasted_iota(jnp.int32, (tm, 128), 1)
        logf = jnp.where(lane < H, jnp.minimum(z, 0.0) - jnp.log(1.0 + jnp.exp(-jnp.abs(z))), 0.0)
        row = lax.broadcasted_iota(jnp.int32, (tm, tm), 0)
        col = lax.broadcasted_iota(jnp.int32, (tm, tm), 1)
        tri = (col <= row).astype(BF16)
        c = _exact_dot01(tri, logf) + carry[0:1, :]
        carry[...] = jnp.broadcast_to(c[tm - 1:tm, :], carry.shape)
        c1 = c.astype(BF16).astype(F32)
        r1 = c - c1
        c2 = r1.astype(BF16).astype(F32)
        c3 = (r1 - c2).astype(BF16).astype(F32)
        zc = (c1 + pltpu.roll(c2, 8, axis=1) + pltpu.roll(c3, 16, axis=1)).astype(BF16)
        q = jnp.dot(h, w_ref[:, OFF_Q:OFF_K], preferred_element_type=F32)
        qp_ref[...] = (q + jnp.dot(zc, pq_ref[...], preferred_element_type=F32) + oq_ref[...]).astype(BF16)
        k = jnp.dot(h, w_ref[:, OFF_K:OFF_V], preferred_element_type=F32)
        kp_ref[...] = (k + jnp.dot(zc, pk_ref[...], preferred_element_type=F32) + ok_ref[...]).astype(BF16)
        v_ref[...] = jnp.dot(h, w_ref[:, OFF_V:OFF_BCU], preferred_element_type=F32).astype(BF16)
        bcu_ref[...] = jnp.dot(h, w_ref[:, OFF_BCU:OFF_F], preferred_element_type=F32)

    return pl.pallas_call(
        body, name="in_proj", grid=(s // tm,),
        in_specs=[_rows(tm, D), _full((1, D)), _resident((D, WP)), _full((1, 128)),
                  _full((128, 1024)), _full((128, 1024)), _full((1, 1024)), _full((1, 1024))],
        out_specs=[pl.BlockSpec((D, tm), lambda i: (0, i)), _rows(tm, 1024), _rows(tm, 1024), _rows(tm, AW),
                   _rows(tm, 3 * CW), _rows(tm, 128)],
        out_shape=[jax.ShapeDtypeStruct((D, s), BF16), jax.ShapeDtypeStruct((s, 1024), BF16),
                   jax.ShapeDtypeStruct((s, 1024), BF16), jax.ShapeDtypeStruct((s, AW), BF16),
                   jax.ShapeDtypeStruct((s, 3 * CW), F32), jax.ShapeDtypeStruct((s, 128), F32)],
        scratch_shapes=[pltpu.VMEM((SUBLANES, 128), F32)],
        compiler_params=_cparams(56, ("arbitrary",)),
    )(x, g1, wp, bfp, pq, pk, oq, ok)


def _attn_fwd(qp, kp, v, *, t):
    s = qp.shape[0]
    nq = s // t

    def body(q_ref, k_ref, v_ref, o_ref, lse_ref, mk_ref):
        qi = pl.program_id(1)
        row = lax.broadcasted_iota(jnp.int32, (t, t), 0)
        col = lax.broadcasted_iota(jnp.int32, (t, t), 1)
        lane = lax.broadcasted_iota(jnp.int32, (t, 128), 1)

        def head_step(hh, ki, carry, masked):
            m, l, acc = carry
            off = pl.multiple_of(ki * t, t)
            q = q_ref[:, HP * hh:HP * (hh + 1)]
            k = k_ref[pl.ds(off, t), HP * hh:HP * (hh + 1)]
            sc = lax.dot_general(q, k, NT, preferred_element_type=F32)
            if masked:
                sc = jnp.where(col <= row, sc, -1e30)
            mn = jnp.maximum(m, jnp.max(sc, axis=-1, keepdims=True))
            p = jnp.exp(sc - mn)
            a = jnp.exp(m - mn)
            l = a * l + jnp.sum(p, axis=-1, keepdims=True)
            acc = a * acc + jnp.dot(p.astype(BF16), v_ref[pl.ds(off, t), :], preferred_element_type=F32)
            return mn, l, acc

        def step(ki, carry, masked):
            new = tuple(head_step(hh, ki, carry[hh], masked) for hh in range(2))
            mk_ref[ki] = jnp.where(lane < DH, jnp.broadcast_to(new[0][0], (t, 128)), jnp.broadcast_to(new[1][0], (t, 128)))
            return new

        init = (jnp.full((t, 1), -1e30, F32), jnp.zeros((t, 1), F32), jnp.zeros((t, 128), F32))
        carry = lax.fori_loop(0, qi, functools.partial(step, masked=False), (init, init))
        (m0, l0, acc0), (m1, l1, acc1) = step(qi, carry, True)
        o_ref[...] = jnp.where(lane < DH, acc0 / l0, acc1 / l1)
        lse_ref[...] = jnp.where(lane < DH, jnp.broadcast_to(m0 + jnp.log(l0), (t, 128)),
                                 jnp.broadcast_to(m1 + jnp.log(l1), (t, 128)))

    return pl.pallas_call(
        body, name="attn_fwd", grid=(H // 2, nq),
        in_specs=[pl.BlockSpec((t, 2 * HP), lambda p, i: (i, p)),
                  pl.BlockSpec((s, 2 * HP), lambda p, i: (0, p)),
                  pl.BlockSpec((s, 128), lambda p, i: (0, p))],
        out_specs=[pl.BlockSpec((t, 128), lambda p, i: (i, p)), pl.BlockSpec((t, 128), lambda p, i: (i, p)),
                   pl.BlockSpec((nq, t, 128), lambda p, i: (0, i, p))],
        out_shape=[jax.ShapeDtypeStruct((s, AW), F32), jax.ShapeDtypeStruct((s, AW), F32),
                   jax.ShapeDtypeStruct((nq, s, AW), F32)],
        compiler_params=_cparams(48, ("arbitrary", "arbitrary")),
    )(qp, kp, v)


def _conv_taps(bcu_ref, halo_ref, first, tm):
    z = bcu_ref[:, CW:2 * CW] * bcu_ref[:, 2 * CW:3 * CW]
    zh = jnp.where(first, 0.0, halo_ref[:, CW:2 * CW] * halo_ref[:, 2 * CW:3 * CW])
    row = lax.broadcasted_iota(jnp.int32, (tm, CW), 0)
    z1 = jnp.where(row == 0, zh[7:8, :], pltpu.roll(z, 1, axis=0))
    z2 = jnp.where(row == 0, zh[6:7, :], jnp.where(row == 1, zh[7:8, :], pltpu.roll(z, 2, axis=0)))
    return z, z1, z2


def _halo_before(tm, width):
    return pl.BlockSpec((SUBLANES, width), lambda i: (jnp.maximum(i * (tm // SUBLANES) - 1, 0), 0))


def _mix_out(o, bcu, cw8, ga, gc, gsum, w_out, x, g_post, g_ffn_pre, *, tm):
    s = x.shape[0]

    def body(o_ref, bcu_ref, halo_ref, cw_ref, ga_ref, gc_ref, gs_ref, w_ref, x_ref, g_ref, gf_ref,
             merged_ref, y_ref, x2_ref, cv_ref, h2_ref):
        z, z1, z2 = _conv_taps(bcu_ref, halo_ref, pl.program_id(0) == 0, tm)
        cv = cw_ref[0:1, :] * z2 + cw_ref[1:2, :] * z1 + cw_ref[2:3, :] * z
        cv_ref[...] = cv
        conv = bcu_ref[:, 0:CW] * cv
        ov = o_ref[...]
        ra = lax.rsqrt(_split_dot(ov * ov, gs_ref[...]) * (1.0 / DH) + EPS)
        rc = lax.rsqrt(_split_dot(conv * conv, gs_ref[...]) * (1.0 / DH) + EPS)
        merged = jnp.concatenate([ov * ra * ga_ref[...], conv * rc * gc_ref[...]], axis=1).astype(BF16)
        merged_ref[...] = merged
        y = jnp.dot(merged, w_ref[...], preferred_element_type=F32)
        y_ref[...] = y
        x2 = x_ref[...] + _rms_fwd(y, g_ref[...])[0]
        x2_ref[...] = x2
        h2_ref[...] = _rms_fwd(x2, gf_ref[...])[0].astype(BF16)

    return pl.pallas_call(
        body, name="mix_out", grid=(s // tm,),
        in_specs=[_rows(tm, AW), _rows(tm, 3 * CW), _halo_before(tm, 3 * CW), _full((SUBLANES, CW)),
                  _full((1, AW)), _full((1, CW)), _full((CW, CW)), _resident((D, D)), _rows(tm, D), _full((1, D)),
                  _full((1, D))],
        out_specs=[_rows(tm, D), _rows(tm, D), _rows(tm, D), _rows(tm, CW), _rows(tm, D)],
        out_shape=[jax.ShapeDtypeStruct((s, D), BF16), jax.ShapeDtypeStruct((s, D), F32),
                   jax.ShapeDtypeStruct((s, D), F32), jax.ShapeDtypeStruct((s, CW), F32),
                   jax.ShapeDtypeStruct((s, D), BF16)],
        compiler_params=_cparams(48, ("arbitrary",)),
    )(o, bcu, bcu, cw8, ga, gc, gsum, w_out, x, g_post, g_ffn_pre)


def _ffn_fwd_loss(h2, wgu, wd, x2, target, g_post, *, tm):
    s = x2.shape[0]

    def body(h_ref, w_ref, wd_ref, x2_ref, t_ref, g_ref,
             gate_ref, up_ref, a_ref, dx3_ref, dff_ref, loss_ref, dg_ref):
        @pl.when(pl.program_id(0) == 0)
        def _():
            loss_ref[...] = jnp.zeros_like(loss_ref)
            dg_ref[...] = jnp.zeros_like(dg_ref)

        h = h_ref[...]
        ff = None
        for j in range(4):
            gate = jnp.dot(h, w_ref[0, j], preferred_element_type=F32)
            up = jnp.dot(h, w_ref[1, j], preferred_element_type=F32)
            gate_ref[j] = gate.astype(BF16)
            up_ref[j] = up.astype(BF16)
            act = (gate * jax.nn.sigmoid(gate) * up).astype(BF16)
            a_ref[j] = act
            part = jnp.dot(act, wd_ref[j], preferred_element_type=F32)
            ff = part if ff is None else ff + part
        out, n, r = _rms_fwd(ff, g_ref[...])
        e = x2_ref[...] + out - t_ref[...]
        loss_ref[...] += _fold8(e * e)
        dx3 = e * (1.0 / D)
        dx3_ref[...] = dx3
        dff, dg = _rms_bwd(dx3, n, r, g_ref[...])
        dff_ref[...] = dff.astype(BF16)
        dg_ref[...] += _fold8(dg)

    blk4 = pl.BlockSpec((4, tm, FB), lambda i: (0, i, 0))
    return pl.pallas_call(
        body, name="ffn_fwd_loss", grid=(s // tm,),
        in_specs=[_rows(tm, D), _resident((2, 4, D, FB)), _resident((4, FB, D)), _rows(tm, D), _rows(tm, D), _full((1, D))],
        out_specs=[blk4, blk4, blk4, _rows(tm, D), _rows(tm, D), _full((SUBLANES, D)), _full((SUBLANES, D))],
        out_shape=[jax.ShapeDtypeStruct((4, s, FB), BF16)] * 3
        + [jax.ShapeDtypeStruct((s, D), F32), jax.ShapeDtypeStruct((s, D), BF16),
           jax.ShapeDtypeStruct((SUBLANES, D), F32), jax.ShapeDtypeStruct((SUBLANES, D), F32)],
        compiler_params=_cparams(56, ("arbitrary",)),
    )(h2, wgu, wd, x2, target, g_post)


def _ffn_bwd(dff, wd, gate, up, wgu, x2, g_pre, dx3, y, g_post, *, tm):
    s = x2.shape[0]

    def body(dff_ref, wd_ref, gate_ref, up_ref, w_ref, x2_ref, gpre_ref, dx3_ref, y_ref, gpost_ref,
             dgu_ref, dx2_ref, dy_ref, dgpre_ref, dgpost_ref):
        @pl.when(pl.program_id(0) == 0)
        def _():
            dgpre_ref[...] = jnp.zeros_like(dgpre_ref)
            dgpost_ref[...] = jnp.zeros_like(dgpost_ref)

        dff = dff_ref[...]
        dh2 = None
        for j in range(4):
            da = lax.dot_general(dff, wd_ref[j], NT, preferred_element_type=F32)
            g = gate_ref[j].astype(F32)
            sg = jax.nn.sigmoid(g)
            dgate = (da * up_ref[j].astype(F32) * (sg * (1.0 + g * (1.0 - sg)))).astype(BF16)
            dup = (da * (g * sg)).astype(BF16)
            dgu_ref[0, j] = dgate
            dgu_ref[1, j] = dup
            part = (lax.dot_general(dgate, w_ref[0, j], NT, preferred_element_type=F32)
                    + lax.dot_general(dup, w_ref[1, j], NT, preferred_element_type=F32))
            dh2 = part if dh2 is None else dh2 + part
        _, n2, r2 = _rms_fwd(x2_ref[...], gpre_ref[...])
        dxn, dg = _rms_bwd(dh2, n2, r2, gpre_ref[...])
        dgpre_ref[...] += _fold8(dg)
        dx2 = dx3_ref[...] + dxn
        dx2_ref[...] = dx2
        _, ny, ry = _rms_fwd(y_ref[...], gpost_ref[...])
        dy, dg2 = _rms_bwd(dx2, ny, ry, gpost_ref[...])
        dy_ref[...] = dy.astype(BF16)
        dgpost_ref[...] += _fold8(dg2)

    blk4 = pl.BlockSpec((4, tm, FB), lambda i: (0, i, 0))
    return pl.pallas_call(
        body, name="ffn_bwd", grid=(s // tm,),
        in_specs=[_rows(tm, D), _resident((4, FB, D)), blk4, blk4, _resident((2, 4, D, FB)), _rows(tm, D), _full((1, D)),
                  _rows(tm, D), _rows(tm, D), _full((1, D))],
        out_specs=[pl.BlockSpec((2, 4, tm, FB), lambda i: (0, 0, i, 0)), _rows(tm, D), _rows(tm, D),
                   _full((SUBLANES, D)), _full((SUBLANES, D))],
        out_shape=[jax.ShapeDtypeStruct((2, 4, s, FB), BF16), jax.ShapeDtypeStruct((s, D), F32),
                   jax.ShapeDtypeStruct((s, D), BF16), jax.ShapeDtypeStruct((SUBLANES, D), F32),
                   jax.ShapeDtypeStruct((SUBLANES, D), F32)],
        compiler_params=_cparams(56, ("arbitrary",)),
    )(dff, wd, gate, up, wgu, x2, g_pre, dx3, y, g_post)


def _grad_matmul(a, b, *, ta, tb, ts, name):
    s, ka = a.shape
    nb = b.shape[1]
    ts = min(ts, s)
    nk = s // ts

    def body(a_ref, b_ref, o_ref, acc):
        k = pl.program_id(2)

        @pl.when(k == 0)
        def _():
            acc[...] = jnp.zeros_like(acc)

        acc[...] += lax.dot_general(a_ref[...], b_ref[...], TN, preferred_element_type=F32)

        @pl.when(k == nk - 1)
        def _():
            o_ref[...] = acc[...].astype(BF16)

    return pl.pallas_call(
        body, name=name, grid=(ka // ta, nb // tb, nk),
        in_specs=[pl.BlockSpec((ts, ta), lambda i, j, k: (k, i)), pl.BlockSpec((ts, tb), lambda i, j, k: (k, j))],
        out_specs=pl.BlockSpec((ta, tb), lambda i, j, k: (i, j)),
        out_shape=jax.ShapeDtypeStruct((ka, nb), BF16),
        scratch_shapes=[pltpu.VMEM((ta, tb), F32)],
        compiler_params=_cparams(48, ("arbitrary", "arbitrary", "arbitrary")),
    )(a, b)


def _grad_matmul_t(at, b, *, tb, name):
    ka, s = at.shape
    blocked = b.ndim == 3
    nb = b.shape[-1]
    steps = b.shape[0] if blocked else nb // tb
    width = nb if blocked else tb

    def body(a_ref, b_ref, o_ref):
        bv = b_ref[0] if blocked else b_ref[...]
        res = jnp.dot(a_ref[...], bv, preferred_element_type=F32).astype(BF16)
        if blocked:
            o_ref[0] = res
        else:
            o_ref[...] = res

    if blocked:
        b_spec = pl.BlockSpec((1, s, nb), lambda j: (j, 0, 0))
        o_spec = pl.BlockSpec((1, ka, nb), lambda j: (j, 0, 0))
        o_shape = jax.ShapeDtypeStruct((steps, ka, nb), BF16)
    else:
        b_spec = pl.BlockSpec((s, width), lambda j: (0, j))
        o_spec = pl.BlockSpec((ka, width), lambda j: (0, j))
        o_shape = jax.ShapeDtypeStruct((ka, nb), BF16)
    return pl.pallas_call(
        body, name=name, grid=(steps,),
        in_specs=[_resident((ka, s)), b_spec], out_specs=o_spec, out_shape=o_shape,
        compiler_params=_cparams(56, ("arbitrary",)),
    )(at, b)


GW_TILE = 256


def _grad_w_in(h1t, pieces):
    ka, s = h1t.shape
    widths = [p.shape[1] for p in pieces]
    assert all(w % GW_TILE == 0 for w in widths)
    first = [sum(widths[:i]) // GW_TILE for i in range(len(pieces))]
    count = [w // GW_TILE for w in widths]

    def body(a_ref, *refs):
        o_ref = refs[-1]
        j = pl.program_id(0)
        for ref, f0, n in zip(refs[:-1], first, count):
            @pl.when((j >= f0) & (j < f0 + n))
            def _(ref=ref):
                o_ref[...] = jnp.dot(a_ref[...], ref[...], preferred_element_type=F32).astype(BF16)

    def spec(f0, n):
        return pl.BlockSpec((s, GW_TILE), lambda j: (0, jnp.clip(j - f0, 0, n - 1)))

    return pl.pallas_call(
        body, name="grad_w_in", grid=(sum(count),),
        in_specs=[_resident((ka, s))] + [spec(f0, n) for f0, n in zip(first, count)],
        out_specs=pl.BlockSpec((ka, GW_TILE), lambda j: (0, j)),
        out_shape=jax.ShapeDtypeStruct((ka, sum(widths)), BF16),
        compiler_params=_cparams(56, ("arbitrary",)),
    )(h1t, *pieces)


def _grad_matmul_blocks(a, b, *, ts, name):
    nblk = a.shape[0] if a.ndim == 3 else b.shape[0]
    s = a.shape[-2]
    ka, nb = a.shape[-1], b.shape[-1]
    ts = min(ts, s)
    nk = s // ts

    def body(a_ref, b_ref, o_ref, acc):
        k = pl.program_id(1)

        @pl.when(k == 0)
        def _():
            acc[...] = jnp.zeros_like(acc)

        av = a_ref[0] if a.ndim == 3 else a_ref[...]
        bv = b_ref[0] if b.ndim == 3 else b_ref[...]
        acc[...] += lax.dot_general(av, bv, TN, preferred_element_type=F32)

        @pl.when(k == nk - 1)
        def _():
            o_ref[0] = acc[...].astype(BF16)

    def spec(arr, width):
        if arr.ndim == 3:
            return pl.BlockSpec((1, ts, width), lambda j, k: (j, k, 0))
        return pl.BlockSpec((ts, width), lambda j, k: (k, 0))

    return pl.pallas_call(
        body, name=name, grid=(nblk, nk),
        in_specs=[spec(a, ka), spec(b, nb)],
        out_specs=pl.BlockSpec((1, ka, nb), lambda j, k: (j, 0, 0)),
        out_shape=jax.ShapeDtypeStruct((nblk, ka, nb), BF16),
        scratch_shapes=[pltpu.VMEM((ka, nb), F32)],
        compiler_params=_cparams(48, ("arbitrary", "arbitrary")),
    )(a, b)


def _mix_bwd(dy, w_out, o, cv, bcu, ga, gc, gsum, *, tm):
    s = dy.shape[0]

    def group_norm_bwd(dn_out, v, g, gs):
        r = lax.rsqrt(_split_dot(v * v, gs) * (1.0 / DH) + EPS)
        n = v * r
        dn = dn_out * g
        return r * (dn - n * (_split_dot(dn * n, gs) * (1.0 / DH))), dn_out * n

    def body(dy_ref, w_ref, o_ref, cv_ref, bcu_ref, ga_ref, gc_ref, gs_ref,
             do_ref, dl_ref, dcv_ref, db_ref, dga_ref, dgc_ref):
        @pl.when(pl.program_id(0) == 0)
        def _():
            dga_ref[...] = jnp.zeros_like(dga_ref)
            dgc_ref[...] = jnp.zeros_like(dgc_ref)

        dm = lax.dot_general(dy_ref[...], w_ref[...], NT, preferred_element_type=F32)
        ov = o_ref[...]
        do, dga = group_norm_bwd(dm[:, 0:AW], ov, ga_ref[...], gs_ref[...])
        dob = do.astype(BF16)
        do_ref[...] = dob
        dl_ref[...] = _split_dot(dob.astype(F32) * ov, gs_ref[...])
        dga_ref[...] += _fold8(dga)
        gate_b = bcu_ref[:, 0:CW]
        cv = cv_ref[...]
        dconv, dgc = group_norm_bwd(dm[:, AW:D], gate_b * cv, gc_ref[...], gs_ref[...])
        dgc_ref[...] += _fold8(dgc)
        dcv_ref[...] = dconv * gate_b
        db_ref[...] = (dconv * cv).astype(BF16)

    return pl.pallas_call(
        body, name="mix_bwd", grid=(s // tm,),
        in_specs=[_rows(tm, D), _resident((D, D)), _rows(tm, AW), _rows(tm, CW), _rows(tm, 3 * CW),
                  _full((1, AW)), _full((1, CW)), _full((CW, CW))],
        out_specs=[_rows(tm, AW), _rows(tm, AW), _rows(tm, CW), _rows(tm, CW),
                   _full((SUBLANES, AW)), _full((SUBLANES, CW))],
        out_shape=[jax.ShapeDtypeStruct((s, AW), BF16), jax.ShapeDtypeStruct((s, AW), F32),
                   jax.ShapeDtypeStruct((s, CW), F32), jax.ShapeDtypeStruct((s, CW), BF16),
                   jax.ShapeDtypeStruct((SUBLANES, AW), F32), jax.ShapeDtypeStruct((SUBLANES, CW), F32)],
        compiler_params=_cparams(48, ("arbitrary",)),
    )(dy, w_out, o, cv, bcu, ga, gc, gsum)


def _conv_bwd(dcv, db, bcu, cw8, *, tm):
    s = dcv.shape[0]
    nt = s // tm

    def body(dcv_ref, nxt_ref, db_ref, bcu_ref, halo_ref, cw_ref, dbcu_ref, dw_ref):
        i = pl.program_id(0)

        @pl.when(i == 0)
        def _():
            dw_ref[...] = jnp.zeros_like(dw_ref)

        z, z1, z2 = _conv_taps(bcu_ref, halo_ref, i == 0, tm)
        d = dcv_ref[...]
        dw_ref[0] += _fold8(d * z2)
        dw_ref[1] += _fold8(d * z1)
        dw_ref[2] += _fold8(d * z)
        nx = jnp.where(i == nt - 1, 0.0, nxt_ref[...])
        row = lax.broadcasted_iota(jnp.int32, (tm, CW), 0)
        d1 = jnp.where(row == tm - 1, nx[0:1, :], pltpu.roll(d, tm - 1, axis=0))
        d2 = jnp.where(row == tm - 2, nx[0:1, :], jnp.where(row == tm - 1, nx[1:2, :], pltpu.roll(d, tm - 2, axis=0)))
        dz = cw_ref[2:3, :] * d + cw_ref[1:2, :] * d1 + cw_ref[0:1, :] * d2
        dbcu_ref[:, 0:CW] = db_ref[...]
        dbcu_ref[:, CW:2 * CW] = (dz * bcu_ref[:, 2 * CW:3 * CW]).astype(BF16)
        dbcu_ref[:, 2 * CW:3 * CW] = (dz * bcu_ref[:, CW:2 * CW]).astype(BF16)

    return pl.pallas_call(
        body, name="conv_bwd", grid=(nt,),
        in_specs=[_rows(tm, CW),
                  pl.BlockSpec((SUBLANES, CW), lambda i: (jnp.minimum((i + 1) * (tm // SUBLANES), s // SUBLANES - 1), 0)),
                  _rows(tm, CW), _rows(tm, 3 * CW), _halo_before(tm, 3 * CW), _full((SUBLANES, CW))],
        out_specs=[_rows(tm, 3 * CW), _full((3, SUBLANES, CW))],
        out_shape=[jax.ShapeDtypeStruct((s, 3 * CW), BF16), jax.ShapeDtypeStruct((3, SUBLANES, CW), F32)],
        compiler_params=_cparams(48, ("arbitrary",)),
    )(dcv, dcv, db, bcu, bcu, cw8)


def _attn_bwd(qp, kp, v, do, lse, dl, mk, *, t):
    s = qp.shape[0]
    nq = s // t

    def body(q_ref, k_ref, v_ref, do_ref, lse_ref, dl_ref, mk_ref, dq_ref, dk_ref, dv_ref, dkx_ref, dq_acc):
        ki = pl.program_id(1)

        @pl.when(ki == 0)
        def _():
            dq_acc[...] = jnp.zeros_like(dq_acc)

        row = lax.broadcasted_iota(jnp.int32, (t, t), 0)
        col = lax.broadcasted_iota(jnp.int32, (t, t), 1)
        lane = lax.broadcasted_iota(jnp.int32, (t, 128), 1)

        def head_step(hh, qi, carry, masked):
            dk, dv, cs = carry
            off = pl.multiple_of(qi * t, t)
            rows = pl.ds(off, t)
            kh = k_ref[:, HP * hh:HP * (hh + 1)]
            q = q_ref[rows, HP * hh:HP * (hh + 1)]
            in_head = (lane >= DH * hh) & (lane < DH * (hh + 1))
            m_col = mk_ref[0, rows, DH * hh:DH * hh + 1]
            scale = jnp.exp(m_col - lse_ref[rows, DH * hh:DH * hh + 1])
            dom = jnp.where(in_head, do_ref[rows, :], jnp.zeros((), BF16))
            sc = lax.dot_general(q, kh, NT, preferred_element_type=F32) - m_col
            if masked:
                sc = jnp.where(col <= row, sc, -1e30)
            pt = jnp.exp(sc).astype(BF16)
            dp = lax.dot_general(dom, v_ref[...], NT, preferred_element_type=F32)
            ds32 = (pt.astype(F32) * scale) * (dp - dl_ref[rows, DH * hh:DH * hh + 1])
            ds = ds32.astype(BF16)
            cs = cs + _fold8(ds32)
            dv = dv + jnp.dot((dom.astype(F32) * scale).astype(BF16).T, pt, preferred_element_type=F32)
            dk = dk + jnp.dot(q.T, ds, preferred_element_type=F32)
            dq_acc[rows, HP * hh:HP * (hh + 1)] += jnp.dot(ds, kh, preferred_element_type=F32)
            return dk, dv, cs

        def step(qi, carry, masked):
            return tuple(head_step(hh, qi, carry[hh], masked) for hh in range(2))

        zero = (jnp.zeros((HP, t), F32), jnp.zeros((128, t), F32), jnp.zeros((SUBLANES, t), F32))
        carry = step(ki, (zero, zero), True)
        (dk0, dv0, cs0), (dk1, dv1, cs1) = lax.fori_loop(ki + 1, nq, functools.partial(step, masked=False), carry)
        dk_ref[:, 0:HP] = dk0.T.astype(BF16)
        dk_ref[:, HP:2 * HP] = dk1.T.astype(BF16)
        dv_ref[...] = (dv0 + dv1).T.astype(BF16)

        def as_column(cs):
            return lax.dot_general(cs, jnp.ones((SUBLANES, 128), F32), TN, precision=HIGHEST, preferred_element_type=F32)

        dkx_ref[...] = jnp.where(lane < DH, as_column(cs0), as_column(cs1))

        @pl.when(ki == nq - 1)
        def _():
            dq_ref[...] = dq_acc[...].astype(BF16)

    return pl.pallas_call(
        body, name="attn_bwd", grid=(H // 2, nq),
        in_specs=[pl.BlockSpec((s, 2 * HP), lambda p, i: (0, p)),
                  pl.BlockSpec((t, 2 * HP), lambda p, i: (i, p)),
                  pl.BlockSpec((t, 128), lambda p, i: (i, p)),
                  pl.BlockSpec((s, 128), lambda p, i: (0, p)),
                  pl.BlockSpec((s, 128), lambda p, i: (0, p)),
                  pl.BlockSpec((s, 128), lambda p, i: (0, p)),
                  pl.BlockSpec((1, s, 128), lambda p, i: (i, 0, p))],
        out_specs=[pl.BlockSpec((s, 2 * HP), lambda p, i: (0, p)),
                   pl.BlockSpec((t, 2 * HP), lambda p, i: (i, p)),
                   pl.BlockSpec((t, 128), lambda p, i: (i, p)),
                   pl.BlockSpec((t, 128), lambda p, i: (i, p))],
        out_shape=[jax.ShapeDtypeStruct((s, 1024), BF16), jax.ShapeDtypeStruct((s, 1024), BF16),
                   jax.ShapeDtypeStruct((s, AW), BF16), jax.ShapeDtypeStruct((s, AW), F32)],
        scratch_shapes=[pltpu.VMEM((s, 2 * HP), F32)],
        compiler_params=_cparams(56, ("arbitrary", "arbitrary")),
    )(qp, kp, v, do, lse, dl, mk)


def _forget_bwd(dkx, z, sel, *, tm):
    s = dkx.shape[0]
    nt = s // tm

    def body(dk_ref, z_ref, sel_ref, dfl_ref, dbf_ref, carry):
        @pl.when(pl.program_id(0) == 0)
        def _():
            carry[...] = jnp.zeros_like(carry)
            dbf_ref[...] = jnp.zeros_like(dbf_ref)

        dc = _split_dot(dk_ref[...], sel_ref[...])
        row = lax.broadcasted_iota(jnp.int32, (tm, tm), 0)
        col = lax.broadcasted_iota(jnp.int32, (tm, tm), 1)
        tri = (col >= row).astype(BF16)
        dlogf = _exact_dot01(tri, dc) + carry[0:1, :]
        carry[...] = jnp.broadcast_to(dlogf[0:1, :], carry.shape)
        dz = dlogf * (1.0 - jax.nn.sigmoid(z_ref[...]))
        dfl_ref[:, 0:128] = dz.astype(BF16)
        dfl_ref[:, 128:GW_TILE] = jnp.zeros((tm, GW_TILE - 128), BF16)
        dbf_ref[...] += _fold8(dz)

    rev = lambda i: (nt - 1 - i, 0)
    return pl.pallas_call(
        body, name="forget_bwd", grid=(nt,),
        in_specs=[pl.BlockSpec((tm, AW), rev), pl.BlockSpec((tm, 128), rev), _full((AW, 128))],
        out_specs=[pl.BlockSpec((tm, GW_TILE), rev), _full((SUBLANES, 128))],
        out_shape=[jax.ShapeDtypeStruct((s, GW_TILE), BF16), jax.ShapeDtypeStruct((SUBLANES, 128), F32)],
        scratch_shapes=[pltpu.VMEM((SUBLANES, 128), F32)],
        compiler_params=_cparams(48, ("arbitrary",)),
    )(dkx, z, sel)


def _in_proj_bwd(pieces, wp, x, g1, dx2, *, tm):
    s = x.shape[0]

    def body(q_ref, k_ref, v_ref, bcu_ref, f_ref, w_ref, x_ref, g_ref, dx2_ref, dx_ref, dg_ref):
        @pl.when(pl.program_id(0) == 0)
        def _():
            dg_ref[...] = jnp.zeros_like(dg_ref)

        dh = None
        for ref, (lo, hi) in zip((q_ref, k_ref, v_ref, bcu_ref, f_ref), PIECES):
            part = lax.dot_general(ref[...], w_ref[:, lo:hi], NT, preferred_element_type=F32)
            dh = part if dh is None else dh + part
        _, n, r = _rms_fwd(x_ref[...], g_ref[...])
        dxn, dg = _rms_bwd(dh, n, r, g_ref[...])
        dx_ref[...] = dx2_ref[...] + dxn
        dg_ref[...] += _fold8(dg)

    return pl.pallas_call(
        body, name="in_proj_bwd", grid=(s // tm,),
        in_specs=[_rows(tm, hi - lo) for lo, hi in PIECES] + [_resident((D, WP)), _rows(tm, D), _full((1, D)), _rows(tm, D)],
        out_specs=[_rows(tm, D), _full((SUBLANES, D))],
        out_shape=[jax.ShapeDtypeStruct((s, D), F32), jax.ShapeDtypeStruct((SUBLANES, D), F32)],
        compiler_params=_cparams(56, ("arbitrary",)),
    )(*pieces, wp, x, g1, dx2)


def _position():
    return lax.axis_index("x"), lax.axis_index("y"), lax.axis_index("c")


ANY = pl.BlockSpec(memory_space=pl.ANY)


def _all_gather(shards):
    n = len(shards)

    def body(*refs):
        x_refs, out_refs = refs[:n], refs[n:2 * n]
        send_sems, recv_sems, local_sems = refs[2 * n:]
        x, y, c = _position()
        me, sibling = (x, y, c), (x, y, 1 - c)
        chips = [(1 - x, y), (x, 1 - y), (1 - x, 1 - y)]

        def copy(a, k, block, to, own=False):
            slot = out_refs[a].at[4 * block[0] + 2 * block[1] + block[2]]
            return pltpu.make_async_remote_copy(
                src_ref=x_refs[a] if own else slot, dst_ref=slot,
                send_sem=send_sems.at[7 * a + k], recv_sem=recv_sems.at[7 * a + k], device_id=to, device_id_type=MESH_ID)

        mine = [pltpu.make_async_copy(x_refs[a], out_refs[a].at[4 * x + 2 * y + c], local_sems.at[a]) for a in range(n)]
        for cp in mine:
            cp.start()
        first = []
        for a in range(n):
            first.append(copy(a, 0, me, sibling, own=True))
            first += [copy(a, 1 + j, me, (*chip, c), own=True) for j, chip in enumerate(chips)]
        for cp in first:
            cp.start()
        passed = []
        for j, chip in enumerate(chips):
            for a in range(n):
                copy(a, 1 + j, (*chip, c), me).wait_recv()
                fwd = copy(a, 4 + j, (*chip, c), sibling)
                fwd.start()
                passed.append(fwd)
        for a in range(n):
            copy(a, 0, sibling, me).wait_recv()
            for j, chip in enumerate(chips):
                copy(a, 4 + j, (*chip, 1 - c), me).wait_recv()
        for cp in first + passed:
            cp.wait_send()
        for cp in mine:
            cp.wait()

    return pl.pallas_call(
        body, name="all_gather_weights",
        out_shape=[jax.ShapeDtypeStruct((NDEV,) + sh.shape, sh.dtype) for sh in shards],
        in_specs=[ANY] * n, out_specs=[ANY] * n,
        scratch_shapes=[pltpu.SemaphoreType.DMA((7 * n,)), pltpu.SemaphoreType.DMA((7 * n,)), pltpu.SemaphoreType.DMA((n,))],
    )(*shards)


def _pair_exchange(grads):
    n = len(grads)

    def body(*refs):
        g_refs, out_refs = refs[:n], refs[n:2 * n]
        send_sems, recv_sems = refs[2 * n:]
        x, y, c = _position()
        copies = [pltpu.make_async_remote_copy(
            src_ref=g_refs[a].at[:, pl.ds(1 - c, 1)], dst_ref=out_refs[a], send_sem=send_sems.at[a],
            recv_sem=recv_sems.at[a], device_id=(x, y, 1 - c), device_id_type=MESH_ID) for a in range(n)]
        for cp in copies:
            cp.start()
        for cp in copies:
            cp.wait()

    return pl.pallas_call(
        body, name="grad_pair_exchange",
        out_shape=[jax.ShapeDtypeStruct((4, 1) + g.shape[2:], g.dtype) for g in grads],
        in_specs=[ANY] * n, out_specs=[ANY] * n,
        scratch_shapes=[pltpu.SemaphoreType.DMA((n,)), pltpu.SemaphoreType.DMA((n,))],
    )(*grads)


def _pair_sum(g, got, idx, *, tr, name):
    r, c = g.shape[2:]

    def body(idx_ref, g_ref, got_ref, pb_ref, own_ref):
        p = g_ref[0, 0].astype(F32) + got_ref[0, 0].astype(F32)
        pb_ref[0] = p.astype(BF16)

        @pl.when(pl.program_id(1) == idx_ref[1])
        def _():
            own_ref[...] = p

    return pl.pallas_call(
        body, name=name,
        grid_spec=pltpu.PrefetchScalarGridSpec(
            num_scalar_prefetch=1, grid=(r // tr, 4),
            in_specs=[pl.BlockSpec((1, 1, tr, c), lambda i, j, idx: (j, idx[0], i, 0)),
                      pl.BlockSpec((1, 1, tr, c), lambda i, j, idx: (j, 0, i, 0))],
            out_specs=[pl.BlockSpec((1, tr, c), lambda i, j, idx: (j, i, 0)),
                       pl.BlockSpec((tr, c), lambda i, j, idx: (i, 0))]),
        out_shape=[jax.ShapeDtypeStruct((4, r, c), BF16), jax.ShapeDtypeStruct((r, c), F32)],
        compiler_params=_cparams(32, ("arbitrary", "arbitrary")),
    )(idx, g, got)


HBM = pl.BlockSpec(memory_space=pltpu.HBM)
SEM = pl.BlockSpec(memory_space=pltpu.SEMAPHORE)
DATAFLOW = pltpu.SideEffectType.DATAFLOW_SIDE_EFFECTING


PEERS = {"gather": NDEV - 1, "scatter": NDEV - 1, "chips": 3}


def _exchange_copies(src_refs, land_refs, send_sems, recv_sems, mode):
    x, y, c = _position()
    me, my_chip = 4 * x + 2 * y + c, 2 * x + y
    npeers = PEERS[mode]
    copies = []
    for a, (s_ref, l_ref) in enumerate(zip(src_refs, land_refs)):
        for k in range(npeers):
            if mode == "chips":
                px, py, pc = x ^ ((k + 1) >> 1), y ^ ((k + 1) & 1), c
                src, dst = s_ref.at[2 * px + py], l_ref.at[my_chip]
            else:
                px, py, pc = x ^ ((k + 1) >> 2), y ^ (((k + 1) >> 1) & 1), c ^ ((k + 1) & 1)
                src, dst = (s_ref.at[4 * px + 2 * py + pc] if mode == "scatter" else s_ref), l_ref.at[me]
            copies.append(pltpu.make_async_remote_copy(
                src_ref=src, dst_ref=dst, send_sem=send_sems.at[npeers * a + k], recv_sem=recv_sems.at[npeers * a + k],
                device_id=(px, py, pc), device_id_type=MESH_ID))
    return copies


def _exchange_start(srcs, lands, *, mode, name):
    n = len(srcs)
    nsem = PEERS[mode] * n

    def body(*refs):
        token = refs[-1]
        for cp in _exchange_copies(refs[:n], refs[n:2 * n], refs[2 * n], refs[2 * n + 1], mode):
            cp.start()
        token[...] = jnp.zeros_like(token)

    arrays = list(srcs) + list(lands)
    outs = pl.pallas_call(
        body, name=name,
        out_shape=(pltpu.SemaphoreType.DMA((nsem,)), pltpu.SemaphoreType.DMA((nsem,)),
                   *[pltpu.HBM(a.shape, a.dtype) for a in arrays], jax.ShapeDtypeStruct((SUBLANES, LANES), F32)),
        in_specs=[HBM] * (2 * n),
        out_specs=(SEM, SEM, *[HBM] * (2 * n), pl.BlockSpec(memory_space=pltpu.VMEM)),
        input_output_aliases={i: 2 + i for i in range(2 * n)},
        compiler_params=pltpu.CompilerParams(has_side_effects=DATAFLOW),
    )(*[pltpu.with_memory_space_constraint(a, pltpu.HBM) for a in arrays])
    return outs[0], outs[1], outs[2:2 + n], outs[2 + n:2 + 2 * n], outs[-1]


def _exchange_wait(send_sems, recv_sems, srcs, lands, after, *, mode, name):
    n = len(srcs)

    def body(*refs):
        for cp in _exchange_copies(refs[:n], refs[n:2 * n], refs[2 * n], refs[2 * n + 1], mode):
            cp.wait_send()
            cp.wait_recv()

    arrays = list(srcs) + list(lands)
    outs = pl.pallas_call(
        body, name=name,
        out_shape=tuple(pltpu.HBM(a.shape, a.dtype) for a in arrays),
        in_specs=[HBM] * (2 * n) + [SEM, SEM, ANY],
        out_specs=tuple([HBM] * (2 * n)),
        input_output_aliases={i: i for i in range(2 * n)},
        compiler_params=pltpu.CompilerParams(has_side_effects=DATAFLOW),
    )(*arrays, send_sems, recv_sems, after)
    return outs[n:]


def _own_slot(value, me):
    return lax.dynamic_update_index_in_dim(lax.empty((NDEV,) + value.shape, value.dtype), value, me, 0)


def _small_all_reduce(parts):
    def body(gmp_ref, gmo_ref, gfp_ref, gfo_ref, ga_ref, gc_ref, dw_ref, bf_ref, loss_ref,
             out_ref, buf, send_sems, recv_sems):
        x, y, c = _position()
        me = 4 * x + 2 * y + c

        def colsum(v):
            return jnp.sum(v, axis=0, keepdims=True)

        loss = jnp.sum(colsum(loss_ref[...]), axis=1, keepdims=True) * (0.5 / D)
        rows = [colsum(gmp_ref[...]), colsum(gmo_ref[...]), colsum(gfp_ref[...]), colsum(gfo_ref[...]),
                jnp.concatenate([colsum(ga_ref[...]), colsum(gc_ref[...])], axis=1),
                jnp.concatenate([colsum(dw_ref[0]), colsum(dw_ref[1])], axis=1),
                jnp.concatenate([colsum(dw_ref[2]), colsum(bf_ref[...]), jnp.broadcast_to(loss, (1, 128)),
                                 jnp.zeros((1, 256), F32)], axis=1),
                jnp.zeros((1, D), F32)]
        buf[me] = jnp.concatenate(rows, axis=0)
        copies = []
        for mm in range(1, NDEV):
            peer = (x ^ (mm >> 2), y ^ ((mm >> 1) & 1), c ^ (mm & 1))
            copies.append(pltpu.make_async_remote_copy(
                src_ref=buf.at[me], dst_ref=buf.at[me], send_sem=send_sems.at[mm - 1], recv_sem=recv_sems.at[mm - 1],
                device_id=peer, device_id_type=MESH_ID))
        for cp in copies:
            cp.start()
        for cp in copies:
            cp.wait_recv()
        for cp in copies:
            cp.wait_send()
        acc = buf[0]
        for d in range(1, NDEV):
            acc = acc + buf[d]
        out_ref[...] = acc

    vm = pl.BlockSpec(memory_space=pltpu.VMEM)
    return pl.pallas_call(
        body, name="small_all_reduce",
        out_shape=jax.ShapeDtypeStruct((SUBLANES, D), F32),
        in_specs=[vm] * len(parts), out_specs=vm,
        scratch_shapes=[pltpu.VMEM((NDEV, SUBLANES, D), F32), pltpu.SemaphoreType.DMA((7,)), pltpu.SemaphoreType.DMA((7,))],
    )(*parts)


def _adam_update(w, g, m, v):
    nm = ADAM_B1 * m + (1.0 - ADAM_B1) * g
    nv = ADAM_B2 * v + (1.0 - ADAM_B2) * (g * g)
    m_hat = nm / (1.0 - ADAM_B1 ** ADAM_STEP)
    v_hat = nv / (1.0 - ADAM_B2 ** ADAM_STEP)
    return -ADAM_LR * (m_hat / (jnp.sqrt(v_hat) + ADAM_EPS) + ADAM_WD * w), nm, nv


def _adamw(w, g, m, v, *, tr, name):
    rows, cols = w.shape

    def body(w_ref, g_ref, m_ref, v_ref, d_ref, nm_ref, nv_ref):
        d_ref[...], nm_ref[...], nv_ref[...] = _adam_update(w_ref[...], g_ref[...], m_ref[...], v_ref[...])

    spec = pl.BlockSpec((tr, cols), lambda i: (i, 0))
    return pl.pallas_call(
        body, name=name, grid=(rows // tr,),
        in_specs=[spec] * 4, out_specs=[spec] * 3,
        out_shape=[jax.ShapeDtypeStruct((rows, cols), F32)] * 3,
        compiler_params=_cparams(32, ("arbitrary",)),
    )(w, g, m, v)


def _chip_sum_adamw(got, own, idx, w, m, v, *, tr, name):
    rows, cols = w.shape

    def body(idx_ref, got_ref, own_ref, w_ref, m_ref, v_ref, g_ref, d_ref, nm_ref, nv_ref):
        g = jnp.zeros((tr, cols), F32)
        for j in range(4):
            g = g + jnp.where(idx_ref[1] == j, own_ref[...], got_ref[j].astype(F32))
        g_ref[...] = g
        d_ref[...], nm_ref[...], nv_ref[...] = _adam_update(w_ref[...], g, m_ref[...], v_ref[...])

    spec = pl.BlockSpec((tr, cols), lambda i, idx: (i, 0))
    return pl.pallas_call(
        body, name=name,
        grid_spec=pltpu.PrefetchScalarGridSpec(
            num_scalar_prefetch=1, grid=(rows // tr,),
            in_specs=[pl.BlockSpec((4, tr, cols), lambda i, idx: (0, i, 0)), spec, spec, spec, spec],
            out_specs=[spec] * 4),
        out_shape=[jax.ShapeDtypeStruct((rows, cols), F32)] * 4,
        compiler_params=_cparams(32, ("arbitrary",)),
    )(idx, got, own, w, m, v)


def _device_sum_adamw(land, w, m, v, *, tr, name):
    rows, cols = w.shape

    def body(land_ref, w_ref, m_ref, v_ref, g_ref, d_ref, nm_ref, nv_ref):
        g = land_ref[0].astype(F32)
        for dev in range(1, NDEV):
            g = g + land_ref[dev].astype(F32)
        g_ref[...] = g
        d_ref[...], nm_ref[...], nv_ref[...] = _adam_update(w_ref[...], g, m_ref[...], v_ref[...])

    spec = pl.BlockSpec((tr, cols), lambda i: (i, 0))
    return pl.pallas_call(
        body, name=name, grid=(rows // tr,),
        in_specs=[pl.BlockSpec((NDEV, tr, cols), lambda i: (0, i, 0)), spec, spec, spec],
        out_specs=[spec] * 4,
        out_shape=[jax.ShapeDtypeStruct((rows, cols), F32)] * 4,
        compiler_params=_cparams(32, ("arbitrary",)),
    )(land, w, m, v)


def _placement_constants():
    j = jnp.arange(128)[:, None]
    lane = jnp.arange(1024)[None, :]
    head, sub = lane // HP, lane % HP
    piece, jh = j // H, j % H
    valid = (j < 3 * H) & (jh == head)
    pq = jnp.where(valid & (sub == DH + piece), 1.0, 0.0).astype(BF16)
    pk = jnp.where(valid & (sub == DH + 3 + piece), -1.0, 0.0).astype(BF16)
    oq = jnp.where((sub >= DH + 3) & (sub < DH + 6), 1.0, 0.0).astype(F32)
    ok = jnp.where((sub >= DH) & (sub < DH + 3), 1.0, 0.0).astype(F32)
    r = jnp.arange(AW)[:, None]
    cc = jnp.arange(128)[None, :]
    sel = jnp.where((r % DH == 3) & (r // DH == cc), -1.0, 0.0).astype(BF16)
    gi = jnp.arange(CW)
    gsum = (gi[:, None] // DH == gi[None, :] // DH).astype(BF16)
    return pq, pk, oq, ok, sel, gsum


def _local_step(xs, tgt, wp, late_weights, cw8, bfp, g_attn_out, g_conv_out,
                g_mix_pre, g_mix_post, g_ffn_pre, g_ffn_post, early_grads=None, last_grad=None):
    pq, pk, oq, ok, sel, gsum = _placement_constants()
    h1t, qp, kp, vv, bcu, zf = _in_proj(xs, g_mix_pre, wp, bfp, pq, pk, oq, ok, tm=512)
    o, lse, mk = _attn_fwd(qp, kp, vv, t=512)
    w_out_f, wgu, wd = late_weights(lse)
    merged, y, x2, cv, h2 = _mix_out(o, bcu, cw8, g_attn_out, g_conv_out, gsum, w_out_f, xs, g_mix_post, g_ffn_pre, tm=512)
    gate, up, act, dx3, dff, loss_p, dg_ffn_post = _ffn_fwd_loss(h2, wgu, wd, x2, tgt, g_ffn_post, tm=512)

    dgu, dx2, dy, dg_ffn_pre, dg_mix_post = _ffn_bwd(dff, wd, gate, up, wgu, x2, g_ffn_pre, dx3, y, g_mix_post, tm=256)
    dw_down = _grad_matmul_blocks(act, dff, ts=4096, name="grad_w_down")
    dw_gu = _grad_matmul_blocks(dgu.reshape(NDEV, -1, FB), h2, ts=4096, name="grad_w_gate_up")
    dw_out = _grad_matmul(merged, dy, ta=1024, tb=1024, ts=2048, name="grad_w_out")
    token = early_grads(dw_out, dw_gu, dw_down) if early_grads is not None else None
    ga = g_attn_out if token is None else g_attn_out + token[0:1, 0:1]
    do, dl, dcv, db, dg_attn, dg_conv = _mix_bwd(dy, w_out_f, o, cv, bcu, ga, g_conv_out, gsum, tm=512)
    dbcu, dtaps = _conv_bwd(dcv, db, bcu, cw8, tm=512)
    dqp, dkp, dv, dkx = _attn_bwd(qp, kp, vv, do, lse, dl, mk, t=512)
    dfl, dbf = _forget_bwd(dkx, zf, sel, tm=512)
    pieces = (dqp, dkp, dv, dbcu, dfl)
    dwp = _grad_w_in(h1t, pieces)
    token = last_grad(dwp) if last_grad is not None else None
    g1 = g_mix_pre if token is None else g_mix_pre + token[0:1, 0:1]
    grad_x, dg_mix_pre = _in_proj_bwd(pieces, wp, xs, g1, dx2, tm=512)
    return (grad_x, dwp, dw_out, dw_gu, dw_down, dg_mix_pre, dg_mix_post, dg_ffn_pre, dg_ffn_post, dg_attn, dg_conv,
            dtaps, dbf, loss_p)


BIG_TILES = {"w_in": 256, "w_out": 128, "w_gate_up": 176, "w_down": 176}


def kernel(x, w_in, b_forget, conv_w, g_attn_out, g_conv_out, w_out, g_mix_pre, g_mix_post, w_gate_up, w_down, g_ffn_pre, g_ffn_post, loss_target, m_w_in, m_b_forget, m_conv_w, m_g_attn_out, m_g_conv_out, m_w_out, m_g_mix_pre, m_g_mix_post, m_w_gate_up, m_w_down, m_g_ffn_pre, m_g_ffn_post, v_w_in, v_b_forget, v_conv_w, v_g_attn_out, v_g_conv_out, v_w_out, v_g_mix_pre, v_g_mix_post, v_w_gate_up, v_w_down, v_g_ffn_pre, v_g_ffn_post):
    xc, yc, cc = _position()
    my_chip = 2 * xc + yc
    me = 2 * my_chip + cc
    idx = jnp.stack([cc, my_chip]).astype(jnp.int32)
    tables = _in_layout_tables()
    pad_in = lambda a: jnp.pad(a, ((0, 0), (0, IN_PAD - IN_COLS)))

    g_in, g_taps = _all_gather([pad_in(w_in[0]).astype(BF16), conv_w[0]])
    wp = _assemble_w_in(g_in, tables, tr=256)
    cw8 = jnp.pad(g_taps.transpose(1, 0, 2).reshape(3, CW), ((0, SUBLANES - 3), (0, 0)))

    late = [w_out[0].astype(BF16), w_gate_up[0].astype(BF16), w_down[0].astype(BF16)]
    ssem, rsem, late_thru, land_thru, token = _exchange_start(
        late, [_own_slot(s, me) for s in late], mode="gather", name="gather_late_start")
    bfp = jnp.pad(b_forget, ((0, 0), (0, 128 - H))) + token[0:1, :]

    def late_weights(after):
        l_out, l_gu, l_down = _exchange_wait(ssem, rsem, late_thru, land_thru, after, mode="gather", name="gather_late_wait")
        return l_out.reshape(D, D), l_gu.reshape(2, 4, D, FB), l_down.reshape(4, FB, D)

    early = {}

    def early_grads(dw_out, dw_gu, dw_down):
        srcs = [dw_out.reshape(NDEV, D // NDEV, D), dw_gu, dw_down.reshape(NDEV, DFF // NDEV, D)]
        lands = [_own_slot(lax.dynamic_index_in_dim(s, me, 0, keepdims=False), me) for s in srcs]
        early["handles"] = _exchange_start(srcs, lands, mode="scatter", name="scatter_early_start")
        return early["handles"][4]

    last = {}

    def last_grad(dwp):
        g_w_in = _disassemble_w_in(dwp, tables, tr=256).reshape(4, 2, D, IN_PAD)
        (from_sibling,) = _pair_exchange([g_w_in])
        pair_b, last["own"] = _pair_sum(g_w_in, from_sibling, idx, tr=BIG_TILES["w_in"], name="grad_pair_sum_w_in")
        land = lax.dynamic_update_index_in_dim(lax.empty(pair_b.shape, pair_b.dtype),
                                               lax.dynamic_index_in_dim(pair_b, my_chip, 0, keepdims=False), my_chip, 0)
        last["handles"] = _exchange_start([pair_b], [land], mode="chips", name="chips_w_in_start")
        return last["handles"][4]

    (grad_x, dwp, dw_out, dw_gu, dw_down, dg_mix_pre, dg_mix_post, dg_ffn_pre, dg_ffn_post, dg_attn, dg_conv,
     dtaps, dbf, loss_p) = _local_step(x[0], loss_target[0], wp, late_weights, cw8, bfp, g_attn_out, g_conv_out,
                                        g_mix_pre, g_mix_post, g_ffn_pre, g_ffn_post, early_grads, last_grad)

    e_ssem, e_rsem, e_srcs, e_lands, _ = early["handles"]
    land_out, land_gu, land_down = _exchange_wait(e_ssem, e_rsem, e_srcs, e_lands, dg_mix_pre, mode="scatter",
                                                  name="scatter_early_wait")
    res = {}
    big = {"w_out": (land_out, w_out[0], m_w_out[0], v_w_out[0]),
           "w_gate_up": (land_gu, w_gate_up[0].T, m_w_gate_up[0].T, v_w_gate_up[0].T),
           "w_down": (land_down, w_down[0], m_w_down[0], v_w_down[0])}
    for name, (land, w, m, v) in big.items():
        outs = _device_sum_adamw(land, w, m, v, tr=BIG_TILES[name], name="adamw_" + name)
        res[name] = [(o.T if name == "w_gate_up" else o)[None] for o in outs]
    c_ssem, c_rsem, c_srcs, c_lands, _ = last["handles"]
    after = sum(res[n][1][0, :SUBLANES, :LANES] for n in big)
    (from_chips,) = _exchange_wait(c_ssem, c_rsem, c_srcs, c_lands, after, mode="chips", name="chips_w_in_wait")
    outs = _chip_sum_adamw(from_chips, last["own"], idx, pad_in(w_in[0]), pad_in(m_w_in[0]), pad_in(v_w_in[0]),
                           tr=BIG_TILES["w_in"], name="adamw_w_in")
    res["w_in"] = [o[:, :IN_COLS][None] for o in outs]

    small = _small_all_reduce([dg_mix_pre, dg_mix_post, dg_ffn_pre, dg_ffn_post, dg_attn, dg_conv, dtaps, dbf, loss_p])
    taps_full = jnp.concatenate([small[5:6, :CW], small[5:6, CW:], small[6:7, :CW]], axis=0)
    small_grads = {
        "b_forget": small[6:7, CW:CW + H], "conv_w": lax.dynamic_slice(taps_full, (0, me * 64), (3, 64)),
        "g_attn_out": small[4:5, :AW], "g_conv_out": small[4:5, AW:], "g_mix_pre": small[0:1], "g_mix_post": small[1:2],
        "g_ffn_pre": small[2:3], "g_ffn_post": small[3:4]}
    loss = small[6, CW + 128]
    smalls = {"b_forget": (b_forget, m_b_forget, v_b_forget), "conv_w": (conv_w[0], m_conv_w[0], v_conv_w[0]),
              "g_attn_out": (g_attn_out, m_g_attn_out, v_g_attn_out), "g_conv_out": (g_conv_out, m_g_conv_out, v_g_conv_out),
              "g_mix_pre": (g_mix_pre, m_g_mix_pre, v_g_mix_pre), "g_mix_post": (g_mix_post, m_g_mix_post, v_g_mix_post),
              "g_ffn_pre": (g_ffn_pre, m_g_ffn_pre, v_g_ffn_pre), "g_ffn_post": (g_ffn_post, m_g_ffn_post, v_g_ffn_post)}
    for name, (w, m, v) in smalls.items():
        g = small_grads[name]
        outs = [g] + list(_adamw(w, g, m, v, tr=w.shape[0], name="adamw_" + name))
        res[name] = [o[None] for o in outs] if name == "conv_w" else outs

    order = ["w_in", "b_forget", "conv_w", "g_attn_out", "g_conv_out", "w_out", "g_mix_pre", "g_mix_post",
             "w_gate_up", "w_down", "g_ffn_pre", "g_ffn_post"]
    outs = [loss, grad_x[None]]
    for k in range(4):
        outs += [res[n][k] for n in order]
    return tuple(outs)
```

```python
import functools

import numpy as np

import jax
import jax.numpy as jnp
from jax import lax
from jax.experimental import pallas as pl
from jax.experimental.pallas import tpu as pltpu

F32 = jnp.float32
BF16 = jnp.bfloat16
HIGHEST = lax.Precision.HIGHEST
MESH_ID = pl.DeviceIdType.MESH

D = 1024
H = 8
DH = 64
AW = 512
CW = 512
DFF = 2816
FB = DFF // 4
HP = 128
OFF_Q, OFF_K, OFF_V, OFF_BCU, OFF_F = 0, 1024, 2048, 2560, 4096
WP = OFF_F + 128
PIECES = ((OFF_Q, OFF_K), (OFF_K, OFF_V), (OFF_V, OFF_BCU), (OFF_BCU, OFF_F), (OFF_F, WP))
EPS = 1e-6
NDEV = 8
LANES = 128
SUBLANES = 8
IN_COLS = 385
IN_PAD = 512
WIN = 896
ADAM_LR, ADAM_B1, ADAM_B2, ADAM_EPS, ADAM_WD, ADAM_STEP = 0.001, 0.9, 0.999, 1e-08, 0.01, 10

NT = (((1,), (1,)), ((), ()))
TN = (((0,), (0,)), ((), ()))


def _cparams(vmem_mb=None, sem=None):
    kw = {}
    if vmem_mb is not None:
        kw["vmem_limit_bytes"] = vmem_mb << 20
    if sem is not None:
        kw["dimension_semantics"] = sem
    return pltpu.CompilerParams(**kw)


def _full(shape):
    return pl.BlockSpec(shape, lambda *_: (0,) * len(shape))


def _resident(shape):
    return pl.BlockSpec(shape, lambda *_: (0,) * len(shape), pipeline_mode=pl.Buffered(1))


def _rows(tm, width):
    return pl.BlockSpec((tm, width), lambda i: (i, 0))


def _fold8(v):
    r, w = v.shape
    return jnp.sum(v.reshape(r // SUBLANES, SUBLANES, w), axis=0)


def _split_dot(v, m01):
    hi = v.astype(BF16)
    lo = (v - hi.astype(F32)).astype(BF16)
    return (jnp.dot(hi, m01, preferred_element_type=F32)
            + jnp.dot(lo, m01, preferred_element_type=F32))


def _exact_dot01(m01, v):
    p1 = v.astype(BF16)
    r1 = v - p1.astype(F32)
    p2 = r1.astype(BF16)
    p3 = (r1 - p2.astype(F32)).astype(BF16)
    return (jnp.dot(m01, p1, preferred_element_type=F32) + jnp.dot(m01, p2, preferred_element_type=F32)
            + jnp.dot(m01, p3, preferred_element_type=F32))


def _rms_fwd(v, g):
    r = lax.rsqrt(jnp.mean(v * v, axis=-1, keepdims=True) + EPS)
    n = v * r
    return n * g, n, r


def _rms_bwd(do, n, r, g):
    dn = do * g
    return r * (dn - n * jnp.mean(dn * n, axis=-1, keepdims=True)), do * n


def _padded_column(n):
    if n < AW:
        return OFF_Q + HP * (n // DH) + n % DH, 0.125
    if n < 2 * AW:
        m = n - AW
        return OFF_K + HP * (m // DH) + m % DH, 1.0
    if n < 3 * AW:
        return OFF_V + n - 2 * AW, 1.0
    if n < 3 * AW + H:
        return OFF_F + n - 3 * AW, 1.0
    return OFF_BCU + n - 3 * AW - H, 1.0


def _in_layout_tables():
    dest = -np.ones((IN_PAD, LANES), np.int32)
    dest_f = -np.ones((IN_PAD, LANES), np.int32)
    scale = np.zeros((IN_PAD, LANES), np.float32)
    starts = []
    for k in range(NDEV):
        cols = [_padded_column(IN_COLS * k + j) for j in range(IN_COLS)]
        main = [c for c, _ in cols if c < OFF_F]
        ws = min((min(main) // LANES) * LANES, OFF_F - WIN)
        assert ws <= min(main) and max(main) < ws + WIN
        starts.append(ws)
        for j, (c, sc) in enumerate(cols):
            scale[j, k] = sc
            if c < OFF_F:
                dest[j, k] = c - ws
            else:
                dest_f[j, k] = c - OFF_F
    f_shards = tuple(k for k in range(NDEV) if (dest_f[:, k] >= 0).any())
    return tuple(starts), f_shards, jnp.asarray(dest), jnp.asarray(dest_f), jnp.asarray(scale)


def _perm(dest_ref, scale_ref, k, width):
    lane = lax.broadcasted_iota(jnp.int32, (IN_PAD, width), 1)
    return jnp.where(dest_ref[:, k:k + 1] == lane, scale_ref[:, k:k + 1], 0.0).astype(BF16)


def _assemble_w_in(blocks, tables, *, tr):
    starts, f_shards, dest, dest_f, scale = tables

    def body(b_ref, dest_ref, destf_ref, scale_ref, o_ref):
        o_ref[...] = jnp.zeros_like(o_ref)
        for k in range(NDEV):
            b = b_ref[k]
            ws = starts[k]
            part = jnp.dot(b, _perm(dest_ref, scale_ref, k, WIN), preferred_element_type=F32)
            o_ref[:, ws:ws + WIN] += part.astype(BF16)
            if k in f_shards:
                part = jnp.dot(b, _perm(destf_ref, scale_ref, k, 128), preferred_element_type=F32)
                o_ref[:, OFF_F:WP] += part.astype(BF16)

    tab = _full((IN_PAD, LANES))
    return pl.pallas_call(
        body, name="assemble_w_in", grid=(D // tr,),
        in_specs=[pl.BlockSpec((NDEV, tr, IN_PAD), lambda i: (0, i, 0)), tab, tab, tab],
        out_specs=_rows(tr, WP),
        out_shape=jax.ShapeDtypeStruct((D, WP), BF16),
        compiler_params=_cparams(48, ("arbitrary",)),
    )(blocks, dest, dest_f, scale)


def _disassemble_w_in(dwp, tables, *, tr):
    starts, f_shards, dest, dest_f, scale = tables
    width = dwp.shape[1]

    def body(g_ref, dest_ref, destf_ref, scale_ref, o_ref):
        for k in range(NDEV):
            ws = starts[k]
            acc = lax.dot_general(g_ref[:, ws:ws + WIN], _perm(dest_ref, scale_ref, k, WIN), NT, preferred_element_type=F32)
            if k in f_shards:
                acc = acc + lax.dot_general(g_ref[:, OFF_F:WP], _perm(destf_ref, scale_ref, k, 128), NT,
                                            preferred_element_type=F32)
            o_ref[k] = acc.astype(BF16)

    tab = _full((IN_PAD, LANES))
    return pl.pallas_call(
        body, name="disassemble_w_in", grid=(D // tr,),
        in_specs=[_rows(tr, width), tab, tab, tab],
        out_specs=pl.BlockSpec((NDEV, tr, IN_PAD), lambda i: (0, i, 0)),
        out_shape=jax.ShapeDtypeStruct((NDEV, D, IN_PAD), BF16),
        compiler_params=_cparams(48, ("arbitrary",)),
    )(dwp, dest, dest_f, scale)


def _in_proj(x, g1, wp, bfp, pq, pk, oq, ok, *, tm):
    s = x.shape[0]

    def body(x_ref, g_ref, w_ref, bf_ref, pq_ref, pk_ref, oq_ref, ok_ref,
             ht_ref, qp_ref, kp_ref, v_ref, bcu_ref, z_ref, carry):
        @pl.when(pl.program_id(0) == 0)
        def _():
            carry[...] = jnp.zeros_like(carry)

        h = _rms_fwd(x_ref[...], g_ref[...])[0].astype(BF16)
        ht_ref[...] = h.T
        z = jnp.dot(h, w_ref[:, OFF_F:WP], preferred_element_type=F32) + bf_ref[...]
        z_ref[...] = z
        lane = lax.broadcasted_iota(jnp.int32, (tm, 128), 1)
        logf = jnp.where(lane < H, jnp.minimum(z, 0.0) - jnp.log(1.0 + jnp.exp(-jnp.abs(z))), 0.0)
        row = lax.broadcasted_iota(jnp.int32, (tm, tm), 0)
        col = lax.broadcasted_iota(jnp.int32, (tm, tm), 1)
        tri = (col <= row).astype(BF16)
        c = _exact_dot01(tri, logf) + carry[0:1, :]
        carry[...] = jnp.broadcast_to(c[tm - 1:tm, :], carry.shape)
        c1 = c.astype(BF16).astype(F32)
        r1 = c - c1
        c2 = r1.astype(BF16).astype(F32)
        c3 = (r1 - c2).astype(BF16).astype(F32)
        zc = (c1 + pltpu.roll(c2, 8, axis=1) + pltpu.roll(c3, 16, axis=1)).astype(BF16)
        q = jnp.dot(h, w_ref[:, OFF_Q:OFF_K], preferred_element_type=F32)
        qp_ref[...] = (q + jnp.dot(zc, pq_ref[...], preferred_element_type=F32) + oq_ref[...]).astype(BF16)
        k = jnp.dot(h, w_ref[:, OFF_K:OFF_V], preferred_element_type=F32)
        kp_ref[...] = (k + jnp.dot(zc, pk_ref[...], preferred_element_type=F32) + ok_ref[...]).astype(BF16)
        v_ref[...] = jnp.dot(h, w_ref[:, OFF_V:OFF_BCU], preferred_element_type=F32).astype(BF16)
        bcu_ref[...] = jnp.dot(h, w_ref[:, OFF_BCU:OFF_F], preferred_element_type=F32)

    return pl.pallas_call(
        body, name="in_proj", grid=(s // tm,),
        in_specs=[_rows(tm, D), _full((1, D)), _resident((D, WP)), _full((1, 128)),
                  _full((128, 1024)), _full((128, 1024)), _full((1, 1024)), _full((1, 1024))],
        out_specs=[pl.BlockSpec((D, tm), lambda i: (0, i)), _rows(tm, 1024), _rows(tm, 1024), _rows(tm, AW),
                   _rows(tm, 3 * CW), _rows(tm, 128)],
        out_shape=[jax.ShapeDtypeStruct((D, s), BF16), jax.ShapeDtypeStruct((s, 1024), BF16),
                   jax.ShapeDtypeStruct((s, 1024), BF16), jax.ShapeDtypeStruct((s, AW), BF16),
                   jax.ShapeDtypeStruct((s, 3 * CW), F32), jax.ShapeDtypeStruct((s, 128), F32)],
        scratch_shapes=[pltpu.VMEM((SUBLANES, 128), F32)],
        compiler_params=_cparams(56, ("arbitrary",)),
    )(x, g1, wp, bfp, pq, pk, oq, ok)


def _attn_fwd(qp, kp, v, *, t):
    s = qp.shape[0]
    nq = s // t

    def body(q_ref, k_ref, v_ref, o_ref, lse_ref, mk_ref):
        qi = pl.program_id(1)
        row = lax.broadcasted_iota(jnp.int32, (t, t), 0)
        col = lax.broadcasted_iota(jnp.int32, (t, t), 1)
        lane = lax.broadcasted_iota(jnp.int32, (t, 128), 1)

        def head_step(hh, ki, carry, masked):
            m, l, acc = carry
            off = pl.multiple_of(ki * t, t)
            q = q_ref[:, HP * hh:HP * (hh + 1)]
            k = k_ref[pl.ds(off, t), HP * hh:HP * (hh + 1)]
            sc = lax.dot_general(q, k, NT, preferred_element_type=F32)
            if masked:
                sc = jnp.where(col <= row, sc, -1e30)
            mn = jnp.maximum(m, jnp.max(sc, axis=-1, keepdims=True))
            p = jnp.exp(sc - mn)
            a = jnp.exp(m - mn)
            l = a * l + jnp.sum(p, axis=-1, keepdims=True)
            acc = a * acc + jnp.dot(p.astype(BF16), v_ref[pl.ds(off, t), :], preferred_element_type=F32)
            return mn, l, acc

        def step(ki, carry, masked):
            new = tuple(head_step(hh, ki, carry[hh], masked) for hh in range(2))
            mk_ref[ki] = jnp.where(lane < DH, jnp.broadcast_to(new[0][0], (t, 128)), jnp.broadcast_to(new[1][0], (t, 128)))
            return new

        init = (jnp.full((t, 1), -1e30, F32), jnp.zeros((t, 1), F32), jnp.zeros((t, 128), F32))
        carry = lax.fori_loop(0, qi, functools.partial(step, masked=False), (init, init))
        (m0, l0, acc0), (m1, l1, acc1) = step(qi, carry, True)
        o_ref[...] = jnp.where(lane < DH, acc0 / l0, acc1 / l1)
        lse_ref[...] = jnp.where(lane < DH, jnp.broadcast_to(m0 + jnp.log(l0), (t, 128)),
                                 jnp.broadcast_to(m1 + jnp.log(l1), (t, 128)))

    return pl.pallas_call(
        body, name="attn_fwd", grid=(H // 2, nq),
        in_specs=[pl.BlockSpec((t, 2 * HP), lambda p, i: (i, p)),
                  pl.BlockSpec((s, 2 * HP), lambda p, i: (0, p)),
                  pl.BlockSpec((s, 128), lambda p, i: (0, p))],
        out_specs=[pl.BlockSpec((t, 128), lambda p, i: (i, p)), pl.BlockSpec((t, 128), lambda p, i: (i, p)),
                   pl.BlockSpec((nq, t, 128), lambda p, i: (0, i, p))],
        out_shape=[jax.ShapeDtypeStruct((s, AW), F32), jax.ShapeDtypeStruct((s, AW), F32),
                   jax.ShapeDtypeStruct((nq, s, AW), F32)],
        compiler_params=_cparams(48, ("arbitrary", "arbitrary")),
    )(qp, kp, v)


def _conv_taps(bcu_ref, halo_ref, first, tm):
    z = bcu_ref[:, CW:2 * CW] * bcu_ref[:, 2 * CW:3 * CW]
    zh = jnp.where(first, 0.0, halo_ref[:, CW:2 * CW] * halo_ref[:, 2 * CW:3 * CW])
    row = lax.broadcasted_iota(jnp.int32, (tm, CW), 0)
    z1 = jnp.where(row == 0, zh[7:8, :], pltpu.roll(z, 1, axis=0))
    z2 = jnp.where(row == 0, zh[6:7, :], jnp.where(row == 1, zh[7:8, :], pltpu.roll(z, 2, axis=0)))
    return z, z1, z2


def _halo_before(tm, width):
    return pl.BlockSpec((SUBLANES, width), lambda i: (jnp.maximum(i * (tm // SUBLANES) - 1, 0), 0))


def _mix_out(o, bcu, cw8, ga, gc, gsum, w_out, x, g_post, g_ffn_pre, *, tm):
    s = x.shape[0]

    def body(o_ref, bcu_ref, halo_ref, cw_ref, ga_ref, gc_ref, gs_ref, w_ref, x_ref, g_ref, gf_ref,
             merged_ref, y_ref, x2_ref, cv_ref, h2_ref):
        z, z1, z2 = _conv_taps(bcu_ref, halo_ref, pl.program_id(0) == 0, tm)
        cv = cw_ref[0:1, :] * z2 + cw_ref[1:2, :] * z1 + cw_ref[2:3, :] * z
        cv_ref[...] = cv
        conv = bcu_ref[:, 0:CW] * cv
        ov = o_ref[...]
        ra = lax.rsqrt(_split_dot(ov * ov, gs_ref[...]) * (1.0 / DH) + EPS)
        rc = lax.rsqrt(_split_dot(conv * conv, gs_ref[...]) * (1.0 / DH) + EPS)
        merged = jnp.concatenate([ov * ra * ga_ref[...], conv * rc * gc_ref[...]], axis=1).astype(BF16)
        merged_ref[...] = merged
        y = jnp.dot(merged, w_ref[...], preferred_element_type=F32)
        y_ref[...] = y
        x2 = x_ref[...] + _rms_fwd(y, g_ref[...])[0]
        x2_ref[...] = x2
        h2_ref[...] = _rms_fwd(x2, gf_ref[...])[0].astype(BF16)

    return pl.pallas_call(
        body, name="mix_out", grid=(s // tm,),
        in_specs=[_rows(tm, AW), _rows(tm, 3 * CW), _halo_before(tm, 3 * CW), _full((SUBLANES, CW)),
                  _full((1, AW)), _full((1, CW)), _full((CW, CW)), _resident((D, D)), _rows(tm, D), _full((1, D)),
                  _full((1, D))],
        out_specs=[_rows(tm, D), _rows(tm, D), _rows(tm, D), _rows(tm, CW), _rows(tm, D)],
        out_shape=[jax.ShapeDtypeStruct((s, D), BF16), jax.ShapeDtypeStruct((s, D), F32),
                   jax.ShapeDtypeStruct((s, D), F32), jax.ShapeDtypeStruct((s, CW), F32),
                   jax.ShapeDtypeStruct((s, D), BF16)],
        compiler_params=_cparams(48, ("arbitrary",)),
    )(o, bcu, bcu, cw8, ga, gc, gsum, w_out, x, g_post, g_ffn_pre)


def _ffn_fwd_loss(h2, wgu, wd, x2, target, g_post, *, tm):
    s = x2.shape[0]

    def body(h_ref, w_ref, wd_ref, x2_ref, t_ref, g_ref,
             gate_ref, up_ref, a_ref, dx3_ref, dff_ref, loss_ref, dg_ref):
        @pl.when(pl.program_id(0) == 0)
        def _():
            loss_ref[...] = jnp.zeros_like(loss_ref)
            dg_ref[...] = jnp.zeros_like(dg_ref)

        h = h_ref[...]
        ff = None
        for j in range(4):
            gate = jnp.dot(h, w_ref[0, j], preferred_element_type=F32)
            up = jnp.dot(h, w_ref[1, j], preferred_element_type=F32)
            gate_ref[j] = gate.astype(BF16)
            up_ref[j] = up.astype(BF16)
            act = (gate * jax.nn.sigmoid(gate) * up).astype(BF16)
            a_ref[j] = act
            part = jnp.dot(act, wd_ref[j], preferred_element_type=F32)
            ff = part if ff is None else ff + part
        out, n, r = _rms_fwd(ff, g_ref[...])
        e = x2_ref[...] + out - t_ref[...]
        loss_ref[...] += _fold8(e * e)
        dx3 = e * (1.0 / D)
        dx3_ref[...] = dx3
        dff, dg = _rms_bwd(dx3, n, r, g_ref[...])
        dff_ref[...] = dff.astype(BF16)
        dg_ref[...] += _fold8(dg)

    blk4 = pl.BlockSpec((4, tm, FB), lambda i: (0, i, 0))
    return pl.pallas_call(
        body, name="ffn_fwd_loss", grid=(s // tm,),
        in_specs=[_rows(tm, D), _resident((2, 4, D, FB)), _resident((4, FB, D)), _rows(tm, D), _rows(tm, D), _full((1, D))],
        out_specs=[blk4, blk4, blk4, _rows(tm, D), _rows(tm, D), _full((SUBLANES, D)), _full((SUBLANES, D))],
        out_shape=[jax.ShapeDtypeStruct((4, s, FB), BF16)] * 3
        + [jax.ShapeDtypeStruct((s, D), F32), jax.ShapeDtypeStruct((s, D), BF16),
           jax.ShapeDtypeStruct((SUBLANES, D), F32), jax.ShapeDtypeStruct((SUBLANES, D), F32)],
        compiler_params=_cparams(56, ("arbitrary",)),
    )(h2, wgu, wd, x2, target, g_post)


def _ffn_bwd(dff, wd, gate, up, wgu, x2, g_pre, dx3, y, g_post, *, tm):
    s = x2.shape[0]

    def body(dff_ref, wd_ref, gate_ref, up_ref, w_ref, x2_ref, gpre_ref, dx3_ref, y_ref, gpost_ref,
             dgu_ref, dx2_ref, dy_ref, dgpre_ref, dgpost_ref):
        @pl.when(pl.program_id(0) == 0)
        def _():
            dgpre_ref[...] = jnp.zeros_like(dgpre_ref)
            dgpost_ref[...] = jnp.zeros_like(dgpost_ref)

        dff = dff_ref[...]
        dh2 = None
        for j in range(4):
            da = lax.dot_general(dff, wd_ref[j], NT, preferred_element_type=F32)
            g = gate_ref[j].astype(F32)
            sg = jax.nn.sigmoid(g)
            dgate = (da * up_ref[j].astype(F32) * (sg * (1.0 + g * (1.0 - sg)))).astype(BF16)
            dup = (da * (g * sg)).astype(BF16)
            dgu_ref[0, j] = dgate
            dgu_ref[1, j] = dup
            part = (lax.dot_general(dgate, w_ref[0, j], NT, preferred_element_type=F32)
                    + lax.dot_general(dup, w_ref[1, j], NT, preferred_element_type=F32))
            dh2 = part if dh2 is None else dh2 + part
        _, n2, r2 = _rms_fwd(x2_ref[...], gpre_ref[...])
        dxn, dg = _rms_bwd(dh2, n2, r2, gpre_ref[...])
        dgpre_ref[...] += _fold8(dg)
        dx2 = dx3_ref[...] + dxn
        dx2_ref[...] = dx2
        _, ny, ry = _rms_fwd(y_ref[...], gpost_ref[...])
        dy, dg2 = _rms_bwd(dx2, ny, ry, gpost_ref[...])
        dy_ref[...] = dy.astype(BF16)
        dgpost_ref[...] += _fold8(dg2)

    blk4 = pl.BlockSpec((4, tm, FB), lambda i: (0, i, 0))
    return pl.pallas_call(
        body, name="ffn_bwd", grid=(s // tm,),
        in_specs=[_rows(tm, D), _resident((4, FB, D)), blk4, blk4, _resident((2, 4, D, FB)), _rows(tm, D), _full((1, D)),
                  _rows(tm, D), _rows(tm, D), _full((1, D))],
        out_specs=[pl.BlockSpec((2, 4, tm, FB), lambda i: (0, 0, i, 0)), _rows(tm, D), _rows(tm, D),
                   _full((SUBLANES, D)), _full((SUBLANES, D))],
        out_shape=[jax.ShapeDtypeStruct((2, 4, s, FB), BF16), jax.ShapeDtypeStruct((s, D), F32),
                   jax.ShapeDtypeStruct((s, D), BF16), jax.ShapeDtypeStruct((SUBLANES, D), F32),
                   jax.ShapeDtypeStruct((SUBLANES, D), F32)],
        compiler_params=_cparams(56, ("arbitrary",)),
    )(dff, wd, gate, up, wgu, x2, g_pre, dx3, y, g_post)


def _grad_matmul(a, b, *, ta, tb, ts, name):
    s, ka = a.shape
    nb = b.shape[1]
    ts = min(ts, s)
    nk = s // ts

    def body(a_ref, b_ref, o_ref, acc):
        k = pl.program_id(2)

        @pl.when(k == 0)
        def _():
            acc[...] = jnp.zeros_like(acc)

        acc[...] += lax.dot_general(a_ref[...], b_ref[...], TN, preferred_element_type=F32)

        @pl.when(k == nk - 1)
        def _():
            o_ref[...] = acc[...].astype(BF16)

    return pl.pallas_call(
        body, name=name, grid=(ka // ta, nb // tb, nk),
        in_specs=[pl.BlockSpec((ts, ta), lambda i, j, k: (k, i)), pl.BlockSpec((ts, tb), lambda i, j, k: (k, j))],
        out_specs=pl.BlockSpec((ta, tb), lambda i, j, k: (i, j)),
        out_shape=jax.ShapeDtypeStruct((ka, nb), BF16),
        scratch_shapes=[pltpu.VMEM((ta, tb), F32)],
        compiler_params=_cparams(48, ("arbitrary", "arbitrary", "arbitrary")),
    )(a, b)


def _grad_matmul_t(at, b, *, tb, name):
    ka, s = at.shape
    blocked = b.ndim == 3
    nb = b.shape[-1]
    steps = b.shape[0] if blocked else nb // tb
    width = nb if blocked else tb

    def body(a_ref, b_ref, o_ref):
        bv = b_ref[0] if blocked else b_ref[...]
        res = jnp.dot(a_ref[...], bv, preferred_element_type=F32).astype(BF16)
        if blocked:
            o_ref[0] = res
        else:
            o_ref[...] = res

    if blocked:
        b_spec = pl.BlockSpec((1, s, nb), lambda j: (j, 0, 0))
        o_spec = pl.BlockSpec((1, ka, nb), lambda j: (j, 0, 0))
        o_shape = jax.ShapeDtypeStruct((steps, ka, nb), BF16)
    else:
        b_spec = pl.BlockSpec((s, width), lambda j: (0, j))
        o_spec = pl.BlockSpec((ka, width), lambda j: (0, j))
        o_shape = jax.ShapeDtypeStruct((ka, nb), BF16)
    return pl.pallas_call(
        body, name=name, grid=(steps,),
        in_specs=[_resident((ka, s)), b_spec], out_specs=o_spec, out_shape=o_shape,
        compiler_params=_cparams(56, ("arbitrary",)),
    )(at, b)


GW_TILE = 256


def _grad_w_in(h1t, pieces):
    ka, s = h1t.shape
    widths = [p.shape[1] for p in pieces]
    assert all(w % GW_TILE == 0 for w in widths)
    first = [sum(widths[:i]) // GW_TILE for i in range(len(pieces))]
    count = [w // GW_TILE for w in widths]

    def body(a_ref, *refs):
        o_ref = refs[-1]
        j = pl.program_id(0)
        for ref, f0, n in zip(refs[:-1], first, count):
            @pl.when((j >= f0) & (j < f0 + n))
            def _(ref=ref):
                o_ref[...] = jnp.dot(a_ref[...], ref[...], preferred_element_type=F32).astype(BF16)

    def spec(f0, n):
        return pl.BlockSpec((s, GW_TILE), lambda j: (0, jnp.clip(j - f0, 0, n - 1)))

    return pl.pallas_call(
        body, name="grad_w_in", grid=(sum(count),),
        in_specs=[_resident((ka, s))] + [spec(f0, n) for f0, n in zip(first, count)],
        out_specs=pl.BlockSpec((ka, GW_TILE), lambda j: (0, j)),
        out_shape=jax.ShapeDtypeStruct((ka, sum(widths)), BF16),
        compiler_params=_cparams(56, ("arbitrary",)),
    )(h1t, *pieces)


def _grad_matmul_blocks(a, b, *, ts, name):
    nblk = a.shape[0] if a.ndim == 3 else b.shape[0]
    s = a.shape[-2]
    ka, nb = a.shape[-1], b.shape[-1]
    ts = min(ts, s)
    nk = s // ts

    def body(a_ref, b_ref, o_ref, acc):
        k = pl.program_id(1)

        @pl.when(k == 0)
        def _():
            acc[...] = jnp.zeros_like(acc)

        av = a_ref[0] if a.ndim == 3 else a_ref[...]
        bv = b_ref[0] if b.ndim == 3 else b_ref[...]
        acc[...] += lax.dot_general(av, bv, TN, preferred_element_type=F32)

        @pl.when(k == nk - 1)
        def _():
            o_ref[0] = acc[...].astype(BF16)

    def spec(arr, width):
        if arr.ndim == 3:
            return pl.BlockSpec((1, ts, width), lambda j, k: (j, k, 0))
        return pl.BlockSpec((ts, width), lambda j, k: (k, 0))

    return pl.pallas_call(
        body, name=name, grid=(nblk, nk),
        in_specs=[spec(a, ka), spec(b, nb)],
        out_specs=pl.BlockSpec((1, ka, nb), lambda j, k: (j, 0, 0)),
        out_shape=jax.ShapeDtypeStruct((nblk, ka, nb), BF16),
        scratch_shapes=[pltpu.VMEM((ka, nb), F32)],
        compiler_params=_cparams(48, ("arbitrary", "arbitrary")),
    )(a, b)


def _mix_bwd(dy, w_out, o, cv, bcu, ga, gc, gsum, *, tm):
    s = dy.shape[0]

    def group_norm_bwd(dn_out, v, g, gs):
        r = lax.rsqrt(_split_dot(v * v, gs) * (1.0 / DH) + EPS)
        n = v * r
        dn = dn_out * g
        return r * (dn - n * (_split_dot(dn * n, gs) * (1.0 / DH))), dn_out * n

    def body(dy_ref, w_ref, o_ref, cv_ref, bcu_ref, ga_ref, gc_ref, gs_ref,
             do_ref, dl_ref, dcv_ref, db_ref, dga_ref, dgc_ref):
        @pl.when(pl.program_id(0) == 0)
        def _():
            dga_ref[...] = jnp.zeros_like(dga_ref)
            dgc_ref[...] = jnp.zeros_like(dgc_ref)

        dm = lax.dot_general(dy_ref[...], w_ref[...], NT, preferred_element_type=F32)
        ov = o_ref[...]
        do, dga = group_norm_bwd(dm[:, 0:AW], ov, ga_ref[...], gs_ref[...])
        dob = do.astype(BF16)
        do_ref[...] = dob
        dl_ref[...] = _split_dot(dob.astype(F32) * ov, gs_ref[...])
        dga_ref[...] += _fold8(dga)
        gate_b = bcu_ref[:, 0:CW]
        cv = cv_ref[...]
        dconv, dgc = group_norm_bwd(dm[:, AW:D], gate_b * cv, gc_ref[...], gs_ref[...])
        dgc_ref[...] += _fold8(dgc)
        dcv_ref[...] = dconv * gate_b
        db_ref[...] = (dconv * cv).astype(BF16)

    return pl.pallas_call(
        body, name="mix_bwd", grid=(s // tm,),
        in_specs=[_rows(tm, D), _resident((D, D)), _rows(tm, AW), _rows(tm, CW), _rows(tm, 3 * CW),
                  _full((1, AW)), _full((1, CW)), _full((CW, CW))],
        out_specs=[_rows(tm, AW), _rows(tm, AW), _rows(tm, CW), _rows(tm, CW),
                   _full((SUBLANES, AW)), _full((SUBLANES, CW))],
        out_shape=[jax.ShapeDtypeStruct((s, AW), BF16), jax.ShapeDtypeStruct((s, AW), F32),
                   jax.ShapeDtypeStruct((s, CW), F32), jax.ShapeDtypeStruct((s, CW), BF16),
                   jax.ShapeDtypeStruct((SUBLANES, AW), F32), jax.ShapeDtypeStruct((SUBLANES, CW), F32)],
        compiler_params=_cparams(48, ("arbitrary",)),
    )(dy, w_out, o, cv, bcu, ga, gc, gsum)


def _conv_bwd(dcv, db, bcu, cw8, *, tm):
    s = dcv.shape[0]
    nt = s // tm

    def body(dcv_ref, nxt_ref, db_ref, bcu_ref, halo_ref, cw_ref, dbcu_ref, dw_ref):
        i = pl.program_id(0)

        @pl.when(i == 0)
        def _():
            dw_ref[...] = jnp.zeros_like(dw_ref)

        z, z1, z2 = _conv_taps(bcu_ref, halo_ref, i == 0, tm)
        d = dcv_ref[...]
        dw_ref[0] += _fold8(d * z2)
        dw_ref[1] += _fold8(d * z1)
        dw_ref[2] += _fold8(d * z)
        nx = jnp.where(i == nt - 1, 0.0, nxt_ref[...])
        row = lax.broadcasted_iota(jnp.int32, (tm, CW), 0)
        d1 = jnp.where(row == tm - 1, nx[0:1, :], pltpu.roll(d, tm - 1, axis=0))
        d2 = jnp.where(row == tm - 2, nx[0:1, :], jnp.where(row == tm - 1, nx[1:2, :], pltpu.roll(d, tm - 2, axis=0)))
        dz = cw_ref[2:3, :] * d + cw_ref[1:2, :] * d1 + cw_ref[0:1, :] * d2
        dbcu_ref[:, 0:CW] = db_ref[...]
        dbcu_ref[:, CW:2 * CW] = (dz * bcu_ref[:, 2 * CW:3 * CW]).astype(BF16)
        dbcu_ref[:, 2 * CW:3 * CW] = (dz * bcu_ref[:, CW:2 * CW]).astype(BF16)

    return pl.pallas_call(
        body, name="conv_bwd", grid=(nt,),
        in_specs=[_rows(tm, CW),
                  pl.BlockSpec((SUBLANES, CW), lambda i: (jnp.minimum((i + 1) * (tm // SUBLANES), s // SUBLANES - 1), 0)),
                  _rows(tm, CW), _rows(tm, 3 * CW), _halo_before(tm, 3 * CW), _full((SUBLANES, CW))],
        out_specs=[_rows(tm, 3 * CW), _full((3, SUBLANES, CW))],
        out_shape=[jax.ShapeDtypeStruct((s, 3 * CW), BF16), jax.ShapeDtypeStruct((3, SUBLANES, CW), F32)],
        compiler_params=_cparams(48, ("arbitrary",)),
    )(dcv, dcv, db, bcu, bcu, cw8)


def _attn_bwd(qp, kp, v, do, lse, dl, mk, *, t):
    s = qp.shape[0]
    nq = s // t

    def body(q_ref, k_ref, v_ref, do_ref, lse_ref, dl_ref, mk_ref, dq_ref, dk_ref, dv_ref, dkx_ref, dq_acc):
        ki = pl.program_id(1)

        @pl.when(ki == 0)
        def _():
            dq_acc[...] = jnp.zeros_like(dq_acc)

        row = lax.broadcasted_iota(jnp.int32, (t, t), 0)
        col = lax.broadcasted_iota(jnp.int32, (t, t), 1)
        lane = lax.broadcasted_iota(jnp.int32, (t, 128), 1)

        def head_step(hh, qi, carry, masked):
            dk, dv, cs = carry
            off = pl.multiple_of(qi * t, t)
            rows = pl.ds(off, t)
            kh = k_ref[:, HP * hh:HP * (hh + 1)]
            q = q_ref[rows, HP * hh:HP * (hh + 1)]
            in_head = (lane >= DH * hh) & (lane < DH * (hh + 1))
            m_col = mk_ref[0, rows, DH * hh:DH * hh + 1]
            scale = jnp.exp(m_col - lse_ref[rows, DH * hh:DH * hh + 1])
            dom = jnp.where(in_head, do_ref[rows, :], jnp.zeros((), BF16))
            sc = lax.dot_general(q, kh, NT, preferred_element_type=F32) - m_col
            if masked:
                sc = jnp.where(col <= row, sc, -1e30)
            pt = jnp.exp(sc).astype(BF16)
            dp = lax.dot_general(dom, v_ref[...], NT, preferred_element_type=F32)
            ds32 = (pt.astype(F32) * scale) * (dp - dl_ref[rows, DH * hh:DH * hh + 1])
            ds = ds32.astype(BF16)
            cs = cs + _fold8(ds32)
            dv = dv + jnp.dot((dom.astype(F32) * scale).astype(BF16).T, pt, preferred_element_type=F32)
            dk = dk + jnp.dot(q.T, ds, preferred_element_type=F32)
            dq_acc[rows, HP * hh:HP * (hh + 1)] += jnp.dot(ds, kh, preferred_element_type=F32)
            return dk, dv, cs

        def step(qi, carry, masked):
            return tuple(head_step(hh, qi, carry[hh], masked) for hh in range(2))

        zero = (jnp.zeros((HP, t), F32), jnp.zeros((128, t), F32), jnp.zeros((SUBLANES, t), F32))
        carry = step(ki, (zero, zero), True)
        (dk0, dv0, cs0), (dk1, dv1, cs1) = lax.fori_loop(ki + 1, nq, functools.partial(step, masked=False), carry)
        dk_ref[:, 0:HP] = dk0.T.astype(BF16)
        dk_ref[:, HP:2 * HP] = dk1.T.astype(BF16)
        dv_ref[...] = (dv0 + dv1).T.astype(BF16)

        def as_column(cs):
            return lax.dot_general(cs, jnp.ones((SUBLANES, 128), F32), TN, precision=HIGHEST, preferred_element_type=F32)

        dkx_ref[...] = jnp.where(lane < DH, as_column(cs0), as_column(cs1))

        @pl.when(ki == nq - 1)
        def _():
            dq_ref[...] = dq_acc[...].astype(BF16)

    return pl.pallas_call(
        body, name="attn_bwd", grid=(H // 2, nq),
        in_specs=[pl.BlockSpec((s, 2 * HP), lambda p, i: (0, p)),
                  pl.BlockSpec((t, 2 * HP), lambda p, i: (i, p)),
                  pl.BlockSpec((t, 128), lambda p, i: (i, p)),
                  pl.BlockSpec((s, 128), lambda p, i: (0, p)),
                  pl.BlockSpec((s, 128), lambda p, i: (0, p)),
                  pl.BlockSpec((s, 128), lambda p, i: (0, p)),
                  pl.BlockSpec((1, s, 128), lambda p, i: (i, 0, p))],
        out_specs=[pl.BlockSpec((s, 2 * HP), lambda p, i: (0, p)),
                   pl.BlockSpec((t, 2 * HP), lambda p, i: (i, p)),
                   pl.BlockSpec((t, 128), lambda p, i: (i, p)),
                   pl.BlockSpec((t, 128), lambda p, i: (i, p))],
        out_shape=[jax.ShapeDtypeStruct((s, 1024), BF16), jax.ShapeDtypeStruct((s, 1024), BF16),
                   jax.ShapeDtypeStruct((s, AW), BF16), jax.ShapeDtypeStruct((s, AW), F32)],
        scratch_shapes=[pltpu.VMEM((s, 2 * HP), F32)],
        compiler_params=_cparams(56, ("arbitrary", "arbitrary")),
    )(qp, kp, v, do, lse, dl, mk)


def _forget_bwd(dkx, z, sel, *, tm):
    s = dkx.shape[0]
    nt = s // tm

    def body(dk_ref, z_ref, sel_ref, dfl_ref, dbf_ref, carry):
        @pl.when(pl.program_id(0) == 0)
        def _():
            carry[...] = jnp.zeros_like(carry)
            dbf_ref[...] = jnp.zeros_like(dbf_ref)

        dc = _split_dot(dk_ref[...], sel_ref[...])
        row = lax.broadcasted_iota(jnp.int32, (tm, tm), 0)
        col = lax.broadcasted_iota(jnp.int32, (tm, tm), 1)
        tri = (col >= row).astype(BF16)
        dlogf = _exact_dot01(tri, dc) + carry[0:1, :]
        carry[...] = jnp.broadcast_to(dlogf[0:1, :], carry.shape)
        dz = dlogf * (1.0 - jax.nn.sigmoid(z_ref[...]))
        dfl_ref[:, 0:128] = dz.astype(BF16)
        dfl_ref[:, 128:GW_TILE] = jnp.zeros((tm, GW_TILE - 128), BF16)
        dbf_ref[...] += _fold8(dz)

    rev = lambda i: (nt - 1 - i, 0)
    return pl.pallas_call(
        body, name="forget_bwd", grid=(nt,),
        in_specs=[pl.BlockSpec((tm, AW), rev), pl.BlockSpec((tm, 128), rev), _full((AW, 128))],
        out_specs=[pl.BlockSpec((tm, GW_TILE), rev), _full((SUBLANES, 128))],
        out_shape=[jax.ShapeDtypeStruct((s, GW_TILE), BF16), jax.ShapeDtypeStruct((SUBLANES, 128), F32)],
        scratch_shapes=[pltpu.VMEM((SUBLANES, 128), F32)],
        compiler_params=_cparams(48, ("arbitrary",)),
    )(dkx, z, sel)


def _in_proj_bwd(pieces, wp, x, g1, dx2, *, tm):
    s = x.shape[0]

    def body(q_ref, k_ref, v_ref, bcu_ref, f_ref, w_ref, x_ref, g_ref, dx2_ref, dx_ref, dg_ref):
        @pl.when(pl.program_id(0) == 0)
        def _():
            dg_ref[...] = jnp.zeros_like(dg_ref)

        dh = None
        for ref, (lo, hi) in zip((q_ref, k_ref, v_ref, bcu_ref, f_ref), PIECES):
            part = lax.dot_general(ref[...], w_ref[:, lo:hi], NT, preferred_element_type=F32)
            dh = part if dh is None else dh + part
        _, n, r = _rms_fwd(x_ref[...], g_ref[...])
        dxn, dg = _rms_bwd(dh, n, r, g_ref[...])
        dx_ref[...] = dx2_ref[...] + dxn
        dg_ref[...] += _fold8(dg)

    return pl.pallas_call(
        body, name="in_proj_bwd", grid=(s // tm,),
        in_specs=[_rows(tm, hi - lo) for lo, hi in PIECES] + [_resident((D, WP)), _rows(tm, D), _full((1, D)), _rows(tm, D)],
        out_specs=[_rows(tm, D), _full((SUBLANES, D))],
        out_shape=[jax.ShapeDtypeStruct((s, D), F32), jax.ShapeDtypeStruct((SUBLANES, D), F32)],
        compiler_params=_cparams(56, ("arbitrary",)),
    )(*pieces, wp, x, g1, dx2)


def _position():
    return lax.axis_index("x"), lax.axis_index("y"), lax.axis_index("c")


ANY = pl.BlockSpec(memory_space=pl.ANY)


def _all_gather(shards):
    n = len(shards)

    def body(*refs):
        x_refs, out_refs = refs[:n], refs[n:2 * n]
        send_sems, recv_sems, local_sems = refs[2 * n:]
        x, y, c = _position()
        me, sibling = (x, y, c), (x, y, 1 - c)
        chips = [(1 - x, y), (x, 1 - y), (1 - x, 1 - y)]

        def copy(a, k, block, to, own=False):
            slot = out_refs[a].at[4 * block[0] + 2 * block[1] + block[2]]
            return pltpu.make_async_remote_copy(
                src_ref=x_refs[a] if own else slot, dst_ref=slot,
                send_sem=send_sems.at[7 * a + k], recv_sem=recv_sems.at[7 * a + k], device_id=to, device_id_type=MESH_ID)

        mine = [pltpu.make_async_copy(x_refs[a], out_refs[a].at[4 * x + 2 * y + c], local_sems.at[a]) for a in range(n)]
        for cp in mine:
            cp.start()
        first = []
        for a in range(n):
            first.append(copy(a, 0, me, sibling, own=True))
            first += [copy(a, 1 + j, me, (*chip, c), own=True) for j, chip in enumerate(chips)]
        for cp in first:
            cp.start()
        passed = []
        for j, chip in enumerate(chips):
            for a in range(n):
                copy(a, 1 + j, (*chip, c), me).wait_recv()
                fwd = copy(a, 4 + j, (*chip, c), sibling)
                fwd.start()
                passed.append(fwd)
        for a in range(n):
            copy(a, 0, sibling, me).wait_recv()
            for j, chip in enumerate(chips):
                copy(a, 4 + j, (*chip, 1 - c), me).wait_recv()
        for cp in first + passed:
            cp.wait_send()
        for cp in mine:
            cp.wait()

    return pl.pallas_call(
        body, name="all_gather_weights",
        out_shape=[jax.ShapeDtypeStruct((NDEV,) + sh.shape, sh.dtype) for sh in shards],
        in_specs=[ANY] * n, out_specs=[ANY] * n,
        scratch_shapes=[pltpu.SemaphoreType.DMA((7 * n,)), pltpu.SemaphoreType.DMA((7 * n,)), pltpu.SemaphoreType.DMA((n,))],
    )(*shards)


def _pair_exchange(grads):
    n = len(grads)

    def body(*refs):
        g_refs, out_refs = refs[:n], refs[n:2 * n]
        send_sems, recv_sems = refs[2 * n:]
        x, y, c = _position()
        copies = [pltpu.make_async_remote_copy(
            src_ref=g_refs[a].at[:, pl.ds(1 - c, 1)], dst_ref=out_refs[a], send_sem=send_sems.at[a],
            recv_sem=recv_sems.at[a], device_id=(x, y, 1 - c), device_id_type=MESH_ID) for a in range(n)]
        for cp in copies:
            cp.start()
        for cp in copies:
            cp.wait()

    return pl.pallas_call(
        body, name="grad_pair_exchange",
        out_shape=[jax.ShapeDtypeStruct((4, 1) + g.shape[2:], g.dtype) for g in grads],
        in_specs=[ANY] * n, out_specs=[ANY] * n,
        scratch_shapes=[pltpu.SemaphoreType.DMA((n,)), pltpu.SemaphoreType.DMA((n,))],
    )(*grads)


def _pair_sum(g, got, idx, *, tr, name):
    r, c = g.shape[2:]

    def body(idx_ref, g_ref, got_ref, pb_ref, own_ref):
        p = g_ref[0, 0].astype(F32) + got_ref[0, 0].astype(F32)
        pb_ref[0] = p.astype(BF16)

        @pl.when(pl.program_id(1) == idx_ref[1])
        def _():
            own_ref[...] = p

    return pl.pallas_call(
        body, name=name,
        grid_spec=pltpu.PrefetchScalarGridSpec(
            num_scalar_prefetch=1, grid=(r // tr, 4),
            in_specs=[pl.BlockSpec((1, 1, tr, c), lambda i, j, idx: (j, idx[0], i, 0)),
                      pl.BlockSpec((1, 1, tr, c), lambda i, j, idx: (j, 0, i, 0))],
            out_specs=[pl.BlockSpec((1, tr, c), lambda i, j, idx: (j, i, 0)),
                       pl.BlockSpec((tr, c), lambda i, j, idx: (i, 0))]),
        out_shape=[jax.ShapeDtypeStruct((4, r, c), BF16), jax.ShapeDtypeStruct((r, c), F32)],
        compiler_params=_cparams(32, ("arbitrary", "arbitrary")),
    )(idx, g, got)


HBM = pl.BlockSpec(memory_space=pltpu.HBM)
SEM = pl.BlockSpec(memory_space=pltpu.SEMAPHORE)
DATAFLOW = pltpu.SideEffectType.DATAFLOW_SIDE_EFFECTING


PEERS = {"gather": NDEV - 1, "scatter": NDEV - 1, "chips": 3}


def _exchange_copies(src_refs, land_refs, send_sems, recv_sems, mode):
    x, y, c = _position()
    me, my_chip = 4 * x + 2 * y + c, 2 * x + y
    npeers = PEERS[mode]
    copies = []
    for a, (s_ref, l_ref) in enumerate(zip(src_refs, land_refs)):
        for k in range(npeers):
            if mode == "chips":
                px, py, pc = x ^ ((k + 1) >> 1), y ^ ((k + 1) & 1), c
                src, dst = s_ref.at[2 * px + py], l_ref.at[my_chip]
            else:
                px, py, pc = x ^ ((k + 1) >> 2), y ^ (((k + 1) >> 1) & 1), c ^ ((k + 1) & 1)
                src, dst = (s_ref.at[4 * px + 2 * py + pc] if mode == "scatter" else s_ref), l_ref.at[me]
            copies.append(pltpu.make_async_remote_copy(
                src_ref=src, dst_ref=dst, send_sem=send_sems.at[npeers * a + k], recv_sem=recv_sems.at[npeers * a + k],
                device_id=(px, py, pc), device_id_type=MESH_ID))
    return copies


def _exchange_start(srcs, lands, after, *, mode, name):
    n = len(srcs)
    nsem = PEERS[mode] * n

    def body(*refs):
        token = refs[-1]
        for cp in _exchange_copies(refs[:n], refs[n:2 * n], refs[2 * n + 1], refs[2 * n + 2], mode):
            cp.start()
        token[...] = jnp.zeros_like(token)

    arrays = list(srcs) + list(lands)
    outs = pl.pallas_call(
        body, name=name,
        out_shape=(pltpu.SemaphoreType.DMA((nsem,)), pltpu.SemaphoreType.DMA((nsem,)),
                   *[pltpu.HBM(a.shape, a.dtype) for a in arrays], jax.ShapeDtypeStruct((SUBLANES, LANES), F32)),
        in_specs=[HBM] * (2 * n) + [ANY],
        out_specs=(SEM, SEM, *[HBM] * (2 * n), pl.BlockSpec(memory_space=pltpu.VMEM)),
        input_output_aliases={i: 2 + i for i in range(2 * n)},
        compiler_params=pltpu.CompilerParams(has_side_effects=DATAFLOW),
    )(*[pltpu.with_memory_space_constraint(a, pltpu.HBM) for a in arrays], after)
    return outs[0], outs[1], outs[2:2 + n], outs[2 + n:2 + 2 * n], outs[-1]


def _exchange_wait(send_sems, recv_sems, srcs, lands, after, *, mode, name):
    n = len(srcs)

    def body(*refs):
        for cp in _exchange_copies(refs[:n], refs[n:2 * n], refs[2 * n], refs[2 * n + 1], mode):
            cp.wait_send()
            cp.wait_recv()

    arrays = list(srcs) + list(lands)
    outs = pl.pallas_call(
        body, name=name,
        out_shape=tuple(pltpu.HBM(a.shape, a.dtype) for a in arrays),
        in_specs=[HBM] * (2 * n) + [SEM, SEM, ANY],
        out_specs=tuple([HBM] * (2 * n)),
        input_output_aliases={i: i for i in range(2 * n)},
        compiler_params=pltpu.CompilerParams(has_side_effects=DATAFLOW),
    )(*arrays, send_sems, recv_sems, after)
    return outs[n:]


def _own_slot(value, me):
    return lax.dynamic_update_index_in_dim(lax.empty((NDEV,) + value.shape, value.dtype), value, me, 0)


def _small_all_reduce(parts):
    def body(gmp_ref, gmo_ref, gfp_ref, gfo_ref, ga_ref, gc_ref, dw_ref, bf_ref, loss_ref,
             out_ref, buf, send_sems, recv_sems):
        x, y, c = _position()
        me = 4 * x + 2 * y + c

        def colsum(v):
            return jnp.sum(v, axis=0, keepdims=True)

        loss = jnp.sum(colsum(loss_ref[...]), axis=1, keepdims=True) * (0.5 / D)
        rows = [colsum(gmp_ref[...]), colsum(gmo_ref[...]), colsum(gfp_ref[...]), colsum(gfo_ref[...]),
                jnp.concatenate([colsum(ga_ref[...]), colsum(gc_ref[...])], axis=1),
                jnp.concatenate([colsum(dw_ref[0]), colsum(dw_ref[1])], axis=1),
                jnp.concatenate([colsum(dw_ref[2]), colsum(bf_ref[...]), jnp.broadcast_to(loss, (1, 128)),
                                 jnp.zeros((1, 256), F32)], axis=1),
                jnp.zeros((1, D), F32)]
        buf[me] = jnp.concatenate(rows, axis=0)
        copies = []
        for mm in range(1, NDEV):
            peer = (x ^ (mm >> 2), y ^ ((mm >> 1) & 1), c ^ (mm & 1))
            copies.append(pltpu.make_async_remote_copy(
                src_ref=buf.at[me], dst_ref=buf.at[me], send_sem=send_sems.at[mm - 1], recv_sem=recv_sems.at[mm - 1],
                device_id=peer, device_id_type=MESH_ID))
        for cp in copies:
            cp.start()
        for cp in copies:
            cp.wait_recv()
        for cp in copies:
            cp.wait_send()
        acc = buf[0]
        for d in range(1, NDEV):
            acc = acc + buf[d]
        out_ref[...] = acc

    vm = pl.BlockSpec(memory_space=pltpu.VMEM)
    return pl.pallas_call(
        body, name="small_all_reduce",
        out_shape=jax.ShapeDtypeStruct((SUBLANES, D), F32),
        in_specs=[vm] * len(parts), out_specs=vm,
        scratch_shapes=[pltpu.VMEM((NDEV, SUBLANES, D), F32), pltpu.SemaphoreType.DMA((7,)), pltpu.SemaphoreType.DMA((7,))],
    )(*parts)


def _adam_update(w, g, m, v):
    nm = ADAM_B1 * m + (1.0 - ADAM_B1) * g
    nv = ADAM_B2 * v + (1.0 - ADAM_B2) * (g * g)
    m_hat = nm / (1.0 - ADAM_B1 ** ADAM_STEP)
    v_hat = nv / (1.0 - ADAM_B2 ** ADAM_STEP)
    return -ADAM_LR * (m_hat / (jnp.sqrt(v_hat) + ADAM_EPS) + ADAM_WD * w), nm, nv


def _adamw(w, g, m, v, *, tr, name):
    rows, cols = w.shape

    def body(w_ref, g_ref, m_ref, v_ref, d_ref, nm_ref, nv_ref):
        d_ref[...], nm_ref[...], nv_ref[...] = _adam_update(w_ref[...], g_ref[...], m_ref[...], v_ref[...])

    spec = pl.BlockSpec((tr, cols), lambda i: (i, 0))
    return pl.pallas_call(
        body, name=name, grid=(rows // tr,),
        in_specs=[spec] * 4, out_specs=[spec] * 3,
        out_shape=[jax.ShapeDtypeStruct((rows, cols), F32)] * 3,
        compiler_params=_cparams(32, ("arbitrary",)),
    )(w, g, m, v)


def _chip_sum_adamw(got, own, idx, w, m, v, *, tr, name):
    rows, cols = w.shape

    def body(idx_ref, got_ref, own_ref, w_ref, m_ref, v_ref, g_ref, d_ref, nm_ref, nv_ref):
        g = jnp.zeros((tr, cols), F32)
        for j in range(4):
            g = g + jnp.where(idx_ref[1] == j, own_ref[...], got_ref[j].astype(F32))
        g_ref[...] = g
        d_ref[...], nm_ref[...], nv_ref[...] = _adam_update(w_ref[...], g, m_ref[...], v_ref[...])

    spec = pl.BlockSpec((tr, cols), lambda i, idx: (i, 0))
    return pl.pallas_call(
        body, name=name,
        grid_spec=pltpu.PrefetchScalarGridSpec(
            num_scalar_prefetch=1, grid=(rows // tr,),
            in_specs=[pl.BlockSpec((4, tr, cols), lambda i, idx: (0, i, 0)), spec, spec, spec, spec],
            out_specs=[spec] * 4),
        out_shape=[jax.ShapeDtypeStruct((rows, cols), F32)] * 4,
        compiler_params=_cparams(32, ("arbitrary",)),
    )(idx, got, own, w, m, v)


def _device_sum_adamw(land, w, m, v, *, tr, name):
    rows, cols = w.shape

    def body(land_ref, w_ref, m_ref, v_ref, g_ref, d_ref, nm_ref, nv_ref):
        g = land_ref[0].astype(F32)
        for dev in range(1, NDEV):
            g = g + land_ref[dev].astype(F32)
        g_ref[...] = g
        d_ref[...], nm_ref[...], nv_ref[...] = _adam_update(w_ref[...], g, m_ref[...], v_ref[...])

    spec = pl.BlockSpec((tr, cols), lambda i: (i, 0))
    return pl.pallas_call(
        body, name=name, grid=(rows // tr,),
        in_specs=[pl.BlockSpec((NDEV, tr, cols), lambda i: (0, i, 0)), spec, spec, spec],
        out_specs=[spec] * 4,
        out_shape=[jax.ShapeDtypeStruct((rows, cols), F32)] * 4,
        compiler_params=_cparams(32, ("arbitrary",)),
    )(land, w, m, v)


def _placement_constants():
    j = jnp.arange(128)[:, None]
    lane = jnp.arange(1024)[None, :]
    head, sub = lane // HP, lane % HP
    piece, jh = j // H, j % H
    valid = (j < 3 * H) & (jh == head)
    pq = jnp.where(valid & (sub == DH + piece), 1.0, 0.0).astype(BF16)
    pk = jnp.where(valid & (sub == DH + 3 + piece), -1.0, 0.0).astype(BF16)
    oq = jnp.where((sub >= DH + 3) & (sub < DH + 6), 1.0, 0.0).astype(F32)
    ok = jnp.where((sub >= DH) & (sub < DH + 3), 1.0, 0.0).astype(F32)
    r = jnp.arange(AW)[:, None]
    cc = jnp.arange(128)[None, :]
    sel = jnp.where((r % DH == 3) & (r // DH == cc), -1.0, 0.0).astype(BF16)
    gi = jnp.arange(CW)
    gsum = (gi[:, None] // DH == gi[None, :] // DH).astype(BF16)
    return pq, pk, oq, ok, sel, gsum


def _local_step(xs, tgt, wp, late_weights, cw8, bfp, g_attn_out, g_conv_out,
                g_mix_pre, g_mix_post, g_ffn_pre, g_ffn_post, early_grads=None, last_grad=None):
    pq, pk, oq, ok, sel, gsum = _placement_constants()
    h1t, qp, kp, vv, bcu, zf = _in_proj(xs, g_mix_pre, wp, bfp, pq, pk, oq, ok, tm=512)
    o, lse, mk = _attn_fwd(qp, kp, vv, t=512)
    w_out_f, wgu, wd = late_weights(lse)
    merged, y, x2, cv, h2 = _mix_out(o, bcu, cw8, g_attn_out, g_conv_out, gsum, w_out_f, xs, g_mix_post, g_ffn_pre, tm=512)
    gate, up, act, dx3, dff, loss_p, dg_ffn_post = _ffn_fwd_loss(h2, wgu, wd, x2, tgt, g_ffn_post, tm=512)

    dgu, dx2, dy, dg_ffn_pre, dg_mix_post = _ffn_bwd(dff, wd, gate, up, wgu, x2, g_ffn_pre, dx3, y, g_mix_post, tm=256)
    dw_down = _grad_matmul_blocks(act, dff, ts=4096, name="grad_w_down")
    dw_gu = _grad_matmul_blocks(dgu.reshape(NDEV, -1, FB), h2, ts=4096, name="grad_w_gate_up")
    dw_out = _grad_matmul(merged, dy, ta=1024, tb=1024, ts=2048, name="grad_w_out")
    token = early_grads(dw_out, dw_gu, dw_down) if early_grads is not None else None
    ga = g_attn_out if token is None else g_attn_out + token[0:1, 0:1]
    do, dl, dcv, db, dg_attn, dg_conv = _mix_bwd(dy, w_out_f, o, cv, bcu, ga, g_conv_out, gsum, tm=512)
    dbcu, dtaps = _conv_bwd(dcv, db, bcu, cw8, tm=512)
    dqp, dkp, dv, dkx = _attn_bwd(qp, kp, vv, do, lse, dl, mk, t=512)
    dfl, dbf = _forget_bwd(dkx, zf, sel, tm=512)
    pieces = (dqp, dkp, dv, dbcu, dfl)
    dwp = _grad_w_in(h1t, pieces)
    token = last_grad(dwp) if last_grad is not None else None
    g1 = g_mix_pre if token is None else g_mix_pre + token[0:1, 0:1]
    grad_x, dg_mix_pre = _in_proj_bwd(pieces, wp, xs, g1, dx2, tm=512)
    return (grad_x, dwp, dw_out, dw_gu, dw_down, dg_mix_pre, dg_mix_post, dg_ffn_pre, dg_ffn_post, dg_attn, dg_conv,
            dtaps, dbf, loss_p)


BIG_TILES = {"w_in": 256, "w_out": 128, "w_gate_up": 176, "w_down": 176}


def kernel(x, w_in, b_forget, conv_w, g_attn_out, g_conv_out, w_out, g_mix_pre, g_mix_post, w_gate_up, w_down, g_ffn_pre, g_ffn_post, loss_target, m_w_in, m_b_forget, m_conv_w, m_g_attn_out, m_g_conv_out, m_w_out, m_g_mix_pre, m_g_mix_post, m_w_gate_up, m_w_down, m_g_ffn_pre, m_g_ffn_post, v_w_in, v_b_forget, v_conv_w, v_g_attn_out, v_g_conv_out, v_w_out, v_g_mix_pre, v_g_mix_post, v_w_gate_up, v_w_down, v_g_ffn_pre, v_g_ffn_post):
    xc, yc, cc = _position()
    my_chip = 2 * xc + yc
    me = 2 * my_chip + cc
    idx = jnp.stack([cc, my_chip]).astype(jnp.int32)
    tables = _in_layout_tables()
    pad_in = lambda a: jnp.pad(a, ((0, 0), (0, IN_PAD - IN_COLS)))

    g_in, g_taps = _all_gather([pad_in(w_in[0]).astype(BF16), conv_w[0]])
    wp = _assemble_w_in(g_in, tables, tr=256)
    cw8 = jnp.pad(g_taps.transpose(1, 0, 2).reshape(3, CW), ((0, SUBLANES - 3), (0, 0)))

    late = [w_out[0].astype(BF16), w_gate_up[0].astype(BF16), w_down[0].astype(BF16)]
    ssem, rsem, late_thru, land_thru, token = _exchange_start(
        late, [_own_slot(s, me) for s in late], g_in, mode="gather", name="gather_late_start")
    bfp = jnp.pad(b_forget, ((0, 0), (0, 128 - H))) + token[0:1, :]

    def late_weights(after):
        l_out, l_gu, l_down = _exchange_wait(ssem, rsem, late_thru, land_thru, after, mode="gather", name="gather_late_wait")
        return l_out.reshape(D, D), l_gu.reshape(2, 4, D, FB), l_down.reshape(4, FB, D)

    early = {}

    def early_grads(dw_out, dw_gu, dw_down):
        srcs = [dw_out.reshape(NDEV, D // NDEV, D), dw_gu, dw_down.reshape(NDEV, DFF // NDEV, D)]
        lands = [_own_slot(lax.dynamic_index_in_dim(s, me, 0, keepdims=False), me) for s in srcs]
        early["handles"] = _exchange_start(srcs, lands, dw_out, mode="scatter", name="scatter_early_start")
        return early["handles"][4]

    last = {}

    def last_grad(dwp):
        g_w_in = _disassemble_w_in(dwp, tables, tr=256).reshape(4, 2, D, IN_PAD)
        (from_sibling,) = _pair_exchange([g_w_in])
        pair_b, last["own"] = _pair_sum(g_w_in, from_sibling, idx, tr=BIG_TILES["w_in"], name="grad_pair_sum_w_in")
        land = lax.dynamic_update_index_in_dim(lax.empty(pair_b.shape, pair_b.dtype),
                                               lax.dynamic_index_in_dim(pair_b, my_chip, 0, keepdims=False), my_chip, 0)
        last["handles"] = _exchange_start([pair_b], [land], last["own"], mode="chips", name="chips_w_in_start")
        return last["handles"][4]

    (grad_x, dwp, dw_out, dw_gu, dw_down, dg_mix_pre, dg_mix_post, dg_ffn_pre, dg_ffn_post, dg_attn, dg_conv,
     dtaps, dbf, loss_p) = _local_step(x[0], loss_target[0], wp, late_weights, cw8, bfp, g_attn_out, g_conv_out,
                                        g_mix_pre, g_mix_post, g_ffn_pre, g_ffn_post, early_grads, last_grad)

    e_ssem, e_rsem, e_srcs, e_lands, _ = early["handles"]
    land_out, land_gu, land_down = _exchange_wait(e_ssem, e_rsem, e_srcs, e_lands, dg_mix_pre, mode="scatter",
                                                  name="scatter_early_wait")
    res = {}
    big = {"w_out": (land_out, w_out[0], m_w_out[0], v_w_out[0]),
           "w_gate_up": (land_gu, w_gate_up[0].T, m_w_gate_up[0].T, v_w_gate_up[0].T),
           "w_down": (land_down, w_down[0], m_w_down[0], v_w_down[0])}
    for name, (land, w, m, v) in big.items():
        outs = _device_sum_adamw(land, w, m, v, tr=BIG_TILES[name], name="adamw_" + name)
        res[name] = [(o.T if name == "w_gate_up" else o)[None] for o in outs]
    c_ssem, c_rsem, c_srcs, c_lands, _ = last["handles"]
    after = sum(res[n][1][0, :SUBLANES, :LANES] for n in big)
    (from_chips,) = _exchange_wait(c_ssem, c_rsem, c_srcs, c_lands, after, mode="chips", name="chips_w_in_wait")
    outs = _chip_sum_adamw(from_chips, last["own"], idx, pad_in(w_in[0]), pad_in(m_w_in[0]), pad_in(v_w_in[0]),
                           tr=BIG_TILES["w_in"], name="adamw_w_in")
    res["w_in"] = [o[:, :IN_COLS][None] for o in outs]

    small = _small_all_reduce([dg_mix_pre, dg_mix_post, dg_ffn_pre, dg_ffn_post, dg_attn, dg_conv, dtaps, dbf, loss_p])
    taps_full = jnp.concatenate([small[5:6, :CW], small[5:6, CW:], small[6:7, :CW]], axis=0)
    small_grads = {
        "b_forget": small[6:7, CW:CW + H], "conv_w": lax.dynamic_slice(taps_full, (0, me * 64), (3, 64)),
        "g_attn_out": small[4:5, :AW], "g_conv_out": small[4:5, AW:], "g_mix_pre": small[0:1], "g_mix_post": small[1:2],
        "g_ffn_pre": small[2:3], "g_ffn_post": small[3:4]}
    loss = small[6, CW + 128]
    smalls = {"b_forget": (b_forget, m_b_forget, v_b_forget), "conv_w": (conv_w[0], m_conv_w[0], v_conv_w[0]),
              "g_attn_out": (g_attn_out, m_g_attn_out, v_g_attn_out), "g_conv_out": (g_conv_out, m_g_conv_out, v_g_conv_out),
              "g_mix_pre": (g_mix_pre, m_g_mix_pre, v_g_mix_pre), "g_mix_post": (g_mix_post, m_g_mix_post, v_g_mix_post),
              "g_ffn_pre": (g_ffn_pre, m_g_ffn_pre, v_g_ffn_pre), "g_ffn_post": (g_ffn_post, m_g_ffn_post, v_g_ffn_post)}
    for name, (w, m, v) in smalls.items():
        g = small_grads[name]
        outs = [g] + list(_adamw(w, g, m, v, tr=w.shape[0], name="adamw_" + name))
        res[name] = [o[None] for o in outs] if name == "conv_w" else outs

    order = ["w_in", "b_forget", "conv_w", "g_attn_out", "g_conv_out", "w_out", "g_mix_pre", "g_mix_post",
             "w_gate_up", "w_down", "g_ffn_pre", "g_ffn_post"]
    outs = [loss, grad_x[None]]
    for k in range(4):
        outs += [res[n][k] for n in order]
    return tuple(outs)
```

```python
import functools

import numpy as np

import jax
import jax.numpy as jnp
from jax import lax
from jax.experimental import pallas as pl
from jax.experimental.pallas import tpu as pltpu

F32 = jnp.float32
BF16 = jnp.bfloat16
HIGHEST = lax.Precision.HIGHEST
MESH_ID = pl.DeviceIdType.MESH

D = 1024
H = 8
DH = 64
AW = 512
CW = 512
DFF = 2816
FB = DFF // 4
HP = 128
OFF_Q, OFF_K, OFF_V, OFF_BCU, OFF_F = 0, 512, 1024, 1536, 3072
WP = OFF_F + 128
PIECES = ((OFF_Q, OFF_K), (OFF_K, OFF_V), (OFF_V, OFF_BCU), (OFF_BCU, OFF_F), (OFF_F, WP))
EPS = 1e-6
NDEV = 8
LANES = 128
SUBLANES = 8
IN_COLS = 385
IN_PAD = 512
WIN = 640
ADAM_LR, ADAM_B1, ADAM_B2, ADAM_EPS, ADAM_WD, ADAM_STEP = 0.001, 0.9, 0.999, 1e-08, 0.01, 10

NT = (((1,), (1,)), ((), ()))
TN = (((0,), (0,)), ((), ()))


def _cparams(vmem_mb=None, sem=None):
    kw = {}
    if vmem_mb is not None:
        kw["vmem_limit_bytes"] = vmem_mb << 20
    if sem is not None:
        kw["dimension_semantics"] = sem
    return pltpu.CompilerParams(**kw)


def _full(shape):
    return pl.BlockSpec(shape, lambda *_: (0,) * len(shape))


def _resident(shape):
    return pl.BlockSpec(shape, lambda *_: (0,) * len(shape), pipeline_mode=pl.Buffered(1))


def _rows(tm, width):
    return pl.BlockSpec((tm, width), lambda i: (i, 0))


def _fold8(v):
    r, w = v.shape
    return jnp.sum(v.reshape(r // SUBLANES, SUBLANES, w), axis=0)


def _split_dot(v, m01):
    hi = v.astype(BF16)
    lo = (v - hi.astype(F32)).astype(BF16)
    return (jnp.dot(hi, m01, preferred_element_type=F32)
            + jnp.dot(lo, m01, preferred_element_type=F32))


def _exact_dot01(m01, v):
    p1 = v.astype(BF16)
    r1 = v - p1.astype(F32)
    p2 = r1.astype(BF16)
    p3 = (r1 - p2.astype(F32)).astype(BF16)
    return (jnp.dot(m01, p1, preferred_element_type=F32) + jnp.dot(m01, p2, preferred_element_type=F32)
            + jnp.dot(m01, p3, preferred_element_type=F32))


def _rms_fwd(v, g):
    r = lax.rsqrt(jnp.mean(v * v, axis=-1, keepdims=True) + EPS)
    n = v * r
    return n * g, n, r


def _rms_bwd(do, n, r, g):
    dn = do * g
    return r * (dn - n * jnp.mean(dn * n, axis=-1, keepdims=True)), do * n


def _padded_column(n):
    if n < AW:
        return OFF_Q + n, 0.125
    if n < 3 * AW:
        return n, 1.0
    if n < 3 * AW + H:
        return OFF_F + n - 3 * AW, 1.0
    return OFF_BCU + n - 3 * AW - H, 1.0


def _in_layout_tables():
    dest = -np.ones((IN_PAD, LANES), np.int32)
    dest_f = -np.ones((IN_PAD, LANES), np.int32)
    scale = np.zeros((IN_PAD, LANES), np.float32)
    starts = []
    for k in range(NDEV):
        cols = [_padded_column(IN_COLS * k + j) for j in range(IN_COLS)]
        main = [c for c, _ in cols if c < OFF_F]
        ws = min((min(main) // LANES) * LANES, OFF_F - WIN)
        assert ws <= min(main) and max(main) < ws + WIN
        starts.append(ws)
        for j, (c, sc) in enumerate(cols):
            scale[j, k] = sc
            if c < OFF_F:
                dest[j, k] = c - ws
            else:
                dest_f[j, k] = c - OFF_F
    f_shards = tuple(k for k in range(NDEV) if (dest_f[:, k] >= 0).any())
    return tuple(starts), f_shards, jnp.asarray(dest), jnp.asarray(dest_f), jnp.asarray(scale)


def _perm(dest_ref, scale_ref, k, width):
    lane = lax.broadcasted_iota(jnp.int32, (IN_PAD, width), 1)
    return jnp.where(dest_ref[:, k:k + 1] == lane, scale_ref[:, k:k + 1], 0.0).astype(BF16)


def _assemble_w_in(blocks, tables, *, tr):
    starts, f_shards, dest, dest_f, scale = tables

    def body(b_ref, dest_ref, destf_ref, scale_ref, o_ref):
        o_ref[...] = jnp.zeros_like(o_ref)
        for k in range(NDEV):
            b = b_ref[k]
            ws = starts[k]
            part = jnp.dot(b, _perm(dest_ref, scale_ref, k, WIN), preferred_element_type=F32)
            o_ref[:, ws:ws + WIN] += part.astype(BF16)
            if k in f_shards:
                part = jnp.dot(b, _perm(destf_ref, scale_ref, k, 128), preferred_element_type=F32)
                o_ref[:, OFF_F:WP] += part.astype(BF16)

    tab = _full((IN_PAD, LANES))
    return pl.pallas_call(
        body, name="assemble_w_in", grid=(D // tr,),
        in_specs=[pl.BlockSpec((NDEV, tr, IN_PAD), lambda i: (0, i, 0)), tab, tab, tab],
        out_specs=_rows(tr, WP),
        out_shape=jax.ShapeDtypeStruct((D, WP), BF16),
        compiler_params=_cparams(48, ("arbitrary",)),
    )(blocks, dest, dest_f, scale)


def _disassemble_w_in(dwp, tables, *, tr):
    starts, f_shards, dest, dest_f, scale = tables
    width = dwp.shape[1]

    def body(g_ref, dest_ref, destf_ref, scale_ref, o_ref):
        for k in range(NDEV):
            ws = starts[k]
            acc = lax.dot_general(g_ref[:, ws:ws + WIN], _perm(dest_ref, scale_ref, k, WIN), NT, preferred_element_type=F32)
            if k in f_shards:
                acc = acc + lax.dot_general(g_ref[:, OFF_F:WP], _perm(destf_ref, scale_ref, k, 128), NT,
                                            preferred_element_type=F32)
            o_ref[k] = acc.astype(BF16)

    tab = _full((IN_PAD, LANES))
    return pl.pallas_call(
        body, name="disassemble_w_in", grid=(D // tr,),
        in_specs=[_rows(tr, width), tab, tab, tab],
        out_specs=pl.BlockSpec((NDEV, tr, IN_PAD), lambda i: (0, i, 0)),
        out_shape=jax.ShapeDtypeStruct((NDEV, D, IN_PAD), BF16),
        compiler_params=_cparams(48, ("arbitrary",)),
    )(dwp, dest, dest_f, scale)


def _in_proj(x, g1, wp, bfp, pq, pk, oq, ok, *, tm):
    s = x.shape[0]

    def body(x_ref, g_ref, w_ref, bf_ref, pq_ref, pk_ref, oq_ref, ok_ref,
             ht_ref, qp_ref, kp_ref, v_ref, bcu_ref, z_ref, carry):
        @pl.when(pl.program_id(0) == 0)
        def _():
            carry[...] = jnp.zeros_like(carry)

        h = _rms_fwd(x_ref[...], g_ref[...])[0].astype(BF16)
        ht_ref[...] = h.T
        z = jnp.dot(h, w_ref[:, OFF_F:WP], preferred_element_type=F32) + bf_ref[...]
        z_ref[...] = z
        lane = lax.broadcasted_iota(jnp.int32, (tm, 128), 1)
        logf = jnp.where(lane < H, jnp.minimum(z, 0.0) - jnp.log(1.0 + jnp.exp(-jnp.abs(z))), 0.0)
        row = lax.broadcasted_iota(jnp.int32, (tm, tm), 0)
        col = lax.broadcasted_iota(jnp.int32, (tm, tm), 1)
        tri = (col <= row).astype(BF16)
        c = _exact_dot01(tri, logf) + carry[0:1, :]
        carry[...] = jnp.broadcast_to(c[tm - 1:tm, :], carry.shape)
        c1 = c.astype(BF16).astype(F32)
        r1 = c - c1
        c2 = r1.astype(BF16).astype(F32)
        c3 = (r1 - c2).astype(BF16).astype(F32)
        zc = (c1 + pltpu.roll(c2, 8, axis=1) + pltpu.roll(c3, 16, axis=1)).astype(BF16)

        def pad_heads(v):
            blocks = []
            for pair in range(H // 2):
                two = v[:, 128 * pair:128 * (pair + 1)]
                blocks.append(jnp.where(lane < DH, two, 0.0))
                blocks.append(jnp.where(lane < DH, pltpu.roll(two, DH, axis=1), 0.0))
            return jnp.concatenate(blocks, axis=1)

        q = jnp.dot(h, w_ref[:, OFF_Q:OFF_K], preferred_element_type=F32)
        qp_ref[...] = (pad_heads(q) + jnp.dot(zc, pq_ref[...], preferred_element_type=F32) + oq_ref[...]).astype(BF16)
        k = jnp.dot(h, w_ref[:, OFF_K:OFF_V], preferred_element_type=F32)
        kp_ref[...] = (pad_heads(k) + jnp.dot(zc, pk_ref[...], preferred_element_type=F32) + ok_ref[...]).astype(BF16)
        v_ref[...] = jnp.dot(h, w_ref[:, OFF_V:OFF_BCU], preferred_element_type=F32).astype(BF16)
        bcu_ref[...] = jnp.dot(h, w_ref[:, OFF_BCU:OFF_F], preferred_element_type=F32)

    return pl.pallas_call(
        body, name="in_proj", grid=(s // tm,),
        in_specs=[_rows(tm, D), _full((1, D)), _resident((D, WP)), _full((1, 128)),
                  _full((128, 1024)), _full((128, 1024)), _full((1, 1024)), _full((1, 1024))],
        out_specs=[pl.BlockSpec((D, tm), lambda i: (0, i)), _rows(tm, 1024), _rows(tm, 1024), _rows(tm, AW),
                   _rows(tm, 3 * CW), _rows(tm, 128)],
        out_shape=[jax.ShapeDtypeStruct((D, s), BF16), jax.ShapeDtypeStruct((s, 1024), BF16),
                   jax.ShapeDtypeStruct((s, 1024), BF16), jax.ShapeDtypeStruct((s, AW), BF16),
                   jax.ShapeDtypeStruct((s, 3 * CW), F32), jax.ShapeDtypeStruct((s, 128), F32)],
        scratch_shapes=[pltpu.VMEM((SUBLANES, 128), F32)],
        compiler_params=_cparams(56, ("arbitrary",)),
    )(x, g1, wp, bfp, pq, pk, oq, ok)


def _attn_fwd(qp, kp, v, *, t):
    s = qp.shape[0]
    nq = s // t

    def body(q_ref, k_ref, v_ref, o_ref, lse_ref, mk_ref):
        qi = pl.program_id(1)
        row = lax.broadcasted_iota(jnp.int32, (t, t), 0)
        col = lax.broadcasted_iota(jnp.int32, (t, t), 1)
        lane = lax.broadcasted_iota(jnp.int32, (t, 128), 1)

        def head_step(hh, ki, carry, masked):
            m, l, acc = carry
            off = pl.multiple_of(ki * t, t)
            q = q_ref[:, HP * hh:HP * (hh + 1)]
            k = k_ref[pl.ds(off, t), HP * hh:HP * (hh + 1)]
            sc = lax.dot_general(q, k, NT, preferred_element_type=F32)
            if masked:
                sc = jnp.where(col <= row, sc, -1e30)
            mn = jnp.maximum(m, jnp.max(sc, axis=-1, keepdims=True))
            p = jnp.exp(sc - mn)
            a = jnp.exp(m - mn)
            l = a * l + jnp.sum(p, axis=-1, keepdims=True)
            acc = a * acc + jnp.dot(p.astype(BF16), v_ref[pl.ds(off, t), :], preferred_element_type=F32)
            return mn, l, acc

        def step(ki, carry, masked):
            new = tuple(head_step(hh, ki, carry[hh], masked) for hh in range(2))
            mk_ref[ki] = jnp.where(lane < DH, jnp.broadcast_to(new[0][0], (t, 128)), jnp.broadcast_to(new[1][0], (t, 128)))
            return new

        init = (jnp.full((t, 1), -1e30, F32), jnp.zeros((t, 1), F32), jnp.zeros((t, 128), F32))
        carry = lax.fori_loop(0, qi, functools.partial(step, masked=False), (init, init))
        (m0, l0, acc0), (m1, l1, acc1) = step(qi, carry, True)
        o_ref[...] = jnp.where(lane < DH, acc0 / l0, acc1 / l1)
        lse_ref[...] = jnp.where(lane < DH, jnp.broadcast_to(m0 + jnp.log(l0), (t, 128)),
                                 jnp.broadcast_to(m1 + jnp.log(l1), (t, 128)))

    return pl.pallas_call(
        body, name="attn_fwd", grid=(H // 2, nq),
        in_specs=[pl.BlockSpec((t, 2 * HP), lambda p, i: (i, p)),
                  pl.BlockSpec((s, 2 * HP), lambda p, i: (0, p)),
                  pl.BlockSpec((s, 128), lambda p, i: (0, p))],
        out_specs=[pl.BlockSpec((t, 128), lambda p, i: (i, p)), pl.BlockSpec((t, 128), lambda p, i: (i, p)),
                   pl.BlockSpec((nq, t, 128), lambda p, i: (0, i, p))],
        out_shape=[jax.ShapeDtypeStruct((s, AW), F32), jax.ShapeDtypeStruct((s, AW), F32),
                   jax.ShapeDtypeStruct((nq, s, AW), F32)],
        compiler_params=_cparams(48, ("arbitrary", "arbitrary")),
    )(qp, kp, v)


def _conv_taps(bcu_ref, halo_ref, first, tm):
    z = bcu_ref[:, CW:2 * CW] * bcu_ref[:, 2 * CW:3 * CW]
    zh = jnp.where(first, 0.0, halo_ref[:, CW:2 * CW] * halo_ref[:, 2 * CW:3 * CW])
    row = lax.broadcasted_iota(jnp.int32, (tm, CW), 0)
    z1 = jnp.where(row == 0, zh[7:8, :], pltpu.roll(z, 1, axis=0))
    z2 = jnp.where(row == 0, zh[6:7, :], jnp.where(row == 1, zh[7:8, :], pltpu.roll(z, 2, axis=0)))
    return z, z1, z2


def _halo_before(tm, width):
    return pl.BlockSpec((SUBLANES, width), lambda i: (jnp.maximum(i * (tm // SUBLANES) - 1, 0), 0))


def _mix_out(o, bcu, cw8, ga, gc, gsum, w_out, x, g_post, g_ffn_pre, *, tm):
    s = x.shape[0]

    def body(o_ref, bcu_ref, halo_ref, cw_ref, ga_ref, gc_ref, gs_ref, w_ref, x_ref, g_ref, gf_ref,
             merged_ref, y_ref, x2_ref, cv_ref, h2_ref):
        z, z1, z2 = _conv_taps(bcu_ref, halo_ref, pl.program_id(0) == 0, tm)
        cv = cw_ref[0:1, :] * z2 + cw_ref[1:2, :] * z1 + cw_ref[2:3, :] * z
        cv_ref[...] = cv
        conv = bcu_ref[:, 0:CW] * cv
        ov = o_ref[...]
        ra = lax.rsqrt(_split_dot(ov * ov, gs_ref[...]) * (1.0 / DH) + EPS)
        rc = lax.rsqrt(_split_dot(conv * conv, gs_ref[...]) * (1.0 / DH) + EPS)
        merged = jnp.concatenate([ov * ra * ga_ref[...], conv * rc * gc_ref[...]], axis=1).astype(BF16)
        merged_ref[...] = merged
        y = jnp.dot(merged, w_ref[...], preferred_element_type=F32)
        y_ref[...] = y
        x2 = x_ref[...] + _rms_fwd(y, g_ref[...])[0]
        x2_ref[...] = x2
        h2_ref[...] = _rms_fwd(x2, gf_ref[...])[0].astype(BF16)

    return pl.pallas_call(
        body, name="mix_out", grid=(s // tm,),
        in_specs=[_rows(tm, AW), _rows(tm, 3 * CW), _halo_before(tm, 3 * CW), _full((SUBLANES, CW)),
                  _full((1, AW)), _full((1, CW)), _full((CW, CW)), _resident((D, D)), _rows(tm, D), _full((1, D)),
                  _full((1, D))],
        out_specs=[_rows(tm, D), _rows(tm, D), _rows(tm, D), _rows(tm, CW), _rows(tm, D)],
        out_shape=[jax.ShapeDtypeStruct((s, D), BF16), jax.ShapeDtypeStruct((s, D), F32),
                   jax.ShapeDtypeStruct((s, D), F32), jax.ShapeDtypeStruct((s, CW), F32),
                   jax.ShapeDtypeStruct((s, D), BF16)],
        compiler_params=_cparams(48, ("arbitrary",)),
    )(o, bcu, bcu, cw8, ga, gc, gsum, w_out, x, g_post, g_ffn_pre)


def _ffn_fwd_loss(h2, wgu, wd, x2, target, g_post, *, tm):
    s = x2.shape[0]

    def body(h_ref, w_ref, wd_ref, x2_ref, t_ref, g_ref,
             gate_ref, up_ref, a_ref, dx3_ref, dff_ref, loss_ref, dg_ref):
        @pl.when(pl.program_id(0) == 0)
        def _():
            loss_ref[...] = jnp.zeros_like(loss_ref)
            dg_ref[...] = jnp.zeros_like(dg_ref)

        h = h_ref[...]
        ff = None
        for j in range(4):
            gate = jnp.dot(h, w_ref[0, j], preferred_element_type=F32)
            up = jnp.dot(h, w_ref[1, j], preferred_element_type=F32)
            gate_ref[j] = gate.astype(BF16)
            up_ref[j] = up.astype(BF16)
            act = (gate * jax.nn.sigmoid(gate) * up).astype(BF16)
            a_ref[j] = act
            part = jnp.dot(act, wd_ref[j], preferred_element_type=F32)
            ff = part if ff is None else ff + part
        out, n, r = _rms_fwd(ff, g_ref[...])
        e = x2_ref[...] + out - t_ref[...]
        loss_ref[...] += _fold8(e * e)
        dx3 = e * (1.0 / D)
        dx3_ref[...] = dx3
        dff, dg = _rms_bwd(dx3, n, r, g_ref[...])
        dff_ref[...] = dff.astype(BF16)
        dg_ref[...] += _fold8(dg)

    blk4 = pl.BlockSpec((4, tm, FB), lambda i: (0, i, 0))
    return pl.pallas_call(
        body, name="ffn_fwd_loss", grid=(s // tm,),
        in_specs=[_rows(tm, D), _resident((2, 4, D, FB)), _resident((4, FB, D)), _rows(tm, D), _rows(tm, D), _full((1, D))],
        out_specs=[blk4, blk4, blk4, _rows(tm, D), _rows(tm, D), _full((SUBLANES, D)), _full((SUBLANES, D))],
        out_shape=[jax.ShapeDtypeStruct((4, s, FB), BF16)] * 3
        + [jax.ShapeDtypeStruct((s, D), F32), jax.ShapeDtypeStruct((s, D), BF16),
           jax.ShapeDtypeStruct((SUBLANES, D), F32), jax.ShapeDtypeStruct((SUBLANES, D), F32)],
        compiler_params=_cparams(56, ("arbitrary",)),
    )(h2, wgu, wd, x2, target, g_post)


def _ffn_bwd(dff, wd, gate, up, wgu, x2, g_pre, dx3, y, g_post, *, tm):
    s = x2.shape[0]

    def body(dff_ref, wd_ref, gate_ref, up_ref, w_ref, x2_ref, gpre_ref, dx3_ref, y_ref, gpost_ref,
             dgu_ref, dx2_ref, dy_ref, dgpre_ref, dgpost_ref):
        @pl.when(pl.program_id(0) == 0)
        def _():
            dgpre_ref[...] = jnp.zeros_like(dgpre_ref)
            dgpost_ref[...] = jnp.zeros_like(dgpost_ref)

        dff = dff_ref[...]
        dh2 = None
        for j in range(4):
            da = lax.dot_general(dff, wd_ref[j], NT, preferred_element_type=F32)
            g = gate_ref[j].astype(F32)
            sg = jax.nn.sigmoid(g)
            dgate = (da * up_ref[j].astype(F32) * (sg * (1.0 + g * (1.0 - sg)))).astype(BF16)
            dup = (da * (g * sg)).astype(BF16)
            dgu_ref[0, j] = dgate
            dgu_ref[1, j] = dup
            part = (lax.dot_general(dgate, w_ref[0, j], NT, preferred_element_type=F32)
                    + lax.dot_general(dup, w_ref[1, j], NT, preferred_element_type=F32))
            dh2 = part if dh2 is None else dh2 + part
        _, n2, r2 = _rms_fwd(x2_ref[...], gpre_ref[...])
        dxn, dg = _rms_bwd(dh2, n2, r2, gpre_ref[...])
        dgpre_ref[...] += _fold8(dg)
        dx2 = dx3_ref[...] + dxn
        dx2_ref[...] = dx2
        _, ny, ry = _rms_fwd(y_ref[...], gpost_ref[...])
        dy, dg2 = _rms_bwd(dx2, ny, ry, gpost_ref[...])
        dy_ref[...] = dy.astype(BF16)
        dgpost_ref[...] += _fold8(dg2)

    blk4 = pl.BlockSpec((4, tm, FB), lambda i: (0, i, 0))
    return pl.pallas_call(
        body, name="ffn_bwd", grid=(s // tm,),
        in_specs=[_rows(tm, D), _resident((4, FB, D)), blk4, blk4, _resident((2, 4, D, FB)), _rows(tm, D), _full((1, D)),
                  _rows(tm, D), _rows(tm, D), _full((1, D))],
        out_specs=[pl.BlockSpec((2, 4, tm, FB), lambda i: (0, 0, i, 0)), _rows(tm, D), _rows(tm, D),
                   _full((SUBLANES, D)), _full((SUBLANES, D))],
        out_shape=[jax.ShapeDtypeStruct((2, 4, s, FB), BF16), jax.ShapeDtypeStruct((s, D), F32),
                   jax.ShapeDtypeStruct((s, D), BF16), jax.ShapeDtypeStruct((SUBLANES, D), F32),
                   jax.ShapeDtypeStruct((SUBLANES, D), F32)],
        compiler_params=_cparams(56, ("arbitrary",)),
    )(dff, wd, gate, up, wgu, x2, g_pre, dx3, y, g_post)


def _grad_matmul(a, b, *, ta, tb, ts, name):
    s, ka = a.shape
    nb = b.shape[1]
    ts = min(ts, s)
    nk = s // ts

    def body(a_ref, b_ref, o_ref, acc):
        k = pl.program_id(2)

        @pl.when(k == 0)
        def _():
            acc[...] = jnp.zeros_like(acc)

        acc[...] += lax.dot_general(a_ref[...], b_ref[...], TN, preferred_element_type=F32)

        @pl.when(k == nk - 1)
        def _():
            o_ref[...] = acc[...].astype(BF16)

    return pl.pallas_call(
        body, name=name, grid=(ka // ta, nb // tb, nk),
        in_specs=[pl.BlockSpec((ts, ta), lambda i, j, k: (k, i)), pl.BlockSpec((ts, tb), lambda i, j, k: (k, j))],
        out_specs=pl.BlockSpec((ta, tb), lambda i, j, k: (i, j)),
        out_shape=jax.ShapeDtypeStruct((ka, nb), BF16),
        scratch_shapes=[pltpu.VMEM((ta, tb), F32)],
        compiler_params=_cparams(48, ("arbitrary", "arbitrary", "arbitrary")),
    )(a, b)


def _grad_matmul_t(at, b, *, tb, name):
    ka, s = at.shape
    blocked = b.ndim == 3
    nb = b.shape[-1]
    steps = b.shape[0] if blocked else nb // tb
    width = nb if blocked else tb

    def body(a_ref, b_ref, o_ref):
        bv = b_ref[0] if blocked else b_ref[...]
        res = jnp.dot(a_ref[...], bv, preferred_element_type=F32).astype(BF16)
        if blocked:
            o_ref[0] = res
        else:
            o_ref[...] = res

    if blocked:
        b_spec = pl.BlockSpec((1, s, nb), lambda j: (j, 0, 0))
        o_spec = pl.BlockSpec((1, ka, nb), lambda j: (j, 0, 0))
        o_shape = jax.ShapeDtypeStruct((steps, ka, nb), BF16)
    else:
        b_spec = pl.BlockSpec((s, width), lambda j: (0, j))
        o_spec = pl.BlockSpec((ka, width), lambda j: (0, j))
        o_shape = jax.ShapeDtypeStruct((ka, nb), BF16)
    return pl.pallas_call(
        body, name=name, grid=(steps,),
        in_specs=[_resident((ka, s)), b_spec], out_specs=o_spec, out_shape=o_shape,
        compiler_params=_cparams(56, ("arbitrary",)),
    )(at, b)


GW_TILE = 256


def _grad_w_in(h1t, pieces):
    ka, s = h1t.shape
    widths = [p.shape[1] for p in pieces]
    assert all(w % GW_TILE == 0 for w in widths)
    first = [sum(widths[:i]) // GW_TILE for i in range(len(pieces))]
    count = [w // GW_TILE for w in widths]

    def body(a_ref, *refs):
        o_ref = refs[-1]
        j = pl.program_id(0)
        for ref, f0, n in zip(refs[:-1], first, count):
            @pl.when((j >= f0) & (j < f0 + n))
            def _(ref=ref):
                o_ref[...] = jnp.dot(a_ref[...], ref[...], preferred_element_type=F32).astype(BF16)

    def spec(f0, n):
        return pl.BlockSpec((s, GW_TILE), lambda j: (0, jnp.clip(j - f0, 0, n - 1)))

    return pl.pallas_call(
        body, name="grad_w_in", grid=(sum(count),),
        in_specs=[_resident((ka, s))] + [spec(f0, n) for f0, n in zip(first, count)],
        out_specs=pl.BlockSpec((ka, GW_TILE), lambda j: (0, j)),
        out_shape=jax.ShapeDtypeStruct((ka, sum(widths)), BF16),
        compiler_params=_cparams(56, ("arbitrary",)),
    )(h1t, *pieces)


def _grad_matmul_blocks(a, b, *, ts, name):
    nblk = a.shape[0] if a.ndim == 3 else b.shape[0]
    s = a.shape[-2]
    ka, nb = a.shape[-1], b.shape[-1]
    ts = min(ts, s)
    nk = s // ts

    def body(a_ref, b_ref, o_ref, acc):
        k = pl.program_id(1)

        @pl.when(k == 0)
        def _():
            acc[...] = jnp.zeros_like(acc)

        av = a_ref[0] if a.ndim == 3 else a_ref[...]
        bv = b_ref[0] if b.ndim == 3 else b_ref[...]
        acc[...] += lax.dot_general(av, bv, TN, preferred_element_type=F32)

        @pl.when(k == nk - 1)
        def _():
            o_ref[0] = acc[...].astype(BF16)

    def spec(arr, width):
        if arr.ndim == 3:
            return pl.BlockSpec((1, ts, width), lambda j, k: (j, k, 0))
        return pl.BlockSpec((ts, width), lambda j, k: (k, 0))

    return pl.pallas_call(
        body, name=name, grid=(nblk, nk),
        in_specs=[spec(a, ka), spec(b, nb)],
        out_specs=pl.BlockSpec((1, ka, nb), lambda j, k: (j, 0, 0)),
        out_shape=jax.ShapeDtypeStruct((nblk, ka, nb), BF16),
        scratch_shapes=[pltpu.VMEM((ka, nb), F32)],
        compiler_params=_cparams(48, ("arbitrary", "arbitrary")),
    )(a, b)


def _mix_bwd(dy, w_out, o, cv, bcu, ga, gc, gsum, *, tm):
    s = dy.shape[0]

    def group_norm_bwd(dn_out, v, g, gs):
        r = lax.rsqrt(_split_dot(v * v, gs) * (1.0 / DH) + EPS)
        n = v * r
        dn = dn_out * g
        return r * (dn - n * (_split_dot(dn * n, gs) * (1.0 / DH))), dn_out * n

    def body(dy_ref, w_ref, o_ref, cv_ref, bcu_ref, ga_ref, gc_ref, gs_ref,
             do_ref, dl_ref, dcv_ref, db_ref, dga_ref, dgc_ref):
        @pl.when(pl.program_id(0) == 0)
        def _():
            dga_ref[...] = jnp.zeros_like(dga_ref)
            dgc_ref[...] = jnp.zeros_like(dgc_ref)

        dm = lax.dot_general(dy_ref[...], w_ref[...], NT, preferred_element_type=F32)
        ov = o_ref[...]
        do, dga = group_norm_bwd(dm[:, 0:AW], ov, ga_ref[...], gs_ref[...])
        dob = do.astype(BF16)
        do_ref[...] = dob
        dl_ref[...] = _split_dot(dob.astype(F32) * ov, gs_ref[...])
        dga_ref[...] += _fold8(dga)
        gate_b = bcu_ref[:, 0:CW]
        cv = cv_ref[...]
        dconv, dgc = group_norm_bwd(dm[:, AW:D], gate_b * cv, gc_ref[...], gs_ref[...])
        dgc_ref[...] += _fold8(dgc)
        dcv_ref[...] = dconv * gate_b
        db_ref[...] = (dconv * cv).astype(BF16)

    return pl.pallas_call(
        body, name="mix_bwd", grid=(s // tm,),
        in_specs=[_rows(tm, D), _resident((D, D)), _rows(tm, AW), _rows(tm, CW), _rows(tm, 3 * CW),
                  _full((1, AW)), _full((1, CW)), _full((CW, CW))],
        out_specs=[_rows(tm, AW), _rows(tm, AW), _rows(tm, CW), _rows(tm, CW),
                   _full((SUBLANES, AW)), _full((SUBLANES, CW))],
        out_shape=[jax.ShapeDtypeStruct((s, AW), BF16), jax.ShapeDtypeStruct((s, AW), F32),
                   jax.ShapeDtypeStruct((s, CW), F32), jax.ShapeDtypeStruct((s, CW), BF16),
                   jax.ShapeDtypeStruct((SUBLANES, AW), F32), jax.ShapeDtypeStruct((SUBLANES, CW), F32)],
        compiler_params=_cparams(48, ("arbitrary",)),
    )(dy, w_out, o, cv, bcu, ga, gc, gsum)


def _conv_bwd(dcv, db, bcu, cw8, *, tm):
    s = dcv.shape[0]
    nt = s // tm

    def body(dcv_ref, nxt_ref, db_ref, bcu_ref, halo_ref, cw_ref, dbcu_ref, dw_ref):
        i = pl.program_id(0)

        @pl.when(i == 0)
        def _():
            dw_ref[...] = jnp.zeros_like(dw_ref)

        z, z1, z2 = _conv_taps(bcu_ref, halo_ref, i == 0, tm)
        d = dcv_ref[...]
        dw_ref[0] += _fold8(d * z2)
        dw_ref[1] += _fold8(d * z1)
        dw_ref[2] += _fold8(d * z)
        nx = jnp.where(i == nt - 1, 0.0, nxt_ref[...])
        row = lax.broadcasted_iota(jnp.int32, (tm, CW), 0)
        d1 = jnp.where(row == tm - 1, nx[0:1, :], pltpu.roll(d, tm - 1, axis=0))
        d2 = jnp.where(row == tm - 2, nx[0:1, :], jnp.where(row == tm - 1, nx[1:2, :], pltpu.roll(d, tm - 2, axis=0)))
        dz = cw_ref[2:3, :] * d + cw_ref[1:2, :] * d1 + cw_ref[0:1, :] * d2
        dbcu_ref[:, 0:CW] = db_ref[...]
        dbcu_ref[:, CW:2 * CW] = (dz * bcu_ref[:, 2 * CW:3 * CW]).astype(BF16)
        dbcu_ref[:, 2 * CW:3 * CW] = (dz * bcu_ref[:, CW:2 * CW]).astype(BF16)

    return pl.pallas_call(
        body, name="conv_bwd", grid=(nt,),
        in_specs=[_rows(tm, CW),
                  pl.BlockSpec((SUBLANES, CW), lambda i: (jnp.minimum((i + 1) * (tm // SUBLANES), s // SUBLANES - 1), 0)),
                  _rows(tm, CW), _rows(tm, 3 * CW), _halo_before(tm, 3 * CW), _full((SUBLANES, CW))],
        out_specs=[_rows(tm, 3 * CW), _full((3, SUBLANES, CW))],
        out_shape=[jax.ShapeDtypeStruct((s, 3 * CW), BF16), jax.ShapeDtypeStruct((3, SUBLANES, CW), F32)],
        compiler_params=_cparams(48, ("arbitrary",)),
    )(dcv, dcv, db, bcu, bcu, cw8)


def _attn_bwd(qp, kp, v, do, lse, dl, mk, *, t):
    s = qp.shape[0]
    nq = s // t

    def body(q_ref, k_ref, v_ref, do_ref, lse_ref, dl_ref, mk_ref, dq_ref, dk_ref, dv_ref, dkx_ref, dq_acc):
        ki = pl.program_id(1)

        @pl.when(ki == 0)
        def _():
            dq_acc[...] = jnp.zeros_like(dq_acc)

        row = lax.broadcasted_iota(jnp.int32, (t, t), 0)
        col = lax.broadcasted_iota(jnp.int32, (t, t), 1)
        lane = lax.broadcasted_iota(jnp.int32, (t, 128), 1)

        def head_step(hh, qi, carry, masked):
            dk, dv, cs = carry
            off = pl.multiple_of(qi * t, t)
            rows = pl.ds(off, t)
            kh = k_ref[:, HP * hh:HP * (hh + 1)]
            q = q_ref[rows, HP * hh:HP * (hh + 1)]
            in_head = (lane >= DH * hh) & (lane < DH * (hh + 1))
            m_col = mk_ref[0, rows, DH * hh:DH * hh + 1]
            scale = jnp.exp(m_col - lse_ref[rows, DH * hh:DH * hh + 1])
            dom = jnp.where(in_head, do_ref[rows, :], jnp.zeros((), BF16))
            sc = lax.dot_general(q, kh, NT, preferred_element_type=F32) - m_col
            if masked:
                sc = jnp.where(col <= row, sc, -1e30)
            pt = jnp.exp(sc).astype(BF16)
            dp = lax.dot_general(dom, v_ref[...], NT, preferred_element_type=F32)
            ds32 = (pt.astype(F32) * scale) * (dp - dl_ref[rows, DH * hh:DH * hh + 1])
            ds = ds32.astype(BF16)
            cs = cs + _fold8(ds32)
            dv = dv + jnp.dot((dom.astype(F32) * scale).astype(BF16).T, pt, preferred_element_type=F32)
            dk = dk + jnp.dot(q.T, ds, preferred_element_type=F32)
            dq_acc[rows, HP * hh:HP * (hh + 1)] += jnp.dot(ds, kh, preferred_element_type=F32)
            return dk, dv, cs

        def step(qi, carry, masked):
            return tuple(head_step(hh, qi, carry[hh], masked) for hh in range(2))

        zero = (jnp.zeros((HP, t), F32), jnp.zeros((128, t), F32), jnp.zeros((SUBLANES, t), F32))
        carry = step(ki, (zero, zero), True)
        (dk0, dv0, cs0), (dk1, dv1, cs1) = lax.fori_loop(ki + 1, nq, functools.partial(step, masked=False), carry)
        def two_heads(a0, a1):
            return jnp.where(lane < DH, a0, pltpu.roll(a1, DH, axis=1))

        dk_ref[...] = two_heads(dk0.T, dk1.T).astype(BF16)
        dv_ref[...] = (dv0 + dv1).T.astype(BF16)

        def as_column(cs):
            return lax.dot_general(cs, jnp.ones((SUBLANES, 128), F32), TN, precision=HIGHEST, preferred_element_type=F32)

        dkx_ref[...] = jnp.where(lane < DH, as_column(cs0), as_column(cs1))

        @pl.when(ki == nq - 1)
        def _():
            for c in range(s // t):
                rows = slice(c * t, (c + 1) * t)
                dq_ref[rows, :] = two_heads(dq_acc[rows, 0:HP], dq_acc[rows, HP:2 * HP]).astype(BF16)

    return pl.pallas_call(
        body, name="attn_bwd", grid=(H // 2, nq),
        in_specs=[pl.BlockSpec((s, 2 * HP), lambda p, i: (0, p)),
                  pl.BlockSpec((t, 2 * HP), lambda p, i: (i, p)),
                  pl.BlockSpec((t, 128), lambda p, i: (i, p)),
                  pl.BlockSpec((s, 128), lambda p, i: (0, p)),
                  pl.BlockSpec((s, 128), lambda p, i: (0, p)),
                  pl.BlockSpec((s, 128), lambda p, i: (0, p)),
                  pl.BlockSpec((1, s, 128), lambda p, i: (i, 0, p))],
        out_specs=[pl.BlockSpec((s, 128), lambda p, i: (0, p)),
                   pl.BlockSpec((t, 128), lambda p, i: (i, p)),
                   pl.BlockSpec((t, 128), lambda p, i: (i, p)),
                   pl.BlockSpec((t, 128), lambda p, i: (i, p))],
        out_shape=[jax.ShapeDtypeStruct((s, AW), BF16), jax.ShapeDtypeStruct((s, AW), BF16),
                   jax.ShapeDtypeStruct((s, AW), BF16), jax.ShapeDtypeStruct((s, AW), F32)],
        scratch_shapes=[pltpu.VMEM((s, 2 * HP), F32)],
        compiler_params=_cparams(56, ("arbitrary", "arbitrary")),
    )(qp, kp, v, do, lse, dl, mk)


def _forget_bwd(dkx, z, sel, *, tm):
    s = dkx.shape[0]
    nt = s // tm

    def body(dk_ref, z_ref, sel_ref, dfl_ref, dbf_ref, carry):
        @pl.when(pl.program_id(0) == 0)
        def _():
            carry[...] = jnp.zeros_like(carry)
            dbf_ref[...] = jnp.zeros_like(dbf_ref)

        dc = _split_dot(dk_ref[...], sel_ref[...])
        row = lax.broadcasted_iota(jnp.int32, (tm, tm), 0)
        col = lax.broadcasted_iota(jnp.int32, (tm, tm), 1)
        tri = (col >= row).astype(BF16)
        dlogf = _exact_dot01(tri, dc) + carry[0:1, :]
        carry[...] = jnp.broadcast_to(dlogf[0:1, :], carry.shape)
        dz = dlogf * (1.0 - jax.nn.sigmoid(z_ref[...]))
        dfl_ref[:, 0:128] = dz.astype(BF16)
        dfl_ref[:, 128:GW_TILE] = jnp.zeros((tm, GW_TILE - 128), BF16)
        dbf_ref[...] += _fold8(dz)

    rev = lambda i: (nt - 1 - i, 0)
    return pl.pallas_call(
        body, name="forget_bwd", grid=(nt,),
        in_specs=[pl.BlockSpec((tm, AW), rev), pl.BlockSpec((tm, 128), rev), _full((AW, 128))],
        out_specs=[pl.BlockSpec((tm, GW_TILE), rev), _full((SUBLANES, 128))],
        out_shape=[jax.ShapeDtypeStruct((s, GW_TILE), BF16), jax.ShapeDtypeStruct((SUBLANES, 128), F32)],
        scratch_shapes=[pltpu.VMEM((SUBLANES, 128), F32)],
        compiler_params=_cparams(48, ("arbitrary",)),
    )(dkx, z, sel)


def _in_proj_bwd(pieces, wp, x, g1, dx2, *, tm):
    s = x.shape[0]

    def body(q_ref, k_ref, v_ref, bcu_ref, f_ref, w_ref, x_ref, g_ref, dx2_ref, dx_ref, dg_ref):
        @pl.when(pl.program_id(0) == 0)
        def _():
            dg_ref[...] = jnp.zeros_like(dg_ref)

        dh = None
        for ref, (lo, hi) in zip((q_ref, k_ref, v_ref, bcu_ref, f_ref), PIECES):
            part = lax.dot_general(ref[...], w_ref[:, lo:hi], NT, preferred_element_type=F32)
            dh = part if dh is None else dh + part
        _, n, r = _rms_fwd(x_ref[...], g_ref[...])
        dxn, dg = _rms_bwd(dh, n, r, g_ref[...])
        dx_ref[...] = dx2_ref[...] + dxn
        dg_ref[...] += _fold8(dg)

    return pl.pallas_call(
        body, name="in_proj_bwd", grid=(s // tm,),
        in_specs=[_rows(tm, hi - lo) for lo, hi in PIECES] + [_resident((D, WP)), _rows(tm, D), _full((1, D)), _rows(tm, D)],
        out_specs=[_rows(tm, D), _full((SUBLANES, D))],
        out_shape=[jax.ShapeDtypeStruct((s, D), F32), jax.ShapeDtypeStruct((SUBLANES, D), F32)],
        compiler_params=_cparams(56, ("arbitrary",)),
    )(*pieces, wp, x, g1, dx2)


def _position():
    return lax.axis_index("x"), lax.axis_index("y"), lax.axis_index("c")


ANY = pl.BlockSpec(memory_space=pl.ANY)


def _all_gather(shards):
    n = len(shards)

    def body(*refs):
        x_refs, out_refs = refs[:n], refs[n:2 * n]
        send_sems, recv_sems, local_sems = refs[2 * n:]
        x, y, c = _position()
        me, sibling = (x, y, c), (x, y, 1 - c)
        chips = [(1 - x, y), (x, 1 - y), (1 - x, 1 - y)]

        def copy(a, k, block, to, own=False):
            slot = out_refs[a].at[4 * block[0] + 2 * block[1] + block[2]]
            return pltpu.make_async_remote_copy(
                src_ref=x_refs[a] if own else slot, dst_ref=slot,
                send_sem=send_sems.at[7 * a + k], recv_sem=recv_sems.at[7 * a + k], device_id=to, device_id_type=MESH_ID)

        mine = [pltpu.make_async_copy(x_refs[a], out_refs[a].at[4 * x + 2 * y + c], local_sems.at[a]) for a in range(n)]
        for cp in mine:
            cp.start()
        first = []
        for a in range(n):
            first.append(copy(a, 0, me, sibling, own=True))
            first += [copy(a, 1 + j, me, (*chip, c), own=True) for j, chip in enumerate(chips)]
        for cp in first:
            cp.start()
        passed = []
        for j, chip in enumerate(chips):
            for a in range(n):
                copy(a, 1 + j, (*chip, c), me).wait_recv()
                fwd = copy(a, 4 + j, (*chip, c), sibling)
                fwd.start()
                passed.append(fwd)
        for a in range(n):
            copy(a, 0, sibling, me).wait_recv()
            for j, chip in enumerate(chips):
                copy(a, 4 + j, (*chip, 1 - c), me).wait_recv()
        for cp in first + passed:
            cp.wait_send()
        for cp in mine:
            cp.wait()

    return pl.pallas_call(
        body, name="all_gather_weights",
        out_shape=[jax.ShapeDtypeStruct((NDEV,) + sh.shape, sh.dtype) for sh in shards],
        in_specs=[ANY] * n, out_specs=[ANY] * n,
        scratch_shapes=[pltpu.SemaphoreType.DMA((7 * n,)), pltpu.SemaphoreType.DMA((7 * n,)), pltpu.SemaphoreType.DMA((n,))],
    )(*shards)


def _pair_exchange(grads):
    n = len(grads)

    def body(*refs):
        g_refs, out_refs = refs[:n], refs[n:2 * n]
        send_sems, recv_sems = refs[2 * n:]
        x, y, c = _position()
        copies = [pltpu.make_async_remote_copy(
            src_ref=g_refs[a].at[:, pl.ds(1 - c, 1)], dst_ref=out_refs[a], send_sem=send_sems.at[a],
            recv_sem=recv_sems.at[a], device_id=(x, y, 1 - c), device_id_type=MESH_ID) for a in range(n)]
        for cp in copies:
            cp.start()
        for cp in copies:
            cp.wait()

    return pl.pallas_call(
        body, name="grad_pair_exchange",
        out_shape=[jax.ShapeDtypeStruct((4, 1) + g.shape[2:], g.dtype) for g in grads],
        in_specs=[ANY] * n, out_specs=[ANY] * n,
        scratch_shapes=[pltpu.SemaphoreType.DMA((n,)), pltpu.SemaphoreType.DMA((n,))],
    )(*grads)


def _pair_sum(g, got, idx, *, tr, name):
    r, c = g.shape[2:]

    def body(idx_ref, g_ref, got_ref, pb_ref, own_ref):
        p = g_ref[0, 0].astype(F32) + got_ref[0, 0].astype(F32)
        pb_ref[0] = p.astype(BF16)

        @pl.when(pl.program_id(1) == idx_ref[1])
        def _():
            own_ref[...] = p

    return pl.pallas_call(
        body, name=name,
        grid_spec=pltpu.PrefetchScalarGridSpec(
            num_scalar_prefetch=1, grid=(r // tr, 4),
            in_specs=[pl.BlockSpec((1, 1, tr, c), lambda i, j, idx: (j, idx[0], i, 0)),
                      pl.BlockSpec((1, 1, tr, c), lambda i, j, idx: (j, 0, i, 0))],
            out_specs=[pl.BlockSpec((1, tr, c), lambda i, j, idx: (j, i, 0)),
                       pl.BlockSpec((tr, c), lambda i, j, idx: (i, 0))]),
        out_shape=[jax.ShapeDtypeStruct((4, r, c), BF16), jax.ShapeDtypeStruct((r, c), F32)],
        compiler_params=_cparams(32, ("arbitrary", "arbitrary")),
    )(idx, g, got)


HBM = pl.BlockSpec(memory_space=pltpu.HBM)
SEM = pl.BlockSpec(memory_space=pltpu.SEMAPHORE)
DATAFLOW = pltpu.SideEffectType.DATAFLOW_SIDE_EFFECTING


PEERS = {"gather": NDEV - 1, "scatter": NDEV - 1, "chips": 3}


def _exchange_copies(src_refs, land_refs, send_sems, recv_sems, mode):
    x, y, c = _position()
    me, my_chip = 4 * x + 2 * y + c, 2 * x + y
    npeers = PEERS[mode]
    copies = []
    for a, (s_ref, l_ref) in enumerate(zip(src_refs, land_refs)):
        for k in range(npeers):
            if mode == "chips":
                px, py, pc = x ^ ((k + 1) >> 1), y ^ ((k + 1) & 1), c
                src, dst = s_ref.at[2 * px + py], l_ref.at[my_chip]
            else:
                px, py, pc = x ^ ((k + 1) >> 2), y ^ (((k + 1) >> 1) & 1), c ^ ((k + 1) & 1)
                src, dst = (s_ref.at[4 * px + 2 * py + pc] if mode == "scatter" else s_ref), l_ref.at[me]
            copies.append(pltpu.make_async_remote_copy(
                src_ref=src, dst_ref=dst, send_sem=send_sems.at[npeers * a + k], recv_sem=recv_sems.at[npeers * a + k],
                device_id=(px, py, pc), device_id_type=MESH_ID))
    return copies


def _exchange_start(srcs, lands, after, *, mode, name):
    n = len(srcs)
    nsem = PEERS[mode] * n

    def body(*refs):
        token = refs[-1]
        for cp in _exchange_copies(refs[:n], refs[n:2 * n], refs[2 * n + 1], refs[2 * n + 2], mode):
            cp.start()
        token[...] = jnp.zeros_like(token)

    arrays = list(srcs) + list(lands)
    outs = pl.pallas_call(
        body, name=name,
        out_shape=(pltpu.SemaphoreType.DMA((nsem,)), pltpu.SemaphoreType.DMA((nsem,)),
                   *[pltpu.HBM(a.shape, a.dtype) for a in arrays], jax.ShapeDtypeStruct((SUBLANES, LANES), F32)),
        in_specs=[HBM] * (2 * n) + [ANY],
        out_specs=(SEM, SEM, *[HBM] * (2 * n), pl.BlockSpec(memory_space=pltpu.VMEM)),
        input_output_aliases={i: 2 + i for i in range(2 * n)},
        compiler_params=pltpu.CompilerParams(has_side_effects=DATAFLOW),
    )(*[pltpu.with_memory_space_constraint(a, pltpu.HBM) for a in arrays], after)
    return outs[0], outs[1], outs[2:2 + n], outs[2 + n:2 + 2 * n], outs[-1]


def _exchange_wait(send_sems, recv_sems, srcs, lands, after, *, mode, name):
    n = len(srcs)

    def body(*refs):
        for cp in _exchange_copies(refs[:n], refs[n:2 * n], refs[2 * n], refs[2 * n + 1], mode):
            cp.wait_send()
            cp.wait_recv()

    arrays = list(srcs) + list(lands)
    outs = pl.pallas_call(
        body, name=name,
        out_shape=tuple(pltpu.HBM(a.shape, a.dtype) for a in arrays),
        in_specs=[HBM] * (2 * n) + [SEM, SEM, ANY],
        out_specs=tuple([HBM] * (2 * n)),
        input_output_aliases={i: i for i in range(2 * n)},
        compiler_params=pltpu.CompilerParams(has_side_effects=DATAFLOW),
    )(*arrays, send_sems, recv_sems, after)
    return outs[n:]


def _own_slot(value, me):
    return lax.dynamic_update_index_in_dim(lax.empty((NDEV,) + value.shape, value.dtype), value, me, 0)


def _small_all_reduce(parts):
    def body(gmp_ref, gmo_ref, gfp_ref, gfo_ref, ga_ref, gc_ref, dw_ref, bf_ref, loss_ref,
             out_ref, buf, send_sems, recv_sems):
        x, y, c = _position()
        me = 4 * x + 2 * y + c

        def colsum(v):
            return jnp.sum(v, axis=0, keepdims=True)

        loss = jnp.sum(colsum(loss_ref[...]), axis=1, keepdims=True) * (0.5 / D)
        rows = [colsum(gmp_ref[...]), colsum(gmo_ref[...]), colsum(gfp_ref[...]), colsum(gfo_ref[...]),
                jnp.concatenate([colsum(ga_ref[...]), colsum(gc_ref[...])], axis=1),
                jnp.concatenate([colsum(dw_ref[0]), colsum(dw_ref[1])], axis=1),
                jnp.concatenate([colsum(dw_ref[2]), colsum(bf_ref[...]), jnp.broadcast_to(loss, (1, 128)),
                                 jnp.zeros((1, 256), F32)], axis=1),
                jnp.zeros((1, D), F32)]
        buf[me] = jnp.concatenate(rows, axis=0)
        copies = []
        for mm in range(1, NDEV):
            peer = (x ^ (mm >> 2), y ^ ((mm >> 1) & 1), c ^ (mm & 1))
            copies.append(pltpu.make_async_remote_copy(
                src_ref=buf.at[me], dst_ref=buf.at[me], send_sem=send_sems.at[mm - 1], recv_sem=recv_sems.at[mm - 1],
                device_id=peer, device_id_type=MESH_ID))
        for cp in copies:
            cp.start()
        for cp in copies:
            cp.wait_recv()
        for cp in copies:
            cp.wait_send()
        acc = buf[0]
        for d in range(1, NDEV):
            acc = acc + buf[d]
        out_ref[...] = acc

    vm = pl.BlockSpec(memory_space=pltpu.VMEM)
    return pl.pallas_call(
        body, name="small_all_reduce",
        out_shape=jax.ShapeDtypeStruct((SUBLANES, D), F32),
        in_specs=[vm] * len(parts), out_specs=vm,
        scratch_shapes=[pltpu.VMEM((NDEV, SUBLANES, D), F32), pltpu.SemaphoreType.DMA((7,)), pltpu.SemaphoreType.DMA((7,))],
    )(*parts)


def _adam_update(w, g, m, v):
    nm = ADAM_B1 * m + (1.0 - ADAM_B1) * g
    nv = ADAM_B2 * v + (1.0 - ADAM_B2) * (g * g)
    m_hat = nm / (1.0 - ADAM_B1 ** ADAM_STEP)
    v_hat = nv / (1.0 - ADAM_B2 ** ADAM_STEP)
    return -ADAM_LR * (m_hat / (jnp.sqrt(v_hat) + ADAM_EPS) + ADAM_WD * w), nm, nv


def _adamw(w, g, m, v, *, tr, name):
    rows, cols = w.shape

    def body(w_ref, g_ref, m_ref, v_ref, d_ref, nm_ref, nv_ref):
        d_ref[...], nm_ref[...], nv_ref[...] = _adam_update(w_ref[...], g_ref[...], m_ref[...], v_ref[...])

    spec = pl.BlockSpec((tr, cols), lambda i: (i, 0))
    return pl.pallas_call(
        body, name=name, grid=(rows // tr,),
        in_specs=[spec] * 4, out_specs=[spec] * 3,
        out_shape=[jax.ShapeDtypeStruct((rows, cols), F32)] * 3,
        compiler_params=_cparams(32, ("arbitrary",)),
    )(w, g, m, v)


def _chip_sum_adamw(got, own, idx, w, m, v, *, tr, name):
    rows, cols = w.shape

    def body(idx_ref, got_ref, own_ref, w_ref, m_ref, v_ref, g_ref, d_ref, nm_ref, nv_ref):
        g = jnp.zeros((tr, cols), F32)
        for j in range(4):
            g = g + jnp.where(idx_ref[1] == j, own_ref[...], got_ref[j].astype(F32))
        g_ref[...] = g
        d_ref[...], nm_ref[...], nv_ref[...] = _adam_update(w_ref[...], g, m_ref[...], v_ref[...])

    spec = pl.BlockSpec((tr, cols), lambda i, idx: (i, 0))
    return pl.pallas_call(
        body, name=name,
        grid_spec=pltpu.PrefetchScalarGridSpec(
            num_scalar_prefetch=1, grid=(rows // tr,),
            in_specs=[pl.BlockSpec((4, tr, cols), lambda i, idx: (0, i, 0)), spec, spec, spec, spec],
            out_specs=[spec] * 4),
        out_shape=[jax.ShapeDtypeStruct((rows, cols), F32)] * 4,
        compiler_params=_cparams(32, ("arbitrary",)),
    )(idx, got, own, w, m, v)


def _device_sum_adamw(land, w, m, v, *, tr, name):
    rows, cols = w.shape

    def body(land_ref, w_ref, m_ref, v_ref, g_ref, d_ref, nm_ref, nv_ref):
        g = land_ref[0].astype(F32)
        for dev in range(1, NDEV):
            g = g + land_ref[dev].astype(F32)
        g_ref[...] = g
        d_ref[...], nm_ref[...], nv_ref[...] = _adam_update(w_ref[...], g, m_ref[...], v_ref[...])

    spec = pl.BlockSpec((tr, cols), lambda i: (i, 0))
    return pl.pallas_call(
        body, name=name, grid=(rows // tr,),
        in_specs=[pl.BlockSpec((NDEV, tr, cols), lambda i: (0, i, 0)), spec, spec, spec],
        out_specs=[spec] * 4,
        out_shape=[jax.ShapeDtypeStruct((rows, cols), F32)] * 4,
        compiler_params=_cparams(32, ("arbitrary",)),
    )(land, w, m, v)


def _placement_constants():
    j = jnp.arange(128)[:, None]
    lane = jnp.arange(1024)[None, :]
    head, sub = lane // HP, lane % HP
    piece, jh = j // H, j % H
    valid = (j < 3 * H) & (jh == head)
    pq = jnp.where(valid & (sub == DH + piece), 1.0, 0.0).astype(BF16)
    pk = jnp.where(valid & (sub == DH + 3 + piece), -1.0, 0.0).astype(BF16)
    oq = jnp.where((sub >= DH + 3) & (sub < DH + 6), 1.0, 0.0).astype(F32)
    ok = jnp.where((sub >= DH) & (sub < DH + 3), 1.0, 0.0).astype(F32)
    r = jnp.arange(AW)[:, None]
    cc = jnp.arange(128)[None, :]
    sel = jnp.where((r % DH == 3) & (r // DH == cc), -1.0, 0.0).astype(BF16)
    gi = jnp.arange(CW)
    gsum = (gi[:, None] // DH == gi[None, :] // DH).astype(BF16)
    return pq, pk, oq, ok, sel, gsum


def _local_step(xs, tgt, wp, late_weights, cw8, bfp, g_attn_out, g_conv_out,
                g_mix_pre, g_mix_post, g_ffn_pre, g_ffn_post, early_grads=None, last_grad=None):
    pq, pk, oq, ok, sel, gsum = _placement_constants()
    h1t, qp, kp, vv, bcu, zf = _in_proj(xs, g_mix_pre, wp, bfp, pq, pk, oq, ok, tm=512)
    o, lse, mk = _attn_fwd(qp, kp, vv, t=512)
    w_out_f, wgu, wd = late_weights(lse)
    merged, y, x2, cv, h2 = _mix_out(o, bcu, cw8, g_attn_out, g_conv_out, gsum, w_out_f, xs, g_mix_post, g_ffn_pre, tm=512)
    gate, up, act, dx3, dff, loss_p, dg_ffn_post = _ffn_fwd_loss(h2, wgu, wd, x2, tgt, g_ffn_post, tm=512)

    dgu, dx2, dy, dg_ffn_pre, dg_mix_post = _ffn_bwd(dff, wd, gate, up, wgu, x2, g_ffn_pre, dx3, y, g_mix_post, tm=256)
    dw_down = _grad_matmul_blocks(act, dff, ts=4096, name="grad_w_down")
    dw_gu = _grad_matmul_blocks(dgu.reshape(NDEV, -1, FB), h2, ts=4096, name="grad_w_gate_up")
    dw_out = _grad_matmul(merged, dy, ta=1024, tb=1024, ts=2048, name="grad_w_out")
    token = early_grads(dw_out, dw_gu, dw_down) if early_grads is not None else None
    ga = g_attn_out if token is None else g_attn_out + token[0:1, 0:1]
    do, dl, dcv, db, dg_attn, dg_conv = _mix_bwd(dy, w_out_f, o, cv, bcu, ga, g_conv_out, gsum, tm=512)
    dbcu, dtaps = _conv_bwd(dcv, db, bcu, cw8, tm=512)
    dqp, dkp, dv, dkx = _attn_bwd(qp, kp, vv, do, lse, dl, mk, t=512)
    dfl, dbf = _forget_bwd(dkx, zf, sel, tm=512)
    pieces = (dqp, dkp, dv, dbcu, dfl)
    dwp = _grad_w_in(h1t, pieces)
    token = last_grad(dwp) if last_grad is not None else None
    g1 = g_mix_pre if token is None else g_mix_pre + token[0:1, 0:1]
    grad_x, dg_mix_pre = _in_proj_bwd(pieces, wp, xs, g1, dx2, tm=512)
    return (grad_x, dwp, dw_out, dw_gu, dw_down, dg_mix_pre, dg_mix_post, dg_ffn_pre, dg_ffn_post, dg_attn, dg_conv,
            dtaps, dbf, loss_p)


BIG_TILES = {"w_in": 256, "w_out": 128, "w_gate_up": 176, "w_down": 176}


def kernel(x, w_in, b_forget, conv_w, g_attn_out, g_conv_out, w_out, g_mix_pre, g_mix_post, w_gate_up, w_down, g_ffn_pre, g_ffn_post, loss_target, m_w_in, m_b_forget, m_conv_w, m_g_attn_out, m_g_conv_out, m_w_out, m_g_mix_pre, m_g_mix_post, m_w_gate_up, m_w_down, m_g_ffn_pre, m_g_ffn_post, v_w_in, v_b_forget, v_conv_w, v_g_attn_out, v_g_conv_out, v_w_out, v_g_mix_pre, v_g_mix_post, v_w_gate_up, v_w_down, v_g_ffn_pre, v_g_ffn_post):
    xc, yc, cc = _position()
    my_chip = 2 * xc + yc
    me = 2 * my_chip + cc
    idx = jnp.stack([cc, my_chip]).astype(jnp.int32)
    tables = _in_layout_tables()
    pad_in = lambda a: jnp.pad(a, ((0, 0), (0, IN_PAD - IN_COLS)))

    g_in, g_taps = _all_gather([pad_in(w_in[0]).astype(BF16), conv_w[0]])
    wp = _assemble_w_in(g_in, tables, tr=256)
    cw8 = jnp.pad(g_taps.transpose(1, 0, 2).reshape(3, CW), ((0, SUBLANES - 3), (0, 0)))

    late = [w_out[0].astype(BF16), w_gate_up[0].astype(BF16), w_down[0].astype(BF16)]
    ssem, rsem, late_thru, land_thru, token = _exchange_start(
        late, [_own_slot(s, me) for s in late], g_in, mode="gather", name="gather_late_start")
    bfp = jnp.pad(b_forget, ((0, 0), (0, 128 - H))) + token[0:1, :]

    def late_weights(after):
        l_out, l_gu, l_down = _exchange_wait(ssem, rsem, late_thru, land_thru, after, mode="gather", name="gather_late_wait")
        return l_out.reshape(D, D), l_gu.reshape(2, 4, D, FB), l_down.reshape(4, FB, D)

    early = {}

    def early_grads(dw_out, dw_gu, dw_down):
        srcs = [dw_out.reshape(NDEV, D // NDEV, D), dw_gu, dw_down.reshape(NDEV, DFF // NDEV, D)]
        lands = [_own_slot(lax.dynamic_index_in_dim(s, me, 0, keepdims=False), me) for s in srcs]
        early["handles"] = _exchange_start(srcs, lands, dw_out, mode="scatter", name="scatter_early_start")
        return early["handles"][4]

    last = {}

    def last_grad(dwp):
        g_w_in = _disassemble_w_in(dwp, tables, tr=256).reshape(4, 2, D, IN_PAD)
        (from_sibling,) = _pair_exchange([g_w_in])
        pair_b, last["own"] = _pair_sum(g_w_in, from_sibling, idx, tr=BIG_TILES["w_in"], name="grad_pair_sum_w_in")
        land = lax.dynamic_update_index_in_dim(lax.empty(pair_b.shape, pair_b.dtype),
                                               lax.dynamic_index_in_dim(pair_b, my_chip, 0, keepdims=False), my_chip, 0)
        last["handles"] = _exchange_start([pair_b], [land], last["own"], mode="chips", name="chips_w_in_start")
        return last["handles"][4]

    (grad_x, dwp, dw_out, dw_gu, dw_down, dg_mix_pre, dg_mix_post, dg_ffn_pre, dg_ffn_post, dg_attn, dg_conv,
     dtaps, dbf, loss_p) = _local_step(x[0], loss_target[0], wp, late_weights, cw8, bfp, g_attn_out, g_conv_out,
                                        g_mix_pre, g_mix_post, g_ffn_pre, g_ffn_post, early_grads, last_grad)

    e_ssem, e_rsem, e_srcs, e_lands, _ = early["handles"]
    land_out, land_gu, land_down = _exchange_wait(e_ssem, e_rsem, e_srcs, e_lands, dg_mix_pre, mode="scatter",
                                                  name="scatter_early_wait")
    res = {}
    big = {"w_out": (land_out, w_out[0], m_w_out[0], v_w_out[0]),
           "w_gate_up": (land_gu, w_gate_up[0].T, m_w_gate_up[0].T, v_w_gate_up[0].T),
           "w_down": (land_down, w_down[0], m_w_down[0], v_w_down[0])}
    for name, (land, w, m, v) in big.items():
        outs = _device_sum_adamw(land, w, m, v, tr=BIG_TILES[name], name="adamw_" + name)
        res[name] = [(o.T if name == "w_gate_up" else o)[None] for o in outs]
    c_ssem, c_rsem, c_srcs, c_lands, _ = last["handles"]
    after = sum(res[n][1][0, :SUBLANES, :LANES] for n in big)
    (from_chips,) = _exchange_wait(c_ssem, c_rsem, c_srcs, c_lands, after, mode="chips", name="chips_w_in_wait")
    outs = _chip_sum_adamw(from_chips, last["own"], idx, pad_in(w_in[0]), pad_in(m_w_in[0]), pad_in(v_w_in[0]),
                           tr=BIG_TILES["w_in"], name="adamw_w_in")
    res["w_in"] = [o[:, :IN_COLS][None] for o in outs]

    small = _small_all_reduce([dg_mix_pre, dg_mix_post, dg_ffn_pre, dg_ffn_post, dg_attn, dg_conv, dtaps, dbf, loss_p])
    taps_full = jnp.concatenate([small[5:6, :CW], small[5:6, CW:], small[6:7, :CW]], axis=0)
    small_grads = {
        "b_forget": small[6:7, CW:CW + H], "conv_w": lax.dynamic_slice(taps_full, (0, me * 64), (3, 64)),
        "g_attn_out": small[4:5, :AW], "g_conv_out": small[4:5, AW:], "g_mix_pre": small[0:1], "g_mix_post": small[1:2],
        "g_ffn_pre": small[2:3], "g_ffn_post": small[3:4]}
    loss = small[6, CW + 128]
    smalls = {"b_forget": (b_forget, m_b_forget, v_b_forget), "conv_w": (conv_w[0], m_conv_w[0], v_conv_w[0]),
              "g_attn_out": (g_attn_out, m_g_attn_out, v_g_attn_out), "g_conv_out": (g_conv_out, m_g_conv_out, v_g_conv_out),
              "g_mix_pre": (g_mix_pre, m_g_mix_pre, v_g_mix_pre), "g_mix_post": (g_mix_post, m_g_mix_post, v_g_mix_post),
              "g_ffn_pre": (g_ffn_pre, m_g_ffn_pre, v_g_ffn_pre), "g_ffn_post": (g_ffn_post, m_g_ffn_post, v_g_ffn_post)}
    for name, (w, m, v) in smalls.items():
        g = small_grads[name]
        outs = [g] + list(_adamw(w, g, m, v, tr=w.shape[0], name="adamw_" + name))
        res[name] = [o[None] for o in outs] if name == "conv_w" else outs

    order = ["w_in", "b_forget", "conv_w", "g_attn_out", "g_conv_out", "w_out", "g_mix_pre", "g_mix_post",
             "w_gate_up", "w_down", "g_ffn_pre", "g_ffn_post"]
    outs = [loss, grad_x[None]]
    for k in range(4):
        outs += [res[n][k] for n in order]
    return tuple(outs)
```

```python
import functools

import numpy as np

import jax
import jax.numpy as jnp
from jax import lax
from jax.experimental import pallas as pl
from jax.experimental.pallas import tpu as pltpu

F32 = jnp.float32
BF16 = jnp.bfloat16
HIGHEST = lax.Precision.HIGHEST
MESH_ID = pl.DeviceIdType.MESH

D = 1024
H = 8
DH = 64
AW = 512
CW = 512
DFF = 2816
FB = DFF // 4
HP = 128
OFF_Q, OFF_K, OFF_V, OFF_BCU, OFF_F = 0, 512, 1024, 1536, 3072
WP = OFF_F + 128
PIECES = ((OFF_Q, OFF_K), (OFF_K, OFF_V), (OFF_V, OFF_BCU), (OFF_BCU, OFF_F), (OFF_F, WP))
EPS = 1e-6
NDEV = 8
LANES = 128
SUBLANES = 8
IN_COLS = 385
IN_PAD = 512
WIN = 640
ADAM_LR, ADAM_B1, ADAM_B2, ADAM_EPS, ADAM_WD, ADAM_STEP = 0.001, 0.9, 0.999, 1e-08, 0.01, 10

NT = (((1,), (1,)), ((), ()))
TN = (((0,), (0,)), ((), ()))


def _cparams(vmem_mb=None, sem=None):
    kw = {}
    if vmem_mb is not None:
        kw["vmem_limit_bytes"] = vmem_mb << 20
    if sem is not None:
        kw["dimension_semantics"] = sem
    return pltpu.CompilerParams(**kw)


def _full(shape):
    return pl.BlockSpec(shape, lambda *_: (0,) * len(shape))


def _resident(shape):
    return pl.BlockSpec(shape, lambda *_: (0,) * len(shape), pipeline_mode=pl.Buffered(1))


def _rows(tm, width):
    return pl.BlockSpec((tm, width), lambda i: (i, 0))


def _fold8(v):
    r, w = v.shape
    return jnp.sum(v.reshape(r // SUBLANES, SUBLANES, w), axis=0)


def _split_dot(v, m01):
    hi = v.astype(BF16)
    lo = (v - hi.astype(F32)).astype(BF16)
    return (jnp.dot(hi, m01, preferred_element_type=F32)
            + jnp.dot(lo, m01, preferred_element_type=F32))


def _exact_dot01(m01, v):
    p1 = v.astype(BF16)
    r1 = v - p1.astype(F32)
    p2 = r1.astype(BF16)
    p3 = (r1 - p2.astype(F32)).astype(BF16)
    return (jnp.dot(m01, p1, preferred_element_type=F32) + jnp.dot(m01, p2, preferred_element_type=F32)
            + jnp.dot(m01, p3, preferred_element_type=F32))


def _rms_fwd(v, g):
    r = lax.rsqrt(jnp.mean(v * v, axis=-1, keepdims=True) + EPS)
    n = v * r
    return n * g, n, r


def _rms_bwd(do, n, r, g):
    dn = do * g
    return r * (dn - n * jnp.mean(dn * n, axis=-1, keepdims=True)), do * n


def _padded_column(n):
    if n < AW:
        return OFF_Q + n, 0.125
    if n < 3 * AW:
        return n, 1.0
    if n < 3 * AW + H:
        return OFF_F + n - 3 * AW, 1.0
    return OFF_BCU + n - 3 * AW - H, 1.0


def _in_layout_tables():
    dest = -np.ones((IN_PAD, LANES), np.int32)
    dest_f = -np.ones((IN_PAD, LANES), np.int32)
    scale = np.zeros((IN_PAD, LANES), np.float32)
    starts = []
    for k in range(NDEV):
        cols = [_padded_column(IN_COLS * k + j) for j in range(IN_COLS)]
        main = [c for c, _ in cols if c < OFF_F]
        ws = min((min(main) // LANES) * LANES, OFF_F - WIN)
        assert ws <= min(main) and max(main) < ws + WIN
        starts.append(ws)
        for j, (c, sc) in enumerate(cols):
            scale[j, k] = sc
            if c < OFF_F:
                dest[j, k] = c - ws
            else:
                dest_f[j, k] = c - OFF_F
    f_shards = tuple(k for k in range(NDEV) if (dest_f[:, k] >= 0).any())
    return tuple(starts), f_shards, jnp.asarray(dest), jnp.asarray(dest_f), jnp.asarray(scale)


def _perm(dest_ref, scale_ref, k, width):
    lane = lax.broadcasted_iota(jnp.int32, (IN_PAD, width), 1)
    return jnp.where(dest_ref[:, k:k + 1] == lane, scale_ref[:, k:k + 1], 0.0).astype(BF16)


def _assemble_w_in(blocks, tables, *, tr):
    starts, f_shards, dest, dest_f, scale = tables

    def body(b_ref, dest_ref, destf_ref, scale_ref, o_ref):
        o_ref[...] = jnp.zeros_like(o_ref)
        for k in range(NDEV):
            b = b_ref[k]
            ws = starts[k]
            part = jnp.dot(b, _perm(dest_ref, scale_ref, k, WIN), preferred_element_type=F32)
            o_ref[:, ws:ws + WIN] += part.astype(BF16)
            if k in f_shards:
                part = jnp.dot(b, _perm(destf_ref, scale_ref, k, 128), preferred_element_type=F32)
                o_ref[:, OFF_F:WP] += part.astype(BF16)

    tab = _full((IN_PAD, LANES))
    return pl.pallas_call(
        body, name="assemble_w_in", grid=(D // tr,),
        in_specs=[pl.BlockSpec((NDEV, tr, IN_PAD), lambda i: (0, i, 0)), tab, tab, tab],
        out_specs=_rows(tr, WP),
        out_shape=jax.ShapeDtypeStruct((D, WP), BF16),
        compiler_params=_cparams(48, ("arbitrary",)),
    )(blocks, dest, dest_f, scale)


def _disassemble_w_in(dwp, tables, *, tr):
    starts, f_shards, dest, dest_f, scale = tables
    width = dwp.shape[1]

    def body(g_ref, dest_ref, destf_ref, scale_ref, o_ref):
        for k in range(NDEV):
            ws = starts[k]
            acc = lax.dot_general(g_ref[:, ws:ws + WIN], _perm(dest_ref, scale_ref, k, WIN), NT, preferred_element_type=F32)
            if k in f_shards:
                acc = acc + lax.dot_general(g_ref[:, OFF_F:WP], _perm(destf_ref, scale_ref, k, 128), NT,
                                            preferred_element_type=F32)
            o_ref[k] = acc.astype(BF16)

    tab = _full((IN_PAD, LANES))
    return pl.pallas_call(
        body, name="disassemble_w_in", grid=(D // tr,),
        in_specs=[_rows(tr, width), tab, tab, tab],
        out_specs=pl.BlockSpec((NDEV, tr, IN_PAD), lambda i: (0, i, 0)),
        out_shape=jax.ShapeDtypeStruct((NDEV, D, IN_PAD), BF16),
        compiler_params=_cparams(48, ("arbitrary",)),
    )(dwp, dest, dest_f, scale)


def _in_proj(x, g1, wp, bfp, pq, pk, oq, ok, *, tm):
    s = x.shape[0]

    def body(x_ref, g_ref, w_ref, bf_ref, pq_ref, pk_ref, oq_ref, ok_ref,
             ht_ref, qp_ref, kp_ref, v_ref, bcu_ref, z_ref, carry):
        @pl.when(pl.program_id(0) == 0)
        def _():
            carry[...] = jnp.zeros_like(carry)

        h = _rms_fwd(x_ref[...], g_ref[...])[0].astype(BF16)
        ht_ref[...] = h.T
        z = jnp.dot(h, w_ref[:, OFF_F:WP], preferred_element_type=F32) + bf_ref[...]
        z_ref[...] = z
        lane = lax.broadcasted_iota(jnp.int32, (tm, 128), 1)
        logf = jnp.where(lane < H, jnp.minimum(z, 0.0) - jnp.log(1.0 + jnp.exp(-jnp.abs(z))), 0.0)
        row = lax.broadcasted_iota(jnp.int32, (tm, tm), 0)
        col = lax.broadcasted_iota(jnp.int32, (tm, tm), 1)
        tri = (col <= row).astype(BF16)
        c = _exact_dot01(tri, logf) + carry[0:1, :]
        carry[...] = jnp.broadcast_to(c[tm - 1:tm, :], carry.shape)
        c1 = c.astype(BF16).astype(F32)
        r1 = c - c1
        c2 = r1.astype(BF16).astype(F32)
        c3 = (r1 - c2).astype(BF16).astype(F32)
        zc = (c1 + pltpu.roll(c2, 8, axis=1) + pltpu.roll(c3, 16, axis=1)).astype(BF16)

        def pad_heads(v):
            blocks = []
            for pair in range(H // 2):
                two = v[:, 128 * pair:128 * (pair + 1)]
                blocks.append(jnp.where(lane < DH, two, 0.0))
                blocks.append(jnp.where(lane < DH, pltpu.roll(two, DH, axis=1), 0.0))
            return jnp.concatenate(blocks, axis=1)

        q = jnp.dot(h, w_ref[:, OFF_Q:OFF_K], preferred_element_type=F32)
        qp_ref[...] = (pad_heads(q) + jnp.dot(zc, pq_ref[...], preferred_element_type=F32) + oq_ref[...]).astype(BF16)
        k = jnp.dot(h, w_ref[:, OFF_K:OFF_V], preferred_element_type=F32)
        kp_ref[...] = (pad_heads(k) + jnp.dot(zc, pk_ref[...], preferred_element_type=F32) + ok_ref[...]).astype(BF16)
        v = pad_heads(jnp.dot(h, w_ref[:, OFF_V:OFF_BCU], preferred_element_type=F32))
        ones_lane = lax.broadcasted_iota(jnp.int32, (tm, H * HP), 1) % HP == DH
        v_ref[...] = jnp.where(ones_lane, 1.0, v).astype(BF16)
        bcu_ref[...] = jnp.dot(h, w_ref[:, OFF_BCU:OFF_F], preferred_element_type=F32)

    return pl.pallas_call(
        body, name="in_proj", grid=(s // tm,),
        in_specs=[_rows(tm, D), _full((1, D)), _resident((D, WP)), _full((1, 128)),
                  _full((128, 1024)), _full((128, 1024)), _full((1, 1024)), _full((1, 1024))],
        out_specs=[pl.BlockSpec((D, tm), lambda i: (0, i)), _rows(tm, 1024), _rows(tm, 1024), _rows(tm, 1024),
                   _rows(tm, 3 * CW), _rows(tm, 128)],
        out_shape=[jax.ShapeDtypeStruct((D, s), BF16), jax.ShapeDtypeStruct((s, 1024), BF16),
                   jax.ShapeDtypeStruct((s, 1024), BF16), jax.ShapeDtypeStruct((s, 1024), BF16),
                   jax.ShapeDtypeStruct((s, 3 * CW), F32), jax.ShapeDtypeStruct((s, 128), F32)],
        scratch_shapes=[pltpu.VMEM((SUBLANES, 128), F32)],
        compiler_params=_cparams(56, ("arbitrary",)),
    )(x, g1, wp, bfp, pq, pk, oq, ok)


def _attn_fwd(qp, kp, v, *, t):
    s = qp.shape[0]
    nq = s // t

    def body(q_ref, k_ref, v_ref, o_ref, lse_ref, mk_ref):
        qi = pl.program_id(1)
        row = lax.broadcasted_iota(jnp.int32, (t, t), 0)
        col = lax.broadcasted_iota(jnp.int32, (t, t), 1)
        lane = lax.broadcasted_iota(jnp.int32, (t, 128), 1)

        def head_step(hh, ki, carry, masked):
            m, acc = carry
            off = pl.multiple_of(ki * t, t)
            q = q_ref[:, HP * hh:HP * (hh + 1)]
            k = k_ref[pl.ds(off, t), HP * hh:HP * (hh + 1)]
            sc = lax.dot_general(q, k, NT, preferred_element_type=F32)
            if masked:
                sc = jnp.where(col <= row, sc, -1e30)
            mn = jnp.maximum(m, jnp.max(sc, axis=-1, keepdims=True))
            p = jnp.exp(sc - mn).astype(BF16)
            acc = jnp.exp(m - mn) * acc + jnp.dot(p, v_ref[pl.ds(off, t), HP * hh:HP * (hh + 1)],
                                                  preferred_element_type=F32)
            return mn, acc

        def step(ki, carry, masked):
            new = tuple(head_step(hh, ki, carry[hh], masked) for hh in range(2))
            mk_ref[ki] = jnp.where(lane < DH, jnp.broadcast_to(new[0][0], (t, 128)), jnp.broadcast_to(new[1][0], (t, 128)))
            return new

        init = (jnp.full((t, 1), -1e30, F32), jnp.zeros((t, 128), F32))
        carry = lax.fori_loop(0, qi, functools.partial(step, masked=False), (init, init))
        (m0, acc0), (m1, acc1) = step(qi, carry, True)
        l0, l1 = acc0[:, DH:DH + 1], acc1[:, DH:DH + 1]
        o_ref[...] = jnp.where(lane < DH, acc0 / l0, pltpu.roll(acc1 / l1, DH, axis=1))
        lse_ref[...] = jnp.where(lane < DH, jnp.broadcast_to(m0 + jnp.log(l0), (t, 128)),
                                 jnp.broadcast_to(m1 + jnp.log(l1), (t, 128)))

    return pl.pallas_call(
        body, name="attn_fwd", grid=(H // 2, nq),
        in_specs=[pl.BlockSpec((t, 2 * HP), lambda p, i: (i, p)),
                  pl.BlockSpec((s, 2 * HP), lambda p, i: (0, p)),
                  pl.BlockSpec((s, 2 * HP), lambda p, i: (0, p))],
        out_specs=[pl.BlockSpec((t, 128), lambda p, i: (i, p)), pl.BlockSpec((t, 128), lambda p, i: (i, p)),
                   pl.BlockSpec((nq, t, 128), lambda p, i: (0, i, p))],
        out_shape=[jax.ShapeDtypeStruct((s, AW), F32), jax.ShapeDtypeStruct((s, AW), F32),
                   jax.ShapeDtypeStruct((nq, s, AW), F32)],
        compiler_params=_cparams(48, ("arbitrary", "arbitrary")),
    )(qp, kp, v)


def _conv_taps(bcu_ref, halo_ref, first, tm):
    z = bcu_ref[:, CW:2 * CW] * bcu_ref[:, 2 * CW:3 * CW]
    zh = jnp.where(first, 0.0, halo_ref[:, CW:2 * CW] * halo_ref[:, 2 * CW:3 * CW])
    row = lax.broadcasted_iota(jnp.int32, (tm, CW), 0)
    z1 = jnp.where(row == 0, zh[7:8, :], pltpu.roll(z, 1, axis=0))
    z2 = jnp.where(row == 0, zh[6:7, :], jnp.where(row == 1, zh[7:8, :], pltpu.roll(z, 2, axis=0)))
    return z, z1, z2


def _halo_before(tm, width):
    return pl.BlockSpec((SUBLANES, width), lambda i: (jnp.maximum(i * (tm // SUBLANES) - 1, 0), 0))


def _mix_out(o, bcu, cw8, ga, gc, gsum, w_out, x, g_post, g_ffn_pre, *, tm):
    s = x.shape[0]

    def body(o_ref, bcu_ref, halo_ref, cw_ref, ga_ref, gc_ref, gs_ref, w_ref, x_ref, g_ref, gf_ref,
             merged_ref, y_ref, x2_ref, cv_ref, h2_ref):
        z, z1, z2 = _conv_taps(bcu_ref, halo_ref, pl.program_id(0) == 0, tm)
        cv = cw_ref[0:1, :] * z2 + cw_ref[1:2, :] * z1 + cw_ref[2:3, :] * z
        cv_ref[...] = cv
        conv = bcu_ref[:, 0:CW] * cv
        ov = o_ref[...]
        ra = lax.rsqrt(_split_dot(ov * ov, gs_ref[...]) * (1.0 / DH) + EPS)
        rc = lax.rsqrt(_split_dot(conv * conv, gs_ref[...]) * (1.0 / DH) + EPS)
        merged = jnp.concatenate([ov * ra * ga_ref[...], conv * rc * gc_ref[...]], axis=1).astype(BF16)
        merged_ref[...] = merged
        y = jnp.dot(merged, w_ref[...], preferred_element_type=F32)
        y_ref[...] = y
        x2 = x_ref[...] + _rms_fwd(y, g_ref[...])[0]
        x2_ref[...] = x2
        h2_ref[...] = _rms_fwd(x2, gf_ref[...])[0].astype(BF16)

    return pl.pallas_call(
        body, name="mix_out", grid=(s // tm,),
        in_specs=[_rows(tm, AW), _rows(tm, 3 * CW), _halo_before(tm, 3 * CW), _full((SUBLANES, CW)),
                  _full((1, AW)), _full((1, CW)), _full((CW, CW)), _resident((D, D)), _rows(tm, D), _full((1, D)),
                  _full((1, D))],
        out_specs=[_rows(tm, D), _rows(tm, D), _rows(tm, D), _rows(tm, CW), _rows(tm, D)],
        out_shape=[jax.ShapeDtypeStruct((s, D), BF16), jax.ShapeDtypeStruct((s, D), F32),
                   jax.ShapeDtypeStruct((s, D), F32), jax.ShapeDtypeStruct((s, CW), F32),
                   jax.ShapeDtypeStruct((s, D), BF16)],
        compiler_params=_cparams(48, ("arbitrary",)),
    )(o, bcu, bcu, cw8, ga, gc, gsum, w_out, x, g_post, g_ffn_pre)


def _ffn_fwd_loss(h2, wgu, wd, x2, target, g_post, *, tm):
    s = x2.shape[0]

    def body(h_ref, w_ref, wd_ref, x2_ref, t_ref, g_ref,
             gate_ref, up_ref, a_ref, dx3_ref, dff_ref, loss_ref, dg_ref):
        @pl.when(pl.program_id(0) == 0)
        def _():
            loss_ref[...] = jnp.zeros_like(loss_ref)
            dg_ref[...] = jnp.zeros_like(dg_ref)

        h = h_ref[...]
        ff = None
        for j in range(4):
            gate = jnp.dot(h, w_ref[0, j], preferred_element_type=F32)
            up = jnp.dot(h, w_ref[1, j], preferred_element_type=F32)
            gate_ref[j] = gate.astype(BF16)
            up_ref[j] = up.astype(BF16)
            act = (gate * jax.nn.sigmoid(gate) * up).astype(BF16)
            a_ref[j] = act
            part = jnp.dot(act, wd_ref[j], preferred_element_type=F32)
            ff = part if ff is None else ff + part
        out, n, r = _rms_fwd(ff, g_ref[...])
        e = x2_ref[...] + out - t_ref[...]
        loss_ref[...] += _fold8(e * e)
        dx3 = e * (1.0 / D)
        dx3_ref[...] = dx3
        dff, dg = _rms_bwd(dx3, n, r, g_ref[...])
        dff_ref[...] = dff.astype(BF16)
        dg_ref[...] += _fold8(dg)

    blk4 = pl.BlockSpec((4, tm, FB), lambda i: (0, i, 0))
    return pl.pallas_call(
        body, name="ffn_fwd_loss", grid=(s // tm,),
        in_specs=[_rows(tm, D), _resident((2, 4, D, FB)), _resident((4, FB, D)), _rows(tm, D), _rows(tm, D), _full((1, D))],
        out_specs=[blk4, blk4, blk4, _rows(tm, D), _rows(tm, D), _full((SUBLANES, D)), _full((SUBLANES, D))],
        out_shape=[jax.ShapeDtypeStruct((4, s, FB), BF16)] * 3
        + [jax.ShapeDtypeStruct((s, D), F32), jax.ShapeDtypeStruct((s, D), BF16),
           jax.ShapeDtypeStruct((SUBLANES, D), F32), jax.ShapeDtypeStruct((SUBLANES, D), F32)],
        compiler_params=_cparams(56, ("arbitrary",)),
    )(h2, wgu, wd, x2, target, g_post)


def _ffn_bwd(dff, wd, gate, up, wgu, x2, g_pre, dx3, y, g_post, *, tm):
    s = x2.shape[0]

    def body(dff_ref, wd_ref, gate_ref, up_ref, w_ref, x2_ref, gpre_ref, dx3_ref, y_ref, gpost_ref,
             dgu_ref, dx2_ref, dy_ref, dgpre_ref, dgpost_ref):
        @pl.when(pl.program_id(0) == 0)
        def _():
            dgpre_ref[...] = jnp.zeros_like(dgpre_ref)
            dgpost_ref[...] = jnp.zeros_like(dgpost_ref)

        dff = dff_ref[...]
        dh2 = None
        for j in range(4):
            da = lax.dot_general(dff, wd_ref[j], NT, preferred_element_type=F32)
            g = gate_ref[j].astype(F32)
            sg = jax.nn.sigmoid(g)
            dgate = (da * up_ref[j].astype(F32) * (sg * (1.0 + g * (1.0 - sg)))).astype(BF16)
            dup = (da * (g * sg)).astype(BF16)
            dgu_ref[0, j] = dgate
            dgu_ref[1, j] = dup
            part = (lax.dot_general(dgate, w_ref[0, j], NT, preferred_element_type=F32)
                    + lax.dot_general(dup, w_ref[1, j], NT, preferred_element_type=F32))
            dh2 = part if dh2 is None else dh2 + part
        _, n2, r2 = _rms_fwd(x2_ref[...], gpre_ref[...])
        dxn, dg = _rms_bwd(dh2, n2, r2, gpre_ref[...])
        dgpre_ref[...] += _fold8(dg)
        dx2 = dx3_ref[...] + dxn
        dx2_ref[...] = dx2
        _, ny, ry = _rms_fwd(y_ref[...], gpost_ref[...])
        dy, dg2 = _rms_bwd(dx2, ny, ry, gpost_ref[...])
        dy_ref[...] = dy.astype(BF16)
        dgpost_ref[...] += _fold8(dg2)

    blk4 = pl.BlockSpec((4, tm, FB), lambda i: (0, i, 0))
    return pl.pallas_call(
        body, name="ffn_bwd", grid=(s // tm,),
        in_specs=[_rows(tm, D), _resident((4, FB, D)), blk4, blk4, _resident((2, 4, D, FB)), _rows(tm, D), _full((1, D)),
                  _rows(tm, D), _rows(tm, D), _full((1, D))],
        out_specs=[pl.BlockSpec((2, 4, tm, FB), lambda i: (0, 0, i, 0)), _rows(tm, D), _rows(tm, D),
                   _full((SUBLANES, D)), _full((SUBLANES, D))],
        out_shape=[jax.ShapeDtypeStruct((2, 4, s, FB), BF16), jax.ShapeDtypeStruct((s, D), F32),
                   jax.ShapeDtypeStruct((s, D), BF16), jax.ShapeDtypeStruct((SUBLANES, D), F32),
                   jax.ShapeDtypeStruct((SUBLANES, D), F32)],
        compiler_params=_cparams(56, ("arbitrary",)),
    )(dff, wd, gate, up, wgu, x2, g_pre, dx3, y, g_post)


def _grad_matmul(a, b, *, ta, tb, ts, name):
    s, ka = a.shape
    nb = b.shape[1]
    ts = min(ts, s)
    nk = s // ts

    def body(a_ref, b_ref, o_ref, acc):
        k = pl.program_id(2)

        @pl.when(k == 0)
        def _():
            acc[...] = jnp.zeros_like(acc)

        acc[...] += lax.dot_general(a_ref[...], b_ref[...], TN, preferred_element_type=F32)

        @pl.when(k == nk - 1)
        def _():
            o_ref[...] = acc[...].astype(BF16)

    return pl.pallas_call(
        body, name=name, grid=(ka // ta, nb // tb, nk),
        in_specs=[pl.BlockSpec((ts, ta), lambda i, j, k: (k, i)), pl.BlockSpec((ts, tb), lambda i, j, k: (k, j))],
        out_specs=pl.BlockSpec((ta, tb), lambda i, j, k: (i, j)),
        out_shape=jax.ShapeDtypeStruct((ka, nb), BF16),
        scratch_shapes=[pltpu.VMEM((ta, tb), F32)],
        compiler_params=_cparams(48, ("arbitrary", "arbitrary", "arbitrary")),
    )(a, b)


def _grad_matmul_t(at, b, *, tb, name):
    ka, s = at.shape
    blocked = b.ndim == 3
    nb = b.shape[-1]
    steps = b.shape[0] if blocked else nb // tb
    width = nb if blocked else tb

    def body(a_ref, b_ref, o_ref):
        bv = b_ref[0] if blocked else b_ref[...]
        res = jnp.dot(a_ref[...], bv, preferred_element_type=F32).astype(BF16)
        if blocked:
            o_ref[0] = res
        else:
            o_ref[...] = res

    if blocked:
        b_spec = pl.BlockSpec((1, s, nb), lambda j: (j, 0, 0))
        o_spec = pl.BlockSpec((1, ka, nb), lambda j: (j, 0, 0))
        o_shape = jax.ShapeDtypeStruct((steps, ka, nb), BF16)
    else:
        b_spec = pl.BlockSpec((s, width), lambda j: (0, j))
        o_spec = pl.BlockSpec((ka, width), lambda j: (0, j))
        o_shape = jax.ShapeDtypeStruct((ka, nb), BF16)
    return pl.pallas_call(
        body, name=name, grid=(steps,),
        in_specs=[_resident((ka, s)), b_spec], out_specs=o_spec, out_shape=o_shape,
        compiler_params=_cparams(56, ("arbitrary",)),
    )(at, b)


GW_TILE = 256


def _grad_w_in(h1t, pieces):
    ka, s = h1t.shape
    widths = [p.shape[1] for p in pieces]
    assert all(w % GW_TILE == 0 for w in widths)
    first = [sum(widths[:i]) // GW_TILE for i in range(len(pieces))]
    count = [w // GW_TILE for w in widths]

    def body(a_ref, *refs):
        o_ref = refs[-1]
        j = pl.program_id(0)
        for ref, f0, n in zip(refs[:-1], first, count):
            @pl.when((j >= f0) & (j < f0 + n))
            def _(ref=ref):
                o_ref[...] = jnp.dot(a_ref[...], ref[...], preferred_element_type=F32).astype(BF16)

    def spec(f0, n):
        return pl.BlockSpec((s, GW_TILE), lambda j: (0, jnp.clip(j - f0, 0, n - 1)))

    return pl.pallas_call(
        body, name="grad_w_in", grid=(sum(count),),
        in_specs=[_resident((ka, s))] + [spec(f0, n) for f0, n in zip(first, count)],
        out_specs=pl.BlockSpec((ka, GW_TILE), lambda j: (0, j)),
        out_shape=jax.ShapeDtypeStruct((ka, sum(widths)), BF16),
        compiler_params=_cparams(56, ("arbitrary",)),
    )(h1t, *pieces)


def _grad_matmul_blocks(a, b, *, ts, name):
    nblk = a.shape[0] if a.ndim == 3 else b.shape[0]
    s = a.shape[-2]
    ka, nb = a.shape[-1], b.shape[-1]
    ts = min(ts, s)
    nk = s // ts

    def body(a_ref, b_ref, o_ref, acc):
        k = pl.program_id(1)

        @pl.when(k == 0)
        def _():
            acc[...] = jnp.zeros_like(acc)

        av = a_ref[0] if a.ndim == 3 else a_ref[...]
        bv = b_ref[0] if b.ndim == 3 else b_ref[...]
        acc[...] += lax.dot_general(av, bv, TN, preferred_element_type=F32)

        @pl.when(k == nk - 1)
        def _():
            o_ref[0] = acc[...].astype(BF16)

    def spec(arr, width):
        if arr.ndim == 3:
            return pl.BlockSpec((1, ts, width), lambda j, k: (j, k, 0))
        return pl.BlockSpec((ts, width), lambda j, k: (k, 0))

    return pl.pallas_call(
        body, name=name, grid=(nblk, nk),
        in_specs=[spec(a, ka), spec(b, nb)],
        out_specs=pl.BlockSpec((1, ka, nb), lambda j, k: (j, 0, 0)),
        out_shape=jax.ShapeDtypeStruct((nblk, ka, nb), BF16),
        scratch_shapes=[pltpu.VMEM((ka, nb), F32)],
        compiler_params=_cparams(48, ("arbitrary", "arbitrary")),
    )(a, b)


def _mix_bwd(dy, w_out, o, cv, bcu, ga, gc, gsum, *, tm):
    s = dy.shape[0]

    def group_norm_bwd(dn_out, v, g, gs):
        r = lax.rsqrt(_split_dot(v * v, gs) * (1.0 / DH) + EPS)
        n = v * r
        dn = dn_out * g
        return r * (dn - n * (_split_dot(dn * n, gs) * (1.0 / DH))), dn_out * n

    def body(dy_ref, w_ref, o_ref, cv_ref, bcu_ref, ga_ref, gc_ref, gs_ref,
             do_ref, dl_ref, dcv_ref, db_ref, dga_ref, dgc_ref):
        @pl.when(pl.program_id(0) == 0)
        def _():
            dga_ref[...] = jnp.zeros_like(dga_ref)
            dgc_ref[...] = jnp.zeros_like(dgc_ref)

        dm = lax.dot_general(dy_ref[...], w_ref[...], NT, preferred_element_type=F32)
        ov = o_ref[...]
        do, dga = group_norm_bwd(dm[:, 0:AW], ov, ga_ref[...], gs_ref[...])
        dob = do.astype(BF16)
        do_ref[...] = dob
        dl_ref[...] = _split_dot(dob.astype(F32) * ov, gs_ref[...])
        dga_ref[...] += _fold8(dga)
        gate_b = bcu_ref[:, 0:CW]
        cv = cv_ref[...]
        dconv, dgc = group_norm_bwd(dm[:, AW:D], gate_b * cv, gc_ref[...], gs_ref[...])
        dgc_ref[...] += _fold8(dgc)
        dcv_ref[...] = dconv * gate_b
        db_ref[...] = (dconv * cv).astype(BF16)

    return pl.pallas_call(
        body, name="mix_bwd", grid=(s // tm,),
        in_specs=[_rows(tm, D), _resident((D, D)), _rows(tm, AW), _rows(tm, CW), _rows(tm, 3 * CW),
                  _full((1, AW)), _full((1, CW)), _full((CW, CW))],
        out_specs=[_rows(tm, AW), _rows(tm, AW), _rows(tm, CW), _rows(tm, CW),
                   _full((SUBLANES, AW)), _full((SUBLANES, CW))],
        out_shape=[jax.ShapeDtypeStruct((s, AW), BF16), jax.ShapeDtypeStruct((s, AW), F32),
                   jax.ShapeDtypeStruct((s, CW), F32), jax.ShapeDtypeStruct((s, CW), BF16),
                   jax.ShapeDtypeStruct((SUBLANES, AW), F32), jax.ShapeDtypeStruct((SUBLANES, CW), F32)],
        compiler_params=_cparams(48, ("arbitrary",)),
    )(dy, w_out, o, cv, bcu, ga, gc, gsum)


def _conv_bwd(dcv, db, bcu, cw8, *, tm):
    s = dcv.shape[0]
    nt = s // tm

    def body(dcv_ref, nxt_ref, db_ref, bcu_ref, halo_ref, cw_ref, dbcu_ref, dw_ref):
        i = pl.program_id(0)

        @pl.when(i == 0)
        def _():
            dw_ref[...] = jnp.zeros_like(dw_ref)

        z, z1, z2 = _conv_taps(bcu_ref, halo_ref, i == 0, tm)
        d = dcv_ref[...]
        dw_ref[0] += _fold8(d * z2)
        dw_ref[1] += _fold8(d * z1)
        dw_ref[2] += _fold8(d * z)
        nx = jnp.where(i == nt - 1, 0.0, nxt_ref[...])
        row = lax.broadcasted_iota(jnp.int32, (tm, CW), 0)
        d1 = jnp.where(row == tm - 1, nx[0:1, :], pltpu.roll(d, tm - 1, axis=0))
        d2 = jnp.where(row == tm - 2, nx[0:1, :], jnp.where(row == tm - 1, nx[1:2, :], pltpu.roll(d, tm - 2, axis=0)))
        dz = cw_ref[2:3, :] * d + cw_ref[1:2, :] * d1 + cw_ref[0:1, :] * d2
        dbcu_ref[:, 0:CW] = db_ref[...]
        dbcu_ref[:, CW:2 * CW] = (dz * bcu_ref[:, 2 * CW:3 * CW]).astype(BF16)
        dbcu_ref[:, 2 * CW:3 * CW] = (dz * bcu_ref[:, CW:2 * CW]).astype(BF16)

    return pl.pallas_call(
        body, name="conv_bwd", grid=(nt,),
        in_specs=[_rows(tm, CW),
                  pl.BlockSpec((SUBLANES, CW), lambda i: (jnp.minimum((i + 1) * (tm // SUBLANES), s // SUBLANES - 1), 0)),
                  _rows(tm, CW), _rows(tm, 3 * CW), _halo_before(tm, 3 * CW), _full((SUBLANES, CW))],
        out_specs=[_rows(tm, 3 * CW), _full((3, SUBLANES, CW))],
        out_shape=[jax.ShapeDtypeStruct((s, 3 * CW), BF16), jax.ShapeDtypeStruct((3, SUBLANES, CW), F32)],
        compiler_params=_cparams(48, ("arbitrary",)),
    )(dcv, dcv, db, bcu, bcu, cw8)


def _attn_bwd(qp, kp, v, do, lse, dl, mk, *, t):
    s = qp.shape[0]
    nq = s // t

    def body(q_ref, k_ref, v_ref, do_ref, lse_ref, dl_ref, mk_ref, dq_ref, dk_ref, dv_ref, dkx_ref, dq_acc):
        ki = pl.program_id(1)

        @pl.when(ki == 0)
        def _():
            dq_acc[...] = jnp.zeros_like(dq_acc)

        row = lax.broadcasted_iota(jnp.int32, (t, t), 0)
        col = lax.broadcasted_iota(jnp.int32, (t, t), 1)
        lane = lax.broadcasted_iota(jnp.int32, (t, 128), 1)

        def head_step(hh, qi, carry, masked):
            dk, dv, cs = carry
            off = pl.multiple_of(qi * t, t)
            rows = pl.ds(off, t)
            kh = k_ref[:, HP * hh:HP * (hh + 1)]
            q = q_ref[rows, HP * hh:HP * (hh + 1)]
            m_col = mk_ref[0, rows, DH * hh:DH * hh + 1]
            scale = jnp.exp(m_col - lse_ref[rows, DH * hh:DH * hh + 1])
            do2 = do_ref[rows, :]
            dom = jnp.where(lane < DH, do2 if hh == 0 else pltpu.roll(do2, DH, axis=1), jnp.zeros((), BF16))
            sc = lax.dot_general(q, kh, NT, preferred_element_type=F32) - m_col
            if masked:
                sc = jnp.where(col <= row, sc, -1e30)
            pt = jnp.exp(sc).astype(BF16)
            dp = lax.dot_general(dom, v_ref[:, HP * hh:HP * (hh + 1)], NT, preferred_element_type=F32)
            ds32 = (pt.astype(F32) * scale) * (dp - dl_ref[rows, DH * hh:DH * hh + 1])
            ds = ds32.astype(BF16)
            cs = cs + _fold8(ds32)
            dv = dv + jnp.dot((dom.astype(F32) * scale).astype(BF16).T, pt, preferred_element_type=F32)
            dk = dk + jnp.dot(q.T, ds, preferred_element_type=F32)
            dq_acc[rows, HP * hh:HP * (hh + 1)] += jnp.dot(ds, kh, preferred_element_type=F32)
            return dk, dv, cs

        def step(qi, carry, masked):
            return tuple(head_step(hh, qi, carry[hh], masked) for hh in range(2))

        zero = (jnp.zeros((HP, t), F32), jnp.zeros((128, t), F32), jnp.zeros((SUBLANES, t), F32))
        carry = step(ki, (zero, zero), True)
        (dk0, dv0, cs0), (dk1, dv1, cs1) = lax.fori_loop(ki + 1, nq, functools.partial(step, masked=False), carry)
        def two_heads(a0, a1):
            return jnp.where(lane < DH, a0, pltpu.roll(a1, DH, axis=1))

        dk_ref[...] = two_heads(dk0.T, dk1.T).astype(BF16)
        dv_ref[...] = two_heads(dv0.T, dv1.T).astype(BF16)

        def as_column(cs):
            return lax.dot_general(cs, jnp.ones((SUBLANES, 128), F32), TN, precision=HIGHEST, preferred_element_type=F32)

        dkx_ref[...] = jnp.where(lane < DH, as_column(cs0), as_column(cs1))

        @pl.when(ki == nq - 1)
        def _():
            for c in range(s // t):
                rows = slice(c * t, (c + 1) * t)
                dq_ref[rows, :] = two_heads(dq_acc[rows, 0:HP], dq_acc[rows, HP:2 * HP]).astype(BF16)

    return pl.pallas_call(
        body, name="attn_bwd", grid=(H // 2, nq),
        in_specs=[pl.BlockSpec((s, 2 * HP), lambda p, i: (0, p)),
                  pl.BlockSpec((t, 2 * HP), lambda p, i: (i, p)),
                  pl.BlockSpec((t, 2 * HP), lambda p, i: (i, p)),
                  pl.BlockSpec((s, 128), lambda p, i: (0, p)),
                  pl.BlockSpec((s, 128), lambda p, i: (0, p)),
                  pl.BlockSpec((s, 128), lambda p, i: (0, p)),
                  pl.BlockSpec((1, s, 128), lambda p, i: (i, 0, p))],
        out_specs=[pl.BlockSpec((s, 128), lambda p, i: (0, p)),
                   pl.BlockSpec((t, 128), lambda p, i: (i, p)),
                   pl.BlockSpec((t, 128), lambda p, i: (i, p)),
                   pl.BlockSpec((t, 128), lambda p, i: (i, p))],
        out_shape=[jax.ShapeDtypeStruct((s, AW), BF16), jax.ShapeDtypeStruct((s, AW), BF16),
                   jax.ShapeDtypeStruct((s, AW), BF16), jax.ShapeDtypeStruct((s, AW), F32)],
        scratch_shapes=[pltpu.VMEM((s, 2 * HP), F32)],
        compiler_params=_cparams(56, ("arbitrary", "arbitrary")),
    )(qp, kp, v, do, lse, dl, mk)


def _forget_bwd(dkx, z, sel, *, tm):
    s = dkx.shape[0]
    nt = s // tm

    def body(dk_ref, z_ref, sel_ref, dfl_ref, dbf_ref, carry):
        @pl.when(pl.program_id(0) == 0)
        def _():
            carry[...] = jnp.zeros_like(carry)
            dbf_ref[...] = jnp.zeros_like(dbf_ref)

        dc = _split_dot(dk_ref[...], sel_ref[...])
        row = lax.broadcasted_iota(jnp.int32, (tm, tm), 0)
        col = lax.broadcasted_iota(jnp.int32, (tm, tm), 1)
        tri = (col >= row).astype(BF16)
        dlogf = _exact_dot01(tri, dc) + carry[0:1, :]
        carry[...] = jnp.broadcast_to(dlogf[0:1, :], carry.shape)
        dz = dlogf * (1.0 - jax.nn.sigmoid(z_ref[...]))
        dfl_ref[:, 0:128] = dz.astype(BF16)
        dfl_ref[:, 128:GW_TILE] = jnp.zeros((tm, GW_TILE - 128), BF16)
        dbf_ref[...] += _fold8(dz)

    rev = lambda i: (nt - 1 - i, 0)
    return pl.pallas_call(
        body, name="forget_bwd", grid=(nt,),
        in_specs=[pl.BlockSpec((tm, AW), rev), pl.BlockSpec((tm, 128), rev), _full((AW, 128))],
        out_specs=[pl.BlockSpec((tm, GW_TILE), rev), _full((SUBLANES, 128))],
        out_shape=[jax.ShapeDtypeStruct((s, GW_TILE), BF16), jax.ShapeDtypeStruct((SUBLANES, 128), F32)],
        scratch_shapes=[pltpu.VMEM((SUBLANES, 128), F32)],
        compiler_params=_cparams(48, ("arbitrary",)),
    )(dkx, z, sel)


def _in_proj_bwd(pieces, wp, x, g1, dx2, *, tm):
    s = x.shape[0]

    def body(q_ref, k_ref, v_ref, bcu_ref, f_ref, w_ref, x_ref, g_ref, dx2_ref, dx_ref, dg_ref):
        @pl.when(pl.program_id(0) == 0)
        def _():
            dg_ref[...] = jnp.zeros_like(dg_ref)

        dh = None
        for ref, (lo, hi) in zip((q_ref, k_ref, v_ref, bcu_ref, f_ref), PIECES):
            part = lax.dot_general(ref[...], w_ref[:, lo:hi], NT, preferred_element_type=F32)
            dh = part if dh is None else dh + part
        _, n, r = _rms_fwd(x_ref[...], g_ref[...])
        dxn, dg = _rms_bwd(dh, n, r, g_ref[...])
        dx_ref[...] = dx2_ref[...] + dxn
        dg_ref[...] += _fold8(dg)

    return pl.pallas_call(
        body, name="in_proj_bwd", grid=(s // tm,),
        in_specs=[_rows(tm, hi - lo) for lo, hi in PIECES] + [_resident((D, WP)), _rows(tm, D), _full((1, D)), _rows(tm, D)],
        out_specs=[_rows(tm, D), _full((SUBLANES, D))],
        out_shape=[jax.ShapeDtypeStruct((s, D), F32), jax.ShapeDtypeStruct((SUBLANES, D), F32)],
        compiler_params=_cparams(56, ("arbitrary",)),
    )(*pieces, wp, x, g1, dx2)


def _position():
    return lax.axis_index("x"), lax.axis_index("y"), lax.axis_index("c")


ANY = pl.BlockSpec(memory_space=pl.ANY)


def _all_gather(shards):
    n = len(shards)

    def body(*refs):
        x_refs, out_refs = refs[:n], refs[n:2 * n]
        send_sems, recv_sems, local_sems = refs[2 * n:]
        x, y, c = _position()
        me, sibling = (x, y, c), (x, y, 1 - c)
        chips = [(1 - x, y), (x, 1 - y), (1 - x, 1 - y)]

        def copy(a, k, block, to, own=False):
            slot = out_refs[a].at[4 * block[0] + 2 * block[1] + block[2]]
            return pltpu.make_async_remote_copy(
                src_ref=x_refs[a] if own else slot, dst_ref=slot,
                send_sem=send_sems.at[7 * a + k], recv_sem=recv_sems.at[7 * a + k], device_id=to, device_id_type=MESH_ID)

        mine = [pltpu.make_async_copy(x_refs[a], out_refs[a].at[4 * x + 2 * y + c], local_sems.at[a]) for a in range(n)]
        for cp in mine:
            cp.start()
        first = []
        for a in range(n):
            first.append(copy(a, 0, me, sibling, own=True))
            first += [copy(a, 1 + j, me, (*chip, c), own=True) for j, chip in enumerate(chips)]
        for cp in first:
            cp.start()
        passed = []
        for j, chip in enumerate(chips):
            for a in range(n):
                copy(a, 1 + j, (*chip, c), me).wait_recv()
                fwd = copy(a, 4 + j, (*chip, c), sibling)
                fwd.start()
                passed.append(fwd)
        for a in range(n):
            copy(a, 0, sibling, me).wait_recv()
            for j, chip in enumerate(chips):
                copy(a, 4 + j, (*chip, 1 - c), me).wait_recv()
        for cp in first + passed:
            cp.wait_send()
        for cp in mine:
            cp.wait()

    return pl.pallas_call(
        body, name="all_gather_weights",
        out_shape=[jax.ShapeDtypeStruct((NDEV,) + sh.shape, sh.dtype) for sh in shards],
        in_specs=[ANY] * n, out_specs=[ANY] * n,
        scratch_shapes=[pltpu.SemaphoreType.DMA((7 * n,)), pltpu.SemaphoreType.DMA((7 * n,)), pltpu.SemaphoreType.DMA((n,))],
    )(*shards)


def _pair_exchange(grads):
    n = len(grads)

    def body(*refs):
        g_refs, out_refs = refs[:n], refs[n:2 * n]
        send_sems, recv_sems = refs[2 * n:]
        x, y, c = _position()
        copies = [pltpu.make_async_remote_copy(
            src_ref=g_refs[a].at[:, pl.ds(1 - c, 1)], dst_ref=out_refs[a], send_sem=send_sems.at[a],
            recv_sem=recv_sems.at[a], device_id=(x, y, 1 - c), device_id_type=MESH_ID) for a in range(n)]
        for cp in copies:
            cp.start()
        for cp in copies:
            cp.wait()

    return pl.pallas_call(
        body, name="grad_pair_exchange",
        out_shape=[jax.ShapeDtypeStruct((4, 1) + g.shape[2:], g.dtype) for g in grads],
        in_specs=[ANY] * n, out_specs=[ANY] * n,
        scratch_shapes=[pltpu.SemaphoreType.DMA((n,)), pltpu.SemaphoreType.DMA((n,))],
    )(*grads)


def _pair_sum(g, got, idx, *, tr, name):
    r, c = g.shape[2:]

    def body(idx_ref, g_ref, got_ref, pb_ref, own_ref):
        p = g_ref[0, 0].astype(F32) + got_ref[0, 0].astype(F32)
        pb_ref[0] = p.astype(BF16)

        @pl.when(pl.program_id(1) == idx_ref[1])
        def _():
            own_ref[...] = p

    return pl.pallas_call(
        body, name=name,
        grid_spec=pltpu.PrefetchScalarGridSpec(
            num_scalar_prefetch=1, grid=(r // tr, 4),
            in_specs=[pl.BlockSpec((1, 1, tr, c), lambda i, j, idx: (j, idx[0], i, 0)),
                      pl.BlockSpec((1, 1, tr, c), lambda i, j, idx: (j, 0, i, 0))],
            out_specs=[pl.BlockSpec((1, tr, c), lambda i, j, idx: (j, i, 0)),
                       pl.BlockSpec((tr, c), lambda i, j, idx: (i, 0))]),
        out_shape=[jax.ShapeDtypeStruct((4, r, c), BF16), jax.ShapeDtypeStruct((r, c), F32)],
        compiler_params=_cparams(32, ("arbitrary", "arbitrary")),
    )(idx, g, got)


HBM = pl.BlockSpec(memory_space=pltpu.HBM)
SEM = pl.BlockSpec(memory_space=pltpu.SEMAPHORE)
DATAFLOW = pltpu.SideEffectType.DATAFLOW_SIDE_EFFECTING


PEERS = {"gather": NDEV - 1, "scatter": NDEV - 1, "chips": 3}


def _exchange_copies(src_refs, land_refs, send_sems, recv_sems, mode):
    x, y, c = _position()
    me, my_chip = 4 * x + 2 * y + c, 2 * x + y
    npeers = PEERS[mode]
    copies = []
    for a, (s_ref, l_ref) in enumerate(zip(src_refs, land_refs)):
        for k in range(npeers):
            if mode == "chips":
                px, py, pc = x ^ ((k + 1) >> 1), y ^ ((k + 1) & 1), c
                src, dst = s_ref.at[2 * px + py], l_ref.at[my_chip]
            else:
                px, py, pc = x ^ ((k + 1) >> 2), y ^ (((k + 1) >> 1) & 1), c ^ ((k + 1) & 1)
                src, dst = (s_ref.at[4 * px + 2 * py + pc] if mode == "scatter" else s_ref), l_ref.at[me]
            copies.append(pltpu.make_async_remote_copy(
                src_ref=src, dst_ref=dst, send_sem=send_sems.at[npeers * a + k], recv_sem=recv_sems.at[npeers * a + k],
                device_id=(px, py, pc), device_id_type=MESH_ID))
    return copies


def _exchange_start(srcs, lands, after, *, mode, name):
    n = len(srcs)
    nsem = PEERS[mode] * n

    def body(*refs):
        token = refs[-1]
        for cp in _exchange_copies(refs[:n], refs[n:2 * n], refs[2 * n + 1], refs[2 * n + 2], mode):
            cp.start()
        token[...] = jnp.zeros_like(token)

    arrays = list(srcs) + list(lands)
    outs = pl.pallas_call(
        body, name=name,
        out_shape=(pltpu.SemaphoreType.DMA((nsem,)), pltpu.SemaphoreType.DMA((nsem,)),
                   *[pltpu.HBM(a.shape, a.dtype) for a in arrays], jax.ShapeDtypeStruct((SUBLANES, LANES), F32)),
        in_specs=[HBM] * (2 * n) + [ANY],
        out_specs=(SEM, SEM, *[HBM] * (2 * n), pl.BlockSpec(memory_space=pltpu.VMEM)),
        input_output_aliases={i: 2 + i for i in range(2 * n)},
        compiler_params=pltpu.CompilerParams(has_side_effects=DATAFLOW),
    )(*[pltpu.with_memory_space_constraint(a, pltpu.HBM) for a in arrays], after)
    return outs[0], outs[1], outs[2:2 + n], outs[2 + n:2 + 2 * n], outs[-1]


def _exchange_wait(send_sems, recv_sems, srcs, lands, after, *, mode, name):
    n = len(srcs)

    def body(*refs):
        for cp in _exchange_copies(refs[:n], refs[n:2 * n], refs[2 * n], refs[2 * n + 1], mode):
            cp.wait_send()
            cp.wait_recv()

    arrays = list(srcs) + list(lands)
    outs = pl.pallas_call(
        body, name=name,
        out_shape=tuple(pltpu.HBM(a.shape, a.dtype) for a in arrays),
        in_specs=[HBM] * (2 * n) + [SEM, SEM, ANY],
        out_specs=tuple([HBM] * (2 * n)),
        input_output_aliases={i: i for i in range(2 * n)},
        compiler_params=pltpu.CompilerParams(has_side_effects=DATAFLOW),
    )(*arrays, send_sems, recv_sems, after)
    return outs[n:]


def _own_slot(value, me):
    return lax.dynamic_update_index_in_dim(lax.empty((NDEV,) + value.shape, value.dtype), value, me, 0)


def _small_all_reduce(parts):
    def body(gmp_ref, gmo_ref, gfp_ref, gfo_ref, ga_ref, gc_ref, dw_ref, bf_ref, loss_ref,
             out_ref, buf, send_sems, recv_sems):
        x, y, c = _position()
        me = 4 * x + 2 * y + c

        def colsum(v):
            return jnp.sum(v, axis=0, keepdims=True)

        loss = jnp.sum(colsum(loss_ref[...]), axis=1, keepdims=True) * (0.5 / D)
        rows = [colsum(gmp_ref[...]), colsum(gmo_ref[...]), colsum(gfp_ref[...]), colsum(gfo_ref[...]),
                jnp.concatenate([colsum(ga_ref[...]), colsum(gc_ref[...])], axis=1),
                jnp.concatenate([colsum(dw_ref[0]), colsum(dw_ref[1])], axis=1),
                jnp.concatenate([colsum(dw_ref[2]), colsum(bf_ref[...]), jnp.broadcast_to(loss, (1, 128)),
                                 jnp.zeros((1, 256), F32)], axis=1),
                jnp.zeros((1, D), F32)]
        buf[me] = jnp.concatenate(rows, axis=0)
        copies = []
        for mm in range(1, NDEV):
            peer = (x ^ (mm >> 2), y ^ ((mm >> 1) & 1), c ^ (mm & 1))
            copies.append(pltpu.make_async_remote_copy(
                src_ref=buf.at[me], dst_ref=buf.at[me], send_sem=send_sems.at[mm - 1], recv_sem=recv_sems.at[mm - 1],
                device_id=peer, device_id_type=MESH_ID))
        for cp in copies:
            cp.start()
        for cp in copies:
            cp.wait_recv()
        for cp in copies:
            cp.wait_send()
        acc = buf[0]
        for d in range(1, NDEV):
            acc = acc + buf[d]
        out_ref[...] = acc

    vm = pl.BlockSpec(memory_space=pltpu.VMEM)
    return pl.pallas_call(
        body, name="small_all_reduce",
        out_shape=jax.ShapeDtypeStruct((SUBLANES, D), F32),
        in_specs=[vm] * len(parts), out_specs=vm,
        scratch_shapes=[pltpu.VMEM((NDEV, SUBLANES, D), F32), pltpu.SemaphoreType.DMA((7,)), pltpu.SemaphoreType.DMA((7,))],
    )(*parts)


def _adam_update(w, g, m, v):
    nm = ADAM_B1 * m + (1.0 - ADAM_B1) * g
    nv = ADAM_B2 * v + (1.0 - ADAM_B2) * (g * g)
    m_hat = nm / (1.0 - ADAM_B1 ** ADAM_STEP)
    v_hat = nv / (1.0 - ADAM_B2 ** ADAM_STEP)
    return -ADAM_LR * (m_hat / (jnp.sqrt(v_hat) + ADAM_EPS) + ADAM_WD * w), nm, nv


def _adamw(w, g, m, v, *, tr, name):
    rows, cols = w.shape

    def body(w_ref, g_ref, m_ref, v_ref, d_ref, nm_ref, nv_ref):
        d_ref[...], nm_ref[...], nv_ref[...] = _adam_update(w_ref[...], g_ref[...], m_ref[...], v_ref[...])

    spec = pl.BlockSpec((tr, cols), lambda i: (i, 0))
    return pl.pallas_call(
        body, name=name, grid=(rows // tr,),
        in_specs=[spec] * 4, out_specs=[spec] * 3,
        out_shape=[jax.ShapeDtypeStruct((rows, cols), F32)] * 3,
        compiler_params=_cparams(32, ("arbitrary",)),
    )(w, g, m, v)


def _chip_sum_adamw(got, own, idx, w, m, v, *, tr, name):
    rows, cols = w.shape

    def body(idx_ref, got_ref, own_ref, w_ref, m_ref, v_ref, g_ref, d_ref, nm_ref, nv_ref):
        g = jnp.zeros((tr, cols), F32)
        for j in range(4):
            g = g + jnp.where(idx_ref[1] == j, own_ref[...], got_ref[j].astype(F32))
        g_ref[...] = g
        d_ref[...], nm_ref[...], nv_ref[...] = _adam_update(w_ref[...], g, m_ref[...], v_ref[...])

    spec = pl.BlockSpec((tr, cols), lambda i, idx: (i, 0))
    return pl.pallas_call(
        body, name=name,
        grid_spec=pltpu.PrefetchScalarGridSpec(
            num_scalar_prefetch=1, grid=(rows // tr,),
            in_specs=[pl.BlockSpec((4, tr, cols), lambda i, idx: (0, i, 0)), spec, spec, spec, spec],
            out_specs=[spec] * 4),
        out_shape=[jax.ShapeDtypeStruct((rows, cols), F32)] * 4,
        compiler_params=_cparams(32, ("arbitrary",)),
    )(idx, got, own, w, m, v)


def _device_sum_adamw(land, w, m, v, *, tr, name):
    rows, cols = w.shape

    def body(land_ref, w_ref, m_ref, v_ref, g_ref, d_ref, nm_ref, nv_ref):
        g = land_ref[0].astype(F32)
        for dev in range(1, NDEV):
            g = g + land_ref[dev].astype(F32)
        g_ref[...] = g
        d_ref[...], nm_ref[...], nv_ref[...] = _adam_update(w_ref[...], g, m_ref[...], v_ref[...])

    spec = pl.BlockSpec((tr, cols), lambda i: (i, 0))
    return pl.pallas_call(
        body, name=name, grid=(rows // tr,),
        in_specs=[pl.BlockSpec((NDEV, tr, cols), lambda i: (0, i, 0)), spec, spec, spec],
        out_specs=[spec] * 4,
        out_shape=[jax.ShapeDtypeStruct((rows, cols), F32)] * 4,
        compiler_params=_cparams(32, ("arbitrary",)),
    )(land, w, m, v)


def _placement_constants():
    j = jnp.arange(128)[:, None]
    lane = jnp.arange(1024)[None, :]
    head, sub = lane // HP, lane % HP
    piece, jh = j // H, j % H
    valid = (j < 3 * H) & (jh == head)
    pq = jnp.where(valid & (sub == DH + piece), 1.0, 0.0).astype(BF16)
    pk = jnp.where(valid & (sub == DH + 3 + piece), -1.0, 0.0).astype(BF16)
    oq = jnp.where((sub >= DH + 3) & (sub < DH + 6), 1.0, 0.0).astype(F32)
    ok = jnp.where((sub >= DH) & (sub < DH + 3), 1.0, 0.0).astype(F32)
    r = jnp.arange(AW)[:, None]
    cc = jnp.arange(128)[None, :]
    sel = jnp.where((r % DH == 3) & (r // DH == cc), -1.0, 0.0).astype(BF16)
    gi = jnp.arange(CW)
    gsum = (gi[:, None] // DH == gi[None, :] // DH).astype(BF16)
    return pq, pk, oq, ok, sel, gsum


def _local_step(xs, tgt, wp, late_weights, cw8, bfp, g_attn_out, g_conv_out,
                g_mix_pre, g_mix_post, g_ffn_pre, g_ffn_post, early_grads=None, last_grad=None):
    pq, pk, oq, ok, sel, gsum = _placement_constants()
    h1t, qp, kp, vv, bcu, zf = _in_proj(xs, g_mix_pre, wp, bfp, pq, pk, oq, ok, tm=512)
    o, lse, mk = _attn_fwd(qp, kp, vv, t=512)
    w_out_f, wgu, wd = late_weights(lse)
    merged, y, x2, cv, h2 = _mix_out(o, bcu, cw8, g_attn_out, g_conv_out, gsum, w_out_f, xs, g_mix_post, g_ffn_pre, tm=512)
    gate, up, act, dx3, dff, loss_p, dg_ffn_post = _ffn_fwd_loss(h2, wgu, wd, x2, tgt, g_ffn_post, tm=512)

    dgu, dx2, dy, dg_ffn_pre, dg_mix_post = _ffn_bwd(dff, wd, gate, up, wgu, x2, g_ffn_pre, dx3, y, g_mix_post, tm=256)
    dw_down = _grad_matmul_blocks(act, dff, ts=4096, name="grad_w_down")
    dw_gu = _grad_matmul_blocks(dgu.reshape(NDEV, -1, FB), h2, ts=4096, name="grad_w_gate_up")
    dw_out = _grad_matmul(merged, dy, ta=1024, tb=1024, ts=2048, name="grad_w_out")
    token = early_grads(dw_out, dw_gu, dw_down) if early_grads is not None else None
    ga = g_attn_out if token is None else g_attn_out + token[0:1, 0:1]
    do, dl, dcv, db, dg_attn, dg_conv = _mix_bwd(dy, w_out_f, o, cv, bcu, ga, g_conv_out, gsum, tm=512)
    dbcu, dtaps = _conv_bwd(dcv, db, bcu, cw8, tm=512)
    dqp, dkp, dv, dkx = _attn_bwd(qp, kp, vv, do, lse, dl, mk, t=512)
    dfl, dbf = _forget_bwd(dkx, zf, sel, tm=512)
    pieces = (dqp, dkp, dv, dbcu, dfl)
    dwp = _grad_w_in(h1t, pieces)
    token = last_grad(dwp) if last_grad is not None else None
    g1 = g_mix_pre if token is None else g_mix_pre + token[0:1, 0:1]
    grad_x, dg_mix_pre = _in_proj_bwd(pieces, wp, xs, g1, dx2, tm=512)
    return (grad_x, dwp, dw_out, dw_gu, dw_down, dg_mix_pre, dg_mix_post, dg_ffn_pre, dg_ffn_post, dg_attn, dg_conv,
            dtaps, dbf, loss_p)


BIG_TILES = {"w_in": 256, "w_out": 128, "w_gate_up": 176, "w_down": 176}


def kernel(x, w_in, b_forget, conv_w, g_attn_out, g_conv_out, w_out, g_mix_pre, g_mix_post, w_gate_up, w_down, g_ffn_pre, g_ffn_post, loss_target, m_w_in, m_b_forget, m_conv_w, m_g_attn_out, m_g_conv_out, m_w_out, m_g_mix_pre, m_g_mix_post, m_w_gate_up, m_w_down, m_g_ffn_pre, m_g_ffn_post, v_w_in, v_b_forget, v_conv_w, v_g_attn_out, v_g_conv_out, v_w_out, v_g_mix_pre, v_g_mix_post, v_w_gate_up, v_w_down, v_g_ffn_pre, v_g_ffn_post):
    xc, yc, cc = _position()
    my_chip = 2 * xc + yc
    me = 2 * my_chip + cc
    idx = jnp.stack([cc, my_chip]).astype(jnp.int32)
    tables = _in_layout_tables()
    pad_in = lambda a: jnp.pad(a, ((0, 0), (0, IN_PAD - IN_COLS)))

    g_in, g_taps = _all_gather([pad_in(w_in[0]).astype(BF16), conv_w[0]])
    wp = _assemble_w_in(g_in, tables, tr=256)
    cw8 = jnp.pad(g_taps.transpose(1, 0, 2).reshape(3, CW), ((0, SUBLANES - 3), (0, 0)))

    late = [w_out[0].astype(BF16), w_gate_up[0].astype(BF16), w_down[0].astype(BF16)]
    ssem, rsem, late_thru, land_thru, token = _exchange_start(
        late, [_own_slot(s, me) for s in late], g_in, mode="gather", name="gather_late_start")
    bfp = jnp.pad(b_forget, ((0, 0), (0, 128 - H))) + token[0:1, :]

    def late_weights(after):
        l_out, l_gu, l_down = _exchange_wait(ssem, rsem, late_thru, land_thru, after, mode="gather", name="gather_late_wait")
        return l_out.reshape(D, D), l_gu.reshape(2, 4, D, FB), l_down.reshape(4, FB, D)

    early = {}

    def early_grads(dw_out, dw_gu, dw_down):
        srcs = [dw_out.reshape(NDEV, D // NDEV, D), dw_gu, dw_down.reshape(NDEV, DFF // NDEV, D)]
        lands = [_own_slot(lax.dynamic_index_in_dim(s, me, 0, keepdims=False), me) for s in srcs]
        early["handles"] = _exchange_start(srcs, lands, dw_out, mode="scatter", name="scatter_early_start")
        return early["handles"][4]

    last = {}

    def last_grad(dwp):
        g_w_in = _disassemble_w_in(dwp, tables, tr=256).reshape(4, 2, D, IN_PAD)
        (from_sibling,) = _pair_exchange([g_w_in])
        pair_b, last["own"] = _pair_sum(g_w_in, from_sibling, idx, tr=BIG_TILES["w_in"], name="grad_pair_sum_w_in")
        land = lax.dynamic_update_index_in_dim(lax.empty(pair_b.shape, pair_b.dtype),
                                               lax.dynamic_index_in_dim(pair_b, my_chip, 0, keepdims=False), my_chip, 0)
        last["handles"] = _exchange_start([pair_b], [land], last["own"], mode="chips", name="chips_w_in_start")
        return last["handles"][4]

    (grad_x, dwp, dw_out, dw_gu, dw_down, dg_mix_pre, dg_mix_post, dg_ffn_pre, dg_ffn_post, dg_attn, dg_conv,
     dtaps, dbf, loss_p) = _local_step(x[0], loss_target[0], wp, late_weights, cw8, bfp, g_attn_out, g_conv_out,
                                        g_mix_pre, g_mix_post, g_ffn_pre, g_ffn_post, early_grads, last_grad)

    e_ssem, e_rsem, e_srcs, e_lands, _ = early["handles"]
    land_out, land_gu, land_down = _exchange_wait(e_ssem, e_rsem, e_srcs, e_lands, dg_mix_pre, mode="scatter",
                                                  name="scatter_early_wait")
    res = {}
    big = {"w_out": (land_out, w_out[0], m_w_out[0], v_w_out[0]),
           "w_gate_up": (land_gu, w_gate_up[0].T, m_w_gate_up[0].T, v_w_gate_up[0].T),
           "w_down": (land_down, w_down[0], m_w_down[0], v_w_down[0])}
    for name, (land, w, m, v) in big.items():
        outs = _device_sum_adamw(land, w, m, v, tr=BIG_TILES[name], name="adamw_" + name)
        res[name] = [(o.T if name == "w_gate_up" else o)[None] for o in outs]
    c_ssem, c_rsem, c_srcs, c_lands, _ = last["handles"]
    after = sum(res[n][1][0, :SUBLANES, :LANES] for n in big)
    (from_chips,) = _exchange_wait(c_ssem, c_rsem, c_srcs, c_lands, after, mode="chips", name="chips_w_in_wait")
    outs = _chip_sum_adamw(from_chips, last["own"], idx, pad_in(w_in[0]), pad_in(m_w_in[0]), pad_in(v_w_in[0]),
                           tr=BIG_TILES["w_in"], name="adamw_w_in")
    res["w_in"] = [o[:, :IN_COLS][None] for o in outs]

    small = _small_all_reduce([dg_mix_pre, dg_mix_post, dg_ffn_pre, dg_ffn_post, dg_attn, dg_conv, dtaps, dbf, loss_p])
    taps_full = jnp.concatenate([small[5:6, :CW], small[5:6, CW:], small[6:7, :CW]], axis=0)
    small_grads = {
        "b_forget": small[6:7, CW:CW + H], "conv_w": lax.dynamic_slice(taps_full, (0, me * 64), (3, 64)),
        "g_attn_out": small[4:5, :AW], "g_conv_out": small[4:5, AW:], "g_mix_pre": small[0:1], "g_mix_post": small[1:2],
        "g_ffn_pre": small[2:3], "g_ffn_post": small[3:4]}
    loss = small[6, CW + 128]
    smalls = {"b_forget": (b_forget, m_b_forget, v_b_forget), "conv_w": (conv_w[0], m_conv_w[0], v_conv_w[0]),
              "g_attn_out": (g_attn_out, m_g_attn_out, v_g_attn_out), "g_conv_out": (g_conv_out, m_g_conv_out, v_g_conv_out),
              "g_mix_pre": (g_mix_pre, m_g_mix_pre, v_g_mix_pre), "g_mix_post": (g_mix_post, m_g_mix_post, v_g_mix_post),
              "g_ffn_pre": (g_ffn_pre, m_g_ffn_pre, v_g_ffn_pre), "g_ffn_post": (g_ffn_post, m_g_ffn_post, v_g_ffn_post)}
    for name, (w, m, v) in smalls.items():
        g = small_grads[name]
        outs = [g] + list(_adamw(w, g, m, v, tr=w.shape[0], name="adamw_" + name))
        res[name] = [o[None] for o in outs] if name == "conv_w" else outs

    order = ["w_in", "b_forget", "conv_w", "g_attn_out", "g_conv_out", "w_out", "g_mix_pre", "g_mix_post",
             "w_gate_up", "w_down", "g_ffn_pre", "g_ffn_post"]
    outs = [loss, grad_x[None]]
    for k in range(4):
        outs += [res[n][k] for n in order]
    return tuple(outs)
```

```python
import functools

import numpy as np

import jax
import jax.numpy as jnp
from jax import lax
from jax.experimental import pallas as pl
from jax.experimental.pallas import tpu as pltpu

F32 = jnp.float32
BF16 = jnp.bfloat16
HIGHEST = lax.Precision.HIGHEST
MESH_ID = pl.DeviceIdType.MESH

D = 1024
H = 8
DH = 64
AW = 512
CW = 512
DFF = 2816
FB = DFF // 4
HP = 128
OFF_Q, OFF_K, OFF_V, OFF_BCU, OFF_F = 0, 512, 1024, 1536, 3072
WP = OFF_F + 128
PIECES = ((OFF_Q, OFF_K), (OFF_K, OFF_V), (OFF_V, OFF_BCU), (OFF_BCU, OFF_F), (OFF_F, WP))
EPS = 1e-6
NDEV = 8
LANES = 128
SUBLANES = 8
IN_COLS = 385
IN_PAD = 512
WIN = 640
ADAM_LR, ADAM_B1, ADAM_B2, ADAM_EPS, ADAM_WD, ADAM_STEP = 0.001, 0.9, 0.999, 1e-08, 0.01, 10

NT = (((1,), (1,)), ((), ()))
TN = (((0,), (0,)), ((), ()))


def _cparams(vmem_mb=None, sem=None):
    kw = {}
    if vmem_mb is not None:
        kw["vmem_limit_bytes"] = vmem_mb << 20
    if sem is not None:
        kw["dimension_semantics"] = sem
    return pltpu.CompilerParams(**kw)


def _full(shape):
    return pl.BlockSpec(shape, lambda *_: (0,) * len(shape))


def _resident(shape):
    return pl.BlockSpec(shape, lambda *_: (0,) * len(shape), pipeline_mode=pl.Buffered(1))


def _rows(tm, width):
    return pl.BlockSpec((tm, width), lambda i: (i, 0))


def _fold8(v):
    r, w = v.shape
    return jnp.sum(v.reshape(r // SUBLANES, SUBLANES, w), axis=0)


def _split_dot(v, m01):
    hi = v.astype(BF16)
    lo = (v - hi.astype(F32)).astype(BF16)
    return (jnp.dot(hi, m01, preferred_element_type=F32)
            + jnp.dot(lo, m01, preferred_element_type=F32))


def _exact_dot01(m01, v):
    p1 = v.astype(BF16)
    r1 = v - p1.astype(F32)
    p2 = r1.astype(BF16)
    p3 = (r1 - p2.astype(F32)).astype(BF16)
    return (jnp.dot(m01, p1, preferred_element_type=F32) + jnp.dot(m01, p2, preferred_element_type=F32)
            + jnp.dot(m01, p3, preferred_element_type=F32))


def _rms_fwd(v, g):
    r = lax.rsqrt(jnp.mean(v * v, axis=-1, keepdims=True) + EPS)
    n = v * r
    return n * g, n, r


def _rms_bwd(do, n, r, g):
    dn = do * g
    return r * (dn - n * jnp.mean(dn * n, axis=-1, keepdims=True)), do * n


def _padded_column(n):
    if n < AW:
        return OFF_Q + n, 0.125
    if n < 3 * AW:
        return n, 1.0
    if n < 3 * AW + H:
        return OFF_F + n - 3 * AW, 1.0
    return OFF_BCU + n - 3 * AW - H, 1.0


def _in_layout_tables():
    dest = -np.ones((IN_PAD, LANES), np.int32)
    dest_f = -np.ones((IN_PAD, LANES), np.int32)
    scale = np.zeros((IN_PAD, LANES), np.float32)
    starts = []
    for k in range(NDEV):
        cols = [_padded_column(IN_COLS * k + j) for j in range(IN_COLS)]
        main = [c for c, _ in cols if c < OFF_F]
        ws = min((min(main) // LANES) * LANES, OFF_F - WIN)
        assert ws <= min(main) and max(main) < ws + WIN
        starts.append(ws)
        for j, (c, sc) in enumerate(cols):
            scale[j, k] = sc
            if c < OFF_F:
                dest[j, k] = c - ws
            else:
                dest_f[j, k] = c - OFF_F
    f_shards = tuple(k for k in range(NDEV) if (dest_f[:, k] >= 0).any())
    return tuple(starts), f_shards, jnp.asarray(dest), jnp.asarray(dest_f), jnp.asarray(scale)


def _perm(dest_ref, scale_ref, k, width):
    lane = lax.broadcasted_iota(jnp.int32, (IN_PAD, width), 1)
    return jnp.where(dest_ref[:, k:k + 1] == lane, scale_ref[:, k:k + 1], 0.0).astype(BF16)


def _assemble_w_in(blocks, tables, *, tr):
    starts, f_shards, dest, dest_f, scale = tables

    def body(b_ref, dest_ref, destf_ref, scale_ref, o_ref):
        o_ref[...] = jnp.zeros_like(o_ref)
        for k in range(NDEV):
            b = b_ref[k]
            ws = starts[k]
            part = jnp.dot(b, _perm(dest_ref, scale_ref, k, WIN), preferred_element_type=F32)
            o_ref[:, ws:ws + WIN] += part.astype(BF16)
            if k in f_shards:
                part = jnp.dot(b, _perm(destf_ref, scale_ref, k, 128), preferred_element_type=F32)
                o_ref[:, OFF_F:WP] += part.astype(BF16)

    tab = _full((IN_PAD, LANES))
    return pl.pallas_call(
        body, name="assemble_w_in", grid=(D // tr,),
        in_specs=[pl.BlockSpec((NDEV, tr, IN_PAD), lambda i: (0, i, 0)), tab, tab, tab],
        out_specs=_rows(tr, WP),
        out_shape=jax.ShapeDtypeStruct((D, WP), BF16),
        compiler_params=_cparams(48, ("arbitrary",)),
    )(blocks, dest, dest_f, scale)


def _disassemble_w_in(dwp, tables, *, tr):
    starts, f_shards, dest, dest_f, scale = tables
    width = dwp.shape[1]

    def body(g_ref, dest_ref, destf_ref, scale_ref, o_ref):
        for k in range(NDEV):
            ws = starts[k]
            acc = lax.dot_general(g_ref[:, ws:ws + WIN], _perm(dest_ref, scale_ref, k, WIN), NT, preferred_element_type=F32)
            if k in f_shards:
                acc = acc + lax.dot_general(g_ref[:, OFF_F:WP], _perm(destf_ref, scale_ref, k, 128), NT,
                                            preferred_element_type=F32)
            o_ref[k] = acc.astype(BF16)

    tab = _full((IN_PAD, LANES))
    return pl.pallas_call(
        body, name="disassemble_w_in", grid=(D // tr,),
        in_specs=[_rows(tr, width), tab, tab, tab],
        out_specs=pl.BlockSpec((NDEV, tr, IN_PAD), lambda i: (0, i, 0)),
        out_shape=jax.ShapeDtypeStruct((NDEV, D, IN_PAD), BF16),
        compiler_params=_cparams(48, ("arbitrary",)),
    )(dwp, dest, dest_f, scale)


def _in_proj(x, g1, wp, bfp, pq, pk, oq, ok, *, tm):
    s = x.shape[0]

    def body(x_ref, g_ref, w_ref, bf_ref, pq_ref, pk_ref, oq_ref, ok_ref,
             ht_ref, qp_ref, kp_ref, v_ref, bcu_ref, z_ref, carry):
        @pl.when(pl.program_id(0) == 0)
        def _():
            carry[...] = jnp.zeros_like(carry)

        h = _rms_fwd(x_ref[...], g_ref[...])[0].astype(BF16)
        ht_ref[...] = h.T
        z = jnp.dot(h, w_ref[:, OFF_F:WP], preferred_element_type=F32) + bf_ref[...]
        z_ref[...] = z
        lane = lax.broadcasted_iota(jnp.int32, (tm, 128), 1)
        logf = jnp.where(lane < H, jnp.minimum(z, 0.0) - jnp.log(1.0 + jnp.exp(-jnp.abs(z))), 0.0)
        row = lax.broadcasted_iota(jnp.int32, (tm, tm), 0)
        col = lax.broadcasted_iota(jnp.int32, (tm, tm), 1)
        tri = (col <= row).astype(BF16)
        c = _exact_dot01(tri, logf) + carry[0:1, :]
        carry[...] = jnp.broadcast_to(c[tm - 1:tm, :], carry.shape)
        c1 = c.astype(BF16).astype(F32)
        r1 = c - c1
        c2 = r1.astype(BF16).astype(F32)
        c3 = (r1 - c2).astype(BF16).astype(F32)
        zc = (c1 + pltpu.roll(c2, 8, axis=1) + pltpu.roll(c3, 16, axis=1)).astype(BF16)

        def pad_heads(v):
            blocks = []
            for pair in range(H // 2):
                two = v[:, 128 * pair:128 * (pair + 1)]
                blocks.append(jnp.where(lane < DH, two, 0.0))
                blocks.append(jnp.where(lane < DH, pltpu.roll(two, DH, axis=1), 0.0))
            return jnp.concatenate(blocks, axis=1)

        q = jnp.dot(h, w_ref[:, OFF_Q:OFF_K], preferred_element_type=F32)
        qp_ref[...] = (pad_heads(q) + jnp.dot(zc, pq_ref[...], preferred_element_type=F32) + oq_ref[...]).astype(BF16)
        k = jnp.dot(h, w_ref[:, OFF_K:OFF_V], preferred_element_type=F32)
        kp_ref[...] = (pad_heads(k) + jnp.dot(zc, pk_ref[...], preferred_element_type=F32) + ok_ref[...]).astype(BF16)
        v = pad_heads(jnp.dot(h, w_ref[:, OFF_V:OFF_BCU], preferred_element_type=F32))
        ones_lane = lax.broadcasted_iota(jnp.int32, (tm, H * HP), 1) % HP == DH
        v_ref[...] = jnp.where(ones_lane, 1.0, v).astype(BF16)
        bcu_ref[...] = jnp.dot(h, w_ref[:, OFF_BCU:OFF_F], preferred_element_type=F32).astype(BF16)

    return pl.pallas_call(
        body, name="in_proj", grid=(s // tm,),
        in_specs=[_rows(tm, D), _full((1, D)), _resident((D, WP)), _full((1, 128)),
                  _full((128, 1024)), _full((128, 1024)), _full((1, 1024)), _full((1, 1024))],
        out_specs=[pl.BlockSpec((D, tm), lambda i: (0, i)), _rows(tm, 1024), _rows(tm, 1024), _rows(tm, 1024),
                   _rows(tm, 3 * CW), _rows(tm, 128)],
        out_shape=[jax.ShapeDtypeStruct((D, s), BF16), jax.ShapeDtypeStruct((s, 1024), BF16),
                   jax.ShapeDtypeStruct((s, 1024), BF16), jax.ShapeDtypeStruct((s, 1024), BF16),
                   jax.ShapeDtypeStruct((s, 3 * CW), BF16), jax.ShapeDtypeStruct((s, 128), F32)],
        scratch_shapes=[pltpu.VMEM((SUBLANES, 128), F32)],
        compiler_params=_cparams(56, ("arbitrary",)),
    )(x, g1, wp, bfp, pq, pk, oq, ok)


def _attn_fwd(qp, kp, v, *, t):
    s = qp.shape[0]
    nq = s // t

    def body(q_ref, k_ref, v_ref, o_ref, lse_ref, mk_ref):
        qi = pl.program_id(1)
        row = lax.broadcasted_iota(jnp.int32, (t, t), 0)
        col = lax.broadcasted_iota(jnp.int32, (t, t), 1)
        lane = lax.broadcasted_iota(jnp.int32, (t, 128), 1)

        def head_step(hh, ki, carry, masked):
            m, acc = carry
            off = pl.multiple_of(ki * t, t)
            q = q_ref[:, HP * hh:HP * (hh + 1)]
            k = k_ref[pl.ds(off, t), HP * hh:HP * (hh + 1)]
            sc = lax.dot_general(q, k, NT, preferred_element_type=F32)
            if masked:
                sc = jnp.where(col <= row, sc, -1e30)
            mn = jnp.maximum(m, jnp.max(sc, axis=-1, keepdims=True))
            p = jnp.exp(sc - mn).astype(BF16)
            acc = jnp.exp(m - mn) * acc + jnp.dot(p, v_ref[pl.ds(off, t), HP * hh:HP * (hh + 1)],
                                                  preferred_element_type=F32)
            return mn, acc

        def step(ki, carry, masked):
            new = tuple(head_step(hh, ki, carry[hh], masked) for hh in range(2))
            mk_ref[ki] = jnp.where(lane < DH, jnp.broadcast_to(new[0][0], (t, 128)), jnp.broadcast_to(new[1][0], (t, 128)))
            return new

        init = (jnp.full((t, 1), -1e30, F32), jnp.zeros((t, 128), F32))
        carry = lax.fori_loop(0, qi, functools.partial(step, masked=False), (init, init))
        (m0, acc0), (m1, acc1) = step(qi, carry, True)
        l0, l1 = acc0[:, DH:DH + 1], acc1[:, DH:DH + 1]
        o_ref[...] = jnp.where(lane < DH, acc0 / l0, pltpu.roll(acc1 / l1, DH, axis=1))
        lse_ref[...] = jnp.where(lane < DH, jnp.broadcast_to(m0 + jnp.log(l0), (t, 128)),
                                 jnp.broadcast_to(m1 + jnp.log(l1), (t, 128)))

    return pl.pallas_call(
        body, name="attn_fwd", grid=(H // 2, nq),
        in_specs=[pl.BlockSpec((t, 2 * HP), lambda p, i: (i, p)),
                  pl.BlockSpec((s, 2 * HP), lambda p, i: (0, p)),
                  pl.BlockSpec((s, 2 * HP), lambda p, i: (0, p))],
        out_specs=[pl.BlockSpec((t, 128), lambda p, i: (i, p)), pl.BlockSpec((t, 128), lambda p, i: (i, p)),
                   pl.BlockSpec((nq, t, 128), lambda p, i: (0, i, p))],
        out_shape=[jax.ShapeDtypeStruct((s, AW), F32), jax.ShapeDtypeStruct((s, AW), F32),
                   jax.ShapeDtypeStruct((nq, s, AW), F32)],
        compiler_params=_cparams(48, ("arbitrary", "arbitrary")),
    )(qp, kp, v)


HALO = 16


def _conv_taps(bcu_ref, halo_ref, first, tm):
    z = bcu_ref[:, CW:2 * CW].astype(F32) * bcu_ref[:, 2 * CW:3 * CW].astype(F32)
    zh = jnp.where(first, 0.0, halo_ref[:, CW:2 * CW].astype(F32) * halo_ref[:, 2 * CW:3 * CW].astype(F32))
    row = lax.broadcasted_iota(jnp.int32, (tm, CW), 0)
    last, before = zh[HALO - 1:HALO, :], zh[HALO - 2:HALO - 1, :]
    z1 = jnp.where(row == 0, last, pltpu.roll(z, 1, axis=0))
    z2 = jnp.where(row == 0, before, jnp.where(row == 1, last, pltpu.roll(z, 2, axis=0)))
    return z, z1, z2


def _halo_before(tm, width):
    return pl.BlockSpec((HALO, width), lambda i: (jnp.maximum(i * (tm // HALO) - 1, 0), 0))


def _mix_out(o, bcu, cw8, ga, gc, gsum, w_out, x, g_post, g_ffn_pre, *, tm):
    s = x.shape[0]

    def body(o_ref, bcu_ref, halo_ref, cw_ref, ga_ref, gc_ref, gs_ref, w_ref, x_ref, g_ref, gf_ref,
             merged_ref, y_ref, x2_ref, cv_ref, h2_ref):
        z, z1, z2 = _conv_taps(bcu_ref, halo_ref, pl.program_id(0) == 0, tm)
        cv = cw_ref[0:1, :] * z2 + cw_ref[1:2, :] * z1 + cw_ref[2:3, :] * z
        cv_ref[...] = cv
        conv = bcu_ref[:, 0:CW].astype(F32) * cv
        ov = o_ref[...]
        ra = lax.rsqrt(_split_dot(ov * ov, gs_ref[...]) * (1.0 / DH) + EPS)
        rc = lax.rsqrt(_split_dot(conv * conv, gs_ref[...]) * (1.0 / DH) + EPS)
        merged = jnp.concatenate([ov * ra * ga_ref[...], conv * rc * gc_ref[...]], axis=1).astype(BF16)
        merged_ref[...] = merged
        y = jnp.dot(merged, w_ref[...], preferred_element_type=F32)
        y_ref[...] = y
        x2 = x_ref[...] + _rms_fwd(y, g_ref[...])[0]
        x2_ref[...] = x2
        h2_ref[...] = _rms_fwd(x2, gf_ref[...])[0].astype(BF16)

    return pl.pallas_call(
        body, name="mix_out", grid=(s // tm,),
        in_specs=[_rows(tm, AW), _rows(tm, 3 * CW), _halo_before(tm, 3 * CW), _full((SUBLANES, CW)),
                  _full((1, AW)), _full((1, CW)), _full((CW, CW)), _resident((D, D)), _rows(tm, D), _full((1, D)),
                  _full((1, D))],
        out_specs=[_rows(tm, D), _rows(tm, D), _rows(tm, D), _rows(tm, CW), _rows(tm, D)],
        out_shape=[jax.ShapeDtypeStruct((s, D), BF16), jax.ShapeDtypeStruct((s, D), F32),
                   jax.ShapeDtypeStruct((s, D), F32), jax.ShapeDtypeStruct((s, CW), F32),
                   jax.ShapeDtypeStruct((s, D), BF16)],
        compiler_params=_cparams(48, ("arbitrary",)),
    )(o, bcu, bcu, cw8, ga, gc, gsum, w_out, x, g_post, g_ffn_pre)


def _ffn_fwd_loss(h2, wgu, wd, x2, target, g_post, *, tm):
    s = x2.shape[0]

    def body(h_ref, w_ref, wd_ref, x2_ref, t_ref, g_ref,
             gate_ref, up_ref, a_ref, dx3_ref, dff_ref, loss_ref, dg_ref):
        @pl.when(pl.program_id(0) == 0)
        def _():
            loss_ref[...] = jnp.zeros_like(loss_ref)
            dg_ref[...] = jnp.zeros_like(dg_ref)

        h = h_ref[...]
        ff = None
        for j in range(4):
            gate = jnp.dot(h, w_ref[0, j], preferred_element_type=F32)
            up = jnp.dot(h, w_ref[1, j], preferred_element_type=F32)
            gate_ref[j] = gate.astype(BF16)
            up_ref[j] = up.astype(BF16)
            act = (gate * jax.nn.sigmoid(gate) * up).astype(BF16)
            a_ref[j] = act
            part = jnp.dot(act, wd_ref[j], preferred_element_type=F32)
            ff = part if ff is None else ff + part
        out, n, r = _rms_fwd(ff, g_ref[...])
        e = x2_ref[...] + out - t_ref[...]
        loss_ref[...] += _fold8(e * e)
        dx3 = e * (1.0 / D)
        dx3_ref[...] = dx3
        dff, dg = _rms_bwd(dx3, n, r, g_ref[...])
        dff_ref[...] = dff.astype(BF16)
        dg_ref[...] += _fold8(dg)

    blk4 = pl.BlockSpec((4, tm, FB), lambda i: (0, i, 0))
    return pl.pallas_call(
        body, name="ffn_fwd_loss", grid=(s // tm,),
        in_specs=[_rows(tm, D), _resident((2, 4, D, FB)), _resident((4, FB, D)), _rows(tm, D), _rows(tm, D), _full((1, D))],
        out_specs=[blk4, blk4, blk4, _rows(tm, D), _rows(tm, D), _full((SUBLANES, D)), _full((SUBLANES, D))],
        out_shape=[jax.ShapeDtypeStruct((4, s, FB), BF16)] * 3
        + [jax.ShapeDtypeStruct((s, D), F32), jax.ShapeDtypeStruct((s, D), BF16),
           jax.ShapeDtypeStruct((SUBLANES, D), F32), jax.ShapeDtypeStruct((SUBLANES, D), F32)],
        compiler_params=_cparams(56, ("arbitrary",)),
    )(h2, wgu, wd, x2, target, g_post)


def _ffn_bwd(dff, wd, gate, up, wgu, x2, g_pre, dx3, y, g_post, *, tm):
    s = x2.shape[0]

    def body(dff_ref, wd_ref, gate_ref, up_ref, w_ref, x2_ref, gpre_ref, dx3_ref, y_ref, gpost_ref,
             dgu_ref, dx2_ref, dy_ref, dgpre_ref, dgpost_ref):
        @pl.when(pl.program_id(0) == 0)
        def _():
            dgpre_ref[...] = jnp.zeros_like(dgpre_ref)
            dgpost_ref[...] = jnp.zeros_like(dgpost_ref)

        dff = dff_ref[...]
        dh2 = None
        for j in range(4):
            da = lax.dot_general(dff, wd_ref[j], NT, preferred_element_type=F32)
            g = gate_ref[j].astype(F32)
            sg = jax.nn.sigmoid(g)
            dgate = (da * up_ref[j].astype(F32) * (sg * (1.0 + g * (1.0 - sg)))).astype(BF16)
            dup = (da * (g * sg)).astype(BF16)
            dgu_ref[0, j] = dgate
            dgu_ref[1, j] = dup
            part = (lax.dot_general(dgate, w_ref[0, j], NT, preferred_element_type=F32)
                    + lax.dot_general(dup, w_ref[1, j], NT, preferred_element_type=F32))
            dh2 = part if dh2 is None else dh2 + part
        _, n2, r2 = _rms_fwd(x2_ref[...], gpre_ref[...])
        dxn, dg = _rms_bwd(dh2, n2, r2, gpre_ref[...])
        dgpre_ref[...] += _fold8(dg)
        dx2 = dx3_ref[...] + dxn
        dx2_ref[...] = dx2
        _, ny, ry = _rms_fwd(y_ref[...], gpost_ref[...])
        dy, dg2 = _rms_bwd(dx2, ny, ry, gpost_ref[...])
        dy_ref[...] = dy.astype(BF16)
        dgpost_ref[...] += _fold8(dg2)

    blk4 = pl.BlockSpec((4, tm, FB), lambda i: (0, i, 0))
    return pl.pallas_call(
        body, name="ffn_bwd", grid=(s // tm,),
        in_specs=[_rows(tm, D), _resident((4, FB, D)), blk4, blk4, _resident((2, 4, D, FB)), _rows(tm, D), _full((1, D)),
                  _rows(tm, D), _rows(tm, D), _full((1, D))],
        out_specs=[pl.BlockSpec((2, 4, tm, FB), lambda i: (0, 0, i, 0)), _rows(tm, D), _rows(tm, D),
                   _full((SUBLANES, D)), _full((SUBLANES, D))],
        out_shape=[jax.ShapeDtypeStruct((2, 4, s, FB), BF16), jax.ShapeDtypeStruct((s, D), F32),
                   jax.ShapeDtypeStruct((s, D), BF16), jax.ShapeDtypeStruct((SUBLANES, D), F32),
                   jax.ShapeDtypeStruct((SUBLANES, D), F32)],
        compiler_params=_cparams(56, ("arbitrary",)),
    )(dff, wd, gate, up, wgu, x2, g_pre, dx3, y, g_post)


def _grad_matmul(a, b, *, ta, tb, ts, name):
    s, ka = a.shape
    nb = b.shape[1]
    ts = min(ts, s)
    nk = s // ts

    def body(a_ref, b_ref, o_ref, acc):
        k = pl.program_id(2)

        @pl.when(k == 0)
        def _():
            acc[...] = jnp.zeros_like(acc)

        acc[...] += lax.dot_general(a_ref[...], b_ref[...], TN, preferred_element_type=F32)

        @pl.when(k == nk - 1)
        def _():
            o_ref[...] = acc[...].astype(BF16)

    return pl.pallas_call(
        body, name=name, grid=(ka // ta, nb // tb, nk),
        in_specs=[pl.BlockSpec((ts, ta), lambda i, j, k: (k, i)), pl.BlockSpec((ts, tb), lambda i, j, k: (k, j))],
        out_specs=pl.BlockSpec((ta, tb), lambda i, j, k: (i, j)),
        out_shape=jax.ShapeDtypeStruct((ka, nb), BF16),
        scratch_shapes=[pltpu.VMEM((ta, tb), F32)],
        compiler_params=_cparams(48, ("arbitrary", "arbitrary", "arbitrary")),
    )(a, b)


def _grad_matmul_t(at, b, *, tb, name):
    ka, s = at.shape
    blocked = b.ndim == 3
    nb = b.shape[-1]
    steps = b.shape[0] if blocked else nb // tb
    width = nb if blocked else tb

    def body(a_ref, b_ref, o_ref):
        bv = b_ref[0] if blocked else b_ref[...]
        res = jnp.dot(a_ref[...], bv, preferred_element_type=F32).astype(BF16)
        if blocked:
            o_ref[0] = res
        else:
            o_ref[...] = res

    if blocked:
        b_spec = pl.BlockSpec((1, s, nb), lambda j: (j, 0, 0))
        o_spec = pl.BlockSpec((1, ka, nb), lambda j: (j, 0, 0))
        o_shape = jax.ShapeDtypeStruct((steps, ka, nb), BF16)
    else:
        b_spec = pl.BlockSpec((s, width), lambda j: (0, j))
        o_spec = pl.BlockSpec((ka, width), lambda j: (0, j))
        o_shape = jax.ShapeDtypeStruct((ka, nb), BF16)
    return pl.pallas_call(
        body, name=name, grid=(steps,),
        in_specs=[_resident((ka, s)), b_spec], out_specs=o_spec, out_shape=o_shape,
        compiler_params=_cparams(56, ("arbitrary",)),
    )(at, b)


GW_TILE = 256


def _grad_w_in(h1t, pieces):
    ka, s = h1t.shape
    widths = [p.shape[1] for p in pieces]
    assert all(w % GW_TILE == 0 for w in widths)
    first = [sum(widths[:i]) // GW_TILE for i in range(len(pieces))]
    count = [w // GW_TILE for w in widths]

    def body(a_ref, *refs):
        o_ref = refs[-1]
        j = pl.program_id(0)
        for ref, f0, n in zip(refs[:-1], first, count):
            @pl.when((j >= f0) & (j < f0 + n))
            def _(ref=ref):
                o_ref[...] = jnp.dot(a_ref[...], ref[...], preferred_element_type=F32).astype(BF16)

    def spec(f0, n):
        return pl.BlockSpec((s, GW_TILE), lambda j: (0, jnp.clip(j - f0, 0, n - 1)))

    return pl.pallas_call(
        body, name="grad_w_in", grid=(sum(count),),
        in_specs=[_resident((ka, s))] + [spec(f0, n) for f0, n in zip(first, count)],
        out_specs=pl.BlockSpec((ka, GW_TILE), lambda j: (0, j)),
        out_shape=jax.ShapeDtypeStruct((ka, sum(widths)), BF16),
        compiler_params=_cparams(56, ("arbitrary",)),
    )(h1t, *pieces)


def _grad_matmul_blocks(a, b, *, ts, name):
    nblk = a.shape[0] if a.ndim == 3 else b.shape[0]
    s = a.shape[-2]
    ka, nb = a.shape[-1], b.shape[-1]
    ts = min(ts, s)
    nk = s // ts

    def body(a_ref, b_ref, o_ref, acc):
        k = pl.program_id(1)

        @pl.when(k == 0)
        def _():
            acc[...] = jnp.zeros_like(acc)

        av = a_ref[0] if a.ndim == 3 else a_ref[...]
        bv = b_ref[0] if b.ndim == 3 else b_ref[...]
        acc[...] += lax.dot_general(av, bv, TN, preferred_element_type=F32)

        @pl.when(k == nk - 1)
        def _():
            o_ref[0] = acc[...].astype(BF16)

    def spec(arr, width):
        if arr.ndim == 3:
            return pl.BlockSpec((1, ts, width), lambda j, k: (j, k, 0))
        return pl.BlockSpec((ts, width), lambda j, k: (k, 0))

    return pl.pallas_call(
        body, name=name, grid=(nblk, nk),
        in_specs=[spec(a, ka), spec(b, nb)],
        out_specs=pl.BlockSpec((1, ka, nb), lambda j, k: (j, 0, 0)),
        out_shape=jax.ShapeDtypeStruct((nblk, ka, nb), BF16),
        scratch_shapes=[pltpu.VMEM((ka, nb), F32)],
        compiler_params=_cparams(48, ("arbitrary", "arbitrary")),
    )(a, b)


def _mix_bwd(dy, w_out, o, cv, bcu, ga, gc, gsum, *, tm):
    s = dy.shape[0]

    def group_norm_bwd(dn_out, v, g, gs):
        r = lax.rsqrt(_split_dot(v * v, gs) * (1.0 / DH) + EPS)
        n = v * r
        dn = dn_out * g
        return r * (dn - n * (_split_dot(dn * n, gs) * (1.0 / DH))), dn_out * n

    def body(dy_ref, w_ref, o_ref, cv_ref, bcu_ref, ga_ref, gc_ref, gs_ref,
             do_ref, dl_ref, dcv_ref, db_ref, dga_ref, dgc_ref):
        @pl.when(pl.program_id(0) == 0)
        def _():
            dga_ref[...] = jnp.zeros_like(dga_ref)
            dgc_ref[...] = jnp.zeros_like(dgc_ref)

        dm = lax.dot_general(dy_ref[...], w_ref[...], NT, preferred_element_type=F32)
        ov = o_ref[...]
        do, dga = group_norm_bwd(dm[:, 0:AW], ov, ga_ref[...], gs_ref[...])
        dob = do.astype(BF16)
        do_ref[...] = dob
        dl_ref[...] = _split_dot(dob.astype(F32) * ov, gs_ref[...])
        dga_ref[...] += _fold8(dga)
        gate_b = bcu_ref[:, 0:CW].astype(F32)
        cv = cv_ref[...]
        dconv, dgc = group_norm_bwd(dm[:, AW:D], gate_b * cv, gc_ref[...], gs_ref[...])
        dgc_ref[...] += _fold8(dgc)
        dcv_ref[...] = dconv * gate_b
        db_ref[...] = (dconv * cv).astype(BF16)

    return pl.pallas_call(
        body, name="mix_bwd", grid=(s // tm,),
        in_specs=[_rows(tm, D), _resident((D, D)), _rows(tm, AW), _rows(tm, CW), _rows(tm, 3 * CW),
                  _full((1, AW)), _full((1, CW)), _full((CW, CW))],
        out_specs=[_rows(tm, AW), _rows(tm, AW), _rows(tm, CW), _rows(tm, CW),
                   _full((SUBLANES, AW)), _full((SUBLANES, CW))],
        out_shape=[jax.ShapeDtypeStruct((s, AW), BF16), jax.ShapeDtypeStruct((s, AW), F32),
                   jax.ShapeDtypeStruct((s, CW), F32), jax.ShapeDtypeStruct((s, CW), BF16),
                   jax.ShapeDtypeStruct((SUBLANES, AW), F32), jax.ShapeDtypeStruct((SUBLANES, CW), F32)],
        compiler_params=_cparams(48, ("arbitrary",)),
    )(dy, w_out, o, cv, bcu, ga, gc, gsum)


def _conv_bwd(dcv, db, bcu, cw8, *, tm):
    s = dcv.shape[0]
    nt = s // tm

    def body(dcv_ref, nxt_ref, db_ref, bcu_ref, halo_ref, cw_ref, dbcu_ref, dw_ref):
        i = pl.program_id(0)

        @pl.when(i == 0)
        def _():
            dw_ref[...] = jnp.zeros_like(dw_ref)

        z, z1, z2 = _conv_taps(bcu_ref, halo_ref, i == 0, tm)
        d = dcv_ref[...]
        dw_ref[0] += _fold8(d * z2)
        dw_ref[1] += _fold8(d * z1)
        dw_ref[2] += _fold8(d * z)
        nx = jnp.where(i == nt - 1, 0.0, nxt_ref[...])
        row = lax.broadcasted_iota(jnp.int32, (tm, CW), 0)
        d1 = jnp.where(row == tm - 1, nx[0:1, :], pltpu.roll(d, tm - 1, axis=0))
        d2 = jnp.where(row == tm - 2, nx[0:1, :], jnp.where(row == tm - 1, nx[1:2, :], pltpu.roll(d, tm - 2, axis=0)))
        dz = cw_ref[2:3, :] * d + cw_ref[1:2, :] * d1 + cw_ref[0:1, :] * d2
        dbcu_ref[:, 0:CW] = db_ref[...]
        dbcu_ref[:, CW:2 * CW] = (dz * bcu_ref[:, 2 * CW:3 * CW].astype(F32)).astype(BF16)
        dbcu_ref[:, 2 * CW:3 * CW] = (dz * bcu_ref[:, CW:2 * CW].astype(F32)).astype(BF16)

    return pl.pallas_call(
        body, name="conv_bwd", grid=(nt,),
        in_specs=[_rows(tm, CW),
                  pl.BlockSpec((SUBLANES, CW), lambda i: (jnp.minimum((i + 1) * (tm // SUBLANES), s // SUBLANES - 1), 0)),
                  _rows(tm, CW), _rows(tm, 3 * CW), _halo_before(tm, 3 * CW), _full((SUBLANES, CW))],
        out_specs=[_rows(tm, 3 * CW), _full((3, SUBLANES, CW))],
        out_shape=[jax.ShapeDtypeStruct((s, 3 * CW), BF16), jax.ShapeDtypeStruct((3, SUBLANES, CW), F32)],
        compiler_params=_cparams(48, ("arbitrary",)),
    )(dcv, dcv, db, bcu, bcu, cw8)


def _attn_bwd(qp, kp, v, do, lse, dl, mk, *, t):
    s = qp.shape[0]
    nq = s // t

    def body(q_ref, k_ref, v_ref, do_ref, lse_ref, dl_ref, mk_ref, dq_ref, dk_ref, dv_ref, dkx_ref, dq_acc):
        ki = pl.program_id(1)

        @pl.when(ki == 0)
        def _():
            dq_acc[...] = jnp.zeros_like(dq_acc)

        row = lax.broadcasted_iota(jnp.int32, (t, t), 0)
        col = lax.broadcasted_iota(jnp.int32, (t, t), 1)
        lane = lax.broadcasted_iota(jnp.int32, (t, 128), 1)

        def head_step(hh, qi, carry, masked):
            dk, dv, cs = carry
            off = pl.multiple_of(qi * t, t)
            rows = pl.ds(off, t)
            kh = k_ref[:, HP * hh:HP * (hh + 1)]
            q = q_ref[rows, HP * hh:HP * (hh + 1)]
            m_col = mk_ref[0, rows, DH * hh:DH * hh + 1]
            scale = jnp.exp(m_col - lse_ref[rows, DH * hh:DH * hh + 1])
            do2 = do_ref[rows, :]
            dom = jnp.where(lane < DH, do2 if hh == 0 else pltpu.roll(do2, DH, axis=1), jnp.zeros((), BF16))
            sc = lax.dot_general(q, kh, NT, preferred_element_type=F32) - m_col
            if masked:
                sc = jnp.where(col <= row, sc, -1e30)
            pt = jnp.exp(sc).astype(BF16)
            dp = lax.dot_general(dom, v_ref[:, HP * hh:HP * (hh + 1)], NT, preferred_element_type=F32)
            ds32 = (pt.astype(F32) * scale) * (dp - dl_ref[rows, DH * hh:DH * hh + 1])
            ds = ds32.astype(BF16)
            cs = cs + _fold8(ds32)
            dv = dv + jnp.dot((dom.astype(F32) * scale).astype(BF16).T, pt, preferred_element_type=F32)
            dk = dk + jnp.dot(q.T, ds, preferred_element_type=F32)
            dq_acc[rows, HP * hh:HP * (hh + 1)] += jnp.dot(ds, kh, preferred_element_type=F32)
            return dk, dv, cs

        def step(qi, carry, masked):
            return tuple(head_step(hh, qi, carry[hh], masked) for hh in range(2))

        zero = (jnp.zeros((HP, t), F32), jnp.zeros((128, t), F32), jnp.zeros((SUBLANES, t), F32))
        carry = step(ki, (zero, zero), True)
        (dk0, dv0, cs0), (dk1, dv1, cs1) = lax.fori_loop(ki + 1, nq, functools.partial(step, masked=False), carry)
        def two_heads(a0, a1):
            return jnp.where(lane < DH, a0, pltpu.roll(a1, DH, axis=1))

        dk_ref[...] = two_heads(dk0.T, dk1.T).astype(BF16)
        dv_ref[...] = two_heads(dv0.T, dv1.T).astype(BF16)

        def as_column(cs):
            return lax.dot_general(cs, jnp.ones((SUBLANES, 128), F32), TN, precision=HIGHEST, preferred_element_type=F32)

        dkx_ref[...] = jnp.where(lane < DH, as_column(cs0), as_column(cs1))

        @pl.when(ki == nq - 1)
        def _():
            for c in range(s // t):
                rows = slice(c * t, (c + 1) * t)
                dq_ref[rows, :] = two_heads(dq_acc[rows, 0:HP], dq_acc[rows, HP:2 * HP]).astype(BF16)

    return pl.pallas_call(
        body, name="attn_bwd", grid=(H // 2, nq),
        in_specs=[pl.BlockSpec((s, 2 * HP), lambda p, i: (0, p)),
                  pl.BlockSpec((t, 2 * HP), lambda p, i: (i, p)),
                  pl.BlockSpec((t, 2 * HP), lambda p, i: (i, p)),
                  pl.BlockSpec((s, 128), lambda p, i: (0, p)),
                  pl.BlockSpec((s, 128), lambda p, i: (0, p)),
                  pl.BlockSpec((s, 128), lambda p, i: (0, p)),
                  pl.BlockSpec((1, s, 128), lambda p, i: (i, 0, p))],
        out_specs=[pl.BlockSpec((s, 128), lambda p, i: (0, p)),
                   pl.BlockSpec((t, 128), lambda p, i: (i, p)),
                   pl.BlockSpec((t, 128), lambda p, i: (i, p)),
                   pl.BlockSpec((t, 128), lambda p, i: (i, p))],
        out_shape=[jax.ShapeDtypeStruct((s, AW), BF16), jax.ShapeDtypeStruct((s, AW), BF16),
                   jax.ShapeDtypeStruct((s, AW), BF16), jax.ShapeDtypeStruct((s, AW), F32)],
        scratch_shapes=[pltpu.VMEM((s, 2 * HP), F32)],
        compiler_params=_cparams(56, ("arbitrary", "arbitrary")),
    )(qp, kp, v, do, lse, dl, mk)


def _forget_bwd(dkx, z, sel, *, tm):
    s = dkx.shape[0]
    nt = s // tm

    def body(dk_ref, z_ref, sel_ref, dfl_ref, dbf_ref, carry):
        @pl.when(pl.program_id(0) == 0)
        def _():
            carry[...] = jnp.zeros_like(carry)
            dbf_ref[...] = jnp.zeros_like(dbf_ref)

        dc = _split_dot(dk_ref[...], sel_ref[...])
        row = lax.broadcasted_iota(jnp.int32, (tm, tm), 0)
        col = lax.broadcasted_iota(jnp.int32, (tm, tm), 1)
        tri = (col >= row).astype(BF16)
        dlogf = _exact_dot01(tri, dc) + carry[0:1, :]
        carry[...] = jnp.broadcast_to(dlogf[0:1, :], carry.shape)
        dz = dlogf * (1.0 - jax.nn.sigmoid(z_ref[...]))
        dfl_ref[:, 0:128] = dz.astype(BF16)
        dfl_ref[:, 128:GW_TILE] = jnp.zeros((tm, GW_TILE - 128), BF16)
        dbf_ref[...] += _fold8(dz)

    rev = lambda i: (nt - 1 - i, 0)
    return pl.pallas_call(
        body, name="forget_bwd", grid=(nt,),
        in_specs=[pl.BlockSpec((tm, AW), rev), pl.BlockSpec((tm, 128), rev), _full((AW, 128))],
        out_specs=[pl.BlockSpec((tm, GW_TILE), rev), _full((SUBLANES, 128))],
        out_shape=[jax.ShapeDtypeStruct((s, GW_TILE), BF16), jax.ShapeDtypeStruct((SUBLANES, 128), F32)],
        scratch_shapes=[pltpu.VMEM((SUBLANES, 128), F32)],
        compiler_params=_cparams(48, ("arbitrary",)),
    )(dkx, z, sel)


def _in_proj_bwd(pieces, wp, x, g1, dx2, *, tm):
    s = x.shape[0]

    def body(q_ref, k_ref, v_ref, bcu_ref, f_ref, w_ref, x_ref, g_ref, dx2_ref, dx_ref, dg_ref):
        @pl.when(pl.program_id(0) == 0)
        def _():
            dg_ref[...] = jnp.zeros_like(dg_ref)

        dh = None
        for ref, (lo, hi) in zip((q_ref, k_ref, v_ref, bcu_ref, f_ref), PIECES):
            part = lax.dot_general(ref[...], w_ref[:, lo:hi], NT, preferred_element_type=F32)
            dh = part if dh is None else dh + part
        _, n, r = _rms_fwd(x_ref[...], g_ref[...])
        dxn, dg = _rms_bwd(dh, n, r, g_ref[...])
        dx_ref[...] = dx2_ref[...] + dxn
        dg_ref[...] += _fold8(dg)

    return pl.pallas_call(
        body, name="in_proj_bwd", grid=(s // tm,),
        in_specs=[_rows(tm, hi - lo) for lo, hi in PIECES] + [_resident((D, WP)), _rows(tm, D), _full((1, D)), _rows(tm, D)],
        out_specs=[_rows(tm, D), _full((SUBLANES, D))],
        out_shape=[jax.ShapeDtypeStruct((s, D), F32), jax.ShapeDtypeStruct((SUBLANES, D), F32)],
        compiler_params=_cparams(56, ("arbitrary",)),
    )(*pieces, wp, x, g1, dx2)


def _position():
    return lax.axis_index("x"), lax.axis_index("y"), lax.axis_index("c")


ANY = pl.BlockSpec(memory_space=pl.ANY)


def _all_gather(shards):
    n = len(shards)

    def body(*refs):
        x_refs, out_refs = refs[:n], refs[n:2 * n]
        send_sems, recv_sems, local_sems = refs[2 * n:]
        x, y, c = _position()
        me, sibling = (x, y, c), (x, y, 1 - c)
        chips = [(1 - x, y), (x, 1 - y), (1 - x, 1 - y)]

        def copy(a, k, block, to, own=False):
            slot = out_refs[a].at[4 * block[0] + 2 * block[1] + block[2]]
            return pltpu.make_async_remote_copy(
                src_ref=x_refs[a] if own else slot, dst_ref=slot,
                send_sem=send_sems.at[7 * a + k], recv_sem=recv_sems.at[7 * a + k], device_id=to, device_id_type=MESH_ID)

        mine = [pltpu.make_async_copy(x_refs[a], out_refs[a].at[4 * x + 2 * y + c], local_sems.at[a]) for a in range(n)]
        for cp in mine:
            cp.start()
        first = []
        for a in range(n):
            first.append(copy(a, 0, me, sibling, own=True))
            first += [copy(a, 1 + j, me, (*chip, c), own=True) for j, chip in enumerate(chips)]
        for cp in first:
            cp.start()
        passed = []
        for j, chip in enumerate(chips):
            for a in range(n):
                copy(a, 1 + j, (*chip, c), me).wait_recv()
                fwd = copy(a, 4 + j, (*chip, c), sibling)
                fwd.start()
                passed.append(fwd)
        for a in range(n):
            copy(a, 0, sibling, me).wait_recv()
            for j, chip in enumerate(chips):
                copy(a, 4 + j, (*chip, 1 - c), me).wait_recv()
        for cp in first + passed:
            cp.wait_send()
        for cp in mine:
            cp.wait()

    return pl.pallas_call(
        body, name="all_gather_weights",
        out_shape=[jax.ShapeDtypeStruct((NDEV,) + sh.shape, sh.dtype) for sh in shards],
        in_specs=[ANY] * n, out_specs=[ANY] * n,
        scratch_shapes=[pltpu.SemaphoreType.DMA((7 * n,)), pltpu.SemaphoreType.DMA((7 * n,)), pltpu.SemaphoreType.DMA((n,))],
    )(*shards)


def _pair_exchange(grads):
    n = len(grads)

    def body(*refs):
        g_refs, out_refs = refs[:n], refs[n:2 * n]
        send_sems, recv_sems = refs[2 * n:]
        x, y, c = _position()
        copies = [pltpu.make_async_remote_copy(
            src_ref=g_refs[a].at[:, pl.ds(1 - c, 1)], dst_ref=out_refs[a], send_sem=send_sems.at[a],
            recv_sem=recv_sems.at[a], device_id=(x, y, 1 - c), device_id_type=MESH_ID) for a in range(n)]
        for cp in copies:
            cp.start()
        for cp in copies:
            cp.wait()

    return pl.pallas_call(
        body, name="grad_pair_exchange",
        out_shape=[jax.ShapeDtypeStruct((4, 1) + g.shape[2:], g.dtype) for g in grads],
        in_specs=[ANY] * n, out_specs=[ANY] * n,
        scratch_shapes=[pltpu.SemaphoreType.DMA((n,)), pltpu.SemaphoreType.DMA((n,))],
    )(*grads)


def _pair_sum(g, got, idx, *, tr, name):
    r, c = g.shape[2:]

    def body(idx_ref, g_ref, got_ref, pb_ref, own_ref):
        p = g_ref[0, 0].astype(F32) + got_ref[0, 0].astype(F32)
        pb_ref[0] = p.astype(BF16)

        @pl.when(pl.program_id(1) == idx_ref[1])
        def _():
            own_ref[...] = p

    return pl.pallas_call(
        body, name=name,
        grid_spec=pltpu.PrefetchScalarGridSpec(
            num_scalar_prefetch=1, grid=(r // tr, 4),
            in_specs=[pl.BlockSpec((1, 1, tr, c), lambda i, j, idx: (j, idx[0], i, 0)),
                      pl.BlockSpec((1, 1, tr, c), lambda i, j, idx: (j, 0, i, 0))],
            out_specs=[pl.BlockSpec((1, tr, c), lambda i, j, idx: (j, i, 0)),
                       pl.BlockSpec((tr, c), lambda i, j, idx: (i, 0))]),
        out_shape=[jax.ShapeDtypeStruct((4, r, c), BF16), jax.ShapeDtypeStruct((r, c), F32)],
        compiler_params=_cparams(32, ("arbitrary", "arbitrary")),
    )(idx, g, got)


HBM = pl.BlockSpec(memory_space=pltpu.HBM)
SEM = pl.BlockSpec(memory_space=pltpu.SEMAPHORE)
DATAFLOW = pltpu.SideEffectType.DATAFLOW_SIDE_EFFECTING


PEERS = {"gather": NDEV - 1, "scatter": NDEV - 1, "chips": 3}


def _exchange_copies(src_refs, land_refs, send_sems, recv_sems, mode):
    x, y, c = _position()
    me, my_chip = 4 * x + 2 * y + c, 2 * x + y
    npeers = PEERS[mode]
    copies = []
    for a, (s_ref, l_ref) in enumerate(zip(src_refs, land_refs)):
        for k in range(npeers):
            if mode == "chips":
                px, py, pc = x ^ ((k + 1) >> 1), y ^ ((k + 1) & 1), c
                src, dst = s_ref.at[2 * px + py], l_ref.at[my_chip]
            else:
                px, py, pc = x ^ ((k + 1) >> 2), y ^ (((k + 1) >> 1) & 1), c ^ ((k + 1) & 1)
                src, dst = (s_ref.at[4 * px + 2 * py + pc] if mode == "scatter" else s_ref), l_ref.at[me]
            copies.append(pltpu.make_async_remote_copy(
                src_ref=src, dst_ref=dst, send_sem=send_sems.at[npeers * a + k], recv_sem=recv_sems.at[npeers * a + k],
                device_id=(px, py, pc), device_id_type=MESH_ID))
    return copies


def _exchange_start(srcs, lands, after, *, mode, name):
    n = len(srcs)
    nsem = PEERS[mode] * n

    def body(*refs):
        token = refs[-1]
        for cp in _exchange_copies(refs[:n], refs[n:2 * n], refs[2 * n + 1], refs[2 * n + 2], mode):
            cp.start()
        token[...] = jnp.zeros_like(token)

    arrays = list(srcs) + list(lands)
    outs = pl.pallas_call(
        body, name=name,
        out_shape=(pltpu.SemaphoreType.DMA((nsem,)), pltpu.SemaphoreType.DMA((nsem,)),
                   *[pltpu.HBM(a.shape, a.dtype) for a in arrays], jax.ShapeDtypeStruct((SUBLANES, LANES), F32)),
        in_specs=[HBM] * (2 * n) + [ANY],
        out_specs=(SEM, SEM, *[HBM] * (2 * n), pl.BlockSpec(memory_space=pltpu.VMEM)),
        input_output_aliases={i: 2 + i for i in range(2 * n)},
        compiler_params=pltpu.CompilerParams(has_side_effects=DATAFLOW),
    )(*[pltpu.with_memory_space_constraint(a, pltpu.HBM) for a in arrays], after)
    return outs[0], outs[1], outs[2:2 + n], outs[2 + n:2 + 2 * n], outs[-1]


def _exchange_wait(send_sems, recv_sems, srcs, lands, after, *, mode, name):
    n = len(srcs)

    def body(*refs):
        for cp in _exchange_copies(refs[:n], refs[n:2 * n], refs[2 * n], refs[2 * n + 1], mode):
            cp.wait_send()
            cp.wait_recv()

    arrays = list(srcs) + list(lands)
    outs = pl.pallas_call(
        body, name=name,
        out_shape=tuple(pltpu.HBM(a.shape, a.dtype) for a in arrays),
        in_specs=[HBM] * (2 * n) + [SEM, SEM, ANY],
        out_specs=tuple([HBM] * (2 * n)),
        input_output_aliases={i: i for i in range(2 * n)},
        compiler_params=pltpu.CompilerParams(has_side_effects=DATAFLOW),
    )(*arrays, send_sems, recv_sems, after)
    return outs[n:]


def _own_slot(value, me):
    return lax.dynamic_update_index_in_dim(lax.empty((NDEV,) + value.shape, value.dtype), value, me, 0)


def _small_all_reduce(parts):
    def body(gmp_ref, gmo_ref, gfp_ref, gfo_ref, ga_ref, gc_ref, dw_ref, bf_ref, loss_ref,
             out_ref, buf, send_sems, recv_sems):
        x, y, c = _position()
        me = 4 * x + 2 * y + c

        def colsum(v):
            return jnp.sum(v, axis=0, keepdims=True)

        loss = jnp.sum(colsum(loss_ref[...]), axis=1, keepdims=True) * (0.5 / D)
        rows = [colsum(gmp_ref[...]), colsum(gmo_ref[...]), colsum(gfp_ref[...]), colsum(gfo_ref[...]),
                jnp.concatenate([colsum(ga_ref[...]), colsum(gc_ref[...])], axis=1),
                jnp.concatenate([colsum(dw_ref[0]), colsum(dw_ref[1])], axis=1),
                jnp.concatenate([colsum(dw_ref[2]), colsum(bf_ref[...]), jnp.broadcast_to(loss, (1, 128)),
                                 jnp.zeros((1, 256), F32)], axis=1),
                jnp.zeros((1, D), F32)]
        buf[me] = jnp.concatenate(rows, axis=0)
        copies = []
        for mm in range(1, NDEV):
            peer = (x ^ (mm >> 2), y ^ ((mm >> 1) & 1), c ^ (mm & 1))
            copies.append(pltpu.make_async_remote_copy(
                src_ref=buf.at[me], dst_ref=buf.at[me], send_sem=send_sems.at[mm - 1], recv_sem=recv_sems.at[mm - 1],
                device_id=peer, device_id_type=MESH_ID))
        for cp in copies:
            cp.start()
        for cp in copies:
            cp.wait_recv()
        for cp in copies:
            cp.wait_send()
        acc = buf[0]
        for d in range(1, NDEV):
            acc = acc + buf[d]
        out_ref[...] = acc

    vm = pl.BlockSpec(memory_space=pltpu.VMEM)
    return pl.pallas_call(
        body, name="small_all_reduce",
        out_shape=jax.ShapeDtypeStruct((SUBLANES, D), F32),
        in_specs=[vm] * len(parts), out_specs=vm,
        scratch_shapes=[pltpu.VMEM((NDEV, SUBLANES, D), F32), pltpu.SemaphoreType.DMA((7,)), pltpu.SemaphoreType.DMA((7,))],
    )(*parts)


def _adam_update(w, g, m, v):
    nm = ADAM_B1 * m + (1.0 - ADAM_B1) * g
    nv = ADAM_B2 * v + (1.0 - ADAM_B2) * (g * g)
    m_hat = nm / (1.0 - ADAM_B1 ** ADAM_STEP)
    v_hat = nv / (1.0 - ADAM_B2 ** ADAM_STEP)
    return -ADAM_LR * (m_hat / (jnp.sqrt(v_hat) + ADAM_EPS) + ADAM_WD * w), nm, nv


def _adamw(w, g, m, v, *, tr, name):
    rows, cols = w.shape

    def body(w_ref, g_ref, m_ref, v_ref, d_ref, nm_ref, nv_ref):
        d_ref[...], nm_ref[...], nv_ref[...] = _adam_update(w_ref[...], g_ref[...], m_ref[...], v_ref[...])

    spec = pl.BlockSpec((tr, cols), lambda i: (i, 0))
    return pl.pallas_call(
        body, name=name, grid=(rows // tr,),
        in_specs=[spec] * 4, out_specs=[spec] * 3,
        out_shape=[jax.ShapeDtypeStruct((rows, cols), F32)] * 3,
        compiler_params=_cparams(32, ("arbitrary",)),
    )(w, g, m, v)


def _chip_sum_adamw(got, own, idx, w, m, v, *, tr, name):
    rows, cols = w.shape
    gcols = own.shape[1]

    def body(idx_ref, got_ref, own_ref, w_ref, m_ref, v_ref, g_ref, d_ref, nm_ref, nv_ref):
        g = jnp.zeros((tr, gcols), F32)
        for j in range(4):
            g = g + jnp.where(idx_ref[1] == j, own_ref[...], got_ref[j].astype(F32))
        g = g[:, :cols]
        g_ref[...] = g
        d_ref[...], nm_ref[...], nv_ref[...] = _adam_update(w_ref[...], g, m_ref[...], v_ref[...])

    spec = pl.BlockSpec((tr, cols), lambda i, idx: (i, 0))
    gspec = pl.BlockSpec((tr, gcols), lambda i, idx: (i, 0))
    return pl.pallas_call(
        body, name=name,
        grid_spec=pltpu.PrefetchScalarGridSpec(
            num_scalar_prefetch=1, grid=(rows // tr,),
            in_specs=[pl.BlockSpec((4, tr, gcols), lambda i, idx: (0, i, 0)), gspec, spec, spec, spec],
            out_specs=[spec] * 4),
        out_shape=[jax.ShapeDtypeStruct((rows, cols), F32)] * 4,
        compiler_params=_cparams(32, ("arbitrary",)),
    )(idx, got, own, w, m, v)


def _device_sum_adamw(land, w, m, v, *, tr, name):
    rows, cols = w.shape

    def body(land_ref, w_ref, m_ref, v_ref, g_ref, d_ref, nm_ref, nv_ref):
        g = land_ref[0].astype(F32)
        for dev in range(1, NDEV):
            g = g + land_ref[dev].astype(F32)
        g_ref[...] = g
        d_ref[...], nm_ref[...], nv_ref[...] = _adam_update(w_ref[...], g, m_ref[...], v_ref[...])

    spec = pl.BlockSpec((tr, cols), lambda i: (i, 0))
    return pl.pallas_call(
        body, name=name, grid=(rows // tr,),
        in_specs=[pl.BlockSpec((NDEV, tr, cols), lambda i: (0, i, 0)), spec, spec, spec],
        out_specs=[spec] * 4,
        out_shape=[jax.ShapeDtypeStruct((rows, cols), F32)] * 4,
        compiler_params=_cparams(32, ("arbitrary",)),
    )(land, w, m, v)


def _placement_constants():
    j = jnp.arange(128)[:, None]
    lane = jnp.arange(1024)[None, :]
    head, sub = lane // HP, lane % HP
    piece, jh = j // H, j % H
    valid = (j < 3 * H) & (jh == head)
    pq = jnp.where(valid & (sub == DH + piece), 1.0, 0.0).astype(BF16)
    pk = jnp.where(valid & (sub == DH + 3 + piece), -1.0, 0.0).astype(BF16)
    oq = jnp.where((sub >= DH + 3) & (sub < DH + 6), 1.0, 0.0).astype(F32)
    ok = jnp.where((sub >= DH) & (sub < DH + 3), 1.0, 0.0).astype(F32)
    r = jnp.arange(AW)[:, None]
    cc = jnp.arange(128)[None, :]
    sel = jnp.where((r % DH == 3) & (r // DH == cc), -1.0, 0.0).astype(BF16)
    gi = jnp.arange(CW)
    gsum = (gi[:, None] // DH == gi[None, :] // DH).astype(BF16)
    return pq, pk, oq, ok, sel, gsum


def _local_step(xs, tgt, wp, late_weights, cw8, bfp, g_attn_out, g_conv_out,
                g_mix_pre, g_mix_post, g_ffn_pre, g_ffn_post, early_grads=None, last_grad=None):
    pq, pk, oq, ok, sel, gsum = _placement_constants()
    h1t, qp, kp, vv, bcu, zf = _in_proj(xs, g_mix_pre, wp, bfp, pq, pk, oq, ok, tm=512)
    o, lse, mk = _attn_fwd(qp, kp, vv, t=512)
    w_out_f, wgu, wd = late_weights(lse)
    merged, y, x2, cv, h2 = _mix_out(o, bcu, cw8, g_attn_out, g_conv_out, gsum, w_out_f, xs, g_mix_post, g_ffn_pre, tm=512)
    gate, up, act, dx3, dff, loss_p, dg_ffn_post = _ffn_fwd_loss(h2, wgu, wd, x2, tgt, g_ffn_post, tm=512)

    dgu, dx2, dy, dg_ffn_pre, dg_mix_post = _ffn_bwd(dff, wd, gate, up, wgu, x2, g_ffn_pre, dx3, y, g_mix_post, tm=256)
    dw_down = _grad_matmul_blocks(act, dff, ts=4096, name="grad_w_down")
    dw_gu = _grad_matmul_blocks(dgu.reshape(NDEV, -1, FB), h2, ts=4096, name="grad_w_gate_up")
    dw_out = _grad_matmul(merged, dy, ta=1024, tb=1024, ts=2048, name="grad_w_out")
    token = early_grads(dw_out, dw_gu, dw_down) if early_grads is not None else None
    ga = g_attn_out if token is None else g_attn_out + token[0:1, 0:1]
    do, dl, dcv, db, dg_attn, dg_conv = _mix_bwd(dy, w_out_f, o, cv, bcu, ga, g_conv_out, gsum, tm=512)
    dbcu, dtaps = _conv_bwd(dcv, db, bcu, cw8, tm=512)
    dqp, dkp, dv, dkx = _attn_bwd(qp, kp, vv, do, lse, dl, mk, t=512)
    dfl, dbf = _forget_bwd(dkx, zf, sel, tm=512)
    pieces = (dqp, dkp, dv, dbcu, dfl)
    dwp = _grad_w_in(h1t, pieces)
    token = last_grad(dwp) if last_grad is not None else None
    g1 = g_mix_pre if token is None else g_mix_pre + token[0:1, 0:1]
    grad_x, dg_mix_pre = _in_proj_bwd(pieces, wp, xs, g1, dx2, tm=512)
    return (grad_x, dwp, dw_out, dw_gu, dw_down, dg_mix_pre, dg_mix_post, dg_ffn_pre, dg_ffn_post, dg_attn, dg_conv,
            dtaps, dbf, loss_p)


BIG_TILES = {"w_in": 256, "w_out": 128, "w_gate_up": 176, "w_down": 176}


def kernel(x, w_in, b_forget, conv_w, g_attn_out, g_conv_out, w_out, g_mix_pre, g_mix_post, w_gate_up, w_down, g_ffn_pre, g_ffn_post, loss_target, m_w_in, m_b_forget, m_conv_w, m_g_attn_out, m_g_conv_out, m_w_out, m_g_mix_pre, m_g_mix_post, m_w_gate_up, m_w_down, m_g_ffn_pre, m_g_ffn_post, v_w_in, v_b_forget, v_conv_w, v_g_attn_out, v_g_conv_out, v_w_out, v_g_mix_pre, v_g_mix_post, v_w_gate_up, v_w_down, v_g_ffn_pre, v_g_ffn_post):
    xc, yc, cc = _position()
    my_chip = 2 * xc + yc
    me = 2 * my_chip + cc
    idx = jnp.stack([cc, my_chip]).astype(jnp.int32)
    tables = _in_layout_tables()
    pad_in = lambda a: jnp.pad(a, ((0, 0), (0, IN_PAD - IN_COLS)))

    g_in, g_taps = _all_gather([pad_in(w_in[0]).astype(BF16), conv_w[0]])
    wp = _assemble_w_in(g_in, tables, tr=256)
    cw8 = jnp.pad(g_taps.transpose(1, 0, 2).reshape(3, CW), ((0, SUBLANES - 3), (0, 0)))

    late = [w_out[0].astype(BF16), w_gate_up[0].astype(BF16), w_down[0].astype(BF16)]
    ssem, rsem, late_thru, land_thru, token = _exchange_start(
        late, [_own_slot(s, me) for s in late], g_in, mode="gather", name="gather_late_start")
    bfp = jnp.pad(b_forget, ((0, 0), (0, 128 - H))) + token[0:1, :]

    def late_weights(after):
        l_out, l_gu, l_down = _exchange_wait(ssem, rsem, late_thru, land_thru, after, mode="gather", name="gather_late_wait")
        return l_out.reshape(D, D), l_gu.reshape(2, 4, D, FB), l_down.reshape(4, FB, D)

    early = {}

    def early_grads(dw_out, dw_gu, dw_down):
        srcs = [dw_out.reshape(NDEV, D // NDEV, D), dw_gu, dw_down.reshape(NDEV, DFF // NDEV, D)]
        lands = [_own_slot(lax.dynamic_index_in_dim(s, me, 0, keepdims=False), me) for s in srcs]
        early["handles"] = _exchange_start(srcs, lands, dw_out, mode="scatter", name="scatter_early_start")
        return early["handles"][4]

    last = {}

    def last_grad(dwp):
        g_w_in = _disassemble_w_in(dwp, tables, tr=256).reshape(4, 2, D, IN_PAD)
        (from_sibling,) = _pair_exchange([g_w_in])
        pair_b, last["own"] = _pair_sum(g_w_in, from_sibling, idx, tr=BIG_TILES["w_in"], name="grad_pair_sum_w_in")
        land = lax.dynamic_update_index_in_dim(lax.empty(pair_b.shape, pair_b.dtype),
                                               lax.dynamic_index_in_dim(pair_b, my_chip, 0, keepdims=False), my_chip, 0)
        last["handles"] = _exchange_start([pair_b], [land], last["own"], mode="chips", name="chips_w_in_start")
        return last["handles"][4]

    (grad_x, dwp, dw_out, dw_gu, dw_down, dg_mix_pre, dg_mix_post, dg_ffn_pre, dg_ffn_post, dg_attn, dg_conv,
     dtaps, dbf, loss_p) = _local_step(x[0], loss_target[0], wp, late_weights, cw8, bfp, g_attn_out, g_conv_out,
                                        g_mix_pre, g_mix_post, g_ffn_pre, g_ffn_post, early_grads, last_grad)

    e_ssem, e_rsem, e_srcs, e_lands, _ = early["handles"]
    land_out, land_gu, land_down = _exchange_wait(e_ssem, e_rsem, e_srcs, e_lands, dg_mix_pre, mode="scatter",
                                                  name="scatter_early_wait")
    res = {}
    big = {"w_out": (land_out, w_out[0], m_w_out[0], v_w_out[0]),
           "w_gate_up": (land_gu, w_gate_up[0].T, m_w_gate_up[0].T, v_w_gate_up[0].T),
           "w_down": (land_down, w_down[0], m_w_down[0], v_w_down[0])}
    for name, (land, w, m, v) in big.items():
        outs = _device_sum_adamw(land, w, m, v, tr=BIG_TILES[name], name="adamw_" + name)
        res[name] = [(o.T if name == "w_gate_up" else o)[None] for o in outs]
    c_ssem, c_rsem, c_srcs, c_lands, _ = last["handles"]
    after = sum(res[n][1][0, :SUBLANES, :LANES] for n in big)
    (from_chips,) = _exchange_wait(c_ssem, c_rsem, c_srcs, c_lands, after, mode="chips", name="chips_w_in_wait")
    outs = _chip_sum_adamw(from_chips, last["own"], idx, w_in[0], m_w_in[0], v_w_in[0],
                           tr=BIG_TILES["w_in"], name="adamw_w_in")
    res["w_in"] = [o[None] for o in outs]

    small = _small_all_reduce([dg_mix_pre, dg_mix_post, dg_ffn_pre, dg_ffn_post, dg_attn, dg_conv, dtaps, dbf, loss_p])
    taps_full = jnp.concatenate([small[5:6, :CW], small[5:6, CW:], small[6:7, :CW]], axis=0)
    small_grads = {
        "b_forget": small[6:7, CW:CW + H], "conv_w": lax.dynamic_slice(taps_full, (0, me * 64), (3, 64)),
        "g_attn_out": small[4:5, :AW], "g_conv_out": small[4:5, AW:], "g_mix_pre": small[0:1], "g_mix_post": small[1:2],
        "g_ffn_pre": small[2:3], "g_ffn_post": small[3:4]}
    loss = small[6, CW + 128]
    smalls = {"b_forget": (b_forget, m_b_forget, v_b_forget), "conv_w": (conv_w[0], m_conv_w[0], v_conv_w[0]),
              "g_attn_out": (g_attn_out, m_g_attn_out, v_g_attn_out), "g_conv_out": (g_conv_out, m_g_conv_out, v_g_conv_out),
              "g_mix_pre": (g_mix_pre, m_g_mix_pre, v_g_mix_pre), "g_mix_post": (g_mix_post, m_g_mix_post, v_g_mix_post),
              "g_ffn_pre": (g_ffn_pre, m_g_ffn_pre, v_g_ffn_pre), "g_ffn_post": (g_ffn_post, m_g_ffn_post, v_g_ffn_post)}
    for name, (w, m, v) in smalls.items():
        g = small_grads[name]
        outs = [g] + list(_adamw(w, g, m, v, tr=w.shape[0], name="adamw_" + name))
        res[name] = [o[None] for o in outs] if name == "conv_w" else outs

    order = ["w_in", "b_forget", "conv_w", "g_attn_out", "g_conv_out", "w_out", "g_mix_pre", "g_mix_post",
             "w_gate_up", "w_down", "g_ffn_pre", "g_ffn_post"]
    outs = [loss, grad_x[None]]
    for k in range(4):
        outs += [res[n][k] for n in order]
    return tuple(outs)
```

```python
import functools

import numpy as np

import jax
import jax.numpy as jnp
from jax import lax
from jax.experimental import pallas as pl
from jax.experimental.pallas import tpu as pltpu

F32 = jnp.float32
BF16 = jnp.bfloat16
HIGHEST = lax.Precision.HIGHEST
MESH_ID = pl.DeviceIdType.MESH

D = 1024
H = 8
DH = 64
AW = 512
CW = 512
DFF = 2816
FB = DFF // 4
HP = 128
OFF_Q, OFF_K, OFF_V, OFF_BCU, OFF_F = 0, 512, 1024, 1536, 3072
WP = OFF_F + 128
PIECES = ((OFF_Q, OFF_K), (OFF_K, OFF_V), (OFF_V, OFF_BCU), (OFF_BCU, OFF_F), (OFF_F, WP))
EPS = 1e-6
NDEV = 8
LANES = 128
SUBLANES = 8
IN_COLS = 385
IN_PAD = 512
WIN = 640
ADAM_LR, ADAM_B1, ADAM_B2, ADAM_EPS, ADAM_WD, ADAM_STEP = 0.001, 0.9, 0.999, 1e-08, 0.01, 10

NT = (((1,), (1,)), ((), ()))
TN = (((0,), (0,)), ((), ()))


def _cparams(vmem_mb=None, sem=None):
    kw = {}
    if vmem_mb is not None:
        kw["vmem_limit_bytes"] = vmem_mb << 20
    if sem is not None:
        kw["dimension_semantics"] = sem
    return pltpu.CompilerParams(**kw)


def _full(shape):
    return pl.BlockSpec(shape, lambda *_: (0,) * len(shape))


def _resident(shape):
    return pl.BlockSpec(shape, lambda *_: (0,) * len(shape), pipeline_mode=pl.Buffered(1))


def _rows(tm, width):
    return pl.BlockSpec((tm, width), lambda i: (i, 0))


def _fold8(v):
    r, w = v.shape
    return jnp.sum(v.reshape(r // SUBLANES, SUBLANES, w), axis=0)


def _split_dot(v, m01):
    hi = v.astype(BF16)
    lo = (v - hi.astype(F32)).astype(BF16)
    return (jnp.dot(hi, m01, preferred_element_type=F32)
            + jnp.dot(lo, m01, preferred_element_type=F32))


GS = 256


def _group_sum(v, g01):
    parts = [_split_dot(v[:, c:c + GS], g01) for c in range(0, v.shape[1], GS)]
    return parts[0] if len(parts) == 1 else jnp.concatenate(parts, axis=1)


def _exact_dot01(m01, v):
    p1 = v.astype(BF16)
    r1 = v - p1.astype(F32)
    p2 = r1.astype(BF16)
    p3 = (r1 - p2.astype(F32)).astype(BF16)
    return (jnp.dot(m01, p1, preferred_element_type=F32) + jnp.dot(m01, p2, preferred_element_type=F32)
            + jnp.dot(m01, p3, preferred_element_type=F32))


def _rms_fwd(v, g):
    r = lax.rsqrt(jnp.mean(v * v, axis=-1, keepdims=True) + EPS)
    n = v * r
    return n * g, n, r


def _rms_bwd(do, n, r, g):
    dn = do * g
    return r * (dn - n * jnp.mean(dn * n, axis=-1, keepdims=True)), do * n


def _padded_column(n):
    if n < AW:
        return OFF_Q + n, 0.125
    if n < 3 * AW:
        return n, 1.0
    if n < 3 * AW + H:
        return OFF_F + n - 3 * AW, 1.0
    return OFF_BCU + n - 3 * AW - H, 1.0


def _in_layout_tables():
    dest = -np.ones((IN_PAD, LANES), np.int32)
    dest_f = -np.ones((IN_PAD, LANES), np.int32)
    scale = np.zeros((IN_PAD, LANES), np.float32)
    starts = []
    for k in range(NDEV):
        cols = [_padded_column(IN_COLS * k + j) for j in range(IN_COLS)]
        main = [c for c, _ in cols if c < OFF_F]
        ws = min((min(main) // LANES) * LANES, OFF_F - WIN)
        assert ws <= min(main) and max(main) < ws + WIN
        starts.append(ws)
        for j, (c, sc) in enumerate(cols):
            scale[j, k] = sc
            if c < OFF_F:
                dest[j, k] = c - ws
            else:
                dest_f[j, k] = c - OFF_F
    f_shards = tuple(k for k in range(NDEV) if (dest_f[:, k] >= 0).any())
    return tuple(starts), f_shards, jnp.asarray(dest), jnp.asarray(dest_f), jnp.asarray(scale)


def _perm(dest_ref, scale_ref, k, width):
    lane = lax.broadcasted_iota(jnp.int32, (IN_PAD, width), 1)
    return jnp.where(dest_ref[:, k:k + 1] == lane, scale_ref[:, k:k + 1], 0.0).astype(BF16)


def _assemble_w_in(blocks, tables, *, tr):
    starts, f_shards, dest, dest_f, scale = tables

    def body(b_ref, dest_ref, destf_ref, scale_ref, o_ref):
        o_ref[...] = jnp.zeros_like(o_ref)
        for k in range(NDEV):
            b = b_ref[k]
            ws = starts[k]
            part = jnp.dot(b, _perm(dest_ref, scale_ref, k, WIN), preferred_element_type=F32)
            o_ref[:, ws:ws + WIN] += part.astype(BF16)
            if k in f_shards:
                part = jnp.dot(b, _perm(destf_ref, scale_ref, k, 128), preferred_element_type=F32)
                o_ref[:, OFF_F:WP] += part.astype(BF16)

    tab = _full((IN_PAD, LANES))
    return pl.pallas_call(
        body, name="assemble_w_in", grid=(D // tr,),
        in_specs=[pl.BlockSpec((NDEV, tr, IN_PAD), lambda i: (0, i, 0)), tab, tab, tab],
        out_specs=_rows(tr, WP),
        out_shape=jax.ShapeDtypeStruct((D, WP), BF16),
        compiler_params=_cparams(48, ("arbitrary",)),
    )(blocks, dest, dest_f, scale)


def _disassemble_w_in(dwp, tables, *, tr):
    starts, f_shards, dest, dest_f, scale = tables
    width = dwp.shape[1]

    def body(g_ref, dest_ref, destf_ref, scale_ref, o_ref):
        for k in range(NDEV):
            ws = starts[k]
            acc = lax.dot_general(g_ref[:, ws:ws + WIN], _perm(dest_ref, scale_ref, k, WIN), NT, preferred_element_type=F32)
            if k in f_shards:
                acc = acc + lax.dot_general(g_ref[:, OFF_F:WP], _perm(destf_ref, scale_ref, k, 128), NT,
                                            preferred_element_type=F32)
            o_ref[k] = acc.astype(BF16)

    tab = _full((IN_PAD, LANES))
    return pl.pallas_call(
        body, name="disassemble_w_in", grid=(D // tr,),
        in_specs=[_rows(tr, width), tab, tab, tab],
        out_specs=pl.BlockSpec((NDEV, tr, IN_PAD), lambda i: (0, i, 0)),
        out_shape=jax.ShapeDtypeStruct((NDEV, D, IN_PAD), BF16),
        compiler_params=_cparams(48, ("arbitrary",)),
    )(dwp, dest, dest_f, scale)


def _in_proj(x, g1, wp, bfp, pq, pk, oq, ok, *, tm):
    s = x.shape[0]

    def body(x_ref, g_ref, w_ref, bf_ref, pq_ref, pk_ref, oq_ref, ok_ref,
             ht_ref, qp_ref, kp_ref, v_ref, bcu_ref, z_ref, carry):
        @pl.when(pl.program_id(0) == 0)
        def _():
            carry[...] = jnp.zeros_like(carry)

        h = _rms_fwd(x_ref[...], g_ref[...])[0].astype(BF16)
        ht_ref[...] = h.T
        z = jnp.dot(h, w_ref[:, OFF_F:WP], preferred_element_type=F32) + bf_ref[...]
        z_ref[...] = z
        lane = lax.broadcasted_iota(jnp.int32, (tm, 128), 1)
        logf = jnp.where(lane < H, jnp.minimum(z, 0.0) - jnp.log(1.0 + jnp.exp(-jnp.abs(z))), 0.0)
        row = lax.broadcasted_iota(jnp.int32, (tm, tm), 0)
        col = lax.broadcasted_iota(jnp.int32, (tm, tm), 1)
        tri = (col <= row).astype(BF16)
        c = _exact_dot01(tri, logf) + carry[0:1, :]
        carry[...] = jnp.broadcast_to(c[tm - 1:tm, :], carry.shape)
        c1 = c.astype(BF16).astype(F32)
        r1 = c - c1
        c2 = r1.astype(BF16).astype(F32)
        c3 = (r1 - c2).astype(BF16).astype(F32)
        zc = (c1 + pltpu.roll(c2, 8, axis=1) + pltpu.roll(c3, 16, axis=1)).astype(BF16)

        def pad_heads(v):
            blocks = []
            for pair in range(H // 2):
                two = v[:, 128 * pair:128 * (pair + 1)]
                blocks.append(jnp.where(lane < DH, two, 0.0))
                blocks.append(jnp.where(lane < DH, pltpu.roll(two, DH, axis=1), 0.0))
            return jnp.concatenate(blocks, axis=1)

        q = jnp.dot(h, w_ref[:, OFF_Q:OFF_K], preferred_element_type=F32)
        qp_ref[...] = (pad_heads(q) + jnp.dot(zc, pq_ref[...], preferred_element_type=F32) + oq_ref[...]).astype(BF16)
        k = jnp.dot(h, w_ref[:, OFF_K:OFF_V], preferred_element_type=F32)
        kp_ref[...] = (pad_heads(k) + jnp.dot(zc, pk_ref[...], preferred_element_type=F32) + ok_ref[...]).astype(BF16)
        v = pad_heads(jnp.dot(h, w_ref[:, OFF_V:OFF_BCU], preferred_element_type=F32))
        ones_lane = lax.broadcasted_iota(jnp.int32, (tm, H * HP), 1) % HP == DH
        v_ref[...] = jnp.where(ones_lane, 1.0, v).astype(BF16)
        bcu_ref[...] = jnp.dot(h, w_ref[:, OFF_BCU:OFF_F], preferred_element_type=F32).astype(BF16)

    return pl.pallas_call(
        body, name="in_proj", grid=(s // tm,),
        in_specs=[_rows(tm, D), _full((1, D)), _resident((D, WP)), _full((1, 128)),
                  _full((128, 1024)), _full((128, 1024)), _full((1, 1024)), _full((1, 1024))],
        out_specs=[pl.BlockSpec((D, tm), lambda i: (0, i)), _rows(tm, 1024), _rows(tm, 1024), _rows(tm, 1024),
                   _rows(tm, 3 * CW), _rows(tm, 128)],
        out_shape=[jax.ShapeDtypeStruct((D, s), BF16), jax.ShapeDtypeStruct((s, 1024), BF16),
                   jax.ShapeDtypeStruct((s, 1024), BF16), jax.ShapeDtypeStruct((s, 1024), BF16),
                   jax.ShapeDtypeStruct((s, 3 * CW), BF16), jax.ShapeDtypeStruct((s, 128), F32)],
        scratch_shapes=[pltpu.VMEM((SUBLANES, 128), F32)],
        compiler_params=_cparams(56, ("arbitrary",)),
    )(x, g1, wp, bfp, pq, pk, oq, ok)


def _attn_fwd(qp, kp, v, *, t):
    s = qp.shape[0]
    nq = s // t

    def body(q_ref, k_ref, v_ref, o_ref, lse_ref, mk_ref):
        qi = pl.program_id(1)
        row = lax.broadcasted_iota(jnp.int32, (t, t), 0)
        col = lax.broadcasted_iota(jnp.int32, (t, t), 1)
        lane = lax.broadcasted_iota(jnp.int32, (t, 128), 1)

        def head_step(hh, ki, carry, masked):
            m, acc = carry
            off = pl.multiple_of(ki * t, t)
            q = q_ref[:, HP * hh:HP * (hh + 1)]
            k = k_ref[pl.ds(off, t), HP * hh:HP * (hh + 1)]
            sc = lax.dot_general(q, k, NT, preferred_element_type=F32)
            if masked:
                sc = jnp.where(col <= row, sc, -1e30)
            mn = jnp.maximum(m, jnp.max(sc, axis=-1, keepdims=True))
            p = jnp.exp(sc - mn).astype(BF16)
            acc = jnp.exp(m - mn) * acc + jnp.dot(p, v_ref[pl.ds(off, t), HP * hh:HP * (hh + 1)],
                                                  preferred_element_type=F32)
            return mn, acc

        def step(ki, carry, masked):
            new = tuple(head_step(hh, ki, carry[hh], masked) for hh in range(2))
            mk_ref[ki] = jnp.where(lane < DH, jnp.broadcast_to(new[0][0], (t, 128)), jnp.broadcast_to(new[1][0], (t, 128)))
            return new

        init = (jnp.full((t, 1), -1e30, F32), jnp.zeros((t, 128), F32))
        carry = lax.fori_loop(0, qi, functools.partial(step, masked=False), (init, init))
        (m0, acc0), (m1, acc1) = step(qi, carry, True)
        l0, l1 = acc0[:, DH:DH + 1], acc1[:, DH:DH + 1]
        o_ref[...] = jnp.where(lane < DH, acc0 / l0, pltpu.roll(acc1 / l1, DH, axis=1))
        lse_ref[...] = jnp.where(lane < DH, jnp.broadcast_to(m0 + jnp.log(l0), (t, 128)),
                                 jnp.broadcast_to(m1 + jnp.log(l1), (t, 128)))

    return pl.pallas_call(
        body, name="attn_fwd", grid=(H // 2, nq),
        in_specs=[pl.BlockSpec((t, 2 * HP), lambda p, i: (i, p)),
                  pl.BlockSpec((s, 2 * HP), lambda p, i: (0, p)),
                  pl.BlockSpec((s, 2 * HP), lambda p, i: (0, p))],
        out_specs=[pl.BlockSpec((t, 128), lambda p, i: (i, p)), pl.BlockSpec((t, 128), lambda p, i: (i, p)),
                   pl.BlockSpec((nq, t, 128), lambda p, i: (0, i, p))],
        out_shape=[jax.ShapeDtypeStruct((s, AW), F32), jax.ShapeDtypeStruct((s, AW), F32),
                   jax.ShapeDtypeStruct((nq, s, AW), F32)],
        compiler_params=_cparams(48, ("arbitrary", "arbitrary")),
    )(qp, kp, v)


HALO = 16


def _conv_taps(bcu_ref, halo_ref, first, tm):
    z = bcu_ref[:, CW:2 * CW].astype(F32) * bcu_ref[:, 2 * CW:3 * CW].astype(F32)
    zh = jnp.where(first, 0.0, halo_ref[:, CW:2 * CW].astype(F32) * halo_ref[:, 2 * CW:3 * CW].astype(F32))
    row = lax.broadcasted_iota(jnp.int32, (tm, CW), 0)
    last, before = zh[HALO - 1:HALO, :], zh[HALO - 2:HALO - 1, :]
    z1 = jnp.where(row == 0, last, pltpu.roll(z, 1, axis=0))
    z2 = jnp.where(row == 0, before, jnp.where(row == 1, last, pltpu.roll(z, 2, axis=0)))
    return z, z1, z2


def _halo_before(tm, width):
    return pl.BlockSpec((HALO, width), lambda i: (jnp.maximum(i * (tm // HALO) - 1, 0), 0))


def _mix_out(o, bcu, cw8, ga, gc, gsum, w_out, x, g_post, g_ffn_pre, *, tm):
    s = x.shape[0]

    def body(o_ref, bcu_ref, halo_ref, cw_ref, ga_ref, gc_ref, gs_ref, w_ref, x_ref, g_ref, gf_ref,
             merged_ref, y_ref, x2_ref, cv_ref, h2_ref):
        z, z1, z2 = _conv_taps(bcu_ref, halo_ref, pl.program_id(0) == 0, tm)
        cv = cw_ref[0:1, :] * z2 + cw_ref[1:2, :] * z1 + cw_ref[2:3, :] * z
        cv_ref[...] = cv
        conv = bcu_ref[:, 0:CW].astype(F32) * cv
        ov = o_ref[...]
        ra = lax.rsqrt(_group_sum(ov * ov, gs_ref[...]) * (1.0 / DH) + EPS)
        rc = lax.rsqrt(_group_sum(conv * conv, gs_ref[...]) * (1.0 / DH) + EPS)
        merged = jnp.concatenate([ov * ra * ga_ref[...], conv * rc * gc_ref[...]], axis=1).astype(BF16)
        merged_ref[...] = merged
        y = jnp.dot(merged, w_ref[...], preferred_element_type=F32)
        y_ref[...] = y
        x2 = x_ref[...] + _rms_fwd(y, g_ref[...])[0]
        x2_ref[...] = x2
        h2_ref[...] = _rms_fwd(x2, gf_ref[...])[0].astype(BF16)

    return pl.pallas_call(
        body, name="mix_out", grid=(s // tm,),
        in_specs=[_rows(tm, AW), _rows(tm, 3 * CW), _halo_before(tm, 3 * CW), _full((SUBLANES, CW)),
                  _full((1, AW)), _full((1, CW)), _full((GS, GS)), _resident((D, D)), _rows(tm, D), _full((1, D)),
                  _full((1, D))],
        out_specs=[_rows(tm, D), _rows(tm, D), _rows(tm, D), _rows(tm, CW), _rows(tm, D)],
        out_shape=[jax.ShapeDtypeStruct((s, D), BF16), jax.ShapeDtypeStruct((s, D), F32),
                   jax.ShapeDtypeStruct((s, D), F32), jax.ShapeDtypeStruct((s, CW), F32),
                   jax.ShapeDtypeStruct((s, D), BF16)],
        compiler_params=_cparams(48, ("arbitrary",)),
    )(o, bcu, bcu, cw8, ga, gc, gsum, w_out, x, g_post, g_ffn_pre)


def _ffn_fwd_loss(h2, wgu, wd, x2, target, g_post, *, tm):
    s = x2.shape[0]

    def body(h_ref, w_ref, wd_ref, x2_ref, t_ref, g_ref,
             gate_ref, up_ref, a_ref, dx3_ref, dff_ref, loss_ref, dg_ref):
        @pl.when(pl.program_id(0) == 0)
        def _():
            loss_ref[...] = jnp.zeros_like(loss_ref)
            dg_ref[...] = jnp.zeros_like(dg_ref)

        h = h_ref[...]
        ff = None
        for j in range(4):
            gate = jnp.dot(h, w_ref[0, j], preferred_element_type=F32)
            up = jnp.dot(h, w_ref[1, j], preferred_element_type=F32)
            gate_ref[j] = gate.astype(BF16)
            up_ref[j] = up.astype(BF16)
            act = (gate * jax.nn.sigmoid(gate) * up).astype(BF16)
            a_ref[j] = act
            part = jnp.dot(act, wd_ref[j], preferred_element_type=F32)
            ff = part if ff is None else ff + part
        out, n, r = _rms_fwd(ff, g_ref[...])
        e = x2_ref[...] + out - t_ref[...]
        loss_ref[...] += _fold8(e * e)
        dx3 = e * (1.0 / D)
        dx3_ref[...] = dx3
        dff, dg = _rms_bwd(dx3, n, r, g_ref[...])
        dff_ref[...] = dff.astype(BF16)
        dg_ref[...] += _fold8(dg)

    blk4 = pl.BlockSpec((4, tm, FB), lambda i: (0, i, 0))
    return pl.pallas_call(
        body, name="ffn_fwd_loss", grid=(s // tm,),
        in_specs=[_rows(tm, D), _resident((2, 4, D, FB)), _resident((4, FB, D)), _rows(tm, D), _rows(tm, D), _full((1, D))],
        out_specs=[blk4, blk4, blk4, _rows(tm, D), _rows(tm, D), _full((SUBLANES, D)), _full((SUBLANES, D))],
        out_shape=[jax.ShapeDtypeStruct((4, s, FB), BF16)] * 3
        + [jax.ShapeDtypeStruct((s, D), F32), jax.ShapeDtypeStruct((s, D), BF16),
           jax.ShapeDtypeStruct((SUBLANES, D), F32), jax.ShapeDtypeStruct((SUBLANES, D), F32)],
        compiler_params=_cparams(56, ("arbitrary",)),
    )(h2, wgu, wd, x2, target, g_post)


def _ffn_bwd(dff, wd, gate, up, wgu, x2, g_pre, dx3, y, g_post, *, tm):
    s = x2.shape[0]

    def body(dff_ref, wd_ref, gate_ref, up_ref, w_ref, x2_ref, gpre_ref, dx3_ref, y_ref, gpost_ref,
             dgu_ref, dx2_ref, dy_ref, dgpre_ref, dgpost_ref):
        @pl.when(pl.program_id(0) == 0)
        def _():
            dgpre_ref[...] = jnp.zeros_like(dgpre_ref)
            dgpost_ref[...] = jnp.zeros_like(dgpost_ref)

        dff = dff_ref[...]
        dh2 = None
        for j in range(4):
            da = lax.dot_general(dff, wd_ref[j], NT, preferred_element_type=F32)
            g = gate_ref[j].astype(F32)
            sg = jax.nn.sigmoid(g)
            dgate = (da * up_ref[j].astype(F32) * (sg * (1.0 + g * (1.0 - sg)))).astype(BF16)
            dup = (da * (g * sg)).astype(BF16)
            dgu_ref[0, j] = dgate
            dgu_ref[1, j] = dup
            part = (lax.dot_general(dgate, w_ref[0, j], NT, preferred_element_type=F32)
                    + lax.dot_general(dup, w_ref[1, j], NT, preferred_element_type=F32))
            dh2 = part if dh2 is None else dh2 + part
        _, n2, r2 = _rms_fwd(x2_ref[...], gpre_ref[...])
        dxn, dg = _rms_bwd(dh2, n2, r2, gpre_ref[...])
        dgpre_ref[...] += _fold8(dg)
        dx2 = dx3_ref[...] + dxn
        dx2_ref[...] = dx2
        _, ny, ry = _rms_fwd(y_ref[...], gpost_ref[...])
        dy, dg2 = _rms_bwd(dx2, ny, ry, gpost_ref[...])
        dy_ref[...] = dy.astype(BF16)
        dgpost_ref[...] += _fold8(dg2)

    blk4 = pl.BlockSpec((4, tm, FB), lambda i: (0, i, 0))
    return pl.pallas_call(
        body, name="ffn_bwd", grid=(s // tm,),
        in_specs=[_rows(tm, D), _resident((4, FB, D)), blk4, blk4, _resident((2, 4, D, FB)), _rows(tm, D), _full((1, D)),
                  _rows(tm, D), _rows(tm, D), _full((1, D))],
        out_specs=[pl.BlockSpec((2, 4, tm, FB), lambda i: (0, 0, i, 0)), _rows(tm, D), _rows(tm, D),
                   _full((SUBLANES, D)), _full((SUBLANES, D))],
        out_shape=[jax.ShapeDtypeStruct((2, 4, s, FB), BF16), jax.ShapeDtypeStruct((s, D), F32),
                   jax.ShapeDtypeStruct((s, D), BF16), jax.ShapeDtypeStruct((SUBLANES, D), F32),
                   jax.ShapeDtypeStruct((SUBLANES, D), F32)],
        compiler_params=_cparams(56, ("arbitrary",)),
    )(dff, wd, gate, up, wgu, x2, g_pre, dx3, y, g_post)


def _grad_matmul(a, b, *, ta, tb, ts, name):
    s, ka = a.shape
    nb = b.shape[1]
    ts = min(ts, s)
    nk = s // ts

    def body(a_ref, b_ref, o_ref, acc):
        k = pl.program_id(2)

        @pl.when(k == 0)
        def _():
            acc[...] = jnp.zeros_like(acc)

        acc[...] += lax.dot_general(a_ref[...], b_ref[...], TN, preferred_element_type=F32)

        @pl.when(k == nk - 1)
        def _():
            o_ref[...] = acc[...].astype(BF16)

    return pl.pallas_call(
        body, name=name, grid=(ka // ta, nb // tb, nk),
        in_specs=[pl.BlockSpec((ts, ta), lambda i, j, k: (k, i)), pl.BlockSpec((ts, tb), lambda i, j, k: (k, j))],
        out_specs=pl.BlockSpec((ta, tb), lambda i, j, k: (i, j)),
        out_shape=jax.ShapeDtypeStruct((ka, nb), BF16),
        scratch_shapes=[pltpu.VMEM((ta, tb), F32)],
        compiler_params=_cparams(48, ("arbitrary", "arbitrary", "arbitrary")),
    )(a, b)


def _grad_matmul_t(at, b, *, tb, name):
    ka, s = at.shape
    blocked = b.ndim == 3
    nb = b.shape[-1]
    steps = b.shape[0] if blocked else nb // tb
    width = nb if blocked else tb

    def body(a_ref, b_ref, o_ref):
        bv = b_ref[0] if blocked else b_ref[...]
        res = jnp.dot(a_ref[...], bv, preferred_element_type=F32).astype(BF16)
        if blocked:
            o_ref[0] = res
        else:
            o_ref[...] = res

    if blocked:
        b_spec = pl.BlockSpec((1, s, nb), lambda j: (j, 0, 0))
        o_spec = pl.BlockSpec((1, ka, nb), lambda j: (j, 0, 0))
        o_shape = jax.ShapeDtypeStruct((steps, ka, nb), BF16)
    else:
        b_spec = pl.BlockSpec((s, width), lambda j: (0, j))
        o_spec = pl.BlockSpec((ka, width), lambda j: (0, j))
        o_shape = jax.ShapeDtypeStruct((ka, nb), BF16)
    return pl.pallas_call(
        body, name=name, grid=(steps,),
        in_specs=[_resident((ka, s)), b_spec], out_specs=o_spec, out_shape=o_shape,
        compiler_params=_cparams(56, ("arbitrary",)),
    )(at, b)


GW_TILE = 256


def _grad_w_in(h1t, pieces):
    ka, s = h1t.shape
    widths = [p.shape[1] for p in pieces]
    assert all(w % GW_TILE == 0 for w in widths)
    first = [sum(widths[:i]) // GW_TILE for i in range(len(pieces))]
    count = [w // GW_TILE for w in widths]

    def body(a_ref, *refs):
        o_ref = refs[-1]
        j = pl.program_id(0)
        for ref, f0, n in zip(refs[:-1], first, count):
            @pl.when((j >= f0) & (j < f0 + n))
            def _(ref=ref):
                o_ref[...] = jnp.dot(a_ref[...], ref[...], preferred_element_type=F32).astype(BF16)

    def spec(f0, n):
        return pl.BlockSpec((s, GW_TILE), lambda j: (0, jnp.clip(j - f0, 0, n - 1)))

    return pl.pallas_call(
        body, name="grad_w_in", grid=(sum(count),),
        in_specs=[_resident((ka, s))] + [spec(f0, n) for f0, n in zip(first, count)],
        out_specs=pl.BlockSpec((ka, GW_TILE), lambda j: (0, j)),
        out_shape=jax.ShapeDtypeStruct((ka, sum(widths)), BF16),
        compiler_params=_cparams(56, ("arbitrary",)),
    )(h1t, *pieces)


def _grad_matmul_blocks(a, b, *, ts, name):
    nblk = a.shape[0] if a.ndim == 3 else b.shape[0]
    s = a.shape[-2]
    ka, nb = a.shape[-1], b.shape[-1]
    ts = min(ts, s)
    nk = s // ts

    def body(a_ref, b_ref, o_ref, acc):
        k = pl.program_id(1)

        @pl.when(k == 0)
        def _():
            acc[...] = jnp.zeros_like(acc)

        av = a_ref[0] if a.ndim == 3 else a_ref[...]
        bv = b_ref[0] if b.ndim == 3 else b_ref[...]
        acc[...] += lax.dot_general(av, bv, TN, preferred_element_type=F32)

        @pl.when(k == nk - 1)
        def _():
            o_ref[0] = acc[...].astype(BF16)

    def spec(arr, width):
        if arr.ndim == 3:
            return pl.BlockSpec((1, ts, width), lambda j, k: (j, k, 0))
        return pl.BlockSpec((ts, width), lambda j, k: (k, 0))

    return pl.pallas_call(
        body, name=name, grid=(nblk, nk),
        in_specs=[spec(a, ka), spec(b, nb)],
        out_specs=pl.BlockSpec((1, ka, nb), lambda j, k: (j, 0, 0)),
        out_shape=jax.ShapeDtypeStruct((nblk, ka, nb), BF16),
        scratch_shapes=[pltpu.VMEM((ka, nb), F32)],
        compiler_params=_cparams(48, ("arbitrary", "arbitrary")),
    )(a, b)


def _mix_bwd(dy, w_out, o, cv, bcu, ga, gc, gsum, *, tm):
    s = dy.shape[0]

    def group_norm_bwd(dn_out, v, g, gs):
        r = lax.rsqrt(_group_sum(v * v, gs) * (1.0 / DH) + EPS)
        n = v * r
        dn = dn_out * g
        return r * (dn - n * (_group_sum(dn * n, gs) * (1.0 / DH))), dn_out * n

    def body(dy_ref, w_ref, o_ref, cv_ref, bcu_ref, ga_ref, gc_ref, gs_ref,
             do_ref, dl_ref, dcv_ref, db_ref, dga_ref, dgc_ref):
        @pl.when(pl.program_id(0) == 0)
        def _():
            dga_ref[...] = jnp.zeros_like(dga_ref)
            dgc_ref[...] = jnp.zeros_like(dgc_ref)

        dm = lax.dot_general(dy_ref[...], w_ref[...], NT, preferred_element_type=F32)
        ov = o_ref[...]
        do, dga = group_norm_bwd(dm[:, 0:AW], ov, ga_ref[...], gs_ref[...])
        dob = do.astype(BF16)
        do_ref[...] = dob
        dl_ref[...] = _group_sum(dob.astype(F32) * ov, gs_ref[...])
        dga_ref[...] += _fold8(dga)
        gate_b = bcu_ref[:, 0:CW].astype(F32)
        cv = cv_ref[...]
        dconv, dgc = group_norm_bwd(dm[:, AW:D], gate_b * cv, gc_ref[...], gs_ref[...])
        dgc_ref[...] += _fold8(dgc)
        dcv_ref[...] = dconv * gate_b
        db_ref[...] = (dconv * cv).astype(BF16)

    return pl.pallas_call(
        body, name="mix_bwd", grid=(s // tm,),
        in_specs=[_rows(tm, D), _resident((D, D)), _rows(tm, AW), _rows(tm, CW), _rows(tm, 3 * CW),
                  _full((1, AW)), _full((1, CW)), _full((GS, GS))],
        out_specs=[_rows(tm, AW), _rows(tm, AW), _rows(tm, CW), _rows(tm, CW),
                   _full((SUBLANES, AW)), _full((SUBLANES, CW))],
        out_shape=[jax.ShapeDtypeStruct((s, AW), BF16), jax.ShapeDtypeStruct((s, AW), F32),
                   jax.ShapeDtypeStruct((s, CW), F32), jax.ShapeDtypeStruct((s, CW), BF16),
                   jax.ShapeDtypeStruct((SUBLANES, AW), F32), jax.ShapeDtypeStruct((SUBLANES, CW), F32)],
        compiler_params=_cparams(48, ("arbitrary",)),
    )(dy, w_out, o, cv, bcu, ga, gc, gsum)


def _conv_bwd(dcv, db, bcu, cw8, *, tm):
    s = dcv.shape[0]
    nt = s // tm

    def body(dcv_ref, nxt_ref, db_ref, bcu_ref, halo_ref, cw_ref, dbcu_ref, dw_ref):
        i = pl.program_id(0)

        @pl.when(i == 0)
        def _():
            dw_ref[...] = jnp.zeros_like(dw_ref)

        z, z1, z2 = _conv_taps(bcu_ref, halo_ref, i == 0, tm)
        d = dcv_ref[...]
        dw_ref[0] += _fold8(d * z2)
        dw_ref[1] += _fold8(d * z1)
        dw_ref[2] += _fold8(d * z)
        nx = jnp.where(i == nt - 1, 0.0, nxt_ref[...])
        row = lax.broadcasted_iota(jnp.int32, (tm, CW), 0)
        d1 = jnp.where(row == tm - 1, nx[0:1, :], pltpu.roll(d, tm - 1, axis=0))
        d2 = jnp.where(row == tm - 2, nx[0:1, :], jnp.where(row == tm - 1, nx[1:2, :], pltpu.roll(d, tm - 2, axis=0)))
        dz = cw_ref[2:3, :] * d + cw_ref[1:2, :] * d1 + cw_ref[0:1, :] * d2
        dbcu_ref[:, 0:CW] = db_ref[...]
        dbcu_ref[:, CW:2 * CW] = (dz * bcu_ref[:, 2 * CW:3 * CW].astype(F32)).astype(BF16)
        dbcu_ref[:, 2 * CW:3 * CW] = (dz * bcu_ref[:, CW:2 * CW].astype(F32)).astype(BF16)

    return pl.pallas_call(
        body, name="conv_bwd", grid=(nt,),
        in_specs=[_rows(tm, CW),
                  pl.BlockSpec((SUBLANES, CW), lambda i: (jnp.minimum((i + 1) * (tm // SUBLANES), s // SUBLANES - 1), 0)),
                  _rows(tm, CW), _rows(tm, 3 * CW), _halo_before(tm, 3 * CW), _full((SUBLANES, CW))],
        out_specs=[_rows(tm, 3 * CW), _full((3, SUBLANES, CW))],
        out_shape=[jax.ShapeDtypeStruct((s, 3 * CW), BF16), jax.ShapeDtypeStruct((3, SUBLANES, CW), F32)],
        compiler_params=_cparams(48, ("arbitrary",)),
    )(dcv, dcv, db, bcu, bcu, cw8)


def _attn_bwd(qp, kp, v, do, lse, dl, mk, *, t):
    s = qp.shape[0]
    nq = s // t

    def body(q_ref, k_ref, v_ref, do_ref, lse_ref, dl_ref, mk_ref, dq_ref, dk_ref, dv_ref, dkx_ref, dq_acc):
        ki = pl.program_id(1)

        @pl.when(ki == 0)
        def _():
            dq_acc[...] = jnp.zeros_like(dq_acc)

        row = lax.broadcasted_iota(jnp.int32, (t, t), 0)
        col = lax.broadcasted_iota(jnp.int32, (t, t), 1)
        lane = lax.broadcasted_iota(jnp.int32, (t, 128), 1)

        def head_step(hh, qi, carry, masked):
            dk, dv, cs = carry
            off = pl.multiple_of(qi * t, t)
            rows = pl.ds(off, t)
            kh = k_ref[:, HP * hh:HP * (hh + 1)]
            q = q_ref[rows, HP * hh:HP * (hh + 1)]
            m_col = mk_ref[0, rows, DH * hh:DH * hh + 1]
            scale = jnp.exp(m_col - lse_ref[rows, DH * hh:DH * hh + 1])
            do2 = do_ref[rows, :]
            dom = jnp.where(lane < DH, do2 if hh == 0 else pltpu.roll(do2, DH, axis=1), jnp.zeros((), BF16))
            sc = lax.dot_general(q, kh, NT, preferred_element_type=F32) - m_col
            if masked:
                sc = jnp.where(col <= row, sc, -1e30)
            pt = jnp.exp(sc).astype(BF16)
            dp = lax.dot_general(dom, v_ref[:, HP * hh:HP * (hh + 1)], NT, preferred_element_type=F32)
            ds32 = (pt.astype(F32) * scale) * (dp - dl_ref[rows, DH * hh:DH * hh + 1])
            ds = ds32.astype(BF16)
            cs = cs + _fold8(ds32)
            dv = dv + jnp.dot((dom.astype(F32) * scale).astype(BF16).T, pt, preferred_element_type=F32)
            dk = dk + jnp.dot(q.T, ds, preferred_element_type=F32)
            dq_acc[rows, HP * hh:HP * (hh + 1)] += jnp.dot(ds, kh, preferred_element_type=F32)
            return dk, dv, cs

        def step(qi, carry, masked):
            return tuple(head_step(hh, qi, carry[hh], masked) for hh in range(2))

        zero = (jnp.zeros((HP, t), F32), jnp.zeros((128, t), F32), jnp.zeros((SUBLANES, t), F32))
        carry = step(ki, (zero, zero), True)
        (dk0, dv0, cs0), (dk1, dv1, cs1) = lax.fori_loop(ki + 1, nq, functools.partial(step, masked=False), carry)
        def two_heads(a0, a1):
            return jnp.where(lane < DH, a0, pltpu.roll(a1, DH, axis=1))

        dk_ref[...] = two_heads(dk0.T, dk1.T).astype(BF16)
        dv_ref[...] = two_heads(dv0.T, dv1.T).astype(BF16)

        def as_column(cs):
            return lax.dot_general(cs, jnp.ones((SUBLANES, 128), F32), TN, precision=HIGHEST, preferred_element_type=F32)

        dkx_ref[...] = jnp.where(lane < DH, as_column(cs0), as_column(cs1))

        @pl.when(ki == nq - 1)
        def _():
            for c in range(s // t):
                rows = slice(c * t, (c + 1) * t)
                dq_ref[rows, :] = two_heads(dq_acc[rows, 0:HP], dq_acc[rows, HP:2 * HP]).astype(BF16)

    return pl.pallas_call(
        body, name="attn_bwd", grid=(H // 2, nq),
        in_specs=[pl.BlockSpec((s, 2 * HP), lambda p, i: (0, p)),
                  pl.BlockSpec((t, 2 * HP), lambda p, i: (i, p)),
                  pl.BlockSpec((t, 2 * HP), lambda p, i: (i, p)),
                  pl.BlockSpec((s, 128), lambda p, i: (0, p)),
                  pl.BlockSpec((s, 128), lambda p, i: (0, p)),
                  pl.BlockSpec((s, 128), lambda p, i: (0, p)),
                  pl.BlockSpec((1, s, 128), lambda p, i: (i, 0, p))],
        out_specs=[pl.BlockSpec((s, 128), lambda p, i: (0, p)),
                   pl.BlockSpec((t, 128), lambda p, i: (i, p)),
                   pl.BlockSpec((t, 128), lambda p, i: (i, p)),
                   pl.BlockSpec((t, 128), lambda p, i: (i, p))],
        out_shape=[jax.ShapeDtypeStruct((s, AW), BF16), jax.ShapeDtypeStruct((s, AW), BF16),
                   jax.ShapeDtypeStruct((s, AW), BF16), jax.ShapeDtypeStruct((s, AW), F32)],
        scratch_shapes=[pltpu.VMEM((s, 2 * HP), F32)],
        compiler_params=_cparams(56, ("arbitrary", "arbitrary")),
    )(qp, kp, v, do, lse, dl, mk)


def _forget_bwd(dkx, z, sel, *, tm):
    s = dkx.shape[0]
    nt = s // tm

    def body(dk_ref, z_ref, sel_ref, dfl_ref, dbf_ref, carry):
        @pl.when(pl.program_id(0) == 0)
        def _():
            carry[...] = jnp.zeros_like(carry)
            dbf_ref[...] = jnp.zeros_like(dbf_ref)

        dc = _split_dot(dk_ref[...], sel_ref[...])
        row = lax.broadcasted_iota(jnp.int32, (tm, tm), 0)
        col = lax.broadcasted_iota(jnp.int32, (tm, tm), 1)
        tri = (col >= row).astype(BF16)
        dlogf = _exact_dot01(tri, dc) + carry[0:1, :]
        carry[...] = jnp.broadcast_to(dlogf[0:1, :], carry.shape)
        dz = dlogf * (1.0 - jax.nn.sigmoid(z_ref[...]))
        dfl_ref[:, 0:128] = dz.astype(BF16)
        dfl_ref[:, 128:GW_TILE] = jnp.zeros((tm, GW_TILE - 128), BF16)
        dbf_ref[...] += _fold8(dz)

    rev = lambda i: (nt - 1 - i, 0)
    return pl.pallas_call(
        body, name="forget_bwd", grid=(nt,),
        in_specs=[pl.BlockSpec((tm, AW), rev), pl.BlockSpec((tm, 128), rev), _full((AW, 128))],
        out_specs=[pl.BlockSpec((tm, GW_TILE), rev), _full((SUBLANES, 128))],
        out_shape=[jax.ShapeDtypeStruct((s, GW_TILE), BF16), jax.ShapeDtypeStruct((SUBLANES, 128), F32)],
        scratch_shapes=[pltpu.VMEM((SUBLANES, 128), F32)],
        compiler_params=_cparams(48, ("arbitrary",)),
    )(dkx, z, sel)


def _in_proj_bwd(pieces, wp, x, g1, dx2, *, tm):
    s = x.shape[0]

    def body(q_ref, k_ref, v_ref, bcu_ref, f_ref, w_ref, x_ref, g_ref, dx2_ref, dx_ref, dg_ref):
        @pl.when(pl.program_id(0) == 0)
        def _():
            dg_ref[...] = jnp.zeros_like(dg_ref)

        dh = None
        for ref, (lo, hi) in zip((q_ref, k_ref, v_ref, bcu_ref, f_ref), PIECES):
            part = lax.dot_general(ref[...], w_ref[:, lo:hi], NT, preferred_element_type=F32)
            dh = part if dh is None else dh + part
        _, n, r = _rms_fwd(x_ref[...], g_ref[...])
        dxn, dg = _rms_bwd(dh, n, r, g_ref[...])
        dx_ref[...] = dx2_ref[...] + dxn
        dg_ref[...] += _fold8(dg)

    return pl.pallas_call(
        body, name="in_proj_bwd", grid=(s // tm,),
        in_specs=[_rows(tm, hi - lo) for lo, hi in PIECES] + [_resident((D, WP)), _rows(tm, D), _full((1, D)), _rows(tm, D)],
        out_specs=[_rows(tm, D), _full((SUBLANES, D))],
        out_shape=[jax.ShapeDtypeStruct((s, D), F32), jax.ShapeDtypeStruct((SUBLANES, D), F32)],
        compiler_params=_cparams(56, ("arbitrary",)),
    )(*pieces, wp, x, g1, dx2)


def _position():
    return lax.axis_index("x"), lax.axis_index("y"), lax.axis_index("c")


ANY = pl.BlockSpec(memory_space=pl.ANY)


def _all_gather(shards):
    n = len(shards)

    def body(*refs):
        x_refs, out_refs = refs[:n], refs[n:2 * n]
        send_sems, recv_sems, local_sems = refs[2 * n:]
        x, y, c = _position()
        me, sibling = (x, y, c), (x, y, 1 - c)
        chips = [(1 - x, y), (x, 1 - y), (1 - x, 1 - y)]

        def copy(a, k, block, to, own=False):
            slot = out_refs[a].at[4 * block[0] + 2 * block[1] + block[2]]
            return pltpu.make_async_remote_copy(
                src_ref=x_refs[a] if own else slot, dst_ref=slot,
                send_sem=send_sems.at[7 * a + k], recv_sem=recv_sems.at[7 * a + k], device_id=to, device_id_type=MESH_ID)

        mine = [pltpu.make_async_copy(x_refs[a], out_refs[a].at[4 * x + 2 * y + c], local_sems.at[a]) for a in range(n)]
        for cp in mine:
            cp.start()
        first = []
        for a in range(n):
            first.append(copy(a, 0, me, sibling, own=True))
            first += [copy(a, 1 + j, me, (*chip, c), own=True) for j, chip in enumerate(chips)]
        for cp in first:
            cp.start()
        passed = []
        for j, chip in enumerate(chips):
            for a in range(n):
                copy(a, 1 + j, (*chip, c), me).wait_recv()
                fwd = copy(a, 4 + j, (*chip, c), sibling)
                fwd.start()
                passed.append(fwd)
        for a in range(n):
            copy(a, 0, sibling, me).wait_recv()
            for j, chip in enumerate(chips):
                copy(a, 4 + j, (*chip, 1 - c), me).wait_recv()
        for cp in first + passed:
            cp.wait_send()
        for cp in mine:
            cp.wait()

    return pl.pallas_call(
        body, name="all_gather_weights",
        out_shape=[jax.ShapeDtypeStruct((NDEV,) + sh.shape, sh.dtype) for sh in shards],
        in_specs=[ANY] * n, out_specs=[ANY] * n,
        scratch_shapes=[pltpu.SemaphoreType.DMA((7 * n,)), pltpu.SemaphoreType.DMA((7 * n,)), pltpu.SemaphoreType.DMA((n,))],
    )(*shards)


def _pair_exchange(grads):
    n = len(grads)

    def body(*refs):
        g_refs, out_refs = refs[:n], refs[n:2 * n]
        send_sems, recv_sems = refs[2 * n:]
        x, y, c = _position()
        copies = [pltpu.make_async_remote_copy(
            src_ref=g_refs[a].at[:, pl.ds(1 - c, 1)], dst_ref=out_refs[a], send_sem=send_sems.at[a],
            recv_sem=recv_sems.at[a], device_id=(x, y, 1 - c), device_id_type=MESH_ID) for a in range(n)]
        for cp in copies:
            cp.start()
        for cp in copies:
            cp.wait()

    return pl.pallas_call(
        body, name="grad_pair_exchange",
        out_shape=[jax.ShapeDtypeStruct((4, 1) + g.shape[2:], g.dtype) for g in grads],
        in_specs=[ANY] * n, out_specs=[ANY] * n,
        scratch_shapes=[pltpu.SemaphoreType.DMA((n,)), pltpu.SemaphoreType.DMA((n,))],
    )(*grads)


def _pair_sum(g, got, idx, *, tr, name):
    r, c = g.shape[2:]

    def body(idx_ref, g_ref, got_ref, pb_ref, own_ref):
        p = g_ref[0, 0].astype(F32) + got_ref[0, 0].astype(F32)
        pb_ref[0] = p.astype(BF16)

        @pl.when(pl.program_id(1) == idx_ref[1])
        def _():
            own_ref[...] = p

    return pl.pallas_call(
        body, name=name,
        grid_spec=pltpu.PrefetchScalarGridSpec(
            num_scalar_prefetch=1, grid=(r // tr, 4),
            in_specs=[pl.BlockSpec((1, 1, tr, c), lambda i, j, idx: (j, idx[0], i, 0)),
                      pl.BlockSpec((1, 1, tr, c), lambda i, j, idx: (j, 0, i, 0))],
            out_specs=[pl.BlockSpec((1, tr, c), lambda i, j, idx: (j, i, 0)),
                       pl.BlockSpec((tr, c), lambda i, j, idx: (i, 0))]),
        out_shape=[jax.ShapeDtypeStruct((4, r, c), BF16), jax.ShapeDtypeStruct((r, c), F32)],
        compiler_params=_cparams(32, ("arbitrary", "arbitrary")),
    )(idx, g, got)


HBM = pl.BlockSpec(memory_space=pltpu.HBM)
SEM = pl.BlockSpec(memory_space=pltpu.SEMAPHORE)
DATAFLOW = pltpu.SideEffectType.DATAFLOW_SIDE_EFFECTING


PEERS = {"gather": NDEV - 1, "scatter": NDEV - 1, "chips": 3}


def _exchange_copies(src_refs, land_refs, send_sems, recv_sems, mode):
    x, y, c = _position()
    me, my_chip = 4 * x + 2 * y + c, 2 * x + y
    npeers = PEERS[mode]
    copies = []
    for a, (s_ref, l_ref) in enumerate(zip(src_refs, land_refs)):
        for k in range(npeers):
            if mode == "chips":
                px, py, pc = x ^ ((k + 1) >> 1), y ^ ((k + 1) & 1), c
                src, dst = s_ref.at[2 * px + py], l_ref.at[my_chip]
            else:
                px, py, pc = x ^ ((k + 1) >> 2), y ^ (((k + 1) >> 1) & 1), c ^ ((k + 1) & 1)
                src, dst = (s_ref.at[4 * px + 2 * py + pc] if mode == "scatter" else s_ref), l_ref.at[me]
            copies.append(pltpu.make_async_remote_copy(
                src_ref=src, dst_ref=dst, send_sem=send_sems.at[npeers * a + k], recv_sem=recv_sems.at[npeers * a + k],
                device_id=(px, py, pc), device_id_type=MESH_ID))
    return copies


def _exchange_start(srcs, lands, after, *, mode, name):
    n = len(srcs)
    nsem = PEERS[mode] * n

    def body(*refs):
        token = refs[-1]
        for cp in _exchange_copies(refs[:n], refs[n:2 * n], refs[2 * n + 1], refs[2 * n + 2], mode):
            cp.start()
        token[...] = jnp.zeros_like(token)

    arrays = list(srcs) + list(lands)
    outs = pl.pallas_call(
        body, name=name,
        out_shape=(pltpu.SemaphoreType.DMA((nsem,)), pltpu.SemaphoreType.DMA((nsem,)),
                   *[pltpu.HBM(a.shape, a.dtype) for a in arrays], jax.ShapeDtypeStruct((SUBLANES, LANES), F32)),
        in_specs=[HBM] * (2 * n) + [ANY],
        out_specs=(SEM, SEM, *[HBM] * (2 * n), pl.BlockSpec(memory_space=pltpu.VMEM)),
        input_output_aliases={i: 2 + i for i in range(2 * n)},
        compiler_params=pltpu.CompilerParams(has_side_effects=DATAFLOW),
    )(*[pltpu.with_memory_space_constraint(a, pltpu.HBM) for a in arrays], after)
    return outs[0], outs[1], outs[2:2 + n], outs[2 + n:2 + 2 * n], outs[-1]


def _exchange_wait(send_sems, recv_sems, srcs, lands, after, *, mode, name):
    n = len(srcs)

    def body(*refs):
        for cp in _exchange_copies(refs[:n], refs[n:2 * n], refs[2 * n], refs[2 * n + 1], mode):
            cp.wait_send()
            cp.wait_recv()

    arrays = list(srcs) + list(lands)
    outs = pl.pallas_call(
        body, name=name,
        out_shape=tuple(pltpu.HBM(a.shape, a.dtype) for a in arrays),
        in_specs=[HBM] * (2 * n) + [SEM, SEM, ANY],
        out_specs=tuple([HBM] * (2 * n)),
        input_output_aliases={i: i for i in range(2 * n)},
        compiler_params=pltpu.CompilerParams(has_side_effects=DATAFLOW),
    )(*arrays, send_sems, recv_sems, after)
    return outs[n:]


def _own_slot(value, me):
    return lax.dynamic_update_index_in_dim(lax.empty((NDEV,) + value.shape, value.dtype), value, me, 0)


def _small_all_reduce(parts):
    def body(gmp_ref, gmo_ref, gfp_ref, gfo_ref, ga_ref, gc_ref, dw_ref, bf_ref, loss_ref,
             out_ref, buf, send_sems, recv_sems):
        x, y, c = _position()
        me = 4 * x + 2 * y + c

        def colsum(v):
            return jnp.sum(v, axis=0, keepdims=True)

        loss = jnp.sum(colsum(loss_ref[...]), axis=1, keepdims=True) * (0.5 / D)
        rows = [colsum(gmp_ref[...]), colsum(gmo_ref[...]), colsum(gfp_ref[...]), colsum(gfo_ref[...]),
                jnp.concatenate([colsum(ga_ref[...]), colsum(gc_ref[...])], axis=1),
                jnp.concatenate([colsum(dw_ref[0]), colsum(dw_ref[1])], axis=1),
                jnp.concatenate([colsum(dw_ref[2]), colsum(bf_ref[...]), jnp.broadcast_to(loss, (1, 128)),
                                 jnp.zeros((1, 256), F32)], axis=1),
                jnp.zeros((1, D), F32)]
        buf[me] = jnp.concatenate(rows, axis=0)
        copies = []
        for mm in range(1, NDEV):
            peer = (x ^ (mm >> 2), y ^ ((mm >> 1) & 1), c ^ (mm & 1))
            copies.append(pltpu.make_async_remote_copy(
                src_ref=buf.at[me], dst_ref=buf.at[me], send_sem=send_sems.at[mm - 1], recv_sem=recv_sems.at[mm - 1],
                device_id=peer, device_id_type=MESH_ID))
        for cp in copies:
            cp.start()
        for cp in copies:
            cp.wait_recv()
        for cp in copies:
            cp.wait_send()
        acc = buf[0]
        for d in range(1, NDEV):
            acc = acc + buf[d]
        out_ref[...] = acc

    vm = pl.BlockSpec(memory_space=pltpu.VMEM)
    return pl.pallas_call(
        body, name="small_all_reduce",
        out_shape=jax.ShapeDtypeStruct((SUBLANES, D), F32),
        in_specs=[vm] * len(parts), out_specs=vm,
        scratch_shapes=[pltpu.VMEM((NDEV, SUBLANES, D), F32), pltpu.SemaphoreType.DMA((7,)), pltpu.SemaphoreType.DMA((7,))],
    )(*parts)


def _adam_update(w, g, m, v):
    nm = ADAM_B1 * m + (1.0 - ADAM_B1) * g
    nv = ADAM_B2 * v + (1.0 - ADAM_B2) * (g * g)
    m_hat = nm / (1.0 - ADAM_B1 ** ADAM_STEP)
    v_hat = nv / (1.0 - ADAM_B2 ** ADAM_STEP)
    return -ADAM_LR * (m_hat / (jnp.sqrt(v_hat) + ADAM_EPS) + ADAM_WD * w), nm, nv


def _adamw(w, g, m, v, *, tr, name):
    rows, cols = w.shape

    def body(w_ref, g_ref, m_ref, v_ref, d_ref, nm_ref, nv_ref):
        d_ref[...], nm_ref[...], nv_ref[...] = _adam_update(w_ref[...], g_ref[...], m_ref[...], v_ref[...])

    spec = pl.BlockSpec((tr, cols), lambda i: (i, 0))
    return pl.pallas_call(
        body, name=name, grid=(rows // tr,),
        in_specs=[spec] * 4, out_specs=[spec] * 3,
        out_shape=[jax.ShapeDtypeStruct((rows, cols), F32)] * 3,
        compiler_params=_cparams(32, ("arbitrary",)),
    )(w, g, m, v)


def _chip_sum_adamw(got, own, idx, w, m, v, *, tr, name):
    rows, cols = w.shape
    gcols = own.shape[1]

    def body(idx_ref, got_ref, own_ref, w_ref, m_ref, v_ref, g_ref, d_ref, nm_ref, nv_ref):
        g = jnp.zeros((tr, gcols), F32)
        for j in range(4):
            g = g + jnp.where(idx_ref[1] == j, own_ref[...], got_ref[j].astype(F32))
        g = g[:, :cols]
        g_ref[...] = g
        d_ref[...], nm_ref[...], nv_ref[...] = _adam_update(w_ref[...], g, m_ref[...], v_ref[...])

    spec = pl.BlockSpec((tr, cols), lambda i, idx: (i, 0))
    gspec = pl.BlockSpec((tr, gcols), lambda i, idx: (i, 0))
    return pl.pallas_call(
        body, name=name,
        grid_spec=pltpu.PrefetchScalarGridSpec(
            num_scalar_prefetch=1, grid=(rows // tr,),
            in_specs=[pl.BlockSpec((4, tr, gcols), lambda i, idx: (0, i, 0)), gspec, spec, spec, spec],
            out_specs=[spec] * 4),
        out_shape=[jax.ShapeDtypeStruct((rows, cols), F32)] * 4,
        compiler_params=_cparams(32, ("arbitrary",)),
    )(idx, got, own, w, m, v)


def _device_sum_adamw(land, w, m, v, *, tr, name):
    rows, cols = w.shape

    def body(land_ref, w_ref, m_ref, v_ref, g_ref, d_ref, nm_ref, nv_ref):
        g = land_ref[0].astype(F32)
        for dev in range(1, NDEV):
            g = g + land_ref[dev].astype(F32)
        g_ref[...] = g
        d_ref[...], nm_ref[...], nv_ref[...] = _adam_update(w_ref[...], g, m_ref[...], v_ref[...])

    spec = pl.BlockSpec((tr, cols), lambda i: (i, 0))
    return pl.pallas_call(
        body, name=name, grid=(rows // tr,),
        in_specs=[pl.BlockSpec((NDEV, tr, cols), lambda i: (0, i, 0)), spec, spec, spec],
        out_specs=[spec] * 4,
        out_shape=[jax.ShapeDtypeStruct((rows, cols), F32)] * 4,
        compiler_params=_cparams(32, ("arbitrary",)),
    )(land, w, m, v)


def _placement_constants():
    j = jnp.arange(128)[:, None]
    lane = jnp.arange(1024)[None, :]
    head, sub = lane // HP, lane % HP
    piece, jh = j // H, j % H
    valid = (j < 3 * H) & (jh == head)
    pq = jnp.where(valid & (sub == DH + piece), 1.0, 0.0).astype(BF16)
    pk = jnp.where(valid & (sub == DH + 3 + piece), -1.0, 0.0).astype(BF16)
    oq = jnp.where((sub >= DH + 3) & (sub < DH + 6), 1.0, 0.0).astype(F32)
    ok = jnp.where((sub >= DH) & (sub < DH + 3), 1.0, 0.0).astype(F32)
    r = jnp.arange(AW)[:, None]
    cc = jnp.arange(128)[None, :]
    sel = jnp.where((r % DH == 3) & (r // DH == cc), -1.0, 0.0).astype(BF16)
    gi = jnp.arange(GS)
    gsum = (gi[:, None] // DH == gi[None, :] // DH).astype(BF16)
    return pq, pk, oq, ok, sel, gsum


def _local_step(xs, tgt, wp, late_weights, cw8, bfp, g_attn_out, g_conv_out,
                g_mix_pre, g_mix_post, g_ffn_pre, g_ffn_post, early_grads=None, last_grad=None):
    pq, pk, oq, ok, sel, gsum = _placement_constants()
    h1t, qp, kp, vv, bcu, zf = _in_proj(xs, g_mix_pre, wp, bfp, pq, pk, oq, ok, tm=512)
    o, lse, mk = _attn_fwd(qp, kp, vv, t=512)
    w_out_f, wgu, wd = late_weights(lse)
    merged, y, x2, cv, h2 = _mix_out(o, bcu, cw8, g_attn_out, g_conv_out, gsum, w_out_f, xs, g_mix_post, g_ffn_pre, tm=512)
    gate, up, act, dx3, dff, loss_p, dg_ffn_post = _ffn_fwd_loss(h2, wgu, wd, x2, tgt, g_ffn_post, tm=512)

    dgu, dx2, dy, dg_ffn_pre, dg_mix_post = _ffn_bwd(dff, wd, gate, up, wgu, x2, g_ffn_pre, dx3, y, g_mix_post, tm=256)
    dw_down = _grad_matmul_blocks(act, dff, ts=4096, name="grad_w_down")
    dw_gu = _grad_matmul_blocks(dgu.reshape(NDEV, -1, FB), h2, ts=4096, name="grad_w_gate_up")
    dw_out = _grad_matmul(merged, dy, ta=1024, tb=1024, ts=2048, name="grad_w_out")
    token = early_grads(dw_out, dw_gu, dw_down) if early_grads is not None else None
    ga = g_attn_out if token is None else g_attn_out + token[0:1, 0:1]
    do, dl, dcv, db, dg_attn, dg_conv = _mix_bwd(dy, w_out_f, o, cv, bcu, ga, g_conv_out, gsum, tm=512)
    dbcu, dtaps = _conv_bwd(dcv, db, bcu, cw8, tm=512)
    dqp, dkp, dv, dkx = _attn_bwd(qp, kp, vv, do, lse, dl, mk, t=512)
    dfl, dbf = _forget_bwd(dkx, zf, sel, tm=512)
    pieces = (dqp, dkp, dv, dbcu, dfl)
    dwp = _grad_w_in(h1t, pieces)
    token = last_grad(dwp) if last_grad is not None else None
    g1 = g_mix_pre if token is None else g_mix_pre + token[0:1, 0:1]
    grad_x, dg_mix_pre = _in_proj_bwd(pieces, wp, xs, g1, dx2, tm=512)
    return (grad_x, dwp, dw_out, dw_gu, dw_down, dg_mix_pre, dg_mix_post, dg_ffn_pre, dg_ffn_post, dg_attn, dg_conv,
            dtaps, dbf, loss_p)


BIG_TILES = {"w_in": 256, "w_out": 128, "w_gate_up": 176, "w_down": 176}


def kernel(x, w_in, b_forget, conv_w, g_attn_out, g_conv_out, w_out, g_mix_pre, g_mix_post, w_gate_up, w_down, g_ffn_pre, g_ffn_post, loss_target, m_w_in, m_b_forget, m_conv_w, m_g_attn_out, m_g_conv_out, m_w_out, m_g_mix_pre, m_g_mix_post, m_w_gate_up, m_w_down, m_g_ffn_pre, m_g_ffn_post, v_w_in, v_b_forget, v_conv_w, v_g_attn_out, v_g_conv_out, v_w_out, v_g_mix_pre, v_g_mix_post, v_w_gate_up, v_w_down, v_g_ffn_pre, v_g_ffn_post):
    xc, yc, cc = _position()
    my_chip = 2 * xc + yc
    me = 2 * my_chip + cc
    idx = jnp.stack([cc, my_chip]).astype(jnp.int32)
    tables = _in_layout_tables()
    pad_in = lambda a: jnp.pad(a, ((0, 0), (0, IN_PAD - IN_COLS)))

    g_in, g_taps = _all_gather([pad_in(w_in[0]).astype(BF16), conv_w[0]])
    wp = _assemble_w_in(g_in, tables, tr=256)
    cw8 = jnp.pad(g_taps.transpose(1, 0, 2).reshape(3, CW), ((0, SUBLANES - 3), (0, 0)))

    late = [w_out[0].astype(BF16), w_gate_up[0].astype(BF16), w_down[0].astype(BF16)]
    ssem, rsem, late_thru, land_thru, token = _exchange_start(
        late, [_own_slot(s, me) for s in late], g_in, mode="gather", name="gather_late_start")
    bfp = jnp.pad(b_forget, ((0, 0), (0, 128 - H))) + token[0:1, :]

    def late_weights(after):
        l_out, l_gu, l_down = _exchange_wait(ssem, rsem, late_thru, land_thru, after, mode="gather", name="gather_late_wait")
        return l_out.reshape(D, D), l_gu.reshape(2, 4, D, FB), l_down.reshape(4, FB, D)

    early = {}

    def early_grads(dw_out, dw_gu, dw_down):
        srcs = [dw_out.reshape(NDEV, D // NDEV, D), dw_gu, dw_down.reshape(NDEV, DFF // NDEV, D)]
        lands = [_own_slot(lax.dynamic_index_in_dim(s, me, 0, keepdims=False), me) for s in srcs]
        early["handles"] = _exchange_start(srcs, lands, dw_out, mode="scatter", name="scatter_early_start")
        return early["handles"][4]

    last = {}

    def last_grad(dwp):
        g_w_in = _disassemble_w_in(dwp, tables, tr=256).reshape(4, 2, D, IN_PAD)
        (from_sibling,) = _pair_exchange([g_w_in])
        pair_b, last["own"] = _pair_sum(g_w_in, from_sibling, idx, tr=BIG_TILES["w_in"], name="grad_pair_sum_w_in")
        land = lax.dynamic_update_index_in_dim(lax.empty(pair_b.shape, pair_b.dtype),
                                               lax.dynamic_index_in_dim(pair_b, my_chip, 0, keepdims=False), my_chip, 0)
        last["handles"] = _exchange_start([pair_b], [land], last["own"], mode="chips", name="chips_w_in_start")
        return last["handles"][4]

    (grad_x, dwp, dw_out, dw_gu, dw_down, dg_mix_pre, dg_mix_post, dg_ffn_pre, dg_ffn_post, dg_attn, dg_conv,
     dtaps, dbf, loss_p) = _local_step(x[0], loss_target[0], wp, late_weights, cw8, bfp, g_attn_out, g_conv_out,
                                        g_mix_pre, g_mix_post, g_ffn_pre, g_ffn_post, early_grads, last_grad)

    e_ssem, e_rsem, e_srcs, e_lands, _ = early["handles"]
    land_out, land_gu, land_down = _exchange_wait(e_ssem, e_rsem, e_srcs, e_lands, dg_mix_pre, mode="scatter",
                                                  name="scatter_early_wait")
    res = {}
    big = {"w_out": (land_out, w_out[0], m_w_out[0], v_w_out[0]),
           "w_gate_up": (land_gu, w_gate_up[0].T, m_w_gate_up[0].T, v_w_gate_up[0].T),
           "w_down": (land_down, w_down[0], m_w_down[0], v_w_down[0])}
    for name, (land, w, m, v) in big.items():
        outs = _device_sum_adamw(land, w, m, v, tr=BIG_TILES[name], name="adamw_" + name)
        res[name] = [(o.T if name == "w_gate_up" else o)[None] for o in outs]
    c_ssem, c_rsem, c_srcs, c_lands, _ = last["handles"]
    after = sum(res[n][1][0, :SUBLANES, :LANES] for n in big)
    (from_chips,) = _exchange_wait(c_ssem, c_rsem, c_srcs, c_lands, after, mode="chips", name="chips_w_in_wait")
    outs = _chip_sum_adamw(from_chips, last["own"], idx, w_in[0], m_w_in[0], v_w_in[0],
                           tr=BIG_TILES["w_in"], name="adamw_w_in")
    res["w_in"] = [o[None] for o in outs]

    small = _small_all_reduce([dg_mix_pre, dg_mix_post, dg_ffn_pre, dg_ffn_post, dg_attn, dg_conv, dtaps, dbf, loss_p])
    taps_full = jnp.concatenate([small[5:6, :CW], small[5:6, CW:], small[6:7, :CW]], axis=0)
    small_grads = {
        "b_forget": small[6:7, CW:CW + H], "conv_w": lax.dynamic_slice(taps_full, (0, me * 64), (3, 64)),
        "g_attn_out": small[4:5, :AW], "g_conv_out": small[4:5, AW:], "g_mix_pre": small[0:1], "g_mix_post": small[1:2],
        "g_ffn_pre": small[2:3], "g_ffn_post": small[3:4]}
    loss = small[6, CW + 128]
    smalls = {"b_forget": (b_forget, m_b_forget, v_b_forget), "conv_w": (conv_w[0], m_conv_w[0], v_conv_w[0]),
              "g_attn_out": (g_attn_out, m_g_attn_out, v_g_attn_out), "g_conv_out": (g_conv_out, m_g_conv_out, v_g_conv_out),
              "g_mix_pre": (g_mix_pre, m_g_mix_pre, v_g_mix_pre), "g_mix_post": (g_mix_post, m_g_mix_post, v_g_mix_post),
              "g_ffn_pre": (g_ffn_pre, m_g_ffn_pre, v_g_ffn_pre), "g_ffn_post": (g_ffn_post, m_g_ffn_post, v_g_ffn_post)}
    for name, (w, m, v) in smalls.items():
        g = small_grads[name]
        outs = [g] + list(_adamw(w, g, m, v, tr=w.shape[0], name="adamw_" + name))
        res[name] = [o[None] for o in outs] if name == "conv_w" else outs

    order = ["w_in", "b_forget", "conv_w", "g_attn_out", "g_conv_out", "w_out", "g_mix_pre", "g_mix_post",
             "w_gate_up", "w_down", "g_ffn_pre", "g_ffn_post"]
    outs = [loss, grad_x[None]]
    for k in range(4):
        outs += [res[n][k] for n in order]
    return tuple(outs)
```

```python
import functools

import numpy as np

import jax
import jax.numpy as jnp
from jax import lax
from jax.experimental import pallas as pl
from jax.experimental.pallas import tpu as pltpu

F32 = jnp.float32
BF16 = jnp.bfloat16
MESH_ID = pl.DeviceIdType.MESH

D = 1024
H = 8
DH = 64
AW = 512
CW = 512
DFF = 2816
FB = DFF // 4
HP = 128
OFF_Q, OFF_K, OFF_V, OFF_BCU, OFF_F = 0, 512, 1024, 1536, 3072
WP = OFF_F + 128
PIECES = ((OFF_Q, OFF_K), (OFF_K, OFF_V), (OFF_V, OFF_BCU), (OFF_BCU, OFF_F), (OFF_F, WP))
EPS = 1e-6
NDEV = 8
LANES = 128
SUBLANES = 8
IN_COLS = 385
IN_PAD = 512
WIN = 640
ADAM_LR, ADAM_B1, ADAM_B2, ADAM_EPS, ADAM_WD, ADAM_STEP = 0.001, 0.9, 0.999, 1e-08, 0.01, 10

NT = (((1,), (1,)), ((), ()))
TN = (((0,), (0,)), ((), ()))


def _cparams(vmem_mb=None, sem=None):
    kw = {}
    if vmem_mb is not None:
        kw["vmem_limit_bytes"] = vmem_mb << 20
    if sem is not None:
        kw["dimension_semantics"] = sem
    return pltpu.CompilerParams(**kw)


def _full(shape):
    return pl.BlockSpec(shape, lambda *_: (0,) * len(shape))


def _resident(shape):
    return pl.BlockSpec(shape, lambda *_: (0,) * len(shape), pipeline_mode=pl.Buffered(1))


def _rows(tm, width):
    return pl.BlockSpec((tm, width), lambda i: (i, 0))


def _fold8(v):
    r, w = v.shape
    return jnp.sum(v.reshape(r // SUBLANES, SUBLANES, w), axis=0)


def _split_dot(v, m01):
    hi = v.astype(BF16)
    lo = (v - hi.astype(F32)).astype(BF16)
    return (jnp.dot(hi, m01, preferred_element_type=F32)
            + jnp.dot(lo, m01, preferred_element_type=F32))


GS = 256


def _group_sum(v, g01):
    parts = [_split_dot(v[:, c:c + GS], g01) for c in range(0, v.shape[1], GS)]
    return parts[0] if len(parts) == 1 else jnp.concatenate(parts, axis=1)


def _exact_dot01(m01, v):
    p1 = v.astype(BF16)
    r1 = v - p1.astype(F32)
    p2 = r1.astype(BF16)
    p3 = (r1 - p2.astype(F32)).astype(BF16)
    return (jnp.dot(m01, p1, preferred_element_type=F32) + jnp.dot(m01, p2, preferred_element_type=F32)
            + jnp.dot(m01, p3, preferred_element_type=F32))


def _rms_fwd(v, g):
    r = lax.rsqrt(jnp.mean(v * v, axis=-1, keepdims=True) + EPS)
    n = v * r
    return n * g, n, r


def _rms_bwd(do, n, r, g):
    dn = do * g
    return r * (dn - n * jnp.mean(dn * n, axis=-1, keepdims=True)), do * n


def _padded_column(n):
    if n < AW:
        return OFF_Q + n, 0.125
    if n < 3 * AW:
        return n, 1.0
    if n < 3 * AW + H:
        return OFF_F + n - 3 * AW, 1.0
    return OFF_BCU + n - 3 * AW - H, 1.0


def _in_layout_tables():
    dest = -np.ones((IN_PAD, LANES), np.int32)
    dest_f = -np.ones((IN_PAD, LANES), np.int32)
    scale = np.zeros((IN_PAD, LANES), np.float32)
    starts = []
    for k in range(NDEV):
        cols = [_padded_column(IN_COLS * k + j) for j in range(IN_COLS)]
        main = [c for c, _ in cols if c < OFF_F]
        ws = min((min(main) // LANES) * LANES, OFF_F - WIN)
        assert ws <= min(main) and max(main) < ws + WIN
        starts.append(ws)
        for j, (c, sc) in enumerate(cols):
            scale[j, k] = sc
            if c < OFF_F:
                dest[j, k] = c - ws
            else:
                dest_f[j, k] = c - OFF_F
    f_shards = tuple(k for k in range(NDEV) if (dest_f[:, k] >= 0).any())
    return tuple(starts), f_shards, jnp.asarray(dest), jnp.asarray(dest_f), jnp.asarray(scale)


def _perm(dest_ref, scale_ref, k, width):
    lane = lax.broadcasted_iota(jnp.int32, (IN_PAD, width), 1)
    return jnp.where(dest_ref[:, k:k + 1] == lane, scale_ref[:, k:k + 1], 0.0).astype(BF16)


def _assemble_w_in(blocks, tables, *, tr):
    starts, f_shards, dest, dest_f, scale = tables

    def body(b_ref, dest_ref, destf_ref, scale_ref, o_ref):
        o_ref[...] = jnp.zeros_like(o_ref)
        for k in range(NDEV):
            b = b_ref[k]
            ws = starts[k]
            part = jnp.dot(b, _perm(dest_ref, scale_ref, k, WIN), preferred_element_type=F32)
            o_ref[:, ws:ws + WIN] += part.astype(BF16)
            if k in f_shards:
                part = jnp.dot(b, _perm(destf_ref, scale_ref, k, 128), preferred_element_type=F32)
                o_ref[:, OFF_F:WP] += part.astype(BF16)

    tab = _full((IN_PAD, LANES))
    return pl.pallas_call(
        body, name="assemble_w_in", grid=(D // tr,),
        in_specs=[pl.BlockSpec((NDEV, tr, IN_PAD), lambda i: (0, i, 0)), tab, tab, tab],
        out_specs=_rows(tr, WP),
        out_shape=jax.ShapeDtypeStruct((D, WP), BF16),
        compiler_params=_cparams(48, ("arbitrary",)),
    )(blocks, dest, dest_f, scale)


def _disassemble_w_in(dwp, tables, *, tr):
    starts, f_shards, dest, dest_f, scale = tables
    width = dwp.shape[1]

    def body(g_ref, dest_ref, destf_ref, scale_ref, o_ref):
        for k in range(NDEV):
            ws = starts[k]
            acc = lax.dot_general(g_ref[:, ws:ws + WIN], _perm(dest_ref, scale_ref, k, WIN), NT, preferred_element_type=F32)
            if k in f_shards:
                acc = acc + lax.dot_general(g_ref[:, OFF_F:WP], _perm(destf_ref, scale_ref, k, 128), NT,
                                            preferred_element_type=F32)
            o_ref[k] = acc.astype(BF16)

    tab = _full((IN_PAD, LANES))
    return pl.pallas_call(
        body, name="disassemble_w_in", grid=(D // tr,),
        in_specs=[_rows(tr, width), tab, tab, tab],
        out_specs=pl.BlockSpec((NDEV, tr, IN_PAD), lambda i: (0, i, 0)),
        out_shape=jax.ShapeDtypeStruct((NDEV, D, IN_PAD), BF16),
        compiler_params=_cparams(48, ("arbitrary",)),
    )(dwp, dest, dest_f, scale)


def _in_proj(x, g1, wp, bfp, pq, pk, oq, ok, *, tm):
    s = x.shape[0]

    def body(x_ref, g_ref, w_ref, bf_ref, pq_ref, pk_ref, oq_ref, ok_ref,
             ht_ref, qp_ref, kp_ref, v_ref, bcu_ref, z_ref, carry):
        @pl.when(pl.program_id(0) == 0)
        def _():
            carry[...] = jnp.zeros_like(carry)

        h = _rms_fwd(x_ref[...], g_ref[...])[0].astype(BF16)
        ht_ref[...] = h.T
        z = jnp.dot(h, w_ref[:, OFF_F:WP], preferred_element_type=F32) + bf_ref[...]
        z_ref[...] = z
        lane = lax.broadcasted_iota(jnp.int32, (tm, 128), 1)
        logf = jnp.where(lane < H, jnp.minimum(z, 0.0) - jnp.log(1.0 + jnp.exp(-jnp.abs(z))), 0.0)
        row = lax.broadcasted_iota(jnp.int32, (tm, tm), 0)
        col = lax.broadcasted_iota(jnp.int32, (tm, tm), 1)
        tri = (col <= row).astype(BF16)
        c = _exact_dot01(tri, logf) + carry[0:1, :]
        carry[...] = jnp.broadcast_to(c[tm - 1:tm, :], carry.shape)
        c1 = c.astype(BF16).astype(F32)
        r1 = c - c1
        c2 = r1.astype(BF16).astype(F32)
        c3 = (r1 - c2).astype(BF16).astype(F32)
        zc = (c1 + pltpu.roll(c2, 8, axis=1) + pltpu.roll(c3, 16, axis=1)).astype(BF16)

        def pad_heads(v):
            blocks = []
            for pair in range(H // 2):
                two = v[:, 128 * pair:128 * (pair + 1)]
                blocks.append(jnp.where(lane < DH, two, 0.0))
                blocks.append(jnp.where(lane < DH, pltpu.roll(two, DH, axis=1), 0.0))
            return jnp.concatenate(blocks, axis=1)

        q = jnp.dot(h, w_ref[:, OFF_Q:OFF_K], preferred_element_type=F32)
        qp_ref[...] = (pad_heads(q) + jnp.dot(zc, pq_ref[...], preferred_element_type=F32) + oq_ref[...]).astype(BF16)
        k = jnp.dot(h, w_ref[:, OFF_K:OFF_V], preferred_element_type=F32)
        kp_ref[...] = (pad_heads(k) + jnp.dot(zc, pk_ref[...], preferred_element_type=F32) + ok_ref[...]).astype(BF16)
        v = pad_heads(jnp.dot(h, w_ref[:, OFF_V:OFF_BCU], preferred_element_type=F32))
        ones_lane = lax.broadcasted_iota(jnp.int32, (tm, H * HP), 1) % HP == DH
        v_ref[...] = jnp.where(ones_lane, 1.0, v).astype(BF16)
        bcu_ref[...] = jnp.dot(h, w_ref[:, OFF_BCU:OFF_F], preferred_element_type=F32).astype(BF16)

    return pl.pallas_call(
        body, name="in_proj", grid=(s // tm,),
        in_specs=[_rows(tm, D), _full((1, D)), _resident((D, WP)), _full((1, 128)),
                  _full((128, 1024)), _full((128, 1024)), _full((1, 1024)), _full((1, 1024))],
        out_specs=[pl.BlockSpec((D, tm), lambda i: (0, i)), _rows(tm, 1024), _rows(tm, 1024), _rows(tm, 1024),
                   _rows(tm, 3 * CW), _rows(tm, 128)],
        out_shape=[jax.ShapeDtypeStruct((D, s), BF16), jax.ShapeDtypeStruct((s, 1024), BF16),
                   jax.ShapeDtypeStruct((s, 1024), BF16), jax.ShapeDtypeStruct((s, 1024), BF16),
                   jax.ShapeDtypeStruct((s, 3 * CW), BF16), jax.ShapeDtypeStruct((s, 128), F32)],
        scratch_shapes=[pltpu.VMEM((SUBLANES, 128), F32)],
        compiler_params=_cparams(56, ("arbitrary",)),
    )(x, g1, wp, bfp, pq, pk, oq, ok)


def _attn_fwd(qp, kp, v, *, t):
    s = qp.shape[0]
    nq = s // t

    def body(q_ref, k_ref, v_ref, o_ref, lse_ref, mk_ref):
        qi = pl.program_id(1)
        row = lax.broadcasted_iota(jnp.int32, (t, t), 0)
        col = lax.broadcasted_iota(jnp.int32, (t, t), 1)
        lane = lax.broadcasted_iota(jnp.int32, (t, 128), 1)

        def head_step(hh, ki, carry, masked):
            m, acc = carry
            off = pl.multiple_of(ki * t, t)
            q = q_ref[:, HP * hh:HP * (hh + 1)]
            k = k_ref[pl.ds(off, t), HP * hh:HP * (hh + 1)]
            sc = lax.dot_general(q, k, NT, preferred_element_type=F32)
            if masked:
                sc = jnp.where(col <= row, sc, -1e30)
            mn = jnp.maximum(m, jnp.max(sc, axis=-1, keepdims=True))
            p = jnp.exp(sc - mn).astype(BF16)
            acc = jnp.exp(m - mn) * acc + jnp.dot(p, v_ref[pl.ds(off, t), HP * hh:HP * (hh + 1)],
                                                  preferred_element_type=F32)
            return mn, acc

        def step(ki, carry, masked):
            new = tuple(head_step(hh, ki, carry[hh], masked) for hh in range(2))
            mk_ref[ki] = jnp.where(lane < DH, jnp.broadcast_to(new[0][0], (t, 128)), jnp.broadcast_to(new[1][0], (t, 128)))
            return new

        init = (jnp.full((t, 1), -1e30, F32), jnp.zeros((t, 128), F32))
        carry = lax.fori_loop(0, qi, functools.partial(step, masked=False), (init, init))
        (m0, acc0), (m1, acc1) = step(qi, carry, True)
        l0, l1 = acc0[:, DH:DH + 1], acc1[:, DH:DH + 1]
        o_ref[...] = jnp.where(lane < DH, acc0 / l0, pltpu.roll(acc1 / l1, DH, axis=1))
        lse_ref[...] = jnp.where(lane < DH, jnp.broadcast_to(m0 + jnp.log(l0), (t, 128)),
                                 jnp.broadcast_to(m1 + jnp.log(l1), (t, 128)))

    return pl.pallas_call(
        body, name="attn_fwd", grid=(H // 2, nq),
        in_specs=[pl.BlockSpec((t, 2 * HP), lambda p, i: (i, p)),
                  pl.BlockSpec((s, 2 * HP), lambda p, i: (0, p)),
                  pl.BlockSpec((s, 2 * HP), lambda p, i: (0, p))],
        out_specs=[pl.BlockSpec((t, 128), lambda p, i: (i, p)), pl.BlockSpec((t, 128), lambda p, i: (i, p)),
                   pl.BlockSpec((nq, t, 128), lambda p, i: (0, i, p))],
        out_shape=[jax.ShapeDtypeStruct((s, AW), F32), jax.ShapeDtypeStruct((s, AW), F32),
                   jax.ShapeDtypeStruct((nq, s, AW), F32)],
        compiler_params=_cparams(48, ("arbitrary", "arbitrary")),
    )(qp, kp, v)


HALO = 16


def _conv_taps(bcu_ref, halo_ref, first, tm):
    z = bcu_ref[:, CW:2 * CW].astype(F32) * bcu_ref[:, 2 * CW:3 * CW].astype(F32)
    zh = jnp.where(first, 0.0, halo_ref[:, CW:2 * CW].astype(F32) * halo_ref[:, 2 * CW:3 * CW].astype(F32))
    row = lax.broadcasted_iota(jnp.int32, (tm, CW), 0)
    last, before = zh[HALO - 1:HALO, :], zh[HALO - 2:HALO - 1, :]
    z1 = jnp.where(row == 0, last, pltpu.roll(z, 1, axis=0))
    z2 = jnp.where(row == 0, before, jnp.where(row == 1, last, pltpu.roll(z, 2, axis=0)))
    return z, z1, z2


def _halo_before(tm, width):
    return pl.BlockSpec((HALO, width), lambda i: (jnp.maximum(i * (tm // HALO) - 1, 0), 0))


def _mix_out(o, bcu, cw8, ga, gc, gsum, w_out, x, g_post, g_ffn_pre, *, tm):
    s = x.shape[0]

    def body(o_ref, bcu_ref, halo_ref, cw_ref, ga_ref, gc_ref, gs_ref, w_ref, x_ref, g_ref, gf_ref,
             merged_ref, y_ref, x2_ref, cv_ref, h2_ref):
        z, z1, z2 = _conv_taps(bcu_ref, halo_ref, pl.program_id(0) == 0, tm)
        cv = cw_ref[0:1, :] * z2 + cw_ref[1:2, :] * z1 + cw_ref[2:3, :] * z
        cv_ref[...] = cv
        conv = bcu_ref[:, 0:CW].astype(F32) * cv
        ov = o_ref[...]
        ra = lax.rsqrt(_group_sum(ov * ov, gs_ref[...]) * (1.0 / DH) + EPS)
        rc = lax.rsqrt(_group_sum(conv * conv, gs_ref[...]) * (1.0 / DH) + EPS)
        merged = jnp.concatenate([ov * ra * ga_ref[...], conv * rc * gc_ref[...]], axis=1).astype(BF16)
        merged_ref[...] = merged
        y = jnp.dot(merged, w_ref[...], preferred_element_type=F32)
        y_ref[...] = y
        x2 = x_ref[...] + _rms_fwd(y, g_ref[...])[0]
        x2_ref[...] = x2
        h2_ref[...] = _rms_fwd(x2, gf_ref[...])[0].astype(BF16)

    return pl.pallas_call(
        body, name="mix_out", grid=(s // tm,),
        in_specs=[_rows(tm, AW), _rows(tm, 3 * CW), _halo_before(tm, 3 * CW), _full((SUBLANES, CW)),
                  _full((1, AW)), _full((1, CW)), _full((GS, GS)), _resident((D, D)), _rows(tm, D), _full((1, D)),
                  _full((1, D))],
        out_specs=[_rows(tm, D), _rows(tm, D), _rows(tm, D), _rows(tm, CW), _rows(tm, D)],
        out_shape=[jax.ShapeDtypeStruct((s, D), BF16), jax.ShapeDtypeStruct((s, D), F32),
                   jax.ShapeDtypeStruct((s, D), F32), jax.ShapeDtypeStruct((s, CW), F32),
                   jax.ShapeDtypeStruct((s, D), BF16)],
        compiler_params=_cparams(48, ("arbitrary",)),
    )(o, bcu, bcu, cw8, ga, gc, gsum, w_out, x, g_post, g_ffn_pre)


def _ffn_fwd_loss(h2, wgu, wd, x2, target, g_post, *, tm):
    s = x2.shape[0]

    def body(h_ref, w_ref, wd_ref, x2_ref, t_ref, g_ref,
             gate_ref, up_ref, a_ref, dx3_ref, dff_ref, loss_ref, dg_ref):
        @pl.when(pl.program_id(0) == 0)
        def _():
            loss_ref[...] = jnp.zeros_like(loss_ref)
            dg_ref[...] = jnp.zeros_like(dg_ref)

        h = h_ref[...]
        ff = None
        for j in range(4):
            gate = jnp.dot(h, w_ref[0, j], preferred_element_type=F32)
            up = jnp.dot(h, w_ref[1, j], preferred_element_type=F32)
            gate_ref[j] = gate.astype(BF16)
            up_ref[j] = up.astype(BF16)
            act = (gate * jax.nn.sigmoid(gate) * up).astype(BF16)
            a_ref[j] = act
            part = jnp.dot(act, wd_ref[j], preferred_element_type=F32)
            ff = part if ff is None else ff + part
        out, n, r = _rms_fwd(ff, g_ref[...])
        e = x2_ref[...] + out - t_ref[...]
        loss_ref[...] += _fold8(e * e)
        dx3 = e * (1.0 / D)
        dx3_ref[...] = dx3
        dff, dg = _rms_bwd(dx3, n, r, g_ref[...])
        dff_ref[...] = dff.astype(BF16)
        dg_ref[...] += _fold8(dg)

    blk4 = pl.BlockSpec((4, tm, FB), lambda i: (0, i, 0))
    return pl.pallas_call(
        body, name="ffn_fwd_loss", grid=(s // tm,),
        in_specs=[_rows(tm, D), _resident((2, 4, D, FB)), _resident((4, FB, D)), _rows(tm, D), _rows(tm, D), _full((1, D))],
        out_specs=[blk4, blk4, blk4, _rows(tm, D), _rows(tm, D), _full((SUBLANES, D)), _full((SUBLANES, D))],
        out_shape=[jax.ShapeDtypeStruct((4, s, FB), BF16)] * 3
        + [jax.ShapeDtypeStruct((s, D), F32), jax.ShapeDtypeStruct((s, D), BF16),
           jax.ShapeDtypeStruct((SUBLANES, D), F32), jax.ShapeDtypeStruct((SUBLANES, D), F32)],
        compiler_params=_cparams(56, ("arbitrary",)),
    )(h2, wgu, wd, x2, target, g_post)


def _ffn_bwd(dff, wd, gate, up, wgu, x2, g_pre, dx3, y, g_post, *, tm):
    s = x2.shape[0]

    def body(dff_ref, wd_ref, gate_ref, up_ref, w_ref, x2_ref, gpre_ref, dx3_ref, y_ref, gpost_ref,
             dgu_ref, dx2_ref, dy_ref, dgpre_ref, dgpost_ref):
        @pl.when(pl.program_id(0) == 0)
        def _():
            dgpre_ref[...] = jnp.zeros_like(dgpre_ref)
            dgpost_ref[...] = jnp.zeros_like(dgpost_ref)

        dff = dff_ref[...]
        dh2 = None
        for j in range(4):
            da = lax.dot_general(dff, wd_ref[j], NT, preferred_element_type=F32)
            g = gate_ref[j].astype(F32)
            sg = jax.nn.sigmoid(g)
            dgate = (da * up_ref[j].astype(F32) * (sg * (1.0 + g * (1.0 - sg)))).astype(BF16)
            dup = (da * (g * sg)).astype(BF16)
            dgu_ref[0, j] = dgate
            dgu_ref[1, j] = dup
            part = (lax.dot_general(dgate, w_ref[0, j], NT, preferred_element_type=F32)
                    + lax.dot_general(dup, w_ref[1, j], NT, preferred_element_type=F32))
            dh2 = part if dh2 is None else dh2 + part
        _, n2, r2 = _rms_fwd(x2_ref[...], gpre_ref[...])
        dxn, dg = _rms_bwd(dh2, n2, r2, gpre_ref[...])
        dgpre_ref[...] += _fold8(dg)
        dx2 = dx3_ref[...] + dxn
        dx2_ref[...] = dx2
        _, ny, ry = _rms_fwd(y_ref[...], gpost_ref[...])
        dy, dg2 = _rms_bwd(dx2, ny, ry, gpost_ref[...])
        dy_ref[...] = dy.astype(BF16)
        dgpost_ref[...] += _fold8(dg2)

    blk4 = pl.BlockSpec((4, tm, FB), lambda i: (0, i, 0))
    return pl.pallas_call(
        body, name="ffn_bwd", grid=(s // tm,),
        in_specs=[_rows(tm, D), _resident((4, FB, D)), blk4, blk4, _resident((2, 4, D, FB)), _rows(tm, D), _full((1, D)),
                  _rows(tm, D), _rows(tm, D), _full((1, D))],
        out_specs=[pl.BlockSpec((2, 4, tm, FB), lambda i: (0, 0, i, 0)), _rows(tm, D), _rows(tm, D),
                   _full((SUBLANES, D)), _full((SUBLANES, D))],
        out_shape=[jax.ShapeDtypeStruct((2, 4, s, FB), BF16), jax.ShapeDtypeStruct((s, D), F32),
                   jax.ShapeDtypeStruct((s, D), BF16), jax.ShapeDtypeStruct((SUBLANES, D), F32),
                   jax.ShapeDtypeStruct((SUBLANES, D), F32)],
        compiler_params=_cparams(56, ("arbitrary",)),
    )(dff, wd, gate, up, wgu, x2, g_pre, dx3, y, g_post)


def _grad_matmul(a, b, *, ta, tb, ts, name):
    s, ka = a.shape
    nb = b.shape[1]
    ts = min(ts, s)
    nk = s // ts

    def body(a_ref, b_ref, o_ref, acc):
        k = pl.program_id(2)

        @pl.when(k == 0)
        def _():
            acc[...] = jnp.zeros_like(acc)

        acc[...] += lax.dot_general(a_ref[...], b_ref[...], TN, preferred_element_type=F32)

        @pl.when(k == nk - 1)
        def _():
            o_ref[...] = acc[...].astype(BF16)

    return pl.pallas_call(
        body, name=name, grid=(ka // ta, nb // tb, nk),
        in_specs=[pl.BlockSpec((ts, ta), lambda i, j, k: (k, i)), pl.BlockSpec((ts, tb), lambda i, j, k: (k, j))],
        out_specs=pl.BlockSpec((ta, tb), lambda i, j, k: (i, j)),
        out_shape=jax.ShapeDtypeStruct((ka, nb), BF16),
        scratch_shapes=[pltpu.VMEM((ta, tb), F32)],
        compiler_params=_cparams(48, ("arbitrary", "arbitrary", "arbitrary")),
    )(a, b)


def _grad_matmul_t(at, b, *, tb, name):
    ka, s = at.shape
    blocked = b.ndim == 3
    nb = b.shape[-1]
    steps = b.shape[0] if blocked else nb // tb
    width = nb if blocked else tb

    def body(a_ref, b_ref, o_ref):
        bv = b_ref[0] if blocked else b_ref[...]
        res = jnp.dot(a_ref[...], bv, preferred_element_type=F32).astype(BF16)
        if blocked:
            o_ref[0] = res
        else:
            o_ref[...] = res

    if blocked:
        b_spec = pl.BlockSpec((1, s, nb), lambda j: (j, 0, 0))
        o_spec = pl.BlockSpec((1, ka, nb), lambda j: (j, 0, 0))
        o_shape = jax.ShapeDtypeStruct((steps, ka, nb), BF16)
    else:
        b_spec = pl.BlockSpec((s, width), lambda j: (0, j))
        o_spec = pl.BlockSpec((ka, width), lambda j: (0, j))
        o_shape = jax.ShapeDtypeStruct((ka, nb), BF16)
    return pl.pallas_call(
        body, name=name, grid=(steps,),
        in_specs=[_resident((ka, s)), b_spec], out_specs=o_spec, out_shape=o_shape,
        compiler_params=_cparams(56, ("arbitrary",)),
    )(at, b)


GW_TILE = 256


def _grad_w_in(h1t, pieces):
    ka, s = h1t.shape
    widths = [p.shape[1] for p in pieces]
    assert all(w % GW_TILE == 0 for w in widths)
    first = [sum(widths[:i]) // GW_TILE for i in range(len(pieces))]
    count = [w // GW_TILE for w in widths]

    def body(a_ref, *refs):
        o_ref = refs[-1]
        j = pl.program_id(0)
        for ref, f0, n in zip(refs[:-1], first, count):
            @pl.when((j >= f0) & (j < f0 + n))
            def _(ref=ref):
                o_ref[...] = jnp.dot(a_ref[...], ref[...], preferred_element_type=F32).astype(BF16)

    def spec(f0, n):
        return pl.BlockSpec((s, GW_TILE), lambda j: (0, jnp.clip(j - f0, 0, n - 1)))

    return pl.pallas_call(
        body, name="grad_w_in", grid=(sum(count),),
        in_specs=[_resident((ka, s))] + [spec(f0, n) for f0, n in zip(first, count)],
        out_specs=pl.BlockSpec((ka, GW_TILE), lambda j: (0, j)),
        out_shape=jax.ShapeDtypeStruct((ka, sum(widths)), BF16),
        compiler_params=_cparams(56, ("arbitrary",)),
    )(h1t, *pieces)


def _grad_matmul_blocks(a, b, *, ts, name):
    nblk = a.shape[0] if a.ndim == 3 else b.shape[0]
    s = a.shape[-2]
    ka, nb = a.shape[-1], b.shape[-1]
    ts = min(ts, s)
    nk = s // ts

    def body(a_ref, b_ref, o_ref, acc):
        k = pl.program_id(1)

        @pl.when(k == 0)
        def _():
            acc[...] = jnp.zeros_like(acc)

        av = a_ref[0] if a.ndim == 3 else a_ref[...]
        bv = b_ref[0] if b.ndim == 3 else b_ref[...]
        acc[...] += lax.dot_general(av, bv, TN, preferred_element_type=F32)

        @pl.when(k == nk - 1)
        def _():
            o_ref[0] = acc[...].astype(BF16)

    def spec(arr, width):
        if arr.ndim == 3:
            return pl.BlockSpec((1, ts, width), lambda j, k: (j, k, 0))
        return pl.BlockSpec((ts, width), lambda j, k: (k, 0))

    return pl.pallas_call(
        body, name=name, grid=(nblk, nk),
        in_specs=[spec(a, ka), spec(b, nb)],
        out_specs=pl.BlockSpec((1, ka, nb), lambda j, k: (j, 0, 0)),
        out_shape=jax.ShapeDtypeStruct((nblk, ka, nb), BF16),
        scratch_shapes=[pltpu.VMEM((ka, nb), F32)],
        compiler_params=_cparams(48, ("arbitrary", "arbitrary")),
    )(a, b)


def _mix_bwd(dy, w_out, o, cv, bcu, ga, gc, gsum, *, tm):
    s = dy.shape[0]

    def group_norm_bwd(dn_out, v, g, gs):
        r = lax.rsqrt(_group_sum(v * v, gs) * (1.0 / DH) + EPS)
        n = v * r
        dn = dn_out * g
        return r * (dn - n * (_group_sum(dn * n, gs) * (1.0 / DH))), dn_out * n

    def body(dy_ref, w_ref, o_ref, cv_ref, bcu_ref, ga_ref, gc_ref, gs_ref,
             do_ref, dl_ref, dcv_ref, db_ref, dga_ref, dgc_ref):
        @pl.when(pl.program_id(0) == 0)
        def _():
            dga_ref[...] = jnp.zeros_like(dga_ref)
            dgc_ref[...] = jnp.zeros_like(dgc_ref)

        dm = lax.dot_general(dy_ref[...], w_ref[...], NT, preferred_element_type=F32)
        ov = o_ref[...]
        do, dga = group_norm_bwd(dm[:, 0:AW], ov, ga_ref[...], gs_ref[...])
        dob = do.astype(BF16)
        do_ref[...] = dob
        dl_ref[...] = _group_sum(dob.astype(F32) * ov, gs_ref[...])
        dga_ref[...] += _fold8(dga)
        gate_b = bcu_ref[:, 0:CW].astype(F32)
        cv = cv_ref[...]
        dconv, dgc = group_norm_bwd(dm[:, AW:D], gate_b * cv, gc_ref[...], gs_ref[...])
        dgc_ref[...] += _fold8(dgc)
        dcv_ref[...] = dconv * gate_b
        db_ref[...] = (dconv * cv).astype(BF16)

    return pl.pallas_call(
        body, name="mix_bwd", grid=(s // tm,),
        in_specs=[_rows(tm, D), _resident((D, D)), _rows(tm, AW), _rows(tm, CW), _rows(tm, 3 * CW),
                  _full((1, AW)), _full((1, CW)), _full((GS, GS))],
        out_specs=[_rows(tm, AW), _rows(tm, AW), _rows(tm, CW), _rows(tm, CW),
                   _full((SUBLANES, AW)), _full((SUBLANES, CW))],
        out_shape=[jax.ShapeDtypeStruct((s, AW), BF16), jax.ShapeDtypeStruct((s, AW), F32),
                   jax.ShapeDtypeStruct((s, CW), F32), jax.ShapeDtypeStruct((s, CW), BF16),
                   jax.ShapeDtypeStruct((SUBLANES, AW), F32), jax.ShapeDtypeStruct((SUBLANES, CW), F32)],
        compiler_params=_cparams(48, ("arbitrary",)),
    )(dy, w_out, o, cv, bcu, ga, gc, gsum)


def _conv_bwd(dcv, db, bcu, cw8, *, tm):
    s = dcv.shape[0]
    nt = s // tm

    def body(dcv_ref, nxt_ref, db_ref, bcu_ref, halo_ref, cw_ref, dbcu_ref, dw_ref):
        i = pl.program_id(0)

        @pl.when(i == 0)
        def _():
            dw_ref[...] = jnp.zeros_like(dw_ref)

        z, z1, z2 = _conv_taps(bcu_ref, halo_ref, i == 0, tm)
        d = dcv_ref[...]
        dw_ref[0] += _fold8(d * z2)
        dw_ref[1] += _fold8(d * z1)
        dw_ref[2] += _fold8(d * z)
        nx = jnp.where(i == nt - 1, 0.0, nxt_ref[...])
        row = lax.broadcasted_iota(jnp.int32, (tm, CW), 0)
        d1 = jnp.where(row == tm - 1, nx[0:1, :], pltpu.roll(d, tm - 1, axis=0))
        d2 = jnp.where(row == tm - 2, nx[0:1, :], jnp.where(row == tm - 1, nx[1:2, :], pltpu.roll(d, tm - 2, axis=0)))
        dz = cw_ref[2:3, :] * d + cw_ref[1:2, :] * d1 + cw_ref[0:1, :] * d2
        dbcu_ref[:, 0:CW] = db_ref[...]
        dbcu_ref[:, CW:2 * CW] = (dz * bcu_ref[:, 2 * CW:3 * CW].astype(F32)).astype(BF16)
        dbcu_ref[:, 2 * CW:3 * CW] = (dz * bcu_ref[:, CW:2 * CW].astype(F32)).astype(BF16)

    return pl.pallas_call(
        body, name="conv_bwd", grid=(nt,),
        in_specs=[_rows(tm, CW),
                  pl.BlockSpec((SUBLANES, CW), lambda i: (jnp.minimum((i + 1) * (tm // SUBLANES), s // SUBLANES - 1), 0)),
                  _rows(tm, CW), _rows(tm, 3 * CW), _halo_before(tm, 3 * CW), _full((SUBLANES, CW))],
        out_specs=[_rows(tm, 3 * CW), _full((3, SUBLANES, CW))],
        out_shape=[jax.ShapeDtypeStruct((s, 3 * CW), BF16), jax.ShapeDtypeStruct((3, SUBLANES, CW), F32)],
        compiler_params=_cparams(48, ("arbitrary",)),
    )(dcv, dcv, db, bcu, bcu, cw8)


def _attn_bwd(qp, kp, v, do, lse, dl, mk, *, t):
    s = qp.shape[0]
    nq = s // t

    def body(q_ref, k_ref, v_ref, do_ref, lse_ref, dl_ref, mk_ref, dq_ref, dk_ref, dv_ref, dkx_ref, dq_acc):
        ki = pl.program_id(1)

        @pl.when(ki == 0)
        def _():
            dq_acc[...] = jnp.zeros_like(dq_acc)

        row = lax.broadcasted_iota(jnp.int32, (t, t), 0)
        col = lax.broadcasted_iota(jnp.int32, (t, t), 1)
        lane = lax.broadcasted_iota(jnp.int32, (t, 128), 1)

        def head_step(hh, qi, carry, masked):
            dk, dv, cs = carry
            off = pl.multiple_of(qi * t, t)
            rows = pl.ds(off, t)
            kh = k_ref[:, HP * hh:HP * (hh + 1)]
            q = q_ref[rows, HP * hh:HP * (hh + 1)]
            m_col = mk_ref[0, rows, DH * hh:DH * hh + 1]
            scale = jnp.exp(m_col - lse_ref[rows, DH * hh:DH * hh + 1])
            do2 = do_ref[rows, :]
            dom = jnp.where(lane < DH, do2 if hh == 0 else pltpu.roll(do2, DH, axis=1), jnp.zeros((), BF16))
            sc = lax.dot_general(q, kh, NT, preferred_element_type=F32) - m_col
            if masked:
                sc = jnp.where(col <= row, sc, -1e30)
            pt = jnp.exp(sc).astype(BF16)
            dp = lax.dot_general(dom, v_ref[:, HP * hh:HP * (hh + 1)], NT, preferred_element_type=F32)
            ds32 = (pt.astype(F32) * scale) * (dp - dl_ref[rows, DH * hh:DH * hh + 1])
            ds = ds32.astype(BF16)
            cs = cs + _fold8(ds32)
            dv = dv + jnp.dot((dom.astype(F32) * scale).astype(BF16).T, pt, preferred_element_type=F32)
            dk = dk + jnp.dot(q.T, ds, preferred_element_type=F32)
            dq_acc[rows, HP * hh:HP * (hh + 1)] += jnp.dot(ds, kh, preferred_element_type=F32)
            return dk, dv, cs

        def step(qi, carry, masked):
            return tuple(head_step(hh, qi, carry[hh], masked) for hh in range(2))

        zero = (jnp.zeros((HP, t), F32), jnp.zeros((128, t), F32), jnp.zeros((SUBLANES, t), F32))
        carry = step(ki, (zero, zero), True)
        (dk0, dv0, cs0), (dk1, dv1, cs1) = lax.fori_loop(ki + 1, nq, functools.partial(step, masked=False), carry)
        def two_heads(a0, a1):
            return jnp.where(lane < DH, a0, pltpu.roll(a1, DH, axis=1))

        dk_ref[...] = two_heads(dk0.T, dk1.T).astype(BF16)
        dv_ref[...] = two_heads(dv0.T, dv1.T).astype(BF16)

        def as_column(cs):
            return jnp.broadcast_to(jnp.sum(cs, axis=0, keepdims=True), (128, t)).T

        dkx_ref[...] = jnp.where(lane < DH, as_column(cs0), as_column(cs1))

        @pl.when(ki == nq - 1)
        def _():
            for c in range(s // t):
                rows = slice(c * t, (c + 1) * t)
                dq_ref[rows, :] = two_heads(dq_acc[rows, 0:HP], dq_acc[rows, HP:2 * HP]).astype(BF16)

    return pl.pallas_call(
        body, name="attn_bwd", grid=(H // 2, nq),
        in_specs=[pl.BlockSpec((s, 2 * HP), lambda p, i: (0, p)),
                  pl.BlockSpec((t, 2 * HP), lambda p, i: (i, p)),
                  pl.BlockSpec((t, 2 * HP), lambda p, i: (i, p)),
                  pl.BlockSpec((s, 128), lambda p, i: (0, p)),
                  pl.BlockSpec((s, 128), lambda p, i: (0, p)),
                  pl.BlockSpec((s, 128), lambda p, i: (0, p)),
                  pl.BlockSpec((1, s, 128), lambda p, i: (i, 0, p))],
        out_specs=[pl.BlockSpec((s, 128), lambda p, i: (0, p)),
                   pl.BlockSpec((t, 128), lambda p, i: (i, p)),
                   pl.BlockSpec((t, 128), lambda p, i: (i, p)),
                   pl.BlockSpec((t, 128), lambda p, i: (i, p))],
        out_shape=[jax.ShapeDtypeStruct((s, AW), BF16), jax.ShapeDtypeStruct((s, AW), BF16),
                   jax.ShapeDtypeStruct((s, AW), BF16), jax.ShapeDtypeStruct((s, AW), F32)],
        scratch_shapes=[pltpu.VMEM((s, 2 * HP), F32)],
        compiler_params=_cparams(56, ("arbitrary", "arbitrary")),
    )(qp, kp, v, do, lse, dl, mk)


def _forget_bwd(dkx, z, sel, *, tm):
    s = dkx.shape[0]
    nt = s // tm

    def body(dk_ref, z_ref, sel_ref, dfl_ref, dbf_ref, carry):
        @pl.when(pl.program_id(0) == 0)
        def _():
            carry[...] = jnp.zeros_like(carry)
            dbf_ref[...] = jnp.zeros_like(dbf_ref)

        dc = _split_dot(dk_ref[...], sel_ref[...])
        row = lax.broadcasted_iota(jnp.int32, (tm, tm), 0)
        col = lax.broadcasted_iota(jnp.int32, (tm, tm), 1)
        tri = (col >= row).astype(BF16)
        dlogf = _exact_dot01(tri, dc) + carry[0:1, :]
        carry[...] = jnp.broadcast_to(dlogf[0:1, :], carry.shape)
        dz = dlogf * (1.0 - jax.nn.sigmoid(z_ref[...]))
        dfl_ref[:, 0:128] = dz.astype(BF16)
        dfl_ref[:, 128:GW_TILE] = jnp.zeros((tm, GW_TILE - 128), BF16)
        dbf_ref[...] += _fold8(dz)

    rev = lambda i: (nt - 1 - i, 0)
    return pl.pallas_call(
        body, name="forget_bwd", grid=(nt,),
        in_specs=[pl.BlockSpec((tm, AW), rev), pl.BlockSpec((tm, 128), rev), _full((AW, 128))],
        out_specs=[pl.BlockSpec((tm, GW_TILE), rev), _full((SUBLANES, 128))],
        out_shape=[jax.ShapeDtypeStruct((s, GW_TILE), BF16), jax.ShapeDtypeStruct((SUBLANES, 128), F32)],
        scratch_shapes=[pltpu.VMEM((SUBLANES, 128), F32)],
        compiler_params=_cparams(48, ("arbitrary",)),
    )(dkx, z, sel)


def _in_proj_bwd(pieces, wp, x, g1, dx2, *, tm):
    s = x.shape[0]

    def body(q_ref, k_ref, v_ref, bcu_ref, f_ref, w_ref, x_ref, g_ref, dx2_ref, dx_ref, dg_ref):
        @pl.when(pl.program_id(0) == 0)
        def _():
            dg_ref[...] = jnp.zeros_like(dg_ref)

        dh = None
        for ref, (lo, hi) in zip((q_ref, k_ref, v_ref, bcu_ref, f_ref), PIECES):
            part = lax.dot_general(ref[...], w_ref[:, lo:hi], NT, preferred_element_type=F32)
            dh = part if dh is None else dh + part
        _, n, r = _rms_fwd(x_ref[...], g_ref[...])
        dxn, dg = _rms_bwd(dh, n, r, g_ref[...])
        dx_ref[...] = dx2_ref[...] + dxn
        dg_ref[...] += _fold8(dg)

    return pl.pallas_call(
        body, name="in_proj_bwd", grid=(s // tm,),
        in_specs=[_rows(tm, hi - lo) for lo, hi in PIECES] + [_resident((D, WP)), _rows(tm, D), _full((1, D)), _rows(tm, D)],
        out_specs=[_rows(tm, D), _full((SUBLANES, D))],
        out_shape=[jax.ShapeDtypeStruct((s, D), F32), jax.ShapeDtypeStruct((SUBLANES, D), F32)],
        compiler_params=_cparams(56, ("arbitrary",)),
    )(*pieces, wp, x, g1, dx2)


def _position():
    return lax.axis_index("x"), lax.axis_index("y"), lax.axis_index("c")


ANY = pl.BlockSpec(memory_space=pl.ANY)


def _all_gather(shards):
    n = len(shards)

    def body(*refs):
        x_refs, out_refs = refs[:n], refs[n:2 * n]
        send_sems, recv_sems, local_sems = refs[2 * n:]
        x, y, c = _position()
        me, sibling = (x, y, c), (x, y, 1 - c)
        chips = [(1 - x, y), (x, 1 - y), (1 - x, 1 - y)]

        def copy(a, k, block, to, own=False):
            slot = out_refs[a].at[4 * block[0] + 2 * block[1] + block[2]]
            return pltpu.make_async_remote_copy(
                src_ref=x_refs[a] if own else slot, dst_ref=slot,
                send_sem=send_sems.at[7 * a + k], recv_sem=recv_sems.at[7 * a + k], device_id=to, device_id_type=MESH_ID)

        mine = [pltpu.make_async_copy(x_refs[a], out_refs[a].at[4 * x + 2 * y + c], local_sems.at[a]) for a in range(n)]
        for cp in mine:
            cp.start()
        first = []
        for a in range(n):
            first.append(copy(a, 0, me, sibling, own=True))
            first += [copy(a, 1 + j, me, (*chip, c), own=True) for j, chip in enumerate(chips)]
        for cp in first:
            cp.start()
        passed = []
        for j, chip in enumerate(chips):
            for a in range(n):
                copy(a, 1 + j, (*chip, c), me).wait_recv()
                fwd = copy(a, 4 + j, (*chip, c), sibling)
                fwd.start()
                passed.append(fwd)
        for a in range(n):
            copy(a, 0, sibling, me).wait_recv()
            for j, chip in enumerate(chips):
                copy(a, 4 + j, (*chip, 1 - c), me).wait_recv()
        for cp in first + passed:
            cp.wait_send()
        for cp in mine:
            cp.wait()

    return pl.pallas_call(
        body, name="all_gather_weights",
        out_shape=[jax.ShapeDtypeStruct((NDEV,) + sh.shape, sh.dtype) for sh in shards],
        in_specs=[ANY] * n, out_specs=[ANY] * n,
        scratch_shapes=[pltpu.SemaphoreType.DMA((7 * n,)), pltpu.SemaphoreType.DMA((7 * n,)), pltpu.SemaphoreType.DMA((n,))],
    )(*shards)


def _pair_exchange(grads):
    n = len(grads)

    def body(*refs):
        g_refs, out_refs = refs[:n], refs[n:2 * n]
        send_sems, recv_sems = refs[2 * n:]
        x, y, c = _position()
        copies = [pltpu.make_async_remote_copy(
            src_ref=g_refs[a].at[:, pl.ds(1 - c, 1)], dst_ref=out_refs[a], send_sem=send_sems.at[a],
            recv_sem=recv_sems.at[a], device_id=(x, y, 1 - c), device_id_type=MESH_ID) for a in range(n)]
        for cp in copies:
            cp.start()
        for cp in copies:
            cp.wait()

    return pl.pallas_call(
        body, name="grad_pair_exchange",
        out_shape=[jax.ShapeDtypeStruct((4, 1) + g.shape[2:], g.dtype) for g in grads],
        in_specs=[ANY] * n, out_specs=[ANY] * n,
        scratch_shapes=[pltpu.SemaphoreType.DMA((n,)), pltpu.SemaphoreType.DMA((n,))],
    )(*grads)


def _pair_sum(g, got, idx, *, tr, name):
    r, c = g.shape[2:]

    def body(idx_ref, g_ref, got_ref, pb_ref, own_ref):
        p = g_ref[0, 0].astype(F32) + got_ref[0, 0].astype(F32)
        pb_ref[0] = p.astype(BF16)

        @pl.when(pl.program_id(1) == idx_ref[1])
        def _():
            own_ref[...] = p

    return pl.pallas_call(
        body, name=name,
        grid_spec=pltpu.PrefetchScalarGridSpec(
            num_scalar_prefetch=1, grid=(r // tr, 4),
            in_specs=[pl.BlockSpec((1, 1, tr, c), lambda i, j, idx: (j, idx[0], i, 0)),
                      pl.BlockSpec((1, 1, tr, c), lambda i, j, idx: (j, 0, i, 0))],
            out_specs=[pl.BlockSpec((1, tr, c), lambda i, j, idx: (j, i, 0)),
                       pl.BlockSpec((tr, c), lambda i, j, idx: (i, 0))]),
        out_shape=[jax.ShapeDtypeStruct((4, r, c), BF16), jax.ShapeDtypeStruct((r, c), F32)],
        compiler_params=_cparams(32, ("arbitrary", "arbitrary")),
    )(idx, g, got)


HBM = pl.BlockSpec(memory_space=pltpu.HBM)
SEM = pl.BlockSpec(memory_space=pltpu.SEMAPHORE)
DATAFLOW = pltpu.SideEffectType.DATAFLOW_SIDE_EFFECTING


PEERS = {"gather": NDEV - 1, "scatter": NDEV - 1, "chips": 3}


def _exchange_copies(src_refs, land_refs, send_sems, recv_sems, mode):
    x, y, c = _position()
    me, my_chip = 4 * x + 2 * y + c, 2 * x + y
    npeers = PEERS[mode]
    copies = []
    for a, (s_ref, l_ref) in enumerate(zip(src_refs, land_refs)):
        for k in range(npeers):
            if mode == "chips":
                px, py, pc = x ^ ((k + 1) >> 1), y ^ ((k + 1) & 1), c
                src, dst = s_ref.at[2 * px + py], l_ref.at[my_chip]
            else:
                px, py, pc = x ^ ((k + 1) >> 2), y ^ (((k + 1) >> 1) & 1), c ^ ((k + 1) & 1)
                src, dst = (s_ref.at[4 * px + 2 * py + pc] if mode == "scatter" else s_ref), l_ref.at[me]
            copies.append(pltpu.make_async_remote_copy(
                src_ref=src, dst_ref=dst, send_sem=send_sems.at[npeers * a + k], recv_sem=recv_sems.at[npeers * a + k],
                device_id=(px, py, pc), device_id_type=MESH_ID))
    return copies


def _exchange_start(srcs, lands, after, *, mode, name):
    n = len(srcs)
    nsem = PEERS[mode] * n

    def body(*refs):
        token = refs[-1]
        for cp in _exchange_copies(refs[:n], refs[n:2 * n], refs[2 * n + 1], refs[2 * n + 2], mode):
            cp.start()
        token[...] = jnp.zeros_like(token)

    arrays = list(srcs) + list(lands)
    outs = pl.pallas_call(
        body, name=name,
        out_shape=(pltpu.SemaphoreType.DMA((nsem,)), pltpu.SemaphoreType.DMA((nsem,)),
                   *[pltpu.HBM(a.shape, a.dtype) for a in arrays], jax.ShapeDtypeStruct((SUBLANES, LANES), F32)),
        in_specs=[HBM] * (2 * n) + [ANY],
        out_specs=(SEM, SEM, *[HBM] * (2 * n), pl.BlockSpec(memory_space=pltpu.VMEM)),
        input_output_aliases={i: 2 + i for i in range(2 * n)},
        compiler_params=pltpu.CompilerParams(has_side_effects=DATAFLOW),
    )(*[pltpu.with_memory_space_constraint(a, pltpu.HBM) for a in arrays], after)
    return outs[0], outs[1], outs[2:2 + n], outs[2 + n:2 + 2 * n], outs[-1]


def _exchange_wait(send_sems, recv_sems, srcs, lands, after, *, mode, name):
    n = len(srcs)

    def body(*refs):
        for cp in _exchange_copies(refs[:n], refs[n:2 * n], refs[2 * n], refs[2 * n + 1], mode):
            cp.wait_send()
            cp.wait_recv()

    arrays = list(srcs) + list(lands)
    outs = pl.pallas_call(
        body, name=name,
        out_shape=tuple(pltpu.HBM(a.shape, a.dtype) for a in arrays),
        in_specs=[HBM] * (2 * n) + [SEM, SEM, ANY],
        out_specs=tuple([HBM] * (2 * n)),
        input_output_aliases={i: i for i in range(2 * n)},
        compiler_params=pltpu.CompilerParams(has_side_effects=DATAFLOW),
    )(*arrays, send_sems, recv_sems, after)
    return outs[n:]


def _own_slot(value, me):
    return lax.dynamic_update_index_in_dim(lax.empty((NDEV,) + value.shape, value.dtype), value, me, 0)


def _small_all_reduce(parts):
    def body(gmp_ref, gmo_ref, gfp_ref, gfo_ref, ga_ref, gc_ref, dw_ref, bf_ref, loss_ref,
             out_ref, buf, send_sems, recv_sems):
        x, y, c = _position()
        me = 4 * x + 2 * y + c

        def colsum(v):
            return jnp.sum(v, axis=0, keepdims=True)

        loss = jnp.sum(colsum(loss_ref[...]), axis=1, keepdims=True) * (0.5 / D)
        rows = [colsum(gmp_ref[...]), colsum(gmo_ref[...]), colsum(gfp_ref[...]), colsum(gfo_ref[...]),
                jnp.concatenate([colsum(ga_ref[...]), colsum(gc_ref[...])], axis=1),
                jnp.concatenate([colsum(dw_ref[0]), colsum(dw_ref[1])], axis=1),
                jnp.concatenate([colsum(dw_ref[2]), colsum(bf_ref[...]), jnp.broadcast_to(loss, (1, 128)),
                                 jnp.zeros((1, 256), F32)], axis=1),
                jnp.zeros((1, D), F32)]
        buf[me] = jnp.concatenate(rows, axis=0)
        copies = []
        for mm in range(1, NDEV):
            peer = (x ^ (mm >> 2), y ^ ((mm >> 1) & 1), c ^ (mm & 1))
            copies.append(pltpu.make_async_remote_copy(
                src_ref=buf.at[me], dst_ref=buf.at[me], send_sem=send_sems.at[mm - 1], recv_sem=recv_sems.at[mm - 1],
                device_id=peer, device_id_type=MESH_ID))
        for cp in copies:
            cp.start()
        for cp in copies:
            cp.wait_recv()
        for cp in copies:
            cp.wait_send()
        acc = buf[0]
        for d in range(1, NDEV):
            acc = acc + buf[d]
        out_ref[...] = acc

    vm = pl.BlockSpec(memory_space=pltpu.VMEM)
    return pl.pallas_call(
        body, name="small_all_reduce",
        out_shape=jax.ShapeDtypeStruct((SUBLANES, D), F32),
        in_specs=[vm] * len(parts), out_specs=vm,
        scratch_shapes=[pltpu.VMEM((NDEV, SUBLANES, D), F32), pltpu.SemaphoreType.DMA((7,)), pltpu.SemaphoreType.DMA((7,))],
    )(*parts)


def _adam_update(w, g, m, v):
    nm = ADAM_B1 * m + (1.0 - ADAM_B1) * g
    nv = ADAM_B2 * v + (1.0 - ADAM_B2) * (g * g)
    m_hat = nm / (1.0 - ADAM_B1 ** ADAM_STEP)
    v_hat = nv / (1.0 - ADAM_B2 ** ADAM_STEP)
    return -ADAM_LR * (m_hat / (jnp.sqrt(v_hat) + ADAM_EPS) + ADAM_WD * w), nm, nv


SMALL_SLOTS = {"g_mix_pre": (0, 0, D), "g_mix_post": (1, 0, D), "g_ffn_pre": (2, 0, D), "g_ffn_post": (3, 0, D),
               "g_attn_out": (4, 0, AW), "g_conv_out": (4, AW, CW), "b_forget": (6, CW, H)}


def _small_adamw(small, conv_grad, params):
    names = list(params)
    n = len(names)

    def body(*refs):
        small_ref, cg_ref = refs[0], refs[1]
        ins, outs = refs[2:2 + 3 * n], refs[2 + 3 * n:]
        for i, name in enumerate(names):
            w_ref, m_ref, v_ref = ins[3 * i:3 * i + 3]
            g_ref, d_ref, nm_ref, nv_ref = outs[4 * i:4 * i + 4]
            if name == "conv_w":
                g = cg_ref[...]
            else:
                r, c0, width = SMALL_SLOTS[name]
                g = small_ref[r:r + 1, c0:c0 + width]
            g_ref[...] = g
            d_ref[...], nm_ref[...], nv_ref[...] = _adam_update(w_ref[...], g, m_ref[...], v_ref[...])

    vm = pl.BlockSpec(memory_space=pltpu.VMEM)
    flat = [a for name in names for a in params[name]]
    outs = pl.pallas_call(
        body, name="adamw_small",
        in_specs=[vm] * (2 + 3 * n), out_specs=[vm] * (4 * n),
        out_shape=[jax.ShapeDtypeStruct(params[name][0].shape, F32) for name in names for _ in range(4)],
    )(small, conv_grad, *flat)
    return {name: outs[4 * i:4 * i + 4] for i, name in enumerate(names)}


def _chip_sum_adamw(got, own, idx, w, m, v, *, tr, name):
    rows, cols = w.shape
    gcols = own.shape[1]

    def body(idx_ref, got_ref, own_ref, w_ref, m_ref, v_ref, g_ref, d_ref, nm_ref, nv_ref):
        g = jnp.zeros((tr, gcols), F32)
        for j in range(4):
            g = g + jnp.where(idx_ref[1] == j, own_ref[...], got_ref[j].astype(F32))
        g = g[:, :cols]
        g_ref[...] = g
        d_ref[...], nm_ref[...], nv_ref[...] = _adam_update(w_ref[...], g, m_ref[...], v_ref[...])

    spec = pl.BlockSpec((tr, cols), lambda i, idx: (i, 0))
    gspec = pl.BlockSpec((tr, gcols), lambda i, idx: (i, 0))
    return pl.pallas_call(
        body, name=name,
        grid_spec=pltpu.PrefetchScalarGridSpec(
            num_scalar_prefetch=1, grid=(rows // tr,),
            in_specs=[pl.BlockSpec((4, tr, gcols), lambda i, idx: (0, i, 0)), gspec, spec, spec, spec],
            out_specs=[spec] * 4),
        out_shape=[jax.ShapeDtypeStruct((rows, cols), F32)] * 4,
        compiler_params=_cparams(32, ("arbitrary",)),
    )(idx, got, own, w, m, v)


def _device_sum_adamw(land, w, m, v, *, tr, name):
    rows, cols = w.shape

    def body(land_ref, w_ref, m_ref, v_ref, g_ref, d_ref, nm_ref, nv_ref):
        g = land_ref[0].astype(F32)
        for dev in range(1, NDEV):
            g = g + land_ref[dev].astype(F32)
        g_ref[...] = g
        d_ref[...], nm_ref[...], nv_ref[...] = _adam_update(w_ref[...], g, m_ref[...], v_ref[...])

    spec = pl.BlockSpec((tr, cols), lambda i: (i, 0))
    return pl.pallas_call(
        body, name=name, grid=(rows // tr,),
        in_specs=[pl.BlockSpec((NDEV, tr, cols), lambda i: (0, i, 0)), spec, spec, spec],
        out_specs=[spec] * 4,
        out_shape=[jax.ShapeDtypeStruct((rows, cols), F32)] * 4,
        compiler_params=_cparams(32, ("arbitrary",)),
    )(land, w, m, v)


def _placement_constants():
    j = jnp.arange(128)[:, None]
    lane = jnp.arange(1024)[None, :]
    head, sub = lane // HP, lane % HP
    piece, jh = j // H, j % H
    valid = (j < 3 * H) & (jh == head)
    pq = jnp.where(valid & (sub == DH + piece), 1.0, 0.0).astype(BF16)
    pk = jnp.where(valid & (sub == DH + 3 + piece), -1.0, 0.0).astype(BF16)
    oq = jnp.where((sub >= DH + 3) & (sub < DH + 6), 1.0, 0.0).astype(F32)
    ok = jnp.where((sub >= DH) & (sub < DH + 3), 1.0, 0.0).astype(F32)
    r = jnp.arange(AW)[:, None]
    cc = jnp.arange(128)[None, :]
    sel = jnp.where((r % DH == 3) & (r // DH == cc), -1.0, 0.0).astype(BF16)
    gi = jnp.arange(GS)
    gsum = (gi[:, None] // DH == gi[None, :] // DH).astype(BF16)
    return pq, pk, oq, ok, sel, gsum


def _local_step(xs, tgt, wp, late_weights, cw8, bfp, g_attn_out, g_conv_out,
                g_mix_pre, g_mix_post, g_ffn_pre, g_ffn_post, early_grads=None, last_grad=None):
    pq, pk, oq, ok, sel, gsum = _placement_constants()
    h1t, qp, kp, vv, bcu, zf = _in_proj(xs, g_mix_pre, wp, bfp, pq, pk, oq, ok, tm=512)
    o, lse, mk = _attn_fwd(qp, kp, vv, t=512)
    w_out_f, wgu, wd = late_weights(lse)
    merged, y, x2, cv, h2 = _mix_out(o, bcu, cw8, g_attn_out, g_conv_out, gsum, w_out_f, xs, g_mix_post, g_ffn_pre, tm=512)
    gate, up, act, dx3, dff, loss_p, dg_ffn_post = _ffn_fwd_loss(h2, wgu, wd, x2, tgt, g_ffn_post, tm=512)

    dgu, dx2, dy, dg_ffn_pre, dg_mix_post = _ffn_bwd(dff, wd, gate, up, wgu, x2, g_ffn_pre, dx3, y, g_mix_post, tm=256)
    dw_down = _grad_matmul_blocks(act, dff, ts=4096, name="grad_w_down")
    dw_gu = _grad_matmul_blocks(dgu.reshape(NDEV, -1, FB), h2, ts=4096, name="grad_w_gate_up")
    dw_out = _grad_matmul(merged, dy, ta=1024, tb=1024, ts=2048, name="grad_w_out")
    token = early_grads(dw_out, dw_gu, dw_down) if early_grads is not None else None
    ga = g_attn_out if token is None else g_attn_out + token[0:1, 0:1]
    do, dl, dcv, db, dg_attn, dg_conv = _mix_bwd(dy, w_out_f, o, cv, bcu, ga, g_conv_out, gsum, tm=512)
    dbcu, dtaps = _conv_bwd(dcv, db, bcu, cw8, tm=512)
    dqp, dkp, dv, dkx = _attn_bwd(qp, kp, vv, do, lse, dl, mk, t=512)
    dfl, dbf = _forget_bwd(dkx, zf, sel, tm=512)
    pieces = (dqp, dkp, dv, dbcu, dfl)
    dwp = _grad_w_in(h1t, pieces)
    token = last_grad(dwp) if last_grad is not None else None
    g1 = g_mix_pre if token is None else g_mix_pre + token[0:1, 0:1]
    grad_x, dg_mix_pre = _in_proj_bwd(pieces, wp, xs, g1, dx2, tm=512)
    return (grad_x, dwp, dw_out, dw_gu, dw_down, dg_mix_pre, dg_mix_post, dg_ffn_pre, dg_ffn_post, dg_attn, dg_conv,
            dtaps, dbf, loss_p)


BIG_TILES = {"w_in": 256, "w_out": 128, "w_gate_up": 176, "w_down": 176}


def kernel(x, w_in, b_forget, conv_w, g_attn_out, g_conv_out, w_out, g_mix_pre, g_mix_post, w_gate_up, w_down, g_ffn_pre, g_ffn_post, loss_target, m_w_in, m_b_forget, m_conv_w, m_g_attn_out, m_g_conv_out, m_w_out, m_g_mix_pre, m_g_mix_post, m_w_gate_up, m_w_down, m_g_ffn_pre, m_g_ffn_post, v_w_in, v_b_forget, v_conv_w, v_g_attn_out, v_g_conv_out, v_w_out, v_g_mix_pre, v_g_mix_post, v_w_gate_up, v_w_down, v_g_ffn_pre, v_g_ffn_post):
    xc, yc, cc = _position()
    my_chip = 2 * xc + yc
    me = 2 * my_chip + cc
    idx = jnp.stack([cc, my_chip]).astype(jnp.int32)
    tables = _in_layout_tables()
    pad_in = lambda a: jnp.pad(a, ((0, 0), (0, IN_PAD - IN_COLS)))

    g_in, g_taps = _all_gather([pad_in(w_in[0]).astype(BF16), conv_w[0]])
    wp = _assemble_w_in(g_in, tables, tr=256)
    cw8 = jnp.pad(g_taps.transpose(1, 0, 2).reshape(3, CW), ((0, SUBLANES - 3), (0, 0)))

    late = [w_out[0].astype(BF16), w_gate_up[0].astype(BF16), w_down[0].astype(BF16)]
    ssem, rsem, late_thru, land_thru, token = _exchange_start(
        late, [_own_slot(s, me) for s in late], g_in, mode="gather", name="gather_late_start")
    bfp = jnp.pad(b_forget, ((0, 0), (0, 128 - H))) + token[0:1, :]

    def late_weights(after):
        l_out, l_gu, l_down = _exchange_wait(ssem, rsem, late_thru, land_thru, after, mode="gather", name="gather_late_wait")
        return l_out.reshape(D, D), l_gu.reshape(2, 4, D, FB), l_down.reshape(4, FB, D)

    early = {}

    def early_grads(dw_out, dw_gu, dw_down):
        srcs = [dw_out.reshape(NDEV, D // NDEV, D), dw_gu, dw_down.reshape(NDEV, DFF // NDEV, D)]
        lands = [_own_slot(lax.dynamic_index_in_dim(s, me, 0, keepdims=False), me) for s in srcs]
        early["handles"] = _exchange_start(srcs, lands, dw_out, mode="scatter", name="scatter_early_start")
        return early["handles"][4]

    last = {}

    def last_grad(dwp):
        g_w_in = _disassemble_w_in(dwp, tables, tr=256).reshape(4, 2, D, IN_PAD)
        (from_sibling,) = _pair_exchange([g_w_in])
        pair_b, last["own"] = _pair_sum(g_w_in, from_sibling, idx, tr=BIG_TILES["w_in"], name="grad_pair_sum_w_in")
        land = lax.dynamic_update_index_in_dim(lax.empty(pair_b.shape, pair_b.dtype),
                                               lax.dynamic_index_in_dim(pair_b, my_chip, 0, keepdims=False), my_chip, 0)
        last["handles"] = _exchange_start([pair_b], [land], last["own"], mode="chips", name="chips_w_in_start")
        return last["handles"][4]

    (grad_x, dwp, dw_out, dw_gu, dw_down, dg_mix_pre, dg_mix_post, dg_ffn_pre, dg_ffn_post, dg_attn, dg_conv,
     dtaps, dbf, loss_p) = _local_step(x[0], loss_target[0], wp, late_weights, cw8, bfp, g_attn_out, g_conv_out,
                                        g_mix_pre, g_mix_post, g_ffn_pre, g_ffn_post, early_grads, last_grad)

    e_ssem, e_rsem, e_srcs, e_lands, _ = early["handles"]
    land_out, land_gu, land_down = _exchange_wait(e_ssem, e_rsem, e_srcs, e_lands, dg_mix_pre, mode="scatter",
                                                  name="scatter_early_wait")
    res = {}
    big = {"w_out": (land_out, w_out[0], m_w_out[0], v_w_out[0]),
           "w_gate_up": (land_gu, w_gate_up[0].T, m_w_gate_up[0].T, v_w_gate_up[0].T),
           "w_down": (land_down, w_down[0], m_w_down[0], v_w_down[0])}
    for name, (land, w, m, v) in big.items():
        outs = _device_sum_adamw(land, w, m, v, tr=BIG_TILES[name], name="adamw_" + name)
        res[name] = [(o.T if name == "w_gate_up" else o)[None] for o in outs]
    c_ssem, c_rsem, c_srcs, c_lands, _ = last["handles"]
    after = sum(res[n][1][0, :SUBLANES, :LANES] for n in big)
    (from_chips,) = _exchange_wait(c_ssem, c_rsem, c_srcs, c_lands, after, mode="chips", name="chips_w_in_wait")
    outs = _chip_sum_adamw(from_chips, last["own"], idx, w_in[0], m_w_in[0], v_w_in[0],
                           tr=BIG_TILES["w_in"], name="adamw_w_in")
    res["w_in"] = [o[None] for o in outs]

    small = _small_all_reduce([dg_mix_pre, dg_mix_post, dg_ffn_pre, dg_ffn_post, dg_attn, dg_conv, dtaps, dbf, loss_p])
    taps_full = jnp.concatenate([small[5:6, :CW], small[5:6, CW:], small[6:7, :CW]], axis=0)
    loss = small[6, CW + 128]
    smalls = {"b_forget": (b_forget, m_b_forget, v_b_forget), "conv_w": (conv_w[0], m_conv_w[0], v_conv_w[0]),
              "g_attn_out": (g_attn_out, m_g_attn_out, v_g_attn_out), "g_conv_out": (g_conv_out, m_g_conv_out, v_g_conv_out),
              "g_mix_pre": (g_mix_pre, m_g_mix_pre, v_g_mix_pre), "g_mix_post": (g_mix_post, m_g_mix_post, v_g_mix_post),
              "g_ffn_pre": (g_ffn_pre, m_g_ffn_pre, v_g_ffn_pre), "g_ffn_post": (g_ffn_post, m_g_ffn_post, v_g_ffn_post)}
    for name, outs in _small_adamw(small, lax.dynamic_slice(taps_full, (0, me * 64), (3, 64)), smalls).items():
        res[name] = [o[None] for o in outs] if name == "conv_w" else list(outs)

    order = ["w_in", "b_forget", "conv_w", "g_attn_out", "g_conv_out", "w_out", "g_mix_pre", "g_mix_post",
             "w_gate_up", "w_down", "g_ffn_pre", "g_ffn_post"]
    outs = [loss, grad_x[None]]
    for k in range(4):
        outs += [res[n][k] for n in order]
    return tuple(outs)
```

```python
import functools

import numpy as np

import jax
import jax.numpy as jnp
from jax import lax
from jax.experimental import pallas as pl
from jax.experimental.pallas import tpu as pltpu

F32 = jnp.float32
BF16 = jnp.bfloat16
MESH_ID = pl.DeviceIdType.MESH

D = 1024
H = 8
DH = 64
AW = 512
CW = 512
DFF = 2816
FB = DFF // 4
HP = 128
OFF_Q, OFF_K, OFF_V, OFF_BCU, OFF_F = 0, 512, 1024, 1536, 3072
WP = OFF_F + 128
PIECES = ((OFF_Q, OFF_K), (OFF_K, OFF_V), (OFF_V, OFF_BCU), (OFF_BCU, OFF_F), (OFF_F, WP))
EPS = 1e-6
NDEV = 8
LANES = 128
SUBLANES = 8
IN_COLS = 385
IN_PAD = 512
WIN = 640
ADAM_LR, ADAM_B1, ADAM_B2, ADAM_EPS, ADAM_WD, ADAM_STEP = 0.001, 0.9, 0.999, 1e-08, 0.01, 10

NT = (((1,), (1,)), ((), ()))
TN = (((0,), (0,)), ((), ()))


def _cparams(vmem_mb=None, sem=None):
    kw = {}
    if vmem_mb is not None:
        kw["vmem_limit_bytes"] = vmem_mb << 20
    if sem is not None:
        kw["dimension_semantics"] = sem
    return pltpu.CompilerParams(**kw)


def _full(shape):
    return pl.BlockSpec(shape, lambda *_: (0,) * len(shape))


def _resident(shape):
    return pl.BlockSpec(shape, lambda *_: (0,) * len(shape), pipeline_mode=pl.Buffered(1))


def _rows(tm, width):
    return pl.BlockSpec((tm, width), lambda i: (i, 0))


def _fold8(v):
    r, w = v.shape
    return jnp.sum(v.reshape(r // SUBLANES, SUBLANES, w), axis=0)


def _split_dot(v, m01):
    hi = v.astype(BF16)
    lo = (v - hi.astype(F32)).astype(BF16)
    return (jnp.dot(hi, m01, preferred_element_type=F32)
            + jnp.dot(lo, m01, preferred_element_type=F32))


GS = 256


def _group_sum(v, g01):
    parts = [_split_dot(v[:, c:c + GS], g01) for c in range(0, v.shape[1], GS)]
    return parts[0] if len(parts) == 1 else jnp.concatenate(parts, axis=1)


def _exact_dot01(m01, v):
    p1 = v.astype(BF16)
    r1 = v - p1.astype(F32)
    p2 = r1.astype(BF16)
    p3 = (r1 - p2.astype(F32)).astype(BF16)
    return (jnp.dot(m01, p1, preferred_element_type=F32) + jnp.dot(m01, p2, preferred_element_type=F32)
            + jnp.dot(m01, p3, preferred_element_type=F32))


def _rms_fwd(v, g):
    r = lax.rsqrt(jnp.mean(v * v, axis=-1, keepdims=True) + EPS)
    n = v * r
    return n * g, n, r


def _rms_bwd(do, n, r, g):
    dn = do * g
    return r * (dn - n * jnp.mean(dn * n, axis=-1, keepdims=True)), do * n


def _padded_column(n):
    if n < AW:
        return OFF_Q + n, 0.125
    if n < 3 * AW:
        return n, 1.0
    if n < 3 * AW + H:
        return OFF_F + n - 3 * AW, 1.0
    return OFF_BCU + n - 3 * AW - H, 1.0


def _in_layout_tables():
    dest = -np.ones((IN_PAD, LANES), np.int32)
    dest_f = -np.ones((IN_PAD, LANES), np.int32)
    scale = np.zeros((IN_PAD, LANES), np.float32)
    starts = []
    for k in range(NDEV):
        cols = [_padded_column(IN_COLS * k + j) for j in range(IN_COLS)]
        main = [c for c, _ in cols if c < OFF_F]
        ws = min((min(main) // LANES) * LANES, OFF_F - WIN)
        assert ws <= min(main) and max(main) < ws + WIN
        starts.append(ws)
        for j, (c, sc) in enumerate(cols):
            scale[j, k] = sc
            if c < OFF_F:
                dest[j, k] = c - ws
            else:
                dest_f[j, k] = c - OFF_F
    f_shards = tuple(k for k in range(NDEV) if (dest_f[:, k] >= 0).any())
    return tuple(starts), f_shards, jnp.asarray(dest), jnp.asarray(dest_f), jnp.asarray(scale)


def _perm(dest_ref, scale_ref, k, width):
    lane = lax.broadcasted_iota(jnp.int32, (IN_PAD, width), 1)
    return jnp.where(dest_ref[:, k:k + 1] == lane, scale_ref[:, k:k + 1], 0.0).astype(BF16)


def _assemble_w_in(blocks, tables, *, tr):
    starts, f_shards, dest, dest_f, scale = tables

    def body(b_ref, dest_ref, destf_ref, scale_ref, o_ref):
        o_ref[...] = jnp.zeros_like(o_ref)
        for k in range(NDEV):
            b = b_ref[k]
            ws = starts[k]
            part = jnp.dot(b, _perm(dest_ref, scale_ref, k, WIN), preferred_element_type=F32)
            o_ref[:, ws:ws + WIN] += part.astype(BF16)
            if k in f_shards:
                part = jnp.dot(b, _perm(destf_ref, scale_ref, k, 128), preferred_element_type=F32)
                o_ref[:, OFF_F:WP] += part.astype(BF16)

    tab = _full((IN_PAD, LANES))
    return pl.pallas_call(
        body, name="assemble_w_in", grid=(D // tr,),
        in_specs=[pl.BlockSpec((NDEV, tr, IN_PAD), lambda i: (0, i, 0)), tab, tab, tab],
        out_specs=_rows(tr, WP),
        out_shape=jax.ShapeDtypeStruct((D, WP), BF16),
        compiler_params=_cparams(48, ("arbitrary",)),
    )(blocks, dest, dest_f, scale)


def _disassemble_w_in(dwp, tables, *, tr):
    starts, f_shards, dest, dest_f, scale = tables
    width = dwp.shape[1]

    def body(g_ref, dest_ref, destf_ref, scale_ref, o_ref):
        for k in range(NDEV):
            ws = starts[k]
            acc = lax.dot_general(g_ref[:, ws:ws + WIN], _perm(dest_ref, scale_ref, k, WIN), NT, preferred_element_type=F32)
            if k in f_shards:
                acc = acc + lax.dot_general(g_ref[:, OFF_F:WP], _perm(destf_ref, scale_ref, k, 128), NT,
                                            preferred_element_type=F32)
            o_ref[k] = acc.astype(BF16)

    tab = _full((IN_PAD, LANES))
    return pl.pallas_call(
        body, name="disassemble_w_in", grid=(D // tr,),
        in_specs=[_rows(tr, width), tab, tab, tab],
        out_specs=pl.BlockSpec((NDEV, tr, IN_PAD), lambda i: (0, i, 0)),
        out_shape=jax.ShapeDtypeStruct((NDEV, D, IN_PAD), BF16),
        compiler_params=_cparams(48, ("arbitrary",)),
    )(dwp, dest, dest_f, scale)


def _in_proj(x, g1, wp, bfp, pq, pk, oq, ok, *, tm):
    s = x.shape[0]

    def body(x_ref, g_ref, w_ref, bf_ref, pq_ref, pk_ref, oq_ref, ok_ref,
             ht_ref, qp_ref, kp_ref, v_ref, bcu_ref, z_ref, carry):
        @pl.when(pl.program_id(0) == 0)
        def _():
            carry[...] = jnp.zeros_like(carry)

        h = _rms_fwd(x_ref[...], g_ref[...])[0].astype(BF16)
        ht_ref[...] = h.T
        z = jnp.dot(h, w_ref[:, OFF_F:WP], preferred_element_type=F32) + bf_ref[...]
        z_ref[...] = z
        lane = lax.broadcasted_iota(jnp.int32, (tm, 128), 1)
        logf = jnp.where(lane < H, jnp.minimum(z, 0.0) - jnp.log(1.0 + jnp.exp(-jnp.abs(z))), 0.0)
        row = lax.broadcasted_iota(jnp.int32, (tm, tm), 0)
        col = lax.broadcasted_iota(jnp.int32, (tm, tm), 1)
        tri = (col <= row).astype(BF16)
        c = _exact_dot01(tri, logf) + carry[0:1, :]
        carry[...] = jnp.broadcast_to(c[tm - 1:tm, :], carry.shape)
        c1 = c.astype(BF16).astype(F32)
        r1 = c - c1
        c2 = r1.astype(BF16).astype(F32)
        c3 = (r1 - c2).astype(BF16).astype(F32)
        zc = (c1 + pltpu.roll(c2, 8, axis=1) + pltpu.roll(c3, 16, axis=1)).astype(BF16)

        def pad_heads(v):
            blocks = []
            for pair in range(H // 2):
                two = v[:, 128 * pair:128 * (pair + 1)]
                blocks.append(jnp.where(lane < DH, two, 0.0))
                blocks.append(jnp.where(lane < DH, pltpu.roll(two, DH, axis=1), 0.0))
            return jnp.concatenate(blocks, axis=1)

        q = jnp.dot(h, w_ref[:, OFF_Q:OFF_K], preferred_element_type=F32)
        qp_ref[...] = (pad_heads(q) + jnp.dot(zc, pq_ref[...], preferred_element_type=F32) + oq_ref[...]).astype(BF16)
        k = jnp.dot(h, w_ref[:, OFF_K:OFF_V], preferred_element_type=F32)
        kp_ref[...] = (pad_heads(k) + jnp.dot(zc, pk_ref[...], preferred_element_type=F32) + ok_ref[...]).astype(BF16)
        v = pad_heads(jnp.dot(h, w_ref[:, OFF_V:OFF_BCU], preferred_element_type=F32))
        ones_lane = lax.broadcasted_iota(jnp.int32, (tm, H * HP), 1) % HP == DH
        v_ref[...] = jnp.where(ones_lane, 1.0, v).astype(BF16)
        bcu_ref[...] = jnp.dot(h, w_ref[:, OFF_BCU:OFF_F], preferred_element_type=F32).astype(BF16)

    return pl.pallas_call(
        body, name="in_proj", grid=(s // tm,),
        in_specs=[_rows(tm, D), _full((1, D)), _resident((D, WP)), _full((1, 128)),
                  _full((128, 1024)), _full((128, 1024)), _full((1, 1024)), _full((1, 1024))],
        out_specs=[pl.BlockSpec((D, tm), lambda i: (0, i)), _rows(tm, 1024), _rows(tm, 1024), _rows(tm, 1024),
                   _rows(tm, 3 * CW), _rows(tm, 128)],
        out_shape=[jax.ShapeDtypeStruct((D, s), BF16), jax.ShapeDtypeStruct((s, 1024), BF16),
                   jax.ShapeDtypeStruct((s, 1024), BF16), jax.ShapeDtypeStruct((s, 1024), BF16),
                   jax.ShapeDtypeStruct((s, 3 * CW), BF16), jax.ShapeDtypeStruct((s, 128), F32)],
        scratch_shapes=[pltpu.VMEM((SUBLANES, 128), F32)],
        compiler_params=_cparams(56, ("arbitrary",)),
    )(x, g1, wp, bfp, pq, pk, oq, ok)


def _attn_fwd(qp, kp, v, *, t):
    s = qp.shape[0]
    nq = s // t

    def body(q_ref, k_ref, v_ref, o_ref, lse_ref, mk_ref):
        qi = pl.program_id(1)
        row = lax.broadcasted_iota(jnp.int32, (t, t), 0)
        col = lax.broadcasted_iota(jnp.int32, (t, t), 1)
        lane = lax.broadcasted_iota(jnp.int32, (t, 128), 1)

        def head_step(hh, ki, carry, masked):
            m, acc = carry
            off = pl.multiple_of(ki * t, t)
            q = q_ref[:, HP * hh:HP * (hh + 1)]
            k = k_ref[pl.ds(off, t), HP * hh:HP * (hh + 1)]
            sc = lax.dot_general(q, k, NT, preferred_element_type=F32)
            if masked:
                sc = jnp.where(col <= row, sc, -1e30)
            mn = jnp.maximum(m, jnp.max(sc, axis=-1, keepdims=True))
            p = jnp.exp(sc - mn).astype(BF16)
            acc = jnp.exp(m - mn) * acc + jnp.dot(p, v_ref[pl.ds(off, t), HP * hh:HP * (hh + 1)],
                                                  preferred_element_type=F32)
            return mn, acc

        def step(ki, carry, masked):
            new = tuple(head_step(hh, ki, carry[hh], masked) for hh in range(2))
            mk_ref[ki] = jnp.where(lane < DH, jnp.broadcast_to(new[0][0], (t, 128)), jnp.broadcast_to(new[1][0], (t, 128)))
            return new

        init = (jnp.full((t, 1), -1e30, F32), jnp.zeros((t, 128), F32))
        carry = lax.fori_loop(0, qi, functools.partial(step, masked=False), (init, init))
        (m0, acc0), (m1, acc1) = step(qi, carry, True)
        l0, l1 = acc0[:, DH:DH + 1], acc1[:, DH:DH + 1]
        o_ref[...] = jnp.where(lane < DH, acc0 / l0, pltpu.roll(acc1 / l1, DH, axis=1))
        lse_ref[...] = jnp.where(lane < DH, jnp.broadcast_to(m0 + jnp.log(l0), (t, 128)),
                                 jnp.broadcast_to(m1 + jnp.log(l1), (t, 128)))

    return pl.pallas_call(
        body, name="attn_fwd", grid=(H // 2, nq),
        in_specs=[pl.BlockSpec((t, 2 * HP), lambda p, i: (i, p)),
                  pl.BlockSpec((s, 2 * HP), lambda p, i: (0, p)),
                  pl.BlockSpec((s, 2 * HP), lambda p, i: (0, p))],
        out_specs=[pl.BlockSpec((t, 128), lambda p, i: (i, p)), pl.BlockSpec((t, 128), lambda p, i: (i, p)),
                   pl.BlockSpec((nq, t, 128), lambda p, i: (0, i, p))],
        out_shape=[jax.ShapeDtypeStruct((s, AW), F32), jax.ShapeDtypeStruct((s, AW), F32),
                   jax.ShapeDtypeStruct((nq, s, AW), F32)],
        compiler_params=_cparams(48, ("arbitrary", "arbitrary")),
    )(qp, kp, v)


HALO = 16


def _conv_taps(bcu_ref, halo_ref, first, tm):
    z = bcu_ref[:, CW:2 * CW].astype(F32) * bcu_ref[:, 2 * CW:3 * CW].astype(F32)
    zh = jnp.where(first, 0.0, halo_ref[:, CW:2 * CW].astype(F32) * halo_ref[:, 2 * CW:3 * CW].astype(F32))
    row = lax.broadcasted_iota(jnp.int32, (tm, CW), 0)
    last, before = zh[HALO - 1:HALO, :], zh[HALO - 2:HALO - 1, :]
    z1 = jnp.where(row == 0, last, pltpu.roll(z, 1, axis=0))
    z2 = jnp.where(row == 0, before, jnp.where(row == 1, last, pltpu.roll(z, 2, axis=0)))
    return z, z1, z2


def _halo_before(tm, width):
    return pl.BlockSpec((HALO, width), lambda i: (jnp.maximum(i * (tm // HALO) - 1, 0), 0))


def _mix_out(o, bcu, cw8, ga, gc, gsum, w_out, x, g_post, g_ffn_pre, *, tm):
    s = x.shape[0]

    def body(o_ref, bcu_ref, halo_ref, cw_ref, ga_ref, gc_ref, gs_ref, w_ref, x_ref, g_ref, gf_ref,
             merged_ref, y_ref, x2_ref, cv_ref, h2_ref):
        z, z1, z2 = _conv_taps(bcu_ref, halo_ref, pl.program_id(0) == 0, tm)
        cv = cw_ref[0:1, :] * z2 + cw_ref[1:2, :] * z1 + cw_ref[2:3, :] * z
        cv_ref[...] = cv
        conv = bcu_ref[:, 0:CW].astype(F32) * cv
        ov = o_ref[...]
        ra = lax.rsqrt(_group_sum(ov * ov, gs_ref[...]) * (1.0 / DH) + EPS)
        rc = lax.rsqrt(_group_sum(conv * conv, gs_ref[...]) * (1.0 / DH) + EPS)
        merged = jnp.concatenate([ov * ra * ga_ref[...], conv * rc * gc_ref[...]], axis=1).astype(BF16)
        merged_ref[...] = merged
        y = jnp.dot(merged, w_ref[...], preferred_element_type=F32)
        y_ref[...] = y
        x2 = x_ref[...] + _rms_fwd(y, g_ref[...])[0]
        x2_ref[...] = x2
        h2_ref[...] = _rms_fwd(x2, gf_ref[...])[0].astype(BF16)

    return pl.pallas_call(
        body, name="mix_out", grid=(s // tm,),
        in_specs=[_rows(tm, AW), _rows(tm, 3 * CW), _halo_before(tm, 3 * CW), _full((SUBLANES, CW)),
                  _full((1, AW)), _full((1, CW)), _full((GS, GS)), _resident((D, D)), _rows(tm, D), _full((1, D)),
                  _full((1, D))],
        out_specs=[_rows(tm, D), _rows(tm, D), _rows(tm, D), _rows(tm, CW), _rows(tm, D)],
        out_shape=[jax.ShapeDtypeStruct((s, D), BF16), jax.ShapeDtypeStruct((s, D), F32),
                   jax.ShapeDtypeStruct((s, D), F32), jax.ShapeDtypeStruct((s, CW), F32),
                   jax.ShapeDtypeStruct((s, D), BF16)],
        compiler_params=_cparams(48, ("arbitrary",)),
    )(o, bcu, bcu, cw8, ga, gc, gsum, w_out, x, g_post, g_ffn_pre)


def _ffn_fwd_loss(h2, wgu, wd, x2, target, g_post, *, tm):
    s = x2.shape[0]

    def body(h_ref, w_ref, wd_ref, x2_ref, t_ref, g_ref,
             gate_ref, up_ref, a_ref, dx3_ref, dff_ref, loss_ref, dg_ref):
        @pl.when(pl.program_id(0) == 0)
        def _():
            loss_ref[...] = jnp.zeros_like(loss_ref)
            dg_ref[...] = jnp.zeros_like(dg_ref)

        h = h_ref[...]
        ff = None
        for j in range(4):
            gate = jnp.dot(h, w_ref[0, j], preferred_element_type=F32)
            up = jnp.dot(h, w_ref[1, j], preferred_element_type=F32)
            gate_ref[j] = gate.astype(BF16)
            up_ref[j] = up.astype(BF16)
            act = (gate * jax.nn.sigmoid(gate) * up).astype(BF16)
            a_ref[j] = act
            part = jnp.dot(act, wd_ref[j], preferred_element_type=F32)
            ff = part if ff is None else ff + part
        out, n, r = _rms_fwd(ff, g_ref[...])
        e = x2_ref[...] + out - t_ref[...]
        loss_ref[...] += _fold8(e * e)
        dx3 = e * (1.0 / D)
        dx3_ref[...] = dx3
        dff, dg = _rms_bwd(dx3, n, r, g_ref[...])
        dff_ref[...] = dff.astype(BF16)
        dg_ref[...] += _fold8(dg)

    blk4 = pl.BlockSpec((4, tm, FB), lambda i: (0, i, 0))
    return pl.pallas_call(
        body, name="ffn_fwd_loss", grid=(s // tm,),
        in_specs=[_rows(tm, D), _resident((2, 4, D, FB)), _resident((4, FB, D)), _rows(tm, D), _rows(tm, D), _full((1, D))],
        out_specs=[blk4, blk4, blk4, _rows(tm, D), _rows(tm, D), _full((SUBLANES, D)), _full((SUBLANES, D))],
        out_shape=[jax.ShapeDtypeStruct((4, s, FB), BF16)] * 3
        + [jax.ShapeDtypeStruct((s, D), F32), jax.ShapeDtypeStruct((s, D), BF16),
           jax.ShapeDtypeStruct((SUBLANES, D), F32), jax.ShapeDtypeStruct((SUBLANES, D), F32)],
        compiler_params=_cparams(56, ("arbitrary",)),
    )(h2, wgu, wd, x2, target, g_post)


def _ffn_bwd(dff, wd, gate, up, wgu, x2, g_pre, dx3, y, g_post, *, tm):
    s = x2.shape[0]

    def body(dff_ref, wd_ref, gate_ref, up_ref, w_ref, x2_ref, gpre_ref, dx3_ref, y_ref, gpost_ref,
             dgu_ref, dx2_ref, dy_ref, dgpre_ref, dgpost_ref):
        @pl.when(pl.program_id(0) == 0)
        def _():
            dgpre_ref[...] = jnp.zeros_like(dgpre_ref)
            dgpost_ref[...] = jnp.zeros_like(dgpost_ref)

        dff = dff_ref[...]
        dh2 = None
        for j in range(4):
            da = lax.dot_general(dff, wd_ref[j], NT, preferred_element_type=F32)
            g = gate_ref[j].astype(F32)
            sg = jax.nn.sigmoid(g)
            dgate = (da * up_ref[j].astype(F32) * (sg * (1.0 + g * (1.0 - sg)))).astype(BF16)
            dup = (da * (g * sg)).astype(BF16)
            dgu_ref[0, j] = dgate
            dgu_ref[1, j] = dup
            part = (lax.dot_general(dgate, w_ref[0, j], NT, preferred_element_type=F32)
                    + lax.dot_general(dup, w_ref[1, j], NT, preferred_element_type=F32))
            dh2 = part if dh2 is None else dh2 + part
        _, n2, r2 = _rms_fwd(x2_ref[...], gpre_ref[...])
        dxn, dg = _rms_bwd(dh2, n2, r2, gpre_ref[...])
        dgpre_ref[...] += _fold8(dg)
        dx2 = dx3_ref[...] + dxn
        dx2_ref[...] = dx2
        _, ny, ry = _rms_fwd(y_ref[...], gpost_ref[...])
        dy, dg2 = _rms_bwd(dx2, ny, ry, gpost_ref[...])
        dy_ref[...] = dy.astype(BF16)
        dgpost_ref[...] += _fold8(dg2)

    blk4 = pl.BlockSpec((4, tm, FB), lambda i: (0, i, 0))
    return pl.pallas_call(
        body, name="ffn_bwd", grid=(s // tm,),
        in_specs=[_rows(tm, D), _resident((4, FB, D)), blk4, blk4, _resident((2, 4, D, FB)), _rows(tm, D), _full((1, D)),
                  _rows(tm, D), _rows(tm, D), _full((1, D))],
        out_specs=[pl.BlockSpec((2, 4, tm, FB), lambda i: (0, 0, i, 0)), _rows(tm, D), _rows(tm, D),
                   _full((SUBLANES, D)), _full((SUBLANES, D))],
        out_shape=[jax.ShapeDtypeStruct((2, 4, s, FB), BF16), jax.ShapeDtypeStruct((s, D), F32),
                   jax.ShapeDtypeStruct((s, D), BF16), jax.ShapeDtypeStruct((SUBLANES, D), F32),
                   jax.ShapeDtypeStruct((SUBLANES, D), F32)],
        compiler_params=_cparams(56, ("arbitrary",)),
    )(dff, wd, gate, up, wgu, x2, g_pre, dx3, y, g_post)


def _grad_matmul(a, b, *, ta, tb, ts, name):
    s, ka = a.shape
    nb = b.shape[1]
    ts = min(ts, s)
    nk = s // ts

    def body(a_ref, b_ref, o_ref, acc):
        k = pl.program_id(2)

        @pl.when(k == 0)
        def _():
            acc[...] = jnp.zeros_like(acc)

        acc[...] += lax.dot_general(a_ref[...], b_ref[...], TN, preferred_element_type=F32)

        @pl.when(k == nk - 1)
        def _():
            o_ref[...] = acc[...].astype(BF16)

    return pl.pallas_call(
        body, name=name, grid=(ka // ta, nb // tb, nk),
        in_specs=[pl.BlockSpec((ts, ta), lambda i, j, k: (k, i)), pl.BlockSpec((ts, tb), lambda i, j, k: (k, j))],
        out_specs=pl.BlockSpec((ta, tb), lambda i, j, k: (i, j)),
        out_shape=jax.ShapeDtypeStruct((ka, nb), BF16),
        scratch_shapes=[pltpu.VMEM((ta, tb), F32)],
        compiler_params=_cparams(48, ("arbitrary", "arbitrary", "arbitrary")),
    )(a, b)


def _grad_matmul_t(at, b, *, tb, name):
    ka, s = at.shape
    blocked = b.ndim == 3
    nb = b.shape[-1]
    steps = b.shape[0] if blocked else nb // tb
    width = nb if blocked else tb

    def body(a_ref, b_ref, o_ref):
        bv = b_ref[0] if blocked else b_ref[...]
        res = jnp.dot(a_ref[...], bv, preferred_element_type=F32).astype(BF16)
        if blocked:
            o_ref[0] = res
        else:
            o_ref[...] = res

    if blocked:
        b_spec = pl.BlockSpec((1, s, nb), lambda j: (j, 0, 0))
        o_spec = pl.BlockSpec((1, ka, nb), lambda j: (j, 0, 0))
        o_shape = jax.ShapeDtypeStruct((steps, ka, nb), BF16)
    else:
        b_spec = pl.BlockSpec((s, width), lambda j: (0, j))
        o_spec = pl.BlockSpec((ka, width), lambda j: (0, j))
        o_shape = jax.ShapeDtypeStruct((ka, nb), BF16)
    return pl.pallas_call(
        body, name=name, grid=(steps,),
        in_specs=[_resident((ka, s)), b_spec], out_specs=o_spec, out_shape=o_shape,
        compiler_params=_cparams(56, ("arbitrary",)),
    )(at, b)


GW_TILE = 256


def _grad_w_in(h1t, pieces):
    ka, s = h1t.shape
    widths = [p.shape[1] for p in pieces]
    assert all(w % GW_TILE == 0 for w in widths)
    first = [sum(widths[:i]) // GW_TILE for i in range(len(pieces))]
    count = [w // GW_TILE for w in widths]

    def body(a_ref, *refs):
        o_ref = refs[-1]
        j = pl.program_id(0)
        for ref, f0, n in zip(refs[:-1], first, count):
            @pl.when((j >= f0) & (j < f0 + n))
            def _(ref=ref):
                o_ref[...] = jnp.dot(a_ref[...], ref[...], preferred_element_type=F32).astype(BF16)

    def spec(f0, n):
        return pl.BlockSpec((s, GW_TILE), lambda j: (0, jnp.clip(j - f0, 0, n - 1)))

    return pl.pallas_call(
        body, name="grad_w_in", grid=(sum(count),),
        in_specs=[_resident((ka, s))] + [spec(f0, n) for f0, n in zip(first, count)],
        out_specs=pl.BlockSpec((ka, GW_TILE), lambda j: (0, j)),
        out_shape=jax.ShapeDtypeStruct((ka, sum(widths)), BF16),
        compiler_params=_cparams(56, ("arbitrary",)),
    )(h1t, *pieces)


def _grad_matmul_blocks(a, b, *, ts, name):
    nblk = a.shape[0] if a.ndim == 3 else b.shape[0]
    s = a.shape[-2]
    ka, nb = a.shape[-1], b.shape[-1]
    ts = min(ts, s)
    nk = s // ts

    def body(a_ref, b_ref, o_ref, acc):
        k = pl.program_id(1)

        @pl.when(k == 0)
        def _():
            acc[...] = jnp.zeros_like(acc)

        av = a_ref[0] if a.ndim == 3 else a_ref[...]
        bv = b_ref[0] if b.ndim == 3 else b_ref[...]
        acc[...] += lax.dot_general(av, bv, TN, preferred_element_type=F32)

        @pl.when(k == nk - 1)
        def _():
            o_ref[0] = acc[...].astype(BF16)

    def spec(arr, width):
        if arr.ndim == 3:
            return pl.BlockSpec((1, ts, width), lambda j, k: (j, k, 0))
        return pl.BlockSpec((ts, width), lambda j, k: (k, 0))

    return pl.pallas_call(
        body, name=name, grid=(nblk, nk),
        in_specs=[spec(a, ka), spec(b, nb)],
        out_specs=pl.BlockSpec((1, ka, nb), lambda j, k: (j, 0, 0)),
        out_shape=jax.ShapeDtypeStruct((nblk, ka, nb), BF16),
        scratch_shapes=[pltpu.VMEM((ka, nb), F32)],
        compiler_params=_cparams(48, ("arbitrary", "arbitrary")),
    )(a, b)


def _mix_bwd(dy, w_out, o, cv, bcu, ga, gc, gsum, *, tm):
    s = dy.shape[0]

    def group_norm_bwd(dn_out, v, g, gs):
        r = lax.rsqrt(_group_sum(v * v, gs) * (1.0 / DH) + EPS)
        n = v * r
        dn = dn_out * g
        return r * (dn - n * (_group_sum(dn * n, gs) * (1.0 / DH))), dn_out * n

    def body(dy_ref, w_ref, o_ref, cv_ref, bcu_ref, ga_ref, gc_ref, gs_ref,
             do_ref, dl_ref, dcv_ref, db_ref, dga_ref, dgc_ref):
        @pl.when(pl.program_id(0) == 0)
        def _():
            dga_ref[...] = jnp.zeros_like(dga_ref)
            dgc_ref[...] = jnp.zeros_like(dgc_ref)

        dm = lax.dot_general(dy_ref[...], w_ref[...], NT, preferred_element_type=F32)
        ov = o_ref[...]
        do, dga = group_norm_bwd(dm[:, 0:AW], ov, ga_ref[...], gs_ref[...])
        dob = do.astype(BF16)
        do_ref[...] = dob
        dl_ref[...] = _group_sum(dob.astype(F32) * ov, gs_ref[...])
        dga_ref[...] += _fold8(dga)
        gate_b = bcu_ref[:, 0:CW].astype(F32)
        cv = cv_ref[...]
        dconv, dgc = group_norm_bwd(dm[:, AW:D], gate_b * cv, gc_ref[...], gs_ref[...])
        dgc_ref[...] += _fold8(dgc)
        dcv_ref[...] = dconv * gate_b
        db_ref[...] = (dconv * cv).astype(BF16)

    return pl.pallas_call(
        body, name="mix_bwd", grid=(s // tm,),
        in_specs=[_rows(tm, D), _resident((D, D)), _rows(tm, AW), _rows(tm, CW), _rows(tm, 3 * CW),
                  _full((1, AW)), _full((1, CW)), _full((GS, GS))],
        out_specs=[_rows(tm, AW), _rows(tm, AW), _rows(tm, CW), _rows(tm, CW),
                   _full((SUBLANES, AW)), _full((SUBLANES, CW))],
        out_shape=[jax.ShapeDtypeStruct((s, AW), BF16), jax.ShapeDtypeStruct((s, AW), F32),
                   jax.ShapeDtypeStruct((s, CW), F32), jax.ShapeDtypeStruct((s, CW), BF16),
                   jax.ShapeDtypeStruct((SUBLANES, AW), F32), jax.ShapeDtypeStruct((SUBLANES, CW), F32)],
        compiler_params=_cparams(48, ("arbitrary",)),
    )(dy, w_out, o, cv, bcu, ga, gc, gsum)


def _conv_bwd(dcv, db, bcu, cw8, *, tm):
    s = dcv.shape[0]
    nt = s // tm

    def body(dcv_ref, nxt_ref, db_ref, bcu_ref, halo_ref, cw_ref, dbcu_ref, dw_ref):
        i = pl.program_id(0)

        @pl.when(i == 0)
        def _():
            dw_ref[...] = jnp.zeros_like(dw_ref)

        z, z1, z2 = _conv_taps(bcu_ref, halo_ref, i == 0, tm)
        d = dcv_ref[...]
        dw_ref[0] += _fold8(d * z2)
        dw_ref[1] += _fold8(d * z1)
        dw_ref[2] += _fold8(d * z)
        nx = jnp.where(i == nt - 1, 0.0, nxt_ref[...])
        row = lax.broadcasted_iota(jnp.int32, (tm, CW), 0)
        d1 = jnp.where(row == tm - 1, nx[0:1, :], pltpu.roll(d, tm - 1, axis=0))
        d2 = jnp.where(row == tm - 2, nx[0:1, :], jnp.where(row == tm - 1, nx[1:2, :], pltpu.roll(d, tm - 2, axis=0)))
        dz = cw_ref[2:3, :] * d + cw_ref[1:2, :] * d1 + cw_ref[0:1, :] * d2
        dbcu_ref[:, 0:CW] = db_ref[...]
        dbcu_ref[:, CW:2 * CW] = (dz * bcu_ref[:, 2 * CW:3 * CW].astype(F32)).astype(BF16)
        dbcu_ref[:, 2 * CW:3 * CW] = (dz * bcu_ref[:, CW:2 * CW].astype(F32)).astype(BF16)

    return pl.pallas_call(
        body, name="conv_bwd", grid=(nt,),
        in_specs=[_rows(tm, CW),
                  pl.BlockSpec((SUBLANES, CW), lambda i: (jnp.minimum((i + 1) * (tm // SUBLANES), s // SUBLANES - 1), 0)),
                  _rows(tm, CW), _rows(tm, 3 * CW), _halo_before(tm, 3 * CW), _full((SUBLANES, CW))],
        out_specs=[_rows(tm, 3 * CW), _full((3, SUBLANES, CW))],
        out_shape=[jax.ShapeDtypeStruct((s, 3 * CW), BF16), jax.ShapeDtypeStruct((3, SUBLANES, CW), F32)],
        compiler_params=_cparams(48, ("arbitrary",)),
    )(dcv, dcv, db, bcu, bcu, cw8)


def _attn_bwd(qp, kp, v, do, lse, dl, mk, *, t):
    s = qp.shape[0]
    nq = s // t

    def body(q_ref, k_ref, v_ref, do_ref, lse_ref, dl_ref, mk_ref, dq_ref, dk_ref, dv_ref, dkx_ref, dq_acc):
        ki = pl.program_id(1)

        @pl.when(ki == 0)
        def _():
            dq_acc[...] = jnp.zeros_like(dq_acc)

        row = lax.broadcasted_iota(jnp.int32, (t, t), 0)
        col = lax.broadcasted_iota(jnp.int32, (t, t), 1)
        lane = lax.broadcasted_iota(jnp.int32, (t, 128), 1)

        def head_step(hh, qi, carry, masked):
            dk, dv, cs = carry
            off = pl.multiple_of(qi * t, t)
            rows = pl.ds(off, t)
            kh = k_ref[:, HP * hh:HP * (hh + 1)]
            q = q_ref[rows, HP * hh:HP * (hh + 1)]
            m_col = mk_ref[0, rows, DH * hh:DH * hh + 1]
            scale = jnp.exp(m_col - lse_ref[rows, DH * hh:DH * hh + 1])
            do2 = do_ref[rows, :]
            dom = jnp.where(lane < DH, do2 if hh == 0 else pltpu.roll(do2, DH, axis=1), jnp.zeros((), BF16))
            sc = lax.dot_general(q, kh, NT, preferred_element_type=F32) - m_col
            if masked:
                sc = jnp.where(col <= row, sc, -1e30)
            pt = jnp.exp(sc).astype(BF16)
            dp = lax.dot_general(dom, v_ref[:, HP * hh:HP * (hh + 1)], NT, preferred_element_type=F32)
            ds32 = (pt.astype(F32) * scale) * (dp - dl_ref[rows, DH * hh:DH * hh + 1])
            ds = ds32.astype(BF16)
            cs = cs + _fold8(ds32)
            dv = dv + jnp.dot((dom.astype(F32) * scale).astype(BF16).T, pt, preferred_element_type=F32)
            dk = dk + jnp.dot(q.T, ds, preferred_element_type=F32)
            dq_acc[rows, HP * hh:HP * (hh + 1)] += jnp.dot(ds, kh, preferred_element_type=F32)
            return dk, dv, cs

        def step(qi, carry, masked):
            return tuple(head_step(hh, qi, carry[hh], masked) for hh in range(2))

        zero = (jnp.zeros((HP, t), F32), jnp.zeros((128, t), F32), jnp.zeros((SUBLANES, t), F32))
        carry = step(ki, (zero, zero), True)
        (dk0, dv0, cs0), (dk1, dv1, cs1) = lax.fori_loop(ki + 1, nq, functools.partial(step, masked=False), carry)
        def two_heads(a0, a1):
            return jnp.where(lane < DH, a0, pltpu.roll(a1, DH, axis=1))

        def rows_to_lanes(a0, a1):
            return jnp.concatenate([a0, a1], axis=0).T

        dk_ref[...] = rows_to_lanes(dk0[0:DH], dk1[0:DH]).astype(BF16)
        dv_ref[...] = rows_to_lanes(dv0[0:DH], dv1[0:DH]).astype(BF16)
        total = lambda cs: jnp.broadcast_to(jnp.sum(cs, axis=0, keepdims=True), (DH, t))
        dkx_ref[...] = rows_to_lanes(total(cs0), total(cs1))

        @pl.when(ki == nq - 1)
        def _():
            for c in range(s // t):
                rows = slice(c * t, (c + 1) * t)
                dq_ref[rows, :] = two_heads(dq_acc[rows, 0:HP], dq_acc[rows, HP:2 * HP]).astype(BF16)

    return pl.pallas_call(
        body, name="attn_bwd", grid=(H // 2, nq),
        in_specs=[pl.BlockSpec((s, 2 * HP), lambda p, i: (0, p)),
                  pl.BlockSpec((t, 2 * HP), lambda p, i: (i, p)),
                  pl.BlockSpec((t, 2 * HP), lambda p, i: (i, p)),
                  pl.BlockSpec((s, 128), lambda p, i: (0, p)),
                  pl.BlockSpec((s, 128), lambda p, i: (0, p)),
                  pl.BlockSpec((s, 128), lambda p, i: (0, p)),
                  pl.BlockSpec((1, s, 128), lambda p, i: (i, 0, p))],
        out_specs=[pl.BlockSpec((s, 128), lambda p, i: (0, p)),
                   pl.BlockSpec((t, 128), lambda p, i: (i, p)),
                   pl.BlockSpec((t, 128), lambda p, i: (i, p)),
                   pl.BlockSpec((t, 128), lambda p, i: (i, p))],
        out_shape=[jax.ShapeDtypeStruct((s, AW), BF16), jax.ShapeDtypeStruct((s, AW), BF16),
                   jax.ShapeDtypeStruct((s, AW), BF16), jax.ShapeDtypeStruct((s, AW), F32)],
        scratch_shapes=[pltpu.VMEM((s, 2 * HP), F32)],
        compiler_params=_cparams(56, ("arbitrary", "arbitrary")),
    )(qp, kp, v, do, lse, dl, mk)


def _forget_bwd(dkx, z, sel, *, tm):
    s = dkx.shape[0]
    nt = s // tm

    def body(dk_ref, z_ref, sel_ref, dfl_ref, dbf_ref, carry):
        @pl.when(pl.program_id(0) == 0)
        def _():
            carry[...] = jnp.zeros_like(carry)
            dbf_ref[...] = jnp.zeros_like(dbf_ref)

        dc = _split_dot(dk_ref[...], sel_ref[...])
        row = lax.broadcasted_iota(jnp.int32, (tm, tm), 0)
        col = lax.broadcasted_iota(jnp.int32, (tm, tm), 1)
        tri = (col >= row).astype(BF16)
        dlogf = _exact_dot01(tri, dc) + carry[0:1, :]
        carry[...] = jnp.broadcast_to(dlogf[0:1, :], carry.shape)
        dz = dlogf * (1.0 - jax.nn.sigmoid(z_ref[...]))
        dfl_ref[:, 0:128] = dz.astype(BF16)
        dfl_ref[:, 128:GW_TILE] = jnp.zeros((tm, GW_TILE - 128), BF16)
        dbf_ref[...] += _fold8(dz)

    rev = lambda i: (nt - 1 - i, 0)
    return pl.pallas_call(
        body, name="forget_bwd", grid=(nt,),
        in_specs=[pl.BlockSpec((tm, AW), rev), pl.BlockSpec((tm, 128), rev), _full((AW, 128))],
        out_specs=[pl.BlockSpec((tm, GW_TILE), rev), _full((SUBLANES, 128))],
        out_shape=[jax.ShapeDtypeStruct((s, GW_TILE), BF16), jax.ShapeDtypeStruct((SUBLANES, 128), F32)],
        scratch_shapes=[pltpu.VMEM((SUBLANES, 128), F32)],
        compiler_params=_cparams(48, ("arbitrary",)),
    )(dkx, z, sel)


def _in_proj_bwd(pieces, wp, x, g1, dx2, *, tm):
    s = x.shape[0]

    def body(q_ref, k_ref, v_ref, bcu_ref, f_ref, w_ref, x_ref, g_ref, dx2_ref, dx_ref, dg_ref):
        @pl.when(pl.program_id(0) == 0)
        def _():
            dg_ref[...] = jnp.zeros_like(dg_ref)

        dh = None
        for ref, (lo, hi) in zip((q_ref, k_ref, v_ref, bcu_ref, f_ref), PIECES):
            part = lax.dot_general(ref[...], w_ref[:, lo:hi], NT, preferred_element_type=F32)
            dh = part if dh is None else dh + part
        _, n, r = _rms_fwd(x_ref[...], g_ref[...])
        dxn, dg = _rms_bwd(dh, n, r, g_ref[...])
        dx_ref[...] = dx2_ref[...] + dxn
        dg_ref[...] += _fold8(dg)

    return pl.pallas_call(
        body, name="in_proj_bwd", grid=(s // tm,),
        in_specs=[_rows(tm, hi - lo) for lo, hi in PIECES] + [_resident((D, WP)), _rows(tm, D), _full((1, D)), _rows(tm, D)],
        out_specs=[_rows(tm, D), _full((SUBLANES, D))],
        out_shape=[jax.ShapeDtypeStruct((s, D), F32), jax.ShapeDtypeStruct((SUBLANES, D), F32)],
        compiler_params=_cparams(56, ("arbitrary",)),
    )(*pieces, wp, x, g1, dx2)


def _position():
    return lax.axis_index("x"), lax.axis_index("y"), lax.axis_index("c")


ANY = pl.BlockSpec(memory_space=pl.ANY)


def _all_gather(shards):
    n = len(shards)

    def body(*refs):
        x_refs, out_refs = refs[:n], refs[n:2 * n]
        send_sems, recv_sems, local_sems = refs[2 * n:]
        x, y, c = _position()
        me, sibling = (x, y, c), (x, y, 1 - c)
        chips = [(1 - x, y), (x, 1 - y), (1 - x, 1 - y)]

        def copy(a, k, block, to, own=False):
            slot = out_refs[a].at[4 * block[0] + 2 * block[1] + block[2]]
            return pltpu.make_async_remote_copy(
                src_ref=x_refs[a] if own else slot, dst_ref=slot,
                send_sem=send_sems.at[7 * a + k], recv_sem=recv_sems.at[7 * a + k], device_id=to, device_id_type=MESH_ID)

        mine = [pltpu.make_async_copy(x_refs[a], out_refs[a].at[4 * x + 2 * y + c], local_sems.at[a]) for a in range(n)]
        for cp in mine:
            cp.start()
        first = []
        for a in range(n):
            first.append(copy(a, 0, me, sibling, own=True))
            first += [copy(a, 1 + j, me, (*chip, c), own=True) for j, chip in enumerate(chips)]
        for cp in first:
            cp.start()
        passed = []
        for j, chip in enumerate(chips):
            for a in range(n):
                copy(a, 1 + j, (*chip, c), me).wait_recv()
                fwd = copy(a, 4 + j, (*chip, c), sibling)
                fwd.start()
                passed.append(fwd)
        for a in range(n):
            copy(a, 0, sibling, me).wait_recv()
            for j, chip in enumerate(chips):
                copy(a, 4 + j, (*chip, 1 - c), me).wait_recv()
        for cp in first + passed:
            cp.wait_send()
        for cp in mine:
            cp.wait()

    return pl.pallas_call(
        body, name="all_gather_weights",
        out_shape=[jax.ShapeDtypeStruct((NDEV,) + sh.shape, sh.dtype) for sh in shards],
        in_specs=[ANY] * n, out_specs=[ANY] * n,
        scratch_shapes=[pltpu.SemaphoreType.DMA((7 * n,)), pltpu.SemaphoreType.DMA((7 * n,)), pltpu.SemaphoreType.DMA((n,))],
    )(*shards)


def _pair_exchange(grads):
    n = len(grads)

    def body(*refs):
        g_refs, out_refs = refs[:n], refs[n:2 * n]
        send_sems, recv_sems = refs[2 * n:]
        x, y, c = _position()
        copies = [pltpu.make_async_remote_copy(
            src_ref=g_refs[a].at[:, pl.ds(1 - c, 1)], dst_ref=out_refs[a], send_sem=send_sems.at[a],
            recv_sem=recv_sems.at[a], device_id=(x, y, 1 - c), device_id_type=MESH_ID) for a in range(n)]
        for cp in copies:
            cp.start()
        for cp in copies:
            cp.wait()

    return pl.pallas_call(
        body, name="grad_pair_exchange",
        out_shape=[jax.ShapeDtypeStruct((4, 1) + g.shape[2:], g.dtype) for g in grads],
        in_specs=[ANY] * n, out_specs=[ANY] * n,
        scratch_shapes=[pltpu.SemaphoreType.DMA((n,)), pltpu.SemaphoreType.DMA((n,))],
    )(*grads)


def _pair_sum(g, got, idx, *, tr, name):
    r, c = g.shape[2:]

    def body(idx_ref, g_ref, got_ref, pb_ref, own_ref):
        p = g_ref[0, 0].astype(F32) + got_ref[0, 0].astype(F32)
        pb_ref[0] = p.astype(BF16)

        @pl.when(pl.program_id(1) == idx_ref[1])
        def _():
            own_ref[...] = p

    return pl.pallas_call(
        body, name=name,
        grid_spec=pltpu.PrefetchScalarGridSpec(
            num_scalar_prefetch=1, grid=(r // tr, 4),
            in_specs=[pl.BlockSpec((1, 1, tr, c), lambda i, j, idx: (j, idx[0], i, 0)),
                      pl.BlockSpec((1, 1, tr, c), lambda i, j, idx: (j, 0, i, 0))],
            out_specs=[pl.BlockSpec((1, tr, c), lambda i, j, idx: (j, i, 0)),
                       pl.BlockSpec((tr, c), lambda i, j, idx: (i, 0))]),
        out_shape=[jax.ShapeDtypeStruct((4, r, c), BF16), jax.ShapeDtypeStruct((r, c), F32)],
        compiler_params=_cparams(32, ("arbitrary", "arbitrary")),
    )(idx, g, got)


HBM = pl.BlockSpec(memory_space=pltpu.HBM)
SEM = pl.BlockSpec(memory_space=pltpu.SEMAPHORE)
DATAFLOW = pltpu.SideEffectType.DATAFLOW_SIDE_EFFECTING


PEERS = {"gather": NDEV - 1, "scatter": NDEV - 1, "chips": 3}


def _exchange_copies(src_refs, land_refs, send_sems, recv_sems, mode):
    x, y, c = _position()
    me, my_chip = 4 * x + 2 * y + c, 2 * x + y
    npeers = PEERS[mode]
    copies = []
    for a, (s_ref, l_ref) in enumerate(zip(src_refs, land_refs)):
        for k in range(npeers):
            if mode == "chips":
                px, py, pc = x ^ ((k + 1) >> 1), y ^ ((k + 1) & 1), c
                src, dst = s_ref.at[2 * px + py], l_ref.at[my_chip]
            else:
                px, py, pc = x ^ ((k + 1) >> 2), y ^ (((k + 1) >> 1) & 1), c ^ ((k + 1) & 1)
                src, dst = (s_ref.at[4 * px + 2 * py + pc] if mode == "scatter" else s_ref), l_ref.at[me]
            copies.append(pltpu.make_async_remote_copy(
                src_ref=src, dst_ref=dst, send_sem=send_sems.at[npeers * a + k], recv_sem=recv_sems.at[npeers * a + k],
                device_id=(px, py, pc), device_id_type=MESH_ID))
    return copies


def _exchange_start(srcs, lands, after, *, mode, name):
    n = len(srcs)
    nsem = PEERS[mode] * n

    def body(*refs):
        token = refs[-1]
        for cp in _exchange_copies(refs[:n], refs[n:2 * n], refs[2 * n + 1], refs[2 * n + 2], mode):
            cp.start()
        token[...] = jnp.zeros_like(token)

    arrays = list(srcs) + list(lands)
    outs = pl.pallas_call(
        body, name=name,
        out_shape=(pltpu.SemaphoreType.DMA((nsem,)), pltpu.SemaphoreType.DMA((nsem,)),
                   *[pltpu.HBM(a.shape, a.dtype) for a in arrays], jax.ShapeDtypeStruct((SUBLANES, LANES), F32)),
        in_specs=[HBM] * (2 * n) + [ANY],
        out_specs=(SEM, SEM, *[HBM] * (2 * n), pl.BlockSpec(memory_space=pltpu.VMEM)),
        input_output_aliases={i: 2 + i for i in range(2 * n)},
        compiler_params=pltpu.CompilerParams(has_side_effects=DATAFLOW),
    )(*[pltpu.with_memory_space_constraint(a, pltpu.HBM) for a in arrays], after)
    return outs[0], outs[1], outs[2:2 + n], outs[2 + n:2 + 2 * n], outs[-1]


def _exchange_wait(send_sems, recv_sems, srcs, lands, after, *, mode, name):
    n = len(srcs)

    def body(*refs):
        for cp in _exchange_copies(refs[:n], refs[n:2 * n], refs[2 * n], refs[2 * n + 1], mode):
            cp.wait_send()
            cp.wait_recv()

    arrays = list(srcs) + list(lands)
    outs = pl.pallas_call(
        body, name=name,
        out_shape=tuple(pltpu.HBM(a.shape, a.dtype) for a in arrays),
        in_specs=[HBM] * (2 * n) + [SEM, SEM, ANY],
        out_specs=tuple([HBM] * (2 * n)),
        input_output_aliases={i: i for i in range(2 * n)},
        compiler_params=pltpu.CompilerParams(has_side_effects=DATAFLOW),
    )(*arrays, send_sems, recv_sems, after)
    return outs[n:]


def _own_slot(value, me):
    return lax.dynamic_update_index_in_dim(lax.empty((NDEV,) + value.shape, value.dtype), value, me, 0)


def _small_all_reduce(parts):
    def body(gmp_ref, gmo_ref, gfp_ref, gfo_ref, ga_ref, gc_ref, dw_ref, bf_ref, loss_ref,
             out_ref, buf, send_sems, recv_sems):
        x, y, c = _position()
        me = 4 * x + 2 * y + c

        def colsum(v):
            return jnp.sum(v, axis=0, keepdims=True)

        loss = jnp.sum(colsum(loss_ref[...]), axis=1, keepdims=True) * (0.5 / D)
        rows = [colsum(gmp_ref[...]), colsum(gmo_ref[...]), colsum(gfp_ref[...]), colsum(gfo_ref[...]),
                jnp.concatenate([colsum(ga_ref[...]), colsum(gc_ref[...])], axis=1),
                jnp.concatenate([colsum(dw_ref[0]), colsum(dw_ref[1])], axis=1),
                jnp.concatenate([colsum(dw_ref[2]), colsum(bf_ref[...]), jnp.broadcast_to(loss, (1, 128)),
                                 jnp.zeros((1, 256), F32)], axis=1),
                jnp.zeros((1, D), F32)]
        buf[me] = jnp.concatenate(rows, axis=0)
        copies = []
        for mm in range(1, NDEV):
            peer = (x ^ (mm >> 2), y ^ ((mm >> 1) & 1), c ^ (mm & 1))
            copies.append(pltpu.make_async_remote_copy(
                src_ref=buf.at[me], dst_ref=buf.at[me], send_sem=send_sems.at[mm - 1], recv_sem=recv_sems.at[mm - 1],
                device_id=peer, device_id_type=MESH_ID))
        for cp in copies:
            cp.start()
        for cp in copies:
            cp.wait_recv()
        for cp in copies:
            cp.wait_send()
        acc = buf[0]
        for d in range(1, NDEV):
            acc = acc + buf[d]
        out_ref[...] = acc

    vm = pl.BlockSpec(memory_space=pltpu.VMEM)
    return pl.pallas_call(
        body, name="small_all_reduce",
        out_shape=jax.ShapeDtypeStruct((SUBLANES, D), F32),
        in_specs=[vm] * len(parts), out_specs=vm,
        scratch_shapes=[pltpu.VMEM((NDEV, SUBLANES, D), F32), pltpu.SemaphoreType.DMA((7,)), pltpu.SemaphoreType.DMA((7,))],
    )(*parts)


def _adam_update(w, g, m, v):
    nm = ADAM_B1 * m + (1.0 - ADAM_B1) * g
    nv = ADAM_B2 * v + (1.0 - ADAM_B2) * (g * g)
    m_hat = nm / (1.0 - ADAM_B1 ** ADAM_STEP)
    v_hat = nv / (1.0 - ADAM_B2 ** ADAM_STEP)
    return -ADAM_LR * (m_hat / (jnp.sqrt(v_hat) + ADAM_EPS) + ADAM_WD * w), nm, nv


SMALL_SLOTS = {"g_mix_pre": (0, 0, D), "g_mix_post": (1, 0, D), "g_ffn_pre": (2, 0, D), "g_ffn_post": (3, 0, D),
               "g_attn_out": (4, 0, AW), "g_conv_out": (4, AW, CW), "b_forget": (6, CW, H)}


def _small_adamw(small, conv_grad, params):
    names = list(params)
    n = len(names)

    def body(*refs):
        small_ref, cg_ref = refs[0], refs[1]
        ins, outs = refs[2:2 + 3 * n], refs[2 + 3 * n:]
        for i, name in enumerate(names):
            w_ref, m_ref, v_ref = ins[3 * i:3 * i + 3]
            g_ref, d_ref, nm_ref, nv_ref = outs[4 * i:4 * i + 4]
            if name == "conv_w":
                g = cg_ref[...]
            else:
                r, c0, width = SMALL_SLOTS[name]
                g = small_ref[r:r + 1, c0:c0 + width]
            g_ref[...] = g
            d_ref[...], nm_ref[...], nv_ref[...] = _adam_update(w_ref[...], g, m_ref[...], v_ref[...])

    vm = pl.BlockSpec(memory_space=pltpu.VMEM)
    flat = [a for name in names for a in params[name]]
    outs = pl.pallas_call(
        body, name="adamw_small",
        in_specs=[vm] * (2 + 3 * n), out_specs=[vm] * (4 * n),
        out_shape=[jax.ShapeDtypeStruct(params[name][0].shape, F32) for name in names for _ in range(4)],
    )(small, conv_grad, *flat)
    return {name: outs[4 * i:4 * i + 4] for i, name in enumerate(names)}


def _chip_sum_adamw(got, own, idx, w, m, v, *, tr, name):
    rows, cols = w.shape
    gcols = own.shape[1]

    def body(idx_ref, got_ref, own_ref, w_ref, m_ref, v_ref, g_ref, d_ref, nm_ref, nv_ref):
        g = jnp.zeros((tr, gcols), F32)
        for j in range(4):
            g = g + jnp.where(idx_ref[1] == j, own_ref[...], got_ref[j].astype(F32))
        g = g[:, :cols]
        g_ref[...] = g
        d_ref[...], nm_ref[...], nv_ref[...] = _adam_update(w_ref[...], g, m_ref[...], v_ref[...])

    spec = pl.BlockSpec((tr, cols), lambda i, idx: (i, 0))
    gspec = pl.BlockSpec((tr, gcols), lambda i, idx: (i, 0))
    return pl.pallas_call(
        body, name=name,
        grid_spec=pltpu.PrefetchScalarGridSpec(
            num_scalar_prefetch=1, grid=(rows // tr,),
            in_specs=[pl.BlockSpec((4, tr, gcols), lambda i, idx: (0, i, 0)), gspec, spec, spec, spec],
            out_specs=[spec] * 4),
        out_shape=[jax.ShapeDtypeStruct((rows, cols), F32)] * 4,
        compiler_params=_cparams(32, ("arbitrary",)),
    )(idx, got, own, w, m, v)


def _device_sum_adamw(land, w, m, v, *, tr, name):
    rows, cols = w.shape

    def body(land_ref, w_ref, m_ref, v_ref, g_ref, d_ref, nm_ref, nv_ref):
        g = land_ref[0].astype(F32)
        for dev in range(1, NDEV):
            g = g + land_ref[dev].astype(F32)
        g_ref[...] = g
        d_ref[...], nm_ref[...], nv_ref[...] = _adam_update(w_ref[...], g, m_ref[...], v_ref[...])

    spec = pl.BlockSpec((tr, cols), lambda i: (i, 0))
    return pl.pallas_call(
        body, name=name, grid=(rows // tr,),
        in_specs=[pl.BlockSpec((NDEV, tr, cols), lambda i: (0, i, 0)), spec, spec, spec],
        out_specs=[spec] * 4,
        out_shape=[jax.ShapeDtypeStruct((rows, cols), F32)] * 4,
        compiler_params=_cparams(32, ("arbitrary",)),
    )(land, w, m, v)


def _placement_constants():
    j = jnp.arange(128)[:, None]
    lane = jnp.arange(1024)[None, :]
    head, sub = lane // HP, lane % HP
    piece, jh = j // H, j % H
    valid = (j < 3 * H) & (jh == head)
    pq = jnp.where(valid & (sub == DH + piece), 1.0, 0.0).astype(BF16)
    pk = jnp.where(valid & (sub == DH + 3 + piece), -1.0, 0.0).astype(BF16)
    oq = jnp.where((sub >= DH + 3) & (sub < DH + 6), 1.0, 0.0).astype(F32)
    ok = jnp.where((sub >= DH) & (sub < DH + 3), 1.0, 0.0).astype(F32)
    r = jnp.arange(AW)[:, None]
    cc = jnp.arange(128)[None, :]
    sel = jnp.where((r % DH == 3) & (r // DH == cc), -1.0, 0.0).astype(BF16)
    gi = jnp.arange(GS)
    gsum = (gi[:, None] // DH == gi[None, :] // DH).astype(BF16)
    return pq, pk, oq, ok, sel, gsum


def _local_step(xs, tgt, wp, late_weights, cw8, bfp, g_attn_out, g_conv_out,
                g_mix_pre, g_mix_post, g_ffn_pre, g_ffn_post, early_grads=None, last_grad=None):
    pq, pk, oq, ok, sel, gsum = _placement_constants()
    h1t, qp, kp, vv, bcu, zf = _in_proj(xs, g_mix_pre, wp, bfp, pq, pk, oq, ok, tm=512)
    o, lse, mk = _attn_fwd(qp, kp, vv, t=512)
    w_out_f, wgu, wd = late_weights(lse)
    merged, y, x2, cv, h2 = _mix_out(o, bcu, cw8, g_attn_out, g_conv_out, gsum, w_out_f, xs, g_mix_post, g_ffn_pre, tm=512)
    gate, up, act, dx3, dff, loss_p, dg_ffn_post = _ffn_fwd_loss(h2, wgu, wd, x2, tgt, g_ffn_post, tm=512)

    dgu, dx2, dy, dg_ffn_pre, dg_mix_post = _ffn_bwd(dff, wd, gate, up, wgu, x2, g_ffn_pre, dx3, y, g_mix_post, tm=256)
    dw_down = _grad_matmul_blocks(act, dff, ts=4096, name="grad_w_down")
    dw_gu = _grad_matmul_blocks(dgu.reshape(NDEV, -1, FB), h2, ts=4096, name="grad_w_gate_up")
    dw_out = _grad_matmul(merged, dy, ta=1024, tb=1024, ts=2048, name="grad_w_out")
    token = early_grads(dw_out, dw_gu, dw_down) if early_grads is not None else None
    ga = g_attn_out if token is None else g_attn_out + token[0:1, 0:1]
    do, dl, dcv, db, dg_attn, dg_conv = _mix_bwd(dy, w_out_f, o, cv, bcu, ga, g_conv_out, gsum, tm=512)
    dbcu, dtaps = _conv_bwd(dcv, db, bcu, cw8, tm=512)
    dqp, dkp, dv, dkx = _attn_bwd(qp, kp, vv, do, lse, dl, mk, t=512)
    dfl, dbf = _forget_bwd(dkx, zf, sel, tm=512)
    pieces = (dqp, dkp, dv, dbcu, dfl)
    dwp = _grad_w_in(h1t, pieces)
    token = last_grad(dwp) if last_grad is not None else None
    g1 = g_mix_pre if token is None else g_mix_pre + token[0:1, 0:1]
    grad_x, dg_mix_pre = _in_proj_bwd(pieces, wp, xs, g1, dx2, tm=512)
    return (grad_x, dwp, dw_out, dw_gu, dw_down, dg_mix_pre, dg_mix_post, dg_ffn_pre, dg_ffn_post, dg_attn, dg_conv,
            dtaps, dbf, loss_p)


BIG_TILES = {"w_in": 256, "w_out": 128, "w_gate_up": 176, "w_down": 176}


def kernel(x, w_in, b_forget, conv_w, g_attn_out, g_conv_out, w_out, g_mix_pre, g_mix_post, w_gate_up, w_down, g_ffn_pre, g_ffn_post, loss_target, m_w_in, m_b_forget, m_conv_w, m_g_attn_out, m_g_conv_out, m_w_out, m_g_mix_pre, m_g_mix_post, m_w_gate_up, m_w_down, m_g_ffn_pre, m_g_ffn_post, v_w_in, v_b_forget, v_conv_w, v_g_attn_out, v_g_conv_out, v_w_out, v_g_mix_pre, v_g_mix_post, v_w_gate_up, v_w_down, v_g_ffn_pre, v_g_ffn_post):
    xc, yc, cc = _position()
    my_chip = 2 * xc + yc
    me = 2 * my_chip + cc
    idx = jnp.stack([cc, my_chip]).astype(jnp.int32)
    tables = _in_layout_tables()
    pad_in = lambda a: jnp.pad(a, ((0, 0), (0, IN_PAD - IN_COLS)))

    g_in, g_taps = _all_gather([pad_in(w_in[0]).astype(BF16), conv_w[0]])
    wp = _assemble_w_in(g_in, tables, tr=256)
    cw8 = jnp.pad(g_taps.transpose(1, 0, 2).reshape(3, CW), ((0, SUBLANES - 3), (0, 0)))

    late = [w_out[0].astype(BF16), w_gate_up[0].astype(BF16), w_down[0].astype(BF16)]
    ssem, rsem, late_thru, land_thru, token = _exchange_start(
        late, [_own_slot(s, me) for s in late], g_in, mode="gather", name="gather_late_start")
    bfp = jnp.pad(b_forget, ((0, 0), (0, 128 - H))) + token[0:1, :]

    def late_weights(after):
        l_out, l_gu, l_down = _exchange_wait(ssem, rsem, late_thru, land_thru, after, mode="gather", name="gather_late_wait")
        return l_out.reshape(D, D), l_gu.reshape(2, 4, D, FB), l_down.reshape(4, FB, D)

    early = {}

    def early_grads(dw_out, dw_gu, dw_down):
        srcs = [dw_out.reshape(NDEV, D // NDEV, D), dw_gu, dw_down.reshape(NDEV, DFF // NDEV, D)]
        lands = [_own_slot(lax.dynamic_index_in_dim(s, me, 0, keepdims=False), me) for s in srcs]
        early["handles"] = _exchange_start(srcs, lands, dw_out, mode="scatter", name="scatter_early_start")
        return early["handles"][4]

    last = {}

    def last_grad(dwp):
        g_w_in = _disassemble_w_in(dwp, tables, tr=256).reshape(4, 2, D, IN_PAD)
        (from_sibling,) = _pair_exchange([g_w_in])
        pair_b, last["own"] = _pair_sum(g_w_in, from_sibling, idx, tr=BIG_TILES["w_in"], name="grad_pair_sum_w_in")
        land = lax.dynamic_update_index_in_dim(lax.empty(pair_b.shape, pair_b.dtype),
                                               lax.dynamic_index_in_dim(pair_b, my_chip, 0, keepdims=False), my_chip, 0)
        last["handles"] = _exchange_start([pair_b], [land], last["own"], mode="chips", name="chips_w_in_start")
        return last["handles"][4]

    (grad_x, dwp, dw_out, dw_gu, dw_down, dg_mix_pre, dg_mix_post, dg_ffn_pre, dg_ffn_post, dg_attn, dg_conv,
     dtaps, dbf, loss_p) = _local_step(x[0], loss_target[0], wp, late_weights, cw8, bfp, g_attn_out, g_conv_out,
                                        g_mix_pre, g_mix_post, g_ffn_pre, g_ffn_post, early_grads, last_grad)

    e_ssem, e_rsem, e_srcs, e_lands, _ = early["handles"]
    land_out, land_gu, land_down = _exchange_wait(e_ssem, e_rsem, e_srcs, e_lands, dg_mix_pre, mode="scatter",
                                                  name="scatter_early_wait")
    res = {}
    big = {"w_out": (land_out, w_out[0], m_w_out[0], v_w_out[0]),
           "w_gate_up": (land_gu, w_gate_up[0].T, m_w_gate_up[0].T, v_w_gate_up[0].T),
           "w_down": (land_down, w_down[0], m_w_down[0], v_w_down[0])}
    for name, (land, w, m, v) in big.items():
        outs = _device_sum_adamw(land, w, m, v, tr=BIG_TILES[name], name="adamw_" + name)
        res[name] = [(o.T if name == "w_gate_up" else o)[None] for o in outs]
    c_ssem, c_rsem, c_srcs, c_lands, _ = last["handles"]
    after = sum(res[n][1][0, :SUBLANES, :LANES] for n in big)
    (from_chips,) = _exchange_wait(c_ssem, c_rsem, c_srcs, c_lands, after, mode="chips", name="chips_w_in_wait")
    outs = _chip_sum_adamw(from_chips, last["own"], idx, w_in[0], m_w_in[0], v_w_in[0],
                           tr=BIG_TILES["w_in"], name="adamw_w_in")
    res["w_in"] = [o[None] for o in outs]

    small = _small_all_reduce([dg_mix_pre, dg_mix_post, dg_ffn_pre, dg_ffn_post, dg_attn, dg_conv, dtaps, dbf, loss_p])
    taps_full = jnp.concatenate([small[5:6, :CW], small[5:6, CW:], small[6:7, :CW]], axis=0)
    loss = small[6, CW + 128]
    smalls = {"b_forget": (b_forget, m_b_forget, v_b_forget), "conv_w": (conv_w[0], m_conv_w[0], v_conv_w[0]),
              "g_attn_out": (g_attn_out, m_g_attn_out, v_g_attn_out), "g_conv_out": (g_conv_out, m_g_conv_out, v_g_conv_out),
              "g_mix_pre": (g_mix_pre, m_g_mix_pre, v_g_mix_pre), "g_mix_post": (g_mix_post, m_g_mix_post, v_g_mix_post),
              "g_ffn_pre": (g_ffn_pre, m_g_ffn_pre, v_g_ffn_pre), "g_ffn_post": (g_ffn_post, m_g_ffn_post, v_g_ffn_post)}
    for name, outs in _small_adamw(small, lax.dynamic_slice(taps_full, (0, me * 64), (3, 64)), smalls).items():
        res[name] = [o[None] for o in outs] if name == "conv_w" else list(outs)

    order = ["w_in", "b_forget", "conv_w", "g_attn_out", "g_conv_out", "w_out", "g_mix_pre", "g_mix_post",
             "w_gate_up", "w_down", "g_ffn_pre", "g_ffn_post"]
    outs = [loss, grad_x[None]]
    for k in range(4):
        outs += [res[n][k] for n in order]
    return tuple(outs)
```

```python
import functools

import numpy as np

import jax
import jax.numpy as jnp
from jax import lax
from jax.experimental import pallas as pl
from jax.experimental.pallas import tpu as pltpu

F32 = jnp.float32
BF16 = jnp.bfloat16
MESH_ID = pl.DeviceIdType.MESH

D = 1024
H = 8
DH = 64
AW = 512
CW = 512
DFF = 2816
FB = DFF // 4
HP = 128
OFF_Q, OFF_K, OFF_V, OFF_BCU, OFF_F = 0, 512, 1024, 1536, 3072
WP = OFF_F + 128
PIECES = ((OFF_Q, OFF_K), (OFF_K, OFF_V), (OFF_V, OFF_BCU), (OFF_BCU, OFF_F), (OFF_F, WP))
EPS = 1e-6
NDEV = 8
LANES = 128
SUBLANES = 8
IN_COLS = 385
IN_PAD = 512
IN_MAIN = 384
WIN = 640
ADAM_LR, ADAM_B1, ADAM_B2, ADAM_EPS, ADAM_WD, ADAM_STEP = 0.001, 0.9, 0.999, 1e-08, 0.01, 10

NT = (((1,), (1,)), ((), ()))
TN = (((0,), (0,)), ((), ()))


def _cparams(vmem_mb=None, sem=None):
    kw = {}
    if vmem_mb is not None:
        kw["vmem_limit_bytes"] = vmem_mb << 20
    if sem is not None:
        kw["dimension_semantics"] = sem
    return pltpu.CompilerParams(**kw)


def _full(shape):
    return pl.BlockSpec(shape, lambda *_: (0,) * len(shape))


def _resident(shape):
    return pl.BlockSpec(shape, lambda *_: (0,) * len(shape), pipeline_mode=pl.Buffered(1))


def _rows(tm, width):
    return pl.BlockSpec((tm, width), lambda i: (i, 0))


def _fold8(v):
    r, w = v.shape
    return jnp.sum(v.reshape(r // SUBLANES, SUBLANES, w), axis=0)


def _split_dot(v, m01):
    hi = v.astype(BF16)
    lo = (v - hi.astype(F32)).astype(BF16)
    return (jnp.dot(hi, m01, preferred_element_type=F32)
            + jnp.dot(lo, m01, preferred_element_type=F32))


GS = 256


def _group_sum(v, g01):
    parts = [_split_dot(v[:, c:c + GS], g01) for c in range(0, v.shape[1], GS)]
    return parts[0] if len(parts) == 1 else jnp.concatenate(parts, axis=1)


def _exact_dot01(m01, v):
    p1 = v.astype(BF16)
    r1 = v - p1.astype(F32)
    p2 = r1.astype(BF16)
    p3 = (r1 - p2.astype(F32)).astype(BF16)
    return (jnp.dot(m01, p1, preferred_element_type=F32) + jnp.dot(m01, p2, preferred_element_type=F32)
            + jnp.dot(m01, p3, preferred_element_type=F32))


def _rms_fwd(v, g):
    r = lax.rsqrt(jnp.mean(v * v, axis=-1, keepdims=True) + EPS)
    n = v * r
    return n * g, n, r


def _rms_bwd(do, n, r, g):
    dn = do * g
    return r * (dn - n * jnp.mean(dn * n, axis=-1, keepdims=True)), do * n


def _padded_column(n):
    if n < AW:
        return OFF_Q + n, 0.125
    if n < 3 * AW:
        return n, 1.0
    if n < 3 * AW + H:
        return OFF_F + n - 3 * AW, 1.0
    return OFF_BCU + n - 3 * AW - H, 1.0


def _in_layout_tables():
    dest = -np.ones((IN_PAD, LANES), np.int32)
    dest_f = -np.ones((IN_PAD, LANES), np.int32)
    scale = np.zeros((IN_PAD, LANES), np.float32)
    starts = []
    for k in range(NDEV):
        cols = [_padded_column(IN_COLS * k + j) for j in range(IN_COLS)]
        main = [c for c, _ in cols if c < OFF_F]
        ws = min((min(main) // LANES) * LANES, OFF_F - WIN)
        assert ws <= min(main) and max(main) < ws + WIN
        starts.append(ws)
        for j, (c, sc) in enumerate(cols):
            scale[j, k] = sc
            if c < OFF_F:
                dest[j, k] = c - ws
            else:
                dest_f[j, k] = c - OFF_F
    f_shards = tuple(k for k in range(NDEV) if (dest_f[:, k] >= 0).any())
    return tuple(starts), f_shards, jnp.asarray(dest), jnp.asarray(dest_f), jnp.asarray(scale)


def _perm(dest_ref, scale_ref, k, width, rows=IN_PAD):
    lane = lax.broadcasted_iota(jnp.int32, (rows, width), 1)
    return jnp.where(dest_ref[0:rows, k:k + 1] == lane, scale_ref[0:rows, k:k + 1], 0.0).astype(BF16)


def _assemble_w_in(blocks, last_cols, tables, *, tr):
    starts, f_shards, dest, dest_f, scale = tables
    last = [_padded_column(IN_COLS * k + IN_MAIN) for k in range(NDEV)]
    f_main = [any(_padded_column(IN_COLS * k + j)[0] >= OFF_F for j in range(IN_MAIN)) for k in range(NDEV)]
    assert IN_COLS == IN_MAIN + 1

    def body(b_ref, c_ref, dest_ref, destf_ref, scale_ref, o_ref):
        o_ref[...] = jnp.zeros_like(o_ref)
        lane = lax.broadcasted_iota(jnp.int32, (tr, LANES), 1)
        for k in range(NDEV):
            b = b_ref[k]
            ws = starts[k]
            part = jnp.dot(b, _perm(dest_ref, scale_ref, k, WIN, IN_MAIN), preferred_element_type=F32)
            o_ref[:, ws:ws + WIN] += part.astype(BF16)
            if f_main[k]:
                part = jnp.dot(b, _perm(destf_ref, scale_ref, k, 128, IN_MAIN), preferred_element_type=F32)
                o_ref[:, OFF_F:WP] += part.astype(BF16)
            col, sc = last[k]
            tile = (col // LANES) * LANES
            o_ref[:, tile:tile + LANES] += jnp.where(lane == col - tile, c_ref[:, k:k + 1] * sc, 0.0).astype(BF16)

    tab = _full((IN_PAD, LANES))
    return pl.pallas_call(
        body, name="assemble_w_in", grid=(D // tr,),
        in_specs=[pl.BlockSpec((NDEV, tr, IN_MAIN), lambda i: (0, i, 0)), _rows(tr, LANES), tab, tab, tab],
        out_specs=_rows(tr, WP),
        out_shape=jax.ShapeDtypeStruct((D, WP), BF16),
        compiler_params=_cparams(48, ("arbitrary",)),
    )(blocks, last_cols, dest, dest_f, scale)


def _disassemble_w_in(dwp, tables, *, tr):
    starts, f_shards, dest, dest_f, scale = tables
    width = dwp.shape[1]

    def body(g_ref, dest_ref, destf_ref, scale_ref, o_ref):
        for k in range(NDEV):
            ws = starts[k]
            acc = lax.dot_general(g_ref[:, ws:ws + WIN], _perm(dest_ref, scale_ref, k, WIN), NT, preferred_element_type=F32)
            if k in f_shards:
                acc = acc + lax.dot_general(g_ref[:, OFF_F:WP], _perm(destf_ref, scale_ref, k, 128), NT,
                                            preferred_element_type=F32)
            o_ref[k] = acc.astype(BF16)

    tab = _full((IN_PAD, LANES))
    return pl.pallas_call(
        body, name="disassemble_w_in", grid=(D // tr,),
        in_specs=[_rows(tr, width), tab, tab, tab],
        out_specs=pl.BlockSpec((NDEV, tr, IN_PAD), lambda i: (0, i, 0)),
        out_shape=jax.ShapeDtypeStruct((NDEV, D, IN_PAD), BF16),
        compiler_params=_cparams(48, ("arbitrary",)),
    )(dwp, dest, dest_f, scale)


def _in_proj(x, g1, wp, bfp, pq, pk, oq, ok, *, tm):
    s = x.shape[0]

    def body(x_ref, g_ref, w_ref, bf_ref, pq_ref, pk_ref, oq_ref, ok_ref,
             ht_ref, qp_ref, kp_ref, v_ref, bcu_ref, z_ref, carry):
        @pl.when(pl.program_id(0) == 0)
        def _():
            carry[...] = jnp.zeros_like(carry)

        h = _rms_fwd(x_ref[...], g_ref[...])[0].astype(BF16)
        ht_ref[...] = h.T
        z = jnp.dot(h, w_ref[:, OFF_F:WP], preferred_element_type=F32) + bf_ref[...]
        z_ref[...] = z
        lane = lax.broadcasted_iota(jnp.int32, (tm, 128), 1)
        logf = jnp.where(lane < H, jnp.minimum(z, 0.0) - jnp.log(1.0 + jnp.exp(-jnp.abs(z))), 0.0)
        row = lax.broadcasted_iota(jnp.int32, (tm, tm), 0)
        col = lax.broadcasted_iota(jnp.int32, (tm, tm), 1)
        tri = (col <= row).astype(BF16)
        c = _exact_dot01(tri, logf) + carry[0:1, :]
        carry[...] = jnp.broadcast_to(c[tm - 1:tm, :], carry.shape)
        c1 = c.astype(BF16).astype(F32)
        r1 = c - c1
        c2 = r1.astype(BF16).astype(F32)
        c3 = (r1 - c2).astype(BF16).astype(F32)
        zc = (c1 + pltpu.roll(c2, 8, axis=1) + pltpu.roll(c3, 16, axis=1)).astype(BF16)

        def pad_heads(v):
            blocks = []
            for pair in range(H // 2):
                two = v[:, 128 * pair:128 * (pair + 1)]
                blocks.append(jnp.where(lane < DH, two, 0.0))
                blocks.append(jnp.where(lane < DH, pltpu.roll(two, DH, axis=1), 0.0))
            return jnp.concatenate(blocks, axis=1)

        q = jnp.dot(h, w_ref[:, OFF_Q:OFF_K], preferred_element_type=F32)
        qp_ref[...] = (pad_heads(q) + jnp.dot(zc, pq_ref[...], preferred_element_type=F32) + oq_ref[...]).astype(BF16)
        k = jnp.dot(h, w_ref[:, OFF_K:OFF_V], preferred_element_type=F32)
        kp_ref[...] = (pad_heads(k) + jnp.dot(zc, pk_ref[...], preferred_element_type=F32) + ok_ref[...]).astype(BF16)
        v = pad_heads(jnp.dot(h, w_ref[:, OFF_V:OFF_BCU], preferred_element_type=F32))
        ones_lane = lax.broadcasted_iota(jnp.int32, (tm, H * HP), 1) % HP == DH
        v_ref[...] = jnp.where(ones_lane, 1.0, v).astype(BF16)
        bcu_ref[...] = jnp.dot(h, w_ref[:, OFF_BCU:OFF_F], preferred_element_type=F32).astype(BF16)

    return pl.pallas_call(
        body, name="in_proj", grid=(s // tm,),
        in_specs=[_rows(tm, D), _full((1, D)), _resident((D, WP)), _full((1, 128)),
                  _full((128, 1024)), _full((128, 1024)), _full((1, 1024)), _full((1, 1024))],
        out_specs=[pl.BlockSpec((D, tm), lambda i: (0, i)), _rows(tm, 1024), _rows(tm, 1024), _rows(tm, 1024),
                   _rows(tm, 3 * CW), _rows(tm, 128)],
        out_shape=[jax.ShapeDtypeStruct((D, s), BF16), jax.ShapeDtypeStruct((s, 1024), BF16),
                   jax.ShapeDtypeStruct((s, 1024), BF16), jax.ShapeDtypeStruct((s, 1024), BF16),
                   jax.ShapeDtypeStruct((s, 3 * CW), BF16), jax.ShapeDtypeStruct((s, 128), F32)],
        scratch_shapes=[pltpu.VMEM((SUBLANES, 128), F32)],
        compiler_params=_cparams(56, ("arbitrary",)),
    )(x, g1, wp, bfp, pq, pk, oq, ok)


def _attn_fwd(qp, kp, v, *, t):
    s = qp.shape[0]
    nq = s // t

    def body(q_ref, k_ref, v_ref, o_ref, lse_ref, mk_ref):
        qi = pl.program_id(1)
        row = lax.broadcasted_iota(jnp.int32, (t, t), 0)
        col = lax.broadcasted_iota(jnp.int32, (t, t), 1)
        lane = lax.broadcasted_iota(jnp.int32, (t, 128), 1)

        def head_step(hh, ki, carry, masked):
            m, acc = carry
            off = pl.multiple_of(ki * t, t)
            q = q_ref[:, HP * hh:HP * (hh + 1)]
            k = k_ref[pl.ds(off, t), HP * hh:HP * (hh + 1)]
            sc = lax.dot_general(q, k, NT, preferred_element_type=F32)
            if masked:
                sc = jnp.where(col <= row, sc, -1e30)
            mn = jnp.maximum(m, jnp.max(sc, axis=-1, keepdims=True))
            p = jnp.exp(sc - mn).astype(BF16)
            acc = jnp.exp(m - mn) * acc + jnp.dot(p, v_ref[pl.ds(off, t), HP * hh:HP * (hh + 1)],
                                                  preferred_element_type=F32)
            return mn, acc

        def step(ki, carry, masked):
            new = tuple(head_step(hh, ki, carry[hh], masked) for hh in range(2))
            mk_ref[ki] = jnp.where(lane < DH, jnp.broadcast_to(new[0][0], (t, 128)), jnp.broadcast_to(new[1][0], (t, 128)))
            return new

        init = (jnp.full((t, 1), -1e30, F32), jnp.zeros((t, 128), F32))
        carry = lax.fori_loop(0, qi, functools.partial(step, masked=False), (init, init))
        (m0, acc0), (m1, acc1) = step(qi, carry, True)
        l0, l1 = acc0[:, DH:DH + 1], acc1[:, DH:DH + 1]
        o_ref[...] = jnp.where(lane < DH, acc0 / l0, pltpu.roll(acc1 / l1, DH, axis=1))
        lse_ref[...] = jnp.where(lane < DH, jnp.broadcast_to(m0 + jnp.log(l0), (t, 128)),
                                 jnp.broadcast_to(m1 + jnp.log(l1), (t, 128)))

    return pl.pallas_call(
        body, name="attn_fwd", grid=(H // 2, nq),
        in_specs=[pl.BlockSpec((t, 2 * HP), lambda p, i: (i, p)),
                  pl.BlockSpec((s, 2 * HP), lambda p, i: (0, p)),
                  pl.BlockSpec((s, 2 * HP), lambda p, i: (0, p))],
        out_specs=[pl.BlockSpec((t, 128), lambda p, i: (i, p)), pl.BlockSpec((t, 128), lambda p, i: (i, p)),
                   pl.BlockSpec((nq, t, 128), lambda p, i: (0, i, p))],
        out_shape=[jax.ShapeDtypeStruct((s, AW), F32), jax.ShapeDtypeStruct((s, AW), F32),
                   jax.ShapeDtypeStruct((nq, s, AW), F32)],
        compiler_params=_cparams(48, ("arbitrary", "arbitrary")),
    )(qp, kp, v)


HALO = 16


def _conv_taps(bcu_ref, halo_ref, first, tm):
    z = bcu_ref[:, CW:2 * CW].astype(F32) * bcu_ref[:, 2 * CW:3 * CW].astype(F32)
    zh = jnp.where(first, 0.0, halo_ref[:, CW:2 * CW].astype(F32) * halo_ref[:, 2 * CW:3 * CW].astype(F32))
    row = lax.broadcasted_iota(jnp.int32, (tm, CW), 0)
    last, before = zh[HALO - 1:HALO, :], zh[HALO - 2:HALO - 1, :]
    z1 = jnp.where(row == 0, last, pltpu.roll(z, 1, axis=0))
    z2 = jnp.where(row == 0, before, jnp.where(row == 1, last, pltpu.roll(z, 2, axis=0)))
    return z, z1, z2


def _halo_before(tm, width):
    return pl.BlockSpec((HALO, width), lambda i: (jnp.maximum(i * (tm // HALO) - 1, 0), 0))


def _mix_out(o, bcu, cw8, ga, gc, gsum, w_out, x, g_post, g_ffn_pre, *, tm):
    s = x.shape[0]

    def body(o_ref, bcu_ref, halo_ref, cw_ref, ga_ref, gc_ref, gs_ref, w_ref, x_ref, g_ref, gf_ref,
             merged_ref, y_ref, x2_ref, cv_ref, h2_ref):
        z, z1, z2 = _conv_taps(bcu_ref, halo_ref, pl.program_id(0) == 0, tm)
        cv = cw_ref[0:1, :] * z2 + cw_ref[1:2, :] * z1 + cw_ref[2:3, :] * z
        cv_ref[...] = cv
        conv = bcu_ref[:, 0:CW].astype(F32) * cv
        ov = o_ref[...]
        ra = lax.rsqrt(_group_sum(ov * ov, gs_ref[...]) * (1.0 / DH) + EPS)
        rc = lax.rsqrt(_group_sum(conv * conv, gs_ref[...]) * (1.0 / DH) + EPS)
        merged = jnp.concatenate([ov * ra * ga_ref[...], conv * rc * gc_ref[...]], axis=1).astype(BF16)
        merged_ref[...] = merged
        y = jnp.dot(merged, w_ref[...], preferred_element_type=F32)
        y_ref[...] = y
        x2 = x_ref[...] + _rms_fwd(y, g_ref[...])[0]
        x2_ref[...] = x2
        h2_ref[...] = _rms_fwd(x2, gf_ref[...])[0].astype(BF16)

    return pl.pallas_call(
        body, name="mix_out", grid=(s // tm,),
        in_specs=[_rows(tm, AW), _rows(tm, 3 * CW), _halo_before(tm, 3 * CW), _full((SUBLANES, CW)),
                  _full((1, AW)), _full((1, CW)), _full((GS, GS)), _resident((D, D)), _rows(tm, D), _full((1, D)),
                  _full((1, D))],
        out_specs=[_rows(tm, D), _rows(tm, D), _rows(tm, D), _rows(tm, CW), _rows(tm, D)],
        out_shape=[jax.ShapeDtypeStruct((s, D), BF16), jax.ShapeDtypeStruct((s, D), F32),
                   jax.ShapeDtypeStruct((s, D), F32), jax.ShapeDtypeStruct((s, CW), F32),
                   jax.ShapeDtypeStruct((s, D), BF16)],
        compiler_params=_cparams(48, ("arbitrary",)),
    )(o, bcu, bcu, cw8, ga, gc, gsum, w_out, x, g_post, g_ffn_pre)


def _ffn_fwd_loss(h2, wgu, wd, x2, target, g_post, *, tm):
    s = x2.shape[0]

    def body(h_ref, w_ref, wd_ref, x2_ref, t_ref, g_ref,
             gate_ref, up_ref, a_ref, dx3_ref, dff_ref, loss_ref, dg_ref):
        @pl.when(pl.program_id(0) == 0)
        def _():
            loss_ref[...] = jnp.zeros_like(loss_ref)
            dg_ref[...] = jnp.zeros_like(dg_ref)

        h = h_ref[...]
        ff = None
        for j in range(4):
            gate = jnp.dot(h, w_ref[0, j], preferred_element_type=F32)
            up = jnp.dot(h, w_ref[1, j], preferred_element_type=F32)
            gate_ref[j] = gate.astype(BF16)
            up_ref[j] = up.astype(BF16)
            act = (gate * jax.nn.sigmoid(gate) * up).astype(BF16)
            a_ref[j] = act
            part = jnp.dot(act, wd_ref[j], preferred_element_type=F32)
            ff = part if ff is None else ff + part
        out, n, r = _rms_fwd(ff, g_ref[...])
        e = x2_ref[...] + out - t_ref[...]
        loss_ref[...] += _fold8(e * e)
        dx3 = e * (1.0 / D)
        dx3_ref[...] = dx3
        dff, dg = _rms_bwd(dx3, n, r, g_ref[...])
        dff_ref[...] = dff.astype(BF16)
        dg_ref[...] += _fold8(dg)

    blk4 = pl.BlockSpec((4, tm, FB), lambda i: (0, i, 0))
    return pl.pallas_call(
        body, name="ffn_fwd_loss", grid=(s // tm,),
        in_specs=[_rows(tm, D), _resident((2, 4, D, FB)), _resident((4, FB, D)), _rows(tm, D), _rows(tm, D), _full((1, D))],
        out_specs=[blk4, blk4, blk4, _rows(tm, D), _rows(tm, D), _full((SUBLANES, D)), _full((SUBLANES, D))],
        out_shape=[jax.ShapeDtypeStruct((4, s, FB), BF16)] * 3
        + [jax.ShapeDtypeStruct((s, D), F32), jax.ShapeDtypeStruct((s, D), BF16),
           jax.ShapeDtypeStruct((SUBLANES, D), F32), jax.ShapeDtypeStruct((SUBLANES, D), F32)],
        compiler_params=_cparams(56, ("arbitrary",)),
    )(h2, wgu, wd, x2, target, g_post)


def _ffn_bwd(dff, wd, gate, up, wgu, x2, g_pre, dx3, y, g_post, *, tm):
    s = x2.shape[0]

    def body(dff_ref, wd_ref, gate_ref, up_ref, w_ref, x2_ref, gpre_ref, dx3_ref, y_ref, gpost_ref,
             dgu_ref, dx2_ref, dy_ref, dgpre_ref, dgpost_ref):
        @pl.when(pl.program_id(0) == 0)
        def _():
            dgpre_ref[...] = jnp.zeros_like(dgpre_ref)
            dgpost_ref[...] = jnp.zeros_like(dgpost_ref)

        dff = dff_ref[...]
        dh2 = None
        for j in range(4):
            da = lax.dot_general(dff, wd_ref[j], NT, preferred_element_type=F32)
            g = gate_ref[j].astype(F32)
            sg = jax.nn.sigmoid(g)
            dgate = (da * up_ref[j].astype(F32) * (sg * (1.0 + g * (1.0 - sg)))).astype(BF16)
            dup = (da * (g * sg)).astype(BF16)
            dgu_ref[0, j] = dgate
            dgu_ref[1, j] = dup
            part = (lax.dot_general(dgate, w_ref[0, j], NT, preferred_element_type=F32)
                    + lax.dot_general(dup, w_ref[1, j], NT, preferred_element_type=F32))
            dh2 = part if dh2 is None else dh2 + part
        _, n2, r2 = _rms_fwd(x2_ref[...], gpre_ref[...])
        dxn, dg = _rms_bwd(dh2, n2, r2, gpre_ref[...])
        dgpre_ref[...] += _fold8(dg)
        dx2 = dx3_ref[...] + dxn
        dx2_ref[...] = dx2
        _, ny, ry = _rms_fwd(y_ref[...], gpost_ref[...])
        dy, dg2 = _rms_bwd(dx2, ny, ry, gpost_ref[...])
        dy_ref[...] = dy.astype(BF16)
        dgpost_ref[...] += _fold8(dg2)

    blk4 = pl.BlockSpec((4, tm, FB), lambda i: (0, i, 0))
    return pl.pallas_call(
        body, name="ffn_bwd", grid=(s // tm,),
        in_specs=[_rows(tm, D), _resident((4, FB, D)), blk4, blk4, _resident((2, 4, D, FB)), _rows(tm, D), _full((1, D)),
                  _rows(tm, D), _rows(tm, D), _full((1, D))],
        out_specs=[pl.BlockSpec((2, 4, tm, FB), lambda i: (0, 0, i, 0)), _rows(tm, D), _rows(tm, D),
                   _full((SUBLANES, D)), _full((SUBLANES, D))],
        out_shape=[jax.ShapeDtypeStruct((2, 4, s, FB), BF16), jax.ShapeDtypeStruct((s, D), F32),
                   jax.ShapeDtypeStruct((s, D), BF16), jax.ShapeDtypeStruct((SUBLANES, D), F32),
                   jax.ShapeDtypeStruct((SUBLANES, D), F32)],
        compiler_params=_cparams(56, ("arbitrary",)),
    )(dff, wd, gate, up, wgu, x2, g_pre, dx3, y, g_post)


def _grad_matmul(a, b, *, ta, tb, ts, name):
    s, ka = a.shape
    nb = b.shape[1]
    ts = min(ts, s)
    nk = s // ts

    def body(a_ref, b_ref, o_ref, acc):
        k = pl.program_id(2)

        @pl.when(k == 0)
        def _():
            acc[...] = jnp.zeros_like(acc)

        acc[...] += lax.dot_general(a_ref[...], b_ref[...], TN, preferred_element_type=F32)

        @pl.when(k == nk - 1)
        def _():
            o_ref[...] = acc[...].astype(BF16)

    return pl.pallas_call(
        body, name=name, grid=(ka // ta, nb // tb, nk),
        in_specs=[pl.BlockSpec((ts, ta), lambda i, j, k: (k, i)), pl.BlockSpec((ts, tb), lambda i, j, k: (k, j))],
        out_specs=pl.BlockSpec((ta, tb), lambda i, j, k: (i, j)),
        out_shape=jax.ShapeDtypeStruct((ka, nb), BF16),
        scratch_shapes=[pltpu.VMEM((ta, tb), F32)],
        compiler_params=_cparams(48, ("arbitrary", "arbitrary", "arbitrary")),
    )(a, b)


GW_TILE = 256


def _grad_w_in(h1t, pieces):
    ka, s = h1t.shape
    widths = [p.shape[1] for p in pieces]
    assert all(w % GW_TILE == 0 for w in widths)
    first = [sum(widths[:i]) // GW_TILE for i in range(len(pieces))]
    count = [w // GW_TILE for w in widths]

    def body(a_ref, *refs):
        o_ref = refs[-1]
        j = pl.program_id(0)
        for ref, f0, n in zip(refs[:-1], first, count):
            @pl.when((j >= f0) & (j < f0 + n))
            def _(ref=ref):
                o_ref[...] = jnp.dot(a_ref[...], ref[...], preferred_element_type=F32).astype(BF16)

    def spec(f0, n):
        return pl.BlockSpec((s, GW_TILE), lambda j: (0, jnp.clip(j - f0, 0, n - 1)))

    return pl.pallas_call(
        body, name="grad_w_in", grid=(sum(count),),
        in_specs=[_resident((ka, s))] + [spec(f0, n) for f0, n in zip(first, count)],
        out_specs=pl.BlockSpec((ka, GW_TILE), lambda j: (0, j)),
        out_shape=jax.ShapeDtypeStruct((ka, sum(widths)), BF16),
        compiler_params=_cparams(56, ("arbitrary",)),
    )(h1t, *pieces)


def _grad_matmul_blocks(a, b, *, ts, name):
    nblk = a.shape[0] if a.ndim == 3 else b.shape[0]
    s = a.shape[-2]
    ka, nb = a.shape[-1], b.shape[-1]
    ts = min(ts, s)
    nk = s // ts

    def body(a_ref, b_ref, o_ref, acc):
        k = pl.program_id(1)

        @pl.when(k == 0)
        def _():
            acc[...] = jnp.zeros_like(acc)

        av = a_ref[0] if a.ndim == 3 else a_ref[...]
        bv = b_ref[0] if b.ndim == 3 else b_ref[...]
        acc[...] += lax.dot_general(av, bv, TN, preferred_element_type=F32)

        @pl.when(k == nk - 1)
        def _():
            o_ref[0] = acc[...].astype(BF16)

    def spec(arr, width):
        if arr.ndim == 3:
            return pl.BlockSpec((1, ts, width), lambda j, k: (j, k, 0))
        return pl.BlockSpec((ts, width), lambda j, k: (k, 0))

    return pl.pallas_call(
        body, name=name, grid=(nblk, nk),
        in_specs=[spec(a, ka), spec(b, nb)],
        out_specs=pl.BlockSpec((1, ka, nb), lambda j, k: (j, 0, 0)),
        out_shape=jax.ShapeDtypeStruct((nblk, ka, nb), BF16),
        scratch_shapes=[pltpu.VMEM((ka, nb), F32)],
        compiler_params=_cparams(48, ("arbitrary", "arbitrary")),
    )(a, b)


def _mix_bwd(dy, w_out, o, cv, bcu, ga, gc, gsum, *, tm):
    s = dy.shape[0]

    def group_norm_bwd(dn_out, v, g, gs):
        r = lax.rsqrt(_group_sum(v * v, gs) * (1.0 / DH) + EPS)
        n = v * r
        dn = dn_out * g
        return r * (dn - n * (_group_sum(dn * n, gs) * (1.0 / DH))), dn_out * n

    def body(dy_ref, w_ref, o_ref, cv_ref, bcu_ref, ga_ref, gc_ref, gs_ref,
             do_ref, dl_ref, dcv_ref, db_ref, dga_ref, dgc_ref):
        @pl.when(pl.program_id(0) == 0)
        def _():
            dga_ref[...] = jnp.zeros_like(dga_ref)
            dgc_ref[...] = jnp.zeros_like(dgc_ref)

        dm = lax.dot_general(dy_ref[...], w_ref[...], NT, preferred_element_type=F32)
        ov = o_ref[...]
        do, dga = group_norm_bwd(dm[:, 0:AW], ov, ga_ref[...], gs_ref[...])
        dob = do.astype(BF16)
        do_ref[...] = dob
        dl_ref[...] = _group_sum(dob.astype(F32) * ov, gs_ref[...])
        dga_ref[...] += _fold8(dga)
        gate_b = bcu_ref[:, 0:CW].astype(F32)
        cv = cv_ref[...]
        dconv, dgc = group_norm_bwd(dm[:, AW:D], gate_b * cv, gc_ref[...], gs_ref[...])
        dgc_ref[...] += _fold8(dgc)
        dcv_ref[...] = dconv * gate_b
        db_ref[...] = (dconv * cv).astype(BF16)

    return pl.pallas_call(
        body, name="mix_bwd", grid=(s // tm,),
        in_specs=[_rows(tm, D), _resident((D, D)), _rows(tm, AW), _rows(tm, CW), _rows(tm, 3 * CW),
                  _full((1, AW)), _full((1, CW)), _full((GS, GS))],
        out_specs=[_rows(tm, AW), _rows(tm, AW), _rows(tm, CW), _rows(tm, CW),
                   _full((SUBLANES, AW)), _full((SUBLANES, CW))],
        out_shape=[jax.ShapeDtypeStruct((s, AW), BF16), jax.ShapeDtypeStruct((s, AW), F32),
                   jax.ShapeDtypeStruct((s, CW), F32), jax.ShapeDtypeStruct((s, CW), BF16),
                   jax.ShapeDtypeStruct((SUBLANES, AW), F32), jax.ShapeDtypeStruct((SUBLANES, CW), F32)],
        compiler_params=_cparams(48, ("arbitrary",)),
    )(dy, w_out, o, cv, bcu, ga, gc, gsum)


def _conv_bwd(dcv, db, bcu, cw8, *, tm):
    s = dcv.shape[0]
    nt = s // tm

    def body(dcv_ref, nxt_ref, db_ref, bcu_ref, halo_ref, cw_ref, dbcu_ref, dw_ref):
        i = pl.program_id(0)

        @pl.when(i == 0)
        def _():
            dw_ref[...] = jnp.zeros_like(dw_ref)

        z, z1, z2 = _conv_taps(bcu_ref, halo_ref, i == 0, tm)
        d = dcv_ref[...]
        dw_ref[0] += _fold8(d * z2)
        dw_ref[1] += _fold8(d * z1)
        dw_ref[2] += _fold8(d * z)
        nx = jnp.where(i == nt - 1, 0.0, nxt_ref[...])
        row = lax.broadcasted_iota(jnp.int32, (tm, CW), 0)
        d1 = jnp.where(row == tm - 1, nx[0:1, :], pltpu.roll(d, tm - 1, axis=0))
        d2 = jnp.where(row == tm - 2, nx[0:1, :], jnp.where(row == tm - 1, nx[1:2, :], pltpu.roll(d, tm - 2, axis=0)))
        dz = cw_ref[2:3, :] * d + cw_ref[1:2, :] * d1 + cw_ref[0:1, :] * d2
        dbcu_ref[:, 0:CW] = db_ref[...]
        dbcu_ref[:, CW:2 * CW] = (dz * bcu_ref[:, 2 * CW:3 * CW].astype(F32)).astype(BF16)
        dbcu_ref[:, 2 * CW:3 * CW] = (dz * bcu_ref[:, CW:2 * CW].astype(F32)).astype(BF16)

    return pl.pallas_call(
        body, name="conv_bwd", grid=(nt,),
        in_specs=[_rows(tm, CW),
                  pl.BlockSpec((SUBLANES, CW), lambda i: (jnp.minimum((i + 1) * (tm // SUBLANES), s // SUBLANES - 1), 0)),
                  _rows(tm, CW), _rows(tm, 3 * CW), _halo_before(tm, 3 * CW), _full((SUBLANES, CW))],
        out_specs=[_rows(tm, 3 * CW), _full((3, SUBLANES, CW))],
        out_shape=[jax.ShapeDtypeStruct((s, 3 * CW), BF16), jax.ShapeDtypeStruct((3, SUBLANES, CW), F32)],
        compiler_params=_cparams(48, ("arbitrary",)),
    )(dcv, dcv, db, bcu, bcu, cw8)


def _attn_bwd(qp, kp, v, do, lse, dl, mk, *, t):
    s = qp.shape[0]
    nq = s // t

    def body(q_ref, k_ref, v_ref, do_ref, lse_ref, dl_ref, mk_ref, dq_ref, dk_ref, dv_ref, dkx_ref, dq_acc):
        ki = pl.program_id(1)

        @pl.when(ki == 0)
        def _():
            dq_acc[...] = jnp.zeros_like(dq_acc)

        row = lax.broadcasted_iota(jnp.int32, (t, t), 0)
        col = lax.broadcasted_iota(jnp.int32, (t, t), 1)
        lane = lax.broadcasted_iota(jnp.int32, (t, 128), 1)

        def head_step(hh, qi, carry, masked):
            dk, dv, cs = carry
            off = pl.multiple_of(qi * t, t)
            rows = pl.ds(off, t)
            kh = k_ref[:, HP * hh:HP * (hh + 1)]
            q = q_ref[rows, HP * hh:HP * (hh + 1)]
            m_col = mk_ref[0, rows, DH * hh:DH * hh + 1]
            scale = jnp.exp(m_col - lse_ref[rows, DH * hh:DH * hh + 1])
            do2 = do_ref[rows, :]
            dom = jnp.where(lane < DH, do2 if hh == 0 else pltpu.roll(do2, DH, axis=1), jnp.zeros((), BF16))
            sc = lax.dot_general(q, kh, NT, preferred_element_type=F32) - m_col
            if masked:
                sc = jnp.where(col <= row, sc, -1e30)
            pt = jnp.exp(sc).astype(BF16)
            dp = lax.dot_general(dom, v_ref[:, HP * hh:HP * (hh + 1)], NT, preferred_element_type=F32)
            ds32 = (pt.astype(F32) * scale) * (dp - dl_ref[rows, DH * hh:DH * hh + 1])
            ds = ds32.astype(BF16)
            cs = cs + _fold8(ds32)
            dv = dv + jnp.dot((dom.astype(F32) * scale).astype(BF16).T, pt, preferred_element_type=F32)
            dk = dk + jnp.dot(q.T, ds, preferred_element_type=F32)
            dq_acc[rows, HP * hh:HP * (hh + 1)] += jnp.dot(ds, kh, preferred_element_type=F32)
            return dk, dv, cs

        def step(qi, carry, masked):
            return tuple(head_step(hh, qi, carry[hh], masked) for hh in range(2))

        zero = (jnp.zeros((HP, t), F32), jnp.zeros((128, t), F32), jnp.zeros((SUBLANES, t), F32))
        carry = step(ki, (zero, zero), True)
        (dk0, dv0, cs0), (dk1, dv1, cs1) = lax.fori_loop(ki + 1, nq, functools.partial(step, masked=False), carry)
        def two_heads(a0, a1):
            return jnp.where(lane < DH, a0, pltpu.roll(a1, DH, axis=1))

        def rows_to_lanes(a0, a1):
            return jnp.concatenate([a0, a1], axis=0).T

        dk_ref[...] = rows_to_lanes(dk0[0:DH], dk1[0:DH]).astype(BF16)
        dv_ref[...] = rows_to_lanes(dv0[0:DH], dv1[0:DH]).astype(BF16)
        total = lambda cs: jnp.broadcast_to(jnp.sum(cs, axis=0, keepdims=True), (DH, t))
        dkx_ref[...] = rows_to_lanes(total(cs0), total(cs1))

        @pl.when(ki == nq - 1)
        def _():
            for c in range(s // t):
                rows = slice(c * t, (c + 1) * t)
                dq_ref[rows, :] = two_heads(dq_acc[rows, 0:HP], dq_acc[rows, HP:2 * HP]).astype(BF16)

    return pl.pallas_call(
        body, name="attn_bwd", grid=(H // 2, nq),
        in_specs=[pl.BlockSpec((s, 2 * HP), lambda p, i: (0, p)),
                  pl.BlockSpec((t, 2 * HP), lambda p, i: (i, p)),
                  pl.BlockSpec((t, 2 * HP), lambda p, i: (i, p)),
                  pl.BlockSpec((s, 128), lambda p, i: (0, p)),
                  pl.BlockSpec((s, 128), lambda p, i: (0, p)),
                  pl.BlockSpec((s, 128), lambda p, i: (0, p)),
                  pl.BlockSpec((1, s, 128), lambda p, i: (i, 0, p))],
        out_specs=[pl.BlockSpec((s, 128), lambda p, i: (0, p)),
                   pl.BlockSpec((t, 128), lambda p, i: (i, p)),
                   pl.BlockSpec((t, 128), lambda p, i: (i, p)),
                   pl.BlockSpec((t, 128), lambda p, i: (i, p))],
        out_shape=[jax.ShapeDtypeStruct((s, AW), BF16), jax.ShapeDtypeStruct((s, AW), BF16),
                   jax.ShapeDtypeStruct((s, AW), BF16), jax.ShapeDtypeStruct((s, AW), F32)],
        scratch_shapes=[pltpu.VMEM((s, 2 * HP), F32)],
        compiler_params=_cparams(56, ("arbitrary", "arbitrary")),
    )(qp, kp, v, do, lse, dl, mk)


def _forget_bwd(dkx, z, sel, *, tm):
    s = dkx.shape[0]
    nt = s // tm

    def body(dk_ref, z_ref, sel_ref, dfl_ref, dbf_ref, carry):
        @pl.when(pl.program_id(0) == 0)
        def _():
            carry[...] = jnp.zeros_like(carry)
            dbf_ref[...] = jnp.zeros_like(dbf_ref)

        dc = _split_dot(dk_ref[...], sel_ref[...])
        row = lax.broadcasted_iota(jnp.int32, (tm, tm), 0)
        col = lax.broadcasted_iota(jnp.int32, (tm, tm), 1)
        tri = (col >= row).astype(BF16)
        dlogf = _exact_dot01(tri, dc) + carry[0:1, :]
        carry[...] = jnp.broadcast_to(dlogf[0:1, :], carry.shape)
        dz = dlogf * (1.0 - jax.nn.sigmoid(z_ref[...]))
        dfl_ref[:, 0:128] = dz.astype(BF16)
        dfl_ref[:, 128:GW_TILE] = jnp.zeros((tm, GW_TILE - 128), BF16)
        dbf_ref[...] += _fold8(dz)

    rev = lambda i: (nt - 1 - i, 0)
    return pl.pallas_call(
        body, name="forget_bwd", grid=(nt,),
        in_specs=[pl.BlockSpec((tm, AW), rev), pl.BlockSpec((tm, 128), rev), _full((AW, 128))],
        out_specs=[pl.BlockSpec((tm, GW_TILE), rev), _full((SUBLANES, 128))],
        out_shape=[jax.ShapeDtypeStruct((s, GW_TILE), BF16), jax.ShapeDtypeStruct((SUBLANES, 128), F32)],
        scratch_shapes=[pltpu.VMEM((SUBLANES, 128), F32)],
        compiler_params=_cparams(48, ("arbitrary",)),
    )(dkx, z, sel)


def _in_proj_bwd(pieces, wp, x, g1, dx2, *, tm):
    s = x.shape[0]

    def body(q_ref, k_ref, v_ref, bcu_ref, f_ref, w_ref, x_ref, g_ref, dx2_ref, dx_ref, dg_ref):
        @pl.when(pl.program_id(0) == 0)
        def _():
            dg_ref[...] = jnp.zeros_like(dg_ref)

        dh = None
        for ref, (lo, hi) in zip((q_ref, k_ref, v_ref, bcu_ref, f_ref), PIECES):
            part = lax.dot_general(ref[...], w_ref[:, lo:hi], NT, preferred_element_type=F32)
            dh = part if dh is None else dh + part
        _, n, r = _rms_fwd(x_ref[...], g_ref[...])
        dxn, dg = _rms_bwd(dh, n, r, g_ref[...])
        dx_ref[...] = dx2_ref[...] + dxn
        dg_ref[...] += _fold8(dg)

    return pl.pallas_call(
        body, name="in_proj_bwd", grid=(s // tm,),
        in_specs=[_rows(tm, hi - lo) for lo, hi in PIECES] + [_resident((D, WP)), _rows(tm, D), _full((1, D)), _rows(tm, D)],
        out_specs=[_rows(tm, D), _full((SUBLANES, D))],
        out_shape=[jax.ShapeDtypeStruct((s, D), F32), jax.ShapeDtypeStruct((SUBLANES, D), F32)],
        compiler_params=_cparams(56, ("arbitrary",)),
    )(*pieces, wp, x, g1, dx2)


def _position():
    return lax.axis_index("x"), lax.axis_index("y"), lax.axis_index("c")


ANY = pl.BlockSpec(memory_space=pl.ANY)


def _all_gather(shards):
    n = len(shards)

    def body(*refs):
        x_refs, out_refs = refs[:n], refs[n:2 * n]
        send_sems, recv_sems, local_sems = refs[2 * n:]
        x, y, c = _position()
        me, sibling = (x, y, c), (x, y, 1 - c)
        chips = [(1 - x, y), (x, 1 - y), (1 - x, 1 - y)]

        def copy(a, k, block, to, own=False):
            slot = out_refs[a].at[4 * block[0] + 2 * block[1] + block[2]]
            return pltpu.make_async_remote_copy(
                src_ref=x_refs[a] if own else slot, dst_ref=slot,
                send_sem=send_sems.at[7 * a + k], recv_sem=recv_sems.at[7 * a + k], device_id=to, device_id_type=MESH_ID)

        mine = [pltpu.make_async_copy(x_refs[a], out_refs[a].at[4 * x + 2 * y + c], local_sems.at[a]) for a in range(n)]
        for cp in mine:
            cp.start()
        first = []
        for a in range(n):
            first.append(copy(a, 0, me, sibling, own=True))
            first += [copy(a, 1 + j, me, (*chip, c), own=True) for j, chip in enumerate(chips)]
        for cp in first:
            cp.start()
        passed = []
        for j, chip in enumerate(chips):
            for a in range(n):
                copy(a, 1 + j, (*chip, c), me).wait_recv()
                fwd = copy(a, 4 + j, (*chip, c), sibling)
                fwd.start()
                passed.append(fwd)
        for a in range(n):
            copy(a, 0, sibling, me).wait_recv()
            for j, chip in enumerate(chips):
                copy(a, 4 + j, (*chip, 1 - c), me).wait_recv()
        for cp in first + passed:
            cp.wait_send()
        for cp in mine:
            cp.wait()

    return pl.pallas_call(
        body, name="all_gather_weights",
        out_shape=[jax.ShapeDtypeStruct((NDEV,) + sh.shape, sh.dtype) for sh in shards],
        in_specs=[ANY] * n, out_specs=[ANY] * n,
        scratch_shapes=[pltpu.SemaphoreType.DMA((7 * n,)), pltpu.SemaphoreType.DMA((7 * n,)), pltpu.SemaphoreType.DMA((n,))],
    )(*shards)


def _pair_exchange(grads):
    n = len(grads)

    def body(*refs):
        g_refs, out_refs = refs[:n], refs[n:2 * n]
        send_sems, recv_sems = refs[2 * n:]
        x, y, c = _position()
        copies = [pltpu.make_async_remote_copy(
            src_ref=g_refs[a].at[:, pl.ds(1 - c, 1)], dst_ref=out_refs[a], send_sem=send_sems.at[a],
            recv_sem=recv_sems.at[a], device_id=(x, y, 1 - c), device_id_type=MESH_ID) for a in range(n)]
        for cp in copies:
            cp.start()
        for cp in copies:
            cp.wait()

    return pl.pallas_call(
        body, name="grad_pair_exchange",
        out_shape=[jax.ShapeDtypeStruct((4, 1) + g.shape[2:], g.dtype) for g in grads],
        in_specs=[ANY] * n, out_specs=[ANY] * n,
        scratch_shapes=[pltpu.SemaphoreType.DMA((n,)), pltpu.SemaphoreType.DMA((n,))],
    )(*grads)


def _pair_sum(g, got, idx, *, tr, name):
    r, c = g.shape[2:]

    def body(idx_ref, g_ref, got_ref, pb_ref, own_ref):
        p = g_ref[0, 0].astype(F32) + got_ref[0, 0].astype(F32)
        pb_ref[0] = p.astype(BF16)

        @pl.when(pl.program_id(1) == idx_ref[1])
        def _():
            own_ref[...] = p

    return pl.pallas_call(
        body, name=name,
        grid_spec=pltpu.PrefetchScalarGridSpec(
            num_scalar_prefetch=1, grid=(r // tr, 4),
            in_specs=[pl.BlockSpec((1, 1, tr, c), lambda i, j, idx: (j, idx[0], i, 0)),
                      pl.BlockSpec((1, 1, tr, c), lambda i, j, idx: (j, 0, i, 0))],
            out_specs=[pl.BlockSpec((1, tr, c), lambda i, j, idx: (j, i, 0)),
                       pl.BlockSpec((tr, c), lambda i, j, idx: (i, 0))]),
        out_shape=[jax.ShapeDtypeStruct((4, r, c), BF16), jax.ShapeDtypeStruct((r, c), F32)],
        compiler_params=_cparams(32, ("arbitrary", "arbitrary")),
    )(idx, g, got)


HBM = pl.BlockSpec(memory_space=pltpu.HBM)
SEM = pl.BlockSpec(memory_space=pltpu.SEMAPHORE)
DATAFLOW = pltpu.SideEffectType.DATAFLOW_SIDE_EFFECTING


PEERS = {"gather": NDEV - 1, "scatter": NDEV - 1, "chips": 3}


def _exchange_copies(src_refs, land_refs, send_sems, recv_sems, mode):
    x, y, c = _position()
    me, my_chip = 4 * x + 2 * y + c, 2 * x + y
    npeers = PEERS[mode]
    copies = []
    for a, (s_ref, l_ref) in enumerate(zip(src_refs, land_refs)):
        for k in range(npeers):
            if mode == "chips":
                px, py, pc = x ^ ((k + 1) >> 1), y ^ ((k + 1) & 1), c
                src, dst = s_ref.at[2 * px + py], l_ref.at[my_chip]
            else:
                px, py, pc = x ^ ((k + 1) >> 2), y ^ (((k + 1) >> 1) & 1), c ^ ((k + 1) & 1)
                src, dst = (s_ref.at[4 * px + 2 * py + pc] if mode == "scatter" else s_ref), l_ref.at[me]
            copies.append(pltpu.make_async_remote_copy(
                src_ref=src, dst_ref=dst, send_sem=send_sems.at[npeers * a + k], recv_sem=recv_sems.at[npeers * a + k],
                device_id=(px, py, pc), device_id_type=MESH_ID))
    return copies


def _exchange_start(srcs, lands, after, *, mode, name):
    n = len(srcs)
    nsem = PEERS[mode] * n

    def body(*refs):
        token = refs[-1]
        for cp in _exchange_copies(refs[:n], refs[n:2 * n], refs[2 * n + 1], refs[2 * n + 2], mode):
            cp.start()
        token[...] = jnp.zeros_like(token)

    arrays = list(srcs) + list(lands)
    outs = pl.pallas_call(
        body, name=name,
        out_shape=(pltpu.SemaphoreType.DMA((nsem,)), pltpu.SemaphoreType.DMA((nsem,)),
                   *[pltpu.HBM(a.shape, a.dtype) for a in arrays], jax.ShapeDtypeStruct((SUBLANES, LANES), F32)),
        in_specs=[HBM] * (2 * n) + [ANY],
        out_specs=(SEM, SEM, *[HBM] * (2 * n), pl.BlockSpec(memory_space=pltpu.VMEM)),
        input_output_aliases={i: 2 + i for i in range(2 * n)},
        compiler_params=pltpu.CompilerParams(has_side_effects=DATAFLOW),
    )(*[pltpu.with_memory_space_constraint(a, pltpu.HBM) for a in arrays], after)
    return outs[0], outs[1], outs[2:2 + n], outs[2 + n:2 + 2 * n], outs[-1]


def _exchange_wait(send_sems, recv_sems, srcs, lands, after, *, mode, name):
    n = len(srcs)

    def body(*refs):
        for cp in _exchange_copies(refs[:n], refs[n:2 * n], refs[2 * n], refs[2 * n + 1], mode):
            cp.wait_send()
            cp.wait_recv()

    arrays = list(srcs) + list(lands)
    outs = pl.pallas_call(
        body, name=name,
        out_shape=tuple(pltpu.HBM(a.shape, a.dtype) for a in arrays),
        in_specs=[HBM] * (2 * n) + [SEM, SEM, ANY],
        out_specs=tuple([HBM] * (2 * n)),
        input_output_aliases={i: i for i in range(2 * n)},
        compiler_params=pltpu.CompilerParams(has_side_effects=DATAFLOW),
    )(*arrays, send_sems, recv_sems, after)
    return outs[n:]


def _own_slot(value, me):
    return lax.dynamic_update_index_in_dim(lax.empty((NDEV,) + value.shape, value.dtype), value, me, 0)


def _small_all_reduce(parts):
    def body(gmp_ref, gmo_ref, gfp_ref, gfo_ref, ga_ref, gc_ref, dw_ref, bf_ref, loss_ref,
             out_ref, buf, send_sems, recv_sems):
        x, y, c = _position()
        me = 4 * x + 2 * y + c

        def colsum(v):
            return jnp.sum(v, axis=0, keepdims=True)

        loss = jnp.sum(colsum(loss_ref[...]), axis=1, keepdims=True) * (0.5 / D)
        rows = [colsum(gmp_ref[...]), colsum(gmo_ref[...]), colsum(gfp_ref[...]), colsum(gfo_ref[...]),
                jnp.concatenate([colsum(ga_ref[...]), colsum(gc_ref[...])], axis=1),
                jnp.concatenate([colsum(dw_ref[0]), colsum(dw_ref[1])], axis=1),
                jnp.concatenate([colsum(dw_ref[2]), colsum(bf_ref[...]), jnp.broadcast_to(loss, (1, 128)),
                                 jnp.zeros((1, 256), F32)], axis=1),
                jnp.zeros((1, D), F32)]
        buf[me] = jnp.concatenate(rows, axis=0)
        copies = []
        for mm in range(1, NDEV):
            peer = (x ^ (mm >> 2), y ^ ((mm >> 1) & 1), c ^ (mm & 1))
            copies.append(pltpu.make_async_remote_copy(
                src_ref=buf.at[me], dst_ref=buf.at[me], send_sem=send_sems.at[mm - 1], recv_sem=recv_sems.at[mm - 1],
                device_id=peer, device_id_type=MESH_ID))
        for cp in copies:
            cp.start()
        for cp in copies:
            cp.wait_recv()
        for cp in copies:
            cp.wait_send()
        acc = buf[0]
        for d in range(1, NDEV):
            acc = acc + buf[d]
        out_ref[...] = acc

    vm = pl.BlockSpec(memory_space=pltpu.VMEM)
    return pl.pallas_call(
        body, name="small_all_reduce",
        out_shape=jax.ShapeDtypeStruct((SUBLANES, D), F32),
        in_specs=[vm] * len(parts), out_specs=vm,
        scratch_shapes=[pltpu.VMEM((NDEV, SUBLANES, D), F32), pltpu.SemaphoreType.DMA((7,)), pltpu.SemaphoreType.DMA((7,))],
    )(*parts)


def _adam_update(w, g, m, v):
    nm = ADAM_B1 * m + (1.0 - ADAM_B1) * g
    nv = ADAM_B2 * v + (1.0 - ADAM_B2) * (g * g)
    m_hat = nm / (1.0 - ADAM_B1 ** ADAM_STEP)
    v_hat = nv / (1.0 - ADAM_B2 ** ADAM_STEP)
    return -ADAM_LR * (m_hat / (jnp.sqrt(v_hat) + ADAM_EPS) + ADAM_WD * w), nm, nv


SMALL_SLOTS = {"g_mix_pre": (0, 0, D), "g_mix_post": (1, 0, D), "g_ffn_pre": (2, 0, D), "g_ffn_post": (3, 0, D),
               "g_attn_out": (4, 0, AW), "g_conv_out": (4, AW, CW), "b_forget": (6, CW, H)}


def _small_adamw(small, conv_grad, params):
    names = list(params)
    n = len(names)

    def body(*refs):
        small_ref, cg_ref = refs[0], refs[1]
        ins, outs = refs[2:2 + 3 * n], refs[2 + 3 * n:]
        for i, name in enumerate(names):
            w_ref, m_ref, v_ref = ins[3 * i:3 * i + 3]
            g_ref, d_ref, nm_ref, nv_ref = outs[4 * i:4 * i + 4]
            if name == "conv_w":
                g = cg_ref[...]
            else:
                r, c0, width = SMALL_SLOTS[name]
                g = small_ref[r:r + 1, c0:c0 + width]
            g_ref[...] = g
            d_ref[...], nm_ref[...], nv_ref[...] = _adam_update(w_ref[...], g, m_ref[...], v_ref[...])

    vm = pl.BlockSpec(memory_space=pltpu.VMEM)
    flat = [a for name in names for a in params[name]]
    outs = pl.pallas_call(
        body, name="adamw_small",
        in_specs=[vm] * (2 + 3 * n), out_specs=[vm] * (4 * n),
        out_shape=[jax.ShapeDtypeStruct(params[name][0].shape, F32) for name in names for _ in range(4)],
    )(small, conv_grad, *flat)
    return {name: outs[4 * i:4 * i + 4] for i, name in enumerate(names)}


def _chip_sum_adamw(got, own, idx, w, m, v, *, tr, name):
    rows, cols = w.shape
    gcols = own.shape[1]

    def body(idx_ref, got_ref, own_ref, w_ref, m_ref, v_ref, g_ref, d_ref, nm_ref, nv_ref):
        g = jnp.zeros((tr, gcols), F32)
        for j in range(4):
            g = g + jnp.where(idx_ref[1] == j, own_ref[...], got_ref[j].astype(F32))
        g = g[:, :cols]
        g_ref[...] = g
        d_ref[...], nm_ref[...], nv_ref[...] = _adam_update(w_ref[...], g, m_ref[...], v_ref[...])

    spec = pl.BlockSpec((tr, cols), lambda i, idx: (i, 0))
    gspec = pl.BlockSpec((tr, gcols), lambda i, idx: (i, 0))
    return pl.pallas_call(
        body, name=name,
        grid_spec=pltpu.PrefetchScalarGridSpec(
            num_scalar_prefetch=1, grid=(rows // tr,),
            in_specs=[pl.BlockSpec((4, tr, gcols), lambda i, idx: (0, i, 0)), gspec, spec, spec, spec],
            out_specs=[spec] * 4),
        out_shape=[jax.ShapeDtypeStruct((rows, cols), F32)] * 4,
        compiler_params=_cparams(32, ("arbitrary",)),
    )(idx, got, own, w, m, v)


def _device_sum_adamw(land, w, m, v, *, tr, name):
    rows, cols = w.shape

    def body(land_ref, w_ref, m_ref, v_ref, g_ref, d_ref, nm_ref, nv_ref):
        g = land_ref[0].astype(F32)
        for dev in range(1, NDEV):
            g = g + land_ref[dev].astype(F32)
        g_ref[...] = g
        d_ref[...], nm_ref[...], nv_ref[...] = _adam_update(w_ref[...], g, m_ref[...], v_ref[...])

    spec = pl.BlockSpec((tr, cols), lambda i: (i, 0))
    return pl.pallas_call(
        body, name=name, grid=(rows // tr,),
        in_specs=[pl.BlockSpec((NDEV, tr, cols), lambda i: (0, i, 0)), spec, spec, spec],
        out_specs=[spec] * 4,
        out_shape=[jax.ShapeDtypeStruct((rows, cols), F32)] * 4,
        compiler_params=_cparams(32, ("arbitrary",)),
    )(land, w, m, v)


def _placement_constants():
    j = jnp.arange(128)[:, None]
    lane = jnp.arange(1024)[None, :]
    head, sub = lane // HP, lane % HP
    piece, jh = j // H, j % H
    valid = (j < 3 * H) & (jh == head)
    pq = jnp.where(valid & (sub == DH + piece), 1.0, 0.0).astype(BF16)
    pk = jnp.where(valid & (sub == DH + 3 + piece), -1.0, 0.0).astype(BF16)
    oq = jnp.where((sub >= DH + 3) & (sub < DH + 6), 1.0, 0.0).astype(F32)
    ok = jnp.where((sub >= DH) & (sub < DH + 3), 1.0, 0.0).astype(F32)
    r = jnp.arange(AW)[:, None]
    cc = jnp.arange(128)[None, :]
    sel = jnp.where((r % DH == 3) & (r // DH == cc), -1.0, 0.0).astype(BF16)
    gi = jnp.arange(GS)
    gsum = (gi[:, None] // DH == gi[None, :] // DH).astype(BF16)
    return pq, pk, oq, ok, sel, gsum


def _local_step(xs, tgt, wp, late_weights, cw8, bfp, g_attn_out, g_conv_out,
                g_mix_pre, g_mix_post, g_ffn_pre, g_ffn_post, early_grads=None, last_grad=None):
    pq, pk, oq, ok, sel, gsum = _placement_constants()
    h1t, qp, kp, vv, bcu, zf = _in_proj(xs, g_mix_pre, wp, bfp, pq, pk, oq, ok, tm=512)
    o, lse, mk = _attn_fwd(qp, kp, vv, t=512)
    w_out_f, wgu, wd = late_weights(lse)
    merged, y, x2, cv, h2 = _mix_out(o, bcu, cw8, g_attn_out, g_conv_out, gsum, w_out_f, xs, g_mix_post, g_ffn_pre, tm=512)
    gate, up, act, dx3, dff, loss_p, dg_ffn_post = _ffn_fwd_loss(h2, wgu, wd, x2, tgt, g_ffn_post, tm=512)

    dgu, dx2, dy, dg_ffn_pre, dg_mix_post = _ffn_bwd(dff, wd, gate, up, wgu, x2, g_ffn_pre, dx3, y, g_mix_post, tm=256)
    dw_down = _grad_matmul_blocks(act, dff, ts=4096, name="grad_w_down")
    dw_gu = _grad_matmul_blocks(dgu.reshape(NDEV, -1, FB), h2, ts=4096, name="grad_w_gate_up")
    dw_out = _grad_matmul(merged, dy, ta=1024, tb=1024, ts=2048, name="grad_w_out")
    token = early_grads(dw_out, dw_gu, dw_down) if early_grads is not None else None
    ga = g_attn_out if token is None else g_attn_out + token[0:1, 0:1]
    do, dl, dcv, db, dg_attn, dg_conv = _mix_bwd(dy, w_out_f, o, cv, bcu, ga, g_conv_out, gsum, tm=512)
    dbcu, dtaps = _conv_bwd(dcv, db, bcu, cw8, tm=512)
    dqp, dkp, dv, dkx = _attn_bwd(qp, kp, vv, do, lse, dl, mk, t=512)
    dfl, dbf = _forget_bwd(dkx, zf, sel, tm=512)
    pieces = (dqp, dkp, dv, dbcu, dfl)
    dwp = _grad_w_in(h1t, pieces)
    token = last_grad(dwp) if last_grad is not None else None
    g1 = g_mix_pre if token is None else g_mix_pre + token[0:1, 0:1]
    grad_x, dg_mix_pre = _in_proj_bwd(pieces, wp, xs, g1, dx2, tm=512)
    return (grad_x, dwp, dw_out, dw_gu, dw_down, dg_mix_pre, dg_mix_post, dg_ffn_pre, dg_ffn_post, dg_attn, dg_conv,
            dtaps, dbf, loss_p)


BIG_TILES = {"w_in": 256, "w_out": 128, "w_gate_up": 176, "w_down": 176}


def kernel(x, w_in, b_forget, conv_w, g_attn_out, g_conv_out, w_out, g_mix_pre, g_mix_post, w_gate_up, w_down, g_ffn_pre, g_ffn_post, loss_target, m_w_in, m_b_forget, m_conv_w, m_g_attn_out, m_g_conv_out, m_w_out, m_g_mix_pre, m_g_mix_post, m_w_gate_up, m_w_down, m_g_ffn_pre, m_g_ffn_post, v_w_in, v_b_forget, v_conv_w, v_g_attn_out, v_g_conv_out, v_w_out, v_g_mix_pre, v_g_mix_post, v_w_gate_up, v_w_down, v_g_ffn_pre, v_g_ffn_post):
    xc, yc, cc = _position()
    my_chip = 2 * xc + yc
    me = 2 * my_chip + cc
    idx = jnp.stack([cc, my_chip]).astype(jnp.int32)
    tables = _in_layout_tables()

    w_in_b = w_in[0].astype(BF16)
    g_in, g_last, g_taps = _all_gather([w_in_b[:, :IN_MAIN], w_in_b[:, IN_MAIN].reshape(SUBLANES, LANES), conv_w[0]])
    last_cols = jnp.pad(g_last.reshape(NDEV, D).T.astype(F32), ((0, 0), (0, LANES - NDEV)))
    wp = _assemble_w_in(g_in, last_cols, tables, tr=256)
    cw8 = jnp.pad(g_taps.transpose(1, 0, 2).reshape(3, CW), ((0, SUBLANES - 3), (0, 0)))

    late = [w_out[0].astype(BF16), w_gate_up[0].astype(BF16), w_down[0].astype(BF16)]
    ssem, rsem, late_thru, land_thru, token = _exchange_start(
        late, [_own_slot(s, me) for s in late], g_in, mode="gather", name="gather_late_start")
    bfp = jnp.pad(b_forget, ((0, 0), (0, 128 - H))) + token[0:1, :]

    def late_weights(after):
        l_out, l_gu, l_down = _exchange_wait(ssem, rsem, late_thru, land_thru, after, mode="gather", name="gather_late_wait")
        return l_out.reshape(D, D), l_gu.reshape(2, 4, D, FB), l_down.reshape(4, FB, D)

    early = {}

    def early_grads(dw_out, dw_gu, dw_down):
        srcs = [dw_out.reshape(NDEV, D // NDEV, D), dw_gu, dw_down.reshape(NDEV, DFF // NDEV, D)]
        lands = [_own_slot(lax.dynamic_index_in_dim(s, me, 0, keepdims=False), me) for s in srcs]
        early["handles"] = _exchange_start(srcs, lands, dw_out, mode="scatter", name="scatter_early_start")
        return early["handles"][4]

    last = {}

    def last_grad(dwp):
        g_w_in = _disassemble_w_in(dwp, tables, tr=256).reshape(4, 2, D, IN_PAD)
        (from_sibling,) = _pair_exchange([g_w_in])
        pair_b, last["own"] = _pair_sum(g_w_in, from_sibling, idx, tr=BIG_TILES["w_in"], name="grad_pair_sum_w_in")
        land = lax.dynamic_update_index_in_dim(lax.empty(pair_b.shape, pair_b.dtype),
                                               lax.dynamic_index_in_dim(pair_b, my_chip, 0, keepdims=False), my_chip, 0)
        last["handles"] = _exchange_start([pair_b], [land], last["own"], mode="chips", name="chips_w_in_start")
        return last["handles"][4]

    (grad_x, dwp, dw_out, dw_gu, dw_down, dg_mix_pre, dg_mix_post, dg_ffn_pre, dg_ffn_post, dg_attn, dg_conv,
     dtaps, dbf, loss_p) = _local_step(x[0], loss_target[0], wp, late_weights, cw8, bfp, g_attn_out, g_conv_out,
                                        g_mix_pre, g_mix_post, g_ffn_pre, g_ffn_post, early_grads, last_grad)

    e_ssem, e_rsem, e_srcs, e_lands, _ = early["handles"]
    land_out, land_gu, land_down = _exchange_wait(e_ssem, e_rsem, e_srcs, e_lands, dg_mix_pre, mode="scatter",
                                                  name="scatter_early_wait")
    res = {}
    big = {"w_out": (land_out, w_out[0], m_w_out[0], v_w_out[0]),
           "w_gate_up": (land_gu, w_gate_up[0].T, m_w_gate_up[0].T, v_w_gate_up[0].T),
           "w_down": (land_down, w_down[0], m_w_down[0], v_w_down[0])}
    for name, (land, w, m, v) in big.items():
        outs = _device_sum_adamw(land, w, m, v, tr=BIG_TILES[name], name="adamw_" + name)
        res[name] = [(o.T if name == "w_gate_up" else o)[None] for o in outs]
    c_ssem, c_rsem, c_srcs, c_lands, _ = last["handles"]
    after = sum(res[n][1][0, :SUBLANES, :LANES] for n in big)
    (from_chips,) = _exchange_wait(c_ssem, c_rsem, c_srcs, c_lands, after, mode="chips", name="chips_w_in_wait")
    outs = _chip_sum_adamw(from_chips, last["own"], idx, w_in[0], m_w_in[0], v_w_in[0],
                           tr=BIG_TILES["w_in"], name="adamw_w_in")
    res["w_in"] = [o[None] for o in outs]

    small = _small_all_reduce([dg_mix_pre, dg_mix_post, dg_ffn_pre, dg_ffn_post, dg_attn, dg_conv, dtaps, dbf, loss_p])
    taps_full = jnp.concatenate([small[5:6, :CW], small[5:6, CW:], small[6:7, :CW]], axis=0)
    loss = small[6, CW + 128]
    smalls = {"b_forget": (b_forget, m_b_forget, v_b_forget), "conv_w": (conv_w[0], m_conv_w[0], v_conv_w[0]),
              "g_attn_out": (g_attn_out, m_g_attn_out, v_g_attn_out), "g_conv_out": (g_conv_out, m_g_conv_out, v_g_conv_out),
              "g_mix_pre": (g_mix_pre, m_g_mix_pre, v_g_mix_pre), "g_mix_post": (g_mix_post, m_g_mix_post, v_g_mix_post),
              "g_ffn_pre": (g_ffn_pre, m_g_ffn_pre, v_g_ffn_pre), "g_ffn_post": (g_ffn_post, m_g_ffn_post, v_g_ffn_post)}
    for name, outs in _small_adamw(small, lax.dynamic_slice(taps_full, (0, me * 64), (3, 64)), smalls).items():
        res[name] = [o[None] for o in outs] if name == "conv_w" else list(outs)

    order = ["w_in", "b_forget", "conv_w", "g_attn_out", "g_conv_out", "w_out", "g_mix_pre", "g_mix_post",
             "w_gate_up", "w_down", "g_ffn_pre", "g_ffn_post"]
    outs = [loss, grad_x[None]]
    for k in range(4):
        outs += [res[n][k] for n in order]
    return tuple(outs)
```

```python
import functools

import numpy as np

import jax
import jax.numpy as jnp
from jax import lax
from jax.experimental import pallas as pl
from jax.experimental.pallas import tpu as pltpu

F32 = jnp.float32
BF16 = jnp.bfloat16
MESH_ID = pl.DeviceIdType.MESH

D = 1024
H = 8
DH = 64
AW = 512
CW = 512
DFF = 2816
FB = DFF // 4
HP = 128
OFF_Q, OFF_K, OFF_V, OFF_BCU, OFF_F = 0, 512, 1024, 1536, 3072
WP = OFF_F + 128
PIECES = ((OFF_Q, OFF_K), (OFF_K, OFF_V), (OFF_V, OFF_BCU), (OFF_BCU, OFF_F), (OFF_F, WP))
EPS = 1e-6
NDEV = 8
LANES = 128
SUBLANES = 8
IN_COLS = 385
IN_PAD = 512
IN_MAIN = 384
WIN = 640
ADAM_LR, ADAM_B1, ADAM_B2, ADAM_EPS, ADAM_WD, ADAM_STEP = 0.001, 0.9, 0.999, 1e-08, 0.01, 10

NT = (((1,), (1,)), ((), ()))
TN = (((0,), (0,)), ((), ()))


def _cparams(vmem_mb=None, sem=None):
    kw = {}
    if vmem_mb is not None:
        kw["vmem_limit_bytes"] = vmem_mb << 20
    if sem is not None:
        kw["dimension_semantics"] = sem
    return pltpu.CompilerParams(**kw)


def _full(shape):
    return pl.BlockSpec(shape, lambda *_: (0,) * len(shape))


def _resident(shape):
    return pl.BlockSpec(shape, lambda *_: (0,) * len(shape), pipeline_mode=pl.Buffered(1))


def _rows(tm, width):
    return pl.BlockSpec((tm, width), lambda i: (i, 0))


def _fold8(v):
    r, w = v.shape
    return jnp.sum(v.reshape(r // SUBLANES, SUBLANES, w), axis=0)


def _split_dot(v, m01):
    hi = v.astype(BF16)
    lo = (v - hi.astype(F32)).astype(BF16)
    return (jnp.dot(hi, m01, preferred_element_type=F32)
            + jnp.dot(lo, m01, preferred_element_type=F32))


GS = 256


def _group_sum(v, g01):
    parts = [_split_dot(v[:, c:c + GS], g01) for c in range(0, v.shape[1], GS)]
    return parts[0] if len(parts) == 1 else jnp.concatenate(parts, axis=1)


def _exact_dot01(m01, v):
    p1 = v.astype(BF16)
    r1 = v - p1.astype(F32)
    p2 = r1.astype(BF16)
    p3 = (r1 - p2.astype(F32)).astype(BF16)
    return (jnp.dot(m01, p1, preferred_element_type=F32) + jnp.dot(m01, p2, preferred_element_type=F32)
            + jnp.dot(m01, p3, preferred_element_type=F32))


def _rms_fwd(v, g):
    r = lax.rsqrt(jnp.mean(v * v, axis=-1, keepdims=True) + EPS)
    n = v * r
    return n * g, n, r


def _rms_bwd(do, n, r, g):
    dn = do * g
    return r * (dn - n * jnp.mean(dn * n, axis=-1, keepdims=True)), do * n


def _padded_column(n):
    if n < AW:
        return OFF_Q + n, 0.125
    if n < 3 * AW:
        return n, 1.0
    if n < 3 * AW + H:
        return OFF_F + n - 3 * AW, 1.0
    return OFF_BCU + n - 3 * AW - H, 1.0


def _in_layout_tables():
    dest = -np.ones((IN_PAD, LANES), np.int32)
    dest_f = -np.ones((IN_PAD, LANES), np.int32)
    scale = np.zeros((IN_PAD, LANES), np.float32)
    starts = []
    for k in range(NDEV):
        cols = [_padded_column(IN_COLS * k + j) for j in range(IN_COLS)]
        main = [c for c, _ in cols if c < OFF_F]
        ws = min((min(main) // LANES) * LANES, OFF_F - WIN)
        assert ws <= min(main) and max(main) < ws + WIN
        starts.append(ws)
        for j, (c, sc) in enumerate(cols):
            scale[j, k] = sc
            if c < OFF_F:
                dest[j, k] = c - ws
            else:
                dest_f[j, k] = c - OFF_F
    f_shards = tuple(k for k in range(NDEV) if (dest_f[:, k] >= 0).any())
    return tuple(starts), f_shards, jnp.asarray(dest), jnp.asarray(dest_f), jnp.asarray(scale)


def _perm(dest_ref, scale_ref, k, width, rows=IN_PAD):
    lane = lax.broadcasted_iota(jnp.int32, (rows, width), 1)
    return jnp.where(dest_ref[0:rows, k:k + 1] == lane, scale_ref[0:rows, k:k + 1], 0.0).astype(BF16)


def _assemble_w_in(blocks, last_cols, tables, *, tr):
    starts, f_shards, dest, dest_f, scale = tables
    last = [_padded_column(IN_COLS * k + IN_MAIN) for k in range(NDEV)]
    f_main = [any(_padded_column(IN_COLS * k + j)[0] >= OFF_F for j in range(IN_MAIN)) for k in range(NDEV)]
    assert IN_COLS == IN_MAIN + 1

    def body(b_ref, c_ref, dest_ref, destf_ref, scale_ref, o_ref):
        o_ref[...] = jnp.zeros_like(o_ref)
        lane = lax.broadcasted_iota(jnp.int32, (tr, LANES), 1)
        for k in range(NDEV):
            b = b_ref[k]
            ws = starts[k]
            part = jnp.dot(b, _perm(dest_ref, scale_ref, k, WIN, IN_MAIN), preferred_element_type=F32)
            o_ref[:, ws:ws + WIN] += part.astype(BF16)
            if f_main[k]:
                part = jnp.dot(b, _perm(destf_ref, scale_ref, k, 128, IN_MAIN), preferred_element_type=F32)
                o_ref[:, OFF_F:WP] += part.astype(BF16)
            col, sc = last[k]
            tile = (col // LANES) * LANES
            o_ref[:, tile:tile + LANES] += jnp.where(lane == col - tile, c_ref[:, k:k + 1] * sc, 0.0).astype(BF16)

    tab = _full((IN_PAD, LANES))
    return pl.pallas_call(
        body, name="assemble_w_in", grid=(D // tr,),
        in_specs=[pl.BlockSpec((NDEV, tr, IN_MAIN), lambda i: (0, i, 0)), _rows(tr, LANES), tab, tab, tab],
        out_specs=_rows(tr, WP),
        out_shape=jax.ShapeDtypeStruct((D, WP), BF16),
        compiler_params=_cparams(48, ("arbitrary",)),
    )(blocks, last_cols, dest, dest_f, scale)


def _disassemble_w_in(dwp, tables, *, tr):
    starts, f_shards, dest, dest_f, scale = tables
    width = dwp.shape[1]

    def body(g_ref, dest_ref, destf_ref, scale_ref, o_ref):
        for k in range(NDEV):
            ws = starts[k]
            acc = lax.dot_general(g_ref[:, ws:ws + WIN], _perm(dest_ref, scale_ref, k, WIN), NT, preferred_element_type=F32)
            if k in f_shards:
                acc = acc + lax.dot_general(g_ref[:, OFF_F:WP], _perm(destf_ref, scale_ref, k, 128), NT,
                                            preferred_element_type=F32)
            o_ref[k] = acc.astype(BF16)

    tab = _full((IN_PAD, LANES))
    return pl.pallas_call(
        body, name="disassemble_w_in", grid=(D // tr,),
        in_specs=[_rows(tr, width), tab, tab, tab],
        out_specs=pl.BlockSpec((NDEV, tr, IN_PAD), lambda i: (0, i, 0)),
        out_shape=jax.ShapeDtypeStruct((NDEV, D, IN_PAD), BF16),
        compiler_params=_cparams(48, ("arbitrary",)),
    )(dwp, dest, dest_f, scale)


def _in_proj(x, g1, wp, bfp, pq, pk, oq, ok, *, tm):
    s = x.shape[0]

    def body(x_ref, g_ref, w_ref, bf_ref, pq_ref, pk_ref, oq_ref, ok_ref,
             ht_ref, qp_ref, kp_ref, v_ref, bcu_ref, z_ref, carry):
        @pl.when(pl.program_id(0) == 0)
        def _():
            carry[...] = jnp.zeros_like(carry)

        h = _rms_fwd(x_ref[...], g_ref[...])[0].astype(BF16)
        ht_ref[...] = h.T
        z = jnp.dot(h, w_ref[:, OFF_F:WP], preferred_element_type=F32) + bf_ref[...]
        z_ref[...] = z
        lane = lax.broadcasted_iota(jnp.int32, (tm, 128), 1)
        logf = jnp.where(lane < H, jnp.minimum(z, 0.0) - jnp.log(1.0 + jnp.exp(-jnp.abs(z))), 0.0)
        row = lax.broadcasted_iota(jnp.int32, (tm, tm), 0)
        col = lax.broadcasted_iota(jnp.int32, (tm, tm), 1)
        tri = (col <= row).astype(BF16)
        c = _exact_dot01(tri, logf) + carry[0:1, :]
        carry[...] = jnp.broadcast_to(c[tm - 1:tm, :], carry.shape)
        c1 = c.astype(BF16).astype(F32)
        r1 = c - c1
        c2 = r1.astype(BF16).astype(F32)
        c3 = (r1 - c2).astype(BF16).astype(F32)
        zc = (c1 + pltpu.roll(c2, 8, axis=1) + pltpu.roll(c3, 16, axis=1)).astype(BF16)

        def pad_heads(v):
            blocks = []
            for pair in range(H // 2):
                two = v[:, 128 * pair:128 * (pair + 1)]
                blocks.append(jnp.where(lane < DH, two, 0.0))
                blocks.append(jnp.where(lane < DH, pltpu.roll(two, DH, axis=1), 0.0))
            return jnp.concatenate(blocks, axis=1)

        q = jnp.dot(h, w_ref[:, OFF_Q:OFF_K], preferred_element_type=F32)
        qp_ref[...] = (pad_heads(q) + jnp.dot(zc, pq_ref[...], preferred_element_type=F32) + oq_ref[...]).astype(BF16)
        k = jnp.dot(h, w_ref[:, OFF_K:OFF_V], preferred_element_type=F32)
        kp_ref[...] = (pad_heads(k) + jnp.dot(zc, pk_ref[...], preferred_element_type=F32) + ok_ref[...]).astype(BF16)
        v = pad_heads(jnp.dot(h, w_ref[:, OFF_V:OFF_BCU], preferred_element_type=F32))
        ones_lane = lax.broadcasted_iota(jnp.int32, (tm, H * HP), 1) % HP == DH
        v_ref[...] = jnp.where(ones_lane, 1.0, v).astype(BF16)
        bcu_ref[...] = jnp.dot(h, w_ref[:, OFF_BCU:OFF_F], preferred_element_type=F32).astype(BF16)

    return pl.pallas_call(
        body, name="in_proj", grid=(s // tm,),
        in_specs=[_rows(tm, D), _full((1, D)), _resident((D, WP)), _full((1, 128)),
                  _full((128, 1024)), _full((128, 1024)), _full((1, 1024)), _full((1, 1024))],
        out_specs=[pl.BlockSpec((D, tm), lambda i: (0, i)), _rows(tm, 1024), _rows(tm, 1024), _rows(tm, 1024),
                   _rows(tm, 3 * CW), _rows(tm, 128)],
        out_shape=[jax.ShapeDtypeStruct((D, s), BF16), jax.ShapeDtypeStruct((s, 1024), BF16),
                   jax.ShapeDtypeStruct((s, 1024), BF16), jax.ShapeDtypeStruct((s, 1024), BF16),
                   jax.ShapeDtypeStruct((s, 3 * CW), BF16), jax.ShapeDtypeStruct((s, 128), F32)],
        scratch_shapes=[pltpu.VMEM((SUBLANES, 128), F32)],
        compiler_params=_cparams(56, ("arbitrary",)),
    )(x, g1, wp, bfp, pq, pk, oq, ok)


def _attn_fwd(qp, kp, v, *, t):
    s = qp.shape[0]
    nq = s // t

    def body(q_ref, k_ref, v_ref, o_ref, lse_ref, mk_ref):
        qi = pl.program_id(1)
        row = lax.broadcasted_iota(jnp.int32, (t, t), 0)
        col = lax.broadcasted_iota(jnp.int32, (t, t), 1)
        lane = lax.broadcasted_iota(jnp.int32, (t, 128), 1)

        def head_step(hh, ki, carry, masked):
            m, acc = carry
            off = pl.multiple_of(ki * t, t)
            q = q_ref[:, HP * hh:HP * (hh + 1)]
            k = k_ref[pl.ds(off, t), HP * hh:HP * (hh + 1)]
            sc = lax.dot_general(q, k, NT, preferred_element_type=F32)
            if masked:
                sc = jnp.where(col <= row, sc, -1e30)
            mn = jnp.maximum(m, jnp.max(sc, axis=-1, keepdims=True))
            p = jnp.exp(sc - mn).astype(BF16)
            acc = jnp.exp(m - mn) * acc + jnp.dot(p, v_ref[pl.ds(off, t), HP * hh:HP * (hh + 1)],
                                                  preferred_element_type=F32)
            return mn, acc

        def step(ki, carry, masked):
            new = tuple(head_step(hh, ki, carry[hh], masked) for hh in range(2))
            mk_ref[ki] = jnp.where(lane < DH, jnp.broadcast_to(new[0][0], (t, 128)), jnp.broadcast_to(new[1][0], (t, 128)))
            return new

        init = (jnp.full((t, 1), -1e30, F32), jnp.zeros((t, 128), F32))
        carry = lax.fori_loop(0, qi, functools.partial(step, masked=False), (init, init))
        (m0, acc0), (m1, acc1) = step(qi, carry, True)
        l0, l1 = acc0[:, DH:DH + 1], acc1[:, DH:DH + 1]
        o_ref[...] = jnp.where(lane < DH, acc0 / l0, pltpu.roll(acc1 / l1, DH, axis=1))
        lse_ref[...] = jnp.where(lane < DH, jnp.broadcast_to(m0 + jnp.log(l0), (t, 128)),
                                 jnp.broadcast_to(m1 + jnp.log(l1), (t, 128)))

    return pl.pallas_call(
        body, name="attn_fwd", grid=(H // 2, nq),
        in_specs=[pl.BlockSpec((t, 2 * HP), lambda p, i: (i, p)),
                  pl.BlockSpec((s, 2 * HP), lambda p, i: (0, p)),
                  pl.BlockSpec((s, 2 * HP), lambda p, i: (0, p))],
        out_specs=[pl.BlockSpec((t, 128), lambda p, i: (i, p)), pl.BlockSpec((t, 128), lambda p, i: (i, p)),
                   pl.BlockSpec((nq, t, 128), lambda p, i: (0, i, p))],
        out_shape=[jax.ShapeDtypeStruct((s, AW), F32), jax.ShapeDtypeStruct((s, AW), F32),
                   jax.ShapeDtypeStruct((nq, s, AW), F32)],
        compiler_params=_cparams(48, ("arbitrary", "arbitrary")),
    )(qp, kp, v)


HALO = 16


def _conv_taps(bcu_ref, halo_ref, first, tm):
    z = bcu_ref[:, CW:2 * CW].astype(F32) * bcu_ref[:, 2 * CW:3 * CW].astype(F32)
    zh = jnp.where(first, 0.0, halo_ref[:, CW:2 * CW].astype(F32) * halo_ref[:, 2 * CW:3 * CW].astype(F32))
    row = lax.broadcasted_iota(jnp.int32, (tm, CW), 0)
    last, before = zh[HALO - 1:HALO, :], zh[HALO - 2:HALO - 1, :]
    z1 = jnp.where(row == 0, last, pltpu.roll(z, 1, axis=0))
    z2 = jnp.where(row == 0, before, jnp.where(row == 1, last, pltpu.roll(z, 2, axis=0)))
    return z, z1, z2


def _halo_before(tm, width):
    return pl.BlockSpec((HALO, width), lambda i: (jnp.maximum(i * (tm // HALO) - 1, 0), 0))


def _mix_out(o, bcu, cw8, ga, gc, gsum, w_out, x, g_post, g_ffn_pre, *, tm):
    s = x.shape[0]

    def body(o_ref, bcu_ref, halo_ref, cw_ref, ga_ref, gc_ref, gs_ref, w_ref, x_ref, g_ref, gf_ref,
             merged_ref, y_ref, x2_ref, cv_ref, h2_ref):
        z, z1, z2 = _conv_taps(bcu_ref, halo_ref, pl.program_id(0) == 0, tm)
        cv = cw_ref[0:1, :] * z2 + cw_ref[1:2, :] * z1 + cw_ref[2:3, :] * z
        cv_ref[...] = cv
        conv = bcu_ref[:, 0:CW].astype(F32) * cv
        ov = o_ref[...]
        ra = lax.rsqrt(_group_sum(ov * ov, gs_ref[...]) * (1.0 / DH) + EPS)
        rc = lax.rsqrt(_group_sum(conv * conv, gs_ref[...]) * (1.0 / DH) + EPS)
        merged = jnp.concatenate([ov * ra * ga_ref[...], conv * rc * gc_ref[...]], axis=1).astype(BF16)
        merged_ref[...] = merged
        y = jnp.dot(merged, w_ref[...], preferred_element_type=F32)
        y_ref[...] = y
        x2 = x_ref[...] + _rms_fwd(y, g_ref[...])[0]
        x2_ref[...] = x2
        h2_ref[...] = _rms_fwd(x2, gf_ref[...])[0].astype(BF16)

    return pl.pallas_call(
        body, name="mix_out", grid=(s // tm,),
        in_specs=[_rows(tm, AW), _rows(tm, 3 * CW), _halo_before(tm, 3 * CW), _full((SUBLANES, CW)),
                  _full((1, AW)), _full((1, CW)), _full((GS, GS)), _resident((D, D)), _rows(tm, D), _full((1, D)),
                  _full((1, D))],
        out_specs=[_rows(tm, D), _rows(tm, D), _rows(tm, D), _rows(tm, CW), _rows(tm, D)],
        out_shape=[jax.ShapeDtypeStruct((s, D), BF16), jax.ShapeDtypeStruct((s, D), F32),
                   jax.ShapeDtypeStruct((s, D), F32), jax.ShapeDtypeStruct((s, CW), F32),
                   jax.ShapeDtypeStruct((s, D), BF16)],
        compiler_params=_cparams(48, ("arbitrary",)),
    )(o, bcu, bcu, cw8, ga, gc, gsum, w_out, x, g_post, g_ffn_pre)


def _ffn_fwd_loss(h2, wgu, wd, x2, target, g_post, *, tm):
    s = x2.shape[0]

    def body(h_ref, w_ref, wd_ref, x2_ref, t_ref, g_ref,
             gate_ref, up_ref, a_ref, dx3_ref, dff_ref, loss_ref, dg_ref):
        @pl.when(pl.program_id(0) == 0)
        def _():
            loss_ref[...] = jnp.zeros_like(loss_ref)
            dg_ref[...] = jnp.zeros_like(dg_ref)

        h = h_ref[...]
        ff = None
        for j in range(4):
            gate = jnp.dot(h, w_ref[0, j], preferred_element_type=F32)
            up = jnp.dot(h, w_ref[1, j], preferred_element_type=F32)
            gate_ref[j] = gate.astype(BF16)
            up_ref[j] = up.astype(BF16)
            act = (gate * jax.nn.sigmoid(gate) * up).astype(BF16)
            a_ref[j] = act
            part = jnp.dot(act, wd_ref[j], preferred_element_type=F32)
            ff = part if ff is None else ff + part
        out, n, r = _rms_fwd(ff, g_ref[...])
        e = x2_ref[...] + out - t_ref[...]
        loss_ref[...] += _fold8(e * e)
        dx3 = e * (1.0 / D)
        dx3_ref[...] = dx3
        dff, dg = _rms_bwd(dx3, n, r, g_ref[...])
        dff_ref[...] = dff.astype(BF16)
        dg_ref[...] += _fold8(dg)

    blk4 = pl.BlockSpec((4, tm, FB), lambda i: (0, i, 0))
    return pl.pallas_call(
        body, name="ffn_fwd_loss", grid=(s // tm,),
        in_specs=[_rows(tm, D), _resident((2, 4, D, FB)), _resident((4, FB, D)), _rows(tm, D), _rows(tm, D), _full((1, D))],
        out_specs=[blk4, blk4, blk4, _rows(tm, D), _rows(tm, D), _full((SUBLANES, D)), _full((SUBLANES, D))],
        out_shape=[jax.ShapeDtypeStruct((4, s, FB), BF16)] * 3
        + [jax.ShapeDtypeStruct((s, D), F32), jax.ShapeDtypeStruct((s, D), BF16),
           jax.ShapeDtypeStruct((SUBLANES, D), F32), jax.ShapeDtypeStruct((SUBLANES, D), F32)],
        compiler_params=_cparams(56, ("arbitrary",)),
    )(h2, wgu, wd, x2, target, g_post)


def _ffn_bwd(dff, wd, gate, up, wgu, x2, g_pre, dx3, y, g_post, *, tm):
    s = x2.shape[0]

    def body(dff_ref, wd_ref, gate_ref, up_ref, w_ref, x2_ref, gpre_ref, dx3_ref, y_ref, gpost_ref,
             dgu_ref, dx2_ref, dy_ref, dgpre_ref, dgpost_ref):
        @pl.when(pl.program_id(0) == 0)
        def _():
            dgpre_ref[...] = jnp.zeros_like(dgpre_ref)
            dgpost_ref[...] = jnp.zeros_like(dgpost_ref)

        dff = dff_ref[...]
        dh2 = None
        for j in range(4):
            da = lax.dot_general(dff, wd_ref[j], NT, preferred_element_type=F32)
            g = gate_ref[j].astype(F32)
            sg = jax.nn.sigmoid(g)
            dgate = (da * up_ref[j].astype(F32) * (sg * (1.0 + g * (1.0 - sg)))).astype(BF16)
            dup = (da * (g * sg)).astype(BF16)
            dgu_ref[0, j] = dgate
            dgu_ref[1, j] = dup
            part = (lax.dot_general(dgate, w_ref[0, j], NT, preferred_element_type=F32)
                    + lax.dot_general(dup, w_ref[1, j], NT, preferred_element_type=F32))
            dh2 = part if dh2 is None else dh2 + part
        _, n2, r2 = _rms_fwd(x2_ref[...], gpre_ref[...])
        dxn, dg = _rms_bwd(dh2, n2, r2, gpre_ref[...])
        dgpre_ref[...] += _fold8(dg)
        dx2 = dx3_ref[...] + dxn
        dx2_ref[...] = dx2
        _, ny, ry = _rms_fwd(y_ref[...], gpost_ref[...])
        dy, dg2 = _rms_bwd(dx2, ny, ry, gpost_ref[...])
        dy_ref[...] = dy.astype(BF16)
        dgpost_ref[...] += _fold8(dg2)

    blk4 = pl.BlockSpec((4, tm, FB), lambda i: (0, i, 0))
    return pl.pallas_call(
        body, name="ffn_bwd", grid=(s // tm,),
        in_specs=[_rows(tm, D), _resident((4, FB, D)), blk4, blk4, _resident((2, 4, D, FB)), _rows(tm, D), _full((1, D)),
                  _rows(tm, D), _rows(tm, D), _full((1, D))],
        out_specs=[pl.BlockSpec((2, 4, tm, FB), lambda i: (0, 0, i, 0)), _rows(tm, D), _rows(tm, D),
                   _full((SUBLANES, D)), _full((SUBLANES, D))],
        out_shape=[jax.ShapeDtypeStruct((2, 4, s, FB), BF16), jax.ShapeDtypeStruct((s, D), F32),
                   jax.ShapeDtypeStruct((s, D), BF16), jax.ShapeDtypeStruct((SUBLANES, D), F32),
                   jax.ShapeDtypeStruct((SUBLANES, D), F32)],
        compiler_params=_cparams(56, ("arbitrary",)),
    )(dff, wd, gate, up, wgu, x2, g_pre, dx3, y, g_post)


def _grad_matmul(a, b, *, ta, tb, ts, name):
    s, ka = a.shape
    nb = b.shape[1]
    ts = min(ts, s)
    nk = s // ts

    def body(a_ref, b_ref, o_ref, acc):
        k = pl.program_id(2)

        @pl.when(k == 0)
        def _():
            acc[...] = jnp.zeros_like(acc)

        acc[...] += lax.dot_general(a_ref[...], b_ref[...], TN, preferred_element_type=F32)

        @pl.when(k == nk - 1)
        def _():
            o_ref[...] = acc[...].astype(BF16)

    return pl.pallas_call(
        body, name=name, grid=(ka // ta, nb // tb, nk),
        in_specs=[pl.BlockSpec((ts, ta), lambda i, j, k: (k, i)), pl.BlockSpec((ts, tb), lambda i, j, k: (k, j))],
        out_specs=pl.BlockSpec((ta, tb), lambda i, j, k: (i, j)),
        out_shape=jax.ShapeDtypeStruct((ka, nb), BF16),
        scratch_shapes=[pltpu.VMEM((ta, tb), F32)],
        compiler_params=_cparams(48, ("arbitrary", "arbitrary", "arbitrary")),
    )(a, b)


GW_TILE = 256


def _grad_w_in(h1t, pieces):
    ka, s = h1t.shape
    widths = [p.shape[1] for p in pieces]
    assert all(w % GW_TILE == 0 for w in widths)
    first = [sum(widths[:i]) // GW_TILE for i in range(len(pieces))]
    count = [w // GW_TILE for w in widths]

    def body(a_ref, *refs):
        o_ref = refs[-1]
        j = pl.program_id(0)
        for ref, f0, n in zip(refs[:-1], first, count):
            @pl.when((j >= f0) & (j < f0 + n))
            def _(ref=ref):
                o_ref[...] = jnp.dot(a_ref[...], ref[...], preferred_element_type=F32).astype(BF16)

    def spec(f0, n):
        return pl.BlockSpec((s, GW_TILE), lambda j: (0, jnp.clip(j - f0, 0, n - 1)))

    return pl.pallas_call(
        body, name="grad_w_in", grid=(sum(count),),
        in_specs=[_resident((ka, s))] + [spec(f0, n) for f0, n in zip(first, count)],
        out_specs=pl.BlockSpec((ka, GW_TILE), lambda j: (0, j)),
        out_shape=jax.ShapeDtypeStruct((ka, sum(widths)), BF16),
        compiler_params=_cparams(56, ("arbitrary",)),
    )(h1t, *pieces)


def _grad_matmul_blocks(a, b, *, ts, name):
    nblk = a.shape[0] if a.ndim == 3 else b.shape[0]
    s = a.shape[-2]
    ka, nb = a.shape[-1], b.shape[-1]
    ts = min(ts, s)
    nk = s // ts

    def body(a_ref, b_ref, o_ref, acc):
        k = pl.program_id(1)

        @pl.when(k == 0)
        def _():
            acc[...] = jnp.zeros_like(acc)

        av = a_ref[0] if a.ndim == 3 else a_ref[...]
        bv = b_ref[0] if b.ndim == 3 else b_ref[...]
        acc[...] += lax.dot_general(av, bv, TN, preferred_element_type=F32)

        @pl.when(k == nk - 1)
        def _():
            o_ref[0] = acc[...].astype(BF16)

    def spec(arr, width):
        if arr.ndim == 3:
            return pl.BlockSpec((1, ts, width), lambda j, k: (j, k, 0))
        return pl.BlockSpec((ts, width), lambda j, k: (k, 0))

    return pl.pallas_call(
        body, name=name, grid=(nblk, nk),
        in_specs=[spec(a, ka), spec(b, nb)],
        out_specs=pl.BlockSpec((1, ka, nb), lambda j, k: (j, 0, 0)),
        out_shape=jax.ShapeDtypeStruct((nblk, ka, nb), BF16),
        scratch_shapes=[pltpu.VMEM((ka, nb), F32)],
        compiler_params=_cparams(48, ("arbitrary", "arbitrary")),
    )(a, b)


def _mix_bwd(dy, w_out, o, cv, bcu, ga, gc, gsum, *, tm):
    s = dy.shape[0]

    def group_norm_bwd(dn_out, v, g, gs):
        r = lax.rsqrt(_group_sum(v * v, gs) * (1.0 / DH) + EPS)
        n = v * r
        dn = dn_out * g
        return r * (dn - n * (_group_sum(dn * n, gs) * (1.0 / DH))), dn_out * n

    def body(dy_ref, w_ref, o_ref, cv_ref, bcu_ref, ga_ref, gc_ref, gs_ref,
             do_ref, dl_ref, dcv_ref, db_ref, dga_ref, dgc_ref):
        @pl.when(pl.program_id(0) == 0)
        def _():
            dga_ref[...] = jnp.zeros_like(dga_ref)
            dgc_ref[...] = jnp.zeros_like(dgc_ref)

        dm = lax.dot_general(dy_ref[...], w_ref[...], NT, preferred_element_type=F32)
        ov = o_ref[...]
        do, dga = group_norm_bwd(dm[:, 0:AW], ov, ga_ref[...], gs_ref[...])
        dob = do.astype(BF16)
        do_ref[...] = dob
        dl_ref[...] = _group_sum(dob.astype(F32) * ov, gs_ref[...])
        dga_ref[...] += _fold8(dga)
        gate_b = bcu_ref[:, 0:CW].astype(F32)
        cv = cv_ref[...]
        dconv, dgc = group_norm_bwd(dm[:, AW:D], gate_b * cv, gc_ref[...], gs_ref[...])
        dgc_ref[...] += _fold8(dgc)
        dcv_ref[...] = dconv * gate_b
        db_ref[...] = (dconv * cv).astype(BF16)

    return pl.pallas_call(
        body, name="mix_bwd", grid=(s // tm,),
        in_specs=[_rows(tm, D), _resident((D, D)), _rows(tm, AW), _rows(tm, CW), _rows(tm, 3 * CW),
                  _full((1, AW)), _full((1, CW)), _full((GS, GS))],
        out_specs=[_rows(tm, AW), _rows(tm, AW), _rows(tm, CW), _rows(tm, CW),
                   _full((SUBLANES, AW)), _full((SUBLANES, CW))],
        out_shape=[jax.ShapeDtypeStruct((s, AW), BF16), jax.ShapeDtypeStruct((s, AW), F32),
                   jax.ShapeDtypeStruct((s, CW), F32), jax.ShapeDtypeStruct((s, CW), BF16),
                   jax.ShapeDtypeStruct((SUBLANES, AW), F32), jax.ShapeDtypeStruct((SUBLANES, CW), F32)],
        compiler_params=_cparams(48, ("arbitrary",)),
    )(dy, w_out, o, cv, bcu, ga, gc, gsum)


def _conv_bwd(dcv, db, bcu, cw8, *, tm):
    s = dcv.shape[0]
    nt = s // tm

    def body(dcv_ref, nxt_ref, db_ref, bcu_ref, halo_ref, cw_ref, dbcu_ref, dw_ref):
        i = pl.program_id(0)

        @pl.when(i == 0)
        def _():
            dw_ref[...] = jnp.zeros_like(dw_ref)

        z, z1, z2 = _conv_taps(bcu_ref, halo_ref, i == 0, tm)
        d = dcv_ref[...]
        dw_ref[0] += _fold8(d * z2)
        dw_ref[1] += _fold8(d * z1)
        dw_ref[2] += _fold8(d * z)
        nx = jnp.where(i == nt - 1, 0.0, nxt_ref[...])
        row = lax.broadcasted_iota(jnp.int32, (tm, CW), 0)
        d1 = jnp.where(row == tm - 1, nx[0:1, :], pltpu.roll(d, tm - 1, axis=0))
        d2 = jnp.where(row == tm - 2, nx[0:1, :], jnp.where(row == tm - 1, nx[1:2, :], pltpu.roll(d, tm - 2, axis=0)))
        dz = cw_ref[2:3, :] * d + cw_ref[1:2, :] * d1 + cw_ref[0:1, :] * d2
        dbcu_ref[:, 0:CW] = db_ref[...]
        dbcu_ref[:, CW:2 * CW] = (dz * bcu_ref[:, 2 * CW:3 * CW].astype(F32)).astype(BF16)
        dbcu_ref[:, 2 * CW:3 * CW] = (dz * bcu_ref[:, CW:2 * CW].astype(F32)).astype(BF16)

    return pl.pallas_call(
        body, name="conv_bwd", grid=(nt,),
        in_specs=[_rows(tm, CW),
                  pl.BlockSpec((SUBLANES, CW), lambda i: (jnp.minimum((i + 1) * (tm // SUBLANES), s // SUBLANES - 1), 0)),
                  _rows(tm, CW), _rows(tm, 3 * CW), _halo_before(tm, 3 * CW), _full((SUBLANES, CW))],
        out_specs=[_rows(tm, 3 * CW), _full((3, SUBLANES, CW))],
        out_shape=[jax.ShapeDtypeStruct((s, 3 * CW), BF16), jax.ShapeDtypeStruct((3, SUBLANES, CW), F32)],
        compiler_params=_cparams(48, ("arbitrary",)),
    )(dcv, dcv, db, bcu, bcu, cw8)


def _attn_bwd(qp, kp, v, do, lse, dl, mk, *, t):
    s = qp.shape[0]
    nq = s // t

    def body(q_ref, k_ref, v_ref, do_ref, lse_ref, dl_ref, mk_ref, dq_ref, dk_ref, dv_ref, dkx_ref, dq_acc):
        ki = pl.program_id(1)

        @pl.when(ki == 0)
        def _():
            dq_acc[...] = jnp.zeros_like(dq_acc)

        row = lax.broadcasted_iota(jnp.int32, (t, t), 0)
        col = lax.broadcasted_iota(jnp.int32, (t, t), 1)
        lane = lax.broadcasted_iota(jnp.int32, (t, 128), 1)

        def head_step(hh, qi, carry, masked):
            dk, dv, cs = carry
            off = pl.multiple_of(qi * t, t)
            rows = pl.ds(off, t)
            kh = k_ref[:, HP * hh:HP * (hh + 1)]
            q = q_ref[rows, HP * hh:HP * (hh + 1)]
            m_col = mk_ref[0, rows, DH * hh:DH * hh + 1]
            scale = jnp.exp(m_col - lse_ref[rows, DH * hh:DH * hh + 1])
            do2 = do_ref[rows, :]
            dom = jnp.where(lane < DH, do2 if hh == 0 else pltpu.roll(do2, DH, axis=1), jnp.zeros((), BF16))
            sc = lax.dot_general(q, kh, NT, preferred_element_type=F32) - m_col
            if masked:
                sc = jnp.where(col <= row, sc, -1e30)
            pt = jnp.exp(sc).astype(BF16)
            dp = lax.dot_general(dom, v_ref[:, HP * hh:HP * (hh + 1)], NT, preferred_element_type=F32)
            ds32 = (pt.astype(F32) * scale) * (dp - dl_ref[rows, DH * hh:DH * hh + 1])
            ds = ds32.astype(BF16)
            cs = cs + _fold8(ds32)
            dv = dv + jnp.dot((dom.astype(F32) * scale).astype(BF16).T, pt, preferred_element_type=F32)
            dk = dk + jnp.dot(q.T, ds, preferred_element_type=F32)
            dq_acc[rows, HP * hh:HP * (hh + 1)] += jnp.dot(ds, kh, preferred_element_type=F32)
            return dk, dv, cs

        def step(qi, carry, masked):
            return tuple(head_step(hh, qi, carry[hh], masked) for hh in range(2))

        zero = (jnp.zeros((HP, t), F32), jnp.zeros((128, t), F32), jnp.zeros((SUBLANES, t), F32))
        carry = step(ki, (zero, zero), True)
        (dk0, dv0, cs0), (dk1, dv1, cs1) = lax.fori_loop(ki + 1, nq, functools.partial(step, masked=False), carry)
        def two_heads(a0, a1):
            return jnp.where(lane < DH, a0, pltpu.roll(a1, DH, axis=1))

        def rows_to_lanes(a0, a1):
            return jnp.concatenate([a0, a1], axis=0).T

        dk_ref[...] = rows_to_lanes(dk0[0:DH], dk1[0:DH]).astype(BF16)
        dv_ref[...] = rows_to_lanes(dv0[0:DH], dv1[0:DH]).astype(BF16)
        total = lambda cs: jnp.broadcast_to(jnp.sum(cs, axis=0, keepdims=True), (DH, t))
        dkx_ref[...] = rows_to_lanes(total(cs0), total(cs1))

        @pl.when(ki == nq - 1)
        def _():
            for c in range(s // t):
                rows = slice(c * t, (c + 1) * t)
                dq_ref[rows, :] = two_heads(dq_acc[rows, 0:HP], dq_acc[rows, HP:2 * HP]).astype(BF16)

    return pl.pallas_call(
        body, name="attn_bwd", grid=(H // 2, nq),
        in_specs=[pl.BlockSpec((s, 2 * HP), lambda p, i: (0, p)),
                  pl.BlockSpec((t, 2 * HP), lambda p, i: (i, p)),
                  pl.BlockSpec((t, 2 * HP), lambda p, i: (i, p)),
                  pl.BlockSpec((s, 128), lambda p, i: (0, p)),
                  pl.BlockSpec((s, 128), lambda p, i: (0, p)),
                  pl.BlockSpec((s, 128), lambda p, i: (0, p)),
                  pl.BlockSpec((1, s, 128), lambda p, i: (i, 0, p))],
        out_specs=[pl.BlockSpec((s, 128), lambda p, i: (0, p)),
                   pl.BlockSpec((t, 128), lambda p, i: (i, p)),
                   pl.BlockSpec((t, 128), lambda p, i: (i, p)),
                   pl.BlockSpec((t, 128), lambda p, i: (i, p))],
        out_shape=[jax.ShapeDtypeStruct((s, AW), BF16), jax.ShapeDtypeStruct((s, AW), BF16),
                   jax.ShapeDtypeStruct((s, AW), BF16), jax.ShapeDtypeStruct((s, AW), F32)],
        scratch_shapes=[pltpu.VMEM((s, 2 * HP), F32)],
        compiler_params=_cparams(56, ("arbitrary", "arbitrary")),
    )(qp, kp, v, do, lse, dl, mk)


def _forget_bwd(dkx, z, sel, *, tm):
    s = dkx.shape[0]
    nt = s // tm

    def body(dk_ref, z_ref, sel_ref, dfl_ref, dbf_ref, carry):
        @pl.when(pl.program_id(0) == 0)
        def _():
            carry[...] = jnp.zeros_like(carry)
            dbf_ref[...] = jnp.zeros_like(dbf_ref)

        dc = _split_dot(dk_ref[...], sel_ref[...])
        row = lax.broadcasted_iota(jnp.int32, (tm, tm), 0)
        col = lax.broadcasted_iota(jnp.int32, (tm, tm), 1)
        tri = (col >= row).astype(BF16)
        dlogf = _exact_dot01(tri, dc) + carry[0:1, :]
        carry[...] = jnp.broadcast_to(dlogf[0:1, :], carry.shape)
        dz = dlogf * (1.0 - jax.nn.sigmoid(z_ref[...]))
        dfl_ref[:, 0:128] = dz.astype(BF16)
        dfl_ref[:, 128:GW_TILE] = jnp.zeros((tm, GW_TILE - 128), BF16)
        dbf_ref[...] += _fold8(dz)

    rev = lambda i: (nt - 1 - i, 0)
    return pl.pallas_call(
        body, name="forget_bwd", grid=(nt,),
        in_specs=[pl.BlockSpec((tm, AW), rev), pl.BlockSpec((tm, 128), rev), _full((AW, 128))],
        out_specs=[pl.BlockSpec((tm, GW_TILE), rev), _full((SUBLANES, 128))],
        out_shape=[jax.ShapeDtypeStruct((s, GW_TILE), BF16), jax.ShapeDtypeStruct((SUBLANES, 128), F32)],
        scratch_shapes=[pltpu.VMEM((SUBLANES, 128), F32)],
        compiler_params=_cparams(48, ("arbitrary",)),
    )(dkx, z, sel)


def _in_proj_bwd(pieces, wp, x, g1, dx2, *, tm):
    s = x.shape[0]

    def body(q_ref, k_ref, v_ref, bcu_ref, f_ref, w_ref, x_ref, g_ref, dx2_ref, dx_ref, dg_ref):
        @pl.when(pl.program_id(0) == 0)
        def _():
            dg_ref[...] = jnp.zeros_like(dg_ref)

        dh = None
        for ref, (lo, hi) in zip((q_ref, k_ref, v_ref, bcu_ref, f_ref), PIECES):
            part = lax.dot_general(ref[...], w_ref[:, lo:hi], NT, preferred_element_type=F32)
            dh = part if dh is None else dh + part
        _, n, r = _rms_fwd(x_ref[...], g_ref[...])
        dxn, dg = _rms_bwd(dh, n, r, g_ref[...])
        dx_ref[...] = dx2_ref[...] + dxn
        dg_ref[...] += _fold8(dg)

    return pl.pallas_call(
        body, name="in_proj_bwd", grid=(s // tm,),
        in_specs=[_rows(tm, hi - lo) for lo, hi in PIECES] + [_resident((D, WP)), _rows(tm, D), _full((1, D)), _rows(tm, D)],
        out_specs=[_rows(tm, D), _full((SUBLANES, D))],
        out_shape=[jax.ShapeDtypeStruct((s, D), F32), jax.ShapeDtypeStruct((SUBLANES, D), F32)],
        compiler_params=_cparams(56, ("arbitrary",)),
    )(*pieces, wp, x, g1, dx2)


def _position():
    return lax.axis_index("x"), lax.axis_index("y"), lax.axis_index("c")


ANY = pl.BlockSpec(memory_space=pl.ANY)


def _all_gather(shards):
    n = len(shards)

    def body(*refs):
        x_refs, out_refs = refs[:n], refs[n:2 * n]
        send_sems, recv_sems, local_sems = refs[2 * n:]
        x, y, c = _position()
        me, sibling = (x, y, c), (x, y, 1 - c)
        chips = [(1 - x, y), (x, 1 - y), (1 - x, 1 - y)]

        def copy(a, k, block, to, own=False):
            slot = out_refs[a].at[4 * block[0] + 2 * block[1] + block[2]]
            return pltpu.make_async_remote_copy(
                src_ref=x_refs[a] if own else slot, dst_ref=slot,
                send_sem=send_sems.at[7 * a + k], recv_sem=recv_sems.at[7 * a + k], device_id=to, device_id_type=MESH_ID)

        mine = [pltpu.make_async_copy(x_refs[a], out_refs[a].at[4 * x + 2 * y + c], local_sems.at[a]) for a in range(n)]
        for cp in mine:
            cp.start()
        first = []
        for a in range(n):
            first.append(copy(a, 0, me, sibling, own=True))
            first += [copy(a, 1 + j, me, (*chip, c), own=True) for j, chip in enumerate(chips)]
        for cp in first:
            cp.start()
        passed = []
        for j, chip in enumerate(chips):
            for a in range(n):
                copy(a, 1 + j, (*chip, c), me).wait_recv()
                fwd = copy(a, 4 + j, (*chip, c), sibling)
                fwd.start()
                passed.append(fwd)
        for a in range(n):
            copy(a, 0, sibling, me).wait_recv()
            for j, chip in enumerate(chips):
                copy(a, 4 + j, (*chip, 1 - c), me).wait_recv()
        for cp in first + passed:
            cp.wait_send()
        for cp in mine:
            cp.wait()

    return pl.pallas_call(
        body, name="all_gather_weights",
        out_shape=[jax.ShapeDtypeStruct((NDEV,) + sh.shape, sh.dtype) for sh in shards],
        in_specs=[ANY] * n, out_specs=[ANY] * n,
        scratch_shapes=[pltpu.SemaphoreType.DMA((7 * n,)), pltpu.SemaphoreType.DMA((7 * n,)), pltpu.SemaphoreType.DMA((n,))],
    )(*shards)


def _pair_exchange(grads):
    n = len(grads)

    def body(*refs):
        g_refs, out_refs = refs[:n], refs[n:2 * n]
        send_sems, recv_sems = refs[2 * n:]
        x, y, c = _position()
        copies = [pltpu.make_async_remote_copy(
            src_ref=g_refs[a].at[:, pl.ds(1 - c, 1)], dst_ref=out_refs[a], send_sem=send_sems.at[a],
            recv_sem=recv_sems.at[a], device_id=(x, y, 1 - c), device_id_type=MESH_ID) for a in range(n)]
        for cp in copies:
            cp.start()
        for cp in copies:
            cp.wait()

    return pl.pallas_call(
        body, name="grad_pair_exchange",
        out_shape=[jax.ShapeDtypeStruct((4, 1) + g.shape[2:], g.dtype) for g in grads],
        in_specs=[ANY] * n, out_specs=[ANY] * n,
        scratch_shapes=[pltpu.SemaphoreType.DMA((n,)), pltpu.SemaphoreType.DMA((n,))],
    )(*grads)


def _pair_sum(g, got, idx, *, tr, name):
    r, c = g.shape[2:]

    def body(idx_ref, g_ref, got_ref, pb_ref, own_ref):
        p = g_ref[0, 0].astype(F32) + got_ref[0, 0].astype(F32)
        pb_ref[0] = p.astype(BF16)

        @pl.when(pl.program_id(1) == idx_ref[1])
        def _():
            own_ref[...] = p

    return pl.pallas_call(
        body, name=name,
        grid_spec=pltpu.PrefetchScalarGridSpec(
            num_scalar_prefetch=1, grid=(r // tr, 4),
            in_specs=[pl.BlockSpec((1, 1, tr, c), lambda i, j, idx: (j, idx[0], i, 0)),
                      pl.BlockSpec((1, 1, tr, c), lambda i, j, idx: (j, 0, i, 0))],
            out_specs=[pl.BlockSpec((1, tr, c), lambda i, j, idx: (j, i, 0)),
                       pl.BlockSpec((tr, c), lambda i, j, idx: (i, 0))]),
        out_shape=[jax.ShapeDtypeStruct((4, r, c), BF16), jax.ShapeDtypeStruct((r, c), F32)],
        compiler_params=_cparams(32, ("arbitrary", "arbitrary")),
    )(idx, g, got)


HBM = pl.BlockSpec(memory_space=pltpu.HBM)
SEM = pl.BlockSpec(memory_space=pltpu.SEMAPHORE)
DATAFLOW = pltpu.SideEffectType.DATAFLOW_SIDE_EFFECTING


PEERS = {"gather": NDEV - 1, "scatter": NDEV - 1, "chips": 3}


def _exchange_copies(src_refs, land_refs, send_sems, recv_sems, mode):
    x, y, c = _position()
    me, my_chip = 4 * x + 2 * y + c, 2 * x + y
    npeers = PEERS[mode]
    copies = []
    for a, (s_ref, l_ref) in enumerate(zip(src_refs, land_refs)):
        for k in range(npeers):
            if mode == "chips":
                px, py, pc = x ^ ((k + 1) >> 1), y ^ ((k + 1) & 1), c
                src, dst = s_ref.at[2 * px + py], l_ref.at[my_chip]
            else:
                px, py, pc = x ^ ((k + 1) >> 2), y ^ (((k + 1) >> 1) & 1), c ^ ((k + 1) & 1)
                src, dst = (s_ref.at[4 * px + 2 * py + pc] if mode == "scatter" else s_ref), l_ref.at[me]
            copies.append(pltpu.make_async_remote_copy(
                src_ref=src, dst_ref=dst, send_sem=send_sems.at[npeers * a + k], recv_sem=recv_sems.at[npeers * a + k],
                device_id=(px, py, pc), device_id_type=MESH_ID))
    return copies


def _exchange_start(srcs, lands, after, *, mode, name):
    n = len(srcs)
    nsem = PEERS[mode] * n

    def body(*refs):
        token = refs[-1]
        for cp in _exchange_copies(refs[:n], refs[n:2 * n], refs[2 * n + 1], refs[2 * n + 2], mode):
            cp.start()
        token[...] = jnp.zeros_like(token)

    arrays = list(srcs) + list(lands)
    outs = pl.pallas_call(
        body, name=name,
        out_shape=(pltpu.SemaphoreType.DMA((nsem,)), pltpu.SemaphoreType.DMA((nsem,)),
                   *[pltpu.HBM(a.shape, a.dtype) for a in arrays], jax.ShapeDtypeStruct((SUBLANES, LANES), F32)),
        in_specs=[HBM] * (2 * n) + [ANY],
        out_specs=(SEM, SEM, *[HBM] * (2 * n), pl.BlockSpec(memory_space=pltpu.VMEM)),
        input_output_aliases={i: 2 + i for i in range(2 * n)},
        compiler_params=pltpu.CompilerParams(has_side_effects=DATAFLOW),
    )(*[pltpu.with_memory_space_constraint(a, pltpu.HBM) for a in arrays], after)
    return outs[0], outs[1], outs[2:2 + n], outs[2 + n:2 + 2 * n], outs[-1]


def _exchange_wait(send_sems, recv_sems, srcs, lands, after, *, mode, name):
    n = len(srcs)

    def body(*refs):
        for cp in _exchange_copies(refs[:n], refs[n:2 * n], refs[2 * n], refs[2 * n + 1], mode):
            cp.wait_send()
            cp.wait_recv()

    arrays = list(srcs) + list(lands)
    outs = pl.pallas_call(
        body, name=name,
        out_shape=tuple(pltpu.HBM(a.shape, a.dtype) for a in arrays),
        in_specs=[HBM] * (2 * n) + [SEM, SEM, ANY],
        out_specs=tuple([HBM] * (2 * n)),
        input_output_aliases={i: i for i in range(2 * n)},
        compiler_params=pltpu.CompilerParams(has_side_effects=DATAFLOW),
    )(*arrays, send_sems, recv_sems, after)
    return outs[n:]


def _own_slot(value, me):
    return lax.dynamic_update_index_in_dim(lax.empty((NDEV,) + value.shape, value.dtype), value, me, 0)


def _small_all_reduce(parts):
    def body(gmp_ref, gmo_ref, gfp_ref, gfo_ref, ga_ref, gc_ref, dw_ref, bf_ref, loss_ref,
             out_ref, buf, send_sems, recv_sems):
        x, y, c = _position()
        me = 4 * x + 2 * y + c

        def colsum(v):
            return jnp.sum(v, axis=0, keepdims=True)

        loss = jnp.sum(colsum(loss_ref[...]), axis=1, keepdims=True) * (0.5 / D)
        rows = [colsum(gmp_ref[...]), colsum(gmo_ref[...]), colsum(gfp_ref[...]), colsum(gfo_ref[...]),
                jnp.concatenate([colsum(ga_ref[...]), colsum(gc_ref[...])], axis=1),
                jnp.concatenate([colsum(dw_ref[0]), colsum(dw_ref[1])], axis=1),
                jnp.concatenate([colsum(dw_ref[2]), colsum(bf_ref[...]), jnp.broadcast_to(loss, (1, 128)),
                                 jnp.zeros((1, 256), F32)], axis=1),
                jnp.zeros((1, D), F32)]
        buf[me] = jnp.concatenate(rows, axis=0)
        copies = []
        for mm in range(1, NDEV):
            peer = (x ^ (mm >> 2), y ^ ((mm >> 1) & 1), c ^ (mm & 1))
            copies.append(pltpu.make_async_remote_copy(
                src_ref=buf.at[me], dst_ref=buf.at[me], send_sem=send_sems.at[mm - 1], recv_sem=recv_sems.at[mm - 1],
                device_id=peer, device_id_type=MESH_ID))
        for cp in copies:
            cp.start()
        for cp in copies:
            cp.wait_recv()
        for cp in copies:
            cp.wait_send()
        acc = buf[0]
        for d in range(1, NDEV):
            acc = acc + buf[d]
        out_ref[...] = acc

    vm = pl.BlockSpec(memory_space=pltpu.VMEM)
    return pl.pallas_call(
        body, name="small_all_reduce",
        out_shape=jax.ShapeDtypeStruct((SUBLANES, D), F32),
        in_specs=[vm] * len(parts), out_specs=vm,
        scratch_shapes=[pltpu.VMEM((NDEV, SUBLANES, D), F32), pltpu.SemaphoreType.DMA((7,)), pltpu.SemaphoreType.DMA((7,))],
    )(*parts)


def _adam_update(w, g, m, v):
    nm = ADAM_B1 * m + (1.0 - ADAM_B1) * g
    nv = ADAM_B2 * v + (1.0 - ADAM_B2) * (g * g)
    m_hat = nm / (1.0 - ADAM_B1 ** ADAM_STEP)
    v_hat = nv / (1.0 - ADAM_B2 ** ADAM_STEP)
    return -ADAM_LR * (m_hat / (jnp.sqrt(v_hat) + ADAM_EPS) + ADAM_WD * w), nm, nv


SMALL_SLOTS = {"g_mix_pre": (0, 0, D), "g_mix_post": (1, 0, D), "g_ffn_pre": (2, 0, D), "g_ffn_post": (3, 0, D),
               "g_attn_out": (4, 0, AW), "g_conv_out": (4, AW, CW), "b_forget": (6, CW, H)}


def _small_adamw(small, conv_grad, params):
    names = list(params)
    n = len(names)

    def body(*refs):
        small_ref, cg_ref = refs[0], refs[1]
        ins, outs = refs[2:2 + 3 * n], refs[2 + 3 * n:]
        for i, name in enumerate(names):
            w_ref, m_ref, v_ref = ins[3 * i:3 * i + 3]
            g_ref, d_ref, nm_ref, nv_ref = outs[4 * i:4 * i + 4]
            if name == "conv_w":
                g = cg_ref[...]
            else:
                r, c0, width = SMALL_SLOTS[name]
                g = small_ref[r:r + 1, c0:c0 + width]
            g_ref[...] = g
            d_ref[...], nm_ref[...], nv_ref[...] = _adam_update(w_ref[...], g, m_ref[...], v_ref[...])

    vm = pl.BlockSpec(memory_space=pltpu.VMEM)
    flat = [a for name in names for a in params[name]]
    outs = pl.pallas_call(
        body, name="adamw_small",
        in_specs=[vm] * (2 + 3 * n), out_specs=[vm] * (4 * n),
        out_shape=[jax.ShapeDtypeStruct(params[name][0].shape, F32) for name in names for _ in range(4)],
    )(small, conv_grad, *flat)
    return {name: outs[4 * i:4 * i + 4] for i, name in enumerate(names)}


def _chip_sum_adamw(got, own, idx, wt, mt, vt, *, tr, name):
    cols, rows = wt.shape
    gcols = own.shape[1]

    def body(idx_ref, got_ref, own_ref, w_ref, m_ref, v_ref, g_ref, d_ref, nm_ref, nv_ref):
        g = jnp.zeros((tr, gcols), F32)
        for j in range(4):
            g = g + jnp.where(idx_ref[1] == j, own_ref[...], got_ref[j].astype(F32))
        g = g.T[:cols]
        g_ref[...] = g
        d_ref[...], nm_ref[...], nv_ref[...] = _adam_update(w_ref[...], g, m_ref[...], v_ref[...])

    spec = pl.BlockSpec((cols, tr), lambda i, idx: (0, i))
    gspec = pl.BlockSpec((tr, gcols), lambda i, idx: (i, 0))
    return pl.pallas_call(
        body, name=name,
        grid_spec=pltpu.PrefetchScalarGridSpec(
            num_scalar_prefetch=1, grid=(rows // tr,),
            in_specs=[pl.BlockSpec((4, tr, gcols), lambda i, idx: (0, i, 0)), gspec, spec, spec, spec],
            out_specs=[spec] * 4),
        out_shape=[jax.ShapeDtypeStruct((cols, rows), F32)] * 4,
        compiler_params=_cparams(32, ("arbitrary",)),
    )(idx, got, own, wt, mt, vt)


def _device_sum_adamw(land, w, m, v, *, tr, name):
    rows, cols = w.shape

    def body(land_ref, w_ref, m_ref, v_ref, g_ref, d_ref, nm_ref, nv_ref):
        g = land_ref[0].astype(F32)
        for dev in range(1, NDEV):
            g = g + land_ref[dev].astype(F32)
        g_ref[...] = g
        d_ref[...], nm_ref[...], nv_ref[...] = _adam_update(w_ref[...], g, m_ref[...], v_ref[...])

    spec = pl.BlockSpec((tr, cols), lambda i: (i, 0))
    return pl.pallas_call(
        body, name=name, grid=(rows // tr,),
        in_specs=[pl.BlockSpec((NDEV, tr, cols), lambda i: (0, i, 0)), spec, spec, spec],
        out_specs=[spec] * 4,
        out_shape=[jax.ShapeDtypeStruct((rows, cols), F32)] * 4,
        compiler_params=_cparams(32, ("arbitrary",)),
    )(land, w, m, v)


def _placement_constants():
    j = jnp.arange(128)[:, None]
    lane = jnp.arange(1024)[None, :]
    head, sub = lane // HP, lane % HP
    piece, jh = j // H, j % H
    valid = (j < 3 * H) & (jh == head)
    pq = jnp.where(valid & (sub == DH + piece), 1.0, 0.0).astype(BF16)
    pk = jnp.where(valid & (sub == DH + 3 + piece), -1.0, 0.0).astype(BF16)
    oq = jnp.where((sub >= DH + 3) & (sub < DH + 6), 1.0, 0.0).astype(F32)
    ok = jnp.where((sub >= DH) & (sub < DH + 3), 1.0, 0.0).astype(F32)
    r = jnp.arange(AW)[:, None]
    cc = jnp.arange(128)[None, :]
    sel = jnp.where((r % DH == 3) & (r // DH == cc), -1.0, 0.0).astype(BF16)
    gi = jnp.arange(GS)
    gsum = (gi[:, None] // DH == gi[None, :] // DH).astype(BF16)
    return pq, pk, oq, ok, sel, gsum


def _local_step(xs, tgt, wp, late_weights, cw8, bfp, g_attn_out, g_conv_out,
                g_mix_pre, g_mix_post, g_ffn_pre, g_ffn_post, early_grads=None, last_grad=None):
    pq, pk, oq, ok, sel, gsum = _placement_constants()
    h1t, qp, kp, vv, bcu, zf = _in_proj(xs, g_mix_pre, wp, bfp, pq, pk, oq, ok, tm=512)
    o, lse, mk = _attn_fwd(qp, kp, vv, t=512)
    w_out_f, wgu, wd = late_weights(lse)
    merged, y, x2, cv, h2 = _mix_out(o, bcu, cw8, g_attn_out, g_conv_out, gsum, w_out_f, xs, g_mix_post, g_ffn_pre, tm=512)
    gate, up, act, dx3, dff, loss_p, dg_ffn_post = _ffn_fwd_loss(h2, wgu, wd, x2, tgt, g_ffn_post, tm=512)

    dgu, dx2, dy, dg_ffn_pre, dg_mix_post = _ffn_bwd(dff, wd, gate, up, wgu, x2, g_ffn_pre, dx3, y, g_mix_post, tm=256)
    dw_down = _grad_matmul_blocks(act, dff, ts=4096, name="grad_w_down")
    dw_gu = _grad_matmul_blocks(dgu.reshape(NDEV, -1, FB), h2, ts=4096, name="grad_w_gate_up")
    dw_out = _grad_matmul(merged, dy, ta=1024, tb=1024, ts=2048, name="grad_w_out")
    token = early_grads(dw_out, dw_gu, dw_down) if early_grads is not None else None
    ga = g_attn_out if token is None else g_attn_out + token[0:1, 0:1]
    do, dl, dcv, db, dg_attn, dg_conv = _mix_bwd(dy, w_out_f, o, cv, bcu, ga, g_conv_out, gsum, tm=512)
    dbcu, dtaps = _conv_bwd(dcv, db, bcu, cw8, tm=512)
    dqp, dkp, dv, dkx = _attn_bwd(qp, kp, vv, do, lse, dl, mk, t=512)
    dfl, dbf = _forget_bwd(dkx, zf, sel, tm=512)
    pieces = (dqp, dkp, dv, dbcu, dfl)
    dwp = _grad_w_in(h1t, pieces)
    token = last_grad(dwp) if last_grad is not None else None
    g1 = g_mix_pre if token is None else g_mix_pre + token[0:1, 0:1]
    grad_x, dg_mix_pre = _in_proj_bwd(pieces, wp, xs, g1, dx2, tm=512)
    return (grad_x, dwp, dw_out, dw_gu, dw_down, dg_mix_pre, dg_mix_post, dg_ffn_pre, dg_ffn_post, dg_attn, dg_conv,
            dtaps, dbf, loss_p)


BIG_TILES = {"w_in": 256, "w_out": 128, "w_gate_up": 176, "w_down": 176}


def kernel(x, w_in, b_forget, conv_w, g_attn_out, g_conv_out, w_out, g_mix_pre, g_mix_post, w_gate_up, w_down, g_ffn_pre, g_ffn_post, loss_target, m_w_in, m_b_forget, m_conv_w, m_g_attn_out, m_g_conv_out, m_w_out, m_g_mix_pre, m_g_mix_post, m_w_gate_up, m_w_down, m_g_ffn_pre, m_g_ffn_post, v_w_in, v_b_forget, v_conv_w, v_g_attn_out, v_g_conv_out, v_w_out, v_g_mix_pre, v_g_mix_post, v_w_gate_up, v_w_down, v_g_ffn_pre, v_g_ffn_post):
    xc, yc, cc = _position()
    my_chip = 2 * xc + yc
    me = 2 * my_chip + cc
    idx = jnp.stack([cc, my_chip]).astype(jnp.int32)
    tables = _in_layout_tables()

    w_in_b = w_in[0].astype(BF16)
    g_in, g_last, g_taps = _all_gather([w_in_b[:, :IN_MAIN], w_in_b[:, IN_MAIN].reshape(SUBLANES, LANES), conv_w[0]])
    last_cols = jnp.pad(g_last.reshape(NDEV, D).T.astype(F32), ((0, 0), (0, LANES - NDEV)))
    wp = _assemble_w_in(g_in, last_cols, tables, tr=256)
    cw8 = jnp.pad(g_taps.transpose(1, 0, 2).reshape(3, CW), ((0, SUBLANES - 3), (0, 0)))

    late = [w_out[0].astype(BF16), w_gate_up[0].astype(BF16), w_down[0].astype(BF16)]
    ssem, rsem, late_thru, land_thru, token = _exchange_start(
        late, [_own_slot(s, me) for s in late], g_in, mode="gather", name="gather_late_start")
    bfp = jnp.pad(b_forget, ((0, 0), (0, 128 - H))) + token[0:1, :]

    def late_weights(after):
        l_out, l_gu, l_down = _exchange_wait(ssem, rsem, late_thru, land_thru, after, mode="gather", name="gather_late_wait")
        return l_out.reshape(D, D), l_gu.reshape(2, 4, D, FB), l_down.reshape(4, FB, D)

    early = {}

    def early_grads(dw_out, dw_gu, dw_down):
        srcs = [dw_out.reshape(NDEV, D // NDEV, D), dw_gu, dw_down.reshape(NDEV, DFF // NDEV, D)]
        lands = [_own_slot(lax.dynamic_index_in_dim(s, me, 0, keepdims=False), me) for s in srcs]
        early["handles"] = _exchange_start(srcs, lands, dw_out, mode="scatter", name="scatter_early_start")
        return early["handles"][4]

    last = {}

    def last_grad(dwp):
        g_w_in = _disassemble_w_in(dwp, tables, tr=256).reshape(4, 2, D, IN_PAD)
        (from_sibling,) = _pair_exchange([g_w_in])
        pair_b, last["own"] = _pair_sum(g_w_in, from_sibling, idx, tr=BIG_TILES["w_in"], name="grad_pair_sum_w_in")
        land = lax.dynamic_update_index_in_dim(lax.empty(pair_b.shape, pair_b.dtype),
                                               lax.dynamic_index_in_dim(pair_b, my_chip, 0, keepdims=False), my_chip, 0)
        last["handles"] = _exchange_start([pair_b], [land], last["own"], mode="chips", name="chips_w_in_start")
        return last["handles"][4]

    (grad_x, dwp, dw_out, dw_gu, dw_down, dg_mix_pre, dg_mix_post, dg_ffn_pre, dg_ffn_post, dg_attn, dg_conv,
     dtaps, dbf, loss_p) = _local_step(x[0], loss_target[0], wp, late_weights, cw8, bfp, g_attn_out, g_conv_out,
                                        g_mix_pre, g_mix_post, g_ffn_pre, g_ffn_post, early_grads, last_grad)

    e_ssem, e_rsem, e_srcs, e_lands, _ = early["handles"]
    land_out, land_gu, land_down = _exchange_wait(e_ssem, e_rsem, e_srcs, e_lands, dg_mix_pre, mode="scatter",
                                                  name="scatter_early_wait")
    res = {}
    big = {"w_out": (land_out, w_out[0], m_w_out[0], v_w_out[0]),
           "w_gate_up": (land_gu, w_gate_up[0].T, m_w_gate_up[0].T, v_w_gate_up[0].T),
           "w_down": (land_down, w_down[0], m_w_down[0], v_w_down[0])}
    for name, (land, w, m, v) in big.items():
        outs = _device_sum_adamw(land, w, m, v, tr=BIG_TILES[name], name="adamw_" + name)
        res[name] = [(o.T if name == "w_gate_up" else o)[None] for o in outs]
    c_ssem, c_rsem, c_srcs, c_lands, _ = last["handles"]
    after = sum(res[n][1][0, :SUBLANES, :LANES] for n in big)
    (from_chips,) = _exchange_wait(c_ssem, c_rsem, c_srcs, c_lands, after, mode="chips", name="chips_w_in_wait")
    outs = _chip_sum_adamw(from_chips, last["own"], idx, w_in[0].T, m_w_in[0].T, v_w_in[0].T,
                           tr=BIG_TILES["w_in"], name="adamw_w_in")
    res["w_in"] = [o.T[None] for o in outs]

    small = _small_all_reduce([dg_mix_pre, dg_mix_post, dg_ffn_pre, dg_ffn_post, dg_attn, dg_conv, dtaps, dbf, loss_p])
    taps_full = jnp.concatenate([small[5:6, :CW], small[5:6, CW:], small[6:7, :CW]], axis=0)
    loss = small[6, CW + 128]
    smalls = {"b_forget": (b_forget, m_b_forget, v_b_forget), "conv_w": (conv_w[0], m_conv_w[0], v_conv_w[0]),
              "g_attn_out": (g_attn_out, m_g_attn_out, v_g_attn_out), "g_conv_out": (g_conv_out, m_g_conv_out, v_g_conv_out),
              "g_mix_pre": (g_mix_pre, m_g_mix_pre, v_g_mix_pre), "g_mix_post": (g_mix_post, m_g_mix_post, v_g_mix_post),
              "g_ffn_pre": (g_ffn_pre, m_g_ffn_pre, v_g_ffn_pre), "g_ffn_post": (g_ffn_post, m_g_ffn_post, v_g_ffn_post)}
    for name, outs in _small_adamw(small, lax.dynamic_slice(taps_full, (0, me * 64), (3, 64)), smalls).items():
        res[name] = [o[None] for o in outs] if name == "conv_w" else list(outs)

    order = ["w_in", "b_forget", "conv_w", "g_attn_out", "g_conv_out", "w_out", "g_mix_pre", "g_mix_post",
             "w_gate_up", "w_down", "g_ffn_pre", "g_ffn_post"]
    outs = [loss, grad_x[None]]
    for k in range(4):
        outs += [res[n][k] for n in order]
    return tuple(outs)
```

```python
import functools

import numpy as np

import jax
import jax.numpy as jnp
from jax import lax
from jax.experimental import pallas as pl
from jax.experimental.pallas import tpu as pltpu

F32 = jnp.float32
BF16 = jnp.bfloat16
MESH_ID = pl.DeviceIdType.MESH

D = 1024
H = 8
DH = 64
AW = 512
CW = 512
DFF = 2816
FB = DFF // 4
HP = 128
OFF_Q, OFF_K, OFF_V, OFF_BCU, OFF_F = 0, 512, 1024, 1536, 3072
WP = OFF_F + 128
PIECES = ((OFF_Q, OFF_K), (OFF_K, OFF_V), (OFF_V, OFF_BCU), (OFF_BCU, OFF_F), (OFF_F, WP))
EPS = 1e-6
NDEV = 8
LANES = 128
SUBLANES = 8
IN_COLS = 385
IN_PAD = 512
IN_MAIN = 384
WIN = 640
ADAM_LR, ADAM_B1, ADAM_B2, ADAM_EPS, ADAM_WD, ADAM_STEP = 0.001, 0.9, 0.999, 1e-08, 0.01, 10

NT = (((1,), (1,)), ((), ()))
TN = (((0,), (0,)), ((), ()))


def _cparams(vmem_mb=None, sem=None):
    kw = {}
    if vmem_mb is not None:
        kw["vmem_limit_bytes"] = vmem_mb << 20
    if sem is not None:
        kw["dimension_semantics"] = sem
    return pltpu.CompilerParams(**kw)


def _full(shape):
    return pl.BlockSpec(shape, lambda *_: (0,) * len(shape))


def _resident(shape):
    return pl.BlockSpec(shape, lambda *_: (0,) * len(shape), pipeline_mode=pl.Buffered(1))


def _rows(tm, width):
    return pl.BlockSpec((tm, width), lambda i: (i, 0))


def _fold8(v):
    r, w = v.shape
    return jnp.sum(v.reshape(r // SUBLANES, SUBLANES, w), axis=0)


def _split_dot(v, m01):
    hi = v.astype(BF16)
    lo = (v - hi.astype(F32)).astype(BF16)
    return (jnp.dot(hi, m01, preferred_element_type=F32)
            + jnp.dot(lo, m01, preferred_element_type=F32))


GS = 256


def _group_sum(v, g01):
    parts = [_split_dot(v[:, c:c + GS], g01) for c in range(0, v.shape[1], GS)]
    return parts[0] if len(parts) == 1 else jnp.concatenate(parts, axis=1)


def _exact_dot01(m01, v):
    p1 = v.astype(BF16)
    r1 = v - p1.astype(F32)
    p2 = r1.astype(BF16)
    p3 = (r1 - p2.astype(F32)).astype(BF16)
    return (jnp.dot(m01, p1, preferred_element_type=F32) + jnp.dot(m01, p2, preferred_element_type=F32)
            + jnp.dot(m01, p3, preferred_element_type=F32))


def _rms_fwd(v, g):
    r = lax.rsqrt(jnp.mean(v * v, axis=-1, keepdims=True) + EPS)
    n = v * r
    return n * g, n, r


def _rms_bwd(do, n, r, g):
    dn = do * g
    return r * (dn - n * jnp.mean(dn * n, axis=-1, keepdims=True)), do * n


def _padded_column(n):
    if n < AW:
        return OFF_Q + n, 0.125
    if n < 3 * AW:
        return n, 1.0
    if n < 3 * AW + H:
        return OFF_F + n - 3 * AW, 1.0
    return OFF_BCU + n - 3 * AW - H, 1.0


def _in_layout_tables():
    dest = -np.ones((IN_PAD, LANES), np.int32)
    dest_f = -np.ones((IN_PAD, LANES), np.int32)
    scale = np.zeros((IN_PAD, LANES), np.float32)
    starts = []
    for k in range(NDEV):
        cols = [_padded_column(IN_COLS * k + j) for j in range(IN_COLS)]
        main = [c for c, _ in cols if c < OFF_F]
        ws = min((min(main) // LANES) * LANES, OFF_F - WIN)
        assert ws <= min(main) and max(main) < ws + WIN
        starts.append(ws)
        for j, (c, sc) in enumerate(cols):
            scale[j, k] = sc
            if c < OFF_F:
                dest[j, k] = c - ws
            else:
                dest_f[j, k] = c - OFF_F
    f_shards = tuple(k for k in range(NDEV) if (dest_f[:, k] >= 0).any())
    return tuple(starts), f_shards, jnp.asarray(dest), jnp.asarray(dest_f), jnp.asarray(scale)


def _perm(dest_ref, scale_ref, k, width, rows=IN_PAD):
    lane = lax.broadcasted_iota(jnp.int32, (rows, width), 1)
    return jnp.where(dest_ref[0:rows, k:k + 1] == lane, scale_ref[0:rows, k:k + 1], 0.0).astype(BF16)


def _assemble_w_in(blocks, last_cols, tables, *, tr):
    starts, f_shards, dest, dest_f, scale = tables
    last = [_padded_column(IN_COLS * k + IN_MAIN) for k in range(NDEV)]
    f_main = [any(_padded_column(IN_COLS * k + j)[0] >= OFF_F for j in range(IN_MAIN)) for k in range(NDEV)]
    assert IN_COLS == IN_MAIN + 1

    def body(b_ref, c_ref, dest_ref, destf_ref, scale_ref, o_ref):
        o_ref[...] = jnp.zeros_like(o_ref)
        lane = lax.broadcasted_iota(jnp.int32, (tr, LANES), 1)
        for k in range(NDEV):
            b = b_ref[k]
            ws = starts[k]
            part = jnp.dot(b, _perm(dest_ref, scale_ref, k, WIN, IN_MAIN), preferred_element_type=F32)
            o_ref[:, ws:ws + WIN] += part.astype(BF16)
            if f_main[k]:
                part = jnp.dot(b, _perm(destf_ref, scale_ref, k, 128, IN_MAIN), preferred_element_type=F32)
                o_ref[:, OFF_F:WP] += part.astype(BF16)
            col, sc = last[k]
            tile = (col // LANES) * LANES
            o_ref[:, tile:tile + LANES] += jnp.where(lane == col - tile, c_ref[:, k:k + 1] * sc, 0.0).astype(BF16)

    tab = _full((IN_PAD, LANES))
    return pl.pallas_call(
        body, name="assemble_w_in", grid=(D // tr,),
        in_specs=[pl.BlockSpec((NDEV, tr, IN_MAIN), lambda i: (0, i, 0)), _rows(tr, LANES), tab, tab, tab],
        out_specs=_rows(tr, WP),
        out_shape=jax.ShapeDtypeStruct((D, WP), BF16),
        compiler_params=_cparams(48, ("arbitrary",)),
    )(blocks, last_cols, dest, dest_f, scale)


def _disassemble_w_in(dwp, tables, *, tr):
    starts, f_shards, dest, dest_f, scale = tables
    width = dwp.shape[1]

    def body(g_ref, dest_ref, destf_ref, scale_ref, o_ref):
        for k in range(NDEV):
            ws = starts[k]
            acc = lax.dot_general(g_ref[:, ws:ws + WIN], _perm(dest_ref, scale_ref, k, WIN), NT, preferred_element_type=F32)
            if k in f_shards:
                acc = acc + lax.dot_general(g_ref[:, OFF_F:WP], _perm(destf_ref, scale_ref, k, 128), NT,
                                            preferred_element_type=F32)
            o_ref[k] = acc.astype(BF16)

    tab = _full((IN_PAD, LANES))
    return pl.pallas_call(
        body, name="disassemble_w_in", grid=(D // tr,),
        in_specs=[_rows(tr, width), tab, tab, tab],
        out_specs=pl.BlockSpec((NDEV, tr, IN_PAD), lambda i: (0, i, 0)),
        out_shape=jax.ShapeDtypeStruct((NDEV, D, IN_PAD), BF16),
        compiler_params=_cparams(48, ("arbitrary",)),
    )(dwp, dest, dest_f, scale)


def _in_proj(x, g1, wp, bfp, pq, pk, oq, ok, *, tm):
    s = x.shape[0]

    def body(x_ref, g_ref, w_ref, bf_ref, pq_ref, pk_ref, oq_ref, ok_ref,
             ht_ref, qp_ref, kp_ref, v_ref, bcu_ref, z_ref, carry):
        @pl.when(pl.program_id(0) == 0)
        def _():
            carry[...] = jnp.zeros_like(carry)

        h = _rms_fwd(x_ref[...], g_ref[...])[0].astype(BF16)
        ht_ref[...] = h.T
        z = jnp.dot(h, w_ref[:, OFF_F:WP], preferred_element_type=F32) + bf_ref[...]
        z_ref[...] = z
        lane = lax.broadcasted_iota(jnp.int32, (tm, 128), 1)
        logf = jnp.where(lane < H, jnp.minimum(z, 0.0) - jnp.log(1.0 + jnp.exp(-jnp.abs(z))), 0.0)
        row = lax.broadcasted_iota(jnp.int32, (tm, tm), 0)
        col = lax.broadcasted_iota(jnp.int32, (tm, tm), 1)
        tri = (col <= row).astype(BF16)
        c = _exact_dot01(tri, logf) + carry[0:1, :]
        carry[...] = jnp.broadcast_to(c[tm - 1:tm, :], carry.shape)
        c1 = c.astype(BF16).astype(F32)
        r1 = c - c1
        c2 = r1.astype(BF16).astype(F32)
        c3 = (r1 - c2).astype(BF16).astype(F32)
        zc = (c1 + pltpu.roll(c2, 8, axis=1) + pltpu.roll(c3, 16, axis=1)).astype(BF16)

        def pad_heads(v):
            blocks = []
            for pair in range(H // 2):
                two = v[:, 128 * pair:128 * (pair + 1)]
                blocks.append(jnp.where(lane < DH, two, 0.0))
                blocks.append(jnp.where(lane < DH, pltpu.roll(two, DH, axis=1), 0.0))
            return jnp.concatenate(blocks, axis=1)

        q = jnp.dot(h, w_ref[:, OFF_Q:OFF_K], preferred_element_type=F32)
        qp_ref[...] = (pad_heads(q) + jnp.dot(zc, pq_ref[...], preferred_element_type=F32) + oq_ref[...]).astype(BF16)
        k = jnp.dot(h, w_ref[:, OFF_K:OFF_V], preferred_element_type=F32)
        kp_ref[...] = (pad_heads(k) + jnp.dot(zc, pk_ref[...], preferred_element_type=F32) + ok_ref[...]).astype(BF16)
        v = pad_heads(jnp.dot(h, w_ref[:, OFF_V:OFF_BCU], preferred_element_type=F32))
        ones_lane = lax.broadcasted_iota(jnp.int32, (tm, H * HP), 1) % HP == DH
        v_ref[...] = jnp.where(ones_lane, 1.0, v).astype(BF16)
        bcu_ref[...] = jnp.dot(h, w_ref[:, OFF_BCU:OFF_F], preferred_element_type=F32).astype(BF16)

    return pl.pallas_call(
        body, name="in_proj", grid=(s // tm,),
        in_specs=[_rows(tm, D), _full((1, D)), _resident((D, WP)), _full((1, 128)),
                  _full((128, 1024)), _full((128, 1024)), _full((1, 1024)), _full((1, 1024))],
        out_specs=[pl.BlockSpec((D, tm), lambda i: (0, i)), _rows(tm, 1024), _rows(tm, 1024), _rows(tm, 1024),
                   _rows(tm, 3 * CW), _rows(tm, 128)],
        out_shape=[jax.ShapeDtypeStruct((D, s), BF16), jax.ShapeDtypeStruct((s, 1024), BF16),
                   jax.ShapeDtypeStruct((s, 1024), BF16), jax.ShapeDtypeStruct((s, 1024), BF16),
                   jax.ShapeDtypeStruct((s, 3 * CW), BF16), jax.ShapeDtypeStruct((s, 128), F32)],
        scratch_shapes=[pltpu.VMEM((SUBLANES, 128), F32)],
        compiler_params=_cparams(56, ("arbitrary",)),
    )(x, g1, wp, bfp, pq, pk, oq, ok)


def _attn_fwd(qp, kp, v, *, t):
    s = qp.shape[0]
    nq = s // t

    def body(q_ref, k_ref, v_ref, o_ref, lse_ref, mk_ref):
        qi = pl.program_id(1)
        row = lax.broadcasted_iota(jnp.int32, (t, t), 0)
        col = lax.broadcasted_iota(jnp.int32, (t, t), 1)
        lane = lax.broadcasted_iota(jnp.int32, (t, 128), 1)

        def head_step(hh, ki, carry, masked):
            m, acc = carry
            off = pl.multiple_of(ki * t, t)
            q = q_ref[:, HP * hh:HP * (hh + 1)]
            k = k_ref[pl.ds(off, t), HP * hh:HP * (hh + 1)]
            sc = lax.dot_general(q, k, NT, preferred_element_type=F32)
            if masked:
                sc = jnp.where(col <= row, sc, -1e30)
            mn = jnp.maximum(m, jnp.max(sc, axis=-1, keepdims=True))
            p = jnp.exp(sc - mn).astype(BF16)
            acc = jnp.exp(m - mn) * acc + jnp.dot(p, v_ref[pl.ds(off, t), HP * hh:HP * (hh + 1)],
                                                  preferred_element_type=F32)
            return mn, acc

        def step(ki, carry, masked):
            new = tuple(head_step(hh, ki, carry[hh], masked) for hh in range(2))
            mk_ref[ki] = jnp.where(lane < DH, jnp.broadcast_to(new[0][0], (t, 128)), jnp.broadcast_to(new[1][0], (t, 128)))
            return new

        init = (jnp.full((t, 1), -1e30, F32), jnp.zeros((t, 128), F32))
        carry = lax.fori_loop(0, qi, functools.partial(step, masked=False), (init, init))
        (m0, acc0), (m1, acc1) = step(qi, carry, True)
        l0, l1 = acc0[:, DH:DH + 1], acc1[:, DH:DH + 1]
        o_ref[...] = jnp.where(lane < DH, acc0 / l0, pltpu.roll(acc1 / l1, DH, axis=1))
        lse_ref[...] = jnp.where(lane < DH, jnp.broadcast_to(m0 + jnp.log(l0), (t, 128)),
                                 jnp.broadcast_to(m1 + jnp.log(l1), (t, 128)))

    return pl.pallas_call(
        body, name="attn_fwd", grid=(H // 2, nq),
        in_specs=[pl.BlockSpec((t, 2 * HP), lambda p, i: (i, p)),
                  pl.BlockSpec((s, 2 * HP), lambda p, i: (0, p)),
                  pl.BlockSpec((s, 2 * HP), lambda p, i: (0, p))],
        out_specs=[pl.BlockSpec((t, 128), lambda p, i: (i, p)), pl.BlockSpec((t, 128), lambda p, i: (i, p)),
                   pl.BlockSpec((nq, t, 128), lambda p, i: (0, i, p))],
        out_shape=[jax.ShapeDtypeStruct((s, AW), F32), jax.ShapeDtypeStruct((s, AW), F32),
                   jax.ShapeDtypeStruct((nq, s, AW), F32)],
        compiler_params=_cparams(48, ("arbitrary", "arbitrary")),
    )(qp, kp, v)


HALO = 16


def _conv_taps(bcu_ref, halo_ref, first, tm):
    z = bcu_ref[:, CW:2 * CW].astype(F32) * bcu_ref[:, 2 * CW:3 * CW].astype(F32)
    zh = jnp.where(first, 0.0, halo_ref[:, CW:2 * CW].astype(F32) * halo_ref[:, 2 * CW:3 * CW].astype(F32))
    row = lax.broadcasted_iota(jnp.int32, (tm, CW), 0)
    last, before = zh[HALO - 1:HALO, :], zh[HALO - 2:HALO - 1, :]
    z1 = jnp.where(row == 0, last, pltpu.roll(z, 1, axis=0))
    z2 = jnp.where(row == 0, before, jnp.where(row == 1, last, pltpu.roll(z, 2, axis=0)))
    return z, z1, z2


def _halo_before(tm, width):
    return pl.BlockSpec((HALO, width), lambda i: (jnp.maximum(i * (tm // HALO) - 1, 0), 0))


def _mix_out(o, bcu, cw8, ga, gc, gsum, w_out, x, g_post, g_ffn_pre, *, tm):
    s = x.shape[0]

    def body(o_ref, bcu_ref, halo_ref, cw_ref, ga_ref, gc_ref, gs_ref, w_ref, x_ref, g_ref, gf_ref,
             merged_ref, y_ref, x2_ref, cv_ref, h2_ref):
        z, z1, z2 = _conv_taps(bcu_ref, halo_ref, pl.program_id(0) == 0, tm)
        cv = cw_ref[0:1, :] * z2 + cw_ref[1:2, :] * z1 + cw_ref[2:3, :] * z
        cv_ref[...] = cv
        conv = bcu_ref[:, 0:CW].astype(F32) * cv
        ov = o_ref[...]
        ra = lax.rsqrt(_group_sum(ov * ov, gs_ref[...]) * (1.0 / DH) + EPS)
        rc = lax.rsqrt(_group_sum(conv * conv, gs_ref[...]) * (1.0 / DH) + EPS)
        merged = jnp.concatenate([ov * ra * ga_ref[...], conv * rc * gc_ref[...]], axis=1).astype(BF16)
        merged_ref[...] = merged
        y = jnp.dot(merged, w_ref[...], preferred_element_type=F32)
        y_ref[...] = y
        x2 = x_ref[...] + _rms_fwd(y, g_ref[...])[0]
        x2_ref[...] = x2
        h2_ref[...] = _rms_fwd(x2, gf_ref[...])[0].astype(BF16)

    return pl.pallas_call(
        body, name="mix_out", grid=(s // tm,),
        in_specs=[_rows(tm, AW), _rows(tm, 3 * CW), _halo_before(tm, 3 * CW), _full((SUBLANES, CW)),
                  _full((1, AW)), _full((1, CW)), _full((GS, GS)), _resident((D, D)), _rows(tm, D), _full((1, D)),
                  _full((1, D))],
        out_specs=[_rows(tm, D), _rows(tm, D), _rows(tm, D), _rows(tm, CW), _rows(tm, D)],
        out_shape=[jax.ShapeDtypeStruct((s, D), BF16), jax.ShapeDtypeStruct((s, D), F32),
                   jax.ShapeDtypeStruct((s, D), F32), jax.ShapeDtypeStruct((s, CW), F32),
                   jax.ShapeDtypeStruct((s, D), BF16)],
        compiler_params=_cparams(48, ("arbitrary",)),
    )(o, bcu, bcu, cw8, ga, gc, gsum, w_out, x, g_post, g_ffn_pre)


def _ffn_fwd_loss(h2, wgu, wd, x2, target, g_post, *, tm):
    s = x2.shape[0]

    def body(h_ref, w_ref, wd_ref, x2_ref, t_ref, g_ref,
             gate_ref, up_ref, a_ref, dx3_ref, dff_ref, loss_ref, dg_ref):
        @pl.when(pl.program_id(0) == 0)
        def _():
            loss_ref[...] = jnp.zeros_like(loss_ref)
            dg_ref[...] = jnp.zeros_like(dg_ref)

        h = h_ref[...]
        ff = None
        for j in range(4):
            gate = lax.dot_general(h, w_ref[0, j], NT, preferred_element_type=F32)
            up = lax.dot_general(h, w_ref[1, j], NT, preferred_element_type=F32)
            gate_ref[j] = gate.astype(BF16)
            up_ref[j] = up.astype(BF16)
            act = (gate * jax.nn.sigmoid(gate) * up).astype(BF16)
            a_ref[j] = act
            part = jnp.dot(act, wd_ref[j], preferred_element_type=F32)
            ff = part if ff is None else ff + part
        out, n, r = _rms_fwd(ff, g_ref[...])
        e = x2_ref[...] + out - t_ref[...]
        loss_ref[...] += _fold8(e * e)
        dx3 = e * (1.0 / D)
        dx3_ref[...] = dx3
        dff, dg = _rms_bwd(dx3, n, r, g_ref[...])
        dff_ref[...] = dff.astype(BF16)
        dg_ref[...] += _fold8(dg)

    blk4 = pl.BlockSpec((4, tm, FB), lambda i: (0, i, 0))
    return pl.pallas_call(
        body, name="ffn_fwd_loss", grid=(s // tm,),
        in_specs=[_rows(tm, D), _resident((2, 4, FB, D)), _resident((4, FB, D)), _rows(tm, D), _rows(tm, D), _full((1, D))],
        out_specs=[blk4, blk4, blk4, _rows(tm, D), _rows(tm, D), _full((SUBLANES, D)), _full((SUBLANES, D))],
        out_shape=[jax.ShapeDtypeStruct((4, s, FB), BF16)] * 3
        + [jax.ShapeDtypeStruct((s, D), F32), jax.ShapeDtypeStruct((s, D), BF16),
           jax.ShapeDtypeStruct((SUBLANES, D), F32), jax.ShapeDtypeStruct((SUBLANES, D), F32)],
        compiler_params=_cparams(56, ("arbitrary",)),
    )(h2, wgu, wd, x2, target, g_post)


def _ffn_bwd(dff, wd, gate, up, wgu, x2, g_pre, dx3, y, g_post, *, tm):
    s = x2.shape[0]

    def body(dff_ref, wd_ref, gate_ref, up_ref, w_ref, x2_ref, gpre_ref, dx3_ref, y_ref, gpost_ref,
             dgu_ref, dx2_ref, dy_ref, dgpre_ref, dgpost_ref):
        @pl.when(pl.program_id(0) == 0)
        def _():
            dgpre_ref[...] = jnp.zeros_like(dgpre_ref)
            dgpost_ref[...] = jnp.zeros_like(dgpost_ref)

        dff = dff_ref[...]
        dh2 = None
        for j in range(4):
            da = lax.dot_general(dff, wd_ref[j], NT, preferred_element_type=F32)
            g = gate_ref[j].astype(F32)
            sg = jax.nn.sigmoid(g)
            dgate = (da * up_ref[j].astype(F32) * (sg * (1.0 + g * (1.0 - sg)))).astype(BF16)
            dup = (da * (g * sg)).astype(BF16)
            dgu_ref[0, j] = dgate
            dgu_ref[1, j] = dup
            part = (jnp.dot(dgate, w_ref[0, j], preferred_element_type=F32)
                    + jnp.dot(dup, w_ref[1, j], preferred_element_type=F32))
            dh2 = part if dh2 is None else dh2 + part
        _, n2, r2 = _rms_fwd(x2_ref[...], gpre_ref[...])
        dxn, dg = _rms_bwd(dh2, n2, r2, gpre_ref[...])
        dgpre_ref[...] += _fold8(dg)
        dx2 = dx3_ref[...] + dxn
        dx2_ref[...] = dx2
        _, ny, ry = _rms_fwd(y_ref[...], gpost_ref[...])
        dy, dg2 = _rms_bwd(dx2, ny, ry, gpost_ref[...])
        dy_ref[...] = dy.astype(BF16)
        dgpost_ref[...] += _fold8(dg2)

    blk4 = pl.BlockSpec((4, tm, FB), lambda i: (0, i, 0))
    return pl.pallas_call(
        body, name="ffn_bwd", grid=(s // tm,),
        in_specs=[_rows(tm, D), _resident((4, FB, D)), blk4, blk4, _resident((2, 4, FB, D)), _rows(tm, D), _full((1, D)),
                  _rows(tm, D), _rows(tm, D), _full((1, D))],
        out_specs=[pl.BlockSpec((2, 4, tm, FB), lambda i: (0, 0, i, 0)), _rows(tm, D), _rows(tm, D),
                   _full((SUBLANES, D)), _full((SUBLANES, D))],
        out_shape=[jax.ShapeDtypeStruct((2, 4, s, FB), BF16), jax.ShapeDtypeStruct((s, D), F32),
                   jax.ShapeDtypeStruct((s, D), BF16), jax.ShapeDtypeStruct((SUBLANES, D), F32),
                   jax.ShapeDtypeStruct((SUBLANES, D), F32)],
        compiler_params=_cparams(56, ("arbitrary",)),
    )(dff, wd, gate, up, wgu, x2, g_pre, dx3, y, g_post)


def _grad_matmul(a, b, *, ta, tb, ts, name):
    s, ka = a.shape
    nb = b.shape[1]
    ts = min(ts, s)
    nk = s // ts

    def body(a_ref, b_ref, o_ref, acc):
        k = pl.program_id(2)

        @pl.when(k == 0)
        def _():
            acc[...] = jnp.zeros_like(acc)

        acc[...] += lax.dot_general(a_ref[...], b_ref[...], TN, preferred_element_type=F32)

        @pl.when(k == nk - 1)
        def _():
            o_ref[...] = acc[...].astype(BF16)

    return pl.pallas_call(
        body, name=name, grid=(ka // ta, nb // tb, nk),
        in_specs=[pl.BlockSpec((ts, ta), lambda i, j, k: (k, i)), pl.BlockSpec((ts, tb), lambda i, j, k: (k, j))],
        out_specs=pl.BlockSpec((ta, tb), lambda i, j, k: (i, j)),
        out_shape=jax.ShapeDtypeStruct((ka, nb), BF16),
        scratch_shapes=[pltpu.VMEM((ta, tb), F32)],
        compiler_params=_cparams(48, ("arbitrary", "arbitrary", "arbitrary")),
    )(a, b)


GW_TILE = 256


def _grad_w_in(h1t, pieces):
    ka, s = h1t.shape
    widths = [p.shape[1] for p in pieces]
    assert all(w % GW_TILE == 0 for w in widths)
    first = [sum(widths[:i]) // GW_TILE for i in range(len(pieces))]
    count = [w // GW_TILE for w in widths]

    def body(a_ref, *refs):
        o_ref = refs[-1]
        j = pl.program_id(0)
        for ref, f0, n in zip(refs[:-1], first, count):
            @pl.when((j >= f0) & (j < f0 + n))
            def _(ref=ref):
                o_ref[...] = jnp.dot(a_ref[...], ref[...], preferred_element_type=F32).astype(BF16)

    def spec(f0, n):
        return pl.BlockSpec((s, GW_TILE), lambda j: (0, jnp.clip(j - f0, 0, n - 1)))

    return pl.pallas_call(
        body, name="grad_w_in", grid=(sum(count),),
        in_specs=[_resident((ka, s))] + [spec(f0, n) for f0, n in zip(first, count)],
        out_specs=pl.BlockSpec((ka, GW_TILE), lambda j: (0, j)),
        out_shape=jax.ShapeDtypeStruct((ka, sum(widths)), BF16),
        compiler_params=_cparams(56, ("arbitrary",)),
    )(h1t, *pieces)


def _grad_matmul_blocks(a, b, *, ts, name):
    nblk = a.shape[0] if a.ndim == 3 else b.shape[0]
    s = a.shape[-2]
    ka, nb = a.shape[-1], b.shape[-1]
    ts = min(ts, s)
    nk = s // ts

    def body(a_ref, b_ref, o_ref, acc):
        k = pl.program_id(1)

        @pl.when(k == 0)
        def _():
            acc[...] = jnp.zeros_like(acc)

        av = a_ref[0] if a.ndim == 3 else a_ref[...]
        bv = b_ref[0] if b.ndim == 3 else b_ref[...]
        acc[...] += lax.dot_general(av, bv, TN, preferred_element_type=F32)

        @pl.when(k == nk - 1)
        def _():
            o_ref[0] = acc[...].astype(BF16)

    def spec(arr, width):
        if arr.ndim == 3:
            return pl.BlockSpec((1, ts, width), lambda j, k: (j, k, 0))
        return pl.BlockSpec((ts, width), lambda j, k: (k, 0))

    return pl.pallas_call(
        body, name=name, grid=(nblk, nk),
        in_specs=[spec(a, ka), spec(b, nb)],
        out_specs=pl.BlockSpec((1, ka, nb), lambda j, k: (j, 0, 0)),
        out_shape=jax.ShapeDtypeStruct((nblk, ka, nb), BF16),
        scratch_shapes=[pltpu.VMEM((ka, nb), F32)],
        compiler_params=_cparams(48, ("arbitrary", "arbitrary")),
    )(a, b)


def _mix_bwd(dy, w_out, o, cv, bcu, ga, gc, gsum, *, tm):
    s = dy.shape[0]

    def group_norm_bwd(dn_out, v, g, gs):
        r = lax.rsqrt(_group_sum(v * v, gs) * (1.0 / DH) + EPS)
        n = v * r
        dn = dn_out * g
        return r * (dn - n * (_group_sum(dn * n, gs) * (1.0 / DH))), dn_out * n

    def body(dy_ref, w_ref, o_ref, cv_ref, bcu_ref, ga_ref, gc_ref, gs_ref,
             do_ref, dl_ref, dcv_ref, db_ref, dga_ref, dgc_ref):
        @pl.when(pl.program_id(0) == 0)
        def _():
            dga_ref[...] = jnp.zeros_like(dga_ref)
            dgc_ref[...] = jnp.zeros_like(dgc_ref)

        dm = lax.dot_general(dy_ref[...], w_ref[...], NT, preferred_element_type=F32)
        ov = o_ref[...]
        do, dga = group_norm_bwd(dm[:, 0:AW], ov, ga_ref[...], gs_ref[...])
        dob = do.astype(BF16)
        do_ref[...] = dob
        dl_ref[...] = _group_sum(dob.astype(F32) * ov, gs_ref[...])
        dga_ref[...] += _fold8(dga)
        gate_b = bcu_ref[:, 0:CW].astype(F32)
        cv = cv_ref[...]
        dconv, dgc = group_norm_bwd(dm[:, AW:D], gate_b * cv, gc_ref[...], gs_ref[...])
        dgc_ref[...] += _fold8(dgc)
        dcv_ref[...] = dconv * gate_b
        db_ref[...] = (dconv * cv).astype(BF16)

    return pl.pallas_call(
        body, name="mix_bwd", grid=(s // tm,),
        in_specs=[_rows(tm, D), _resident((D, D)), _rows(tm, AW), _rows(tm, CW), _rows(tm, 3 * CW),
                  _full((1, AW)), _full((1, CW)), _full((GS, GS))],
        out_specs=[_rows(tm, AW), _rows(tm, AW), _rows(tm, CW), _rows(tm, CW),
                   _full((SUBLANES, AW)), _full((SUBLANES, CW))],
        out_shape=[jax.ShapeDtypeStruct((s, AW), BF16), jax.ShapeDtypeStruct((s, AW), F32),
                   jax.ShapeDtypeStruct((s, CW), F32), jax.ShapeDtypeStruct((s, CW), BF16),
                   jax.ShapeDtypeStruct((SUBLANES, AW), F32), jax.ShapeDtypeStruct((SUBLANES, CW), F32)],
        compiler_params=_cparams(48, ("arbitrary",)),
    )(dy, w_out, o, cv, bcu, ga, gc, gsum)


def _conv_bwd(dcv, db, bcu, cw8, *, tm):
    s = dcv.shape[0]
    nt = s // tm

    def body(dcv_ref, nxt_ref, db_ref, bcu_ref, halo_ref, cw_ref, dbcu_ref, dw_ref):
        i = pl.program_id(0)

        @pl.when(i == 0)
        def _():
            dw_ref[...] = jnp.zeros_like(dw_ref)

        z, z1, z2 = _conv_taps(bcu_ref, halo_ref, i == 0, tm)
        d = dcv_ref[...]
        dw_ref[0] += _fold8(d * z2)
        dw_ref[1] += _fold8(d * z1)
        dw_ref[2] += _fold8(d * z)
        nx = jnp.where(i == nt - 1, 0.0, nxt_ref[...])
        row = lax.broadcasted_iota(jnp.int32, (tm, CW), 0)
        d1 = jnp.where(row == tm - 1, nx[0:1, :], pltpu.roll(d, tm - 1, axis=0))
        d2 = jnp.where(row == tm - 2, nx[0:1, :], jnp.where(row == tm - 1, nx[1:2, :], pltpu.roll(d, tm - 2, axis=0)))
        dz = cw_ref[2:3, :] * d + cw_ref[1:2, :] * d1 + cw_ref[0:1, :] * d2
        dbcu_ref[:, 0:CW] = db_ref[...]
        dbcu_ref[:, CW:2 * CW] = (dz * bcu_ref[:, 2 * CW:3 * CW].astype(F32)).astype(BF16)
        dbcu_ref[:, 2 * CW:3 * CW] = (dz * bcu_ref[:, CW:2 * CW].astype(F32)).astype(BF16)

    return pl.pallas_call(
        body, name="conv_bwd", grid=(nt,),
        in_specs=[_rows(tm, CW),
                  pl.BlockSpec((SUBLANES, CW), lambda i: (jnp.minimum((i + 1) * (tm // SUBLANES), s // SUBLANES - 1), 0)),
                  _rows(tm, CW), _rows(tm, 3 * CW), _halo_before(tm, 3 * CW), _full((SUBLANES, CW))],
        out_specs=[_rows(tm, 3 * CW), _full((3, SUBLANES, CW))],
        out_shape=[jax.ShapeDtypeStruct((s, 3 * CW), BF16), jax.ShapeDtypeStruct((3, SUBLANES, CW), F32)],
        compiler_params=_cparams(48, ("arbitrary",)),
    )(dcv, dcv, db, bcu, bcu, cw8)


def _attn_bwd(qp, kp, v, do, lse, dl, mk, *, t):
    s = qp.shape[0]
    nq = s // t

    def body(q_ref, k_ref, v_ref, do_ref, lse_ref, dl_ref, mk_ref, dq_ref, dk_ref, dv_ref, dkx_ref, dq_acc):
        ki = pl.program_id(1)

        @pl.when(ki == 0)
        def _():
            dq_acc[...] = jnp.zeros_like(dq_acc)

        row = lax.broadcasted_iota(jnp.int32, (t, t), 0)
        col = lax.broadcasted_iota(jnp.int32, (t, t), 1)
        lane = lax.broadcasted_iota(jnp.int32, (t, 128), 1)

        def head_step(hh, qi, carry, masked):
            dk, dv, cs = carry
            off = pl.multiple_of(qi * t, t)
            rows = pl.ds(off, t)
            kh = k_ref[:, HP * hh:HP * (hh + 1)]
            q = q_ref[rows, HP * hh:HP * (hh + 1)]
            m_col = mk_ref[0, rows, DH * hh:DH * hh + 1]
            scale = jnp.exp(m_col - lse_ref[rows, DH * hh:DH * hh + 1])
            do2 = do_ref[rows, :]
            dom = jnp.where(lane < DH, do2 if hh == 0 else pltpu.roll(do2, DH, axis=1), jnp.zeros((), BF16))
            sc = lax.dot_general(q, kh, NT, preferred_element_type=F32) - m_col
            if masked:
                sc = jnp.where(col <= row, sc, -1e30)
            pt = jnp.exp(sc).astype(BF16)
            dp = lax.dot_general(dom, v_ref[:, HP * hh:HP * (hh + 1)], NT, preferred_element_type=F32)
            ds32 = (pt.astype(F32) * scale) * (dp - dl_ref[rows, DH * hh:DH * hh + 1])
            ds = ds32.astype(BF16)
            cs = cs + _fold8(ds32)
            dv = dv + jnp.dot((dom.astype(F32) * scale).astype(BF16).T, pt, preferred_element_type=F32)
            dk = dk + jnp.dot(q.T, ds, preferred_element_type=F32)
            dq_acc[rows, HP * hh:HP * (hh + 1)] += jnp.dot(ds, kh, preferred_element_type=F32)
            return dk, dv, cs

        def step(qi, carry, masked):
            return tuple(head_step(hh, qi, carry[hh], masked) for hh in range(2))

        zero = (jnp.zeros((HP, t), F32), jnp.zeros((128, t), F32), jnp.zeros((SUBLANES, t), F32))
        carry = step(ki, (zero, zero), True)
        (dk0, dv0, cs0), (dk1, dv1, cs1) = lax.fori_loop(ki + 1, nq, functools.partial(step, masked=False), carry)
        def two_heads(a0, a1):
            return jnp.where(lane < DH, a0, pltpu.roll(a1, DH, axis=1))

        def rows_to_lanes(a0, a1):
            return jnp.concatenate([a0, a1], axis=0).T

        dk_ref[...] = rows_to_lanes(dk0[0:DH], dk1[0:DH]).astype(BF16)
        dv_ref[...] = rows_to_lanes(dv0[0:DH], dv1[0:DH]).astype(BF16)
        total = lambda cs: jnp.broadcast_to(jnp.sum(cs, axis=0, keepdims=True), (DH, t))
        dkx_ref[...] = rows_to_lanes(total(cs0), total(cs1))

        @pl.when(ki == nq - 1)
        def _():
            for c in range(s // t):
                rows = slice(c * t, (c + 1) * t)
                dq_ref[rows, :] = two_heads(dq_acc[rows, 0:HP], dq_acc[rows, HP:2 * HP]).astype(BF16)

    return pl.pallas_call(
        body, name="attn_bwd", grid=(H // 2, nq),
        in_specs=[pl.BlockSpec((s, 2 * HP), lambda p, i: (0, p)),
                  pl.BlockSpec((t, 2 * HP), lambda p, i: (i, p)),
                  pl.BlockSpec((t, 2 * HP), lambda p, i: (i, p)),
                  pl.BlockSpec((s, 128), lambda p, i: (0, p)),
                  pl.BlockSpec((s, 128), lambda p, i: (0, p)),
                  pl.BlockSpec((s, 128), lambda p, i: (0, p)),
                  pl.BlockSpec((1, s, 128), lambda p, i: (i, 0, p))],
        out_specs=[pl.BlockSpec((s, 128), lambda p, i: (0, p)),
                   pl.BlockSpec((t, 128), lambda p, i: (i, p)),
                   pl.BlockSpec((t, 128), lambda p, i: (i, p)),
                   pl.BlockSpec((t, 128), lambda p, i: (i, p))],
        out_shape=[jax.ShapeDtypeStruct((s, AW), BF16), jax.ShapeDtypeStruct((s, AW), BF16),
                   jax.ShapeDtypeStruct((s, AW), BF16), jax.ShapeDtypeStruct((s, AW), F32)],
        scratch_shapes=[pltpu.VMEM((s, 2 * HP), F32)],
        compiler_params=_cparams(56, ("arbitrary", "arbitrary")),
    )(qp, kp, v, do, lse, dl, mk)


def _forget_bwd(dkx, z, sel, *, tm):
    s = dkx.shape[0]
    nt = s // tm

    def body(dk_ref, z_ref, sel_ref, dfl_ref, dbf_ref, carry):
        @pl.when(pl.program_id(0) == 0)
        def _():
            carry[...] = jnp.zeros_like(carry)
            dbf_ref[...] = jnp.zeros_like(dbf_ref)

        dc = _split_dot(dk_ref[...], sel_ref[...])
        row = lax.broadcasted_iota(jnp.int32, (tm, tm), 0)
        col = lax.broadcasted_iota(jnp.int32, (tm, tm), 1)
        tri = (col >= row).astype(BF16)
        dlogf = _exact_dot01(tri, dc) + carry[0:1, :]
        carry[...] = jnp.broadcast_to(dlogf[0:1, :], carry.shape)
        dz = dlogf * (1.0 - jax.nn.sigmoid(z_ref[...]))
        dfl_ref[:, 0:128] = dz.astype(BF16)
        dfl_ref[:, 128:GW_TILE] = jnp.zeros((tm, GW_TILE - 128), BF16)
        dbf_ref[...] += _fold8(dz)

    rev = lambda i: (nt - 1 - i, 0)
    return pl.pallas_call(
        body, name="forget_bwd", grid=(nt,),
        in_specs=[pl.BlockSpec((tm, AW), rev), pl.BlockSpec((tm, 128), rev), _full((AW, 128))],
        out_specs=[pl.BlockSpec((tm, GW_TILE), rev), _full((SUBLANES, 128))],
        out_shape=[jax.ShapeDtypeStruct((s, GW_TILE), BF16), jax.ShapeDtypeStruct((SUBLANES, 128), F32)],
        scratch_shapes=[pltpu.VMEM((SUBLANES, 128), F32)],
        compiler_params=_cparams(48, ("arbitrary",)),
    )(dkx, z, sel)


def _in_proj_bwd(pieces, wp, x, g1, dx2, *, tm):
    s = x.shape[0]

    def body(q_ref, k_ref, v_ref, bcu_ref, f_ref, w_ref, x_ref, g_ref, dx2_ref, dx_ref, dg_ref):
        @pl.when(pl.program_id(0) == 0)
        def _():
            dg_ref[...] = jnp.zeros_like(dg_ref)

        dh = None
        for ref, (lo, hi) in zip((q_ref, k_ref, v_ref, bcu_ref, f_ref), PIECES):
            part = lax.dot_general(ref[...], w_ref[:, lo:hi], NT, preferred_element_type=F32)
            dh = part if dh is None else dh + part
        _, n, r = _rms_fwd(x_ref[...], g_ref[...])
        dxn, dg = _rms_bwd(dh, n, r, g_ref[...])
        dx_ref[...] = dx2_ref[...] + dxn
        dg_ref[...] += _fold8(dg)

    return pl.pallas_call(
        body, name="in_proj_bwd", grid=(s // tm,),
        in_specs=[_rows(tm, hi - lo) for lo, hi in PIECES] + [_resident((D, WP)), _rows(tm, D), _full((1, D)), _rows(tm, D)],
        out_specs=[_rows(tm, D), _full((SUBLANES, D))],
        out_shape=[jax.ShapeDtypeStruct((s, D), F32), jax.ShapeDtypeStruct((SUBLANES, D), F32)],
        compiler_params=_cparams(56, ("arbitrary",)),
    )(*pieces, wp, x, g1, dx2)


def _position():
    return lax.axis_index("x"), lax.axis_index("y"), lax.axis_index("c")


ANY = pl.BlockSpec(memory_space=pl.ANY)


def _all_gather(shards):
    n = len(shards)

    def body(*refs):
        x_refs, out_refs = refs[:n], refs[n:2 * n]
        send_sems, recv_sems, local_sems = refs[2 * n:]
        x, y, c = _position()
        me, sibling = (x, y, c), (x, y, 1 - c)
        chips = [(1 - x, y), (x, 1 - y), (1 - x, 1 - y)]

        def copy(a, k, block, to, own=False):
            slot = out_refs[a].at[4 * block[0] + 2 * block[1] + block[2]]
            return pltpu.make_async_remote_copy(
                src_ref=x_refs[a] if own else slot, dst_ref=slot,
                send_sem=send_sems.at[7 * a + k], recv_sem=recv_sems.at[7 * a + k], device_id=to, device_id_type=MESH_ID)

        mine = [pltpu.make_async_copy(x_refs[a], out_refs[a].at[4 * x + 2 * y + c], local_sems.at[a]) for a in range(n)]
        for cp in mine:
            cp.start()
        first = []
        for a in range(n):
            first.append(copy(a, 0, me, sibling, own=True))
            first += [copy(a, 1 + j, me, (*chip, c), own=True) for j, chip in enumerate(chips)]
        for cp in first:
            cp.start()
        passed = []
        for j, chip in enumerate(chips):
            for a in range(n):
                copy(a, 1 + j, (*chip, c), me).wait_recv()
                fwd = copy(a, 4 + j, (*chip, c), sibling)
                fwd.start()
                passed.append(fwd)
        for a in range(n):
            copy(a, 0, sibling, me).wait_recv()
            for j, chip in enumerate(chips):
                copy(a, 4 + j, (*chip, 1 - c), me).wait_recv()
        for cp in first + passed:
            cp.wait_send()
        for cp in mine:
            cp.wait()

    return pl.pallas_call(
        body, name="all_gather_weights",
        out_shape=[jax.ShapeDtypeStruct((NDEV,) + sh.shape, sh.dtype) for sh in shards],
        in_specs=[ANY] * n, out_specs=[ANY] * n,
        scratch_shapes=[pltpu.SemaphoreType.DMA((7 * n,)), pltpu.SemaphoreType.DMA((7 * n,)), pltpu.SemaphoreType.DMA((n,))],
    )(*shards)


def _pair_exchange(grads):
    n = len(grads)

    def body(*refs):
        g_refs, out_refs = refs[:n], refs[n:2 * n]
        send_sems, recv_sems = refs[2 * n:]
        x, y, c = _position()
        copies = [pltpu.make_async_remote_copy(
            src_ref=g_refs[a].at[:, pl.ds(1 - c, 1)], dst_ref=out_refs[a], send_sem=send_sems.at[a],
            recv_sem=recv_sems.at[a], device_id=(x, y, 1 - c), device_id_type=MESH_ID) for a in range(n)]
        for cp in copies:
            cp.start()
        for cp in copies:
            cp.wait()

    return pl.pallas_call(
        body, name="grad_pair_exchange",
        out_shape=[jax.ShapeDtypeStruct((4, 1) + g.shape[2:], g.dtype) for g in grads],
        in_specs=[ANY] * n, out_specs=[ANY] * n,
        scratch_shapes=[pltpu.SemaphoreType.DMA((n,)), pltpu.SemaphoreType.DMA((n,))],
    )(*grads)


def _pair_sum(g, got, idx, *, tr, name):
    r, c = g.shape[2:]

    def body(idx_ref, g_ref, got_ref, pb_ref, own_ref):
        p = g_ref[0, 0].astype(F32) + got_ref[0, 0].astype(F32)
        pb_ref[0] = p.astype(BF16)

        @pl.when(pl.program_id(1) == idx_ref[1])
        def _():
            own_ref[...] = p

    return pl.pallas_call(
        body, name=name,
        grid_spec=pltpu.PrefetchScalarGridSpec(
            num_scalar_prefetch=1, grid=(r // tr, 4),
            in_specs=[pl.BlockSpec((1, 1, tr, c), lambda i, j, idx: (j, idx[0], i, 0)),
                      pl.BlockSpec((1, 1, tr, c), lambda i, j, idx: (j, 0, i, 0))],
            out_specs=[pl.BlockSpec((1, tr, c), lambda i, j, idx: (j, i, 0)),
                       pl.BlockSpec((tr, c), lambda i, j, idx: (i, 0))]),
        out_shape=[jax.ShapeDtypeStruct((4, r, c), BF16), jax.ShapeDtypeStruct((r, c), F32)],
        compiler_params=_cparams(32, ("arbitrary", "arbitrary")),
    )(idx, g, got)


HBM = pl.BlockSpec(memory_space=pltpu.HBM)
SEM = pl.BlockSpec(memory_space=pltpu.SEMAPHORE)
DATAFLOW = pltpu.SideEffectType.DATAFLOW_SIDE_EFFECTING


PEERS = {"gather": NDEV - 1, "scatter": NDEV - 1, "chips": 3}


def _exchange_copies(src_refs, land_refs, send_sems, recv_sems, mode):
    x, y, c = _position()
    me, my_chip = 4 * x + 2 * y + c, 2 * x + y
    npeers = PEERS[mode]
    copies = []
    for a, (s_ref, l_ref) in enumerate(zip(src_refs, land_refs)):
        for k in range(npeers):
            if mode == "chips":
                px, py, pc = x ^ ((k + 1) >> 1), y ^ ((k + 1) & 1), c
                src, dst = s_ref.at[2 * px + py], l_ref.at[my_chip]
            else:
                px, py, pc = x ^ ((k + 1) >> 2), y ^ (((k + 1) >> 1) & 1), c ^ ((k + 1) & 1)
                src, dst = (s_ref.at[4 * px + 2 * py + pc] if mode == "scatter" else s_ref), l_ref.at[me]
            copies.append(pltpu.make_async_remote_copy(
                src_ref=src, dst_ref=dst, send_sem=send_sems.at[npeers * a + k], recv_sem=recv_sems.at[npeers * a + k],
                device_id=(px, py, pc), device_id_type=MESH_ID))
    return copies


def _exchange_start(srcs, lands, after, *, mode, name):
    n = len(srcs)
    nsem = PEERS[mode] * n

    def body(*refs):
        token = refs[-1]
        for cp in _exchange_copies(refs[:n], refs[n:2 * n], refs[2 * n + 1], refs[2 * n + 2], mode):
            cp.start()
        token[...] = jnp.zeros_like(token)

    arrays = list(srcs) + list(lands)
    outs = pl.pallas_call(
        body, name=name,
        out_shape=(pltpu.SemaphoreType.DMA((nsem,)), pltpu.SemaphoreType.DMA((nsem,)),
                   *[pltpu.HBM(a.shape, a.dtype) for a in arrays], jax.ShapeDtypeStruct((SUBLANES, LANES), F32)),
        in_specs=[HBM] * (2 * n) + [ANY],
        out_specs=(SEM, SEM, *[HBM] * (2 * n), pl.BlockSpec(memory_space=pltpu.VMEM)),
        input_output_aliases={i: 2 + i for i in range(2 * n)},
        compiler_params=pltpu.CompilerParams(has_side_effects=DATAFLOW),
    )(*[pltpu.with_memory_space_constraint(a, pltpu.HBM) for a in arrays], after)
    return outs[0], outs[1], outs[2:2 + n], outs[2 + n:2 + 2 * n], outs[-1]


def _exchange_wait(send_sems, recv_sems, srcs, lands, after, *, mode, name):
    n = len(srcs)

    def body(*refs):
        for cp in _exchange_copies(refs[:n], refs[n:2 * n], refs[2 * n], refs[2 * n + 1], mode):
            cp.wait_send()
            cp.wait_recv()

    arrays = list(srcs) + list(lands)
    outs = pl.pallas_call(
        body, name=name,
        out_shape=tuple(pltpu.HBM(a.shape, a.dtype) for a in arrays),
        in_specs=[HBM] * (2 * n) + [SEM, SEM, ANY],
        out_specs=tuple([HBM] * (2 * n)),
        input_output_aliases={i: i for i in range(2 * n)},
        compiler_params=pltpu.CompilerParams(has_side_effects=DATAFLOW),
    )(*arrays, send_sems, recv_sems, after)
    return outs[n:]


def _own_slot(value, me):
    return lax.dynamic_update_index_in_dim(lax.empty((NDEV,) + value.shape, value.dtype), value, me, 0)


def _small_all_reduce(parts):
    def body(gmp_ref, gmo_ref, gfp_ref, gfo_ref, ga_ref, gc_ref, dw_ref, bf_ref, loss_ref,
             out_ref, buf, send_sems, recv_sems):
        x, y, c = _position()
        me = 4 * x + 2 * y + c

        def colsum(v):
            return jnp.sum(v, axis=0, keepdims=True)

        loss = jnp.sum(colsum(loss_ref[...]), axis=1, keepdims=True) * (0.5 / D)
        rows = [colsum(gmp_ref[...]), colsum(gmo_ref[...]), colsum(gfp_ref[...]), colsum(gfo_ref[...]),
                jnp.concatenate([colsum(ga_ref[...]), colsum(gc_ref[...])], axis=1),
                jnp.concatenate([colsum(dw_ref[0]), colsum(dw_ref[1])], axis=1),
                jnp.concatenate([colsum(dw_ref[2]), colsum(bf_ref[...]), jnp.broadcast_to(loss, (1, 128)),
                                 jnp.zeros((1, 256), F32)], axis=1),
                jnp.zeros((1, D), F32)]
        buf[me] = jnp.concatenate(rows, axis=0)
        copies = []
        for mm in range(1, NDEV):
            peer = (x ^ (mm >> 2), y ^ ((mm >> 1) & 1), c ^ (mm & 1))
            copies.append(pltpu.make_async_remote_copy(
                src_ref=buf.at[me], dst_ref=buf.at[me], send_sem=send_sems.at[mm - 1], recv_sem=recv_sems.at[mm - 1],
                device_id=peer, device_id_type=MESH_ID))
        for cp in copies:
            cp.start()
        for cp in copies:
            cp.wait_recv()
        for cp in copies:
            cp.wait_send()
        acc = buf[0]
        for d in range(1, NDEV):
            acc = acc + buf[d]
        out_ref[...] = acc

    vm = pl.BlockSpec(memory_space=pltpu.VMEM)
    return pl.pallas_call(
        body, name="small_all_reduce",
        out_shape=jax.ShapeDtypeStruct((SUBLANES, D), F32),
        in_specs=[vm] * len(parts), out_specs=vm,
        scratch_shapes=[pltpu.VMEM((NDEV, SUBLANES, D), F32), pltpu.SemaphoreType.DMA((7,)), pltpu.SemaphoreType.DMA((7,))],
    )(*parts)


def _adam_update(w, g, m, v):
    nm = ADAM_B1 * m + (1.0 - ADAM_B1) * g
    nv = ADAM_B2 * v + (1.0 - ADAM_B2) * (g * g)
    m_hat = nm / (1.0 - ADAM_B1 ** ADAM_STEP)
    v_hat = nv / (1.0 - ADAM_B2 ** ADAM_STEP)
    return -ADAM_LR * (m_hat / (jnp.sqrt(v_hat) + ADAM_EPS) + ADAM_WD * w), nm, nv


SMALL_SLOTS = {"g_mix_pre": (0, 0, D), "g_mix_post": (1, 0, D), "g_ffn_pre": (2, 0, D), "g_ffn_post": (3, 0, D),
               "g_attn_out": (4, 0, AW), "g_conv_out": (4, AW, CW), "b_forget": (6, CW, H)}


def _small_adamw(small, conv_grad, params):
    names = list(params)
    n = len(names)

    def body(*refs):
        small_ref, cg_ref = refs[0], refs[1]
        ins, outs = refs[2:2 + 3 * n], refs[2 + 3 * n:]
        for i, name in enumerate(names):
            w_ref, m_ref, v_ref = ins[3 * i:3 * i + 3]
            g_ref, d_ref, nm_ref, nv_ref = outs[4 * i:4 * i + 4]
            if name == "conv_w":
                g = cg_ref[...]
            else:
                r, c0, width = SMALL_SLOTS[name]
                g = small_ref[r:r + 1, c0:c0 + width]
            g_ref[...] = g
            d_ref[...], nm_ref[...], nv_ref[...] = _adam_update(w_ref[...], g, m_ref[...], v_ref[...])

    vm = pl.BlockSpec(memory_space=pltpu.VMEM)
    flat = [a for name in names for a in params[name]]
    outs = pl.pallas_call(
        body, name="adamw_small",
        in_specs=[vm] * (2 + 3 * n), out_specs=[vm] * (4 * n),
        out_shape=[jax.ShapeDtypeStruct(params[name][0].shape, F32) for name in names for _ in range(4)],
    )(small, conv_grad, *flat)
    return {name: outs[4 * i:4 * i + 4] for i, name in enumerate(names)}


def _chip_sum_adamw(got, own, idx, wt, mt, vt, *, tr, name):
    cols, rows = wt.shape
    gcols = own.shape[1]

    def body(idx_ref, got_ref, own_ref, w_ref, m_ref, v_ref, g_ref, d_ref, nm_ref, nv_ref):
        g = jnp.zeros((tr, gcols), F32)
        for j in range(4):
            g = g + jnp.where(idx_ref[1] == j, own_ref[...], got_ref[j].astype(F32))
        g = g.T[:cols]
        g_ref[...] = g
        d_ref[...], nm_ref[...], nv_ref[...] = _adam_update(w_ref[...], g, m_ref[...], v_ref[...])

    spec = pl.BlockSpec((cols, tr), lambda i, idx: (0, i))
    gspec = pl.BlockSpec((tr, gcols), lambda i, idx: (i, 0))
    return pl.pallas_call(
        body, name=name,
        grid_spec=pltpu.PrefetchScalarGridSpec(
            num_scalar_prefetch=1, grid=(rows // tr,),
            in_specs=[pl.BlockSpec((4, tr, gcols), lambda i, idx: (0, i, 0)), gspec, spec, spec, spec],
            out_specs=[spec] * 4),
        out_shape=[jax.ShapeDtypeStruct((cols, rows), F32)] * 4,
        compiler_params=_cparams(32, ("arbitrary",)),
    )(idx, got, own, wt, mt, vt)


def _device_sum_adamw(land, w, m, v, *, tr, name):
    rows, cols = w.shape

    def body(land_ref, w_ref, m_ref, v_ref, g_ref, d_ref, nm_ref, nv_ref):
        g = land_ref[0].astype(F32)
        for dev in range(1, NDEV):
            g = g + land_ref[dev].astype(F32)
        g_ref[...] = g
        d_ref[...], nm_ref[...], nv_ref[...] = _adam_update(w_ref[...], g, m_ref[...], v_ref[...])

    spec = pl.BlockSpec((tr, cols), lambda i: (i, 0))
    return pl.pallas_call(
        body, name=name, grid=(rows // tr,),
        in_specs=[pl.BlockSpec((NDEV, tr, cols), lambda i: (0, i, 0)), spec, spec, spec],
        out_specs=[spec] * 4,
        out_shape=[jax.ShapeDtypeStruct((rows, cols), F32)] * 4,
        compiler_params=_cparams(32, ("arbitrary",)),
    )(land, w, m, v)


def _placement_constants():
    j = jnp.arange(128)[:, None]
    lane = jnp.arange(1024)[None, :]
    head, sub = lane // HP, lane % HP
    piece, jh = j // H, j % H
    valid = (j < 3 * H) & (jh == head)
    pq = jnp.where(valid & (sub == DH + piece), 1.0, 0.0).astype(BF16)
    pk = jnp.where(valid & (sub == DH + 3 + piece), -1.0, 0.0).astype(BF16)
    oq = jnp.where((sub >= DH + 3) & (sub < DH + 6), 1.0, 0.0).astype(F32)
    ok = jnp.where((sub >= DH) & (sub < DH + 3), 1.0, 0.0).astype(F32)
    r = jnp.arange(AW)[:, None]
    cc = jnp.arange(128)[None, :]
    sel = jnp.where((r % DH == 3) & (r // DH == cc), -1.0, 0.0).astype(BF16)
    gi = jnp.arange(GS)
    gsum = (gi[:, None] // DH == gi[None, :] // DH).astype(BF16)
    return pq, pk, oq, ok, sel, gsum


def _local_step(xs, tgt, wp, late_weights, cw8, bfp, g_attn_out, g_conv_out,
                g_mix_pre, g_mix_post, g_ffn_pre, g_ffn_post, early_grads=None, last_grad=None):
    pq, pk, oq, ok, sel, gsum = _placement_constants()
    h1t, qp, kp, vv, bcu, zf = _in_proj(xs, g_mix_pre, wp, bfp, pq, pk, oq, ok, tm=512)
    o, lse, mk = _attn_fwd(qp, kp, vv, t=512)
    w_out_f, wgu, wd = late_weights(lse)
    merged, y, x2, cv, h2 = _mix_out(o, bcu, cw8, g_attn_out, g_conv_out, gsum, w_out_f, xs, g_mix_post, g_ffn_pre, tm=512)
    gate, up, act, dx3, dff, loss_p, dg_ffn_post = _ffn_fwd_loss(h2, wgu, wd, x2, tgt, g_ffn_post, tm=512)

    dgu, dx2, dy, dg_ffn_pre, dg_mix_post = _ffn_bwd(dff, wd, gate, up, wgu, x2, g_ffn_pre, dx3, y, g_mix_post, tm=256)
    dw_down = _grad_matmul_blocks(act, dff, ts=4096, name="grad_w_down")
    dw_gu = _grad_matmul_blocks(dgu.reshape(NDEV, -1, FB), h2, ts=4096, name="grad_w_gate_up")
    dw_out = _grad_matmul(merged, dy, ta=1024, tb=1024, ts=2048, name="grad_w_out")
    token = early_grads(dw_out, dw_gu, dw_down) if early_grads is not None else None
    ga = g_attn_out if token is None else g_attn_out + token[0:1, 0:1]
    do, dl, dcv, db, dg_attn, dg_conv = _mix_bwd(dy, w_out_f, o, cv, bcu, ga, g_conv_out, gsum, tm=512)
    dbcu, dtaps = _conv_bwd(dcv, db, bcu, cw8, tm=512)
    dqp, dkp, dv, dkx = _attn_bwd(qp, kp, vv, do, lse, dl, mk, t=512)
    dfl, dbf = _forget_bwd(dkx, zf, sel, tm=512)
    pieces = (dqp, dkp, dv, dbcu, dfl)
    dwp = _grad_w_in(h1t, pieces)
    token = last_grad(dwp) if last_grad is not None else None
    g1 = g_mix_pre if token is None else g_mix_pre + token[0:1, 0:1]
    grad_x, dg_mix_pre = _in_proj_bwd(pieces, wp, xs, g1, dx2, tm=512)
    return (grad_x, dwp, dw_out, dw_gu, dw_down, dg_mix_pre, dg_mix_post, dg_ffn_pre, dg_ffn_post, dg_attn, dg_conv,
            dtaps, dbf, loss_p)


BIG_TILES = {"w_in": 256, "w_out": 128, "w_gate_up": 176, "w_down": 176}


def kernel(x, w_in, b_forget, conv_w, g_attn_out, g_conv_out, w_out, g_mix_pre, g_mix_post, w_gate_up, w_down, g_ffn_pre, g_ffn_post, loss_target, m_w_in, m_b_forget, m_conv_w, m_g_attn_out, m_g_conv_out, m_w_out, m_g_mix_pre, m_g_mix_post, m_w_gate_up, m_w_down, m_g_ffn_pre, m_g_ffn_post, v_w_in, v_b_forget, v_conv_w, v_g_attn_out, v_g_conv_out, v_w_out, v_g_mix_pre, v_g_mix_post, v_w_gate_up, v_w_down, v_g_ffn_pre, v_g_ffn_post):
    xc, yc, cc = _position()
    my_chip = 2 * xc + yc
    me = 2 * my_chip + cc
    idx = jnp.stack([cc, my_chip]).astype(jnp.int32)
    tables = _in_layout_tables()

    w_in_b = w_in[0].astype(BF16)
    g_in, g_last, g_taps = _all_gather([w_in_b[:, :IN_MAIN], w_in_b[:, IN_MAIN].reshape(SUBLANES, LANES), conv_w[0]])
    last_cols = jnp.pad(g_last.reshape(NDEV, D).T.astype(F32), ((0, 0), (0, LANES - NDEV)))
    wp = _assemble_w_in(g_in, last_cols, tables, tr=256)
    cw8 = jnp.pad(g_taps.transpose(1, 0, 2).reshape(3, CW), ((0, SUBLANES - 3), (0, 0)))

    late = [w_out[0].astype(BF16), w_gate_up[0].T.astype(BF16), w_down[0].astype(BF16)]
    ssem, rsem, late_thru, land_thru, token = _exchange_start(
        late, [_own_slot(s, me) for s in late], g_in, mode="gather", name="gather_late_start")
    bfp = jnp.pad(b_forget, ((0, 0), (0, 128 - H))) + token[0:1, :]

    def late_weights(after):
        l_out, l_gu, l_down = _exchange_wait(ssem, rsem, late_thru, land_thru, after, mode="gather", name="gather_late_wait")
        return l_out.reshape(D, D), l_gu.reshape(2, 4, FB, D), l_down.reshape(4, FB, D)

    early = {}

    def early_grads(dw_out, dw_gu, dw_down):
        srcs = [dw_out.reshape(NDEV, D // NDEV, D), dw_gu, dw_down.reshape(NDEV, DFF // NDEV, D)]
        lands = [_own_slot(lax.dynamic_index_in_dim(s, me, 0, keepdims=False), me) for s in srcs]
        early["handles"] = _exchange_start(srcs, lands, dw_out, mode="scatter", name="scatter_early_start")
        return early["handles"][4]

    last = {}

    def last_grad(dwp):
        g_w_in = _disassemble_w_in(dwp, tables, tr=256).reshape(4, 2, D, IN_PAD)
        (from_sibling,) = _pair_exchange([g_w_in])
        pair_b, last["own"] = _pair_sum(g_w_in, from_sibling, idx, tr=BIG_TILES["w_in"], name="grad_pair_sum_w_in")
        land = lax.dynamic_update_index_in_dim(lax.empty(pair_b.shape, pair_b.dtype),
                                               lax.dynamic_index_in_dim(pair_b, my_chip, 0, keepdims=False), my_chip, 0)
        last["handles"] = _exchange_start([pair_b], [land], last["own"], mode="chips", name="chips_w_in_start")
        return last["handles"][4]

    (grad_x, dwp, dw_out, dw_gu, dw_down, dg_mix_pre, dg_mix_post, dg_ffn_pre, dg_ffn_post, dg_attn, dg_conv,
     dtaps, dbf, loss_p) = _local_step(x[0], loss_target[0], wp, late_weights, cw8, bfp, g_attn_out, g_conv_out,
                                        g_mix_pre, g_mix_post, g_ffn_pre, g_ffn_post, early_grads, last_grad)

    e_ssem, e_rsem, e_srcs, e_lands, _ = early["handles"]
    land_out, land_gu, land_down = _exchange_wait(e_ssem, e_rsem, e_srcs, e_lands, dg_mix_pre, mode="scatter",
                                                  name="scatter_early_wait")
    res = {}
    big = {"w_out": (land_out, w_out[0], m_w_out[0], v_w_out[0]),
           "w_gate_up": (land_gu, w_gate_up[0].T, m_w_gate_up[0].T, v_w_gate_up[0].T),
           "w_down": (land_down, w_down[0], m_w_down[0], v_w_down[0])}
    for name, (land, w, m, v) in big.items():
        outs = _device_sum_adamw(land, w, m, v, tr=BIG_TILES[name], name="adamw_" + name)
        res[name] = [(o.T if name == "w_gate_up" else o)[None] for o in outs]
    c_ssem, c_rsem, c_srcs, c_lands, _ = last["handles"]
    after = sum(res[n][1][0, :SUBLANES, :LANES] for n in big)
    (from_chips,) = _exchange_wait(c_ssem, c_rsem, c_srcs, c_lands, after, mode="chips", name="chips_w_in_wait")
    outs = _chip_sum_adamw(from_chips, last["own"], idx, w_in[0].T, m_w_in[0].T, v_w_in[0].T,
                           tr=BIG_TILES["w_in"], name="adamw_w_in")
    res["w_in"] = [o.T[None] for o in outs]

    small = _small_all_reduce([dg_mix_pre, dg_mix_post, dg_ffn_pre, dg_ffn_post, dg_attn, dg_conv, dtaps, dbf, loss_p])
    taps_full = jnp.concatenate([small[5:6, :CW], small[5:6, CW:], small[6:7, :CW]], axis=0)
    loss = small[6, CW + 128]
    smalls = {"b_forget": (b_forget, m_b_forget, v_b_forget), "conv_w": (conv_w[0], m_conv_w[0], v_conv_w[0]),
              "g_attn_out": (g_attn_out, m_g_attn_out, v_g_attn_out), "g_conv_out": (g_conv_out, m_g_conv_out, v_g_conv_out),
              "g_mix_pre": (g_mix_pre, m_g_mix_pre, v_g_mix_pre), "g_mix_post": (g_mix_post, m_g_mix_post, v_g_mix_post),
              "g_ffn_pre": (g_ffn_pre, m_g_ffn_pre, v_g_ffn_pre), "g_ffn_post": (g_ffn_post, m_g_ffn_post, v_g_ffn_post)}
    for name, outs in _small_adamw(small, lax.dynamic_slice(taps_full, (0, me * 64), (3, 64)), smalls).items():
        res[name] = [o[None] for o in outs] if name == "conv_w" else list(outs)

    order = ["w_in", "b_forget", "conv_w", "g_attn_out", "g_conv_out", "w_out", "g_mix_pre", "g_mix_post",
             "w_gate_up", "w_down", "g_ffn_pre", "g_ffn_post"]
    outs = [loss, grad_x[None]]
    for k in range(4):
        outs += [res[n][k] for n in order]
    return tuple(outs)
```

```python
import functools

import numpy as np

import jax
import jax.numpy as jnp
from jax import lax
from jax.experimental import pallas as pl
from jax.experimental.pallas import tpu as pltpu

F32 = jnp.float32
BF16 = jnp.bfloat16
MESH_ID = pl.DeviceIdType.MESH

D = 1024
H = 8
DH = 64
AW = 512
CW = 512
DFF = 2816
FB = DFF // 4
HP = 128
OFF_Q, OFF_K, OFF_V, OFF_BCU, OFF_F = 0, 512, 1024, 1536, 3072
WP = OFF_F + 128
PIECES = ((OFF_Q, OFF_K), (OFF_K, OFF_V), (OFF_V, OFF_BCU), (OFF_BCU, OFF_F), (OFF_F, WP))
EPS = 1e-6
NDEV = 8
LANES = 128
SUBLANES = 8
IN_COLS = 385
IN_PAD = 512
IN_MAIN = 384
WIN = 640
ADAM_LR, ADAM_B1, ADAM_B2, ADAM_EPS, ADAM_WD, ADAM_STEP = 0.001, 0.9, 0.999, 1e-08, 0.01, 10

NT = (((1,), (1,)), ((), ()))
TN = (((0,), (0,)), ((), ()))


def _cparams(vmem_mb=None, sem=None):
    kw = {}
    if vmem_mb is not None:
        kw["vmem_limit_bytes"] = vmem_mb << 20
    if sem is not None:
        kw["dimension_semantics"] = sem
    return pltpu.CompilerParams(**kw)


def _full(shape):
    return pl.BlockSpec(shape, lambda *_: (0,) * len(shape))


def _resident(shape):
    return pl.BlockSpec(shape, lambda *_: (0,) * len(shape), pipeline_mode=pl.Buffered(1))


def _rows(tm, width):
    return pl.BlockSpec((tm, width), lambda i: (i, 0))


def _fold8(v):
    r, w = v.shape
    return jnp.sum(v.reshape(r // SUBLANES, SUBLANES, w), axis=0)


def _split_dot(v, m01):
    hi = v.astype(BF16)
    lo = (v - hi.astype(F32)).astype(BF16)
    return (jnp.dot(hi, m01, preferred_element_type=F32)
            + jnp.dot(lo, m01, preferred_element_type=F32))


GS = 256


def _group_sum(v, g01):
    parts = [_split_dot(v[:, c:c + GS], g01) for c in range(0, v.shape[1], GS)]
    return parts[0] if len(parts) == 1 else jnp.concatenate(parts, axis=1)


def _exact_dot01(m01, v):
    p1 = v.astype(BF16)
    r1 = v - p1.astype(F32)
    p2 = r1.astype(BF16)
    p3 = (r1 - p2.astype(F32)).astype(BF16)
    return (jnp.dot(m01, p1, preferred_element_type=F32) + jnp.dot(m01, p2, preferred_element_type=F32)
            + jnp.dot(m01, p3, preferred_element_type=F32))


def _rms_fwd(v, g):
    r = lax.rsqrt(jnp.mean(v * v, axis=-1, keepdims=True) + EPS)
    n = v * r
    return n * g, n, r


def _rms_bwd(do, n, r, g):
    dn = do * g
    return r * (dn - n * jnp.mean(dn * n, axis=-1, keepdims=True)), do * n


def _padded_column(n):
    if n < AW:
        return OFF_Q + n, 0.125
    if n < 3 * AW:
        return n, 1.0
    if n < 3 * AW + H:
        return OFF_F + n - 3 * AW, 1.0
    return OFF_BCU + n - 3 * AW - H, 1.0


def _in_layout_tables():
    dest = -np.ones((IN_PAD, LANES), np.int32)
    dest_f = -np.ones((IN_PAD, LANES), np.int32)
    scale = np.zeros((IN_PAD, LANES), np.float32)
    starts = []
    for k in range(NDEV):
        cols = [_padded_column(IN_COLS * k + j) for j in range(IN_COLS)]
        main = [c for c, _ in cols if c < OFF_F]
        ws = min((min(main) // LANES) * LANES, OFF_F - WIN)
        assert ws <= min(main) and max(main) < ws + WIN
        starts.append(ws)
        for j, (c, sc) in enumerate(cols):
            scale[j, k] = sc
            if c < OFF_F:
                dest[j, k] = c - ws
            else:
                dest_f[j, k] = c - OFF_F
    f_shards = tuple(k for k in range(NDEV) if (dest_f[:, k] >= 0).any())
    return tuple(starts), f_shards, jnp.asarray(dest), jnp.asarray(dest_f), jnp.asarray(scale)


def _perm(dest_ref, scale_ref, k, width, rows=IN_PAD):
    lane = lax.broadcasted_iota(jnp.int32, (rows, width), 1)
    return jnp.where(dest_ref[0:rows, k:k + 1] == lane, scale_ref[0:rows, k:k + 1], 0.0).astype(BF16)


def _assemble_w_in(blocks, last_cols, tables, *, tr):
    starts, f_shards, dest, dest_f, scale = tables
    last = [_padded_column(IN_COLS * k + IN_MAIN) for k in range(NDEV)]
    f_main = [any(_padded_column(IN_COLS * k + j)[0] >= OFF_F for j in range(IN_MAIN)) for k in range(NDEV)]
    assert IN_COLS == IN_MAIN + 1

    def body(b_ref, c_ref, dest_ref, destf_ref, scale_ref, o_ref):
        o_ref[...] = jnp.zeros_like(o_ref)
        lane = lax.broadcasted_iota(jnp.int32, (tr, LANES), 1)
        for k in range(NDEV):
            b = b_ref[k]
            ws = starts[k]
            part = jnp.dot(b, _perm(dest_ref, scale_ref, k, WIN, IN_MAIN), preferred_element_type=F32)
            o_ref[:, ws:ws + WIN] += part.astype(BF16)
            if f_main[k]:
                part = jnp.dot(b, _perm(destf_ref, scale_ref, k, 128, IN_MAIN), preferred_element_type=F32)
                o_ref[:, OFF_F:WP] += part.astype(BF16)
            col, sc = last[k]
            tile = (col // LANES) * LANES
            o_ref[:, tile:tile + LANES] += jnp.where(lane == col - tile, c_ref[:, k:k + 1] * sc, 0.0).astype(BF16)

    tab = _full((IN_PAD, LANES))
    return pl.pallas_call(
        body, name="assemble_w_in", grid=(D // tr,),
        in_specs=[pl.BlockSpec((NDEV, tr, IN_MAIN), lambda i: (0, i, 0)), _rows(tr, LANES), tab, tab, tab],
        out_specs=_rows(tr, WP),
        out_shape=jax.ShapeDtypeStruct((D, WP), BF16),
        compiler_params=_cparams(48, ("arbitrary",)),
    )(blocks, last_cols, dest, dest_f, scale)


def _disassemble_w_in(dwp, tables, *, tr):
    starts, f_shards, dest, dest_f, scale = tables
    width = dwp.shape[1]

    def body(g_ref, dest_ref, destf_ref, scale_ref, o_ref):
        for k in range(NDEV):
            ws = starts[k]
            acc = lax.dot_general(g_ref[:, ws:ws + WIN], _perm(dest_ref, scale_ref, k, WIN), NT, preferred_element_type=F32)
            if k in f_shards:
                acc = acc + lax.dot_general(g_ref[:, OFF_F:WP], _perm(destf_ref, scale_ref, k, 128), NT,
                                            preferred_element_type=F32)
            o_ref[k] = acc.astype(BF16)

    tab = _full((IN_PAD, LANES))
    return pl.pallas_call(
        body, name="disassemble_w_in", grid=(D // tr,),
        in_specs=[_rows(tr, width), tab, tab, tab],
        out_specs=pl.BlockSpec((NDEV, tr, IN_PAD), lambda i: (0, i, 0)),
        out_shape=jax.ShapeDtypeStruct((NDEV, D, IN_PAD), BF16),
        compiler_params=_cparams(48, ("arbitrary",)),
    )(dwp, dest, dest_f, scale)


def _in_proj(x, g1, wp, bfp, pq, pk, oq, ok, *, tm):
    s = x.shape[0]

    def body(x_ref, g_ref, w_ref, bf_ref, pq_ref, pk_ref, oq_ref, ok_ref,
             ht_ref, qp_ref, kp_ref, v_ref, bcu_ref, z_ref, carry):
        @pl.when(pl.program_id(0) == 0)
        def _():
            carry[...] = jnp.zeros_like(carry)

        h = _rms_fwd(x_ref[...], g_ref[...])[0].astype(BF16)
        ht_ref[...] = h.T
        z = jnp.dot(h, w_ref[:, OFF_F:WP], preferred_element_type=F32) + bf_ref[...]
        z_ref[...] = z
        lane = lax.broadcasted_iota(jnp.int32, (tm, 128), 1)
        logf = jnp.where(lane < H, jnp.minimum(z, 0.0) - jnp.log(1.0 + jnp.exp(-jnp.abs(z))), 0.0)
        row = lax.broadcasted_iota(jnp.int32, (tm, tm), 0)
        col = lax.broadcasted_iota(jnp.int32, (tm, tm), 1)
        tri = (col <= row).astype(BF16)
        c = _exact_dot01(tri, logf) + carry[0:1, :]
        carry[...] = jnp.broadcast_to(c[tm - 1:tm, :], carry.shape)
        c1 = c.astype(BF16).astype(F32)
        r1 = c - c1
        c2 = r1.astype(BF16).astype(F32)
        c3 = (r1 - c2).astype(BF16).astype(F32)
        zc = (c1 + pltpu.roll(c2, 8, axis=1) + pltpu.roll(c3, 16, axis=1)).astype(BF16)

        def pad_heads(v):
            blocks = []
            for pair in range(H // 2):
                two = v[:, 128 * pair:128 * (pair + 1)]
                blocks.append(jnp.where(lane < DH, two, 0.0))
                blocks.append(jnp.where(lane < DH, pltpu.roll(two, DH, axis=1), 0.0))
            return jnp.concatenate(blocks, axis=1)

        q = jnp.dot(h, w_ref[:, OFF_Q:OFF_K], preferred_element_type=F32)
        qp_ref[...] = (pad_heads(q) + jnp.dot(zc, pq_ref[...], preferred_element_type=F32) + oq_ref[...]).astype(BF16)
        k = jnp.dot(h, w_ref[:, OFF_K:OFF_V], preferred_element_type=F32)
        kp_ref[...] = (pad_heads(k) + jnp.dot(zc, pk_ref[...], preferred_element_type=F32) + ok_ref[...]).astype(BF16)
        v = pad_heads(jnp.dot(h, w_ref[:, OFF_V:OFF_BCU], preferred_element_type=F32))
        ones_lane = lax.broadcasted_iota(jnp.int32, (tm, H * HP), 1) % HP == DH
        v_ref[...] = jnp.where(ones_lane, 1.0, v).astype(BF16)
        bcu_ref[...] = jnp.dot(h, w_ref[:, OFF_BCU:OFF_F], preferred_element_type=F32).astype(BF16)

    return pl.pallas_call(
        body, name="in_proj", grid=(s // tm,),
        in_specs=[_rows(tm, D), _full((1, D)), _resident((D, WP)), _full((1, 128)),
                  _full((128, 1024)), _full((128, 1024)), _full((1, 1024)), _full((1, 1024))],
        out_specs=[pl.BlockSpec((D, tm), lambda i: (0, i)), _rows(tm, 1024), _rows(tm, 1024), _rows(tm, 1024),
                   _rows(tm, 3 * CW), _rows(tm, 128)],
        out_shape=[jax.ShapeDtypeStruct((D, s), BF16), jax.ShapeDtypeStruct((s, 1024), BF16),
                   jax.ShapeDtypeStruct((s, 1024), BF16), jax.ShapeDtypeStruct((s, 1024), BF16),
                   jax.ShapeDtypeStruct((s, 3 * CW), BF16), jax.ShapeDtypeStruct((s, 128), F32)],
        scratch_shapes=[pltpu.VMEM((SUBLANES, 128), F32)],
        compiler_params=_cparams(56, ("arbitrary",)),
    )(x, g1, wp, bfp, pq, pk, oq, ok)


def _attn_fwd(qp, kp, v, *, t):
    s = qp.shape[0]
    nq = s // t

    def body(q_ref, k_ref, v_ref, o_ref, lse_ref, mk_ref):
        pi = pl.program_id(1)
        row = lax.broadcasted_iota(jnp.int32, (t, t), 0)
        col = lax.broadcasted_iota(jnp.int32, (t, t), 1)
        lane = lax.broadcasted_iota(jnp.int32, (t, 128), 1)

        def head_step(hh, rows, ki, carry, masked):
            m, acc = carry
            off = pl.multiple_of(ki * t, t)
            q = q_ref[rows, HP * hh:HP * (hh + 1)]
            k = k_ref[pl.ds(off, t), HP * hh:HP * (hh + 1)]
            sc = lax.dot_general(q, k, NT, preferred_element_type=F32)
            if masked:
                sc = jnp.where(col <= row, sc, -1e30)
            mn = jnp.maximum(m, jnp.max(sc, axis=-1, keepdims=True))
            p = jnp.exp(sc - mn).astype(BF16)
            acc = jnp.exp(m - mn) * acc + jnp.dot(p, v_ref[pl.ds(off, t), HP * hh:HP * (hh + 1)],
                                                  preferred_element_type=F32)
            return mn, acc

        def step(rows, ki, carry, masked):
            new = tuple(head_step(hh, rows, ki, carry[hh], masked) for hh in range(2))
            mk_ref[ki, rows] = jnp.where(lane < DH, jnp.broadcast_to(new[0][0], (t, 128)),
                                         jnp.broadcast_to(new[1][0], (t, 128)))
            return new

        init = (jnp.full((t, 1), -1e30, F32), jnp.zeros((t, 128), F32))
        for half in range(2):
            rows = slice(half * t, (half + 1) * t)

            def pair(j, carry, rows=rows):
                return step(rows, 2 * j + 1, step(rows, 2 * j, carry, False), False)

            carry = lax.fori_loop(0, pi, pair, (init, init))
            if half == 1:
                carry = step(rows, 2 * pi, carry, False)
            (m0, acc0), (m1, acc1) = step(rows, 2 * pi + half, carry, True)
            l0, l1 = acc0[:, DH:DH + 1], acc1[:, DH:DH + 1]
            o_ref[rows, :] = jnp.where(lane < DH, acc0 / l0, pltpu.roll(acc1 / l1, DH, axis=1))
            lse_ref[rows, :] = jnp.where(lane < DH, jnp.broadcast_to(m0 + jnp.log(l0), (t, 128)),
                                         jnp.broadcast_to(m1 + jnp.log(l1), (t, 128)))

    return pl.pallas_call(
        body, name="attn_fwd", grid=(H // 2, nq // 2),
        in_specs=[pl.BlockSpec((2 * t, 2 * HP), lambda p, i: (i, p)),
                  pl.BlockSpec((s, 2 * HP), lambda p, i: (0, p)),
                  pl.BlockSpec((s, 2 * HP), lambda p, i: (0, p))],
        out_specs=[pl.BlockSpec((2 * t, 128), lambda p, i: (i, p)), pl.BlockSpec((2 * t, 128), lambda p, i: (i, p)),
                   pl.BlockSpec((nq, 2 * t, 128), lambda p, i: (0, i, p))],
        out_shape=[jax.ShapeDtypeStruct((s, AW), F32), jax.ShapeDtypeStruct((s, AW), F32),
                   jax.ShapeDtypeStruct((nq, s, AW), F32)],
        compiler_params=_cparams(48, ("arbitrary", "arbitrary")),
    )(qp, kp, v)


HALO = 16


def _conv_taps(bcu_ref, halo_ref, first, tm):
    z = bcu_ref[:, CW:2 * CW].astype(F32) * bcu_ref[:, 2 * CW:3 * CW].astype(F32)
    zh = jnp.where(first, 0.0, halo_ref[:, CW:2 * CW].astype(F32) * halo_ref[:, 2 * CW:3 * CW].astype(F32))
    row = lax.broadcasted_iota(jnp.int32, (tm, CW), 0)
    last, before = zh[HALO - 1:HALO, :], zh[HALO - 2:HALO - 1, :]
    z1 = jnp.where(row == 0, last, pltpu.roll(z, 1, axis=0))
    z2 = jnp.where(row == 0, before, jnp.where(row == 1, last, pltpu.roll(z, 2, axis=0)))
    return z, z1, z2


def _halo_before(tm, width):
    return pl.BlockSpec((HALO, width), lambda i: (jnp.maximum(i * (tm // HALO) - 1, 0), 0))


def _mix_out(o, bcu, cw8, ga, gc, gsum, w_out, x, g_post, g_ffn_pre, *, tm):
    s = x.shape[0]

    def body(o_ref, bcu_ref, halo_ref, cw_ref, ga_ref, gc_ref, gs_ref, w_ref, x_ref, g_ref, gf_ref,
             merged_ref, y_ref, x2_ref, cv_ref, h2_ref):
        z, z1, z2 = _conv_taps(bcu_ref, halo_ref, pl.program_id(0) == 0, tm)
        cv = cw_ref[0:1, :] * z2 + cw_ref[1:2, :] * z1 + cw_ref[2:3, :] * z
        cv_ref[...] = cv
        conv = bcu_ref[:, 0:CW].astype(F32) * cv
        ov = o_ref[...]
        ra = lax.rsqrt(_group_sum(ov * ov, gs_ref[...]) * (1.0 / DH) + EPS)
        rc = lax.rsqrt(_group_sum(conv * conv, gs_ref[...]) * (1.0 / DH) + EPS)
        merged = jnp.concatenate([ov * ra * ga_ref[...], conv * rc * gc_ref[...]], axis=1).astype(BF16)
        merged_ref[...] = merged
        y = jnp.dot(merged, w_ref[...], preferred_element_type=F32)
        y_ref[...] = y
        x2 = x_ref[...] + _rms_fwd(y, g_ref[...])[0]
        x2_ref[...] = x2
        h2_ref[...] = _rms_fwd(x2, gf_ref[...])[0].astype(BF16)

    return pl.pallas_call(
        body, name="mix_out", grid=(s // tm,),
        in_specs=[_rows(tm, AW), _rows(tm, 3 * CW), _halo_before(tm, 3 * CW), _full((SUBLANES, CW)),
                  _full((1, AW)), _full((1, CW)), _full((GS, GS)), _resident((D, D)), _rows(tm, D), _full((1, D)),
                  _full((1, D))],
        out_specs=[_rows(tm, D), _rows(tm, D), _rows(tm, D), _rows(tm, CW), _rows(tm, D)],
        out_shape=[jax.ShapeDtypeStruct((s, D), BF16), jax.ShapeDtypeStruct((s, D), F32),
                   jax.ShapeDtypeStruct((s, D), F32), jax.ShapeDtypeStruct((s, CW), F32),
                   jax.ShapeDtypeStruct((s, D), BF16)],
        compiler_params=_cparams(48, ("arbitrary",)),
    )(o, bcu, bcu, cw8, ga, gc, gsum, w_out, x, g_post, g_ffn_pre)


def _ffn_fwd_loss(h2, wgu, wd, x2, target, g_post, *, tm):
    s = x2.shape[0]

    def body(h_ref, w_ref, wd_ref, x2_ref, t_ref, g_ref,
             gate_ref, up_ref, a_ref, dx3_ref, dff_ref, loss_ref, dg_ref):
        @pl.when(pl.program_id(0) == 0)
        def _():
            loss_ref[...] = jnp.zeros_like(loss_ref)
            dg_ref[...] = jnp.zeros_like(dg_ref)

        h = h_ref[...]
        ff = None
        for j in range(4):
            gate = lax.dot_general(h, w_ref[0, j], NT, preferred_element_type=F32)
            up = lax.dot_general(h, w_ref[1, j], NT, preferred_element_type=F32)
            gate_ref[j] = gate.astype(BF16)
            up_ref[j] = up.astype(BF16)
            act = (gate * jax.nn.sigmoid(gate) * up).astype(BF16)
            a_ref[j] = act
            part = jnp.dot(act, wd_ref[j], preferred_element_type=F32)
            ff = part if ff is None else ff + part
        out, n, r = _rms_fwd(ff, g_ref[...])
        e = x2_ref[...] + out - t_ref[...]
        loss_ref[...] += _fold8(e * e)
        dx3 = e * (1.0 / D)
        dx3_ref[...] = dx3
        dff, dg = _rms_bwd(dx3, n, r, g_ref[...])
        dff_ref[...] = dff.astype(BF16)
        dg_ref[...] += _fold8(dg)

    blk4 = pl.BlockSpec((4, tm, FB), lambda i: (0, i, 0))
    return pl.pallas_call(
        body, name="ffn_fwd_loss", grid=(s // tm,),
        in_specs=[_rows(tm, D), _resident((2, 4, FB, D)), _resident((4, FB, D)), _rows(tm, D), _rows(tm, D), _full((1, D))],
        out_specs=[blk4, blk4, blk4, _rows(tm, D), _rows(tm, D), _full((SUBLANES, D)), _full((SUBLANES, D))],
        out_shape=[jax.ShapeDtypeStruct((4, s, FB), BF16)] * 3
        + [jax.ShapeDtypeStruct((s, D), F32), jax.ShapeDtypeStruct((s, D), BF16),
           jax.ShapeDtypeStruct((SUBLANES, D), F32), jax.ShapeDtypeStruct((SUBLANES, D), F32)],
        compiler_params=_cparams(56, ("arbitrary",)),
    )(h2, wgu, wd, x2, target, g_post)


def _ffn_bwd(dff, wd, gate, up, wgu, x2, g_pre, dx3, y, g_post, *, tm):
    s = x2.shape[0]

    def body(dff_ref, wd_ref, gate_ref, up_ref, w_ref, x2_ref, gpre_ref, dx3_ref, y_ref, gpost_ref,
             dgu_ref, dx2_ref, dy_ref, dgpre_ref, dgpost_ref):
        @pl.when(pl.program_id(0) == 0)
        def _():
            dgpre_ref[...] = jnp.zeros_like(dgpre_ref)
            dgpost_ref[...] = jnp.zeros_like(dgpost_ref)

        dff = dff_ref[...]
        dh2 = None
        for j in range(4):
            da = lax.dot_general(dff, wd_ref[j], NT, preferred_element_type=F32)
            g = gate_ref[j].astype(F32)
            sg = jax.nn.sigmoid(g)
            dgate = (da * up_ref[j].astype(F32) * (sg * (1.0 + g * (1.0 - sg)))).astype(BF16)
            dup = (da * (g * sg)).astype(BF16)
            dgu_ref[0, j] = dgate
            dgu_ref[1, j] = dup
            part = (jnp.dot(dgate, w_ref[0, j], preferred_element_type=F32)
                    + jnp.dot(dup, w_ref[1, j], preferred_element_type=F32))
            dh2 = part if dh2 is None else dh2 + part
        _, n2, r2 = _rms_fwd(x2_ref[...], gpre_ref[...])
        dxn, dg = _rms_bwd(dh2, n2, r2, gpre_ref[...])
        dgpre_ref[...] += _fold8(dg)
        dx2 = dx3_ref[...] + dxn
        dx2_ref[...] = dx2
        _, ny, ry = _rms_fwd(y_ref[...], gpost_ref[...])
        dy, dg2 = _rms_bwd(dx2, ny, ry, gpost_ref[...])
        dy_ref[...] = dy.astype(BF16)
        dgpost_ref[...] += _fold8(dg2)

    blk4 = pl.BlockSpec((4, tm, FB), lambda i: (0, i, 0))
    return pl.pallas_call(
        body, name="ffn_bwd", grid=(s // tm,),
        in_specs=[_rows(tm, D), _resident((4, FB, D)), blk4, blk4, _resident((2, 4, FB, D)), _rows(tm, D), _full((1, D)),
                  _rows(tm, D), _rows(tm, D), _full((1, D))],
        out_specs=[pl.BlockSpec((2, 4, tm, FB), lambda i: (0, 0, i, 0)), _rows(tm, D), _rows(tm, D),
                   _full((SUBLANES, D)), _full((SUBLANES, D))],
        out_shape=[jax.ShapeDtypeStruct((2, 4, s, FB), BF16), jax.ShapeDtypeStruct((s, D), F32),
                   jax.ShapeDtypeStruct((s, D), BF16), jax.ShapeDtypeStruct((SUBLANES, D), F32),
                   jax.ShapeDtypeStruct((SUBLANES, D), F32)],
        compiler_params=_cparams(56, ("arbitrary",)),
    )(dff, wd, gate, up, wgu, x2, g_pre, dx3, y, g_post)


def _grad_matmul(a, b, *, ta, tb, ts, name):
    s, ka = a.shape
    nb = b.shape[1]
    ts = min(ts, s)
    nk = s // ts

    def body(a_ref, b_ref, o_ref, acc):
        k = pl.program_id(2)

        @pl.when(k == 0)
        def _():
            acc[...] = jnp.zeros_like(acc)

        acc[...] += lax.dot_general(a_ref[...], b_ref[...], TN, preferred_element_type=F32)

        @pl.when(k == nk - 1)
        def _():
            o_ref[...] = acc[...].astype(BF16)

    return pl.pallas_call(
        body, name=name, grid=(ka // ta, nb // tb, nk),
        in_specs=[pl.BlockSpec((ts, ta), lambda i, j, k: (k, i)), pl.BlockSpec((ts, tb), lambda i, j, k: (k, j))],
        out_specs=pl.BlockSpec((ta, tb), lambda i, j, k: (i, j)),
        out_shape=jax.ShapeDtypeStruct((ka, nb), BF16),
        scratch_shapes=[pltpu.VMEM((ta, tb), F32)],
        compiler_params=_cparams(48, ("arbitrary", "arbitrary", "arbitrary")),
    )(a, b)


GW_TILE = 256


def _grad_w_in(h1t, pieces):
    ka, s = h1t.shape
    widths = [p.shape[1] for p in pieces]
    assert all(w % GW_TILE == 0 for w in widths)
    first = [sum(widths[:i]) // GW_TILE for i in range(len(pieces))]
    count = [w // GW_TILE for w in widths]

    def body(a_ref, *refs):
        o_ref = refs[-1]
        j = pl.program_id(0)
        for ref, f0, n in zip(refs[:-1], first, count):
            @pl.when((j >= f0) & (j < f0 + n))
            def _(ref=ref):
                o_ref[...] = jnp.dot(a_ref[...], ref[...], preferred_element_type=F32).astype(BF16)

    def spec(f0, n):
        return pl.BlockSpec((s, GW_TILE), lambda j: (0, jnp.clip(j - f0, 0, n - 1)))

    return pl.pallas_call(
        body, name="grad_w_in", grid=(sum(count),),
        in_specs=[_resident((ka, s))] + [spec(f0, n) for f0, n in zip(first, count)],
        out_specs=pl.BlockSpec((ka, GW_TILE), lambda j: (0, j)),
        out_shape=jax.ShapeDtypeStruct((ka, sum(widths)), BF16),
        compiler_params=_cparams(56, ("arbitrary",)),
    )(h1t, *pieces)


def _grad_matmul_blocks(a, b, *, ts, name):
    nblk = a.shape[0] if a.ndim == 3 else b.shape[0]
    s = a.shape[-2]
    ka, nb = a.shape[-1], b.shape[-1]
    ts = min(ts, s)
    nk = s // ts

    def body(a_ref, b_ref, o_ref, acc):
        k = pl.program_id(1)

        @pl.when(k == 0)
        def _():
            acc[...] = jnp.zeros_like(acc)

        av = a_ref[0] if a.ndim == 3 else a_ref[...]
        bv = b_ref[0] if b.ndim == 3 else b_ref[...]
        acc[...] += lax.dot_general(av, bv, TN, preferred_element_type=F32)

        @pl.when(k == nk - 1)
        def _():
            o_ref[0] = acc[...].astype(BF16)

    def spec(arr, width):
        if arr.ndim == 3:
            return pl.BlockSpec((1, ts, width), lambda j, k: (j, k, 0))
        return pl.BlockSpec((ts, width), lambda j, k: (k, 0))

    return pl.pallas_call(
        body, name=name, grid=(nblk, nk),
        in_specs=[spec(a, ka), spec(b, nb)],
        out_specs=pl.BlockSpec((1, ka, nb), lambda j, k: (j, 0, 0)),
        out_shape=jax.ShapeDtypeStruct((nblk, ka, nb), BF16),
        scratch_shapes=[pltpu.VMEM((ka, nb), F32)],
        compiler_params=_cparams(48, ("arbitrary", "arbitrary")),
    )(a, b)


def _mix_bwd(dy, w_out, o, cv, bcu, ga, gc, gsum, *, tm):
    s = dy.shape[0]

    def group_norm_bwd(dn_out, v, g, gs):
        r = lax.rsqrt(_group_sum(v * v, gs) * (1.0 / DH) + EPS)
        n = v * r
        dn = dn_out * g
        return r * (dn - n * (_group_sum(dn * n, gs) * (1.0 / DH))), dn_out * n

    def body(dy_ref, w_ref, o_ref, cv_ref, bcu_ref, ga_ref, gc_ref, gs_ref,
             do_ref, dl_ref, dcv_ref, db_ref, dga_ref, dgc_ref):
        @pl.when(pl.program_id(0) == 0)
        def _():
            dga_ref[...] = jnp.zeros_like(dga_ref)
            dgc_ref[...] = jnp.zeros_like(dgc_ref)

        dm = lax.dot_general(dy_ref[...], w_ref[...], NT, preferred_element_type=F32)
        ov = o_ref[...]
        do, dga = group_norm_bwd(dm[:, 0:AW], ov, ga_ref[...], gs_ref[...])
        dob = do.astype(BF16)
        do_ref[...] = dob
        dl_ref[...] = _group_sum(dob.astype(F32) * ov, gs_ref[...])
        dga_ref[...] += _fold8(dga)
        gate_b = bcu_ref[:, 0:CW].astype(F32)
        cv = cv_ref[...]
        dconv, dgc = group_norm_bwd(dm[:, AW:D], gate_b * cv, gc_ref[...], gs_ref[...])
        dgc_ref[...] += _fold8(dgc)
        dcv_ref[...] = dconv * gate_b
        db_ref[...] = (dconv * cv).astype(BF16)

    return pl.pallas_call(
        body, name="mix_bwd", grid=(s // tm,),
        in_specs=[_rows(tm, D), _resident((D, D)), _rows(tm, AW), _rows(tm, CW), _rows(tm, 3 * CW),
                  _full((1, AW)), _full((1, CW)), _full((GS, GS))],
        out_specs=[_rows(tm, AW), _rows(tm, AW), _rows(tm, CW), _rows(tm, CW),
                   _full((SUBLANES, AW)), _full((SUBLANES, CW))],
        out_shape=[jax.ShapeDtypeStruct((s, AW), BF16), jax.ShapeDtypeStruct((s, AW), F32),
                   jax.ShapeDtypeStruct((s, CW), F32), jax.ShapeDtypeStruct((s, CW), BF16),
                   jax.ShapeDtypeStruct((SUBLANES, AW), F32), jax.ShapeDtypeStruct((SUBLANES, CW), F32)],
        compiler_params=_cparams(48, ("arbitrary",)),
    )(dy, w_out, o, cv, bcu, ga, gc, gsum)


def _conv_bwd(dcv, db, bcu, cw8, *, tm):
    s = dcv.shape[0]
    nt = s // tm

    def body(dcv_ref, nxt_ref, db_ref, bcu_ref, halo_ref, cw_ref, dbcu_ref, dw_ref):
        i = pl.program_id(0)

        @pl.when(i == 0)
        def _():
            dw_ref[...] = jnp.zeros_like(dw_ref)

        z, z1, z2 = _conv_taps(bcu_ref, halo_ref, i == 0, tm)
        d = dcv_ref[...]
        dw_ref[0] += _fold8(d * z2)
        dw_ref[1] += _fold8(d * z1)
        dw_ref[2] += _fold8(d * z)
        nx = jnp.where(i == nt - 1, 0.0, nxt_ref[...])
        row = lax.broadcasted_iota(jnp.int32, (tm, CW), 0)
        d1 = jnp.where(row == tm - 1, nx[0:1, :], pltpu.roll(d, tm - 1, axis=0))
        d2 = jnp.where(row == tm - 2, nx[0:1, :], jnp.where(row == tm - 1, nx[1:2, :], pltpu.roll(d, tm - 2, axis=0)))
        dz = cw_ref[2:3, :] * d + cw_ref[1:2, :] * d1 + cw_ref[0:1, :] * d2
        dbcu_ref[:, 0:CW] = db_ref[...]
        dbcu_ref[:, CW:2 * CW] = (dz * bcu_ref[:, 2 * CW:3 * CW].astype(F32)).astype(BF16)
        dbcu_ref[:, 2 * CW:3 * CW] = (dz * bcu_ref[:, CW:2 * CW].astype(F32)).astype(BF16)

    return pl.pallas_call(
        body, name="conv_bwd", grid=(nt,),
        in_specs=[_rows(tm, CW),
                  pl.BlockSpec((SUBLANES, CW), lambda i: (jnp.minimum((i + 1) * (tm // SUBLANES), s // SUBLANES - 1), 0)),
                  _rows(tm, CW), _rows(tm, 3 * CW), _halo_before(tm, 3 * CW), _full((SUBLANES, CW))],
        out_specs=[_rows(tm, 3 * CW), _full((3, SUBLANES, CW))],
        out_shape=[jax.ShapeDtypeStruct((s, 3 * CW), BF16), jax.ShapeDtypeStruct((3, SUBLANES, CW), F32)],
        compiler_params=_cparams(48, ("arbitrary",)),
    )(dcv, dcv, db, bcu, bcu, cw8)


def _attn_bwd(qp, kp, v, do, lse, dl, mk, *, t):
    s = qp.shape[0]
    nq = s // t

    def body(q_ref, k_ref, v_ref, do_ref, lse_ref, dl_ref, mk_ref, dq_ref, dk_ref, dv_ref, dkx_ref, dq_acc):
        pi = pl.program_id(1)

        @pl.when(pi == 0)
        def _():
            dq_acc[...] = jnp.zeros_like(dq_acc)

        row = lax.broadcasted_iota(jnp.int32, (t, t), 0)
        col = lax.broadcasted_iota(jnp.int32, (t, t), 1)
        lane = lax.broadcasted_iota(jnp.int32, (t, 128), 1)

        def head_step(hh, half, qi, carry, masked):
            dk, dv, cs = carry
            off = pl.multiple_of(qi * t, t)
            rows = pl.ds(off, t)
            keys = slice(half * t, (half + 1) * t)
            kh = k_ref[keys, HP * hh:HP * (hh + 1)]
            q = q_ref[rows, HP * hh:HP * (hh + 1)]
            m_col = mk_ref[half, rows, DH * hh:DH * hh + 1]
            scale = jnp.exp(m_col - lse_ref[rows, DH * hh:DH * hh + 1])
            do2 = do_ref[rows, :]
            dom = jnp.where(lane < DH, do2 if hh == 0 else pltpu.roll(do2, DH, axis=1), jnp.zeros((), BF16))
            sc = lax.dot_general(q, kh, NT, preferred_element_type=F32) - m_col
            if masked:
                sc = jnp.where(col <= row, sc, -1e30)
            pt = jnp.exp(sc).astype(BF16)
            dp = lax.dot_general(dom, v_ref[keys, HP * hh:HP * (hh + 1)], NT, preferred_element_type=F32)
            ds32 = (pt.astype(F32) * scale) * (dp - dl_ref[rows, DH * hh:DH * hh + 1])
            ds = ds32.astype(BF16)
            cs = cs + _fold8(ds32)
            dv = dv + jnp.dot((dom.astype(F32) * scale).astype(BF16).T, pt, preferred_element_type=F32)
            dk = dk + jnp.dot(q.T, ds, preferred_element_type=F32)
            dq_acc[rows, HP * hh:HP * (hh + 1)] += jnp.dot(ds, kh, preferred_element_type=F32)
            return dk, dv, cs

        def step(half, qi, carry, masked):
            return tuple(head_step(hh, half, qi, carry[hh], masked) for hh in range(2))

        def two_heads(a0, a1):
            return jnp.where(lane < DH, a0, pltpu.roll(a1, DH, axis=1))

        def rows_to_lanes(a0, a1):
            return jnp.concatenate([a0, a1], axis=0).T

        zero = (jnp.zeros((HP, t), F32), jnp.zeros((128, t), F32), jnp.zeros((SUBLANES, t), F32))
        for half in range(2):
            keys = slice(half * t, (half + 1) * t)
            carry = step(half, 2 * pi + half, (zero, zero), True)
            if half == 0:
                carry = step(half, 2 * pi + 1, carry, False)

            def pair(j, carry, half=half):
                qi = 2 * (pi + 1 + j)
                return step(half, qi + 1, step(half, qi, carry, False), False)

            (dk0, dv0, cs0), (dk1, dv1, cs1) = lax.fori_loop(0, nq // 2 - 1 - pi, pair, carry)
            dk_ref[keys, :] = rows_to_lanes(dk0[0:DH], dk1[0:DH]).astype(BF16)
            dv_ref[keys, :] = rows_to_lanes(dv0[0:DH], dv1[0:DH]).astype(BF16)
            total = lambda cs: jnp.broadcast_to(jnp.sum(cs, axis=0, keepdims=True), (DH, t))
            dkx_ref[keys, :] = rows_to_lanes(total(cs0), total(cs1))

        @pl.when(pi == nq // 2 - 1)
        def _():
            for c in range(s // t):
                rows = slice(c * t, (c + 1) * t)
                dq_ref[rows, :] = two_heads(dq_acc[rows, 0:HP], dq_acc[rows, HP:2 * HP]).astype(BF16)

    return pl.pallas_call(
        body, name="attn_bwd", grid=(H // 2, nq // 2),
        in_specs=[pl.BlockSpec((s, 2 * HP), lambda p, i: (0, p)),
                  pl.BlockSpec((2 * t, 2 * HP), lambda p, i: (i, p)),
                  pl.BlockSpec((2 * t, 2 * HP), lambda p, i: (i, p)),
                  pl.BlockSpec((s, 128), lambda p, i: (0, p)),
                  pl.BlockSpec((s, 128), lambda p, i: (0, p)),
                  pl.BlockSpec((s, 128), lambda p, i: (0, p)),
                  pl.BlockSpec((2, s, 128), lambda p, i: (i, 0, p))],
        out_specs=[pl.BlockSpec((s, 128), lambda p, i: (0, p)),
                   pl.BlockSpec((2 * t, 128), lambda p, i: (i, p)),
                   pl.BlockSpec((2 * t, 128), lambda p, i: (i, p)),
                   pl.BlockSpec((2 * t, 128), lambda p, i: (i, p))],
        out_shape=[jax.ShapeDtypeStruct((s, AW), BF16), jax.ShapeDtypeStruct((s, AW), BF16),
                   jax.ShapeDtypeStruct((s, AW), BF16), jax.ShapeDtypeStruct((s, AW), F32)],
        scratch_shapes=[pltpu.VMEM((s, 2 * HP), F32)],
        compiler_params=_cparams(56, ("arbitrary", "arbitrary")),
    )(qp, kp, v, do, lse, dl, mk)


def _forget_bwd(dkx, z, sel, *, tm):
    s = dkx.shape[0]
    nt = s // tm

    def body(dk_ref, z_ref, sel_ref, dfl_ref, dbf_ref, carry):
        @pl.when(pl.program_id(0) == 0)
        def _():
            carry[...] = jnp.zeros_like(carry)
            dbf_ref[...] = jnp.zeros_like(dbf_ref)

        dc = _split_dot(dk_ref[...], sel_ref[...])
        row = lax.broadcasted_iota(jnp.int32, (tm, tm), 0)
        col = lax.broadcasted_iota(jnp.int32, (tm, tm), 1)
        tri = (col >= row).astype(BF16)
        dlogf = _exact_dot01(tri, dc) + carry[0:1, :]
        carry[...] = jnp.broadcast_to(dlogf[0:1, :], carry.shape)
        dz = dlogf * (1.0 - jax.nn.sigmoid(z_ref[...]))
        dfl_ref[:, 0:128] = dz.astype(BF16)
        dfl_ref[:, 128:GW_TILE] = jnp.zeros((tm, GW_TILE - 128), BF16)
        dbf_ref[...] += _fold8(dz)

    rev = lambda i: (nt - 1 - i, 0)
    return pl.pallas_call(
        body, name="forget_bwd", grid=(nt,),
        in_specs=[pl.BlockSpec((tm, AW), rev), pl.BlockSpec((tm, 128), rev), _full((AW, 128))],
        out_specs=[pl.BlockSpec((tm, GW_TILE), rev), _full((SUBLANES, 128))],
        out_shape=[jax.ShapeDtypeStruct((s, GW_TILE), BF16), jax.ShapeDtypeStruct((SUBLANES, 128), F32)],
        scratch_shapes=[pltpu.VMEM((SUBLANES, 128), F32)],
        compiler_params=_cparams(48, ("arbitrary",)),
    )(dkx, z, sel)


def _in_proj_bwd(pieces, wp, x, g1, dx2, *, tm):
    s = x.shape[0]

    def body(q_ref, k_ref, v_ref, bcu_ref, f_ref, w_ref, x_ref, g_ref, dx2_ref, dx_ref, dg_ref):
        @pl.when(pl.program_id(0) == 0)
        def _():
            dg_ref[...] = jnp.zeros_like(dg_ref)

        dh = None
        for ref, (lo, hi) in zip((q_ref, k_ref, v_ref, bcu_ref, f_ref), PIECES):
            part = lax.dot_general(ref[...], w_ref[:, lo:hi], NT, preferred_element_type=F32)
            dh = part if dh is None else dh + part
        _, n, r = _rms_fwd(x_ref[...], g_ref[...])
        dxn, dg = _rms_bwd(dh, n, r, g_ref[...])
        dx_ref[...] = dx2_ref[...] + dxn
        dg_ref[...] += _fold8(dg)

    return pl.pallas_call(
        body, name="in_proj_bwd", grid=(s // tm,),
        in_specs=[_rows(tm, hi - lo) for lo, hi in PIECES] + [_resident((D, WP)), _rows(tm, D), _full((1, D)), _rows(tm, D)],
        out_specs=[_rows(tm, D), _full((SUBLANES, D))],
        out_shape=[jax.ShapeDtypeStruct((s, D), F32), jax.ShapeDtypeStruct((SUBLANES, D), F32)],
        compiler_params=_cparams(56, ("arbitrary",)),
    )(*pieces, wp, x, g1, dx2)


def _position():
    return lax.axis_index("x"), lax.axis_index("y"), lax.axis_index("c")


ANY = pl.BlockSpec(memory_space=pl.ANY)


def _all_gather(shards):
    n = len(shards)

    def body(*refs):
        x_refs, out_refs = refs[:n], refs[n:2 * n]
        send_sems, recv_sems, local_sems = refs[2 * n:]
        x, y, c = _position()
        me, sibling = (x, y, c), (x, y, 1 - c)
        chips = [(1 - x, y), (x, 1 - y), (1 - x, 1 - y)]

        def copy(a, k, block, to, own=False):
            slot = out_refs[a].at[4 * block[0] + 2 * block[1] + block[2]]
            return pltpu.make_async_remote_copy(
                src_ref=x_refs[a] if own else slot, dst_ref=slot,
                send_sem=send_sems.at[7 * a + k], recv_sem=recv_sems.at[7 * a + k], device_id=to, device_id_type=MESH_ID)

        mine = [pltpu.make_async_copy(x_refs[a], out_refs[a].at[4 * x + 2 * y + c], local_sems.at[a]) for a in range(n)]
        for cp in mine:
            cp.start()
        first = []
        for a in range(n):
            first.append(copy(a, 0, me, sibling, own=True))
            first += [copy(a, 1 + j, me, (*chip, c), own=True) for j, chip in enumerate(chips)]
        for cp in first:
            cp.start()
        passed = []
        for j, chip in enumerate(chips):
            for a in range(n):
                copy(a, 1 + j, (*chip, c), me).wait_recv()
                fwd = copy(a, 4 + j, (*chip, c), sibling)
                fwd.start()
                passed.append(fwd)
        for a in range(n):
            copy(a, 0, sibling, me).wait_recv()
            for j, chip in enumerate(chips):
                copy(a, 4 + j, (*chip, 1 - c), me).wait_recv()
        for cp in first + passed:
            cp.wait_send()
        for cp in mine:
            cp.wait()

    return pl.pallas_call(
        body, name="all_gather_weights",
        out_shape=[jax.ShapeDtypeStruct((NDEV,) + sh.shape, sh.dtype) for sh in shards],
        in_specs=[ANY] * n, out_specs=[ANY] * n,
        scratch_shapes=[pltpu.SemaphoreType.DMA((7 * n,)), pltpu.SemaphoreType.DMA((7 * n,)), pltpu.SemaphoreType.DMA((n,))],
    )(*shards)


def _pair_exchange(grads):
    n = len(grads)

    def body(*refs):
        g_refs, out_refs = refs[:n], refs[n:2 * n]
        send_sems, recv_sems = refs[2 * n:]
        x, y, c = _position()
        copies = [pltpu.make_async_remote_copy(
            src_ref=g_refs[a].at[:, pl.ds(1 - c, 1)], dst_ref=out_refs[a], send_sem=send_sems.at[a],
            recv_sem=recv_sems.at[a], device_id=(x, y, 1 - c), device_id_type=MESH_ID) for a in range(n)]
        for cp in copies:
            cp.start()
        for cp in copies:
            cp.wait()

    return pl.pallas_call(
        body, name="grad_pair_exchange",
        out_shape=[jax.ShapeDtypeStruct((4, 1) + g.shape[2:], g.dtype) for g in grads],
        in_specs=[ANY] * n, out_specs=[ANY] * n,
        scratch_shapes=[pltpu.SemaphoreType.DMA((n,)), pltpu.SemaphoreType.DMA((n,))],
    )(*grads)


def _pair_sum(g, got, idx, *, tr, name):
    r, c = g.shape[2:]

    def body(idx_ref, g_ref, got_ref, pb_ref, own_ref):
        p = g_ref[0, 0].astype(F32) + got_ref[0, 0].astype(F32)
        pb_ref[0] = p.astype(BF16)

        @pl.when(pl.program_id(1) == idx_ref[1])
        def _():
            own_ref[...] = p

    return pl.pallas_call(
        body, name=name,
        grid_spec=pltpu.PrefetchScalarGridSpec(
            num_scalar_prefetch=1, grid=(r // tr, 4),
            in_specs=[pl.BlockSpec((1, 1, tr, c), lambda i, j, idx: (j, idx[0], i, 0)),
                      pl.BlockSpec((1, 1, tr, c), lambda i, j, idx: (j, 0, i, 0))],
            out_specs=[pl.BlockSpec((1, tr, c), lambda i, j, idx: (j, i, 0)),
                       pl.BlockSpec((tr, c), lambda i, j, idx: (i, 0))]),
        out_shape=[jax.ShapeDtypeStruct((4, r, c), BF16), jax.ShapeDtypeStruct((r, c), F32)],
        compiler_params=_cparams(32, ("arbitrary", "arbitrary")),
    )(idx, g, got)


HBM = pl.BlockSpec(memory_space=pltpu.HBM)
SEM = pl.BlockSpec(memory_space=pltpu.SEMAPHORE)
DATAFLOW = pltpu.SideEffectType.DATAFLOW_SIDE_EFFECTING


PEERS = {"gather": NDEV - 1, "scatter": NDEV - 1, "chips": 3}


def _exchange_copies(src_refs, land_refs, send_sems, recv_sems, mode):
    x, y, c = _position()
    me, my_chip = 4 * x + 2 * y + c, 2 * x + y
    npeers = PEERS[mode]
    copies = []
    for a, (s_ref, l_ref) in enumerate(zip(src_refs, land_refs)):
        for k in range(npeers):
            if mode == "chips":
                px, py, pc = x ^ ((k + 1) >> 1), y ^ ((k + 1) & 1), c
                src, dst = s_ref.at[2 * px + py], l_ref.at[my_chip]
            else:
                px, py, pc = x ^ ((k + 1) >> 2), y ^ (((k + 1) >> 1) & 1), c ^ ((k + 1) & 1)
                src, dst = (s_ref.at[4 * px + 2 * py + pc] if mode == "scatter" else s_ref), l_ref.at[me]
            copies.append(pltpu.make_async_remote_copy(
                src_ref=src, dst_ref=dst, send_sem=send_sems.at[npeers * a + k], recv_sem=recv_sems.at[npeers * a + k],
                device_id=(px, py, pc), device_id_type=MESH_ID))
    return copies


def _exchange_start(srcs, lands, after, *, mode, name):
    n = len(srcs)
    nsem = PEERS[mode] * n

    def body(*refs):
        token = refs[-1]
        for cp in _exchange_copies(refs[:n], refs[n:2 * n], refs[2 * n + 1], refs[2 * n + 2], mode):
            cp.start()
        token[...] = jnp.zeros_like(token)

    arrays = list(srcs) + list(lands)
    outs = pl.pallas_call(
        body, name=name,
        out_shape=(pltpu.SemaphoreType.DMA((nsem,)), pltpu.SemaphoreType.DMA((nsem,)),
                   *[pltpu.HBM(a.shape, a.dtype) for a in arrays], jax.ShapeDtypeStruct((SUBLANES, LANES), F32)),
        in_specs=[HBM] * (2 * n) + [ANY],
        out_specs=(SEM, SEM, *[HBM] * (2 * n), pl.BlockSpec(memory_space=pltpu.VMEM)),
        input_output_aliases={i: 2 + i for i in range(2 * n)},
        compiler_params=pltpu.CompilerParams(has_side_effects=DATAFLOW),
    )(*[pltpu.with_memory_space_constraint(a, pltpu.HBM) for a in arrays], after)
    return outs[0], outs[1], outs[2:2 + n], outs[2 + n:2 + 2 * n], outs[-1]


def _exchange_wait(send_sems, recv_sems, srcs, lands, after, *, mode, name):
    n = len(srcs)

    def body(*refs):
        for cp in _exchange_copies(refs[:n], refs[n:2 * n], refs[2 * n], refs[2 * n + 1], mode):
            cp.wait_send()
            cp.wait_recv()

    arrays = list(srcs) + list(lands)
    outs = pl.pallas_call(
        body, name=name,
        out_shape=tuple(pltpu.HBM(a.shape, a.dtype) for a in arrays),
        in_specs=[HBM] * (2 * n) + [SEM, SEM, ANY],
        out_specs=tuple([HBM] * (2 * n)),
        input_output_aliases={i: i for i in range(2 * n)},
        compiler_params=pltpu.CompilerParams(has_side_effects=DATAFLOW),
    )(*arrays, send_sems, recv_sems, after)
    return outs[n:]


def _own_slot(value, me):
    return lax.dynamic_update_index_in_dim(lax.empty((NDEV,) + value.shape, value.dtype), value, me, 0)


def _small_all_reduce(parts):
    def body(gmp_ref, gmo_ref, gfp_ref, gfo_ref, ga_ref, gc_ref, dw_ref, bf_ref, loss_ref,
             out_ref, buf, send_sems, recv_sems):
        x, y, c = _position()
        me = 4 * x + 2 * y + c

        def colsum(v):
            return jnp.sum(v, axis=0, keepdims=True)

        loss = jnp.sum(colsum(loss_ref[...]), axis=1, keepdims=True) * (0.5 / D)
        rows = [colsum(gmp_ref[...]), colsum(gmo_ref[...]), colsum(gfp_ref[...]), colsum(gfo_ref[...]),
                jnp.concatenate([colsum(ga_ref[...]), colsum(gc_ref[...])], axis=1),
                jnp.concatenate([colsum(dw_ref[0]), colsum(dw_ref[1])], axis=1),
                jnp.concatenate([colsum(dw_ref[2]), colsum(bf_ref[...]), jnp.broadcast_to(loss, (1, 128)),
                                 jnp.zeros((1, 256), F32)], axis=1),
                jnp.zeros((1, D), F32)]
        buf[me] = jnp.concatenate(rows, axis=0)
        copies = []
        for mm in range(1, NDEV):
            peer = (x ^ (mm >> 2), y ^ ((mm >> 1) & 1), c ^ (mm & 1))
            copies.append(pltpu.make_async_remote_copy(
                src_ref=buf.at[me], dst_ref=buf.at[me], send_sem=send_sems.at[mm - 1], recv_sem=recv_sems.at[mm - 1],
                device_id=peer, device_id_type=MESH_ID))
        for cp in copies:
            cp.start()
        for cp in copies:
            cp.wait_recv()
        for cp in copies:
            cp.wait_send()
        acc = buf[0]
        for d in range(1, NDEV):
            acc = acc + buf[d]
        out_ref[...] = acc

    vm = pl.BlockSpec(memory_space=pltpu.VMEM)
    return pl.pallas_call(
        body, name="small_all_reduce",
        out_shape=jax.ShapeDtypeStruct((SUBLANES, D), F32),
        in_specs=[vm] * len(parts), out_specs=vm,
        scratch_shapes=[pltpu.VMEM((NDEV, SUBLANES, D), F32), pltpu.SemaphoreType.DMA((7,)), pltpu.SemaphoreType.DMA((7,))],
    )(*parts)


def _adam_update(w, g, m, v):
    nm = ADAM_B1 * m + (1.0 - ADAM_B1) * g
    nv = ADAM_B2 * v + (1.0 - ADAM_B2) * (g * g)
    m_hat = nm / (1.0 - ADAM_B1 ** ADAM_STEP)
    v_hat = nv / (1.0 - ADAM_B2 ** ADAM_STEP)
    return -ADAM_LR * (m_hat / (jnp.sqrt(v_hat) + ADAM_EPS) + ADAM_WD * w), nm, nv


SMALL_SLOTS = {"g_mix_pre": (0, 0, D), "g_mix_post": (1, 0, D), "g_ffn_pre": (2, 0, D), "g_ffn_post": (3, 0, D),
               "g_attn_out": (4, 0, AW), "g_conv_out": (4, AW, CW), "b_forget": (6, CW, H)}


def _small_adamw(small, conv_grad, params):
    names = list(params)
    n = len(names)

    def body(*refs):
        small_ref, cg_ref = refs[0], refs[1]
        ins, outs = refs[2:2 + 3 * n], refs[2 + 3 * n:]
        for i, name in enumerate(names):
            w_ref, m_ref, v_ref = ins[3 * i:3 * i + 3]
            g_ref, d_ref, nm_ref, nv_ref = outs[4 * i:4 * i + 4]
            if name == "conv_w":
                g = cg_ref[...]
            else:
                r, c0, width = SMALL_SLOTS[name]
                g = small_ref[r:r + 1, c0:c0 + width]
            g_ref[...] = g
            d_ref[...], nm_ref[...], nv_ref[...] = _adam_update(w_ref[...], g, m_ref[...], v_ref[...])

    vm = pl.BlockSpec(memory_space=pltpu.VMEM)
    flat = [a for name in names for a in params[name]]
    outs = pl.pallas_call(
        body, name="adamw_small",
        in_specs=[vm] * (2 + 3 * n), out_specs=[vm] * (4 * n),
        out_shape=[jax.ShapeDtypeStruct(params[name][0].shape, F32) for name in names for _ in range(4)],
    )(small, conv_grad, *flat)
    return {name: outs[4 * i:4 * i + 4] for i, name in enumerate(names)}


def _chip_sum_adamw(got, own, idx, wt, mt, vt, *, tr, name):
    cols, rows = wt.shape
    gcols = own.shape[1]

    def body(idx_ref, got_ref, own_ref, w_ref, m_ref, v_ref, g_ref, d_ref, nm_ref, nv_ref):
        g = jnp.zeros((tr, gcols), F32)
        for j in range(4):
            g = g + jnp.where(idx_ref[1] == j, own_ref[...], got_ref[j].astype(F32))
        g = g.T[:cols]
        g_ref[...] = g
        d_ref[...], nm_ref[...], nv_ref[...] = _adam_update(w_ref[...], g, m_ref[...], v_ref[...])

    spec = pl.BlockSpec((cols, tr), lambda i, idx: (0, i))
    gspec = pl.BlockSpec((tr, gcols), lambda i, idx: (i, 0))
    return pl.pallas_call(
        body, name=name,
        grid_spec=pltpu.PrefetchScalarGridSpec(
            num_scalar_prefetch=1, grid=(rows // tr,),
            in_specs=[pl.BlockSpec((4, tr, gcols), lambda i, idx: (0, i, 0)), gspec, spec, spec, spec],
            out_specs=[spec] * 4),
        out_shape=[jax.ShapeDtypeStruct((cols, rows), F32)] * 4,
        compiler_params=_cparams(32, ("arbitrary",)),
    )(idx, got, own, wt, mt, vt)


def _device_sum_adamw(land, w, m, v, *, tr, name):
    rows, cols = w.shape

    def body(land_ref, w_ref, m_ref, v_ref, g_ref, d_ref, nm_ref, nv_ref):
        g = land_ref[0].astype(F32)
        for dev in range(1, NDEV):
            g = g + land_ref[dev].astype(F32)
        g_ref[...] = g
        d_ref[...], nm_ref[...], nv_ref[...] = _adam_update(w_ref[...], g, m_ref[...], v_ref[...])

    spec = pl.BlockSpec((tr, cols), lambda i: (i, 0))
    return pl.pallas_call(
        body, name=name, grid=(rows // tr,),
        in_specs=[pl.BlockSpec((NDEV, tr, cols), lambda i: (0, i, 0)), spec, spec, spec],
        out_specs=[spec] * 4,
        out_shape=[jax.ShapeDtypeStruct((rows, cols), F32)] * 4,
        compiler_params=_cparams(32, ("arbitrary",)),
    )(land, w, m, v)


def _placement_constants():
    j = jnp.arange(128)[:, None]
    lane = jnp.arange(1024)[None, :]
    head, sub = lane // HP, lane % HP
    piece, jh = j // H, j % H
    valid = (j < 3 * H) & (jh == head)
    pq = jnp.where(valid & (sub == DH + piece), 1.0, 0.0).astype(BF16)
    pk = jnp.where(valid & (sub == DH + 3 + piece), -1.0, 0.0).astype(BF16)
    oq = jnp.where((sub >= DH + 3) & (sub < DH + 6), 1.0, 0.0).astype(F32)
    ok = jnp.where((sub >= DH) & (sub < DH + 3), 1.0, 0.0).astype(F32)
    r = jnp.arange(AW)[:, None]
    cc = jnp.arange(128)[None, :]
    sel = jnp.where((r % DH == 3) & (r // DH == cc), -1.0, 0.0).astype(BF16)
    gi = jnp.arange(GS)
    gsum = (gi[:, None] // DH == gi[None, :] // DH).astype(BF16)
    return pq, pk, oq, ok, sel, gsum


def _local_step(xs, tgt, wp, late_weights, cw8, bfp, g_attn_out, g_conv_out,
                g_mix_pre, g_mix_post, g_ffn_pre, g_ffn_post, early_grads=None, last_grad=None):
    pq, pk, oq, ok, sel, gsum = _placement_constants()
    h1t, qp, kp, vv, bcu, zf = _in_proj(xs, g_mix_pre, wp, bfp, pq, pk, oq, ok, tm=512)
    o, lse, mk = _attn_fwd(qp, kp, vv, t=512)
    w_out_f, wgu, wd = late_weights(lse)
    merged, y, x2, cv, h2 = _mix_out(o, bcu, cw8, g_attn_out, g_conv_out, gsum, w_out_f, xs, g_mix_post, g_ffn_pre, tm=512)
    gate, up, act, dx3, dff, loss_p, dg_ffn_post = _ffn_fwd_loss(h2, wgu, wd, x2, tgt, g_ffn_post, tm=512)

    dgu, dx2, dy, dg_ffn_pre, dg_mix_post = _ffn_bwd(dff, wd, gate, up, wgu, x2, g_ffn_pre, dx3, y, g_mix_post, tm=256)
    dw_down = _grad_matmul_blocks(act, dff, ts=4096, name="grad_w_down")
    dw_gu = _grad_matmul_blocks(dgu.reshape(NDEV, -1, FB), h2, ts=4096, name="grad_w_gate_up")
    dw_out = _grad_matmul(merged, dy, ta=1024, tb=1024, ts=2048, name="grad_w_out")
    token = early_grads(dw_out, dw_gu, dw_down) if early_grads is not None else None
    ga = g_attn_out if token is None else g_attn_out + token[0:1, 0:1]
    do, dl, dcv, db, dg_attn, dg_conv = _mix_bwd(dy, w_out_f, o, cv, bcu, ga, g_conv_out, gsum, tm=512)
    dbcu, dtaps = _conv_bwd(dcv, db, bcu, cw8, tm=512)
    dqp, dkp, dv, dkx = _attn_bwd(qp, kp, vv, do, lse, dl, mk, t=512)
    dfl, dbf = _forget_bwd(dkx, zf, sel, tm=512)
    pieces = (dqp, dkp, dv, dbcu, dfl)
    dwp = _grad_w_in(h1t, pieces)
    token = last_grad(dwp) if last_grad is not None else None
    g1 = g_mix_pre if token is None else g_mix_pre + token[0:1, 0:1]
    grad_x, dg_mix_pre = _in_proj_bwd(pieces, wp, xs, g1, dx2, tm=512)
    return (grad_x, dwp, dw_out, dw_gu, dw_down, dg_mix_pre, dg_mix_post, dg_ffn_pre, dg_ffn_post, dg_attn, dg_conv,
            dtaps, dbf, loss_p)


BIG_TILES = {"w_in": 256, "w_out": 128, "w_gate_up": 176, "w_down": 176}


def kernel(x, w_in, b_forget, conv_w, g_attn_out, g_conv_out, w_out, g_mix_pre, g_mix_post, w_gate_up, w_down, g_ffn_pre, g_ffn_post, loss_target, m_w_in, m_b_forget, m_conv_w, m_g_attn_out, m_g_conv_out, m_w_out, m_g_mix_pre, m_g_mix_post, m_w_gate_up, m_w_down, m_g_ffn_pre, m_g_ffn_post, v_w_in, v_b_forget, v_conv_w, v_g_attn_out, v_g_conv_out, v_w_out, v_g_mix_pre, v_g_mix_post, v_w_gate_up, v_w_down, v_g_ffn_pre, v_g_ffn_post):
    xc, yc, cc = _position()
    my_chip = 2 * xc + yc
    me = 2 * my_chip + cc
    idx = jnp.stack([cc, my_chip]).astype(jnp.int32)
    tables = _in_layout_tables()

    w_in_b = w_in[0].astype(BF16)
    g_in, g_last, g_taps = _all_gather([w_in_b[:, :IN_MAIN], w_in_b[:, IN_MAIN].reshape(SUBLANES, LANES), conv_w[0]])
    last_cols = jnp.pad(g_last.reshape(NDEV, D).T.astype(F32), ((0, 0), (0, LANES - NDEV)))
    wp = _assemble_w_in(g_in, last_cols, tables, tr=256)
    cw8 = jnp.pad(g_taps.transpose(1, 0, 2).reshape(3, CW), ((0, SUBLANES - 3), (0, 0)))

    late = [w_out[0].astype(BF16), w_gate_up[0].T.astype(BF16), w_down[0].astype(BF16)]
    ssem, rsem, late_thru, land_thru, token = _exchange_start(
        late, [_own_slot(s, me) for s in late], g_in, mode="gather", name="gather_late_start")
    bfp = jnp.pad(b_forget, ((0, 0), (0, 128 - H))) + token[0:1, :]

    def late_weights(after):
        l_out, l_gu, l_down = _exchange_wait(ssem, rsem, late_thru, land_thru, after, mode="gather", name="gather_late_wait")
        return l_out.reshape(D, D), l_gu.reshape(2, 4, FB, D), l_down.reshape(4, FB, D)

    early = {}

    def early_grads(dw_out, dw_gu, dw_down):
        srcs = [dw_out.reshape(NDEV, D // NDEV, D), dw_gu, dw_down.reshape(NDEV, DFF // NDEV, D)]
        lands = [_own_slot(lax.dynamic_index_in_dim(s, me, 0, keepdims=False), me) for s in srcs]
        early["handles"] = _exchange_start(srcs, lands, dw_out, mode="scatter", name="scatter_early_start")
        return early["handles"][4]

    last = {}

    def last_grad(dwp):
        g_w_in = _disassemble_w_in(dwp, tables, tr=256).reshape(4, 2, D, IN_PAD)
        (from_sibling,) = _pair_exchange([g_w_in])
        pair_b, last["own"] = _pair_sum(g_w_in, from_sibling, idx, tr=BIG_TILES["w_in"], name="grad_pair_sum_w_in")
        land = lax.dynamic_update_index_in_dim(lax.empty(pair_b.shape, pair_b.dtype),
                                               lax.dynamic_index_in_dim(pair_b, my_chip, 0, keepdims=False), my_chip, 0)
        last["handles"] = _exchange_start([pair_b], [land], last["own"], mode="chips", name="chips_w_in_start")
        return last["handles"][4]

    (grad_x, dwp, dw_out, dw_gu, dw_down, dg_mix_pre, dg_mix_post, dg_ffn_pre, dg_ffn_post, dg_attn, dg_conv,
     dtaps, dbf, loss_p) = _local_step(x[0], loss_target[0], wp, late_weights, cw8, bfp, g_attn_out, g_conv_out,
                                        g_mix_pre, g_mix_post, g_ffn_pre, g_ffn_post, early_grads, last_grad)

    e_ssem, e_rsem, e_srcs, e_lands, _ = early["handles"]
    land_out, land_gu, land_down = _exchange_wait(e_ssem, e_rsem, e_srcs, e_lands, dg_mix_pre, mode="scatter",
                                                  name="scatter_early_wait")
    res = {}
    big = {"w_out": (land_out, w_out[0], m_w_out[0], v_w_out[0]),
           "w_gate_up": (land_gu, w_gate_up[0].T, m_w_gate_up[0].T, v_w_gate_up[0].T),
           "w_down": (land_down, w_down[0], m_w_down[0], v_w_down[0])}
    for name, (land, w, m, v) in big.items():
        outs = _device_sum_adamw(land, w, m, v, tr=BIG_TILES[name], name="adamw_" + name)
        res[name] = [(o.T if name == "w_gate_up" else o)[None] for o in outs]
    c_ssem, c_rsem, c_srcs, c_lands, _ = last["handles"]
    after = sum(res[n][1][0, :SUBLANES, :LANES] for n in big)
    (from_chips,) = _exchange_wait(c_ssem, c_rsem, c_srcs, c_lands, after, mode="chips", name="chips_w_in_wait")
    outs = _chip_sum_adamw(from_chips, last["own"], idx, w_in[0].T, m_w_in[0].T, v_w_in[0].T,
                           tr=BIG_TILES["w_in"], name="adamw_w_in")
    res["w_in"] = [o.T[None] for o in outs]

    small = _small_all_reduce([dg_mix_pre, dg_mix_post, dg_ffn_pre, dg_ffn_post, dg_attn, dg_conv, dtaps, dbf, loss_p])
    taps_full = jnp.concatenate([small[5:6, :CW], small[5:6, CW:], small[6:7, :CW]], axis=0)
    loss = small[6, CW + 128]
    smalls = {"b_forget": (b_forget, m_b_forget, v_b_forget), "conv_w": (conv_w[0], m_conv_w[0], v_conv_w[0]),
              "g_attn_out": (g_attn_out, m_g_attn_out, v_g_attn_out), "g_conv_out": (g_conv_out, m_g_conv_out, v_g_conv_out),
              "g_mix_pre": (g_mix_pre, m_g_mix_pre, v_g_mix_pre), "g_mix_post": (g_mix_post, m_g_mix_post, v_g_mix_post),
              "g_ffn_pre": (g_ffn_pre, m_g_ffn_pre, v_g_ffn_pre), "g_ffn_post": (g_ffn_post, m_g_ffn_post, v_g_ffn_post)}
    for name, outs in _small_adamw(small, lax.dynamic_slice(taps_full, (0, me * 64), (3, 64)), smalls).items():
        res[name] = [o[None] for o in outs] if name == "conv_w" else list(outs)

    order = ["w_in", "b_forget", "conv_w", "g_attn_out", "g_conv_out", "w_out", "g_mix_pre", "g_mix_post",
             "w_gate_up", "w_down", "g_ffn_pre", "g_ffn_post"]
    outs = [loss, grad_x[None]]
    for k in range(4):
        outs += [res[n][k] for n in order]
    return tuple(outs)
```

```python
import functools

import numpy as np

import jax
import jax.numpy as jnp
from jax import lax
from jax.experimental import pallas as pl
from jax.experimental.pallas import tpu as pltpu

F32 = jnp.float32
BF16 = jnp.bfloat16
MESH_ID = pl.DeviceIdType.MESH

D = 1024
H = 8
DH = 64
AW = 512
CW = 512
DFF = 2816
FB = DFF // 4
HP = 128
OFF_Q, OFF_K, OFF_V, OFF_BCU, OFF_F = 0, 512, 1024, 1536, 3072
WP = OFF_F + 128
PIECES = ((OFF_Q, OFF_K), (OFF_K, OFF_V), (OFF_V, OFF_BCU), (OFF_BCU, OFF_F), (OFF_F, WP))
EPS = 1e-6
NDEV = 8
LANES = 128
SUBLANES = 8
IN_COLS = 385
IN_PAD = 512
IN_MAIN = 384
WIN = 640
ADAM_LR, ADAM_B1, ADAM_B2, ADAM_EPS, ADAM_WD, ADAM_STEP = 0.001, 0.9, 0.999, 1e-08, 0.01, 10

NT = (((1,), (1,)), ((), ()))
TN = (((0,), (0,)), ((), ()))


def _cparams(vmem_mb=None, sem=None):
    kw = {}
    if vmem_mb is not None:
        kw["vmem_limit_bytes"] = vmem_mb << 20
    if sem is not None:
        kw["dimension_semantics"] = sem
    return pltpu.CompilerParams(**kw)


def _full(shape):
    return pl.BlockSpec(shape, lambda *_: (0,) * len(shape))


def _resident(shape):
    return pl.BlockSpec(shape, lambda *_: (0,) * len(shape), pipeline_mode=pl.Buffered(1))


def _rows(tm, width):
    return pl.BlockSpec((tm, width), lambda i: (i, 0))


def _fold8(v):
    r, w = v.shape
    return jnp.sum(v.reshape(r // SUBLANES, SUBLANES, w), axis=0)


def _split_dot(v, m01):
    hi = v.astype(BF16)
    lo = (v - hi.astype(F32)).astype(BF16)
    return (jnp.dot(hi, m01, preferred_element_type=F32)
            + jnp.dot(lo, m01, preferred_element_type=F32))


GS = 256


def _group_sum(v, g01):
    parts = [_split_dot(v[:, c:c + GS], g01) for c in range(0, v.shape[1], GS)]
    return parts[0] if len(parts) == 1 else jnp.concatenate(parts, axis=1)


def _exact_dot01(m01, v):
    p1 = v.astype(BF16)
    r1 = v - p1.astype(F32)
    p2 = r1.astype(BF16)
    p3 = (r1 - p2.astype(F32)).astype(BF16)
    return (jnp.dot(m01, p1, preferred_element_type=F32) + jnp.dot(m01, p2, preferred_element_type=F32)
            + jnp.dot(m01, p3, preferred_element_type=F32))


def _rms_fwd(v, g):
    r = lax.rsqrt(jnp.mean(v * v, axis=-1, keepdims=True) + EPS)
    n = v * r
    return n * g, n, r


def _rms_bwd(do, n, r, g):
    dn = do * g
    return r * (dn - n * jnp.mean(dn * n, axis=-1, keepdims=True)), do * n


def _padded_column(n):
    if n < AW:
        return OFF_Q + n, 0.125
    if n < 3 * AW:
        return n, 1.0
    if n < 3 * AW + H:
        return OFF_F + n - 3 * AW, 1.0
    return OFF_BCU + n - 3 * AW - H, 1.0


def _in_layout_tables():
    dest = -np.ones((IN_PAD, LANES), np.int32)
    dest_f = -np.ones((IN_PAD, LANES), np.int32)
    scale = np.zeros((IN_PAD, LANES), np.float32)
    starts = []
    for k in range(NDEV):
        cols = [_padded_column(IN_COLS * k + j) for j in range(IN_COLS)]
        main = [c for c, _ in cols if c < OFF_F]
        ws = min((min(main) // LANES) * LANES, OFF_F - WIN)
        assert ws <= min(main) and max(main) < ws + WIN
        starts.append(ws)
        for j, (c, sc) in enumerate(cols):
            scale[j, k] = sc
            if c < OFF_F:
                dest[j, k] = c - ws
            else:
                dest_f[j, k] = c - OFF_F
    f_shards = tuple(k for k in range(NDEV) if (dest_f[:, k] >= 0).any())
    return tuple(starts), f_shards, jnp.asarray(dest), jnp.asarray(dest_f), jnp.asarray(scale)


def _perm(dest_ref, scale_ref, k, width, rows=IN_PAD):
    lane = lax.broadcasted_iota(jnp.int32, (rows, width), 1)
    return jnp.where(dest_ref[0:rows, k:k + 1] == lane, scale_ref[0:rows, k:k + 1], 0.0).astype(BF16)


def _assemble_w_in(blocks, last_cols, tables, *, tr):
    starts, f_shards, dest, dest_f, scale = tables
    last = [_padded_column(IN_COLS * k + IN_MAIN) for k in range(NDEV)]
    f_main = [any(_padded_column(IN_COLS * k + j)[0] >= OFF_F for j in range(IN_MAIN)) for k in range(NDEV)]
    assert IN_COLS == IN_MAIN + 1

    def body(b_ref, c_ref, dest_ref, destf_ref, scale_ref, o_ref):
        o_ref[...] = jnp.zeros_like(o_ref)
        lane = lax.broadcasted_iota(jnp.int32, (tr, LANES), 1)
        for k in range(NDEV):
            b = b_ref[k]
            ws = starts[k]
            part = jnp.dot(b, _perm(dest_ref, scale_ref, k, WIN, IN_MAIN), preferred_element_type=F32)
            o_ref[:, ws:ws + WIN] += part.astype(BF16)
            if f_main[k]:
                part = jnp.dot(b, _perm(destf_ref, scale_ref, k, 128, IN_MAIN), preferred_element_type=F32)
                o_ref[:, OFF_F:WP] += part.astype(BF16)
            col, sc = last[k]
            tile = (col // LANES) * LANES
            o_ref[:, tile:tile + LANES] += jnp.where(lane == col - tile, c_ref[:, k:k + 1] * sc, 0.0).astype(BF16)

    tab = _full((IN_PAD, LANES))
    return pl.pallas_call(
        body, name="assemble_w_in", grid=(D // tr,),
        in_specs=[pl.BlockSpec((NDEV, tr, IN_MAIN), lambda i: (0, i, 0)), _rows(tr, LANES), tab, tab, tab],
        out_specs=_rows(tr, WP),
        out_shape=jax.ShapeDtypeStruct((D, WP), BF16),
        compiler_params=_cparams(48, ("arbitrary",)),
    )(blocks, last_cols, dest, dest_f, scale)


def _disassemble_w_in(dwp, tables, *, tr):
    starts, f_shards, dest, dest_f, scale = tables
    width = dwp.shape[1]

    def body(g_ref, dest_ref, destf_ref, scale_ref, o_ref):
        for k in range(NDEV):
            ws = starts[k]
            acc = lax.dot_general(g_ref[:, ws:ws + WIN], _perm(dest_ref, scale_ref, k, WIN), NT, preferred_element_type=F32)
            if k in f_shards:
                acc = acc + lax.dot_general(g_ref[:, OFF_F:WP], _perm(destf_ref, scale_ref, k, 128), NT,
                                            preferred_element_type=F32)
            o_ref[k] = acc.astype(BF16)

    tab = _full((IN_PAD, LANES))
    return pl.pallas_call(
        body, name="disassemble_w_in", grid=(D // tr,),
        in_specs=[_rows(tr, width), tab, tab, tab],
        out_specs=pl.BlockSpec((NDEV, tr, IN_PAD), lambda i: (0, i, 0)),
        out_shape=jax.ShapeDtypeStruct((NDEV, D, IN_PAD), BF16),
        compiler_params=_cparams(48, ("arbitrary",)),
    )(dwp, dest, dest_f, scale)


def _in_proj(x, g1, wp, bfp, pq, pk, oq, ok, *, tm):
    s = x.shape[0]

    def body(x_ref, g_ref, w_ref, bf_ref, pq_ref, pk_ref, oq_ref, ok_ref,
             ht_ref, qp_ref, kp_ref, v_ref, bcu_ref, z_ref, carry):
        @pl.when(pl.program_id(0) == 0)
        def _():
            carry[...] = jnp.zeros_like(carry)

        h = _rms_fwd(x_ref[...], g_ref[...])[0].astype(BF16)
        ht_ref[...] = h.T
        z = jnp.dot(h, w_ref[:, OFF_F:WP], preferred_element_type=F32) + bf_ref[...]
        z_ref[...] = z
        lane = lax.broadcasted_iota(jnp.int32, (tm, 128), 1)
        logf = jnp.where(lane < H, jnp.minimum(z, 0.0) - jnp.log(1.0 + jnp.exp(-jnp.abs(z))), 0.0)
        row = lax.broadcasted_iota(jnp.int32, (tm, tm), 0)
        col = lax.broadcasted_iota(jnp.int32, (tm, tm), 1)
        tri = (col <= row).astype(BF16)
        c = _exact_dot01(tri, logf) + carry[0:1, :]
        carry[...] = jnp.broadcast_to(c[tm - 1:tm, :], carry.shape)
        c1 = c.astype(BF16).astype(F32)
        r1 = c - c1
        c2 = r1.astype(BF16).astype(F32)
        c3 = (r1 - c2).astype(BF16).astype(F32)
        zc = (c1 + pltpu.roll(c2, 8, axis=1) + pltpu.roll(c3, 16, axis=1)).astype(BF16)

        def pad_heads(v):
            blocks = []
            for pair in range(H // 2):
                two = v[:, 128 * pair:128 * (pair + 1)]
                blocks.append(jnp.where(lane < DH, two, 0.0))
                blocks.append(jnp.where(lane < DH, pltpu.roll(two, DH, axis=1), 0.0))
            return jnp.concatenate(blocks, axis=1)

        q = jnp.dot(h, w_ref[:, OFF_Q:OFF_K], preferred_element_type=F32)
        qp_ref[...] = (pad_heads(q) + jnp.dot(zc, pq_ref[...], preferred_element_type=F32) + oq_ref[...]).astype(BF16)
        k = jnp.dot(h, w_ref[:, OFF_K:OFF_V], preferred_element_type=F32)
        kp_ref[...] = (pad_heads(k) + jnp.dot(zc, pk_ref[...], preferred_element_type=F32) + ok_ref[...]).astype(BF16)
        v = pad_heads(jnp.dot(h, w_ref[:, OFF_V:OFF_BCU], preferred_element_type=F32))
        ones_lane = lax.broadcasted_iota(jnp.int32, (tm, H * HP), 1) % HP == DH
        v_ref[...] = jnp.where(ones_lane, 1.0, v).astype(BF16)
        bcu_ref[...] = jnp.dot(h, w_ref[:, OFF_BCU:OFF_F], preferred_element_type=F32).astype(BF16)

    return pl.pallas_call(
        body, name="in_proj", grid=(s // tm,),
        in_specs=[_rows(tm, D), _full((1, D)), _resident((D, WP)), _full((1, 128)),
                  _full((128, 1024)), _full((128, 1024)), _full((1, 1024)), _full((1, 1024))],
        out_specs=[pl.BlockSpec((D, tm), lambda i: (0, i)), _rows(tm, 1024), _rows(tm, 1024), _rows(tm, 1024),
                   _rows(tm, 3 * CW), _rows(tm, 128)],
        out_shape=[jax.ShapeDtypeStruct((D, s), BF16), jax.ShapeDtypeStruct((s, 1024), BF16),
                   jax.ShapeDtypeStruct((s, 1024), BF16), jax.ShapeDtypeStruct((s, 1024), BF16),
                   jax.ShapeDtypeStruct((s, 3 * CW), BF16), jax.ShapeDtypeStruct((s, 128), F32)],
        scratch_shapes=[pltpu.VMEM((SUBLANES, 128), F32)],
        compiler_params=_cparams(56, ("arbitrary",)),
    )(x, g1, wp, bfp, pq, pk, oq, ok)


def _attn_fwd(qp, kp, v, *, t):
    s = qp.shape[0]
    nq = s // t

    def body(q_ref, k_ref, v_ref, o_ref, lse_ref, mk_ref):
        pi = pl.program_id(1)
        row = lax.broadcasted_iota(jnp.int32, (t, t), 0)
        col = lax.broadcasted_iota(jnp.int32, (t, t), 1)
        lane = lax.broadcasted_iota(jnp.int32, (t, 128), 1)

        def head_step(hh, rows, ki, carry, masked):
            m, acc = carry
            off = pl.multiple_of(ki * t, t)
            q = q_ref[rows, HP * hh:HP * (hh + 1)]
            k = k_ref[pl.ds(off, t), HP * hh:HP * (hh + 1)]
            sc = lax.dot_general(q, k, NT, preferred_element_type=F32)
            if masked:
                sc = jnp.where(col <= row, sc, -1e30)
            mn = jnp.maximum(m, jnp.max(sc, axis=-1, keepdims=True))
            p = jnp.exp(sc - mn).astype(BF16)
            acc = jnp.exp(m - mn) * acc + jnp.dot(p, v_ref[pl.ds(off, t), HP * hh:HP * (hh + 1)],
                                                  preferred_element_type=F32)
            return mn, acc

        def step(rows, ki, carry, masked):
            new = tuple(head_step(hh, rows, ki, carry[hh], masked) for hh in range(2))
            mk_ref[ki, rows] = jnp.where(lane < DH, jnp.broadcast_to(new[0][0], (t, 128)),
                                         jnp.broadcast_to(new[1][0], (t, 128)))
            return new

        init = (jnp.full((t, 1), -1e30, F32), jnp.zeros((t, 128), F32))
        top, bottom = slice(0, t), slice(t, 2 * t)

        def quad(j, carry):
            c0, c1 = carry
            c0 = step(top, 2 * j, c0, False)
            c1 = step(bottom, 2 * j, c1, False)
            c0 = step(top, 2 * j + 1, c0, False)
            c1 = step(bottom, 2 * j + 1, c1, False)
            return c0, c1

        c0, c1 = lax.fori_loop(0, pi, quad, ((init, init), (init, init)))
        f0 = step(top, 2 * pi, c0, True)
        c1 = step(bottom, 2 * pi, c1, False)
        f1 = step(bottom, 2 * pi + 1, c1, True)
        for rows, ((m0, acc0), (m1, acc1)) in ((top, f0), (bottom, f1)):
            l0, l1 = acc0[:, DH:DH + 1], acc1[:, DH:DH + 1]
            o_ref[rows, :] = jnp.where(lane < DH, acc0 / l0, pltpu.roll(acc1 / l1, DH, axis=1))
            lse_ref[rows, :] = jnp.where(lane < DH, jnp.broadcast_to(m0 + jnp.log(l0), (t, 128)),
                                         jnp.broadcast_to(m1 + jnp.log(l1), (t, 128)))

    return pl.pallas_call(
        body, name="attn_fwd", grid=(H // 2, nq // 2),
        in_specs=[pl.BlockSpec((2 * t, 2 * HP), lambda p, i: (i, p)),
                  pl.BlockSpec((s, 2 * HP), lambda p, i: (0, p)),
                  pl.BlockSpec((s, 2 * HP), lambda p, i: (0, p))],
        out_specs=[pl.BlockSpec((2 * t, 128), lambda p, i: (i, p)), pl.BlockSpec((2 * t, 128), lambda p, i: (i, p)),
                   pl.BlockSpec((nq, 2 * t, 128), lambda p, i: (0, i, p))],
        out_shape=[jax.ShapeDtypeStruct((s, AW), F32), jax.ShapeDtypeStruct((s, AW), F32),
                   jax.ShapeDtypeStruct((nq, s, AW), F32)],
        compiler_params=_cparams(48, ("arbitrary", "arbitrary")),
    )(qp, kp, v)


HALO = 16


def _conv_taps(bcu_ref, halo_ref, first, tm):
    z = bcu_ref[:, CW:2 * CW].astype(F32) * bcu_ref[:, 2 * CW:3 * CW].astype(F32)
    zh = jnp.where(first, 0.0, halo_ref[:, CW:2 * CW].astype(F32) * halo_ref[:, 2 * CW:3 * CW].astype(F32))
    row = lax.broadcasted_iota(jnp.int32, (tm, CW), 0)
    last, before = zh[HALO - 1:HALO, :], zh[HALO - 2:HALO - 1, :]
    z1 = jnp.where(row == 0, last, pltpu.roll(z, 1, axis=0))
    z2 = jnp.where(row == 0, before, jnp.where(row == 1, last, pltpu.roll(z, 2, axis=0)))
    return z, z1, z2


def _halo_before(tm, width):
    return pl.BlockSpec((HALO, width), lambda i: (jnp.maximum(i * (tm // HALO) - 1, 0), 0))


def _mix_out(o, bcu, cw8, ga, gc, gsum, w_out, x, g_post, g_ffn_pre, *, tm):
    s = x.shape[0]

    def body(o_ref, bcu_ref, halo_ref, cw_ref, ga_ref, gc_ref, gs_ref, w_ref, x_ref, g_ref, gf_ref,
             merged_ref, y_ref, x2_ref, cv_ref, h2_ref):
        z, z1, z2 = _conv_taps(bcu_ref, halo_ref, pl.program_id(0) == 0, tm)
        cv = cw_ref[0:1, :] * z2 + cw_ref[1:2, :] * z1 + cw_ref[2:3, :] * z
        cv_ref[...] = cv
        conv = bcu_ref[:, 0:CW].astype(F32) * cv
        ov = o_ref[...]
        ra = lax.rsqrt(_group_sum(ov * ov, gs_ref[...]) * (1.0 / DH) + EPS)
        rc = lax.rsqrt(_group_sum(conv * conv, gs_ref[...]) * (1.0 / DH) + EPS)
        merged = jnp.concatenate([ov * ra * ga_ref[...], conv * rc * gc_ref[...]], axis=1).astype(BF16)
        merged_ref[...] = merged
        y = jnp.dot(merged, w_ref[...], preferred_element_type=F32)
        y_ref[...] = y
        x2 = x_ref[...] + _rms_fwd(y, g_ref[...])[0]
        x2_ref[...] = x2
        h2_ref[...] = _rms_fwd(x2, gf_ref[...])[0].astype(BF16)

    return pl.pallas_call(
        body, name="mix_out", grid=(s // tm,),
        in_specs=[_rows(tm, AW), _rows(tm, 3 * CW), _halo_before(tm, 3 * CW), _full((SUBLANES, CW)),
                  _full((1, AW)), _full((1, CW)), _full((GS, GS)), _resident((D, D)), _rows(tm, D), _full((1, D)),
                  _full((1, D))],
        out_specs=[_rows(tm, D), _rows(tm, D), _rows(tm, D), _rows(tm, CW), _rows(tm, D)],
        out_shape=[jax.ShapeDtypeStruct((s, D), BF16), jax.ShapeDtypeStruct((s, D), F32),
                   jax.ShapeDtypeStruct((s, D), F32), jax.ShapeDtypeStruct((s, CW), F32),
                   jax.ShapeDtypeStruct((s, D), BF16)],
        compiler_params=_cparams(48, ("arbitrary",)),
    )(o, bcu, bcu, cw8, ga, gc, gsum, w_out, x, g_post, g_ffn_pre)


def _ffn_fwd_loss(h2, wgu, wd, x2, target, g_post, *, tm):
    s = x2.shape[0]

    def body(h_ref, w_ref, wd_ref, x2_ref, t_ref, g_ref,
             gate_ref, up_ref, a_ref, dx3_ref, dff_ref, loss_ref, dg_ref):
        @pl.when(pl.program_id(0) == 0)
        def _():
            loss_ref[...] = jnp.zeros_like(loss_ref)
            dg_ref[...] = jnp.zeros_like(dg_ref)

        h = h_ref[...]
        ff = None
        for j in range(4):
            gate = lax.dot_general(h, w_ref[0, j], NT, preferred_element_type=F32)
            up = lax.dot_general(h, w_ref[1, j], NT, preferred_element_type=F32)
            gate_ref[j] = gate.astype(BF16)
            up_ref[j] = up.astype(BF16)
            act = (gate * jax.nn.sigmoid(gate) * up).astype(BF16)
            a_ref[j] = act
            part = jnp.dot(act, wd_ref[j], preferred_element_type=F32)
            ff = part if ff is None else ff + part
        out, n, r = _rms_fwd(ff, g_ref[...])
        e = x2_ref[...] + out - t_ref[...]
        loss_ref[...] += _fold8(e * e)
        dx3 = e * (1.0 / D)
        dx3_ref[...] = dx3
        dff, dg = _rms_bwd(dx3, n, r, g_ref[...])
        dff_ref[...] = dff.astype(BF16)
        dg_ref[...] += _fold8(dg)

    blk4 = pl.BlockSpec((4, tm, FB), lambda i: (0, i, 0))
    return pl.pallas_call(
        body, name="ffn_fwd_loss", grid=(s // tm,),
        in_specs=[_rows(tm, D), _resident((2, 4, FB, D)), _resident((4, FB, D)), _rows(tm, D), _rows(tm, D), _full((1, D))],
        out_specs=[blk4, blk4, blk4, _rows(tm, D), _rows(tm, D), _full((SUBLANES, D)), _full((SUBLANES, D))],
        out_shape=[jax.ShapeDtypeStruct((4, s, FB), BF16)] * 3
        + [jax.ShapeDtypeStruct((s, D), F32), jax.ShapeDtypeStruct((s, D), BF16),
           jax.ShapeDtypeStruct((SUBLANES, D), F32), jax.ShapeDtypeStruct((SUBLANES, D), F32)],
        compiler_params=_cparams(56, ("arbitrary",)),
    )(h2, wgu, wd, x2, target, g_post)


def _ffn_bwd(dff, wd, gate, up, wgu, x2, g_pre, dx3, y, g_post, *, tm):
    s = x2.shape[0]

    def body(dff_ref, wd_ref, gate_ref, up_ref, w_ref, x2_ref, gpre_ref, dx3_ref, y_ref, gpost_ref,
             dgu_ref, dx2_ref, dy_ref, dgpre_ref, dgpost_ref):
        @pl.when(pl.program_id(0) == 0)
        def _():
            dgpre_ref[...] = jnp.zeros_like(dgpre_ref)
            dgpost_ref[...] = jnp.zeros_like(dgpost_ref)

        dff = dff_ref[...]
        dh2 = None
        for j in range(4):
            da = lax.dot_general(dff, wd_ref[j], NT, preferred_element_type=F32)
            g = gate_ref[j].astype(F32)
            sg = jax.nn.sigmoid(g)
            dgate = (da * up_ref[j].astype(F32) * (sg * (1.0 + g * (1.0 - sg)))).astype(BF16)
            dup = (da * (g * sg)).astype(BF16)
            dgu_ref[0, j] = dgate
            dgu_ref[1, j] = dup
            part = (jnp.dot(dgate, w_ref[0, j], preferred_element_type=F32)
                    + jnp.dot(dup, w_ref[1, j], preferred_element_type=F32))
            dh2 = part if dh2 is None else dh2 + part
        _, n2, r2 = _rms_fwd(x2_ref[...], gpre_ref[...])
        dxn, dg = _rms_bwd(dh2, n2, r2, gpre_ref[...])
        dgpre_ref[...] += _fold8(dg)
        dx2 = dx3_ref[...] + dxn
        dx2_ref[...] = dx2
        _, ny, ry = _rms_fwd(y_ref[...], gpost_ref[...])
        dy, dg2 = _rms_bwd(dx2, ny, ry, gpost_ref[...])
        dy_ref[...] = dy.astype(BF16)
        dgpost_ref[...] += _fold8(dg2)

    blk4 = pl.BlockSpec((4, tm, FB), lambda i: (0, i, 0))
    return pl.pallas_call(
        body, name="ffn_bwd", grid=(s // tm,),
        in_specs=[_rows(tm, D), _resident((4, FB, D)), blk4, blk4, _resident((2, 4, FB, D)), _rows(tm, D), _full((1, D)),
                  _rows(tm, D), _rows(tm, D), _full((1, D))],
        out_specs=[pl.BlockSpec((2, 4, tm, FB), lambda i: (0, 0, i, 0)), _rows(tm, D), _rows(tm, D),
                   _full((SUBLANES, D)), _full((SUBLANES, D))],
        out_shape=[jax.ShapeDtypeStruct((2, 4, s, FB), BF16), jax.ShapeDtypeStruct((s, D), F32),
                   jax.ShapeDtypeStruct((s, D), BF16), jax.ShapeDtypeStruct((SUBLANES, D), F32),
                   jax.ShapeDtypeStruct((SUBLANES, D), F32)],
        compiler_params=_cparams(56, ("arbitrary",)),
    )(dff, wd, gate, up, wgu, x2, g_pre, dx3, y, g_post)


def _grad_matmul(a, b, *, ta, tb, ts, name):
    s, ka = a.shape
    nb = b.shape[1]
    ts = min(ts, s)
    nk = s // ts

    def body(a_ref, b_ref, o_ref, acc):
        k = pl.program_id(2)

        @pl.when(k == 0)
        def _():
            acc[...] = jnp.zeros_like(acc)

        acc[...] += lax.dot_general(a_ref[...], b_ref[...], TN, preferred_element_type=F32)

        @pl.when(k == nk - 1)
        def _():
            o_ref[...] = acc[...].astype(BF16)

    return pl.pallas_call(
        body, name=name, grid=(ka // ta, nb // tb, nk),
        in_specs=[pl.BlockSpec((ts, ta), lambda i, j, k: (k, i)), pl.BlockSpec((ts, tb), lambda i, j, k: (k, j))],
        out_specs=pl.BlockSpec((ta, tb), lambda i, j, k: (i, j)),
        out_shape=jax.ShapeDtypeStruct((ka, nb), BF16),
        scratch_shapes=[pltpu.VMEM((ta, tb), F32)],
        compiler_params=_cparams(48, ("arbitrary", "arbitrary", "arbitrary")),
    )(a, b)


GW_TILE = 256


def _grad_w_in(h1t, pieces):
    ka, s = h1t.shape
    widths = [p.shape[1] for p in pieces]
    assert all(w % GW_TILE == 0 for w in widths)
    first = [sum(widths[:i]) // GW_TILE for i in range(len(pieces))]
    count = [w // GW_TILE for w in widths]

    def body(a_ref, *refs):
        o_ref = refs[-1]
        j = pl.program_id(0)
        for ref, f0, n in zip(refs[:-1], first, count):
            @pl.when((j >= f0) & (j < f0 + n))
            def _(ref=ref):
                o_ref[...] = jnp.dot(a_ref[...], ref[...], preferred_element_type=F32).astype(BF16)

    def spec(f0, n):
        return pl.BlockSpec((s, GW_TILE), lambda j: (0, jnp.clip(j - f0, 0, n - 1)))

    return pl.pallas_call(
        body, name="grad_w_in", grid=(sum(count),),
        in_specs=[_resident((ka, s))] + [spec(f0, n) for f0, n in zip(first, count)],
        out_specs=pl.BlockSpec((ka, GW_TILE), lambda j: (0, j)),
        out_shape=jax.ShapeDtypeStruct((ka, sum(widths)), BF16),
        compiler_params=_cparams(56, ("arbitrary",)),
    )(h1t, *pieces)


def _grad_matmul_blocks(a, b, *, ts, name):
    nblk = a.shape[0] if a.ndim == 3 else b.shape[0]
    s = a.shape[-2]
    ka, nb = a.shape[-1], b.shape[-1]
    ts = min(ts, s)
    nk = s // ts

    def body(a_ref, b_ref, o_ref, acc):
        k = pl.program_id(1)

        @pl.when(k == 0)
        def _():
            acc[...] = jnp.zeros_like(acc)

        av = a_ref[0] if a.ndim == 3 else a_ref[...]
        bv = b_ref[0] if b.ndim == 3 else b_ref[...]
        acc[...] += lax.dot_general(av, bv, TN, preferred_element_type=F32)

        @pl.when(k == nk - 1)
        def _():
            o_ref[0] = acc[...].astype(BF16)

    def spec(arr, width):
        if arr.ndim == 3:
            return pl.BlockSpec((1, ts, width), lambda j, k: (j, k, 0))
        return pl.BlockSpec((ts, width), lambda j, k: (k, 0))

    return pl.pallas_call(
        body, name=name, grid=(nblk, nk),
        in_specs=[spec(a, ka), spec(b, nb)],
        out_specs=pl.BlockSpec((1, ka, nb), lambda j, k: (j, 0, 0)),
        out_shape=jax.ShapeDtypeStruct((nblk, ka, nb), BF16),
        scratch_shapes=[pltpu.VMEM((ka, nb), F32)],
        compiler_params=_cparams(48, ("arbitrary", "arbitrary")),
    )(a, b)


def _mix_bwd(dy, w_out, o, cv, bcu, ga, gc, gsum, *, tm):
    s = dy.shape[0]

    def group_norm_bwd(dn_out, v, g, gs):
        r = lax.rsqrt(_group_sum(v * v, gs) * (1.0 / DH) + EPS)
        n = v * r
        dn = dn_out * g
        return r * (dn - n * (_group_sum(dn * n, gs) * (1.0 / DH))), dn_out * n

    def body(dy_ref, w_ref, o_ref, cv_ref, bcu_ref, ga_ref, gc_ref, gs_ref,
             do_ref, dl_ref, dcv_ref, db_ref, dga_ref, dgc_ref):
        @pl.when(pl.program_id(0) == 0)
        def _():
            dga_ref[...] = jnp.zeros_like(dga_ref)
            dgc_ref[...] = jnp.zeros_like(dgc_ref)

        dm = lax.dot_general(dy_ref[...], w_ref[...], NT, preferred_element_type=F32)
        ov = o_ref[...]
        do, dga = group_norm_bwd(dm[:, 0:AW], ov, ga_ref[...], gs_ref[...])
        dob = do.astype(BF16)
        do_ref[...] = dob
        dl_ref[...] = _group_sum(dob.astype(F32) * ov, gs_ref[...])
        dga_ref[...] += _fold8(dga)
        gate_b = bcu_ref[:, 0:CW].astype(F32)
        cv = cv_ref[...]
        dconv, dgc = group_norm_bwd(dm[:, AW:D], gate_b * cv, gc_ref[...], gs_ref[...])
        dgc_ref[...] += _fold8(dgc)
        dcv_ref[...] = dconv * gate_b
        db_ref[...] = (dconv * cv).astype(BF16)

    return pl.pallas_call(
        body, name="mix_bwd", grid=(s // tm,),
        in_specs=[_rows(tm, D), _resident((D, D)), _rows(tm, AW), _rows(tm, CW), _rows(tm, 3 * CW),
                  _full((1, AW)), _full((1, CW)), _full((GS, GS))],
        out_specs=[_rows(tm, AW), _rows(tm, AW), _rows(tm, CW), _rows(tm, CW),
                   _full((SUBLANES, AW)), _full((SUBLANES, CW))],
        out_shape=[jax.ShapeDtypeStruct((s, AW), BF16), jax.ShapeDtypeStruct((s, AW), F32),
                   jax.ShapeDtypeStruct((s, CW), F32), jax.ShapeDtypeStruct((s, CW), BF16),
                   jax.ShapeDtypeStruct((SUBLANES, AW), F32), jax.ShapeDtypeStruct((SUBLANES, CW), F32)],
        compiler_params=_cparams(48, ("arbitrary",)),
    )(dy, w_out, o, cv, bcu, ga, gc, gsum)


def _conv_bwd(dcv, db, bcu, cw8, *, tm):
    s = dcv.shape[0]
    nt = s // tm

    def body(dcv_ref, nxt_ref, db_ref, bcu_ref, halo_ref, cw_ref, dbcu_ref, dw_ref):
        i = pl.program_id(0)

        @pl.when(i == 0)
        def _():
            dw_ref[...] = jnp.zeros_like(dw_ref)

        z, z1, z2 = _conv_taps(bcu_ref, halo_ref, i == 0, tm)
        d = dcv_ref[...]
        dw_ref[0] += _fold8(d * z2)
        dw_ref[1] += _fold8(d * z1)
        dw_ref[2] += _fold8(d * z)
        nx = jnp.where(i == nt - 1, 0.0, nxt_ref[...])
        row = lax.broadcasted_iota(jnp.int32, (tm, CW), 0)
        d1 = jnp.where(row == tm - 1, nx[0:1, :], pltpu.roll(d, tm - 1, axis=0))
        d2 = jnp.where(row == tm - 2, nx[0:1, :], jnp.where(row == tm - 1, nx[1:2, :], pltpu.roll(d, tm - 2, axis=0)))
        dz = cw_ref[2:3, :] * d + cw_ref[1:2, :] * d1 + cw_ref[0:1, :] * d2
        dbcu_ref[:, 0:CW] = db_ref[...]
        dbcu_ref[:, CW:2 * CW] = (dz * bcu_ref[:, 2 * CW:3 * CW].astype(F32)).astype(BF16)
        dbcu_ref[:, 2 * CW:3 * CW] = (dz * bcu_ref[:, CW:2 * CW].astype(F32)).astype(BF16)

    return pl.pallas_call(
        body, name="conv_bwd", grid=(nt,),
        in_specs=[_rows(tm, CW),
                  pl.BlockSpec((SUBLANES, CW), lambda i: (jnp.minimum((i + 1) * (tm // SUBLANES), s // SUBLANES - 1), 0)),
                  _rows(tm, CW), _rows(tm, 3 * CW), _halo_before(tm, 3 * CW), _full((SUBLANES, CW))],
        out_specs=[_rows(tm, 3 * CW), _full((3, SUBLANES, CW))],
        out_shape=[jax.ShapeDtypeStruct((s, 3 * CW), BF16), jax.ShapeDtypeStruct((3, SUBLANES, CW), F32)],
        compiler_params=_cparams(48, ("arbitrary",)),
    )(dcv, dcv, db, bcu, bcu, cw8)


def _attn_bwd(qp, kp, v, do, lse, dl, mk, *, t):
    s = qp.shape[0]
    nq = s // t

    def body(q_ref, k_ref, v_ref, do_ref, lse_ref, dl_ref, mk_ref, dq_ref, dk_ref, dv_ref, dkx_ref, dq_acc):
        pi = pl.program_id(1)

        @pl.when(pi == 0)
        def _():
            dq_acc[...] = jnp.zeros_like(dq_acc)

        row = lax.broadcasted_iota(jnp.int32, (t, t), 0)
        col = lax.broadcasted_iota(jnp.int32, (t, t), 1)
        lane = lax.broadcasted_iota(jnp.int32, (t, 128), 1)

        def head_step(hh, qi, carry, modes):
            off = pl.multiple_of(qi * t, t)
            rows = pl.ds(off, t)
            q = q_ref[rows, HP * hh:HP * (hh + 1)]
            qt = q.T
            lse_col = lse_ref[rows, DH * hh:DH * hh + 1]
            dl_col = dl_ref[rows, DH * hh:DH * hh + 1]
            do2 = do_ref[rows, :]
            dom = jnp.where(lane < DH, do2 if hh == 0 else pltpu.roll(do2, DH, axis=1), jnp.zeros((), BF16))
            new, dss = [], []
            for half, masked in enumerate(modes):
                if masked is None:
                    new.append(carry[half])
                    continue
                dk, dv, cs = carry[half]
                keys = slice(half * t, (half + 1) * t)
                m_col = mk_ref[half, rows, DH * hh:DH * hh + 1]
                scale = jnp.exp(m_col - lse_col)
                sc = lax.dot_general(q, k_ref[keys, HP * hh:HP * (hh + 1)], NT, preferred_element_type=F32) - m_col
                if masked:
                    sc = jnp.where(col <= row, sc, -1e30)
                pt = jnp.exp(sc).astype(BF16)
                dp = lax.dot_general(dom, v_ref[keys, HP * hh:HP * (hh + 1)], NT, preferred_element_type=F32)
                ds32 = (pt.astype(F32) * scale) * (dp - dl_col)
                ds = ds32.astype(BF16)
                cs = cs + _fold8(ds32)
                dv = dv + jnp.dot((dom.astype(F32) * scale).astype(BF16).T, pt, preferred_element_type=F32)
                dk = dk + jnp.dot(qt, ds, preferred_element_type=F32)
                new.append((dk, dv, cs))
                dss.append((half, ds))
            if len(dss) == 2:
                dq = jnp.dot(jnp.concatenate([dss[0][1], dss[1][1]], axis=1), k_ref[:, HP * hh:HP * (hh + 1)],
                             preferred_element_type=F32)
            else:
                half, ds = dss[0]
                dq = jnp.dot(ds, k_ref[half * t:(half + 1) * t, HP * hh:HP * (hh + 1)], preferred_element_type=F32)
            dq_acc[rows, HP * hh:HP * (hh + 1)] += dq
            return tuple(new)

        def step(qi, carry, modes):
            return tuple(head_step(hh, qi, carry[hh], modes) for hh in range(2))

        def two_heads(a0, a1):
            return jnp.where(lane < DH, a0, pltpu.roll(a1, DH, axis=1))

        def rows_to_lanes(a0, a1):
            return jnp.concatenate([a0, a1], axis=0).T

        zero = (jnp.zeros((HP, t), F32), jnp.zeros((128, t), F32), jnp.zeros((SUBLANES, t), F32))
        carry = step(2 * pi, ((zero, zero), (zero, zero)), (True, None))
        carry = step(2 * pi + 1, carry, (False, True))

        def pair(j, carry):
            qi = 2 * (pi + 1 + j)
            return step(qi + 1, step(qi, carry, (False, False)), (False, False))

        carry = lax.fori_loop(0, nq // 2 - 1 - pi, pair, carry)
        for half in range(2):
            keys = slice(half * t, (half + 1) * t)
            (dk0, dv0, cs0), (dk1, dv1, cs1) = carry[0][half], carry[1][half]
            dk_ref[keys, :] = rows_to_lanes(dk0[0:DH], dk1[0:DH]).astype(BF16)
            dv_ref[keys, :] = rows_to_lanes(dv0[0:DH], dv1[0:DH]).astype(BF16)
            total = lambda cs: jnp.broadcast_to(jnp.sum(cs, axis=0, keepdims=True), (DH, t))
            dkx_ref[keys, :] = rows_to_lanes(total(cs0), total(cs1))

        @pl.when(pi == nq // 2 - 1)
        def _():
            for c in range(s // t):
                rows = slice(c * t, (c + 1) * t)
                dq_ref[rows, :] = two_heads(dq_acc[rows, 0:HP], dq_acc[rows, HP:2 * HP]).astype(BF16)

    return pl.pallas_call(
        body, name="attn_bwd", grid=(H // 2, nq // 2),
        in_specs=[pl.BlockSpec((s, 2 * HP), lambda p, i: (0, p)),
                  pl.BlockSpec((2 * t, 2 * HP), lambda p, i: (i, p)),
                  pl.BlockSpec((2 * t, 2 * HP), lambda p, i: (i, p)),
                  pl.BlockSpec((s, 128), lambda p, i: (0, p)),
                  pl.BlockSpec((s, 128), lambda p, i: (0, p)),
                  pl.BlockSpec((s, 128), lambda p, i: (0, p)),
                  pl.BlockSpec((2, s, 128), lambda p, i: (i, 0, p))],
        out_specs=[pl.BlockSpec((s, 128), lambda p, i: (0, p)),
                   pl.BlockSpec((2 * t, 128), lambda p, i: (i, p)),
                   pl.BlockSpec((2 * t, 128), lambda p, i: (i, p)),
                   pl.BlockSpec((2 * t, 128), lambda p, i: (i, p))],
        out_shape=[jax.ShapeDtypeStruct((s, AW), BF16), jax.ShapeDtypeStruct((s, AW), BF16),
                   jax.ShapeDtypeStruct((s, AW), BF16), jax.ShapeDtypeStruct((s, AW), F32)],
        scratch_shapes=[pltpu.VMEM((s, 2 * HP), F32)],
        compiler_params=_cparams(56, ("arbitrary", "arbitrary")),
    )(qp, kp, v, do, lse, dl, mk)


def _forget_bwd(dkx, z, sel, *, tm):
    s = dkx.shape[0]
    nt = s // tm

    def body(dk_ref, z_ref, sel_ref, dfl_ref, dbf_ref, carry):
        @pl.when(pl.program_id(0) == 0)
        def _():
            carry[...] = jnp.zeros_like(carry)
            dbf_ref[...] = jnp.zeros_like(dbf_ref)

        dc = _split_dot(dk_ref[...], sel_ref[...])
        row = lax.broadcasted_iota(jnp.int32, (tm, tm), 0)
        col = lax.broadcasted_iota(jnp.int32, (tm, tm), 1)
        tri = (col >= row).astype(BF16)
        dlogf = _exact_dot01(tri, dc) + carry[0:1, :]
        carry[...] = jnp.broadcast_to(dlogf[0:1, :], carry.shape)
        dz = dlogf * (1.0 - jax.nn.sigmoid(z_ref[...]))
        dfl_ref[:, 0:128] = dz.astype(BF16)
        dfl_ref[:, 128:GW_TILE] = jnp.zeros((tm, GW_TILE - 128), BF16)
        dbf_ref[...] += _fold8(dz)

    rev = lambda i: (nt - 1 - i, 0)
    return pl.pallas_call(
        body, name="forget_bwd", grid=(nt,),
        in_specs=[pl.BlockSpec((tm, AW), rev), pl.BlockSpec((tm, 128), rev), _full((AW, 128))],
        out_specs=[pl.BlockSpec((tm, GW_TILE), rev), _full((SUBLANES, 128))],
        out_shape=[jax.ShapeDtypeStruct((s, GW_TILE), BF16), jax.ShapeDtypeStruct((SUBLANES, 128), F32)],
        scratch_shapes=[pltpu.VMEM((SUBLANES, 128), F32)],
        compiler_params=_cparams(48, ("arbitrary",)),
    )(dkx, z, sel)


def _in_proj_bwd(pieces, wp, x, g1, dx2, *, tm):
    s = x.shape[0]

    def body(q_ref, k_ref, v_ref, bcu_ref, f_ref, w_ref, x_ref, g_ref, dx2_ref, dx_ref, dg_ref):
        @pl.when(pl.program_id(0) == 0)
        def _():
            dg_ref[...] = jnp.zeros_like(dg_ref)

        dh = None
        for ref, (lo, hi) in zip((q_ref, k_ref, v_ref, bcu_ref, f_ref), PIECES):
            part = lax.dot_general(ref[...], w_ref[:, lo:hi], NT, preferred_element_type=F32)
            dh = part if dh is None else dh + part
        _, n, r = _rms_fwd(x_ref[...], g_ref[...])
        dxn, dg = _rms_bwd(dh, n, r, g_ref[...])
        dx_ref[...] = dx2_ref[...] + dxn
        dg_ref[...] += _fold8(dg)

    return pl.pallas_call(
        body, name="in_proj_bwd", grid=(s // tm,),
        in_specs=[_rows(tm, hi - lo) for lo, hi in PIECES] + [_resident((D, WP)), _rows(tm, D), _full((1, D)), _rows(tm, D)],
        out_specs=[_rows(tm, D), _full((SUBLANES, D))],
        out_shape=[jax.ShapeDtypeStruct((s, D), F32), jax.ShapeDtypeStruct((SUBLANES, D), F32)],
        compiler_params=_cparams(56, ("arbitrary",)),
    )(*pieces, wp, x, g1, dx2)


def _position():
    return lax.axis_index("x"), lax.axis_index("y"), lax.axis_index("c")


ANY = pl.BlockSpec(memory_space=pl.ANY)


def _all_gather(shards):
    n = len(shards)

    def body(*refs):
        x_refs, out_refs = refs[:n], refs[n:2 * n]
        send_sems, recv_sems, local_sems = refs[2 * n:]
        x, y, c = _position()
        me, sibling = (x, y, c), (x, y, 1 - c)
        chips = [(1 - x, y), (x, 1 - y), (1 - x, 1 - y)]

        def copy(a, k, block, to, own=False):
            slot = out_refs[a].at[4 * block[0] + 2 * block[1] + block[2]]
            return pltpu.make_async_remote_copy(
                src_ref=x_refs[a] if own else slot, dst_ref=slot,
                send_sem=send_sems.at[7 * a + k], recv_sem=recv_sems.at[7 * a + k], device_id=to, device_id_type=MESH_ID)

        mine = [pltpu.make_async_copy(x_refs[a], out_refs[a].at[4 * x + 2 * y + c], local_sems.at[a]) for a in range(n)]
        for cp in mine:
            cp.start()
        first = []
        for a in range(n):
            first.append(copy(a, 0, me, sibling, own=True))
            first += [copy(a, 1 + j, me, (*chip, c), own=True) for j, chip in enumerate(chips)]
        for cp in first:
            cp.start()
        passed = []
        for j, chip in enumerate(chips):
            for a in range(n):
                copy(a, 1 + j, (*chip, c), me).wait_recv()
                fwd = copy(a, 4 + j, (*chip, c), sibling)
                fwd.start()
                passed.append(fwd)
        for a in range(n):
            copy(a, 0, sibling, me).wait_recv()
            for j, chip in enumerate(chips):
                copy(a, 4 + j, (*chip, 1 - c), me).wait_recv()
        for cp in first + passed:
            cp.wait_send()
        for cp in mine:
            cp.wait()

    return pl.pallas_call(
        body, name="all_gather_weights",
        out_shape=[jax.ShapeDtypeStruct((NDEV,) + sh.shape, sh.dtype) for sh in shards],
        in_specs=[ANY] * n, out_specs=[ANY] * n,
        scratch_shapes=[pltpu.SemaphoreType.DMA((7 * n,)), pltpu.SemaphoreType.DMA((7 * n,)), pltpu.SemaphoreType.DMA((n,))],
    )(*shards)


def _pair_exchange(grads):
    n = len(grads)

    def body(*refs):
        g_refs, out_refs = refs[:n], refs[n:2 * n]
        send_sems, recv_sems = refs[2 * n:]
        x, y, c = _position()
        copies = [pltpu.make_async_remote_copy(
            src_ref=g_refs[a].at[:, pl.ds(1 - c, 1)], dst_ref=out_refs[a], send_sem=send_sems.at[a],
            recv_sem=recv_sems.at[a], device_id=(x, y, 1 - c), device_id_type=MESH_ID) for a in range(n)]
        for cp in copies:
            cp.start()
        for cp in copies:
            cp.wait()

    return pl.pallas_call(
        body, name="grad_pair_exchange",
        out_shape=[jax.ShapeDtypeStruct((4, 1) + g.shape[2:], g.dtype) for g in grads],
        in_specs=[ANY] * n, out_specs=[ANY] * n,
        scratch_shapes=[pltpu.SemaphoreType.DMA((n,)), pltpu.SemaphoreType.DMA((n,))],
    )(*grads)


def _pair_sum(g, got, idx, *, tr, name):
    r, c = g.shape[2:]

    def body(idx_ref, g_ref, got_ref, pb_ref, own_ref):
        p = g_ref[0, 0].astype(F32) + got_ref[0, 0].astype(F32)
        pb_ref[0] = p.astype(BF16)

        @pl.when(pl.program_id(1) == idx_ref[1])
        def _():
            own_ref[...] = p

    return pl.pallas_call(
        body, name=name,
        grid_spec=pltpu.PrefetchScalarGridSpec(
            num_scalar_prefetch=1, grid=(r // tr, 4),
            in_specs=[pl.BlockSpec((1, 1, tr, c), lambda i, j, idx: (j, idx[0], i, 0)),
                      pl.BlockSpec((1, 1, tr, c), lambda i, j, idx: (j, 0, i, 0))],
            out_specs=[pl.BlockSpec((1, tr, c), lambda i, j, idx: (j, i, 0)),
                       pl.BlockSpec((tr, c), lambda i, j, idx: (i, 0))]),
        out_shape=[jax.ShapeDtypeStruct((4, r, c), BF16), jax.ShapeDtypeStruct((r, c), F32)],
        compiler_params=_cparams(32, ("arbitrary", "arbitrary")),
    )(idx, g, got)


HBM = pl.BlockSpec(memory_space=pltpu.HBM)
SEM = pl.BlockSpec(memory_space=pltpu.SEMAPHORE)
DATAFLOW = pltpu.SideEffectType.DATAFLOW_SIDE_EFFECTING


PEERS = {"gather": NDEV - 1, "scatter": NDEV - 1, "chips": 3}


def _exchange_copies(src_refs, land_refs, send_sems, recv_sems, mode):
    x, y, c = _position()
    me, my_chip = 4 * x + 2 * y + c, 2 * x + y
    npeers = PEERS[mode]
    copies = []
    for a, (s_ref, l_ref) in enumerate(zip(src_refs, land_refs)):
        for k in range(npeers):
            if mode == "chips":
                px, py, pc = x ^ ((k + 1) >> 1), y ^ ((k + 1) & 1), c
                src, dst = s_ref.at[2 * px + py], l_ref.at[my_chip]
            else:
                px, py, pc = x ^ ((k + 1) >> 2), y ^ (((k + 1) >> 1) & 1), c ^ ((k + 1) & 1)
                src, dst = (s_ref.at[4 * px + 2 * py + pc] if mode == "scatter" else s_ref), l_ref.at[me]
            copies.append(pltpu.make_async_remote_copy(
                src_ref=src, dst_ref=dst, send_sem=send_sems.at[npeers * a + k], recv_sem=recv_sems.at[npeers * a + k],
                device_id=(px, py, pc), device_id_type=MESH_ID))
    return copies


def _exchange_start(srcs, lands, after, *, mode, name):
    n = len(srcs)
    nsem = PEERS[mode] * n

    def body(*refs):
        token = refs[-1]
        for cp in _exchange_copies(refs[:n], refs[n:2 * n], refs[2 * n + 1], refs[2 * n + 2], mode):
            cp.start()
        token[...] = jnp.zeros_like(token)

    arrays = list(srcs) + list(lands)
    outs = pl.pallas_call(
        body, name=name,
        out_shape=(pltpu.SemaphoreType.DMA((nsem,)), pltpu.SemaphoreType.DMA((nsem,)),
                   *[pltpu.HBM(a.shape, a.dtype) for a in arrays], jax.ShapeDtypeStruct((SUBLANES, LANES), F32)),
        in_specs=[HBM] * (2 * n) + [ANY],
        out_specs=(SEM, SEM, *[HBM] * (2 * n), pl.BlockSpec(memory_space=pltpu.VMEM)),
        input_output_aliases={i: 2 + i for i in range(2 * n)},
        compiler_params=pltpu.CompilerParams(has_side_effects=DATAFLOW),
    )(*[pltpu.with_memory_space_constraint(a, pltpu.HBM) for a in arrays], after)
    return outs[0], outs[1], outs[2:2 + n], outs[2 + n:2 + 2 * n], outs[-1]


def _exchange_wait(send_sems, recv_sems, srcs, lands, after, *, mode, name):
    n = len(srcs)

    def body(*refs):
        for cp in _exchange_copies(refs[:n], refs[n:2 * n], refs[2 * n], refs[2 * n + 1], mode):
            cp.wait_send()
            cp.wait_recv()

    arrays = list(srcs) + list(lands)
    outs = pl.pallas_call(
        body, name=name,
        out_shape=tuple(pltpu.HBM(a.shape, a.dtype) for a in arrays),
        in_specs=[HBM] * (2 * n) + [SEM, SEM, ANY],
        out_specs=tuple([HBM] * (2 * n)),
        input_output_aliases={i: i for i in range(2 * n)},
        compiler_params=pltpu.CompilerParams(has_side_effects=DATAFLOW),
    )(*arrays, send_sems, recv_sems, after)
    return outs[n:]


def _own_slot(value, me):
    return lax.dynamic_update_index_in_dim(lax.empty((NDEV,) + value.shape, value.dtype), value, me, 0)


def _small_all_reduce(parts):
    def body(gmp_ref, gmo_ref, gfp_ref, gfo_ref, ga_ref, gc_ref, dw_ref, bf_ref, loss_ref,
             out_ref, buf, send_sems, recv_sems):
        x, y, c = _position()
        me = 4 * x + 2 * y + c

        def colsum(v):
            return jnp.sum(v, axis=0, keepdims=True)

        loss = jnp.sum(colsum(loss_ref[...]), axis=1, keepdims=True) * (0.5 / D)
        rows = [colsum(gmp_ref[...]), colsum(gmo_ref[...]), colsum(gfp_ref[...]), colsum(gfo_ref[...]),
                jnp.concatenate([colsum(ga_ref[...]), colsum(gc_ref[...])], axis=1),
                jnp.concatenate([colsum(dw_ref[0]), colsum(dw_ref[1])], axis=1),
                jnp.concatenate([colsum(dw_ref[2]), colsum(bf_ref[...]), jnp.broadcast_to(loss, (1, 128)),
                                 jnp.zeros((1, 256), F32)], axis=1),
                jnp.zeros((1, D), F32)]
        buf[me] = jnp.concatenate(rows, axis=0)
        copies = []
        for mm in range(1, NDEV):
            peer = (x ^ (mm >> 2), y ^ ((mm >> 1) & 1), c ^ (mm & 1))
            copies.append(pltpu.make_async_remote_copy(
                src_ref=buf.at[me], dst_ref=buf.at[me], send_sem=send_sems.at[mm - 1], recv_sem=recv_sems.at[mm - 1],
                device_id=peer, device_id_type=MESH_ID))
        for cp in copies:
            cp.start()
        for cp in copies:
            cp.wait_recv()
        for cp in copies:
            cp.wait_send()
        acc = buf[0]
        for d in range(1, NDEV):
            acc = acc + buf[d]
        out_ref[...] = acc

    vm = pl.BlockSpec(memory_space=pltpu.VMEM)
    return pl.pallas_call(
        body, name="small_all_reduce",
        out_shape=jax.ShapeDtypeStruct((SUBLANES, D), F32),
        in_specs=[vm] * len(parts), out_specs=vm,
        scratch_shapes=[pltpu.VMEM((NDEV, SUBLANES, D), F32), pltpu.SemaphoreType.DMA((7,)), pltpu.SemaphoreType.DMA((7,))],
    )(*parts)


def _adam_update(w, g, m, v):
    nm = ADAM_B1 * m + (1.0 - ADAM_B1) * g
    nv = ADAM_B2 * v + (1.0 - ADAM_B2) * (g * g)
    m_hat = nm / (1.0 - ADAM_B1 ** ADAM_STEP)
    v_hat = nv / (1.0 - ADAM_B2 ** ADAM_STEP)
    return -ADAM_LR * (m_hat / (jnp.sqrt(v_hat) + ADAM_EPS) + ADAM_WD * w), nm, nv


SMALL_SLOTS = {"g_mix_pre": (0, 0, D), "g_mix_post": (1, 0, D), "g_ffn_pre": (2, 0, D), "g_ffn_post": (3, 0, D),
               "g_attn_out": (4, 0, AW), "g_conv_out": (4, AW, CW), "b_forget": (6, CW, H)}


def _small_adamw(small, conv_grad, params):
    names = list(params)
    n = len(names)

    def body(*refs):
        small_ref, cg_ref = refs[0], refs[1]
        ins, outs = refs[2:2 + 3 * n], refs[2 + 3 * n:]
        for i, name in enumerate(names):
            w_ref, m_ref, v_ref = ins[3 * i:3 * i + 3]
            g_ref, d_ref, nm_ref, nv_ref = outs[4 * i:4 * i + 4]
            if name == "conv_w":
                g = cg_ref[...]
            else:
                r, c0, width = SMALL_SLOTS[name]
                g = small_ref[r:r + 1, c0:c0 + width]
            g_ref[...] = g
            d_ref[...], nm_ref[...], nv_ref[...] = _adam_update(w_ref[...], g, m_ref[...], v_ref[...])

    vm = pl.BlockSpec(memory_space=pltpu.VMEM)
    flat = [a for name in names for a in params[name]]
    outs = pl.pallas_call(
        body, name="adamw_small",
        in_specs=[vm] * (2 + 3 * n), out_specs=[vm] * (4 * n),
        out_shape=[jax.ShapeDtypeStruct(params[name][0].shape, F32) for name in names for _ in range(4)],
    )(small, conv_grad, *flat)
    return {name: outs[4 * i:4 * i + 4] for i, name in enumerate(names)}


def _chip_sum_adamw(got, own, idx, wt, mt, vt, *, tr, name):
    cols, rows = wt.shape
    gcols = own.shape[1]

    def body(idx_ref, got_ref, own_ref, w_ref, m_ref, v_ref, g_ref, d_ref, nm_ref, nv_ref):
        g = jnp.zeros((tr, gcols), F32)
        for j in range(4):
            g = g + jnp.where(idx_ref[1] == j, own_ref[...], got_ref[j].astype(F32))
        g = g.T[:cols]
        g_ref[...] = g
        d_ref[...], nm_ref[...], nv_ref[...] = _adam_update(w_ref[...], g, m_ref[...], v_ref[...])

    spec = pl.BlockSpec((cols, tr), lambda i, idx: (0, i))
    gspec = pl.BlockSpec((tr, gcols), lambda i, idx: (i, 0))
    return pl.pallas_call(
        body, name=name,
        grid_spec=pltpu.PrefetchScalarGridSpec(
            num_scalar_prefetch=1, grid=(rows // tr,),
            in_specs=[pl.BlockSpec((4, tr, gcols), lambda i, idx: (0, i, 0)), gspec, spec, spec, spec],
            out_specs=[spec] * 4),
        out_shape=[jax.ShapeDtypeStruct((cols, rows), F32)] * 4,
        compiler_params=_cparams(32, ("arbitrary",)),
    )(idx, got, own, wt, mt, vt)


def _device_sum_adamw(land, w, m, v, *, tr, name):
    rows, cols = w.shape

    def body(land_ref, w_ref, m_ref, v_ref, g_ref, d_ref, nm_ref, nv_ref):
        g = land_ref[0].astype(F32)
        for dev in range(1, NDEV):
            g = g + land_ref[dev].astype(F32)
        g_ref[...] = g
        d_ref[...], nm_ref[...], nv_ref[...] = _adam_update(w_ref[...], g, m_ref[...], v_ref[...])

    spec = pl.BlockSpec((tr, cols), lambda i: (i, 0))
    return pl.pallas_call(
        body, name=name, grid=(rows // tr,),
        in_specs=[pl.BlockSpec((NDEV, tr, cols), lambda i: (0, i, 0)), spec, spec, spec],
        out_specs=[spec] * 4,
        out_shape=[jax.ShapeDtypeStruct((rows, cols), F32)] * 4,
        compiler_params=_cparams(32, ("arbitrary",)),
    )(land, w, m, v)


def _placement_constants():
    j = jnp.arange(128)[:, None]
    lane = jnp.arange(1024)[None, :]
    head, sub = lane // HP, lane % HP
    piece, jh = j // H, j % H
    valid = (j < 3 * H) & (jh == head)
    pq = jnp.where(valid & (sub == DH + piece), 1.0, 0.0).astype(BF16)
    pk = jnp.where(valid & (sub == DH + 3 + piece), -1.0, 0.0).astype(BF16)
    oq = jnp.where((sub >= DH + 3) & (sub < DH + 6), 1.0, 0.0).astype(F32)
    ok = jnp.where((sub >= DH) & (sub < DH + 3), 1.0, 0.0).astype(F32)
    r = jnp.arange(AW)[:, None]
    cc = jnp.arange(128)[None, :]
    sel = jnp.where((r % DH == 3) & (r // DH == cc), -1.0, 0.0).astype(BF16)
    gi = jnp.arange(GS)
    gsum = (gi[:, None] // DH == gi[None, :] // DH).astype(BF16)
    return pq, pk, oq, ok, sel, gsum


def _local_step(xs, tgt, wp, late_weights, cw8, bfp, g_attn_out, g_conv_out,
                g_mix_pre, g_mix_post, g_ffn_pre, g_ffn_post, early_grads=None, last_grad=None):
    pq, pk, oq, ok, sel, gsum = _placement_constants()
    h1t, qp, kp, vv, bcu, zf = _in_proj(xs, g_mix_pre, wp, bfp, pq, pk, oq, ok, tm=512)
    o, lse, mk = _attn_fwd(qp, kp, vv, t=512)
    w_out_f, wgu, wd = late_weights(lse)
    merged, y, x2, cv, h2 = _mix_out(o, bcu, cw8, g_attn_out, g_conv_out, gsum, w_out_f, xs, g_mix_post, g_ffn_pre, tm=512)
    gate, up, act, dx3, dff, loss_p, dg_ffn_post = _ffn_fwd_loss(h2, wgu, wd, x2, tgt, g_ffn_post, tm=512)

    dgu, dx2, dy, dg_ffn_pre, dg_mix_post = _ffn_bwd(dff, wd, gate, up, wgu, x2, g_ffn_pre, dx3, y, g_mix_post, tm=256)
    dw_down = _grad_matmul_blocks(act, dff, ts=4096, name="grad_w_down")
    dw_gu = _grad_matmul_blocks(dgu.reshape(NDEV, -1, FB), h2, ts=4096, name="grad_w_gate_up")
    dw_out = _grad_matmul(merged, dy, ta=1024, tb=1024, ts=2048, name="grad_w_out")
    token = early_grads(dw_out, dw_gu, dw_down) if early_grads is not None else None
    ga = g_attn_out if token is None else g_attn_out + token[0:1, 0:1]
    do, dl, dcv, db, dg_attn, dg_conv = _mix_bwd(dy, w_out_f, o, cv, bcu, ga, g_conv_out, gsum, tm=512)
    dbcu, dtaps = _conv_bwd(dcv, db, bcu, cw8, tm=512)
    dqp, dkp, dv, dkx = _attn_bwd(qp, kp, vv, do, lse, dl, mk, t=512)
    dfl, dbf = _forget_bwd(dkx, zf, sel, tm=512)
    pieces = (dqp, dkp, dv, dbcu, dfl)
    dwp = _grad_w_in(h1t, pieces)
    token = last_grad(dwp) if last_grad is not None else None
    g1 = g_mix_pre if token is None else g_mix_pre + token[0:1, 0:1]
    grad_x, dg_mix_pre = _in_proj_bwd(pieces, wp, xs, g1, dx2, tm=512)
    return (grad_x, dwp, dw_out, dw_gu, dw_down, dg_mix_pre, dg_mix_post, dg_ffn_pre, dg_ffn_post, dg_attn, dg_conv,
            dtaps, dbf, loss_p)


BIG_TILES = {"w_in": 256, "w_out": 128, "w_gate_up": 176, "w_down": 176}


def kernel(x, w_in, b_forget, conv_w, g_attn_out, g_conv_out, w_out, g_mix_pre, g_mix_post, w_gate_up, w_down, g_ffn_pre, g_ffn_post, loss_target, m_w_in, m_b_forget, m_conv_w, m_g_attn_out, m_g_conv_out, m_w_out, m_g_mix_pre, m_g_mix_post, m_w_gate_up, m_w_down, m_g_ffn_pre, m_g_ffn_post, v_w_in, v_b_forget, v_conv_w, v_g_attn_out, v_g_conv_out, v_w_out, v_g_mix_pre, v_g_mix_post, v_w_gate_up, v_w_down, v_g_ffn_pre, v_g_ffn_post):
    xc, yc, cc = _position()
    my_chip = 2 * xc + yc
    me = 2 * my_chip + cc
    idx = jnp.stack([cc, my_chip]).astype(jnp.int32)
    tables = _in_layout_tables()

    w_in_b = w_in[0].astype(BF16)
    g_in, g_last, g_taps = _all_gather([w_in_b[:, :IN_MAIN], w_in_b[:, IN_MAIN].reshape(SUBLANES, LANES), conv_w[0]])
    last_cols = jnp.pad(g_last.reshape(NDEV, D).T.astype(F32), ((0, 0), (0, LANES - NDEV)))
    wp = _assemble_w_in(g_in, last_cols, tables, tr=256)
    cw8 = jnp.pad(g_taps.transpose(1, 0, 2).reshape(3, CW), ((0, SUBLANES - 3), (0, 0)))

    late = [w_out[0].astype(BF16), w_gate_up[0].T.astype(BF16), w_down[0].astype(BF16)]
    ssem, rsem, late_thru, land_thru, token = _exchange_start(
        late, [_own_slot(s, me) for s in late], g_in, mode="gather", name="gather_late_start")
    bfp = jnp.pad(b_forget, ((0, 0), (0, 128 - H))) + token[0:1, :]

    def late_weights(after):
        l_out, l_gu, l_down = _exchange_wait(ssem, rsem, late_thru, land_thru, after, mode="gather", name="gather_late_wait")
        return l_out.reshape(D, D), l_gu.reshape(2, 4, FB, D), l_down.reshape(4, FB, D)

    early = {}

    def early_grads(dw_out, dw_gu, dw_down):
        srcs = [dw_out.reshape(NDEV, D // NDEV, D), dw_gu, dw_down.reshape(NDEV, DFF // NDEV, D)]
        lands = [_own_slot(lax.dynamic_index_in_dim(s, me, 0, keepdims=False), me) for s in srcs]
        early["handles"] = _exchange_start(srcs, lands, dw_out, mode="scatter", name="scatter_early_start")
        return early["handles"][4]

    last = {}

    def last_grad(dwp):
        g_w_in = _disassemble_w_in(dwp, tables, tr=256).reshape(4, 2, D, IN_PAD)
        (from_sibling,) = _pair_exchange([g_w_in])
        pair_b, last["own"] = _pair_sum(g_w_in, from_sibling, idx, tr=BIG_TILES["w_in"], name="grad_pair_sum_w_in")
        land = lax.dynamic_update_index_in_dim(lax.empty(pair_b.shape, pair_b.dtype),
                                               lax.dynamic_index_in_dim(pair_b, my_chip, 0, keepdims=False), my_chip, 0)
        last["handles"] = _exchange_start([pair_b], [land], last["own"], mode="chips", name="chips_w_in_start")
        return last["handles"][4]

    (grad_x, dwp, dw_out, dw_gu, dw_down, dg_mix_pre, dg_mix_post, dg_ffn_pre, dg_ffn_post, dg_attn, dg_conv,
     dtaps, dbf, loss_p) = _local_step(x[0], loss_target[0], wp, late_weights, cw8, bfp, g_attn_out, g_conv_out,
                                        g_mix_pre, g_mix_post, g_ffn_pre, g_ffn_post, early_grads, last_grad)

    e_ssem, e_rsem, e_srcs, e_lands, _ = early["handles"]
    land_out, land_gu, land_down = _exchange_wait(e_ssem, e_rsem, e_srcs, e_lands, dg_mix_pre, mode="scatter",
                                                  name="scatter_early_wait")
    res = {}
    big = {"w_out": (land_out, w_out[0], m_w_out[0], v_w_out[0]),
           "w_gate_up": (land_gu, w_gate_up[0].T, m_w_gate_up[0].T, v_w_gate_up[0].T),
           "w_down": (land_down, w_down[0], m_w_down[0], v_w_down[0])}
    for name, (land, w, m, v) in big.items():
        outs = _device_sum_adamw(land, w, m, v, tr=BIG_TILES[name], name="adamw_" + name)
        res[name] = [(o.T if name == "w_gate_up" else o)[None] for o in outs]
    c_ssem, c_rsem, c_srcs, c_lands, _ = last["handles"]
    after = sum(res[n][1][0, :SUBLANES, :LANES] for n in big)
    (from_chips,) = _exchange_wait(c_ssem, c_rsem, c_srcs, c_lands, after, mode="chips", name="chips_w_in_wait")
    outs = _chip_sum_adamw(from_chips, last["own"], idx, w_in[0].T, m_w_in[0].T, v_w_in[0].T,
                           tr=BIG_TILES["w_in"], name="adamw_w_in")
    res["w_in"] = [o.T[None] for o in outs]

    small = _small_all_reduce([dg_mix_pre, dg_mix_post, dg_ffn_pre, dg_ffn_post, dg_attn, dg_conv, dtaps, dbf, loss_p])
    taps_full = jnp.concatenate([small[5:6, :CW], small[5:6, CW:], small[6:7, :CW]], axis=0)
    loss = small[6, CW + 128]
    smalls = {"b_forget": (b_forget, m_b_forget, v_b_forget), "conv_w": (conv_w[0], m_conv_w[0], v_conv_w[0]),
              "g_attn_out": (g_attn_out, m_g_attn_out, v_g_attn_out), "g_conv_out": (g_conv_out, m_g_conv_out, v_g_conv_out),
              "g_mix_pre": (g_mix_pre, m_g_mix_pre, v_g_mix_pre), "g_mix_post": (g_mix_post, m_g_mix_post, v_g_mix_post),
              "g_ffn_pre": (g_ffn_pre, m_g_ffn_pre, v_g_ffn_pre), "g_ffn_post": (g_ffn_post, m_g_ffn_post, v_g_ffn_post)}
    for name, outs in _small_adamw(small, lax.dynamic_slice(taps_full, (0, me * 64), (3, 64)), smalls).items():
        res[name] = [o[None] for o in outs] if name == "conv_w" else list(outs)

    order = ["w_in", "b_forget", "conv_w", "g_attn_out", "g_conv_out", "w_out", "g_mix_pre", "g_mix_post",
             "w_gate_up", "w_down", "g_ffn_pre", "g_ffn_post"]
    outs = [loss, grad_x[None]]
    for k in range(4):
        outs += [res[n][k] for n in order]
    return tuple(outs)
```

```python
import functools

import numpy as np

import jax
import jax.numpy as jnp
from jax import lax
from jax.experimental import pallas as pl
from jax.experimental.pallas import tpu as pltpu

F32 = jnp.float32
BF16 = jnp.bfloat16
MESH_ID = pl.DeviceIdType.MESH

D = 1024
H = 8
DH = 64
AW = 512
CW = 512
DFF = 2816
FB = DFF // 4
HP = 128
OFF_Q, OFF_K, OFF_V, OFF_BCU, OFF_F = 0, 512, 1024, 1536, 3072
WP = OFF_F + 128
PIECES = ((OFF_Q, OFF_K), (OFF_K, OFF_V), (OFF_V, OFF_BCU), (OFF_BCU, OFF_F), (OFF_F, WP))
EPS = 1e-6
NDEV = 8
LANES = 128
SUBLANES = 8
IN_COLS = 385
IN_PAD = 512
IN_MAIN = 384
WIN = 640
ADAM_LR, ADAM_B1, ADAM_B2, ADAM_EPS, ADAM_WD, ADAM_STEP = 0.001, 0.9, 0.999, 1e-08, 0.01, 10

NT = (((1,), (1,)), ((), ()))
TN = (((0,), (0,)), ((), ()))


def _cparams(vmem_mb=None, sem=None):
    kw = {}
    if vmem_mb is not None:
        kw["vmem_limit_bytes"] = vmem_mb << 20
    if sem is not None:
        kw["dimension_semantics"] = sem
    return pltpu.CompilerParams(**kw)


def _full(shape):
    return pl.BlockSpec(shape, lambda *_: (0,) * len(shape))


def _resident(shape):
    return pl.BlockSpec(shape, lambda *_: (0,) * len(shape), pipeline_mode=pl.Buffered(1))


def _rows(tm, width):
    return pl.BlockSpec((tm, width), lambda i: (i, 0))


def _fold8(v):
    r, w = v.shape
    return jnp.sum(v.reshape(r // SUBLANES, SUBLANES, w), axis=0)


def _split_dot(v, m01):
    hi = v.astype(BF16)
    lo = (v - hi.astype(F32)).astype(BF16)
    return (jnp.dot(hi, m01, preferred_element_type=F32)
            + jnp.dot(lo, m01, preferred_element_type=F32))


GS = 256


def _group_sum(v, g01):
    parts = [_split_dot(v[:, c:c + GS], g01) for c in range(0, v.shape[1], GS)]
    return parts[0] if len(parts) == 1 else jnp.concatenate(parts, axis=1)


def _exact_dot01(m01, v):
    p1 = v.astype(BF16)
    r1 = v - p1.astype(F32)
    p2 = r1.astype(BF16)
    p3 = (r1 - p2.astype(F32)).astype(BF16)
    return (jnp.dot(m01, p1, preferred_element_type=F32) + jnp.dot(m01, p2, preferred_element_type=F32)
            + jnp.dot(m01, p3, preferred_element_type=F32))


def _rms_fwd(v, g):
    r = lax.rsqrt(jnp.mean(v * v, axis=-1, keepdims=True) + EPS)
    n = v * r
    return n * g, n, r


def _rms_bwd(do, n, r, g):
    dn = do * g
    return r * (dn - n * jnp.mean(dn * n, axis=-1, keepdims=True)), do * n


def _padded_column(n):
    if n < AW:
        return OFF_Q + n, 0.125
    if n < 3 * AW:
        return n, 1.0
    if n < 3 * AW + H:
        return OFF_F + n - 3 * AW, 1.0
    return OFF_BCU + n - 3 * AW - H, 1.0


def _in_layout_tables():
    dest = -np.ones((IN_PAD, LANES), np.int32)
    dest_f = -np.ones((IN_PAD, LANES), np.int32)
    scale = np.zeros((IN_PAD, LANES), np.float32)
    starts = []
    for k in range(NDEV):
        cols = [_padded_column(IN_COLS * k + j) for j in range(IN_COLS)]
        main = [c for c, _ in cols if c < OFF_F]
        ws = min((min(main) // LANES) * LANES, OFF_F - WIN)
        assert ws <= min(main) and max(main) < ws + WIN
        starts.append(ws)
        for j, (c, sc) in enumerate(cols):
            scale[j, k] = sc
            if c < OFF_F:
                dest[j, k] = c - ws
            else:
                dest_f[j, k] = c - OFF_F
    f_shards = tuple(k for k in range(NDEV) if (dest_f[:, k] >= 0).any())
    return tuple(starts), f_shards, jnp.asarray(dest), jnp.asarray(dest_f), jnp.asarray(scale)


def _perm(dest_ref, scale_ref, k, width, rows=IN_PAD):
    lane = lax.broadcasted_iota(jnp.int32, (rows, width), 1)
    return jnp.where(dest_ref[0:rows, k:k + 1] == lane, scale_ref[0:rows, k:k + 1], 0.0).astype(BF16)


def _assemble_w_in(blocks, last_cols, tables, *, tr):
    starts, f_shards, dest, dest_f, scale = tables
    last = [_padded_column(IN_COLS * k + IN_MAIN) for k in range(NDEV)]
    f_main = [any(_padded_column(IN_COLS * k + j)[0] >= OFF_F for j in range(IN_MAIN)) for k in range(NDEV)]
    assert IN_COLS == IN_MAIN + 1

    def body(b_ref, c_ref, dest_ref, destf_ref, scale_ref, o_ref):
        o_ref[...] = jnp.zeros_like(o_ref)
        lane = lax.broadcasted_iota(jnp.int32, (tr, LANES), 1)
        for k in range(NDEV):
            b = b_ref[k]
            ws = starts[k]
            part = jnp.dot(b, _perm(dest_ref, scale_ref, k, WIN, IN_MAIN), preferred_element_type=F32)
            o_ref[:, ws:ws + WIN] += part.astype(BF16)
            if f_main[k]:
                part = jnp.dot(b, _perm(destf_ref, scale_ref, k, 128, IN_MAIN), preferred_element_type=F32)
                o_ref[:, OFF_F:WP] += part.astype(BF16)
            col, sc = last[k]
            tile = (col // LANES) * LANES
            o_ref[:, tile:tile + LANES] += jnp.where(lane == col - tile, c_ref[:, k:k + 1] * sc, 0.0).astype(BF16)

    tab = _full((IN_PAD, LANES))
    return pl.pallas_call(
        body, name="assemble_w_in", grid=(D // tr,),
        in_specs=[pl.BlockSpec((NDEV, tr, IN_MAIN), lambda i: (0, i, 0)), _rows(tr, LANES), tab, tab, tab],
        out_specs=_rows(tr, WP),
        out_shape=jax.ShapeDtypeStruct((D, WP), BF16),
        compiler_params=_cparams(48, ("arbitrary",)),
    )(blocks, last_cols, dest, dest_f, scale)


def _disassemble_w_in(dwp, tables, *, tr):
    starts, f_shards, dest, dest_f, scale = tables
    width = dwp.shape[1]

    def body(g_ref, dest_ref, destf_ref, scale_ref, o_ref):
        for k in range(NDEV):
            ws = starts[k]
            acc = lax.dot_general(g_ref[:, ws:ws + WIN], _perm(dest_ref, scale_ref, k, WIN), NT, preferred_element_type=F32)
            if k in f_shards:
                acc = acc + lax.dot_general(g_ref[:, OFF_F:WP], _perm(destf_ref, scale_ref, k, 128), NT,
                                            preferred_element_type=F32)
            o_ref[k] = acc.astype(BF16)

    tab = _full((IN_PAD, LANES))
    return pl.pallas_call(
        body, name="disassemble_w_in", grid=(D // tr,),
        in_specs=[_rows(tr, width), tab, tab, tab],
        out_specs=pl.BlockSpec((NDEV, tr, IN_PAD), lambda i: (0, i, 0)),
        out_shape=jax.ShapeDtypeStruct((NDEV, D, IN_PAD), BF16),
        compiler_params=_cparams(48, ("arbitrary",)),
    )(dwp, dest, dest_f, scale)


def _in_proj(x, g1, wp, bfp, pq, pk, oq, ok, *, tm):
    s = x.shape[0]

    def body(x_ref, g_ref, w_ref, bf_ref, pq_ref, pk_ref, oq_ref, ok_ref,
             ht_ref, qp_ref, kp_ref, v_ref, bcu_ref, z_ref, carry):
        @pl.when(pl.program_id(0) == 0)
        def _():
            carry[...] = jnp.zeros_like(carry)

        h = _rms_fwd(x_ref[...], g_ref[...])[0].astype(BF16)
        ht_ref[...] = h.T
        z = jnp.dot(h, w_ref[:, OFF_F:WP], preferred_element_type=F32) + bf_ref[...]
        z_ref[...] = z
        lane = lax.broadcasted_iota(jnp.int32, (tm, 128), 1)
        logf = jnp.where(lane < H, jnp.minimum(z, 0.0) - jnp.log(1.0 + jnp.exp(-jnp.abs(z))), 0.0)
        row = lax.broadcasted_iota(jnp.int32, (tm, tm), 0)
        col = lax.broadcasted_iota(jnp.int32, (tm, tm), 1)
        tri = (col <= row).astype(BF16)
        c = _exact_dot01(tri, logf) + carry[0:1, :]
        carry[...] = jnp.broadcast_to(c[tm - 1:tm, :], carry.shape)
        c1 = c.astype(BF16).astype(F32)
        r1 = c - c1
        c2 = r1.astype(BF16).astype(F32)
        c3 = (r1 - c2).astype(BF16).astype(F32)
        zc = (c1 + pltpu.roll(c2, 8, axis=1) + pltpu.roll(c3, 16, axis=1)).astype(BF16)

        def pad_heads(v):
            blocks = []
            for pair in range(H // 2):
                two = v[:, 128 * pair:128 * (pair + 1)]
                blocks.append(jnp.where(lane < DH, two, 0.0))
                blocks.append(jnp.where(lane < DH, pltpu.roll(two, DH, axis=1), 0.0))
            return jnp.concatenate(blocks, axis=1)

        q = jnp.dot(h, w_ref[:, OFF_Q:OFF_K], preferred_element_type=F32)
        qp_ref[...] = (pad_heads(q) + jnp.dot(zc, pq_ref[...], preferred_element_type=F32) + oq_ref[...]).astype(BF16)
        k = jnp.dot(h, w_ref[:, OFF_K:OFF_V], preferred_element_type=F32)
        kp_ref[...] = (pad_heads(k) + jnp.dot(zc, pk_ref[...], preferred_element_type=F32) + ok_ref[...]).astype(BF16)
        v = pad_heads(jnp.dot(h, w_ref[:, OFF_V:OFF_BCU], preferred_element_type=F32))
        ones_lane = lax.broadcasted_iota(jnp.int32, (tm, H * HP), 1) % HP == DH
        v_ref[...] = jnp.where(ones_lane, 1.0, v).astype(BF16)
        bcu_ref[...] = jnp.dot(h, w_ref[:, OFF_BCU:OFF_F], preferred_element_type=F32).astype(BF16)

    return pl.pallas_call(
        body, name="in_proj", grid=(s // tm,),
        in_specs=[_rows(tm, D), _full((1, D)), _resident((D, WP)), _full((1, 128)),
                  _full((128, 1024)), _full((128, 1024)), _full((1, 1024)), _full((1, 1024))],
        out_specs=[pl.BlockSpec((D, tm), lambda i: (0, i)), _rows(tm, 1024), _rows(tm, 1024), _rows(tm, 1024),
                   _rows(tm, 3 * CW), _rows(tm, 128)],
        out_shape=[jax.ShapeDtypeStruct((D, s), BF16), jax.ShapeDtypeStruct((s, 1024), BF16),
                   jax.ShapeDtypeStruct((s, 1024), BF16), jax.ShapeDtypeStruct((s, 1024), BF16),
                   jax.ShapeDtypeStruct((s, 3 * CW), BF16), jax.ShapeDtypeStruct((s, 128), F32)],
        scratch_shapes=[pltpu.VMEM((SUBLANES, 128), F32)],
        compiler_params=_cparams(56, ("arbitrary",)),
    )(x, g1, wp, bfp, pq, pk, oq, ok)


def _attn_fwd(qp, kp, v, *, t):
    s = qp.shape[0]
    nq = s // t

    def body(q_ref, k_ref, v_ref, o_ref, lse_ref, mk_ref):
        pi = pl.program_id(1)
        row = lax.broadcasted_iota(jnp.int32, (t, t), 0)
        col = lax.broadcasted_iota(jnp.int32, (t, t), 1)
        lane = lax.broadcasted_iota(jnp.int32, (t, 128), 1)

        def head_step(hh, rows, ki, carry, masked):
            m, acc = carry
            off = pl.multiple_of(ki * t, t)
            q = q_ref[rows, HP * hh:HP * (hh + 1)]
            k = k_ref[pl.ds(off, t), HP * hh:HP * (hh + 1)]
            sc = lax.dot_general(q, k, NT, preferred_element_type=F32)
            if masked:
                sc = jnp.where(col <= row, sc, -1e30)
            mn = jnp.maximum(m, jnp.max(sc, axis=-1, keepdims=True))
            p = jnp.exp(sc - mn).astype(BF16)
            acc = jnp.exp(m - mn) * acc + jnp.dot(p, v_ref[pl.ds(off, t), HP * hh:HP * (hh + 1)],
                                                  preferred_element_type=F32)
            return mn, acc

        def step(rows, ki, carry, masked):
            new = tuple(head_step(hh, rows, ki, carry[hh], masked) for hh in range(2))
            mk_ref[ki, rows] = jnp.where(lane < DH, jnp.broadcast_to(new[0][0], (t, 128)),
                                         jnp.broadcast_to(new[1][0], (t, 128)))
            return new

        init = (jnp.full((t, 1), -1e30, F32), jnp.zeros((t, 128), F32))
        top, bottom = slice(0, t), slice(t, 2 * t)

        def quad(j, carry):
            c0, c1 = carry
            c0 = step(top, 2 * j, c0, False)
            c1 = step(bottom, 2 * j, c1, False)
            c0 = step(top, 2 * j + 1, c0, False)
            c1 = step(bottom, 2 * j + 1, c1, False)
            return c0, c1

        c0, c1 = lax.fori_loop(0, pi, quad, ((init, init), (init, init)))
        f0 = step(top, 2 * pi, c0, True)
        c1 = step(bottom, 2 * pi, c1, False)
        f1 = step(bottom, 2 * pi + 1, c1, True)
        for rows, ((m0, acc0), (m1, acc1)) in ((top, f0), (bottom, f1)):
            l0, l1 = acc0[:, DH:DH + 1], acc1[:, DH:DH + 1]
            o_ref[rows, :] = jnp.where(lane < DH, acc0 / l0, pltpu.roll(acc1 / l1, DH, axis=1))
            lse_ref[rows, :] = jnp.where(lane < DH, jnp.broadcast_to(m0 + jnp.log(l0), (t, 128)),
                                         jnp.broadcast_to(m1 + jnp.log(l1), (t, 128)))

    return pl.pallas_call(
        body, name="attn_fwd", grid=(H // 2, nq // 2),
        in_specs=[pl.BlockSpec((2 * t, 2 * HP), lambda p, i: (i, p)),
                  pl.BlockSpec((s, 2 * HP), lambda p, i: (0, p)),
                  pl.BlockSpec((s, 2 * HP), lambda p, i: (0, p))],
        out_specs=[pl.BlockSpec((2 * t, 128), lambda p, i: (i, p)), pl.BlockSpec((2 * t, 128), lambda p, i: (i, p)),
                   pl.BlockSpec((nq, 2 * t, 128), lambda p, i: (0, i, p))],
        out_shape=[jax.ShapeDtypeStruct((s, AW), F32), jax.ShapeDtypeStruct((s, AW), F32),
                   jax.ShapeDtypeStruct((nq, s, AW), F32)],
        compiler_params=_cparams(48, ("arbitrary", "arbitrary")),
    )(qp, kp, v)


HALO = 16


def _conv_taps(bcu_ref, halo_ref, first, tm):
    z = bcu_ref[:, CW:2 * CW].astype(F32) * bcu_ref[:, 2 * CW:3 * CW].astype(F32)
    zh = jnp.where(first, 0.0, halo_ref[:, CW:2 * CW].astype(F32) * halo_ref[:, 2 * CW:3 * CW].astype(F32))
    row = lax.broadcasted_iota(jnp.int32, (tm, CW), 0)
    last, before = zh[HALO - 1:HALO, :], zh[HALO - 2:HALO - 1, :]
    z1 = jnp.where(row == 0, last, pltpu.roll(z, 1, axis=0))
    z2 = jnp.where(row == 0, before, jnp.where(row == 1, last, pltpu.roll(z, 2, axis=0)))
    return z, z1, z2


def _halo_before(tm, width):
    return pl.BlockSpec((HALO, width), lambda i: (jnp.maximum(i * (tm // HALO) - 1, 0), 0))


def _mix_out(o, bcu, cw8, ga, gc, gsum, w_out, x, g_post, g_ffn_pre, *, tm):
    s = x.shape[0]

    def body(o_ref, bcu_ref, halo_ref, cw_ref, ga_ref, gc_ref, gs_ref, w_ref, x_ref, g_ref, gf_ref,
             merged_ref, y_ref, x2_ref, cv_ref, h2_ref):
        z, z1, z2 = _conv_taps(bcu_ref, halo_ref, pl.program_id(0) == 0, tm)
        cv = cw_ref[0:1, :] * z2 + cw_ref[1:2, :] * z1 + cw_ref[2:3, :] * z
        cv_ref[...] = cv
        conv = bcu_ref[:, 0:CW].astype(F32) * cv
        ov = o_ref[...]
        ra = lax.rsqrt(_group_sum(ov * ov, gs_ref[...]) * (1.0 / DH) + EPS)
        rc = lax.rsqrt(_group_sum(conv * conv, gs_ref[...]) * (1.0 / DH) + EPS)
        merged = jnp.concatenate([ov * ra * ga_ref[...], conv * rc * gc_ref[...]], axis=1).astype(BF16)
        merged_ref[...] = merged
        y = jnp.dot(merged, w_ref[...], preferred_element_type=F32)
        y_ref[...] = y
        x2 = x_ref[...] + _rms_fwd(y, g_ref[...])[0]
        x2_ref[...] = x2
        h2_ref[...] = _rms_fwd(x2, gf_ref[...])[0].astype(BF16)

    return pl.pallas_call(
        body, name="mix_out", grid=(s // tm,),
        in_specs=[_rows(tm, AW), _rows(tm, 3 * CW), _halo_before(tm, 3 * CW), _full((SUBLANES, CW)),
                  _full((1, AW)), _full((1, CW)), _full((GS, GS)), _resident((D, D)), _rows(tm, D), _full((1, D)),
                  _full((1, D))],
        out_specs=[_rows(tm, D), _rows(tm, D), _rows(tm, D), _rows(tm, CW), _rows(tm, D)],
        out_shape=[jax.ShapeDtypeStruct((s, D), BF16), jax.ShapeDtypeStruct((s, D), F32),
                   jax.ShapeDtypeStruct((s, D), F32), jax.ShapeDtypeStruct((s, CW), F32),
                   jax.ShapeDtypeStruct((s, D), BF16)],
        compiler_params=_cparams(48, ("arbitrary",)),
    )(o, bcu, bcu, cw8, ga, gc, gsum, w_out, x, g_post, g_ffn_pre)


def _ffn_fwd_loss(h2, wgu, wd, x2, target, g_post, *, tm):
    s = x2.shape[0]

    def body(h_ref, w_ref, wd_ref, x2_ref, t_ref, g_ref,
             gate_ref, up_ref, a_ref, dx3_ref, dff_ref, loss_ref, dg_ref):
        @pl.when(pl.program_id(0) == 0)
        def _():
            loss_ref[...] = jnp.zeros_like(loss_ref)
            dg_ref[...] = jnp.zeros_like(dg_ref)

        h = h_ref[...]
        ff = None
        for j in range(4):
            gate = lax.dot_general(h, w_ref[0, j], NT, preferred_element_type=F32)
            up = lax.dot_general(h, w_ref[1, j], NT, preferred_element_type=F32)
            gate_ref[j] = gate.astype(BF16)
            up_ref[j] = up.astype(BF16)
            act = (gate * jax.nn.sigmoid(gate) * up).astype(BF16)
            a_ref[j] = act
            part = jnp.dot(act, wd_ref[j], preferred_element_type=F32)
            ff = part if ff is None else ff + part
        out, n, r = _rms_fwd(ff, g_ref[...])
        e = x2_ref[...] + out - t_ref[...]
        loss_ref[...] += _fold8(e * e)
        dx3 = e * (1.0 / D)
        dx3_ref[...] = dx3
        dff, dg = _rms_bwd(dx3, n, r, g_ref[...])
        dff_ref[...] = dff.astype(BF16)
        dg_ref[...] += _fold8(dg)

    blk4 = pl.BlockSpec((4, tm, FB), lambda i: (0, i, 0))
    return pl.pallas_call(
        body, name="ffn_fwd_loss", grid=(s // tm,),
        in_specs=[_rows(tm, D), _resident((2, 4, FB, D)), _resident((4, FB, D)), _rows(tm, D), _rows(tm, D), _full((1, D))],
        out_specs=[blk4, blk4, blk4, _rows(tm, D), _rows(tm, D), _full((SUBLANES, D)), _full((SUBLANES, D))],
        out_shape=[jax.ShapeDtypeStruct((4, s, FB), BF16)] * 3
        + [jax.ShapeDtypeStruct((s, D), F32), jax.ShapeDtypeStruct((s, D), BF16),
           jax.ShapeDtypeStruct((SUBLANES, D), F32), jax.ShapeDtypeStruct((SUBLANES, D), F32)],
        compiler_params=_cparams(56, ("arbitrary",)),
    )(h2, wgu, wd, x2, target, g_post)


def _ffn_bwd(dff, wd, gate, up, wgu, x2, g_pre, dx3, y, g_post, *, tm):
    s = x2.shape[0]

    def body(dff_ref, wd_ref, gate_ref, up_ref, w_ref, x2_ref, gpre_ref, dx3_ref, y_ref, gpost_ref,
             dgu_ref, dx2_ref, dy_ref, dgpre_ref, dgpost_ref):
        @pl.when(pl.program_id(0) == 0)
        def _():
            dgpre_ref[...] = jnp.zeros_like(dgpre_ref)
            dgpost_ref[...] = jnp.zeros_like(dgpost_ref)

        dff = dff_ref[...]
        dh2 = None
        for j in range(4):
            da = lax.dot_general(dff, wd_ref[j], NT, preferred_element_type=F32)
            g = gate_ref[j].astype(F32)
            sg = jax.nn.sigmoid(g)
            dgate = (da * up_ref[j].astype(F32) * (sg * (1.0 + g * (1.0 - sg)))).astype(BF16)
            dup = (da * (g * sg)).astype(BF16)
            dgu_ref[0, j] = dgate
            dgu_ref[1, j] = dup
            part = (jnp.dot(dgate, w_ref[0, j], preferred_element_type=F32)
                    + jnp.dot(dup, w_ref[1, j], preferred_element_type=F32))
            dh2 = part if dh2 is None else dh2 + part
        _, n2, r2 = _rms_fwd(x2_ref[...], gpre_ref[...])
        dxn, dg = _rms_bwd(dh2, n2, r2, gpre_ref[...])
        dgpre_ref[...] += _fold8(dg)
        dx2 = dx3_ref[...] + dxn
        dx2_ref[...] = dx2
        _, ny, ry = _rms_fwd(y_ref[...], gpost_ref[...])
        dy, dg2 = _rms_bwd(dx2, ny, ry, gpost_ref[...])
        dy_ref[...] = dy.astype(BF16)
        dgpost_ref[...] += _fold8(dg2)

    blk4 = pl.BlockSpec((4, tm, FB), lambda i: (0, i, 0))
    return pl.pallas_call(
        body, name="ffn_bwd", grid=(s // tm,),
        in_specs=[_rows(tm, D), _resident((4, FB, D)), blk4, blk4, _resident((2, 4, FB, D)), _rows(tm, D), _full((1, D)),
                  _rows(tm, D), _rows(tm, D), _full((1, D))],
        out_specs=[pl.BlockSpec((2, 4, tm, FB), lambda i: (0, 0, i, 0)), _rows(tm, D), _rows(tm, D),
                   _full((SUBLANES, D)), _full((SUBLANES, D))],
        out_shape=[jax.ShapeDtypeStruct((2, 4, s, FB), BF16), jax.ShapeDtypeStruct((s, D), F32),
                   jax.ShapeDtypeStruct((s, D), BF16), jax.ShapeDtypeStruct((SUBLANES, D), F32),
                   jax.ShapeDtypeStruct((SUBLANES, D), F32)],
        compiler_params=_cparams(56, ("arbitrary",)),
    )(dff, wd, gate, up, wgu, x2, g_pre, dx3, y, g_post)


def _grad_matmul(a, b, *, ta, tb, ts, name):
    s, ka = a.shape
    nb = b.shape[1]
    ts = min(ts, s)
    nk = s // ts

    def body(a_ref, b_ref, o_ref, acc):
        k = pl.program_id(2)

        @pl.when(k == 0)
        def _():
            acc[...] = jnp.zeros_like(acc)

        acc[...] += lax.dot_general(a_ref[...], b_ref[...], TN, preferred_element_type=F32)

        @pl.when(k == nk - 1)
        def _():
            o_ref[...] = acc[...].astype(BF16)

    return pl.pallas_call(
        body, name=name, grid=(ka // ta, nb // tb, nk),
        in_specs=[pl.BlockSpec((ts, ta), lambda i, j, k: (k, i)), pl.BlockSpec((ts, tb), lambda i, j, k: (k, j))],
        out_specs=pl.BlockSpec((ta, tb), lambda i, j, k: (i, j)),
        out_shape=jax.ShapeDtypeStruct((ka, nb), BF16),
        scratch_shapes=[pltpu.VMEM((ta, tb), F32)],
        compiler_params=_cparams(48, ("arbitrary", "arbitrary", "arbitrary")),
    )(a, b)


GW_TILE = 256


def _grad_w_in(h1t, pieces):
    ka, s = h1t.shape
    widths = [p.shape[1] for p in pieces]
    assert all(w % GW_TILE == 0 for w in widths)
    first = [sum(widths[:i]) // GW_TILE for i in range(len(pieces))]
    count = [w // GW_TILE for w in widths]

    def body(a_ref, *refs):
        o_ref = refs[-1]
        j = pl.program_id(0)
        for ref, f0, n in zip(refs[:-1], first, count):
            @pl.when((j >= f0) & (j < f0 + n))
            def _(ref=ref):
                o_ref[...] = jnp.dot(a_ref[...], ref[...], preferred_element_type=F32).astype(BF16)

    def spec(f0, n):
        return pl.BlockSpec((s, GW_TILE), lambda j: (0, jnp.clip(j - f0, 0, n - 1)))

    return pl.pallas_call(
        body, name="grad_w_in", grid=(sum(count),),
        in_specs=[_resident((ka, s))] + [spec(f0, n) for f0, n in zip(first, count)],
        out_specs=pl.BlockSpec((ka, GW_TILE), lambda j: (0, j)),
        out_shape=jax.ShapeDtypeStruct((ka, sum(widths)), BF16),
        compiler_params=_cparams(56, ("arbitrary",)),
    )(h1t, *pieces)


def _grad_matmul_blocks(a, b, *, ts, name):
    nblk = a.shape[0] if a.ndim == 3 else b.shape[0]
    s = a.shape[-2]
    ka, nb = a.shape[-1], b.shape[-1]
    ts = min(ts, s)
    nk = s // ts

    def body(a_ref, b_ref, o_ref, acc):
        k = pl.program_id(1)

        @pl.when(k == 0)
        def _():
            acc[...] = jnp.zeros_like(acc)

        av = a_ref[0] if a.ndim == 3 else a_ref[...]
        bv = b_ref[0] if b.ndim == 3 else b_ref[...]
        acc[...] += lax.dot_general(av, bv, TN, preferred_element_type=F32)

        @pl.when(k == nk - 1)
        def _():
            o_ref[0] = acc[...].astype(BF16)

    def spec(arr, width):
        if arr.ndim == 3:
            return pl.BlockSpec((1, ts, width), lambda j, k: (j, k, 0))
        return pl.BlockSpec((ts, width), lambda j, k: (k, 0))

    return pl.pallas_call(
        body, name=name, grid=(nblk, nk),
        in_specs=[spec(a, ka), spec(b, nb)],
        out_specs=pl.BlockSpec((1, ka, nb), lambda j, k: (j, 0, 0)),
        out_shape=jax.ShapeDtypeStruct((nblk, ka, nb), BF16),
        scratch_shapes=[pltpu.VMEM((ka, nb), F32)],
        compiler_params=_cparams(48, ("arbitrary", "arbitrary")),
    )(a, b)


def _mix_bwd(dy, w_out, o, cv, bcu, ga, gc, gsum, *, tm):
    s = dy.shape[0]

    def group_norm_bwd(dn_out, v, g, gs):
        r = lax.rsqrt(_group_sum(v * v, gs) * (1.0 / DH) + EPS)
        n = v * r
        dn = dn_out * g
        return r * (dn - n * (_group_sum(dn * n, gs) * (1.0 / DH))), dn_out * n

    def body(dy_ref, w_ref, o_ref, cv_ref, bcu_ref, ga_ref, gc_ref, gs_ref,
             do_ref, dl_ref, dcv_ref, db_ref, dga_ref, dgc_ref):
        @pl.when(pl.program_id(0) == 0)
        def _():
            dga_ref[...] = jnp.zeros_like(dga_ref)
            dgc_ref[...] = jnp.zeros_like(dgc_ref)

        dm = lax.dot_general(dy_ref[...], w_ref[...], NT, preferred_element_type=F32)
        ov = o_ref[...]
        do, dga = group_norm_bwd(dm[:, 0:AW], ov, ga_ref[...], gs_ref[...])
        dob = do.astype(BF16)
        do_ref[...] = dob
        dl_ref[...] = _group_sum(dob.astype(F32) * ov, gs_ref[...])
        dga_ref[...] += _fold8(dga)
        gate_b = bcu_ref[:, 0:CW].astype(F32)
        cv = cv_ref[...]
        dconv, dgc = group_norm_bwd(dm[:, AW:D], gate_b * cv, gc_ref[...], gs_ref[...])
        dgc_ref[...] += _fold8(dgc)
        dcv_ref[...] = dconv * gate_b
        db_ref[...] = (dconv * cv).astype(BF16)

    return pl.pallas_call(
        body, name="mix_bwd", grid=(s // tm,),
        in_specs=[_rows(tm, D), _resident((D, D)), _rows(tm, AW), _rows(tm, CW), _rows(tm, 3 * CW),
                  _full((1, AW)), _full((1, CW)), _full((GS, GS))],
        out_specs=[_rows(tm, AW), _rows(tm, AW), _rows(tm, CW), _rows(tm, CW),
                   _full((SUBLANES, AW)), _full((SUBLANES, CW))],
        out_shape=[jax.ShapeDtypeStruct((s, AW), BF16), jax.ShapeDtypeStruct((s, AW), F32),
                   jax.ShapeDtypeStruct((s, CW), F32), jax.ShapeDtypeStruct((s, CW), BF16),
                   jax.ShapeDtypeStruct((SUBLANES, AW), F32), jax.ShapeDtypeStruct((SUBLANES, CW), F32)],
        compiler_params=_cparams(48, ("arbitrary",)),
    )(dy, w_out, o, cv, bcu, ga, gc, gsum)


def _conv_bwd(dcv, db, bcu, cw8, *, tm):
    s = dcv.shape[0]
    nt = s // tm

    def body(dcv_ref, nxt_ref, db_ref, bcu_ref, halo_ref, cw_ref, dbcu_ref, dw_ref):
        i = pl.program_id(0)

        @pl.when(i == 0)
        def _():
            dw_ref[...] = jnp.zeros_like(dw_ref)

        z, z1, z2 = _conv_taps(bcu_ref, halo_ref, i == 0, tm)
        d = dcv_ref[...]
        dw_ref[0] += _fold8(d * z2)
        dw_ref[1] += _fold8(d * z1)
        dw_ref[2] += _fold8(d * z)
        nx = jnp.where(i == nt - 1, 0.0, nxt_ref[...])
        row = lax.broadcasted_iota(jnp.int32, (tm, CW), 0)
        d1 = jnp.where(row == tm - 1, nx[0:1, :], pltpu.roll(d, tm - 1, axis=0))
        d2 = jnp.where(row == tm - 2, nx[0:1, :], jnp.where(row == tm - 1, nx[1:2, :], pltpu.roll(d, tm - 2, axis=0)))
        dz = cw_ref[2:3, :] * d + cw_ref[1:2, :] * d1 + cw_ref[0:1, :] * d2
        dbcu_ref[:, 0:CW] = db_ref[...]
        dbcu_ref[:, CW:2 * CW] = (dz * bcu_ref[:, 2 * CW:3 * CW].astype(F32)).astype(BF16)
        dbcu_ref[:, 2 * CW:3 * CW] = (dz * bcu_ref[:, CW:2 * CW].astype(F32)).astype(BF16)

    return pl.pallas_call(
        body, name="conv_bwd", grid=(nt,),
        in_specs=[_rows(tm, CW),
                  pl.BlockSpec((SUBLANES, CW), lambda i: (jnp.minimum((i + 1) * (tm // SUBLANES), s // SUBLANES - 1), 0)),
                  _rows(tm, CW), _rows(tm, 3 * CW), _halo_before(tm, 3 * CW), _full((SUBLANES, CW))],
        out_specs=[_rows(tm, 3 * CW), _full((3, SUBLANES, CW))],
        out_shape=[jax.ShapeDtypeStruct((s, 3 * CW), BF16), jax.ShapeDtypeStruct((3, SUBLANES, CW), F32)],
        compiler_params=_cparams(48, ("arbitrary",)),
    )(dcv, dcv, db, bcu, bcu, cw8)


def _attn_bwd(qp, kp, v, do, lse, dl, mk, *, t):
    s = qp.shape[0]
    nq = s // t

    def body(q_ref, k_ref, v_ref, do_ref, lse_ref, dl_ref, mk_ref, dq_ref, dk_ref, dv_ref, dkx_ref, dq_acc):
        pi = pl.program_id(1)

        @pl.when(pi == 0)
        def _():
            dq_acc[...] = jnp.zeros_like(dq_acc)

        row = lax.broadcasted_iota(jnp.int32, (t, t), 0)
        col = lax.broadcasted_iota(jnp.int32, (t, t), 1)
        lane = lax.broadcasted_iota(jnp.int32, (t, 128), 1)

        def head_step(hh, qi, carry, modes):
            off = pl.multiple_of(qi * t, t)
            rows = pl.ds(off, t)
            q = q_ref[rows, HP * hh:HP * (hh + 1)]
            qt = q.T
            lse_col = lse_ref[rows, DH * hh:DH * hh + 1]
            dl_col = dl_ref[rows, DH * hh:DH * hh + 1]
            do2 = do_ref[rows, :]
            dom = jnp.where(lane < DH, do2 if hh == 0 else pltpu.roll(do2, DH, axis=1), jnp.zeros((), BF16))
            new, dss = [], []
            for half, masked in enumerate(modes):
                if masked is None:
                    new.append(carry[half])
                    continue
                dk, dv, cs = carry[half]
                keys = slice(half * t, (half + 1) * t)
                m_col = mk_ref[half, rows, DH * hh:DH * hh + 1]
                scale = jnp.exp(m_col - lse_col)
                sc = lax.dot_general(q, k_ref[keys, HP * hh:HP * (hh + 1)], NT, preferred_element_type=F32) - m_col
                if masked:
                    sc = jnp.where(col <= row, sc, -1e30)
                pt = jnp.exp(sc).astype(BF16)
                dp = lax.dot_general(dom, v_ref[keys, HP * hh:HP * (hh + 1)], NT, preferred_element_type=F32)
                ds32 = (pt.astype(F32) * scale) * (dp - dl_col)
                ds = ds32.astype(BF16)
                cs = cs + _fold8(ds32)
                dv = dv + jnp.dot((dom.astype(F32) * scale).astype(BF16).T, pt, preferred_element_type=F32)
                dk = dk + jnp.dot(qt, ds, preferred_element_type=F32)
                new.append((dk, dv, cs))
                dss.append((half, ds))
            if len(dss) == 2:
                dq = jnp.dot(jnp.concatenate([dss[0][1], dss[1][1]], axis=1), k_ref[:, HP * hh:HP * (hh + 1)],
                             preferred_element_type=F32)
            else:
                half, ds = dss[0]
                dq = jnp.dot(ds, k_ref[half * t:(half + 1) * t, HP * hh:HP * (hh + 1)], preferred_element_type=F32)
            dq_acc[rows, HP * hh:HP * (hh + 1)] += dq
            return tuple(new)

        def step(qi, carry, modes):
            return tuple(head_step(hh, qi, carry[hh], modes) for hh in range(2))

        def two_heads(a0, a1):
            return jnp.where(lane < DH, a0, pltpu.roll(a1, DH, axis=1))

        def rows_to_lanes(a0, a1):
            return jnp.concatenate([a0, a1], axis=0).T

        zero = (jnp.zeros((HP, t), F32), jnp.zeros((128, t), F32), jnp.zeros((SUBLANES, t), F32))
        carry = step(2 * pi, ((zero, zero), (zero, zero)), (True, None))
        carry = step(2 * pi + 1, carry, (False, True))

        def pair(j, carry):
            qi = 2 * (pi + 1 + j)
            return step(qi + 1, step(qi, carry, (False, False)), (False, False))

        carry = lax.fori_loop(0, nq // 2 - 1 - pi, pair, carry)
        for half in range(2):
            keys = slice(half * t, (half + 1) * t)
            (dk0, dv0, cs0), (dk1, dv1, cs1) = carry[0][half], carry[1][half]
            dk_ref[keys, :] = rows_to_lanes(dk0[0:DH], dk1[0:DH]).astype(BF16)
            dv_ref[keys, :] = rows_to_lanes(dv0[0:DH], dv1[0:DH]).astype(BF16)
            total = lambda cs: jnp.broadcast_to(jnp.sum(cs, axis=0, keepdims=True), (DH, t))
            dkx_ref[keys, :] = rows_to_lanes(total(cs0), total(cs1))

        @pl.when(pi == nq // 2 - 1)
        def _():
            for c in range(s // t):
                rows = slice(c * t, (c + 1) * t)
                dq_ref[rows, :] = two_heads(dq_acc[rows, 0:HP], dq_acc[rows, HP:2 * HP]).astype(BF16)

    return pl.pallas_call(
        body, name="attn_bwd", grid=(H // 2, nq // 2),
        in_specs=[pl.BlockSpec((s, 2 * HP), lambda p, i: (0, p)),
                  pl.BlockSpec((2 * t, 2 * HP), lambda p, i: (i, p)),
                  pl.BlockSpec((2 * t, 2 * HP), lambda p, i: (i, p)),
                  pl.BlockSpec((s, 128), lambda p, i: (0, p)),
                  pl.BlockSpec((s, 128), lambda p, i: (0, p)),
                  pl.BlockSpec((s, 128), lambda p, i: (0, p)),
                  pl.BlockSpec((2, s, 128), lambda p, i: (i, 0, p))],
        out_specs=[pl.BlockSpec((s, 128), lambda p, i: (0, p)),
                   pl.BlockSpec((2 * t, 128), lambda p, i: (i, p)),
                   pl.BlockSpec((2 * t, 128), lambda p, i: (i, p)),
                   pl.BlockSpec((2 * t, 128), lambda p, i: (i, p))],
        out_shape=[jax.ShapeDtypeStruct((s, AW), BF16), jax.ShapeDtypeStruct((s, AW), BF16),
                   jax.ShapeDtypeStruct((s, AW), BF16), jax.ShapeDtypeStruct((s, AW), F32)],
        scratch_shapes=[pltpu.VMEM((s, 2 * HP), F32)],
        compiler_params=_cparams(56, ("arbitrary", "arbitrary")),
    )(qp, kp, v, do, lse, dl, mk)


def _forget_bwd(dkx, z, sel, *, tm):
    s = dkx.shape[0]
    nt = s // tm

    def body(dk_ref, z_ref, sel_ref, dfl_ref, dbf_ref, carry):
        @pl.when(pl.program_id(0) == 0)
        def _():
            carry[...] = jnp.zeros_like(carry)
            dbf_ref[...] = jnp.zeros_like(dbf_ref)

        dc = _split_dot(dk_ref[...], sel_ref[...])
        row = lax.broadcasted_iota(jnp.int32, (tm, tm), 0)
        col = lax.broadcasted_iota(jnp.int32, (tm, tm), 1)
        tri = (col >= row).astype(BF16)
        dlogf = _exact_dot01(tri, dc) + carry[0:1, :]
        carry[...] = jnp.broadcast_to(dlogf[0:1, :], carry.shape)
        dz = dlogf * (1.0 - jax.nn.sigmoid(z_ref[...]))
        dfl_ref[:, 0:128] = dz.astype(BF16)
        dfl_ref[:, 128:GW_TILE] = jnp.zeros((tm, GW_TILE - 128), BF16)
        dbf_ref[...] += _fold8(dz)

    rev = lambda i: (nt - 1 - i, 0)
    return pl.pallas_call(
        body, name="forget_bwd", grid=(nt,),
        in_specs=[pl.BlockSpec((tm, AW), rev), pl.BlockSpec((tm, 128), rev), _full((AW, 128))],
        out_specs=[pl.BlockSpec((tm, GW_TILE), rev), _full((SUBLANES, 128))],
        out_shape=[jax.ShapeDtypeStruct((s, GW_TILE), BF16), jax.ShapeDtypeStruct((SUBLANES, 128), F32)],
        scratch_shapes=[pltpu.VMEM((SUBLANES, 128), F32)],
        compiler_params=_cparams(48, ("arbitrary",)),
    )(dkx, z, sel)


def _in_proj_bwd(pieces, wp, x, g1, dx2, *, tm):
    s = x.shape[0]

    def body(q_ref, k_ref, v_ref, bcu_ref, f_ref, w_ref, x_ref, g_ref, dx2_ref, dx_ref, dg_ref):
        @pl.when(pl.program_id(0) == 0)
        def _():
            dg_ref[...] = jnp.zeros_like(dg_ref)

        dh = None
        for ref, (lo, hi) in zip((q_ref, k_ref, v_ref, bcu_ref, f_ref), PIECES):
            part = lax.dot_general(ref[...], w_ref[:, lo:hi], NT, preferred_element_type=F32)
            dh = part if dh is None else dh + part
        _, n, r = _rms_fwd(x_ref[...], g_ref[...])
        dxn, dg = _rms_bwd(dh, n, r, g_ref[...])
        dx_ref[...] = dx2_ref[...] + dxn
        dg_ref[...] += _fold8(dg)

    return pl.pallas_call(
        body, name="in_proj_bwd", grid=(s // tm,),
        in_specs=[_rows(tm, hi - lo) for lo, hi in PIECES] + [_resident((D, WP)), _rows(tm, D), _full((1, D)), _rows(tm, D)],
        out_specs=[_rows(tm, D), _full((SUBLANES, D))],
        out_shape=[jax.ShapeDtypeStruct((s, D), F32), jax.ShapeDtypeStruct((SUBLANES, D), F32)],
        compiler_params=_cparams(56, ("arbitrary",)),
    )(*pieces, wp, x, g1, dx2)


def _position():
    return lax.axis_index("x"), lax.axis_index("y"), lax.axis_index("c")


ANY = pl.BlockSpec(memory_space=pl.ANY)


def _all_gather(shards):
    n = len(shards)

    def body(*refs):
        x_refs, out_refs = refs[:n], refs[n:2 * n]
        send_sems, recv_sems, local_sems = refs[2 * n:]
        x, y, c = _position()
        me, sibling = (x, y, c), (x, y, 1 - c)
        chips = [(1 - x, y), (x, 1 - y), (1 - x, 1 - y)]

        def copy(a, k, block, to, own=False):
            slot = out_refs[a].at[4 * block[0] + 2 * block[1] + block[2]]
            return pltpu.make_async_remote_copy(
                src_ref=x_refs[a] if own else slot, dst_ref=slot,
                send_sem=send_sems.at[7 * a + k], recv_sem=recv_sems.at[7 * a + k], device_id=to, device_id_type=MESH_ID)

        mine = [pltpu.make_async_copy(x_refs[a], out_refs[a].at[4 * x + 2 * y + c], local_sems.at[a]) for a in range(n)]
        for cp in mine:
            cp.start()
        first = []
        for a in range(n):
            first.append(copy(a, 0, me, sibling, own=True))
            first += [copy(a, 1 + j, me, (*chip, c), own=True) for j, chip in enumerate(chips)]
        for cp in first:
            cp.start()
        passed = []
        for j, chip in enumerate(chips):
            for a in range(n):
                copy(a, 1 + j, (*chip, c), me).wait_recv()
                fwd = copy(a, 4 + j, (*chip, c), sibling)
                fwd.start()
                passed.append(fwd)
        for a in range(n):
            copy(a, 0, sibling, me).wait_recv()
            for j, chip in enumerate(chips):
                copy(a, 4 + j, (*chip, 1 - c), me).wait_recv()
        for cp in first + passed:
            cp.wait_send()
        for cp in mine:
            cp.wait()

    return pl.pallas_call(
        body, name="all_gather_weights",
        out_shape=[jax.ShapeDtypeStruct((NDEV,) + sh.shape, sh.dtype) for sh in shards],
        in_specs=[ANY] * n, out_specs=[ANY] * n,
        scratch_shapes=[pltpu.SemaphoreType.DMA((7 * n,)), pltpu.SemaphoreType.DMA((7 * n,)), pltpu.SemaphoreType.DMA((n,))],
    )(*shards)


def _pair_exchange(grads):
    n = len(grads)

    def body(*refs):
        g_refs, out_refs = refs[:n], refs[n:2 * n]
        send_sems, recv_sems = refs[2 * n:]
        x, y, c = _position()
        copies = [pltpu.make_async_remote_copy(
            src_ref=g_refs[a].at[:, pl.ds(1 - c, 1)], dst_ref=out_refs[a], send_sem=send_sems.at[a],
            recv_sem=recv_sems.at[a], device_id=(x, y, 1 - c), device_id_type=MESH_ID) for a in range(n)]
        for cp in copies:
            cp.start()
        for cp in copies:
            cp.wait()

    return pl.pallas_call(
        body, name="grad_pair_exchange",
        out_shape=[jax.ShapeDtypeStruct((4, 1) + g.shape[2:], g.dtype) for g in grads],
        in_specs=[ANY] * n, out_specs=[ANY] * n,
        scratch_shapes=[pltpu.SemaphoreType.DMA((n,)), pltpu.SemaphoreType.DMA((n,))],
    )(*grads)


def _pair_sum(g, got, idx, *, tr, name):
    r, c = g.shape[2:]

    def body(idx_ref, g_ref, got_ref, pb_ref, own_ref):
        p = g_ref[0, 0].astype(F32) + got_ref[0, 0].astype(F32)
        pb_ref[0] = p.astype(BF16)

        @pl.when(pl.program_id(1) == idx_ref[1])
        def _():
            own_ref[...] = p

    return pl.pallas_call(
        body, name=name,
        grid_spec=pltpu.PrefetchScalarGridSpec(
            num_scalar_prefetch=1, grid=(r // tr, 4),
            in_specs=[pl.BlockSpec((1, 1, tr, c), lambda i, j, idx: (j, idx[0], i, 0)),
                      pl.BlockSpec((1, 1, tr, c), lambda i, j, idx: (j, 0, i, 0))],
            out_specs=[pl.BlockSpec((1, tr, c), lambda i, j, idx: (j, i, 0)),
                       pl.BlockSpec((tr, c), lambda i, j, idx: (i, 0))]),
        out_shape=[jax.ShapeDtypeStruct((4, r, c), BF16), jax.ShapeDtypeStruct((r, c), F32)],
        compiler_params=_cparams(32, ("arbitrary", "arbitrary")),
    )(idx, g, got)


HBM = pl.BlockSpec(memory_space=pltpu.HBM)
SEM = pl.BlockSpec(memory_space=pltpu.SEMAPHORE)
DATAFLOW = pltpu.SideEffectType.DATAFLOW_SIDE_EFFECTING


PEERS = {"gather": NDEV - 1, "scatter": NDEV - 1, "chips": 3}


def _exchange_copies(src_refs, land_refs, send_sems, recv_sems, mode):
    x, y, c = _position()
    me, my_chip = 4 * x + 2 * y + c, 2 * x + y
    npeers = PEERS[mode]
    copies, own = [], []
    for a, (s_ref, l_ref) in enumerate(zip(src_refs, land_refs)):
        for k in range(npeers):
            if mode == "chips":
                px, py, pc = x ^ ((k + 1) >> 1), y ^ ((k + 1) & 1), c
                src, dst = s_ref.at[2 * px + py], l_ref.at[my_chip]
            else:
                px, py, pc = x ^ ((k + 1) >> 2), y ^ (((k + 1) >> 1) & 1), c ^ ((k + 1) & 1)
                src, dst = (s_ref.at[4 * px + 2 * py + pc] if mode == "scatter" else s_ref), l_ref.at[me]
            copies.append(pltpu.make_async_remote_copy(
                src_ref=src, dst_ref=dst, send_sem=send_sems.at[npeers * a + k], recv_sem=recv_sems.at[npeers * a + k],
                device_id=(px, py, pc), device_id_type=MESH_ID))
        slot = my_chip if mode == "chips" else me
        own.append(pltpu.make_async_copy(s_ref if mode == "gather" else s_ref.at[slot], l_ref.at[slot],
                                         send_sems.at[npeers * len(src_refs) + a]))
    return copies, own


def _exchange_start(srcs, lands, after, *, mode, name):
    n = len(srcs)
    nsem = PEERS[mode] * n

    def body(*refs):
        token = refs[-1]
        copies, own = _exchange_copies(refs[:n], refs[n:2 * n], refs[2 * n + 1], refs[2 * n + 2], mode)
        for cp in copies + own:
            cp.start()
        token[...] = jnp.zeros_like(token)

    arrays = list(srcs) + list(lands)
    outs = pl.pallas_call(
        body, name=name,
        out_shape=(pltpu.SemaphoreType.DMA((nsem + n,)), pltpu.SemaphoreType.DMA((nsem,)),
                   *[pltpu.HBM(a.shape, a.dtype) for a in arrays], jax.ShapeDtypeStruct((SUBLANES, LANES), F32)),
        in_specs=[HBM] * (2 * n) + [ANY],
        out_specs=(SEM, SEM, *[HBM] * (2 * n), pl.BlockSpec(memory_space=pltpu.VMEM)),
        input_output_aliases={i: 2 + i for i in range(2 * n)},
        compiler_params=pltpu.CompilerParams(has_side_effects=DATAFLOW),
    )(*[pltpu.with_memory_space_constraint(a, pltpu.HBM) for a in arrays], after)
    return outs[0], outs[1], outs[2:2 + n], outs[2 + n:2 + 2 * n], outs[-1]


def _exchange_wait(send_sems, recv_sems, srcs, lands, after, *, mode, name):
    n = len(srcs)

    def body(*refs):
        copies, own = _exchange_copies(refs[:n], refs[n:2 * n], refs[2 * n], refs[2 * n + 1], mode)
        for cp in copies:
            cp.wait_send()
            cp.wait_recv()
        for cp in own:
            cp.wait()

    arrays = list(srcs) + list(lands)
    outs = pl.pallas_call(
        body, name=name,
        out_shape=tuple(pltpu.HBM(a.shape, a.dtype) for a in arrays),
        in_specs=[HBM] * (2 * n) + [SEM, SEM, ANY],
        out_specs=tuple([HBM] * (2 * n)),
        input_output_aliases={i: i for i in range(2 * n)},
        compiler_params=pltpu.CompilerParams(has_side_effects=DATAFLOW),
    )(*arrays, send_sems, recv_sems, after)
    return outs[n:]


def _small_all_reduce(parts):
    def body(gmp_ref, gmo_ref, gfp_ref, gfo_ref, ga_ref, gc_ref, dw_ref, bf_ref, loss_ref,
             out_ref, buf, send_sems, recv_sems):
        x, y, c = _position()
        me = 4 * x + 2 * y + c

        def colsum(v):
            return jnp.sum(v, axis=0, keepdims=True)

        loss = jnp.sum(colsum(loss_ref[...]), axis=1, keepdims=True) * (0.5 / D)
        rows = [colsum(gmp_ref[...]), colsum(gmo_ref[...]), colsum(gfp_ref[...]), colsum(gfo_ref[...]),
                jnp.concatenate([colsum(ga_ref[...]), colsum(gc_ref[...])], axis=1),
                jnp.concatenate([colsum(dw_ref[0]), colsum(dw_ref[1])], axis=1),
                jnp.concatenate([colsum(dw_ref[2]), colsum(bf_ref[...]), jnp.broadcast_to(loss, (1, 128)),
                                 jnp.zeros((1, 256), F32)], axis=1),
                jnp.zeros((1, D), F32)]
        buf[me] = jnp.concatenate(rows, axis=0)
        copies = []
        for mm in range(1, NDEV):
            peer = (x ^ (mm >> 2), y ^ ((mm >> 1) & 1), c ^ (mm & 1))
            copies.append(pltpu.make_async_remote_copy(
                src_ref=buf.at[me], dst_ref=buf.at[me], send_sem=send_sems.at[mm - 1], recv_sem=recv_sems.at[mm - 1],
                device_id=peer, device_id_type=MESH_ID))
        for cp in copies:
            cp.start()
        for cp in copies:
            cp.wait_recv()
        for cp in copies:
            cp.wait_send()
        acc = buf[0]
        for d in range(1, NDEV):
            acc = acc + buf[d]
        out_ref[...] = acc

    vm = pl.BlockSpec(memory_space=pltpu.VMEM)
    return pl.pallas_call(
        body, name="small_all_reduce",
        out_shape=jax.ShapeDtypeStruct((SUBLANES, D), F32),
        in_specs=[vm] * len(parts), out_specs=vm,
        scratch_shapes=[pltpu.VMEM((NDEV, SUBLANES, D), F32), pltpu.SemaphoreType.DMA((7,)), pltpu.SemaphoreType.DMA((7,))],
    )(*parts)


def _adam_update(w, g, m, v):
    nm = ADAM_B1 * m + (1.0 - ADAM_B1) * g
    nv = ADAM_B2 * v + (1.0 - ADAM_B2) * (g * g)
    m_hat = nm / (1.0 - ADAM_B1 ** ADAM_STEP)
    v_hat = nv / (1.0 - ADAM_B2 ** ADAM_STEP)
    return -ADAM_LR * (m_hat / (jnp.sqrt(v_hat) + ADAM_EPS) + ADAM_WD * w), nm, nv


SMALL_SLOTS = {"g_mix_pre": (0, 0, D), "g_mix_post": (1, 0, D), "g_ffn_pre": (2, 0, D), "g_ffn_post": (3, 0, D),
               "g_attn_out": (4, 0, AW), "g_conv_out": (4, AW, CW), "b_forget": (6, CW, H)}


def _small_adamw(small, conv_grad, params):
    names = list(params)
    n = len(names)

    def body(*refs):
        small_ref, cg_ref = refs[0], refs[1]
        ins, outs = refs[2:2 + 3 * n], refs[2 + 3 * n:]
        for i, name in enumerate(names):
            w_ref, m_ref, v_ref = ins[3 * i:3 * i + 3]
            g_ref, d_ref, nm_ref, nv_ref = outs[4 * i:4 * i + 4]
            if name == "conv_w":
                g = cg_ref[...]
            else:
                r, c0, width = SMALL_SLOTS[name]
                g = small_ref[r:r + 1, c0:c0 + width]
            g_ref[...] = g
            d_ref[...], nm_ref[...], nv_ref[...] = _adam_update(w_ref[...], g, m_ref[...], v_ref[...])

    vm = pl.BlockSpec(memory_space=pltpu.VMEM)
    flat = [a for name in names for a in params[name]]
    outs = pl.pallas_call(
        body, name="adamw_small",
        in_specs=[vm] * (2 + 3 * n), out_specs=[vm] * (4 * n),
        out_shape=[jax.ShapeDtypeStruct(params[name][0].shape, F32) for name in names for _ in range(4)],
    )(small, conv_grad, *flat)
    return {name: outs[4 * i:4 * i + 4] for i, name in enumerate(names)}


def _chip_sum_adamw(got, own, idx, wt, mt, vt, *, tr, name):
    cols, rows = wt.shape
    gcols = own.shape[1]

    def body(idx_ref, got_ref, own_ref, w_ref, m_ref, v_ref, g_ref, d_ref, nm_ref, nv_ref):
        g = jnp.zeros((tr, gcols), F32)
        for j in range(4):
            g = g + jnp.where(idx_ref[1] == j, own_ref[...], got_ref[j].astype(F32))
        g = g.T[:cols]
        g_ref[...] = g
        d_ref[...], nm_ref[...], nv_ref[...] = _adam_update(w_ref[...], g, m_ref[...], v_ref[...])

    spec = pl.BlockSpec((cols, tr), lambda i, idx: (0, i))
    gspec = pl.BlockSpec((tr, gcols), lambda i, idx: (i, 0))
    return pl.pallas_call(
        body, name=name,
        grid_spec=pltpu.PrefetchScalarGridSpec(
            num_scalar_prefetch=1, grid=(rows // tr,),
            in_specs=[pl.BlockSpec((4, tr, gcols), lambda i, idx: (0, i, 0)), gspec, spec, spec, spec],
            out_specs=[spec] * 4),
        out_shape=[jax.ShapeDtypeStruct((cols, rows), F32)] * 4,
        compiler_params=_cparams(32, ("arbitrary",)),
    )(idx, got, own, wt, mt, vt)


def _device_sum_adamw(land, w, m, v, *, tr, name):
    rows, cols = w.shape

    def body(land_ref, w_ref, m_ref, v_ref, g_ref, d_ref, nm_ref, nv_ref):
        g = land_ref[0].astype(F32)
        for dev in range(1, NDEV):
            g = g + land_ref[dev].astype(F32)
        g_ref[...] = g
        d_ref[...], nm_ref[...], nv_ref[...] = _adam_update(w_ref[...], g, m_ref[...], v_ref[...])

    spec = pl.BlockSpec((tr, cols), lambda i: (i, 0))
    return pl.pallas_call(
        body, name=name, grid=(rows // tr,),
        in_specs=[pl.BlockSpec((NDEV, tr, cols), lambda i: (0, i, 0)), spec, spec, spec],
        out_specs=[spec] * 4,
        out_shape=[jax.ShapeDtypeStruct((rows, cols), F32)] * 4,
        compiler_params=_cparams(32, ("arbitrary",)),
    )(land, w, m, v)


def _placement_constants():
    j = jnp.arange(128)[:, None]
    lane = jnp.arange(1024)[None, :]
    head, sub = lane // HP, lane % HP
    piece, jh = j // H, j % H
    valid = (j < 3 * H) & (jh == head)
    pq = jnp.where(valid & (sub == DH + piece), 1.0, 0.0).astype(BF16)
    pk = jnp.where(valid & (sub == DH + 3 + piece), -1.0, 0.0).astype(BF16)
    oq = jnp.where((sub >= DH + 3) & (sub < DH + 6), 1.0, 0.0).astype(F32)
    ok = jnp.where((sub >= DH) & (sub < DH + 3), 1.0, 0.0).astype(F32)
    r = jnp.arange(AW)[:, None]
    cc = jnp.arange(128)[None, :]
    sel = jnp.where((r % DH == 3) & (r // DH == cc), -1.0, 0.0).astype(BF16)
    gi = jnp.arange(GS)
    gsum = (gi[:, None] // DH == gi[None, :] // DH).astype(BF16)
    return pq, pk, oq, ok, sel, gsum


def _local_step(xs, tgt, wp, late_weights, cw8, bfp, g_attn_out, g_conv_out,
                g_mix_pre, g_mix_post, g_ffn_pre, g_ffn_post, early_grads=None, last_grad=None):
    pq, pk, oq, ok, sel, gsum = _placement_constants()
    h1t, qp, kp, vv, bcu, zf = _in_proj(xs, g_mix_pre, wp, bfp, pq, pk, oq, ok, tm=512)
    o, lse, mk = _attn_fwd(qp, kp, vv, t=512)
    w_out_f, wgu, wd = late_weights(lse)
    merged, y, x2, cv, h2 = _mix_out(o, bcu, cw8, g_attn_out, g_conv_out, gsum, w_out_f, xs, g_mix_post, g_ffn_pre, tm=512)
    gate, up, act, dx3, dff, loss_p, dg_ffn_post = _ffn_fwd_loss(h2, wgu, wd, x2, tgt, g_ffn_post, tm=512)

    dgu, dx2, dy, dg_ffn_pre, dg_mix_post = _ffn_bwd(dff, wd, gate, up, wgu, x2, g_ffn_pre, dx3, y, g_mix_post, tm=256)
    dw_down = _grad_matmul_blocks(act, dff, ts=4096, name="grad_w_down")
    dw_gu = _grad_matmul_blocks(dgu.reshape(NDEV, -1, FB), h2, ts=4096, name="grad_w_gate_up")
    dw_out = _grad_matmul(merged, dy, ta=1024, tb=1024, ts=2048, name="grad_w_out")
    token = early_grads(dw_out, dw_gu, dw_down) if early_grads is not None else None
    ga = g_attn_out if token is None else g_attn_out + token[0:1, 0:1]
    do, dl, dcv, db, dg_attn, dg_conv = _mix_bwd(dy, w_out_f, o, cv, bcu, ga, g_conv_out, gsum, tm=512)
    dbcu, dtaps = _conv_bwd(dcv, db, bcu, cw8, tm=512)
    dqp, dkp, dv, dkx = _attn_bwd(qp, kp, vv, do, lse, dl, mk, t=512)
    dfl, dbf = _forget_bwd(dkx, zf, sel, tm=512)
    pieces = (dqp, dkp, dv, dbcu, dfl)
    dwp = _grad_w_in(h1t, pieces)
    token = last_grad(dwp) if last_grad is not None else None
    g1 = g_mix_pre if token is None else g_mix_pre + token[0:1, 0:1]
    grad_x, dg_mix_pre = _in_proj_bwd(pieces, wp, xs, g1, dx2, tm=512)
    return (grad_x, dwp, dw_out, dw_gu, dw_down, dg_mix_pre, dg_mix_post, dg_ffn_pre, dg_ffn_post, dg_attn, dg_conv,
            dtaps, dbf, loss_p)


BIG_TILES = {"w_in": 256, "w_out": 128, "w_gate_up": 176, "w_down": 176}


def kernel(x, w_in, b_forget, conv_w, g_attn_out, g_conv_out, w_out, g_mix_pre, g_mix_post, w_gate_up, w_down, g_ffn_pre, g_ffn_post, loss_target, m_w_in, m_b_forget, m_conv_w, m_g_attn_out, m_g_conv_out, m_w_out, m_g_mix_pre, m_g_mix_post, m_w_gate_up, m_w_down, m_g_ffn_pre, m_g_ffn_post, v_w_in, v_b_forget, v_conv_w, v_g_attn_out, v_g_conv_out, v_w_out, v_g_mix_pre, v_g_mix_post, v_w_gate_up, v_w_down, v_g_ffn_pre, v_g_ffn_post):
    xc, yc, cc = _position()
    my_chip = 2 * xc + yc
    me = 2 * my_chip + cc
    idx = jnp.stack([cc, my_chip]).astype(jnp.int32)
    tables = _in_layout_tables()

    w_in_b = w_in[0].astype(BF16)
    g_in, g_last, g_taps = _all_gather([w_in_b[:, :IN_MAIN], w_in_b[:, IN_MAIN].reshape(SUBLANES, LANES), conv_w[0]])
    last_cols = jnp.pad(g_last.reshape(NDEV, D).T.astype(F32), ((0, 0), (0, LANES - NDEV)))
    wp = _assemble_w_in(g_in, last_cols, tables, tr=256)
    cw8 = jnp.pad(g_taps.transpose(1, 0, 2).reshape(3, CW), ((0, SUBLANES - 3), (0, 0)))

    late = [w_out[0].astype(BF16), w_gate_up[0].T.astype(BF16), w_down[0].astype(BF16)]
    ssem, rsem, late_thru, land_thru, token = _exchange_start(
        late, [lax.empty((NDEV,) + s.shape, s.dtype) for s in late], g_in, mode="gather",
        name="gather_late_start")
    bfp = jnp.pad(b_forget, ((0, 0), (0, 128 - H))) + token[0:1, :]

    def late_weights(after):
        l_out, l_gu, l_down = _exchange_wait(ssem, rsem, late_thru, land_thru, after, mode="gather", name="gather_late_wait")
        return l_out.reshape(D, D), l_gu.reshape(2, 4, FB, D), l_down.reshape(4, FB, D)

    early = {}

    def early_grads(dw_out, dw_gu, dw_down):
        srcs = [dw_out.reshape(NDEV, D // NDEV, D), dw_gu, dw_down.reshape(NDEV, DFF // NDEV, D)]
        lands = [lax.empty(s.shape, s.dtype) for s in srcs]
        early["handles"] = _exchange_start(srcs, lands, dw_out, mode="scatter", name="scatter_early_start")
        return early["handles"][4]

    last = {}

    def last_grad(dwp):
        g_w_in = _disassemble_w_in(dwp, tables, tr=256).reshape(4, 2, D, IN_PAD)
        (from_sibling,) = _pair_exchange([g_w_in])
        pair_b, last["own"] = _pair_sum(g_w_in, from_sibling, idx, tr=BIG_TILES["w_in"], name="grad_pair_sum_w_in")
        last["handles"] = _exchange_start([pair_b], [lax.empty(pair_b.shape, pair_b.dtype)], last["own"], mode="chips",
                                          name="chips_w_in_start")
        return last["handles"][4]

    (grad_x, dwp, dw_out, dw_gu, dw_down, dg_mix_pre, dg_mix_post, dg_ffn_pre, dg_ffn_post, dg_attn, dg_conv,
     dtaps, dbf, loss_p) = _local_step(x[0], loss_target[0], wp, late_weights, cw8, bfp, g_attn_out, g_conv_out,
                                        g_mix_pre, g_mix_post, g_ffn_pre, g_ffn_post, early_grads, last_grad)

    e_ssem, e_rsem, e_srcs, e_lands, _ = early["handles"]
    land_out, land_gu, land_down = _exchange_wait(e_ssem, e_rsem, e_srcs, e_lands, dg_mix_pre, mode="scatter",
                                                  name="scatter_early_wait")
    res = {}
    big = {"w_out": (land_out, w_out[0], m_w_out[0], v_w_out[0]),
           "w_gate_up": (land_gu, w_gate_up[0].T, m_w_gate_up[0].T, v_w_gate_up[0].T),
           "w_down": (land_down, w_down[0], m_w_down[0], v_w_down[0])}
    for name, (land, w, m, v) in big.items():
        outs = _device_sum_adamw(land, w, m, v, tr=BIG_TILES[name], name="adamw_" + name)
        res[name] = [(o.T if name == "w_gate_up" else o)[None] for o in outs]
    c_ssem, c_rsem, c_srcs, c_lands, _ = last["handles"]
    after = sum(res[n][1][0, :SUBLANES, :LANES] for n in big)
    (from_chips,) = _exchange_wait(c_ssem, c_rsem, c_srcs, c_lands, after, mode="chips", name="chips_w_in_wait")
    outs = _chip_sum_adamw(from_chips, last["own"], idx, w_in[0].T, m_w_in[0].T, v_w_in[0].T,
                           tr=BIG_TILES["w_in"], name="adamw_w_in")
    res["w_in"] = [o.T[None] for o in outs]

    small = _small_all_reduce([dg_mix_pre, dg_mix_post, dg_ffn_pre, dg_ffn_post, dg_attn, dg_conv, dtaps, dbf, loss_p])
    taps_full = jnp.concatenate([small[5:6, :CW], small[5:6, CW:], small[6:7, :CW]], axis=0)
    loss = small[6, CW + 128]
    taps_first = lambda a: a.transpose(1, 0, 2)
    smalls = {"b_forget": (b_forget, m_b_forget, v_b_forget),
              "conv_w": (taps_first(conv_w), taps_first(m_conv_w), taps_first(v_conv_w)),
              "g_attn_out": (g_attn_out, m_g_attn_out, v_g_attn_out), "g_conv_out": (g_conv_out, m_g_conv_out, v_g_conv_out),
              "g_mix_pre": (g_mix_pre, m_g_mix_pre, v_g_mix_pre), "g_mix_post": (g_mix_post, m_g_mix_post, v_g_mix_post),
              "g_ffn_pre": (g_ffn_pre, m_g_ffn_pre, v_g_ffn_pre), "g_ffn_post": (g_ffn_post, m_g_ffn_post, v_g_ffn_post)}
    own_taps = lax.dynamic_slice(taps_full, (0, me * 64), (3, 64))[:, None, :]
    for name, outs in _small_adamw(small, own_taps, smalls).items():
        res[name] = [taps_first(o) for o in outs] if name == "conv_w" else list(outs)

    order = ["w_in", "b_forget", "conv_w", "g_attn_out", "g_conv_out", "w_out", "g_mix_pre", "g_mix_post",
             "w_gate_up", "w_down", "g_ffn_pre", "g_ffn_post"]
    outs = [loss, grad_x[None]]
    for k in range(4):
        outs += [res[n][k] for n in order]
    return tuple(outs)
```

```python
import functools

import numpy as np

import jax
import jax.numpy as jnp
from jax import lax
from jax.experimental import pallas as pl
from jax.experimental.pallas import tpu as pltpu

F32 = jnp.float32
BF16 = jnp.bfloat16
MESH_ID = pl.DeviceIdType.MESH

D = 1024
H = 8
DH = 64
AW = 512
CW = 512
DFF = 2816
FB = DFF // 4
FF_CHUNKS = ((0, 768), (768, 768), (1536, 768), (2304, 512))
HP = 128
OFF_Q, OFF_K, OFF_V, OFF_BCU, OFF_F = 0, 512, 1024, 1536, 3072
WP = OFF_F + 128
PIECES = ((OFF_Q, OFF_K), (OFF_K, OFF_V), (OFF_V, OFF_BCU), (OFF_BCU, OFF_F), (OFF_F, WP))
EPS = 1e-6
NDEV = 8
LANES = 128
SUBLANES = 8
IN_COLS = 385
IN_PAD = 512
IN_MAIN = 384
WIN = 640
ADAM_LR, ADAM_B1, ADAM_B2, ADAM_EPS, ADAM_WD, ADAM_STEP = 0.001, 0.9, 0.999, 1e-08, 0.01, 10

NT = (((1,), (1,)), ((), ()))
TN = (((0,), (0,)), ((), ()))


def _cparams(vmem_mb=None, sem=None):
    kw = {}
    if vmem_mb is not None:
        kw["vmem_limit_bytes"] = vmem_mb << 20
    if sem is not None:
        kw["dimension_semantics"] = sem
    return pltpu.CompilerParams(**kw)


def _full(shape):
    return pl.BlockSpec(shape, lambda *_: (0,) * len(shape))


def _resident(shape):
    return pl.BlockSpec(shape, lambda *_: (0,) * len(shape), pipeline_mode=pl.Buffered(1))


def _rows(tm, width):
    return pl.BlockSpec((tm, width), lambda i: (i, 0))


def _fold8(v):
    r, w = v.shape
    return jnp.sum(v.reshape(r // SUBLANES, SUBLANES, w), axis=0)


def _split_dot(v, m01):
    hi = v.astype(BF16)
    lo = (v - hi.astype(F32)).astype(BF16)
    return (jnp.dot(hi, m01, preferred_element_type=F32)
            + jnp.dot(lo, m01, preferred_element_type=F32))


GS = 256


def _group_sum(v, g01):
    parts = [_split_dot(v[:, c:c + GS], g01) for c in range(0, v.shape[1], GS)]
    return parts[0] if len(parts) == 1 else jnp.concatenate(parts, axis=1)


def _exact_dot01(m01, v):
    p1 = v.astype(BF16)
    r1 = v - p1.astype(F32)
    p2 = r1.astype(BF16)
    p3 = (r1 - p2.astype(F32)).astype(BF16)
    return (jnp.dot(m01, p1, preferred_element_type=F32) + jnp.dot(m01, p2, preferred_element_type=F32)
            + jnp.dot(m01, p3, preferred_element_type=F32))


def _rms_fwd(v, g):
    r = lax.rsqrt(jnp.mean(v * v, axis=-1, keepdims=True) + EPS)
    n = v * r
    return n * g, n, r


def _rms_bwd(do, n, r, g):
    dn = do * g
    return r * (dn - n * jnp.mean(dn * n, axis=-1, keepdims=True)), do * n


def _padded_column(n):
    if n < AW:
        return OFF_Q + n, 0.125
    if n < 3 * AW:
        return n, 1.0
    if n < 3 * AW + H:
        return OFF_F + n - 3 * AW, 1.0
    return OFF_BCU + n - 3 * AW - H, 1.0


def _in_layout_tables():
    dest = -np.ones((IN_PAD, LANES), np.int32)
    dest_f = -np.ones((IN_PAD, LANES), np.int32)
    scale = np.zeros((IN_PAD, LANES), np.float32)
    starts = []
    for k in range(NDEV):
        cols = [_padded_column(IN_COLS * k + j) for j in range(IN_COLS)]
        main = [c for c, _ in cols if c < OFF_F]
        ws = min((min(main) // LANES) * LANES, OFF_F - WIN)
        assert ws <= min(main) and max(main) < ws + WIN
        starts.append(ws)
        for j, (c, sc) in enumerate(cols):
            scale[j, k] = sc
            if c < OFF_F:
                dest[j, k] = c - ws
            else:
                dest_f[j, k] = c - OFF_F
    f_shards = tuple(k for k in range(NDEV) if (dest_f[:, k] >= 0).any())
    return tuple(starts), f_shards, jnp.asarray(dest), jnp.asarray(dest_f), jnp.asarray(scale)


def _perm(dest_ref, scale_ref, k, width, rows=IN_PAD):
    lane = lax.broadcasted_iota(jnp.int32, (rows, width), 1)
    return jnp.where(dest_ref[0:rows, k:k + 1] == lane, scale_ref[0:rows, k:k + 1], 0.0).astype(BF16)


def _assemble_w_in(blocks, last_cols, tables, *, tr):
    starts, f_shards, dest, dest_f, scale = tables
    last = [_padded_column(IN_COLS * k + IN_MAIN) for k in range(NDEV)]
    f_main = [any(_padded_column(IN_COLS * k + j)[0] >= OFF_F for j in range(IN_MAIN)) for k in range(NDEV)]
    assert IN_COLS == IN_MAIN + 1

    def body(b_ref, c_ref, dest_ref, destf_ref, scale_ref, o_ref):
        o_ref[...] = jnp.zeros_like(o_ref)
        lane = lax.broadcasted_iota(jnp.int32, (tr, LANES), 1)
        for k in range(NDEV):
            b = b_ref[k]
            ws = starts[k]
            part = jnp.dot(b, _perm(dest_ref, scale_ref, k, WIN, IN_MAIN), preferred_element_type=F32)
            o_ref[:, ws:ws + WIN] += part.astype(BF16)
            if f_main[k]:
                part = jnp.dot(b, _perm(destf_ref, scale_ref, k, 128, IN_MAIN), preferred_element_type=F32)
                o_ref[:, OFF_F:WP] += part.astype(BF16)
            col, sc = last[k]
            tile = (col // LANES) * LANES
            o_ref[:, tile:tile + LANES] += jnp.where(lane == col - tile, c_ref[:, k:k + 1] * sc, 0.0).astype(BF16)

    tab = _full((IN_PAD, LANES))
    return pl.pallas_call(
        body, name="assemble_w_in", grid=(D // tr,),
        in_specs=[pl.BlockSpec((NDEV, tr, IN_MAIN), lambda i: (0, i, 0)), _rows(tr, LANES), tab, tab, tab],
        out_specs=_rows(tr, WP),
        out_shape=jax.ShapeDtypeStruct((D, WP), BF16),
        compiler_params=_cparams(48, ("arbitrary",)),
    )(blocks, last_cols, dest, dest_f, scale)


def _disassemble_w_in(dwp, tables, *, tr):
    starts, f_shards, dest, dest_f, scale = tables
    width = dwp.shape[1]

    def body(g_ref, dest_ref, destf_ref, scale_ref, o_ref):
        for k in range(NDEV):
            ws = starts[k]
            acc = lax.dot_general(g_ref[:, ws:ws + WIN], _perm(dest_ref, scale_ref, k, WIN), NT, preferred_element_type=F32)
            if k in f_shards:
                acc = acc + lax.dot_general(g_ref[:, OFF_F:WP], _perm(destf_ref, scale_ref, k, 128), NT,
                                            preferred_element_type=F32)
            o_ref[k] = acc.astype(BF16)

    tab = _full((IN_PAD, LANES))
    return pl.pallas_call(
        body, name="disassemble_w_in", grid=(D // tr,),
        in_specs=[_rows(tr, width), tab, tab, tab],
        out_specs=pl.BlockSpec((NDEV, tr, IN_PAD), lambda i: (0, i, 0)),
        out_shape=jax.ShapeDtypeStruct((NDEV, D, IN_PAD), BF16),
        compiler_params=_cparams(48, ("arbitrary",)),
    )(dwp, dest, dest_f, scale)


def _in_proj(x, g1, wp, bfp, pq, pk, oq, ok, *, tm):
    s = x.shape[0]

    def body(x_ref, g_ref, w_ref, bf_ref, pq_ref, pk_ref, oq_ref, ok_ref,
             ht_ref, qp_ref, kp_ref, v_ref, bcu_ref, z_ref, carry):
        @pl.when(pl.program_id(0) == 0)
        def _():
            carry[...] = jnp.zeros_like(carry)

        h = _rms_fwd(x_ref[...], g_ref[...])[0].astype(BF16)
        ht_ref[...] = h.T
        z = jnp.dot(h, w_ref[:, OFF_F:WP], preferred_element_type=F32) + bf_ref[...]
        z_ref[...] = z
        lane = lax.broadcasted_iota(jnp.int32, (tm, 128), 1)
        logf = jnp.where(lane < H, jnp.minimum(z, 0.0) - jnp.log(1.0 + jnp.exp(-jnp.abs(z))), 0.0)
        row = lax.broadcasted_iota(jnp.int32, (tm, tm), 0)
        col = lax.broadcasted_iota(jnp.int32, (tm, tm), 1)
        tri = (col <= row).astype(BF16)
        c = _exact_dot01(tri, logf) + carry[0:1, :]
        carry[...] = jnp.broadcast_to(c[tm - 1:tm, :], carry.shape)
        c1 = c.astype(BF16).astype(F32)
        r1 = c - c1
        c2 = r1.astype(BF16).astype(F32)
        c3 = (r1 - c2).astype(BF16).astype(F32)
        zc = (c1 + pltpu.roll(c2, 8, axis=1) + pltpu.roll(c3, 16, axis=1)).astype(BF16)

        def pad_heads(v):
            blocks = []
            for pair in range(H // 2):
                two = v[:, 128 * pair:128 * (pair + 1)]
                blocks.append(jnp.where(lane < DH, two, 0.0))
                blocks.append(jnp.where(lane < DH, pltpu.roll(two, DH, axis=1), 0.0))
            return jnp.concatenate(blocks, axis=1)

        q = jnp.dot(h, w_ref[:, OFF_Q:OFF_K], preferred_element_type=F32)
        qp_ref[...] = (pad_heads(q) + jnp.dot(zc, pq_ref[...], preferred_element_type=F32) + oq_ref[...]).astype(BF16)
        k = jnp.dot(h, w_ref[:, OFF_K:OFF_V], preferred_element_type=F32)
        kp_ref[...] = (pad_heads(k) + jnp.dot(zc, pk_ref[...], preferred_element_type=F32) + ok_ref[...]).astype(BF16)
        v = pad_heads(jnp.dot(h, w_ref[:, OFF_V:OFF_BCU], preferred_element_type=F32))
        ones_lane = lax.broadcasted_iota(jnp.int32, (tm, H * HP), 1) % HP == DH
        v_ref[...] = jnp.where(ones_lane, 1.0, v).astype(BF16)
        bcu_ref[...] = jnp.dot(h, w_ref[:, OFF_BCU:OFF_F], preferred_element_type=F32).astype(BF16)

    return pl.pallas_call(
        body, name="in_proj", grid=(s // tm,),
        in_specs=[_rows(tm, D), _full((1, D)), _resident((D, WP)), _full((1, 128)),
                  _full((128, 1024)), _full((128, 1024)), _full((1, 1024)), _full((1, 1024))],
        out_specs=[pl.BlockSpec((D, tm), lambda i: (0, i)), _rows(tm, 1024), _rows(tm, 1024), _rows(tm, 1024),
                   _rows(tm, 3 * CW), _rows(tm, 128)],
        out_shape=[jax.ShapeDtypeStruct((D, s), BF16), jax.ShapeDtypeStruct((s, 1024), BF16),
                   jax.ShapeDtypeStruct((s, 1024), BF16), jax.ShapeDtypeStruct((s, 1024), BF16),
                   jax.ShapeDtypeStruct((s, 3 * CW), BF16), jax.ShapeDtypeStruct((s, 128), F32)],
        scratch_shapes=[pltpu.VMEM((SUBLANES, 128), F32)],
        compiler_params=_cparams(56, ("arbitrary",)),
    )(x, g1, wp, bfp, pq, pk, oq, ok)


def _attn_fwd(qp, kp, v, *, t):
    s = qp.shape[0]
    nq = s // t

    def body(q_ref, k_ref, v_ref, o_ref, lse_ref, mk_ref):
        pi = pl.program_id(1)
        row = lax.broadcasted_iota(jnp.int32, (t, t), 0)
        col = lax.broadcasted_iota(jnp.int32, (t, t), 1)
        lane = lax.broadcasted_iota(jnp.int32, (t, 128), 1)

        def head_step(hh, rows, ki, carry, masked):
            m, acc = carry
            off = pl.multiple_of(ki * t, t)
            q = q_ref[rows, HP * hh:HP * (hh + 1)]
            k = k_ref[pl.ds(off, t), HP * hh:HP * (hh + 1)]
            sc = lax.dot_general(q, k, NT, preferred_element_type=F32)
            if masked:
                sc = jnp.where(col <= row, sc, -1e30)
            mn = jnp.maximum(m, jnp.max(sc, axis=-1, keepdims=True))
            p = jnp.exp(sc - mn).astype(BF16)
            acc = jnp.exp(m - mn) * acc + jnp.dot(p, v_ref[pl.ds(off, t), HP * hh:HP * (hh + 1)],
                                                  preferred_element_type=F32)
            return mn, acc

        def step(rows, ki, carry, masked):
            new = tuple(head_step(hh, rows, ki, carry[hh], masked) for hh in range(2))
            mk_ref[ki, rows] = jnp.where(lane < DH, jnp.broadcast_to(new[0][0], (t, 128)),
                                         jnp.broadcast_to(new[1][0], (t, 128)))
            return new

        init = (jnp.full((t, 1), -1e30, F32), jnp.zeros((t, 128), F32))
        top, bottom = slice(0, t), slice(t, 2 * t)

        def quad(j, carry):
            c0, c1 = carry
            c0 = step(top, 2 * j, c0, False)
            c1 = step(bottom, 2 * j, c1, False)
            c0 = step(top, 2 * j + 1, c0, False)
            c1 = step(bottom, 2 * j + 1, c1, False)
            return c0, c1

        c0, c1 = lax.fori_loop(0, pi, quad, ((init, init), (init, init)))
        f0 = step(top, 2 * pi, c0, True)
        c1 = step(bottom, 2 * pi, c1, False)
        f1 = step(bottom, 2 * pi + 1, c1, True)
        for rows, ((m0, acc0), (m1, acc1)) in ((top, f0), (bottom, f1)):
            l0, l1 = acc0[:, DH:DH + 1], acc1[:, DH:DH + 1]
            o_ref[rows, :] = jnp.where(lane < DH, acc0 / l0, pltpu.roll(acc1 / l1, DH, axis=1))
            lse_ref[rows, :] = jnp.where(lane < DH, jnp.broadcast_to(m0 + jnp.log(l0), (t, 128)),
                                         jnp.broadcast_to(m1 + jnp.log(l1), (t, 128)))

    return pl.pallas_call(
        body, name="attn_fwd", grid=(H // 2, nq // 2),
        in_specs=[pl.BlockSpec((2 * t, 2 * HP), lambda p, i: (i, p)),
                  pl.BlockSpec((s, 2 * HP), lambda p, i: (0, p)),
                  pl.BlockSpec((s, 2 * HP), lambda p, i: (0, p))],
        out_specs=[pl.BlockSpec((2 * t, 128), lambda p, i: (i, p)), pl.BlockSpec((2 * t, 128), lambda p, i: (i, p)),
                   pl.BlockSpec((nq, 2 * t, 128), lambda p, i: (0, i, p))],
        out_shape=[jax.ShapeDtypeStruct((s, AW), F32), jax.ShapeDtypeStruct((s, AW), F32),
                   jax.ShapeDtypeStruct((nq, s, AW), F32)],
        compiler_params=_cparams(48, ("arbitrary", "arbitrary")),
    )(qp, kp, v)


HALO = 16


def _conv_taps(bcu_ref, halo_ref, first, tm):
    z = bcu_ref[:, CW:2 * CW].astype(F32) * bcu_ref[:, 2 * CW:3 * CW].astype(F32)
    zh = jnp.where(first, 0.0, halo_ref[:, CW:2 * CW].astype(F32) * halo_ref[:, 2 * CW:3 * CW].astype(F32))
    row = lax.broadcasted_iota(jnp.int32, (tm, CW), 0)
    last, before = zh[HALO - 1:HALO, :], zh[HALO - 2:HALO - 1, :]
    z1 = jnp.where(row == 0, last, pltpu.roll(z, 1, axis=0))
    z2 = jnp.where(row == 0, before, jnp.where(row == 1, last, pltpu.roll(z, 2, axis=0)))
    return z, z1, z2


def _halo_before(tm, width):
    return pl.BlockSpec((HALO, width), lambda i: (jnp.maximum(i * (tm // HALO) - 1, 0), 0))


def _mix_out(o, bcu, cw8, ga, gc, gsum, w_out, x, g_post, g_ffn_pre, *, tm):
    s = x.shape[0]

    def body(o_ref, bcu_ref, halo_ref, cw_ref, ga_ref, gc_ref, gs_ref, w_ref, x_ref, g_ref, gf_ref,
             merged_ref, y_ref, x2_ref, cv_ref, h2_ref):
        z, z1, z2 = _conv_taps(bcu_ref, halo_ref, pl.program_id(0) == 0, tm)
        cv = cw_ref[0:1, :] * z2 + cw_ref[1:2, :] * z1 + cw_ref[2:3, :] * z
        cv_ref[...] = cv
        conv = bcu_ref[:, 0:CW].astype(F32) * cv
        ov = o_ref[...]
        ra = lax.rsqrt(_group_sum(ov * ov, gs_ref[...]) * (1.0 / DH) + EPS)
        rc = lax.rsqrt(_group_sum(conv * conv, gs_ref[...]) * (1.0 / DH) + EPS)
        merged = jnp.concatenate([ov * ra * ga_ref[...], conv * rc * gc_ref[...]], axis=1).astype(BF16)
        merged_ref[...] = merged
        y = jnp.dot(merged, w_ref[...], preferred_element_type=F32)
        y_ref[...] = y
        x2 = x_ref[...] + _rms_fwd(y, g_ref[...])[0]
        x2_ref[...] = x2
        h2_ref[...] = _rms_fwd(x2, gf_ref[...])[0].astype(BF16)

    return pl.pallas_call(
        body, name="mix_out", grid=(s // tm,),
        in_specs=[_rows(tm, AW), _rows(tm, 3 * CW), _halo_before(tm, 3 * CW), _full((SUBLANES, CW)),
                  _full((1, AW)), _full((1, CW)), _full((GS, GS)), _resident((D, D)), _rows(tm, D), _full((1, D)),
                  _full((1, D))],
        out_specs=[_rows(tm, D), _rows(tm, D), _rows(tm, D), _rows(tm, CW), _rows(tm, D)],
        out_shape=[jax.ShapeDtypeStruct((s, D), BF16), jax.ShapeDtypeStruct((s, D), F32),
                   jax.ShapeDtypeStruct((s, D), F32), jax.ShapeDtypeStruct((s, CW), F32),
                   jax.ShapeDtypeStruct((s, D), BF16)],
        compiler_params=_cparams(48, ("arbitrary",)),
    )(o, bcu, bcu, cw8, ga, gc, gsum, w_out, x, g_post, g_ffn_pre)


def _ffn_fwd_loss(h2, wgu, wd, x2, target, g_post, *, tm):
    s = x2.shape[0]

    def body(h_ref, w_ref, wd_ref, x2_ref, t_ref, g_ref,
             gate_ref, up_ref, a_ref, dx3_ref, dff_ref, loss_ref, dg_ref):
        @pl.when(pl.program_id(0) == 0)
        def _():
            loss_ref[...] = jnp.zeros_like(loss_ref)
            dg_ref[...] = jnp.zeros_like(dg_ref)

        h = h_ref[...]
        ff = None
        for c0, n in FF_CHUNKS:
            cols = slice(c0, c0 + n)
            gate = lax.dot_general(h, w_ref[0, cols, :], NT, preferred_element_type=F32)
            up = lax.dot_general(h, w_ref[1, cols, :], NT, preferred_element_type=F32)
            gate_ref[:, cols] = gate.astype(BF16)
            up_ref[:, cols] = up.astype(BF16)
            act = (gate * jax.nn.sigmoid(gate) * up).astype(BF16)
            a_ref[:, cols] = act
            part = jnp.dot(act, wd_ref[cols, :], preferred_element_type=F32)
            ff = part if ff is None else ff + part
        out, n, r = _rms_fwd(ff, g_ref[...])
        e = x2_ref[...] + out - t_ref[...]
        loss_ref[...] += _fold8(e * e)
        dx3 = e * (1.0 / D)
        dx3_ref[...] = dx3
        dff, dg = _rms_bwd(dx3, n, r, g_ref[...])
        dff_ref[...] = dff.astype(BF16)
        dg_ref[...] += _fold8(dg)

    wide = _rows(tm, DFF)
    return pl.pallas_call(
        body, name="ffn_fwd_loss", grid=(s // tm,),
        in_specs=[_rows(tm, D), _resident((2, DFF, D)), _resident((DFF, D)), _rows(tm, D), _rows(tm, D), _full((1, D))],
        out_specs=[wide, wide, wide, _rows(tm, D), _rows(tm, D), _full((SUBLANES, D)), _full((SUBLANES, D))],
        out_shape=[jax.ShapeDtypeStruct((s, DFF), BF16)] * 3
        + [jax.ShapeDtypeStruct((s, D), F32), jax.ShapeDtypeStruct((s, D), BF16),
           jax.ShapeDtypeStruct((SUBLANES, D), F32), jax.ShapeDtypeStruct((SUBLANES, D), F32)],
        compiler_params=_cparams(56, ("arbitrary",)),
    )(h2, wgu, wd, x2, target, g_post)


def _ffn_bwd(dff, wd, gate, up, wgu, x2, g_pre, dx3, y, g_post, *, tm):
    s = x2.shape[0]

    def body(dff_ref, wd_ref, gate_ref, up_ref, w_ref, x2_ref, gpre_ref, dx3_ref, y_ref, gpost_ref,
             dgu_ref, dx2_ref, dy_ref, dgpre_ref, dgpost_ref):
        @pl.when(pl.program_id(0) == 0)
        def _():
            dgpre_ref[...] = jnp.zeros_like(dgpre_ref)
            dgpost_ref[...] = jnp.zeros_like(dgpost_ref)

        dff = dff_ref[...]
        dh2 = None
        for c0, n in FF_CHUNKS:
            cols = slice(c0, c0 + n)
            da = lax.dot_general(dff, wd_ref[cols, :], NT, preferred_element_type=F32)
            g = gate_ref[:, cols].astype(F32)
            sg = jax.nn.sigmoid(g)
            dgate = (da * up_ref[:, cols].astype(F32) * (sg * (1.0 + g * (1.0 - sg)))).astype(BF16)
            dup = (da * (g * sg)).astype(BF16)
            dgu_ref[:, cols] = dgate
            dgu_ref[:, DFF + c0:DFF + c0 + n] = dup
            part = (jnp.dot(dgate, w_ref[0, cols, :], preferred_element_type=F32)
                    + jnp.dot(dup, w_ref[1, cols, :], preferred_element_type=F32))
            dh2 = part if dh2 is None else dh2 + part
        _, n2, r2 = _rms_fwd(x2_ref[...], gpre_ref[...])
        dxn, dg = _rms_bwd(dh2, n2, r2, gpre_ref[...])
        dgpre_ref[...] += _fold8(dg)
        dx2 = dx3_ref[...] + dxn
        dx2_ref[...] = dx2
        _, ny, ry = _rms_fwd(y_ref[...], gpost_ref[...])
        dy, dg2 = _rms_bwd(dx2, ny, ry, gpost_ref[...])
        dy_ref[...] = dy.astype(BF16)
        dgpost_ref[...] += _fold8(dg2)

    wide = _rows(tm, DFF)
    return pl.pallas_call(
        body, name="ffn_bwd", grid=(s // tm,),
        in_specs=[_rows(tm, D), _resident((DFF, D)), wide, wide, _resident((2, DFF, D)), _rows(tm, D), _full((1, D)),
                  _rows(tm, D), _rows(tm, D), _full((1, D))],
        out_specs=[_rows(tm, 2 * DFF), _rows(tm, D), _rows(tm, D),
                   _full((SUBLANES, D)), _full((SUBLANES, D))],
        out_shape=[jax.ShapeDtypeStruct((s, 2 * DFF), BF16), jax.ShapeDtypeStruct((s, D), F32),
                   jax.ShapeDtypeStruct((s, D), BF16), jax.ShapeDtypeStruct((SUBLANES, D), F32),
                   jax.ShapeDtypeStruct((SUBLANES, D), F32)],
        compiler_params=_cparams(56, ("arbitrary",)),
    )(dff, wd, gate, up, wgu, x2, g_pre, dx3, y, g_post)


def _grad_matmul(a, b, *, ta, tb, ts, name, vmem_mb=48):
    s, ka = a.shape
    nb = b.shape[1]
    ts = min(ts, s)
    nk = s // ts

    def body(a_ref, b_ref, o_ref, *acc):
        if nk == 1:
            o_ref[...] = lax.dot_general(a_ref[...], b_ref[...], TN, preferred_element_type=F32).astype(BF16)
            return
        k = pl.program_id(2)

        @pl.when(k == 0)
        def _():
            acc[0][...] = jnp.zeros_like(acc[0])

        acc[0][...] += lax.dot_general(a_ref[...], b_ref[...], TN, preferred_element_type=F32)

        @pl.when(k == nk - 1)
        def _():
            o_ref[...] = acc[0][...].astype(BF16)

    whole_b = {"pipeline_mode": pl.Buffered(1)} if nk == 1 and nb == tb else {}
    return pl.pallas_call(
        body, name=name, grid=(ka // ta, nb // tb, nk),
        in_specs=[pl.BlockSpec((ts, ta), lambda i, j, k: (k, i)),
                  pl.BlockSpec((ts, tb), lambda i, j, k: (k, j), **whole_b)],
        out_specs=pl.BlockSpec((ta, tb), lambda i, j, k: (i, j)),
        out_shape=jax.ShapeDtypeStruct((ka, nb), BF16),
        scratch_shapes=[pltpu.VMEM((ta, tb), F32)] if nk > 1 else [],
        compiler_params=_cparams(vmem_mb, ("arbitrary", "arbitrary", "arbitrary")),
    )(a, b)


GW_TILE = 256


def _grad_w_in(h1t, pieces):
    ka, s = h1t.shape
    widths = [p.shape[1] for p in pieces]
    assert all(w % GW_TILE == 0 for w in widths)
    first = [sum(widths[:i]) // GW_TILE for i in range(len(pieces))]
    count = [w // GW_TILE for w in widths]

    def body(a_ref, *refs):
        o_ref = refs[-1]
        j = pl.program_id(0)
        for ref, f0, n in zip(refs[:-1], first, count):
            @pl.when((j >= f0) & (j < f0 + n))
            def _(ref=ref):
                o_ref[...] = jnp.dot(a_ref[...], ref[...], preferred_element_type=F32).astype(BF16)

    def spec(f0, n):
        return pl.BlockSpec((s, GW_TILE), lambda j: (0, jnp.clip(j - f0, 0, n - 1)))

    return pl.pallas_call(
        body, name="grad_w_in", grid=(sum(count),),
        in_specs=[_resident((ka, s))] + [spec(f0, n) for f0, n in zip(first, count)],
        out_specs=pl.BlockSpec((ka, GW_TILE), lambda j: (0, j)),
        out_shape=jax.ShapeDtypeStruct((ka, sum(widths)), BF16),
        compiler_params=_cparams(56, ("arbitrary",)),
    )(h1t, *pieces)


def _mix_bwd(dy, w_out, o, cv, bcu, ga, gc, gsum, *, tm):
    s = dy.shape[0]

    def group_norm_bwd(dn_out, v, g, gs):
        r = lax.rsqrt(_group_sum(v * v, gs) * (1.0 / DH) + EPS)
        n = v * r
        dn = dn_out * g
        return r * (dn - n * (_group_sum(dn * n, gs) * (1.0 / DH))), dn_out * n

    def body(dy_ref, w_ref, o_ref, cv_ref, bcu_ref, ga_ref, gc_ref, gs_ref,
             do_ref, dl_ref, dcv_ref, db_ref, dga_ref, dgc_ref):
        @pl.when(pl.program_id(0) == 0)
        def _():
            dga_ref[...] = jnp.zeros_like(dga_ref)
            dgc_ref[...] = jnp.zeros_like(dgc_ref)

        dm = lax.dot_general(dy_ref[...], w_ref[...], NT, preferred_element_type=F32)
        ov = o_ref[...]
        do, dga = group_norm_bwd(dm[:, 0:AW], ov, ga_ref[...], gs_ref[...])
        dob = do.astype(BF16)
        do_ref[...] = dob
        dl_ref[...] = _group_sum(dob.astype(F32) * ov, gs_ref[...])
        dga_ref[...] += _fold8(dga)
        gate_b = bcu_ref[:, 0:CW].astype(F32)
        cv = cv_ref[...]
        dconv, dgc = group_norm_bwd(dm[:, AW:D], gate_b * cv, gc_ref[...], gs_ref[...])
        dgc_ref[...] += _fold8(dgc)
        dcv_ref[...] = dconv * gate_b
        db_ref[...] = (dconv * cv).astype(BF16)

    return pl.pallas_call(
        body, name="mix_bwd", grid=(s // tm,),
        in_specs=[_rows(tm, D), _resident((D, D)), _rows(tm, AW), _rows(tm, CW), _rows(tm, 3 * CW),
                  _full((1, AW)), _full((1, CW)), _full((GS, GS))],
        out_specs=[_rows(tm, AW), _rows(tm, AW), _rows(tm, CW), _rows(tm, CW),
                   _full((SUBLANES, AW)), _full((SUBLANES, CW))],
        out_shape=[jax.ShapeDtypeStruct((s, AW), BF16), jax.ShapeDtypeStruct((s, AW), F32),
                   jax.ShapeDtypeStruct((s, CW), F32), jax.ShapeDtypeStruct((s, CW), BF16),
                   jax.ShapeDtypeStruct((SUBLANES, AW), F32), jax.ShapeDtypeStruct((SUBLANES, CW), F32)],
        compiler_params=_cparams(48, ("arbitrary",)),
    )(dy, w_out, o, cv, bcu, ga, gc, gsum)


def _conv_bwd(dcv, db, bcu, cw8, *, tm):
    s = dcv.shape[0]
    nt = s // tm

    def body(dcv_ref, nxt_ref, db_ref, bcu_ref, halo_ref, cw_ref, dbcu_ref, dw_ref):
        i = pl.program_id(0)

        @pl.when(i == 0)
        def _():
            dw_ref[...] = jnp.zeros_like(dw_ref)

        z, z1, z2 = _conv_taps(bcu_ref, halo_ref, i == 0, tm)
        d = dcv_ref[...]
        dw_ref[0] += _fold8(d * z2)
        dw_ref[1] += _fold8(d * z1)
        dw_ref[2] += _fold8(d * z)
        nx = jnp.where(i == nt - 1, 0.0, nxt_ref[...])
        row = lax.broadcasted_iota(jnp.int32, (tm, CW), 0)
        d1 = jnp.where(row == tm - 1, nx[0:1, :], pltpu.roll(d, tm - 1, axis=0))
        d2 = jnp.where(row == tm - 2, nx[0:1, :], jnp.where(row == tm - 1, nx[1:2, :], pltpu.roll(d, tm - 2, axis=0)))
        dz = cw_ref[2:3, :] * d + cw_ref[1:2, :] * d1 + cw_ref[0:1, :] * d2
        dbcu_ref[:, 0:CW] = db_ref[...]
        dbcu_ref[:, CW:2 * CW] = (dz * bcu_ref[:, 2 * CW:3 * CW].astype(F32)).astype(BF16)
        dbcu_ref[:, 2 * CW:3 * CW] = (dz * bcu_ref[:, CW:2 * CW].astype(F32)).astype(BF16)

    return pl.pallas_call(
        body, name="conv_bwd", grid=(nt,),
        in_specs=[_rows(tm, CW),
                  pl.BlockSpec((SUBLANES, CW), lambda i: (jnp.minimum((i + 1) * (tm // SUBLANES), s // SUBLANES - 1), 0)),
                  _rows(tm, CW), _rows(tm, 3 * CW), _halo_before(tm, 3 * CW), _full((SUBLANES, CW))],
        out_specs=[_rows(tm, 3 * CW), _full((3, SUBLANES, CW))],
        out_shape=[jax.ShapeDtypeStruct((s, 3 * CW), BF16), jax.ShapeDtypeStruct((3, SUBLANES, CW), F32)],
        compiler_params=_cparams(48, ("arbitrary",)),
    )(dcv, dcv, db, bcu, bcu, cw8)


def _attn_bwd(qp, kp, v, do, lse, dl, mk, *, t):
    s = qp.shape[0]
    nq = s // t

    def body(q_ref, k_ref, v_ref, do_ref, lse_ref, dl_ref, mk_ref, dq_ref, dk_ref, dv_ref, dkx_ref, dq_acc):
        pi = pl.program_id(1)

        @pl.when(pi == 0)
        def _():
            dq_acc[...] = jnp.zeros_like(dq_acc)

        row = lax.broadcasted_iota(jnp.int32, (t, t), 0)
        col = lax.broadcasted_iota(jnp.int32, (t, t), 1)
        lane = lax.broadcasted_iota(jnp.int32, (t, 128), 1)

        def head_step(hh, qi, carry, modes):
            off = pl.multiple_of(qi * t, t)
            rows = pl.ds(off, t)
            q = q_ref[rows, HP * hh:HP * (hh + 1)]
            qt = q.T
            lse_col = lse_ref[rows, DH * hh:DH * hh + 1]
            dl_col = dl_ref[rows, DH * hh:DH * hh + 1]
            do2 = do_ref[rows, :]
            dom = jnp.where(lane < DH, do2 if hh == 0 else pltpu.roll(do2, DH, axis=1), jnp.zeros((), BF16))
            new, dss = [], []
            for half, masked in enumerate(modes):
                if masked is None:
                    new.append(carry[half])
                    continue
                dk, dv, cs = carry[half]
                keys = slice(half * t, (half + 1) * t)
                m_col = mk_ref[half, rows, DH * hh:DH * hh + 1]
                scale = jnp.exp(m_col - lse_col)
                sc = lax.dot_general(q, k_ref[keys, HP * hh:HP * (hh + 1)], NT, preferred_element_type=F32) - m_col
                if masked:
                    sc = jnp.where(col <= row, sc, -1e30)
                pt = jnp.exp(sc).astype(BF16)
                dp = lax.dot_general(dom, v_ref[keys, HP * hh:HP * (hh + 1)], NT, preferred_element_type=F32)
                ds32 = (pt.astype(F32) * scale) * (dp - dl_col)
                ds = ds32.astype(BF16)
                cs = cs + _fold8(ds32)
                dv = dv + jnp.dot((dom.astype(F32) * scale).astype(BF16).T, pt, preferred_element_type=F32)
                dk = dk + jnp.dot(qt, ds, preferred_element_type=F32)
                new.append((dk, dv, cs))
                dss.append((half, ds))
            if len(dss) == 2:
                dq = jnp.dot(jnp.concatenate([dss[0][1], dss[1][1]], axis=1), k_ref[:, HP * hh:HP * (hh + 1)],
                             preferred_element_type=F32)
            else:
                half, ds = dss[0]
                dq = jnp.dot(ds, k_ref[half * t:(half + 1) * t, HP * hh:HP * (hh + 1)], preferred_element_type=F32)
            dq_acc[rows, HP * hh:HP * (hh + 1)] += dq
            return tuple(new)

        def step(qi, carry, modes):
            return tuple(head_step(hh, qi, carry[hh], modes) for hh in range(2))

        def two_heads(a0, a1):
            return jnp.where(lane < DH, a0, pltpu.roll(a1, DH, axis=1))

        def rows_to_lanes(a0, a1):
            return jnp.concatenate([a0, a1], axis=0).T

        zero = (jnp.zeros((HP, t), F32), jnp.zeros((128, t), F32), jnp.zeros((SUBLANES, t), F32))
        carry = step(2 * pi, ((zero, zero), (zero, zero)), (True, None))
        carry = step(2 * pi + 1, carry, (False, True))

        def pair(j, carry):
            qi = 2 * (pi + 1 + j)
            return step(qi + 1, step(qi, carry, (False, False)), (False, False))

        carry = lax.fori_loop(0, nq // 2 - 1 - pi, pair, carry)
        for half in range(2):
            keys = slice(half * t, (half + 1) * t)
            (dk0, dv0, cs0), (dk1, dv1, cs1) = carry[0][half], carry[1][half]
            dk_ref[keys, :] = rows_to_lanes(dk0[0:DH], dk1[0:DH]).astype(BF16)
            dv_ref[keys, :] = rows_to_lanes(dv0[0:DH], dv1[0:DH]).astype(BF16)
            total = lambda cs: jnp.broadcast_to(jnp.sum(cs, axis=0, keepdims=True), (DH, t))
            dkx_ref[keys, :] = rows_to_lanes(total(cs0), total(cs1))

        @pl.when(pi == nq // 2 - 1)
        def _():
            for c in range(s // t):
                rows = slice(c * t, (c + 1) * t)
                dq_ref[rows, :] = two_heads(dq_acc[rows, 0:HP], dq_acc[rows, HP:2 * HP]).astype(BF16)

    return pl.pallas_call(
        body, name="attn_bwd", grid=(H // 2, nq // 2),
        in_specs=[pl.BlockSpec((s, 2 * HP), lambda p, i: (0, p)),
                  pl.BlockSpec((2 * t, 2 * HP), lambda p, i: (i, p)),
                  pl.BlockSpec((2 * t, 2 * HP), lambda p, i: (i, p)),
                  pl.BlockSpec((s, 128), lambda p, i: (0, p)),
                  pl.BlockSpec((s, 128), lambda p, i: (0, p)),
                  pl.BlockSpec((s, 128), lambda p, i: (0, p)),
                  pl.BlockSpec((2, s, 128), lambda p, i: (i, 0, p))],
        out_specs=[pl.BlockSpec((s, 128), lambda p, i: (0, p)),
                   pl.BlockSpec((2 * t, 128), lambda p, i: (i, p)),
                   pl.BlockSpec((2 * t, 128), lambda p, i: (i, p)),
                   pl.BlockSpec((2 * t, 128), lambda p, i: (i, p))],
        out_shape=[jax.ShapeDtypeStruct((s, AW), BF16), jax.ShapeDtypeStruct((s, AW), BF16),
                   jax.ShapeDtypeStruct((s, AW), BF16), jax.ShapeDtypeStruct((s, AW), F32)],
        scratch_shapes=[pltpu.VMEM((s, 2 * HP), F32)],
        compiler_params=_cparams(56, ("arbitrary", "arbitrary")),
    )(qp, kp, v, do, lse, dl, mk)


def _forget_bwd(dkx, z, sel, *, tm):
    s = dkx.shape[0]
    nt = s // tm

    def body(dk_ref, z_ref, sel_ref, dfl_ref, dbf_ref, carry):
        @pl.when(pl.program_id(0) == 0)
        def _():
            carry[...] = jnp.zeros_like(carry)
            dbf_ref[...] = jnp.zeros_like(dbf_ref)

        dc = _split_dot(dk_ref[...], sel_ref[...])
        row = lax.broadcasted_iota(jnp.int32, (tm, tm), 0)
        col = lax.broadcasted_iota(jnp.int32, (tm, tm), 1)
        tri = (col >= row).astype(BF16)
        dlogf = _exact_dot01(tri, dc) + carry[0:1, :]
        carry[...] = jnp.broadcast_to(dlogf[0:1, :], carry.shape)
        dz = dlogf * (1.0 - jax.nn.sigmoid(z_ref[...]))
        dfl_ref[:, 0:128] = dz.astype(BF16)
        dfl_ref[:, 128:GW_TILE] = jnp.zeros((tm, GW_TILE - 128), BF16)
        dbf_ref[...] += _fold8(dz)

    rev = lambda i: (nt - 1 - i, 0)
    return pl.pallas_call(
        body, name="forget_bwd", grid=(nt,),
        in_specs=[pl.BlockSpec((tm, AW), rev), pl.BlockSpec((tm, 128), rev), _full((AW, 128))],
        out_specs=[pl.BlockSpec((tm, GW_TILE), rev), _full((SUBLANES, 128))],
        out_shape=[jax.ShapeDtypeStruct((s, GW_TILE), BF16), jax.ShapeDtypeStruct((SUBLANES, 128), F32)],
        scratch_shapes=[pltpu.VMEM((SUBLANES, 128), F32)],
        compiler_params=_cparams(48, ("arbitrary",)),
    )(dkx, z, sel)


def _in_proj_bwd(pieces, wp, x, g1, dx2, *, tm):
    s = x.shape[0]

    def body(q_ref, k_ref, v_ref, bcu_ref, f_ref, w_ref, x_ref, g_ref, dx2_ref, dx_ref, dg_ref):
        @pl.when(pl.program_id(0) == 0)
        def _():
            dg_ref[...] = jnp.zeros_like(dg_ref)

        dh = None
        for ref, (lo, hi) in zip((q_ref, k_ref, v_ref, bcu_ref, f_ref), PIECES):
            part = lax.dot_general(ref[...], w_ref[:, lo:hi], NT, preferred_element_type=F32)
            dh = part if dh is None else dh + part
        _, n, r = _rms_fwd(x_ref[...], g_ref[...])
        dxn, dg = _rms_bwd(dh, n, r, g_ref[...])
        dx_ref[...] = dx2_ref[...] + dxn
        dg_ref[...] += _fold8(dg)

    return pl.pallas_call(
        body, name="in_proj_bwd", grid=(s // tm,),
        in_specs=[_rows(tm, hi - lo) for lo, hi in PIECES] + [_resident((D, WP)), _rows(tm, D), _full((1, D)), _rows(tm, D)],
        out_specs=[_rows(tm, D), _full((SUBLANES, D))],
        out_shape=[jax.ShapeDtypeStruct((s, D), F32), jax.ShapeDtypeStruct((SUBLANES, D), F32)],
        compiler_params=_cparams(56, ("arbitrary",)),
    )(*pieces, wp, x, g1, dx2)


def _position():
    return lax.axis_index("x"), lax.axis_index("y"), lax.axis_index("c")


ANY = pl.BlockSpec(memory_space=pl.ANY)


def _all_gather(shards):
    n = len(shards)

    def body(*refs):
        x_refs, out_refs = refs[:n], refs[n:2 * n]
        send_sems, recv_sems, local_sems = refs[2 * n:]
        x, y, c = _position()
        me, sibling = (x, y, c), (x, y, 1 - c)
        chips = [(1 - x, y), (x, 1 - y), (1 - x, 1 - y)]

        def copy(a, k, block, to, own=False):
            slot = out_refs[a].at[4 * block[0] + 2 * block[1] + block[2]]
            return pltpu.make_async_remote_copy(
                src_ref=x_refs[a] if own else slot, dst_ref=slot,
                send_sem=send_sems.at[7 * a + k], recv_sem=recv_sems.at[7 * a + k], device_id=to, device_id_type=MESH_ID)

        mine = [pltpu.make_async_copy(x_refs[a], out_refs[a].at[4 * x + 2 * y + c], local_sems.at[a]) for a in range(n)]
        for cp in mine:
            cp.start()
        first = []
        for a in range(n):
            first.append(copy(a, 0, me, sibling, own=True))
            first += [copy(a, 1 + j, me, (*chip, c), own=True) for j, chip in enumerate(chips)]
        for cp in first:
            cp.start()
        passed = []
        for j, chip in enumerate(chips):
            for a in range(n):
                copy(a, 1 + j, (*chip, c), me).wait_recv()
                fwd = copy(a, 4 + j, (*chip, c), sibling)
                fwd.start()
                passed.append(fwd)
        for a in range(n):
            copy(a, 0, sibling, me).wait_recv()
            for j, chip in enumerate(chips):
                copy(a, 4 + j, (*chip, 1 - c), me).wait_recv()
        for cp in first + passed:
            cp.wait_send()
        for cp in mine:
            cp.wait()

    return pl.pallas_call(
        body, name="all_gather_weights",
        out_shape=[jax.ShapeDtypeStruct((NDEV,) + sh.shape, sh.dtype) for sh in shards],
        in_specs=[ANY] * n, out_specs=[ANY] * n,
        scratch_shapes=[pltpu.SemaphoreType.DMA((7 * n,)), pltpu.SemaphoreType.DMA((7 * n,)), pltpu.SemaphoreType.DMA((n,))],
    )(*shards)


def _pair_exchange(grads):
    n = len(grads)

    def body(*refs):
        g_refs, out_refs = refs[:n], refs[n:2 * n]
        send_sems, recv_sems = refs[2 * n:]
        x, y, c = _position()
        copies = [pltpu.make_async_remote_copy(
            src_ref=g_refs[a].at[:, pl.ds(1 - c, 1)], dst_ref=out_refs[a], send_sem=send_sems.at[a],
            recv_sem=recv_sems.at[a], device_id=(x, y, 1 - c), device_id_type=MESH_ID) for a in range(n)]
        for cp in copies:
            cp.start()
        for cp in copies:
            cp.wait()

    return pl.pallas_call(
        body, name="grad_pair_exchange",
        out_shape=[jax.ShapeDtypeStruct((4, 1) + g.shape[2:], g.dtype) for g in grads],
        in_specs=[ANY] * n, out_specs=[ANY] * n,
        scratch_shapes=[pltpu.SemaphoreType.DMA((n,)), pltpu.SemaphoreType.DMA((n,))],
    )(*grads)


def _pair_sum(g, got, idx, *, tr, name):
    r, c = g.shape[2:]

    def body(idx_ref, g_ref, got_ref, pb_ref, own_ref):
        p = g_ref[0, 0].astype(F32) + got_ref[0, 0].astype(F32)
        pb_ref[0] = p.astype(BF16)

        @pl.when(pl.program_id(1) == idx_ref[1])
        def _():
            own_ref[...] = p

    return pl.pallas_call(
        body, name=name,
        grid_spec=pltpu.PrefetchScalarGridSpec(
            num_scalar_prefetch=1, grid=(r // tr, 4),
            in_specs=[pl.BlockSpec((1, 1, tr, c), lambda i, j, idx: (j, idx[0], i, 0)),
                      pl.BlockSpec((1, 1, tr, c), lambda i, j, idx: (j, 0, i, 0))],
            out_specs=[pl.BlockSpec((1, tr, c), lambda i, j, idx: (j, i, 0)),
                       pl.BlockSpec((tr, c), lambda i, j, idx: (i, 0))]),
        out_shape=[jax.ShapeDtypeStruct((4, r, c), BF16), jax.ShapeDtypeStruct((r, c), F32)],
        compiler_params=_cparams(32, ("arbitrary", "arbitrary")),
    )(idx, g, got)


HBM = pl.BlockSpec(memory_space=pltpu.HBM)
SEM = pl.BlockSpec(memory_space=pltpu.SEMAPHORE)
DATAFLOW = pltpu.SideEffectType.DATAFLOW_SIDE_EFFECTING


PEERS = {"gather": NDEV - 1, "scatter": NDEV - 1, "chips": 3}


def _exchange_copies(src_refs, land_refs, send_sems, recv_sems, mode):
    x, y, c = _position()
    me, my_chip = 4 * x + 2 * y + c, 2 * x + y
    npeers = PEERS[mode]
    copies, own = [], []
    for a, (s_ref, l_ref) in enumerate(zip(src_refs, land_refs)):
        for k in range(npeers):
            if mode == "chips":
                px, py, pc = x ^ ((k + 1) >> 1), y ^ ((k + 1) & 1), c
                src, dst = s_ref.at[2 * px + py], l_ref.at[my_chip]
            else:
                px, py, pc = x ^ ((k + 1) >> 2), y ^ (((k + 1) >> 1) & 1), c ^ ((k + 1) & 1)
                src, dst = (s_ref.at[4 * px + 2 * py + pc] if mode == "scatter" else s_ref), l_ref.at[me]
            copies.append(pltpu.make_async_remote_copy(
                src_ref=src, dst_ref=dst, send_sem=send_sems.at[npeers * a + k], recv_sem=recv_sems.at[npeers * a + k],
                device_id=(px, py, pc), device_id_type=MESH_ID))
        slot = my_chip if mode == "chips" else me
        own.append(pltpu.make_async_copy(s_ref if mode == "gather" else s_ref.at[slot], l_ref.at[slot],
                                         send_sems.at[npeers * len(src_refs) + a]))
    return copies, own


def _exchange_start(srcs, lands, after, *, mode, name):
    n = len(srcs)
    nsem = PEERS[mode] * n

    def body(*refs):
        token = refs[-1]
        copies, own = _exchange_copies(refs[:n], refs[n:2 * n], refs[2 * n + 1], refs[2 * n + 2], mode)
        for cp in copies + own:
            cp.start()
        token[...] = jnp.zeros_like(token)

    arrays = list(srcs) + list(lands)
    outs = pl.pallas_call(
        body, name=name,
        out_shape=(pltpu.SemaphoreType.DMA((nsem + n,)), pltpu.SemaphoreType.DMA((nsem,)),
                   *[pltpu.HBM(a.shape, a.dtype) for a in arrays], jax.ShapeDtypeStruct((SUBLANES, LANES), F32)),
        in_specs=[HBM] * (2 * n) + [ANY],
        out_specs=(SEM, SEM, *[HBM] * (2 * n), pl.BlockSpec(memory_space=pltpu.VMEM)),
        input_output_aliases={i: 2 + i for i in range(2 * n)},
        compiler_params=pltpu.CompilerParams(has_side_effects=DATAFLOW),
    )(*[pltpu.with_memory_space_constraint(a, pltpu.HBM) for a in arrays], after)
    return outs[0], outs[1], outs[2:2 + n], outs[2 + n:2 + 2 * n], outs[-1]


def _exchange_wait(send_sems, recv_sems, srcs, lands, after, *, mode, name):
    n = len(srcs)

    def body(*refs):
        copies, own = _exchange_copies(refs[:n], refs[n:2 * n], refs[2 * n], refs[2 * n + 1], mode)
        for cp in copies:
            cp.wait_send()
            cp.wait_recv()
        for cp in own:
            cp.wait()

    arrays = list(srcs) + list(lands)
    outs = pl.pallas_call(
        body, name=name,
        out_shape=tuple(pltpu.HBM(a.shape, a.dtype) for a in arrays),
        in_specs=[HBM] * (2 * n) + [SEM, SEM, ANY],
        out_specs=tuple([HBM] * (2 * n)),
        input_output_aliases={i: i for i in range(2 * n)},
        compiler_params=pltpu.CompilerParams(has_side_effects=DATAFLOW),
    )(*arrays, send_sems, recv_sems, after)
    return outs[n:]


def _small_all_reduce(parts):
    def body(gmp_ref, gmo_ref, gfp_ref, gfo_ref, ga_ref, gc_ref, dw_ref, bf_ref, loss_ref,
             out_ref, buf, send_sems, recv_sems):
        x, y, c = _position()
        me = 4 * x + 2 * y + c

        def colsum(v):
            return jnp.sum(v, axis=0, keepdims=True)

        loss = jnp.sum(colsum(loss_ref[...]), axis=1, keepdims=True) * (0.5 / D)
        rows = [colsum(gmp_ref[...]), colsum(gmo_ref[...]), colsum(gfp_ref[...]), colsum(gfo_ref[...]),
                jnp.concatenate([colsum(ga_ref[...]), colsum(gc_ref[...])], axis=1),
                jnp.concatenate([colsum(dw_ref[0]), colsum(dw_ref[1])], axis=1),
                jnp.concatenate([colsum(dw_ref[2]), colsum(bf_ref[...]), jnp.broadcast_to(loss, (1, 128)),
                                 jnp.zeros((1, 256), F32)], axis=1),
                jnp.zeros((1, D), F32)]
        buf[me] = jnp.concatenate(rows, axis=0)
        copies = []
        for mm in range(1, NDEV):
            peer = (x ^ (mm >> 2), y ^ ((mm >> 1) & 1), c ^ (mm & 1))
            copies.append(pltpu.make_async_remote_copy(
                src_ref=buf.at[me], dst_ref=buf.at[me], send_sem=send_sems.at[mm - 1], recv_sem=recv_sems.at[mm - 1],
                device_id=peer, device_id_type=MESH_ID))
        for cp in copies:
            cp.start()
        for cp in copies:
            cp.wait_recv()
        for cp in copies:
            cp.wait_send()
        acc = buf[0]
        for d in range(1, NDEV):
            acc = acc + buf[d]
        out_ref[...] = acc

    vm = pl.BlockSpec(memory_space=pltpu.VMEM)
    return pl.pallas_call(
        body, name="small_all_reduce",
        out_shape=jax.ShapeDtypeStruct((SUBLANES, D), F32),
        in_specs=[vm] * len(parts), out_specs=vm,
        scratch_shapes=[pltpu.VMEM((NDEV, SUBLANES, D), F32), pltpu.SemaphoreType.DMA((7,)), pltpu.SemaphoreType.DMA((7,))],
    )(*parts)


def _adam_update(w, g, m, v):
    nm = ADAM_B1 * m + (1.0 - ADAM_B1) * g
    nv = ADAM_B2 * v + (1.0 - ADAM_B2) * (g * g)
    m_hat = nm / (1.0 - ADAM_B1 ** ADAM_STEP)
    v_hat = nv / (1.0 - ADAM_B2 ** ADAM_STEP)
    return -ADAM_LR * (m_hat / (jnp.sqrt(v_hat) + ADAM_EPS) + ADAM_WD * w), nm, nv


SMALL_SLOTS = {"g_mix_pre": (0, 0, D), "g_mix_post": (1, 0, D), "g_ffn_pre": (2, 0, D), "g_ffn_post": (3, 0, D),
               "g_attn_out": (4, 0, AW), "g_conv_out": (4, AW, CW), "b_forget": (6, CW, H)}


def _small_adamw(small, conv_grad, params):
    names = list(params)
    n = len(names)

    def body(*refs):
        small_ref, cg_ref = refs[0], refs[1]
        ins, outs = refs[2:2 + 3 * n], refs[2 + 3 * n:]
        for i, name in enumerate(names):
            w_ref, m_ref, v_ref = ins[3 * i:3 * i + 3]
            g_ref, d_ref, nm_ref, nv_ref = outs[4 * i:4 * i + 4]
            if name == "conv_w":
                g = cg_ref[...]
            else:
                r, c0, width = SMALL_SLOTS[name]
                g = small_ref[r:r + 1, c0:c0 + width]
            g_ref[...] = g
            d_ref[...], nm_ref[...], nv_ref[...] = _adam_update(w_ref[...], g, m_ref[...], v_ref[...])

    vm = pl.BlockSpec(memory_space=pltpu.VMEM)
    flat = [a for name in names for a in params[name]]
    outs = pl.pallas_call(
        body, name="adamw_small",
        in_specs=[vm] * (2 + 3 * n), out_specs=[vm] * (4 * n),
        out_shape=[jax.ShapeDtypeStruct(params[name][0].shape, F32) for name in names for _ in range(4)],
    )(small, conv_grad, *flat)
    return {name: outs[4 * i:4 * i + 4] for i, name in enumerate(names)}


def _chip_sum_adamw(got, own, idx, wt, mt, vt, *, tr, name):
    cols, rows = wt.shape
    gcols = own.shape[1]

    def body(idx_ref, got_ref, own_ref, w_ref, m_ref, v_ref, g_ref, d_ref, nm_ref, nv_ref):
        g = jnp.zeros((tr, gcols), F32)
        for j in range(4):
            g = g + jnp.where(idx_ref[1] == j, own_ref[...], got_ref[j].astype(F32))
        g = g.T[:cols]
        g_ref[...] = g
        d_ref[...], nm_ref[...], nv_ref[...] = _adam_update(w_ref[...], g, m_ref[...], v_ref[...])

    spec = pl.BlockSpec((cols, tr), lambda i, idx: (0, i))
    gspec = pl.BlockSpec((tr, gcols), lambda i, idx: (i, 0))
    return pl.pallas_call(
        body, name=name,
        grid_spec=pltpu.PrefetchScalarGridSpec(
            num_scalar_prefetch=1, grid=(rows // tr,),
            in_specs=[pl.BlockSpec((4, tr, gcols), lambda i, idx: (0, i, 0)), gspec, spec, spec, spec],
            out_specs=[spec] * 4),
        out_shape=[jax.ShapeDtypeStruct((cols, rows), F32)] * 4,
        compiler_params=_cparams(32, ("arbitrary",)),
    )(idx, got, own, wt, mt, vt)


def _device_sum_adamw(land, w, m, v, *, tr, name):
    rows, cols = w.shape

    def body(land_ref, w_ref, m_ref, v_ref, g_ref, d_ref, nm_ref, nv_ref):
        g = land_ref[0].astype(F32)
        for dev in range(1, NDEV):
            g = g + land_ref[dev].astype(F32)
        g_ref[...] = g
        d_ref[...], nm_ref[...], nv_ref[...] = _adam_update(w_ref[...], g, m_ref[...], v_ref[...])

    spec = pl.BlockSpec((tr, cols), lambda i: (i, 0))
    return pl.pallas_call(
        body, name=name, grid=(rows // tr,),
        in_specs=[pl.BlockSpec((NDEV, tr, cols), lambda i: (0, i, 0)), spec, spec, spec],
        out_specs=[spec] * 4,
        out_shape=[jax.ShapeDtypeStruct((rows, cols), F32)] * 4,
        compiler_params=_cparams(32, ("arbitrary",)),
    )(land, w, m, v)


def _placement_constants():
    j = jnp.arange(128)[:, None]
    lane = jnp.arange(1024)[None, :]
    head, sub = lane // HP, lane % HP
    piece, jh = j // H, j % H
    valid = (j < 3 * H) & (jh == head)
    pq = jnp.where(valid & (sub == DH + piece), 1.0, 0.0).astype(BF16)
    pk = jnp.where(valid & (sub == DH + 3 + piece), -1.0, 0.0).astype(BF16)
    oq = jnp.where((sub >= DH + 3) & (sub < DH + 6), 1.0, 0.0).astype(F32)
    ok = jnp.where((sub >= DH) & (sub < DH + 3), 1.0, 0.0).astype(F32)
    r = jnp.arange(AW)[:, None]
    cc = jnp.arange(128)[None, :]
    sel = jnp.where((r % DH == 3) & (r // DH == cc), -1.0, 0.0).astype(BF16)
    gi = jnp.arange(GS)
    gsum = (gi[:, None] // DH == gi[None, :] // DH).astype(BF16)
    return pq, pk, oq, ok, sel, gsum


def _local_step(xs, tgt, wp, late_weights, cw8, bfp, g_attn_out, g_conv_out,
                g_mix_pre, g_mix_post, g_ffn_pre, g_ffn_post, early_grads=None, last_grad=None):
    pq, pk, oq, ok, sel, gsum = _placement_constants()
    h1t, qp, kp, vv, bcu, zf = _in_proj(xs, g_mix_pre, wp, bfp, pq, pk, oq, ok, tm=512)
    o, lse, mk = _attn_fwd(qp, kp, vv, t=512)
    w_out_f, wgu, wd = late_weights(lse)
    merged, y, x2, cv, h2 = _mix_out(o, bcu, cw8, g_attn_out, g_conv_out, gsum, w_out_f, xs, g_mix_post, g_ffn_pre, tm=512)
    gate, up, act, dx3, dff, loss_p, dg_ffn_post = _ffn_fwd_loss(h2, wgu, wd, x2, tgt, g_ffn_post, tm=512)

    dgu, dx2, dy, dg_ffn_pre, dg_mix_post = _ffn_bwd(dff, wd, gate, up, wgu, x2, g_ffn_pre, dx3, y, g_mix_post, tm=256)
    dw_down = _grad_matmul(act, dff, ta=DFF // 2, tb=D, ts=4096, name="grad_w_down", vmem_mb=56)
    dw_gu = _grad_matmul(dgu, h2, ta=DFF // 2, tb=D, ts=4096, name="grad_w_gate_up", vmem_mb=56).reshape(NDEV, FB, D)
    dw_out = _grad_matmul(merged, dy, ta=1024, tb=1024, ts=2048, name="grad_w_out")
    token = early_grads(dw_out, dw_gu, dw_down) if early_grads is not None else None
    ga = g_attn_out if token is None else g_attn_out + token[0:1, 0:1]
    do, dl, dcv, db, dg_attn, dg_conv = _mix_bwd(dy, w_out_f, o, cv, bcu, ga, g_conv_out, gsum, tm=512)
    dbcu, dtaps = _conv_bwd(dcv, db, bcu, cw8, tm=512)
    dqp, dkp, dv, dkx = _attn_bwd(qp, kp, vv, do, lse, dl, mk, t=512)
    dfl, dbf = _forget_bwd(dkx, zf, sel, tm=512)
    pieces = (dqp, dkp, dv, dbcu, dfl)
    dwp = _grad_w_in(h1t, pieces)
    token = last_grad(dwp) if last_grad is not None else None
    g1 = g_mix_pre if token is None else g_mix_pre + token[0:1, 0:1]
    grad_x, dg_mix_pre = _in_proj_bwd(pieces, wp, xs, g1, dx2, tm=512)
    return (grad_x, dwp, dw_out, dw_gu, dw_down, dg_mix_pre, dg_mix_post, dg_ffn_pre, dg_ffn_post, dg_attn, dg_conv,
            dtaps, dbf, loss_p)


BIG_TILES = {"w_in": 256, "w_out": 128, "w_gate_up": 176, "w_down": 176}


def kernel(x, w_in, b_forget, conv_w, g_attn_out, g_conv_out, w_out, g_mix_pre, g_mix_post, w_gate_up, w_down, g_ffn_pre, g_ffn_post, loss_target, m_w_in, m_b_forget, m_conv_w, m_g_attn_out, m_g_conv_out, m_w_out, m_g_mix_pre, m_g_mix_post, m_w_gate_up, m_w_down, m_g_ffn_pre, m_g_ffn_post, v_w_in, v_b_forget, v_conv_w, v_g_attn_out, v_g_conv_out, v_w_out, v_g_mix_pre, v_g_mix_post, v_w_gate_up, v_w_down, v_g_ffn_pre, v_g_ffn_post):
    xc, yc, cc = _position()
    my_chip = 2 * xc + yc
    me = 2 * my_chip + cc
    idx = jnp.stack([cc, my_chip]).astype(jnp.int32)
    tables = _in_layout_tables()

    w_in_b = w_in[0].astype(BF16)
    g_in, g_last, g_taps = _all_gather([w_in_b[:, :IN_MAIN], w_in_b[:, IN_MAIN].reshape(SUBLANES, LANES), conv_w[0]])
    last_cols = jnp.pad(g_last.reshape(NDEV, D).T.astype(F32), ((0, 0), (0, LANES - NDEV)))
    wp = _assemble_w_in(g_in, last_cols, tables, tr=256)
    cw8 = jnp.pad(g_taps.transpose(1, 0, 2).reshape(3, CW), ((0, SUBLANES - 3), (0, 0)))

    late = [w_out[0].astype(BF16), w_gate_up[0].T.astype(BF16), w_down[0].astype(BF16)]
    ssem, rsem, late_thru, land_thru, token = _exchange_start(
        late, [lax.empty((NDEV,) + s.shape, s.dtype) for s in late], g_in, mode="gather",
        name="gather_late_start")
    bfp = jnp.pad(b_forget, ((0, 0), (0, 128 - H))) + token[0:1, :]

    def late_weights(after):
        l_out, l_gu, l_down = _exchange_wait(ssem, rsem, late_thru, land_thru, after, mode="gather", name="gather_late_wait")
        return l_out.reshape(D, D), l_gu.reshape(2, DFF, D), l_down.reshape(DFF, D)

    early = {}

    def early_grads(dw_out, dw_gu, dw_down):
        srcs = [dw_out.reshape(NDEV, D // NDEV, D), dw_gu, dw_down.reshape(NDEV, DFF // NDEV, D)]
        lands = [lax.empty(s.shape, s.dtype) for s in srcs]
        early["handles"] = _exchange_start(srcs, lands, dw_out, mode="scatter", name="scatter_early_start")
        return early["handles"][4]

    last = {}

    def last_grad(dwp):
        g_w_in = _disassemble_w_in(dwp, tables, tr=256).reshape(4, 2, D, IN_PAD)
        (from_sibling,) = _pair_exchange([g_w_in])
        pair_b, last["own"] = _pair_sum(g_w_in, from_sibling, idx, tr=BIG_TILES["w_in"], name="grad_pair_sum_w_in")
        last["handles"] = _exchange_start([pair_b], [lax.empty(pair_b.shape, pair_b.dtype)], last["own"], mode="chips",
                                          name="chips_w_in_start")
        return last["handles"][4]

    (grad_x, dwp, dw_out, dw_gu, dw_down, dg_mix_pre, dg_mix_post, dg_ffn_pre, dg_ffn_post, dg_attn, dg_conv,
     dtaps, dbf, loss_p) = _local_step(x[0], loss_target[0], wp, late_weights, cw8, bfp, g_attn_out, g_conv_out,
                                        g_mix_pre, g_mix_post, g_ffn_pre, g_ffn_post, early_grads, last_grad)

    e_ssem, e_rsem, e_srcs, e_lands, _ = early["handles"]
    land_out, land_gu, land_down = _exchange_wait(e_ssem, e_rsem, e_srcs, e_lands, dg_mix_pre, mode="scatter",
                                                  name="scatter_early_wait")
    res = {}
    big = {"w_out": (land_out, w_out[0], m_w_out[0], v_w_out[0]),
           "w_gate_up": (land_gu, w_gate_up[0].T, m_w_gate_up[0].T, v_w_gate_up[0].T),
           "w_down": (land_down, w_down[0], m_w_down[0], v_w_down[0])}
    for name, (land, w, m, v) in big.items():
        outs = _device_sum_adamw(land, w, m, v, tr=BIG_TILES[name], name="adamw_" + name)
        res[name] = [(o.T if name == "w_gate_up" else o)[None] for o in outs]
    c_ssem, c_rsem, c_srcs, c_lands, _ = last["handles"]
    after = sum(res[n][1][0, :SUBLANES, :LANES] for n in big)
    (from_chips,) = _exchange_wait(c_ssem, c_rsem, c_srcs, c_lands, after, mode="chips", name="chips_w_in_wait")
    outs = _chip_sum_adamw(from_chips, last["own"], idx, w_in[0].T, m_w_in[0].T, v_w_in[0].T,
                           tr=BIG_TILES["w_in"], name="adamw_w_in")
    res["w_in"] = [o.T[None] for o in outs]

    small = _small_all_reduce([dg_mix_pre, dg_mix_post, dg_ffn_pre, dg_ffn_post, dg_attn, dg_conv, dtaps, dbf, loss_p])
    taps_full = jnp.concatenate([small[5:6, :CW], small[5:6, CW:], small[6:7, :CW]], axis=0)
    loss = small[6, CW + 128]
    taps_first = lambda a: a.transpose(1, 0, 2)
    smalls = {"b_forget": (b_forget, m_b_forget, v_b_forget),
              "conv_w": (taps_first(conv_w), taps_first(m_conv_w), taps_first(v_conv_w)),
              "g_attn_out": (g_attn_out, m_g_attn_out, v_g_attn_out), "g_conv_out": (g_conv_out, m_g_conv_out, v_g_conv_out),
              "g_mix_pre": (g_mix_pre, m_g_mix_pre, v_g_mix_pre), "g_mix_post": (g_mix_post, m_g_mix_post, v_g_mix_post),
              "g_ffn_pre": (g_ffn_pre, m_g_ffn_pre, v_g_ffn_pre), "g_ffn_post": (g_ffn_post, m_g_ffn_post, v_g_ffn_post)}
    own_taps = lax.dynamic_slice(taps_full, (0, me * 64), (3, 64))[:, None, :]
    for name, outs in _small_adamw(small, own_taps, smalls).items():
        res[name] = [taps_first(o) for o in outs] if name == "conv_w" else list(outs)

    order = ["w_in", "b_forget", "conv_w", "g_attn_out", "g_conv_out", "w_out", "g_mix_pre", "g_mix_post",
             "w_gate_up", "w_down", "g_ffn_pre", "g_ffn_post"]
    outs = [loss, grad_x[None]]
    for k in range(4):
        outs += [res[n][k] for n in order]
    return tuple(outs)
```

```python
import functools

import numpy as np

import jax
import jax.numpy as jnp
from jax import lax
from jax.experimental import pallas as pl
from jax.experimental.pallas import tpu as pltpu

F32 = jnp.float32
BF16 = jnp.bfloat16
MESH_ID = pl.DeviceIdType.MESH

D = 1024
H = 8
DH = 64
AW = 512
CW = 512
DFF = 2816
FB = DFF // 4
FF_CHUNKS = ((0, 768), (768, 768), (1536, 768), (2304, 512))
HP = 128
OFF_Q, OFF_K, OFF_V, OFF_BCU, OFF_F = 0, 512, 1024, 1536, 3072
WP = OFF_F + 128
PIECES = ((OFF_Q, OFF_K), (OFF_K, OFF_V), (OFF_V, OFF_BCU), (OFF_BCU, OFF_F), (OFF_F, WP))
EPS = 1e-6
NDEV = 8
LANES = 128
SUBLANES = 8
IN_COLS = 385
IN_PAD = 512
IN_MAIN = 384
WIN = 640
ADAM_LR, ADAM_B1, ADAM_B2, ADAM_EPS, ADAM_WD, ADAM_STEP = 0.001, 0.9, 0.999, 1e-08, 0.01, 10

NT = (((1,), (1,)), ((), ()))
TN = (((0,), (0,)), ((), ()))


def _cparams(vmem_mb=None, sem=None):
    kw = {}
    if vmem_mb is not None:
        kw["vmem_limit_bytes"] = vmem_mb << 20
    if sem is not None:
        kw["dimension_semantics"] = sem
    return pltpu.CompilerParams(**kw)


def _full(shape):
    return pl.BlockSpec(shape, lambda *_: (0,) * len(shape))


def _resident(shape):
    return pl.BlockSpec(shape, lambda *_: (0,) * len(shape), pipeline_mode=pl.Buffered(1))


def _rows(tm, width):
    return pl.BlockSpec((tm, width), lambda i: (i, 0))


def _fold8(v):
    r, w = v.shape
    return jnp.sum(v.reshape(r // SUBLANES, SUBLANES, w), axis=0)


def _split_dot(v, m01):
    hi = v.astype(BF16)
    lo = (v - hi.astype(F32)).astype(BF16)
    return (jnp.dot(hi, m01, preferred_element_type=F32)
            + jnp.dot(lo, m01, preferred_element_type=F32))


GS = 256


def _group_sum(v, g01):
    parts = [_split_dot(v[:, c:c + GS], g01) for c in range(0, v.shape[1], GS)]
    return parts[0] if len(parts) == 1 else jnp.concatenate(parts, axis=1)


def _exact_dot01(m01, v):
    p1 = v.astype(BF16)
    r1 = v - p1.astype(F32)
    p2 = r1.astype(BF16)
    p3 = (r1 - p2.astype(F32)).astype(BF16)
    return (jnp.dot(m01, p1, preferred_element_type=F32) + jnp.dot(m01, p2, preferred_element_type=F32)
            + jnp.dot(m01, p3, preferred_element_type=F32))


def _rms_fwd(v, g):
    r = lax.rsqrt(jnp.mean(v * v, axis=-1, keepdims=True) + EPS)
    n = v * r
    return n * g, n, r


def _rms_bwd(do, n, r, g):
    dn = do * g
    return r * (dn - n * jnp.mean(dn * n, axis=-1, keepdims=True)), do * n


def _padded_column(n):
    if n < AW:
        return OFF_Q + n, 0.125
    if n < 3 * AW:
        return n, 1.0
    if n < 3 * AW + H:
        return OFF_F + n - 3 * AW, 1.0
    return OFF_BCU + n - 3 * AW - H, 1.0


def _in_layout_tables():
    dest = -np.ones((IN_PAD, LANES), np.int32)
    dest_f = -np.ones((IN_PAD, LANES), np.int32)
    scale = np.zeros((IN_PAD, LANES), np.float32)
    starts = []
    for k in range(NDEV):
        cols = [_padded_column(IN_COLS * k + j) for j in range(IN_COLS)]
        main = [c for c, _ in cols if c < OFF_F]
        ws = min((min(main) // LANES) * LANES, OFF_F - WIN)
        assert ws <= min(main) and max(main) < ws + WIN
        starts.append(ws)
        for j, (c, sc) in enumerate(cols):
            scale[j, k] = sc
            if c < OFF_F:
                dest[j, k] = c - ws
            else:
                dest_f[j, k] = c - OFF_F
    f_shards = tuple(k for k in range(NDEV) if (dest_f[:, k] >= 0).any())
    return tuple(starts), f_shards, jnp.asarray(dest), jnp.asarray(dest_f), jnp.asarray(scale)


def _perm(dest_ref, scale_ref, k, width, rows=IN_PAD):
    lane = lax.broadcasted_iota(jnp.int32, (rows, width), 1)
    return jnp.where(dest_ref[0:rows, k:k + 1] == lane, scale_ref[0:rows, k:k + 1], 0.0).astype(BF16)


def _assemble_w_in(blocks, last_cols, tables, *, tr):
    starts, f_shards, dest, dest_f, scale = tables
    last = [_padded_column(IN_COLS * k + IN_MAIN) for k in range(NDEV)]
    f_main = [any(_padded_column(IN_COLS * k + j)[0] >= OFF_F for j in range(IN_MAIN)) for k in range(NDEV)]
    assert IN_COLS == IN_MAIN + 1

    def body(b_ref, c_ref, dest_ref, destf_ref, scale_ref, o_ref):
        o_ref[...] = jnp.zeros_like(o_ref)
        lane = lax.broadcasted_iota(jnp.int32, (tr, LANES), 1)
        for k in range(NDEV):
            b = b_ref[k]
            ws = starts[k]
            part = jnp.dot(b, _perm(dest_ref, scale_ref, k, WIN, IN_MAIN), preferred_element_type=F32)
            o_ref[:, ws:ws + WIN] += part.astype(BF16)
            if f_main[k]:
                part = jnp.dot(b, _perm(destf_ref, scale_ref, k, 128, IN_MAIN), preferred_element_type=F32)
                o_ref[:, OFF_F:WP] += part.astype(BF16)
            col, sc = last[k]
            tile = (col // LANES) * LANES
            o_ref[:, tile:tile + LANES] += jnp.where(lane == col - tile, c_ref[:, k:k + 1] * sc, 0.0).astype(BF16)

    tab = _full((IN_PAD, LANES))
    return pl.pallas_call(
        body, name="assemble_w_in", grid=(D // tr,),
        in_specs=[pl.BlockSpec((NDEV, tr, IN_MAIN), lambda i: (0, i, 0)), _rows(tr, LANES), tab, tab, tab],
        out_specs=_rows(tr, WP),
        out_shape=jax.ShapeDtypeStruct((D, WP), BF16),
        compiler_params=_cparams(48, ("arbitrary",)),
    )(blocks, last_cols, dest, dest_f, scale)


def _disassemble_w_in(dwp, tables, *, tr):
    starts, f_shards, dest, dest_f, scale = tables
    width = dwp.shape[1]

    def body(g_ref, dest_ref, destf_ref, scale_ref, o_ref):
        for k in range(NDEV):
            ws = starts[k]
            acc = lax.dot_general(g_ref[:, ws:ws + WIN], _perm(dest_ref, scale_ref, k, WIN), NT, preferred_element_type=F32)
            if k in f_shards:
                acc = acc + lax.dot_general(g_ref[:, OFF_F:WP], _perm(destf_ref, scale_ref, k, 128), NT,
                                            preferred_element_type=F32)
            o_ref[k] = acc.astype(BF16)

    tab = _full((IN_PAD, LANES))
    return pl.pallas_call(
        body, name="disassemble_w_in", grid=(D // tr,),
        in_specs=[_rows(tr, width), tab, tab, tab],
        out_specs=pl.BlockSpec((NDEV, tr, IN_PAD), lambda i: (0, i, 0)),
        out_shape=jax.ShapeDtypeStruct((NDEV, D, IN_PAD), BF16),
        compiler_params=_cparams(48, ("arbitrary",)),
    )(dwp, dest, dest_f, scale)


def _in_proj(x, g1, wp, bfp, pq, pk, oq, ok, *, tm):
    s = x.shape[0]

    def body(x_ref, g_ref, w_ref, bf_ref, pq_ref, pk_ref, oq_ref, ok_ref,
             ht_ref, qp_ref, kp_ref, v_ref, bcu_ref, z_ref, carry):
        @pl.when(pl.program_id(0) == 0)
        def _():
            carry[...] = jnp.zeros_like(carry)

        h = _rms_fwd(x_ref[...], g_ref[...])[0].astype(BF16)
        ht_ref[...] = h.T
        z = jnp.dot(h, w_ref[:, OFF_F:WP], preferred_element_type=F32) + bf_ref[...]
        z_ref[...] = z
        lane = lax.broadcasted_iota(jnp.int32, (tm, 128), 1)
        logf = jnp.where(lane < H, jnp.minimum(z, 0.0) - jnp.log(1.0 + jnp.exp(-jnp.abs(z))), 0.0)
        row = lax.broadcasted_iota(jnp.int32, (tm, tm), 0)
        col = lax.broadcasted_iota(jnp.int32, (tm, tm), 1)
        tri = (col <= row).astype(BF16)
        c = _exact_dot01(tri, logf) + carry[0:1, :]
        carry[...] = jnp.broadcast_to(c[tm - 1:tm, :], carry.shape)
        c1 = c.astype(BF16).astype(F32)
        r1 = c - c1
        c2 = r1.astype(BF16).astype(F32)
        c3 = (r1 - c2).astype(BF16).astype(F32)
        zc = (c1 + pltpu.roll(c2, 8, axis=1) + pltpu.roll(c3, 16, axis=1)).astype(BF16)

        def pad_heads(v):
            blocks = []
            for pair in range(H // 2):
                two = v[:, 128 * pair:128 * (pair + 1)]
                blocks.append(jnp.where(lane < DH, two, 0.0))
                blocks.append(jnp.where(lane < DH, pltpu.roll(two, DH, axis=1), 0.0))
            return jnp.concatenate(blocks, axis=1)

        q = jnp.dot(h, w_ref[:, OFF_Q:OFF_K], preferred_element_type=F32)
        qp_ref[...] = (pad_heads(q) + jnp.dot(zc, pq_ref[...], preferred_element_type=F32) + oq_ref[...]).astype(BF16)
        k = jnp.dot(h, w_ref[:, OFF_K:OFF_V], preferred_element_type=F32)
        kp_ref[...] = (pad_heads(k) + jnp.dot(zc, pk_ref[...], preferred_element_type=F32) + ok_ref[...]).astype(BF16)
        v = pad_heads(jnp.dot(h, w_ref[:, OFF_V:OFF_BCU], preferred_element_type=F32))
        ones_lane = lax.broadcasted_iota(jnp.int32, (tm, H * HP), 1) % HP == DH
        v_ref[...] = jnp.where(ones_lane, 1.0, v).astype(BF16)
        bcu_ref[...] = jnp.dot(h, w_ref[:, OFF_BCU:OFF_F], preferred_element_type=F32).astype(BF16)

    return pl.pallas_call(
        body, name="in_proj", grid=(s // tm,),
        in_specs=[_rows(tm, D), _full((1, D)), _resident((D, WP)), _full((1, 128)),
                  _full((128, 1024)), _full((128, 1024)), _full((1, 1024)), _full((1, 1024))],
        out_specs=[pl.BlockSpec((D, tm), lambda i: (0, i)), _rows(tm, 1024), _rows(tm, 1024), _rows(tm, 1024),
                   _rows(tm, 3 * CW), _rows(tm, 128)],
        out_shape=[jax.ShapeDtypeStruct((D, s), BF16), jax.ShapeDtypeStruct((s, 1024), BF16),
                   jax.ShapeDtypeStruct((s, 1024), BF16), jax.ShapeDtypeStruct((s, 1024), BF16),
                   jax.ShapeDtypeStruct((s, 3 * CW), BF16), jax.ShapeDtypeStruct((s, 128), F32)],
        scratch_shapes=[pltpu.VMEM((SUBLANES, 128), F32)],
        compiler_params=_cparams(56, ("arbitrary",)),
    )(x, g1, wp, bfp, pq, pk, oq, ok)


def _attn_fwd(qp, kp, v, *, t):
    s = qp.shape[0]
    nq = s // t

    def body(q_ref, k_ref, v_ref, o_ref, lse_ref, mk_ref):
        pi = pl.program_id(1)
        row = lax.broadcasted_iota(jnp.int32, (t, t), 0)
        col = lax.broadcasted_iota(jnp.int32, (t, t), 1)
        lane = lax.broadcasted_iota(jnp.int32, (t, 128), 1)

        def head_step(hh, rows, ki, carry, masked):
            m, acc = carry
            off = pl.multiple_of(ki * t, t)
            q = q_ref[rows, HP * hh:HP * (hh + 1)]
            k = k_ref[pl.ds(off, t), HP * hh:HP * (hh + 1)]
            sc = lax.dot_general(q, k, NT, preferred_element_type=F32)
            if masked:
                sc = jnp.where(col <= row, sc, -1e30)
            mn = jnp.maximum(m, jnp.max(sc, axis=-1, keepdims=True))
            p = jnp.exp(sc - mn).astype(BF16)
            acc = jnp.exp(m - mn) * acc + jnp.dot(p, v_ref[pl.ds(off, t), HP * hh:HP * (hh + 1)],
                                                  preferred_element_type=F32)
            return mn, acc

        def step(rows, ki, carry, masked):
            new = tuple(head_step(hh, rows, ki, carry[hh], masked) for hh in range(2))
            mk_ref[ki, rows] = jnp.where(lane < DH, jnp.broadcast_to(new[0][0], (t, 128)),
                                         jnp.broadcast_to(new[1][0], (t, 128)))
            return new

        init = (jnp.full((t, 1), -1e30, F32), jnp.zeros((t, 128), F32))
        top, bottom = slice(0, t), slice(t, 2 * t)

        def quad(j, carry):
            c0, c1 = carry
            c0 = step(top, 2 * j, c0, False)
            c1 = step(bottom, 2 * j, c1, False)
            c0 = step(top, 2 * j + 1, c0, False)
            c1 = step(bottom, 2 * j + 1, c1, False)
            return c0, c1

        c0, c1 = lax.fori_loop(0, pi, quad, ((init, init), (init, init)))
        f0 = step(top, 2 * pi, c0, True)
        c1 = step(bottom, 2 * pi, c1, False)
        f1 = step(bottom, 2 * pi + 1, c1, True)
        for rows, ((m0, acc0), (m1, acc1)) in ((top, f0), (bottom, f1)):
            l0, l1 = acc0[:, DH:DH + 1], acc1[:, DH:DH + 1]
            o_ref[rows, :] = jnp.where(lane < DH, acc0 / l0, pltpu.roll(acc1 / l1, DH, axis=1))
            lse_ref[rows, :] = jnp.where(lane < DH, jnp.broadcast_to(m0 + jnp.log(l0), (t, 128)),
                                         jnp.broadcast_to(m1 + jnp.log(l1), (t, 128)))

    return pl.pallas_call(
        body, name="attn_fwd", grid=(H // 2, nq // 2),
        in_specs=[pl.BlockSpec((2 * t, 2 * HP), lambda p, i: (i, p)),
                  pl.BlockSpec((s, 2 * HP), lambda p, i: (0, p)),
                  pl.BlockSpec((s, 2 * HP), lambda p, i: (0, p))],
        out_specs=[pl.BlockSpec((2 * t, 128), lambda p, i: (i, p)), pl.BlockSpec((2 * t, 128), lambda p, i: (i, p)),
                   pl.BlockSpec((nq, 2 * t, 128), lambda p, i: (0, i, p))],
        out_shape=[jax.ShapeDtypeStruct((s, AW), F32), jax.ShapeDtypeStruct((s, AW), F32),
                   jax.ShapeDtypeStruct((nq, s, AW), F32)],
        compiler_params=_cparams(48, ("arbitrary", "arbitrary")),
    )(qp, kp, v)


HALO = 16


def _conv_taps(bcu_ref, halo_ref, first, tm):
    z = bcu_ref[:, CW:2 * CW].astype(F32) * bcu_ref[:, 2 * CW:3 * CW].astype(F32)
    zh = jnp.where(first, 0.0, halo_ref[:, CW:2 * CW].astype(F32) * halo_ref[:, 2 * CW:3 * CW].astype(F32))
    row = lax.broadcasted_iota(jnp.int32, (tm, CW), 0)
    last, before = zh[HALO - 1:HALO, :], zh[HALO - 2:HALO - 1, :]
    z1 = jnp.where(row == 0, last, pltpu.roll(z, 1, axis=0))
    z2 = jnp.where(row == 0, before, jnp.where(row == 1, last, pltpu.roll(z, 2, axis=0)))
    return z, z1, z2


def _halo_before(tm, width):
    return pl.BlockSpec((HALO, width), lambda i: (jnp.maximum(i * (tm // HALO) - 1, 0), 0))


def _mix_out(o, bcu, cw8, ga, gc, gsum, w_out, x, g_post, g_ffn_pre, *, tm):
    s = x.shape[0]

    def body(o_ref, bcu_ref, halo_ref, cw_ref, ga_ref, gc_ref, gs_ref, w_ref, x_ref, g_ref, gf_ref,
             merged_ref, y_ref, x2_ref, cv_ref, h2_ref):
        z, z1, z2 = _conv_taps(bcu_ref, halo_ref, pl.program_id(0) == 0, tm)
        cv = cw_ref[0:1, :] * z2 + cw_ref[1:2, :] * z1 + cw_ref[2:3, :] * z
        cv_ref[...] = cv
        conv = bcu_ref[:, 0:CW].astype(F32) * cv
        ov = o_ref[...]
        ra = lax.rsqrt(_group_sum(ov * ov, gs_ref[...]) * (1.0 / DH) + EPS)
        rc = lax.rsqrt(_group_sum(conv * conv, gs_ref[...]) * (1.0 / DH) + EPS)
        merged = jnp.concatenate([ov * ra * ga_ref[...], conv * rc * gc_ref[...]], axis=1).astype(BF16)
        merged_ref[...] = merged
        y = jnp.dot(merged, w_ref[...], preferred_element_type=F32)
        y_ref[...] = y
        x2 = x_ref[...] + _rms_fwd(y, g_ref[...])[0]
        x2_ref[...] = x2
        h2_ref[...] = _rms_fwd(x2, gf_ref[...])[0].astype(BF16)

    return pl.pallas_call(
        body, name="mix_out", grid=(s // tm,),
        in_specs=[_rows(tm, AW), _rows(tm, 3 * CW), _halo_before(tm, 3 * CW), _full((SUBLANES, CW)),
                  _full((1, AW)), _full((1, CW)), _full((GS, GS)), _resident((D, D)), _rows(tm, D), _full((1, D)),
                  _full((1, D))],
        out_specs=[_rows(tm, D), _rows(tm, D), _rows(tm, D), _rows(tm, CW), _rows(tm, D)],
        out_shape=[jax.ShapeDtypeStruct((s, D), BF16), jax.ShapeDtypeStruct((s, D), F32),
                   jax.ShapeDtypeStruct((s, D), F32), jax.ShapeDtypeStruct((s, CW), F32),
                   jax.ShapeDtypeStruct((s, D), BF16)],
        compiler_params=_cparams(48, ("arbitrary",)),
    )(o, bcu, bcu, cw8, ga, gc, gsum, w_out, x, g_post, g_ffn_pre)


def _ffn_fwd_loss(h2, wgu, wd, x2, target, g_post, *, tm):
    s = x2.shape[0]

    def body(h_ref, w_ref, wd_ref, x2_ref, t_ref, g_ref,
             gate_ref, up_ref, a_ref, dx3_ref, dff_ref, loss_ref, dg_ref):
        @pl.when(pl.program_id(0) == 0)
        def _():
            loss_ref[...] = jnp.zeros_like(loss_ref)
            dg_ref[...] = jnp.zeros_like(dg_ref)

        h = h_ref[...]
        ff = None
        for c0, n in FF_CHUNKS:
            cols = slice(c0, c0 + n)
            gate = lax.dot_general(h, w_ref[0, cols, :], NT, preferred_element_type=F32)
            up = lax.dot_general(h, w_ref[1, cols, :], NT, preferred_element_type=F32)
            gate_ref[:, cols] = gate.astype(BF16)
            up_ref[:, cols] = up.astype(BF16)
            act = (gate * jax.nn.sigmoid(gate) * up).astype(BF16)
            a_ref[:, cols] = act
            part = jnp.dot(act, wd_ref[cols, :], preferred_element_type=F32)
            ff = part if ff is None else ff + part
        out, n, r = _rms_fwd(ff, g_ref[...])
        e = x2_ref[...] + out - t_ref[...]
        loss_ref[...] += _fold8(e * e)
        dx3 = e * (1.0 / D)
        dx3_ref[...] = dx3
        dff, dg = _rms_bwd(dx3, n, r, g_ref[...])
        dff_ref[...] = dff.astype(BF16)
        dg_ref[...] += _fold8(dg)

    wide = _rows(tm, DFF)
    return pl.pallas_call(
        body, name="ffn_fwd_loss", grid=(s // tm,),
        in_specs=[_rows(tm, D), _resident((2, DFF, D)), _resident((DFF, D)), _rows(tm, D), _rows(tm, D), _full((1, D))],
        out_specs=[wide, wide, wide, _rows(tm, D), _rows(tm, D), _full((SUBLANES, D)), _full((SUBLANES, D))],
        out_shape=[jax.ShapeDtypeStruct((s, DFF), BF16)] * 3
        + [jax.ShapeDtypeStruct((s, D), F32), jax.ShapeDtypeStruct((s, D), BF16),
           jax.ShapeDtypeStruct((SUBLANES, D), F32), jax.ShapeDtypeStruct((SUBLANES, D), F32)],
        compiler_params=_cparams(56, ("arbitrary",)),
    )(h2, wgu, wd, x2, target, g_post)


def _ffn_bwd(dff, wd, gate, up, wgu, x2, g_pre, dx3, y, g_post, *, tm):
    s = x2.shape[0]

    def body(dff_ref, wd_ref, gate_ref, up_ref, w_ref, x2_ref, gpre_ref, dx3_ref, y_ref, gpost_ref,
             dgu_ref, dx2_ref, dy_ref, dgpre_ref, dgpost_ref):
        @pl.when(pl.program_id(0) == 0)
        def _():
            dgpre_ref[...] = jnp.zeros_like(dgpre_ref)
            dgpost_ref[...] = jnp.zeros_like(dgpost_ref)

        dff = dff_ref[...]
        dh2 = None
        for c0, n in FF_CHUNKS:
            cols = slice(c0, c0 + n)
            da = lax.dot_general(dff, wd_ref[cols, :], NT, preferred_element_type=F32)
            g = gate_ref[:, cols].astype(F32)
            sg = jax.nn.sigmoid(g)
            dgate = (da * up_ref[:, cols].astype(F32) * (sg * (1.0 + g * (1.0 - sg)))).astype(BF16)
            dup = (da * (g * sg)).astype(BF16)
            dgu_ref[:, cols] = dgate
            dgu_ref[:, DFF + c0:DFF + c0 + n] = dup
            part = (jnp.dot(dgate, w_ref[0, cols, :], preferred_element_type=F32)
                    + jnp.dot(dup, w_ref[1, cols, :], preferred_element_type=F32))
            dh2 = part if dh2 is None else dh2 + part
        _, n2, r2 = _rms_fwd(x2_ref[...], gpre_ref[...])
        dxn, dg = _rms_bwd(dh2, n2, r2, gpre_ref[...])
        dgpre_ref[...] += _fold8(dg)
        dx2 = dx3_ref[...] + dxn
        dx2_ref[...] = dx2
        _, ny, ry = _rms_fwd(y_ref[...], gpost_ref[...])
        dy, dg2 = _rms_bwd(dx2, ny, ry, gpost_ref[...])
        dy_ref[...] = dy.astype(BF16)
        dgpost_ref[...] += _fold8(dg2)

    wide = _rows(tm, DFF)
    return pl.pallas_call(
        body, name="ffn_bwd", grid=(s // tm,),
        in_specs=[_rows(tm, D), _resident((DFF, D)), wide, wide, _resident((2, DFF, D)), _rows(tm, D), _full((1, D)),
                  _rows(tm, D), _rows(tm, D), _full((1, D))],
        out_specs=[_rows(tm, 2 * DFF), _rows(tm, D), _rows(tm, D),
                   _full((SUBLANES, D)), _full((SUBLANES, D))],
        out_shape=[jax.ShapeDtypeStruct((s, 2 * DFF), BF16), jax.ShapeDtypeStruct((s, D), F32),
                   jax.ShapeDtypeStruct((s, D), BF16), jax.ShapeDtypeStruct((SUBLANES, D), F32),
                   jax.ShapeDtypeStruct((SUBLANES, D), F32)],
        compiler_params=_cparams(56, ("arbitrary",)),
    )(dff, wd, gate, up, wgu, x2, g_pre, dx3, y, g_post)


def _grad_matmul(a, b, *, ta, tb, ts, name, vmem_mb=48):
    s, ka = a.shape
    nb = b.shape[1]
    ts = min(ts, s)
    nk = s // ts

    def body(a_ref, b_ref, o_ref, *acc):
        if nk == 1:
            o_ref[...] = lax.dot_general(a_ref[...], b_ref[...], TN, preferred_element_type=F32).astype(BF16)
            return
        k = pl.program_id(2)

        @pl.when(k == 0)
        def _():
            acc[0][...] = jnp.zeros_like(acc[0])

        acc[0][...] += lax.dot_general(a_ref[...], b_ref[...], TN, preferred_element_type=F32)

        @pl.when(k == nk - 1)
        def _():
            o_ref[...] = acc[0][...].astype(BF16)

    whole_b = {"pipeline_mode": pl.Buffered(1)} if nk == 1 and nb == tb else {}
    return pl.pallas_call(
        body, name=name, grid=(ka // ta, nb // tb, nk),
        in_specs=[pl.BlockSpec((ts, ta), lambda i, j, k: (k, i)),
                  pl.BlockSpec((ts, tb), lambda i, j, k: (k, j), **whole_b)],
        out_specs=pl.BlockSpec((ta, tb), lambda i, j, k: (i, j)),
        out_shape=jax.ShapeDtypeStruct((ka, nb), BF16),
        scratch_shapes=[pltpu.VMEM((ta, tb), F32)] if nk > 1 else [],
        compiler_params=_cparams(vmem_mb, ("arbitrary", "arbitrary", "arbitrary")),
    )(a, b)


GW_TILE = 256


def _grad_w_in(h1t, pieces):
    ka, s = h1t.shape
    widths = [p.shape[1] for p in pieces]
    assert all(w % GW_TILE == 0 for w in widths)
    first = [sum(widths[:i]) // GW_TILE for i in range(len(pieces))]
    count = [w // GW_TILE for w in widths]

    def body(a_ref, *refs):
        o_ref = refs[-1]
        j = pl.program_id(0)
        for ref, f0, n in zip(refs[:-1], first, count):
            @pl.when((j >= f0) & (j < f0 + n))
            def _(ref=ref):
                o_ref[...] = jnp.dot(a_ref[...], ref[...], preferred_element_type=F32).astype(BF16)

    def spec(f0, n):
        return pl.BlockSpec((s, GW_TILE), lambda j: (0, jnp.clip(j - f0, 0, n - 1)))

    return pl.pallas_call(
        body, name="grad_w_in", grid=(sum(count),),
        in_specs=[_resident((ka, s))] + [spec(f0, n) for f0, n in zip(first, count)],
        out_specs=pl.BlockSpec((ka, GW_TILE), lambda j: (0, j)),
        out_shape=jax.ShapeDtypeStruct((ka, sum(widths)), BF16),
        compiler_params=_cparams(56, ("arbitrary",)),
    )(h1t, *pieces)


def _mix_bwd(dy, w_out, o, cv, bcu, ga, gc, gsum, after, *, tm):
    s = dy.shape[0]

    def group_norm_bwd(dn_out, v, g, gs):
        r = lax.rsqrt(_group_sum(v * v, gs) * (1.0 / DH) + EPS)
        n = v * r
        dn = dn_out * g
        return r * (dn - n * (_group_sum(dn * n, gs) * (1.0 / DH))), dn_out * n

    def body(dy_ref, w_ref, o_ref, cv_ref, bcu_ref, ga_ref, gc_ref, gs_ref, after_ref,
             do_ref, dl_ref, dcv_ref, db_ref, dga_ref, dgc_ref):
        @pl.when(pl.program_id(0) == 0)
        def _():
            dga_ref[...] = jnp.zeros_like(dga_ref)
            dgc_ref[...] = jnp.zeros_like(dgc_ref)

        dm = lax.dot_general(dy_ref[...], w_ref[...], NT, preferred_element_type=F32)
        ov = o_ref[...]
        do, dga = group_norm_bwd(dm[:, 0:AW], ov, ga_ref[...], gs_ref[...])
        dob = do.astype(BF16)
        do_ref[...] = dob
        dl_ref[...] = _group_sum(dob.astype(F32) * ov, gs_ref[...])
        dga_ref[...] += _fold8(dga)
        gate_b = bcu_ref[:, 0:CW].astype(F32)
        cv = cv_ref[...]
        dconv, dgc = group_norm_bwd(dm[:, AW:D], gate_b * cv, gc_ref[...], gs_ref[...])
        dgc_ref[...] += _fold8(dgc)
        dcv_ref[...] = dconv * gate_b
        db_ref[...] = (dconv * cv).astype(BF16)

    return pl.pallas_call(
        body, name="mix_bwd", grid=(s // tm,),
        in_specs=[_rows(tm, D), _resident((D, D)), _rows(tm, AW), _rows(tm, CW), _rows(tm, 3 * CW),
                  _full((1, AW)), _full((1, CW)), _full((GS, GS)), ANY],
        out_specs=[_rows(tm, AW), _rows(tm, AW), _rows(tm, CW), _rows(tm, CW),
                   _full((SUBLANES, AW)), _full((SUBLANES, CW))],
        out_shape=[jax.ShapeDtypeStruct((s, AW), BF16), jax.ShapeDtypeStruct((s, AW), F32),
                   jax.ShapeDtypeStruct((s, CW), F32), jax.ShapeDtypeStruct((s, CW), BF16),
                   jax.ShapeDtypeStruct((SUBLANES, AW), F32), jax.ShapeDtypeStruct((SUBLANES, CW), F32)],
        compiler_params=_cparams(48, ("arbitrary",)),
    )(dy, w_out, o, cv, bcu, ga, gc, gsum, after)


def _conv_bwd(dcv, db, bcu, cw8, *, tm):
    s = dcv.shape[0]
    nt = s // tm

    def body(dcv_ref, nxt_ref, db_ref, bcu_ref, halo_ref, cw_ref, dbcu_ref, dw_ref):
        i = pl.program_id(0)

        @pl.when(i == 0)
        def _():
            dw_ref[...] = jnp.zeros_like(dw_ref)

        z, z1, z2 = _conv_taps(bcu_ref, halo_ref, i == 0, tm)
        d = dcv_ref[...]
        dw_ref[0] += _fold8(d * z2)
        dw_ref[1] += _fold8(d * z1)
        dw_ref[2] += _fold8(d * z)
        nx = jnp.where(i == nt - 1, 0.0, nxt_ref[...])
        row = lax.broadcasted_iota(jnp.int32, (tm, CW), 0)
        d1 = jnp.where(row == tm - 1, nx[0:1, :], pltpu.roll(d, tm - 1, axis=0))
        d2 = jnp.where(row == tm - 2, nx[0:1, :], jnp.where(row == tm - 1, nx[1:2, :], pltpu.roll(d, tm - 2, axis=0)))
        dz = cw_ref[2:3, :] * d + cw_ref[1:2, :] * d1 + cw_ref[0:1, :] * d2
        dbcu_ref[:, 0:CW] = db_ref[...]
        dbcu_ref[:, CW:2 * CW] = (dz * bcu_ref[:, 2 * CW:3 * CW].astype(F32)).astype(BF16)
        dbcu_ref[:, 2 * CW:3 * CW] = (dz * bcu_ref[:, CW:2 * CW].astype(F32)).astype(BF16)

    return pl.pallas_call(
        body, name="conv_bwd", grid=(nt,),
        in_specs=[_rows(tm, CW),
                  pl.BlockSpec((SUBLANES, CW), lambda i: (jnp.minimum((i + 1) * (tm // SUBLANES), s // SUBLANES - 1), 0)),
                  _rows(tm, CW), _rows(tm, 3 * CW), _halo_before(tm, 3 * CW), _full((SUBLANES, CW))],
        out_specs=[_rows(tm, 3 * CW), _full((3, SUBLANES, CW))],
        out_shape=[jax.ShapeDtypeStruct((s, 3 * CW), BF16), jax.ShapeDtypeStruct((3, SUBLANES, CW), F32)],
        compiler_params=_cparams(48, ("arbitrary",)),
    )(dcv, dcv, db, bcu, bcu, cw8)


def _attn_bwd(qp, kp, v, do, lse, dl, mk, *, t):
    s = qp.shape[0]
    nq = s // t

    def body(q_ref, k_ref, v_ref, do_ref, lse_ref, dl_ref, mk_ref, dq_ref, dk_ref, dv_ref, dkx_ref, dq_acc):
        pi = pl.program_id(1)

        @pl.when(pi == 0)
        def _():
            dq_acc[...] = jnp.zeros_like(dq_acc)

        row = lax.broadcasted_iota(jnp.int32, (t, t), 0)
        col = lax.broadcasted_iota(jnp.int32, (t, t), 1)
        lane = lax.broadcasted_iota(jnp.int32, (t, 128), 1)

        def head_step(hh, qi, carry, modes):
            off = pl.multiple_of(qi * t, t)
            rows = pl.ds(off, t)
            q = q_ref[rows, HP * hh:HP * (hh + 1)]
            qt = q.T
            lse_col = lse_ref[rows, DH * hh:DH * hh + 1]
            dl_col = dl_ref[rows, DH * hh:DH * hh + 1]
            do2 = do_ref[rows, :]
            dom = jnp.where(lane < DH, do2 if hh == 0 else pltpu.roll(do2, DH, axis=1), jnp.zeros((), BF16))
            new, dss = [], []
            for half, masked in enumerate(modes):
                if masked is None:
                    new.append(carry[half])
                    continue
                dk, dv, cs = carry[half]
                keys = slice(half * t, (half + 1) * t)
                m_col = mk_ref[half, rows, DH * hh:DH * hh + 1]
                scale = jnp.exp(m_col - lse_col)
                sc = lax.dot_general(q, k_ref[keys, HP * hh:HP * (hh + 1)], NT, preferred_element_type=F32) - m_col
                if masked:
                    sc = jnp.where(col <= row, sc, -1e30)
                pt = jnp.exp(sc).astype(BF16)
                dp = lax.dot_general(dom, v_ref[keys, HP * hh:HP * (hh + 1)], NT, preferred_element_type=F32)
                ds32 = (pt.astype(F32) * scale) * (dp - dl_col)
                ds = ds32.astype(BF16)
                cs = cs + _fold8(ds32)
                dv = dv + jnp.dot((dom.astype(F32) * scale).astype(BF16).T, pt, preferred_element_type=F32)
                dk = dk + jnp.dot(qt, ds, preferred_element_type=F32)
                new.append((dk, dv, cs))
                dss.append((half, ds))
            if len(dss) == 2:
                dq = jnp.dot(jnp.concatenate([dss[0][1], dss[1][1]], axis=1), k_ref[:, HP * hh:HP * (hh + 1)],
                             preferred_element_type=F32)
            else:
                half, ds = dss[0]
                dq = jnp.dot(ds, k_ref[half * t:(half + 1) * t, HP * hh:HP * (hh + 1)], preferred_element_type=F32)
            dq_acc[rows, HP * hh:HP * (hh + 1)] += dq
            return tuple(new)

        def step(qi, carry, modes):
            return tuple(head_step(hh, qi, carry[hh], modes) for hh in range(2))

        def two_heads(a0, a1):
            return jnp.where(lane < DH, a0, pltpu.roll(a1, DH, axis=1))

        def rows_to_lanes(a0, a1):
            return jnp.concatenate([a0, a1], axis=0).T

        zero = (jnp.zeros((HP, t), F32), jnp.zeros((128, t), F32), jnp.zeros((SUBLANES, t), F32))
        carry = step(2 * pi, ((zero, zero), (zero, zero)), (True, None))
        carry = step(2 * pi + 1, carry, (False, True))

        def pair(j, carry):
            qi = 2 * (pi + 1 + j)
            return step(qi + 1, step(qi, carry, (False, False)), (False, False))

        carry = lax.fori_loop(0, nq // 2 - 1 - pi, pair, carry)
        for half in range(2):
            keys = slice(half * t, (half + 1) * t)
            (dk0, dv0, cs0), (dk1, dv1, cs1) = carry[0][half], carry[1][half]
            dk_ref[keys, :] = rows_to_lanes(dk0[0:DH], dk1[0:DH]).astype(BF16)
            dv_ref[keys, :] = rows_to_lanes(dv0[0:DH], dv1[0:DH]).astype(BF16)
            total = lambda cs: jnp.broadcast_to(jnp.sum(cs, axis=0, keepdims=True), (DH, t))
            dkx_ref[keys, :] = rows_to_lanes(total(cs0), total(cs1))

        @pl.when(pi == nq // 2 - 1)
        def _():
            for c in range(s // t):
                rows = slice(c * t, (c + 1) * t)
                dq_ref[rows, :] = two_heads(dq_acc[rows, 0:HP], dq_acc[rows, HP:2 * HP]).astype(BF16)

    return pl.pallas_call(
        body, name="attn_bwd", grid=(H // 2, nq // 2),
        in_specs=[pl.BlockSpec((s, 2 * HP), lambda p, i: (0, p)),
                  pl.BlockSpec((2 * t, 2 * HP), lambda p, i: (i, p)),
                  pl.BlockSpec((2 * t, 2 * HP), lambda p, i: (i, p)),
                  pl.BlockSpec((s, 128), lambda p, i: (0, p)),
                  pl.BlockSpec((s, 128), lambda p, i: (0, p)),
                  pl.BlockSpec((s, 128), lambda p, i: (0, p)),
                  pl.BlockSpec((2, s, 128), lambda p, i: (i, 0, p))],
        out_specs=[pl.BlockSpec((s, 128), lambda p, i: (0, p)),
                   pl.BlockSpec((2 * t, 128), lambda p, i: (i, p)),
                   pl.BlockSpec((2 * t, 128), lambda p, i: (i, p)),
                   pl.BlockSpec((2 * t, 128), lambda p, i: (i, p))],
        out_shape=[jax.ShapeDtypeStruct((s, AW), BF16), jax.ShapeDtypeStruct((s, AW), BF16),
                   jax.ShapeDtypeStruct((s, AW), BF16), jax.ShapeDtypeStruct((s, AW), F32)],
        scratch_shapes=[pltpu.VMEM((s, 2 * HP), F32)],
        compiler_params=_cparams(56, ("arbitrary", "arbitrary")),
    )(qp, kp, v, do, lse, dl, mk)


def _forget_bwd(dkx, z, sel, *, tm):
    s = dkx.shape[0]
    nt = s // tm

    def body(dk_ref, z_ref, sel_ref, dfl_ref, dbf_ref, carry):
        @pl.when(pl.program_id(0) == 0)
        def _():
            carry[...] = jnp.zeros_like(carry)
            dbf_ref[...] = jnp.zeros_like(dbf_ref)

        dc = _split_dot(dk_ref[...], sel_ref[...])
        row = lax.broadcasted_iota(jnp.int32, (tm, tm), 0)
        col = lax.broadcasted_iota(jnp.int32, (tm, tm), 1)
        tri = (col >= row).astype(BF16)
        dlogf = _exact_dot01(tri, dc) + carry[0:1, :]
        carry[...] = jnp.broadcast_to(dlogf[0:1, :], carry.shape)
        dz = dlogf * (1.0 - jax.nn.sigmoid(z_ref[...]))
        dfl_ref[:, 0:128] = dz.astype(BF16)
        dfl_ref[:, 128:GW_TILE] = jnp.zeros((tm, GW_TILE - 128), BF16)
        dbf_ref[...] += _fold8(dz)

    rev = lambda i: (nt - 1 - i, 0)
    return pl.pallas_call(
        body, name="forget_bwd", grid=(nt,),
        in_specs=[pl.BlockSpec((tm, AW), rev), pl.BlockSpec((tm, 128), rev), _full((AW, 128))],
        out_specs=[pl.BlockSpec((tm, GW_TILE), rev), _full((SUBLANES, 128))],
        out_shape=[jax.ShapeDtypeStruct((s, GW_TILE), BF16), jax.ShapeDtypeStruct((SUBLANES, 128), F32)],
        scratch_shapes=[pltpu.VMEM((SUBLANES, 128), F32)],
        compiler_params=_cparams(48, ("arbitrary",)),
    )(dkx, z, sel)


def _in_proj_bwd(pieces, wp, x, g1, dx2, after, *, tm):
    s = x.shape[0]

    def body(q_ref, k_ref, v_ref, bcu_ref, f_ref, w_ref, x_ref, g_ref, dx2_ref, after_ref, dx_ref, dg_ref):
        @pl.when(pl.program_id(0) == 0)
        def _():
            dg_ref[...] = jnp.zeros_like(dg_ref)

        dh = None
        for ref, (lo, hi) in zip((q_ref, k_ref, v_ref, bcu_ref, f_ref), PIECES):
            part = lax.dot_general(ref[...], w_ref[:, lo:hi], NT, preferred_element_type=F32)
            dh = part if dh is None else dh + part
        _, n, r = _rms_fwd(x_ref[...], g_ref[...])
        dxn, dg = _rms_bwd(dh, n, r, g_ref[...])
        dx_ref[...] = dx2_ref[...] + dxn
        dg_ref[...] += _fold8(dg)

    return pl.pallas_call(
        body, name="in_proj_bwd", grid=(s // tm,),
        in_specs=[_rows(tm, hi - lo) for lo, hi in PIECES]
        + [_resident((D, WP)), _rows(tm, D), _full((1, D)), _rows(tm, D), ANY],
        out_specs=[_rows(tm, D), _full((SUBLANES, D))],
        out_shape=[jax.ShapeDtypeStruct((s, D), F32), jax.ShapeDtypeStruct((SUBLANES, D), F32)],
        compiler_params=_cparams(56, ("arbitrary",)),
    )(*pieces, wp, x, g1, dx2, after)


def _position():
    return lax.axis_index("x"), lax.axis_index("y"), lax.axis_index("c")


ANY = pl.BlockSpec(memory_space=pl.ANY)


def _all_gather(shards):
    n = len(shards)

    def body(*refs):
        x_refs, out_refs = refs[:n], refs[n:2 * n]
        send_sems, recv_sems, local_sems = refs[2 * n:]
        x, y, c = _position()
        me, sibling = (x, y, c), (x, y, 1 - c)
        chips = [(1 - x, y), (x, 1 - y), (1 - x, 1 - y)]

        def copy(a, k, block, to, own=False):
            slot = out_refs[a].at[4 * block[0] + 2 * block[1] + block[2]]
            return pltpu.make_async_remote_copy(
                src_ref=x_refs[a] if own else slot, dst_ref=slot,
                send_sem=send_sems.at[7 * a + k], recv_sem=recv_sems.at[7 * a + k], device_id=to, device_id_type=MESH_ID)

        mine = [pltpu.make_async_copy(x_refs[a], out_refs[a].at[4 * x + 2 * y + c], local_sems.at[a]) for a in range(n)]
        for cp in mine:
            cp.start()
        first = []
        for a in range(n):
            first.append(copy(a, 0, me, sibling, own=True))
            first += [copy(a, 1 + j, me, (*chip, c), own=True) for j, chip in enumerate(chips)]
        for cp in first:
            cp.start()
        passed = []
        for j, chip in enumerate(chips):
            for a in range(n):
                copy(a, 1 + j, (*chip, c), me).wait_recv()
                fwd = copy(a, 4 + j, (*chip, c), sibling)
                fwd.start()
                passed.append(fwd)
        for a in range(n):
            copy(a, 0, sibling, me).wait_recv()
            for j, chip in enumerate(chips):
                copy(a, 4 + j, (*chip, 1 - c), me).wait_recv()
        for cp in first + passed:
            cp.wait_send()
        for cp in mine:
            cp.wait()

    return pl.pallas_call(
        body, name="all_gather_weights",
        out_shape=[jax.ShapeDtypeStruct((NDEV,) + sh.shape, sh.dtype) for sh in shards],
        in_specs=[ANY] * n, out_specs=[ANY] * n,
        scratch_shapes=[pltpu.SemaphoreType.DMA((7 * n,)), pltpu.SemaphoreType.DMA((7 * n,)), pltpu.SemaphoreType.DMA((n,))],
    )(*shards)


def _pair_exchange(grads):
    n = len(grads)

    def body(*refs):
        g_refs, out_refs = refs[:n], refs[n:2 * n]
        send_sems, recv_sems = refs[2 * n:]
        x, y, c = _position()
        copies = [pltpu.make_async_remote_copy(
            src_ref=g_refs[a].at[:, pl.ds(1 - c, 1)], dst_ref=out_refs[a], send_sem=send_sems.at[a],
            recv_sem=recv_sems.at[a], device_id=(x, y, 1 - c), device_id_type=MESH_ID) for a in range(n)]
        for cp in copies:
            cp.start()
        for cp in copies:
            cp.wait()

    return pl.pallas_call(
        body, name="grad_pair_exchange",
        out_shape=[jax.ShapeDtypeStruct((4, 1) + g.shape[2:], g.dtype) for g in grads],
        in_specs=[ANY] * n, out_specs=[ANY] * n,
        scratch_shapes=[pltpu.SemaphoreType.DMA((n,)), pltpu.SemaphoreType.DMA((n,))],
    )(*grads)


def _pair_sum(g, got, idx, *, tr, name):
    r, c = g.shape[2:]

    def body(idx_ref, g_ref, got_ref, pb_ref, own_ref):
        p = g_ref[0, 0].astype(F32) + got_ref[0, 0].astype(F32)
        pb_ref[0] = p.astype(BF16)

        @pl.when(pl.program_id(1) == idx_ref[1])
        def _():
            own_ref[...] = p

    return pl.pallas_call(
        body, name=name,
        grid_spec=pltpu.PrefetchScalarGridSpec(
            num_scalar_prefetch=1, grid=(r // tr, 4),
            in_specs=[pl.BlockSpec((1, 1, tr, c), lambda i, j, idx: (j, idx[0], i, 0)),
                      pl.BlockSpec((1, 1, tr, c), lambda i, j, idx: (j, 0, i, 0))],
            out_specs=[pl.BlockSpec((1, tr, c), lambda i, j, idx: (j, i, 0)),
                       pl.BlockSpec((tr, c), lambda i, j, idx: (i, 0))]),
        out_shape=[jax.ShapeDtypeStruct((4, r, c), BF16), jax.ShapeDtypeStruct((r, c), F32)],
        compiler_params=_cparams(32, ("arbitrary", "arbitrary")),
    )(idx, g, got)


HBM = pl.BlockSpec(memory_space=pltpu.HBM)
SEM = pl.BlockSpec(memory_space=pltpu.SEMAPHORE)
DATAFLOW = pltpu.SideEffectType.DATAFLOW_SIDE_EFFECTING


PEERS = {"gather": NDEV - 1, "scatter": NDEV - 1, "chips": 3}


def _exchange_copies(src_refs, land_refs, send_sems, recv_sems, mode):
    x, y, c = _position()
    me, my_chip = 4 * x + 2 * y + c, 2 * x + y
    npeers = PEERS[mode]
    copies, own = [], []
    for a, (s_ref, l_ref) in enumerate(zip(src_refs, land_refs)):
        for k in range(npeers):
            if mode == "chips":
                px, py, pc = x ^ ((k + 1) >> 1), y ^ ((k + 1) & 1), c
                src, dst = s_ref.at[2 * px + py], l_ref.at[my_chip]
            else:
                px, py, pc = x ^ ((k + 1) >> 2), y ^ (((k + 1) >> 1) & 1), c ^ ((k + 1) & 1)
                src, dst = (s_ref.at[4 * px + 2 * py + pc] if mode == "scatter" else s_ref), l_ref.at[me]
            copies.append(pltpu.make_async_remote_copy(
                src_ref=src, dst_ref=dst, send_sem=send_sems.at[npeers * a + k], recv_sem=recv_sems.at[npeers * a + k],
                device_id=(px, py, pc), device_id_type=MESH_ID))
        slot = my_chip if mode == "chips" else me
        own.append(pltpu.make_async_copy(s_ref if mode == "gather" else s_ref.at[slot], l_ref.at[slot],
                                         send_sems.at[npeers * len(src_refs) + a]))
    return copies, own


def _exchange_start(srcs, lands, after, *, mode, name):
    n = len(srcs)
    nsem = PEERS[mode] * n

    def body(*refs):
        token = refs[-1]
        copies, own = _exchange_copies(refs[:n], refs[n:2 * n], refs[2 * n + 1], refs[2 * n + 2], mode)
        for cp in copies + own:
            cp.start()
        token[...] = jnp.zeros_like(token)

    arrays = list(srcs) + list(lands)
    outs = pl.pallas_call(
        body, name=name,
        out_shape=(pltpu.SemaphoreType.DMA((nsem + n,)), pltpu.SemaphoreType.DMA((nsem,)),
                   *[pltpu.HBM(a.shape, a.dtype) for a in arrays], jax.ShapeDtypeStruct((SUBLANES, LANES), F32)),
        in_specs=[HBM] * (2 * n) + [ANY],
        out_specs=(SEM, SEM, *[HBM] * (2 * n), pl.BlockSpec(memory_space=pltpu.VMEM)),
        input_output_aliases={i: 2 + i for i in range(2 * n)},
        compiler_params=pltpu.CompilerParams(has_side_effects=DATAFLOW),
    )(*[pltpu.with_memory_space_constraint(a, pltpu.HBM) for a in arrays], after)
    return outs[0], outs[1], outs[2:2 + n], outs[2 + n:2 + 2 * n], outs[-1]


def _exchange_wait(send_sems, recv_sems, srcs, lands, after, *, mode, name):
    n = len(srcs)

    def body(*refs):
        copies, own = _exchange_copies(refs[:n], refs[n:2 * n], refs[2 * n], refs[2 * n + 1], mode)
        for cp in copies:
            cp.wait_send()
            cp.wait_recv()
        for cp in own:
            cp.wait()

    arrays = list(srcs) + list(lands)
    outs = pl.pallas_call(
        body, name=name,
        out_shape=tuple(pltpu.HBM(a.shape, a.dtype) for a in arrays),
        in_specs=[HBM] * (2 * n) + [SEM, SEM, ANY],
        out_specs=tuple([HBM] * (2 * n)),
        input_output_aliases={i: i for i in range(2 * n)},
        compiler_params=pltpu.CompilerParams(has_side_effects=DATAFLOW),
    )(*arrays, send_sems, recv_sems, after)
    return outs[n:]


def _small_all_reduce(parts):
    def body(gmp_ref, gmo_ref, gfp_ref, gfo_ref, ga_ref, gc_ref, dw_ref, bf_ref, loss_ref,
             out_ref, buf, send_sems, recv_sems):
        x, y, c = _position()
        me = 4 * x + 2 * y + c

        def colsum(v):
            return jnp.sum(v, axis=0, keepdims=True)

        loss = jnp.sum(colsum(loss_ref[...]), axis=1, keepdims=True) * (0.5 / D)
        rows = [colsum(gmp_ref[...]), colsum(gmo_ref[...]), colsum(gfp_ref[...]), colsum(gfo_ref[...]),
                jnp.concatenate([colsum(ga_ref[...]), colsum(gc_ref[...])], axis=1),
                jnp.concatenate([colsum(dw_ref[0]), colsum(dw_ref[1])], axis=1),
                jnp.concatenate([colsum(dw_ref[2]), colsum(bf_ref[...]), jnp.broadcast_to(loss, (1, 128)),
                                 jnp.zeros((1, 256), F32)], axis=1),
                jnp.zeros((1, D), F32)]
        buf[me] = jnp.concatenate(rows, axis=0)
        copies = []
        for mm in range(1, NDEV):
            peer = (x ^ (mm >> 2), y ^ ((mm >> 1) & 1), c ^ (mm & 1))
            copies.append(pltpu.make_async_remote_copy(
                src_ref=buf.at[me], dst_ref=buf.at[me], send_sem=send_sems.at[mm - 1], recv_sem=recv_sems.at[mm - 1],
                device_id=peer, device_id_type=MESH_ID))
        for cp in copies:
            cp.start()
        for cp in copies:
            cp.wait_recv()
        for cp in copies:
            cp.wait_send()
        acc = buf[0]
        for d in range(1, NDEV):
            acc = acc + buf[d]
        out_ref[...] = acc

    vm = pl.BlockSpec(memory_space=pltpu.VMEM)
    return pl.pallas_call(
        body, name="small_all_reduce",
        out_shape=jax.ShapeDtypeStruct((SUBLANES, D), F32),
        in_specs=[vm] * len(parts), out_specs=vm,
        scratch_shapes=[pltpu.VMEM((NDEV, SUBLANES, D), F32), pltpu.SemaphoreType.DMA((7,)), pltpu.SemaphoreType.DMA((7,))],
    )(*parts)


def _adam_update(w, g, m, v):
    nm = ADAM_B1 * m + (1.0 - ADAM_B1) * g
    nv = ADAM_B2 * v + (1.0 - ADAM_B2) * (g * g)
    m_hat = nm / (1.0 - ADAM_B1 ** ADAM_STEP)
    v_hat = nv / (1.0 - ADAM_B2 ** ADAM_STEP)
    return -ADAM_LR * (m_hat / (jnp.sqrt(v_hat) + ADAM_EPS) + ADAM_WD * w), nm, nv


SMALL_SLOTS = {"g_mix_pre": (0, 0, D), "g_mix_post": (1, 0, D), "g_ffn_pre": (2, 0, D), "g_ffn_post": (3, 0, D),
               "g_attn_out": (4, 0, AW), "g_conv_out": (4, AW, CW), "b_forget": (6, CW, H)}
LOSS_LANE = CW + 128


def _small_adamw(small, conv_grad, params):
    names = list(params)
    n = len(names)

    def body(*refs):
        small_ref, cg_ref = refs[0], refs[1]
        ins, outs = refs[2:2 + 3 * n], refs[2 + 3 * n:]
        for i, name in enumerate(names):
            w_ref, m_ref, v_ref = ins[3 * i:3 * i + 3]
            g_ref, d_ref, nm_ref, nv_ref = outs[4 * i:4 * i + 4]
            if name == "conv_w":
                g = cg_ref[...]
            else:
                r, c0, width = SMALL_SLOTS[name]
                g = small_ref[r:r + 1, c0:c0 + width]
            g_ref[...] = g
            d_ref[...], nm_ref[...], nv_ref[...] = _adam_update(w_ref[...], g, m_ref[...], v_ref[...])
        outs[4 * n][...] = small_ref[6:7, LOSS_LANE:LOSS_LANE + 1]

    vm = pl.BlockSpec(memory_space=pltpu.VMEM)
    flat = [a for name in names for a in params[name]]
    outs = pl.pallas_call(
        body, name="adamw_small",
        in_specs=[vm] * (2 + 3 * n), out_specs=[vm] * (4 * n + 1),
        out_shape=[jax.ShapeDtypeStruct(params[name][0].shape, F32) for name in names for _ in range(4)]
        + [jax.ShapeDtypeStruct((1, 1), F32)],
    )(small, conv_grad, *flat)
    return {name: outs[4 * i:4 * i + 4] for i, name in enumerate(names)}, outs[4 * n].reshape(())


def _chip_sum_adamw(got, own, idx, wt, mt, vt, *, tr, name):
    cols, rows = wt.shape
    gcols = own.shape[1]

    def body(idx_ref, got_ref, own_ref, w_ref, m_ref, v_ref, g_ref, d_ref, nm_ref, nv_ref):
        g = jnp.zeros((tr, gcols), F32)
        for j in range(4):
            g = g + jnp.where(idx_ref[1] == j, own_ref[...], got_ref[j].astype(F32))
        g = g.T[:cols]
        g_ref[...] = g
        d_ref[...], nm_ref[...], nv_ref[...] = _adam_update(w_ref[...], g, m_ref[...], v_ref[...])

    spec = pl.BlockSpec((cols, tr), lambda i, idx: (0, i))
    gspec = pl.BlockSpec((tr, gcols), lambda i, idx: (i, 0))
    return pl.pallas_call(
        body, name=name,
        grid_spec=pltpu.PrefetchScalarGridSpec(
            num_scalar_prefetch=1, grid=(rows // tr,),
            in_specs=[pl.BlockSpec((4, tr, gcols), lambda i, idx: (0, i, 0)), gspec, spec, spec, spec],
            out_specs=[spec] * 4),
        out_shape=[jax.ShapeDtypeStruct((cols, rows), F32)] * 4,
        compiler_params=_cparams(32, ("arbitrary",)),
    )(idx, got, own, wt, mt, vt)


def _device_sum_adamw(land, w, m, v, *, tr, name):
    rows, cols = w.shape

    def body(land_ref, w_ref, m_ref, v_ref, g_ref, d_ref, nm_ref, nv_ref):
        g = land_ref[0].astype(F32)
        for dev in range(1, NDEV):
            g = g + land_ref[dev].astype(F32)
        g_ref[...] = g
        d_ref[...], nm_ref[...], nv_ref[...] = _adam_update(w_ref[...], g, m_ref[...], v_ref[...])

    spec = pl.BlockSpec((tr, cols), lambda i: (i, 0))
    return pl.pallas_call(
        body, name=name, grid=(rows // tr,),
        in_specs=[pl.BlockSpec((NDEV, tr, cols), lambda i: (0, i, 0)), spec, spec, spec],
        out_specs=[spec] * 4,
        out_shape=[jax.ShapeDtypeStruct((rows, cols), F32)] * 4,
        compiler_params=_cparams(32, ("arbitrary",)),
    )(land, w, m, v)


def _placement_constants():
    j = np.arange(128)[:, None]
    lane = np.arange(1024)[None, :]
    head, sub = lane // HP, lane % HP
    piece, jh = j // H, j % H
    valid = (j < 3 * H) & (jh == head)
    pq = np.where(valid & (sub == DH + piece), 1.0, 0.0).astype(BF16)
    pk = np.where(valid & (sub == DH + 3 + piece), -1.0, 0.0).astype(BF16)
    oq = np.where((sub >= DH + 3) & (sub < DH + 6), 1.0, 0.0).astype(np.float32)
    ok = np.where((sub >= DH) & (sub < DH + 3), 1.0, 0.0).astype(np.float32)
    r = np.arange(AW)[:, None]
    cc = np.arange(128)[None, :]
    sel = np.where((r % DH == 3) & (r // DH == cc), -1.0, 0.0).astype(BF16)
    gi = np.arange(GS)
    gsum = (gi[:, None] // DH == gi[None, :] // DH).astype(BF16)
    return tuple(jnp.asarray(c) for c in (pq, pk, oq, ok, sel, gsum))


def _local_step(xs, tgt, wp, late_weights, cw8, bfp, g_attn_out, g_conv_out,
                g_mix_pre, g_mix_post, g_ffn_pre, g_ffn_post, early_grads=None, last_grad=None):
    pq, pk, oq, ok, sel, gsum = _placement_constants()
    h1t, qp, kp, vv, bcu, zf = _in_proj(xs, g_mix_pre, wp, bfp, pq, pk, oq, ok, tm=512)
    o, lse, mk = _attn_fwd(qp, kp, vv, t=512)
    w_out_f, wgu, wd = late_weights(lse)
    merged, y, x2, cv, h2 = _mix_out(o, bcu, cw8, g_attn_out, g_conv_out, gsum, w_out_f, xs, g_mix_post, g_ffn_pre, tm=512)
    gate, up, act, dx3, dff, loss_p, dg_ffn_post = _ffn_fwd_loss(h2, wgu, wd, x2, tgt, g_ffn_post, tm=512)

    dgu, dx2, dy, dg_ffn_pre, dg_mix_post = _ffn_bwd(dff, wd, gate, up, wgu, x2, g_ffn_pre, dx3, y, g_mix_post, tm=256)
    dw_down = _grad_matmul(act, dff, ta=DFF // 2, tb=D, ts=4096, name="grad_w_down", vmem_mb=56)
    dw_gu = _grad_matmul(dgu, h2, ta=DFF // 2, tb=D, ts=4096, name="grad_w_gate_up", vmem_mb=56).reshape(NDEV, FB, D)
    dw_out = _grad_matmul(merged, dy, ta=1024, tb=1024, ts=2048, name="grad_w_out")
    token = early_grads(dw_out, dw_gu, dw_down) if early_grads is not None else dw_out
    do, dl, dcv, db, dg_attn, dg_conv = _mix_bwd(dy, w_out_f, o, cv, bcu, g_attn_out, g_conv_out, gsum, token, tm=512)
    dbcu, dtaps = _conv_bwd(dcv, db, bcu, cw8, tm=512)
    dqp, dkp, dv, dkx = _attn_bwd(qp, kp, vv, do, lse, dl, mk, t=512)
    dfl, dbf = _forget_bwd(dkx, zf, sel, tm=512)
    pieces = (dqp, dkp, dv, dbcu, dfl)
    dwp = _grad_w_in(h1t, pieces)
    token = last_grad(dwp) if last_grad is not None else dwp
    grad_x, dg_mix_pre = _in_proj_bwd(pieces, wp, xs, g_mix_pre, dx2, token, tm=512)
    return (grad_x, dwp, dw_out, dw_gu, dw_down, dg_mix_pre, dg_mix_post, dg_ffn_pre, dg_ffn_post, dg_attn, dg_conv,
            dtaps, dbf, loss_p)


BIG_TILES = {"w_in": 256, "w_out": 128, "w_gate_up": 176, "w_down": 176}


def kernel(x, w_in, b_forget, conv_w, g_attn_out, g_conv_out, w_out, g_mix_pre, g_mix_post, w_gate_up, w_down, g_ffn_pre, g_ffn_post, loss_target, m_w_in, m_b_forget, m_conv_w, m_g_attn_out, m_g_conv_out, m_w_out, m_g_mix_pre, m_g_mix_post, m_w_gate_up, m_w_down, m_g_ffn_pre, m_g_ffn_post, v_w_in, v_b_forget, v_conv_w, v_g_attn_out, v_g_conv_out, v_w_out, v_g_mix_pre, v_g_mix_post, v_w_gate_up, v_w_down, v_g_ffn_pre, v_g_ffn_post):
    xc, yc, cc = _position()
    my_chip = 2 * xc + yc
    me = 2 * my_chip + cc
    idx = jnp.stack([cc, my_chip]).astype(jnp.int32)
    tables = _in_layout_tables()

    w_in_b = w_in[0].astype(BF16)
    g_in, g_last, g_taps = _all_gather([w_in_b[:, :IN_MAIN], w_in_b[:, IN_MAIN].reshape(SUBLANES, LANES), conv_w[0]])
    last_cols = jnp.pad(g_last.reshape(NDEV, D).T.astype(F32), ((0, 0), (0, LANES - NDEV)))
    wp = _assemble_w_in(g_in, last_cols, tables, tr=256)
    cw8 = jnp.pad(g_taps.transpose(1, 0, 2).reshape(3, CW), ((0, SUBLANES - 3), (0, 0)))

    late = [w_out[0].astype(BF16), w_gate_up[0].T.astype(BF16), w_down[0].astype(BF16)]
    ssem, rsem, late_thru, land_thru, token = _exchange_start(
        late, [lax.empty((NDEV,) + s.shape, s.dtype) for s in late], g_in, mode="gather",
        name="gather_late_start")
    bfp = jnp.pad(b_forget, ((0, 0), (0, 128 - H))) + token[0:1, :]

    def late_weights(after):
        l_out, l_gu, l_down = _exchange_wait(ssem, rsem, late_thru, land_thru, after, mode="gather", name="gather_late_wait")
        return l_out.reshape(D, D), l_gu.reshape(2, DFF, D), l_down.reshape(DFF, D)

    early = {}

    def early_grads(dw_out, dw_gu, dw_down):
        srcs = [dw_out.reshape(NDEV, D // NDEV, D), dw_gu, dw_down.reshape(NDEV, DFF // NDEV, D)]
        lands = [lax.empty(s.shape, s.dtype) for s in srcs]
        early["handles"] = _exchange_start(srcs, lands, dw_out, mode="scatter", name="scatter_early_start")
        return early["handles"][4]

    last = {}

    def last_grad(dwp):
        g_w_in = _disassemble_w_in(dwp, tables, tr=256).reshape(4, 2, D, IN_PAD)
        (from_sibling,) = _pair_exchange([g_w_in])
        pair_b, last["own"] = _pair_sum(g_w_in, from_sibling, idx, tr=BIG_TILES["w_in"], name="grad_pair_sum_w_in")
        last["handles"] = _exchange_start([pair_b], [lax.empty(pair_b.shape, pair_b.dtype)], last["own"], mode="chips",
                                          name="chips_w_in_start")
        return last["handles"][4]

    (grad_x, dwp, dw_out, dw_gu, dw_down, dg_mix_pre, dg_mix_post, dg_ffn_pre, dg_ffn_post, dg_attn, dg_conv,
     dtaps, dbf, loss_p) = _local_step(x[0], loss_target[0], wp, late_weights, cw8, bfp, g_attn_out, g_conv_out,
                                        g_mix_pre, g_mix_post, g_ffn_pre, g_ffn_post, early_grads, last_grad)

    e_ssem, e_rsem, e_srcs, e_lands, _ = early["handles"]
    land_out, land_gu, land_down = _exchange_wait(e_ssem, e_rsem, e_srcs, e_lands, dg_mix_pre, mode="scatter",
                                                  name="scatter_early_wait")
    res = {}
    big = {"w_out": (land_out, w_out[0], m_w_out[0], v_w_out[0]),
           "w_gate_up": (land_gu, w_gate_up[0].T, m_w_gate_up[0].T, v_w_gate_up[0].T),
           "w_down": (land_down, w_down[0], m_w_down[0], v_w_down[0])}
    for name, (land, w, m, v) in big.items():
        outs = _device_sum_adamw(land, w, m, v, tr=BIG_TILES[name], name="adamw_" + name)
        res[name] = [(o.T if name == "w_gate_up" else o)[None] for o in outs]
    c_ssem, c_rsem, c_srcs, c_lands, _ = last["handles"]
    after = sum(res[n][1][0, :SUBLANES, :LANES] for n in big)
    (from_chips,) = _exchange_wait(c_ssem, c_rsem, c_srcs, c_lands, after, mode="chips", name="chips_w_in_wait")
    outs = _chip_sum_adamw(from_chips, last["own"], idx, w_in[0].T, m_w_in[0].T, v_w_in[0].T,
                           tr=BIG_TILES["w_in"], name="adamw_w_in")
    res["w_in"] = [o.T[None] for o in outs]

    small = _small_all_reduce([dg_mix_pre, dg_mix_post, dg_ffn_pre, dg_ffn_post, dg_attn, dg_conv, dtaps, dbf, loss_p])
    taps_full = jnp.concatenate([small[5:6, :CW], small[5:6, CW:], small[6:7, :CW]], axis=0)
    taps_first = lambda a: a.transpose(1, 0, 2)
    smalls = {"b_forget": (b_forget, m_b_forget, v_b_forget),
              "conv_w": (taps_first(conv_w), taps_first(m_conv_w), taps_first(v_conv_w)),
              "g_attn_out": (g_attn_out, m_g_attn_out, v_g_attn_out), "g_conv_out": (g_conv_out, m_g_conv_out, v_g_conv_out),
              "g_mix_pre": (g_mix_pre, m_g_mix_pre, v_g_mix_pre), "g_mix_post": (g_mix_post, m_g_mix_post, v_g_mix_post),
              "g_ffn_pre": (g_ffn_pre, m_g_ffn_pre, v_g_ffn_pre), "g_ffn_post": (g_ffn_post, m_g_ffn_post, v_g_ffn_post)}
    own_taps = lax.dynamic_slice(taps_full, (0, me * 64), (3, 64))[:, None, :]
    small_res, loss = _small_adamw(small, own_taps, smalls)
    for name, outs in small_res.items():
        res[name] = [taps_first(o) for o in outs] if name == "conv_w" else list(outs)

    order = ["w_in", "b_forget", "conv_w", "g_attn_out", "g_conv_out", "w_out", "g_mix_pre", "g_mix_post",
             "w_gate_up", "w_down", "g_ffn_pre", "g_ffn_post"]
    outs = [loss, grad_x[None]]
    for k in range(4):
        outs += [res[n][k] for n in order]
    return tuple(outs)
```

```python
import functools

import numpy as np

import jax
import jax.numpy as jnp
from jax import lax
from jax.experimental import pallas as pl
from jax.experimental.pallas import tpu as pltpu

F32 = jnp.float32
BF16 = jnp.bfloat16
MESH_ID = pl.DeviceIdType.MESH

D = 1024
H = 8
DH = 64
AW = 512
CW = 512
DFF = 2816
FB = DFF // 4
FF_CHUNKS = ((0, 768), (768, 768), (1536, 768), (2304, 512))
HP = 128
OFF_Q, OFF_K, OFF_V, OFF_BCU, OFF_F = 0, 512, 1024, 1536, 3072
WP = OFF_F + 128
PIECES = ((OFF_Q, OFF_K), (OFF_K, OFF_V), (OFF_V, OFF_BCU), (OFF_BCU, OFF_F), (OFF_F, WP))
EPS = 1e-6
NDEV = 8
LANES = 128
SUBLANES = 8
IN_COLS = 385
IN_PAD = 512
IN_MAIN = 384
WIN = 640
ADAM_LR, ADAM_B1, ADAM_B2, ADAM_EPS, ADAM_WD, ADAM_STEP = 0.001, 0.9, 0.999, 1e-08, 0.01, 10

NT = (((1,), (1,)), ((), ()))
TN = (((0,), (0,)), ((), ()))


def _cparams(vmem_mb=None, sem=None):
    kw = {}
    if vmem_mb is not None:
        kw["vmem_limit_bytes"] = vmem_mb << 20
    if sem is not None:
        kw["dimension_semantics"] = sem
    return pltpu.CompilerParams(**kw)


def _full(shape):
    return pl.BlockSpec(shape, lambda *_: (0,) * len(shape))


def _resident(shape):
    return pl.BlockSpec(shape, lambda *_: (0,) * len(shape), pipeline_mode=pl.Buffered(1))


def _rows(tm, width):
    return pl.BlockSpec((tm, width), lambda i: (i, 0))


def _fold8(v):
    r, w = v.shape
    return jnp.sum(v.reshape(r // SUBLANES, SUBLANES, w), axis=0)


def _split_dot(v, m01):
    hi = v.astype(BF16)
    lo = (v - hi.astype(F32)).astype(BF16)
    return (jnp.dot(hi, m01, preferred_element_type=F32)
            + jnp.dot(lo, m01, preferred_element_type=F32))


GS = 256


def _group_sum(v, g01):
    parts = [_split_dot(v[:, c:c + GS], g01) for c in range(0, v.shape[1], GS)]
    return parts[0] if len(parts) == 1 else jnp.concatenate(parts, axis=1)


def _exact_dot01(m01, v):
    p1 = v.astype(BF16)
    r1 = v - p1.astype(F32)
    p2 = r1.astype(BF16)
    p3 = (r1 - p2.astype(F32)).astype(BF16)
    return (jnp.dot(m01, p1, preferred_element_type=F32) + jnp.dot(m01, p2, preferred_element_type=F32)
            + jnp.dot(m01, p3, preferred_element_type=F32))


def _rms_fwd(v, g):
    r = lax.rsqrt(jnp.mean(v * v, axis=-1, keepdims=True) + EPS)
    n = v * r
    return n * g, n, r


def _rms_bwd(do, n, r, g):
    dn = do * g
    return r * (dn - n * jnp.mean(dn * n, axis=-1, keepdims=True)), do * n


def _padded_column(n):
    if n < AW:
        return OFF_Q + n, 0.125
    if n < 3 * AW:
        return n, 1.0
    if n < 3 * AW + H:
        return OFF_F + n - 3 * AW, 1.0
    return OFF_BCU + n - 3 * AW - H, 1.0


def _in_layout_tables():
    dest = -np.ones((IN_PAD, LANES), np.int32)
    dest_f = -np.ones((IN_PAD, LANES), np.int32)
    scale = np.zeros((IN_PAD, LANES), np.float32)
    starts = []
    for k in range(NDEV):
        cols = [_padded_column(IN_COLS * k + j) for j in range(IN_COLS)]
        main = [c for c, _ in cols if c < OFF_F]
        ws = min((min(main) // LANES) * LANES, OFF_F - WIN)
        assert ws <= min(main) and max(main) < ws + WIN
        starts.append(ws)
        for j, (c, sc) in enumerate(cols):
            scale[j, k] = sc
            if c < OFF_F:
                dest[j, k] = c - ws
            else:
                dest_f[j, k] = c - OFF_F
    f_shards = tuple(k for k in range(NDEV) if (dest_f[:, k] >= 0).any())
    return tuple(starts), f_shards, jnp.asarray(dest), jnp.asarray(dest_f), jnp.asarray(scale)


def _perm(dest_ref, scale_ref, k, width, rows=IN_PAD):
    lane = lax.broadcasted_iota(jnp.int32, (rows, width), 1)
    return jnp.where(dest_ref[0:rows, k:k + 1] == lane, scale_ref[0:rows, k:k + 1], 0.0).astype(BF16)


def _assemble_w_in(blocks, last_cols, tables, *, tr):
    starts, f_shards, dest, dest_f, scale = tables
    last = [_padded_column(IN_COLS * k + IN_MAIN) for k in range(NDEV)]
    f_main = [any(_padded_column(IN_COLS * k + j)[0] >= OFF_F for j in range(IN_MAIN)) for k in range(NDEV)]
    assert IN_COLS == IN_MAIN + 1

    def body(b_ref, c_ref, dest_ref, destf_ref, scale_ref, o_ref):
        o_ref[...] = jnp.zeros_like(o_ref)
        lane = lax.broadcasted_iota(jnp.int32, (tr, LANES), 1)
        for k in range(NDEV):
            b = b_ref[k]
            ws = starts[k]
            part = jnp.dot(b, _perm(dest_ref, scale_ref, k, WIN, IN_MAIN), preferred_element_type=F32)
            o_ref[:, ws:ws + WIN] += part.astype(BF16)
            if f_main[k]:
                part = jnp.dot(b, _perm(destf_ref, scale_ref, k, 128, IN_MAIN), preferred_element_type=F32)
                o_ref[:, OFF_F:WP] += part.astype(BF16)
            col, sc = last[k]
            tile = (col // LANES) * LANES
            o_ref[:, tile:tile + LANES] += jnp.where(lane == col - tile, c_ref[:, k:k + 1] * sc, 0.0).astype(BF16)

    tab = _full((IN_PAD, LANES))
    return pl.pallas_call(
        body, name="assemble_w_in", grid=(D // tr,),
        in_specs=[pl.BlockSpec((NDEV, tr, IN_MAIN), lambda i: (0, i, 0)), _rows(tr, LANES), tab, tab, tab],
        out_specs=_rows(tr, WP),
        out_shape=jax.ShapeDtypeStruct((D, WP), BF16),
        compiler_params=_cparams(48, ("arbitrary",)),
    )(blocks, last_cols, dest, dest_f, scale)


def _disassemble_w_in(dwp, tables, *, tr):
    starts, f_shards, dest, dest_f, scale = tables
    width = dwp.shape[1]

    def body(g_ref, dest_ref, destf_ref, scale_ref, o_ref):
        for k in range(NDEV):
            ws = starts[k]
            acc = lax.dot_general(g_ref[:, ws:ws + WIN], _perm(dest_ref, scale_ref, k, WIN), NT, preferred_element_type=F32)
            if k in f_shards:
                acc = acc + lax.dot_general(g_ref[:, OFF_F:WP], _perm(destf_ref, scale_ref, k, 128), NT,
                                            preferred_element_type=F32)
            o_ref[k] = acc.astype(BF16)

    tab = _full((IN_PAD, LANES))
    return pl.pallas_call(
        body, name="disassemble_w_in", grid=(D // tr,),
        in_specs=[_rows(tr, width), tab, tab, tab],
        out_specs=pl.BlockSpec((NDEV, tr, IN_PAD), lambda i: (0, i, 0)),
        out_shape=jax.ShapeDtypeStruct((NDEV, D, IN_PAD), BF16),
        compiler_params=_cparams(48, ("arbitrary",)),
    )(dwp, dest, dest_f, scale)


def _in_proj(x, g1, wp, bfp, pq, pk, oq, ok, *, tm):
    s = x.shape[0]

    def body(x_ref, g_ref, w_ref, bf_ref, pq_ref, pk_ref, oq_ref, ok_ref,
             ht_ref, qp_ref, kp_ref, v_ref, bcu_ref, z_ref, carry):
        @pl.when(pl.program_id(0) == 0)
        def _():
            carry[...] = jnp.zeros_like(carry)

        h = _rms_fwd(x_ref[...], g_ref[...])[0].astype(BF16)
        ht_ref[...] = h.T
        z = jnp.dot(h, w_ref[:, OFF_F:WP], preferred_element_type=F32) + bf_ref[...]
        z_ref[...] = z
        lane = lax.broadcasted_iota(jnp.int32, (tm, 128), 1)
        logf = jnp.where(lane < H, jnp.minimum(z, 0.0) - jnp.log(1.0 + jnp.exp(-jnp.abs(z))), 0.0)
        row = lax.broadcasted_iota(jnp.int32, (tm, tm), 0)
        col = lax.broadcasted_iota(jnp.int32, (tm, tm), 1)
        tri = (col <= row).astype(BF16)
        c = _exact_dot01(tri, logf) + carry[0:1, :]
        carry[...] = jnp.broadcast_to(c[tm - 1:tm, :], carry.shape)
        c1 = c.astype(BF16).astype(F32)
        r1 = c - c1
        c2 = r1.astype(BF16).astype(F32)
        c3 = (r1 - c2).astype(BF16).astype(F32)
        zc = (c1 + pltpu.roll(c2, 8, axis=1) + pltpu.roll(c3, 16, axis=1)).astype(BF16)

        def pad_heads(v):
            blocks = []
            for pair in range(H // 2):
                two = v[:, 128 * pair:128 * (pair + 1)]
                blocks.append(jnp.where(lane < DH, two, 0.0))
                blocks.append(jnp.where(lane < DH, pltpu.roll(two, DH, axis=1), 0.0))
            return jnp.concatenate(blocks, axis=1)

        q = jnp.dot(h, w_ref[:, OFF_Q:OFF_K], preferred_element_type=F32)
        qp_ref[...] = (pad_heads(q) + jnp.dot(zc, pq_ref[...], preferred_element_type=F32) + oq_ref[...]).astype(BF16)
        k = jnp.dot(h, w_ref[:, OFF_K:OFF_V], preferred_element_type=F32)
        kp_ref[...] = (pad_heads(k) + jnp.dot(zc, pk_ref[...], preferred_element_type=F32) + ok_ref[...]).astype(BF16)
        v = pad_heads(jnp.dot(h, w_ref[:, OFF_V:OFF_BCU], preferred_element_type=F32))
        ones_lane = lax.broadcasted_iota(jnp.int32, (tm, H * HP), 1) % HP == DH
        v_ref[...] = jnp.where(ones_lane, 1.0, v).astype(BF16)
        bcu_ref[...] = jnp.dot(h, w_ref[:, OFF_BCU:OFF_F], preferred_element_type=F32).astype(BF16)

    return pl.pallas_call(
        body, name="in_proj", grid=(s // tm,),
        in_specs=[_rows(tm, D), _full((1, D)), _resident((D, WP)), _full((1, 128)),
                  _full((128, 1024)), _full((128, 1024)), _full((1, 1024)), _full((1, 1024))],
        out_specs=[pl.BlockSpec((D, tm), lambda i: (0, i)), _rows(tm, 1024), _rows(tm, 1024), _rows(tm, 1024),
                   _rows(tm, 3 * CW), _rows(tm, 128)],
        out_shape=[jax.ShapeDtypeStruct((D, s), BF16), jax.ShapeDtypeStruct((s, 1024), BF16),
                   jax.ShapeDtypeStruct((s, 1024), BF16), jax.ShapeDtypeStruct((s, 1024), BF16),
                   jax.ShapeDtypeStruct((s, 3 * CW), BF16), jax.ShapeDtypeStruct((s, 128), F32)],
        scratch_shapes=[pltpu.VMEM((SUBLANES, 128), F32)],
        compiler_params=_cparams(56, ("arbitrary",)),
    )(x, g1, wp, bfp, pq, pk, oq, ok)


def _attn_fwd(qp, kp, v, *, t):
    s = qp.shape[0]
    nq = s // t

    def body(q_ref, k_ref, v_ref, o_ref, lse_ref, mk_ref):
        pi = pl.program_id(1)
        row = lax.broadcasted_iota(jnp.int32, (t, t), 0)
        col = lax.broadcasted_iota(jnp.int32, (t, t), 1)
        lane = lax.broadcasted_iota(jnp.int32, (t, 128), 1)

        def head_step(hh, rows, ki, carry, masked):
            m, acc = carry
            off = pl.multiple_of(ki * t, t)
            q = q_ref[rows, HP * hh:HP * (hh + 1)]
            k = k_ref[pl.ds(off, t), HP * hh:HP * (hh + 1)]
            sc = lax.dot_general(q, k, NT, preferred_element_type=F32)
            if masked:
                sc = jnp.where(col <= row, sc, -1e30)
            mn = jnp.maximum(m, jnp.max(sc, axis=-1, keepdims=True))
            p = jnp.exp(sc - mn).astype(BF16)
            acc = jnp.exp(m - mn) * acc + jnp.dot(p, v_ref[pl.ds(off, t), HP * hh:HP * (hh + 1)],
                                                  preferred_element_type=F32)
            return mn, acc

        def step(rows, ki, carry, masked):
            new = tuple(head_step(hh, rows, ki, carry[hh], masked) for hh in range(2))
            mk_ref[ki, rows] = jnp.where(lane < DH, jnp.broadcast_to(new[0][0], (t, 128)),
                                         jnp.broadcast_to(new[1][0], (t, 128)))
            return new

        init = (jnp.full((t, 1), -1e30, F32), jnp.zeros((t, 128), F32))
        top, bottom = slice(0, t), slice(t, 2 * t)

        def quad(j, carry):
            c0, c1 = carry
            c0 = step(top, 2 * j, c0, False)
            c1 = step(bottom, 2 * j, c1, False)
            c0 = step(top, 2 * j + 1, c0, False)
            c1 = step(bottom, 2 * j + 1, c1, False)
            return c0, c1

        c0, c1 = lax.fori_loop(0, pi, quad, ((init, init), (init, init)))
        f0 = step(top, 2 * pi, c0, True)
        c1 = step(bottom, 2 * pi, c1, False)
        f1 = step(bottom, 2 * pi + 1, c1, True)
        for rows, ((m0, acc0), (m1, acc1)) in ((top, f0), (bottom, f1)):
            l0, l1 = acc0[:, DH:DH + 1], acc1[:, DH:DH + 1]
            o_ref[rows, :] = jnp.where(lane < DH, acc0 / l0, pltpu.roll(acc1 / l1, DH, axis=1))
            lse_ref[rows, :] = jnp.where(lane < DH, jnp.broadcast_to(m0 + jnp.log(l0), (t, 128)),
                                         jnp.broadcast_to(m1 + jnp.log(l1), (t, 128)))

    return pl.pallas_call(
        body, name="attn_fwd", grid=(H // 2, nq // 2),
        in_specs=[pl.BlockSpec((2 * t, 2 * HP), lambda p, i: (i, p)),
                  pl.BlockSpec((s, 2 * HP), lambda p, i: (0, p)),
                  pl.BlockSpec((s, 2 * HP), lambda p, i: (0, p))],
        out_specs=[pl.BlockSpec((2 * t, 128), lambda p, i: (i, p)), pl.BlockSpec((2 * t, 128), lambda p, i: (i, p)),
                   pl.BlockSpec((nq, 2 * t, 128), lambda p, i: (0, i, p))],
        out_shape=[jax.ShapeDtypeStruct((s, AW), F32), jax.ShapeDtypeStruct((s, AW), F32),
                   jax.ShapeDtypeStruct((nq, s, AW), F32)],
        compiler_params=_cparams(48, ("arbitrary", "arbitrary")),
    )(qp, kp, v)


HALO = 16


def _conv_taps(bcu_ref, halo_ref, first, tm):
    z = bcu_ref[:, CW:2 * CW].astype(F32) * bcu_ref[:, 2 * CW:3 * CW].astype(F32)
    zh = jnp.where(first, 0.0, halo_ref[:, CW:2 * CW].astype(F32) * halo_ref[:, 2 * CW:3 * CW].astype(F32))
    row = lax.broadcasted_iota(jnp.int32, (tm, CW), 0)
    last, before = zh[HALO - 1:HALO, :], zh[HALO - 2:HALO - 1, :]
    z1 = jnp.where(row == 0, last, pltpu.roll(z, 1, axis=0))
    z2 = jnp.where(row == 0, before, jnp.where(row == 1, last, pltpu.roll(z, 2, axis=0)))
    return z, z1, z2


def _halo_before(tm, width):
    return pl.BlockSpec((HALO, width), lambda i: (jnp.maximum(i * (tm // HALO) - 1, 0), 0))


def _mix_out(o, bcu, cw8, ga, gc, gsum, w_out, x, g_post, g_ffn_pre, *, tm):
    s = x.shape[0]

    def body(o_ref, bcu_ref, halo_ref, cw_ref, ga_ref, gc_ref, gs_ref, w_ref, x_ref, g_ref, gf_ref,
             merged_ref, y_ref, x2_ref, cv_ref, h2_ref):
        z, z1, z2 = _conv_taps(bcu_ref, halo_ref, pl.program_id(0) == 0, tm)
        cv = cw_ref[0:1, :] * z2 + cw_ref[1:2, :] * z1 + cw_ref[2:3, :] * z
        cv_ref[...] = cv
        conv = bcu_ref[:, 0:CW].astype(F32) * cv
        ov = o_ref[...]
        ra = lax.rsqrt(_group_sum(ov * ov, gs_ref[...]) * (1.0 / DH) + EPS)
        rc = lax.rsqrt(_group_sum(conv * conv, gs_ref[...]) * (1.0 / DH) + EPS)
        merged = jnp.concatenate([ov * ra * ga_ref[...], conv * rc * gc_ref[...]], axis=1).astype(BF16)
        merged_ref[...] = merged
        y = jnp.dot(merged, w_ref[...], preferred_element_type=F32)
        y_ref[...] = y
        x2 = x_ref[...] + _rms_fwd(y, g_ref[...])[0]
        x2_ref[...] = x2
        h2_ref[...] = _rms_fwd(x2, gf_ref[...])[0].astype(BF16)

    return pl.pallas_call(
        body, name="mix_out", grid=(s // tm,),
        in_specs=[_rows(tm, AW), _rows(tm, 3 * CW), _halo_before(tm, 3 * CW), _full((SUBLANES, CW)),
                  _full((1, AW)), _full((1, CW)), _full((GS, GS)), _resident((D, D)), _rows(tm, D), _full((1, D)),
                  _full((1, D))],
        out_specs=[_rows(tm, D), _rows(tm, D), _rows(tm, D), _rows(tm, CW), _rows(tm, D)],
        out_shape=[jax.ShapeDtypeStruct((s, D), BF16), jax.ShapeDtypeStruct((s, D), F32),
                   jax.ShapeDtypeStruct((s, D), F32), jax.ShapeDtypeStruct((s, CW), F32),
                   jax.ShapeDtypeStruct((s, D), BF16)],
        compiler_params=_cparams(48, ("arbitrary",)),
    )(o, bcu, bcu, cw8, ga, gc, gsum, w_out, x, g_post, g_ffn_pre)


def _ffn_fwd_loss(h2, wgu, wd, x2, target, g_post, *, tm):
    s = x2.shape[0]

    def body(h_ref, w_ref, wd_ref, x2_ref, t_ref, g_ref,
             gate_ref, up_ref, a_ref, dx3_ref, dff_ref, loss_ref, dg_ref):
        @pl.when(pl.program_id(0) == 0)
        def _():
            loss_ref[...] = jnp.zeros_like(loss_ref)
            dg_ref[...] = jnp.zeros_like(dg_ref)

        h = h_ref[...]
        ff = None
        for c0, n in FF_CHUNKS:
            cols = slice(c0, c0 + n)
            gate = lax.dot_general(h, w_ref[0, cols, :], NT, preferred_element_type=F32)
            up = lax.dot_general(h, w_ref[1, cols, :], NT, preferred_element_type=F32)
            gate_ref[:, cols] = gate.astype(BF16)
            up_ref[:, cols] = up.astype(BF16)
            act = (gate * jax.nn.sigmoid(gate) * up).astype(BF16)
            a_ref[:, cols] = act
            part = jnp.dot(act, wd_ref[cols, :], preferred_element_type=F32)
            ff = part if ff is None else ff + part
        out, n, r = _rms_fwd(ff, g_ref[...])
        e = x2_ref[...] + out - t_ref[...]
        loss_ref[...] += _fold8(e * e)
        dx3 = e * (1.0 / D)
        dx3_ref[...] = dx3
        dff, dg = _rms_bwd(dx3, n, r, g_ref[...])
        dff_ref[...] = dff.astype(BF16)
        dg_ref[...] += _fold8(dg)

    wide = _rows(tm, DFF)
    return pl.pallas_call(
        body, name="ffn_fwd_loss", grid=(s // tm,),
        in_specs=[_rows(tm, D), _resident((2, DFF, D)), _resident((DFF, D)), _rows(tm, D), _rows(tm, D), _full((1, D))],
        out_specs=[wide, wide, wide, _rows(tm, D), _rows(tm, D), _full((SUBLANES, D)), _full((SUBLANES, D))],
        out_shape=[jax.ShapeDtypeStruct((s, DFF), BF16)] * 3
        + [jax.ShapeDtypeStruct((s, D), F32), jax.ShapeDtypeStruct((s, D), BF16),
           jax.ShapeDtypeStruct((SUBLANES, D), F32), jax.ShapeDtypeStruct((SUBLANES, D), F32)],
        compiler_params=_cparams(56, ("arbitrary",)),
    )(h2, wgu, wd, x2, target, g_post)


def _ffn_bwd(dff, wd, gate, up, wgu, x2, g_pre, dx3, y, g_post, *, tm):
    s = x2.shape[0]

    def body(dff_ref, wd_ref, gate_ref, up_ref, w_ref, x2_ref, gpre_ref, dx3_ref, y_ref, gpost_ref,
             dgu_ref, dx2_ref, dy_ref, dgpre_ref, dgpost_ref):
        @pl.when(pl.program_id(0) == 0)
        def _():
            dgpre_ref[...] = jnp.zeros_like(dgpre_ref)
            dgpost_ref[...] = jnp.zeros_like(dgpost_ref)

        dff = dff_ref[...]
        dh2 = None
        for c0, n in FF_CHUNKS:
            cols = slice(c0, c0 + n)
            da = lax.dot_general(dff, wd_ref[cols, :], NT, preferred_element_type=F32)
            g = gate_ref[:, cols].astype(F32)
            sg = jax.nn.sigmoid(g)
            dgate = (da * up_ref[:, cols].astype(F32) * (sg * (1.0 + g * (1.0 - sg)))).astype(BF16)
            dup = (da * (g * sg)).astype(BF16)
            dgu_ref[:, cols] = dgate
            dgu_ref[:, DFF + c0:DFF + c0 + n] = dup
            part = (jnp.dot(dgate, w_ref[0, cols, :], preferred_element_type=F32)
                    + jnp.dot(dup, w_ref[1, cols, :], preferred_element_type=F32))
            dh2 = part if dh2 is None else dh2 + part
        _, n2, r2 = _rms_fwd(x2_ref[...], gpre_ref[...])
        dxn, dg = _rms_bwd(dh2, n2, r2, gpre_ref[...])
        dgpre_ref[...] += _fold8(dg)
        dx2 = dx3_ref[...] + dxn
        dx2_ref[...] = dx2
        _, ny, ry = _rms_fwd(y_ref[...], gpost_ref[...])
        dy, dg2 = _rms_bwd(dx2, ny, ry, gpost_ref[...])
        dy_ref[...] = dy.astype(BF16)
        dgpost_ref[...] += _fold8(dg2)

    wide = _rows(tm, DFF)
    return pl.pallas_call(
        body, name="ffn_bwd", grid=(s // tm,),
        in_specs=[_rows(tm, D), _resident((DFF, D)), wide, wide, _resident((2, DFF, D)), _rows(tm, D), _full((1, D)),
                  _rows(tm, D), _rows(tm, D), _full((1, D))],
        out_specs=[_rows(tm, 2 * DFF), _rows(tm, D), _rows(tm, D),
                   _full((SUBLANES, D)), _full((SUBLANES, D))],
        out_shape=[jax.ShapeDtypeStruct((s, 2 * DFF), BF16), jax.ShapeDtypeStruct((s, D), F32),
                   jax.ShapeDtypeStruct((s, D), BF16), jax.ShapeDtypeStruct((SUBLANES, D), F32),
                   jax.ShapeDtypeStruct((SUBLANES, D), F32)],
        compiler_params=_cparams(56, ("arbitrary",)),
    )(dff, wd, gate, up, wgu, x2, g_pre, dx3, y, g_post)


def _grad_matmul(a, b, *, ta, tb, ts, name, vmem_mb=48):
    s, ka = a.shape
    nb = b.shape[1]
    ts = min(ts, s)
    nk = s // ts

    def body(a_ref, b_ref, o_ref, *acc):
        if nk == 1:
            o_ref[...] = lax.dot_general(a_ref[...], b_ref[...], TN, preferred_element_type=F32).astype(BF16)
            return
        k = pl.program_id(2)

        @pl.when(k == 0)
        def _():
            acc[0][...] = jnp.zeros_like(acc[0])

        acc[0][...] += lax.dot_general(a_ref[...], b_ref[...], TN, preferred_element_type=F32)

        @pl.when(k == nk - 1)
        def _():
            o_ref[...] = acc[0][...].astype(BF16)

    whole_b = {"pipeline_mode": pl.Buffered(1)} if nk == 1 and nb == tb else {}
    return pl.pallas_call(
        body, name=name, grid=(ka // ta, nb // tb, nk),
        in_specs=[pl.BlockSpec((ts, ta), lambda i, j, k: (k, i)),
                  pl.BlockSpec((ts, tb), lambda i, j, k: (k, j), **whole_b)],
        out_specs=pl.BlockSpec((ta, tb), lambda i, j, k: (i, j)),
        out_shape=jax.ShapeDtypeStruct((ka, nb), BF16),
        scratch_shapes=[pltpu.VMEM((ta, tb), F32)] if nk > 1 else [],
        compiler_params=_cparams(vmem_mb, ("arbitrary", "arbitrary", "arbitrary")),
    )(a, b)


GW_TILE = 256


def _grad_w_in(h1t, pieces):
    ka, s = h1t.shape
    widths = [p.shape[1] for p in pieces]
    assert all(w % GW_TILE == 0 for w in widths)
    first = [sum(widths[:i]) // GW_TILE for i in range(len(pieces))]
    count = [w // GW_TILE for w in widths]

    def body(a_ref, *refs):
        o_ref = refs[-1]
        j = pl.program_id(0)
        for ref, f0, n in zip(refs[:-1], first, count):
            @pl.when((j >= f0) & (j < f0 + n))
            def _(ref=ref):
                o_ref[...] = jnp.dot(a_ref[...], ref[...], preferred_element_type=F32).astype(BF16)

    def spec(f0, n):
        return pl.BlockSpec((s, GW_TILE), lambda j: (0, jnp.clip(j - f0, 0, n - 1)))

    return pl.pallas_call(
        body, name="grad_w_in", grid=(sum(count),),
        in_specs=[_resident((ka, s))] + [spec(f0, n) for f0, n in zip(first, count)],
        out_specs=pl.BlockSpec((ka, GW_TILE), lambda j: (0, j)),
        out_shape=jax.ShapeDtypeStruct((ka, sum(widths)), BF16),
        compiler_params=_cparams(56, ("arbitrary",)),
    )(h1t, *pieces)


def _mix_bwd(dy, w_out, o, cv, bcu, ga, gc, gsum, after, *, tm):
    s = dy.shape[0]

    def group_norm_bwd(dn_out, v, g, gs):
        r = lax.rsqrt(_group_sum(v * v, gs) * (1.0 / DH) + EPS)
        n = v * r
        dn = dn_out * g
        return r * (dn - n * (_group_sum(dn * n, gs) * (1.0 / DH))), dn_out * n

    def body(dy_ref, w_ref, o_ref, cv_ref, bcu_ref, ga_ref, gc_ref, gs_ref, after_ref,
             do_ref, dl_ref, dcv_ref, db_ref, dga_ref, dgc_ref):
        @pl.when(pl.program_id(0) == 0)
        def _():
            dga_ref[...] = jnp.zeros_like(dga_ref)
            dgc_ref[...] = jnp.zeros_like(dgc_ref)

        dm = lax.dot_general(dy_ref[...], w_ref[...], NT, preferred_element_type=F32)
        ov = o_ref[...]
        do, dga = group_norm_bwd(dm[:, 0:AW], ov, ga_ref[...], gs_ref[...])
        dob = do.astype(BF16)
        do_ref[...] = dob
        dl_ref[...] = _group_sum(dob.astype(F32) * ov, gs_ref[...])
        dga_ref[...] += _fold8(dga)
        gate_b = bcu_ref[:, 0:CW].astype(F32)
        cv = cv_ref[...]
        dconv, dgc = group_norm_bwd(dm[:, AW:D], gate_b * cv, gc_ref[...], gs_ref[...])
        dgc_ref[...] += _fold8(dgc)
        dcv_ref[...] = dconv * gate_b
        db_ref[...] = (dconv * cv).astype(BF16)

    return pl.pallas_call(
        body, name="mix_bwd", grid=(s // tm,),
        in_specs=[_rows(tm, D), _resident((D, D)), _rows(tm, AW), _rows(tm, CW), _rows(tm, 3 * CW),
                  _full((1, AW)), _full((1, CW)), _full((GS, GS)), ANY],
        out_specs=[_rows(tm, AW), _rows(tm, AW), _rows(tm, CW), _rows(tm, CW),
                   _full((SUBLANES, AW)), _full((SUBLANES, CW))],
        out_shape=[jax.ShapeDtypeStruct((s, AW), BF16), jax.ShapeDtypeStruct((s, AW), F32),
                   jax.ShapeDtypeStruct((s, CW), F32), jax.ShapeDtypeStruct((s, CW), BF16),
                   jax.ShapeDtypeStruct((SUBLANES, AW), F32), jax.ShapeDtypeStruct((SUBLANES, CW), F32)],
        compiler_params=_cparams(48, ("arbitrary",)),
    )(dy, w_out, o, cv, bcu, ga, gc, gsum, after)


def _conv_bwd(dcv, db, bcu, cw8, *, tm):
    s = dcv.shape[0]
    nt = s // tm

    def body(dcv_ref, nxt_ref, db_ref, bcu_ref, halo_ref, cw_ref, dbcu_ref, dw_ref):
        i = pl.program_id(0)

        @pl.when(i == 0)
        def _():
            dw_ref[...] = jnp.zeros_like(dw_ref)

        z, z1, z2 = _conv_taps(bcu_ref, halo_ref, i == 0, tm)
        d = dcv_ref[...]
        dw_ref[0] += _fold8(d * z2)
        dw_ref[1] += _fold8(d * z1)
        dw_ref[2] += _fold8(d * z)
        nx = jnp.where(i == nt - 1, 0.0, nxt_ref[...])
        row = lax.broadcasted_iota(jnp.int32, (tm, CW), 0)
        d1 = jnp.where(row == tm - 1, nx[0:1, :], pltpu.roll(d, tm - 1, axis=0))
        d2 = jnp.where(row == tm - 2, nx[0:1, :], jnp.where(row == tm - 1, nx[1:2, :], pltpu.roll(d, tm - 2, axis=0)))
        dz = cw_ref[2:3, :] * d + cw_ref[1:2, :] * d1 + cw_ref[0:1, :] * d2
        dbcu_ref[:, 0:CW] = db_ref[...]
        dbcu_ref[:, CW:2 * CW] = (dz * bcu_ref[:, 2 * CW:3 * CW].astype(F32)).astype(BF16)
        dbcu_ref[:, 2 * CW:3 * CW] = (dz * bcu_ref[:, CW:2 * CW].astype(F32)).astype(BF16)

    return pl.pallas_call(
        body, name="conv_bwd", grid=(nt,),
        in_specs=[_rows(tm, CW),
                  pl.BlockSpec((SUBLANES, CW), lambda i: (jnp.minimum((i + 1) * (tm // SUBLANES), s // SUBLANES - 1), 0)),
                  _rows(tm, CW), _rows(tm, 3 * CW), _halo_before(tm, 3 * CW), _full((SUBLANES, CW))],
        out_specs=[_rows(tm, 3 * CW), _full((3, SUBLANES, CW))],
        out_shape=[jax.ShapeDtypeStruct((s, 3 * CW), BF16), jax.ShapeDtypeStruct((3, SUBLANES, CW), F32)],
        compiler_params=_cparams(48, ("arbitrary",)),
    )(dcv, dcv, db, bcu, bcu, cw8)


def _attn_bwd(qp, kp, v, do, lse, dl, mk, *, t):
    s = qp.shape[0]
    nq = s // t

    def body(q_ref, k_ref, v_ref, do_ref, lse_ref, dl_ref, mk_ref, dq_ref, dk_ref, dv_ref, dkx_ref, dq_acc):
        pi = pl.program_id(1)

        @pl.when(pi == 0)
        def _():
            dq_acc[...] = jnp.zeros_like(dq_acc)

        row = lax.broadcasted_iota(jnp.int32, (t, t), 0)
        col = lax.broadcasted_iota(jnp.int32, (t, t), 1)
        lane = lax.broadcasted_iota(jnp.int32, (t, 128), 1)

        def head_step(hh, qi, carry, modes):
            off = pl.multiple_of(qi * t, t)
            rows = pl.ds(off, t)
            q = q_ref[rows, HP * hh:HP * (hh + 1)]
            qt = q.T
            lse_col = lse_ref[rows, DH * hh:DH * hh + 1]
            dl_col = dl_ref[rows, DH * hh:DH * hh + 1]
            do2 = do_ref[rows, :]
            dom = jnp.where(lane < DH, do2 if hh == 0 else pltpu.roll(do2, DH, axis=1), jnp.zeros((), BF16))
            new, dss = [], []
            for half, masked in enumerate(modes):
                if masked is None:
                    new.append(carry[half])
                    continue
                dk, dv, cs = carry[half]
                keys = slice(half * t, (half + 1) * t)
                m_col = mk_ref[half, rows, DH * hh:DH * hh + 1]
                scale = jnp.exp(m_col - lse_col)
                sc = lax.dot_general(q, k_ref[keys, HP * hh:HP * (hh + 1)], NT, preferred_element_type=F32) - m_col
                if masked:
                    sc = jnp.where(col <= row, sc, -1e30)
                pt = jnp.exp(sc).astype(BF16)
                dp = lax.dot_general(dom, v_ref[keys, HP * hh:HP * (hh + 1)], NT, preferred_element_type=F32)
                ds32 = (pt.astype(F32) * scale) * (dp - dl_col)
                ds = ds32.astype(BF16)
                cs = cs + _fold8(ds32)
                dv = dv + jnp.dot((dom.astype(F32) * scale).astype(BF16).T, pt, preferred_element_type=F32)
                dk = dk + jnp.dot(qt, ds, preferred_element_type=F32)
                new.append((dk, dv, cs))
                dss.append((half, ds))
            if len(dss) == 2:
                dq = jnp.dot(jnp.concatenate([dss[0][1], dss[1][1]], axis=1), k_ref[:, HP * hh:HP * (hh + 1)],
                             preferred_element_type=F32)
            else:
                half, ds = dss[0]
                dq = jnp.dot(ds, k_ref[half * t:(half + 1) * t, HP * hh:HP * (hh + 1)], preferred_element_type=F32)
            dq_acc[rows, HP * hh:HP * (hh + 1)] += dq
            return tuple(new)

        def step(qi, carry, modes):
            return tuple(head_step(hh, qi, carry[hh], modes) for hh in range(2))

        def two_heads(a0, a1):
            return jnp.where(lane < DH, a0, pltpu.roll(a1, DH, axis=1))

        def rows_to_lanes(a0, a1):
            return jnp.concatenate([a0, a1], axis=0).T

        zero = (jnp.zeros((HP, t), F32), jnp.zeros((128, t), F32), jnp.zeros((SUBLANES, t), F32))
        carry = step(2 * pi, ((zero, zero), (zero, zero)), (True, None))
        carry = step(2 * pi + 1, carry, (False, True))

        def pair(j, carry):
            qi = 2 * (pi + 1 + j)
            return step(qi + 1, step(qi, carry, (False, False)), (False, False))

        carry = lax.fori_loop(0, nq // 2 - 1 - pi, pair, carry)
        for half in range(2):
            keys = slice(half * t, (half + 1) * t)
            (dk0, dv0, cs0), (dk1, dv1, cs1) = carry[0][half], carry[1][half]
            dk_ref[keys, :] = rows_to_lanes(dk0[0:DH], dk1[0:DH]).astype(BF16)
            dv_ref[keys, :] = rows_to_lanes(dv0[0:DH], dv1[0:DH]).astype(BF16)
            total = lambda cs: jnp.broadcast_to(jnp.sum(cs, axis=0, keepdims=True), (DH, t))
            dkx_ref[keys, :] = rows_to_lanes(total(cs0), total(cs1))

        @pl.when(pi == nq // 2 - 1)
        def _():
            for c in range(s // t):
                rows = slice(c * t, (c + 1) * t)
                dq_ref[rows, :] = two_heads(dq_acc[rows, 0:HP], dq_acc[rows, HP:2 * HP]).astype(BF16)

    return pl.pallas_call(
        body, name="attn_bwd", grid=(H // 2, nq // 2),
        in_specs=[pl.BlockSpec((s, 2 * HP), lambda p, i: (0, p)),
                  pl.BlockSpec((2 * t, 2 * HP), lambda p, i: (i, p)),
                  pl.BlockSpec((2 * t, 2 * HP), lambda p, i: (i, p)),
                  pl.BlockSpec((s, 128), lambda p, i: (0, p)),
                  pl.BlockSpec((s, 128), lambda p, i: (0, p)),
                  pl.BlockSpec((s, 128), lambda p, i: (0, p)),
                  pl.BlockSpec((2, s, 128), lambda p, i: (i, 0, p))],
        out_specs=[pl.BlockSpec((s, 128), lambda p, i: (0, p)),
                   pl.BlockSpec((2 * t, 128), lambda p, i: (i, p)),
                   pl.BlockSpec((2 * t, 128), lambda p, i: (i, p)),
                   pl.BlockSpec((2 * t, 128), lambda p, i: (i, p))],
        out_shape=[jax.ShapeDtypeStruct((s, AW), BF16), jax.ShapeDtypeStruct((s, AW), BF16),
                   jax.ShapeDtypeStruct((s, AW), BF16), jax.ShapeDtypeStruct((s, AW), F32)],
        scratch_shapes=[pltpu.VMEM((s, 2 * HP), F32)],
        compiler_params=_cparams(56, ("arbitrary", "arbitrary")),
    )(qp, kp, v, do, lse, dl, mk)


def _forget_bwd(dkx, z, sel, *, tm):
    s = dkx.shape[0]
    nt = s // tm

    def body(dk_ref, z_ref, sel_ref, dfl_ref, dbf_ref, carry):
        @pl.when(pl.program_id(0) == 0)
        def _():
            carry[...] = jnp.zeros_like(carry)
            dbf_ref[...] = jnp.zeros_like(dbf_ref)

        dc = _split_dot(dk_ref[...], sel_ref[...])
        row = lax.broadcasted_iota(jnp.int32, (tm, tm), 0)
        col = lax.broadcasted_iota(jnp.int32, (tm, tm), 1)
        tri = (col >= row).astype(BF16)
        dlogf = _exact_dot01(tri, dc) + carry[0:1, :]
        carry[...] = jnp.broadcast_to(dlogf[0:1, :], carry.shape)
        dz = dlogf * (1.0 - jax.nn.sigmoid(z_ref[...]))
        dfl_ref[:, 0:128] = dz.astype(BF16)
        dfl_ref[:, 128:GW_TILE] = jnp.zeros((tm, GW_TILE - 128), BF16)
        dbf_ref[...] += _fold8(dz)

    rev = lambda i: (nt - 1 - i, 0)
    return pl.pallas_call(
        body, name="forget_bwd", grid=(nt,),
        in_specs=[pl.BlockSpec((tm, AW), rev), pl.BlockSpec((tm, 128), rev), _full((AW, 128))],
        out_specs=[pl.BlockSpec((tm, GW_TILE), rev), _full((SUBLANES, 128))],
        out_shape=[jax.ShapeDtypeStruct((s, GW_TILE), BF16), jax.ShapeDtypeStruct((SUBLANES, 128), F32)],
        scratch_shapes=[pltpu.VMEM((SUBLANES, 128), F32)],
        compiler_params=_cparams(48, ("arbitrary",)),
    )(dkx, z, sel)


def _in_proj_bwd(pieces, wp, x, g1, dx2, after, *, tm):
    s = x.shape[0]

    def body(q_ref, k_ref, v_ref, bcu_ref, f_ref, w_ref, x_ref, g_ref, dx2_ref, after_ref, dx_ref, dg_ref):
        @pl.when(pl.program_id(0) == 0)
        def _():
            dg_ref[...] = jnp.zeros_like(dg_ref)

        dh = None
        for ref, (lo, hi) in zip((q_ref, k_ref, v_ref, bcu_ref, f_ref), PIECES):
            part = lax.dot_general(ref[...], w_ref[:, lo:hi], NT, preferred_element_type=F32)
            dh = part if dh is None else dh + part
        _, n, r = _rms_fwd(x_ref[...], g_ref[...])
        dxn, dg = _rms_bwd(dh, n, r, g_ref[...])
        dx_ref[...] = dx2_ref[...] + dxn
        dg_ref[...] += _fold8(dg)

    return pl.pallas_call(
        body, name="in_proj_bwd", grid=(s // tm,),
        in_specs=[_rows(tm, hi - lo) for lo, hi in PIECES]
        + [_resident((D, WP)), _rows(tm, D), _full((1, D)), _rows(tm, D), ANY],
        out_specs=[_rows(tm, D), _full((SUBLANES, D))],
        out_shape=[jax.ShapeDtypeStruct((s, D), F32), jax.ShapeDtypeStruct((SUBLANES, D), F32)],
        compiler_params=_cparams(56, ("arbitrary",)),
    )(*pieces, wp, x, g1, dx2, after)


def _position():
    return lax.axis_index("x"), lax.axis_index("y"), lax.axis_index("c")


ANY = pl.BlockSpec(memory_space=pl.ANY)


def _all_gather(shards):
    n = len(shards)

    def body(*refs):
        x_refs, out_refs = refs[:n], refs[n:2 * n]
        send_sems, recv_sems, local_sems = refs[2 * n:]
        x, y, c = _position()
        me, sibling = (x, y, c), (x, y, 1 - c)
        chips = [(1 - x, y), (x, 1 - y), (1 - x, 1 - y)]

        def copy(a, k, block, to, own=False):
            slot = out_refs[a].at[4 * block[0] + 2 * block[1] + block[2]]
            return pltpu.make_async_remote_copy(
                src_ref=x_refs[a] if own else slot, dst_ref=slot,
                send_sem=send_sems.at[7 * a + k], recv_sem=recv_sems.at[7 * a + k], device_id=to, device_id_type=MESH_ID)

        mine = [pltpu.make_async_copy(x_refs[a], out_refs[a].at[4 * x + 2 * y + c], local_sems.at[a]) for a in range(n)]
        for cp in mine:
            cp.start()
        first = []
        for a in range(n):
            first.append(copy(a, 0, me, sibling, own=True))
            first += [copy(a, 1 + j, me, (*chip, c), own=True) for j, chip in enumerate(chips)]
        for cp in first:
            cp.start()
        passed = []
        for j, chip in enumerate(chips):
            for a in range(n):
                copy(a, 1 + j, (*chip, c), me).wait_recv()
                fwd = copy(a, 4 + j, (*chip, c), sibling)
                fwd.start()
                passed.append(fwd)
        for a in range(n):
            copy(a, 0, sibling, me).wait_recv()
            for j, chip in enumerate(chips):
                copy(a, 4 + j, (*chip, 1 - c), me).wait_recv()
        for cp in first + passed:
            cp.wait_send()
        for cp in mine:
            cp.wait()

    return pl.pallas_call(
        body, name="all_gather_weights",
        out_shape=[jax.ShapeDtypeStruct((NDEV,) + sh.shape, sh.dtype) for sh in shards],
        in_specs=[ANY] * n, out_specs=[ANY] * n,
        scratch_shapes=[pltpu.SemaphoreType.DMA((7 * n,)), pltpu.SemaphoreType.DMA((7 * n,)), pltpu.SemaphoreType.DMA((n,))],
    )(*shards)


def _pair_exchange(grads):
    n = len(grads)

    def body(*refs):
        g_refs, out_refs = refs[:n], refs[n:2 * n]
        send_sems, recv_sems = refs[2 * n:]
        x, y, c = _position()
        copies = [pltpu.make_async_remote_copy(
            src_ref=g_refs[a].at[:, pl.ds(1 - c, 1)], dst_ref=out_refs[a], send_sem=send_sems.at[a],
            recv_sem=recv_sems.at[a], device_id=(x, y, 1 - c), device_id_type=MESH_ID) for a in range(n)]
        for cp in copies:
            cp.start()
        for cp in copies:
            cp.wait()

    return pl.pallas_call(
        body, name="grad_pair_exchange",
        out_shape=[jax.ShapeDtypeStruct((4, 1) + g.shape[2:], g.dtype) for g in grads],
        in_specs=[ANY] * n, out_specs=[ANY] * n,
        scratch_shapes=[pltpu.SemaphoreType.DMA((n,)), pltpu.SemaphoreType.DMA((n,))],
    )(*grads)


def _pair_sum(g, got, idx, *, tr, name):
    r, c = g.shape[2:]

    def body(idx_ref, g_ref, got_ref, pb_ref, own_ref):
        p = g_ref[0, 0].astype(F32) + got_ref[0, 0].astype(F32)
        pb_ref[0] = p.astype(BF16)

        @pl.when(pl.program_id(1) == idx_ref[1])
        def _():
            own_ref[...] = p

    return pl.pallas_call(
        body, name=name,
        grid_spec=pltpu.PrefetchScalarGridSpec(
            num_scalar_prefetch=1, grid=(r // tr, 4),
            in_specs=[pl.BlockSpec((1, 1, tr, c), lambda i, j, idx: (j, idx[0], i, 0)),
                      pl.BlockSpec((1, 1, tr, c), lambda i, j, idx: (j, 0, i, 0))],
            out_specs=[pl.BlockSpec((1, tr, c), lambda i, j, idx: (j, i, 0)),
                       pl.BlockSpec((tr, c), lambda i, j, idx: (i, 0))]),
        out_shape=[jax.ShapeDtypeStruct((4, r, c), BF16), jax.ShapeDtypeStruct((r, c), F32)],
        compiler_params=_cparams(62, ("arbitrary", "arbitrary")),
    )(idx, g, got)


HBM = pl.BlockSpec(memory_space=pltpu.HBM)
SEM = pl.BlockSpec(memory_space=pltpu.SEMAPHORE)
DATAFLOW = pltpu.SideEffectType.DATAFLOW_SIDE_EFFECTING


PEERS = {"gather": NDEV - 1, "scatter": NDEV - 1, "chips": 3}


def _exchange_copies(src_refs, land_refs, send_sems, recv_sems, mode):
    x, y, c = _position()
    me, my_chip = 4 * x + 2 * y + c, 2 * x + y
    npeers = PEERS[mode]
    copies, own = [], []
    for a, (s_ref, l_ref) in enumerate(zip(src_refs, land_refs)):
        for k in range(npeers):
            if mode == "chips":
                px, py, pc = x ^ ((k + 1) >> 1), y ^ ((k + 1) & 1), c
                src, dst = s_ref.at[2 * px + py], l_ref.at[my_chip]
            else:
                px, py, pc = x ^ ((k + 1) >> 2), y ^ (((k + 1) >> 1) & 1), c ^ ((k + 1) & 1)
                src, dst = (s_ref.at[4 * px + 2 * py + pc] if mode == "scatter" else s_ref), l_ref.at[me]
            copies.append(pltpu.make_async_remote_copy(
                src_ref=src, dst_ref=dst, send_sem=send_sems.at[npeers * a + k], recv_sem=recv_sems.at[npeers * a + k],
                device_id=(px, py, pc), device_id_type=MESH_ID))
        slot = my_chip if mode == "chips" else me
        own.append(pltpu.make_async_copy(s_ref if mode == "gather" else s_ref.at[slot], l_ref.at[slot],
                                         send_sems.at[npeers * len(src_refs) + a]))
    return copies, own


def _exchange_start(srcs, lands, after, *, mode, name):
    n = len(srcs)
    nsem = PEERS[mode] * n

    def body(*refs):
        token = refs[-1]
        copies, own = _exchange_copies(refs[:n], refs[n:2 * n], refs[2 * n + 1], refs[2 * n + 2], mode)
        for cp in copies + own:
            cp.start()
        token[...] = jnp.zeros_like(token)

    arrays = list(srcs) + list(lands)
    outs = pl.pallas_call(
        body, name=name,
        out_shape=(pltpu.SemaphoreType.DMA((nsem + n,)), pltpu.SemaphoreType.DMA((nsem,)),
                   *[pltpu.HBM(a.shape, a.dtype) for a in arrays], jax.ShapeDtypeStruct((SUBLANES, LANES), F32)),
        in_specs=[HBM] * (2 * n) + [ANY],
        out_specs=(SEM, SEM, *[HBM] * (2 * n), pl.BlockSpec(memory_space=pltpu.VMEM)),
        input_output_aliases={i: 2 + i for i in range(2 * n)},
        compiler_params=pltpu.CompilerParams(has_side_effects=DATAFLOW),
    )(*[pltpu.with_memory_space_constraint(a, pltpu.HBM) for a in arrays], after)
    return outs[0], outs[1], outs[2:2 + n], outs[2 + n:2 + 2 * n], outs[-1]


def _exchange_wait(send_sems, recv_sems, srcs, lands, after, *, mode, name):
    n = len(srcs)

    def body(*refs):
        copies, own = _exchange_copies(refs[:n], refs[n:2 * n], refs[2 * n], refs[2 * n + 1], mode)
        for cp in copies:
            cp.wait_send()
            cp.wait_recv()
        for cp in own:
            cp.wait()

    arrays = list(srcs) + list(lands)
    outs = pl.pallas_call(
        body, name=name,
        out_shape=tuple(pltpu.HBM(a.shape, a.dtype) for a in arrays),
        in_specs=[HBM] * (2 * n) + [SEM, SEM, ANY],
        out_specs=tuple([HBM] * (2 * n)),
        input_output_aliases={i: i for i in range(2 * n)},
        compiler_params=pltpu.CompilerParams(has_side_effects=DATAFLOW),
    )(*arrays, send_sems, recv_sems, after)
    return outs[n:]


def _small_all_reduce(parts):
    def body(gmp_ref, gmo_ref, gfp_ref, gfo_ref, ga_ref, gc_ref, dw_ref, bf_ref, loss_ref,
             out_ref, buf, send_sems, recv_sems):
        x, y, c = _position()
        me = 4 * x + 2 * y + c

        def colsum(v):
            return jnp.sum(v, axis=0, keepdims=True)

        loss = jnp.sum(colsum(loss_ref[...]), axis=1, keepdims=True) * (0.5 / D)
        rows = [colsum(gmp_ref[...]), colsum(gmo_ref[...]), colsum(gfp_ref[...]), colsum(gfo_ref[...]),
                jnp.concatenate([colsum(ga_ref[...]), colsum(gc_ref[...])], axis=1),
                jnp.concatenate([colsum(dw_ref[0]), colsum(dw_ref[1])], axis=1),
                jnp.concatenate([colsum(dw_ref[2]), colsum(bf_ref[...]), jnp.broadcast_to(loss, (1, 128)),
                                 jnp.zeros((1, 256), F32)], axis=1),
                jnp.zeros((1, D), F32)]
        buf[me] = jnp.concatenate(rows, axis=0)
        copies = []
        for mm in range(1, NDEV):
            peer = (x ^ (mm >> 2), y ^ ((mm >> 1) & 1), c ^ (mm & 1))
            copies.append(pltpu.make_async_remote_copy(
                src_ref=buf.at[me], dst_ref=buf.at[me], send_sem=send_sems.at[mm - 1], recv_sem=recv_sems.at[mm - 1],
                device_id=peer, device_id_type=MESH_ID))
        for cp in copies:
            cp.start()
        for cp in copies:
            cp.wait_recv()
        for cp in copies:
            cp.wait_send()
        acc = buf[0]
        for d in range(1, NDEV):
            acc = acc + buf[d]
        out_ref[...] = acc

    vm = pl.BlockSpec(memory_space=pltpu.VMEM)
    return pl.pallas_call(
        body, name="small_all_reduce",
        out_shape=jax.ShapeDtypeStruct((SUBLANES, D), F32),
        in_specs=[vm] * len(parts), out_specs=vm,
        scratch_shapes=[pltpu.VMEM((NDEV, SUBLANES, D), F32), pltpu.SemaphoreType.DMA((7,)), pltpu.SemaphoreType.DMA((7,))],
    )(*parts)


def _adam_update(w, g, m, v):
    nm = ADAM_B1 * m + (1.0 - ADAM_B1) * g
    nv = ADAM_B2 * v + (1.0 - ADAM_B2) * (g * g)
    m_hat = nm / (1.0 - ADAM_B1 ** ADAM_STEP)
    v_hat = nv / (1.0 - ADAM_B2 ** ADAM_STEP)
    return -ADAM_LR * (m_hat / (jnp.sqrt(v_hat) + ADAM_EPS) + ADAM_WD * w), nm, nv


SMALL_SLOTS = {"g_mix_pre": (0, 0, D), "g_mix_post": (1, 0, D), "g_ffn_pre": (2, 0, D), "g_ffn_post": (3, 0, D),
               "g_attn_out": (4, 0, AW), "g_conv_out": (4, AW, CW), "b_forget": (6, CW, H)}
LOSS_LANE = CW + 128


def _small_adamw(small, conv_grad, params):
    names = list(params)
    n = len(names)

    def body(*refs):
        small_ref, cg_ref = refs[0], refs[1]
        ins, outs = refs[2:2 + 3 * n], refs[2 + 3 * n:]
        for i, name in enumerate(names):
            w_ref, m_ref, v_ref = ins[3 * i:3 * i + 3]
            g_ref, d_ref, nm_ref, nv_ref = outs[4 * i:4 * i + 4]
            if name == "conv_w":
                g = cg_ref[...]
            else:
                r, c0, width = SMALL_SLOTS[name]
                g = small_ref[r:r + 1, c0:c0 + width]
            g_ref[...] = g
            d_ref[...], nm_ref[...], nv_ref[...] = _adam_update(w_ref[...], g, m_ref[...], v_ref[...])
        outs[4 * n][...] = small_ref[6:7, LOSS_LANE:LOSS_LANE + 1]

    vm = pl.BlockSpec(memory_space=pltpu.VMEM)
    flat = [a for name in names for a in params[name]]
    outs = pl.pallas_call(
        body, name="adamw_small",
        in_specs=[vm] * (2 + 3 * n), out_specs=[vm] * (4 * n + 1),
        out_shape=[jax.ShapeDtypeStruct(params[name][0].shape, F32) for name in names for _ in range(4)]
        + [jax.ShapeDtypeStruct((1, 1), F32)],
    )(small, conv_grad, *flat)
    return {name: outs[4 * i:4 * i + 4] for i, name in enumerate(names)}, outs[4 * n].reshape(())


def _chip_sum_adamw(got, own, idx, wt, mt, vt, *, tr, name):
    cols, rows = wt.shape
    gcols = own.shape[1]

    def body(idx_ref, got_ref, own_ref, w_ref, m_ref, v_ref, g_ref, d_ref, nm_ref, nv_ref):
        g = jnp.zeros((tr, gcols), F32)
        for j in range(4):
            g = g + jnp.where(idx_ref[1] == j, own_ref[...], got_ref[j].astype(F32))
        g = g.T[:cols]
        g_ref[...] = g
        d_ref[...], nm_ref[...], nv_ref[...] = _adam_update(w_ref[...], g, m_ref[...], v_ref[...])

    spec = pl.BlockSpec((cols, tr), lambda i, idx: (0, i))
    gspec = pl.BlockSpec((tr, gcols), lambda i, idx: (i, 0))
    return pl.pallas_call(
        body, name=name,
        grid_spec=pltpu.PrefetchScalarGridSpec(
            num_scalar_prefetch=1, grid=(rows // tr,),
            in_specs=[pl.BlockSpec((4, tr, gcols), lambda i, idx: (0, i, 0)), gspec, spec, spec, spec],
            out_specs=[spec] * 4),
        out_shape=[jax.ShapeDtypeStruct((cols, rows), F32)] * 4,
        compiler_params=_cparams(32, ("arbitrary",)),
    )(idx, got, own, wt, mt, vt)


def _device_sum_adamw(land, w, m, v, *, tr, name):
    rows, cols = w.shape

    def body(land_ref, w_ref, m_ref, v_ref, g_ref, d_ref, nm_ref, nv_ref):
        g = land_ref[0].astype(F32)
        for dev in range(1, NDEV):
            g = g + land_ref[dev].astype(F32)
        g_ref[...] = g
        d_ref[...], nm_ref[...], nv_ref[...] = _adam_update(w_ref[...], g, m_ref[...], v_ref[...])

    spec = pl.BlockSpec((tr, cols), lambda i: (i, 0))
    return pl.pallas_call(
        body, name=name, grid=(rows // tr,),
        in_specs=[pl.BlockSpec((NDEV, tr, cols), lambda i: (0, i, 0)), spec, spec, spec],
        out_specs=[spec] * 4,
        out_shape=[jax.ShapeDtypeStruct((rows, cols), F32)] * 4,
        compiler_params=_cparams(32, ("arbitrary",)),
    )(land, w, m, v)


def _placement_constants():
    j = np.arange(128)[:, None]
    lane = np.arange(1024)[None, :]
    head, sub = lane // HP, lane % HP
    piece, jh = j // H, j % H
    valid = (j < 3 * H) & (jh == head)
    pq = np.where(valid & (sub == DH + piece), 1.0, 0.0).astype(BF16)
    pk = np.where(valid & (sub == DH + 3 + piece), -1.0, 0.0).astype(BF16)
    oq = np.where((sub >= DH + 3) & (sub < DH + 6), 1.0, 0.0).astype(np.float32)
    ok = np.where((sub >= DH) & (sub < DH + 3), 1.0, 0.0).astype(np.float32)
    r = np.arange(AW)[:, None]
    cc = np.arange(128)[None, :]
    sel = np.where((r % DH == 3) & (r // DH == cc), -1.0, 0.0).astype(BF16)
    gi = np.arange(GS)
    gsum = (gi[:, None] // DH == gi[None, :] // DH).astype(BF16)
    return tuple(jnp.asarray(c) for c in (pq, pk, oq, ok, sel, gsum))


def _local_step(xs, tgt, wp, late_weights, cw8, bfp, g_attn_out, g_conv_out,
                g_mix_pre, g_mix_post, g_ffn_pre, g_ffn_post, early_grads=None, last_grad=None):
    pq, pk, oq, ok, sel, gsum = _placement_constants()
    h1t, qp, kp, vv, bcu, zf = _in_proj(xs, g_mix_pre, wp, bfp, pq, pk, oq, ok, tm=512)
    o, lse, mk = _attn_fwd(qp, kp, vv, t=512)
    w_out_f, wgu, wd = late_weights(lse)
    merged, y, x2, cv, h2 = _mix_out(o, bcu, cw8, g_attn_out, g_conv_out, gsum, w_out_f, xs, g_mix_post, g_ffn_pre, tm=512)
    gate, up, act, dx3, dff, loss_p, dg_ffn_post = _ffn_fwd_loss(h2, wgu, wd, x2, tgt, g_ffn_post, tm=512)

    dgu, dx2, dy, dg_ffn_pre, dg_mix_post = _ffn_bwd(dff, wd, gate, up, wgu, x2, g_ffn_pre, dx3, y, g_mix_post, tm=256)
    dw_down = _grad_matmul(act, dff, ta=DFF // 2, tb=D, ts=4096, name="grad_w_down", vmem_mb=60)
    dw_gu = _grad_matmul(dgu, h2, ta=DFF // 2, tb=D, ts=4096, name="grad_w_gate_up", vmem_mb=60).reshape(NDEV, FB, D)
    dw_out = _grad_matmul(merged, dy, ta=1024, tb=1024, ts=2048, name="grad_w_out", vmem_mb=62)
    token = early_grads(dw_out, dw_gu, dw_down) if early_grads is not None else dw_out
    do, dl, dcv, db, dg_attn, dg_conv = _mix_bwd(dy, w_out_f, o, cv, bcu, g_attn_out, g_conv_out, gsum, token, tm=512)
    dbcu, dtaps = _conv_bwd(dcv, db, bcu, cw8, tm=512)
    dqp, dkp, dv, dkx = _attn_bwd(qp, kp, vv, do, lse, dl, mk, t=512)
    dfl, dbf = _forget_bwd(dkx, zf, sel, tm=512)
    pieces = (dqp, dkp, dv, dbcu, dfl)
    dwp = _grad_w_in(h1t, pieces)
    token = last_grad(dwp) if last_grad is not None else dwp
    grad_x, dg_mix_pre = _in_proj_bwd(pieces, wp, xs, g_mix_pre, dx2, token, tm=512)
    return (grad_x, dwp, dw_out, dw_gu, dw_down, dg_mix_pre, dg_mix_post, dg_ffn_pre, dg_ffn_post, dg_attn, dg_conv,
            dtaps, dbf, loss_p)


BIG_TILES = {"w_in": 256, "w_out": 128, "w_gate_up": 176, "w_down": 176}


def kernel(x, w_in, b_forget, conv_w, g_attn_out, g_conv_out, w_out, g_mix_pre, g_mix_post, w_gate_up, w_down, g_ffn_pre, g_ffn_post, loss_target, m_w_in, m_b_forget, m_conv_w, m_g_attn_out, m_g_conv_out, m_w_out, m_g_mix_pre, m_g_mix_post, m_w_gate_up, m_w_down, m_g_ffn_pre, m_g_ffn_post, v_w_in, v_b_forget, v_conv_w, v_g_attn_out, v_g_conv_out, v_w_out, v_g_mix_pre, v_g_mix_post, v_w_gate_up, v_w_down, v_g_ffn_pre, v_g_ffn_post):
    xc, yc, cc = _position()
    my_chip = 2 * xc + yc
    me = 2 * my_chip + cc
    idx = jnp.stack([cc, my_chip]).astype(jnp.int32)
    tables = _in_layout_tables()

    w_in_b = w_in[0].astype(BF16)
    g_in, g_last, g_taps = _all_gather([w_in_b[:, :IN_MAIN], w_in_b[:, IN_MAIN].reshape(SUBLANES, LANES), conv_w[0]])
    last_cols = jnp.pad(g_last.reshape(NDEV, D).T.astype(F32), ((0, 0), (0, LANES - NDEV)))
    wp = _assemble_w_in(g_in, last_cols, tables, tr=256)
    cw8 = jnp.pad(g_taps.transpose(1, 0, 2).reshape(3, CW), ((0, SUBLANES - 3), (0, 0)))

    late = [w_out[0].astype(BF16), w_gate_up[0].T.astype(BF16), w_down[0].astype(BF16)]
    ssem, rsem, late_thru, land_thru, token = _exchange_start(
        late, [lax.empty((NDEV,) + s.shape, s.dtype) for s in late], g_in, mode="gather",
        name="gather_late_start")
    bfp = jnp.pad(b_forget, ((0, 0), (0, 128 - H))) + token[0:1, :]

    def late_weights(after):
        l_out, l_gu, l_down = _exchange_wait(ssem, rsem, late_thru, land_thru, after, mode="gather", name="gather_late_wait")
        return l_out.reshape(D, D), l_gu.reshape(2, DFF, D), l_down.reshape(DFF, D)

    early = {}

    def early_grads(dw_out, dw_gu, dw_down):
        srcs = [dw_out.reshape(NDEV, D // NDEV, D), dw_gu, dw_down.reshape(NDEV, DFF // NDEV, D)]
        lands = [lax.empty(s.shape, s.dtype) for s in srcs]
        early["handles"] = _exchange_start(srcs, lands, dw_out, mode="scatter", name="scatter_early_start")
        return early["handles"][4]

    last = {}

    def last_grad(dwp):
        g_w_in = _disassemble_w_in(dwp, tables, tr=256).reshape(4, 2, D, IN_PAD)
        (from_sibling,) = _pair_exchange([g_w_in])
        pair_b, last["own"] = _pair_sum(g_w_in, from_sibling, idx, tr=BIG_TILES["w_in"], name="grad_pair_sum_w_in")
        last["handles"] = _exchange_start([pair_b], [lax.empty(pair_b.shape, pair_b.dtype)], last["own"], mode="chips",
                                          name="chips_w_in_start")
        return last["handles"][4]

    (grad_x, dwp, dw_out, dw_gu, dw_down, dg_mix_pre, dg_mix_post, dg_ffn_pre, dg_ffn_post, dg_attn, dg_conv,
     dtaps, dbf, loss_p) = _local_step(x[0], loss_target[0], wp, late_weights, cw8, bfp, g_attn_out, g_conv_out,
                                        g_mix_pre, g_mix_post, g_ffn_pre, g_ffn_post, early_grads, last_grad)

    e_ssem, e_rsem, e_srcs, e_lands, _ = early["handles"]
    land_out, land_gu, land_down = _exchange_wait(e_ssem, e_rsem, e_srcs, e_lands, dg_mix_pre, mode="scatter",
                                                  name="scatter_early_wait")
    res = {}
    big = {"w_out": (land_out, w_out[0], m_w_out[0], v_w_out[0]),
           "w_gate_up": (land_gu, w_gate_up[0].T, m_w_gate_up[0].T, v_w_gate_up[0].T),
           "w_down": (land_down, w_down[0], m_w_down[0], v_w_down[0])}
    for name, (land, w, m, v) in big.items():
        outs = _device_sum_adamw(land, w, m, v, tr=BIG_TILES[name], name="adamw_" + name)
        res[name] = [(o.T if name == "w_gate_up" else o)[None] for o in outs]
    c_ssem, c_rsem, c_srcs, c_lands, _ = last["handles"]
    after = sum(res[n][1][0, :SUBLANES, :LANES] for n in big)
    (from_chips,) = _exchange_wait(c_ssem, c_rsem, c_srcs, c_lands, after, mode="chips", name="chips_w_in_wait")
    outs = _chip_sum_adamw(from_chips, last["own"], idx, w_in[0].T, m_w_in[0].T, v_w_in[0].T,
                           tr=BIG_TILES["w_in"], name="adamw_w_in")
    res["w_in"] = [o.T[None] for o in outs]

    small = _small_all_reduce([dg_mix_pre, dg_mix_post, dg_ffn_pre, dg_ffn_post, dg_attn, dg_conv, dtaps, dbf, loss_p])
    taps_full = jnp.concatenate([small[5:6, :CW], small[5:6, CW:], small[6:7, :CW]], axis=0)
    taps_first = lambda a: a.transpose(1, 0, 2)
    smalls = {"b_forget": (b_forget, m_b_forget, v_b_forget),
              "conv_w": (taps_first(conv_w), taps_first(m_conv_w), taps_first(v_conv_w)),
              "g_attn_out": (g_attn_out, m_g_attn_out, v_g_attn_out), "g_conv_out": (g_conv_out, m_g_conv_out, v_g_conv_out),
              "g_mix_pre": (g_mix_pre, m_g_mix_pre, v_g_mix_pre), "g_mix_post": (g_mix_post, m_g_mix_post, v_g_mix_post),
              "g_ffn_pre": (g_ffn_pre, m_g_ffn_pre, v_g_ffn_pre), "g_ffn_post": (g_ffn_post, m_g_ffn_post, v_g_ffn_post)}
    own_taps = lax.dynamic_slice(taps_full, (0, me * 64), (3, 64))[:, None, :]
    small_res, loss = _small_adamw(small, own_taps, smalls)
    for name, outs in small_res.items():
        res[name] = [taps_first(o) for o in outs] if name == "conv_w" else list(outs)

    order = ["w_in", "b_forget", "conv_w", "g_attn_out", "g_conv_out", "w_out", "g_mix_pre", "g_mix_post",
             "w_gate_up", "w_down", "g_ffn_pre", "g_ffn_post"]
    outs = [loss, grad_x[None]]
    for k in range(4):
        outs += [res[n][k] for n in order]
    return tuple(outs)
```

```python
import functools

import numpy as np

import jax
import jax.numpy as jnp
from jax import lax
from jax.experimental import pallas as pl
from jax.experimental.pallas import tpu as pltpu

F32 = jnp.float32
BF16 = jnp.bfloat16
MESH_ID = pl.DeviceIdType.MESH

D = 1024
H = 8
DH = 64
AW = 512
CW = 512
DFF = 2816
FB = DFF // 4
FF_CHUNKS = ((0, 768), (768, 768), (1536, 768), (2304, 512))
HP = 128
OFF_Q, OFF_K, OFF_V, OFF_BCU, OFF_F = 0, 512, 1024, 1536, 3072
WP = OFF_F + 128
PIECES = ((OFF_Q, OFF_K), (OFF_K, OFF_V), (OFF_V, OFF_BCU), (OFF_BCU, OFF_F), (OFF_F, WP))
EPS = 1e-6
LOG2E, LN2 = 1.4426950408889634, 0.6931471805599453
NDEV = 8
LANES = 128
SUBLANES = 8
IN_COLS = 385
IN_PAD = 512
IN_MAIN = 384
WIN = 640
ADAM_LR, ADAM_B1, ADAM_B2, ADAM_EPS, ADAM_WD, ADAM_STEP = 0.001, 0.9, 0.999, 1e-08, 0.01, 10

NT = (((1,), (1,)), ((), ()))
TN = (((0,), (0,)), ((), ()))


def _cparams(vmem_mb=None, sem=None):
    kw = {}
    if vmem_mb is not None:
        kw["vmem_limit_bytes"] = vmem_mb << 20
    if sem is not None:
        kw["dimension_semantics"] = sem
    return pltpu.CompilerParams(**kw)


def _full(shape):
    return pl.BlockSpec(shape, lambda *_: (0,) * len(shape))


def _resident(shape):
    return pl.BlockSpec(shape, lambda *_: (0,) * len(shape), pipeline_mode=pl.Buffered(1))


def _rows(tm, width):
    return pl.BlockSpec((tm, width), lambda i: (i, 0))


def _fold8(v):
    r, w = v.shape
    return jnp.sum(v.reshape(r // SUBLANES, SUBLANES, w), axis=0)


def _split_dot(v, m01):
    hi = v.astype(BF16)
    lo = (v - hi.astype(F32)).astype(BF16)
    return (jnp.dot(hi, m01, preferred_element_type=F32)
            + jnp.dot(lo, m01, preferred_element_type=F32))


GS = 256


def _group_sum(v, g01):
    parts = [_split_dot(v[:, c:c + GS], g01) for c in range(0, v.shape[1], GS)]
    return parts[0] if len(parts) == 1 else jnp.concatenate(parts, axis=1)


def _exact_dot01(m01, v):
    p1 = v.astype(BF16)
    r1 = v - p1.astype(F32)
    p2 = r1.astype(BF16)
    p3 = (r1 - p2.astype(F32)).astype(BF16)
    return (jnp.dot(m01, p1, preferred_element_type=F32) + jnp.dot(m01, p2, preferred_element_type=F32)
            + jnp.dot(m01, p3, preferred_element_type=F32))


def _rms_fwd(v, g):
    r = lax.rsqrt(jnp.mean(v * v, axis=-1, keepdims=True) + EPS)
    n = v * r
    return n * g, n, r


def _rms_bwd(do, n, r, g):
    dn = do * g
    return r * (dn - n * jnp.mean(dn * n, axis=-1, keepdims=True)), do * n


def _padded_column(n):
    if n < AW:
        return OFF_Q + n, 0.125
    if n < 3 * AW:
        return n, 1.0
    if n < 3 * AW + H:
        return OFF_F + n - 3 * AW, 1.0
    return OFF_BCU + n - 3 * AW - H, 1.0


def _in_layout_tables():
    dest = -np.ones((IN_PAD, LANES), np.int32)
    dest_f = -np.ones((IN_PAD, LANES), np.int32)
    scale = np.zeros((IN_PAD, LANES), np.float32)
    starts = []
    for k in range(NDEV):
        cols = [_padded_column(IN_COLS * k + j) for j in range(IN_COLS)]
        main = [c for c, _ in cols if c < OFF_F]
        ws = min((min(main) // LANES) * LANES, OFF_F - WIN)
        assert ws <= min(main) and max(main) < ws + WIN
        starts.append(ws)
        for j, (c, sc) in enumerate(cols):
            scale[j, k] = sc
            if c < OFF_F:
                dest[j, k] = c - ws
            else:
                dest_f[j, k] = c - OFF_F
    f_shards = tuple(k for k in range(NDEV) if (dest_f[:, k] >= 0).any())
    return tuple(starts), f_shards, jnp.asarray(dest), jnp.asarray(dest_f), jnp.asarray(scale)


def _perm(dest_ref, scale_ref, k, width, rows=IN_PAD):
    lane = lax.broadcasted_iota(jnp.int32, (rows, width), 1)
    return jnp.where(dest_ref[0:rows, k:k + 1] == lane, scale_ref[0:rows, k:k + 1], 0.0).astype(BF16)


def _assemble_w_in(blocks, last_cols, tables, *, tr):
    starts, f_shards, dest, dest_f, scale = tables
    last = [_padded_column(IN_COLS * k + IN_MAIN) for k in range(NDEV)]
    f_main = [any(_padded_column(IN_COLS * k + j)[0] >= OFF_F for j in range(IN_MAIN)) for k in range(NDEV)]
    assert IN_COLS == IN_MAIN + 1

    def body(b_ref, c_ref, dest_ref, destf_ref, scale_ref, o_ref):
        o_ref[...] = jnp.zeros_like(o_ref)
        lane = lax.broadcasted_iota(jnp.int32, (tr, LANES), 1)
        for k in range(NDEV):
            b = b_ref[k]
            ws = starts[k]
            part = jnp.dot(b, _perm(dest_ref, scale_ref, k, WIN, IN_MAIN), preferred_element_type=F32)
            o_ref[:, ws:ws + WIN] += part.astype(BF16)
            if f_main[k]:
                part = jnp.dot(b, _perm(destf_ref, scale_ref, k, 128, IN_MAIN), preferred_element_type=F32)
                o_ref[:, OFF_F:WP] += part.astype(BF16)
            col, sc = last[k]
            tile = (col // LANES) * LANES
            o_ref[:, tile:tile + LANES] += jnp.where(lane == col - tile, c_ref[:, k:k + 1] * sc, 0.0).astype(BF16)

    tab = _full((IN_PAD, LANES))
    return pl.pallas_call(
        body, name="assemble_w_in", grid=(D // tr,),
        in_specs=[pl.BlockSpec((NDEV, tr, IN_MAIN), lambda i: (0, i, 0)), _rows(tr, LANES), tab, tab, tab],
        out_specs=_rows(tr, WP),
        out_shape=jax.ShapeDtypeStruct((D, WP), BF16),
        compiler_params=_cparams(48, ("arbitrary",)),
    )(blocks, last_cols, dest, dest_f, scale)


def _disassemble_w_in(dwp, tables, *, tr):
    starts, f_shards, dest, dest_f, scale = tables
    width = dwp.shape[1]

    def body(g_ref, dest_ref, destf_ref, scale_ref, o_ref):
        for k in range(NDEV):
            ws = starts[k]
            acc = lax.dot_general(g_ref[:, ws:ws + WIN], _perm(dest_ref, scale_ref, k, WIN), NT, preferred_element_type=F32)
            if k in f_shards:
                acc = acc + lax.dot_general(g_ref[:, OFF_F:WP], _perm(destf_ref, scale_ref, k, 128), NT,
                                            preferred_element_type=F32)
            o_ref[k] = acc.astype(BF16)

    tab = _full((IN_PAD, LANES))
    return pl.pallas_call(
        body, name="disassemble_w_in", grid=(D // tr,),
        in_specs=[_rows(tr, width), tab, tab, tab],
        out_specs=pl.BlockSpec((NDEV, tr, IN_PAD), lambda i: (0, i, 0)),
        out_shape=jax.ShapeDtypeStruct((NDEV, D, IN_PAD), BF16),
        compiler_params=_cparams(48, ("arbitrary",)),
    )(dwp, dest, dest_f, scale)


def _in_proj(x, g1, wp, bfp, pq, pk, oq, ok, *, tm):
    s = x.shape[0]

    def body(x_ref, g_ref, w_ref, bf_ref, pq_ref, pk_ref, oq_ref, ok_ref,
             ht_ref, qp_ref, kp_ref, v_ref, bcu_ref, z_ref, carry):
        @pl.when(pl.program_id(0) == 0)
        def _():
            carry[...] = jnp.zeros_like(carry)

        h = _rms_fwd(x_ref[...], g_ref[...])[0].astype(BF16)
        ht_ref[...] = h.T
        z = jnp.dot(h, w_ref[:, OFF_F:WP], preferred_element_type=F32) + bf_ref[...]
        z_ref[...] = z
        lane = lax.broadcasted_iota(jnp.int32, (tm, 128), 1)
        logf = jnp.where(lane < H, jnp.minimum(z, 0.0) - jnp.log(1.0 + jnp.exp(-jnp.abs(z))), 0.0)
        row = lax.broadcasted_iota(jnp.int32, (tm, tm), 0)
        col = lax.broadcasted_iota(jnp.int32, (tm, tm), 1)
        tri = (col <= row).astype(BF16)
        c = _exact_dot01(tri, logf) + carry[0:1, :]
        carry[...] = jnp.broadcast_to(c[tm - 1:tm, :], carry.shape)
        cb = c * LOG2E
        c1 = cb.astype(BF16).astype(F32)
        r1 = cb - c1
        c2 = r1.astype(BF16).astype(F32)
        c3 = (r1 - c2).astype(BF16).astype(F32)
        zc = (c1 + pltpu.roll(c2, 8, axis=1) + pltpu.roll(c3, 16, axis=1)).astype(BF16)

        def pad_heads(v):
            blocks = []
            for pair in range(H // 2):
                two = v[:, 128 * pair:128 * (pair + 1)]
                blocks.append(jnp.where(lane < DH, two, 0.0))
                blocks.append(jnp.where(lane < DH, pltpu.roll(two, DH, axis=1), 0.0))
            return jnp.concatenate(blocks, axis=1)

        q = jnp.dot(h, w_ref[:, OFF_Q:OFF_K], preferred_element_type=F32) * LOG2E
        qp_ref[...] = (pad_heads(q) + jnp.dot(zc, pq_ref[...], preferred_element_type=F32) + oq_ref[...]).astype(BF16)
        k = jnp.dot(h, w_ref[:, OFF_K:OFF_V], preferred_element_type=F32)
        kp_ref[...] = (pad_heads(k) + jnp.dot(zc, pk_ref[...], preferred_element_type=F32) + ok_ref[...]).astype(BF16)
        v = pad_heads(jnp.dot(h, w_ref[:, OFF_V:OFF_BCU], preferred_element_type=F32))
        ones_lane = lax.broadcasted_iota(jnp.int32, (tm, H * HP), 1) % HP == DH
        v_ref[...] = jnp.where(ones_lane, 1.0, v).astype(BF16)
        bcu_ref[...] = jnp.dot(h, w_ref[:, OFF_BCU:OFF_F], preferred_element_type=F32).astype(BF16)

    return pl.pallas_call(
        body, name="in_proj", grid=(s // tm,),
        in_specs=[_rows(tm, D), _full((1, D)), _resident((D, WP)), _full((1, 128)),
                  _full((128, 1024)), _full((128, 1024)), _full((1, 1024)), _full((1, 1024))],
        out_specs=[pl.BlockSpec((D, tm), lambda i: (0, i)), _rows(tm, 1024), _rows(tm, 1024), _rows(tm, 1024),
                   _rows(tm, 3 * CW), _rows(tm, 128)],
        out_shape=[jax.ShapeDtypeStruct((D, s), BF16), jax.ShapeDtypeStruct((s, 1024), BF16),
                   jax.ShapeDtypeStruct((s, 1024), BF16), jax.ShapeDtypeStruct((s, 1024), BF16),
                   jax.ShapeDtypeStruct((s, 3 * CW), BF16), jax.ShapeDtypeStruct((s, 128), F32)],
        scratch_shapes=[pltpu.VMEM((SUBLANES, 128), F32)],
        compiler_params=_cparams(56, ("arbitrary",)),
    )(x, g1, wp, bfp, pq, pk, oq, ok)


def _attn_fwd(qp, kp, v, *, t):
    s = qp.shape[0]
    nq = s // t

    def body(q_ref, k_ref, v_ref, o_ref, lse_ref, mk_ref):
        pi = pl.program_id(1)
        row = lax.broadcasted_iota(jnp.int32, (t, t), 0)
        col = lax.broadcasted_iota(jnp.int32, (t, t), 1)
        lane = lax.broadcasted_iota(jnp.int32, (t, 128), 1)

        def head_step(hh, rows, ki, carry, masked):
            m, acc = carry
            off = pl.multiple_of(ki * t, t)
            q = q_ref[rows, HP * hh:HP * (hh + 1)]
            k = k_ref[pl.ds(off, t), HP * hh:HP * (hh + 1)]
            sc = lax.dot_general(q, k, NT, preferred_element_type=F32)
            if masked:
                sc = jnp.where(col <= row, sc, -1e30)
            mn = jnp.maximum(m, jnp.max(sc, axis=-1, keepdims=True))
            p = jnp.exp2(sc - mn).astype(BF16)
            acc = jnp.exp2(m - mn) * acc + jnp.dot(p, v_ref[pl.ds(off, t), HP * hh:HP * (hh + 1)],
                                                  preferred_element_type=F32)
            return mn, acc

        def step(rows, ki, carry, masked):
            new = tuple(head_step(hh, rows, ki, carry[hh], masked) for hh in range(2))
            mk_ref[ki, rows] = jnp.where(lane < DH, jnp.broadcast_to(new[0][0], (t, 128)),
                                         jnp.broadcast_to(new[1][0], (t, 128)))
            return new

        init = (jnp.full((t, 1), -1e30, F32), jnp.zeros((t, 128), F32))
        top, bottom = slice(0, t), slice(t, 2 * t)

        def quad(j, carry):
            c0, c1 = carry
            c0 = step(top, 2 * j, c0, False)
            c1 = step(bottom, 2 * j, c1, False)
            c0 = step(top, 2 * j + 1, c0, False)
            c1 = step(bottom, 2 * j + 1, c1, False)
            return c0, c1

        c0, c1 = lax.fori_loop(0, pi, quad, ((init, init), (init, init)))
        f0 = step(top, 2 * pi, c0, True)
        c1 = step(bottom, 2 * pi, c1, False)
        f1 = step(bottom, 2 * pi + 1, c1, True)
        for rows, ((m0, acc0), (m1, acc1)) in ((top, f0), (bottom, f1)):
            l0, l1 = acc0[:, DH:DH + 1], acc1[:, DH:DH + 1]
            o_ref[rows, :] = jnp.where(lane < DH, acc0 / l0, pltpu.roll(acc1 / l1, DH, axis=1))
            lse_ref[rows, :] = jnp.where(lane < DH, jnp.broadcast_to(m0 + jnp.log2(l0), (t, 128)),
                                         jnp.broadcast_to(m1 + jnp.log2(l1), (t, 128)))

    return pl.pallas_call(
        body, name="attn_fwd", grid=(H // 2, nq // 2),
        in_specs=[pl.BlockSpec((2 * t, 2 * HP), lambda p, i: (i, p)),
                  pl.BlockSpec((s, 2 * HP), lambda p, i: (0, p)),
                  pl.BlockSpec((s, 2 * HP), lambda p, i: (0, p))],
        out_specs=[pl.BlockSpec((2 * t, 128), lambda p, i: (i, p)), pl.BlockSpec((2 * t, 128), lambda p, i: (i, p)),
                   pl.BlockSpec((nq, 2 * t, 128), lambda p, i: (0, i, p))],
        out_shape=[jax.ShapeDtypeStruct((s, AW), F32), jax.ShapeDtypeStruct((s, AW), F32),
                   jax.ShapeDtypeStruct((nq, s, AW), F32)],
        compiler_params=_cparams(48, ("arbitrary", "arbitrary")),
    )(qp, kp, v)


HALO = 16


def _conv_taps(bcu_ref, halo_ref, first, tm):
    z = bcu_ref[:, CW:2 * CW].astype(F32) * bcu_ref[:, 2 * CW:3 * CW].astype(F32)
    zh = jnp.where(first, 0.0, halo_ref[:, CW:2 * CW].astype(F32) * halo_ref[:, 2 * CW:3 * CW].astype(F32))
    row = lax.broadcasted_iota(jnp.int32, (tm, CW), 0)
    last, before = zh[HALO - 1:HALO, :], zh[HALO - 2:HALO - 1, :]
    z1 = jnp.where(row == 0, last, pltpu.roll(z, 1, axis=0))
    z2 = jnp.where(row == 0, before, jnp.where(row == 1, last, pltpu.roll(z, 2, axis=0)))
    return z, z1, z2


def _halo_before(tm, width):
    return pl.BlockSpec((HALO, width), lambda i: (jnp.maximum(i * (tm // HALO) - 1, 0), 0))


def _mix_out(o, bcu, cw8, ga, gc, gsum, w_out, x, g_post, g_ffn_pre, *, tm):
    s = x.shape[0]

    def body(o_ref, bcu_ref, halo_ref, cw_ref, ga_ref, gc_ref, gs_ref, w_ref, x_ref, g_ref, gf_ref,
             merged_ref, y_ref, x2_ref, cv_ref, h2_ref):
        z, z1, z2 = _conv_taps(bcu_ref, halo_ref, pl.program_id(0) == 0, tm)
        cv = cw_ref[0:1, :] * z2 + cw_ref[1:2, :] * z1 + cw_ref[2:3, :] * z
        cv_ref[...] = cv
        conv = bcu_ref[:, 0:CW].astype(F32) * cv
        ov = o_ref[...]
        ra = lax.rsqrt(_group_sum(ov * ov, gs_ref[...]) * (1.0 / DH) + EPS)
        rc = lax.rsqrt(_group_sum(conv * conv, gs_ref[...]) * (1.0 / DH) + EPS)
        merged = jnp.concatenate([ov * ra * ga_ref[...], conv * rc * gc_ref[...]], axis=1).astype(BF16)
        merged_ref[...] = merged
        y = jnp.dot(merged, w_ref[...], preferred_element_type=F32)
        y_ref[...] = y
        x2 = x_ref[...] + _rms_fwd(y, g_ref[...])[0]
        x2_ref[...] = x2
        h2_ref[...] = _rms_fwd(x2, gf_ref[...])[0].astype(BF16)

    return pl.pallas_call(
        body, name="mix_out", grid=(s // tm,),
        in_specs=[_rows(tm, AW), _rows(tm, 3 * CW), _halo_before(tm, 3 * CW), _full((SUBLANES, CW)),
                  _full((1, AW)), _full((1, CW)), _full((GS, GS)), _resident((D, D)), _rows(tm, D), _full((1, D)),
                  _full((1, D))],
        out_specs=[_rows(tm, D), _rows(tm, D), _rows(tm, D), _rows(tm, CW), _rows(tm, D)],
        out_shape=[jax.ShapeDtypeStruct((s, D), BF16), jax.ShapeDtypeStruct((s, D), F32),
                   jax.ShapeDtypeStruct((s, D), F32), jax.ShapeDtypeStruct((s, CW), F32),
                   jax.ShapeDtypeStruct((s, D), BF16)],
        compiler_params=_cparams(48, ("arbitrary",)),
    )(o, bcu, bcu, cw8, ga, gc, gsum, w_out, x, g_post, g_ffn_pre)


def _ffn_fwd_loss(h2, wgu, wd, x2, target, g_post, *, tm):
    s = x2.shape[0]

    def body(h_ref, w_ref, wd_ref, x2_ref, t_ref, g_ref,
             gate_ref, up_ref, a_ref, dx3_ref, dff_ref, loss_ref, dg_ref):
        @pl.when(pl.program_id(0) == 0)
        def _():
            loss_ref[...] = jnp.zeros_like(loss_ref)
            dg_ref[...] = jnp.zeros_like(dg_ref)

        h = h_ref[...]
        ff = None
        for c0, n in FF_CHUNKS:
            cols = slice(c0, c0 + n)
            gate = lax.dot_general(h, w_ref[0, cols, :], NT, preferred_element_type=F32)
            up = lax.dot_general(h, w_ref[1, cols, :], NT, preferred_element_type=F32)
            gate_ref[:, cols] = gate.astype(BF16)
            up_ref[:, cols] = up.astype(BF16)
            act = (gate * jax.nn.sigmoid(gate) * up).astype(BF16)
            a_ref[:, cols] = act
            part = jnp.dot(act, wd_ref[cols, :], preferred_element_type=F32)
            ff = part if ff is None else ff + part
        out, n, r = _rms_fwd(ff, g_ref[...])
        e = x2_ref[...] + out - t_ref[...]
        loss_ref[...] += _fold8(e * e)
        dx3 = e * (1.0 / D)
        dx3_ref[...] = dx3
        dff, dg = _rms_bwd(dx3, n, r, g_ref[...])
        dff_ref[...] = dff.astype(BF16)
        dg_ref[...] += _fold8(dg)

    wide = _rows(tm, DFF)
    return pl.pallas_call(
        body, name="ffn_fwd_loss", grid=(s // tm,),
        in_specs=[_rows(tm, D), _resident((2, DFF, D)), _resident((DFF, D)), _rows(tm, D), _rows(tm, D), _full((1, D))],
        out_specs=[wide, wide, wide, _rows(tm, D), _rows(tm, D), _full((SUBLANES, D)), _full((SUBLANES, D))],
        out_shape=[jax.ShapeDtypeStruct((s, DFF), BF16)] * 3
        + [jax.ShapeDtypeStruct((s, D), F32), jax.ShapeDtypeStruct((s, D), BF16),
           jax.ShapeDtypeStruct((SUBLANES, D), F32), jax.ShapeDtypeStruct((SUBLANES, D), F32)],
        compiler_params=_cparams(56, ("arbitrary",)),
    )(h2, wgu, wd, x2, target, g_post)


def _ffn_bwd(dff, wd, gate, up, wgu, x2, g_pre, dx3, y, g_post, *, tm):
    s = x2.shape[0]

    def body(dff_ref, wd_ref, gate_ref, up_ref, w_ref, x2_ref, gpre_ref, dx3_ref, y_ref, gpost_ref,
             dgu_ref, dx2_ref, dy_ref, dgpre_ref, dgpost_ref):
        @pl.when(pl.program_id(0) == 0)
        def _():
            dgpre_ref[...] = jnp.zeros_like(dgpre_ref)
            dgpost_ref[...] = jnp.zeros_like(dgpost_ref)

        dff = dff_ref[...]
        dh2 = None
        for c0, n in FF_CHUNKS:
            cols = slice(c0, c0 + n)
            da = lax.dot_general(dff, wd_ref[cols, :], NT, preferred_element_type=F32)
            g = gate_ref[:, cols].astype(F32)
            sg = jax.nn.sigmoid(g)
            dgate = (da * up_ref[:, cols].astype(F32) * (sg * (1.0 + g * (1.0 - sg)))).astype(BF16)
            dup = (da * (g * sg)).astype(BF16)
            dgu_ref[:, cols] = dgate
            dgu_ref[:, DFF + c0:DFF + c0 + n] = dup
            part = (jnp.dot(dgate, w_ref[0, cols, :], preferred_element_type=F32)
                    + jnp.dot(dup, w_ref[1, cols, :], preferred_element_type=F32))
            dh2 = part if dh2 is None else dh2 + part
        _, n2, r2 = _rms_fwd(x2_ref[...], gpre_ref[...])
        dxn, dg = _rms_bwd(dh2, n2, r2, gpre_ref[...])
        dgpre_ref[...] += _fold8(dg)
        dx2 = dx3_ref[...] + dxn
        dx2_ref[...] = dx2
        _, ny, ry = _rms_fwd(y_ref[...], gpost_ref[...])
        dy, dg2 = _rms_bwd(dx2, ny, ry, gpost_ref[...])
        dy_ref[...] = dy.astype(BF16)
        dgpost_ref[...] += _fold8(dg2)

    wide = _rows(tm, DFF)
    return pl.pallas_call(
        body, name="ffn_bwd", grid=(s // tm,),
        in_specs=[_rows(tm, D), _resident((DFF, D)), wide, wide, _resident((2, DFF, D)), _rows(tm, D), _full((1, D)),
                  _rows(tm, D), _rows(tm, D), _full((1, D))],
        out_specs=[_rows(tm, 2 * DFF), _rows(tm, D), _rows(tm, D),
                   _full((SUBLANES, D)), _full((SUBLANES, D))],
        out_shape=[jax.ShapeDtypeStruct((s, 2 * DFF), BF16), jax.ShapeDtypeStruct((s, D), F32),
                   jax.ShapeDtypeStruct((s, D), BF16), jax.ShapeDtypeStruct((SUBLANES, D), F32),
                   jax.ShapeDtypeStruct((SUBLANES, D), F32)],
        compiler_params=_cparams(56, ("arbitrary",)),
    )(dff, wd, gate, up, wgu, x2, g_pre, dx3, y, g_post)


def _grad_matmul(a, b, *, ta, tb, ts, name, vmem_mb=48):
    s, ka = a.shape
    nb = b.shape[1]
    ts = min(ts, s)
    nk = s // ts

    def body(a_ref, b_ref, o_ref, *acc):
        if nk == 1:
            o_ref[...] = lax.dot_general(a_ref[...], b_ref[...], TN, preferred_element_type=F32).astype(BF16)
            return
        k = pl.program_id(2)

        @pl.when(k == 0)
        def _():
            acc[0][...] = jnp.zeros_like(acc[0])

        acc[0][...] += lax.dot_general(a_ref[...], b_ref[...], TN, preferred_element_type=F32)

        @pl.when(k == nk - 1)
        def _():
            o_ref[...] = acc[0][...].astype(BF16)

    whole_b = {"pipeline_mode": pl.Buffered(1)} if nk == 1 and nb == tb else {}
    return pl.pallas_call(
        body, name=name, grid=(ka // ta, nb // tb, nk),
        in_specs=[pl.BlockSpec((ts, ta), lambda i, j, k: (k, i)),
                  pl.BlockSpec((ts, tb), lambda i, j, k: (k, j), **whole_b)],
        out_specs=pl.BlockSpec((ta, tb), lambda i, j, k: (i, j)),
        out_shape=jax.ShapeDtypeStruct((ka, nb), BF16),
        scratch_shapes=[pltpu.VMEM((ta, tb), F32)] if nk > 1 else [],
        compiler_params=_cparams(vmem_mb, ("arbitrary", "arbitrary", "arbitrary")),
    )(a, b)


GW_TILE = 256


def _grad_w_in(h1t, pieces):
    ka, s = h1t.shape
    widths = [p.shape[1] for p in pieces]
    assert all(w % GW_TILE == 0 for w in widths)
    first = [sum(widths[:i]) // GW_TILE for i in range(len(pieces))]
    count = [w // GW_TILE for w in widths]

    def body(a_ref, *refs):
        o_ref = refs[-1]
        j = pl.program_id(0)
        for ref, f0, n in zip(refs[:-1], first, count):
            @pl.when((j >= f0) & (j < f0 + n))
            def _(ref=ref):
                o_ref[...] = jnp.dot(a_ref[...], ref[...], preferred_element_type=F32).astype(BF16)

    def spec(f0, n):
        return pl.BlockSpec((s, GW_TILE), lambda j: (0, jnp.clip(j - f0, 0, n - 1)))

    return pl.pallas_call(
        body, name="grad_w_in", grid=(sum(count),),
        in_specs=[_resident((ka, s))] + [spec(f0, n) for f0, n in zip(first, count)],
        out_specs=pl.BlockSpec((ka, GW_TILE), lambda j: (0, j)),
        out_shape=jax.ShapeDtypeStruct((ka, sum(widths)), BF16),
        compiler_params=_cparams(56, ("arbitrary",)),
    )(h1t, *pieces)


def _mix_bwd(dy, w_out, o, cv, bcu, ga, gc, gsum, after, *, tm):
    s = dy.shape[0]

    def group_norm_bwd(dn_out, v, g, gs):
        r = lax.rsqrt(_group_sum(v * v, gs) * (1.0 / DH) + EPS)
        n = v * r
        dn = dn_out * g
        return r * (dn - n * (_group_sum(dn * n, gs) * (1.0 / DH))), dn_out * n

    def body(dy_ref, w_ref, o_ref, cv_ref, bcu_ref, ga_ref, gc_ref, gs_ref, after_ref,
             do_ref, dl_ref, dcv_ref, db_ref, dga_ref, dgc_ref):
        @pl.when(pl.program_id(0) == 0)
        def _():
            dga_ref[...] = jnp.zeros_like(dga_ref)
            dgc_ref[...] = jnp.zeros_like(dgc_ref)

        dm = lax.dot_general(dy_ref[...], w_ref[...], NT, preferred_element_type=F32)
        ov = o_ref[...]
        do, dga = group_norm_bwd(dm[:, 0:AW], ov, ga_ref[...], gs_ref[...])
        dob = do.astype(BF16)
        do_ref[...] = dob
        dl_ref[...] = _group_sum(dob.astype(F32) * ov, gs_ref[...])
        dga_ref[...] += _fold8(dga)
        gate_b = bcu_ref[:, 0:CW].astype(F32)
        cv = cv_ref[...]
        dconv, dgc = group_norm_bwd(dm[:, AW:D], gate_b * cv, gc_ref[...], gs_ref[...])
        dgc_ref[...] += _fold8(dgc)
        dcv_ref[...] = dconv * gate_b
        db_ref[...] = (dconv * cv).astype(BF16)

    return pl.pallas_call(
        body, name="mix_bwd", grid=(s // tm,),
        in_specs=[_rows(tm, D), _resident((D, D)), _rows(tm, AW), _rows(tm, CW), _rows(tm, 3 * CW),
                  _full((1, AW)), _full((1, CW)), _full((GS, GS)), ANY],
        out_specs=[_rows(tm, AW), _rows(tm, AW), _rows(tm, CW), _rows(tm, CW),
                   _full((SUBLANES, AW)), _full((SUBLANES, CW))],
        out_shape=[jax.ShapeDtypeStruct((s, AW), BF16), jax.ShapeDtypeStruct((s, AW), F32),
                   jax.ShapeDtypeStruct((s, CW), F32), jax.ShapeDtypeStruct((s, CW), BF16),
                   jax.ShapeDtypeStruct((SUBLANES, AW), F32), jax.ShapeDtypeStruct((SUBLANES, CW), F32)],
        compiler_params=_cparams(48, ("arbitrary",)),
    )(dy, w_out, o, cv, bcu, ga, gc, gsum, after)


def _conv_bwd(dcv, db, bcu, cw8, *, tm):
    s = dcv.shape[0]
    nt = s // tm

    def body(dcv_ref, nxt_ref, db_ref, bcu_ref, halo_ref, cw_ref, dbcu_ref, dw_ref):
        i = pl.program_id(0)

        @pl.when(i == 0)
        def _():
            dw_ref[...] = jnp.zeros_like(dw_ref)

        z, z1, z2 = _conv_taps(bcu_ref, halo_ref, i == 0, tm)
        d = dcv_ref[...]
        dw_ref[0] += _fold8(d * z2)
        dw_ref[1] += _fold8(d * z1)
        dw_ref[2] += _fold8(d * z)
        nx = jnp.where(i == nt - 1, 0.0, nxt_ref[...])
        row = lax.broadcasted_iota(jnp.int32, (tm, CW), 0)
        d1 = jnp.where(row == tm - 1, nx[0:1, :], pltpu.roll(d, tm - 1, axis=0))
        d2 = jnp.where(row == tm - 2, nx[0:1, :], jnp.where(row == tm - 1, nx[1:2, :], pltpu.roll(d, tm - 2, axis=0)))
        dz = cw_ref[2:3, :] * d + cw_ref[1:2, :] * d1 + cw_ref[0:1, :] * d2
        dbcu_ref[:, 0:CW] = db_ref[...]
        dbcu_ref[:, CW:2 * CW] = (dz * bcu_ref[:, 2 * CW:3 * CW].astype(F32)).astype(BF16)
        dbcu_ref[:, 2 * CW:3 * CW] = (dz * bcu_ref[:, CW:2 * CW].astype(F32)).astype(BF16)

    return pl.pallas_call(
        body, name="conv_bwd", grid=(nt,),
        in_specs=[_rows(tm, CW),
                  pl.BlockSpec((SUBLANES, CW), lambda i: (jnp.minimum((i + 1) * (tm // SUBLANES), s // SUBLANES - 1), 0)),
                  _rows(tm, CW), _rows(tm, 3 * CW), _halo_before(tm, 3 * CW), _full((SUBLANES, CW))],
        out_specs=[_rows(tm, 3 * CW), _full((3, SUBLANES, CW))],
        out_shape=[jax.ShapeDtypeStruct((s, 3 * CW), BF16), jax.ShapeDtypeStruct((3, SUBLANES, CW), F32)],
        compiler_params=_cparams(48, ("arbitrary",)),
    )(dcv, dcv, db, bcu, bcu, cw8)


def _attn_bwd(qp, kp, v, do, lse, dl, mk, *, t):
    s = qp.shape[0]
    nq = s // t

    def body(q_ref, k_ref, v_ref, do_ref, lse_ref, dl_ref, mk_ref, dq_ref, dk_ref, dv_ref, dkx_ref, dq_acc):
        pi = pl.program_id(1)

        @pl.when(pi == 0)
        def _():
            dq_acc[...] = jnp.zeros_like(dq_acc)

        row = lax.broadcasted_iota(jnp.int32, (t, t), 0)
        col = lax.broadcasted_iota(jnp.int32, (t, t), 1)
        lane = lax.broadcasted_iota(jnp.int32, (t, 128), 1)

        def head_step(hh, qi, carry, modes):
            off = pl.multiple_of(qi * t, t)
            rows = pl.ds(off, t)
            q = q_ref[rows, HP * hh:HP * (hh + 1)]
            qt = q.T
            lse_col = lse_ref[rows, DH * hh:DH * hh + 1]
            dl_col = dl_ref[rows, DH * hh:DH * hh + 1]
            do2 = do_ref[rows, :]
            dom = jnp.where(lane < DH, do2 if hh == 0 else pltpu.roll(do2, DH, axis=1), jnp.zeros((), BF16))
            new, dss = [], []
            for half, masked in enumerate(modes):
                if masked is None:
                    new.append(carry[half])
                    continue
                dk, dv, cs = carry[half]
                keys = slice(half * t, (half + 1) * t)
                m_col = mk_ref[half, rows, DH * hh:DH * hh + 1]
                scale = jnp.exp2(m_col - lse_col)
                sc = lax.dot_general(q, k_ref[keys, HP * hh:HP * (hh + 1)], NT, preferred_element_type=F32) - m_col
                if masked:
                    sc = jnp.where(col <= row, sc, -1e30)
                pt = jnp.exp2(sc).astype(BF16)
                dp = lax.dot_general(dom, v_ref[keys, HP * hh:HP * (hh + 1)], NT, preferred_element_type=F32)
                ds32 = (pt.astype(F32) * scale) * (dp - dl_col)
                ds = ds32.astype(BF16)
                cs = cs + _fold8(ds32)
                dv = dv + jnp.dot((dom.astype(F32) * scale).astype(BF16).T, pt, preferred_element_type=F32)
                dk = dk + jnp.dot(qt, ds, preferred_element_type=F32)
                new.append((dk, dv, cs))
                dss.append((half, ds))
            if len(dss) == 2:
                dq = jnp.dot(jnp.concatenate([dss[0][1], dss[1][1]], axis=1), k_ref[:, HP * hh:HP * (hh + 1)],
                             preferred_element_type=F32)
            else:
                half, ds = dss[0]
                dq = jnp.dot(ds, k_ref[half * t:(half + 1) * t, HP * hh:HP * (hh + 1)], preferred_element_type=F32)
            dq_acc[rows, HP * hh:HP * (hh + 1)] += dq
            return tuple(new)

        def step(qi, carry, modes):
            return tuple(head_step(hh, qi, carry[hh], modes) for hh in range(2))

        def two_heads(a0, a1):
            return jnp.where(lane < DH, a0, pltpu.roll(a1, DH, axis=1))

        def rows_to_lanes(a0, a1):
            return jnp.concatenate([a0, a1], axis=0).T

        zero = (jnp.zeros((HP, t), F32), jnp.zeros((128, t), F32), jnp.zeros((SUBLANES, t), F32))
        carry = step(2 * pi, ((zero, zero), (zero, zero)), (True, None))
        carry = step(2 * pi + 1, carry, (False, True))

        def pair(j, carry):
            qi = 2 * (pi + 1 + j)
            return step(qi + 1, step(qi, carry, (False, False)), (False, False))

        carry = lax.fori_loop(0, nq // 2 - 1 - pi, pair, carry)
        for half in range(2):
            keys = slice(half * t, (half + 1) * t)
            (dk0, dv0, cs0), (dk1, dv1, cs1) = carry[0][half], carry[1][half]
            dk_ref[keys, :] = (rows_to_lanes(dk0[0:DH], dk1[0:DH]) * LN2).astype(BF16)
            dv_ref[keys, :] = rows_to_lanes(dv0[0:DH], dv1[0:DH]).astype(BF16)
            total = lambda cs: jnp.broadcast_to(jnp.sum(cs, axis=0, keepdims=True), (DH, t))
            dkx_ref[keys, :] = rows_to_lanes(total(cs0), total(cs1))

        @pl.when(pi == nq // 2 - 1)
        def _():
            for c in range(s // t):
                rows = slice(c * t, (c + 1) * t)
                dq_ref[rows, :] = two_heads(dq_acc[rows, 0:HP], dq_acc[rows, HP:2 * HP]).astype(BF16)

    return pl.pallas_call(
        body, name="attn_bwd", grid=(H // 2, nq // 2),
        in_specs=[pl.BlockSpec((s, 2 * HP), lambda p, i: (0, p)),
                  pl.BlockSpec((2 * t, 2 * HP), lambda p, i: (i, p)),
                  pl.BlockSpec((2 * t, 2 * HP), lambda p, i: (i, p)),
                  pl.BlockSpec((s, 128), lambda p, i: (0, p)),
                  pl.BlockSpec((s, 128), lambda p, i: (0, p)),
                  pl.BlockSpec((s, 128), lambda p, i: (0, p)),
                  pl.BlockSpec((2, s, 128), lambda p, i: (i, 0, p))],
        out_specs=[pl.BlockSpec((s, 128), lambda p, i: (0, p)),
                   pl.BlockSpec((2 * t, 128), lambda p, i: (i, p)),
                   pl.BlockSpec((2 * t, 128), lambda p, i: (i, p)),
                   pl.BlockSpec((2 * t, 128), lambda p, i: (i, p))],
        out_shape=[jax.ShapeDtypeStruct((s, AW), BF16), jax.ShapeDtypeStruct((s, AW), BF16),
                   jax.ShapeDtypeStruct((s, AW), BF16), jax.ShapeDtypeStruct((s, AW), F32)],
        scratch_shapes=[pltpu.VMEM((s, 2 * HP), F32)],
        compiler_params=_cparams(56, ("arbitrary", "arbitrary")),
    )(qp, kp, v, do, lse, dl, mk)


def _forget_bwd(dkx, z, sel, *, tm):
    s = dkx.shape[0]
    nt = s // tm

    def body(dk_ref, z_ref, sel_ref, dfl_ref, dbf_ref, carry):
        @pl.when(pl.program_id(0) == 0)
        def _():
            carry[...] = jnp.zeros_like(carry)
            dbf_ref[...] = jnp.zeros_like(dbf_ref)

        dc = _split_dot(dk_ref[...], sel_ref[...])
        row = lax.broadcasted_iota(jnp.int32, (tm, tm), 0)
        col = lax.broadcasted_iota(jnp.int32, (tm, tm), 1)
        tri = (col >= row).astype(BF16)
        dlogf = _exact_dot01(tri, dc) + carry[0:1, :]
        carry[...] = jnp.broadcast_to(dlogf[0:1, :], carry.shape)
        dz = dlogf * (1.0 - jax.nn.sigmoid(z_ref[...]))
        dfl_ref[:, 0:128] = dz.astype(BF16)
        dfl_ref[:, 128:GW_TILE] = jnp.zeros((tm, GW_TILE - 128), BF16)
        dbf_ref[...] += _fold8(dz)

    rev = lambda i: (nt - 1 - i, 0)
    return pl.pallas_call(
        body, name="forget_bwd", grid=(nt,),
        in_specs=[pl.BlockSpec((tm, AW), rev), pl.BlockSpec((tm, 128), rev), _full((AW, 128))],
        out_specs=[pl.BlockSpec((tm, GW_TILE), rev), _full((SUBLANES, 128))],
        out_shape=[jax.ShapeDtypeStruct((s, GW_TILE), BF16), jax.ShapeDtypeStruct((SUBLANES, 128), F32)],
        scratch_shapes=[pltpu.VMEM((SUBLANES, 128), F32)],
        compiler_params=_cparams(48, ("arbitrary",)),
    )(dkx, z, sel)


def _in_proj_bwd(pieces, wp, x, g1, dx2, after, *, tm):
    s = x.shape[0]

    def body(q_ref, k_ref, v_ref, bcu_ref, f_ref, w_ref, x_ref, g_ref, dx2_ref, after_ref, dx_ref, dg_ref):
        @pl.when(pl.program_id(0) == 0)
        def _():
            dg_ref[...] = jnp.zeros_like(dg_ref)

        dh = None
        for ref, (lo, hi) in zip((q_ref, k_ref, v_ref, bcu_ref, f_ref), PIECES):
            part = lax.dot_general(ref[...], w_ref[:, lo:hi], NT, preferred_element_type=F32)
            dh = part if dh is None else dh + part
        _, n, r = _rms_fwd(x_ref[...], g_ref[...])
        dxn, dg = _rms_bwd(dh, n, r, g_ref[...])
        dx_ref[...] = dx2_ref[...] + dxn
        dg_ref[...] += _fold8(dg)

    return pl.pallas_call(
        body, name="in_proj_bwd", grid=(s // tm,),
        in_specs=[_rows(tm, hi - lo) for lo, hi in PIECES]
        + [_resident((D, WP)), _rows(tm, D), _full((1, D)), _rows(tm, D), ANY],
        out_specs=[_rows(tm, D), _full((SUBLANES, D))],
        out_shape=[jax.ShapeDtypeStruct((s, D), F32), jax.ShapeDtypeStruct((SUBLANES, D), F32)],
        compiler_params=_cparams(56, ("arbitrary",)),
    )(*pieces, wp, x, g1, dx2, after)


def _position():
    return lax.axis_index("x"), lax.axis_index("y"), lax.axis_index("c")


ANY = pl.BlockSpec(memory_space=pl.ANY)


def _all_gather(shards):
    n = len(shards)

    def body(*refs):
        x_refs, out_refs = refs[:n], refs[n:2 * n]
        send_sems, recv_sems, local_sems = refs[2 * n:]
        x, y, c = _position()
        me, sibling = (x, y, c), (x, y, 1 - c)
        chips = [(1 - x, y), (x, 1 - y), (1 - x, 1 - y)]

        def copy(a, k, block, to, own=False):
            slot = out_refs[a].at[4 * block[0] + 2 * block[1] + block[2]]
            return pltpu.make_async_remote_copy(
                src_ref=x_refs[a] if own else slot, dst_ref=slot,
                send_sem=send_sems.at[7 * a + k], recv_sem=recv_sems.at[7 * a + k], device_id=to, device_id_type=MESH_ID)

        mine = [pltpu.make_async_copy(x_refs[a], out_refs[a].at[4 * x + 2 * y + c], local_sems.at[a]) for a in range(n)]
        for cp in mine:
            cp.start()
        first = []
        for a in range(n):
            first.append(copy(a, 0, me, sibling, own=True))
            first += [copy(a, 1 + j, me, (*chip, c), own=True) for j, chip in enumerate(chips)]
        for cp in first:
            cp.start()
        passed = []
        for j, chip in enumerate(chips):
            for a in range(n):
                copy(a, 1 + j, (*chip, c), me).wait_recv()
                fwd = copy(a, 4 + j, (*chip, c), sibling)
                fwd.start()
                passed.append(fwd)
        for a in range(n):
            copy(a, 0, sibling, me).wait_recv()
            for j, chip in enumerate(chips):
                copy(a, 4 + j, (*chip, 1 - c), me).wait_recv()
        for cp in first + passed:
            cp.wait_send()
        for cp in mine:
            cp.wait()

    return pl.pallas_call(
        body, name="all_gather_weights",
        out_shape=[jax.ShapeDtypeStruct((NDEV,) + sh.shape, sh.dtype) for sh in shards],
        in_specs=[ANY] * n, out_specs=[ANY] * n,
        scratch_shapes=[pltpu.SemaphoreType.DMA((7 * n,)), pltpu.SemaphoreType.DMA((7 * n,)), pltpu.SemaphoreType.DMA((n,))],
    )(*shards)


def _pair_exchange(grads):
    n = len(grads)

    def body(*refs):
        g_refs, out_refs = refs[:n], refs[n:2 * n]
        send_sems, recv_sems = refs[2 * n:]
        x, y, c = _position()
        copies = [pltpu.make_async_remote_copy(
            src_ref=g_refs[a].at[:, pl.ds(1 - c, 1)], dst_ref=out_refs[a], send_sem=send_sems.at[a],
            recv_sem=recv_sems.at[a], device_id=(x, y, 1 - c), device_id_type=MESH_ID) for a in range(n)]
        for cp in copies:
            cp.start()
        for cp in copies:
            cp.wait()

    return pl.pallas_call(
        body, name="grad_pair_exchange",
        out_shape=[jax.ShapeDtypeStruct((4, 1) + g.shape[2:], g.dtype) for g in grads],
        in_specs=[ANY] * n, out_specs=[ANY] * n,
        scratch_shapes=[pltpu.SemaphoreType.DMA((n,)), pltpu.SemaphoreType.DMA((n,))],
    )(*grads)


def _pair_sum(g, got, idx, *, tr, name):
    r, c = g.shape[2:]

    def body(idx_ref, g_ref, got_ref, pb_ref, own_ref):
        p = g_ref[0, 0].astype(F32) + got_ref[0, 0].astype(F32)
        pb_ref[0] = p.astype(BF16)

        @pl.when(pl.program_id(1) == idx_ref[1])
        def _():
            own_ref[...] = p

    return pl.pallas_call(
        body, name=name,
        grid_spec=pltpu.PrefetchScalarGridSpec(
            num_scalar_prefetch=1, grid=(r // tr, 4),
            in_specs=[pl.BlockSpec((1, 1, tr, c), lambda i, j, idx: (j, idx[0], i, 0)),
                      pl.BlockSpec((1, 1, tr, c), lambda i, j, idx: (j, 0, i, 0))],
            out_specs=[pl.BlockSpec((1, tr, c), lambda i, j, idx: (j, i, 0)),
                       pl.BlockSpec((tr, c), lambda i, j, idx: (i, 0))]),
        out_shape=[jax.ShapeDtypeStruct((4, r, c), BF16), jax.ShapeDtypeStruct((r, c), F32)],
        compiler_params=_cparams(32, ("arbitrary", "arbitrary")),
    )(idx, g, got)


HBM = pl.BlockSpec(memory_space=pltpu.HBM)
SEM = pl.BlockSpec(memory_space=pltpu.SEMAPHORE)
DATAFLOW = pltpu.SideEffectType.DATAFLOW_SIDE_EFFECTING


PEERS = {"gather": NDEV - 1, "scatter": NDEV - 1, "chips": 3}


def _exchange_copies(src_refs, land_refs, send_sems, recv_sems, mode):
    x, y, c = _position()
    me, my_chip = 4 * x + 2 * y + c, 2 * x + y
    npeers = PEERS[mode]
    copies, own = [], []
    for a, (s_ref, l_ref) in enumerate(zip(src_refs, land_refs)):
        for k in range(npeers):
            if mode == "chips":
                px, py, pc = x ^ ((k + 1) >> 1), y ^ ((k + 1) & 1), c
                src, dst = s_ref.at[2 * px + py], l_ref.at[my_chip]
            else:
                px, py, pc = x ^ ((k + 1) >> 2), y ^ (((k + 1) >> 1) & 1), c ^ ((k + 1) & 1)
                src, dst = (s_ref.at[4 * px + 2 * py + pc] if mode == "scatter" else s_ref), l_ref.at[me]
            copies.append(pltpu.make_async_remote_copy(
                src_ref=src, dst_ref=dst, send_sem=send_sems.at[npeers * a + k], recv_sem=recv_sems.at[npeers * a + k],
                device_id=(px, py, pc), device_id_type=MESH_ID))
        slot = my_chip if mode == "chips" else me
        own.append(pltpu.make_async_copy(s_ref if mode == "gather" else s_ref.at[slot], l_ref.at[slot],
                                         send_sems.at[npeers * len(src_refs) + a]))
    return copies, own


def _exchange_start(srcs, lands, after, *, mode, name):
    n = len(srcs)
    nsem = PEERS[mode] * n

    def body(*refs):
        token = refs[-1]
        copies, own = _exchange_copies(refs[:n], refs[n:2 * n], refs[2 * n + 1], refs[2 * n + 2], mode)
        for cp in copies + own:
            cp.start()
        token[...] = jnp.zeros_like(token)

    arrays = list(srcs) + list(lands)
    outs = pl.pallas_call(
        body, name=name,
        out_shape=(pltpu.SemaphoreType.DMA((nsem + n,)), pltpu.SemaphoreType.DMA((nsem,)),
                   *[pltpu.HBM(a.shape, a.dtype) for a in arrays], jax.ShapeDtypeStruct((SUBLANES, LANES), F32)),
        in_specs=[HBM] * (2 * n) + [ANY],
        out_specs=(SEM, SEM, *[HBM] * (2 * n), pl.BlockSpec(memory_space=pltpu.VMEM)),
        input_output_aliases={i: 2 + i for i in range(2 * n)},
        compiler_params=pltpu.CompilerParams(has_side_effects=DATAFLOW),
    )(*[pltpu.with_memory_space_constraint(a, pltpu.HBM) for a in arrays], after)
    return outs[0], outs[1], outs[2:2 + n], outs[2 + n:2 + 2 * n], outs[-1]


def _exchange_wait(send_sems, recv_sems, srcs, lands, after, *, mode, name):
    n = len(srcs)

    def body(*refs):
        copies, own = _exchange_copies(refs[:n], refs[n:2 * n], refs[2 * n], refs[2 * n + 1], mode)
        for cp in copies:
            cp.wait_send()
            cp.wait_recv()
        for cp in own:
            cp.wait()

    arrays = list(srcs) + list(lands)
    outs = pl.pallas_call(
        body, name=name,
        out_shape=tuple(pltpu.HBM(a.shape, a.dtype) for a in arrays),
        in_specs=[HBM] * (2 * n) + [SEM, SEM, ANY],
        out_specs=tuple([HBM] * (2 * n)),
        input_output_aliases={i: i for i in range(2 * n)},
        compiler_params=pltpu.CompilerParams(has_side_effects=DATAFLOW),
    )(*arrays, send_sems, recv_sems, after)
    return outs[n:]


def _small_all_reduce(parts):
    def body(gmp_ref, gmo_ref, gfp_ref, gfo_ref, ga_ref, gc_ref, dw_ref, bf_ref, loss_ref,
             out_ref, buf, send_sems, recv_sems):
        x, y, c = _position()
        me = 4 * x + 2 * y + c

        def colsum(v):
            return jnp.sum(v, axis=0, keepdims=True)

        loss = jnp.sum(colsum(loss_ref[...]), axis=1, keepdims=True) * (0.5 / D)
        rows = [colsum(gmp_ref[...]), colsum(gmo_ref[...]), colsum(gfp_ref[...]), colsum(gfo_ref[...]),
                jnp.concatenate([colsum(ga_ref[...]), colsum(gc_ref[...])], axis=1),
                jnp.concatenate([colsum(dw_ref[0]), colsum(dw_ref[1])], axis=1),
                jnp.concatenate([colsum(dw_ref[2]), colsum(bf_ref[...]), jnp.broadcast_to(loss, (1, 128)),
                                 jnp.zeros((1, 256), F32)], axis=1),
                jnp.zeros((1, D), F32)]
        buf[me] = jnp.concatenate(rows, axis=0)
        copies = []
        for mm in range(1, NDEV):
            peer = (x ^ (mm >> 2), y ^ ((mm >> 1) & 1), c ^ (mm & 1))
            copies.append(pltpu.make_async_remote_copy(
                src_ref=buf.at[me], dst_ref=buf.at[me], send_sem=send_sems.at[mm - 1], recv_sem=recv_sems.at[mm - 1],
                device_id=peer, device_id_type=MESH_ID))
        for cp in copies:
            cp.start()
        for cp in copies:
            cp.wait_recv()
        for cp in copies:
            cp.wait_send()
        acc = buf[0]
        for d in range(1, NDEV):
            acc = acc + buf[d]
        out_ref[...] = acc

    vm = pl.BlockSpec(memory_space=pltpu.VMEM)
    return pl.pallas_call(
        body, name="small_all_reduce",
        out_shape=jax.ShapeDtypeStruct((SUBLANES, D), F32),
        in_specs=[vm] * len(parts), out_specs=vm,
        scratch_shapes=[pltpu.VMEM((NDEV, SUBLANES, D), F32), pltpu.SemaphoreType.DMA((7,)), pltpu.SemaphoreType.DMA((7,))],
    )(*parts)


def _adam_update(w, g, m, v):
    nm = ADAM_B1 * m + (1.0 - ADAM_B1) * g
    nv = ADAM_B2 * v + (1.0 - ADAM_B2) * (g * g)
    m_hat = nm / (1.0 - ADAM_B1 ** ADAM_STEP)
    v_hat = nv / (1.0 - ADAM_B2 ** ADAM_STEP)
    return -ADAM_LR * (m_hat / (jnp.sqrt(v_hat) + ADAM_EPS) + ADAM_WD * w), nm, nv


SMALL_SLOTS = {"g_mix_pre": (0, 0, D), "g_mix_post": (1, 0, D), "g_ffn_pre": (2, 0, D), "g_ffn_post": (3, 0, D),
               "g_attn_out": (4, 0, AW), "g_conv_out": (4, AW, CW), "b_forget": (6, CW, H)}
LOSS_LANE = CW + 128


def _small_adamw(small, conv_grad, params):
    names = list(params)
    n = len(names)

    def body(*refs):
        small_ref, cg_ref = refs[0], refs[1]
        ins, outs = refs[2:2 + 3 * n], refs[2 + 3 * n:]
        for i, name in enumerate(names):
            w_ref, m_ref, v_ref = ins[3 * i:3 * i + 3]
            g_ref, d_ref, nm_ref, nv_ref = outs[4 * i:4 * i + 4]
            if name == "conv_w":
                g = cg_ref[...]
            else:
                r, c0, width = SMALL_SLOTS[name]
                g = small_ref[r:r + 1, c0:c0 + width]
            g_ref[...] = g
            d_ref[...], nm_ref[...], nv_ref[...] = _adam_update(w_ref[...], g, m_ref[...], v_ref[...])
        outs[4 * n][...] = small_ref[6:7, LOSS_LANE:LOSS_LANE + 1]

    vm = pl.BlockSpec(memory_space=pltpu.VMEM)
    flat = [a for name in names for a in params[name]]
    outs = pl.pallas_call(
        body, name="adamw_small",
        in_specs=[vm] * (2 + 3 * n), out_specs=[vm] * (4 * n + 1),
        out_shape=[jax.ShapeDtypeStruct(params[name][0].shape, F32) for name in names for _ in range(4)]
        + [jax.ShapeDtypeStruct((1, 1), F32)],
    )(small, conv_grad, *flat)
    return {name: outs[4 * i:4 * i + 4] for i, name in enumerate(names)}, outs[4 * n].reshape(())


def _chip_sum_adamw(got, own, idx, wt, mt, vt, *, tr, name):
    cols, rows = wt.shape
    gcols = own.shape[1]

    def body(idx_ref, got_ref, own_ref, w_ref, m_ref, v_ref, g_ref, d_ref, nm_ref, nv_ref):
        g = jnp.zeros((tr, gcols), F32)
        for j in range(4):
            g = g + jnp.where(idx_ref[1] == j, own_ref[...], got_ref[j].astype(F32))
        g = g.T[:cols]
        g_ref[...] = g
        d_ref[...], nm_ref[...], nv_ref[...] = _adam_update(w_ref[...], g, m_ref[...], v_ref[...])

    spec = pl.BlockSpec((cols, tr), lambda i, idx: (0, i))
    gspec = pl.BlockSpec((tr, gcols), lambda i, idx: (i, 0))
    return pl.pallas_call(
        body, name=name,
        grid_spec=pltpu.PrefetchScalarGridSpec(
            num_scalar_prefetch=1, grid=(rows // tr,),
            in_specs=[pl.BlockSpec((4, tr, gcols), lambda i, idx: (0, i, 0)), gspec, spec, spec, spec],
            out_specs=[spec] * 4),
        out_shape=[jax.ShapeDtypeStruct((cols, rows), F32)] * 4,
        compiler_params=_cparams(32, ("arbitrary",)),
    )(idx, got, own, wt, mt, vt)


def _device_sum_adamw(land, w, m, v, *, tr, name):
    rows, cols = w.shape

    def body(land_ref, w_ref, m_ref, v_ref, g_ref, d_ref, nm_ref, nv_ref):
        g = land_ref[0].astype(F32)
        for dev in range(1, NDEV):
            g = g + land_ref[dev].astype(F32)
        g_ref[...] = g
        d_ref[...], nm_ref[...], nv_ref[...] = _adam_update(w_ref[...], g, m_ref[...], v_ref[...])

    spec = pl.BlockSpec((tr, cols), lambda i: (i, 0))
    return pl.pallas_call(
        body, name=name, grid=(rows // tr,),
        in_specs=[pl.BlockSpec((NDEV, tr, cols), lambda i: (0, i, 0)), spec, spec, spec],
        out_specs=[spec] * 4,
        out_shape=[jax.ShapeDtypeStruct((rows, cols), F32)] * 4,
        compiler_params=_cparams(32, ("arbitrary",)),
    )(land, w, m, v)


def _placement_constants():
    j = np.arange(128)[:, None]
    lane = np.arange(1024)[None, :]
    head, sub = lane // HP, lane % HP
    piece, jh = j // H, j % H
    valid = (j < 3 * H) & (jh == head)
    pq = np.where(valid & (sub == DH + piece), 1.0, 0.0).astype(BF16)
    pk = np.where(valid & (sub == DH + 3 + piece), -1.0, 0.0).astype(BF16)
    oq = np.where((sub >= DH + 3) & (sub < DH + 6), 1.0, 0.0).astype(np.float32)
    ok = np.where((sub >= DH) & (sub < DH + 3), 1.0, 0.0).astype(np.float32)
    r = np.arange(AW)[:, None]
    cc = np.arange(128)[None, :]
    sel = np.where((r % DH == 3) & (r // DH == cc), -1.0, 0.0).astype(BF16)
    gi = np.arange(GS)
    gsum = (gi[:, None] // DH == gi[None, :] // DH).astype(BF16)
    return tuple(jnp.asarray(c) for c in (pq, pk, oq, ok, sel, gsum))


def _local_step(xs, tgt, wp, late_weights, cw8, bfp, g_attn_out, g_conv_out,
                g_mix_pre, g_mix_post, g_ffn_pre, g_ffn_post, early_grads=None, last_grad=None):
    pq, pk, oq, ok, sel, gsum = _placement_constants()
    h1t, qp, kp, vv, bcu, zf = _in_proj(xs, g_mix_pre, wp, bfp, pq, pk, oq, ok, tm=512)
    o, lse, mk = _attn_fwd(qp, kp, vv, t=512)
    w_out_f, wgu, wd = late_weights(lse)
    merged, y, x2, cv, h2 = _mix_out(o, bcu, cw8, g_attn_out, g_conv_out, gsum, w_out_f, xs, g_mix_post, g_ffn_pre, tm=512)
    gate, up, act, dx3, dff, loss_p, dg_ffn_post = _ffn_fwd_loss(h2, wgu, wd, x2, tgt, g_ffn_post, tm=512)

    dgu, dx2, dy, dg_ffn_pre, dg_mix_post = _ffn_bwd(dff, wd, gate, up, wgu, x2, g_ffn_pre, dx3, y, g_mix_post, tm=256)
    dw_down = _grad_matmul(act, dff, ta=DFF // 2, tb=D, ts=4096, name="grad_w_down", vmem_mb=60)
    dw_gu = _grad_matmul(dgu, h2, ta=DFF // 2, tb=D, ts=4096, name="grad_w_gate_up", vmem_mb=60).reshape(NDEV, FB, D)
    dw_out = _grad_matmul(merged, dy, ta=1024, tb=1024, ts=2048, name="grad_w_out")
    token = early_grads(dw_out, dw_gu, dw_down) if early_grads is not None else dw_out
    do, dl, dcv, db, dg_attn, dg_conv = _mix_bwd(dy, w_out_f, o, cv, bcu, g_attn_out, g_conv_out, gsum, token, tm=512)
    dbcu, dtaps = _conv_bwd(dcv, db, bcu, cw8, tm=512)
    dqp, dkp, dv, dkx = _attn_bwd(qp, kp, vv, do, lse, dl, mk, t=512)
    dfl, dbf = _forget_bwd(dkx, zf, sel, tm=512)
    pieces = (dqp, dkp, dv, dbcu, dfl)
    dwp = _grad_w_in(h1t, pieces)
    token = last_grad(dwp) if last_grad is not None else dwp
    grad_x, dg_mix_pre = _in_proj_bwd(pieces, wp, xs, g_mix_pre, dx2, token, tm=512)
    return (grad_x, dwp, dw_out, dw_gu, dw_down, dg_mix_pre, dg_mix_post, dg_ffn_pre, dg_ffn_post, dg_attn, dg_conv,
            dtaps, dbf, loss_p)


BIG_TILES = {"w_in": 256, "w_out": 128, "w_gate_up": 176, "w_down": 176}


def kernel(x, w_in, b_forget, conv_w, g_attn_out, g_conv_out, w_out, g_mix_pre, g_mix_post, w_gate_up, w_down, g_ffn_pre, g_ffn_post, loss_target, m_w_in, m_b_forget, m_conv_w, m_g_attn_out, m_g_conv_out, m_w_out, m_g_mix_pre, m_g_mix_post, m_w_gate_up, m_w_down, m_g_ffn_pre, m_g_ffn_post, v_w_in, v_b_forget, v_conv_w, v_g_attn_out, v_g_conv_out, v_w_out, v_g_mix_pre, v_g_mix_post, v_w_gate_up, v_w_down, v_g_ffn_pre, v_g_ffn_post):
    xc, yc, cc = _position()
    my_chip = 2 * xc + yc
    me = 2 * my_chip + cc
    idx = jnp.stack([cc, my_chip]).astype(jnp.int32)
    tables = _in_layout_tables()

    w_in_b = w_in[0].astype(BF16)
    g_in, g_last, g_taps = _all_gather([w_in_b[:, :IN_MAIN], w_in_b[:, IN_MAIN].reshape(SUBLANES, LANES), conv_w[0]])
    last_cols = jnp.pad(g_last.reshape(NDEV, D).T.astype(F32), ((0, 0), (0, LANES - NDEV)))
    wp = _assemble_w_in(g_in, last_cols, tables, tr=256)
    cw8 = jnp.pad(g_taps.transpose(1, 0, 2).reshape(3, CW), ((0, SUBLANES - 3), (0, 0)))

    late = [w_out[0].astype(BF16), w_gate_up[0].T.astype(BF16), w_down[0].astype(BF16)]
    ssem, rsem, late_thru, land_thru, token = _exchange_start(
        late, [lax.empty((NDEV,) + s.shape, s.dtype) for s in late], g_in, mode="gather",
        name="gather_late_start")
    bfp = jnp.pad(b_forget, ((0, 0), (0, 128 - H))) + token[0:1, :]

    def late_weights(after):
        l_out, l_gu, l_down = _exchange_wait(ssem, rsem, late_thru, land_thru, after, mode="gather", name="gather_late_wait")
        return l_out.reshape(D, D), l_gu.reshape(2, DFF, D), l_down.reshape(DFF, D)

    early = {}

    def early_grads(dw_out, dw_gu, dw_down):
        srcs = [dw_out.reshape(NDEV, D // NDEV, D), dw_gu, dw_down.reshape(NDEV, DFF // NDEV, D)]
        lands = [lax.empty(s.shape, s.dtype) for s in srcs]
        early["handles"] = _exchange_start(srcs, lands, dw_out, mode="scatter", name="scatter_early_start")
        return early["handles"][4]

    last = {}

    def last_grad(dwp):
        g_w_in = _disassemble_w_in(dwp, tables, tr=256).reshape(4, 2, D, IN_PAD)
        (from_sibling,) = _pair_exchange([g_w_in])
        pair_b, last["own"] = _pair_sum(g_w_in, from_sibling, idx, tr=BIG_TILES["w_in"], name="grad_pair_sum_w_in")
        last["handles"] = _exchange_start([pair_b], [lax.empty(pair_b.shape, pair_b.dtype)], last["own"], mode="chips",
                                          name="chips_w_in_start")
        return last["handles"][4]

    (grad_x, dwp, dw_out, dw_gu, dw_down, dg_mix_pre, dg_mix_post, dg_ffn_pre, dg_ffn_post, dg_attn, dg_conv,
     dtaps, dbf, loss_p) = _local_step(x[0], loss_target[0], wp, late_weights, cw8, bfp, g_attn_out, g_conv_out,
                                        g_mix_pre, g_mix_post, g_ffn_pre, g_ffn_post, early_grads, last_grad)

    e_ssem, e_rsem, e_srcs, e_lands, _ = early["handles"]
    land_out, land_gu, land_down = _exchange_wait(e_ssem, e_rsem, e_srcs, e_lands, dg_mix_pre, mode="scatter",
                                                  name="scatter_early_wait")
    res = {}
    big = {"w_out": (land_out, w_out[0], m_w_out[0], v_w_out[0]),
           "w_gate_up": (land_gu, w_gate_up[0].T, m_w_gate_up[0].T, v_w_gate_up[0].T),
           "w_down": (land_down, w_down[0], m_w_down[0], v_w_down[0])}
    for name, (land, w, m, v) in big.items():
        outs = _device_sum_adamw(land, w, m, v, tr=BIG_TILES[name], name="adamw_" + name)
        res[name] = [(o.T if name == "w_gate_up" else o)[None] for o in outs]
    c_ssem, c_rsem, c_srcs, c_lands, _ = last["handles"]
    after = sum(res[n][1][0, :SUBLANES, :LANES] for n in big)
    (from_chips,) = _exchange_wait(c_ssem, c_rsem, c_srcs, c_lands, after, mode="chips", name="chips_w_in_wait")
    outs = _chip_sum_adamw(from_chips, last["own"], idx, w_in[0].T, m_w_in[0].T, v_w_in[0].T,
                           tr=BIG_TILES["w_in"], name="adamw_w_in")
    res["w_in"] = [o.T[None] for o in outs]

    small = _small_all_reduce([dg_mix_pre, dg_mix_post, dg_ffn_pre, dg_ffn_post, dg_attn, dg_conv, dtaps, dbf, loss_p])
    taps_full = jnp.concatenate([small[5:6, :CW], small[5:6, CW:], small[6:7, :CW]], axis=0)
    taps_first = lambda a: a.transpose(1, 0, 2)
    smalls = {"b_forget": (b_forget, m_b_forget, v_b_forget),
              "conv_w": (taps_first(conv_w), taps_first(m_conv_w), taps_first(v_conv_w)),
              "g_attn_out": (g_attn_out, m_g_attn_out, v_g_attn_out), "g_conv_out": (g_conv_out, m_g_conv_out, v_g_conv_out),
              "g_mix_pre": (g_mix_pre, m_g_mix_pre, v_g_mix_pre), "g_mix_post": (g_mix_post, m_g_mix_post, v_g_mix_post),
              "g_ffn_pre": (g_ffn_pre, m_g_ffn_pre, v_g_ffn_pre), "g_ffn_post": (g_ffn_post, m_g_ffn_post, v_g_ffn_post)}
    own_taps = lax.dynamic_slice(taps_full, (0, me * 64), (3, 64))[:, None, :]
    small_res, loss = _small_adamw(small, own_taps, smalls)
    for name, outs in small_res.items():
        res[name] = [taps_first(o) for o in outs] if name == "conv_w" else list(outs)

    order = ["w_in", "b_forget", "conv_w", "g_attn_out", "g_conv_out", "w_out", "g_mix_pre", "g_mix_post",
             "w_gate_up", "w_down", "g_ffn_pre", "g_ffn_post"]
    outs = [loss, grad_x[None]]
    for k in range(4):
        outs += [res[n][k] for n in order]
    return tuple(outs)
```

```python
import functools

import numpy as np

import jax
import jax.numpy as jnp
from jax import lax
from jax.experimental import pallas as pl
from jax.experimental.pallas import tpu as pltpu

F32 = jnp.float32
BF16 = jnp.bfloat16
MESH_ID = pl.DeviceIdType.MESH

D = 1024
H = 8
DH = 64
AW = 512
CW = 512
DFF = 2816
FB = DFF // 4
FF_CHUNKS = ((0, 768), (768, 768), (1536, 768), (2304, 512))
FF_CHUNKS_BWD = ((0, 1024), (1024, 1024), (2048, 768))
HP = 128
OFF_Q, OFF_K, OFF_V, OFF_BCU, OFF_F = 0, 512, 1024, 1536, 3072
WP = OFF_F + 128
PIECES = ((OFF_Q, OFF_K), (OFF_K, OFF_V), (OFF_V, OFF_BCU), (OFF_BCU, OFF_F), (OFF_F, WP))
EPS = 1e-6
LOG2E, LN2 = 1.4426950408889634, 0.6931471805599453
NDEV = 8
LANES = 128
SUBLANES = 8
IN_COLS = 385
IN_PAD = 512
IN_MAIN = 384
WIN = 640
ADAM_LR, ADAM_B1, ADAM_B2, ADAM_EPS, ADAM_WD, ADAM_STEP = 0.001, 0.9, 0.999, 1e-08, 0.01, 10

NT = (((1,), (1,)), ((), ()))
TN = (((0,), (0,)), ((), ()))


def _cparams(vmem_mb=None, sem=None):
    kw = {}
    if vmem_mb is not None:
        kw["vmem_limit_bytes"] = vmem_mb << 20
    if sem is not None:
        kw["dimension_semantics"] = sem
    return pltpu.CompilerParams(**kw)


def _full(shape):
    return pl.BlockSpec(shape, lambda *_: (0,) * len(shape))


def _resident(shape):
    return pl.BlockSpec(shape, lambda *_: (0,) * len(shape), pipeline_mode=pl.Buffered(1))


def _rows(tm, width):
    return pl.BlockSpec((tm, width), lambda i: (i, 0))


def _fold8(v):
    r, w = v.shape
    return jnp.sum(v.reshape(r // SUBLANES, SUBLANES, w), axis=0)


def _split_dot(v, m01):
    hi = v.astype(BF16)
    lo = (v - hi.astype(F32)).astype(BF16)
    return (jnp.dot(hi, m01, preferred_element_type=F32)
            + jnp.dot(lo, m01, preferred_element_type=F32))


GS = 256


def _group_sum(v, g01):
    parts = [_split_dot(v[:, c:c + GS], g01) for c in range(0, v.shape[1], GS)]
    return parts[0] if len(parts) == 1 else jnp.concatenate(parts, axis=1)


def _exact_dot01(m01, v):
    p1 = v.astype(BF16)
    r1 = v - p1.astype(F32)
    p2 = r1.astype(BF16)
    p3 = (r1 - p2.astype(F32)).astype(BF16)
    return (jnp.dot(m01, p1, preferred_element_type=F32) + jnp.dot(m01, p2, preferred_element_type=F32)
            + jnp.dot(m01, p3, preferred_element_type=F32))


def _rms_fwd(v, g):
    r = lax.rsqrt(jnp.mean(v * v, axis=-1, keepdims=True) + EPS)
    n = v * r
    return n * g, n, r


def _rms_bwd(do, n, r, g):
    dn = do * g
    return r * (dn - n * jnp.mean(dn * n, axis=-1, keepdims=True)), do * n


def _padded_column(n):
    if n < AW:
        return OFF_Q + n, 0.125
    if n < 3 * AW:
        return n, 1.0
    if n < 3 * AW + H:
        return OFF_F + n - 3 * AW, 1.0
    return OFF_BCU + n - 3 * AW - H, 1.0


def _in_layout_tables():
    dest = -np.ones((IN_PAD, LANES), np.int32)
    dest_f = -np.ones((IN_PAD, LANES), np.int32)
    scale = np.zeros((IN_PAD, LANES), np.float32)
    starts = []
    for k in range(NDEV):
        cols = [_padded_column(IN_COLS * k + j) for j in range(IN_COLS)]
        main = [c for c, _ in cols if c < OFF_F]
        ws = min((min(main) // LANES) * LANES, OFF_F - WIN)
        assert ws <= min(main) and max(main) < ws + WIN
        starts.append(ws)
        for j, (c, sc) in enumerate(cols):
            scale[j, k] = sc
            if c < OFF_F:
                dest[j, k] = c - ws
            else:
                dest_f[j, k] = c - OFF_F
    f_shards = tuple(k for k in range(NDEV) if (dest_f[:, k] >= 0).any())
    return tuple(starts), f_shards, jnp.asarray(dest), jnp.asarray(dest_f), jnp.asarray(scale)


def _perm(dest_ref, scale_ref, k, width, rows=IN_PAD):
    lane = lax.broadcasted_iota(jnp.int32, (rows, width), 1)
    return jnp.where(dest_ref[0:rows, k:k + 1] == lane, scale_ref[0:rows, k:k + 1], 0.0).astype(BF16)


def _assemble_w_in(blocks, last_cols, tables, *, tr):
    starts, f_shards, dest, dest_f, scale = tables
    last = [_padded_column(IN_COLS * k + IN_MAIN) for k in range(NDEV)]
    f_main = [any(_padded_column(IN_COLS * k + j)[0] >= OFF_F for j in range(IN_MAIN)) for k in range(NDEV)]
    assert IN_COLS == IN_MAIN + 1

    def body(b_ref, c_ref, dest_ref, destf_ref, scale_ref, o_ref):
        o_ref[...] = jnp.zeros_like(o_ref)
        lane = lax.broadcasted_iota(jnp.int32, (tr, LANES), 1)
        for k in range(NDEV):
            b = b_ref[k]
            ws = starts[k]
            part = jnp.dot(b, _perm(dest_ref, scale_ref, k, WIN, IN_MAIN), preferred_element_type=F32)
            o_ref[:, ws:ws + WIN] += part.astype(BF16)
            if f_main[k]:
                part = jnp.dot(b, _perm(destf_ref, scale_ref, k, 128, IN_MAIN), preferred_element_type=F32)
                o_ref[:, OFF_F:WP] += part.astype(BF16)
            col, sc = last[k]
            tile = (col // LANES) * LANES
            o_ref[:, tile:tile + LANES] += jnp.where(lane == col - tile, c_ref[:, k:k + 1] * sc, 0.0).astype(BF16)

    tab = _full((IN_PAD, LANES))
    return pl.pallas_call(
        body, name="assemble_w_in", grid=(D // tr,),
        in_specs=[pl.BlockSpec((NDEV, tr, IN_MAIN), lambda i: (0, i, 0)), _rows(tr, LANES), tab, tab, tab],
        out_specs=_rows(tr, WP),
        out_shape=jax.ShapeDtypeStruct((D, WP), BF16),
        compiler_params=_cparams(48, ("arbitrary",)),
    )(blocks, last_cols, dest, dest_f, scale)


def _disassemble_w_in(dwp, tables, *, tr):
    starts, f_shards, dest, dest_f, scale = tables
    width = dwp.shape[1]

    def body(g_ref, dest_ref, destf_ref, scale_ref, o_ref):
        for k in range(NDEV):
            ws = starts[k]
            acc = lax.dot_general(g_ref[:, ws:ws + WIN], _perm(dest_ref, scale_ref, k, WIN), NT, preferred_element_type=F32)
            if k in f_shards:
                acc = acc + lax.dot_general(g_ref[:, OFF_F:WP], _perm(destf_ref, scale_ref, k, 128), NT,
                                            preferred_element_type=F32)
            o_ref[k] = acc.astype(BF16)

    tab = _full((IN_PAD, LANES))
    return pl.pallas_call(
        body, name="disassemble_w_in", grid=(D // tr,),
        in_specs=[_rows(tr, width), tab, tab, tab],
        out_specs=pl.BlockSpec((NDEV, tr, IN_PAD), lambda i: (0, i, 0)),
        out_shape=jax.ShapeDtypeStruct((NDEV, D, IN_PAD), BF16),
        compiler_params=_cparams(48, ("arbitrary",)),
    )(dwp, dest, dest_f, scale)


def _in_proj(x, g1, wp, bfp, pq, pk, oq, ok, *, tm):
    s = x.shape[0]

    def body(x_ref, g_ref, w_ref, bf_ref, pq_ref, pk_ref, oq_ref, ok_ref,
             ht_ref, qp_ref, kp_ref, v_ref, bcu_ref, z_ref, carry):
        @pl.when(pl.program_id(0) == 0)
        def _():
            carry[...] = jnp.zeros_like(carry)

        h = _rms_fwd(x_ref[...], g_ref[...])[0].astype(BF16)
        ht_ref[...] = h.T
        z = jnp.dot(h, w_ref[:, OFF_F:WP], preferred_element_type=F32) + bf_ref[...]
        z_ref[...] = z
        lane = lax.broadcasted_iota(jnp.int32, (tm, 128), 1)
        logf = jnp.where(lane < H, jnp.minimum(z, 0.0) - jnp.log(1.0 + jnp.exp(-jnp.abs(z))), 0.0)
        row = lax.broadcasted_iota(jnp.int32, (tm, tm), 0)
        col = lax.broadcasted_iota(jnp.int32, (tm, tm), 1)
        tri = (col <= row).astype(BF16)
        c = _exact_dot01(tri, logf) + carry[0:1, :]
        carry[...] = jnp.broadcast_to(c[tm - 1:tm, :], carry.shape)
        cb = c * LOG2E
        c1 = cb.astype(BF16).astype(F32)
        r1 = cb - c1
        c2 = r1.astype(BF16).astype(F32)
        c3 = (r1 - c2).astype(BF16).astype(F32)
        zc = (c1 + pltpu.roll(c2, 8, axis=1) + pltpu.roll(c3, 16, axis=1)).astype(BF16)

        def pad_heads(v):
            blocks = []
            for pair in range(H // 2):
                two = v[:, 128 * pair:128 * (pair + 1)]
                blocks.append(jnp.where(lane < DH, two, 0.0))
                blocks.append(jnp.where(lane < DH, pltpu.roll(two, DH, axis=1), 0.0))
            return jnp.concatenate(blocks, axis=1)

        q = jnp.dot(h, w_ref[:, OFF_Q:OFF_K], preferred_element_type=F32) * LOG2E
        qp_ref[...] = (pad_heads(q) + jnp.dot(zc, pq_ref[...], preferred_element_type=F32) + oq_ref[...]).astype(BF16)
        k = jnp.dot(h, w_ref[:, OFF_K:OFF_V], preferred_element_type=F32)
        kp_ref[...] = (pad_heads(k) + jnp.dot(zc, pk_ref[...], preferred_element_type=F32) + ok_ref[...]).astype(BF16)
        v = pad_heads(jnp.dot(h, w_ref[:, OFF_V:OFF_BCU], preferred_element_type=F32))
        ones_lane = lax.broadcasted_iota(jnp.int32, (tm, H * HP), 1) % HP == DH
        v_ref[...] = jnp.where(ones_lane, 1.0, v).astype(BF16)
        bcu_ref[...] = jnp.dot(h, w_ref[:, OFF_BCU:OFF_F], preferred_element_type=F32).astype(BF16)

    return pl.pallas_call(
        body, name="in_proj", grid=(s // tm,),
        in_specs=[_rows(tm, D), _full((1, D)), _resident((D, WP)), _full((1, 128)),
                  _full((128, 1024)), _full((128, 1024)), _full((1, 1024)), _full((1, 1024))],
        out_specs=[pl.BlockSpec((D, tm), lambda i: (0, i)), _rows(tm, 1024), _rows(tm, 1024), _rows(tm, 1024),
                   _rows(tm, 3 * CW), _rows(tm, 128)],
        out_shape=[jax.ShapeDtypeStruct((D, s), BF16), jax.ShapeDtypeStruct((s, 1024), BF16),
                   jax.ShapeDtypeStruct((s, 1024), BF16), jax.ShapeDtypeStruct((s, 1024), BF16),
                   jax.ShapeDtypeStruct((s, 3 * CW), BF16), jax.ShapeDtypeStruct((s, 128), F32)],
        scratch_shapes=[pltpu.VMEM((SUBLANES, 128), F32)],
        compiler_params=_cparams(56, ("arbitrary",)),
    )(x, g1, wp, bfp, pq, pk, oq, ok)


def _attn_fwd(qp, kp, v, *, t):
    s = qp.shape[0]
    nq = s // t

    def body(q_ref, k_ref, v_ref, o_ref, lse_ref, mk_ref):
        pi = pl.program_id(1)
        row = lax.broadcasted_iota(jnp.int32, (t, t), 0)
        col = lax.broadcasted_iota(jnp.int32, (t, t), 1)
        lane = lax.broadcasted_iota(jnp.int32, (t, 128), 1)

        def head_step(hh, rows, ki, carry, masked):
            m, acc = carry
            off = pl.multiple_of(ki * t, t)
            q = q_ref[rows, HP * hh:HP * (hh + 1)]
            k = k_ref[pl.ds(off, t), HP * hh:HP * (hh + 1)]
            sc = lax.dot_general(q, k, NT, preferred_element_type=F32)
            if masked:
                sc = jnp.where(col <= row, sc, -1e30)
            mn = jnp.maximum(m, jnp.max(sc, axis=-1, keepdims=True))
            p = jnp.exp2(sc - mn).astype(BF16)
            acc = jnp.exp2(m - mn) * acc + jnp.dot(p, v_ref[pl.ds(off, t), HP * hh:HP * (hh + 1)],
                                                  preferred_element_type=F32)
            return mn, acc

        def step(rows, ki, carry, masked):
            new = tuple(head_step(hh, rows, ki, carry[hh], masked) for hh in range(2))
            mk_ref[ki, rows] = jnp.where(lane < DH, jnp.broadcast_to(new[0][0], (t, 128)),
                                         jnp.broadcast_to(new[1][0], (t, 128)))
            return new

        init = (jnp.full((t, 1), -1e30, F32), jnp.zeros((t, 128), F32))
        top, bottom = slice(0, t), slice(t, 2 * t)

        def quad(j, carry):
            c0, c1 = carry
            c0 = step(top, 2 * j, c0, False)
            c1 = step(bottom, 2 * j, c1, False)
            c0 = step(top, 2 * j + 1, c0, False)
            c1 = step(bottom, 2 * j + 1, c1, False)
            return c0, c1

        c0, c1 = lax.fori_loop(0, pi, quad, ((init, init), (init, init)))
        f0 = step(top, 2 * pi, c0, True)
        c1 = step(bottom, 2 * pi, c1, False)
        f1 = step(bottom, 2 * pi + 1, c1, True)
        for rows, ((m0, acc0), (m1, acc1)) in ((top, f0), (bottom, f1)):
            l0, l1 = acc0[:, DH:DH + 1], acc1[:, DH:DH + 1]
            o_ref[rows, :] = jnp.where(lane < DH, acc0 / l0, pltpu.roll(acc1 / l1, DH, axis=1))
            lse_ref[rows, :] = jnp.where(lane < DH, jnp.broadcast_to(m0 + jnp.log2(l0), (t, 128)),
                                         jnp.broadcast_to(m1 + jnp.log2(l1), (t, 128)))

    return pl.pallas_call(
        body, name="attn_fwd", grid=(H // 2, nq // 2),
        in_specs=[pl.BlockSpec((2 * t, 2 * HP), lambda p, i: (i, p)),
                  pl.BlockSpec((s, 2 * HP), lambda p, i: (0, p)),
                  pl.BlockSpec((s, 2 * HP), lambda p, i: (0, p))],
        out_specs=[pl.BlockSpec((2 * t, 128), lambda p, i: (i, p)), pl.BlockSpec((2 * t, 128), lambda p, i: (i, p)),
                   pl.BlockSpec((nq, 2 * t, 128), lambda p, i: (0, i, p))],
        out_shape=[jax.ShapeDtypeStruct((s, AW), F32), jax.ShapeDtypeStruct((s, AW), F32),
                   jax.ShapeDtypeStruct((nq, s, AW), F32)],
        compiler_params=_cparams(48, ("arbitrary", "arbitrary")),
    )(qp, kp, v)


HALO = 16


def _conv_taps(bcu_ref, halo_ref, first, tm):
    z = bcu_ref[:, CW:2 * CW].astype(F32) * bcu_ref[:, 2 * CW:3 * CW].astype(F32)
    zh = jnp.where(first, 0.0, halo_ref[:, CW:2 * CW].astype(F32) * halo_ref[:, 2 * CW:3 * CW].astype(F32))
    row = lax.broadcasted_iota(jnp.int32, (tm, CW), 0)
    last, before = zh[HALO - 1:HALO, :], zh[HALO - 2:HALO - 1, :]
    z1 = jnp.where(row == 0, last, pltpu.roll(z, 1, axis=0))
    z2 = jnp.where(row == 0, before, jnp.where(row == 1, last, pltpu.roll(z, 2, axis=0)))
    return z, z1, z2


def _halo_before(tm, width):
    return pl.BlockSpec((HALO, width), lambda i: (jnp.maximum(i * (tm // HALO) - 1, 0), 0))


def _mix_out(o, bcu, cw8, ga, gc, gsum, w_out, x, g_post, g_ffn_pre, *, tm):
    s = x.shape[0]

    def body(o_ref, bcu_ref, halo_ref, cw_ref, ga_ref, gc_ref, gs_ref, w_ref, x_ref, g_ref, gf_ref,
             merged_ref, y_ref, x2_ref, cv_ref, h2_ref):
        z, z1, z2 = _conv_taps(bcu_ref, halo_ref, pl.program_id(0) == 0, tm)
        cv = cw_ref[0:1, :] * z2 + cw_ref[1:2, :] * z1 + cw_ref[2:3, :] * z
        cv_ref[...] = cv
        conv = bcu_ref[:, 0:CW].astype(F32) * cv
        ov = o_ref[...]
        ra = lax.rsqrt(_group_sum(ov * ov, gs_ref[...]) * (1.0 / DH) + EPS)
        rc = lax.rsqrt(_group_sum(conv * conv, gs_ref[...]) * (1.0 / DH) + EPS)
        merged = jnp.concatenate([ov * ra * ga_ref[...], conv * rc * gc_ref[...]], axis=1).astype(BF16)
        merged_ref[...] = merged
        y = jnp.dot(merged, w_ref[...], preferred_element_type=F32)
        y_ref[...] = y
        x2 = x_ref[...] + _rms_fwd(y, g_ref[...])[0]
        x2_ref[...] = x2
        h2_ref[...] = _rms_fwd(x2, gf_ref[...])[0].astype(BF16)

    return pl.pallas_call(
        body, name="mix_out", grid=(s // tm,),
        in_specs=[_rows(tm, AW), _rows(tm, 3 * CW), _halo_before(tm, 3 * CW), _full((SUBLANES, CW)),
                  _full((1, AW)), _full((1, CW)), _full((GS, GS)), _resident((D, D)), _rows(tm, D), _full((1, D)),
                  _full((1, D))],
        out_specs=[_rows(tm, D), _rows(tm, D), _rows(tm, D), _rows(tm, CW), _rows(tm, D)],
        out_shape=[jax.ShapeDtypeStruct((s, D), BF16), jax.ShapeDtypeStruct((s, D), F32),
                   jax.ShapeDtypeStruct((s, D), F32), jax.ShapeDtypeStruct((s, CW), F32),
                   jax.ShapeDtypeStruct((s, D), BF16)],
        compiler_params=_cparams(48, ("arbitrary",)),
    )(o, bcu, bcu, cw8, ga, gc, gsum, w_out, x, g_post, g_ffn_pre)


def _ffn_fwd_loss(h2, wgu, wd, x2, target, g_post, *, tm):
    s = x2.shape[0]

    def body(h_ref, w_ref, wd_ref, x2_ref, t_ref, g_ref,
             gate_ref, up_ref, a_ref, dx3_ref, dff_ref, loss_ref, dg_ref):
        @pl.when(pl.program_id(0) == 0)
        def _():
            loss_ref[...] = jnp.zeros_like(loss_ref)
            dg_ref[...] = jnp.zeros_like(dg_ref)

        h = h_ref[...]
        ff = None
        for c0, n in FF_CHUNKS:
            cols = slice(c0, c0 + n)
            gate = lax.dot_general(h, w_ref[0, cols, :], NT, preferred_element_type=F32)
            up = lax.dot_general(h, w_ref[1, cols, :], NT, preferred_element_type=F32)
            gate_ref[:, cols] = gate.astype(BF16)
            up_ref[:, cols] = up.astype(BF16)
            act = (gate * jax.nn.sigmoid(gate) * up).astype(BF16)
            a_ref[:, cols] = act
            part = jnp.dot(act, wd_ref[cols, :], preferred_element_type=F32)
            ff = part if ff is None else ff + part
        out, n, r = _rms_fwd(ff, g_ref[...])
        e = x2_ref[...] + out - t_ref[...]
        loss_ref[...] += _fold8(e * e)
        dx3 = e * (1.0 / D)
        dx3_ref[...] = dx3
        dff, dg = _rms_bwd(dx3, n, r, g_ref[...])
        dff_ref[...] = dff.astype(BF16)
        dg_ref[...] += _fold8(dg)

    wide = _rows(tm, DFF)
    return pl.pallas_call(
        body, name="ffn_fwd_loss", grid=(s // tm,),
        in_specs=[_rows(tm, D), _resident((2, DFF, D)), _resident((DFF, D)), _rows(tm, D), _rows(tm, D), _full((1, D))],
        out_specs=[wide, wide, wide, _rows(tm, D), _rows(tm, D), _full((SUBLANES, D)), _full((SUBLANES, D))],
        out_shape=[jax.ShapeDtypeStruct((s, DFF), BF16)] * 3
        + [jax.ShapeDtypeStruct((s, D), F32), jax.ShapeDtypeStruct((s, D), BF16),
           jax.ShapeDtypeStruct((SUBLANES, D), F32), jax.ShapeDtypeStruct((SUBLANES, D), F32)],
        compiler_params=_cparams(56, ("arbitrary",)),
    )(h2, wgu, wd, x2, target, g_post)


def _ffn_bwd(dff, wd, gate, up, wgu, x2, g_pre, dx3, y, g_post, *, tm):
    s = x2.shape[0]

    def body(dff_ref, wd_ref, gate_ref, up_ref, w_ref, x2_ref, gpre_ref, dx3_ref, y_ref, gpost_ref,
             dgu_ref, dx2_ref, dy_ref, dgpre_ref, dgpost_ref):
        @pl.when(pl.program_id(0) == 0)
        def _():
            dgpre_ref[...] = jnp.zeros_like(dgpre_ref)
            dgpost_ref[...] = jnp.zeros_like(dgpost_ref)

        dff = dff_ref[...]
        dh2 = None
        for c0, n in FF_CHUNKS_BWD:
            cols = slice(c0, c0 + n)
            da = lax.dot_general(dff, wd_ref[cols, :], NT, preferred_element_type=F32)
            g = gate_ref[:, cols].astype(F32)
            sg = jax.nn.sigmoid(g)
            dgate = (da * up_ref[:, cols].astype(F32) * (sg * (1.0 + g * (1.0 - sg)))).astype(BF16)
            dup = (da * (g * sg)).astype(BF16)
            dgu_ref[:, cols] = dgate
            dgu_ref[:, DFF + c0:DFF + c0 + n] = dup
            part = (jnp.dot(dgate, w_ref[0, cols, :], preferred_element_type=F32)
                    + jnp.dot(dup, w_ref[1, cols, :], preferred_element_type=F32))
            dh2 = part if dh2 is None else dh2 + part
        _, n2, r2 = _rms_fwd(x2_ref[...], gpre_ref[...])
        dxn, dg = _rms_bwd(dh2, n2, r2, gpre_ref[...])
        dgpre_ref[...] += _fold8(dg)
        dx2 = dx3_ref[...] + dxn
        dx2_ref[...] = dx2
        _, ny, ry = _rms_fwd(y_ref[...], gpost_ref[...])
        dy, dg2 = _rms_bwd(dx2, ny, ry, gpost_ref[...])
        dy_ref[...] = dy.astype(BF16)
        dgpost_ref[...] += _fold8(dg2)

    wide = _rows(tm, DFF)
    return pl.pallas_call(
        body, name="ffn_bwd", grid=(s // tm,),
        in_specs=[_rows(tm, D), _resident((DFF, D)), wide, wide, _resident((2, DFF, D)), _rows(tm, D), _full((1, D)),
                  _rows(tm, D), _rows(tm, D), _full((1, D))],
        out_specs=[_rows(tm, 2 * DFF), _rows(tm, D), _rows(tm, D),
                   _full((SUBLANES, D)), _full((SUBLANES, D))],
        out_shape=[jax.ShapeDtypeStruct((s, 2 * DFF), BF16), jax.ShapeDtypeStruct((s, D), F32),
                   jax.ShapeDtypeStruct((s, D), BF16), jax.ShapeDtypeStruct((SUBLANES, D), F32),
                   jax.ShapeDtypeStruct((SUBLANES, D), F32)],
        compiler_params=_cparams(56, ("arbitrary",)),
    )(dff, wd, gate, up, wgu, x2, g_pre, dx3, y, g_post)


def _grad_matmul(a, b, *, ta, tb, ts, name, vmem_mb=48):
    s, ka = a.shape
    nb = b.shape[1]
    ts = min(ts, s)
    nk = s // ts

    def body(a_ref, b_ref, o_ref, *acc):
        if nk == 1:
            o_ref[...] = lax.dot_general(a_ref[...], b_ref[...], TN, preferred_element_type=F32).astype(BF16)
            return
        k = pl.program_id(2)

        @pl.when(k == 0)
        def _():
            acc[0][...] = jnp.zeros_like(acc[0])

        acc[0][...] += lax.dot_general(a_ref[...], b_ref[...], TN, preferred_element_type=F32)

        @pl.when(k == nk - 1)
        def _():
            o_ref[...] = acc[0][...].astype(BF16)

    whole_b = {"pipeline_mode": pl.Buffered(1)} if nk == 1 and nb == tb else {}
    return pl.pallas_call(
        body, name=name, grid=(ka // ta, nb // tb, nk),
        in_specs=[pl.BlockSpec((ts, ta), lambda i, j, k: (k, i)),
                  pl.BlockSpec((ts, tb), lambda i, j, k: (k, j), **whole_b)],
        out_specs=pl.BlockSpec((ta, tb), lambda i, j, k: (i, j)),
        out_shape=jax.ShapeDtypeStruct((ka, nb), BF16),
        scratch_shapes=[pltpu.VMEM((ta, tb), F32)] if nk > 1 else [],
        compiler_params=_cparams(vmem_mb, ("arbitrary", "arbitrary", "arbitrary")),
    )(a, b)


GW_TILE = 256


def _grad_w_in(h1t, pieces):
    ka, s = h1t.shape
    widths = [p.shape[1] for p in pieces]
    assert all(w % GW_TILE == 0 for w in widths)
    first = [sum(widths[:i]) // GW_TILE for i in range(len(pieces))]
    count = [w // GW_TILE for w in widths]

    def body(a_ref, *refs):
        o_ref = refs[-1]
        j = pl.program_id(0)
        for ref, f0, n in zip(refs[:-1], first, count):
            @pl.when((j >= f0) & (j < f0 + n))
            def _(ref=ref):
                o_ref[...] = jnp.dot(a_ref[...], ref[...], preferred_element_type=F32).astype(BF16)

    def spec(f0, n):
        return pl.BlockSpec((s, GW_TILE), lambda j: (0, jnp.clip(j - f0, 0, n - 1)))

    return pl.pallas_call(
        body, name="grad_w_in", grid=(sum(count),),
        in_specs=[_resident((ka, s))] + [spec(f0, n) for f0, n in zip(first, count)],
        out_specs=pl.BlockSpec((ka, GW_TILE), lambda j: (0, j)),
        out_shape=jax.ShapeDtypeStruct((ka, sum(widths)), BF16),
        compiler_params=_cparams(56, ("arbitrary",)),
    )(h1t, *pieces)


def _mix_bwd(dy, w_out, o, cv, bcu, ga, gc, gsum, after, *, tm):
    s = dy.shape[0]

    def group_norm_bwd(dn_out, v, g, gs):
        r = lax.rsqrt(_group_sum(v * v, gs) * (1.0 / DH) + EPS)
        n = v * r
        dn = dn_out * g
        return r * (dn - n * (_group_sum(dn * n, gs) * (1.0 / DH))), dn_out * n

    def body(dy_ref, w_ref, o_ref, cv_ref, bcu_ref, ga_ref, gc_ref, gs_ref, after_ref,
             do_ref, dl_ref, dcv_ref, db_ref, dga_ref, dgc_ref):
        @pl.when(pl.program_id(0) == 0)
        def _():
            dga_ref[...] = jnp.zeros_like(dga_ref)
            dgc_ref[...] = jnp.zeros_like(dgc_ref)

        dm = lax.dot_general(dy_ref[...], w_ref[...], NT, preferred_element_type=F32)
        ov = o_ref[...]
        do, dga = group_norm_bwd(dm[:, 0:AW], ov, ga_ref[...], gs_ref[...])
        dob = do.astype(BF16)
        do_ref[...] = dob
        dl_ref[...] = _group_sum(dob.astype(F32) * ov, gs_ref[...])
        dga_ref[...] += _fold8(dga)
        gate_b = bcu_ref[:, 0:CW].astype(F32)
        cv = cv_ref[...]
        dconv, dgc = group_norm_bwd(dm[:, AW:D], gate_b * cv, gc_ref[...], gs_ref[...])
        dgc_ref[...] += _fold8(dgc)
        dcv_ref[...] = dconv * gate_b
        db_ref[...] = (dconv * cv).astype(BF16)

    return pl.pallas_call(
        body, name="mix_bwd", grid=(s // tm,),
        in_specs=[_rows(tm, D), _resident((D, D)), _rows(tm, AW), _rows(tm, CW), _rows(tm, 3 * CW),
                  _full((1, AW)), _full((1, CW)), _full((GS, GS)), ANY],
        out_specs=[_rows(tm, AW), _rows(tm, AW), _rows(tm, CW), _rows(tm, CW),
                   _full((SUBLANES, AW)), _full((SUBLANES, CW))],
        out_shape=[jax.ShapeDtypeStruct((s, AW), BF16), jax.ShapeDtypeStruct((s, AW), F32),
                   jax.ShapeDtypeStruct((s, CW), F32), jax.ShapeDtypeStruct((s, CW), BF16),
                   jax.ShapeDtypeStruct((SUBLANES, AW), F32), jax.ShapeDtypeStruct((SUBLANES, CW), F32)],
        compiler_params=_cparams(48, ("arbitrary",)),
    )(dy, w_out, o, cv, bcu, ga, gc, gsum, after)


def _conv_bwd(dcv, db, bcu, cw8, *, tm):
    s = dcv.shape[0]
    nt = s // tm

    def body(dcv_ref, nxt_ref, db_ref, bcu_ref, halo_ref, cw_ref, dbcu_ref, dw_ref):
        i = pl.program_id(0)

        @pl.when(i == 0)
        def _():
            dw_ref[...] = jnp.zeros_like(dw_ref)

        z, z1, z2 = _conv_taps(bcu_ref, halo_ref, i == 0, tm)
        d = dcv_ref[...]
        dw_ref[0] += _fold8(d * z2)
        dw_ref[1] += _fold8(d * z1)
        dw_ref[2] += _fold8(d * z)
        nx = jnp.where(i == nt - 1, 0.0, nxt_ref[...])
        row = lax.broadcasted_iota(jnp.int32, (tm, CW), 0)
        d1 = jnp.where(row == tm - 1, nx[0:1, :], pltpu.roll(d, tm - 1, axis=0))
        d2 = jnp.where(row == tm - 2, nx[0:1, :], jnp.where(row == tm - 1, nx[1:2, :], pltpu.roll(d, tm - 2, axis=0)))
        dz = cw_ref[2:3, :] * d + cw_ref[1:2, :] * d1 + cw_ref[0:1, :] * d2
        dbcu_ref[:, 0:CW] = db_ref[...]
        dbcu_ref[:, CW:2 * CW] = (dz * bcu_ref[:, 2 * CW:3 * CW].astype(F32)).astype(BF16)
        dbcu_ref[:, 2 * CW:3 * CW] = (dz * bcu_ref[:, CW:2 * CW].astype(F32)).astype(BF16)

    return pl.pallas_call(
        body, name="conv_bwd", grid=(nt,),
        in_specs=[_rows(tm, CW),
                  pl.BlockSpec((SUBLANES, CW), lambda i: (jnp.minimum((i + 1) * (tm // SUBLANES), s // SUBLANES - 1), 0)),
                  _rows(tm, CW), _rows(tm, 3 * CW), _halo_before(tm, 3 * CW), _full((SUBLANES, CW))],
        out_specs=[_rows(tm, 3 * CW), _full((3, SUBLANES, CW))],
        out_shape=[jax.ShapeDtypeStruct((s, 3 * CW), BF16), jax.ShapeDtypeStruct((3, SUBLANES, CW), F32)],
        compiler_params=_cparams(48, ("arbitrary",)),
    )(dcv, dcv, db, bcu, bcu, cw8)


def _attn_bwd(qp, kp, v, do, lse, dl, mk, *, t):
    s = qp.shape[0]
    nq = s // t

    def body(q_ref, k_ref, v_ref, do_ref, lse_ref, dl_ref, mk_ref, dq_ref, dk_ref, dv_ref, dkx_ref, dq_acc):
        pi = pl.program_id(1)

        @pl.when(pi == 0)
        def _():
            dq_acc[...] = jnp.zeros_like(dq_acc)

        row = lax.broadcasted_iota(jnp.int32, (t, t), 0)
        col = lax.broadcasted_iota(jnp.int32, (t, t), 1)
        lane = lax.broadcasted_iota(jnp.int32, (t, 128), 1)

        def head_step(hh, qi, carry, modes):
            off = pl.multiple_of(qi * t, t)
            rows = pl.ds(off, t)
            q = q_ref[rows, HP * hh:HP * (hh + 1)]
            qt = q.T
            lse_col = lse_ref[rows, DH * hh:DH * hh + 1]
            dl_col = dl_ref[rows, DH * hh:DH * hh + 1]
            do2 = do_ref[rows, :]
            dom = jnp.where(lane < DH, do2 if hh == 0 else pltpu.roll(do2, DH, axis=1), jnp.zeros((), BF16))
            new, dss = [], []
            for half, masked in enumerate(modes):
                if masked is None:
                    new.append(carry[half])
                    continue
                dk, dv, cs = carry[half]
                keys = slice(half * t, (half + 1) * t)
                m_col = mk_ref[half, rows, DH * hh:DH * hh + 1]
                scale = jnp.exp2(m_col - lse_col)
                sc = lax.dot_general(q, k_ref[keys, HP * hh:HP * (hh + 1)], NT, preferred_element_type=F32) - m_col
                if masked:
                    sc = jnp.where(col <= row, sc, -1e30)
                pt = jnp.exp2(sc).astype(BF16)
                dp = lax.dot_general(dom, v_ref[keys, HP * hh:HP * (hh + 1)], NT, preferred_element_type=F32)
                ds32 = (pt.astype(F32) * scale) * (dp - dl_col)
                ds = ds32.astype(BF16)
                cs = cs + _fold8(ds32)
                dv = dv + jnp.dot((dom.astype(F32) * scale).astype(BF16).T, pt, preferred_element_type=F32)
                dk = dk + jnp.dot(qt, ds, preferred_element_type=F32)
                new.append((dk, dv, cs))
                dss.append((half, ds))
            if len(dss) == 2:
                dq = jnp.dot(jnp.concatenate([dss[0][1], dss[1][1]], axis=1), k_ref[:, HP * hh:HP * (hh + 1)],
                             preferred_element_type=F32)
            else:
                half, ds = dss[0]
                dq = jnp.dot(ds, k_ref[half * t:(half + 1) * t, HP * hh:HP * (hh + 1)], preferred_element_type=F32)
            dq_acc[rows, HP * hh:HP * (hh + 1)] += dq
            return tuple(new)

        def step(qi, carry, modes):
            return tuple(head_step(hh, qi, carry[hh], modes) for hh in range(2))

        def two_heads(a0, a1):
            return jnp.where(lane < DH, a0, pltpu.roll(a1, DH, axis=1))

        def rows_to_lanes(a0, a1):
            return jnp.concatenate([a0, a1], axis=0).T

        zero = (jnp.zeros((HP, t), F32), jnp.zeros((128, t), F32), jnp.zeros((SUBLANES, t), F32))
        carry = step(2 * pi, ((zero, zero), (zero, zero)), (True, None))
        carry = step(2 * pi + 1, carry, (False, True))

        def pair(j, carry):
            qi = 2 * (pi + 1 + j)
            return step(qi + 1, step(qi, carry, (False, False)), (False, False))

        carry = lax.fori_loop(0, nq // 2 - 1 - pi, pair, carry)
        for half in range(2):
            keys = slice(half * t, (half + 1) * t)
            (dk0, dv0, cs0), (dk1, dv1, cs1) = carry[0][half], carry[1][half]
            dk_ref[keys, :] = (rows_to_lanes(dk0[0:DH], dk1[0:DH]) * LN2).astype(BF16)
            dv_ref[keys, :] = rows_to_lanes(dv0[0:DH], dv1[0:DH]).astype(BF16)
            total = lambda cs: jnp.broadcast_to(jnp.sum(cs, axis=0, keepdims=True), (DH, t))
            dkx_ref[keys, :] = rows_to_lanes(total(cs0), total(cs1))

        @pl.when(pi == nq // 2 - 1)
        def _():
            for c in range(s // t):
                rows = slice(c * t, (c + 1) * t)
                dq_ref[rows, :] = two_heads(dq_acc[rows, 0:HP], dq_acc[rows, HP:2 * HP]).astype(BF16)

    return pl.pallas_call(
        body, name="attn_bwd", grid=(H // 2, nq // 2),
        in_specs=[pl.BlockSpec((s, 2 * HP), lambda p, i: (0, p)),
                  pl.BlockSpec((2 * t, 2 * HP), lambda p, i: (i, p)),
                  pl.BlockSpec((2 * t, 2 * HP), lambda p, i: (i, p)),
                  pl.BlockSpec((s, 128), lambda p, i: (0, p)),
                  pl.BlockSpec((s, 128), lambda p, i: (0, p)),
                  pl.BlockSpec((s, 128), lambda p, i: (0, p)),
                  pl.BlockSpec((2, s, 128), lambda p, i: (i, 0, p))],
        out_specs=[pl.BlockSpec((s, 128), lambda p, i: (0, p)),
                   pl.BlockSpec((2 * t, 128), lambda p, i: (i, p)),
                   pl.BlockSpec((2 * t, 128), lambda p, i: (i, p)),
                   pl.BlockSpec((2 * t, 128), lambda p, i: (i, p))],
        out_shape=[jax.ShapeDtypeStruct((s, AW), BF16), jax.ShapeDtypeStruct((s, AW), BF16),
                   jax.ShapeDtypeStruct((s, AW), BF16), jax.ShapeDtypeStruct((s, AW), F32)],
        scratch_shapes=[pltpu.VMEM((s, 2 * HP), F32)],
        compiler_params=_cparams(56, ("arbitrary", "arbitrary")),
    )(qp, kp, v, do, lse, dl, mk)


def _forget_bwd(dkx, z, sel, *, tm):
    s = dkx.shape[0]
    nt = s // tm

    def body(dk_ref, z_ref, sel_ref, dfl_ref, dbf_ref, carry):
        @pl.when(pl.program_id(0) == 0)
        def _():
            carry[...] = jnp.zeros_like(carry)
            dbf_ref[...] = jnp.zeros_like(dbf_ref)

        dc = _split_dot(dk_ref[...], sel_ref[...])
        row = lax.broadcasted_iota(jnp.int32, (tm, tm), 0)
        col = lax.broadcasted_iota(jnp.int32, (tm, tm), 1)
        tri = (col >= row).astype(BF16)
        dlogf = _exact_dot01(tri, dc) + carry[0:1, :]
        carry[...] = jnp.broadcast_to(dlogf[0:1, :], carry.shape)
        dz = dlogf * (1.0 - jax.nn.sigmoid(z_ref[...]))
        dfl_ref[:, 0:128] = dz.astype(BF16)
        dfl_ref[:, 128:GW_TILE] = jnp.zeros((tm, GW_TILE - 128), BF16)
        dbf_ref[...] += _fold8(dz)

    rev = lambda i: (nt - 1 - i, 0)
    return pl.pallas_call(
        body, name="forget_bwd", grid=(nt,),
        in_specs=[pl.BlockSpec((tm, AW), rev), pl.BlockSpec((tm, 128), rev), _full((AW, 128))],
        out_specs=[pl.BlockSpec((tm, GW_TILE), rev), _full((SUBLANES, 128))],
        out_shape=[jax.ShapeDtypeStruct((s, GW_TILE), BF16), jax.ShapeDtypeStruct((SUBLANES, 128), F32)],
        scratch_shapes=[pltpu.VMEM((SUBLANES, 128), F32)],
        compiler_params=_cparams(48, ("arbitrary",)),
    )(dkx, z, sel)


def _in_proj_bwd(pieces, wp, x, g1, dx2, after, *, tm):
    s = x.shape[0]

    def body(q_ref, k_ref, v_ref, bcu_ref, f_ref, w_ref, x_ref, g_ref, dx2_ref, after_ref, dx_ref, dg_ref):
        @pl.when(pl.program_id(0) == 0)
        def _():
            dg_ref[...] = jnp.zeros_like(dg_ref)

        dh = None
        for ref, (lo, hi) in zip((q_ref, k_ref, v_ref, bcu_ref, f_ref), PIECES):
            part = lax.dot_general(ref[...], w_ref[:, lo:hi], NT, preferred_element_type=F32)
            dh = part if dh is None else dh + part
        _, n, r = _rms_fwd(x_ref[...], g_ref[...])
        dxn, dg = _rms_bwd(dh, n, r, g_ref[...])
        dx_ref[...] = dx2_ref[...] + dxn
        dg_ref[...] += _fold8(dg)

    return pl.pallas_call(
        body, name="in_proj_bwd", grid=(s // tm,),
        in_specs=[_rows(tm, hi - lo) for lo, hi in PIECES]
        + [_resident((D, WP)), _rows(tm, D), _full((1, D)), _rows(tm, D), ANY],
        out_specs=[_rows(tm, D), _full((SUBLANES, D))],
        out_shape=[jax.ShapeDtypeStruct((s, D), F32), jax.ShapeDtypeStruct((SUBLANES, D), F32)],
        compiler_params=_cparams(56, ("arbitrary",)),
    )(*pieces, wp, x, g1, dx2, after)


def _position():
    return lax.axis_index("x"), lax.axis_index("y"), lax.axis_index("c")


ANY = pl.BlockSpec(memory_space=pl.ANY)


def _all_gather(shards):
    n = len(shards)

    def body(*refs):
        x_refs, out_refs = refs[:n], refs[n:2 * n]
        send_sems, recv_sems, local_sems = refs[2 * n:]
        x, y, c = _position()
        me, sibling = (x, y, c), (x, y, 1 - c)
        chips = [(1 - x, y), (x, 1 - y), (1 - x, 1 - y)]

        def copy(a, k, block, to, own=False):
            slot = out_refs[a].at[4 * block[0] + 2 * block[1] + block[2]]
            return pltpu.make_async_remote_copy(
                src_ref=x_refs[a] if own else slot, dst_ref=slot,
                send_sem=send_sems.at[7 * a + k], recv_sem=recv_sems.at[7 * a + k], device_id=to, device_id_type=MESH_ID)

        mine = [pltpu.make_async_copy(x_refs[a], out_refs[a].at[4 * x + 2 * y + c], local_sems.at[a]) for a in range(n)]
        for cp in mine:
            cp.start()
        first = []
        for a in range(n):
            first.append(copy(a, 0, me, sibling, own=True))
            first += [copy(a, 1 + j, me, (*chip, c), own=True) for j, chip in enumerate(chips)]
        for cp in first:
            cp.start()
        passed = []
        for j, chip in enumerate(chips):
            for a in range(n):
                copy(a, 1 + j, (*chip, c), me).wait_recv()
                fwd = copy(a, 4 + j, (*chip, c), sibling)
                fwd.start()
                passed.append(fwd)
        for a in range(n):
            copy(a, 0, sibling, me).wait_recv()
            for j, chip in enumerate(chips):
                copy(a, 4 + j, (*chip, 1 - c), me).wait_recv()
        for cp in first + passed:
            cp.wait_send()
        for cp in mine:
            cp.wait()

    return pl.pallas_call(
        body, name="all_gather_weights",
        out_shape=[jax.ShapeDtypeStruct((NDEV,) + sh.shape, sh.dtype) for sh in shards],
        in_specs=[ANY] * n, out_specs=[ANY] * n,
        scratch_shapes=[pltpu.SemaphoreType.DMA((7 * n,)), pltpu.SemaphoreType.DMA((7 * n,)), pltpu.SemaphoreType.DMA((n,))],
    )(*shards)


def _pair_exchange(grads):
    n = len(grads)

    def body(*refs):
        g_refs, out_refs = refs[:n], refs[n:2 * n]
        send_sems, recv_sems = refs[2 * n:]
        x, y, c = _position()
        copies = [pltpu.make_async_remote_copy(
            src_ref=g_refs[a].at[:, pl.ds(1 - c, 1)], dst_ref=out_refs[a], send_sem=send_sems.at[a],
            recv_sem=recv_sems.at[a], device_id=(x, y, 1 - c), device_id_type=MESH_ID) for a in range(n)]
        for cp in copies:
            cp.start()
        for cp in copies:
            cp.wait()

    return pl.pallas_call(
        body, name="grad_pair_exchange",
        out_shape=[jax.ShapeDtypeStruct((4, 1) + g.shape[2:], g.dtype) for g in grads],
        in_specs=[ANY] * n, out_specs=[ANY] * n,
        scratch_shapes=[pltpu.SemaphoreType.DMA((n,)), pltpu.SemaphoreType.DMA((n,))],
    )(*grads)


def _pair_sum(g, got, idx, *, tr, name):
    r, c = g.shape[2:]

    def body(idx_ref, g_ref, got_ref, pb_ref, own_ref):
        p = g_ref[0, 0].astype(F32) + got_ref[0, 0].astype(F32)
        pb_ref[0] = p.astype(BF16)

        @pl.when(pl.program_id(1) == idx_ref[1])
        def _():
            own_ref[...] = p

    return pl.pallas_call(
        body, name=name,
        grid_spec=pltpu.PrefetchScalarGridSpec(
            num_scalar_prefetch=1, grid=(r // tr, 4),
            in_specs=[pl.BlockSpec((1, 1, tr, c), lambda i, j, idx: (j, idx[0], i, 0)),
                      pl.BlockSpec((1, 1, tr, c), lambda i, j, idx: (j, 0, i, 0))],
            out_specs=[pl.BlockSpec((1, tr, c), lambda i, j, idx: (j, i, 0)),
                       pl.BlockSpec((tr, c), lambda i, j, idx: (i, 0))]),
        out_shape=[jax.ShapeDtypeStruct((4, r, c), BF16), jax.ShapeDtypeStruct((r, c), F32)],
        compiler_params=_cparams(32, ("arbitrary", "arbitrary")),
    )(idx, g, got)


HBM = pl.BlockSpec(memory_space=pltpu.HBM)
SEM = pl.BlockSpec(memory_space=pltpu.SEMAPHORE)
DATAFLOW = pltpu.SideEffectType.DATAFLOW_SIDE_EFFECTING


PEERS = {"gather": NDEV - 1, "scatter": NDEV - 1, "chips": 3}


def _exchange_copies(src_refs, land_refs, send_sems, recv_sems, mode):
    x, y, c = _position()
    me, my_chip = 4 * x + 2 * y + c, 2 * x + y
    npeers = PEERS[mode]
    copies, own = [], []
    for a, (s_ref, l_ref) in enumerate(zip(src_refs, land_refs)):
        for k in range(npeers):
            if mode == "chips":
                px, py, pc = x ^ ((k + 1) >> 1), y ^ ((k + 1) & 1), c
                src, dst = s_ref.at[2 * px + py], l_ref.at[my_chip]
            else:
                px, py, pc = x ^ ((k + 1) >> 2), y ^ (((k + 1) >> 1) & 1), c ^ ((k + 1) & 1)
                src, dst = (s_ref.at[4 * px + 2 * py + pc] if mode == "scatter" else s_ref), l_ref.at[me]
            copies.append(pltpu.make_async_remote_copy(
                src_ref=src, dst_ref=dst, send_sem=send_sems.at[npeers * a + k], recv_sem=recv_sems.at[npeers * a + k],
                device_id=(px, py, pc), device_id_type=MESH_ID))
        slot = my_chip if mode == "chips" else me
        own.append(pltpu.make_async_copy(s_ref if mode == "gather" else s_ref.at[slot], l_ref.at[slot],
                                         send_sems.at[npeers * len(src_refs) + a]))
    return copies, own


def _exchange_start(srcs, lands, after, *, mode, name):
    n = len(srcs)
    nsem = PEERS[mode] * n

    def body(*refs):
        token = refs[-1]
        copies, own = _exchange_copies(refs[:n], refs[n:2 * n], refs[2 * n + 1], refs[2 * n + 2], mode)
        for cp in copies + own:
            cp.start()
        token[...] = jnp.zeros_like(token)

    arrays = list(srcs) + list(lands)
    outs = pl.pallas_call(
        body, name=name,
        out_shape=(pltpu.SemaphoreType.DMA((nsem + n,)), pltpu.SemaphoreType.DMA((nsem,)),
                   *[pltpu.HBM(a.shape, a.dtype) for a in arrays], jax.ShapeDtypeStruct((SUBLANES, LANES), F32)),
        in_specs=[HBM] * (2 * n) + [ANY],
        out_specs=(SEM, SEM, *[HBM] * (2 * n), pl.BlockSpec(memory_space=pltpu.VMEM)),
        input_output_aliases={i: 2 + i for i in range(2 * n)},
        compiler_params=pltpu.CompilerParams(has_side_effects=DATAFLOW),
    )(*[pltpu.with_memory_space_constraint(a, pltpu.HBM) for a in arrays], after)
    return outs[0], outs[1], outs[2:2 + n], outs[2 + n:2 + 2 * n], outs[-1]


def _exchange_wait(send_sems, recv_sems, srcs, lands, after, *, mode, name):
    n = len(srcs)

    def body(*refs):
        copies, own = _exchange_copies(refs[:n], refs[n:2 * n], refs[2 * n], refs[2 * n + 1], mode)
        for cp in copies:
            cp.wait_send()
            cp.wait_recv()
        for cp in own:
            cp.wait()

    arrays = list(srcs) + list(lands)
    outs = pl.pallas_call(
        body, name=name,
        out_shape=tuple(pltpu.HBM(a.shape, a.dtype) for a in arrays),
        in_specs=[HBM] * (2 * n) + [SEM, SEM, ANY],
        out_specs=tuple([HBM] * (2 * n)),
        input_output_aliases={i: i for i in range(2 * n)},
        compiler_params=pltpu.CompilerParams(has_side_effects=DATAFLOW),
    )(*arrays, send_sems, recv_sems, after)
    return outs[n:]


def _small_all_reduce(parts):
    def body(gmp_ref, gmo_ref, gfp_ref, gfo_ref, ga_ref, gc_ref, dw_ref, bf_ref, loss_ref,
             out_ref, buf, send_sems, recv_sems):
        x, y, c = _position()
        me = 4 * x + 2 * y + c

        def colsum(v):
            return jnp.sum(v, axis=0, keepdims=True)

        loss = jnp.sum(colsum(loss_ref[...]), axis=1, keepdims=True) * (0.5 / D)
        rows = [colsum(gmp_ref[...]), colsum(gmo_ref[...]), colsum(gfp_ref[...]), colsum(gfo_ref[...]),
                jnp.concatenate([colsum(ga_ref[...]), colsum(gc_ref[...])], axis=1),
                jnp.concatenate([colsum(dw_ref[0]), colsum(dw_ref[1])], axis=1),
                jnp.concatenate([colsum(dw_ref[2]), colsum(bf_ref[...]), jnp.broadcast_to(loss, (1, 128)),
                                 jnp.zeros((1, 256), F32)], axis=1),
                jnp.zeros((1, D), F32)]
        buf[me] = jnp.concatenate(rows, axis=0)
        copies = []
        for mm in range(1, NDEV):
            peer = (x ^ (mm >> 2), y ^ ((mm >> 1) & 1), c ^ (mm & 1))
            copies.append(pltpu.make_async_remote_copy(
                src_ref=buf.at[me], dst_ref=buf.at[me], send_sem=send_sems.at[mm - 1], recv_sem=recv_sems.at[mm - 1],
                device_id=peer, device_id_type=MESH_ID))
        for cp in copies:
            cp.start()
        for cp in copies:
            cp.wait_recv()
        for cp in copies:
            cp.wait_send()
        acc = buf[0]
        for d in range(1, NDEV):
            acc = acc + buf[d]
        out_ref[...] = acc

    vm = pl.BlockSpec(memory_space=pltpu.VMEM)
    return pl.pallas_call(
        body, name="small_all_reduce",
        out_shape=jax.ShapeDtypeStruct((SUBLANES, D), F32),
        in_specs=[vm] * len(parts), out_specs=vm,
        scratch_shapes=[pltpu.VMEM((NDEV, SUBLANES, D), F32), pltpu.SemaphoreType.DMA((7,)), pltpu.SemaphoreType.DMA((7,))],
    )(*parts)


def _adam_update(w, g, m, v):
    nm = ADAM_B1 * m + (1.0 - ADAM_B1) * g
    nv = ADAM_B2 * v + (1.0 - ADAM_B2) * (g * g)
    m_hat = nm / (1.0 - ADAM_B1 ** ADAM_STEP)
    v_hat = nv / (1.0 - ADAM_B2 ** ADAM_STEP)
    return -ADAM_LR * (m_hat / (jnp.sqrt(v_hat) + ADAM_EPS) + ADAM_WD * w), nm, nv


SMALL_SLOTS = {"g_mix_pre": (0, 0, D), "g_mix_post": (1, 0, D), "g_ffn_pre": (2, 0, D), "g_ffn_post": (3, 0, D),
               "g_attn_out": (4, 0, AW), "g_conv_out": (4, AW, CW), "b_forget": (6, CW, H)}
LOSS_LANE = CW + 128


def _small_adamw(small, conv_grad, params):
    names = list(params)
    n = len(names)

    def body(*refs):
        small_ref, cg_ref = refs[0], refs[1]
        ins, outs = refs[2:2 + 3 * n], refs[2 + 3 * n:]
        for i, name in enumerate(names):
            w_ref, m_ref, v_ref = ins[3 * i:3 * i + 3]
            g_ref, d_ref, nm_ref, nv_ref = outs[4 * i:4 * i + 4]
            if name == "conv_w":
                g = cg_ref[...]
            else:
                r, c0, width = SMALL_SLOTS[name]
                g = small_ref[r:r + 1, c0:c0 + width]
            g_ref[...] = g
            d_ref[...], nm_ref[...], nv_ref[...] = _adam_update(w_ref[...], g, m_ref[...], v_ref[...])
        outs[4 * n][...] = small_ref[6:7, LOSS_LANE:LOSS_LANE + 1]

    vm = pl.BlockSpec(memory_space=pltpu.VMEM)
    flat = [a for name in names for a in params[name]]
    outs = pl.pallas_call(
        body, name="adamw_small",
        in_specs=[vm] * (2 + 3 * n), out_specs=[vm] * (4 * n + 1),
        out_shape=[jax.ShapeDtypeStruct(params[name][0].shape, F32) for name in names for _ in range(4)]
        + [jax.ShapeDtypeStruct((1, 1), F32)],
    )(small, conv_grad, *flat)
    return {name: outs[4 * i:4 * i + 4] for i, name in enumerate(names)}, outs[4 * n].reshape(())


def _chip_sum_adamw(got, own, idx, wt, mt, vt, *, tr, name):
    cols, rows = wt.shape
    gcols = own.shape[1]

    def body(idx_ref, got_ref, own_ref, w_ref, m_ref, v_ref, g_ref, d_ref, nm_ref, nv_ref):
        g = jnp.zeros((tr, gcols), F32)
        for j in range(4):
            g = g + jnp.where(idx_ref[1] == j, own_ref[...], got_ref[j].astype(F32))
        g = g.T[:cols]
        g_ref[...] = g
        d_ref[...], nm_ref[...], nv_ref[...] = _adam_update(w_ref[...], g, m_ref[...], v_ref[...])

    spec = pl.BlockSpec((cols, tr), lambda i, idx: (0, i))
    gspec = pl.BlockSpec((tr, gcols), lambda i, idx: (i, 0))
    return pl.pallas_call(
        body, name=name,
        grid_spec=pltpu.PrefetchScalarGridSpec(
            num_scalar_prefetch=1, grid=(rows // tr,),
            in_specs=[pl.BlockSpec((4, tr, gcols), lambda i, idx: (0, i, 0)), gspec, spec, spec, spec],
            out_specs=[spec] * 4),
        out_shape=[jax.ShapeDtypeStruct((cols, rows), F32)] * 4,
        compiler_params=_cparams(32, ("arbitrary",)),
    )(idx, got, own, wt, mt, vt)


def _device_sum_adamw(land, w, m, v, *, tr, name):
    rows, cols = w.shape

    def body(land_ref, w_ref, m_ref, v_ref, g_ref, d_ref, nm_ref, nv_ref):
        g = land_ref[0].astype(F32)
        for dev in range(1, NDEV):
            g = g + land_ref[dev].astype(F32)
        g_ref[...] = g
        d_ref[...], nm_ref[...], nv_ref[...] = _adam_update(w_ref[...], g, m_ref[...], v_ref[...])

    spec = pl.BlockSpec((tr, cols), lambda i: (i, 0))
    return pl.pallas_call(
        body, name=name, grid=(rows // tr,),
        in_specs=[pl.BlockSpec((NDEV, tr, cols), lambda i: (0, i, 0)), spec, spec, spec],
        out_specs=[spec] * 4,
        out_shape=[jax.ShapeDtypeStruct((rows, cols), F32)] * 4,
        compiler_params=_cparams(32, ("arbitrary",)),
    )(land, w, m, v)


def _placement_constants():
    j = np.arange(128)[:, None]
    lane = np.arange(1024)[None, :]
    head, sub = lane // HP, lane % HP
    piece, jh = j // H, j % H
    valid = (j < 3 * H) & (jh == head)
    pq = np.where(valid & (sub == DH + piece), 1.0, 0.0).astype(BF16)
    pk = np.where(valid & (sub == DH + 3 + piece), -1.0, 0.0).astype(BF16)
    oq = np.where((sub >= DH + 3) & (sub < DH + 6), 1.0, 0.0).astype(np.float32)
    ok = np.where((sub >= DH) & (sub < DH + 3), 1.0, 0.0).astype(np.float32)
    r = np.arange(AW)[:, None]
    cc = np.arange(128)[None, :]
    sel = np.where((r % DH == 3) & (r // DH == cc), -1.0, 0.0).astype(BF16)
    gi = np.arange(GS)
    gsum = (gi[:, None] // DH == gi[None, :] // DH).astype(BF16)
    return tuple(jnp.asarray(c) for c in (pq, pk, oq, ok, sel, gsum))


def _local_step(xs, tgt, wp, late_weights, cw8, bfp, g_attn_out, g_conv_out,
                g_mix_pre, g_mix_post, g_ffn_pre, g_ffn_post, early_grads=None, last_grad=None):
    pq, pk, oq, ok, sel, gsum = _placement_constants()
    h1t, qp, kp, vv, bcu, zf = _in_proj(xs, g_mix_pre, wp, bfp, pq, pk, oq, ok, tm=512)
    o, lse, mk = _attn_fwd(qp, kp, vv, t=512)
    w_out_f, wgu, wd = late_weights(lse)
    merged, y, x2, cv, h2 = _mix_out(o, bcu, cw8, g_attn_out, g_conv_out, gsum, w_out_f, xs, g_mix_post, g_ffn_pre, tm=512)
    gate, up, act, dx3, dff, loss_p, dg_ffn_post = _ffn_fwd_loss(h2, wgu, wd, x2, tgt, g_ffn_post, tm=512)

    dgu, dx2, dy, dg_ffn_pre, dg_mix_post = _ffn_bwd(dff, wd, gate, up, wgu, x2, g_ffn_pre, dx3, y, g_mix_post, tm=256)
    dw_down = _grad_matmul(act, dff, ta=DFF // 2, tb=D, ts=4096, name="grad_w_down", vmem_mb=60)
    dw_gu = _grad_matmul(dgu, h2, ta=DFF // 2, tb=D, ts=4096, name="grad_w_gate_up", vmem_mb=60).reshape(NDEV, FB, D)
    dw_out = _grad_matmul(merged, dy, ta=1024, tb=1024, ts=2048, name="grad_w_out")
    token = early_grads(dw_out, dw_gu, dw_down) if early_grads is not None else dw_out
    do, dl, dcv, db, dg_attn, dg_conv = _mix_bwd(dy, w_out_f, o, cv, bcu, g_attn_out, g_conv_out, gsum, token, tm=512)
    dbcu, dtaps = _conv_bwd(dcv, db, bcu, cw8, tm=512)
    dqp, dkp, dv, dkx = _attn_bwd(qp, kp, vv, do, lse, dl, mk, t=512)
    dfl, dbf = _forget_bwd(dkx, zf, sel, tm=512)
    pieces = (dqp, dkp, dv, dbcu, dfl)
    dwp = _grad_w_in(h1t, pieces)
    token = last_grad(dwp) if last_grad is not None else dwp
    grad_x, dg_mix_pre = _in_proj_bwd(pieces, wp, xs, g_mix_pre, dx2, token, tm=512)
    return (grad_x, dwp, dw_out, dw_gu, dw_down, dg_mix_pre, dg_mix_post, dg_ffn_pre, dg_ffn_post, dg_attn, dg_conv,
            dtaps, dbf, loss_p)


BIG_TILES = {"w_in": 256, "w_out": 128, "w_gate_up": 176, "w_down": 176}


def kernel(x, w_in, b_forget, conv_w, g_attn_out, g_conv_out, w_out, g_mix_pre, g_mix_post, w_gate_up, w_down, g_ffn_pre, g_ffn_post, loss_target, m_w_in, m_b_forget, m_conv_w, m_g_attn_out, m_g_conv_out, m_w_out, m_g_mix_pre, m_g_mix_post, m_w_gate_up, m_w_down, m_g_ffn_pre, m_g_ffn_post, v_w_in, v_b_forget, v_conv_w, v_g_attn_out, v_g_conv_out, v_w_out, v_g_mix_pre, v_g_mix_post, v_w_gate_up, v_w_down, v_g_ffn_pre, v_g_ffn_post):
    xc, yc, cc = _position()
    my_chip = 2 * xc + yc
    me = 2 * my_chip + cc
    idx = jnp.stack([cc, my_chip]).astype(jnp.int32)
    tables = _in_layout_tables()

    w_in_b = w_in[0].astype(BF16)
    g_in, g_last, g_taps = _all_gather([w_in_b[:, :IN_MAIN], w_in_b[:, IN_MAIN].reshape(SUBLANES, LANES), conv_w[0]])
    last_cols = jnp.pad(g_last.reshape(NDEV, D).T.astype(F32), ((0, 0), (0, LANES - NDEV)))
    wp = _assemble_w_in(g_in, last_cols, tables, tr=256)
    cw8 = jnp.pad(g_taps.transpose(1, 0, 2).reshape(3, CW), ((0, SUBLANES - 3), (0, 0)))

    late = [w_out[0].astype(BF16), w_gate_up[0].T.astype(BF16), w_down[0].astype(BF16)]
    ssem, rsem, late_thru, land_thru, token = _exchange_start(
        late, [lax.empty((NDEV,) + s.shape, s.dtype) for s in late], g_in, mode="gather",
        name="gather_late_start")
    bfp = jnp.pad(b_forget, ((0, 0), (0, 128 - H))) + token[0:1, :]

    def late_weights(after):
        l_out, l_gu, l_down = _exchange_wait(ssem, rsem, late_thru, land_thru, after, mode="gather", name="gather_late_wait")
        return l_out.reshape(D, D), l_gu.reshape(2, DFF, D), l_down.reshape(DFF, D)

    early = {}

    def early_grads(dw_out, dw_gu, dw_down):
        srcs = [dw_out.reshape(NDEV, D // NDEV, D), dw_gu, dw_down.reshape(NDEV, DFF // NDEV, D)]
        lands = [lax.empty(s.shape, s.dtype) for s in srcs]
        early["handles"] = _exchange_start(srcs, lands, dw_out, mode="scatter", name="scatter_early_start")
        return early["handles"][4]

    last = {}

    def last_grad(dwp):
        g_w_in = _disassemble_w_in(dwp, tables, tr=256).reshape(4, 2, D, IN_PAD)
        (from_sibling,) = _pair_exchange([g_w_in])
        pair_b, last["own"] = _pair_sum(g_w_in, from_sibling, idx, tr=BIG_TILES["w_in"], name="grad_pair_sum_w_in")
        last["handles"] = _exchange_start([pair_b], [lax.empty(pair_b.shape, pair_b.dtype)], last["own"], mode="chips",
                                          name="chips_w_in_start")
        return last["handles"][4]

    (grad_x, dwp, dw_out, dw_gu, dw_down, dg_mix_pre, dg_mix_post, dg_ffn_pre, dg_ffn_post, dg_attn, dg_conv,
     dtaps, dbf, loss_p) = _local_step(x[0], loss_target[0], wp, late_weights, cw8, bfp, g_attn_out, g_conv_out,
                                        g_mix_pre, g_mix_post, g_ffn_pre, g_ffn_post, early_grads, last_grad)

    e_ssem, e_rsem, e_srcs, e_lands, _ = early["handles"]
    land_out, land_gu, land_down = _exchange_wait(e_ssem, e_rsem, e_srcs, e_lands, dg_mix_pre, mode="scatter",
                                                  name="scatter_early_wait")
    res = {}
    big = {"w_out": (land_out, w_out[0], m_w_out[0], v_w_out[0]),
           "w_gate_up": (land_gu, w_gate_up[0].T, m_w_gate_up[0].T, v_w_gate_up[0].T),
           "w_down": (land_down, w_down[0], m_w_down[0], v_w_down[0])}
    for name, (land, w, m, v) in big.items():
        outs = _device_sum_adamw(land, w, m, v, tr=BIG_TILES[name], name="adamw_" + name)
        res[name] = [(o.T if name == "w_gate_up" else o)[None] for o in outs]
    c_ssem, c_rsem, c_srcs, c_lands, _ = last["handles"]
    after = sum(res[n][1][0, :SUBLANES, :LANES] for n in big)
    (from_chips,) = _exchange_wait(c_ssem, c_rsem, c_srcs, c_lands, after, mode="chips", name="chips_w_in_wait")
    outs = _chip_sum_adamw(from_chips, last["own"], idx, w_in[0].T, m_w_in[0].T, v_w_in[0].T,
                           tr=BIG_TILES["w_in"], name="adamw_w_in")
    res["w_in"] = [o.T[None] for o in outs]

    small = _small_all_reduce([dg_mix_pre, dg_mix_post, dg_ffn_pre, dg_ffn_post, dg_attn, dg_conv, dtaps, dbf, loss_p])
    taps_full = jnp.concatenate([small[5:6, :CW], small[5:6, CW:], small[6:7, :CW]], axis=0)
    taps_first = lambda a: a.transpose(1, 0, 2)
    smalls = {"b_forget": (b_forget, m_b_forget, v_b_forget),
              "conv_w": (taps_first(conv_w), taps_first(m_conv_w), taps_first(v_conv_w)),
              "g_attn_out": (g_attn_out, m_g_attn_out, v_g_attn_out), "g_conv_out": (g_conv_out, m_g_conv_out, v_g_conv_out),
              "g_mix_pre": (g_mix_pre, m_g_mix_pre, v_g_mix_pre), "g_mix_post": (g_mix_post, m_g_mix_post, v_g_mix_post),
              "g_ffn_pre": (g_ffn_pre, m_g_ffn_pre, v_g_ffn_pre), "g_ffn_post": (g_ffn_post, m_g_ffn_post, v_g_ffn_post)}
    own_taps = lax.dynamic_slice(taps_full, (0, me * 64), (3, 64))[:, None, :]
    small_res, loss = _small_adamw(small, own_taps, smalls)
    for name, outs in small_res.items():
        res[name] = [taps_first(o) for o in outs] if name == "conv_w" else list(outs)

    order = ["w_in", "b_forget", "conv_w", "g_attn_out", "g_conv_out", "w_out", "g_mix_pre", "g_mix_post",
             "w_gate_up", "w_down", "g_ffn_pre", "g_ffn_post"]
    outs = [loss, grad_x[None]]
    for k in range(4):
        outs += [res[n][k] for n in order]
    return tuple(outs)
```

```python
import functools

import numpy as np

import jax
import jax.numpy as jnp
from jax import lax
from jax.experimental import pallas as pl
from jax.experimental.pallas import tpu as pltpu

F32 = jnp.float32
BF16 = jnp.bfloat16
MESH_ID = pl.DeviceIdType.MESH

D = 1024
H = 8
DH = 64
AW = 512
CW = 512
DFF = 2816
FB = DFF // 4
FF_CHUNKS = ((0, 768), (768, 768), (1536, 768), (2304, 512))
FF_CHUNKS_BWD = ((0, 1024), (1024, 1024), (2048, 768))
HP = 128
OFF_Q, OFF_K, OFF_V, OFF_BCU, OFF_F = 0, 512, 1024, 1536, 3072
WP = OFF_F + 128
PIECES = ((OFF_Q, OFF_K), (OFF_K, OFF_V), (OFF_V, OFF_BCU), (OFF_BCU, OFF_F), (OFF_F, WP))
EPS = 1e-6
LOG2E, LN2 = 1.4426950408889634, 0.6931471805599453
NDEV = 8
LANES = 128
SUBLANES = 8
IN_COLS = 385
IN_PAD = 512
IN_MAIN = 384
WIN = 640
ADAM_LR, ADAM_B1, ADAM_B2, ADAM_EPS, ADAM_WD, ADAM_STEP = 0.001, 0.9, 0.999, 1e-08, 0.01, 10

NT = (((1,), (1,)), ((), ()))
TN = (((0,), (0,)), ((), ()))


def _cparams(vmem_mb=None, sem=None):
    kw = {}
    if vmem_mb is not None:
        kw["vmem_limit_bytes"] = vmem_mb << 20
    if sem is not None:
        kw["dimension_semantics"] = sem
    return pltpu.CompilerParams(**kw)


def _full(shape):
    return pl.BlockSpec(shape, lambda *_: (0,) * len(shape))


def _resident(shape):
    return pl.BlockSpec(shape, lambda *_: (0,) * len(shape), pipeline_mode=pl.Buffered(1))


def _rows(tm, width):
    return pl.BlockSpec((tm, width), lambda i: (i, 0))


def _fold8(v):
    r, w = v.shape
    return jnp.sum(v.reshape(r // SUBLANES, SUBLANES, w), axis=0)


def _split_dot(v, m01):
    hi = v.astype(BF16)
    lo = (v - hi.astype(F32)).astype(BF16)
    return (jnp.dot(hi, m01, preferred_element_type=F32)
            + jnp.dot(lo, m01, preferred_element_type=F32))


GS = 256


def _group_sum(v, g01):
    parts = [_split_dot(v[:, c:c + GS], g01) for c in range(0, v.shape[1], GS)]
    return parts[0] if len(parts) == 1 else jnp.concatenate(parts, axis=1)


def _exact_dot01(m01, v):
    p1 = v.astype(BF16)
    r1 = v - p1.astype(F32)
    p2 = r1.astype(BF16)
    p3 = (r1 - p2.astype(F32)).astype(BF16)
    return (jnp.dot(m01, p1, preferred_element_type=F32) + jnp.dot(m01, p2, preferred_element_type=F32)
            + jnp.dot(m01, p3, preferred_element_type=F32))


def _rms_fwd(v, g):
    r = lax.rsqrt(jnp.mean(v * v, axis=-1, keepdims=True) + EPS)
    n = v * r
    return n * g, n, r


def _rms_bwd(do, n, r, g):
    dn = do * g
    return r * (dn - n * jnp.mean(dn * n, axis=-1, keepdims=True)), do * n


def _padded_column(n):
    if n < AW:
        return OFF_Q + n, 0.125
    if n < 3 * AW:
        return n, 1.0
    if n < 3 * AW + H:
        return OFF_F + n - 3 * AW, 1.0
    return OFF_BCU + n - 3 * AW - H, 1.0


def _in_layout_tables():
    dest = -np.ones((IN_PAD, LANES), np.int32)
    dest_f = -np.ones((IN_PAD, LANES), np.int32)
    scale = np.zeros((IN_PAD, LANES), np.float32)
    starts = []
    for k in range(NDEV):
        cols = [_padded_column(IN_COLS * k + j) for j in range(IN_COLS)]
        main = [c for c, _ in cols if c < OFF_F]
        ws = min((min(main) // LANES) * LANES, OFF_F - WIN)
        assert ws <= min(main) and max(main) < ws + WIN
        starts.append(ws)
        for j, (c, sc) in enumerate(cols):
            scale[j, k] = sc
            if c < OFF_F:
                dest[j, k] = c - ws
            else:
                dest_f[j, k] = c - OFF_F
    f_shards = tuple(k for k in range(NDEV) if (dest_f[:, k] >= 0).any())
    return tuple(starts), f_shards, jnp.asarray(dest), jnp.asarray(dest_f), jnp.asarray(scale)


def _perm(dest_ref, scale_ref, k, width, rows=IN_PAD):
    lane = lax.broadcasted_iota(jnp.int32, (rows, width), 1)
    return jnp.where(dest_ref[0:rows, k:k + 1] == lane, scale_ref[0:rows, k:k + 1], 0.0).astype(BF16)


def _assemble_w_in(blocks, last_cols, tables, *, tr):
    starts, f_shards, dest, dest_f, scale = tables
    last = [_padded_column(IN_COLS * k + IN_MAIN) for k in range(NDEV)]
    f_main = [any(_padded_column(IN_COLS * k + j)[0] >= OFF_F for j in range(IN_MAIN)) for k in range(NDEV)]
    assert IN_COLS == IN_MAIN + 1

    def body(b_ref, c_ref, dest_ref, destf_ref, scale_ref, o_ref):
        o_ref[...] = jnp.zeros_like(o_ref)
        lane = lax.broadcasted_iota(jnp.int32, (tr, LANES), 1)
        for k in range(NDEV):
            b = b_ref[k]
            ws = starts[k]
            part = jnp.dot(b, _perm(dest_ref, scale_ref, k, WIN, IN_MAIN), preferred_element_type=F32)
            o_ref[:, ws:ws + WIN] += part.astype(BF16)
            if f_main[k]:
                part = jnp.dot(b, _perm(destf_ref, scale_ref, k, 128, IN_MAIN), preferred_element_type=F32)
                o_ref[:, OFF_F:WP] += part.astype(BF16)
            col, sc = last[k]
            tile = (col // LANES) * LANES
            o_ref[:, tile:tile + LANES] += jnp.where(lane == col - tile, c_ref[:, k:k + 1] * sc, 0.0).astype(BF16)

    tab = _full((IN_PAD, LANES))
    return pl.pallas_call(
        body, name="assemble_w_in", grid=(D // tr,),
        in_specs=[pl.BlockSpec((NDEV, tr, IN_MAIN), lambda i: (0, i, 0)), _rows(tr, LANES), tab, tab, tab],
        out_specs=_rows(tr, WP),
        out_shape=jax.ShapeDtypeStruct((D, WP), BF16),
        compiler_params=_cparams(48, ("arbitrary",)),
    )(blocks, last_cols, dest, dest_f, scale)


def _disassemble_w_in(dwp, tables, *, tr):
    starts, f_shards, dest, dest_f, scale = tables
    width = dwp.shape[1]

    def body(g_ref, dest_ref, destf_ref, scale_ref, o_ref):
        for k in range(NDEV):
            ws = starts[k]
            acc = lax.dot_general(g_ref[:, ws:ws + WIN], _perm(dest_ref, scale_ref, k, WIN), NT, preferred_element_type=F32)
            if k in f_shards:
                acc = acc + lax.dot_general(g_ref[:, OFF_F:WP], _perm(destf_ref, scale_ref, k, 128), NT,
                                            preferred_element_type=F32)
            o_ref[k] = acc.astype(BF16)

    tab = _full((IN_PAD, LANES))
    return pl.pallas_call(
        body, name="disassemble_w_in", grid=(D // tr,),
        in_specs=[_rows(tr, width), tab, tab, tab],
        out_specs=pl.BlockSpec((NDEV, tr, IN_PAD), lambda i: (0, i, 0)),
        out_shape=jax.ShapeDtypeStruct((NDEV, D, IN_PAD), BF16),
        compiler_params=_cparams(48, ("arbitrary",)),
    )(dwp, dest, dest_f, scale)


def _in_proj(x, g1, wp, bfp, pq, pk, oq, ok, *, tm):
    s = x.shape[0]

    def body(x_ref, g_ref, w_ref, bf_ref, pq_ref, pk_ref, oq_ref, ok_ref,
             ht_ref, qp_ref, kp_ref, v_ref, bcu_ref, z_ref, carry):
        @pl.when(pl.program_id(0) == 0)
        def _():
            carry[...] = jnp.zeros_like(carry)

        h = _rms_fwd(x_ref[...], g_ref[...])[0].astype(BF16)
        ht_ref[...] = h.T
        z = jnp.dot(h, w_ref[:, OFF_F:WP], preferred_element_type=F32) + bf_ref[...]
        z_ref[...] = z
        lane = lax.broadcasted_iota(jnp.int32, (tm, 128), 1)
        logf = jnp.where(lane < H, jnp.minimum(z, 0.0) - jnp.log(1.0 + jnp.exp(-jnp.abs(z))), 0.0)
        row = lax.broadcasted_iota(jnp.int32, (tm, tm), 0)
        col = lax.broadcasted_iota(jnp.int32, (tm, tm), 1)
        tri = (col <= row).astype(BF16)
        c = _exact_dot01(tri, logf) + carry[0:1, :]
        carry[...] = jnp.broadcast_to(c[tm - 1:tm, :], carry.shape)
        cb = c * LOG2E
        c1 = cb.astype(BF16).astype(F32)
        r1 = cb - c1
        c2 = r1.astype(BF16).astype(F32)
        c3 = (r1 - c2).astype(BF16).astype(F32)
        zc = (c1 + pltpu.roll(c2, 8, axis=1) + pltpu.roll(c3, 16, axis=1)).astype(BF16)

        def pad_heads(v):
            blocks = []
            for pair in range(H // 2):
                two = v[:, 128 * pair:128 * (pair + 1)]
                blocks.append(jnp.where(lane < DH, two, 0.0))
                blocks.append(jnp.where(lane < DH, pltpu.roll(two, DH, axis=1), 0.0))
            return jnp.concatenate(blocks, axis=1)

        q = jnp.dot(h, w_ref[:, OFF_Q:OFF_K], preferred_element_type=F32) * LOG2E
        qp_ref[...] = (pad_heads(q) + jnp.dot(zc, pq_ref[...], preferred_element_type=F32) + oq_ref[...]).astype(BF16)
        k = jnp.dot(h, w_ref[:, OFF_K:OFF_V], preferred_element_type=F32)
        kp_ref[...] = (pad_heads(k) + jnp.dot(zc, pk_ref[...], preferred_element_type=F32) + ok_ref[...]).astype(BF16)
        v = pad_heads(jnp.dot(h, w_ref[:, OFF_V:OFF_BCU], preferred_element_type=F32))
        ones_lane = lax.broadcasted_iota(jnp.int32, (tm, H * HP), 1) % HP == DH
        v_ref[...] = jnp.where(ones_lane, 1.0, v).astype(BF16)
        bcu_ref[...] = jnp.dot(h, w_ref[:, OFF_BCU:OFF_F], preferred_element_type=F32).astype(BF16)

    return pl.pallas_call(
        body, name="in_proj", grid=(s // tm,),
        in_specs=[_rows(tm, D), _full((1, D)), _resident((D, WP)), _full((1, 128)),
                  _full((128, 1024)), _full((128, 1024)), _full((1, 1024)), _full((1, 1024))],
        out_specs=[pl.BlockSpec((D, tm), lambda i: (0, i)), _rows(tm, 1024), _rows(tm, 1024), _rows(tm, 1024),
                   _rows(tm, 3 * CW), _rows(tm, 128)],
        out_shape=[jax.ShapeDtypeStruct((D, s), BF16), jax.ShapeDtypeStruct((s, 1024), BF16),
                   jax.ShapeDtypeStruct((s, 1024), BF16), jax.ShapeDtypeStruct((s, 1024), BF16),
                   jax.ShapeDtypeStruct((s, 3 * CW), BF16), jax.ShapeDtypeStruct((s, 128), F32)],
        scratch_shapes=[pltpu.VMEM((SUBLANES, 128), F32)],
        compiler_params=_cparams(56, ("arbitrary",)),
    )(x, g1, wp, bfp, pq, pk, oq, ok)


def _attn_fwd(qp, kp, v, *, t):
    s = qp.shape[0]
    nq = s // t

    def body(q_ref, k_ref, v_ref, o_ref, lse_ref, mk_ref):
        pi = pl.program_id(1)
        row = lax.broadcasted_iota(jnp.int32, (t, t), 0)
        col = lax.broadcasted_iota(jnp.int32, (t, t), 1)
        lane = lax.broadcasted_iota(jnp.int32, (t, 128), 1)

        def head_step(hh, rows, ki, carry, masked):
            m, acc = carry
            off = pl.multiple_of(ki * t, t)
            q = q_ref[rows, HP * hh:HP * (hh + 1)]
            k = k_ref[pl.ds(off, t), HP * hh:HP * (hh + 1)]
            sc = lax.dot_general(q, k, NT, preferred_element_type=F32)
            if masked:
                sc = jnp.where(col <= row, sc, -1e30)
            mn = jnp.maximum(m, jnp.max(sc, axis=-1, keepdims=True))
            p = jnp.exp2(sc - mn).astype(BF16)
            acc = jnp.exp2(m - mn) * acc + jnp.dot(p, v_ref[pl.ds(off, t), HP * hh:HP * (hh + 1)],
                                                  preferred_element_type=F32)
            return mn, acc

        def step(rows, ki, carry, masked):
            new = tuple(head_step(hh, rows, ki, carry[hh], masked) for hh in range(2))
            mk_ref[ki, rows] = jnp.where(lane < DH, jnp.broadcast_to(new[0][0], (t, 128)),
                                         jnp.broadcast_to(new[1][0], (t, 128)))
            return new

        init = (jnp.full((t, 1), -1e30, F32), jnp.zeros((t, 128), F32))
        top, bottom = slice(0, t), slice(t, 2 * t)

        def quad(j, carry):
            c0, c1 = carry
            c0 = step(top, 2 * j, c0, False)
            c1 = step(bottom, 2 * j, c1, False)
            c0 = step(top, 2 * j + 1, c0, False)
            c1 = step(bottom, 2 * j + 1, c1, False)
            return c0, c1

        c0, c1 = lax.fori_loop(0, pi, quad, ((init, init), (init, init)))
        f0 = step(top, 2 * pi, c0, True)
        c1 = step(bottom, 2 * pi, c1, False)
        f1 = step(bottom, 2 * pi + 1, c1, True)
        for rows, ((m0, acc0), (m1, acc1)) in ((top, f0), (bottom, f1)):
            l0, l1 = acc0[:, DH:DH + 1], acc1[:, DH:DH + 1]
            o_ref[rows, :] = jnp.where(lane < DH, acc0 / l0, pltpu.roll(acc1 / l1, DH, axis=1))
            lse_ref[rows, :] = jnp.where(lane < DH, jnp.broadcast_to(m0 + jnp.log2(l0), (t, 128)),
                                         jnp.broadcast_to(m1 + jnp.log2(l1), (t, 128)))

    return pl.pallas_call(
        body, name="attn_fwd", grid=(H // 2, nq // 2),
        in_specs=[pl.BlockSpec((2 * t, 2 * HP), lambda p, i: (i, p)),
                  pl.BlockSpec((s, 2 * HP), lambda p, i: (0, p)),
                  pl.BlockSpec((s, 2 * HP), lambda p, i: (0, p))],
        out_specs=[pl.BlockSpec((2 * t, 128), lambda p, i: (i, p)), pl.BlockSpec((2 * t, 128), lambda p, i: (i, p)),
                   pl.BlockSpec((nq, 2 * t, 128), lambda p, i: (0, i, p))],
        out_shape=[jax.ShapeDtypeStruct((s, AW), F32), jax.ShapeDtypeStruct((s, AW), F32),
                   jax.ShapeDtypeStruct((nq, s, AW), F32)],
        compiler_params=_cparams(48, ("arbitrary", "arbitrary")),
    )(qp, kp, v)


HALO = 16


def _conv_taps(bcu_ref, halo_ref, first, tm):
    z = bcu_ref[:, CW:2 * CW].astype(F32) * bcu_ref[:, 2 * CW:3 * CW].astype(F32)
    zh = jnp.where(first, 0.0, halo_ref[:, CW:2 * CW].astype(F32) * halo_ref[:, 2 * CW:3 * CW].astype(F32))
    row = lax.broadcasted_iota(jnp.int32, (tm, CW), 0)
    last, before = zh[HALO - 1:HALO, :], zh[HALO - 2:HALO - 1, :]
    z1 = jnp.where(row == 0, last, pltpu.roll(z, 1, axis=0))
    z2 = jnp.where(row == 0, before, jnp.where(row == 1, last, pltpu.roll(z, 2, axis=0)))
    return z, z1, z2


def _halo_before(tm, width):
    return pl.BlockSpec((HALO, width), lambda i: (jnp.maximum(i * (tm // HALO) - 1, 0), 0))


def _mix_out(o, bcu, cw8, ga, gc, gsum, w_out, x, g_post, g_ffn_pre, *, tm):
    s = x.shape[0]

    def body(o_ref, bcu_ref, halo_ref, cw_ref, ga_ref, gc_ref, gs_ref, w_ref, x_ref, g_ref, gf_ref,
             merged_ref, y_ref, x2_ref, cv_ref, h2_ref):
        z, z1, z2 = _conv_taps(bcu_ref, halo_ref, pl.program_id(0) == 0, tm)
        cv = cw_ref[0:1, :] * z2 + cw_ref[1:2, :] * z1 + cw_ref[2:3, :] * z
        cv_ref[...] = cv
        conv = bcu_ref[:, 0:CW].astype(F32) * cv
        ov = o_ref[...]
        ra = lax.rsqrt(_group_sum(ov * ov, gs_ref[...]) * (1.0 / DH) + EPS)
        rc = lax.rsqrt(_group_sum(conv * conv, gs_ref[...]) * (1.0 / DH) + EPS)
        merged = jnp.concatenate([ov * ra * ga_ref[...], conv * rc * gc_ref[...]], axis=1).astype(BF16)
        merged_ref[...] = merged
        y = jnp.dot(merged, w_ref[...], preferred_element_type=F32)
        y_ref[...] = y
        x2 = x_ref[...] + _rms_fwd(y, g_ref[...])[0]
        x2_ref[...] = x2
        h2_ref[...] = _rms_fwd(x2, gf_ref[...])[0].astype(BF16)

    return pl.pallas_call(
        body, name="mix_out", grid=(s // tm,),
        in_specs=[_rows(tm, AW), _rows(tm, 3 * CW), _halo_before(tm, 3 * CW), _full((SUBLANES, CW)),
                  _full((1, AW)), _full((1, CW)), _full((GS, GS)), _resident((D, D)), _rows(tm, D), _full((1, D)),
                  _full((1, D))],
        out_specs=[_rows(tm, D), _rows(tm, D), _rows(tm, D), _rows(tm, CW), _rows(tm, D)],
        out_shape=[jax.ShapeDtypeStruct((s, D), BF16), jax.ShapeDtypeStruct((s, D), F32),
                   jax.ShapeDtypeStruct((s, D), F32), jax.ShapeDtypeStruct((s, CW), F32),
                   jax.ShapeDtypeStruct((s, D), BF16)],
        compiler_params=_cparams(48, ("arbitrary",)),
    )(o, bcu, bcu, cw8, ga, gc, gsum, w_out, x, g_post, g_ffn_pre)


def _ffn_fwd_loss(h2, wgu, wd, x2, target, g_post, *, tm):
    s = x2.shape[0]

    def body(h_ref, w_ref, wd_ref, x2_ref, t_ref, g_ref,
             gate_ref, up_ref, a_ref, dx3_ref, dff_ref, loss_ref, dg_ref):
        @pl.when(pl.program_id(0) == 0)
        def _():
            loss_ref[...] = jnp.zeros_like(loss_ref)
            dg_ref[...] = jnp.zeros_like(dg_ref)

        h = h_ref[...]
        ff = None
        for c0, n in FF_CHUNKS:
            cols = slice(c0, c0 + n)
            gate = lax.dot_general(h, w_ref[0, cols, :], NT, preferred_element_type=F32)
            up = lax.dot_general(h, w_ref[1, cols, :], NT, preferred_element_type=F32)
            gate_ref[:, cols] = gate.astype(BF16)
            up_ref[:, cols] = up.astype(BF16)
            act = (gate * jax.nn.sigmoid(gate) * up).astype(BF16)
            a_ref[:, cols] = act
            part = jnp.dot(act, wd_ref[cols, :], preferred_element_type=F32)
            ff = part if ff is None else ff + part
        out, n, r = _rms_fwd(ff, g_ref[...])
        e = x2_ref[...] + out - t_ref[...]
        loss_ref[...] += _fold8(e * e)
        dx3 = e * (1.0 / D)
        dx3_ref[...] = dx3
        dff, dg = _rms_bwd(dx3, n, r, g_ref[...])
        dff_ref[...] = dff.astype(BF16)
        dg_ref[...] += _fold8(dg)

    wide = _rows(tm, DFF)
    return pl.pallas_call(
        body, name="ffn_fwd_loss", grid=(s // tm,),
        in_specs=[_rows(tm, D), _resident((2, DFF, D)), _resident((DFF, D)), _rows(tm, D), _rows(tm, D), _full((1, D))],
        out_specs=[wide, wide, wide, _rows(tm, D), _rows(tm, D), _full((SUBLANES, D)), _full((SUBLANES, D))],
        out_shape=[jax.ShapeDtypeStruct((s, DFF), BF16)] * 3
        + [jax.ShapeDtypeStruct((s, D), F32), jax.ShapeDtypeStruct((s, D), BF16),
           jax.ShapeDtypeStruct((SUBLANES, D), F32), jax.ShapeDtypeStruct((SUBLANES, D), F32)],
        compiler_params=_cparams(56, ("arbitrary",)),
    )(h2, wgu, wd, x2, target, g_post)


def _ffn_bwd(dff, wd, gate, up, wgu, x2, g_pre, dx3, y, g_post, *, tm):
    s = x2.shape[0]

    def body(dff_ref, wd_ref, gate_ref, up_ref, w_ref, x2_ref, gpre_ref, dx3_ref, y_ref, gpost_ref,
             dgu_ref, dx2_ref, dy_ref, dgpre_ref, dgpost_ref):
        @pl.when(pl.program_id(0) == 0)
        def _():
            dgpre_ref[...] = jnp.zeros_like(dgpre_ref)
            dgpost_ref[...] = jnp.zeros_like(dgpost_ref)

        dff = dff_ref[...]
        dh2 = None
        for c0, n in FF_CHUNKS_BWD:
            cols = slice(c0, c0 + n)
            da = lax.dot_general(dff, wd_ref[cols, :], NT, preferred_element_type=F32)
            g = gate_ref[:, cols].astype(F32)
            sg = jax.nn.sigmoid(g)
            dgate = (da * up_ref[:, cols].astype(F32) * (sg * (1.0 + g * (1.0 - sg)))).astype(BF16)
            dup = (da * (g * sg)).astype(BF16)
            dgu_ref[:, cols] = dgate
            dgu_ref[:, DFF + c0:DFF + c0 + n] = dup
            part = (jnp.dot(dgate, w_ref[0, cols, :], preferred_element_type=F32)
                    + jnp.dot(dup, w_ref[1, cols, :], preferred_element_type=F32))
            dh2 = part if dh2 is None else dh2 + part
        _, n2, r2 = _rms_fwd(x2_ref[...], gpre_ref[...])
        dxn, dg = _rms_bwd(dh2, n2, r2, gpre_ref[...])
        dgpre_ref[...] += _fold8(dg)
        dx2 = dx3_ref[...] + dxn
        dx2_ref[...] = dx2
        _, ny, ry = _rms_fwd(y_ref[...], gpost_ref[...])
        dy, dg2 = _rms_bwd(dx2, ny, ry, gpost_ref[...])
        dy_ref[...] = dy.astype(BF16)
        dgpost_ref[...] += _fold8(dg2)

    wide = _rows(tm, DFF)
    return pl.pallas_call(
        body, name="ffn_bwd", grid=(s // tm,),
        in_specs=[_rows(tm, D), _resident((DFF, D)), wide, wide, _resident((2, DFF, D)), _rows(tm, D), _full((1, D)),
                  _rows(tm, D), _rows(tm, D), _full((1, D))],
        out_specs=[_rows(tm, 2 * DFF), _rows(tm, D), _rows(tm, D),
                   _full((SUBLANES, D)), _full((SUBLANES, D))],
        out_shape=[jax.ShapeDtypeStruct((s, 2 * DFF), BF16), jax.ShapeDtypeStruct((s, D), F32),
                   jax.ShapeDtypeStruct((s, D), BF16), jax.ShapeDtypeStruct((SUBLANES, D), F32),
                   jax.ShapeDtypeStruct((SUBLANES, D), F32)],
        compiler_params=_cparams(56, ("arbitrary",)),
    )(dff, wd, gate, up, wgu, x2, g_pre, dx3, y, g_post)


def _grad_matmul(a, b, *, ta, tb, ts, name, vmem_mb=48):
    s, ka = a.shape
    nb = b.shape[1]
    ts = min(ts, s)
    nk = s // ts

    def body(a_ref, b_ref, o_ref, *acc):
        if nk == 1:
            o_ref[...] = lax.dot_general(a_ref[...], b_ref[...], TN, preferred_element_type=F32).astype(BF16)
            return
        k = pl.program_id(2)

        @pl.when(k == 0)
        def _():
            acc[0][...] = jnp.zeros_like(acc[0])

        acc[0][...] += lax.dot_general(a_ref[...], b_ref[...], TN, preferred_element_type=F32)

        @pl.when(k == nk - 1)
        def _():
            o_ref[...] = acc[0][...].astype(BF16)

    whole_b = {"pipeline_mode": pl.Buffered(1)} if nk == 1 and nb == tb else {}
    return pl.pallas_call(
        body, name=name, grid=(ka // ta, nb // tb, nk),
        in_specs=[pl.BlockSpec((ts, ta), lambda i, j, k: (k, i)),
                  pl.BlockSpec((ts, tb), lambda i, j, k: (k, j), **whole_b)],
        out_specs=pl.BlockSpec((ta, tb), lambda i, j, k: (i, j)),
        out_shape=jax.ShapeDtypeStruct((ka, nb), BF16),
        scratch_shapes=[pltpu.VMEM((ta, tb), F32)] if nk > 1 else [],
        compiler_params=_cparams(vmem_mb, ("arbitrary", "arbitrary", "arbitrary")),
    )(a, b)


GW_TILE = 256


def _grad_w_in(h1t, pieces):
    ka, s = h1t.shape
    widths = [p.shape[1] for p in pieces]
    assert all(w % GW_TILE == 0 for w in widths)
    first = [sum(widths[:i]) // GW_TILE for i in range(len(pieces))]
    count = [w // GW_TILE for w in widths]

    def body(a_ref, *refs):
        o_ref = refs[-1]
        j = pl.program_id(0)
        for ref, f0, n in zip(refs[:-1], first, count):
            @pl.when((j >= f0) & (j < f0 + n))
            def _(ref=ref):
                o_ref[...] = jnp.dot(a_ref[...], ref[...], preferred_element_type=F32).astype(BF16)

    def spec(f0, n):
        return pl.BlockSpec((s, GW_TILE), lambda j: (0, jnp.clip(j - f0, 0, n - 1)))

    return pl.pallas_call(
        body, name="grad_w_in", grid=(sum(count),),
        in_specs=[_resident((ka, s))] + [spec(f0, n) for f0, n in zip(first, count)],
        out_specs=pl.BlockSpec((ka, GW_TILE), lambda j: (0, j)),
        out_shape=jax.ShapeDtypeStruct((ka, sum(widths)), BF16),
        compiler_params=_cparams(56, ("arbitrary",)),
    )(h1t, *pieces)


def _mix_bwd(dy, w_out, o, cv, bcu, ga, gc, gsum, after, *, tm):
    s = dy.shape[0]

    def group_norm_bwd(dn_out, v, g, gs):
        r = lax.rsqrt(_group_sum(v * v, gs) * (1.0 / DH) + EPS)
        n = v * r
        dn = dn_out * g
        return r * (dn - n * (_group_sum(dn * n, gs) * (1.0 / DH))), dn_out * n

    def body(dy_ref, w_ref, o_ref, cv_ref, bcu_ref, ga_ref, gc_ref, gs_ref, after_ref,
             do_ref, dl_ref, dcv_ref, db_ref, dga_ref, dgc_ref):
        @pl.when(pl.program_id(0) == 0)
        def _():
            dga_ref[...] = jnp.zeros_like(dga_ref)
            dgc_ref[...] = jnp.zeros_like(dgc_ref)

        dm = lax.dot_general(dy_ref[...], w_ref[...], NT, preferred_element_type=F32)
        ov = o_ref[...]
        do, dga = group_norm_bwd(dm[:, 0:AW], ov, ga_ref[...], gs_ref[...])
        dob = do.astype(BF16)
        do_ref[...] = dob
        dl_ref[...] = _group_sum(dob.astype(F32) * ov, gs_ref[...])
        dga_ref[...] += _fold8(dga)
        gate_b = bcu_ref[:, 0:CW].astype(F32)
        cv = cv_ref[...]
        dconv, dgc = group_norm_bwd(dm[:, AW:D], gate_b * cv, gc_ref[...], gs_ref[...])
        dgc_ref[...] += _fold8(dgc)
        dcv_ref[...] = dconv * gate_b
        db_ref[...] = (dconv * cv).astype(BF16)

    return pl.pallas_call(
        body, name="mix_bwd", grid=(s // tm,),
        in_specs=[_rows(tm, D), _resident((D, D)), _rows(tm, AW), _rows(tm, CW), _rows(tm, 3 * CW),
                  _full((1, AW)), _full((1, CW)), _full((GS, GS)), ANY],
        out_specs=[_rows(tm, AW), _rows(tm, AW), _rows(tm, CW), _rows(tm, CW),
                   _full((SUBLANES, AW)), _full((SUBLANES, CW))],
        out_shape=[jax.ShapeDtypeStruct((s, AW), BF16), jax.ShapeDtypeStruct((s, AW), F32),
                   jax.ShapeDtypeStruct((s, CW), F32), jax.ShapeDtypeStruct((s, CW), BF16),
                   jax.ShapeDtypeStruct((SUBLANES, AW), F32), jax.ShapeDtypeStruct((SUBLANES, CW), F32)],
        compiler_params=_cparams(48, ("arbitrary",)),
    )(dy, w_out, o, cv, bcu, ga, gc, gsum, after)


def _conv_bwd(dcv, db, bcu, cw8, *, tm):
    s = dcv.shape[0]
    nt = s // tm

    def body(dcv_ref, nxt_ref, db_ref, bcu_ref, halo_ref, cw_ref, dbcu_ref, dw_ref):
        i = pl.program_id(0)

        @pl.when(i == 0)
        def _():
            dw_ref[...] = jnp.zeros_like(dw_ref)

        z, z1, z2 = _conv_taps(bcu_ref, halo_ref, i == 0, tm)
        d = dcv_ref[...]
        dw_ref[0] += _fold8(d * z2)
        dw_ref[1] += _fold8(d * z1)
        dw_ref[2] += _fold8(d * z)
        nx = jnp.where(i == nt - 1, 0.0, nxt_ref[...])
        row = lax.broadcasted_iota(jnp.int32, (tm, CW), 0)
        d1 = jnp.where(row == tm - 1, nx[0:1, :], pltpu.roll(d, tm - 1, axis=0))
        d2 = jnp.where(row == tm - 2, nx[0:1, :], jnp.where(row == tm - 1, nx[1:2, :], pltpu.roll(d, tm - 2, axis=0)))
        dz = cw_ref[2:3, :] * d + cw_ref[1:2, :] * d1 + cw_ref[0:1, :] * d2
        dbcu_ref[:, 0:CW] = db_ref[...]
        dbcu_ref[:, CW:2 * CW] = (dz * bcu_ref[:, 2 * CW:3 * CW].astype(F32)).astype(BF16)
        dbcu_ref[:, 2 * CW:3 * CW] = (dz * bcu_ref[:, CW:2 * CW].astype(F32)).astype(BF16)

    return pl.pallas_call(
        body, name="conv_bwd", grid=(nt,),
        in_specs=[_rows(tm, CW),
                  pl.BlockSpec((SUBLANES, CW), lambda i: (jnp.minimum((i + 1) * (tm // SUBLANES), s // SUBLANES - 1), 0)),
                  _rows(tm, CW), _rows(tm, 3 * CW), _halo_before(tm, 3 * CW), _full((SUBLANES, CW))],
        out_specs=[_rows(tm, 3 * CW), _full((3, SUBLANES, CW))],
        out_shape=[jax.ShapeDtypeStruct((s, 3 * CW), BF16), jax.ShapeDtypeStruct((3, SUBLANES, CW), F32)],
        compiler_params=_cparams(48, ("arbitrary",)),
    )(dcv, dcv, db, bcu, bcu, cw8)


def _attn_bwd(qp, kp, v, do, lse, dl, mk, *, t):
    s = qp.shape[0]
    nq = s // t

    def body(q_ref, k_ref, v_ref, do_ref, lse_ref, dl_ref, mk_ref, dq_ref, dk_ref, dv_ref, dkx_ref, dq_acc):
        pi = pl.program_id(1)

        @pl.when(pi == 0)
        def _():
            dq_acc[...] = jnp.zeros_like(dq_acc)

        row = lax.broadcasted_iota(jnp.int32, (t, t), 0)
        col = lax.broadcasted_iota(jnp.int32, (t, t), 1)
        lane = lax.broadcasted_iota(jnp.int32, (t, 128), 1)

        def head_step(hh, qi, carry, modes):
            off = pl.multiple_of(qi * t, t)
            rows = pl.ds(off, t)
            q = q_ref[rows, HP * hh:HP * (hh + 1)]
            qt = q.T
            lse_col = lse_ref[rows, DH * hh:DH * hh + 1]
            dl_col = dl_ref[rows, DH * hh:DH * hh + 1]
            do2 = do_ref[rows, :]
            dom = jnp.where(lane < DH, do2 if hh == 0 else pltpu.roll(do2, DH, axis=1), jnp.zeros((), BF16))
            new, dss = [], []
            for half, masked in enumerate(modes):
                if masked is None:
                    new.append(carry[half])
                    continue
                dk, dv, cs = carry[half]
                keys = slice(half * t, (half + 1) * t)
                m_col = mk_ref[half, rows, DH * hh:DH * hh + 1]
                scale = jnp.exp2(m_col - lse_col)
                sc = lax.dot_general(q, k_ref[keys, HP * hh:HP * (hh + 1)], NT, preferred_element_type=F32) - m_col
                if masked:
                    sc = jnp.where(col <= row, sc, -1e30)
                pt = jnp.exp2(sc).astype(BF16)
                dp = lax.dot_general(dom, v_ref[keys, HP * hh:HP * (hh + 1)], NT, preferred_element_type=F32)
                ds32 = (pt.astype(F32) * scale) * (dp - dl_col)
                ds = ds32.astype(BF16)
                cs = cs + _fold8(ds32)
                dv = dv + jnp.dot((dom.astype(F32) * scale).astype(BF16).T, pt, preferred_element_type=F32)
                dk = dk + jnp.dot(qt, ds, preferred_element_type=F32)
                new.append((dk, dv, cs))
                dss.append((half, ds))
            if len(dss) == 2:
                dq = jnp.dot(jnp.concatenate([dss[0][1], dss[1][1]], axis=1), k_ref[:, HP * hh:HP * (hh + 1)],
                             preferred_element_type=F32)
            else:
                half, ds = dss[0]
                dq = jnp.dot(ds, k_ref[half * t:(half + 1) * t, HP * hh:HP * (hh + 1)], preferred_element_type=F32)
            dq_acc[rows, HP * hh:HP * (hh + 1)] += dq
            return tuple(new)

        def step(qi, carry, modes):
            return tuple(head_step(hh, qi, carry[hh], modes) for hh in range(2))

        def two_heads(a0, a1):
            return jnp.where(lane < DH, a0, pltpu.roll(a1, DH, axis=1))

        def rows_to_lanes(a0, a1):
            return jnp.concatenate([a0, a1], axis=0).T

        zero = (jnp.zeros((HP, t), F32), jnp.zeros((128, t), F32), jnp.zeros((SUBLANES, t), F32))
        carry = step(2 * pi, ((zero, zero), (zero, zero)), (True, None))
        carry = step(2 * pi + 1, carry, (False, True))

        def pair(j, carry):
            qi = 2 * (pi + 1 + j)
            return step(qi + 1, step(qi, carry, (False, False)), (False, False))

        carry = lax.fori_loop(0, nq // 2 - 1 - pi, pair, carry)
        for half in range(2):
            keys = slice(half * t, (half + 1) * t)
            (dk0, dv0, cs0), (dk1, dv1, cs1) = carry[0][half], carry[1][half]
            dk_ref[keys, :] = (rows_to_lanes(dk0[0:DH], dk1[0:DH]) * LN2).astype(BF16)
            dv_ref[keys, :] = rows_to_lanes(dv0[0:DH], dv1[0:DH]).astype(BF16)
            total = lambda cs: jnp.broadcast_to(jnp.sum(cs, axis=0, keepdims=True), (DH, t))
            dkx_ref[keys, :] = rows_to_lanes(total(cs0), total(cs1))

        @pl.when(pi == nq // 2 - 1)
        def _():
            for c in range(s // t):
                rows = slice(c * t, (c + 1) * t)
                dq_ref[rows, :] = two_heads(dq_acc[rows, 0:HP], dq_acc[rows, HP:2 * HP]).astype(BF16)

    return pl.pallas_call(
        body, name="attn_bwd", grid=(H // 2, nq // 2),
        in_specs=[pl.BlockSpec((s, 2 * HP), lambda p, i: (0, p)),
                  pl.BlockSpec((2 * t, 2 * HP), lambda p, i: (i, p)),
                  pl.BlockSpec((2 * t, 2 * HP), lambda p, i: (i, p)),
                  pl.BlockSpec((s, 128), lambda p, i: (0, p)),
                  pl.BlockSpec((s, 128), lambda p, i: (0, p)),
                  pl.BlockSpec((s, 128), lambda p, i: (0, p)),
                  pl.BlockSpec((2, s, 128), lambda p, i: (i, 0, p))],
        out_specs=[pl.BlockSpec((s, 128), lambda p, i: (0, p)),
                   pl.BlockSpec((2 * t, 128), lambda p, i: (i, p)),
                   pl.BlockSpec((2 * t, 128), lambda p, i: (i, p)),
                   pl.BlockSpec((2 * t, 128), lambda p, i: (i, p))],
        out_shape=[jax.ShapeDtypeStruct((s, AW), BF16), jax.ShapeDtypeStruct((s, AW), BF16),
                   jax.ShapeDtypeStruct((s, AW), BF16), jax.ShapeDtypeStruct((s, AW), F32)],
        scratch_shapes=[pltpu.VMEM((s, 2 * HP), F32)],
        compiler_params=_cparams(56, ("arbitrary", "arbitrary")),
    )(qp, kp, v, do, lse, dl, mk)


def _forget_bwd(dkx, z, sel, *, tm):
    s = dkx.shape[0]
    nt = s // tm

    def body(dk_ref, z_ref, sel_ref, dfl_ref, dbf_ref, carry):
        @pl.when(pl.program_id(0) == 0)
        def _():
            carry[...] = jnp.zeros_like(carry)
            dbf_ref[...] = jnp.zeros_like(dbf_ref)

        dc = _split_dot(dk_ref[...], sel_ref[...])
        row = lax.broadcasted_iota(jnp.int32, (tm, tm), 0)
        col = lax.broadcasted_iota(jnp.int32, (tm, tm), 1)
        tri = (col >= row).astype(BF16)
        dlogf = _exact_dot01(tri, dc) + carry[0:1, :]
        carry[...] = jnp.broadcast_to(dlogf[0:1, :], carry.shape)
        dz = dlogf * (1.0 - jax.nn.sigmoid(z_ref[...]))
        dfl_ref[:, 0:128] = dz.astype(BF16)
        dfl_ref[:, 128:GW_TILE] = jnp.zeros((tm, GW_TILE - 128), BF16)
        dbf_ref[...] += _fold8(dz)

    rev = lambda i: (nt - 1 - i, 0)
    return pl.pallas_call(
        body, name="forget_bwd", grid=(nt,),
        in_specs=[pl.BlockSpec((tm, AW), rev), pl.BlockSpec((tm, 128), rev), _full((AW, 128))],
        out_specs=[pl.BlockSpec((tm, GW_TILE), rev), _full((SUBLANES, 128))],
        out_shape=[jax.ShapeDtypeStruct((s, GW_TILE), BF16), jax.ShapeDtypeStruct((SUBLANES, 128), F32)],
        scratch_shapes=[pltpu.VMEM((SUBLANES, 128), F32)],
        compiler_params=_cparams(48, ("arbitrary",)),
    )(dkx, z, sel)


def _in_proj_bwd(pieces, wp, x, g1, dx2, after, *, tm):
    s = x.shape[0]

    def body(q_ref, k_ref, v_ref, bcu_ref, f_ref, w_ref, x_ref, g_ref, dx2_ref, after_ref, dx_ref, dg_ref):
        @pl.when(pl.program_id(0) == 0)
        def _():
            dg_ref[...] = jnp.zeros_like(dg_ref)

        dh = None
        for ref, (lo, hi) in zip((q_ref, k_ref, v_ref, bcu_ref, f_ref), PIECES):
            part = lax.dot_general(ref[...], w_ref[:, lo:hi], NT, preferred_element_type=F32)
            dh = part if dh is None else dh + part
        _, n, r = _rms_fwd(x_ref[...], g_ref[...])
        dxn, dg = _rms_bwd(dh, n, r, g_ref[...])
        dx_ref[...] = dx2_ref[...] + dxn
        dg_ref[...] += _fold8(dg)

    return pl.pallas_call(
        body, name="in_proj_bwd", grid=(s // tm,),
        in_specs=[_rows(tm, hi - lo) for lo, hi in PIECES]
        + [_resident((D, WP)), _rows(tm, D), _full((1, D)), _rows(tm, D), ANY],
        out_specs=[_rows(tm, D), _full((SUBLANES, D))],
        out_shape=[jax.ShapeDtypeStruct((s, D), F32), jax.ShapeDtypeStruct((SUBLANES, D), F32)],
        compiler_params=_cparams(56, ("arbitrary",)),
    )(*pieces, wp, x, g1, dx2, after)


def _position():
    return lax.axis_index("x"), lax.axis_index("y"), lax.axis_index("c")


ANY = pl.BlockSpec(memory_space=pl.ANY)


def _all_gather(shards):
    n = len(shards)

    def body(*refs):
        x_refs, out_refs = refs[:n], refs[n:2 * n]
        send_sems, recv_sems, local_sems = refs[2 * n:]
        x, y, c = _position()
        me, sibling = (x, y, c), (x, y, 1 - c)
        chips = [(1 - x, y), (x, 1 - y), (1 - x, 1 - y)]

        def copy(a, k, block, to, own=False):
            slot = out_refs[a].at[4 * block[0] + 2 * block[1] + block[2]]
            return pltpu.make_async_remote_copy(
                src_ref=x_refs[a] if own else slot, dst_ref=slot,
                send_sem=send_sems.at[7 * a + k], recv_sem=recv_sems.at[7 * a + k], device_id=to, device_id_type=MESH_ID)

        mine = [pltpu.make_async_copy(x_refs[a], out_refs[a].at[4 * x + 2 * y + c], local_sems.at[a]) for a in range(n)]
        for cp in mine:
            cp.start()
        first = []
        for a in range(n):
            first.append(copy(a, 0, me, sibling, own=True))
            first += [copy(a, 1 + j, me, (*chip, c), own=True) for j, chip in enumerate(chips)]
        for cp in first:
            cp.start()
        passed = []
        for j, chip in enumerate(chips):
            for a in range(n):
                copy(a, 1 + j, (*chip, c), me).wait_recv()
                fwd = copy(a, 4 + j, (*chip, c), sibling)
                fwd.start()
                passed.append(fwd)
        for a in range(n):
            copy(a, 0, sibling, me).wait_recv()
            for j, chip in enumerate(chips):
                copy(a, 4 + j, (*chip, 1 - c), me).wait_recv()
        for cp in first + passed:
            cp.wait_send()
        for cp in mine:
            cp.wait()

    return pl.pallas_call(
        body, name="all_gather_weights",
        out_shape=[jax.ShapeDtypeStruct((NDEV,) + sh.shape, sh.dtype) for sh in shards],
        in_specs=[ANY] * n, out_specs=[ANY] * n,
        scratch_shapes=[pltpu.SemaphoreType.DMA((7 * n,)), pltpu.SemaphoreType.DMA((7 * n,)), pltpu.SemaphoreType.DMA((n,))],
    )(*shards)


def _pair_exchange(grads):
    n = len(grads)

    def body(*refs):
        g_refs, out_refs = refs[:n], refs[n:2 * n]
        send_sems, recv_sems = refs[2 * n:]
        x, y, c = _position()
        copies = [pltpu.make_async_remote_copy(
            src_ref=g_refs[a].at[:, pl.ds(1 - c, 1)], dst_ref=out_refs[a], send_sem=send_sems.at[a],
            recv_sem=recv_sems.at[a], device_id=(x, y, 1 - c), device_id_type=MESH_ID) for a in range(n)]
        for cp in copies:
            cp.start()
        for cp in copies:
            cp.wait()

    return pl.pallas_call(
        body, name="grad_pair_exchange",
        out_shape=[jax.ShapeDtypeStruct((4, 1) + g.shape[2:], g.dtype) for g in grads],
        in_specs=[ANY] * n, out_specs=[ANY] * n,
        scratch_shapes=[pltpu.SemaphoreType.DMA((n,)), pltpu.SemaphoreType.DMA((n,))],
    )(*grads)


def _pair_sum(g, got, idx, *, tr, name):
    r, c = g.shape[2:]

    def body(idx_ref, g_ref, got_ref, pb_ref, own_ref):
        p = g_ref[0, 0].astype(F32) + got_ref[0, 0].astype(F32)
        pb_ref[0] = p.astype(BF16)

        @pl.when(pl.program_id(1) == idx_ref[1])
        def _():
            own_ref[...] = p

    return pl.pallas_call(
        body, name=name,
        grid_spec=pltpu.PrefetchScalarGridSpec(
            num_scalar_prefetch=1, grid=(r // tr, 4),
            in_specs=[pl.BlockSpec((1, 1, tr, c), lambda i, j, idx: (j, idx[0], i, 0)),
                      pl.BlockSpec((1, 1, tr, c), lambda i, j, idx: (j, 0, i, 0))],
            out_specs=[pl.BlockSpec((1, tr, c), lambda i, j, idx: (j, i, 0)),
                       pl.BlockSpec((tr, c), lambda i, j, idx: (i, 0))]),
        out_shape=[jax.ShapeDtypeStruct((4, r, c), BF16), jax.ShapeDtypeStruct((r, c), F32)],
        compiler_params=_cparams(62, ("arbitrary", "arbitrary")),
    )(idx, g, got)


HBM = pl.BlockSpec(memory_space=pltpu.HBM)
SEM = pl.BlockSpec(memory_space=pltpu.SEMAPHORE)
DATAFLOW = pltpu.SideEffectType.DATAFLOW_SIDE_EFFECTING


PEERS = {"gather": NDEV - 1, "scatter": NDEV - 1, "chips": 3}


def _exchange_copies(src_refs, land_refs, send_sems, recv_sems, mode):
    x, y, c = _position()
    me, my_chip = 4 * x + 2 * y + c, 2 * x + y
    npeers = PEERS[mode]
    copies, own = [], []
    for a, (s_ref, l_ref) in enumerate(zip(src_refs, land_refs)):
        for k in range(npeers):
            if mode == "chips":
                px, py, pc = x ^ ((k + 1) >> 1), y ^ ((k + 1) & 1), c
                src, dst = s_ref.at[2 * px + py], l_ref.at[my_chip]
            else:
                px, py, pc = x ^ ((k + 1) >> 2), y ^ (((k + 1) >> 1) & 1), c ^ ((k + 1) & 1)
                src, dst = (s_ref.at[4 * px + 2 * py + pc] if mode == "scatter" else s_ref), l_ref.at[me]
            copies.append(pltpu.make_async_remote_copy(
                src_ref=src, dst_ref=dst, send_sem=send_sems.at[npeers * a + k], recv_sem=recv_sems.at[npeers * a + k],
                device_id=(px, py, pc), device_id_type=MESH_ID))
        slot = my_chip if mode == "chips" else me
        own.append(pltpu.make_async_copy(s_ref if mode == "gather" else s_ref.at[slot], l_ref.at[slot],
                                         send_sems.at[npeers * len(src_refs) + a]))
    return copies, own


def _exchange_start(srcs, lands, after, *, mode, name):
    n = len(srcs)
    nsem = PEERS[mode] * n

    def body(*refs):
        token = refs[-1]
        copies, own = _exchange_copies(refs[:n], refs[n:2 * n], refs[2 * n + 1], refs[2 * n + 2], mode)
        for cp in copies + own:
            cp.start()
        token[...] = jnp.zeros_like(token)

    arrays = list(srcs) + list(lands)
    outs = pl.pallas_call(
        body, name=name,
        out_shape=(pltpu.SemaphoreType.DMA((nsem + n,)), pltpu.SemaphoreType.DMA((nsem,)),
                   *[pltpu.HBM(a.shape, a.dtype) for a in arrays], jax.ShapeDtypeStruct((SUBLANES, LANES), F32)),
        in_specs=[HBM] * (2 * n) + [ANY],
        out_specs=(SEM, SEM, *[HBM] * (2 * n), pl.BlockSpec(memory_space=pltpu.VMEM)),
        input_output_aliases={i: 2 + i for i in range(2 * n)},
        compiler_params=pltpu.CompilerParams(has_side_effects=DATAFLOW),
    )(*[pltpu.with_memory_space_constraint(a, pltpu.HBM) for a in arrays], after)
    return outs[0], outs[1], outs[2:2 + n], outs[2 + n:2 + 2 * n], outs[-1]


def _exchange_wait(send_sems, recv_sems, srcs, lands, after, *, mode, name):
    n = len(srcs)

    def body(*refs):
        copies, own = _exchange_copies(refs[:n], refs[n:2 * n], refs[2 * n], refs[2 * n + 1], mode)
        for cp in copies:
            cp.wait_send()
            cp.wait_recv()
        for cp in own:
            cp.wait()

    arrays = list(srcs) + list(lands)
    outs = pl.pallas_call(
        body, name=name,
        out_shape=tuple(pltpu.HBM(a.shape, a.dtype) for a in arrays),
        in_specs=[HBM] * (2 * n) + [SEM, SEM, ANY],
        out_specs=tuple([HBM] * (2 * n)),
        input_output_aliases={i: i for i in range(2 * n)},
        compiler_params=pltpu.CompilerParams(has_side_effects=DATAFLOW),
    )(*arrays, send_sems, recv_sems, after)
    return outs[n:]


def _small_all_reduce(parts):
    def body(gmp_ref, gmo_ref, gfp_ref, gfo_ref, ga_ref, gc_ref, dw_ref, bf_ref, loss_ref,
             out_ref, buf, send_sems, recv_sems):
        x, y, c = _position()
        me = 4 * x + 2 * y + c

        def colsum(v):
            return jnp.sum(v, axis=0, keepdims=True)

        loss = jnp.sum(colsum(loss_ref[...]), axis=1, keepdims=True) * (0.5 / D)
        rows = [colsum(gmp_ref[...]), colsum(gmo_ref[...]), colsum(gfp_ref[...]), colsum(gfo_ref[...]),
                jnp.concatenate([colsum(ga_ref[...]), colsum(gc_ref[...])], axis=1),
                jnp.concatenate([colsum(dw_ref[0]), colsum(dw_ref[1])], axis=1),
                jnp.concatenate([colsum(dw_ref[2]), colsum(bf_ref[...]), jnp.broadcast_to(loss, (1, 128)),
                                 jnp.zeros((1, 256), F32)], axis=1),
                jnp.zeros((1, D), F32)]
        buf[me] = jnp.concatenate(rows, axis=0)
        copies = []
        for mm in range(1, NDEV):
            peer = (x ^ (mm >> 2), y ^ ((mm >> 1) & 1), c ^ (mm & 1))
            copies.append(pltpu.make_async_remote_copy(
                src_ref=buf.at[me], dst_ref=buf.at[me], send_sem=send_sems.at[mm - 1], recv_sem=recv_sems.at[mm - 1],
                device_id=peer, device_id_type=MESH_ID))
        for cp in copies:
            cp.start()
        for cp in copies:
            cp.wait_recv()
        for cp in copies:
            cp.wait_send()
        acc = buf[0]
        for d in range(1, NDEV):
            acc = acc + buf[d]
        out_ref[...] = acc

    vm = pl.BlockSpec(memory_space=pltpu.VMEM)
    return pl.pallas_call(
        body, name="small_all_reduce",
        out_shape=jax.ShapeDtypeStruct((SUBLANES, D), F32),
        in_specs=[vm] * len(parts), out_specs=vm,
        scratch_shapes=[pltpu.VMEM((NDEV, SUBLANES, D), F32), pltpu.SemaphoreType.DMA((7,)), pltpu.SemaphoreType.DMA((7,))],
    )(*parts)


def _adam_update(w, g, m, v):
    nm = ADAM_B1 * m + (1.0 - ADAM_B1) * g
    nv = ADAM_B2 * v + (1.0 - ADAM_B2) * (g * g)
    m_hat = nm / (1.0 - ADAM_B1 ** ADAM_STEP)
    v_hat = nv / (1.0 - ADAM_B2 ** ADAM_STEP)
    return -ADAM_LR * (m_hat / (jnp.sqrt(v_hat) + ADAM_EPS) + ADAM_WD * w), nm, nv


SMALL_SLOTS = {"g_mix_pre": (0, 0, D), "g_mix_post": (1, 0, D), "g_ffn_pre": (2, 0, D), "g_ffn_post": (3, 0, D),
               "g_attn_out": (4, 0, AW), "g_conv_out": (4, AW, CW), "b_forget": (6, CW, H)}
LOSS_LANE = CW + 128


def _small_adamw(small, conv_grad, params):
    names = list(params)
    n = len(names)

    def body(*refs):
        small_ref, cg_ref = refs[0], refs[1]
        ins, outs = refs[2:2 + 3 * n], refs[2 + 3 * n:]
        for i, name in enumerate(names):
            w_ref, m_ref, v_ref = ins[3 * i:3 * i + 3]
            g_ref, d_ref, nm_ref, nv_ref = outs[4 * i:4 * i + 4]
            if name == "conv_w":
                g = cg_ref[...]
            else:
                r, c0, width = SMALL_SLOTS[name]
                g = small_ref[r:r + 1, c0:c0 + width]
            g_ref[...] = g
            d_ref[...], nm_ref[...], nv_ref[...] = _adam_update(w_ref[...], g, m_ref[...], v_ref[...])
        outs[4 * n][...] = small_ref[6:7, LOSS_LANE:LOSS_LANE + 1]

    vm = pl.BlockSpec(memory_space=pltpu.VMEM)
    flat = [a for name in names for a in params[name]]
    outs = pl.pallas_call(
        body, name="adamw_small",
        in_specs=[vm] * (2 + 3 * n), out_specs=[vm] * (4 * n + 1),
        out_shape=[jax.ShapeDtypeStruct(params[name][0].shape, F32) for name in names for _ in range(4)]
        + [jax.ShapeDtypeStruct((1, 1), F32)],
    )(small, conv_grad, *flat)
    return {name: outs[4 * i:4 * i + 4] for i, name in enumerate(names)}, outs[4 * n].reshape(())


def _chip_sum_adamw(got, own, idx, wt, mt, vt, *, tr, name):
    cols, rows = wt.shape
    gcols = own.shape[1]

    def body(idx_ref, got_ref, own_ref, w_ref, m_ref, v_ref, g_ref, d_ref, nm_ref, nv_ref):
        g = jnp.zeros((tr, gcols), F32)
        for j in range(4):
            g = g + jnp.where(idx_ref[1] == j, own_ref[...], got_ref[j].astype(F32))
        g = g.T[:cols]
        g_ref[...] = g
        d_ref[...], nm_ref[...], nv_ref[...] = _adam_update(w_ref[...], g, m_ref[...], v_ref[...])

    spec = pl.BlockSpec((cols, tr), lambda i, idx: (0, i))
    gspec = pl.BlockSpec((tr, gcols), lambda i, idx: (i, 0))
    return pl.pallas_call(
        body, name=name,
        grid_spec=pltpu.PrefetchScalarGridSpec(
            num_scalar_prefetch=1, grid=(rows // tr,),
            in_specs=[pl.BlockSpec((4, tr, gcols), lambda i, idx: (0, i, 0)), gspec, spec, spec, spec],
            out_specs=[spec] * 4),
        out_shape=[jax.ShapeDtypeStruct((cols, rows), F32)] * 4,
        compiler_params=_cparams(32, ("arbitrary",)),
    )(idx, got, own, wt, mt, vt)


def _device_sum_adamw(land, w, m, v, *, tr, name):
    rows, cols = w.shape

    def body(land_ref, w_ref, m_ref, v_ref, g_ref, d_ref, nm_ref, nv_ref):
        g = land_ref[0].astype(F32)
        for dev in range(1, NDEV):
            g = g + land_ref[dev].astype(F32)
        g_ref[...] = g
        d_ref[...], nm_ref[...], nv_ref[...] = _adam_update(w_ref[...], g, m_ref[...], v_ref[...])

    spec = pl.BlockSpec((tr, cols), lambda i: (i, 0))
    return pl.pallas_call(
        body, name=name, grid=(rows // tr,),
        in_specs=[pl.BlockSpec((NDEV, tr, cols), lambda i: (0, i, 0)), spec, spec, spec],
        out_specs=[spec] * 4,
        out_shape=[jax.ShapeDtypeStruct((rows, cols), F32)] * 4,
        compiler_params=_cparams(32, ("arbitrary",)),
    )(land, w, m, v)


def _placement_constants():
    j = np.arange(128)[:, None]
    lane = np.arange(1024)[None, :]
    head, sub = lane // HP, lane % HP
    piece, jh = j // H, j % H
    valid = (j < 3 * H) & (jh == head)
    pq = np.where(valid & (sub == DH + piece), 1.0, 0.0).astype(BF16)
    pk = np.where(valid & (sub == DH + 3 + piece), -1.0, 0.0).astype(BF16)
    oq = np.where((sub >= DH + 3) & (sub < DH + 6), 1.0, 0.0).astype(np.float32)
    ok = np.where((sub >= DH) & (sub < DH + 3), 1.0, 0.0).astype(np.float32)
    r = np.arange(AW)[:, None]
    cc = np.arange(128)[None, :]
    sel = np.where((r % DH == 3) & (r // DH == cc), -1.0, 0.0).astype(BF16)
    gi = np.arange(GS)
    gsum = (gi[:, None] // DH == gi[None, :] // DH).astype(BF16)
    return tuple(jnp.asarray(c) for c in (pq, pk, oq, ok, sel, gsum))


def _local_step(xs, tgt, wp, late_weights, cw8, bfp, g_attn_out, g_conv_out,
                g_mix_pre, g_mix_post, g_ffn_pre, g_ffn_post, early_grads=None, last_grad=None):
    pq, pk, oq, ok, sel, gsum = _placement_constants()
    h1t, qp, kp, vv, bcu, zf = _in_proj(xs, g_mix_pre, wp, bfp, pq, pk, oq, ok, tm=512)
    o, lse, mk = _attn_fwd(qp, kp, vv, t=512)
    w_out_f, wgu, wd = late_weights(lse)
    merged, y, x2, cv, h2 = _mix_out(o, bcu, cw8, g_attn_out, g_conv_out, gsum, w_out_f, xs, g_mix_post, g_ffn_pre, tm=512)
    gate, up, act, dx3, dff, loss_p, dg_ffn_post = _ffn_fwd_loss(h2, wgu, wd, x2, tgt, g_ffn_post, tm=512)

    dgu, dx2, dy, dg_ffn_pre, dg_mix_post = _ffn_bwd(dff, wd, gate, up, wgu, x2, g_ffn_pre, dx3, y, g_mix_post, tm=256)
    dw_down = _grad_matmul(act, dff, ta=DFF // 2, tb=D, ts=4096, name="grad_w_down", vmem_mb=60)
    dw_gu = _grad_matmul(dgu, h2, ta=DFF // 2, tb=D, ts=4096, name="grad_w_gate_up", vmem_mb=60).reshape(NDEV, FB, D)
    dw_out = _grad_matmul(merged, dy, ta=1024, tb=1024, ts=2048, name="grad_w_out")
    token = early_grads(dw_out, dw_gu, dw_down) if early_grads is not None else dw_out
    do, dl, dcv, db, dg_attn, dg_conv = _mix_bwd(dy, w_out_f, o, cv, bcu, g_attn_out, g_conv_out, gsum, token, tm=512)
    dbcu, dtaps = _conv_bwd(dcv, db, bcu, cw8, tm=512)
    dqp, dkp, dv, dkx = _attn_bwd(qp, kp, vv, do, lse, dl, mk, t=512)
    dfl, dbf = _forget_bwd(dkx, zf, sel, tm=512)
    pieces = (dqp, dkp, dv, dbcu, dfl)
    dwp = _grad_w_in(h1t, pieces)
    token = last_grad(dwp) if last_grad is not None else dwp
    grad_x, dg_mix_pre = _in_proj_bwd(pieces, wp, xs, g_mix_pre, dx2, token, tm=512)
    return (grad_x, dwp, dw_out, dw_gu, dw_down, dg_mix_pre, dg_mix_post, dg_ffn_pre, dg_ffn_post, dg_attn, dg_conv,
            dtaps, dbf, loss_p)


BIG_TILES = {"w_in": 256, "w_out": 128, "w_gate_up": 176, "w_down": 176}


def kernel(x, w_in, b_forget, conv_w, g_attn_out, g_conv_out, w_out, g_mix_pre, g_mix_post, w_gate_up, w_down, g_ffn_pre, g_ffn_post, loss_target, m_w_in, m_b_forget, m_conv_w, m_g_attn_out, m_g_conv_out, m_w_out, m_g_mix_pre, m_g_mix_post, m_w_gate_up, m_w_down, m_g_ffn_pre, m_g_ffn_post, v_w_in, v_b_forget, v_conv_w, v_g_attn_out, v_g_conv_out, v_w_out, v_g_mix_pre, v_g_mix_post, v_w_gate_up, v_w_down, v_g_ffn_pre, v_g_ffn_post):
    xc, yc, cc = _position()
    my_chip = 2 * xc + yc
    me = 2 * my_chip + cc
    idx = jnp.stack([cc, my_chip]).astype(jnp.int32)
    tables = _in_layout_tables()

    w_in_b = w_in[0].astype(BF16)
    g_in, g_last, g_taps = _all_gather([w_in_b[:, :IN_MAIN], w_in_b[:, IN_MAIN].reshape(SUBLANES, LANES), conv_w[0]])
    last_cols = jnp.pad(g_last.reshape(NDEV, D).T.astype(F32), ((0, 0), (0, LANES - NDEV)))
    wp = _assemble_w_in(g_in, last_cols, tables, tr=256)
    cw8 = jnp.pad(g_taps.transpose(1, 0, 2).reshape(3, CW), ((0, SUBLANES - 3), (0, 0)))

    late = [w_out[0].astype(BF16), w_gate_up[0].T.astype(BF16), w_down[0].astype(BF16)]
    ssem, rsem, late_thru, land_thru, token = _exchange_start(
        late, [lax.empty((NDEV,) + s.shape, s.dtype) for s in late], g_in, mode="gather",
        name="gather_late_start")
    bfp = jnp.pad(b_forget, ((0, 0), (0, 128 - H))) + token[0:1, :]

    def late_weights(after):
        l_out, l_gu, l_down = _exchange_wait(ssem, rsem, late_thru, land_thru, after, mode="gather", name="gather_late_wait")
        return l_out.reshape(D, D), l_gu.reshape(2, DFF, D), l_down.reshape(DFF, D)

    early = {}

    def early_grads(dw_out, dw_gu, dw_down):
        srcs = [dw_out.reshape(NDEV, D // NDEV, D), dw_gu, dw_down.reshape(NDEV, DFF // NDEV, D)]
        lands = [lax.empty(s.shape, s.dtype) for s in srcs]
        early["handles"] = _exchange_start(srcs, lands, dw_out, mode="scatter", name="scatter_early_start")
        return early["handles"][4]

    last = {}

    def last_grad(dwp):
        g_w_in = _disassemble_w_in(dwp, tables, tr=256).reshape(4, 2, D, IN_PAD)
        (from_sibling,) = _pair_exchange([g_w_in])
        pair_b, last["own"] = _pair_sum(g_w_in, from_sibling, idx, tr=D, name="grad_pair_sum_w_in")
        last["handles"] = _exchange_start([pair_b], [lax.empty(pair_b.shape, pair_b.dtype)], last["own"], mode="chips",
                                          name="chips_w_in_start")
        return last["handles"][4]

    (grad_x, dwp, dw_out, dw_gu, dw_down, dg_mix_pre, dg_mix_post, dg_ffn_pre, dg_ffn_post, dg_attn, dg_conv,
     dtaps, dbf, loss_p) = _local_step(x[0], loss_target[0], wp, late_weights, cw8, bfp, g_attn_out, g_conv_out,
                                        g_mix_pre, g_mix_post, g_ffn_pre, g_ffn_post, early_grads, last_grad)

    e_ssem, e_rsem, e_srcs, e_lands, _ = early["handles"]
    land_out, land_gu, land_down = _exchange_wait(e_ssem, e_rsem, e_srcs, e_lands, dg_mix_pre, mode="scatter",
                                                  name="scatter_early_wait")
    res = {}
    big = {"w_out": (land_out, w_out[0], m_w_out[0], v_w_out[0]),
           "w_gate_up": (land_gu, w_gate_up[0].T, m_w_gate_up[0].T, v_w_gate_up[0].T),
           "w_down": (land_down, w_down[0], m_w_down[0], v_w_down[0])}
    for name, (land, w, m, v) in big.items():
        outs = _device_sum_adamw(land, w, m, v, tr=BIG_TILES[name], name="adamw_" + name)
        res[name] = [(o.T if name == "w_gate_up" else o)[None] for o in outs]
    c_ssem, c_rsem, c_srcs, c_lands, _ = last["handles"]
    after = sum(res[n][1][0, :SUBLANES, :LANES] for n in big)
    (from_chips,) = _exchange_wait(c_ssem, c_rsem, c_srcs, c_lands, after, mode="chips", name="chips_w_in_wait")
    outs = _chip_sum_adamw(from_chips, last["own"], idx, w_in[0].T, m_w_in[0].T, v_w_in[0].T,
                           tr=BIG_TILES["w_in"], name="adamw_w_in")
    res["w_in"] = [o.T[None] for o in outs]

    small = _small_all_reduce([dg_mix_pre, dg_mix_post, dg_ffn_pre, dg_ffn_post, dg_attn, dg_conv, dtaps, dbf, loss_p])
    taps_full = jnp.concatenate([small[5:6, :CW], small[5:6, CW:], small[6:7, :CW]], axis=0)
    taps_first = lambda a: a.transpose(1, 0, 2)
    smalls = {"b_forget": (b_forget, m_b_forget, v_b_forget),
              "conv_w": (taps_first(conv_w), taps_first(m_conv_w), taps_first(v_conv_w)),
              "g_attn_out": (g_attn_out, m_g_attn_out, v_g_attn_out), "g_conv_out": (g_conv_out, m_g_conv_out, v_g_conv_out),
              "g_mix_pre": (g_mix_pre, m_g_mix_pre, v_g_mix_pre), "g_mix_post": (g_mix_post, m_g_mix_post, v_g_mix_post),
              "g_ffn_pre": (g_ffn_pre, m_g_ffn_pre, v_g_ffn_pre), "g_ffn_post": (g_ffn_post, m_g_ffn_post, v_g_ffn_post)}
    own_taps = lax.dynamic_slice(taps_full, (0, me * 64), (3, 64))[:, None, :]
    small_res, loss = _small_adamw(small, own_taps, smalls)
    for name, outs in small_res.items():
        res[name] = [taps_first(o) for o in outs] if name == "conv_w" else list(outs)

    order = ["w_in", "b_forget", "conv_w", "g_attn_out", "g_conv_out", "w_out", "g_mix_pre", "g_mix_post",
             "w_gate_up", "w_down", "g_ffn_pre", "g_ffn_post"]
    outs = [loss, grad_x[None]]
    for k in range(4):
        outs += [res[n][k] for n in order]
    return tuple(outs)
```

```python
import functools

import numpy as np

import jax
import jax.numpy as jnp
from jax import lax
from jax.experimental import pallas as pl
from jax.experimental.pallas import tpu as pltpu

F32 = jnp.float32
BF16 = jnp.bfloat16
MESH_ID = pl.DeviceIdType.MESH

D = 1024
H = 8
DH = 64
AW = 512
CW = 512
DFF = 2816
FB = DFF // 4
FF_CHUNKS = ((0, 768), (768, 768), (1536, 768), (2304, 512))
FF_CHUNKS_BWD = ((0, 1024), (1024, 1024), (2048, 768))
HP = 128
OFF_Q, OFF_K, OFF_V, OFF_BCU, OFF_F = 0, 512, 1024, 1536, 3072
WP = OFF_F + 128
PIECES = ((OFF_Q, OFF_K), (OFF_K, OFF_V), (OFF_V, OFF_BCU), (OFF_BCU, OFF_F), (OFF_F, WP))
EPS = 1e-6
LOG2E, LN2 = 1.4426950408889634, 0.6931471805599453
NDEV = 8
LANES = 128
SUBLANES = 8
IN_COLS = 385
IN_PAD = 512
IN_MAIN = 384
WIN = 640
ADAM_LR, ADAM_B1, ADAM_B2, ADAM_EPS, ADAM_WD, ADAM_STEP = 0.001, 0.9, 0.999, 1e-08, 0.01, 10

NT = (((1,), (1,)), ((), ()))
TN = (((0,), (0,)), ((), ()))


def _cparams(vmem_mb=None, sem=None):
    kw = {}
    if vmem_mb is not None:
        kw["vmem_limit_bytes"] = vmem_mb << 20
    if sem is not None:
        kw["dimension_semantics"] = sem
    return pltpu.CompilerParams(**kw)


def _full(shape):
    return pl.BlockSpec(shape, lambda *_: (0,) * len(shape))


def _resident(shape):
    return pl.BlockSpec(shape, lambda *_: (0,) * len(shape), pipeline_mode=pl.Buffered(1))


def _rows(tm, width):
    return pl.BlockSpec((tm, width), lambda i: (i, 0))


def _fold8(v):
    r, w = v.shape
    return jnp.sum(v.reshape(r // SUBLANES, SUBLANES, w), axis=0)


def _split_dot(v, m01):
    hi = v.astype(BF16)
    lo = (v - hi.astype(F32)).astype(BF16)
    return (jnp.dot(hi, m01, preferred_element_type=F32)
            + jnp.dot(lo, m01, preferred_element_type=F32))


GS = 256


def _group_sum(v, g01):
    parts = [_split_dot(v[:, c:c + GS], g01) for c in range(0, v.shape[1], GS)]
    return parts[0] if len(parts) == 1 else jnp.concatenate(parts, axis=1)


def _exact_dot01(m01, v):
    p1 = v.astype(BF16)
    r1 = v - p1.astype(F32)
    p2 = r1.astype(BF16)
    p3 = (r1 - p2.astype(F32)).astype(BF16)
    return (jnp.dot(m01, p1, preferred_element_type=F32) + jnp.dot(m01, p2, preferred_element_type=F32)
            + jnp.dot(m01, p3, preferred_element_type=F32))


def _rms_fwd(v, g):
    r = lax.rsqrt(jnp.mean(v * v, axis=-1, keepdims=True) + EPS)
    n = v * r
    return n * g, n, r


def _rms_bwd(do, n, r, g):
    dn = do * g
    return r * (dn - n * jnp.mean(dn * n, axis=-1, keepdims=True)), do * n


def _padded_column(n):
    if n < AW:
        return OFF_Q + n, 0.125
    if n < 3 * AW:
        return n, 1.0
    if n < 3 * AW + H:
        return OFF_F + n - 3 * AW, 1.0
    return OFF_BCU + n - 3 * AW - H, 1.0


def _in_layout_tables():
    dest = -np.ones((IN_PAD, LANES), np.int32)
    dest_f = -np.ones((IN_PAD, LANES), np.int32)
    scale = np.zeros((IN_PAD, LANES), np.float32)
    starts = []
    for k in range(NDEV):
        cols = [_padded_column(IN_COLS * k + j) for j in range(IN_COLS)]
        main = [c for c, _ in cols if c < OFF_F]
        ws = min((min(main) // LANES) * LANES, OFF_F - WIN)
        assert ws <= min(main) and max(main) < ws + WIN
        starts.append(ws)
        for j, (c, sc) in enumerate(cols):
            scale[j, k] = sc
            if c < OFF_F:
                dest[j, k] = c - ws
            else:
                dest_f[j, k] = c - OFF_F
    f_shards = tuple(k for k in range(NDEV) if (dest_f[:, k] >= 0).any())
    return tuple(starts), f_shards, jnp.asarray(dest), jnp.asarray(dest_f), jnp.asarray(scale)


def _perm(dest_ref, scale_ref, k, width, rows=IN_PAD):
    lane = lax.broadcasted_iota(jnp.int32, (rows, width), 1)
    return jnp.where(dest_ref[0:rows, k:k + 1] == lane, scale_ref[0:rows, k:k + 1], 0.0).astype(BF16)


def _assemble_w_in(blocks, last_cols, tables, *, tr):
    starts, f_shards, dest, dest_f, scale = tables
    last = [_padded_column(IN_COLS * k + IN_MAIN) for k in range(NDEV)]
    f_main = [any(_padded_column(IN_COLS * k + j)[0] >= OFF_F for j in range(IN_MAIN)) for k in range(NDEV)]
    assert IN_COLS == IN_MAIN + 1

    def body(b_ref, c_ref, dest_ref, destf_ref, scale_ref, o_ref):
        o_ref[...] = jnp.zeros_like(o_ref)
        lane = lax.broadcasted_iota(jnp.int32, (tr, LANES), 1)
        for k in range(NDEV):
            b = b_ref[k]
            ws = starts[k]
            part = jnp.dot(b, _perm(dest_ref, scale_ref, k, WIN, IN_MAIN), preferred_element_type=F32)
            o_ref[:, ws:ws + WIN] += part.astype(BF16)
            if f_main[k]:
                part = jnp.dot(b, _perm(destf_ref, scale_ref, k, 128, IN_MAIN), preferred_element_type=F32)
                o_ref[:, OFF_F:WP] += part.astype(BF16)
            col, sc = last[k]
            tile = (col // LANES) * LANES
            o_ref[:, tile:tile + LANES] += jnp.where(lane == col - tile, c_ref[:, k:k + 1] * sc, 0.0).astype(BF16)

    tab = _full((IN_PAD, LANES))
    return pl.pallas_call(
        body, name="assemble_w_in", grid=(D // tr,),
        in_specs=[pl.BlockSpec((NDEV, tr, IN_MAIN), lambda i: (0, i, 0)), _rows(tr, LANES), tab, tab, tab],
        out_specs=_rows(tr, WP),
        out_shape=jax.ShapeDtypeStruct((D, WP), BF16),
        compiler_params=_cparams(48, ("arbitrary",)),
    )(blocks, last_cols, dest, dest_f, scale)


def _disassemble_w_in(dwp, tables, *, tr):
    starts, f_shards, dest, dest_f, scale = tables
    width = dwp.shape[1]

    def body(g_ref, dest_ref, destf_ref, scale_ref, o_ref):
        for k in range(NDEV):
            ws = starts[k]
            acc = lax.dot_general(g_ref[:, ws:ws + WIN], _perm(dest_ref, scale_ref, k, WIN), NT, preferred_element_type=F32)
            if k in f_shards:
                acc = acc + lax.dot_general(g_ref[:, OFF_F:WP], _perm(destf_ref, scale_ref, k, 128), NT,
                                            preferred_element_type=F32)
            o_ref[k] = acc.astype(BF16)

    tab = _full((IN_PAD, LANES))
    return pl.pallas_call(
        body, name="disassemble_w_in", grid=(D // tr,),
        in_specs=[_rows(tr, width), tab, tab, tab],
        out_specs=pl.BlockSpec((NDEV, tr, IN_PAD), lambda i: (0, i, 0)),
        out_shape=jax.ShapeDtypeStruct((NDEV, D, IN_PAD), BF16),
        compiler_params=_cparams(48, ("arbitrary",)),
    )(dwp, dest, dest_f, scale)


def _in_proj(x, g1, wp, bfp, pq, pk, oq, ok, *, tm):
    s = x.shape[0]

    def body(x_ref, g_ref, w_ref, bf_ref, pq_ref, pk_ref, oq_ref, ok_ref,
             ht_ref, qp_ref, kp_ref, v_ref, bcu_ref, z_ref, carry):
        @pl.when(pl.program_id(0) == 0)
        def _():
            carry[...] = jnp.zeros_like(carry)

        h = _rms_fwd(x_ref[...], g_ref[...])[0].astype(BF16)
        ht_ref[...] = h.T
        z = jnp.dot(h, w_ref[:, OFF_F:WP], preferred_element_type=F32) + bf_ref[...]
        z_ref[...] = z
        lane = lax.broadcasted_iota(jnp.int32, (tm, 128), 1)
        logf = jnp.where(lane < H, jnp.minimum(z, 0.0) - jnp.log(1.0 + jnp.exp(-jnp.abs(z))), 0.0)
        row = lax.broadcasted_iota(jnp.int32, (tm, tm), 0)
        col = lax.broadcasted_iota(jnp.int32, (tm, tm), 1)
        tri = (col <= row).astype(BF16)
        c = _exact_dot01(tri, logf) + carry[0:1, :]
        carry[...] = jnp.broadcast_to(c[tm - 1:tm, :], carry.shape)
        cb = c * LOG2E
        c1 = cb.astype(BF16).astype(F32)
        r1 = cb - c1
        c2 = r1.astype(BF16).astype(F32)
        c3 = (r1 - c2).astype(BF16).astype(F32)
        zc = (c1 + pltpu.roll(c2, 8, axis=1) + pltpu.roll(c3, 16, axis=1)).astype(BF16)

        def pad_heads(v):
            blocks = []
            for pair in range(H // 2):
                two = v[:, 128 * pair:128 * (pair + 1)]
                blocks.append(jnp.where(lane < DH, two, 0.0))
                blocks.append(jnp.where(lane < DH, pltpu.roll(two, DH, axis=1), 0.0))
            return jnp.concatenate(blocks, axis=1)

        q = jnp.dot(h, w_ref[:, OFF_Q:OFF_K], preferred_element_type=F32) * LOG2E
        qp_ref[...] = (pad_heads(q) + jnp.dot(zc, pq_ref[...], preferred_element_type=F32) + oq_ref[...]).astype(BF16)
        k = jnp.dot(h, w_ref[:, OFF_K:OFF_V], preferred_element_type=F32)
        kp_ref[...] = (pad_heads(k) + jnp.dot(zc, pk_ref[...], preferred_element_type=F32) + ok_ref[...]).astype(BF16)
        v = pad_heads(jnp.dot(h, w_ref[:, OFF_V:OFF_BCU], preferred_element_type=F32))
        ones_lane = lax.broadcasted_iota(jnp.int32, (tm, H * HP), 1) % HP == DH
        v_ref[...] = jnp.where(ones_lane, 1.0, v).astype(BF16)
        bcu_ref[...] = jnp.dot(h, w_ref[:, OFF_BCU:OFF_F], preferred_element_type=F32).astype(BF16)

    return pl.pallas_call(
        body, name="in_proj", grid=(s // tm,),
        in_specs=[_rows(tm, D), _full((1, D)), _resident((D, WP)), _full((1, 128)),
                  _full((128, 1024)), _full((128, 1024)), _full((1, 1024)), _full((1, 1024))],
        out_specs=[pl.BlockSpec((D, tm), lambda i: (0, i)), _rows(tm, 1024), _rows(tm, 1024), _rows(tm, 1024),
                   _rows(tm, 3 * CW), _rows(tm, 128)],
        out_shape=[jax.ShapeDtypeStruct((D, s), BF16), jax.ShapeDtypeStruct((s, 1024), BF16),
                   jax.ShapeDtypeStruct((s, 1024), BF16), jax.ShapeDtypeStruct((s, 1024), BF16),
                   jax.ShapeDtypeStruct((s, 3 * CW), BF16), jax.ShapeDtypeStruct((s, 128), F32)],
        scratch_shapes=[pltpu.VMEM((SUBLANES, 128), F32)],
        compiler_params=_cparams(56, ("arbitrary",)),
    )(x, g1, wp, bfp, pq, pk, oq, ok)


def _attn_fwd(qp, kp, v, *, t):
    s = qp.shape[0]
    nq = s // t

    def body(q_ref, k_ref, v_ref, o_ref, lse_ref, mk_ref):
        pi = pl.program_id(1)
        row = lax.broadcasted_iota(jnp.int32, (t, t), 0)
        col = lax.broadcasted_iota(jnp.int32, (t, t), 1)
        lane = lax.broadcasted_iota(jnp.int32, (t, 128), 1)

        def head_step(hh, rows, ki, carry, masked):
            m, acc = carry
            off = pl.multiple_of(ki * t, t)
            q = q_ref[rows, HP * hh:HP * (hh + 1)]
            k = k_ref[pl.ds(off, t), HP * hh:HP * (hh + 1)]
            sc = lax.dot_general(q, k, NT, preferred_element_type=F32)
            if masked:
                sc = jnp.where(col <= row, sc, -1e30)
            mn = jnp.maximum(m, jnp.max(sc, axis=-1, keepdims=True))
            p = jnp.exp2(sc - mn).astype(BF16)
            acc = jnp.exp2(m - mn) * acc + jnp.dot(p, v_ref[pl.ds(off, t), HP * hh:HP * (hh + 1)],
                                                  preferred_element_type=F32)
            return mn, acc

        def step(rows, ki, carry, masked):
            new = tuple(head_step(hh, rows, ki, carry[hh], masked) for hh in range(2))
            mk_ref[ki, rows] = jnp.where(lane < DH, jnp.broadcast_to(new[0][0], (t, 128)),
                                         jnp.broadcast_to(new[1][0], (t, 128)))
            return new

        init = (jnp.full((t, 1), -1e30, F32), jnp.zeros((t, 128), F32))
        top, bottom = slice(0, t), slice(t, 2 * t)

        def quad(j, carry):
            c0, c1 = carry
            c0 = step(top, 2 * j, c0, False)
            c1 = step(bottom, 2 * j, c1, False)
            c0 = step(top, 2 * j + 1, c0, False)
            c1 = step(bottom, 2 * j + 1, c1, False)
            return c0, c1

        c0, c1 = lax.fori_loop(0, pi, quad, ((init, init), (init, init)))
        f0 = step(top, 2 * pi, c0, True)
        c1 = step(bottom, 2 * pi, c1, False)
        f1 = step(bottom, 2 * pi + 1, c1, True)
        for rows, ((m0, acc0), (m1, acc1)) in ((top, f0), (bottom, f1)):
            l0, l1 = acc0[:, DH:DH + 1], acc1[:, DH:DH + 1]
            o_ref[rows, :] = jnp.where(lane < DH, acc0 / l0, pltpu.roll(acc1 / l1, DH, axis=1))
            lse_ref[rows, :] = jnp.where(lane < DH, jnp.broadcast_to(m0 + jnp.log2(l0), (t, 128)),
                                         jnp.broadcast_to(m1 + jnp.log2(l1), (t, 128)))

    return pl.pallas_call(
        body, name="attn_fwd", grid=(H // 2, nq // 2),
        in_specs=[pl.BlockSpec((2 * t, 2 * HP), lambda p, i: (i, p)),
                  pl.BlockSpec((s, 2 * HP), lambda p, i: (0, p)),
                  pl.BlockSpec((s, 2 * HP), lambda p, i: (0, p))],
        out_specs=[pl.BlockSpec((2 * t, 128), lambda p, i: (i, p)), pl.BlockSpec((2 * t, 128), lambda p, i: (i, p)),
                   pl.BlockSpec((nq, 2 * t, 128), lambda p, i: (0, i, p))],
        out_shape=[jax.ShapeDtypeStruct((s, AW), F32), jax.ShapeDtypeStruct((s, AW), F32),
                   jax.ShapeDtypeStruct((nq, s, AW), F32)],
        compiler_params=_cparams(48, ("arbitrary", "arbitrary")),
    )(qp, kp, v)


HALO = 16


def _conv_taps(bcu_ref, halo_ref, first, tm):
    z = bcu_ref[:, CW:2 * CW].astype(F32) * bcu_ref[:, 2 * CW:3 * CW].astype(F32)
    zh = jnp.where(first, 0.0, halo_ref[:, CW:2 * CW].astype(F32) * halo_ref[:, 2 * CW:3 * CW].astype(F32))
    row = lax.broadcasted_iota(jnp.int32, (tm, CW), 0)
    last, before = zh[HALO - 1:HALO, :], zh[HALO - 2:HALO - 1, :]
    z1 = jnp.where(row == 0, last, pltpu.roll(z, 1, axis=0))
    z2 = jnp.where(row == 0, before, jnp.where(row == 1, last, pltpu.roll(z, 2, axis=0)))
    return z, z1, z2


def _halo_before(tm, width):
    return pl.BlockSpec((HALO, width), lambda i: (jnp.maximum(i * (tm // HALO) - 1, 0), 0))


def _mix_out(o, bcu, cw8, ga, gc, gsum, w_out, x, g_post, g_ffn_pre, *, tm):
    s = x.shape[0]

    def body(o_ref, bcu_ref, halo_ref, cw_ref, ga_ref, gc_ref, gs_ref, w_ref, x_ref, g_ref, gf_ref,
             merged_ref, y_ref, x2_ref, cv_ref, h2_ref):
        z, z1, z2 = _conv_taps(bcu_ref, halo_ref, pl.program_id(0) == 0, tm)
        cv = cw_ref[0:1, :] * z2 + cw_ref[1:2, :] * z1 + cw_ref[2:3, :] * z
        cv_ref[...] = cv
        conv = bcu_ref[:, 0:CW].astype(F32) * cv
        ov = o_ref[...]
        ra = lax.rsqrt(_group_sum(ov * ov, gs_ref[...]) * (1.0 / DH) + EPS)
        rc = lax.rsqrt(_group_sum(conv * conv, gs_ref[...]) * (1.0 / DH) + EPS)
        merged = jnp.concatenate([ov * ra * ga_ref[...], conv * rc * gc_ref[...]], axis=1).astype(BF16)
        merged_ref[...] = merged
        y = jnp.dot(merged, w_ref[...], preferred_element_type=F32)
        y_ref[...] = y
        x2 = x_ref[...] + _rms_fwd(y, g_ref[...])[0]
        x2_ref[...] = x2
        h2_ref[...] = _rms_fwd(x2, gf_ref[...])[0].astype(BF16)

    return pl.pallas_call(
        body, name="mix_out", grid=(s // tm,),
        in_specs=[_rows(tm, AW), _rows(tm, 3 * CW), _halo_before(tm, 3 * CW), _full((SUBLANES, CW)),
                  _full((1, AW)), _full((1, CW)), _full((GS, GS)), _resident((D, D)), _rows(tm, D), _full((1, D)),
                  _full((1, D))],
        out_specs=[_rows(tm, D), _rows(tm, D), _rows(tm, D), _rows(tm, CW), _rows(tm, D)],
        out_shape=[jax.ShapeDtypeStruct((s, D), BF16), jax.ShapeDtypeStruct((s, D), F32),
                   jax.ShapeDtypeStruct((s, D), F32), jax.ShapeDtypeStruct((s, CW), F32),
                   jax.ShapeDtypeStruct((s, D), BF16)],
        compiler_params=_cparams(48, ("arbitrary",)),
    )(o, bcu, bcu, cw8, ga, gc, gsum, w_out, x, g_post, g_ffn_pre)


def _ffn_fwd_loss(h2, wgu, wd, x2, target, g_post, *, tm):
    s = x2.shape[0]

    def body(h_ref, w_ref, wd_ref, x2_ref, t_ref, g_ref,
             gate_ref, up_ref, a_ref, dx3_ref, dff_ref, loss_ref, dg_ref):
        @pl.when(pl.program_id(0) == 0)
        def _():
            loss_ref[...] = jnp.zeros_like(loss_ref)
            dg_ref[...] = jnp.zeros_like(dg_ref)

        h = h_ref[...]
        ff = None
        for c0, n in FF_CHUNKS:
            cols = slice(c0, c0 + n)
            gate = lax.dot_general(h, w_ref[0, cols, :], NT, preferred_element_type=F32)
            up = lax.dot_general(h, w_ref[1, cols, :], NT, preferred_element_type=F32)
            gate_ref[:, cols] = gate.astype(BF16)
            up_ref[:, cols] = up.astype(BF16)
            act = (gate * jax.nn.sigmoid(gate) * up).astype(BF16)
            a_ref[:, cols] = act
            part = jnp.dot(act, wd_ref[cols, :], preferred_element_type=F32)
            ff = part if ff is None else ff + part
        out, n, r = _rms_fwd(ff, g_ref[...])
        e = x2_ref[...] + out - t_ref[...]
        loss_ref[...] += _fold8(e * e)
        dx3 = e * (1.0 / D)
        dx3_ref[...] = dx3
        dff, dg = _rms_bwd(dx3, n, r, g_ref[...])
        dff_ref[...] = dff.astype(BF16)
        dg_ref[...] += _fold8(dg)

    wide = _rows(tm, DFF)
    return pl.pallas_call(
        body, name="ffn_fwd_loss", grid=(s // tm,),
        in_specs=[_rows(tm, D), _resident((2, DFF, D)), _resident((DFF, D)), _rows(tm, D), _rows(tm, D), _full((1, D))],
        out_specs=[wide, wide, wide, _rows(tm, D), _rows(tm, D), _full((SUBLANES, D)), _full((SUBLANES, D))],
        out_shape=[jax.ShapeDtypeStruct((s, DFF), BF16)] * 3
        + [jax.ShapeDtypeStruct((s, D), F32), jax.ShapeDtypeStruct((s, D), BF16),
           jax.ShapeDtypeStruct((SUBLANES, D), F32), jax.ShapeDtypeStruct((SUBLANES, D), F32)],
        compiler_params=_cparams(56, ("arbitrary",)),
    )(h2, wgu, wd, x2, target, g_post)


def _ffn_bwd(dff, wd, gate, up, wgu, x2, g_pre, dx3, y, g_post, *, tm):
    s = x2.shape[0]

    def body(dff_ref, wd_ref, gate_ref, up_ref, w_ref, x2_ref, gpre_ref, dx3_ref, y_ref, gpost_ref,
             dgu_ref, dx2_ref, dy_ref, dgpre_ref, dgpost_ref):
        @pl.when(pl.program_id(0) == 0)
        def _():
            dgpre_ref[...] = jnp.zeros_like(dgpre_ref)
            dgpost_ref[...] = jnp.zeros_like(dgpost_ref)

        dff = dff_ref[...]
        dh2 = None
        for c0, n in FF_CHUNKS_BWD:
            cols = slice(c0, c0 + n)
            da = lax.dot_general(dff, wd_ref[cols, :], NT, preferred_element_type=F32)
            g = gate_ref[:, cols].astype(F32)
            sg = jax.nn.sigmoid(g)
            dgate = (da * up_ref[:, cols].astype(F32) * (sg * (1.0 + g * (1.0 - sg)))).astype(BF16)
            dup = (da * (g * sg)).astype(BF16)
            dgu_ref[:, cols] = dgate
            dgu_ref[:, DFF + c0:DFF + c0 + n] = dup
            part = (jnp.dot(dgate, w_ref[0, cols, :], preferred_element_type=F32)
                    + jnp.dot(dup, w_ref[1, cols, :], preferred_element_type=F32))
            dh2 = part if dh2 is None else dh2 + part
        _, n2, r2 = _rms_fwd(x2_ref[...], gpre_ref[...])
        dxn, dg = _rms_bwd(dh2, n2, r2, gpre_ref[...])
        dgpre_ref[...] += _fold8(dg)
        dx2 = dx3_ref[...] + dxn
        dx2_ref[...] = dx2
        _, ny, ry = _rms_fwd(y_ref[...], gpost_ref[...])
        dy, dg2 = _rms_bwd(dx2, ny, ry, gpost_ref[...])
        dy_ref[...] = dy.astype(BF16)
        dgpost_ref[...] += _fold8(dg2)

    wide = _rows(tm, DFF)
    return pl.pallas_call(
        body, name="ffn_bwd", grid=(s // tm,),
        in_specs=[_rows(tm, D), _resident((DFF, D)), wide, wide, _resident((2, DFF, D)), _rows(tm, D), _full((1, D)),
                  _rows(tm, D), _rows(tm, D), _full((1, D))],
        out_specs=[_rows(tm, 2 * DFF), _rows(tm, D), _rows(tm, D),
                   _full((SUBLANES, D)), _full((SUBLANES, D))],
        out_shape=[jax.ShapeDtypeStruct((s, 2 * DFF), BF16), jax.ShapeDtypeStruct((s, D), F32),
                   jax.ShapeDtypeStruct((s, D), BF16), jax.ShapeDtypeStruct((SUBLANES, D), F32),
                   jax.ShapeDtypeStruct((SUBLANES, D), F32)],
        compiler_params=_cparams(56, ("arbitrary",)),
    )(dff, wd, gate, up, wgu, x2, g_pre, dx3, y, g_post)


def _grad_matmul(a, b, *, ta, tb, ts, name, vmem_mb=48):
    s, ka = a.shape
    nb = b.shape[1]
    ts = min(ts, s)
    nk = s // ts

    def body(a_ref, b_ref, o_ref, *acc):
        if nk == 1:
            o_ref[...] = lax.dot_general(a_ref[...], b_ref[...], TN, preferred_element_type=F32).astype(BF16)
            return
        k = pl.program_id(2)

        @pl.when(k == 0)
        def _():
            acc[0][...] = jnp.zeros_like(acc[0])

        acc[0][...] += lax.dot_general(a_ref[...], b_ref[...], TN, preferred_element_type=F32)

        @pl.when(k == nk - 1)
        def _():
            o_ref[...] = acc[0][...].astype(BF16)

    whole_b = {"pipeline_mode": pl.Buffered(1)} if nk == 1 and nb == tb else {}
    return pl.pallas_call(
        body, name=name, grid=(ka // ta, nb // tb, nk),
        in_specs=[pl.BlockSpec((ts, ta), lambda i, j, k: (k, i)),
                  pl.BlockSpec((ts, tb), lambda i, j, k: (k, j), **whole_b)],
        out_specs=pl.BlockSpec((ta, tb), lambda i, j, k: (i, j)),
        out_shape=jax.ShapeDtypeStruct((ka, nb), BF16),
        scratch_shapes=[pltpu.VMEM((ta, tb), F32)] if nk > 1 else [],
        compiler_params=_cparams(vmem_mb, ("arbitrary", "arbitrary", "arbitrary")),
    )(a, b)


GW_TILE = 256


def _grad_w_in(h1t, pieces):
    ka, s = h1t.shape
    widths = [p.shape[1] for p in pieces]
    assert all(w % GW_TILE == 0 for w in widths)
    first = [sum(widths[:i]) // GW_TILE for i in range(len(pieces))]
    count = [w // GW_TILE for w in widths]

    def body(a_ref, *refs):
        o_ref = refs[-1]
        j = pl.program_id(0)
        for ref, f0, n in zip(refs[:-1], first, count):
            @pl.when((j >= f0) & (j < f0 + n))
            def _(ref=ref):
                o_ref[...] = jnp.dot(a_ref[...], ref[...], preferred_element_type=F32).astype(BF16)

    def spec(f0, n):
        return pl.BlockSpec((s, GW_TILE), lambda j: (0, jnp.clip(j - f0, 0, n - 1)))

    return pl.pallas_call(
        body, name="grad_w_in", grid=(sum(count),),
        in_specs=[_resident((ka, s))] + [spec(f0, n) for f0, n in zip(first, count)],
        out_specs=pl.BlockSpec((ka, GW_TILE), lambda j: (0, j)),
        out_shape=jax.ShapeDtypeStruct((ka, sum(widths)), BF16),
        compiler_params=_cparams(56, ("arbitrary",)),
    )(h1t, *pieces)


def _mix_bwd(dy, w_out, o, cv, bcu, ga, gc, gsum, after, *, tm):
    s = dy.shape[0]

    def group_norm_bwd(dn_out, v, g, gs):
        r = lax.rsqrt(_group_sum(v * v, gs) * (1.0 / DH) + EPS)
        n = v * r
        dn = dn_out * g
        return r * (dn - n * (_group_sum(dn * n, gs) * (1.0 / DH))), dn_out * n

    def body(dy_ref, w_ref, o_ref, cv_ref, bcu_ref, ga_ref, gc_ref, gs_ref, after_ref,
             do_ref, dl_ref, dcv_ref, db_ref, dga_ref, dgc_ref):
        @pl.when(pl.program_id(0) == 0)
        def _():
            dga_ref[...] = jnp.zeros_like(dga_ref)
            dgc_ref[...] = jnp.zeros_like(dgc_ref)

        dm = lax.dot_general(dy_ref[...], w_ref[...], NT, preferred_element_type=F32)
        ov = o_ref[...]
        do, dga = group_norm_bwd(dm[:, 0:AW], ov, ga_ref[...], gs_ref[...])
        dob = do.astype(BF16)
        do_ref[...] = dob
        dl_ref[...] = _group_sum(dob.astype(F32) * ov, gs_ref[...])
        dga_ref[...] += _fold8(dga)
        gate_b = bcu_ref[:, 0:CW].astype(F32)
        cv = cv_ref[...]
        dconv, dgc = group_norm_bwd(dm[:, AW:D], gate_b * cv, gc_ref[...], gs_ref[...])
        dgc_ref[...] += _fold8(dgc)
        dcv_ref[...] = dconv * gate_b
        db_ref[...] = (dconv * cv).astype(BF16)

    return pl.pallas_call(
        body, name="mix_bwd", grid=(s // tm,),
        in_specs=[_rows(tm, D), _resident((D, D)), _rows(tm, AW), _rows(tm, CW), _rows(tm, 3 * CW),
                  _full((1, AW)), _full((1, CW)), _full((GS, GS)), ANY],
        out_specs=[_rows(tm, AW), _rows(tm, AW), _rows(tm, CW), _rows(tm, CW),
                   _full((SUBLANES, AW)), _full((SUBLANES, CW))],
        out_shape=[jax.ShapeDtypeStruct((s, AW), BF16), jax.ShapeDtypeStruct((s, AW), F32),
                   jax.ShapeDtypeStruct((s, CW), F32), jax.ShapeDtypeStruct((s, CW), BF16),
                   jax.ShapeDtypeStruct((SUBLANES, AW), F32), jax.ShapeDtypeStruct((SUBLANES, CW), F32)],
        compiler_params=_cparams(48, ("arbitrary",)),
    )(dy, w_out, o, cv, bcu, ga, gc, gsum, after)


def _conv_bwd(dcv, db, bcu, cw8, *, tm):
    s = dcv.shape[0]
    nt = s // tm

    def body(dcv_ref, nxt_ref, db_ref, bcu_ref, halo_ref, cw_ref, dbcu_ref, dw_ref):
        i = pl.program_id(0)

        @pl.when(i == 0)
        def _():
            dw_ref[...] = jnp.zeros_like(dw_ref)

        z, z1, z2 = _conv_taps(bcu_ref, halo_ref, i == 0, tm)
        d = dcv_ref[...]
        dw_ref[0] += _fold8(d * z2)
        dw_ref[1] += _fold8(d * z1)
        dw_ref[2] += _fold8(d * z)
        nx = jnp.where(i == nt - 1, 0.0, nxt_ref[...])
        row = lax.broadcasted_iota(jnp.int32, (tm, CW), 0)
        d1 = jnp.where(row == tm - 1, nx[0:1, :], pltpu.roll(d, tm - 1, axis=0))
        d2 = jnp.where(row == tm - 2, nx[0:1, :], jnp.where(row == tm - 1, nx[1:2, :], pltpu.roll(d, tm - 2, axis=0)))
        dz = cw_ref[2:3, :] * d + cw_ref[1:2, :] * d1 + cw_ref[0:1, :] * d2
        dbcu_ref[:, 0:CW] = db_ref[...]
        dbcu_ref[:, CW:2 * CW] = (dz * bcu_ref[:, 2 * CW:3 * CW].astype(F32)).astype(BF16)
        dbcu_ref[:, 2 * CW:3 * CW] = (dz * bcu_ref[:, CW:2 * CW].astype(F32)).astype(BF16)

    return pl.pallas_call(
        body, name="conv_bwd", grid=(nt,),
        in_specs=[_rows(tm, CW),
                  pl.BlockSpec((SUBLANES, CW), lambda i: (jnp.minimum((i + 1) * (tm // SUBLANES), s // SUBLANES - 1), 0)),
                  _rows(tm, CW), _rows(tm, 3 * CW), _halo_before(tm, 3 * CW), _full((SUBLANES, CW))],
        out_specs=[_rows(tm, 3 * CW), _full((3, SUBLANES, CW))],
        out_shape=[jax.ShapeDtypeStruct((s, 3 * CW), BF16), jax.ShapeDtypeStruct((3, SUBLANES, CW), F32)],
        compiler_params=_cparams(48, ("arbitrary",)),
    )(dcv, dcv, db, bcu, bcu, cw8)


def _attn_bwd(qp, kp, v, do, lse, dl, mk, *, t):
    s = qp.shape[0]
    nq = s // t

    def body(q_ref, k_ref, v_ref, do_ref, lse_ref, dl_ref, mk_ref, dq_ref, dk_ref, dv_ref, dkx_ref, dq_acc):
        pi = pl.program_id(1)

        @pl.when(pi == 0)
        def _():
            dq_acc[...] = jnp.zeros_like(dq_acc)

        row = lax.broadcasted_iota(jnp.int32, (t, t), 0)
        col = lax.broadcasted_iota(jnp.int32, (t, t), 1)
        lane = lax.broadcasted_iota(jnp.int32, (t, 128), 1)

        def head_step(hh, qi, carry, modes):
            off = pl.multiple_of(qi * t, t)
            rows = pl.ds(off, t)
            q = q_ref[rows, HP * hh:HP * (hh + 1)]
            qt = q.T
            lse_col = lse_ref[rows, DH * hh:DH * hh + 1]
            dl_col = dl_ref[rows, DH * hh:DH * hh + 1]
            do2 = do_ref[rows, :]
            dom = jnp.where(lane < DH, do2 if hh == 0 else pltpu.roll(do2, DH, axis=1), jnp.zeros((), BF16))
            new, dss = [], []
            for half, masked in enumerate(modes):
                if masked is None:
                    new.append(carry[half])
                    continue
                dk, dv, cs = carry[half]
                keys = slice(half * t, (half + 1) * t)
                m_col = mk_ref[half, rows, DH * hh:DH * hh + 1]
                scale = jnp.exp2(m_col - lse_col)
                sc = lax.dot_general(q, k_ref[keys, HP * hh:HP * (hh + 1)], NT, preferred_element_type=F32) - m_col
                if masked:
                    sc = jnp.where(col <= row, sc, -1e30)
                pt = jnp.exp2(sc).astype(BF16)
                dp = lax.dot_general(dom, v_ref[keys, HP * hh:HP * (hh + 1)], NT, preferred_element_type=F32)
                ds32 = (pt.astype(F32) * scale) * (dp - dl_col)
                ds = ds32.astype(BF16)
                cs = cs + _fold8(ds32)
                dv = dv + jnp.dot((dom.astype(F32) * scale).astype(BF16).T, pt, preferred_element_type=F32)
                dk = dk + jnp.dot(qt, ds, preferred_element_type=F32)
                new.append((dk, dv, cs))
                dss.append((half, ds))
            if len(dss) == 2:
                dq = jnp.dot(jnp.concatenate([dss[0][1], dss[1][1]], axis=1), k_ref[:, HP * hh:HP * (hh + 1)],
                             preferred_element_type=F32)
            else:
                half, ds = dss[0]
                dq = jnp.dot(ds, k_ref[half * t:(half + 1) * t, HP * hh:HP * (hh + 1)], preferred_element_type=F32)
            dq_acc[rows, HP * hh:HP * (hh + 1)] += dq
            return tuple(new)

        def step(qi, carry, modes):
            return tuple(head_step(hh, qi, carry[hh], modes) for hh in range(2))

        def two_heads(a0, a1):
            return jnp.where(lane < DH, a0, pltpu.roll(a1, DH, axis=1))

        def rows_to_lanes(a0, a1):
            return jnp.concatenate([a0, a1], axis=0).T

        zero = (jnp.zeros((HP, t), F32), jnp.zeros((128, t), F32), jnp.zeros((SUBLANES, t), F32))
        carry = step(2 * pi, ((zero, zero), (zero, zero)), (True, None))
        carry = step(2 * pi + 1, carry, (False, True))

        def pair(j, carry):
            qi = 2 * (pi + 1 + j)
            return step(qi + 1, step(qi, carry, (False, False)), (False, False))

        carry = lax.fori_loop(0, nq // 2 - 1 - pi, pair, carry)
        for half in range(2):
            keys = slice(half * t, (half + 1) * t)
            (dk0, dv0, cs0), (dk1, dv1, cs1) = carry[0][half], carry[1][half]
            dk_ref[keys, :] = (rows_to_lanes(dk0[0:DH], dk1[0:DH]) * LN2).astype(BF16)
            dv_ref[keys, :] = rows_to_lanes(dv0[0:DH], dv1[0:DH]).astype(BF16)
            total = lambda cs: jnp.broadcast_to(jnp.sum(cs, axis=0, keepdims=True), (DH, t))
            dkx_ref[keys, :] = rows_to_lanes(total(cs0), total(cs1))

        @pl.when(pi == nq // 2 - 1)
        def _():
            for c in range(s // t):
                rows = slice(c * t, (c + 1) * t)
                dq_ref[rows, :] = two_heads(dq_acc[rows, 0:HP], dq_acc[rows, HP:2 * HP]).astype(BF16)

    return pl.pallas_call(
        body, name="attn_bwd", grid=(H // 2, nq // 2),
        in_specs=[pl.BlockSpec((s, 2 * HP), lambda p, i: (0, p)),
                  pl.BlockSpec((2 * t, 2 * HP), lambda p, i: (i, p)),
                  pl.BlockSpec((2 * t, 2 * HP), lambda p, i: (i, p)),
                  pl.BlockSpec((s, 128), lambda p, i: (0, p)),
                  pl.BlockSpec((s, 128), lambda p, i: (0, p)),
                  pl.BlockSpec((s, 128), lambda p, i: (0, p)),
                  pl.BlockSpec((2, s, 128), lambda p, i: (i, 0, p))],
        out_specs=[pl.BlockSpec((s, 128), lambda p, i: (0, p)),
                   pl.BlockSpec((2 * t, 128), lambda p, i: (i, p)),
                   pl.BlockSpec((2 * t, 128), lambda p, i: (i, p)),
                   pl.BlockSpec((2 * t, 128), lambda p, i: (i, p))],
        out_shape=[jax.ShapeDtypeStruct((s, AW), BF16), jax.ShapeDtypeStruct((s, AW), BF16),
                   jax.ShapeDtypeStruct((s, AW), BF16), jax.ShapeDtypeStruct((s, AW), F32)],
        scratch_shapes=[pltpu.VMEM((s, 2 * HP), F32)],
        compiler_params=_cparams(56, ("arbitrary", "arbitrary")),
    )(qp, kp, v, do, lse, dl, mk)


def _forget_bwd(dkx, z, sel, *, tm):
    s = dkx.shape[0]
    nt = s // tm

    def body(dk_ref, z_ref, sel_ref, dfl_ref, dbf_ref, carry):
        @pl.when(pl.program_id(0) == 0)
        def _():
            carry[...] = jnp.zeros_like(carry)
            dbf_ref[...] = jnp.zeros_like(dbf_ref)

        dc = _split_dot(dk_ref[...], sel_ref[...])
        row = lax.broadcasted_iota(jnp.int32, (tm, tm), 0)
        col = lax.broadcasted_iota(jnp.int32, (tm, tm), 1)
        tri = (col >= row).astype(BF16)
        dlogf = _exact_dot01(tri, dc) + carry[0:1, :]
        carry[...] = jnp.broadcast_to(dlogf[0:1, :], carry.shape)
        dz = dlogf * (1.0 - jax.nn.sigmoid(z_ref[...]))
        dfl_ref[:, 0:128] = dz.astype(BF16)
        dfl_ref[:, 128:GW_TILE] = jnp.zeros((tm, GW_TILE - 128), BF16)
        dbf_ref[...] += _fold8(dz)

    rev = lambda i: (nt - 1 - i, 0)
    return pl.pallas_call(
        body, name="forget_bwd", grid=(nt,),
        in_specs=[pl.BlockSpec((tm, AW), rev), pl.BlockSpec((tm, 128), rev), _full((AW, 128))],
        out_specs=[pl.BlockSpec((tm, GW_TILE), rev), _full((SUBLANES, 128))],
        out_shape=[jax.ShapeDtypeStruct((s, GW_TILE), BF16), jax.ShapeDtypeStruct((SUBLANES, 128), F32)],
        scratch_shapes=[pltpu.VMEM((SUBLANES, 128), F32)],
        compiler_params=_cparams(62, ("arbitrary",)),
    )(dkx, z, sel)


def _in_proj_bwd(pieces, wp, x, g1, dx2, after, *, tm):
    s = x.shape[0]

    def body(q_ref, k_ref, v_ref, bcu_ref, f_ref, w_ref, x_ref, g_ref, dx2_ref, after_ref, dx_ref, dg_ref):
        @pl.when(pl.program_id(0) == 0)
        def _():
            dg_ref[...] = jnp.zeros_like(dg_ref)

        dh = None
        for ref, (lo, hi) in zip((q_ref, k_ref, v_ref, bcu_ref, f_ref), PIECES):
            part = lax.dot_general(ref[...], w_ref[:, lo:hi], NT, preferred_element_type=F32)
            dh = part if dh is None else dh + part
        _, n, r = _rms_fwd(x_ref[...], g_ref[...])
        dxn, dg = _rms_bwd(dh, n, r, g_ref[...])
        dx_ref[...] = dx2_ref[...] + dxn
        dg_ref[...] += _fold8(dg)

    return pl.pallas_call(
        body, name="in_proj_bwd", grid=(s // tm,),
        in_specs=[_rows(tm, hi - lo) for lo, hi in PIECES]
        + [_resident((D, WP)), _rows(tm, D), _full((1, D)), _rows(tm, D), ANY],
        out_specs=[_rows(tm, D), _full((SUBLANES, D))],
        out_shape=[jax.ShapeDtypeStruct((s, D), F32), jax.ShapeDtypeStruct((SUBLANES, D), F32)],
        compiler_params=_cparams(56, ("arbitrary",)),
    )(*pieces, wp, x, g1, dx2, after)


def _position():
    return lax.axis_index("x"), lax.axis_index("y"), lax.axis_index("c")


ANY = pl.BlockSpec(memory_space=pl.ANY)


def _all_gather(shards):
    n = len(shards)

    def body(*refs):
        x_refs, out_refs = refs[:n], refs[n:2 * n]
        send_sems, recv_sems, local_sems = refs[2 * n:]
        x, y, c = _position()
        me, sibling = (x, y, c), (x, y, 1 - c)
        chips = [(1 - x, y), (x, 1 - y), (1 - x, 1 - y)]

        def copy(a, k, block, to, own=False):
            slot = out_refs[a].at[4 * block[0] + 2 * block[1] + block[2]]
            return pltpu.make_async_remote_copy(
                src_ref=x_refs[a] if own else slot, dst_ref=slot,
                send_sem=send_sems.at[7 * a + k], recv_sem=recv_sems.at[7 * a + k], device_id=to, device_id_type=MESH_ID)

        mine = [pltpu.make_async_copy(x_refs[a], out_refs[a].at[4 * x + 2 * y + c], local_sems.at[a]) for a in range(n)]
        for cp in mine:
            cp.start()
        first = []
        for a in range(n):
            first.append(copy(a, 0, me, sibling, own=True))
            first += [copy(a, 1 + j, me, (*chip, c), own=True) for j, chip in enumerate(chips)]
        for cp in first:
            cp.start()
        passed = []
        for j, chip in enumerate(chips):
            for a in range(n):
                copy(a, 1 + j, (*chip, c), me).wait_recv()
                fwd = copy(a, 4 + j, (*chip, c), sibling)
                fwd.start()
                passed.append(fwd)
        for a in range(n):
            copy(a, 0, sibling, me).wait_recv()
            for j, chip in enumerate(chips):
                copy(a, 4 + j, (*chip, 1 - c), me).wait_recv()
        for cp in first + passed:
            cp.wait_send()
        for cp in mine:
            cp.wait()

    return pl.pallas_call(
        body, name="all_gather_weights",
        out_shape=[jax.ShapeDtypeStruct((NDEV,) + sh.shape, sh.dtype) for sh in shards],
        in_specs=[ANY] * n, out_specs=[ANY] * n,
        scratch_shapes=[pltpu.SemaphoreType.DMA((7 * n,)), pltpu.SemaphoreType.DMA((7 * n,)), pltpu.SemaphoreType.DMA((n,))],
    )(*shards)


def _pair_exchange(grads):
    n = len(grads)

    def body(*refs):
        g_refs, out_refs = refs[:n], refs[n:2 * n]
        send_sems, recv_sems = refs[2 * n:]
        x, y, c = _position()
        copies = [pltpu.make_async_remote_copy(
            src_ref=g_refs[a].at[:, pl.ds(1 - c, 1)], dst_ref=out_refs[a], send_sem=send_sems.at[a],
            recv_sem=recv_sems.at[a], device_id=(x, y, 1 - c), device_id_type=MESH_ID) for a in range(n)]
        for cp in copies:
            cp.start()
        for cp in copies:
            cp.wait()

    return pl.pallas_call(
        body, name="grad_pair_exchange",
        out_shape=[jax.ShapeDtypeStruct((4, 1) + g.shape[2:], g.dtype) for g in grads],
        in_specs=[ANY] * n, out_specs=[ANY] * n,
        scratch_shapes=[pltpu.SemaphoreType.DMA((n,)), pltpu.SemaphoreType.DMA((n,))],
    )(*grads)


def _pair_sum(g, got, idx, *, tr, name):
    r, c = g.shape[2:]

    def body(idx_ref, g_ref, got_ref, pb_ref, own_ref):
        p = g_ref[0, 0].astype(F32) + got_ref[0, 0].astype(F32)
        pb_ref[0] = p.astype(BF16)

        @pl.when(pl.program_id(1) == idx_ref[1])
        def _():
            own_ref[...] = p

    return pl.pallas_call(
        body, name=name,
        grid_spec=pltpu.PrefetchScalarGridSpec(
            num_scalar_prefetch=1, grid=(r // tr, 4),
            in_specs=[pl.BlockSpec((1, 1, tr, c), lambda i, j, idx: (j, idx[0], i, 0)),
                      pl.BlockSpec((1, 1, tr, c), lambda i, j, idx: (j, 0, i, 0))],
            out_specs=[pl.BlockSpec((1, tr, c), lambda i, j, idx: (j, i, 0)),
                       pl.BlockSpec((tr, c), lambda i, j, idx: (i, 0))]),
        out_shape=[jax.ShapeDtypeStruct((4, r, c), BF16), jax.ShapeDtypeStruct((r, c), F32)],
        compiler_params=_cparams(62, ("arbitrary", "arbitrary")),
    )(idx, g, got)


HBM = pl.BlockSpec(memory_space=pltpu.HBM)
SEM = pl.BlockSpec(memory_space=pltpu.SEMAPHORE)
DATAFLOW = pltpu.SideEffectType.DATAFLOW_SIDE_EFFECTING


PEERS = {"gather": NDEV - 1, "scatter": NDEV - 1, "chips": 3}


def _exchange_copies(src_refs, land_refs, send_sems, recv_sems, mode):
    x, y, c = _position()
    me, my_chip = 4 * x + 2 * y + c, 2 * x + y
    npeers = PEERS[mode]
    copies, own = [], []
    for a, (s_ref, l_ref) in enumerate(zip(src_refs, land_refs)):
        for k in range(npeers):
            if mode == "chips":
                px, py, pc = x ^ ((k + 1) >> 1), y ^ ((k + 1) & 1), c
                src, dst = s_ref.at[2 * px + py], l_ref.at[my_chip]
            else:
                px, py, pc = x ^ ((k + 1) >> 2), y ^ (((k + 1) >> 1) & 1), c ^ ((k + 1) & 1)
                src, dst = (s_ref.at[4 * px + 2 * py + pc] if mode == "scatter" else s_ref), l_ref.at[me]
            copies.append(pltpu.make_async_remote_copy(
                src_ref=src, dst_ref=dst, send_sem=send_sems.at[npeers * a + k], recv_sem=recv_sems.at[npeers * a + k],
                device_id=(px, py, pc), device_id_type=MESH_ID))
        slot = my_chip if mode == "chips" else me
        own.append(pltpu.make_async_copy(s_ref if mode == "gather" else s_ref.at[slot], l_ref.at[slot],
                                         send_sems.at[npeers * len(src_refs) + a]))
    return copies, own


def _exchange_start(srcs, lands, after, *, mode, name):
    n = len(srcs)
    nsem = PEERS[mode] * n

    def body(*refs):
        token = refs[-1]
        copies, own = _exchange_copies(refs[:n], refs[n:2 * n], refs[2 * n + 1], refs[2 * n + 2], mode)
        for cp in copies + own:
            cp.start()
        token[...] = jnp.zeros_like(token)

    arrays = list(srcs) + list(lands)
    outs = pl.pallas_call(
        body, name=name,
        out_shape=(pltpu.SemaphoreType.DMA((nsem + n,)), pltpu.SemaphoreType.DMA((nsem,)),
                   *[pltpu.HBM(a.shape, a.dtype) for a in arrays], jax.ShapeDtypeStruct((SUBLANES, LANES), F32)),
        in_specs=[HBM] * (2 * n) + [ANY],
        out_specs=(SEM, SEM, *[HBM] * (2 * n), pl.BlockSpec(memory_space=pltpu.VMEM)),
        input_output_aliases={i: 2 + i for i in range(2 * n)},
        compiler_params=pltpu.CompilerParams(has_side_effects=DATAFLOW),
    )(*[pltpu.with_memory_space_constraint(a, pltpu.HBM) for a in arrays], after)
    return outs[0], outs[1], outs[2:2 + n], outs[2 + n:2 + 2 * n], outs[-1]


def _exchange_wait(send_sems, recv_sems, srcs, lands, after, *, mode, name):
    n = len(srcs)

    def body(*refs):
        copies, own = _exchange_copies(refs[:n], refs[n:2 * n], refs[2 * n], refs[2 * n + 1], mode)
        for cp in copies:
            cp.wait_send()
            cp.wait_recv()
        for cp in own:
            cp.wait()

    arrays = list(srcs) + list(lands)
    outs = pl.pallas_call(
        body, name=name,
        out_shape=tuple(pltpu.HBM(a.shape, a.dtype) for a in arrays),
        in_specs=[HBM] * (2 * n) + [SEM, SEM, ANY],
        out_specs=tuple([HBM] * (2 * n)),
        input_output_aliases={i: i for i in range(2 * n)},
        compiler_params=pltpu.CompilerParams(has_side_effects=DATAFLOW),
    )(*arrays, send_sems, recv_sems, after)
    return outs[n:]


def _small_all_reduce(parts):
    def body(gmp_ref, gmo_ref, gfp_ref, gfo_ref, ga_ref, gc_ref, dw_ref, bf_ref, loss_ref,
             out_ref, buf, send_sems, recv_sems):
        x, y, c = _position()
        me = 4 * x + 2 * y + c

        def colsum(v):
            return jnp.sum(v, axis=0, keepdims=True)

        loss = jnp.sum(colsum(loss_ref[...]), axis=1, keepdims=True) * (0.5 / D)
        rows = [colsum(gmp_ref[...]), colsum(gmo_ref[...]), colsum(gfp_ref[...]), colsum(gfo_ref[...]),
                jnp.concatenate([colsum(ga_ref[...]), colsum(gc_ref[...])], axis=1),
                jnp.concatenate([colsum(dw_ref[0]), colsum(dw_ref[1])], axis=1),
                jnp.concatenate([colsum(dw_ref[2]), colsum(bf_ref[...]), jnp.broadcast_to(loss, (1, 128)),
                                 jnp.zeros((1, 256), F32)], axis=1),
                jnp.zeros((1, D), F32)]
        buf[me] = jnp.concatenate(rows, axis=0)
        copies = []
        for mm in range(1, NDEV):
            peer = (x ^ (mm >> 2), y ^ ((mm >> 1) & 1), c ^ (mm & 1))
            copies.append(pltpu.make_async_remote_copy(
                src_ref=buf.at[me], dst_ref=buf.at[me], send_sem=send_sems.at[mm - 1], recv_sem=recv_sems.at[mm - 1],
                device_id=peer, device_id_type=MESH_ID))
        for cp in copies:
            cp.start()
        for cp in copies:
            cp.wait_recv()
        for cp in copies:
            cp.wait_send()
        acc = buf[0]
        for d in range(1, NDEV):
            acc = acc + buf[d]
        out_ref[...] = acc

    vm = pl.BlockSpec(memory_space=pltpu.VMEM)
    return pl.pallas_call(
        body, name="small_all_reduce",
        out_shape=jax.ShapeDtypeStruct((SUBLANES, D), F32),
        in_specs=[vm] * len(parts), out_specs=vm,
        scratch_shapes=[pltpu.VMEM((NDEV, SUBLANES, D), F32), pltpu.SemaphoreType.DMA((7,)), pltpu.SemaphoreType.DMA((7,))],
    )(*parts)


def _adam_update(w, g, m, v):
    nm = ADAM_B1 * m + (1.0 - ADAM_B1) * g
    nv = ADAM_B2 * v + (1.0 - ADAM_B2) * (g * g)
    m_hat = nm / (1.0 - ADAM_B1 ** ADAM_STEP)
    v_hat = nv / (1.0 - ADAM_B2 ** ADAM_STEP)
    return -ADAM_LR * (m_hat / (jnp.sqrt(v_hat) + ADAM_EPS) + ADAM_WD * w), nm, nv


SMALL_SLOTS = {"g_mix_pre": (0, 0, D), "g_mix_post": (1, 0, D), "g_ffn_pre": (2, 0, D), "g_ffn_post": (3, 0, D),
               "g_attn_out": (4, 0, AW), "g_conv_out": (4, AW, CW), "b_forget": (6, CW, H)}
LOSS_LANE = CW + 128


def _small_adamw(small, conv_grad, params):
    names = list(params)
    n = len(names)

    def body(*refs):
        small_ref, cg_ref = refs[0], refs[1]
        ins, outs = refs[2:2 + 3 * n], refs[2 + 3 * n:]
        for i, name in enumerate(names):
            w_ref, m_ref, v_ref = ins[3 * i:3 * i + 3]
            g_ref, d_ref, nm_ref, nv_ref = outs[4 * i:4 * i + 4]
            if name == "conv_w":
                g = cg_ref[...]
            else:
                r, c0, width = SMALL_SLOTS[name]
                g = small_ref[r:r + 1, c0:c0 + width]
            g_ref[...] = g
            d_ref[...], nm_ref[...], nv_ref[...] = _adam_update(w_ref[...], g, m_ref[...], v_ref[...])
        outs[4 * n][...] = small_ref[6:7, LOSS_LANE:LOSS_LANE + 1]

    vm = pl.BlockSpec(memory_space=pltpu.VMEM)
    flat = [a for name in names for a in params[name]]
    outs = pl.pallas_call(
        body, name="adamw_small",
        in_specs=[vm] * (2 + 3 * n), out_specs=[vm] * (4 * n + 1),
        out_shape=[jax.ShapeDtypeStruct(params[name][0].shape, F32) for name in names for _ in range(4)]
        + [jax.ShapeDtypeStruct((1, 1), F32)],
    )(small, conv_grad, *flat)
    return {name: outs[4 * i:4 * i + 4] for i, name in enumerate(names)}, outs[4 * n].reshape(())


def _chip_sum_adamw(got, own, idx, wt, mt, vt, *, tr, name):
    cols, rows = wt.shape
    gcols = own.shape[1]

    def body(idx_ref, got_ref, own_ref, w_ref, m_ref, v_ref, g_ref, d_ref, nm_ref, nv_ref):
        g = jnp.zeros((tr, gcols), F32)
        for j in range(4):
            g = g + jnp.where(idx_ref[1] == j, own_ref[...], got_ref[j].astype(F32))
        g = g.T[:cols]
        g_ref[...] = g
        d_ref[...], nm_ref[...], nv_ref[...] = _adam_update(w_ref[...], g, m_ref[...], v_ref[...])

    spec = pl.BlockSpec((cols, tr), lambda i, idx: (0, i))
    gspec = pl.BlockSpec((tr, gcols), lambda i, idx: (i, 0))
    return pl.pallas_call(
        body, name=name,
        grid_spec=pltpu.PrefetchScalarGridSpec(
            num_scalar_prefetch=1, grid=(rows // tr,),
            in_specs=[pl.BlockSpec((4, tr, gcols), lambda i, idx: (0, i, 0)), gspec, spec, spec, spec],
            out_specs=[spec] * 4),
        out_shape=[jax.ShapeDtypeStruct((cols, rows), F32)] * 4,
        compiler_params=_cparams(32, ("arbitrary",)),
    )(idx, got, own, wt, mt, vt)


def _device_sum_adamw(land, w, m, v, *, tr, name, vmem_mb=32):
    rows, cols = w.shape

    def body(land_ref, w_ref, m_ref, v_ref, g_ref, d_ref, nm_ref, nv_ref):
        g = land_ref[0].astype(F32)
        for dev in range(1, NDEV):
            g = g + land_ref[dev].astype(F32)
        g_ref[...] = g
        d_ref[...], nm_ref[...], nv_ref[...] = _adam_update(w_ref[...], g, m_ref[...], v_ref[...])

    spec = pl.BlockSpec((tr, cols), lambda i: (i, 0))
    return pl.pallas_call(
        body, name=name, grid=(rows // tr,),
        in_specs=[pl.BlockSpec((NDEV, tr, cols), lambda i: (0, i, 0)), spec, spec, spec],
        out_specs=[spec] * 4,
        out_shape=[jax.ShapeDtypeStruct((rows, cols), F32)] * 4,
        compiler_params=_cparams(vmem_mb, ("arbitrary",)),
    )(land, w, m, v)


def _placement_constants():
    j = np.arange(128)[:, None]
    lane = np.arange(1024)[None, :]
    head, sub = lane // HP, lane % HP
    piece, jh = j // H, j % H
    valid = (j < 3 * H) & (jh == head)
    pq = np.where(valid & (sub == DH + piece), 1.0, 0.0).astype(BF16)
    pk = np.where(valid & (sub == DH + 3 + piece), -1.0, 0.0).astype(BF16)
    oq = np.where((sub >= DH + 3) & (sub < DH + 6), 1.0, 0.0).astype(np.float32)
    ok = np.where((sub >= DH) & (sub < DH + 3), 1.0, 0.0).astype(np.float32)
    r = np.arange(AW)[:, None]
    cc = np.arange(128)[None, :]
    sel = np.where((r % DH == 3) & (r // DH == cc), -1.0, 0.0).astype(BF16)
    gi = np.arange(GS)
    gsum = (gi[:, None] // DH == gi[None, :] // DH).astype(BF16)
    return tuple(jnp.asarray(c) for c in (pq, pk, oq, ok, sel, gsum))


def _local_step(xs, tgt, wp, late_weights, cw8, bfp, g_attn_out, g_conv_out,
                g_mix_pre, g_mix_post, g_ffn_pre, g_ffn_post, early_grads=None, last_grad=None):
    pq, pk, oq, ok, sel, gsum = _placement_constants()
    h1t, qp, kp, vv, bcu, zf = _in_proj(xs, g_mix_pre, wp, bfp, pq, pk, oq, ok, tm=512)
    o, lse, mk = _attn_fwd(qp, kp, vv, t=512)
    w_out_f, wgu, wd = late_weights(lse)
    merged, y, x2, cv, h2 = _mix_out(o, bcu, cw8, g_attn_out, g_conv_out, gsum, w_out_f, xs, g_mix_post, g_ffn_pre, tm=512)
    gate, up, act, dx3, dff, loss_p, dg_ffn_post = _ffn_fwd_loss(h2, wgu, wd, x2, tgt, g_ffn_post, tm=512)

    dgu, dx2, dy, dg_ffn_pre, dg_mix_post = _ffn_bwd(dff, wd, gate, up, wgu, x2, g_ffn_pre, dx3, y, g_mix_post, tm=256)
    dw_down = _grad_matmul(act, dff, ta=DFF // 2, tb=D, ts=4096, name="grad_w_down", vmem_mb=60)
    dw_gu = _grad_matmul(dgu, h2, ta=DFF // 2, tb=D, ts=4096, name="grad_w_gate_up", vmem_mb=60).reshape(NDEV, FB, D)
    dw_out = _grad_matmul(merged, dy, ta=1024, tb=1024, ts=2048, name="grad_w_out")
    token = early_grads(dw_out, dw_gu, dw_down) if early_grads is not None else dw_out
    do, dl, dcv, db, dg_attn, dg_conv = _mix_bwd(dy, w_out_f, o, cv, bcu, g_attn_out, g_conv_out, gsum, token, tm=512)
    dbcu, dtaps = _conv_bwd(dcv, db, bcu, cw8, tm=512)
    dqp, dkp, dv, dkx = _attn_bwd(qp, kp, vv, do, lse, dl, mk, t=512)
    dfl, dbf = _forget_bwd(dkx, zf, sel, tm=512)
    pieces = (dqp, dkp, dv, dbcu, dfl)
    dwp = _grad_w_in(h1t, pieces)
    token = last_grad(dwp) if last_grad is not None else dwp
    grad_x, dg_mix_pre = _in_proj_bwd(pieces, wp, xs, g_mix_pre, dx2, token, tm=512)
    return (grad_x, dwp, dw_out, dw_gu, dw_down, dg_mix_pre, dg_mix_post, dg_ffn_pre, dg_ffn_post, dg_attn, dg_conv,
            dtaps, dbf, loss_p)


BIG_TILES = {"w_in": 256, "w_out": 128, "w_gate_up": 176, "w_down": 176}


def kernel(x, w_in, b_forget, conv_w, g_attn_out, g_conv_out, w_out, g_mix_pre, g_mix_post, w_gate_up, w_down, g_ffn_pre, g_ffn_post, loss_target, m_w_in, m_b_forget, m_conv_w, m_g_attn_out, m_g_conv_out, m_w_out, m_g_mix_pre, m_g_mix_post, m_w_gate_up, m_w_down, m_g_ffn_pre, m_g_ffn_post, v_w_in, v_b_forget, v_conv_w, v_g_attn_out, v_g_conv_out, v_w_out, v_g_mix_pre, v_g_mix_post, v_w_gate_up, v_w_down, v_g_ffn_pre, v_g_ffn_post):
    xc, yc, cc = _position()
    my_chip = 2 * xc + yc
    me = 2 * my_chip + cc
    idx = jnp.stack([cc, my_chip]).astype(jnp.int32)
    tables = _in_layout_tables()

    w_in_b = w_in[0].astype(BF16)
    g_in, g_last, g_taps = _all_gather([w_in_b[:, :IN_MAIN], w_in_b[:, IN_MAIN].reshape(SUBLANES, LANES), conv_w[0]])
    last_cols = jnp.pad(g_last.reshape(NDEV, D).T.astype(F32), ((0, 0), (0, LANES - NDEV)))
    wp = _assemble_w_in(g_in, last_cols, tables, tr=256)
    cw8 = jnp.pad(g_taps.transpose(1, 0, 2).reshape(3, CW), ((0, SUBLANES - 3), (0, 0)))

    late = [w_out[0].astype(BF16), w_gate_up[0].T.astype(BF16), w_down[0].astype(BF16)]
    ssem, rsem, late_thru, land_thru, token = _exchange_start(
        late, [lax.empty((NDEV,) + s.shape, s.dtype) for s in late], g_in, mode="gather",
        name="gather_late_start")
    bfp = jnp.pad(b_forget, ((0, 0), (0, 128 - H))) + token[0:1, :]

    def late_weights(after):
        l_out, l_gu, l_down = _exchange_wait(ssem, rsem, late_thru, land_thru, after, mode="gather", name="gather_late_wait")
        return l_out.reshape(D, D), l_gu.reshape(2, DFF, D), l_down.reshape(DFF, D)

    early = {}

    def early_grads(dw_out, dw_gu, dw_down):
        srcs = [dw_out.reshape(NDEV, D // NDEV, D), dw_gu, dw_down.reshape(NDEV, DFF // NDEV, D)]
        lands = [lax.empty(s.shape, s.dtype) for s in srcs]
        early["handles"] = _exchange_start(srcs, lands, dw_out, mode="scatter", name="scatter_early_start")
        return early["handles"][4]

    last = {}

    def last_grad(dwp):
        g_w_in = _disassemble_w_in(dwp, tables, tr=256).reshape(4, 2, D, IN_PAD)
        (from_sibling,) = _pair_exchange([g_w_in])
        pair_b, last["own"] = _pair_sum(g_w_in, from_sibling, idx, tr=D, name="grad_pair_sum_w_in")
        last["handles"] = _exchange_start([pair_b], [lax.empty(pair_b.shape, pair_b.dtype)], last["own"], mode="chips",
                                          name="chips_w_in_start")
        return last["handles"][4]

    (grad_x, dwp, dw_out, dw_gu, dw_down, dg_mix_pre, dg_mix_post, dg_ffn_pre, dg_ffn_post, dg_attn, dg_conv,
     dtaps, dbf, loss_p) = _local_step(x[0], loss_target[0], wp, late_weights, cw8, bfp, g_attn_out, g_conv_out,
                                        g_mix_pre, g_mix_post, g_ffn_pre, g_ffn_post, early_grads, last_grad)

    e_ssem, e_rsem, e_srcs, e_lands, _ = early["handles"]
    land_out, land_gu, land_down = _exchange_wait(e_ssem, e_rsem, e_srcs, e_lands, dg_mix_pre, mode="scatter",
                                                  name="scatter_early_wait")
    res = {}
    big = {"w_out": (land_out, w_out[0], m_w_out[0], v_w_out[0]),
           "w_gate_up": (land_gu, w_gate_up[0].T, m_w_gate_up[0].T, v_w_gate_up[0].T),
           "w_down": (land_down, w_down[0], m_w_down[0], v_w_down[0])}
    for name, (land, w, m, v) in big.items():
        outs = _device_sum_adamw(land, w, m, v, tr=BIG_TILES[name], name="adamw_" + name,
                                 vmem_mb=62 if name == "w_gate_up" else 32)
        res[name] = [(o.T if name == "w_gate_up" else o)[None] for o in outs]
    c_ssem, c_rsem, c_srcs, c_lands, _ = last["handles"]
    after = sum(res[n][1][0, :SUBLANES, :LANES] for n in big)
    (from_chips,) = _exchange_wait(c_ssem, c_rsem, c_srcs, c_lands, after, mode="chips", name="chips_w_in_wait")
    outs = _chip_sum_adamw(from_chips, last["own"], idx, w_in[0].T, m_w_in[0].T, v_w_in[0].T,
                           tr=BIG_TILES["w_in"], name="adamw_w_in")
    res["w_in"] = [o.T[None] for o in outs]

    small = _small_all_reduce([dg_mix_pre, dg_mix_post, dg_ffn_pre, dg_ffn_post, dg_attn, dg_conv, dtaps, dbf, loss_p])
    taps_full = jnp.concatenate([small[5:6, :CW], small[5:6, CW:], small[6:7, :CW]], axis=0)
    taps_first = lambda a: a.transpose(1, 0, 2)
    smalls = {"b_forget": (b_forget, m_b_forget, v_b_forget),
              "conv_w": (taps_first(conv_w), taps_first(m_conv_w), taps_first(v_conv_w)),
              "g_attn_out": (g_attn_out, m_g_attn_out, v_g_attn_out), "g_conv_out": (g_conv_out, m_g_conv_out, v_g_conv_out),
              "g_mix_pre": (g_mix_pre, m_g_mix_pre, v_g_mix_pre), "g_mix_post": (g_mix_post, m_g_mix_post, v_g_mix_post),
              "g_ffn_pre": (g_ffn_pre, m_g_ffn_pre, v_g_ffn_pre), "g_ffn_post": (g_ffn_post, m_g_ffn_post, v_g_ffn_post)}
    own_taps = lax.dynamic_slice(taps_full, (0, me * 64), (3, 64))[:, None, :]
    small_res, loss = _small_adamw(small, own_taps, smalls)
    for name, outs in small_res.items():
        res[name] = [taps_first(o) for o in outs] if name == "conv_w" else list(outs)

    order = ["w_in", "b_forget", "conv_w", "g_attn_out", "g_conv_out", "w_out", "g_mix_pre", "g_mix_post",
             "w_gate_up", "w_down", "g_ffn_pre", "g_ffn_post"]
    outs = [loss, grad_x[None]]
    for k in range(4):
        outs += [res[n][k] for n in order]
    return tuple(outs)
```

```python
import functools

import numpy as np

import jax
import jax.numpy as jnp
from jax import lax
from jax.experimental import pallas as pl
from jax.experimental.pallas import tpu as pltpu

F32 = jnp.float32
BF16 = jnp.bfloat16
MESH_ID = pl.DeviceIdType.MESH

D = 1024
H = 8
DH = 64
AW = 512
CW = 512
DFF = 2816
FB = DFF // 4
FF_CHUNKS = ((0, 768), (768, 768), (1536, 768), (2304, 512))
FF_CHUNKS_BWD = ((0, 1024), (1024, 1024), (2048, 768))
HP = 128
OFF_Q, OFF_K, OFF_V, OFF_BCU, OFF_F = 0, 512, 1024, 1536, 3072
WP = OFF_F + 128
PIECES = ((OFF_Q, OFF_K), (OFF_K, OFF_V), (OFF_V, OFF_BCU), (OFF_BCU, OFF_F), (OFF_F, WP))
EPS = 1e-6
LOG2E, LN2 = 1.4426950408889634, 0.6931471805599453
NDEV = 8
LANES = 128
SUBLANES = 8
IN_COLS = 385
IN_PAD = 512
IN_MAIN = 384
WIN = 640
ADAM_LR, ADAM_B1, ADAM_B2, ADAM_EPS, ADAM_WD, ADAM_STEP = 0.001, 0.9, 0.999, 1e-08, 0.01, 10

NT = (((1,), (1,)), ((), ()))
TN = (((0,), (0,)), ((), ()))


def _cparams(vmem_mb=None, sem=None):
    kw = {}
    if vmem_mb is not None:
        kw["vmem_limit_bytes"] = vmem_mb << 20
    if sem is not None:
        kw["dimension_semantics"] = sem
    return pltpu.CompilerParams(**kw)


def _full(shape):
    return pl.BlockSpec(shape, lambda *_: (0,) * len(shape))


def _resident(shape):
    return pl.BlockSpec(shape, lambda *_: (0,) * len(shape), pipeline_mode=pl.Buffered(1))


def _rows(tm, width):
    return pl.BlockSpec((tm, width), lambda i: (i, 0))


def _fold8(v):
    r, w = v.shape
    return jnp.sum(v.reshape(r // SUBLANES, SUBLANES, w), axis=0)


def _split_dot(v, m01):
    hi = v.astype(BF16)
    lo = (v - hi.astype(F32)).astype(BF16)
    return (jnp.dot(hi, m01, preferred_element_type=F32)
            + jnp.dot(lo, m01, preferred_element_type=F32))


GS = 256


def _group_sum(v, g01):
    parts = [_split_dot(v[:, c:c + GS], g01) for c in range(0, v.shape[1], GS)]
    return parts[0] if len(parts) == 1 else jnp.concatenate(parts, axis=1)


def _exact_dot01(m01, v):
    p1 = v.astype(BF16)
    r1 = v - p1.astype(F32)
    p2 = r1.astype(BF16)
    p3 = (r1 - p2.astype(F32)).astype(BF16)
    return (jnp.dot(m01, p1, preferred_element_type=F32) + jnp.dot(m01, p2, preferred_element_type=F32)
            + jnp.dot(m01, p3, preferred_element_type=F32))


def _rms_fwd(v, g):
    r = lax.rsqrt(jnp.mean(v * v, axis=-1, keepdims=True) + EPS)
    n = v * r
    return n * g, n, r


def _rms_bwd(do, n, r, g):
    dn = do * g
    return r * (dn - n * jnp.mean(dn * n, axis=-1, keepdims=True)), do * n


def _padded_column(n):
    if n < AW:
        return OFF_Q + n, 0.125
    if n < 3 * AW:
        return n, 1.0
    if n < 3 * AW + H:
        return OFF_F + n - 3 * AW, 1.0
    return OFF_BCU + n - 3 * AW - H, 1.0


def _in_layout_tables():
    dest = -np.ones((IN_PAD, LANES), np.int32)
    dest_f = -np.ones((IN_PAD, LANES), np.int32)
    scale = np.zeros((IN_PAD, LANES), np.float32)
    starts = []
    for k in range(NDEV):
        cols = [_padded_column(IN_COLS * k + j) for j in range(IN_COLS)]
        main = [c for c, _ in cols if c < OFF_F]
        ws = min((min(main) // LANES) * LANES, OFF_F - WIN)
        assert ws <= min(main) and max(main) < ws + WIN
        starts.append(ws)
        for j, (c, sc) in enumerate(cols):
            scale[j, k] = sc
            if c < OFF_F:
                dest[j, k] = c - ws
            else:
                dest_f[j, k] = c - OFF_F
    f_shards = tuple(k for k in range(NDEV) if (dest_f[:, k] >= 0).any())
    return tuple(starts), f_shards, jnp.asarray(dest), jnp.asarray(dest_f), jnp.asarray(scale)


def _perm(dest_ref, scale_ref, k, width, rows=IN_PAD):
    lane = lax.broadcasted_iota(jnp.int32, (rows, width), 1)
    return jnp.where(dest_ref[0:rows, k:k + 1] == lane, scale_ref[0:rows, k:k + 1], 0.0).astype(BF16)


def _assemble_w_in(blocks, last_cols, tables, *, tr):
    starts, f_shards, dest, dest_f, scale = tables
    last = [_padded_column(IN_COLS * k + IN_MAIN) for k in range(NDEV)]
    f_main = [any(_padded_column(IN_COLS * k + j)[0] >= OFF_F for j in range(IN_MAIN)) for k in range(NDEV)]
    assert IN_COLS == IN_MAIN + 1

    def body(b_ref, c_ref, dest_ref, destf_ref, scale_ref, o_ref):
        o_ref[...] = jnp.zeros_like(o_ref)
        lane = lax.broadcasted_iota(jnp.int32, (tr, LANES), 1)
        for k in range(NDEV):
            b = b_ref[k]
            ws = starts[k]
            part = jnp.dot(b, _perm(dest_ref, scale_ref, k, WIN, IN_MAIN), preferred_element_type=F32)
            o_ref[:, ws:ws + WIN] += part.astype(BF16)
            if f_main[k]:
                part = jnp.dot(b, _perm(destf_ref, scale_ref, k, 128, IN_MAIN), preferred_element_type=F32)
                o_ref[:, OFF_F:WP] += part.astype(BF16)
            col, sc = last[k]
            tile = (col // LANES) * LANES
            o_ref[:, tile:tile + LANES] += jnp.where(lane == col - tile, c_ref[:, k:k + 1] * sc, 0.0).astype(BF16)

    tab = _full((IN_PAD, LANES))
    return pl.pallas_call(
        body, name="assemble_w_in", grid=(D // tr,),
        in_specs=[pl.BlockSpec((NDEV, tr, IN_MAIN), lambda i: (0, i, 0)), _rows(tr, LANES), tab, tab, tab],
        out_specs=_rows(tr, WP),
        out_shape=jax.ShapeDtypeStruct((D, WP), BF16),
        compiler_params=_cparams(48, ("arbitrary",)),
    )(blocks, last_cols, dest, dest_f, scale)


def _disassemble_w_in(dwp, tables, *, tr):
    starts, f_shards, dest, dest_f, scale = tables
    width = dwp.shape[1]

    def body(g_ref, dest_ref, destf_ref, scale_ref, o_ref):
        for k in range(NDEV):
            ws = starts[k]
            acc = lax.dot_general(g_ref[:, ws:ws + WIN], _perm(dest_ref, scale_ref, k, WIN), NT, preferred_element_type=F32)
            if k in f_shards:
                acc = acc + lax.dot_general(g_ref[:, OFF_F:WP], _perm(destf_ref, scale_ref, k, 128), NT,
                                            preferred_element_type=F32)
            o_ref[k] = acc.astype(BF16)

    tab = _full((IN_PAD, LANES))
    return pl.pallas_call(
        body, name="disassemble_w_in", grid=(D // tr,),
        in_specs=[_rows(tr, width), tab, tab, tab],
        out_specs=pl.BlockSpec((NDEV, tr, IN_PAD), lambda i: (0, i, 0)),
        out_shape=jax.ShapeDtypeStruct((NDEV, D, IN_PAD), BF16),
        compiler_params=_cparams(48, ("arbitrary",)),
    )(dwp, dest, dest_f, scale)


def _in_proj(x, g1, wp, bfp, pq, pk, oq, ok, *, tm):
    s = x.shape[0]

    def body(x_ref, g_ref, w_ref, bf_ref, pq_ref, pk_ref, oq_ref, ok_ref,
             ht_ref, qp_ref, kp_ref, v_ref, bcu_ref, z_ref, carry):
        @pl.when(pl.program_id(0) == 0)
        def _():
            carry[...] = jnp.zeros_like(carry)

        h = _rms_fwd(x_ref[...], g_ref[...])[0].astype(BF16)
        ht_ref[...] = h.T
        z = jnp.dot(h, w_ref[:, OFF_F:WP], preferred_element_type=F32) + bf_ref[...]
        z_ref[...] = z
        lane = lax.broadcasted_iota(jnp.int32, (tm, 128), 1)
        logf = jnp.where(lane < H, jnp.minimum(z, 0.0) - jnp.log(1.0 + jnp.exp(-jnp.abs(z))), 0.0)
        row = lax.broadcasted_iota(jnp.int32, (tm, tm), 0)
        col = lax.broadcasted_iota(jnp.int32, (tm, tm), 1)
        tri = (col <= row).astype(BF16)
        c = _exact_dot01(tri, logf) + carry[0:1, :]
        carry[...] = jnp.broadcast_to(c[tm - 1:tm, :], carry.shape)
        cb = c * LOG2E
        c1 = cb.astype(BF16).astype(F32)
        r1 = cb - c1
        c2 = r1.astype(BF16).astype(F32)
        c3 = (r1 - c2).astype(BF16).astype(F32)
        zc = (c1 + pltpu.roll(c2, 8, axis=1) + pltpu.roll(c3, 16, axis=1)).astype(BF16)

        def pad_heads(v):
            blocks = []
            for pair in range(H // 2):
                two = v[:, 128 * pair:128 * (pair + 1)]
                blocks.append(jnp.where(lane < DH, two, 0.0))
                blocks.append(jnp.where(lane < DH, pltpu.roll(two, DH, axis=1), 0.0))
            return jnp.concatenate(blocks, axis=1)

        q = jnp.dot(h, w_ref[:, OFF_Q:OFF_K], preferred_element_type=F32) * LOG2E
        qp_ref[...] = (pad_heads(q) + jnp.dot(zc, pq_ref[...], preferred_element_type=F32) + oq_ref[...]).astype(BF16)
        k = jnp.dot(h, w_ref[:, OFF_K:OFF_V], preferred_element_type=F32)
        kp_ref[...] = (pad_heads(k) + jnp.dot(zc, pk_ref[...], preferred_element_type=F32) + ok_ref[...]).astype(BF16)
        v = pad_heads(jnp.dot(h, w_ref[:, OFF_V:OFF_BCU], preferred_element_type=F32))
        ones_lane = lax.broadcasted_iota(jnp.int32, (tm, H * HP), 1) % HP == DH
        v_ref[...] = jnp.where(ones_lane, 1.0, v).astype(BF16)
        bcu_ref[...] = jnp.dot(h, w_ref[:, OFF_BCU:OFF_F], preferred_element_type=F32).astype(BF16)

    return pl.pallas_call(
        body, name="in_proj", grid=(s // tm,),
        in_specs=[_rows(tm, D), _full((1, D)), _resident((D, WP)), _full((1, 128)),
                  _full((128, 1024)), _full((128, 1024)), _full((1, 1024)), _full((1, 1024))],
        out_specs=[pl.BlockSpec((D, tm), lambda i: (0, i)), _rows(tm, 1024), _rows(tm, 1024), _rows(tm, 1024),
                   _rows(tm, 3 * CW), _rows(tm, 128)],
        out_shape=[jax.ShapeDtypeStruct((D, s), BF16), jax.ShapeDtypeStruct((s, 1024), BF16),
                   jax.ShapeDtypeStruct((s, 1024), BF16), jax.ShapeDtypeStruct((s, 1024), BF16),
                   jax.ShapeDtypeStruct((s, 3 * CW), BF16), jax.ShapeDtypeStruct((s, 128), F32)],
        scratch_shapes=[pltpu.VMEM((SUBLANES, 128), F32)],
        compiler_params=_cparams(56, ("arbitrary",)),
    )(x, g1, wp, bfp, pq, pk, oq, ok)


def _attn_fwd(qp, kp, v, *, t):
    s = qp.shape[0]
    nq = s // t

    def body(q_ref, k_ref, v_ref, o_ref, lse_ref, mk_ref):
        pi = pl.program_id(1)
        row = lax.broadcasted_iota(jnp.int32, (t, t), 0)
        col = lax.broadcasted_iota(jnp.int32, (t, t), 1)
        lane = lax.broadcasted_iota(jnp.int32, (t, 128), 1)

        def head_step(hh, rows, ki, carry, masked):
            m, acc = carry
            off = pl.multiple_of(ki * t, t)
            q = q_ref[rows, HP * hh:HP * (hh + 1)]
            k = k_ref[pl.ds(off, t), HP * hh:HP * (hh + 1)]
            sc = lax.dot_general(q, k, NT, preferred_element_type=F32)
            if masked:
                sc = jnp.where(col <= row, sc, -1e30)
            mn = jnp.maximum(m, jnp.max(sc, axis=-1, keepdims=True))
            p = jnp.exp2(sc - mn).astype(BF16)
            acc = jnp.exp2(m - mn) * acc + jnp.dot(p, v_ref[pl.ds(off, t), HP * hh:HP * (hh + 1)],
                                                  preferred_element_type=F32)
            return mn, acc

        def step(rows, ki, carry, masked):
            new = tuple(head_step(hh, rows, ki, carry[hh], masked) for hh in range(2))
            mk_ref[ki, rows] = jnp.where(lane < DH, jnp.broadcast_to(new[0][0], (t, 128)),
                                         jnp.broadcast_to(new[1][0], (t, 128)))
            return new

        init = (jnp.full((t, 1), -1e30, F32), jnp.zeros((t, 128), F32))
        top, bottom = slice(0, t), slice(t, 2 * t)

        def quad(j, carry):
            c0, c1 = carry
            c0 = step(top, 2 * j, c0, False)
            c1 = step(bottom, 2 * j, c1, False)
            c0 = step(top, 2 * j + 1, c0, False)
            c1 = step(bottom, 2 * j + 1, c1, False)
            return c0, c1

        c0, c1 = lax.fori_loop(0, pi, quad, ((init, init), (init, init)))
        f0 = step(top, 2 * pi, c0, True)
        c1 = step(bottom, 2 * pi, c1, False)
        f1 = step(bottom, 2 * pi + 1, c1, True)
        for rows, ((m0, acc0), (m1, acc1)) in ((top, f0), (bottom, f1)):
            l0, l1 = acc0[:, DH:DH + 1], acc1[:, DH:DH + 1]
            o_ref[rows, :] = jnp.where(lane < DH, acc0 / l0, pltpu.roll(acc1 / l1, DH, axis=1))
            lse_ref[rows, :] = jnp.where(lane < DH, jnp.broadcast_to(m0 + jnp.log2(l0), (t, 128)),
                                         jnp.broadcast_to(m1 + jnp.log2(l1), (t, 128)))

    return pl.pallas_call(
        body, name="attn_fwd", grid=(H // 2, nq // 2),
        in_specs=[pl.BlockSpec((2 * t, 2 * HP), lambda p, i: (i, p)),
                  pl.BlockSpec((s, 2 * HP), lambda p, i: (0, p)),
                  pl.BlockSpec((s, 2 * HP), lambda p, i: (0, p))],
        out_specs=[pl.BlockSpec((2 * t, 128), lambda p, i: (i, p)), pl.BlockSpec((2 * t, 128), lambda p, i: (i, p)),
                   pl.BlockSpec((nq, 2 * t, 128), lambda p, i: (0, i, p))],
        out_shape=[jax.ShapeDtypeStruct((s, AW), F32), jax.ShapeDtypeStruct((s, AW), F32),
                   jax.ShapeDtypeStruct((nq, s, AW), F32)],
        compiler_params=_cparams(48, ("arbitrary", "arbitrary")),
    )(qp, kp, v)


HALO = 16


def _conv_taps(bcu_ref, halo_ref, first, tm):
    z = bcu_ref[:, CW:2 * CW].astype(F32) * bcu_ref[:, 2 * CW:3 * CW].astype(F32)
    zh = jnp.where(first, 0.0, halo_ref[:, CW:2 * CW].astype(F32) * halo_ref[:, 2 * CW:3 * CW].astype(F32))
    row = lax.broadcasted_iota(jnp.int32, (tm, CW), 0)
    last, before = zh[HALO - 1:HALO, :], zh[HALO - 2:HALO - 1, :]
    z1 = jnp.where(row == 0, last, pltpu.roll(z, 1, axis=0))
    z2 = jnp.where(row == 0, before, jnp.where(row == 1, last, pltpu.roll(z, 2, axis=0)))
    return z, z1, z2


def _halo_before(tm, width):
    return pl.BlockSpec((HALO, width), lambda i: (jnp.maximum(i * (tm // HALO) - 1, 0), 0))


def _mix_out(o, bcu, cw8, ga, gc, gsum, w_out, x, g_post, g_ffn_pre, *, tm):
    s = x.shape[0]

    def body(o_ref, bcu_ref, halo_ref, cw_ref, ga_ref, gc_ref, gs_ref, w_ref, x_ref, g_ref, gf_ref,
             merged_ref, y_ref, x2_ref, cv_ref, h2_ref):
        z, z1, z2 = _conv_taps(bcu_ref, halo_ref, pl.program_id(0) == 0, tm)
        cv = cw_ref[0:1, :] * z2 + cw_ref[1:2, :] * z1 + cw_ref[2:3, :] * z
        cv_ref[...] = cv
        conv = bcu_ref[:, 0:CW].astype(F32) * cv
        ov = o_ref[...]
        ra = lax.rsqrt(_group_sum(ov * ov, gs_ref[...]) * (1.0 / DH) + EPS)
        rc = lax.rsqrt(_group_sum(conv * conv, gs_ref[...]) * (1.0 / DH) + EPS)
        merged = jnp.concatenate([ov * ra * ga_ref[...], conv * rc * gc_ref[...]], axis=1).astype(BF16)
        merged_ref[...] = merged
        y = jnp.dot(merged, w_ref[...], preferred_element_type=F32)
        y_ref[...] = y
        x2 = x_ref[...] + _rms_fwd(y, g_ref[...])[0]
        x2_ref[...] = x2
        h2_ref[...] = _rms_fwd(x2, gf_ref[...])[0].astype(BF16)

    return pl.pallas_call(
        body, name="mix_out", grid=(s // tm,),
        in_specs=[_rows(tm, AW), _rows(tm, 3 * CW), _halo_before(tm, 3 * CW), _full((SUBLANES, CW)),
                  _full((1, AW)), _full((1, CW)), _full((GS, GS)), _resident((D, D)), _rows(tm, D), _full((1, D)),
                  _full((1, D))],
        out_specs=[_rows(tm, D), _rows(tm, D), _rows(tm, D), _rows(tm, CW), _rows(tm, D)],
        out_shape=[jax.ShapeDtypeStruct((s, D), BF16), jax.ShapeDtypeStruct((s, D), F32),
                   jax.ShapeDtypeStruct((s, D), F32), jax.ShapeDtypeStruct((s, CW), F32),
                   jax.ShapeDtypeStruct((s, D), BF16)],
        compiler_params=_cparams(48, ("arbitrary",)),
    )(o, bcu, bcu, cw8, ga, gc, gsum, w_out, x, g_post, g_ffn_pre)


def _ffn_fwd_loss(h2, wgu, wd, x2, target, g_post, *, tm):
    s = x2.shape[0]

    def body(h_ref, w_ref, wd_ref, x2_ref, t_ref, g_ref,
             gate_ref, up_ref, a_ref, dx3_ref, dff_ref, loss_ref, dg_ref):
        @pl.when(pl.program_id(0) == 0)
        def _():
            loss_ref[...] = jnp.zeros_like(loss_ref)
            dg_ref[...] = jnp.zeros_like(dg_ref)

        h = h_ref[...]
        ff = None
        for c0, n in FF_CHUNKS:
            cols = slice(c0, c0 + n)
            gate = lax.dot_general(h, w_ref[0, cols, :], NT, preferred_element_type=F32)
            up = lax.dot_general(h, w_ref[1, cols, :], NT, preferred_element_type=F32)
            gate_ref[:, cols] = gate.astype(BF16)
            up_ref[:, cols] = up.astype(BF16)
            act = (gate * jax.nn.sigmoid(gate) * up).astype(BF16)
            a_ref[:, cols] = act
            part = jnp.dot(act, wd_ref[cols, :], preferred_element_type=F32)
            ff = part if ff is None else ff + part
        out, n, r = _rms_fwd(ff, g_ref[...])
        e = x2_ref[...] + out - t_ref[...]
        loss_ref[...] += _fold8(e * e)
        dx3 = e * (1.0 / D)
        dx3_ref[...] = dx3
        dff, dg = _rms_bwd(dx3, n, r, g_ref[...])
        dff_ref[...] = dff.astype(BF16)
        dg_ref[...] += _fold8(dg)

    wide = _rows(tm, DFF)
    return pl.pallas_call(
        body, name="ffn_fwd_loss", grid=(s // tm,),
        in_specs=[_rows(tm, D), _resident((2, DFF, D)), _resident((DFF, D)), _rows(tm, D), _rows(tm, D), _full((1, D))],
        out_specs=[wide, wide, wide, _rows(tm, D), _rows(tm, D), _full((SUBLANES, D)), _full((SUBLANES, D))],
        out_shape=[jax.ShapeDtypeStruct((s, DFF), BF16)] * 3
        + [jax.ShapeDtypeStruct((s, D), F32), jax.ShapeDtypeStruct((s, D), BF16),
           jax.ShapeDtypeStruct((SUBLANES, D), F32), jax.ShapeDtypeStruct((SUBLANES, D), F32)],
        compiler_params=_cparams(56, ("arbitrary",)),
    )(h2, wgu, wd, x2, target, g_post)


def _ffn_bwd(dff, wd, gate, up, wgu, x2, g_pre, dx3, y, g_post, *, tm):
    s = x2.shape[0]

    def body(dff_ref, wd_ref, gate_ref, up_ref, w_ref, x2_ref, gpre_ref, dx3_ref, y_ref, gpost_ref,
             dgu_ref, dx2_ref, dy_ref, dgpre_ref, dgpost_ref):
        @pl.when(pl.program_id(0) == 0)
        def _():
            dgpre_ref[...] = jnp.zeros_like(dgpre_ref)
            dgpost_ref[...] = jnp.zeros_like(dgpost_ref)

        dff = dff_ref[...]
        dh2 = None
        for c0, n in FF_CHUNKS_BWD:
            cols = slice(c0, c0 + n)
            da = lax.dot_general(dff, wd_ref[cols, :], NT, preferred_element_type=F32)
            g = gate_ref[:, cols].astype(F32)
            sg = jax.nn.sigmoid(g)
            dgate = (da * up_ref[:, cols].astype(F32) * (sg * (1.0 + g * (1.0 - sg)))).astype(BF16)
            dup = (da * (g * sg)).astype(BF16)
            dgu_ref[:, cols] = dgate
            dgu_ref[:, DFF + c0:DFF + c0 + n] = dup
            part = (jnp.dot(dgate, w_ref[0, cols, :], preferred_element_type=F32)
                    + jnp.dot(dup, w_ref[1, cols, :], preferred_element_type=F32))
            dh2 = part if dh2 is None else dh2 + part
        _, n2, r2 = _rms_fwd(x2_ref[...], gpre_ref[...])
        dxn, dg = _rms_bwd(dh2, n2, r2, gpre_ref[...])
        dgpre_ref[...] += _fold8(dg)
        dx2 = dx3_ref[...] + dxn
        dx2_ref[...] = dx2
        _, ny, ry = _rms_fwd(y_ref[...], gpost_ref[...])
        dy, dg2 = _rms_bwd(dx2, ny, ry, gpost_ref[...])
        dy_ref[...] = dy.astype(BF16)
        dgpost_ref[...] += _fold8(dg2)

    wide = _rows(tm, DFF)
    return pl.pallas_call(
        body, name="ffn_bwd", grid=(s // tm,),
        in_specs=[_rows(tm, D), _resident((DFF, D)), wide, wide, _resident((2, DFF, D)), _rows(tm, D), _full((1, D)),
                  _rows(tm, D), _rows(tm, D), _full((1, D))],
        out_specs=[_rows(tm, 2 * DFF), _rows(tm, D), _rows(tm, D),
                   _full((SUBLANES, D)), _full((SUBLANES, D))],
        out_shape=[jax.ShapeDtypeStruct((s, 2 * DFF), BF16), jax.ShapeDtypeStruct((s, D), F32),
                   jax.ShapeDtypeStruct((s, D), BF16), jax.ShapeDtypeStruct((SUBLANES, D), F32),
                   jax.ShapeDtypeStruct((SUBLANES, D), F32)],
        compiler_params=_cparams(56, ("arbitrary",)),
    )(dff, wd, gate, up, wgu, x2, g_pre, dx3, y, g_post)


def _grad_matmul(a, b, *, ta, tb, ts, name, vmem_mb=48):
    s, ka = a.shape
    nb = b.shape[1]
    ts = min(ts, s)
    nk = s // ts

    def body(a_ref, b_ref, o_ref, *acc):
        if nk == 1:
            o_ref[...] = lax.dot_general(a_ref[...], b_ref[...], TN, preferred_element_type=F32).astype(BF16)
            return
        k = pl.program_id(2)

        @pl.when(k == 0)
        def _():
            acc[0][...] = jnp.zeros_like(acc[0])

        acc[0][...] += lax.dot_general(a_ref[...], b_ref[...], TN, preferred_element_type=F32)

        @pl.when(k == nk - 1)
        def _():
            o_ref[...] = acc[0][...].astype(BF16)

    whole_b = {"pipeline_mode": pl.Buffered(1)} if nk == 1 and nb == tb else {}
    return pl.pallas_call(
        body, name=name, grid=(ka // ta, nb // tb, nk),
        in_specs=[pl.BlockSpec((ts, ta), lambda i, j, k: (k, i)),
                  pl.BlockSpec((ts, tb), lambda i, j, k: (k, j), **whole_b)],
        out_specs=pl.BlockSpec((ta, tb), lambda i, j, k: (i, j)),
        out_shape=jax.ShapeDtypeStruct((ka, nb), BF16),
        scratch_shapes=[pltpu.VMEM((ta, tb), F32)] if nk > 1 else [],
        compiler_params=_cparams(vmem_mb, ("arbitrary", "arbitrary", "arbitrary")),
    )(a, b)


GW_TILE = 256


def _grad_w_in(h1t, pieces):
    ka, s = h1t.shape
    widths = [p.shape[1] for p in pieces]
    assert all(w % GW_TILE == 0 for w in widths)
    first = [sum(widths[:i]) // GW_TILE for i in range(len(pieces))]
    count = [w // GW_TILE for w in widths]

    def body(a_ref, *refs):
        o_ref = refs[-1]
        j = pl.program_id(0)
        for ref, f0, n in zip(refs[:-1], first, count):
            @pl.when((j >= f0) & (j < f0 + n))
            def _(ref=ref):
                o_ref[...] = jnp.dot(a_ref[...], ref[...], preferred_element_type=F32).astype(BF16)

    def spec(f0, n):
        return pl.BlockSpec((s, GW_TILE), lambda j: (0, jnp.clip(j - f0, 0, n - 1)))

    return pl.pallas_call(
        body, name="grad_w_in", grid=(sum(count),),
        in_specs=[_resident((ka, s))] + [spec(f0, n) for f0, n in zip(first, count)],
        out_specs=pl.BlockSpec((ka, GW_TILE), lambda j: (0, j)),
        out_shape=jax.ShapeDtypeStruct((ka, sum(widths)), BF16),
        compiler_params=_cparams(56, ("arbitrary",)),
    )(h1t, *pieces)


def _mix_bwd(dy, w_out, o, cv, bcu, ga, gc, gsum, after, *, tm):
    s = dy.shape[0]

    def group_norm_bwd(dn_out, v, g, gs):
        r = lax.rsqrt(_group_sum(v * v, gs) * (1.0 / DH) + EPS)
        n = v * r
        dn = dn_out * g
        return r * (dn - n * (_group_sum(dn * n, gs) * (1.0 / DH))), dn_out * n

    def body(dy_ref, w_ref, o_ref, cv_ref, bcu_ref, ga_ref, gc_ref, gs_ref, after_ref,
             do_ref, dl_ref, dcv_ref, db_ref, dga_ref, dgc_ref):
        @pl.when(pl.program_id(0) == 0)
        def _():
            dga_ref[...] = jnp.zeros_like(dga_ref)
            dgc_ref[...] = jnp.zeros_like(dgc_ref)

        dm = lax.dot_general(dy_ref[...], w_ref[...], NT, preferred_element_type=F32)
        ov = o_ref[...]
        do, dga = group_norm_bwd(dm[:, 0:AW], ov, ga_ref[...], gs_ref[...])
        dob = do.astype(BF16)
        do_ref[...] = dob
        dl_ref[...] = _group_sum(dob.astype(F32) * ov, gs_ref[...])
        dga_ref[...] += _fold8(dga)
        gate_b = bcu_ref[:, 0:CW].astype(F32)
        cv = cv_ref[...]
        dconv, dgc = group_norm_bwd(dm[:, AW:D], gate_b * cv, gc_ref[...], gs_ref[...])
        dgc_ref[...] += _fold8(dgc)
        dcv_ref[...] = dconv * gate_b
        db_ref[...] = (dconv * cv).astype(BF16)

    return pl.pallas_call(
        body, name="mix_bwd", grid=(s // tm,),
        in_specs=[_rows(tm, D), _resident((D, D)), _rows(tm, AW), _rows(tm, CW), _rows(tm, 3 * CW),
                  _full((1, AW)), _full((1, CW)), _full((GS, GS)), ANY],
        out_specs=[_rows(tm, AW), _rows(tm, AW), _rows(tm, CW), _rows(tm, CW),
                   _full((SUBLANES, AW)), _full((SUBLANES, CW))],
        out_shape=[jax.ShapeDtypeStruct((s, AW), BF16), jax.ShapeDtypeStruct((s, AW), F32),
                   jax.ShapeDtypeStruct((s, CW), F32), jax.ShapeDtypeStruct((s, CW), BF16),
                   jax.ShapeDtypeStruct((SUBLANES, AW), F32), jax.ShapeDtypeStruct((SUBLANES, CW), F32)],
        compiler_params=_cparams(48, ("arbitrary",)),
    )(dy, w_out, o, cv, bcu, ga, gc, gsum, after)


def _conv_bwd(dcv, db, bcu, cw8, *, tm):
    s = dcv.shape[0]
    nt = s // tm

    def body(dcv_ref, nxt_ref, db_ref, bcu_ref, halo_ref, cw_ref, dbcu_ref, dw_ref):
        i = pl.program_id(0)

        @pl.when(i == 0)
        def _():
            dw_ref[...] = jnp.zeros_like(dw_ref)

        z, z1, z2 = _conv_taps(bcu_ref, halo_ref, i == 0, tm)
        d = dcv_ref[...]
        dw_ref[0] += _fold8(d * z2)
        dw_ref[1] += _fold8(d * z1)
        dw_ref[2] += _fold8(d * z)
        nx = jnp.where(i == nt - 1, 0.0, nxt_ref[...])
        row = lax.broadcasted_iota(jnp.int32, (tm, CW), 0)
        d1 = jnp.where(row == tm - 1, nx[0:1, :], pltpu.roll(d, tm - 1, axis=0))
        d2 = jnp.where(row == tm - 2, nx[0:1, :], jnp.where(row == tm - 1, nx[1:2, :], pltpu.roll(d, tm - 2, axis=0)))
        dz = cw_ref[2:3, :] * d + cw_ref[1:2, :] * d1 + cw_ref[0:1, :] * d2
        dbcu_ref[:, 0:CW] = db_ref[...]
        dbcu_ref[:, CW:2 * CW] = (dz * bcu_ref[:, 2 * CW:3 * CW].astype(F32)).astype(BF16)
        dbcu_ref[:, 2 * CW:3 * CW] = (dz * bcu_ref[:, CW:2 * CW].astype(F32)).astype(BF16)

    return pl.pallas_call(
        body, name="conv_bwd", grid=(nt,),
        in_specs=[_rows(tm, CW),
                  pl.BlockSpec((SUBLANES, CW), lambda i: (jnp.minimum((i + 1) * (tm // SUBLANES), s // SUBLANES - 1), 0)),
                  _rows(tm, CW), _rows(tm, 3 * CW), _halo_before(tm, 3 * CW), _full((SUBLANES, CW))],
        out_specs=[_rows(tm, 3 * CW), _full((3, SUBLANES, CW))],
        out_shape=[jax.ShapeDtypeStruct((s, 3 * CW), BF16), jax.ShapeDtypeStruct((3, SUBLANES, CW), F32)],
        compiler_params=_cparams(48, ("arbitrary",)),
    )(dcv, dcv, db, bcu, bcu, cw8)


def _attn_bwd(qp, kp, v, do, lse, dl, mk, *, t):
    s = qp.shape[0]
    nq = s // t

    def body(q_ref, k_ref, v_ref, do_ref, lse_ref, dl_ref, mk_ref, dq_ref, dk_ref, dv_ref, dkx_ref, dq_acc):
        pi = pl.program_id(1)

        @pl.when(pi == 0)
        def _():
            dq_acc[...] = jnp.zeros_like(dq_acc)

        row = lax.broadcasted_iota(jnp.int32, (t, t), 0)
        col = lax.broadcasted_iota(jnp.int32, (t, t), 1)
        lane = lax.broadcasted_iota(jnp.int32, (t, 128), 1)

        def head_step(hh, qi, carry, modes):
            off = pl.multiple_of(qi * t, t)
            rows = pl.ds(off, t)
            q = q_ref[rows, HP * hh:HP * (hh + 1)]
            qt = q.T
            lse_col = lse_ref[rows, DH * hh:DH * hh + 1]
            dl_col = dl_ref[rows, DH * hh:DH * hh + 1]
            do2 = do_ref[rows, :]
            dom = jnp.where(lane < DH, do2 if hh == 0 else pltpu.roll(do2, DH, axis=1), jnp.zeros((), BF16))
            new, dss = [], []
            for half, masked in enumerate(modes):
                if masked is None:
                    new.append(carry[half])
                    continue
                dk, dv, cs = carry[half]
                keys = slice(half * t, (half + 1) * t)
                m_col = mk_ref[half, rows, DH * hh:DH * hh + 1]
                scale = jnp.exp2(m_col - lse_col)
                sc = lax.dot_general(q, k_ref[keys, HP * hh:HP * (hh + 1)], NT, preferred_element_type=F32) - m_col
                if masked:
                    sc = jnp.where(col <= row, sc, -1e30)
                pt = jnp.exp2(sc).astype(BF16)
                dp = lax.dot_general(dom, v_ref[keys, HP * hh:HP * (hh + 1)], NT, preferred_element_type=F32)
                ds32 = (pt.astype(F32) * scale) * (dp - dl_col)
                ds = ds32.astype(BF16)
                cs = cs + _fold8(ds32)
                dv = dv + jnp.dot((dom.astype(F32) * scale).astype(BF16).T, pt, preferred_element_type=F32)
                dk = dk + jnp.dot(qt, ds, preferred_element_type=F32)
                new.append((dk, dv, cs))
                dss.append((half, ds))
            if len(dss) == 2:
                dq = jnp.dot(jnp.concatenate([dss[0][1], dss[1][1]], axis=1), k_ref[:, HP * hh:HP * (hh + 1)],
                             preferred_element_type=F32)
            else:
                half, ds = dss[0]
                dq = jnp.dot(ds, k_ref[half * t:(half + 1) * t, HP * hh:HP * (hh + 1)], preferred_element_type=F32)
            dq_acc[rows, HP * hh:HP * (hh + 1)] += dq
            return tuple(new)

        def step(qi, carry, modes):
            return tuple(head_step(hh, qi, carry[hh], modes) for hh in range(2))

        def two_heads(a0, a1):
            return jnp.where(lane < DH, a0, pltpu.roll(a1, DH, axis=1))

        def rows_to_lanes(a0, a1):
            return jnp.concatenate([a0, a1], axis=0).T

        zero = (jnp.zeros((HP, t), F32), jnp.zeros((128, t), F32), jnp.zeros((SUBLANES, t), F32))
        carry = step(2 * pi, ((zero, zero), (zero, zero)), (True, None))
        carry = step(2 * pi + 1, carry, (False, True))

        def pair(j, carry):
            qi = 2 * (pi + 1 + j)
            return step(qi + 1, step(qi, carry, (False, False)), (False, False))

        carry = lax.fori_loop(0, nq // 2 - 1 - pi, pair, carry)
        for half in range(2):
            keys = slice(half * t, (half + 1) * t)
            (dk0, dv0, cs0), (dk1, dv1, cs1) = carry[0][half], carry[1][half]
            dk_ref[keys, :] = (rows_to_lanes(dk0[0:DH], dk1[0:DH]) * LN2).astype(BF16)
            dv_ref[keys, :] = rows_to_lanes(dv0[0:DH], dv1[0:DH]).astype(BF16)
            total = lambda cs: jnp.broadcast_to(jnp.sum(cs, axis=0, keepdims=True), (DH, t))
            dkx_ref[keys, :] = rows_to_lanes(total(cs0), total(cs1))

        @pl.when(pi == nq // 2 - 1)
        def _():
            for c in range(s // t):
                rows = slice(c * t, (c + 1) * t)
                dq_ref[rows, :] = two_heads(dq_acc[rows, 0:HP], dq_acc[rows, HP:2 * HP]).astype(BF16)

    return pl.pallas_call(
        body, name="attn_bwd", grid=(H // 2, nq // 2),
        in_specs=[pl.BlockSpec((s, 2 * HP), lambda p, i: (0, p)),
                  pl.BlockSpec((2 * t, 2 * HP), lambda p, i: (i, p)),
                  pl.BlockSpec((2 * t, 2 * HP), lambda p, i: (i, p)),
                  pl.BlockSpec((s, 128), lambda p, i: (0, p)),
                  pl.BlockSpec((s, 128), lambda p, i: (0, p)),
                  pl.BlockSpec((s, 128), lambda p, i: (0, p)),
                  pl.BlockSpec((2, s, 128), lambda p, i: (i, 0, p))],
        out_specs=[pl.BlockSpec((s, 128), lambda p, i: (0, p)),
                   pl.BlockSpec((2 * t, 128), lambda p, i: (i, p)),
                   pl.BlockSpec((2 * t, 128), lambda p, i: (i, p)),
                   pl.BlockSpec((2 * t, 128), lambda p, i: (i, p))],
        out_shape=[jax.ShapeDtypeStruct((s, AW), BF16), jax.ShapeDtypeStruct((s, AW), BF16),
                   jax.ShapeDtypeStruct((s, AW), BF16), jax.ShapeDtypeStruct((s, AW), F32)],
        scratch_shapes=[pltpu.VMEM((s, 2 * HP), F32)],
        compiler_params=_cparams(56, ("arbitrary", "arbitrary")),
    )(qp, kp, v, do, lse, dl, mk)


def _forget_bwd(dkx, z, sel, *, tm):
    s = dkx.shape[0]
    nt = s // tm

    def body(dk_ref, z_ref, sel_ref, dfl_ref, dbf_ref, carry):
        @pl.when(pl.program_id(0) == 0)
        def _():
            carry[...] = jnp.zeros_like(carry)
            dbf_ref[...] = jnp.zeros_like(dbf_ref)

        dc = _split_dot(dk_ref[...], sel_ref[...])
        row = lax.broadcasted_iota(jnp.int32, (tm, tm), 0)
        col = lax.broadcasted_iota(jnp.int32, (tm, tm), 1)
        tri = (col >= row).astype(BF16)
        dlogf = _exact_dot01(tri, dc) + carry[0:1, :]
        carry[...] = jnp.broadcast_to(dlogf[0:1, :], carry.shape)
        dz = dlogf * (1.0 - jax.nn.sigmoid(z_ref[...]))
        dfl_ref[:, 0:128] = dz.astype(BF16)
        dfl_ref[:, 128:GW_TILE] = jnp.zeros((tm, GW_TILE - 128), BF16)
        dbf_ref[...] += _fold8(dz)

    rev = lambda i: (nt - 1 - i, 0)
    return pl.pallas_call(
        body, name="forget_bwd", grid=(nt,),
        in_specs=[pl.BlockSpec((tm, AW), rev), pl.BlockSpec((tm, 128), rev), _full((AW, 128))],
        out_specs=[pl.BlockSpec((tm, GW_TILE), rev), _full((SUBLANES, 128))],
        out_shape=[jax.ShapeDtypeStruct((s, GW_TILE), BF16), jax.ShapeDtypeStruct((SUBLANES, 128), F32)],
        scratch_shapes=[pltpu.VMEM((SUBLANES, 128), F32)],
        compiler_params=_cparams(48, ("arbitrary",)),
    )(dkx, z, sel)


def _in_proj_bwd(pieces, wp, x, g1, dx2, after, *, tm):
    s = x.shape[0]

    def body(q_ref, k_ref, v_ref, bcu_ref, f_ref, w_ref, x_ref, g_ref, dx2_ref, after_ref, dx_ref, dg_ref):
        @pl.when(pl.program_id(0) == 0)
        def _():
            dg_ref[...] = jnp.zeros_like(dg_ref)

        dh = None
        for ref, (lo, hi) in zip((q_ref, k_ref, v_ref, bcu_ref, f_ref), PIECES):
            part = lax.dot_general(ref[...], w_ref[:, lo:hi], NT, preferred_element_type=F32)
            dh = part if dh is None else dh + part
        _, n, r = _rms_fwd(x_ref[...], g_ref[...])
        dxn, dg = _rms_bwd(dh, n, r, g_ref[...])
        dx_ref[...] = dx2_ref[...] + dxn
        dg_ref[...] += _fold8(dg)

    return pl.pallas_call(
        body, name="in_proj_bwd", grid=(s // tm,),
        in_specs=[_rows(tm, hi - lo) for lo, hi in PIECES]
        + [_resident((D, WP)), _rows(tm, D), _full((1, D)), _rows(tm, D), ANY],
        out_specs=[_rows(tm, D), _full((SUBLANES, D))],
        out_shape=[jax.ShapeDtypeStruct((s, D), F32), jax.ShapeDtypeStruct((SUBLANES, D), F32)],
        compiler_params=_cparams(56, ("arbitrary",)),
    )(*pieces, wp, x, g1, dx2, after)


def _position():
    return lax.axis_index("x"), lax.axis_index("y"), lax.axis_index("c")


ANY = pl.BlockSpec(memory_space=pl.ANY)


def _all_gather(shards):
    n = len(shards)

    def body(*refs):
        x_refs, out_refs = refs[:n], refs[n:2 * n]
        send_sems, recv_sems, local_sems = refs[2 * n:]
        x, y, c = _position()
        me, sibling = (x, y, c), (x, y, 1 - c)
        chips = [(1 - x, y), (x, 1 - y), (1 - x, 1 - y)]

        def copy(a, k, block, to, own=False):
            slot = out_refs[a].at[4 * block[0] + 2 * block[1] + block[2]]
            return pltpu.make_async_remote_copy(
                src_ref=x_refs[a] if own else slot, dst_ref=slot,
                send_sem=send_sems.at[7 * a + k], recv_sem=recv_sems.at[7 * a + k], device_id=to, device_id_type=MESH_ID)

        mine = [pltpu.make_async_copy(x_refs[a], out_refs[a].at[4 * x + 2 * y + c], local_sems.at[a]) for a in range(n)]
        for cp in mine:
            cp.start()
        first = []
        for a in range(n):
            first.append(copy(a, 0, me, sibling, own=True))
            first += [copy(a, 1 + j, me, (*chip, c), own=True) for j, chip in enumerate(chips)]
        for cp in first:
            cp.start()
        passed = []
        for j, chip in enumerate(chips):
            for a in range(n):
                copy(a, 1 + j, (*chip, c), me).wait_recv()
                fwd = copy(a, 4 + j, (*chip, c), sibling)
                fwd.start()
                passed.append(fwd)
        for a in range(n):
            copy(a, 0, sibling, me).wait_recv()
            for j, chip in enumerate(chips):
                copy(a, 4 + j, (*chip, 1 - c), me).wait_recv()
        for cp in first + passed:
            cp.wait_send()
        for cp in mine:
            cp.wait()

    return pl.pallas_call(
        body, name="all_gather_weights",
        out_shape=[jax.ShapeDtypeStruct((NDEV,) + sh.shape, sh.dtype) for sh in shards],
        in_specs=[ANY] * n, out_specs=[ANY] * n,
        scratch_shapes=[pltpu.SemaphoreType.DMA((7 * n,)), pltpu.SemaphoreType.DMA((7 * n,)), pltpu.SemaphoreType.DMA((n,))],
    )(*shards)


def _pair_exchange(grads):
    n = len(grads)

    def body(*refs):
        g_refs, out_refs = refs[:n], refs[n:2 * n]
        send_sems, recv_sems = refs[2 * n:]
        x, y, c = _position()
        copies = [pltpu.make_async_remote_copy(
            src_ref=g_refs[a].at[:, pl.ds(1 - c, 1)], dst_ref=out_refs[a], send_sem=send_sems.at[a],
            recv_sem=recv_sems.at[a], device_id=(x, y, 1 - c), device_id_type=MESH_ID) for a in range(n)]
        for cp in copies:
            cp.start()
        for cp in copies:
            cp.wait()

    return pl.pallas_call(
        body, name="grad_pair_exchange",
        out_shape=[jax.ShapeDtypeStruct((4, 1) + g.shape[2:], g.dtype) for g in grads],
        in_specs=[ANY] * n, out_specs=[ANY] * n,
        scratch_shapes=[pltpu.SemaphoreType.DMA((n,)), pltpu.SemaphoreType.DMA((n,))],
    )(*grads)


def _pair_sum(g, got, idx, *, tr, name):
    r, c = g.shape[2:]

    def body(idx_ref, g_ref, got_ref, pb_ref, own_ref):
        p = g_ref[0, 0].astype(F32) + got_ref[0, 0].astype(F32)
        pb_ref[0] = p.astype(BF16)

        @pl.when(pl.program_id(1) == idx_ref[1])
        def _():
            own_ref[...] = p

    return pl.pallas_call(
        body, name=name,
        grid_spec=pltpu.PrefetchScalarGridSpec(
            num_scalar_prefetch=1, grid=(r // tr, 4),
            in_specs=[pl.BlockSpec((1, 1, tr, c), lambda i, j, idx: (j, idx[0], i, 0)),
                      pl.BlockSpec((1, 1, tr, c), lambda i, j, idx: (j, 0, i, 0))],
            out_specs=[pl.BlockSpec((1, tr, c), lambda i, j, idx: (j, i, 0)),
                       pl.BlockSpec((tr, c), lambda i, j, idx: (i, 0))]),
        out_shape=[jax.ShapeDtypeStruct((4, r, c), BF16), jax.ShapeDtypeStruct((r, c), F32)],
        compiler_params=_cparams(62, ("arbitrary", "arbitrary")),
    )(idx, g, got)


HBM = pl.BlockSpec(memory_space=pltpu.HBM)
SEM = pl.BlockSpec(memory_space=pltpu.SEMAPHORE)
DATAFLOW = pltpu.SideEffectType.DATAFLOW_SIDE_EFFECTING


PEERS = {"gather": NDEV - 1, "scatter": NDEV - 1, "chips": 3}


def _exchange_copies(src_refs, land_refs, send_sems, recv_sems, mode):
    x, y, c = _position()
    me, my_chip = 4 * x + 2 * y + c, 2 * x + y
    npeers = PEERS[mode]
    copies, own = [], []
    for a, (s_ref, l_ref) in enumerate(zip(src_refs, land_refs)):
        for k in range(npeers):
            if mode == "chips":
                px, py, pc = x ^ ((k + 1) >> 1), y ^ ((k + 1) & 1), c
                src, dst = s_ref.at[2 * px + py], l_ref.at[my_chip]
            else:
                px, py, pc = x ^ ((k + 1) >> 2), y ^ (((k + 1) >> 1) & 1), c ^ ((k + 1) & 1)
                src, dst = (s_ref.at[4 * px + 2 * py + pc] if mode == "scatter" else s_ref), l_ref.at[me]
            copies.append(pltpu.make_async_remote_copy(
                src_ref=src, dst_ref=dst, send_sem=send_sems.at[npeers * a + k], recv_sem=recv_sems.at[npeers * a + k],
                device_id=(px, py, pc), device_id_type=MESH_ID))
        slot = my_chip if mode == "chips" else me
        own.append(pltpu.make_async_copy(s_ref if mode == "gather" else s_ref.at[slot], l_ref.at[slot],
                                         send_sems.at[npeers * len(src_refs) + a]))
    return copies, own


def _exchange_start(srcs, lands, after, *, mode, name):
    n = len(srcs)
    nsem = PEERS[mode] * n

    def body(*refs):
        token = refs[-1]
        copies, own = _exchange_copies(refs[:n], refs[n:2 * n], refs[2 * n + 1], refs[2 * n + 2], mode)
        for cp in copies + own:
            cp.start()
        token[...] = jnp.zeros_like(token)

    arrays = list(srcs) + list(lands)
    outs = pl.pallas_call(
        body, name=name,
        out_shape=(pltpu.SemaphoreType.DMA((nsem + n,)), pltpu.SemaphoreType.DMA((nsem,)),
                   *[pltpu.HBM(a.shape, a.dtype) for a in arrays], jax.ShapeDtypeStruct((SUBLANES, LANES), F32)),
        in_specs=[HBM] * (2 * n) + [ANY],
        out_specs=(SEM, SEM, *[HBM] * (2 * n), pl.BlockSpec(memory_space=pltpu.VMEM)),
        input_output_aliases={i: 2 + i for i in range(2 * n)},
        compiler_params=pltpu.CompilerParams(has_side_effects=DATAFLOW),
    )(*[pltpu.with_memory_space_constraint(a, pltpu.HBM) for a in arrays], after)
    return outs[0], outs[1], outs[2:2 + n], outs[2 + n:2 + 2 * n], outs[-1]


def _exchange_wait(send_sems, recv_sems, srcs, lands, after, *, mode, name):
    n = len(srcs)

    def body(*refs):
        copies, own = _exchange_copies(refs[:n], refs[n:2 * n], refs[2 * n], refs[2 * n + 1], mode)
        for cp in copies:
            cp.wait_send()
            cp.wait_recv()
        for cp in own:
            cp.wait()

    arrays = list(srcs) + list(lands)
    outs = pl.pallas_call(
        body, name=name,
        out_shape=tuple(pltpu.HBM(a.shape, a.dtype) for a in arrays),
        in_specs=[HBM] * (2 * n) + [SEM, SEM, ANY],
        out_specs=tuple([HBM] * (2 * n)),
        input_output_aliases={i: i for i in range(2 * n)},
        compiler_params=pltpu.CompilerParams(has_side_effects=DATAFLOW),
    )(*arrays, send_sems, recv_sems, after)
    return outs[n:]


def _small_pack(parts):
    def body(gmp_ref, gmo_ref, gfp_ref, gfo_ref, ga_ref, gc_ref, dw_ref, bf_ref, loss_ref, out_ref):
        def colsum(v):
            return jnp.sum(v, axis=0, keepdims=True)

        loss = jnp.sum(colsum(loss_ref[...]), axis=1, keepdims=True) * (0.5 / D)
        rows = [colsum(gmp_ref[...]), colsum(gmo_ref[...]), colsum(gfp_ref[...]), colsum(gfo_ref[...]),
                jnp.concatenate([colsum(ga_ref[...]), colsum(gc_ref[...])], axis=1),
                jnp.concatenate([colsum(dw_ref[0]), colsum(dw_ref[1])], axis=1),
                jnp.concatenate([colsum(dw_ref[2]), colsum(bf_ref[...]), jnp.broadcast_to(loss, (1, 128)),
                                 jnp.zeros((1, 256), F32)], axis=1),
                jnp.zeros((1, D), F32)]
        out_ref[...] = jnp.concatenate(rows, axis=0)

    vm = pl.BlockSpec(memory_space=pltpu.VMEM)
    return pl.pallas_call(
        body, name="small_pack", out_shape=jax.ShapeDtypeStruct((SUBLANES, D), F32),
        in_specs=[vm] * len(parts), out_specs=vm,
    )(*parts)


def _small_sum(land):
    def body(land_ref, out_ref):
        acc = land_ref[0]
        for d in range(1, NDEV):
            acc = acc + land_ref[d]
        out_ref[...] = acc

    return pl.pallas_call(
        body, name="small_sum", grid=(1,), out_shape=jax.ShapeDtypeStruct((SUBLANES, D), F32),
        in_specs=[pl.BlockSpec((NDEV, SUBLANES, D), lambda i: (0, 0, 0))],
        out_specs=pl.BlockSpec((SUBLANES, D), lambda i: (0, 0)),
    )(land)


def _adam_update(w, g, m, v):
    nm = ADAM_B1 * m + (1.0 - ADAM_B1) * g
    nv = ADAM_B2 * v + (1.0 - ADAM_B2) * (g * g)
    m_hat = nm / (1.0 - ADAM_B1 ** ADAM_STEP)
    v_hat = nv / (1.0 - ADAM_B2 ** ADAM_STEP)
    return -ADAM_LR * (m_hat / (jnp.sqrt(v_hat) + ADAM_EPS) + ADAM_WD * w), nm, nv


SMALL_SLOTS = {"g_mix_pre": (0, 0, D), "g_mix_post": (1, 0, D), "g_ffn_pre": (2, 0, D), "g_ffn_post": (3, 0, D),
               "g_attn_out": (4, 0, AW), "g_conv_out": (4, AW, CW), "b_forget": (6, CW, H)}
LOSS_LANE = CW + 128


def _small_adamw(small, conv_grad, params):
    names = list(params)
    n = len(names)

    def body(*refs):
        small_ref, cg_ref = refs[0], refs[1]
        ins, outs = refs[2:2 + 3 * n], refs[2 + 3 * n:]
        for i, name in enumerate(names):
            w_ref, m_ref, v_ref = ins[3 * i:3 * i + 3]
            g_ref, d_ref, nm_ref, nv_ref = outs[4 * i:4 * i + 4]
            if name == "conv_w":
                g = cg_ref[...]
            else:
                r, c0, width = SMALL_SLOTS[name]
                g = small_ref[r:r + 1, c0:c0 + width]
            g_ref[...] = g
            d_ref[...], nm_ref[...], nv_ref[...] = _adam_update(w_ref[...], g, m_ref[...], v_ref[...])
        outs[4 * n][...] = small_ref[6:7, LOSS_LANE:LOSS_LANE + 1]

    vm = pl.BlockSpec(memory_space=pltpu.VMEM)
    flat = [a for name in names for a in params[name]]
    outs = pl.pallas_call(
        body, name="adamw_small",
        in_specs=[vm] * (2 + 3 * n), out_specs=[vm] * (4 * n + 1),
        out_shape=[jax.ShapeDtypeStruct(params[name][0].shape, F32) for name in names for _ in range(4)]
        + [jax.ShapeDtypeStruct((1, 1), F32)],
    )(small, conv_grad, *flat)
    return {name: outs[4 * i:4 * i + 4] for i, name in enumerate(names)}, outs[4 * n].reshape(())


def _chip_sum_adamw(got, own, idx, wt, mt, vt, *, tr, name):
    cols, rows = wt.shape
    gcols = own.shape[1]

    def body(idx_ref, got_ref, own_ref, w_ref, m_ref, v_ref, g_ref, d_ref, nm_ref, nv_ref):
        g = jnp.zeros((tr, gcols), F32)
        for j in range(4):
            g = g + jnp.where(idx_ref[1] == j, own_ref[...], got_ref[j].astype(F32))
        g = g.T[:cols]
        g_ref[...] = g
        d_ref[...], nm_ref[...], nv_ref[...] = _adam_update(w_ref[...], g, m_ref[...], v_ref[...])

    spec = pl.BlockSpec((cols, tr), lambda i, idx: (0, i))
    gspec = pl.BlockSpec((tr, gcols), lambda i, idx: (i, 0))
    return pl.pallas_call(
        body, name=name,
        grid_spec=pltpu.PrefetchScalarGridSpec(
            num_scalar_prefetch=1, grid=(rows // tr,),
            in_specs=[pl.BlockSpec((4, tr, gcols), lambda i, idx: (0, i, 0)), gspec, spec, spec, spec],
            out_specs=[spec] * 4),
        out_shape=[jax.ShapeDtypeStruct((cols, rows), F32)] * 4,
        compiler_params=_cparams(32, ("arbitrary",)),
    )(idx, got, own, wt, mt, vt)


def _device_sum_adamw(land, w, m, v, *, tr, name):
    rows, cols = w.shape

    def body(land_ref, w_ref, m_ref, v_ref, g_ref, d_ref, nm_ref, nv_ref):
        g = land_ref[0].astype(F32)
        for dev in range(1, NDEV):
            g = g + land_ref[dev].astype(F32)
        g_ref[...] = g
        d_ref[...], nm_ref[...], nv_ref[...] = _adam_update(w_ref[...], g, m_ref[...], v_ref[...])

    spec = pl.BlockSpec((tr, cols), lambda i: (i, 0))
    return pl.pallas_call(
        body, name=name, grid=(rows // tr,),
        in_specs=[pl.BlockSpec((NDEV, tr, cols), lambda i: (0, i, 0)), spec, spec, spec],
        out_specs=[spec] * 4,
        out_shape=[jax.ShapeDtypeStruct((rows, cols), F32)] * 4,
        compiler_params=_cparams(32, ("arbitrary",)),
    )(land, w, m, v)


def _placement_constants():
    j = np.arange(128)[:, None]
    lane = np.arange(1024)[None, :]
    head, sub = lane // HP, lane % HP
    piece, jh = j // H, j % H
    valid = (j < 3 * H) & (jh == head)
    pq = np.where(valid & (sub == DH + piece), 1.0, 0.0).astype(BF16)
    pk = np.where(valid & (sub == DH + 3 + piece), -1.0, 0.0).astype(BF16)
    oq = np.where((sub >= DH + 3) & (sub < DH + 6), 1.0, 0.0).astype(np.float32)
    ok = np.where((sub >= DH) & (sub < DH + 3), 1.0, 0.0).astype(np.float32)
    r = np.arange(AW)[:, None]
    cc = np.arange(128)[None, :]
    sel = np.where((r % DH == 3) & (r // DH == cc), -1.0, 0.0).astype(BF16)
    gi = np.arange(GS)
    gsum = (gi[:, None] // DH == gi[None, :] // DH).astype(BF16)
    return tuple(jnp.asarray(c) for c in (pq, pk, oq, ok, sel, gsum))


def _local_step(xs, tgt, wp, late_weights, cw8, bfp, g_attn_out, g_conv_out,
                g_mix_pre, g_mix_post, g_ffn_pre, g_ffn_post, early_grads=None, last_grad=None):
    pq, pk, oq, ok, sel, gsum = _placement_constants()
    h1t, qp, kp, vv, bcu, zf = _in_proj(xs, g_mix_pre, wp, bfp, pq, pk, oq, ok, tm=512)
    o, lse, mk = _attn_fwd(qp, kp, vv, t=512)
    w_out_f, wgu, wd = late_weights(lse)
    merged, y, x2, cv, h2 = _mix_out(o, bcu, cw8, g_attn_out, g_conv_out, gsum, w_out_f, xs, g_mix_post, g_ffn_pre, tm=512)
    gate, up, act, dx3, dff, loss_p, dg_ffn_post = _ffn_fwd_loss(h2, wgu, wd, x2, tgt, g_ffn_post, tm=512)

    dgu, dx2, dy, dg_ffn_pre, dg_mix_post = _ffn_bwd(dff, wd, gate, up, wgu, x2, g_ffn_pre, dx3, y, g_mix_post, tm=256)
    dw_down = _grad_matmul(act, dff, ta=DFF // 2, tb=D, ts=4096, name="grad_w_down", vmem_mb=60)
    dw_gu = _grad_matmul(dgu, h2, ta=DFF // 2, tb=D, ts=4096, name="grad_w_gate_up", vmem_mb=60).reshape(NDEV, FB, D)
    dw_out = _grad_matmul(merged, dy, ta=1024, tb=1024, ts=2048, name="grad_w_out")
    token = early_grads(dw_out, dw_gu, dw_down) if early_grads is not None else dw_out
    do, dl, dcv, db, dg_attn, dg_conv = _mix_bwd(dy, w_out_f, o, cv, bcu, g_attn_out, g_conv_out, gsum, token, tm=512)
    dbcu, dtaps = _conv_bwd(dcv, db, bcu, cw8, tm=512)
    dqp, dkp, dv, dkx = _attn_bwd(qp, kp, vv, do, lse, dl, mk, t=512)
    dfl, dbf = _forget_bwd(dkx, zf, sel, tm=512)
    pieces = (dqp, dkp, dv, dbcu, dfl)
    dwp = _grad_w_in(h1t, pieces)
    token = last_grad(dwp) if last_grad is not None else dwp
    grad_x, dg_mix_pre = _in_proj_bwd(pieces, wp, xs, g_mix_pre, dx2, token, tm=512)
    return (grad_x, dwp, dw_out, dw_gu, dw_down, dg_mix_pre, dg_mix_post, dg_ffn_pre, dg_ffn_post, dg_attn, dg_conv,
            dtaps, dbf, loss_p)


BIG_TILES = {"w_in": 256, "w_out": 128, "w_gate_up": 176, "w_down": 176}


def kernel(x, w_in, b_forget, conv_w, g_attn_out, g_conv_out, w_out, g_mix_pre, g_mix_post, w_gate_up, w_down, g_ffn_pre, g_ffn_post, loss_target, m_w_in, m_b_forget, m_conv_w, m_g_attn_out, m_g_conv_out, m_w_out, m_g_mix_pre, m_g_mix_post, m_w_gate_up, m_w_down, m_g_ffn_pre, m_g_ffn_post, v_w_in, v_b_forget, v_conv_w, v_g_attn_out, v_g_conv_out, v_w_out, v_g_mix_pre, v_g_mix_post, v_w_gate_up, v_w_down, v_g_ffn_pre, v_g_ffn_post):
    xc, yc, cc = _position()
    my_chip = 2 * xc + yc
    me = 2 * my_chip + cc
    idx = jnp.stack([cc, my_chip]).astype(jnp.int32)
    tables = _in_layout_tables()

    w_in_b = w_in[0].astype(BF16)
    g_in, g_last, g_taps = _all_gather([w_in_b[:, :IN_MAIN], w_in_b[:, IN_MAIN].reshape(SUBLANES, LANES), conv_w[0]])
    last_cols = jnp.pad(g_last.reshape(NDEV, D).T.astype(F32), ((0, 0), (0, LANES - NDEV)))
    wp = _assemble_w_in(g_in, last_cols, tables, tr=256)
    cw8 = jnp.pad(g_taps.transpose(1, 0, 2).reshape(3, CW), ((0, SUBLANES - 3), (0, 0)))

    late = [w_out[0].astype(BF16), w_gate_up[0].T.astype(BF16), w_down[0].astype(BF16)]
    ssem, rsem, late_thru, land_thru, token = _exchange_start(
        late, [lax.empty((NDEV,) + s.shape, s.dtype) for s in late], g_in, mode="gather",
        name="gather_late_start")
    bfp = jnp.pad(b_forget, ((0, 0), (0, 128 - H))) + token[0:1, :]

    def late_weights(after):
        l_out, l_gu, l_down = _exchange_wait(ssem, rsem, late_thru, land_thru, after, mode="gather", name="gather_late_wait")
        return l_out.reshape(D, D), l_gu.reshape(2, DFF, D), l_down.reshape(DFF, D)

    early = {}

    def early_grads(dw_out, dw_gu, dw_down):
        srcs = [dw_out.reshape(NDEV, D // NDEV, D), dw_gu, dw_down.reshape(NDEV, DFF // NDEV, D)]
        lands = [lax.empty(s.shape, s.dtype) for s in srcs]
        early["handles"] = _exchange_start(srcs, lands, dw_out, mode="scatter", name="scatter_early_start")
        return early["handles"][4]

    last = {}

    def last_grad(dwp):
        g_w_in = _disassemble_w_in(dwp, tables, tr=256).reshape(4, 2, D, IN_PAD)
        (from_sibling,) = _pair_exchange([g_w_in])
        pair_b, last["own"] = _pair_sum(g_w_in, from_sibling, idx, tr=D, name="grad_pair_sum_w_in")
        last["handles"] = _exchange_start([pair_b], [lax.empty(pair_b.shape, pair_b.dtype)], last["own"], mode="chips",
                                          name="chips_w_in_start")
        return last["handles"][4]

    (grad_x, dwp, dw_out, dw_gu, dw_down, dg_mix_pre, dg_mix_post, dg_ffn_pre, dg_ffn_post, dg_attn, dg_conv,
     dtaps, dbf, loss_p) = _local_step(x[0], loss_target[0], wp, late_weights, cw8, bfp, g_attn_out, g_conv_out,
                                        g_mix_pre, g_mix_post, g_ffn_pre, g_ffn_post, early_grads, last_grad)

    share = _small_pack([dg_mix_pre, dg_mix_post, dg_ffn_pre, dg_ffn_post, dg_attn, dg_conv, dtaps, dbf, loss_p])
    s_ssem, s_rsem, s_srcs, s_lands, s_token = _exchange_start(
        [share], [lax.empty((NDEV, SUBLANES, D), F32)], share, mode="gather", name="small_gather_start")

    e_ssem, e_rsem, e_srcs, e_lands, _ = early["handles"]
    land_out, land_gu, land_down = _exchange_wait(e_ssem, e_rsem, e_srcs, e_lands, s_token, mode="scatter",
                                                  name="scatter_early_wait")
    res = {}
    big = {"w_out": (land_out, w_out[0], m_w_out[0], v_w_out[0]),
           "w_gate_up": (land_gu, w_gate_up[0].T, m_w_gate_up[0].T, v_w_gate_up[0].T),
           "w_down": (land_down, w_down[0], m_w_down[0], v_w_down[0])}
    for name, (land, w, m, v) in big.items():
        outs = _device_sum_adamw(land, w, m, v, tr=BIG_TILES[name], name="adamw_" + name)
        res[name] = [(o.T if name == "w_gate_up" else o)[None] for o in outs]
    c_ssem, c_rsem, c_srcs, c_lands, _ = last["handles"]
    after = sum(res[n][1][0, :SUBLANES, :LANES] for n in big)
    (from_chips,) = _exchange_wait(c_ssem, c_rsem, c_srcs, c_lands, after, mode="chips", name="chips_w_in_wait")
    outs = _chip_sum_adamw(from_chips, last["own"], idx, w_in[0].T, m_w_in[0].T, v_w_in[0].T,
                           tr=BIG_TILES["w_in"], name="adamw_w_in")
    res["w_in"] = [o.T[None] for o in outs]

    (land_small,) = _exchange_wait(s_ssem, s_rsem, s_srcs, s_lands, res["w_in"][1][0, :SUBLANES, :LANES], mode="gather",
                                   name="small_gather_wait")
    small = _small_sum(land_small)
    taps_full = jnp.concatenate([small[5:6, :CW], small[5:6, CW:], small[6:7, :CW]], axis=0)
    taps_first = lambda a: a.transpose(1, 0, 2)
    smalls = {"b_forget": (b_forget, m_b_forget, v_b_forget),
              "conv_w": (taps_first(conv_w), taps_first(m_conv_w), taps_first(v_conv_w)),
              "g_attn_out": (g_attn_out, m_g_attn_out, v_g_attn_out), "g_conv_out": (g_conv_out, m_g_conv_out, v_g_conv_out),
              "g_mix_pre": (g_mix_pre, m_g_mix_pre, v_g_mix_pre), "g_mix_post": (g_mix_post, m_g_mix_post, v_g_mix_post),
              "g_ffn_pre": (g_ffn_pre, m_g_ffn_pre, v_g_ffn_pre), "g_ffn_post": (g_ffn_post, m_g_ffn_post, v_g_ffn_post)}
    own_taps = lax.dynamic_slice(taps_full, (0, me * 64), (3, 64))[:, None, :]
    small_res, loss = _small_adamw(small, own_taps, smalls)
    for name, outs in small_res.items():
        res[name] = [taps_first(o) for o in outs] if name == "conv_w" else list(outs)

    order = ["w_in", "b_forget", "conv_w", "g_attn_out", "g_conv_out", "w_out", "g_mix_pre", "g_mix_post",
             "w_gate_up", "w_down", "g_ffn_pre", "g_ffn_post"]
    outs = [loss, grad_x[None]]
    for k in range(4):
        outs += [res[n][k] for n in order]
    return tuple(outs)
```

```python
import functools

import numpy as np

import jax
import jax.numpy as jnp
from jax import lax
from jax.experimental import pallas as pl
from jax.experimental.pallas import tpu as pltpu

F32 = jnp.float32
BF16 = jnp.bfloat16
MESH_ID = pl.DeviceIdType.MESH

D = 1024
H = 8
DH = 64
AW = 512
CW = 512
DFF = 2816
FB = DFF // 4
FF_CHUNKS = ((0, 768), (768, 768), (1536, 768), (2304, 512))
FF_CHUNKS_BWD = ((0, 1024), (1024, 1024), (2048, 768))
HP = 128
OFF_Q, OFF_K, OFF_V, OFF_BCU, OFF_F = 0, 512, 1024, 1536, 3072
WP = OFF_F + 128
PIECES = ((OFF_Q, OFF_K), (OFF_K, OFF_V), (OFF_V, OFF_BCU), (OFF_BCU, OFF_F), (OFF_F, WP))
EPS = 1e-6
LOG2E, LN2 = 1.4426950408889634, 0.6931471805599453
NDEV = 8
LANES = 128
SUBLANES = 8
IN_COLS = 385
IN_PAD = 512
IN_MAIN = 384
WIN = 640
ADAM_LR, ADAM_B1, ADAM_B2, ADAM_EPS, ADAM_WD, ADAM_STEP = 0.001, 0.9, 0.999, 1e-08, 0.01, 10

NT = (((1,), (1,)), ((), ()))
TN = (((0,), (0,)), ((), ()))


def _cparams(vmem_mb=None, sem=None):
    kw = {}
    if vmem_mb is not None:
        kw["vmem_limit_bytes"] = vmem_mb << 20
    if sem is not None:
        kw["dimension_semantics"] = sem
    return pltpu.CompilerParams(**kw)


def _full(shape):
    return pl.BlockSpec(shape, lambda *_: (0,) * len(shape))


def _resident(shape):
    return pl.BlockSpec(shape, lambda *_: (0,) * len(shape), pipeline_mode=pl.Buffered(1))


def _rows(tm, width):
    return pl.BlockSpec((tm, width), lambda i: (i, 0))


def _fold8(v):
    r, w = v.shape
    return jnp.sum(v.reshape(r // SUBLANES, SUBLANES, w), axis=0)


def _split_dot(v, m01):
    hi = v.astype(BF16)
    lo = (v - hi.astype(F32)).astype(BF16)
    return (jnp.dot(hi, m01, preferred_element_type=F32)
            + jnp.dot(lo, m01, preferred_element_type=F32))


GS = 256


def _group_sum(v, g01):
    parts = [_split_dot(v[:, c:c + GS], g01) for c in range(0, v.shape[1], GS)]
    return parts[0] if len(parts) == 1 else jnp.concatenate(parts, axis=1)


def _exact_dot01(m01, v):
    p1 = v.astype(BF16)
    r1 = v - p1.astype(F32)
    p2 = r1.astype(BF16)
    p3 = (r1 - p2.astype(F32)).astype(BF16)
    return (jnp.dot(m01, p1, preferred_element_type=F32) + jnp.dot(m01, p2, preferred_element_type=F32)
            + jnp.dot(m01, p3, preferred_element_type=F32))


def _rms_fwd(v, g):
    r = lax.rsqrt(jnp.mean(v * v, axis=-1, keepdims=True) + EPS)
    n = v * r
    return n * g, n, r


def _rms_bwd(do, n, r, g):
    dn = do * g
    return r * (dn - n * jnp.mean(dn * n, axis=-1, keepdims=True)), do * n


def _padded_column(n):
    if n < AW:
        return OFF_Q + n, 0.125
    if n < 3 * AW:
        return n, 1.0
    if n < 3 * AW + H:
        return OFF_F + n - 3 * AW, 1.0
    return OFF_BCU + n - 3 * AW - H, 1.0


def _in_layout_tables():
    dest = -np.ones((IN_PAD, LANES), np.int32)
    dest_f = -np.ones((IN_PAD, LANES), np.int32)
    scale = np.zeros((IN_PAD, LANES), np.float32)
    starts = []
    for k in range(NDEV):
        cols = [_padded_column(IN_COLS * k + j) for j in range(IN_COLS)]
        main = [c for c, _ in cols if c < OFF_F]
        ws = min((min(main) // LANES) * LANES, OFF_F - WIN)
        assert ws <= min(main) and max(main) < ws + WIN
        starts.append(ws)
        for j, (c, sc) in enumerate(cols):
            scale[j, k] = sc
            if c < OFF_F:
                dest[j, k] = c - ws
            else:
                dest_f[j, k] = c - OFF_F
    f_shards = tuple(k for k in range(NDEV) if (dest_f[:, k] >= 0).any())
    return tuple(starts), f_shards, jnp.asarray(dest), jnp.asarray(dest_f), jnp.asarray(scale)


def _perm(dest_ref, scale_ref, k, width, rows=IN_PAD):
    lane = lax.broadcasted_iota(jnp.int32, (rows, width), 1)
    return jnp.where(dest_ref[0:rows, k:k + 1] == lane, scale_ref[0:rows, k:k + 1], 0.0).astype(BF16)


def _assemble_w_in(blocks, last_cols, tables, *, tr):
    starts, f_shards, dest, dest_f, scale = tables
    last = [_padded_column(IN_COLS * k + IN_MAIN) for k in range(NDEV)]
    f_main = [any(_padded_column(IN_COLS * k + j)[0] >= OFF_F for j in range(IN_MAIN)) for k in range(NDEV)]
    assert IN_COLS == IN_MAIN + 1

    def body(b_ref, c_ref, dest_ref, destf_ref, scale_ref, o_ref):
        o_ref[...] = jnp.zeros_like(o_ref)
        lane = lax.broadcasted_iota(jnp.int32, (tr, LANES), 1)
        for k in range(NDEV):
            b = b_ref[k]
            ws = starts[k]
            part = jnp.dot(b, _perm(dest_ref, scale_ref, k, WIN, IN_MAIN), preferred_element_type=F32)
            o_ref[:, ws:ws + WIN] += part.astype(BF16)
            if f_main[k]:
                part = jnp.dot(b, _perm(destf_ref, scale_ref, k, 128, IN_MAIN), preferred_element_type=F32)
                o_ref[:, OFF_F:WP] += part.astype(BF16)
            col, sc = last[k]
            tile = (col // LANES) * LANES
            o_ref[:, tile:tile + LANES] += jnp.where(lane == col - tile, c_ref[:, k:k + 1] * sc, 0.0).astype(BF16)

    tab = _full((IN_PAD, LANES))
    return pl.pallas_call(
        body, name="assemble_w_in", grid=(D // tr,),
        in_specs=[pl.BlockSpec((NDEV, tr, IN_MAIN), lambda i: (0, i, 0)), _rows(tr, LANES), tab, tab, tab],
        out_specs=_rows(tr, WP),
        out_shape=jax.ShapeDtypeStruct((D, WP), BF16),
        compiler_params=_cparams(48, ("arbitrary",)),
    )(blocks, last_cols, dest, dest_f, scale)


def _disassemble_w_in(dwp, tables, *, tr):
    starts, f_shards, dest, dest_f, scale = tables
    width = dwp.shape[1]

    def body(g_ref, dest_ref, destf_ref, scale_ref, o_ref):
        for k in range(NDEV):
            ws = starts[k]
            acc = lax.dot_general(g_ref[:, ws:ws + WIN], _perm(dest_ref, scale_ref, k, WIN), NT, preferred_element_type=F32)
            if k in f_shards:
                acc = acc + lax.dot_general(g_ref[:, OFF_F:WP], _perm(destf_ref, scale_ref, k, 128), NT,
                                            preferred_element_type=F32)
            o_ref[k] = acc.astype(BF16)

    tab = _full((IN_PAD, LANES))
    return pl.pallas_call(
        body, name="disassemble_w_in", grid=(D // tr,),
        in_specs=[_rows(tr, width), tab, tab, tab],
        out_specs=pl.BlockSpec((NDEV, tr, IN_PAD), lambda i: (0, i, 0)),
        out_shape=jax.ShapeDtypeStruct((NDEV, D, IN_PAD), BF16),
        compiler_params=_cparams(48, ("arbitrary",)),
    )(dwp, dest, dest_f, scale)


def _in_proj(x, g1, wp, bfp, pq, pk, oq, ok, *, tm):
    s = x.shape[0]

    def body(x_ref, g_ref, w_ref, bf_ref, pq_ref, pk_ref, oq_ref, ok_ref,
             ht_ref, qp_ref, kp_ref, v_ref, bcu_ref, z_ref, carry):
        @pl.when(pl.program_id(0) == 0)
        def _():
            carry[...] = jnp.zeros_like(carry)

        h = _rms_fwd(x_ref[...], g_ref[...])[0].astype(BF16)
        ht_ref[...] = h.T
        z = jnp.dot(h, w_ref[:, OFF_F:WP], preferred_element_type=F32) + bf_ref[...]
        z_ref[...] = z
        lane = lax.broadcasted_iota(jnp.int32, (tm, 128), 1)
        logf = jnp.where(lane < H, jnp.minimum(z, 0.0) - jnp.log(1.0 + jnp.exp(-jnp.abs(z))), 0.0)
        row = lax.broadcasted_iota(jnp.int32, (tm, tm), 0)
        col = lax.broadcasted_iota(jnp.int32, (tm, tm), 1)
        tri = (col <= row).astype(BF16)
        c = _exact_dot01(tri, logf) + carry[0:1, :]
        carry[...] = jnp.broadcast_to(c[tm - 1:tm, :], carry.shape)
        cb = c * LOG2E
        c1 = cb.astype(BF16).astype(F32)
        r1 = cb - c1
        c2 = r1.astype(BF16).astype(F32)
        c3 = (r1 - c2).astype(BF16).astype(F32)
        zc = (c1 + pltpu.roll(c2, 8, axis=1) + pltpu.roll(c3, 16, axis=1)).astype(BF16)

        def pad_heads(v):
            blocks = []
            for pair in range(H // 2):
                two = v[:, 128 * pair:128 * (pair + 1)]
                blocks.append(jnp.where(lane < DH, two, 0.0))
                blocks.append(jnp.where(lane < DH, pltpu.roll(two, DH, axis=1), 0.0))
            return jnp.concatenate(blocks, axis=1)

        q = jnp.dot(h, w_ref[:, OFF_Q:OFF_K], preferred_element_type=F32) * LOG2E
        qp_ref[...] = (pad_heads(q) + jnp.dot(zc, pq_ref[...], preferred_element_type=F32) + oq_ref[...]).astype(BF16)
        k = jnp.dot(h, w_ref[:, OFF_K:OFF_V], preferred_element_type=F32)
        kp_ref[...] = (pad_heads(k) + jnp.dot(zc, pk_ref[...], preferred_element_type=F32) + ok_ref[...]).astype(BF16)
        v = pad_heads(jnp.dot(h, w_ref[:, OFF_V:OFF_BCU], preferred_element_type=F32))
        ones_lane = lax.broadcasted_iota(jnp.int32, (tm, H * HP), 1) % HP == DH
        v_ref[...] = jnp.where(ones_lane, 1.0, v).astype(BF16)
        bcu_ref[...] = jnp.dot(h, w_ref[:, OFF_BCU:OFF_F], preferred_element_type=F32).astype(BF16)

    return pl.pallas_call(
        body, name="in_proj", grid=(s // tm,),
        in_specs=[_rows(tm, D), _full((1, D)), _resident((D, WP)), _full((1, 128)),
                  _full((128, 1024)), _full((128, 1024)), _full((1, 1024)), _full((1, 1024))],
        out_specs=[pl.BlockSpec((D, tm), lambda i: (0, i)), _rows(tm, 1024), _rows(tm, 1024), _rows(tm, 1024),
                   _rows(tm, 3 * CW), _rows(tm, 128)],
        out_shape=[jax.ShapeDtypeStruct((D, s), BF16), jax.ShapeDtypeStruct((s, 1024), BF16),
                   jax.ShapeDtypeStruct((s, 1024), BF16), jax.ShapeDtypeStruct((s, 1024), BF16),
                   jax.ShapeDtypeStruct((s, 3 * CW), BF16), jax.ShapeDtypeStruct((s, 128), F32)],
        scratch_shapes=[pltpu.VMEM((SUBLANES, 128), F32)],
        compiler_params=_cparams(56, ("arbitrary",)),
    )(x, g1, wp, bfp, pq, pk, oq, ok)


def _attn_fwd(qp, kp, v, *, t):
    s = qp.shape[0]
    nq = s // t

    def body(q_ref, k_ref, v_ref, o_ref, lse_ref, mk_ref):
        pi = pl.program_id(1)
        row = lax.broadcasted_iota(jnp.int32, (t, t), 0)
        col = lax.broadcasted_iota(jnp.int32, (t, t), 1)
        lane = lax.broadcasted_iota(jnp.int32, (t, 128), 1)

        def head_step(hh, rows, ki, carry, masked):
            m, acc = carry
            off = pl.multiple_of(ki * t, t)
            q = q_ref[rows, HP * hh:HP * (hh + 1)]
            k = k_ref[pl.ds(off, t), HP * hh:HP * (hh + 1)]
            sc = lax.dot_general(q, k, NT, preferred_element_type=F32)
            if masked:
                sc = jnp.where(col <= row, sc, -1e30)
            mn = jnp.maximum(m, jnp.max(sc, axis=-1, keepdims=True))
            p = jnp.exp2(sc - mn).astype(BF16)
            acc = jnp.exp2(m - mn) * acc + jnp.dot(p, v_ref[pl.ds(off, t), HP * hh:HP * (hh + 1)],
                                                  preferred_element_type=F32)
            return mn, acc

        def step(rows, ki, carry, masked):
            new = tuple(head_step(hh, rows, ki, carry[hh], masked) for hh in range(2))
            mk_ref[ki, rows] = jnp.where(lane < DH, jnp.broadcast_to(new[0][0], (t, 128)),
                                         jnp.broadcast_to(new[1][0], (t, 128)))
            return new

        init = (jnp.full((t, 1), -1e30, F32), jnp.zeros((t, 128), F32))
        top, bottom = slice(0, t), slice(t, 2 * t)

        def quad(j, carry):
            c0, c1 = carry
            c0 = step(top, 2 * j, c0, False)
            c1 = step(bottom, 2 * j, c1, False)
            c0 = step(top, 2 * j + 1, c0, False)
            c1 = step(bottom, 2 * j + 1, c1, False)
            return c0, c1

        c0, c1 = lax.fori_loop(0, pi, quad, ((init, init), (init, init)))
        f0 = step(top, 2 * pi, c0, True)
        c1 = step(bottom, 2 * pi, c1, False)
        f1 = step(bottom, 2 * pi + 1, c1, True)
        for rows, ((m0, acc0), (m1, acc1)) in ((top, f0), (bottom, f1)):
            l0, l1 = acc0[:, DH:DH + 1], acc1[:, DH:DH + 1]
            o_ref[rows, :] = jnp.where(lane < DH, acc0 / l0, pltpu.roll(acc1 / l1, DH, axis=1))
            lse_ref[rows, :] = jnp.where(lane < DH, jnp.broadcast_to(m0 + jnp.log2(l0), (t, 128)),
                                         jnp.broadcast_to(m1 + jnp.log2(l1), (t, 128)))

    return pl.pallas_call(
        body, name="attn_fwd", grid=(H // 2, nq // 2),
        in_specs=[pl.BlockSpec((2 * t, 2 * HP), lambda p, i: (i, p)),
                  pl.BlockSpec((s, 2 * HP), lambda p, i: (0, p)),
                  pl.BlockSpec((s, 2 * HP), lambda p, i: (0, p))],
        out_specs=[pl.BlockSpec((2 * t, 128), lambda p, i: (i, p)), pl.BlockSpec((2 * t, 128), lambda p, i: (i, p)),
                   pl.BlockSpec((nq, 2 * t, 128), lambda p, i: (0, i, p))],
        out_shape=[jax.ShapeDtypeStruct((s, AW), F32), jax.ShapeDtypeStruct((s, AW), F32),
                   jax.ShapeDtypeStruct((nq, s, AW), F32)],
        compiler_params=_cparams(48, ("arbitrary", "arbitrary")),
    )(qp, kp, v)


HALO = 16


def _conv_taps(bcu_ref, halo_ref, first, tm):
    z = bcu_ref[:, CW:2 * CW].astype(F32) * bcu_ref[:, 2 * CW:3 * CW].astype(F32)
    zh = jnp.where(first, 0.0, halo_ref[:, CW:2 * CW].astype(F32) * halo_ref[:, 2 * CW:3 * CW].astype(F32))
    row = lax.broadcasted_iota(jnp.int32, (tm, CW), 0)
    last, before = zh[HALO - 1:HALO, :], zh[HALO - 2:HALO - 1, :]
    z1 = jnp.where(row == 0, last, pltpu.roll(z, 1, axis=0))
    z2 = jnp.where(row == 0, before, jnp.where(row == 1, last, pltpu.roll(z, 2, axis=0)))
    return z, z1, z2


def _halo_before(tm, width):
    return pl.BlockSpec((HALO, width), lambda i: (jnp.maximum(i * (tm // HALO) - 1, 0), 0))


def _mix_out(o, bcu, cw8, ga, gc, gsum, w_out, x, g_post, g_ffn_pre, *, tm):
    s = x.shape[0]

    def body(o_ref, bcu_ref, halo_ref, cw_ref, ga_ref, gc_ref, gs_ref, w_ref, x_ref, g_ref, gf_ref,
             merged_ref, y_ref, x2_ref, cv_ref, h2_ref):
        z, z1, z2 = _conv_taps(bcu_ref, halo_ref, pl.program_id(0) == 0, tm)
        cv = cw_ref[0:1, :] * z2 + cw_ref[1:2, :] * z1 + cw_ref[2:3, :] * z
        cv_ref[...] = cv
        conv = bcu_ref[:, 0:CW].astype(F32) * cv
        ov = o_ref[...]
        ra = lax.rsqrt(_group_sum(ov * ov, gs_ref[...]) * (1.0 / DH) + EPS)
        rc = lax.rsqrt(_group_sum(conv * conv, gs_ref[...]) * (1.0 / DH) + EPS)
        merged = jnp.concatenate([ov * ra * ga_ref[...], conv * rc * gc_ref[...]], axis=1).astype(BF16)
        merged_ref[...] = merged
        y = jnp.dot(merged, w_ref[...], preferred_element_type=F32)
        y_ref[...] = y
        x2 = x_ref[...] + _rms_fwd(y, g_ref[...])[0]
        x2_ref[...] = x2
        h2_ref[...] = _rms_fwd(x2, gf_ref[...])[0].astype(BF16)

    return pl.pallas_call(
        body, name="mix_out", grid=(s // tm,),
        in_specs=[_rows(tm, AW), _rows(tm, 3 * CW), _halo_before(tm, 3 * CW), _full((SUBLANES, CW)),
                  _full((1, AW)), _full((1, CW)), _full((GS, GS)), _resident((D, D)), _rows(tm, D), _full((1, D)),
                  _full((1, D))],
        out_specs=[_rows(tm, D), _rows(tm, D), _rows(tm, D), _rows(tm, CW), _rows(tm, D)],
        out_shape=[jax.ShapeDtypeStruct((s, D), BF16), jax.ShapeDtypeStruct((s, D), F32),
                   jax.ShapeDtypeStruct((s, D), F32), jax.ShapeDtypeStruct((s, CW), F32),
                   jax.ShapeDtypeStruct((s, D), BF16)],
        compiler_params=_cparams(48, ("arbitrary",)),
    )(o, bcu, bcu, cw8, ga, gc, gsum, w_out, x, g_post, g_ffn_pre)


def _ffn_fwd_loss(h2, wgu, wd, x2, target, g_post, *, tm):
    s = x2.shape[0]

    def body(h_ref, w_ref, wd_ref, x2_ref, t_ref, g_ref,
             gate_ref, up_ref, a_ref, dx3_ref, dff_ref, loss_ref, dg_ref):
        @pl.when(pl.program_id(0) == 0)
        def _():
            loss_ref[...] = jnp.zeros_like(loss_ref)
            dg_ref[...] = jnp.zeros_like(dg_ref)

        h = h_ref[...]
        ff = None
        for c0, n in FF_CHUNKS:
            cols = slice(c0, c0 + n)
            gate = lax.dot_general(h, w_ref[0, cols, :], NT, preferred_element_type=F32)
            up = lax.dot_general(h, w_ref[1, cols, :], NT, preferred_element_type=F32)
            gate_ref[:, cols] = gate.astype(BF16)
            up_ref[:, cols] = up.astype(BF16)
            act = (gate * jax.nn.sigmoid(gate) * up).astype(BF16)
            a_ref[:, cols] = act
            part = jnp.dot(act, wd_ref[cols, :], preferred_element_type=F32)
            ff = part if ff is None else ff + part
        out, n, r = _rms_fwd(ff, g_ref[...])
        e = x2_ref[...] + out - t_ref[...]
        loss_ref[...] += _fold8(e * e)
        dx3 = e * (1.0 / D)
        dx3_ref[...] = dx3
        dff, dg = _rms_bwd(dx3, n, r, g_ref[...])
        dff_ref[...] = dff.astype(BF16)
        dg_ref[...] += _fold8(dg)

    wide = _rows(tm, DFF)
    return pl.pallas_call(
        body, name="ffn_fwd_loss", grid=(s // tm,),
        in_specs=[_rows(tm, D), _resident((2, DFF, D)), _resident((DFF, D)), _rows(tm, D), _rows(tm, D), _full((1, D))],
        out_specs=[wide, wide, wide, _rows(tm, D), _rows(tm, D), _full((SUBLANES, D)), _full((SUBLANES, D))],
        out_shape=[jax.ShapeDtypeStruct((s, DFF), BF16)] * 3
        + [jax.ShapeDtypeStruct((s, D), F32), jax.ShapeDtypeStruct((s, D), BF16),
           jax.ShapeDtypeStruct((SUBLANES, D), F32), jax.ShapeDtypeStruct((SUBLANES, D), F32)],
        compiler_params=_cparams(56, ("arbitrary",)),
    )(h2, wgu, wd, x2, target, g_post)


def _ffn_bwd(dff, wd, gate, up, wgu, x2, g_pre, dx3, y, g_post, *, tm):
    s = x2.shape[0]

    def body(dff_ref, wd_ref, gate_ref, up_ref, w_ref, x2_ref, gpre_ref, dx3_ref, y_ref, gpost_ref,
             dgu_ref, dx2_ref, dy_ref, dgpre_ref, dgpost_ref):
        @pl.when(pl.program_id(0) == 0)
        def _():
            dgpre_ref[...] = jnp.zeros_like(dgpre_ref)
            dgpost_ref[...] = jnp.zeros_like(dgpost_ref)

        dff = dff_ref[...]
        dh2 = None
        for c0, n in FF_CHUNKS_BWD:
            cols = slice(c0, c0 + n)
            da = lax.dot_general(dff, wd_ref[cols, :], NT, preferred_element_type=F32)
            g = gate_ref[:, cols].astype(F32)
            sg = jax.nn.sigmoid(g)
            dgate = (da * up_ref[:, cols].astype(F32) * (sg * (1.0 + g * (1.0 - sg)))).astype(BF16)
            dup = (da * (g * sg)).astype(BF16)
            dgu_ref[:, cols] = dgate
            dgu_ref[:, DFF + c0:DFF + c0 + n] = dup
            part = (jnp.dot(dgate, w_ref[0, cols, :], preferred_element_type=F32)
                    + jnp.dot(dup, w_ref[1, cols, :], preferred_element_type=F32))
            dh2 = part if dh2 is None else dh2 + part
        _, n2, r2 = _rms_fwd(x2_ref[...], gpre_ref[...])
        dxn, dg = _rms_bwd(dh2, n2, r2, gpre_ref[...])
        dgpre_ref[...] += _fold8(dg)
        dx2 = dx3_ref[...] + dxn
        dx2_ref[...] = dx2
        _, ny, ry = _rms_fwd(y_ref[...], gpost_ref[...])
        dy, dg2 = _rms_bwd(dx2, ny, ry, gpost_ref[...])
        dy_ref[...] = dy.astype(BF16)
        dgpost_ref[...] += _fold8(dg2)

    wide = _rows(tm, DFF)
    return pl.pallas_call(
        body, name="ffn_bwd", grid=(s // tm,),
        in_specs=[_rows(tm, D), _resident((DFF, D)), wide, wide, _resident((2, DFF, D)), _rows(tm, D), _full((1, D)),
                  _rows(tm, D), _rows(tm, D), _full((1, D))],
        out_specs=[_rows(tm, 2 * DFF), _rows(tm, D), _rows(tm, D),
                   _full((SUBLANES, D)), _full((SUBLANES, D))],
        out_shape=[jax.ShapeDtypeStruct((s, 2 * DFF), BF16), jax.ShapeDtypeStruct((s, D), F32),
                   jax.ShapeDtypeStruct((s, D), BF16), jax.ShapeDtypeStruct((SUBLANES, D), F32),
                   jax.ShapeDtypeStruct((SUBLANES, D), F32)],
        compiler_params=_cparams(56, ("arbitrary",)),
    )(dff, wd, gate, up, wgu, x2, g_pre, dx3, y, g_post)


def _grad_matmul(a, b, *, ta, tb, ts, name, vmem_mb=48):
    s, ka = a.shape
    nb = b.shape[1]
    ts = min(ts, s)
    nk = s // ts

    def body(a_ref, b_ref, o_ref, *acc):
        if nk == 1:
            o_ref[...] = lax.dot_general(a_ref[...], b_ref[...], TN, preferred_element_type=F32).astype(BF16)
            return
        k = pl.program_id(2)

        @pl.when(k == 0)
        def _():
            acc[0][...] = jnp.zeros_like(acc[0])

        acc[0][...] += lax.dot_general(a_ref[...], b_ref[...], TN, preferred_element_type=F32)

        @pl.when(k == nk - 1)
        def _():
            o_ref[...] = acc[0][...].astype(BF16)

    whole_b = {"pipeline_mode": pl.Buffered(1)} if nk == 1 and nb == tb else {}
    return pl.pallas_call(
        body, name=name, grid=(ka // ta, nb // tb, nk),
        in_specs=[pl.BlockSpec((ts, ta), lambda i, j, k: (k, i)),
                  pl.BlockSpec((ts, tb), lambda i, j, k: (k, j), **whole_b)],
        out_specs=pl.BlockSpec((ta, tb), lambda i, j, k: (i, j)),
        out_shape=jax.ShapeDtypeStruct((ka, nb), BF16),
        scratch_shapes=[pltpu.VMEM((ta, tb), F32)] if nk > 1 else [],
        compiler_params=_cparams(vmem_mb, ("arbitrary", "arbitrary", "arbitrary")),
    )(a, b)


GW_TILE = 256


def _grad_w_in(h1t, pieces):
    ka, s = h1t.shape
    widths = [p.shape[1] for p in pieces]
    assert all(w % GW_TILE == 0 for w in widths)
    first = [sum(widths[:i]) // GW_TILE for i in range(len(pieces))]
    count = [w // GW_TILE for w in widths]

    def body(a_ref, *refs):
        o_ref = refs[-1]
        j = pl.program_id(0)
        for ref, f0, n in zip(refs[:-1], first, count):
            @pl.when((j >= f0) & (j < f0 + n))
            def _(ref=ref):
                o_ref[...] = jnp.dot(a_ref[...], ref[...], preferred_element_type=F32).astype(BF16)

    def spec(f0, n):
        return pl.BlockSpec((s, GW_TILE), lambda j: (0, jnp.clip(j - f0, 0, n - 1)))

    return pl.pallas_call(
        body, name="grad_w_in", grid=(sum(count),),
        in_specs=[_resident((ka, s))] + [spec(f0, n) for f0, n in zip(first, count)],
        out_specs=pl.BlockSpec((ka, GW_TILE), lambda j: (0, j)),
        out_shape=jax.ShapeDtypeStruct((ka, sum(widths)), BF16),
        compiler_params=_cparams(56, ("arbitrary",)),
    )(h1t, *pieces)


def _mix_bwd(dy, w_out, o, cv, bcu, ga, gc, gsum, after, *, tm):
    s = dy.shape[0]

    def group_norm_bwd(dn_out, v, g, gs):
        r = lax.rsqrt(_group_sum(v * v, gs) * (1.0 / DH) + EPS)
        n = v * r
        dn = dn_out * g
        return r * (dn - n * (_group_sum(dn * n, gs) * (1.0 / DH))), dn_out * n

    def body(dy_ref, w_ref, o_ref, cv_ref, bcu_ref, ga_ref, gc_ref, gs_ref, after_ref,
             do_ref, dl_ref, dcv_ref, db_ref, dga_ref, dgc_ref):
        @pl.when(pl.program_id(0) == 0)
        def _():
            dga_ref[...] = jnp.zeros_like(dga_ref)
            dgc_ref[...] = jnp.zeros_like(dgc_ref)

        dm = lax.dot_general(dy_ref[...], w_ref[...], NT, preferred_element_type=F32)
        ov = o_ref[...]
        do, dga = group_norm_bwd(dm[:, 0:AW], ov, ga_ref[...], gs_ref[...])
        dob = do.astype(BF16)
        do_ref[...] = dob
        dl_ref[...] = _group_sum(dob.astype(F32) * ov, gs_ref[...])
        dga_ref[...] += _fold8(dga)
        gate_b = bcu_ref[:, 0:CW].astype(F32)
        cv = cv_ref[...]
        dconv, dgc = group_norm_bwd(dm[:, AW:D], gate_b * cv, gc_ref[...], gs_ref[...])
        dgc_ref[...] += _fold8(dgc)
        dcv_ref[...] = dconv * gate_b
        db_ref[...] = (dconv * cv).astype(BF16)

    return pl.pallas_call(
        body, name="mix_bwd", grid=(s // tm,),
        in_specs=[_rows(tm, D), _resident((D, D)), _rows(tm, AW), _rows(tm, CW), _rows(tm, 3 * CW),
                  _full((1, AW)), _full((1, CW)), _full((GS, GS)), ANY],
        out_specs=[_rows(tm, AW), _rows(tm, AW), _rows(tm, CW), _rows(tm, CW),
                   _full((SUBLANES, AW)), _full((SUBLANES, CW))],
        out_shape=[jax.ShapeDtypeStruct((s, AW), BF16), jax.ShapeDtypeStruct((s, AW), F32),
                   jax.ShapeDtypeStruct((s, CW), F32), jax.ShapeDtypeStruct((s, CW), BF16),
                   jax.ShapeDtypeStruct((SUBLANES, AW), F32), jax.ShapeDtypeStruct((SUBLANES, CW), F32)],
        compiler_params=_cparams(48, ("arbitrary",)),
    )(dy, w_out, o, cv, bcu, ga, gc, gsum, after)


def _conv_bwd(dcv, db, bcu, cw8, *, tm):
    s = dcv.shape[0]
    nt = s // tm

    def body(dcv_ref, nxt_ref, db_ref, bcu_ref, halo_ref, cw_ref, dbcu_ref, dw_ref):
        i = pl.program_id(0)

        @pl.when(i == 0)
        def _():
            dw_ref[...] = jnp.zeros_like(dw_ref)

        z, z1, z2 = _conv_taps(bcu_ref, halo_ref, i == 0, tm)
        d = dcv_ref[...]
        dw_ref[0] += _fold8(d * z2)
        dw_ref[1] += _fold8(d * z1)
        dw_ref[2] += _fold8(d * z)
        nx = jnp.where(i == nt - 1, 0.0, nxt_ref[...])
        row = lax.broadcasted_iota(jnp.int32, (tm, CW), 0)
        d1 = jnp.where(row == tm - 1, nx[0:1, :], pltpu.roll(d, tm - 1, axis=0))
        d2 = jnp.where(row == tm - 2, nx[0:1, :], jnp.where(row == tm - 1, nx[1:2, :], pltpu.roll(d, tm - 2, axis=0)))
        dz = cw_ref[2:3, :] * d + cw_ref[1:2, :] * d1 + cw_ref[0:1, :] * d2
        dbcu_ref[:, 0:CW] = db_ref[...]
        dbcu_ref[:, CW:2 * CW] = (dz * bcu_ref[:, 2 * CW:3 * CW].astype(F32)).astype(BF16)
        dbcu_ref[:, 2 * CW:3 * CW] = (dz * bcu_ref[:, CW:2 * CW].astype(F32)).astype(BF16)

    return pl.pallas_call(
        body, name="conv_bwd", grid=(nt,),
        in_specs=[_rows(tm, CW),
                  pl.BlockSpec((SUBLANES, CW), lambda i: (jnp.minimum((i + 1) * (tm // SUBLANES), s // SUBLANES - 1), 0)),
                  _rows(tm, CW), _rows(tm, 3 * CW), _halo_before(tm, 3 * CW), _full((SUBLANES, CW))],
        out_specs=[_rows(tm, 3 * CW), _full((3, SUBLANES, CW))],
        out_shape=[jax.ShapeDtypeStruct((s, 3 * CW), BF16), jax.ShapeDtypeStruct((3, SUBLANES, CW), F32)],
        compiler_params=_cparams(48, ("arbitrary",)),
    )(dcv, dcv, db, bcu, bcu, cw8)


def _attn_bwd(qp, kp, v, do, lse, dl, mk, *, t):
    s = qp.shape[0]
    nq = s // t

    def body(q_ref, k_ref, v_ref, do_ref, lse_ref, dl_ref, mk_ref, dq_ref, dk_ref, dv_ref, dkx_ref, dq_acc):
        pi = pl.program_id(1)

        @pl.when(pi == 0)
        def _():
            dq_acc[...] = jnp.zeros_like(dq_acc)

        row = lax.broadcasted_iota(jnp.int32, (t, t), 0)
        col = lax.broadcasted_iota(jnp.int32, (t, t), 1)
        lane = lax.broadcasted_iota(jnp.int32, (t, 128), 1)

        def head_step(hh, qi, carry, modes):
            off = pl.multiple_of(qi * t, t)
            rows = pl.ds(off, t)
            q = q_ref[rows, HP * hh:HP * (hh + 1)]
            qt = q.T
            lse_col = lse_ref[rows, DH * hh:DH * hh + 1]
            dl_col = dl_ref[rows, DH * hh:DH * hh + 1]
            do2 = do_ref[rows, :]
            dom = jnp.where(lane < DH, do2 if hh == 0 else pltpu.roll(do2, DH, axis=1), jnp.zeros((), BF16))
            new, dss = [], []
            for half, masked in enumerate(modes):
                if masked is None:
                    new.append(carry[half])
                    continue
                dk, dv, cs = carry[half]
                keys = slice(half * t, (half + 1) * t)
                m_col = mk_ref[half, rows, DH * hh:DH * hh + 1]
                scale = jnp.exp2(m_col - lse_col)
                sc = lax.dot_general(q, k_ref[keys, HP * hh:HP * (hh + 1)], NT, preferred_element_type=F32) - m_col
                if masked:
                    sc = jnp.where(col <= row, sc, -1e30)
                pt = jnp.exp2(sc).astype(BF16)
                dp = lax.dot_general(dom, v_ref[keys, HP * hh:HP * (hh + 1)], NT, preferred_element_type=F32)
                ds32 = (pt.astype(F32) * scale) * (dp - dl_col)
                ds = ds32.astype(BF16)
                cs = cs + _fold8(ds32)
                dv = dv + jnp.dot((dom.astype(F32) * scale).astype(BF16).T, pt, preferred_element_type=F32)
                dk = dk + jnp.dot(qt, ds, preferred_element_type=F32)
                new.append((dk, dv, cs))
                dss.append((half, ds))
            if len(dss) == 2:
                dq = jnp.dot(jnp.concatenate([dss[0][1], dss[1][1]], axis=1), k_ref[:, HP * hh:HP * (hh + 1)],
                             preferred_element_type=F32)
            else:
                half, ds = dss[0]
                dq = jnp.dot(ds, k_ref[half * t:(half + 1) * t, HP * hh:HP * (hh + 1)], preferred_element_type=F32)
            dq_acc[rows, HP * hh:HP * (hh + 1)] += dq
            return tuple(new)

        def step(qi, carry, modes):
            return tuple(head_step(hh, qi, carry[hh], modes) for hh in range(2))

        def two_heads(a0, a1):
            return jnp.where(lane < DH, a0, pltpu.roll(a1, DH, axis=1))

        def rows_to_lanes(a0, a1):
            return jnp.concatenate([a0, a1], axis=0).T

        zero = (jnp.zeros((HP, t), F32), jnp.zeros((128, t), F32), jnp.zeros((SUBLANES, t), F32))
        carry = step(2 * pi, ((zero, zero), (zero, zero)), (True, None))
        carry = step(2 * pi + 1, carry, (False, True))

        def pair(j, carry):
            qi = 2 * (pi + 1 + j)
            return step(qi + 1, step(qi, carry, (False, False)), (False, False))

        carry = lax.fori_loop(0, nq // 2 - 1 - pi, pair, carry)
        for half in range(2):
            keys = slice(half * t, (half + 1) * t)
            (dk0, dv0, cs0), (dk1, dv1, cs1) = carry[0][half], carry[1][half]
            dk_ref[keys, :] = (rows_to_lanes(dk0[0:DH], dk1[0:DH]) * LN2).astype(BF16)
            dv_ref[keys, :] = rows_to_lanes(dv0[0:DH], dv1[0:DH]).astype(BF16)
            total = lambda cs: jnp.broadcast_to(jnp.sum(cs, axis=0, keepdims=True), (DH, t))
            dkx_ref[keys, :] = rows_to_lanes(total(cs0), total(cs1))

        @pl.when(pi == nq // 2 - 1)
        def _():
            for c in range(s // t):
                rows = slice(c * t, (c + 1) * t)
                dq_ref[rows, :] = two_heads(dq_acc[rows, 0:HP], dq_acc[rows, HP:2 * HP]).astype(BF16)

    return pl.pallas_call(
        body, name="attn_bwd", grid=(H // 2, nq // 2),
        in_specs=[pl.BlockSpec((s, 2 * HP), lambda p, i: (0, p)),
                  pl.BlockSpec((2 * t, 2 * HP), lambda p, i: (i, p)),
                  pl.BlockSpec((2 * t, 2 * HP), lambda p, i: (i, p)),
                  pl.BlockSpec((s, 128), lambda p, i: (0, p)),
                  pl.BlockSpec((s, 128), lambda p, i: (0, p)),
                  pl.BlockSpec((s, 128), lambda p, i: (0, p)),
                  pl.BlockSpec((2, s, 128), lambda p, i: (i, 0, p))],
        out_specs=[pl.BlockSpec((s, 128), lambda p, i: (0, p)),
                   pl.BlockSpec((2 * t, 128), lambda p, i: (i, p)),
                   pl.BlockSpec((2 * t, 128), lambda p, i: (i, p)),
                   pl.BlockSpec((2 * t, 128), lambda p, i: (i, p))],
        out_shape=[jax.ShapeDtypeStruct((s, AW), BF16), jax.ShapeDtypeStruct((s, AW), BF16),
                   jax.ShapeDtypeStruct((s, AW), BF16), jax.ShapeDtypeStruct((s, AW), F32)],
        scratch_shapes=[pltpu.VMEM((s, 2 * HP), F32)],
        compiler_params=_cparams(56, ("arbitrary", "arbitrary")),
    )(qp, kp, v, do, lse, dl, mk)


def _forget_bwd(dkx, z, sel, *, tm):
    s = dkx.shape[0]
    nt = s // tm

    def body(dk_ref, z_ref, sel_ref, dfl_ref, dbf_ref, carry):
        @pl.when(pl.program_id(0) == 0)
        def _():
            carry[...] = jnp.zeros_like(carry)
            dbf_ref[...] = jnp.zeros_like(dbf_ref)

        dc = _split_dot(dk_ref[...], sel_ref[...])
        row = lax.broadcasted_iota(jnp.int32, (tm, tm), 0)
        col = lax.broadcasted_iota(jnp.int32, (tm, tm), 1)
        tri = (col >= row).astype(BF16)
        dlogf = _exact_dot01(tri, dc) + carry[0:1, :]
        carry[...] = jnp.broadcast_to(dlogf[0:1, :], carry.shape)
        dz = dlogf * (1.0 - jax.nn.sigmoid(z_ref[...]))
        dfl_ref[:, 0:128] = dz.astype(BF16)
        dfl_ref[:, 128:GW_TILE] = jnp.zeros((tm, GW_TILE - 128), BF16)
        dbf_ref[...] += _fold8(dz)

    rev = lambda i: (nt - 1 - i, 0)
    return pl.pallas_call(
        body, name="forget_bwd", grid=(nt,),
        in_specs=[pl.BlockSpec((tm, AW), rev), pl.BlockSpec((tm, 128), rev), _full((AW, 128))],
        out_specs=[pl.BlockSpec((tm, GW_TILE), rev), _full((SUBLANES, 128))],
        out_shape=[jax.ShapeDtypeStruct((s, GW_TILE), BF16), jax.ShapeDtypeStruct((SUBLANES, 128), F32)],
        scratch_shapes=[pltpu.VMEM((SUBLANES, 128), F32)],
        compiler_params=_cparams(48, ("arbitrary",)),
    )(dkx, z, sel)


def _in_proj_bwd(pieces, wp, x, g1, dx2, after, *, tm):
    s = x.shape[0]

    def body(q_ref, k_ref, v_ref, bcu_ref, f_ref, w_ref, x_ref, g_ref, dx2_ref, after_ref, dx_ref, dg_ref):
        @pl.when(pl.program_id(0) == 0)
        def _():
            dg_ref[...] = jnp.zeros_like(dg_ref)

        dh = None
        for ref, (lo, hi) in zip((q_ref, k_ref, v_ref, bcu_ref, f_ref), PIECES):
            part = lax.dot_general(ref[...], w_ref[:, lo:hi], NT, preferred_element_type=F32)
            dh = part if dh is None else dh + part
        _, n, r = _rms_fwd(x_ref[...], g_ref[...])
        dxn, dg = _rms_bwd(dh, n, r, g_ref[...])
        dx_ref[...] = dx2_ref[...] + dxn
        dg_ref[...] += _fold8(dg)

    return pl.pallas_call(
        body, name="in_proj_bwd", grid=(s // tm,),
        in_specs=[_rows(tm, hi - lo) for lo, hi in PIECES]
        + [_resident((D, WP)), _rows(tm, D), _full((1, D)), _rows(tm, D), ANY],
        out_specs=[_rows(tm, D), _full((SUBLANES, D))],
        out_shape=[jax.ShapeDtypeStruct((s, D), F32), jax.ShapeDtypeStruct((SUBLANES, D), F32)],
        compiler_params=_cparams(56, ("arbitrary",)),
    )(*pieces, wp, x, g1, dx2, after)


def _position():
    return lax.axis_index("x"), lax.axis_index("y"), lax.axis_index("c")


ANY = pl.BlockSpec(memory_space=pl.ANY)


def _all_gather(shards):
    n = len(shards)

    def body(*refs):
        x_refs, out_refs = refs[:n], refs[n:2 * n]
        send_sems, recv_sems, local_sems = refs[2 * n:]
        x, y, c = _position()
        me, sibling = (x, y, c), (x, y, 1 - c)
        chips = [(1 - x, y), (x, 1 - y), (1 - x, 1 - y)]

        def copy(a, k, block, to, own=False):
            slot = out_refs[a].at[4 * block[0] + 2 * block[1] + block[2]]
            return pltpu.make_async_remote_copy(
                src_ref=x_refs[a] if own else slot, dst_ref=slot,
                send_sem=send_sems.at[7 * a + k], recv_sem=recv_sems.at[7 * a + k], device_id=to, device_id_type=MESH_ID)

        mine = [pltpu.make_async_copy(x_refs[a], out_refs[a].at[4 * x + 2 * y + c], local_sems.at[a]) for a in range(n)]
        for cp in mine:
            cp.start()
        first = []
        for a in range(n):
            first.append(copy(a, 0, me, sibling, own=True))
            first += [copy(a, 1 + j, me, (*chip, c), own=True) for j, chip in enumerate(chips)]
        for cp in first:
            cp.start()
        passed = []
        for j, chip in enumerate(chips):
            for a in range(n):
                copy(a, 1 + j, (*chip, c), me).wait_recv()
                fwd = copy(a, 4 + j, (*chip, c), sibling)
                fwd.start()
                passed.append(fwd)
        for a in range(n):
            copy(a, 0, sibling, me).wait_recv()
            for j, chip in enumerate(chips):
                copy(a, 4 + j, (*chip, 1 - c), me).wait_recv()
        for cp in first + passed:
            cp.wait_send()
        for cp in mine:
            cp.wait()

    return pl.pallas_call(
        body, name="all_gather_weights",
        out_shape=[jax.ShapeDtypeStruct((NDEV,) + sh.shape, sh.dtype) for sh in shards],
        in_specs=[ANY] * n, out_specs=[ANY] * n,
        scratch_shapes=[pltpu.SemaphoreType.DMA((7 * n,)), pltpu.SemaphoreType.DMA((7 * n,)), pltpu.SemaphoreType.DMA((n,))],
    )(*shards)


def _pair_exchange(grads):
    n = len(grads)

    def body(*refs):
        g_refs, out_refs = refs[:n], refs[n:2 * n]
        send_sems, recv_sems = refs[2 * n:]
        x, y, c = _position()
        copies = [pltpu.make_async_remote_copy(
            src_ref=g_refs[a].at[:, pl.ds(1 - c, 1)], dst_ref=out_refs[a], send_sem=send_sems.at[a],
            recv_sem=recv_sems.at[a], device_id=(x, y, 1 - c), device_id_type=MESH_ID) for a in range(n)]
        for cp in copies:
            cp.start()
        for cp in copies:
            cp.wait()

    return pl.pallas_call(
        body, name="grad_pair_exchange",
        out_shape=[jax.ShapeDtypeStruct((4, 1) + g.shape[2:], g.dtype) for g in grads],
        in_specs=[ANY] * n, out_specs=[ANY] * n,
        scratch_shapes=[pltpu.SemaphoreType.DMA((n,)), pltpu.SemaphoreType.DMA((n,))],
    )(*grads)


def _pair_sum(g, got, idx, *, tr, name):
    r, c = g.shape[2:]

    def body(idx_ref, g_ref, got_ref, pb_ref, own_ref):
        p = g_ref[0, 0].astype(F32) + got_ref[0, 0].astype(F32)
        pb_ref[0] = p.astype(BF16)

        @pl.when(pl.program_id(1) == idx_ref[1])
        def _():
            own_ref[...] = p

    return pl.pallas_call(
        body, name=name,
        grid_spec=pltpu.PrefetchScalarGridSpec(
            num_scalar_prefetch=1, grid=(r // tr, 4),
            in_specs=[pl.BlockSpec((1, 1, tr, c), lambda i, j, idx: (j, idx[0], i, 0)),
                      pl.BlockSpec((1, 1, tr, c), lambda i, j, idx: (j, 0, i, 0))],
            out_specs=[pl.BlockSpec((1, tr, c), lambda i, j, idx: (j, i, 0)),
                       pl.BlockSpec((tr, c), lambda i, j, idx: (i, 0))]),
        out_shape=[jax.ShapeDtypeStruct((4, r, c), BF16), jax.ShapeDtypeStruct((r, c), F32)],
        compiler_params=_cparams(62, ("arbitrary", "arbitrary")),
    )(idx, g, got)


HBM = pl.BlockSpec(memory_space=pltpu.HBM)
SEM = pl.BlockSpec(memory_space=pltpu.SEMAPHORE)
DATAFLOW = pltpu.SideEffectType.DATAFLOW_SIDE_EFFECTING


PEERS = {"gather": NDEV - 1, "scatter": NDEV - 1, "chips": 3}


def _exchange_copies(src_refs, land_refs, send_sems, recv_sems, mode):
    x, y, c = _position()
    me, my_chip = 4 * x + 2 * y + c, 2 * x + y
    npeers = PEERS[mode]
    copies, own = [], []
    for a, (s_ref, l_ref) in enumerate(zip(src_refs, land_refs)):
        for k in range(npeers):
            if mode == "chips":
                px, py, pc = x ^ ((k + 1) >> 1), y ^ ((k + 1) & 1), c
                src, dst = s_ref.at[2 * px + py], l_ref.at[my_chip]
            else:
                px, py, pc = x ^ ((k + 1) >> 2), y ^ (((k + 1) >> 1) & 1), c ^ ((k + 1) & 1)
                src, dst = (s_ref.at[4 * px + 2 * py + pc] if mode == "scatter" else s_ref), l_ref.at[me]
            copies.append(pltpu.make_async_remote_copy(
                src_ref=src, dst_ref=dst, send_sem=send_sems.at[npeers * a + k], recv_sem=recv_sems.at[npeers * a + k],
                device_id=(px, py, pc), device_id_type=MESH_ID))
        slot = my_chip if mode == "chips" else me
        own.append(pltpu.make_async_copy(s_ref if mode == "gather" else s_ref.at[slot], l_ref.at[slot],
                                         send_sems.at[npeers * len(src_refs) + a]))
    return copies, own


def _exchange_start(srcs, lands, after, *, mode, name):
    n = len(srcs)
    nsem = PEERS[mode] * n

    def body(*refs):
        token = refs[-1]
        copies, own = _exchange_copies(refs[:n], refs[n:2 * n], refs[2 * n + 1], refs[2 * n + 2], mode)
        for cp in copies + own:
            cp.start()
        token[...] = jnp.zeros_like(token)

    arrays = list(srcs) + list(lands)
    outs = pl.pallas_call(
        body, name=name,
        out_shape=(pltpu.SemaphoreType.DMA((nsem + n,)), pltpu.SemaphoreType.DMA((nsem,)),
                   *[pltpu.HBM(a.shape, a.dtype) for a in arrays], jax.ShapeDtypeStruct((SUBLANES, LANES), F32)),
        in_specs=[HBM] * (2 * n) + [ANY],
        out_specs=(SEM, SEM, *[HBM] * (2 * n), pl.BlockSpec(memory_space=pltpu.VMEM)),
        input_output_aliases={i: 2 + i for i in range(2 * n)},
        compiler_params=pltpu.CompilerParams(has_side_effects=DATAFLOW),
    )(*[pltpu.with_memory_space_constraint(a, pltpu.HBM) for a in arrays], after)
    return outs[0], outs[1], outs[2:2 + n], outs[2 + n:2 + 2 * n], outs[-1]


def _exchange_wait(send_sems, recv_sems, srcs, lands, after, *, mode, name):
    n = len(srcs)

    def body(*refs):
        copies, own = _exchange_copies(refs[:n], refs[n:2 * n], refs[2 * n], refs[2 * n + 1], mode)
        for cp in copies:
            cp.wait_send()
            cp.wait_recv()
        for cp in own:
            cp.wait()

    arrays = list(srcs) + list(lands)
    outs = pl.pallas_call(
        body, name=name,
        out_shape=tuple(pltpu.HBM(a.shape, a.dtype) for a in arrays),
        in_specs=[HBM] * (2 * n) + [SEM, SEM, ANY],
        out_specs=tuple([HBM] * (2 * n)),
        input_output_aliases={i: i for i in range(2 * n)},
        compiler_params=pltpu.CompilerParams(has_side_effects=DATAFLOW),
    )(*arrays, send_sems, recv_sems, after)
    return outs[n:]


def _small_pack(parts):
    def body(gmp_ref, gmo_ref, gfp_ref, gfo_ref, ga_ref, gc_ref, dw_ref, bf_ref, loss_ref, out_ref):
        def colsum(v):
            return jnp.sum(v, axis=0, keepdims=True)

        loss = jnp.sum(colsum(loss_ref[...]), axis=1, keepdims=True) * (0.5 / D)
        rows = [colsum(gmp_ref[...]), colsum(gmo_ref[...]), colsum(gfp_ref[...]), colsum(gfo_ref[...]),
                jnp.concatenate([colsum(ga_ref[...]), colsum(gc_ref[...])], axis=1),
                jnp.concatenate([colsum(dw_ref[0]), colsum(dw_ref[1])], axis=1),
                jnp.concatenate([colsum(dw_ref[2]), colsum(bf_ref[...]), jnp.broadcast_to(loss, (1, 128)),
                                 jnp.zeros((1, 256), F32)], axis=1),
                jnp.zeros((1, D), F32)]
        out_ref[...] = jnp.concatenate(rows, axis=0)

    vm = pl.BlockSpec(memory_space=pltpu.VMEM)
    return pl.pallas_call(
        body, name="small_pack", out_shape=jax.ShapeDtypeStruct((SUBLANES, D), F32),
        in_specs=[vm] * len(parts), out_specs=vm,
    )(*parts)


def _small_sum(land):
    def body(land_ref, out_ref):
        acc = land_ref[0]
        for d in range(1, NDEV):
            acc = acc + land_ref[d]
        out_ref[...] = acc

    return pl.pallas_call(
        body, name="small_sum", grid=(1,), out_shape=jax.ShapeDtypeStruct((SUBLANES, D), F32),
        in_specs=[pl.BlockSpec((NDEV, SUBLANES, D), lambda i: (0, 0, 0))],
        out_specs=pl.BlockSpec((SUBLANES, D), lambda i: (0, 0)),
    )(land)


def _adam_update(w, g, m, v):
    nm = ADAM_B1 * m + (1.0 - ADAM_B1) * g
    nv = ADAM_B2 * v + (1.0 - ADAM_B2) * (g * g)
    m_hat = nm / (1.0 - ADAM_B1 ** ADAM_STEP)
    v_hat = nv / (1.0 - ADAM_B2 ** ADAM_STEP)
    return -ADAM_LR * (m_hat / (jnp.sqrt(v_hat) + ADAM_EPS) + ADAM_WD * w), nm, nv


SMALL_SLOTS = {"g_mix_pre": (0, 0, D), "g_mix_post": (1, 0, D), "g_ffn_pre": (2, 0, D), "g_ffn_post": (3, 0, D),
               "g_attn_out": (4, 0, AW), "g_conv_out": (4, AW, CW), "b_forget": (6, CW, H)}
LOSS_LANE = CW + 128


def _small_adamw(small, conv_grad, params):
    names = list(params)
    n = len(names)

    def body(*refs):
        small_ref, cg_ref = refs[0], refs[1]
        ins, outs = refs[2:2 + 3 * n], refs[2 + 3 * n:]
        for i, name in enumerate(names):
            w_ref, m_ref, v_ref = ins[3 * i:3 * i + 3]
            g_ref, d_ref, nm_ref, nv_ref = outs[4 * i:4 * i + 4]
            if name == "conv_w":
                g = cg_ref[...]
            else:
                r, c0, width = SMALL_SLOTS[name]
                g = small_ref[r:r + 1, c0:c0 + width]
            g_ref[...] = g
            d_ref[...], nm_ref[...], nv_ref[...] = _adam_update(w_ref[...], g, m_ref[...], v_ref[...])
        outs[4 * n][...] = small_ref[6:7, LOSS_LANE:LOSS_LANE + 1]

    vm = pl.BlockSpec(memory_space=pltpu.VMEM)
    flat = [a for name in names for a in params[name]]
    outs = pl.pallas_call(
        body, name="adamw_small",
        in_specs=[vm] * (2 + 3 * n), out_specs=[vm] * (4 * n + 1),
        out_shape=[jax.ShapeDtypeStruct(params[name][0].shape, F32) for name in names for _ in range(4)]
        + [jax.ShapeDtypeStruct((1, 1), F32)],
    )(small, conv_grad, *flat)
    return {name: outs[4 * i:4 * i + 4] for i, name in enumerate(names)}, outs[4 * n].reshape(())


def _chip_sum_adamw(got, own, idx, wt, mt, vt, *, tr, name):
    cols, rows = wt.shape
    gcols = own.shape[1]

    def body(idx_ref, got_ref, own_ref, w_ref, m_ref, v_ref, g_ref, d_ref, nm_ref, nv_ref):
        g = jnp.zeros((tr, gcols), F32)
        for j in range(4):
            g = g + jnp.where(idx_ref[1] == j, own_ref[...], got_ref[j].astype(F32))
        g = g.T[:cols]
        g_ref[...] = g
        d_ref[...], nm_ref[...], nv_ref[...] = _adam_update(w_ref[...], g, m_ref[...], v_ref[...])

    spec = pl.BlockSpec((cols, tr), lambda i, idx: (0, i))
    gspec = pl.BlockSpec((tr, gcols), lambda i, idx: (i, 0))
    return pl.pallas_call(
        body, name=name,
        grid_spec=pltpu.PrefetchScalarGridSpec(
            num_scalar_prefetch=1, grid=(rows // tr,),
            in_specs=[pl.BlockSpec((4, tr, gcols), lambda i, idx: (0, i, 0)), gspec, spec, spec, spec],
            out_specs=[spec] * 4),
        out_shape=[jax.ShapeDtypeStruct((cols, rows), F32)] * 4,
        compiler_params=_cparams(32, ("arbitrary",)),
    )(idx, got, own, wt, mt, vt)


def _device_sum_adamw(land, w, m, v, *, tr, name):
    rows, cols = w.shape

    def body(land_ref, w_ref, m_ref, v_ref, g_ref, d_ref, nm_ref, nv_ref):
        g = land_ref[0].astype(F32)
        for dev in range(1, NDEV):
            g = g + land_ref[dev].astype(F32)
        g_ref[...] = g
        d_ref[...], nm_ref[...], nv_ref[...] = _adam_update(w_ref[...], g, m_ref[...], v_ref[...])

    spec = pl.BlockSpec((tr, cols), lambda i: (i, 0))
    return pl.pallas_call(
        body, name=name, grid=(rows // tr,),
        in_specs=[pl.BlockSpec((NDEV, tr, cols), lambda i: (0, i, 0)), spec, spec, spec],
        out_specs=[spec] * 4,
        out_shape=[jax.ShapeDtypeStruct((rows, cols), F32)] * 4,
        compiler_params=_cparams(32, ("arbitrary",)),
    )(land, w, m, v)


def _placement_constants():
    j = np.arange(128)[:, None]
    lane = np.arange(1024)[None, :]
    head, sub = lane // HP, lane % HP
    piece, jh = j // H, j % H
    valid = (j < 3 * H) & (jh == head)
    pq = np.where(valid & (sub == DH + piece), 1.0, 0.0).astype(BF16)
    pk = np.where(valid & (sub == DH + 3 + piece), -1.0, 0.0).astype(BF16)
    oq = np.where((sub >= DH + 3) & (sub < DH + 6), 1.0, 0.0).astype(np.float32)
    ok = np.where((sub >= DH) & (sub < DH + 3), 1.0, 0.0).astype(np.float32)
    r = np.arange(AW)[:, None]
    cc = np.arange(128)[None, :]
    sel = np.where((r % DH == 3) & (r // DH == cc), -1.0, 0.0).astype(BF16)
    gi = np.arange(GS)
    gsum = (gi[:, None] // DH == gi[None, :] // DH).astype(BF16)
    return tuple(jnp.asarray(c) for c in (pq, pk, oq, ok, sel, gsum))


def _local_step(xs, tgt, wp, late_weights, cw8, bfp, g_attn_out, g_conv_out,
                g_mix_pre, g_mix_post, g_ffn_pre, g_ffn_post, early_grads=None, last_grad=None):
    pq, pk, oq, ok, sel, gsum = _placement_constants()
    h1t, qp, kp, vv, bcu, zf = _in_proj(xs, g_mix_pre, wp, bfp, pq, pk, oq, ok, tm=512)
    o, lse, mk = _attn_fwd(qp, kp, vv, t=512)
    w_out_f, wgu, wd = late_weights(lse)
    merged, y, x2, cv, h2 = _mix_out(o, bcu, cw8, g_attn_out, g_conv_out, gsum, w_out_f, xs, g_mix_post, g_ffn_pre, tm=512)
    gate, up, act, dx3, dff, loss_p, dg_ffn_post = _ffn_fwd_loss(h2, wgu, wd, x2, tgt, g_ffn_post, tm=512)

    dgu, dx2, dy, dg_ffn_pre, dg_mix_post = _ffn_bwd(dff, wd, gate, up, wgu, x2, g_ffn_pre, dx3, y, g_mix_post, tm=256)
    dw_down = _grad_matmul(act, dff, ta=DFF // 2, tb=D, ts=4096, name="grad_w_down", vmem_mb=60)
    dw_gu = _grad_matmul(dgu, h2, ta=DFF // 2, tb=D, ts=4096, name="grad_w_gate_up", vmem_mb=60).reshape(NDEV, FB, D)
    dw_out = _grad_matmul(merged, dy, ta=1024, tb=1024, ts=2048, name="grad_w_out")
    token = early_grads(dw_out, dw_gu, dw_down) if early_grads is not None else dw_out
    do, dl, dcv, db, dg_attn, dg_conv = _mix_bwd(dy, w_out_f, o, cv, bcu, g_attn_out, g_conv_out, gsum, token, tm=512)
    dbcu, dtaps = _conv_bwd(dcv, db, bcu, cw8, tm=512)
    dqp, dkp, dv, dkx = _attn_bwd(qp, kp, vv, do, lse, dl, mk, t=512)
    dfl, dbf = _forget_bwd(dkx, zf, sel, tm=512)
    pieces = (dqp, dkp, dv, dbcu, dfl)
    dwp = _grad_w_in(h1t, pieces)
    token = last_grad(dwp) if last_grad is not None else dwp
    grad_x, dg_mix_pre = _in_proj_bwd(pieces, wp, xs, g_mix_pre, dx2, token, tm=512)
    return (grad_x, dwp, dw_out, dw_gu, dw_down, dg_mix_pre, dg_mix_post, dg_ffn_pre, dg_ffn_post, dg_attn, dg_conv,
            dtaps, dbf, loss_p)


BIG_TILES = {"w_in": 256, "w_out": 128, "w_gate_up": 176, "w_down": 176}


def kernel(x, w_in, b_forget, conv_w, g_attn_out, g_conv_out, w_out, g_mix_pre, g_mix_post, w_gate_up, w_down, g_ffn_pre, g_ffn_post, loss_target, m_w_in, m_b_forget, m_conv_w, m_g_attn_out, m_g_conv_out, m_w_out, m_g_mix_pre, m_g_mix_post, m_w_gate_up, m_w_down, m_g_ffn_pre, m_g_ffn_post, v_w_in, v_b_forget, v_conv_w, v_g_attn_out, v_g_conv_out, v_w_out, v_g_mix_pre, v_g_mix_post, v_w_gate_up, v_w_down, v_g_ffn_pre, v_g_ffn_post):
    xc, yc, cc = _position()
    my_chip = 2 * xc + yc
    me = 2 * my_chip + cc
    idx = jnp.stack([cc, my_chip]).astype(jnp.int32)
    tables = _in_layout_tables()

    w_in_b = w_in[0].astype(BF16)
    g_in, g_last, g_taps = _all_gather([w_in_b[:, :IN_MAIN], w_in_b[:, IN_MAIN].reshape(SUBLANES, LANES), conv_w[0]])
    last_cols = jnp.pad(g_last.reshape(NDEV, D).T.astype(F32), ((0, 0), (0, LANES - NDEV)))
    wp = _assemble_w_in(g_in, last_cols, tables, tr=256)
    cw8 = jnp.pad(g_taps.transpose(1, 0, 2).reshape(3, CW), ((0, SUBLANES - 3), (0, 0)))

    late = [w_out[0].astype(BF16), w_gate_up[0].T.astype(BF16), w_down[0].astype(BF16)]
    ssem, rsem, late_thru, land_thru, token = _exchange_start(
        late, [lax.empty((NDEV,) + s.shape, s.dtype) for s in late], g_in, mode="gather",
        name="gather_late_start")
    bfp = jnp.pad(b_forget, ((0, 0), (0, 128 - H))) + token[0:1, :]

    def late_weights(after):
        l_out, l_gu, l_down = _exchange_wait(ssem, rsem, late_thru, land_thru, after, mode="gather", name="gather_late_wait")
        return l_out.reshape(D, D), l_gu.reshape(2, DFF, D), l_down.reshape(DFF, D)

    early = {}

    def early_grads(dw_out, dw_gu, dw_down):
        srcs = [dw_out.reshape(NDEV, D // NDEV, D), dw_gu, dw_down.reshape(NDEV, DFF // NDEV, D)]
        lands = [lax.empty(s.shape, s.dtype) for s in srcs]
        early["handles"] = _exchange_start(srcs, lands, dw_out, mode="scatter", name="scatter_early_start")
        return early["handles"][4]

    last = {}

    def last_grad(dwp):
        g_w_in = _disassemble_w_in(dwp, tables, tr=256).reshape(4, 2, D, IN_PAD)
        (from_sibling,) = _pair_exchange([g_w_in])
        pair_b, last["own"] = _pair_sum(g_w_in, from_sibling, idx, tr=D, name="grad_pair_sum_w_in")
        last["handles"] = _exchange_start([pair_b], [lax.empty(pair_b.shape, pair_b.dtype)], last["own"], mode="chips",
                                          name="chips_w_in_start")
        return last["handles"][4]

    (grad_x, dwp, dw_out, dw_gu, dw_down, dg_mix_pre, dg_mix_post, dg_ffn_pre, dg_ffn_post, dg_attn, dg_conv,
     dtaps, dbf, loss_p) = _local_step(x[0], loss_target[0], wp, late_weights, cw8, bfp, g_attn_out, g_conv_out,
                                        g_mix_pre, g_mix_post, g_ffn_pre, g_ffn_post, early_grads, last_grad)

    share = _small_pack([dg_mix_pre, dg_mix_post, dg_ffn_pre, dg_ffn_post, dg_attn, dg_conv, dtaps, dbf, loss_p])
    s_ssem, s_rsem, s_srcs, s_lands, s_token = _exchange_start(
        [share], [lax.empty((NDEV, SUBLANES, D), F32)], share, mode="gather", name="small_gather_start")

    e_ssem, e_rsem, e_srcs, e_lands, _ = early["handles"]
    land_out, land_gu, land_down = _exchange_wait(e_ssem, e_rsem, e_srcs, e_lands, s_token, mode="scatter",
                                                  name="scatter_early_wait")
    res = {}
    big = {"w_out": (land_out, w_out[0], m_w_out[0], v_w_out[0]),
           "w_gate_up": (land_gu, w_gate_up[0].T, m_w_gate_up[0].T, v_w_gate_up[0].T),
           "w_down": (land_down, w_down[0], m_w_down[0], v_w_down[0])}
    for name, (land, w, m, v) in big.items():
        outs = _device_sum_adamw(land, w, m, v, tr=BIG_TILES[name], name="adamw_" + name)
        res[name] = [(o.T if name == "w_gate_up" else o)[None] for o in outs]
    c_ssem, c_rsem, c_srcs, c_lands, _ = last["handles"]
    after = sum(res[n][1][0, :SUBLANES, :LANES] for n in big)
    (from_chips,) = _exchange_wait(c_ssem, c_rsem, c_srcs, c_lands, after, mode="chips", name="chips_w_in_wait")
    outs = _chip_sum_adamw(from_chips, last["own"], idx, w_in[0].T, m_w_in[0].T, v_w_in[0].T,
                           tr=BIG_TILES["w_in"], name="adamw_w_in")
    res["w_in"] = [o.T[None] for o in outs]
    w_in_done = outs[1][:SUBLANES, :LANES]

    (land_small,) = _exchange_wait(s_ssem, s_rsem, s_srcs, s_lands, w_in_done, mode="gather", name="small_gather_wait")
    small = _small_sum(land_small)
    taps_full = jnp.concatenate([small[5:6, :CW], small[5:6, CW:], small[6:7, :CW]], axis=0)
    taps_first = lambda a: a.transpose(1, 0, 2)
    smalls = {"b_forget": (b_forget, m_b_forget, v_b_forget),
              "conv_w": (taps_first(conv_w), taps_first(m_conv_w), taps_first(v_conv_w)),
              "g_attn_out": (g_attn_out, m_g_attn_out, v_g_attn_out), "g_conv_out": (g_conv_out, m_g_conv_out, v_g_conv_out),
              "g_mix_pre": (g_mix_pre, m_g_mix_pre, v_g_mix_pre), "g_mix_post": (g_mix_post, m_g_mix_post, v_g_mix_post),
              "g_ffn_pre": (g_ffn_pre, m_g_ffn_pre, v_g_ffn_pre), "g_ffn_post": (g_ffn_post, m_g_ffn_post, v_g_ffn_post)}
    own_taps = lax.dynamic_slice(taps_full, (0, me * 64), (3, 64))[:, None, :]
    small_res, loss = _small_adamw(small, own_taps, smalls)
    for name, outs in small_res.items():
        res[name] = [taps_first(o) for o in outs] if name == "conv_w" else list(outs)

    order = ["w_in", "b_forget", "conv_w", "g_attn_out", "g_conv_out", "w_out", "g_mix_pre", "g_mix_post",
             "w_gate_up", "w_down", "g_ffn_pre", "g_ffn_post"]
    outs = [loss, grad_x[None]]
    for k in range(4):
        outs += [res[n][k] for n in order]
    return tuple(outs)
```

```python
import functools

import numpy as np

import jax
import jax.numpy as jnp
from jax import lax
from jax.experimental import pallas as pl
from jax.experimental.pallas import tpu as pltpu

F32 = jnp.float32
BF16 = jnp.bfloat16
MESH_ID = pl.DeviceIdType.MESH

D = 1024
H = 8
DH = 64
AW = 512
CW = 512
DFF = 2816
FB = DFF // 4
FF_CHUNKS = ((0, 768), (768, 768), (1536, 768), (2304, 512))
FF_CHUNKS_BWD = ((0, 1024), (1024, 1024), (2048, 768))
HP = 128
OFF_Q, OFF_K, OFF_V, OFF_BCU, OFF_F = 0, 512, 1024, 1536, 3072
WP = OFF_F + 128
PIECES = ((OFF_Q, OFF_K), (OFF_K, OFF_V), (OFF_V, OFF_BCU), (OFF_BCU, OFF_F), (OFF_F, WP))
EPS = 1e-6
LOG2E, LN2 = 1.4426950408889634, 0.6931471805599453
NDEV = 8
LANES = 128
SUBLANES = 8
IN_COLS = 385
IN_PAD = 512
IN_MAIN = 384
WIN = 640
ADAM_LR, ADAM_B1, ADAM_B2, ADAM_EPS, ADAM_WD, ADAM_STEP = 0.001, 0.9, 0.999, 1e-08, 0.01, 10

NT = (((1,), (1,)), ((), ()))
TN = (((0,), (0,)), ((), ()))


def _cparams(vmem_mb=None, sem=None):
    kw = {}
    if vmem_mb is not None:
        kw["vmem_limit_bytes"] = vmem_mb << 20
    if sem is not None:
        kw["dimension_semantics"] = sem
    return pltpu.CompilerParams(**kw)


def _full(shape):
    return pl.BlockSpec(shape, lambda *_: (0,) * len(shape))


def _resident(shape):
    return pl.BlockSpec(shape, lambda *_: (0,) * len(shape), pipeline_mode=pl.Buffered(1))


def _rows(tm, width):
    return pl.BlockSpec((tm, width), lambda i: (i, 0))


def _fold8(v):
    r, w = v.shape
    return jnp.sum(v.reshape(r // SUBLANES, SUBLANES, w), axis=0)


def _split_dot(v, m01):
    hi = v.astype(BF16)
    lo = (v - hi.astype(F32)).astype(BF16)
    return (jnp.dot(hi, m01, preferred_element_type=F32)
            + jnp.dot(lo, m01, preferred_element_type=F32))


GS = 256


def _group_sum(v, g01):
    parts = [_split_dot(v[:, c:c + GS], g01) for c in range(0, v.shape[1], GS)]
    return parts[0] if len(parts) == 1 else jnp.concatenate(parts, axis=1)


def _exact_dot01(m01, v):
    p1 = v.astype(BF16)
    r1 = v - p1.astype(F32)
    p2 = r1.astype(BF16)
    p3 = (r1 - p2.astype(F32)).astype(BF16)
    return (jnp.dot(m01, p1, preferred_element_type=F32) + jnp.dot(m01, p2, preferred_element_type=F32)
            + jnp.dot(m01, p3, preferred_element_type=F32))


def _rms_fwd(v, g):
    r = lax.rsqrt(jnp.mean(v * v, axis=-1, keepdims=True) + EPS)
    n = v * r
    return n * g, n, r


def _rms_bwd(do, n, r, g):
    dn = do * g
    return r * (dn - n * jnp.mean(dn * n, axis=-1, keepdims=True)), do * n


def _padded_column(n):
    if n < AW:
        return OFF_Q + n, 0.125
    if n < 3 * AW:
        return n, 1.0
    if n < 3 * AW + H:
        return OFF_F + n - 3 * AW, 1.0
    return OFF_BCU + n - 3 * AW - H, 1.0


def _in_layout_tables():
    dest = -np.ones((IN_PAD, LANES), np.int32)
    dest_f = -np.ones((IN_PAD, LANES), np.int32)
    scale = np.zeros((IN_PAD, LANES), np.float32)
    starts = []
    for k in range(NDEV):
        cols = [_padded_column(IN_COLS * k + j) for j in range(IN_COLS)]
        main = [c for c, _ in cols if c < OFF_F]
        ws = min((min(main) // LANES) * LANES, OFF_F - WIN)
        assert ws <= min(main) and max(main) < ws + WIN
        starts.append(ws)
        for j, (c, sc) in enumerate(cols):
            scale[j, k] = sc
            if c < OFF_F:
                dest[j, k] = c - ws
            else:
                dest_f[j, k] = c - OFF_F
    f_shards = tuple(k for k in range(NDEV) if (dest_f[:, k] >= 0).any())
    return tuple(starts), f_shards, jnp.asarray(dest), jnp.asarray(dest_f), jnp.asarray(scale)


def _perm(dest_ref, scale_ref, k, width, rows=IN_PAD):
    lane = lax.broadcasted_iota(jnp.int32, (rows, width), 1)
    return jnp.where(dest_ref[0:rows, k:k + 1] == lane, scale_ref[0:rows, k:k + 1], 0.0).astype(BF16)


def _assemble_w_in(blocks, last_cols, tables, *, tr):
    starts, f_shards, dest, dest_f, scale = tables
    last = [_padded_column(IN_COLS * k + IN_MAIN) for k in range(NDEV)]
    f_main = [any(_padded_column(IN_COLS * k + j)[0] >= OFF_F for j in range(IN_MAIN)) for k in range(NDEV)]
    assert IN_COLS == IN_MAIN + 1

    def body(b_ref, c_ref, dest_ref, destf_ref, scale_ref, o_ref):
        o_ref[...] = jnp.zeros_like(o_ref)
        lane = lax.broadcasted_iota(jnp.int32, (tr, LANES), 1)
        for k in range(NDEV):
            b = b_ref[k]
            ws = starts[k]
            part = jnp.dot(b, _perm(dest_ref, scale_ref, k, WIN, IN_MAIN), preferred_element_type=F32)
            o_ref[:, ws:ws + WIN] += part.astype(BF16)
            if f_main[k]:
                part = jnp.dot(b, _perm(destf_ref, scale_ref, k, 128, IN_MAIN), preferred_element_type=F32)
                o_ref[:, OFF_F:WP] += part.astype(BF16)
            col, sc = last[k]
            tile = (col // LANES) * LANES
            o_ref[:, tile:tile + LANES] += jnp.where(lane == col - tile, c_ref[:, k:k + 1] * sc, 0.0).astype(BF16)

    tab = _full((IN_PAD, LANES))
    return pl.pallas_call(
        body, name="assemble_w_in", grid=(D // tr,),
        in_specs=[pl.BlockSpec((NDEV, tr, IN_MAIN), lambda i: (0, i, 0)), _rows(tr, LANES), tab, tab, tab],
        out_specs=_rows(tr, WP),
        out_shape=jax.ShapeDtypeStruct((D, WP), BF16),
        compiler_params=_cparams(48, ("arbitrary",)),
    )(blocks, last_cols, dest, dest_f, scale)


def _disassemble_w_in(dwp, tables, *, tr):
    starts, f_shards, dest, dest_f, scale = tables
    width = dwp.shape[1]

    def body(g_ref, dest_ref, destf_ref, scale_ref, o_ref):
        for k in range(NDEV):
            ws = starts[k]
            acc = lax.dot_general(g_ref[:, ws:ws + WIN], _perm(dest_ref, scale_ref, k, WIN), NT, preferred_element_type=F32)
            if k in f_shards:
                acc = acc + lax.dot_general(g_ref[:, OFF_F:WP], _perm(destf_ref, scale_ref, k, 128), NT,
                                            preferred_element_type=F32)
            o_ref[k] = acc.astype(BF16)

    tab = _full((IN_PAD, LANES))
    return pl.pallas_call(
        body, name="disassemble_w_in", grid=(D // tr,),
        in_specs=[_rows(tr, width), tab, tab, tab],
        out_specs=pl.BlockSpec((NDEV, tr, IN_PAD), lambda i: (0, i, 0)),
        out_shape=jax.ShapeDtypeStruct((NDEV, D, IN_PAD), BF16),
        compiler_params=_cparams(48, ("arbitrary",)),
    )(dwp, dest, dest_f, scale)


def _in_proj(x, g1, wp, bfp, pq, pk, oq, ok, *, tm):
    s = x.shape[0]

    def body(x_ref, g_ref, w_ref, bf_ref, pq_ref, pk_ref, oq_ref, ok_ref,
             ht_ref, qp_ref, kp_ref, v_ref, bcu_ref, z_ref, carry):
        @pl.when(pl.program_id(0) == 0)
        def _():
            carry[...] = jnp.zeros_like(carry)

        h = _rms_fwd(x_ref[...], g_ref[...])[0].astype(BF16)
        ht_ref[...] = h.T
        z = jnp.dot(h, w_ref[:, OFF_F:WP], preferred_element_type=F32) + bf_ref[...]
        z_ref[...] = z
        lane = lax.broadcasted_iota(jnp.int32, (tm, 128), 1)
        logf = jnp.where(lane < H, jnp.minimum(z, 0.0) - jnp.log(1.0 + jnp.exp(-jnp.abs(z))), 0.0)
        row = lax.broadcasted_iota(jnp.int32, (tm, tm), 0)
        col = lax.broadcasted_iota(jnp.int32, (tm, tm), 1)
        tri = (col <= row).astype(BF16)
        c = _exact_dot01(tri, logf) + carry[0:1, :]
        carry[...] = jnp.broadcast_to(c[tm - 1:tm, :], carry.shape)
        cb = c * LOG2E
        c1 = cb.astype(BF16).astype(F32)
        r1 = cb - c1
        c2 = r1.astype(BF16).astype(F32)
        c3 = (r1 - c2).astype(BF16).astype(F32)
        zc = (c1 + pltpu.roll(c2, 8, axis=1) + pltpu.roll(c3, 16, axis=1)).astype(BF16)

        def pad_heads(v):
            blocks = []
            for pair in range(H // 2):
                two = v[:, 128 * pair:128 * (pair + 1)]
                blocks.append(jnp.where(lane < DH, two, 0.0))
                blocks.append(jnp.where(lane < DH, pltpu.roll(two, DH, axis=1), 0.0))
            return jnp.concatenate(blocks, axis=1)

        q = jnp.dot(h, w_ref[:, OFF_Q:OFF_K], preferred_element_type=F32) * LOG2E
        qp_ref[...] = (pad_heads(q) + jnp.dot(zc, pq_ref[...], preferred_element_type=F32) + oq_ref[...]).astype(BF16)
        k = jnp.dot(h, w_ref[:, OFF_K:OFF_V], preferred_element_type=F32)
        kp_ref[...] = (pad_heads(k) + jnp.dot(zc, pk_ref[...], preferred_element_type=F32) + ok_ref[...]).astype(BF16)
        v = pad_heads(jnp.dot(h, w_ref[:, OFF_V:OFF_BCU], preferred_element_type=F32))
        ones_lane = lax.broadcasted_iota(jnp.int32, (tm, H * HP), 1) % HP == DH
        v_ref[...] = jnp.where(ones_lane, 1.0, v).astype(BF16)
        bcu_ref[...] = jnp.dot(h, w_ref[:, OFF_BCU:OFF_F], preferred_element_type=F32).astype(BF16)

    return pl.pallas_call(
        body, name="in_proj", grid=(s // tm,),
        in_specs=[_rows(tm, D), _full((1, D)), _resident((D, WP)), _full((1, 128)),
                  _full((128, 1024)), _full((128, 1024)), _full((1, 1024)), _full((1, 1024))],
        out_specs=[pl.BlockSpec((D, tm), lambda i: (0, i)), _rows(tm, 1024), _rows(tm, 1024), _rows(tm, 1024),
                   _rows(tm, 3 * CW), _rows(tm, 128)],
        out_shape=[jax.ShapeDtypeStruct((D, s), BF16), jax.ShapeDtypeStruct((s, 1024), BF16),
                   jax.ShapeDtypeStruct((s, 1024), BF16), jax.ShapeDtypeStruct((s, 1024), BF16),
                   jax.ShapeDtypeStruct((s, 3 * CW), BF16), jax.ShapeDtypeStruct((s, 128), F32)],
        scratch_shapes=[pltpu.VMEM((SUBLANES, 128), F32)],
        compiler_params=_cparams(56, ("arbitrary",)),
    )(x, g1, wp, bfp, pq, pk, oq, ok)


def _attn_fwd(qp, kp, v, *, t):
    s = qp.shape[0]
    nq = s // t

    def body(q_ref, k_ref, v_ref, o_ref, lse_ref, mk_ref):
        pi = pl.program_id(1)
        row = lax.broadcasted_iota(jnp.int32, (t, t), 0)
        col = lax.broadcasted_iota(jnp.int32, (t, t), 1)
        lane = lax.broadcasted_iota(jnp.int32, (t, 128), 1)

        def head_step(hh, rows, ki, carry, masked):
            m, acc = carry
            off = pl.multiple_of(ki * t, t)
            q = q_ref[rows, HP * hh:HP * (hh + 1)]
            k = k_ref[pl.ds(off, t), HP * hh:HP * (hh + 1)]
            sc = lax.dot_general(q, k, NT, preferred_element_type=F32)
            if masked:
                sc = jnp.where(col <= row, sc, -1e30)
            mn = jnp.maximum(m, jnp.max(sc, axis=-1, keepdims=True))
            p = jnp.exp2(sc - mn).astype(BF16)
            acc = jnp.exp2(m - mn) * acc + jnp.dot(p, v_ref[pl.ds(off, t), HP * hh:HP * (hh + 1)],
                                                  preferred_element_type=F32)
            return mn, acc

        def step(rows, ki, carry, masked):
            new = tuple(head_step(hh, rows, ki, carry[hh], masked) for hh in range(2))
            mk_ref[ki, rows] = jnp.where(lane < DH, jnp.broadcast_to(new[0][0], (t, 128)),
                                         jnp.broadcast_to(new[1][0], (t, 128)))
            return new

        init = (jnp.full((t, 1), -1e30, F32), jnp.zeros((t, 128), F32))
        top, bottom = slice(0, t), slice(t, 2 * t)

        def quad(j, carry):
            c0, c1 = carry
            c0 = step(top, 2 * j, c0, False)
            c1 = step(bottom, 2 * j, c1, False)
            c0 = step(top, 2 * j + 1, c0, False)
            c1 = step(bottom, 2 * j + 1, c1, False)
            return c0, c1

        c0, c1 = lax.fori_loop(0, pi, quad, ((init, init), (init, init)))
        f0 = step(top, 2 * pi, c0, True)
        c1 = step(bottom, 2 * pi, c1, False)
        f1 = step(bottom, 2 * pi + 1, c1, True)
        for rows, ((m0, acc0), (m1, acc1)) in ((top, f0), (bottom, f1)):
            l0, l1 = acc0[:, DH:DH + 1], acc1[:, DH:DH + 1]
            o_ref[rows, :] = jnp.where(lane < DH, acc0 / l0, pltpu.roll(acc1 / l1, DH, axis=1))
            lse_ref[rows, :] = jnp.where(lane < DH, jnp.broadcast_to(m0 + jnp.log2(l0), (t, 128)),
                                         jnp.broadcast_to(m1 + jnp.log2(l1), (t, 128)))

    return pl.pallas_call(
        body, name="attn_fwd", grid=(H // 2, nq // 2),
        in_specs=[pl.BlockSpec((2 * t, 2 * HP), lambda p, i: (i, p)),
                  pl.BlockSpec((s, 2 * HP), lambda p, i: (0, p)),
                  pl.BlockSpec((s, 2 * HP), lambda p, i: (0, p))],
        out_specs=[pl.BlockSpec((2 * t, 128), lambda p, i: (i, p)), pl.BlockSpec((2 * t, 128), lambda p, i: (i, p)),
                   pl.BlockSpec((nq, 2 * t, 128), lambda p, i: (0, i, p))],
        out_shape=[jax.ShapeDtypeStruct((s, AW), F32), jax.ShapeDtypeStruct((s, AW), F32),
                   jax.ShapeDtypeStruct((nq, s, AW), F32)],
        compiler_params=_cparams(48, ("arbitrary", "arbitrary")),
    )(qp, kp, v)


HALO = 16


def _conv_taps(bcu_ref, halo_ref, first, tm):
    z = bcu_ref[:, CW:2 * CW].astype(F32) * bcu_ref[:, 2 * CW:3 * CW].astype(F32)
    zh = jnp.where(first, 0.0, halo_ref[:, CW:2 * CW].astype(F32) * halo_ref[:, 2 * CW:3 * CW].astype(F32))
    row = lax.broadcasted_iota(jnp.int32, (tm, CW), 0)
    last, before = zh[HALO - 1:HALO, :], zh[HALO - 2:HALO - 1, :]
    z1 = jnp.where(row == 0, last, pltpu.roll(z, 1, axis=0))
    z2 = jnp.where(row == 0, before, jnp.where(row == 1, last, pltpu.roll(z, 2, axis=0)))
    return z, z1, z2


def _halo_before(tm, width):
    return pl.BlockSpec((HALO, width), lambda i: (jnp.maximum(i * (tm // HALO) - 1, 0), 0))


def _mix_out(o, bcu, cw8, ga, gc, gsum, w_out, x, g_post, g_ffn_pre, *, tm):
    s = x.shape[0]

    def body(o_ref, bcu_ref, halo_ref, cw_ref, ga_ref, gc_ref, gs_ref, w_ref, x_ref, g_ref, gf_ref,
             merged_ref, y_ref, x2_ref, cv_ref, h2_ref):
        z, z1, z2 = _conv_taps(bcu_ref, halo_ref, pl.program_id(0) == 0, tm)
        cv = cw_ref[0:1, :] * z2 + cw_ref[1:2, :] * z1 + cw_ref[2:3, :] * z
        cv_ref[...] = cv
        conv = bcu_ref[:, 0:CW].astype(F32) * cv
        ov = o_ref[...]
        ra = lax.rsqrt(_group_sum(ov * ov, gs_ref[...]) * (1.0 / DH) + EPS)
        rc = lax.rsqrt(_group_sum(conv * conv, gs_ref[...]) * (1.0 / DH) + EPS)
        merged = jnp.concatenate([ov * ra * ga_ref[...], conv * rc * gc_ref[...]], axis=1).astype(BF16)
        merged_ref[...] = merged
        y = jnp.dot(merged, w_ref[...], preferred_element_type=F32)
        y_ref[...] = y
        x2 = x_ref[...] + _rms_fwd(y, g_ref[...])[0]
        x2_ref[...] = x2
        h2_ref[...] = _rms_fwd(x2, gf_ref[...])[0].astype(BF16)

    return pl.pallas_call(
        body, name="mix_out", grid=(s // tm,),
        in_specs=[_rows(tm, AW), _rows(tm, 3 * CW), _halo_before(tm, 3 * CW), _full((SUBLANES, CW)),
                  _full((1, AW)), _full((1, CW)), _full((GS, GS)), _resident((D, D)), _rows(tm, D), _full((1, D)),
                  _full((1, D))],
        out_specs=[_rows(tm, D), _rows(tm, D), _rows(tm, D), _rows(tm, CW), _rows(tm, D)],
        out_shape=[jax.ShapeDtypeStruct((s, D), BF16), jax.ShapeDtypeStruct((s, D), F32),
                   jax.ShapeDtypeStruct((s, D), F32), jax.ShapeDtypeStruct((s, CW), F32),
                   jax.ShapeDtypeStruct((s, D), BF16)],
        compiler_params=_cparams(48, ("arbitrary",)),
    )(o, bcu, bcu, cw8, ga, gc, gsum, w_out, x, g_post, g_ffn_pre)


def _ffn_fwd_loss(h2, wgu, wd, x2, target, g_post, *, tm):
    s = x2.shape[0]

    def body(h_ref, w_ref, wd_ref, x2_ref, t_ref, g_ref,
             gate_ref, up_ref, a_ref, dx3_ref, dff_ref, loss_ref, dg_ref):
        @pl.when(pl.program_id(0) == 0)
        def _():
            loss_ref[...] = jnp.zeros_like(loss_ref)
            dg_ref[...] = jnp.zeros_like(dg_ref)

        h = h_ref[...]
        ff = None
        for c0, n in FF_CHUNKS:
            cols = slice(c0, c0 + n)
            gate = lax.dot_general(h, w_ref[0, cols, :], NT, preferred_element_type=F32)
            up = lax.dot_general(h, w_ref[1, cols, :], NT, preferred_element_type=F32)
            gate_ref[:, cols] = gate.astype(BF16)
            up_ref[:, cols] = up.astype(BF16)
            act = (gate * jax.nn.sigmoid(gate) * up).astype(BF16)
            a_ref[:, cols] = act
            part = jnp.dot(act, wd_ref[cols, :], preferred_element_type=F32)
            ff = part if ff is None else ff + part
        out, n, r = _rms_fwd(ff, g_ref[...])
        e = x2_ref[...] + out - t_ref[...]
        loss_ref[...] += _fold8(e * e)
        dx3 = e * (1.0 / D)
        dx3_ref[...] = dx3
        dff, dg = _rms_bwd(dx3, n, r, g_ref[...])
        dff_ref[...] = dff.astype(BF16)
        dg_ref[...] += _fold8(dg)

    wide = _rows(tm, DFF)
    return pl.pallas_call(
        body, name="ffn_fwd_loss", grid=(s // tm,),
        in_specs=[_rows(tm, D), _resident((2, DFF, D)), _resident((DFF, D)), _rows(tm, D), _rows(tm, D), _full((1, D))],
        out_specs=[wide, wide, wide, _rows(tm, D), _rows(tm, D), _full((SUBLANES, D)), _full((SUBLANES, D))],
        out_shape=[jax.ShapeDtypeStruct((s, DFF), BF16)] * 3
        + [jax.ShapeDtypeStruct((s, D), F32), jax.ShapeDtypeStruct((s, D), BF16),
           jax.ShapeDtypeStruct((SUBLANES, D), F32), jax.ShapeDtypeStruct((SUBLANES, D), F32)],
        compiler_params=_cparams(56, ("arbitrary",)),
    )(h2, wgu, wd, x2, target, g_post)


def _ffn_bwd(dff, wd, gate, up, wgu, x2, g_pre, dx3, y, g_post, *, tm):
    s = x2.shape[0]

    def body(dff_ref, wd_ref, gate_ref, up_ref, w_ref, x2_ref, gpre_ref, dx3_ref, y_ref, gpost_ref,
             dgu_ref, dx2_ref, dy_ref, dgpre_ref, dgpost_ref):
        @pl.when(pl.program_id(0) == 0)
        def _():
            dgpre_ref[...] = jnp.zeros_like(dgpre_ref)
            dgpost_ref[...] = jnp.zeros_like(dgpost_ref)

        dff = dff_ref[...]
        dh2 = None
        for c0, n in FF_CHUNKS_BWD:
            cols = slice(c0, c0 + n)
            da = lax.dot_general(dff, wd_ref[cols, :], NT, preferred_element_type=F32)
            g = gate_ref[:, cols].astype(F32)
            sg = jax.nn.sigmoid(g)
            dgate = (da * up_ref[:, cols].astype(F32) * (sg * (1.0 + g * (1.0 - sg)))).astype(BF16)
            dup = (da * (g * sg)).astype(BF16)
            dgu_ref[:, cols] = dgate
            dgu_ref[:, DFF + c0:DFF + c0 + n] = dup
            part = (jnp.dot(dgate, w_ref[0, cols, :], preferred_element_type=F32)
                    + jnp.dot(dup, w_ref[1, cols, :], preferred_element_type=F32))
            dh2 = part if dh2 is None else dh2 + part
        _, n2, r2 = _rms_fwd(x2_ref[...], gpre_ref[...])
        dxn, dg = _rms_bwd(dh2, n2, r2, gpre_ref[...])
        dgpre_ref[...] += _fold8(dg)
        dx2 = dx3_ref[...] + dxn
        dx2_ref[...] = dx2
        _, ny, ry = _rms_fwd(y_ref[...], gpost_ref[...])
        dy, dg2 = _rms_bwd(dx2, ny, ry, gpost_ref[...])
        dy_ref[...] = dy.astype(BF16)
        dgpost_ref[...] += _fold8(dg2)

    wide = _rows(tm, DFF)
    return pl.pallas_call(
        body, name="ffn_bwd", grid=(s // tm,),
        in_specs=[_rows(tm, D), _resident((DFF, D)), wide, wide, _resident((2, DFF, D)), _rows(tm, D), _full((1, D)),
                  _rows(tm, D), _rows(tm, D), _full((1, D))],
        out_specs=[_rows(tm, 2 * DFF), _rows(tm, D), _rows(tm, D),
                   _full((SUBLANES, D)), _full((SUBLANES, D))],
        out_shape=[jax.ShapeDtypeStruct((s, 2 * DFF), BF16), jax.ShapeDtypeStruct((s, D), F32),
                   jax.ShapeDtypeStruct((s, D), BF16), jax.ShapeDtypeStruct((SUBLANES, D), F32),
                   jax.ShapeDtypeStruct((SUBLANES, D), F32)],
        compiler_params=_cparams(56, ("arbitrary",)),
    )(dff, wd, gate, up, wgu, x2, g_pre, dx3, y, g_post)


def _grad_matmul(a, b, *, ta, tb, ts, name, vmem_mb=48):
    s, ka = a.shape
    nb = b.shape[1]
    ts = min(ts, s)
    nk = s // ts

    def body(a_ref, b_ref, o_ref, *acc):
        if nk == 1:
            o_ref[...] = lax.dot_general(a_ref[...], b_ref[...], TN, preferred_element_type=F32).astype(BF16)
            return
        k = pl.program_id(2)

        @pl.when(k == 0)
        def _():
            acc[0][...] = jnp.zeros_like(acc[0])

        acc[0][...] += lax.dot_general(a_ref[...], b_ref[...], TN, preferred_element_type=F32)

        @pl.when(k == nk - 1)
        def _():
            o_ref[...] = acc[0][...].astype(BF16)

    whole_b = {"pipeline_mode": pl.Buffered(1)} if nk == 1 and nb == tb else {}
    return pl.pallas_call(
        body, name=name, grid=(ka // ta, nb // tb, nk),
        in_specs=[pl.BlockSpec((ts, ta), lambda i, j, k: (k, i)),
                  pl.BlockSpec((ts, tb), lambda i, j, k: (k, j), **whole_b)],
        out_specs=pl.BlockSpec((ta, tb), lambda i, j, k: (i, j)),
        out_shape=jax.ShapeDtypeStruct((ka, nb), BF16),
        scratch_shapes=[pltpu.VMEM((ta, tb), F32)] if nk > 1 else [],
        compiler_params=_cparams(vmem_mb, ("arbitrary", "arbitrary", "arbitrary")),
    )(a, b)


GW_TILE = 256


def _grad_w_in(h1t, pieces):
    ka, s = h1t.shape
    widths = [p.shape[1] for p in pieces]
    assert all(w % GW_TILE == 0 for w in widths)
    first = [sum(widths[:i]) // GW_TILE for i in range(len(pieces))]
    count = [w // GW_TILE for w in widths]

    def body(a_ref, *refs):
        o_ref = refs[-1]
        j = pl.program_id(0)
        for ref, f0, n in zip(refs[:-1], first, count):
            @pl.when((j >= f0) & (j < f0 + n))
            def _(ref=ref):
                o_ref[...] = jnp.dot(a_ref[...], ref[...], preferred_element_type=F32).astype(BF16)

    def spec(f0, n):
        return pl.BlockSpec((s, GW_TILE), lambda j: (0, jnp.clip(j - f0, 0, n - 1)))

    return pl.pallas_call(
        body, name="grad_w_in", grid=(sum(count),),
        in_specs=[_resident((ka, s))] + [spec(f0, n) for f0, n in zip(first, count)],
        out_specs=pl.BlockSpec((ka, GW_TILE), lambda j: (0, j)),
        out_shape=jax.ShapeDtypeStruct((ka, sum(widths)), BF16),
        compiler_params=_cparams(56, ("arbitrary",)),
    )(h1t, *pieces)


def _mix_bwd(dy, w_out, o, cv, bcu, ga, gc, gsum, after, *, tm):
    s = dy.shape[0]

    def group_norm_bwd(dn_out, v, g, gs):
        r = lax.rsqrt(_group_sum(v * v, gs) * (1.0 / DH) + EPS)
        n = v * r
        dn = dn_out * g
        return r * (dn - n * (_group_sum(dn * n, gs) * (1.0 / DH))), dn_out * n

    def body(dy_ref, w_ref, o_ref, cv_ref, bcu_ref, ga_ref, gc_ref, gs_ref, after_ref,
             do_ref, dl_ref, dcv_ref, db_ref, dga_ref, dgc_ref):
        @pl.when(pl.program_id(0) == 0)
        def _():
            dga_ref[...] = jnp.zeros_like(dga_ref)
            dgc_ref[...] = jnp.zeros_like(dgc_ref)

        dm = lax.dot_general(dy_ref[...], w_ref[...], NT, preferred_element_type=F32)
        ov = o_ref[...]
        do, dga = group_norm_bwd(dm[:, 0:AW], ov, ga_ref[...], gs_ref[...])
        dob = do.astype(BF16)
        do_ref[...] = dob
        dl_ref[...] = _group_sum(dob.astype(F32) * ov, gs_ref[...])
        dga_ref[...] += _fold8(dga)
        gate_b = bcu_ref[:, 0:CW].astype(F32)
        cv = cv_ref[...]
        dconv, dgc = group_norm_bwd(dm[:, AW:D], gate_b * cv, gc_ref[...], gs_ref[...])
        dgc_ref[...] += _fold8(dgc)
        dcv_ref[...] = dconv * gate_b
        db_ref[...] = (dconv * cv).astype(BF16)

    return pl.pallas_call(
        body, name="mix_bwd", grid=(s // tm,),
        in_specs=[_rows(tm, D), _resident((D, D)), _rows(tm, AW), _rows(tm, CW), _rows(tm, 3 * CW),
                  _full((1, AW)), _full((1, CW)), _full((GS, GS)), ANY],
        out_specs=[_rows(tm, AW), _rows(tm, AW), _rows(tm, CW), _rows(tm, CW),
                   _full((SUBLANES, AW)), _full((SUBLANES, CW))],
        out_shape=[jax.ShapeDtypeStruct((s, AW), BF16), jax.ShapeDtypeStruct((s, AW), F32),
                   jax.ShapeDtypeStruct((s, CW), F32), jax.ShapeDtypeStruct((s, CW), BF16),
                   jax.ShapeDtypeStruct((SUBLANES, AW), F32), jax.ShapeDtypeStruct((SUBLANES, CW), F32)],
        compiler_params=_cparams(48, ("arbitrary",)),
    )(dy, w_out, o, cv, bcu, ga, gc, gsum, after)


def _conv_bwd(dcv, db, bcu, cw8, *, tm):
    s = dcv.shape[0]
    nt = s // tm

    def body(dcv_ref, nxt_ref, db_ref, bcu_ref, halo_ref, cw_ref, dbcu_ref, dw_ref):
        i = pl.program_id(0)

        @pl.when(i == 0)
        def _():
            dw_ref[...] = jnp.zeros_like(dw_ref)

        z, z1, z2 = _conv_taps(bcu_ref, halo_ref, i == 0, tm)
        d = dcv_ref[...]
        dw_ref[0] += _fold8(d * z2)
        dw_ref[1] += _fold8(d * z1)
        dw_ref[2] += _fold8(d * z)
        nx = jnp.where(i == nt - 1, 0.0, nxt_ref[...])
        row = lax.broadcasted_iota(jnp.int32, (tm, CW), 0)
        d1 = jnp.where(row == tm - 1, nx[0:1, :], pltpu.roll(d, tm - 1, axis=0))
        d2 = jnp.where(row == tm - 2, nx[0:1, :], jnp.where(row == tm - 1, nx[1:2, :], pltpu.roll(d, tm - 2, axis=0)))
        dz = cw_ref[2:3, :] * d + cw_ref[1:2, :] * d1 + cw_ref[0:1, :] * d2
        dbcu_ref[:, 0:CW] = db_ref[...]
        dbcu_ref[:, CW:2 * CW] = (dz * bcu_ref[:, 2 * CW:3 * CW].astype(F32)).astype(BF16)
        dbcu_ref[:, 2 * CW:3 * CW] = (dz * bcu_ref[:, CW:2 * CW].astype(F32)).astype(BF16)

    return pl.pallas_call(
        body, name="conv_bwd", grid=(nt,),
        in_specs=[_rows(tm, CW),
                  pl.BlockSpec((SUBLANES, CW), lambda i: (jnp.minimum((i + 1) * (tm // SUBLANES), s // SUBLANES - 1), 0)),
                  _rows(tm, CW), _rows(tm, 3 * CW), _halo_before(tm, 3 * CW), _full((SUBLANES, CW))],
        out_specs=[_rows(tm, 3 * CW), _full((3, SUBLANES, CW))],
        out_shape=[jax.ShapeDtypeStruct((s, 3 * CW), BF16), jax.ShapeDtypeStruct((3, SUBLANES, CW), F32)],
        compiler_params=_cparams(48, ("arbitrary",)),
    )(dcv, dcv, db, bcu, bcu, cw8)


def _attn_bwd(qp, kp, v, do, lse, dl, mk, *, t):
    s = qp.shape[0]
    nq = s // t

    def body(q_ref, k_ref, v_ref, do_ref, lse_ref, dl_ref, mk_ref, dq_ref, dk_ref, dv_ref, dkx_ref, dq_acc):
        pi = pl.program_id(1)

        @pl.when(pi == 0)
        def _():
            dq_acc[...] = jnp.zeros_like(dq_acc)

        row = lax.broadcasted_iota(jnp.int32, (t, t), 0)
        col = lax.broadcasted_iota(jnp.int32, (t, t), 1)
        lane = lax.broadcasted_iota(jnp.int32, (t, 128), 1)

        def head_step(hh, qi, carry, modes):
            off = pl.multiple_of(qi * t, t)
            rows = pl.ds(off, t)
            q = q_ref[rows, HP * hh:HP * (hh + 1)]
            qt = q.T
            lse_col = lse_ref[rows, DH * hh:DH * hh + 1]
            dl_col = dl_ref[rows, DH * hh:DH * hh + 1]
            do2 = do_ref[rows, :]
            dom = jnp.where(lane < DH, do2 if hh == 0 else pltpu.roll(do2, DH, axis=1), jnp.zeros((), BF16))
            new, dss = [], []
            for half, masked in enumerate(modes):
                if masked is None:
                    new.append(carry[half])
                    continue
                dk, dv, cs = carry[half]
                keys = slice(half * t, (half + 1) * t)
                m_col = mk_ref[half, rows, DH * hh:DH * hh + 1]
                scale = jnp.exp2(m_col - lse_col)
                sc = lax.dot_general(q, k_ref[keys, HP * hh:HP * (hh + 1)], NT, preferred_element_type=F32) - m_col
                if masked:
                    sc = jnp.where(col <= row, sc, -1e30)
                pt = jnp.exp2(sc).astype(BF16)
                dp = lax.dot_general(dom, v_ref[keys, HP * hh:HP * (hh + 1)], NT, preferred_element_type=F32)
                ds32 = (pt.astype(F32) * scale) * (dp - dl_col)
                ds = ds32.astype(BF16)
                cs = cs + _fold8(ds32)
                dv = dv + jnp.dot((dom.astype(F32) * scale).astype(BF16).T, pt, preferred_element_type=F32)
                dk = dk + jnp.dot(qt, ds, preferred_element_type=F32)
                new.append((dk, dv, cs))
                dss.append((half, ds))
            if len(dss) == 2:
                dq = jnp.dot(jnp.concatenate([dss[0][1], dss[1][1]], axis=1), k_ref[:, HP * hh:HP * (hh + 1)],
                             preferred_element_type=F32)
            else:
                half, ds = dss[0]
                dq = jnp.dot(ds, k_ref[half * t:(half + 1) * t, HP * hh:HP * (hh + 1)], preferred_element_type=F32)
            dq_acc[rows, HP * hh:HP * (hh + 1)] += dq
            return tuple(new)

        def step(qi, carry, modes):
            return tuple(head_step(hh, qi, carry[hh], modes) for hh in range(2))

        def two_heads(a0, a1):
            return jnp.where(lane < DH, a0, pltpu.roll(a1, DH, axis=1))

        def rows_to_lanes(a0, a1):
            return jnp.concatenate([a0, a1], axis=0).T

        zero = (jnp.zeros((HP, t), F32), jnp.zeros((128, t), F32), jnp.zeros((SUBLANES, t), F32))
        carry = step(2 * pi, ((zero, zero), (zero, zero)), (True, None))
        carry = step(2 * pi + 1, carry, (False, True))

        def pair(j, carry):
            qi = 2 * (pi + 1 + j)
            return step(qi + 1, step(qi, carry, (False, False)), (False, False))

        carry = lax.fori_loop(0, nq // 2 - 1 - pi, pair, carry)
        for half in range(2):
            keys = slice(half * t, (half + 1) * t)
            (dk0, dv0, cs0), (dk1, dv1, cs1) = carry[0][half], carry[1][half]
            dk_ref[keys, :] = (rows_to_lanes(dk0[0:DH], dk1[0:DH]) * LN2).astype(BF16)
            dv_ref[keys, :] = rows_to_lanes(dv0[0:DH], dv1[0:DH]).astype(BF16)
            total = lambda cs: jnp.broadcast_to(jnp.sum(cs, axis=0, keepdims=True), (DH, t))
            dkx_ref[keys, :] = rows_to_lanes(total(cs0), total(cs1))

        @pl.when(pi == nq // 2 - 1)
        def _():
            for c in range(s // t):
                rows = slice(c * t, (c + 1) * t)
                dq_ref[rows, :] = two_heads(dq_acc[rows, 0:HP], dq_acc[rows, HP:2 * HP]).astype(BF16)

    return pl.pallas_call(
        body, name="attn_bwd", grid=(H // 2, nq // 2),
        in_specs=[pl.BlockSpec((s, 2 * HP), lambda p, i: (0, p)),
                  pl.BlockSpec((2 * t, 2 * HP), lambda p, i: (i, p)),
                  pl.BlockSpec((2 * t, 2 * HP), lambda p, i: (i, p)),
                  pl.BlockSpec((s, 128), lambda p, i: (0, p)),
                  pl.BlockSpec((s, 128), lambda p, i: (0, p)),
                  pl.BlockSpec((s, 128), lambda p, i: (0, p)),
                  pl.BlockSpec((2, s, 128), lambda p, i: (i, 0, p))],
        out_specs=[pl.BlockSpec((s, 128), lambda p, i: (0, p)),
                   pl.BlockSpec((2 * t, 128), lambda p, i: (i, p)),
                   pl.BlockSpec((2 * t, 128), lambda p, i: (i, p)),
                   pl.BlockSpec((2 * t, 128), lambda p, i: (i, p))],
        out_shape=[jax.ShapeDtypeStruct((s, AW), BF16), jax.ShapeDtypeStruct((s, AW), BF16),
                   jax.ShapeDtypeStruct((s, AW), BF16), jax.ShapeDtypeStruct((s, AW), F32)],
        scratch_shapes=[pltpu.VMEM((s, 2 * HP), F32)],
        compiler_params=_cparams(56, ("arbitrary", "arbitrary")),
    )(qp, kp, v, do, lse, dl, mk)


def _forget_bwd(dkx, z, sel, *, tm):
    s = dkx.shape[0]
    nt = s // tm

    def body(dk_ref, z_ref, sel_ref, dfl_ref, dbf_ref, carry):
        @pl.when(pl.program_id(0) == 0)
        def _():
            carry[...] = jnp.zeros_like(carry)
            dbf_ref[...] = jnp.zeros_like(dbf_ref)

        dc = _split_dot(dk_ref[...], sel_ref[...])
        row = lax.broadcasted_iota(jnp.int32, (tm, tm), 0)
        col = lax.broadcasted_iota(jnp.int32, (tm, tm), 1)
        tri = (col >= row).astype(BF16)
        dlogf = _exact_dot01(tri, dc) + carry[0:1, :]
        carry[...] = jnp.broadcast_to(dlogf[0:1, :], carry.shape)
        dz = dlogf * (1.0 - jax.nn.sigmoid(z_ref[...]))
        dfl_ref[:, 0:128] = dz.astype(BF16)
        dfl_ref[:, 128:GW_TILE] = jnp.zeros((tm, GW_TILE - 128), BF16)
        dbf_ref[...] += _fold8(dz)

    rev = lambda i: (nt - 1 - i, 0)
    return pl.pallas_call(
        body, name="forget_bwd", grid=(nt,),
        in_specs=[pl.BlockSpec((tm, AW), rev), pl.BlockSpec((tm, 128), rev), _full((AW, 128))],
        out_specs=[pl.BlockSpec((tm, GW_TILE), rev), _full((SUBLANES, 128))],
        out_shape=[jax.ShapeDtypeStruct((s, GW_TILE), BF16), jax.ShapeDtypeStruct((SUBLANES, 128), F32)],
        scratch_shapes=[pltpu.VMEM((SUBLANES, 128), F32)],
        compiler_params=_cparams(48, ("arbitrary",)),
    )(dkx, z, sel)


def _in_proj_bwd(pieces, wp, x, g1, dx2, after, *, tm):
    s = x.shape[0]

    def body(q_ref, k_ref, v_ref, bcu_ref, f_ref, w_ref, x_ref, g_ref, dx2_ref, after_ref, dx_ref, dg_ref):
        @pl.when(pl.program_id(0) == 0)
        def _():
            dg_ref[...] = jnp.zeros_like(dg_ref)

        dh = None
        for ref, (lo, hi) in zip((q_ref, k_ref, v_ref, bcu_ref, f_ref), PIECES):
            part = lax.dot_general(ref[...], w_ref[:, lo:hi], NT, preferred_element_type=F32)
            dh = part if dh is None else dh + part
        _, n, r = _rms_fwd(x_ref[...], g_ref[...])
        dxn, dg = _rms_bwd(dh, n, r, g_ref[...])
        dx_ref[...] = dx2_ref[...] + dxn
        dg_ref[...] += _fold8(dg)

    return pl.pallas_call(
        body, name="in_proj_bwd", grid=(s // tm,),
        in_specs=[_rows(tm, hi - lo) for lo, hi in PIECES]
        + [_resident((D, WP)), _rows(tm, D), _full((1, D)), _rows(tm, D), ANY],
        out_specs=[_rows(tm, D), _full((SUBLANES, D))],
        out_shape=[jax.ShapeDtypeStruct((s, D), F32), jax.ShapeDtypeStruct((SUBLANES, D), F32)],
        compiler_params=_cparams(56, ("arbitrary",)),
    )(*pieces, wp, x, g1, dx2, after)


def _position():
    return lax.axis_index("x"), lax.axis_index("y"), lax.axis_index("c")


ANY = pl.BlockSpec(memory_space=pl.ANY)


def _all_gather(shards):
    n = len(shards)

    def body(*refs):
        x_refs, out_refs = refs[:n], refs[n:2 * n]
        send_sems, recv_sems, local_sems = refs[2 * n:]
        x, y, c = _position()
        me, sibling = (x, y, c), (x, y, 1 - c)
        chips = [(1 - x, y), (x, 1 - y), (1 - x, 1 - y)]

        def copy(a, k, block, to, own=False):
            slot = out_refs[a].at[4 * block[0] + 2 * block[1] + block[2]]
            return pltpu.make_async_remote_copy(
                src_ref=x_refs[a] if own else slot, dst_ref=slot,
                send_sem=send_sems.at[7 * a + k], recv_sem=recv_sems.at[7 * a + k], device_id=to, device_id_type=MESH_ID)

        mine = [pltpu.make_async_copy(x_refs[a], out_refs[a].at[4 * x + 2 * y + c], local_sems.at[a]) for a in range(n)]
        for cp in mine:
            cp.start()
        first = []
        for a in range(n):
            first.append(copy(a, 0, me, sibling, own=True))
            first += [copy(a, 1 + j, me, (*chip, c), own=True) for j, chip in enumerate(chips)]
        for cp in first:
            cp.start()
        passed = []
        for j, chip in enumerate(chips):
            for a in range(n):
                copy(a, 1 + j, (*chip, c), me).wait_recv()
                fwd = copy(a, 4 + j, (*chip, c), sibling)
                fwd.start()
                passed.append(fwd)
        for a in range(n):
            copy(a, 0, sibling, me).wait_recv()
            for j, chip in enumerate(chips):
                copy(a, 4 + j, (*chip, 1 - c), me).wait_recv()
        for cp in first + passed:
            cp.wait_send()
        for cp in mine:
            cp.wait()

    return pl.pallas_call(
        body, name="all_gather_weights",
        out_shape=[jax.ShapeDtypeStruct((NDEV,) + sh.shape, sh.dtype) for sh in shards],
        in_specs=[ANY] * n, out_specs=[ANY] * n,
        scratch_shapes=[pltpu.SemaphoreType.DMA((7 * n,)), pltpu.SemaphoreType.DMA((7 * n,)), pltpu.SemaphoreType.DMA((n,))],
    )(*shards)


def _pair_exchange(grads):
    n = len(grads)

    def body(*refs):
        g_refs, out_refs = refs[:n], refs[n:2 * n]
        send_sems, recv_sems = refs[2 * n:]
        x, y, c = _position()
        copies = [pltpu.make_async_remote_copy(
            src_ref=g_refs[a].at[:, pl.ds(1 - c, 1)], dst_ref=out_refs[a], send_sem=send_sems.at[a],
            recv_sem=recv_sems.at[a], device_id=(x, y, 1 - c), device_id_type=MESH_ID) for a in range(n)]
        for cp in copies:
            cp.start()
        for cp in copies:
            cp.wait()

    return pl.pallas_call(
        body, name="grad_pair_exchange",
        out_shape=[jax.ShapeDtypeStruct((4, 1) + g.shape[2:], g.dtype) for g in grads],
        in_specs=[ANY] * n, out_specs=[ANY] * n,
        scratch_shapes=[pltpu.SemaphoreType.DMA((n,)), pltpu.SemaphoreType.DMA((n,))],
    )(*grads)


def _pair_sum(g, got, idx, *, tr, name):
    r, c = g.shape[2:]

    def body(idx_ref, g_ref, got_ref, pb_ref, own_ref):
        p = g_ref[0, 0].astype(F32) + got_ref[0, 0].astype(F32)
        pb_ref[0] = p.astype(BF16)

        @pl.when(pl.program_id(1) == idx_ref[1])
        def _():
            own_ref[...] = p

    return pl.pallas_call(
        body, name=name,
        grid_spec=pltpu.PrefetchScalarGridSpec(
            num_scalar_prefetch=1, grid=(r // tr, 4),
            in_specs=[pl.BlockSpec((1, 1, tr, c), lambda i, j, idx: (j, idx[0], i, 0)),
                      pl.BlockSpec((1, 1, tr, c), lambda i, j, idx: (j, 0, i, 0))],
            out_specs=[pl.BlockSpec((1, tr, c), lambda i, j, idx: (j, i, 0)),
                       pl.BlockSpec((tr, c), lambda i, j, idx: (i, 0))]),
        out_shape=[jax.ShapeDtypeStruct((4, r, c), BF16), jax.ShapeDtypeStruct((r, c), F32)],
        compiler_params=_cparams(62, ("arbitrary", "arbitrary")),
    )(idx, g, got)


HBM = pl.BlockSpec(memory_space=pltpu.HBM)
SEM = pl.BlockSpec(memory_space=pltpu.SEMAPHORE)
DATAFLOW = pltpu.SideEffectType.DATAFLOW_SIDE_EFFECTING


PEERS = {"gather": NDEV - 1, "scatter": NDEV - 1, "chips": 3}


def _exchange_copies(src_refs, land_refs, send_sems, recv_sems, mode):
    x, y, c = _position()
    me, my_chip = 4 * x + 2 * y + c, 2 * x + y
    npeers = PEERS[mode]
    copies, own = [], []
    for a, (s_ref, l_ref) in enumerate(zip(src_refs, land_refs)):
        for k in range(npeers):
            if mode == "chips":
                px, py, pc = x ^ ((k + 1) >> 1), y ^ ((k + 1) & 1), c
                src, dst = s_ref.at[2 * px + py], l_ref.at[my_chip]
            else:
                px, py, pc = x ^ ((k + 1) >> 2), y ^ (((k + 1) >> 1) & 1), c ^ ((k + 1) & 1)
                src, dst = (s_ref.at[4 * px + 2 * py + pc] if mode == "scatter" else s_ref), l_ref.at[me]
            copies.append(pltpu.make_async_remote_copy(
                src_ref=src, dst_ref=dst, send_sem=send_sems.at[npeers * a + k], recv_sem=recv_sems.at[npeers * a + k],
                device_id=(px, py, pc), device_id_type=MESH_ID))
        slot = my_chip if mode == "chips" else me
        own.append(pltpu.make_async_copy(s_ref if mode == "gather" else s_ref.at[slot], l_ref.at[slot],
                                         send_sems.at[npeers * len(src_refs) + a]))
    return copies, own


def _exchange_start(srcs, lands, after, *, mode, name):
    n = len(srcs)
    nsem = PEERS[mode] * n

    def body(*refs):
        token = refs[-1]
        copies, own = _exchange_copies(refs[:n], refs[n:2 * n], refs[2 * n + 1], refs[2 * n + 2], mode)
        for cp in copies + own:
            cp.start()
        token[...] = jnp.zeros_like(token)

    arrays = list(srcs) + list(lands)
    outs = pl.pallas_call(
        body, name=name,
        out_shape=(pltpu.SemaphoreType.DMA((nsem + n,)), pltpu.SemaphoreType.DMA((nsem,)),
                   *[pltpu.HBM(a.shape, a.dtype) for a in arrays], jax.ShapeDtypeStruct((SUBLANES, LANES), F32)),
        in_specs=[HBM] * (2 * n) + [ANY],
        out_specs=(SEM, SEM, *[HBM] * (2 * n), pl.BlockSpec(memory_space=pltpu.VMEM)),
        input_output_aliases={i: 2 + i for i in range(2 * n)},
        compiler_params=pltpu.CompilerParams(has_side_effects=DATAFLOW),
    )(*[pltpu.with_memory_space_constraint(a, pltpu.HBM) for a in arrays], after)
    return outs[0], outs[1], outs[2:2 + n], outs[2 + n:2 + 2 * n], outs[-1]


def _exchange_wait(send_sems, recv_sems, srcs, lands, after, *, mode, name):
    n = len(srcs)

    def body(*refs):
        copies, own = _exchange_copies(refs[:n], refs[n:2 * n], refs[2 * n], refs[2 * n + 1], mode)
        for cp in copies:
            cp.wait_send()
            cp.wait_recv()
        for cp in own:
            cp.wait()

    arrays = list(srcs) + list(lands)
    outs = pl.pallas_call(
        body, name=name,
        out_shape=tuple(pltpu.HBM(a.shape, a.dtype) for a in arrays),
        in_specs=[HBM] * (2 * n) + [SEM, SEM, ANY],
        out_specs=tuple([HBM] * (2 * n)),
        input_output_aliases={i: i for i in range(2 * n)},
        compiler_params=pltpu.CompilerParams(has_side_effects=DATAFLOW),
    )(*arrays, send_sems, recv_sems, after)
    return outs[n:]


def _small_start(parts):
    def body(gmp_ref, gmo_ref, gfp_ref, gfo_ref, ga_ref, gc_ref, dw_ref, bf_ref, loss_ref, land_in,
             send_sems, recv_sems, share_ref, land_ref, token, buf, put_sem):
        def colsum(v):
            return jnp.sum(v, axis=0, keepdims=True)

        loss = jnp.sum(colsum(loss_ref[...]), axis=1, keepdims=True) * (0.5 / D)
        rows = [colsum(gmp_ref[...]), colsum(gmo_ref[...]), colsum(gfp_ref[...]), colsum(gfo_ref[...]),
                jnp.concatenate([colsum(ga_ref[...]), colsum(gc_ref[...])], axis=1),
                jnp.concatenate([colsum(dw_ref[0]), colsum(dw_ref[1])], axis=1),
                jnp.concatenate([colsum(dw_ref[2]), colsum(bf_ref[...]), jnp.broadcast_to(loss, (1, 128)),
                                 jnp.zeros((1, 256), F32)], axis=1),
                jnp.zeros((1, D), F32)]
        buf[...] = jnp.concatenate(rows, axis=0)
        put = pltpu.make_async_copy(buf, share_ref, put_sem.at[0])
        put.start()
        put.wait()
        copies, own = _exchange_copies([share_ref], [land_ref], send_sems, recv_sems, "gather")
        for cp in copies + own:
            cp.start()
        token[...] = jnp.zeros_like(token)

    vm = pl.BlockSpec(memory_space=pltpu.VMEM)
    n = len(parts)
    land = lax.empty((NDEV, SUBLANES, D), F32)
    outs = pl.pallas_call(
        body, name="small_gather_start",
        out_shape=(pltpu.SemaphoreType.DMA((PEERS["gather"] + 1,)), pltpu.SemaphoreType.DMA((PEERS["gather"],)),
                   pltpu.HBM((SUBLANES, D), F32), pltpu.HBM(land.shape, land.dtype),
                   jax.ShapeDtypeStruct((SUBLANES, LANES), F32)),
        in_specs=[vm] * n + [HBM], out_specs=(SEM, SEM, HBM, HBM, vm),
        input_output_aliases={n: 3},
        scratch_shapes=[pltpu.VMEM((SUBLANES, D), F32), pltpu.SemaphoreType.DMA((1,))],
        compiler_params=pltpu.CompilerParams(has_side_effects=DATAFLOW),
    )(*parts, pltpu.with_memory_space_constraint(land, pltpu.HBM))
    return outs[0], outs[1], [outs[2]], [outs[3]], outs[4]


def _small_sum(land):
    def body(land_ref, out_ref):
        acc = land_ref[0]
        for d in range(1, NDEV):
            acc = acc + land_ref[d]
        out_ref[...] = acc

    return pl.pallas_call(
        body, name="small_sum", grid=(1,), out_shape=jax.ShapeDtypeStruct((SUBLANES, D), F32),
        in_specs=[pl.BlockSpec((NDEV, SUBLANES, D), lambda i: (0, 0, 0))],
        out_specs=pl.BlockSpec((SUBLANES, D), lambda i: (0, 0)),
    )(land)


def _adam_update(w, g, m, v):
    nm = ADAM_B1 * m + (1.0 - ADAM_B1) * g
    nv = ADAM_B2 * v + (1.0 - ADAM_B2) * (g * g)
    m_hat = nm / (1.0 - ADAM_B1 ** ADAM_STEP)
    v_hat = nv / (1.0 - ADAM_B2 ** ADAM_STEP)
    return -ADAM_LR * (m_hat / (jnp.sqrt(v_hat) + ADAM_EPS) + ADAM_WD * w), nm, nv


SMALL_SLOTS = {"g_mix_pre": (0, 0, D), "g_mix_post": (1, 0, D), "g_ffn_pre": (2, 0, D), "g_ffn_post": (3, 0, D),
               "g_attn_out": (4, 0, AW), "g_conv_out": (4, AW, CW), "b_forget": (6, CW, H)}
LOSS_LANE = CW + 128


def _small_adamw(small, conv_grad, params):
    names = list(params)
    n = len(names)

    def body(*refs):
        small_ref, cg_ref = refs[0], refs[1]
        ins, outs = refs[2:2 + 3 * n], refs[2 + 3 * n:]
        for i, name in enumerate(names):
            w_ref, m_ref, v_ref = ins[3 * i:3 * i + 3]
            g_ref, d_ref, nm_ref, nv_ref = outs[4 * i:4 * i + 4]
            if name == "conv_w":
                g = cg_ref[...]
            else:
                r, c0, width = SMALL_SLOTS[name]
                g = small_ref[r:r + 1, c0:c0 + width]
            g_ref[...] = g
            d_ref[...], nm_ref[...], nv_ref[...] = _adam_update(w_ref[...], g, m_ref[...], v_ref[...])
        outs[4 * n][...] = small_ref[6:7, LOSS_LANE:LOSS_LANE + 1]

    vm = pl.BlockSpec(memory_space=pltpu.VMEM)
    flat = [a for name in names for a in params[name]]
    outs = pl.pallas_call(
        body, name="adamw_small",
        in_specs=[vm] * (2 + 3 * n), out_specs=[vm] * (4 * n + 1),
        out_shape=[jax.ShapeDtypeStruct(params[name][0].shape, F32) for name in names for _ in range(4)]
        + [jax.ShapeDtypeStruct((1, 1), F32)],
    )(small, conv_grad, *flat)
    return {name: outs[4 * i:4 * i + 4] for i, name in enumerate(names)}, outs[4 * n].reshape(())


def _chip_sum_adamw(got, own, idx, wt, mt, vt, *, tr, name):
    cols, rows = wt.shape
    gcols = own.shape[1]

    def body(idx_ref, got_ref, own_ref, w_ref, m_ref, v_ref, g_ref, d_ref, nm_ref, nv_ref):
        g = jnp.zeros((tr, gcols), F32)
        for j in range(4):
            g = g + jnp.where(idx_ref[1] == j, own_ref[...], got_ref[j].astype(F32))
        g = g.T[:cols]
        g_ref[...] = g
        d_ref[...], nm_ref[...], nv_ref[...] = _adam_update(w_ref[...], g, m_ref[...], v_ref[...])

    spec = pl.BlockSpec((cols, tr), lambda i, idx: (0, i))
    gspec = pl.BlockSpec((tr, gcols), lambda i, idx: (i, 0))
    return pl.pallas_call(
        body, name=name,
        grid_spec=pltpu.PrefetchScalarGridSpec(
            num_scalar_prefetch=1, grid=(rows // tr,),
            in_specs=[pl.BlockSpec((4, tr, gcols), lambda i, idx: (0, i, 0)), gspec, spec, spec, spec],
            out_specs=[spec] * 4),
        out_shape=[jax.ShapeDtypeStruct((cols, rows), F32)] * 4,
        compiler_params=_cparams(32, ("arbitrary",)),
    )(idx, got, own, wt, mt, vt)


def _device_sum_adamw(land, w, m, v, *, tr, name):
    rows, cols = w.shape

    def body(land_ref, w_ref, m_ref, v_ref, g_ref, d_ref, nm_ref, nv_ref):
        g = land_ref[0].astype(F32)
        for dev in range(1, NDEV):
            g = g + land_ref[dev].astype(F32)
        g_ref[...] = g
        d_ref[...], nm_ref[...], nv_ref[...] = _adam_update(w_ref[...], g, m_ref[...], v_ref[...])

    spec = pl.BlockSpec((tr, cols), lambda i: (i, 0))
    return pl.pallas_call(
        body, name=name, grid=(rows // tr,),
        in_specs=[pl.BlockSpec((NDEV, tr, cols), lambda i: (0, i, 0)), spec, spec, spec],
        out_specs=[spec] * 4,
        out_shape=[jax.ShapeDtypeStruct((rows, cols), F32)] * 4,
        compiler_params=_cparams(32, ("arbitrary",)),
    )(land, w, m, v)


def _placement_constants():
    j = np.arange(128)[:, None]
    lane = np.arange(1024)[None, :]
    head, sub = lane // HP, lane % HP
    piece, jh = j // H, j % H
    valid = (j < 3 * H) & (jh == head)
    pq = np.where(valid & (sub == DH + piece), 1.0, 0.0).astype(BF16)
    pk = np.where(valid & (sub == DH + 3 + piece), -1.0, 0.0).astype(BF16)
    oq = np.where((sub >= DH + 3) & (sub < DH + 6), 1.0, 0.0).astype(np.float32)
    ok = np.where((sub >= DH) & (sub < DH + 3), 1.0, 0.0).astype(np.float32)
    r = np.arange(AW)[:, None]
    cc = np.arange(128)[None, :]
    sel = np.where((r % DH == 3) & (r // DH == cc), -1.0, 0.0).astype(BF16)
    gi = np.arange(GS)
    gsum = (gi[:, None] // DH == gi[None, :] // DH).astype(BF16)
    return tuple(jnp.asarray(c) for c in (pq, pk, oq, ok, sel, gsum))


def _local_step(xs, tgt, wp, late_weights, cw8, bfp, g_attn_out, g_conv_out,
                g_mix_pre, g_mix_post, g_ffn_pre, g_ffn_post, early_grads=None, last_grad=None):
    pq, pk, oq, ok, sel, gsum = _placement_constants()
    h1t, qp, kp, vv, bcu, zf = _in_proj(xs, g_mix_pre, wp, bfp, pq, pk, oq, ok, tm=512)
    o, lse, mk = _attn_fwd(qp, kp, vv, t=512)
    w_out_f, wgu, wd = late_weights(lse)
    merged, y, x2, cv, h2 = _mix_out(o, bcu, cw8, g_attn_out, g_conv_out, gsum, w_out_f, xs, g_mix_post, g_ffn_pre, tm=512)
    gate, up, act, dx3, dff, loss_p, dg_ffn_post = _ffn_fwd_loss(h2, wgu, wd, x2, tgt, g_ffn_post, tm=512)

    dgu, dx2, dy, dg_ffn_pre, dg_mix_post = _ffn_bwd(dff, wd, gate, up, wgu, x2, g_ffn_pre, dx3, y, g_mix_post, tm=256)
    dw_down = _grad_matmul(act, dff, ta=DFF // 2, tb=D, ts=4096, name="grad_w_down", vmem_mb=60)
    dw_gu = _grad_matmul(dgu, h2, ta=DFF // 2, tb=D, ts=4096, name="grad_w_gate_up", vmem_mb=60).reshape(NDEV, FB, D)
    dw_out = _grad_matmul(merged, dy, ta=1024, tb=1024, ts=2048, name="grad_w_out")
    token = early_grads(dw_out, dw_gu, dw_down) if early_grads is not None else dw_out
    do, dl, dcv, db, dg_attn, dg_conv = _mix_bwd(dy, w_out_f, o, cv, bcu, g_attn_out, g_conv_out, gsum, token, tm=512)
    dbcu, dtaps = _conv_bwd(dcv, db, bcu, cw8, tm=512)
    dqp, dkp, dv, dkx = _attn_bwd(qp, kp, vv, do, lse, dl, mk, t=512)
    dfl, dbf = _forget_bwd(dkx, zf, sel, tm=512)
    pieces = (dqp, dkp, dv, dbcu, dfl)
    dwp = _grad_w_in(h1t, pieces)
    token = last_grad(dwp) if last_grad is not None else dwp
    grad_x, dg_mix_pre = _in_proj_bwd(pieces, wp, xs, g_mix_pre, dx2, token, tm=512)
    return (grad_x, dwp, dw_out, dw_gu, dw_down, dg_mix_pre, dg_mix_post, dg_ffn_pre, dg_ffn_post, dg_attn, dg_conv,
            dtaps, dbf, loss_p)


BIG_TILES = {"w_in": 256, "w_out": 128, "w_gate_up": 176, "w_down": 176}


def kernel(x, w_in, b_forget, conv_w, g_attn_out, g_conv_out, w_out, g_mix_pre, g_mix_post, w_gate_up, w_down, g_ffn_pre, g_ffn_post, loss_target, m_w_in, m_b_forget, m_conv_w, m_g_attn_out, m_g_conv_out, m_w_out, m_g_mix_pre, m_g_mix_post, m_w_gate_up, m_w_down, m_g_ffn_pre, m_g_ffn_post, v_w_in, v_b_forget, v_conv_w, v_g_attn_out, v_g_conv_out, v_w_out, v_g_mix_pre, v_g_mix_post, v_w_gate_up, v_w_down, v_g_ffn_pre, v_g_ffn_post):
    xc, yc, cc = _position()
    my_chip = 2 * xc + yc
    me = 2 * my_chip + cc
    idx = jnp.stack([cc, my_chip]).astype(jnp.int32)
    tables = _in_layout_tables()

    w_in_b = w_in[0].astype(BF16)
    g_in, g_last, g_taps = _all_gather([w_in_b[:, :IN_MAIN], w_in_b[:, IN_MAIN].reshape(SUBLANES, LANES), conv_w[0]])
    last_cols = jnp.pad(g_last.reshape(NDEV, D).T.astype(F32), ((0, 0), (0, LANES - NDEV)))
    wp = _assemble_w_in(g_in, last_cols, tables, tr=256)
    cw8 = jnp.pad(g_taps.transpose(1, 0, 2).reshape(3, CW), ((0, SUBLANES - 3), (0, 0)))

    late = [w_out[0].astype(BF16), w_gate_up[0].T.astype(BF16), w_down[0].astype(BF16)]
    ssem, rsem, late_thru, land_thru, token = _exchange_start(
        late, [lax.empty((NDEV,) + s.shape, s.dtype) for s in late], g_in, mode="gather",
        name="gather_late_start")
    bfp = jnp.pad(b_forget, ((0, 0), (0, 128 - H))) + token[0:1, :]

    def late_weights(after):
        l_out, l_gu, l_down = _exchange_wait(ssem, rsem, late_thru, land_thru, after, mode="gather", name="gather_late_wait")
        return l_out.reshape(D, D), l_gu.reshape(2, DFF, D), l_down.reshape(DFF, D)

    early = {}

    def early_grads(dw_out, dw_gu, dw_down):
        srcs = [dw_out.reshape(NDEV, D // NDEV, D), dw_gu, dw_down.reshape(NDEV, DFF // NDEV, D)]
        lands = [lax.empty(s.shape, s.dtype) for s in srcs]
        early["handles"] = _exchange_start(srcs, lands, dw_out, mode="scatter", name="scatter_early_start")
        return early["handles"][4]

    last = {}

    def last_grad(dwp):
        g_w_in = _disassemble_w_in(dwp, tables, tr=256).reshape(4, 2, D, IN_PAD)
        (from_sibling,) = _pair_exchange([g_w_in])
        pair_b, last["own"] = _pair_sum(g_w_in, from_sibling, idx, tr=D, name="grad_pair_sum_w_in")
        last["handles"] = _exchange_start([pair_b], [lax.empty(pair_b.shape, pair_b.dtype)], last["own"], mode="chips",
                                          name="chips_w_in_start")
        return last["handles"][4]

    (grad_x, dwp, dw_out, dw_gu, dw_down, dg_mix_pre, dg_mix_post, dg_ffn_pre, dg_ffn_post, dg_attn, dg_conv,
     dtaps, dbf, loss_p) = _local_step(x[0], loss_target[0], wp, late_weights, cw8, bfp, g_attn_out, g_conv_out,
                                        g_mix_pre, g_mix_post, g_ffn_pre, g_ffn_post, early_grads, last_grad)

    s_ssem, s_rsem, s_srcs, s_lands, s_token = _small_start(
        [dg_mix_pre, dg_mix_post, dg_ffn_pre, dg_ffn_post, dg_attn, dg_conv, dtaps, dbf, loss_p])

    e_ssem, e_rsem, e_srcs, e_lands, _ = early["handles"]
    land_out, land_gu, land_down = _exchange_wait(e_ssem, e_rsem, e_srcs, e_lands, s_token, mode="scatter",
                                                  name="scatter_early_wait")
    res = {}
    big = {"w_out": (land_out, w_out[0], m_w_out[0], v_w_out[0]),
           "w_gate_up": (land_gu, w_gate_up[0].T, m_w_gate_up[0].T, v_w_gate_up[0].T),
           "w_down": (land_down, w_down[0], m_w_down[0], v_w_down[0])}
    for name, (land, w, m, v) in big.items():
        outs = _device_sum_adamw(land, w, m, v, tr=BIG_TILES[name], name="adamw_" + name)
        res[name] = [(o.T if name == "w_gate_up" else o)[None] for o in outs]
    c_ssem, c_rsem, c_srcs, c_lands, _ = last["handles"]
    after = sum(res[n][1][0, :SUBLANES, :LANES] for n in big)
    (from_chips,) = _exchange_wait(c_ssem, c_rsem, c_srcs, c_lands, after, mode="chips", name="chips_w_in_wait")
    outs = _chip_sum_adamw(from_chips, last["own"], idx, w_in[0].T, m_w_in[0].T, v_w_in[0].T,
                           tr=BIG_TILES["w_in"], name="adamw_w_in")
    res["w_in"] = [o.T[None] for o in outs]
    w_in_done = outs[1][:SUBLANES, :LANES]

    (land_small,) = _exchange_wait(s_ssem, s_rsem, s_srcs, s_lands, w_in_done, mode="gather", name="small_gather_wait")
    small = _small_sum(land_small)
    taps_full = jnp.concatenate([small[5:6, :CW], small[5:6, CW:], small[6:7, :CW]], axis=0)
    taps_first = lambda a: a.transpose(1, 0, 2)
    smalls = {"b_forget": (b_forget, m_b_forget, v_b_forget),
              "conv_w": (taps_first(conv_w), taps_first(m_conv_w), taps_first(v_conv_w)),
              "g_attn_out": (g_attn_out, m_g_attn_out, v_g_attn_out), "g_conv_out": (g_conv_out, m_g_conv_out, v_g_conv_out),
              "g_mix_pre": (g_mix_pre, m_g_mix_pre, v_g_mix_pre), "g_mix_post": (g_mix_post, m_g_mix_post, v_g_mix_post),
              "g_ffn_pre": (g_ffn_pre, m_g_ffn_pre, v_g_ffn_pre), "g_ffn_post": (g_ffn_post, m_g_ffn_post, v_g_ffn_post)}
    own_taps = lax.dynamic_slice(taps_full, (0, me * 64), (3, 64))[:, None, :]
    small_res, loss = _small_adamw(small, own_taps, smalls)
    for name, outs in small_res.items():
        res[name] = [taps_first(o) for o in outs] if name == "conv_w" else list(outs)

    order = ["w_in", "b_forget", "conv_w", "g_attn_out", "g_conv_out", "w_out", "g_mix_pre", "g_mix_post",
             "w_gate_up", "w_down", "g_ffn_pre", "g_ffn_post"]
    outs = [loss, grad_x[None]]
    for k in range(4):
        outs += [res[n][k] for n in order]
    return tuple(outs)
```

```python
import functools

import numpy as np

import jax
import jax.numpy as jnp
from jax import lax
from jax.experimental import pallas as pl
from jax.experimental.pallas import tpu as pltpu

F32 = jnp.float32
BF16 = jnp.bfloat16
MESH_ID = pl.DeviceIdType.MESH

D = 1024
H = 8
DH = 64
AW = 512
CW = 512
DFF = 2816
FB = DFF // 4
FF_CHUNKS = ((0, 768), (768, 768), (1536, 768), (2304, 512))
FF_CHUNKS_BWD = ((0, 1024), (1024, 1024), (2048, 768))
HP = 128
OFF_Q, OFF_K, OFF_V, OFF_BCU, OFF_F = 0, 512, 1024, 1536, 3072
WP = OFF_F + 128
PIECES = ((OFF_Q, OFF_K), (OFF_K, OFF_V), (OFF_V, OFF_BCU), (OFF_BCU, OFF_F), (OFF_F, WP))
EPS = 1e-6
LOG2E, LN2 = 1.4426950408889634, 0.6931471805599453
NDEV = 8
LANES = 128
SUBLANES = 8
IN_COLS = 385
IN_PAD = 512
IN_MAIN = 384
WIN = 640
ADAM_LR, ADAM_B1, ADAM_B2, ADAM_EPS, ADAM_WD, ADAM_STEP = 0.001, 0.9, 0.999, 1e-08, 0.01, 10

NT = (((1,), (1,)), ((), ()))
TN = (((0,), (0,)), ((), ()))


def _cparams(vmem_mb=None, sem=None):
    kw = {}
    if vmem_mb is not None:
        kw["vmem_limit_bytes"] = vmem_mb << 20
    if sem is not None:
        kw["dimension_semantics"] = sem
    return pltpu.CompilerParams(**kw)


def _full(shape):
    return pl.BlockSpec(shape, lambda *_: (0,) * len(shape))


def _resident(shape):
    return pl.BlockSpec(shape, lambda *_: (0,) * len(shape), pipeline_mode=pl.Buffered(1))


def _rows(tm, width):
    return pl.BlockSpec((tm, width), lambda i: (i, 0))


def _fold8(v):
    r, w = v.shape
    return jnp.sum(v.reshape(r // SUBLANES, SUBLANES, w), axis=0)


def _split_dot(v, m01):
    hi = v.astype(BF16)
    lo = (v - hi.astype(F32)).astype(BF16)
    return (jnp.dot(hi, m01, preferred_element_type=F32)
            + jnp.dot(lo, m01, preferred_element_type=F32))


GS = 256


def _group_sum(v, g01):
    parts = [_split_dot(v[:, c:c + GS], g01) for c in range(0, v.shape[1], GS)]
    return parts[0] if len(parts) == 1 else jnp.concatenate(parts, axis=1)


def _exact_dot01(m01, v):
    p1 = v.astype(BF16)
    r1 = v - p1.astype(F32)
    p2 = r1.astype(BF16)
    p3 = (r1 - p2.astype(F32)).astype(BF16)
    return (jnp.dot(m01, p1, preferred_element_type=F32) + jnp.dot(m01, p2, preferred_element_type=F32)
            + jnp.dot(m01, p3, preferred_element_type=F32))


def _rms_fwd(v, g):
    r = lax.rsqrt(jnp.mean(v * v, axis=-1, keepdims=True) + EPS)
    n = v * r
    return n * g, n, r


def _rms_bwd(do, n, r, g):
    dn = do * g
    return r * (dn - n * jnp.mean(dn * n, axis=-1, keepdims=True)), do * n


def _padded_column(n):
    if n < AW:
        return OFF_Q + n, 0.125
    if n < 3 * AW:
        return n, 1.0
    if n < 3 * AW + H:
        return OFF_F + n - 3 * AW, 1.0
    return OFF_BCU + n - 3 * AW - H, 1.0


def _in_layout_tables():
    dest = -np.ones((IN_PAD, LANES), np.int32)
    dest_f = -np.ones((IN_PAD, LANES), np.int32)
    scale = np.zeros((IN_PAD, LANES), np.float32)
    starts = []
    for k in range(NDEV):
        cols = [_padded_column(IN_COLS * k + j) for j in range(IN_COLS)]
        main = [c for c, _ in cols if c < OFF_F]
        ws = min((min(main) // LANES) * LANES, OFF_F - WIN)
        assert ws <= min(main) and max(main) < ws + WIN
        starts.append(ws)
        for j, (c, sc) in enumerate(cols):
            scale[j, k] = sc
            if c < OFF_F:
                dest[j, k] = c - ws
            else:
                dest_f[j, k] = c - OFF_F
    f_shards = tuple(k for k in range(NDEV) if (dest_f[:, k] >= 0).any())
    return tuple(starts), f_shards, jnp.asarray(dest), jnp.asarray(dest_f), jnp.asarray(scale)


def _perm(dest_ref, scale_ref, k, width, rows=IN_PAD):
    lane = lax.broadcasted_iota(jnp.int32, (rows, width), 1)
    return jnp.where(dest_ref[0:rows, k:k + 1] == lane, scale_ref[0:rows, k:k + 1], 0.0).astype(BF16)


def _assemble_w_in(blocks, last_cols, tables, *, tr):
    starts, f_shards, dest, dest_f, scale = tables
    last = [_padded_column(IN_COLS * k + IN_MAIN) for k in range(NDEV)]
    f_main = [any(_padded_column(IN_COLS * k + j)[0] >= OFF_F for j in range(IN_MAIN)) for k in range(NDEV)]
    assert IN_COLS == IN_MAIN + 1

    def body(b_ref, c_ref, dest_ref, destf_ref, scale_ref, o_ref):
        o_ref[...] = jnp.zeros_like(o_ref)
        lane = lax.broadcasted_iota(jnp.int32, (tr, LANES), 1)
        for k in range(NDEV):
            b = b_ref[k]
            ws = starts[k]
            part = jnp.dot(b, _perm(dest_ref, scale_ref, k, WIN, IN_MAIN), preferred_element_type=F32)
            o_ref[:, ws:ws + WIN] += part.astype(BF16)
            if f_main[k]:
                part = jnp.dot(b, _perm(destf_ref, scale_ref, k, 128, IN_MAIN), preferred_element_type=F32)
                o_ref[:, OFF_F:WP] += part.astype(BF16)
            col, sc = last[k]
            tile = (col // LANES) * LANES
            o_ref[:, tile:tile + LANES] += jnp.where(lane == col - tile, c_ref[:, k:k + 1] * sc, 0.0).astype(BF16)

    tab = _full((IN_PAD, LANES))
    return pl.pallas_call(
        body, name="assemble_w_in", grid=(D // tr,),
        in_specs=[pl.BlockSpec((NDEV, tr, IN_MAIN), lambda i: (0, i, 0)), _rows(tr, LANES), tab, tab, tab],
        out_specs=_rows(tr, WP),
        out_shape=jax.ShapeDtypeStruct((D, WP), BF16),
        compiler_params=_cparams(48, ("arbitrary",)),
    )(blocks, last_cols, dest, dest_f, scale)


def _disassemble_w_in(dwp, tables, *, tr):
    starts, f_shards, dest, dest_f, scale = tables
    width = dwp.shape[1]

    def body(g_ref, dest_ref, destf_ref, scale_ref, o_ref):
        for k in range(NDEV):
            ws = starts[k]
            acc = lax.dot_general(g_ref[:, ws:ws + WIN], _perm(dest_ref, scale_ref, k, WIN), NT, preferred_element_type=F32)
            if k in f_shards:
                acc = acc + lax.dot_general(g_ref[:, OFF_F:WP], _perm(destf_ref, scale_ref, k, 128), NT,
                                            preferred_element_type=F32)
            o_ref[k] = acc.astype(BF16)

    tab = _full((IN_PAD, LANES))
    return pl.pallas_call(
        body, name="disassemble_w_in", grid=(D // tr,),
        in_specs=[_rows(tr, width), tab, tab, tab],
        out_specs=pl.BlockSpec((NDEV, tr, IN_PAD), lambda i: (0, i, 0)),
        out_shape=jax.ShapeDtypeStruct((NDEV, D, IN_PAD), BF16),
        compiler_params=_cparams(48, ("arbitrary",)),
    )(dwp, dest, dest_f, scale)


def _in_proj(x, g1, wp, bfp, pq, pk, oq, ok, *, tm):
    s = x.shape[0]

    def body(x_ref, g_ref, w_ref, bf_ref, pq_ref, pk_ref, oq_ref, ok_ref,
             ht_ref, qp_ref, kp_ref, v_ref, bcu_ref, z_ref, carry):
        @pl.when(pl.program_id(0) == 0)
        def _():
            carry[...] = jnp.zeros_like(carry)

        h = _rms_fwd(x_ref[...], g_ref[...])[0].astype(BF16)
        ht_ref[...] = h.T
        z = jnp.dot(h, w_ref[:, OFF_F:WP], preferred_element_type=F32) + bf_ref[...]
        z_ref[...] = z
        lane = lax.broadcasted_iota(jnp.int32, (tm, 128), 1)
        logf = jnp.where(lane < H, jnp.minimum(z, 0.0) - jnp.log(1.0 + jnp.exp(-jnp.abs(z))), 0.0)
        row = lax.broadcasted_iota(jnp.int32, (tm, tm), 0)
        col = lax.broadcasted_iota(jnp.int32, (tm, tm), 1)
        tri = (col <= row).astype(BF16)
        c = _exact_dot01(tri, logf) + carry[0:1, :]
        carry[...] = jnp.broadcast_to(c[tm - 1:tm, :], carry.shape)
        cb = c * LOG2E
        c1 = cb.astype(BF16).astype(F32)
        r1 = cb - c1
        c2 = r1.astype(BF16).astype(F32)
        c3 = (r1 - c2).astype(BF16).astype(F32)
        zc = (c1 + pltpu.roll(c2, 8, axis=1) + pltpu.roll(c3, 16, axis=1)).astype(BF16)

        def pad_heads(v):
            blocks = []
            for pair in range(H // 2):
                two = v[:, 128 * pair:128 * (pair + 1)]
                blocks.append(jnp.where(lane < DH, two, 0.0))
                blocks.append(jnp.where(lane < DH, pltpu.roll(two, DH, axis=1), 0.0))
            return jnp.concatenate(blocks, axis=1)

        q = jnp.dot(h, w_ref[:, OFF_Q:OFF_K], preferred_element_type=F32) * LOG2E
        qp_ref[...] = (pad_heads(q) + jnp.dot(zc, pq_ref[...], preferred_element_type=F32) + oq_ref[...]).astype(BF16)
        k = jnp.dot(h, w_ref[:, OFF_K:OFF_V], preferred_element_type=F32)
        kp_ref[...] = (pad_heads(k) + jnp.dot(zc, pk_ref[...], preferred_element_type=F32) + ok_ref[...]).astype(BF16)
        v = pad_heads(jnp.dot(h, w_ref[:, OFF_V:OFF_BCU], preferred_element_type=F32))
        ones_lane = lax.broadcasted_iota(jnp.int32, (tm, H * HP), 1) % HP == DH
        v_ref[...] = jnp.where(ones_lane, 1.0, v).astype(BF16)
        bcu_ref[...] = jnp.dot(h, w_ref[:, OFF_BCU:OFF_F], preferred_element_type=F32).astype(BF16)

    return pl.pallas_call(
        body, name="in_proj", grid=(s // tm,),
        in_specs=[_rows(tm, D), _full((1, D)), _resident((D, WP)), _full((1, 128)),
                  _full((128, 1024)), _full((128, 1024)), _full((1, 1024)), _full((1, 1024))],
        out_specs=[pl.BlockSpec((D, tm), lambda i: (0, i)), _rows(tm, 1024), _rows(tm, 1024), _rows(tm, 1024),
                   _rows(tm, 3 * CW), _rows(tm, 128)],
        out_shape=[jax.ShapeDtypeStruct((D, s), BF16), jax.ShapeDtypeStruct((s, 1024), BF16),
                   jax.ShapeDtypeStruct((s, 1024), BF16), jax.ShapeDtypeStruct((s, 1024), BF16),
                   jax.ShapeDtypeStruct((s, 3 * CW), BF16), jax.ShapeDtypeStruct((s, 128), F32)],
        scratch_shapes=[pltpu.VMEM((SUBLANES, 128), F32)],
        compiler_params=_cparams(56, ("arbitrary",)),
    )(x, g1, wp, bfp, pq, pk, oq, ok)


def _attn_fwd(qp, kp, v, *, t):
    s = qp.shape[0]
    nq = s // t

    def body(q_ref, k_ref, v_ref, o_ref, lse_ref, mk_ref):
        pi = pl.program_id(1)
        row = lax.broadcasted_iota(jnp.int32, (t, t), 0)
        col = lax.broadcasted_iota(jnp.int32, (t, t), 1)
        lane = lax.broadcasted_iota(jnp.int32, (t, 128), 1)

        def head_step(hh, rows, ki, carry, masked):
            m, acc = carry
            off = pl.multiple_of(ki * t, t)
            q = q_ref[rows, HP * hh:HP * (hh + 1)]
            k = k_ref[pl.ds(off, t), HP * hh:HP * (hh + 1)]
            sc = lax.dot_general(q, k, NT, preferred_element_type=F32)
            if masked:
                sc = jnp.where(col <= row, sc, -1e30)
            mn = jnp.maximum(m, jnp.max(sc, axis=-1, keepdims=True))
            p = jnp.exp2(sc - mn).astype(BF16)
            acc = jnp.exp2(m - mn) * acc + jnp.dot(p, v_ref[pl.ds(off, t), HP * hh:HP * (hh + 1)],
                                                  preferred_element_type=F32)
            return mn, acc

        def step(rows, ki, carry, masked):
            new = tuple(head_step(hh, rows, ki, carry[hh], masked) for hh in range(2))
            mk_ref[ki, rows] = jnp.where(lane < DH, jnp.broadcast_to(new[0][0], (t, 128)),
                                         jnp.broadcast_to(new[1][0], (t, 128)))
            return new

        init = (jnp.full((t, 1), -1e30, F32), jnp.zeros((t, 128), F32))
        top, bottom = slice(0, t), slice(t, 2 * t)

        def quad(j, carry):
            c0, c1 = carry
            c0 = step(top, 2 * j, c0, False)
            c1 = step(bottom, 2 * j, c1, False)
            c0 = step(top, 2 * j + 1, c0, False)
            c1 = step(bottom, 2 * j + 1, c1, False)
            return c0, c1

        c0, c1 = lax.fori_loop(0, pi, quad, ((init, init), (init, init)))
        f0 = step(top, 2 * pi, c0, True)
        c1 = step(bottom, 2 * pi, c1, False)
        f1 = step(bottom, 2 * pi + 1, c1, True)
        for rows, ((m0, acc0), (m1, acc1)) in ((top, f0), (bottom, f1)):
            l0, l1 = acc0[:, DH:DH + 1], acc1[:, DH:DH + 1]
            o_ref[rows, :] = jnp.where(lane < DH, acc0 / l0, pltpu.roll(acc1 / l1, DH, axis=1))
            lse_ref[rows, :] = jnp.where(lane < DH, jnp.broadcast_to(m0 + jnp.log2(l0), (t, 128)),
                                         jnp.broadcast_to(m1 + jnp.log2(l1), (t, 128)))

    return pl.pallas_call(
        body, name="attn_fwd", grid=(H // 2, nq // 2),
        in_specs=[pl.BlockSpec((2 * t, 2 * HP), lambda p, i: (i, p)),
                  pl.BlockSpec((s, 2 * HP), lambda p, i: (0, p)),
                  pl.BlockSpec((s, 2 * HP), lambda p, i: (0, p))],
        out_specs=[pl.BlockSpec((2 * t, 128), lambda p, i: (i, p)), pl.BlockSpec((2 * t, 128), lambda p, i: (i, p)),
                   pl.BlockSpec((nq, 2 * t, 128), lambda p, i: (0, i, p))],
        out_shape=[jax.ShapeDtypeStruct((s, AW), F32), jax.ShapeDtypeStruct((s, AW), F32),
                   jax.ShapeDtypeStruct((nq, s, AW), F32)],
        compiler_params=_cparams(48, ("arbitrary", "arbitrary")),
    )(qp, kp, v)


HALO = 16


def _conv_taps(bcu_ref, halo_ref, first, tm):
    z = bcu_ref[:, CW:2 * CW].astype(F32) * bcu_ref[:, 2 * CW:3 * CW].astype(F32)
    zh = jnp.where(first, 0.0, halo_ref[:, CW:2 * CW].astype(F32) * halo_ref[:, 2 * CW:3 * CW].astype(F32))
    row = lax.broadcasted_iota(jnp.int32, (tm, CW), 0)
    last, before = zh[HALO - 1:HALO, :], zh[HALO - 2:HALO - 1, :]
    z1 = jnp.where(row == 0, last, pltpu.roll(z, 1, axis=0))
    z2 = jnp.where(row == 0, before, jnp.where(row == 1, last, pltpu.roll(z, 2, axis=0)))
    return z, z1, z2


def _halo_before(tm, width):
    return pl.BlockSpec((HALO, width), lambda i: (jnp.maximum(i * (tm // HALO) - 1, 0), 0))


def _mix_out(o, bcu, cw8, ga, gc, gsum, w_out, x, g_post, g_ffn_pre, *, tm):
    s = x.shape[0]

    def body(o_ref, bcu_ref, halo_ref, cw_ref, ga_ref, gc_ref, gs_ref, w_ref, x_ref, g_ref, gf_ref,
             merged_ref, y_ref, x2_ref, cv_ref, h2_ref):
        z, z1, z2 = _conv_taps(bcu_ref, halo_ref, pl.program_id(0) == 0, tm)
        cv = cw_ref[0:1, :] * z2 + cw_ref[1:2, :] * z1 + cw_ref[2:3, :] * z
        cv_ref[...] = cv
        conv = bcu_ref[:, 0:CW].astype(F32) * cv
        ov = o_ref[...]
        ra = lax.rsqrt(_group_sum(ov * ov, gs_ref[...]) * (1.0 / DH) + EPS)
        rc = lax.rsqrt(_group_sum(conv * conv, gs_ref[...]) * (1.0 / DH) + EPS)
        merged = jnp.concatenate([ov * ra * ga_ref[...], conv * rc * gc_ref[...]], axis=1).astype(BF16)
        merged_ref[...] = merged
        y = jnp.dot(merged, w_ref[...], preferred_element_type=F32)
        y_ref[...] = y
        x2 = x_ref[...] + _rms_fwd(y, g_ref[...])[0]
        x2_ref[...] = x2
        h2_ref[...] = _rms_fwd(x2, gf_ref[...])[0].astype(BF16)

    return pl.pallas_call(
        body, name="mix_out", grid=(s // tm,),
        in_specs=[_rows(tm, AW), _rows(tm, 3 * CW), _halo_before(tm, 3 * CW), _full((SUBLANES, CW)),
                  _full((1, AW)), _full((1, CW)), _full((GS, GS)), _resident((D, D)), _rows(tm, D), _full((1, D)),
                  _full((1, D))],
        out_specs=[_rows(tm, D), _rows(tm, D), _rows(tm, D), _rows(tm, CW), _rows(tm, D)],
        out_shape=[jax.ShapeDtypeStruct((s, D), BF16), jax.ShapeDtypeStruct((s, D), F32),
                   jax.ShapeDtypeStruct((s, D), F32), jax.ShapeDtypeStruct((s, CW), F32),
                   jax.ShapeDtypeStruct((s, D), BF16)],
        compiler_params=_cparams(48, ("arbitrary",)),
    )(o, bcu, bcu, cw8, ga, gc, gsum, w_out, x, g_post, g_ffn_pre)


def _ffn_fwd_loss(h2, wgu, wd, x2, target, g_post, *, tm):
    s = x2.shape[0]

    def body(h_ref, w_ref, wd_ref, x2_ref, t_ref, g_ref,
             gate_ref, up_ref, a_ref, dx3_ref, dff_ref, loss_ref, dg_ref):
        @pl.when(pl.program_id(0) == 0)
        def _():
            loss_ref[...] = jnp.zeros_like(loss_ref)
            dg_ref[...] = jnp.zeros_like(dg_ref)

        for rows in (slice(0, tm // 2), slice(tm // 2, tm)):
            h = h_ref[rows, :]
            ff = None
            for c0, n in FF_CHUNKS:
                cols = slice(c0, c0 + n)
                gate = lax.dot_general(h, w_ref[0, cols, :], NT, preferred_element_type=F32)
                up = lax.dot_general(h, w_ref[1, cols, :], NT, preferred_element_type=F32)
                gate_ref[rows, cols] = gate.astype(BF16)
                up_ref[rows, cols] = up.astype(BF16)
                act = (gate * jax.nn.sigmoid(gate) * up).astype(BF16)
                a_ref[rows, cols] = act
                part = jnp.dot(act, wd_ref[cols, :], preferred_element_type=F32)
                ff = part if ff is None else ff + part
            out, n, r = _rms_fwd(ff, g_ref[...])
            e = x2_ref[rows, :] + out - t_ref[rows, :]
            loss_ref[...] += _fold8(e * e)
            dx3 = e * (1.0 / D)
            dx3_ref[rows, :] = dx3
            dff, dg = _rms_bwd(dx3, n, r, g_ref[...])
            dff_ref[rows, :] = dff.astype(BF16)
            dg_ref[...] += _fold8(dg)

    wide = _rows(tm, DFF)
    return pl.pallas_call(
        body, name="ffn_fwd_loss", grid=(s // tm,),
        in_specs=[_rows(tm, D), _resident((2, DFF, D)), _resident((DFF, D)), _rows(tm, D), _rows(tm, D), _full((1, D))],
        out_specs=[wide, wide, wide, _rows(tm, D), _rows(tm, D), _full((SUBLANES, D)), _full((SUBLANES, D))],
        out_shape=[jax.ShapeDtypeStruct((s, DFF), BF16)] * 3
        + [jax.ShapeDtypeStruct((s, D), F32), jax.ShapeDtypeStruct((s, D), BF16),
           jax.ShapeDtypeStruct((SUBLANES, D), F32), jax.ShapeDtypeStruct((SUBLANES, D), F32)],
        compiler_params=_cparams(56, ("arbitrary",)),
    )(h2, wgu, wd, x2, target, g_post)


def _ffn_bwd(dff, wd, gate, up, wgu, x2, g_pre, dx3, y, g_post, *, tm):
    s = x2.shape[0]

    def body(dff_ref, wd_ref, gate_ref, up_ref, w_ref, x2_ref, gpre_ref, dx3_ref, y_ref, gpost_ref,
             dgu_ref, dx2_ref, dy_ref, dgpre_ref, dgpost_ref):
        @pl.when(pl.program_id(0) == 0)
        def _():
            dgpre_ref[...] = jnp.zeros_like(dgpre_ref)
            dgpost_ref[...] = jnp.zeros_like(dgpost_ref)

        dff = dff_ref[...]
        dh2 = None
        for c0, n in FF_CHUNKS_BWD:
            cols = slice(c0, c0 + n)
            da = lax.dot_general(dff, wd_ref[cols, :], NT, preferred_element_type=F32)
            g = gate_ref[:, cols].astype(F32)
            sg = jax.nn.sigmoid(g)
            dgate = (da * up_ref[:, cols].astype(F32) * (sg * (1.0 + g * (1.0 - sg)))).astype(BF16)
            dup = (da * (g * sg)).astype(BF16)
            dgu_ref[:, cols] = dgate
            dgu_ref[:, DFF + c0:DFF + c0 + n] = dup
            part = (jnp.dot(dgate, w_ref[0, cols, :], preferred_element_type=F32)
                    + jnp.dot(dup, w_ref[1, cols, :], preferred_element_type=F32))
            dh2 = part if dh2 is None else dh2 + part
        _, n2, r2 = _rms_fwd(x2_ref[...], gpre_ref[...])
        dxn, dg = _rms_bwd(dh2, n2, r2, gpre_ref[...])
        dgpre_ref[...] += _fold8(dg)
        dx2 = dx3_ref[...] + dxn
        dx2_ref[...] = dx2
        _, ny, ry = _rms_fwd(y_ref[...], gpost_ref[...])
        dy, dg2 = _rms_bwd(dx2, ny, ry, gpost_ref[...])
        dy_ref[...] = dy.astype(BF16)
        dgpost_ref[...] += _fold8(dg2)

    wide = _rows(tm, DFF)
    return pl.pallas_call(
        body, name="ffn_bwd", grid=(s // tm,),
        in_specs=[_rows(tm, D), _resident((DFF, D)), wide, wide, _resident((2, DFF, D)), _rows(tm, D), _full((1, D)),
                  _rows(tm, D), _rows(tm, D), _full((1, D))],
        out_specs=[_rows(tm, 2 * DFF), _rows(tm, D), _rows(tm, D),
                   _full((SUBLANES, D)), _full((SUBLANES, D))],
        out_shape=[jax.ShapeDtypeStruct((s, 2 * DFF), BF16), jax.ShapeDtypeStruct((s, D), F32),
                   jax.ShapeDtypeStruct((s, D), BF16), jax.ShapeDtypeStruct((SUBLANES, D), F32),
                   jax.ShapeDtypeStruct((SUBLANES, D), F32)],
        compiler_params=_cparams(56, ("arbitrary",)),
    )(dff, wd, gate, up, wgu, x2, g_pre, dx3, y, g_post)


def _grad_matmul(a, b, *, ta, tb, ts, name, vmem_mb=48):
    s, ka = a.shape
    nb = b.shape[1]
    ts = min(ts, s)
    nk = s // ts

    def body(a_ref, b_ref, o_ref, *acc):
        if nk == 1:
            o_ref[...] = lax.dot_general(a_ref[...], b_ref[...], TN, preferred_element_type=F32).astype(BF16)
            return
        k = pl.program_id(2)

        @pl.when(k == 0)
        def _():
            acc[0][...] = jnp.zeros_like(acc[0])

        acc[0][...] += lax.dot_general(a_ref[...], b_ref[...], TN, preferred_element_type=F32)

        @pl.when(k == nk - 1)
        def _():
            o_ref[...] = acc[0][...].astype(BF16)

    whole_b = {"pipeline_mode": pl.Buffered(1)} if nk == 1 and nb == tb else {}
    return pl.pallas_call(
        body, name=name, grid=(ka // ta, nb // tb, nk),
        in_specs=[pl.BlockSpec((ts, ta), lambda i, j, k: (k, i)),
                  pl.BlockSpec((ts, tb), lambda i, j, k: (k, j), **whole_b)],
        out_specs=pl.BlockSpec((ta, tb), lambda i, j, k: (i, j)),
        out_shape=jax.ShapeDtypeStruct((ka, nb), BF16),
        scratch_shapes=[pltpu.VMEM((ta, tb), F32)] if nk > 1 else [],
        compiler_params=_cparams(vmem_mb, ("arbitrary", "arbitrary", "arbitrary")),
    )(a, b)


GW_TILE = 256


def _grad_w_in(h1t, pieces):
    ka, s = h1t.shape
    widths = [p.shape[1] for p in pieces]
    assert all(w % GW_TILE == 0 for w in widths)
    first = [sum(widths[:i]) // GW_TILE for i in range(len(pieces))]
    count = [w // GW_TILE for w in widths]

    def body(a_ref, *refs):
        o_ref = refs[-1]
        j = pl.program_id(0)
        for ref, f0, n in zip(refs[:-1], first, count):
            @pl.when((j >= f0) & (j < f0 + n))
            def _(ref=ref):
                o_ref[...] = jnp.dot(a_ref[...], ref[...], preferred_element_type=F32).astype(BF16)

    def spec(f0, n):
        return pl.BlockSpec((s, GW_TILE), lambda j: (0, jnp.clip(j - f0, 0, n - 1)))

    return pl.pallas_call(
        body, name="grad_w_in", grid=(sum(count),),
        in_specs=[_resident((ka, s))] + [spec(f0, n) for f0, n in zip(first, count)],
        out_specs=pl.BlockSpec((ka, GW_TILE), lambda j: (0, j)),
        out_shape=jax.ShapeDtypeStruct((ka, sum(widths)), BF16),
        compiler_params=_cparams(56, ("arbitrary",)),
    )(h1t, *pieces)


def _mix_bwd(dy, w_out, o, cv, bcu, ga, gc, gsum, after, *, tm):
    s = dy.shape[0]

    def group_norm_bwd(dn_out, v, g, gs):
        r = lax.rsqrt(_group_sum(v * v, gs) * (1.0 / DH) + EPS)
        n = v * r
        dn = dn_out * g
        return r * (dn - n * (_group_sum(dn * n, gs) * (1.0 / DH))), dn_out * n

    def body(dy_ref, w_ref, o_ref, cv_ref, bcu_ref, ga_ref, gc_ref, gs_ref, after_ref,
             do_ref, dl_ref, dcv_ref, db_ref, dga_ref, dgc_ref):
        @pl.when(pl.program_id(0) == 0)
        def _():
            dga_ref[...] = jnp.zeros_like(dga_ref)
            dgc_ref[...] = jnp.zeros_like(dgc_ref)

        dm = lax.dot_general(dy_ref[...], w_ref[...], NT, preferred_element_type=F32)
        ov = o_ref[...]
        do, dga = group_norm_bwd(dm[:, 0:AW], ov, ga_ref[...], gs_ref[...])
        dob = do.astype(BF16)
        do_ref[...] = dob
        dl_ref[...] = _group_sum(dob.astype(F32) * ov, gs_ref[...])
        dga_ref[...] += _fold8(dga)
        gate_b = bcu_ref[:, 0:CW].astype(F32)
        cv = cv_ref[...]
        dconv, dgc = group_norm_bwd(dm[:, AW:D], gate_b * cv, gc_ref[...], gs_ref[...])
        dgc_ref[...] += _fold8(dgc)
        dcv_ref[...] = dconv * gate_b
        db_ref[...] = (dconv * cv).astype(BF16)

    return pl.pallas_call(
        body, name="mix_bwd", grid=(s // tm,),
        in_specs=[_rows(tm, D), _resident((D, D)), _rows(tm, AW), _rows(tm, CW), _rows(tm, 3 * CW),
                  _full((1, AW)), _full((1, CW)), _full((GS, GS)), ANY],
        out_specs=[_rows(tm, AW), _rows(tm, AW), _rows(tm, CW), _rows(tm, CW),
                   _full((SUBLANES, AW)), _full((SUBLANES, CW))],
        out_shape=[jax.ShapeDtypeStruct((s, AW), BF16), jax.ShapeDtypeStruct((s, AW), F32),
                   jax.ShapeDtypeStruct((s, CW), F32), jax.ShapeDtypeStruct((s, CW), BF16),
                   jax.ShapeDtypeStruct((SUBLANES, AW), F32), jax.ShapeDtypeStruct((SUBLANES, CW), F32)],
        compiler_params=_cparams(48, ("arbitrary",)),
    )(dy, w_out, o, cv, bcu, ga, gc, gsum, after)


def _conv_bwd(dcv, db, bcu, cw8, *, tm):
    s = dcv.shape[0]
    nt = s // tm

    def body(dcv_ref, nxt_ref, db_ref, bcu_ref, halo_ref, cw_ref, dbcu_ref, dw_ref):
        i = pl.program_id(0)

        @pl.when(i == 0)
        def _():
            dw_ref[...] = jnp.zeros_like(dw_ref)

        z, z1, z2 = _conv_taps(bcu_ref, halo_ref, i == 0, tm)
        d = dcv_ref[...]
        dw_ref[0] += _fold8(d * z2)
        dw_ref[1] += _fold8(d * z1)
        dw_ref[2] += _fold8(d * z)
        nx = jnp.where(i == nt - 1, 0.0, nxt_ref[...])
        row = lax.broadcasted_iota(jnp.int32, (tm, CW), 0)
        d1 = jnp.where(row == tm - 1, nx[0:1, :], pltpu.roll(d, tm - 1, axis=0))
        d2 = jnp.where(row == tm - 2, nx[0:1, :], jnp.where(row == tm - 1, nx[1:2, :], pltpu.roll(d, tm - 2, axis=0)))
        dz = cw_ref[2:3, :] * d + cw_ref[1:2, :] * d1 + cw_ref[0:1, :] * d2
        dbcu_ref[:, 0:CW] = db_ref[...]
        dbcu_ref[:, CW:2 * CW] = (dz * bcu_ref[:, 2 * CW:3 * CW].astype(F32)).astype(BF16)
        dbcu_ref[:, 2 * CW:3 * CW] = (dz * bcu_ref[:, CW:2 * CW].astype(F32)).astype(BF16)

    return pl.pallas_call(
        body, name="conv_bwd", grid=(nt,),
        in_specs=[_rows(tm, CW),
                  pl.BlockSpec((SUBLANES, CW), lambda i: (jnp.minimum((i + 1) * (tm // SUBLANES), s // SUBLANES - 1), 0)),
                  _rows(tm, CW), _rows(tm, 3 * CW), _halo_before(tm, 3 * CW), _full((SUBLANES, CW))],
        out_specs=[_rows(tm, 3 * CW), _full((3, SUBLANES, CW))],
        out_shape=[jax.ShapeDtypeStruct((s, 3 * CW), BF16), jax.ShapeDtypeStruct((3, SUBLANES, CW), F32)],
        compiler_params=_cparams(48, ("arbitrary",)),
    )(dcv, dcv, db, bcu, bcu, cw8)


def _attn_bwd(qp, kp, v, do, lse, dl, mk, *, t):
    s = qp.shape[0]
    nq = s // t

    def body(q_ref, k_ref, v_ref, do_ref, lse_ref, dl_ref, mk_ref, dq_ref, dk_ref, dv_ref, dkx_ref, dq_acc):
        pi = pl.program_id(1)

        @pl.when(pi == 0)
        def _():
            dq_acc[...] = jnp.zeros_like(dq_acc)

        row = lax.broadcasted_iota(jnp.int32, (t, t), 0)
        col = lax.broadcasted_iota(jnp.int32, (t, t), 1)
        lane = lax.broadcasted_iota(jnp.int32, (t, 128), 1)

        def head_step(hh, qi, carry, modes):
            off = pl.multiple_of(qi * t, t)
            rows = pl.ds(off, t)
            q = q_ref[rows, HP * hh:HP * (hh + 1)]
            qt = q.T
            lse_col = lse_ref[rows, DH * hh:DH * hh + 1]
            dl_col = dl_ref[rows, DH * hh:DH * hh + 1]
            do2 = do_ref[rows, :]
            dom = jnp.where(lane < DH, do2 if hh == 0 else pltpu.roll(do2, DH, axis=1), jnp.zeros((), BF16))
            new, dss = [], []
            for half, masked in enumerate(modes):
                if masked is None:
                    new.append(carry[half])
                    continue
                dk, dv, cs = carry[half]
                keys = slice(half * t, (half + 1) * t)
                m_col = mk_ref[half, rows, DH * hh:DH * hh + 1]
                scale = jnp.exp2(m_col - lse_col)
                sc = lax.dot_general(q, k_ref[keys, HP * hh:HP * (hh + 1)], NT, preferred_element_type=F32) - m_col
                if masked:
                    sc = jnp.where(col <= row, sc, -1e30)
                pt = jnp.exp2(sc).astype(BF16)
                dp = lax.dot_general(dom, v_ref[keys, HP * hh:HP * (hh + 1)], NT, preferred_element_type=F32)
                ds32 = (pt.astype(F32) * scale) * (dp - dl_col)
                ds = ds32.astype(BF16)
                cs = cs + _fold8(ds32)
                dv = dv + jnp.dot((dom.astype(F32) * scale).astype(BF16).T, pt, preferred_element_type=F32)
                dk = dk + jnp.dot(qt, ds, preferred_element_type=F32)
                new.append((dk, dv, cs))
                dss.append((half, ds))
            if len(dss) == 2:
                dq = jnp.dot(jnp.concatenate([dss[0][1], dss[1][1]], axis=1), k_ref[:, HP * hh:HP * (hh + 1)],
                             preferred_element_type=F32)
            else:
                half, ds = dss[0]
                dq = jnp.dot(ds, k_ref[half * t:(half + 1) * t, HP * hh:HP * (hh + 1)], preferred_element_type=F32)
            dq_acc[rows, HP * hh:HP * (hh + 1)] += dq
            return tuple(new)

        def step(qi, carry, modes):
            return tuple(head_step(hh, qi, carry[hh], modes) for hh in range(2))

        def two_heads(a0, a1):
            return jnp.where(lane < DH, a0, pltpu.roll(a1, DH, axis=1))

        def rows_to_lanes(a0, a1):
            return jnp.concatenate([a0, a1], axis=0).T

        zero = (jnp.zeros((HP, t), F32), jnp.zeros((128, t), F32), jnp.zeros((SUBLANES, t), F32))
        carry = step(2 * pi, ((zero, zero), (zero, zero)), (True, None))
        carry = step(2 * pi + 1, carry, (False, True))

        def pair(j, carry):
            qi = 2 * (pi + 1 + j)
            return step(qi + 1, step(qi, carry, (False, False)), (False, False))

        carry = lax.fori_loop(0, nq // 2 - 1 - pi, pair, carry)
        for half in range(2):
            keys = slice(half * t, (half + 1) * t)
            (dk0, dv0, cs0), (dk1, dv1, cs1) = carry[0][half], carry[1][half]
            dk_ref[keys, :] = (rows_to_lanes(dk0[0:DH], dk1[0:DH]) * LN2).astype(BF16)
            dv_ref[keys, :] = rows_to_lanes(dv0[0:DH], dv1[0:DH]).astype(BF16)
            total = lambda cs: jnp.broadcast_to(jnp.sum(cs, axis=0, keepdims=True), (DH, t))
            dkx_ref[keys, :] = rows_to_lanes(total(cs0), total(cs1))

        @pl.when(pi == nq // 2 - 1)
        def _():
            for c in range(s // t):
                rows = slice(c * t, (c + 1) * t)
                dq_ref[rows, :] = two_heads(dq_acc[rows, 0:HP], dq_acc[rows, HP:2 * HP]).astype(BF16)

    return pl.pallas_call(
        body, name="attn_bwd", grid=(H // 2, nq // 2),
        in_specs=[pl.BlockSpec((s, 2 * HP), lambda p, i: (0, p)),
                  pl.BlockSpec((2 * t, 2 * HP), lambda p, i: (i, p)),
                  pl.BlockSpec((2 * t, 2 * HP), lambda p, i: (i, p)),
                  pl.BlockSpec((s, 128), lambda p, i: (0, p)),
                  pl.BlockSpec((s, 128), lambda p, i: (0, p)),
                  pl.BlockSpec((s, 128), lambda p, i: (0, p)),
                  pl.BlockSpec((2, s, 128), lambda p, i: (i, 0, p))],
        out_specs=[pl.BlockSpec((s, 128), lambda p, i: (0, p)),
                   pl.BlockSpec((2 * t, 128), lambda p, i: (i, p)),
                   pl.BlockSpec((2 * t, 128), lambda p, i: (i, p)),
                   pl.BlockSpec((2 * t, 128), lambda p, i: (i, p))],
        out_shape=[jax.ShapeDtypeStruct((s, AW), BF16), jax.ShapeDtypeStruct((s, AW), BF16),
                   jax.ShapeDtypeStruct((s, AW), BF16), jax.ShapeDtypeStruct((s, AW), F32)],
        scratch_shapes=[pltpu.VMEM((s, 2 * HP), F32)],
        compiler_params=_cparams(56, ("arbitrary", "arbitrary")),
    )(qp, kp, v, do, lse, dl, mk)


def _forget_bwd(dkx, z, sel, *, tm):
    s = dkx.shape[0]
    nt = s // tm

    def body(dk_ref, z_ref, sel_ref, dfl_ref, dbf_ref, carry):
        @pl.when(pl.program_id(0) == 0)
        def _():
            carry[...] = jnp.zeros_like(carry)
            dbf_ref[...] = jnp.zeros_like(dbf_ref)

        dc = _split_dot(dk_ref[...], sel_ref[...])
        row = lax.broadcasted_iota(jnp.int32, (tm, tm), 0)
        col = lax.broadcasted_iota(jnp.int32, (tm, tm), 1)
        tri = (col >= row).astype(BF16)
        dlogf = _exact_dot01(tri, dc) + carry[0:1, :]
        carry[...] = jnp.broadcast_to(dlogf[0:1, :], carry.shape)
        dz = dlogf * (1.0 - jax.nn.sigmoid(z_ref[...]))
        dfl_ref[:, 0:128] = dz.astype(BF16)
        dfl_ref[:, 128:GW_TILE] = jnp.zeros((tm, GW_TILE - 128), BF16)
        dbf_ref[...] += _fold8(dz)

    rev = lambda i: (nt - 1 - i, 0)
    return pl.pallas_call(
        body, name="forget_bwd", grid=(nt,),
        in_specs=[pl.BlockSpec((tm, AW), rev), pl.BlockSpec((tm, 128), rev), _full((AW, 128))],
        out_specs=[pl.BlockSpec((tm, GW_TILE), rev), _full((SUBLANES, 128))],
        out_shape=[jax.ShapeDtypeStruct((s, GW_TILE), BF16), jax.ShapeDtypeStruct((SUBLANES, 128), F32)],
        scratch_shapes=[pltpu.VMEM((SUBLANES, 128), F32)],
        compiler_params=_cparams(48, ("arbitrary",)),
    )(dkx, z, sel)


def _in_proj_bwd(pieces, wp, x, g1, dx2, after, *, tm):
    s = x.shape[0]

    def body(q_ref, k_ref, v_ref, bcu_ref, f_ref, w_ref, x_ref, g_ref, dx2_ref, after_ref, dx_ref, dg_ref):
        @pl.when(pl.program_id(0) == 0)
        def _():
            dg_ref[...] = jnp.zeros_like(dg_ref)

        dh = None
        for ref, (lo, hi) in zip((q_ref, k_ref, v_ref, bcu_ref, f_ref), PIECES):
            part = lax.dot_general(ref[...], w_ref[:, lo:hi], NT, preferred_element_type=F32)
            dh = part if dh is None else dh + part
        _, n, r = _rms_fwd(x_ref[...], g_ref[...])
        dxn, dg = _rms_bwd(dh, n, r, g_ref[...])
        dx_ref[...] = dx2_ref[...] + dxn
        dg_ref[...] += _fold8(dg)

    return pl.pallas_call(
        body, name="in_proj_bwd", grid=(s // tm,),
        in_specs=[_rows(tm, hi - lo) for lo, hi in PIECES]
        + [_resident((D, WP)), _rows(tm, D), _full((1, D)), _rows(tm, D), ANY],
        out_specs=[_rows(tm, D), _full((SUBLANES, D))],
        out_shape=[jax.ShapeDtypeStruct((s, D), F32), jax.ShapeDtypeStruct((SUBLANES, D), F32)],
        compiler_params=_cparams(56, ("arbitrary",)),
    )(*pieces, wp, x, g1, dx2, after)


def _position():
    return lax.axis_index("x"), lax.axis_index("y"), lax.axis_index("c")


ANY = pl.BlockSpec(memory_space=pl.ANY)


def _all_gather(shards):
    n = len(shards)

    def body(*refs):
        x_refs, out_refs = refs[:n], refs[n:2 * n]
        send_sems, recv_sems, local_sems = refs[2 * n:]
        x, y, c = _position()
        me, sibling = (x, y, c), (x, y, 1 - c)
        chips = [(1 - x, y), (x, 1 - y), (1 - x, 1 - y)]

        def copy(a, k, block, to, own=False):
            slot = out_refs[a].at[4 * block[0] + 2 * block[1] + block[2]]
            return pltpu.make_async_remote_copy(
                src_ref=x_refs[a] if own else slot, dst_ref=slot,
                send_sem=send_sems.at[7 * a + k], recv_sem=recv_sems.at[7 * a + k], device_id=to, device_id_type=MESH_ID)

        mine = [pltpu.make_async_copy(x_refs[a], out_refs[a].at[4 * x + 2 * y + c], local_sems.at[a]) for a in range(n)]
        for cp in mine:
            cp.start()
        first = []
        for a in range(n):
            first.append(copy(a, 0, me, sibling, own=True))
            first += [copy(a, 1 + j, me, (*chip, c), own=True) for j, chip in enumerate(chips)]
        for cp in first:
            cp.start()
        passed = []
        for j, chip in enumerate(chips):
            for a in range(n):
                copy(a, 1 + j, (*chip, c), me).wait_recv()
                fwd = copy(a, 4 + j, (*chip, c), sibling)
                fwd.start()
                passed.append(fwd)
        for a in range(n):
            copy(a, 0, sibling, me).wait_recv()
            for j, chip in enumerate(chips):
                copy(a, 4 + j, (*chip, 1 - c), me).wait_recv()
        for cp in first + passed:
            cp.wait_send()
        for cp in mine:
            cp.wait()

    return pl.pallas_call(
        body, name="all_gather_weights",
        out_shape=[jax.ShapeDtypeStruct((NDEV,) + sh.shape, sh.dtype) for sh in shards],
        in_specs=[ANY] * n, out_specs=[ANY] * n,
        scratch_shapes=[pltpu.SemaphoreType.DMA((7 * n,)), pltpu.SemaphoreType.DMA((7 * n,)), pltpu.SemaphoreType.DMA((n,))],
    )(*shards)


def _pair_exchange(grads):
    n = len(grads)

    def body(*refs):
        g_refs, out_refs = refs[:n], refs[n:2 * n]
        send_sems, recv_sems = refs[2 * n:]
        x, y, c = _position()
        copies = [pltpu.make_async_remote_copy(
            src_ref=g_refs[a].at[:, pl.ds(1 - c, 1)], dst_ref=out_refs[a], send_sem=send_sems.at[a],
            recv_sem=recv_sems.at[a], device_id=(x, y, 1 - c), device_id_type=MESH_ID) for a in range(n)]
        for cp in copies:
            cp.start()
        for cp in copies:
            cp.wait()

    return pl.pallas_call(
        body, name="grad_pair_exchange",
        out_shape=[jax.ShapeDtypeStruct((4, 1) + g.shape[2:], g.dtype) for g in grads],
        in_specs=[ANY] * n, out_specs=[ANY] * n,
        scratch_shapes=[pltpu.SemaphoreType.DMA((n,)), pltpu.SemaphoreType.DMA((n,))],
    )(*grads)


def _pair_sum(g, got, idx, *, tr, name):
    r, c = g.shape[2:]

    def body(idx_ref, g_ref, got_ref, pb_ref, own_ref):
        p = g_ref[0, 0].astype(F32) + got_ref[0, 0].astype(F32)
        pb_ref[0] = p.astype(BF16)

        @pl.when(pl.program_id(1) == idx_ref[1])
        def _():
            own_ref[...] = p

    return pl.pallas_call(
        body, name=name,
        grid_spec=pltpu.PrefetchScalarGridSpec(
            num_scalar_prefetch=1, grid=(r // tr, 4),
            in_specs=[pl.BlockSpec((1, 1, tr, c), lambda i, j, idx: (j, idx[0], i, 0)),
                      pl.BlockSpec((1, 1, tr, c), lambda i, j, idx: (j, 0, i, 0))],
            out_specs=[pl.BlockSpec((1, tr, c), lambda i, j, idx: (j, i, 0)),
                       pl.BlockSpec((tr, c), lambda i, j, idx: (i, 0))]),
        out_shape=[jax.ShapeDtypeStruct((4, r, c), BF16), jax.ShapeDtypeStruct((r, c), F32)],
        compiler_params=_cparams(62, ("arbitrary", "arbitrary")),
    )(idx, g, got)


HBM = pl.BlockSpec(memory_space=pltpu.HBM)
SEM = pl.BlockSpec(memory_space=pltpu.SEMAPHORE)
DATAFLOW = pltpu.SideEffectType.DATAFLOW_SIDE_EFFECTING


PEERS = {"gather": NDEV - 1, "scatter": NDEV - 1, "chips": 3}


def _exchange_copies(src_refs, land_refs, send_sems, recv_sems, mode):
    x, y, c = _position()
    me, my_chip = 4 * x + 2 * y + c, 2 * x + y
    npeers = PEERS[mode]
    copies, own = [], []
    for a, (s_ref, l_ref) in enumerate(zip(src_refs, land_refs)):
        for k in range(npeers):
            if mode == "chips":
                px, py, pc = x ^ ((k + 1) >> 1), y ^ ((k + 1) & 1), c
                src, dst = s_ref.at[2 * px + py], l_ref.at[my_chip]
            else:
                px, py, pc = x ^ ((k + 1) >> 2), y ^ (((k + 1) >> 1) & 1), c ^ ((k + 1) & 1)
                src, dst = (s_ref.at[4 * px + 2 * py + pc] if mode == "scatter" else s_ref), l_ref.at[me]
            copies.append(pltpu.make_async_remote_copy(
                src_ref=src, dst_ref=dst, send_sem=send_sems.at[npeers * a + k], recv_sem=recv_sems.at[npeers * a + k],
                device_id=(px, py, pc), device_id_type=MESH_ID))
        slot = my_chip if mode == "chips" else me
        own.append(pltpu.make_async_copy(s_ref if mode == "gather" else s_ref.at[slot], l_ref.at[slot],
                                         send_sems.at[npeers * len(src_refs) + a]))
    return copies, own


def _exchange_start(srcs, lands, after, *, mode, name):
    n = len(srcs)
    nsem = PEERS[mode] * n

    def body(*refs):
        token = refs[-1]
        copies, own = _exchange_copies(refs[:n], refs[n:2 * n], refs[2 * n + 1], refs[2 * n + 2], mode)
        for cp in copies + own:
            cp.start()
        token[...] = jnp.zeros_like(token)

    arrays = list(srcs) + list(lands)
    outs = pl.pallas_call(
        body, name=name,
        out_shape=(pltpu.SemaphoreType.DMA((nsem + n,)), pltpu.SemaphoreType.DMA((nsem,)),
                   *[pltpu.HBM(a.shape, a.dtype) for a in arrays], jax.ShapeDtypeStruct((SUBLANES, LANES), F32)),
        in_specs=[HBM] * (2 * n) + [ANY],
        out_specs=(SEM, SEM, *[HBM] * (2 * n), pl.BlockSpec(memory_space=pltpu.VMEM)),
        input_output_aliases={i: 2 + i for i in range(2 * n)},
        compiler_params=pltpu.CompilerParams(has_side_effects=DATAFLOW),
    )(*[pltpu.with_memory_space_constraint(a, pltpu.HBM) for a in arrays], after)
    return outs[0], outs[1], outs[2:2 + n], outs[2 + n:2 + 2 * n], outs[-1]


def _exchange_wait(send_sems, recv_sems, srcs, lands, after, *, mode, name):
    n = len(srcs)

    def body(*refs):
        copies, own = _exchange_copies(refs[:n], refs[n:2 * n], refs[2 * n], refs[2 * n + 1], mode)
        for cp in copies:
            cp.wait_send()
            cp.wait_recv()
        for cp in own:
            cp.wait()

    arrays = list(srcs) + list(lands)
    outs = pl.pallas_call(
        body, name=name,
        out_shape=tuple(pltpu.HBM(a.shape, a.dtype) for a in arrays),
        in_specs=[HBM] * (2 * n) + [SEM, SEM, ANY],
        out_specs=tuple([HBM] * (2 * n)),
        input_output_aliases={i: i for i in range(2 * n)},
        compiler_params=pltpu.CompilerParams(has_side_effects=DATAFLOW),
    )(*arrays, send_sems, recv_sems, after)
    return outs[n:]


def _small_start(parts):
    def body(gmp_ref, gmo_ref, gfp_ref, gfo_ref, ga_ref, gc_ref, dw_ref, bf_ref, loss_ref, land_in,
             send_sems, recv_sems, share_ref, land_ref, token, buf, put_sem):
        def colsum(v):
            return jnp.sum(v, axis=0, keepdims=True)

        loss = jnp.sum(colsum(loss_ref[...]), axis=1, keepdims=True) * (0.5 / D)
        rows = [colsum(gmp_ref[...]), colsum(gmo_ref[...]), colsum(gfp_ref[...]), colsum(gfo_ref[...]),
                jnp.concatenate([colsum(ga_ref[...]), colsum(gc_ref[...])], axis=1),
                jnp.concatenate([colsum(dw_ref[0]), colsum(dw_ref[1])], axis=1),
                jnp.concatenate([colsum(dw_ref[2]), colsum(bf_ref[...]), jnp.broadcast_to(loss, (1, 128)),
                                 jnp.zeros((1, 256), F32)], axis=1),
                jnp.zeros((1, D), F32)]
        buf[...] = jnp.concatenate(rows, axis=0)
        put = pltpu.make_async_copy(buf, share_ref, put_sem.at[0])
        put.start()
        put.wait()
        copies, own = _exchange_copies([share_ref], [land_ref], send_sems, recv_sems, "gather")
        for cp in copies + own:
            cp.start()
        token[...] = jnp.zeros_like(token)

    vm = pl.BlockSpec(memory_space=pltpu.VMEM)
    n = len(parts)
    land = lax.empty((NDEV, SUBLANES, D), F32)
    outs = pl.pallas_call(
        body, name="small_gather_start",
        out_shape=(pltpu.SemaphoreType.DMA((PEERS["gather"] + 1,)), pltpu.SemaphoreType.DMA((PEERS["gather"],)),
                   pltpu.HBM((SUBLANES, D), F32), pltpu.HBM(land.shape, land.dtype),
                   jax.ShapeDtypeStruct((SUBLANES, LANES), F32)),
        in_specs=[vm] * n + [HBM], out_specs=(SEM, SEM, HBM, HBM, vm),
        input_output_aliases={n: 3},
        scratch_shapes=[pltpu.VMEM((SUBLANES, D), F32), pltpu.SemaphoreType.DMA((1,))],
        compiler_params=pltpu.CompilerParams(has_side_effects=DATAFLOW),
    )(*parts, pltpu.with_memory_space_constraint(land, pltpu.HBM))
    return outs[0], outs[1], [outs[2]], [outs[3]], outs[4]


def _small_sum(land):
    def body(land_ref, out_ref):
        acc = land_ref[0]
        for d in range(1, NDEV):
            acc = acc + land_ref[d]
        out_ref[...] = acc

    return pl.pallas_call(
        body, name="small_sum", grid=(1,), out_shape=jax.ShapeDtypeStruct((SUBLANES, D), F32),
        in_specs=[pl.BlockSpec((NDEV, SUBLANES, D), lambda i: (0, 0, 0))],
        out_specs=pl.BlockSpec((SUBLANES, D), lambda i: (0, 0)),
    )(land)


def _adam_update(w, g, m, v):
    nm = ADAM_B1 * m + (1.0 - ADAM_B1) * g
    nv = ADAM_B2 * v + (1.0 - ADAM_B2) * (g * g)
    m_hat = nm / (1.0 - ADAM_B1 ** ADAM_STEP)
    v_hat = nv / (1.0 - ADAM_B2 ** ADAM_STEP)
    return -ADAM_LR * (m_hat / (jnp.sqrt(v_hat) + ADAM_EPS) + ADAM_WD * w), nm, nv


SMALL_SLOTS = {"g_mix_pre": (0, 0, D), "g_mix_post": (1, 0, D), "g_ffn_pre": (2, 0, D), "g_ffn_post": (3, 0, D),
               "g_attn_out": (4, 0, AW), "g_conv_out": (4, AW, CW), "b_forget": (6, CW, H)}
LOSS_LANE = CW + 128


def _small_adamw(small, conv_grad, params):
    names = list(params)
    n = len(names)

    def body(*refs):
        small_ref, cg_ref = refs[0], refs[1]
        ins, outs = refs[2:2 + 3 * n], refs[2 + 3 * n:]
        for i, name in enumerate(names):
            w_ref, m_ref, v_ref = ins[3 * i:3 * i + 3]
            g_ref, d_ref, nm_ref, nv_ref = outs[4 * i:4 * i + 4]
            if name == "conv_w":
                g = cg_ref[...]
            else:
                r, c0, width = SMALL_SLOTS[name]
                g = small_ref[r:r + 1, c0:c0 + width]
            g_ref[...] = g
            d_ref[...], nm_ref[...], nv_ref[...] = _adam_update(w_ref[...], g, m_ref[...], v_ref[...])
        outs[4 * n][...] = small_ref[6:7, LOSS_LANE:LOSS_LANE + 1]

    vm = pl.BlockSpec(memory_space=pltpu.VMEM)
    flat = [a for name in names for a in params[name]]
    outs = pl.pallas_call(
        body, name="adamw_small",
        in_specs=[vm] * (2 + 3 * n), out_specs=[vm] * (4 * n + 1),
        out_shape=[jax.ShapeDtypeStruct(params[name][0].shape, F32) for name in names for _ in range(4)]
        + [jax.ShapeDtypeStruct((1, 1), F32)],
    )(small, conv_grad, *flat)
    return {name: outs[4 * i:4 * i + 4] for i, name in enumerate(names)}, outs[4 * n].reshape(())


def _chip_sum_adamw(got, own, idx, wt, mt, vt, *, tr, name):
    cols, rows = wt.shape
    gcols = own.shape[1]

    def body(idx_ref, got_ref, own_ref, w_ref, m_ref, v_ref, g_ref, d_ref, nm_ref, nv_ref):
        g = jnp.zeros((tr, gcols), F32)
        for j in range(4):
            g = g + jnp.where(idx_ref[1] == j, own_ref[...], got_ref[j].astype(F32))
        g = g.T[:cols]
        g_ref[...] = g
        d_ref[...], nm_ref[...], nv_ref[...] = _adam_update(w_ref[...], g, m_ref[...], v_ref[...])

    spec = pl.BlockSpec((cols, tr), lambda i, idx: (0, i))
    gspec = pl.BlockSpec((tr, gcols), lambda i, idx: (i, 0))
    return pl.pallas_call(
        body, name=name,
        grid_spec=pltpu.PrefetchScalarGridSpec(
            num_scalar_prefetch=1, grid=(rows // tr,),
            in_specs=[pl.BlockSpec((4, tr, gcols), lambda i, idx: (0, i, 0)), gspec, spec, spec, spec],
            out_specs=[spec] * 4),
        out_shape=[jax.ShapeDtypeStruct((cols, rows), F32)] * 4,
        compiler_params=_cparams(32, ("arbitrary",)),
    )(idx, got, own, wt, mt, vt)


def _device_sum_adamw(land, w, m, v, *, tr, name):
    rows, cols = w.shape

    def body(land_ref, w_ref, m_ref, v_ref, g_ref, d_ref, nm_ref, nv_ref):
        g = land_ref[0].astype(F32)
        for dev in range(1, NDEV):
            g = g + land_ref[dev].astype(F32)
        g_ref[...] = g
        d_ref[...], nm_ref[...], nv_ref[...] = _adam_update(w_ref[...], g, m_ref[...], v_ref[...])

    spec = pl.BlockSpec((tr, cols), lambda i: (i, 0))
    return pl.pallas_call(
        body, name=name, grid=(rows // tr,),
        in_specs=[pl.BlockSpec((NDEV, tr, cols), lambda i: (0, i, 0)), spec, spec, spec],
        out_specs=[spec] * 4,
        out_shape=[jax.ShapeDtypeStruct((rows, cols), F32)] * 4,
        compiler_params=_cparams(32, ("arbitrary",)),
    )(land, w, m, v)


def _placement_constants():
    j = np.arange(128)[:, None]
    lane = np.arange(1024)[None, :]
    head, sub = lane // HP, lane % HP
    piece, jh = j // H, j % H
    valid = (j < 3 * H) & (jh == head)
    pq = np.where(valid & (sub == DH + piece), 1.0, 0.0).astype(BF16)
    pk = np.where(valid & (sub == DH + 3 + piece), -1.0, 0.0).astype(BF16)
    oq = np.where((sub >= DH + 3) & (sub < DH + 6), 1.0, 0.0).astype(np.float32)
    ok = np.where((sub >= DH) & (sub < DH + 3), 1.0, 0.0).astype(np.float32)
    r = np.arange(AW)[:, None]
    cc = np.arange(128)[None, :]
    sel = np.where((r % DH == 3) & (r // DH == cc), -1.0, 0.0).astype(BF16)
    gi = np.arange(GS)
    gsum = (gi[:, None] // DH == gi[None, :] // DH).astype(BF16)
    return tuple(jnp.asarray(c) for c in (pq, pk, oq, ok, sel, gsum))


def _local_step(xs, tgt, wp, late_weights, cw8, bfp, g_attn_out, g_conv_out,
                g_mix_pre, g_mix_post, g_ffn_pre, g_ffn_post, early_grads=None, last_grad=None):
    pq, pk, oq, ok, sel, gsum = _placement_constants()
    h1t, qp, kp, vv, bcu, zf = _in_proj(xs, g_mix_pre, wp, bfp, pq, pk, oq, ok, tm=512)
    o, lse, mk = _attn_fwd(qp, kp, vv, t=512)
    w_out_f, wgu, wd = late_weights(lse)
    merged, y, x2, cv, h2 = _mix_out(o, bcu, cw8, g_attn_out, g_conv_out, gsum, w_out_f, xs, g_mix_post, g_ffn_pre, tm=512)
    gate, up, act, dx3, dff, loss_p, dg_ffn_post = _ffn_fwd_loss(h2, wgu, wd, x2, tgt, g_ffn_post, tm=512)

    dgu, dx2, dy, dg_ffn_pre, dg_mix_post = _ffn_bwd(dff, wd, gate, up, wgu, x2, g_ffn_pre, dx3, y, g_mix_post, tm=256)
    dw_down = _grad_matmul(act, dff, ta=DFF // 2, tb=D, ts=4096, name="grad_w_down", vmem_mb=60)
    dw_gu = _grad_matmul(dgu, h2, ta=DFF // 2, tb=D, ts=4096, name="grad_w_gate_up", vmem_mb=60).reshape(NDEV, FB, D)
    dw_out = _grad_matmul(merged, dy, ta=1024, tb=1024, ts=2048, name="grad_w_out")
    token = early_grads(dw_out, dw_gu, dw_down) if early_grads is not None else dw_out
    do, dl, dcv, db, dg_attn, dg_conv = _mix_bwd(dy, w_out_f, o, cv, bcu, g_attn_out, g_conv_out, gsum, token, tm=512)
    dbcu, dtaps = _conv_bwd(dcv, db, bcu, cw8, tm=512)
    dqp, dkp, dv, dkx = _attn_bwd(qp, kp, vv, do, lse, dl, mk, t=512)
    dfl, dbf = _forget_bwd(dkx, zf, sel, tm=512)
    pieces = (dqp, dkp, dv, dbcu, dfl)
    dwp = _grad_w_in(h1t, pieces)
    token = last_grad(dwp) if last_grad is not None else dwp
    grad_x, dg_mix_pre = _in_proj_bwd(pieces, wp, xs, g_mix_pre, dx2, token, tm=512)
    return (grad_x, dwp, dw_out, dw_gu, dw_down, dg_mix_pre, dg_mix_post, dg_ffn_pre, dg_ffn_post, dg_attn, dg_conv,
            dtaps, dbf, loss_p)


BIG_TILES = {"w_in": 256, "w_out": 128, "w_gate_up": 176, "w_down": 176}


def kernel(x, w_in, b_forget, conv_w, g_attn_out, g_conv_out, w_out, g_mix_pre, g_mix_post, w_gate_up, w_down, g_ffn_pre, g_ffn_post, loss_target, m_w_in, m_b_forget, m_conv_w, m_g_attn_out, m_g_conv_out, m_w_out, m_g_mix_pre, m_g_mix_post, m_w_gate_up, m_w_down, m_g_ffn_pre, m_g_ffn_post, v_w_in, v_b_forget, v_conv_w, v_g_attn_out, v_g_conv_out, v_w_out, v_g_mix_pre, v_g_mix_post, v_w_gate_up, v_w_down, v_g_ffn_pre, v_g_ffn_post):
    xc, yc, cc = _position()
    my_chip = 2 * xc + yc
    me = 2 * my_chip + cc
    idx = jnp.stack([cc, my_chip]).astype(jnp.int32)
    tables = _in_layout_tables()

    w_in_b = w_in[0].astype(BF16)
    g_in, g_last, g_taps = _all_gather([w_in_b[:, :IN_MAIN], w_in_b[:, IN_MAIN].reshape(SUBLANES, LANES), conv_w[0]])
    last_cols = jnp.pad(g_last.reshape(NDEV, D).T.astype(F32), ((0, 0), (0, LANES - NDEV)))
    wp = _assemble_w_in(g_in, last_cols, tables, tr=256)
    cw8 = jnp.pad(g_taps.transpose(1, 0, 2).reshape(3, CW), ((0, SUBLANES - 3), (0, 0)))

    late = [w_out[0].astype(BF16), w_gate_up[0].T.astype(BF16), w_down[0].astype(BF16)]
    ssem, rsem, late_thru, land_thru, token = _exchange_start(
        late, [lax.empty((NDEV,) + s.shape, s.dtype) for s in late], g_in, mode="gather",
        name="gather_late_start")
    bfp = jnp.pad(b_forget, ((0, 0), (0, 128 - H))) + token[0:1, :]

    def late_weights(after):
        l_out, l_gu, l_down = _exchange_wait(ssem, rsem, late_thru, land_thru, after, mode="gather", name="gather_late_wait")
        return l_out.reshape(D, D), l_gu.reshape(2, DFF, D), l_down.reshape(DFF, D)

    early = {}

    def early_grads(dw_out, dw_gu, dw_down):
        srcs = [dw_out.reshape(NDEV, D // NDEV, D), dw_gu, dw_down.reshape(NDEV, DFF // NDEV, D)]
        lands = [lax.empty(s.shape, s.dtype) for s in srcs]
        early["handles"] = _exchange_start(srcs, lands, dw_out, mode="scatter", name="scatter_early_start")
        return early["handles"][4]

    last = {}

    def last_grad(dwp):
        g_w_in = _disassemble_w_in(dwp, tables, tr=256).reshape(4, 2, D, IN_PAD)
        (from_sibling,) = _pair_exchange([g_w_in])
        pair_b, last["own"] = _pair_sum(g_w_in, from_sibling, idx, tr=D, name="grad_pair_sum_w_in")
        last["handles"] = _exchange_start([pair_b], [lax.empty(pair_b.shape, pair_b.dtype)], last["own"], mode="chips",
                                          name="chips_w_in_start")
        return last["handles"][4]

    (grad_x, dwp, dw_out, dw_gu, dw_down, dg_mix_pre, dg_mix_post, dg_ffn_pre, dg_ffn_post, dg_attn, dg_conv,
     dtaps, dbf, loss_p) = _local_step(x[0], loss_target[0], wp, late_weights, cw8, bfp, g_attn_out, g_conv_out,
                                        g_mix_pre, g_mix_post, g_ffn_pre, g_ffn_post, early_grads, last_grad)

    s_ssem, s_rsem, s_srcs, s_lands, s_token = _small_start(
        [dg_mix_pre, dg_mix_post, dg_ffn_pre, dg_ffn_post, dg_attn, dg_conv, dtaps, dbf, loss_p])

    e_ssem, e_rsem, e_srcs, e_lands, _ = early["handles"]
    land_out, land_gu, land_down = _exchange_wait(e_ssem, e_rsem, e_srcs, e_lands, s_token, mode="scatter",
                                                  name="scatter_early_wait")
    res = {}
    big = {"w_out": (land_out, w_out[0], m_w_out[0], v_w_out[0]),
           "w_gate_up": (land_gu, w_gate_up[0].T, m_w_gate_up[0].T, v_w_gate_up[0].T),
           "w_down": (land_down, w_down[0], m_w_down[0], v_w_down[0])}
    for name, (land, w, m, v) in big.items():
        outs = _device_sum_adamw(land, w, m, v, tr=BIG_TILES[name], name="adamw_" + name)
        res[name] = [(o.T if name == "w_gate_up" else o)[None] for o in outs]
    c_ssem, c_rsem, c_srcs, c_lands, _ = last["handles"]
    after = sum(res[n][1][0, :SUBLANES, :LANES] for n in big)
    (from_chips,) = _exchange_wait(c_ssem, c_rsem, c_srcs, c_lands, after, mode="chips", name="chips_w_in_wait")
    outs = _chip_sum_adamw(from_chips, last["own"], idx, w_in[0].T, m_w_in[0].T, v_w_in[0].T,
                           tr=BIG_TILES["w_in"], name="adamw_w_in")
    res["w_in"] = [o.T[None] for o in outs]
    w_in_done = outs[1][:SUBLANES, :LANES]

    (land_small,) = _exchange_wait(s_ssem, s_rsem, s_srcs, s_lands, w_in_done, mode="gather", name="small_gather_wait")
    small = _small_sum(land_small)
    taps_full = jnp.concatenate([small[5:6, :CW], small[5:6, CW:], small[6:7, :CW]], axis=0)
    taps_first = lambda a: a.transpose(1, 0, 2)
    smalls = {"b_forget": (b_forget, m_b_forget, v_b_forget),
              "conv_w": (taps_first(conv_w), taps_first(m_conv_w), taps_first(v_conv_w)),
              "g_attn_out": (g_attn_out, m_g_attn_out, v_g_attn_out), "g_conv_out": (g_conv_out, m_g_conv_out, v_g_conv_out),
              "g_mix_pre": (g_mix_pre, m_g_mix_pre, v_g_mix_pre), "g_mix_post": (g_mix_post, m_g_mix_post, v_g_mix_post),
              "g_ffn_pre": (g_ffn_pre, m_g_ffn_pre, v_g_ffn_pre), "g_ffn_post": (g_ffn_post, m_g_ffn_post, v_g_ffn_post)}
    own_taps = lax.dynamic_slice(taps_full, (0, me * 64), (3, 64))[:, None, :]
    small_res, loss = _small_adamw(small, own_taps, smalls)
    for name, outs in small_res.items():
        res[name] = [taps_first(o) for o in outs] if name == "conv_w" else list(outs)

    order = ["w_in", "b_forget", "conv_w", "g_attn_out", "g_conv_out", "w_out", "g_mix_pre", "g_mix_post",
             "w_gate_up", "w_down", "g_ffn_pre", "g_ffn_post"]
    outs = [loss, grad_x[None]]
    for k in range(4):
        outs += [res[n][k] for n in order]
    return tuple(outs)
```
